```python
import math
import jax, jax.numpy as jnp
from jax import lax
import numpy as np

D_MODEL = 1024
BATCH = 8
SEQ = 4096
DEPTH = 1

D_RNN = 1280
RNN_BLOCK = 64
N_RNN_BLOCKS = D_RNN // RNN_BLOCK
CONV_WIDTH = 4
LRU_C = 8.0
N_HEADS_MLA = 16
QK_NOPE = 64
QK_ROPE = 32
V_HEAD = 64
Q_LORA = 384
KV_LORA = 256
ROPE_THETA = 10000.0
Q_BLOCK = 128
D_FF = 2816
FFN_CONV = 3
EPS = 1e-6
D_IN = D_RNN + Q_LORA + KV_LORA + QK_ROPE + 2 * D_MODEL

kernel_name = "hybrid_rglru_mla_convffn_adaln"


def rmsnorm(x, g):
    xf = x.astype(jnp.float32)
    y = xf * lax.rsqrt(jnp.mean(xf * xf, axis=-1, keepdims=True) + EPS)
    return (y * g.astype(jnp.float32)).astype(x.dtype)


def causal_dwconv(x, w, b):
    K = w.shape[0]
    S = x.shape[1]
    xp = jnp.pad(x, ((0, 0), (K - 1, 0), (0, 0)))
    y = xp[:, 0:S] * w[0]
    for k in range(1, K):
        y = y + xp[:, k:k + S] * w[k]
    return y + b


def rotary(x, positions):
    half = QK_ROPE // 2
    inv_freq = ROPE_THETA ** (-jnp.arange(half, dtype=jnp.float32) / half)
    ang = positions.astype(jnp.float32)[..., None] * inv_freq
    ang = ang.reshape(ang.shape[:2] + (1,) * (x.ndim - 3) + (half,))
    cos, sin = jnp.cos(ang), jnp.sin(ang)
    xf = x.astype(jnp.float32)
    x1, x2 = xf[..., :half], xf[..., half:]
    out = jnp.concatenate([x1 * cos - x2 * sin, x2 * cos + x1 * sin], axis=-1)
    return out.astype(x.dtype)


def rg_lru(xc, positions, w_a, b_a, w_x, b_x, lru_param):
    B, S, W = xc.shape
    xb = xc.reshape(B, S, N_RNN_BLOCKS, RNN_BLOCK)
    r = jax.nn.sigmoid(jnp.einsum('bsnd,nde->bsne', xb, w_a).reshape(B, S, W) + b_a)
    i = jax.nn.sigmoid(jnp.einsum('bsnd,nde->bsne', xb, w_x).reshape(B, S, W) + b_x)
    log_a = -LRU_C * r.astype(jnp.float32) * jax.nn.softplus(-lru_param.astype(jnp.float32))
    a = jnp.exp(log_a)
    mult = jnp.sqrt(-jnp.expm1(2.0 * log_a))
    reset = (positions == 0)[..., None]
    a = jnp.where(reset, 0.0, a)
    mult = jnp.where(reset, 1.0, mult)
    bterm = mult * (i * xc).astype(jnp.float32)

    def combine(lhs, rhs):
        a1, b1 = lhs
        a2, b2 = rhs
        return a1 * a2, a2 * b1 + b2

    _, h = lax.associative_scan(combine, (a, bterm), axis=1)
    return h.astype(xc.dtype)


def causal_block_attention(q, k, v):
    B, S, H, Dk = q.shape
    Dv = v.shape[-1]
    nb = S // Q_BLOCK
    scale = 1.0 / math.sqrt(Dk)
    qb = q.reshape(B, nb, Q_BLOCK, H, Dk).transpose(1, 0, 2, 3, 4)
    kpos = jnp.arange(S)

    def one_block(args):
        qi, blk = args
        s = jnp.einsum('bqhd,bkhd->bhqk', qi, k).astype(jnp.float32) * scale
        qpos = blk * Q_BLOCK + jnp.arange(Q_BLOCK)
        mask = kpos[None, :] <= qpos[:, None]
        s = jnp.where(mask[None, None], s, -jnp.inf)
        p = jax.nn.softmax(s, axis=-1).astype(v.dtype)
        return jnp.einsum('bhqk,bkhd->bqhd', p, v)

    o = lax.map(one_block, (qb, jnp.arange(nb)))
    return o.transpose(1, 0, 2, 3, 4).reshape(B, S, H * Dv)


def _fwd_setup_inputs(seed: int = 0) -> dict:
    key = jax.random.key(seed)
    ks = jax.random.split(key, 32)

    def nrm(k, shape, fan_in, mult=1.0):
        return jax.random.normal(k, shape, jnp.float32) * (mult * fan_in ** -0.5)

    def gain(k, shape):
        return 1.0 + 0.02 * jax.random.normal(k, shape, jnp.float32)

    def bias(k, shape):
        return 0.01 * jax.random.normal(k, shape, jnp.float32)

    L = DEPTH
    a0 = jax.random.uniform(ks[10], (L, D_RNN), jnp.float32, 0.9, 0.999)
    s0 = a0 ** (1.0 / LRU_C)
    lru_param = jnp.log(s0) - jnp.log1p(-s0)
    positions = (jnp.arange(SEQ, dtype=jnp.int32)[None, :]
                 + jax.random.randint(ks[31], (BATCH, 1), 0, SEQ, dtype=jnp.int32))
    return {
        "x": jax.random.normal(ks[0], (BATCH, SEQ, D_MODEL), jnp.float32),
        "c": jax.random.normal(ks[1], (BATCH, D_MODEL), jnp.float32),
        "positions": positions,
        "w_ada": nrm(ks[2], (L, D_MODEL, 6 * D_MODEL), D_MODEL, 0.5),
        "b_ada": bias(ks[3], (L, 6 * D_MODEL)),
        "norm1_g": gain(ks[4], (L, D_MODEL)),
        "w_in": nrm(ks[5], (L, D_MODEL, D_IN), D_MODEL),
        "conv_w": nrm(ks[6], (L, CONV_WIDTH, D_RNN), CONV_WIDTH),
        "conv_b": bias(ks[7], (L, D_RNN)),
        "w_gate_a": nrm(ks[8], (L, N_RNN_BLOCKS, RNN_BLOCK, RNN_BLOCK), RNN_BLOCK),
        "b_gate_a": bias(ks[9], (L, D_RNN)),
        "w_gate_x": nrm(ks[11], (L, N_RNN_BLOCKS, RNN_BLOCK, RNN_BLOCK), RNN_BLOCK),
        "b_gate_x": bias(ks[12], (L, D_RNN)),
        "lru_param": lru_param,
        "q_norm_g": gain(ks[13], (L, Q_LORA)),
        "w_uq": nrm(ks[14], (L, Q_LORA, N_HEADS_MLA * (QK_NOPE + QK_ROPE)), Q_LORA),
        "kv_norm_g": gain(ks[15], (L, KV_LORA)),
        "w_ukv": nrm(ks[16], (L, KV_LORA, N_HEADS_MLA * (QK_NOPE + V_HEAD)), KV_LORA),
        "w_proj_rnn": nrm(ks[17], (L, D_RNN, D_MODEL), D_RNN),
        "w_proj_mla": nrm(ks[18], (L, N_HEADS_MLA * V_HEAD, D_MODEL), N_HEADS_MLA * V_HEAD),
        "w_out": nrm(ks[19], (L, D_MODEL, D_MODEL), D_MODEL),
        "norm2_g": gain(ks[20], (L, D_MODEL)),
        "w_up": nrm(ks[21], (L, D_MODEL, 2 * D_FF), D_MODEL),
        "ffn_conv_w": nrm(ks[22], (L, FFN_CONV, 2 * D_FF), FFN_CONV),
        "ffn_conv_b": bias(ks[23], (L, 2 * D_FF)),
        "w_down": nrm(ks[24], (L, D_FF, D_MODEL), D_FF),
        "final_g": gain(ks[25], (D_MODEL,)),
    }


def _fwd_reference(x, c, positions, w_ada, b_ada, norm1_g, w_in, conv_w, conv_b,
              w_gate_a, b_gate_a, w_gate_x, b_gate_x, lru_param, q_norm_g, w_uq,
              kv_norm_g, w_ukv, w_proj_rnn, w_proj_mla, w_out, norm2_g, w_up,
              ffn_conv_w, ffn_conv_b, w_down, final_g):
    B, S, D = x.shape
    H = N_HEADS_MLA
    c_act = jax.nn.silu(c)
    for l in range(DEPTH):
        mod = c_act @ w_ada[l] + b_ada[l]
        shift1, scale1, gate1, shift2, scale2, gate2 = [m[:, None, :] for m in jnp.split(mod, 6, axis=-1)]

        h = rmsnorm(x, norm1_g[l]) * (1.0 + scale1) + shift1
        proj = h @ w_in[l]
        o0 = D_RNN
        o1 = o0 + Q_LORA
        o2 = o1 + KV_LORA
        o3 = o2 + QK_ROPE
        o4 = o3 + D_MODEL
        x_rnn, q_lat, kv_lat, k_rope, g_rnn, g_mla = (
            proj[..., :o0], proj[..., o0:o1], proj[..., o1:o2],
            proj[..., o2:o3], proj[..., o3:o4], proj[..., o4:])

        xc = causal_dwconv(x_rnn, conv_w[l], conv_b[l])
        y_rnn = rg_lru(xc, positions, w_gate_a[l], b_gate_a[l], w_gate_x[l], b_gate_x[l], lru_param[l])

        q = (rmsnorm(q_lat, q_norm_g[l]) @ w_uq[l]).reshape(B, S, H, QK_NOPE + QK_ROPE)
        q_rope = rotary(q[..., QK_NOPE:], positions)
        q = jnp.concatenate([q[..., :QK_NOPE], q_rope], axis=-1)
        kv = (rmsnorm(kv_lat, kv_norm_g[l]) @ w_ukv[l]).reshape(B, S, H, QK_NOPE + V_HEAD)
        k_nope, v = kv[..., :QK_NOPE], kv[..., QK_NOPE:]
        k_r = jnp.broadcast_to(rotary(k_rope, positions)[:, :, None, :], (B, S, H, QK_ROPE))
        k = jnp.concatenate([k_nope, k_r], axis=-1)
        y_mla = causal_block_attention(q, k, v)

        merged = (jax.nn.sigmoid(g_rnn) * (y_rnn @ w_proj_rnn[l])
                  + jax.nn.sigmoid(g_mla) * (y_mla @ w_proj_mla[l]))
        x = x + gate1 * (merged @ w_out[l])

        h2 = rmsnorm(x, norm2_g[l]) * (1.0 + scale2) + shift2
        u = causal_dwconv(h2 @ w_up[l], ffn_conv_w[l], ffn_conv_b[l])
        u_gate, u_val = u[..., :D_FF], u[..., D_FF:]
        x = x + gate2 * ((jax.nn.silu(u_gate) * u_val) @ w_down[l])

    return rmsnorm(x, final_g)


import jax as _jax
import jax.numpy as _jnp

TWIN_FORMAT = 'train_step'
FWD_PARAMS = ['x', 'c', 'positions', 'w_ada', 'b_ada', 'norm1_g', 'w_in', 'conv_w', 'conv_b', 'w_gate_a', 'b_gate_a', 'w_gate_x', 'b_gate_x', 'lru_param', 'q_norm_g', 'w_uq', 'kv_norm_g', 'w_ukv', 'w_proj_rnn', 'w_proj_mla', 'w_out', 'norm2_g', 'w_up', 'ffn_conv_w', 'ffn_conv_b', 'w_down', 'final_g']
TWIN_WEIGHTS = ['w_ada', 'b_ada', 'norm1_g', 'w_in', 'conv_w', 'conv_b', 'w_gate_a', 'b_gate_a', 'w_gate_x', 'b_gate_x', 'lru_param', 'q_norm_g', 'w_uq', 'kv_norm_g', 'w_ukv', 'w_proj_rnn', 'w_proj_mla', 'w_out', 'norm2_g', 'w_up', 'ffn_conv_w', 'ffn_conv_b', 'w_down', 'final_g']
TWIN_DIFF_INPUT = 'x'
TWIN_INPUTS = ['x', 'c', 'positions', 'w_ada', 'b_ada', 'norm1_g', 'w_in', 'conv_w', 'conv_b', 'w_gate_a', 'b_gate_a', 'w_gate_x', 'b_gate_x', 'lru_param', 'q_norm_g', 'w_uq', 'kv_norm_g', 'w_ukv', 'w_proj_rnn', 'w_proj_mla', 'w_out', 'norm2_g', 'w_up', 'ffn_conv_w', 'ffn_conv_b', 'w_down', 'final_g', 'loss_target', 'm_w_ada', 'm_b_ada', 'm_norm1_g', 'm_w_in', 'm_conv_w', 'm_conv_b', 'm_w_gate_a', 'm_b_gate_a', 'm_w_gate_x', 'm_b_gate_x', 'm_lru_param', 'm_q_norm_g', 'm_w_uq', 'm_kv_norm_g', 'm_w_ukv', 'm_w_proj_rnn', 'm_w_proj_mla', 'm_w_out', 'm_norm2_g', 'm_w_up', 'm_ffn_conv_w', 'm_ffn_conv_b', 'm_w_down', 'm_final_g', 'v_w_ada', 'v_b_ada', 'v_norm1_g', 'v_w_in', 'v_conv_w', 'v_conv_b', 'v_w_gate_a', 'v_b_gate_a', 'v_w_gate_x', 'v_b_gate_x', 'v_lru_param', 'v_q_norm_g', 'v_w_uq', 'v_kv_norm_g', 'v_w_ukv', 'v_w_proj_rnn', 'v_w_proj_mla', 'v_w_out', 'v_norm2_g', 'v_w_up', 'v_ffn_conv_w', 'v_ffn_conv_b', 'v_w_down', 'v_final_g']
TWIN_OUTPUTS = ['loss', 'grad_x', 'grad_w_ada', 'grad_b_ada', 'grad_norm1_g', 'grad_w_in', 'grad_conv_w', 'grad_conv_b', 'grad_w_gate_a', 'grad_b_gate_a', 'grad_w_gate_x', 'grad_b_gate_x', 'grad_lru_param', 'grad_q_norm_g', 'grad_w_uq', 'grad_kv_norm_g', 'grad_w_ukv', 'grad_w_proj_rnn', 'grad_w_proj_mla', 'grad_w_out', 'grad_norm2_g', 'grad_w_up', 'grad_ffn_conv_w', 'grad_ffn_conv_b', 'grad_w_down', 'grad_final_g', 'delta_w_ada', 'delta_b_ada', 'delta_norm1_g', 'delta_w_in', 'delta_conv_w', 'delta_conv_b', 'delta_w_gate_a', 'delta_b_gate_a', 'delta_w_gate_x', 'delta_b_gate_x', 'delta_lru_param', 'delta_q_norm_g', 'delta_w_uq', 'delta_kv_norm_g', 'delta_w_ukv', 'delta_w_proj_rnn', 'delta_w_proj_mla', 'delta_w_out', 'delta_norm2_g', 'delta_w_up', 'delta_ffn_conv_w', 'delta_ffn_conv_b', 'delta_w_down', 'delta_final_g', 'new_m_w_ada', 'new_m_b_ada', 'new_m_norm1_g', 'new_m_w_in', 'new_m_conv_w', 'new_m_conv_b', 'new_m_w_gate_a', 'new_m_b_gate_a', 'new_m_w_gate_x', 'new_m_b_gate_x', 'new_m_lru_param', 'new_m_q_norm_g', 'new_m_w_uq', 'new_m_kv_norm_g', 'new_m_w_ukv', 'new_m_w_proj_rnn', 'new_m_w_proj_mla', 'new_m_w_out', 'new_m_norm2_g', 'new_m_w_up', 'new_m_ffn_conv_w', 'new_m_ffn_conv_b', 'new_m_w_down', 'new_m_final_g', 'new_v_w_ada', 'new_v_b_ada', 'new_v_norm1_g', 'new_v_w_in', 'new_v_conv_w', 'new_v_conv_b', 'new_v_w_gate_a', 'new_v_b_gate_a', 'new_v_w_gate_x', 'new_v_b_gate_x', 'new_v_lru_param', 'new_v_q_norm_g', 'new_v_w_uq', 'new_v_kv_norm_g', 'new_v_w_ukv', 'new_v_w_proj_rnn', 'new_v_w_proj_mla', 'new_v_w_out', 'new_v_norm2_g', 'new_v_w_up', 'new_v_ffn_conv_w', 'new_v_ffn_conv_b', 'new_v_w_down', 'new_v_final_g']
TWIN_LEAF_KINDS = {'loss': 'loss', 'grad_x': 'grad_x', 'grad_w_ada': 'grad_w', 'grad_b_ada': 'grad_w', 'grad_norm1_g': 'grad_w', 'grad_w_in': 'grad_w', 'grad_conv_w': 'grad_w', 'grad_conv_b': 'grad_w', 'grad_w_gate_a': 'grad_w', 'grad_b_gate_a': 'grad_w', 'grad_w_gate_x': 'grad_w', 'grad_b_gate_x': 'grad_w', 'grad_lru_param': 'grad_w', 'grad_q_norm_g': 'grad_w', 'grad_w_uq': 'grad_w', 'grad_kv_norm_g': 'grad_w', 'grad_w_ukv': 'grad_w', 'grad_w_proj_rnn': 'grad_w', 'grad_w_proj_mla': 'grad_w', 'grad_w_out': 'grad_w', 'grad_norm2_g': 'grad_w', 'grad_w_up': 'grad_w', 'grad_ffn_conv_w': 'grad_w', 'grad_ffn_conv_b': 'grad_w', 'grad_w_down': 'grad_w', 'grad_final_g': 'grad_w', 'delta_w_ada': 'delta_w', 'delta_b_ada': 'delta_w', 'delta_norm1_g': 'delta_w', 'delta_w_in': 'delta_w', 'delta_conv_w': 'delta_w', 'delta_conv_b': 'delta_w', 'delta_w_gate_a': 'delta_w', 'delta_b_gate_a': 'delta_w', 'delta_w_gate_x': 'delta_w', 'delta_b_gate_x': 'delta_w', 'delta_lru_param': 'delta_w', 'delta_q_norm_g': 'delta_w', 'delta_w_uq': 'delta_w', 'delta_kv_norm_g': 'delta_w', 'delta_w_ukv': 'delta_w', 'delta_w_proj_rnn': 'delta_w', 'delta_w_proj_mla': 'delta_w', 'delta_w_out': 'delta_w', 'delta_norm2_g': 'delta_w', 'delta_w_up': 'delta_w', 'delta_ffn_conv_w': 'delta_w', 'delta_ffn_conv_b': 'delta_w', 'delta_w_down': 'delta_w', 'delta_final_g': 'delta_w', 'new_m_w_ada': 'new_m', 'new_m_b_ada': 'new_m', 'new_m_norm1_g': 'new_m', 'new_m_w_in': 'new_m', 'new_m_conv_w': 'new_m', 'new_m_conv_b': 'new_m', 'new_m_w_gate_a': 'new_m', 'new_m_b_gate_a': 'new_m', 'new_m_w_gate_x': 'new_m', 'new_m_b_gate_x': 'new_m', 'new_m_lru_param': 'new_m', 'new_m_q_norm_g': 'new_m', 'new_m_w_uq': 'new_m', 'new_m_kv_norm_g': 'new_m', 'new_m_w_ukv': 'new_m', 'new_m_w_proj_rnn': 'new_m', 'new_m_w_proj_mla': 'new_m', 'new_m_w_out': 'new_m', 'new_m_norm2_g': 'new_m', 'new_m_w_up': 'new_m', 'new_m_ffn_conv_w': 'new_m', 'new_m_ffn_conv_b': 'new_m', 'new_m_w_down': 'new_m', 'new_m_final_g': 'new_m', 'new_v_w_ada': 'new_v', 'new_v_b_ada': 'new_v', 'new_v_norm1_g': 'new_v', 'new_v_w_in': 'new_v', 'new_v_conv_w': 'new_v', 'new_v_conv_b': 'new_v', 'new_v_w_gate_a': 'new_v', 'new_v_b_gate_a': 'new_v', 'new_v_w_gate_x': 'new_v', 'new_v_b_gate_x': 'new_v', 'new_v_lru_param': 'new_v', 'new_v_q_norm_g': 'new_v', 'new_v_w_uq': 'new_v', 'new_v_kv_norm_g': 'new_v', 'new_v_w_ukv': 'new_v', 'new_v_w_proj_rnn': 'new_v', 'new_v_w_proj_mla': 'new_v', 'new_v_w_out': 'new_v', 'new_v_norm2_g': 'new_v', 'new_v_w_up': 'new_v', 'new_v_ffn_conv_w': 'new_v', 'new_v_ffn_conv_b': 'new_v', 'new_v_w_down': 'new_v', 'new_v_final_g': 'new_v'}


def _forward(args):
    return _fwd_reference(*[args[k] for k in FWD_PARAMS])


def _output_shape():
    out = _jax.eval_shape(lambda: _forward(_fwd_setup_inputs(0)))
    return out.shape, out.dtype

N_MICROBATCH = 1
ADAM_LR = 0.001
ADAM_B1 = 0.9
ADAM_B2 = 0.999
ADAM_EPS = 1e-08
ADAM_WD = 0.01
ADAM_STEP = 10
PER_EXAMPLE_BATCH_AXIS = {'x': 0, 'c': 0, 'positions': 0, 'loss_target': 0}
SHARED_INPUTS = []
_WEIGHT_DTYPES = {'w_ada': _jnp.float32, 'b_ada': _jnp.float32, 'norm1_g': _jnp.float32, 'w_in': _jnp.float32, 'conv_w': _jnp.float32, 'conv_b': _jnp.float32, 'w_gate_a': _jnp.float32, 'b_gate_a': _jnp.float32, 'w_gate_x': _jnp.float32, 'b_gate_x': _jnp.float32, 'lru_param': _jnp.float32, 'q_norm_g': _jnp.float32, 'w_uq': _jnp.float32, 'kv_norm_g': _jnp.float32, 'w_ukv': _jnp.float32, 'w_proj_rnn': _jnp.float32, 'w_proj_mla': _jnp.float32, 'w_out': _jnp.float32, 'norm2_g': _jnp.float32, 'w_up': _jnp.float32, 'ffn_conv_w': _jnp.float32, 'ffn_conv_b': _jnp.float32, 'w_down': _jnp.float32, 'final_g': _jnp.float32}
MOMENT_SCALE = {'w_ada': 9.785805e-02, 'b_ada': 1.589993e-01, 'norm1_g': 2.950328e-02, 'w_in': 4.363157e-02, 'conv_w': 7.856375e-02, 'conv_b': 2.532264e-01, 'w_gate_a': 9.810760e-03, 'b_gate_a': 1.337748e-02, 'w_gate_x': 1.864887e-02, 'b_gate_x': 2.634271e-02, 'lru_param': 3.612669e-02, 'q_norm_g': 8.878610e-03, 'w_uq': 4.315614e-03, 'kv_norm_g': 2.268722e-02, 'w_ukv': 7.974577e-03, 'w_proj_rnn': 8.121148e-02, 'w_proj_mla': 1.058762e-02, 'w_out': 7.812412e-02, 'norm2_g': 5.548143e-02, 'w_up': 2.380161e-02, 'ffn_conv_w': 2.360201e-02, 'ffn_conv_b': 2.097115e-02, 'w_down': 3.866672e-02, 'final_g': 3.213481e+01}


def _to_microbatches(a, axis):
    t = _jnp.moveaxis(a, axis, 0)
    t = t.reshape((N_MICROBATCH, t.shape[0] // N_MICROBATCH) + t.shape[1:])
    return _jnp.moveaxis(t, 1, axis + 1)


def setup_inputs(seed: int = 0) -> dict:
    inp = _fwd_setup_inputs(seed)
    key = _jax.random.fold_in(_jax.random.key(seed), 7919)
    shape, _ = _output_shape()
    out = dict(inp)
    out["loss_target"] = _jax.random.normal(_jax.random.fold_in(key, 0), shape, _jnp.float32)
    for i, name in enumerate(TWIN_WEIGHTS):
        w = inp[name].astype(_jnp.float32)
        if MOMENT_SCALE is None:
            s = _jnp.sqrt(_jnp.mean(_jnp.square(w)) + 1e-30)
        else:
            s = MOMENT_SCALE[name]
        km, kv = _jax.random.split(_jax.random.fold_in(key, i + 1))
        out[name] = w
        out["m_" + name] = s * _jax.random.normal(km, w.shape, _jnp.float32)
        out["v_" + name] = (s * s) * _jax.random.uniform(kv, w.shape, _jnp.float32, 0.5, 1.5)
    if N_MICROBATCH > 1:
        for name, axis in PER_EXAMPLE_BATCH_AXIS.items():
            out[name] = _to_microbatches(out[name], axis)
    return {'x': out['x'], 'c': out['c'], 'positions': out['positions'], 'w_ada': out['w_ada'], 'b_ada': out['b_ada'], 'norm1_g': out['norm1_g'], 'w_in': out['w_in'], 'conv_w': out['conv_w'], 'conv_b': out['conv_b'], 'w_gate_a': out['w_gate_a'], 'b_gate_a': out['b_gate_a'], 'w_gate_x': out['w_gate_x'], 'b_gate_x': out['b_gate_x'], 'lru_param': out['lru_param'], 'q_norm_g': out['q_norm_g'], 'w_uq': out['w_uq'], 'kv_norm_g': out['kv_norm_g'], 'w_ukv': out['w_ukv'], 'w_proj_rnn': out['w_proj_rnn'], 'w_proj_mla': out['w_proj_mla'], 'w_out': out['w_out'], 'norm2_g': out['norm2_g'], 'w_up': out['w_up'], 'ffn_conv_w': out['ffn_conv_w'], 'ffn_conv_b': out['ffn_conv_b'], 'w_down': out['w_down'], 'final_g': out['final_g'], 'loss_target': out['loss_target'], 'm_w_ada': out['m_w_ada'], 'm_b_ada': out['m_b_ada'], 'm_norm1_g': out['m_norm1_g'], 'm_w_in': out['m_w_in'], 'm_conv_w': out['m_conv_w'], 'm_conv_b': out['m_conv_b'], 'm_w_gate_a': out['m_w_gate_a'], 'm_b_gate_a': out['m_b_gate_a'], 'm_w_gate_x': out['m_w_gate_x'], 'm_b_gate_x': out['m_b_gate_x'], 'm_lru_param': out['m_lru_param'], 'm_q_norm_g': out['m_q_norm_g'], 'm_w_uq': out['m_w_uq'], 'm_kv_norm_g': out['m_kv_norm_g'], 'm_w_ukv': out['m_w_ukv'], 'm_w_proj_rnn': out['m_w_proj_rnn'], 'm_w_proj_mla': out['m_w_proj_mla'], 'm_w_out': out['m_w_out'], 'm_norm2_g': out['m_norm2_g'], 'm_w_up': out['m_w_up'], 'm_ffn_conv_w': out['m_ffn_conv_w'], 'm_ffn_conv_b': out['m_ffn_conv_b'], 'm_w_down': out['m_w_down'], 'm_final_g': out['m_final_g'], 'v_w_ada': out['v_w_ada'], 'v_b_ada': out['v_b_ada'], 'v_norm1_g': out['v_norm1_g'], 'v_w_in': out['v_w_in'], 'v_conv_w': out['v_conv_w'], 'v_conv_b': out['v_conv_b'], 'v_w_gate_a': out['v_w_gate_a'], 'v_b_gate_a': out['v_b_gate_a'], 'v_w_gate_x': out['v_w_gate_x'], 'v_b_gate_x': out['v_b_gate_x'], 'v_lru_param': out['v_lru_param'], 'v_q_norm_g': out['v_q_norm_g'], 'v_w_uq': out['v_w_uq'], 'v_kv_norm_g': out['v_kv_norm_g'], 'v_w_ukv': out['v_w_ukv'], 'v_w_proj_rnn': out['v_w_proj_rnn'], 'v_w_proj_mla': out['v_w_proj_mla'], 'v_w_out': out['v_w_out'], 'v_norm2_g': out['v_norm2_g'], 'v_w_up': out['v_w_up'], 'v_ffn_conv_w': out['v_ffn_conv_w'], 'v_ffn_conv_b': out['v_ffn_conv_b'], 'v_w_down': out['v_w_down'], 'v_final_g': out['v_final_g']}


def _loss(weights, diff, rest, loss_target):
    with _jax.named_scope("forward"):
        args = {**rest, TWIN_DIFF_INPUT: diff, **{k: w.astype(_WEIGHT_DTYPES[k]) for k, w in weights.items()}}
        y = _forward(args)
    with _jax.named_scope("loss_head"):
        err = _jnp.square(y.astype(_jnp.float32) - loss_target)
        return 0.5 * _jnp.sum(_jnp.mean(err, axis=-1)) if err.ndim else 0.5 * err


def _adamw(w, g, m, v):
    m = ADAM_B1 * m + (1.0 - ADAM_B1) * g
    v = ADAM_B2 * v + (1.0 - ADAM_B2) * _jnp.square(g)
    m_hat = m / (1.0 - ADAM_B1 ** ADAM_STEP)
    v_hat = v / (1.0 - ADAM_B2 ** ADAM_STEP)
    delta = -ADAM_LR * (m_hat / (_jnp.sqrt(v_hat) + ADAM_EPS) + ADAM_WD * w)
    return delta, m, v


def reference(x, c, positions, w_ada, b_ada, norm1_g, w_in, conv_w, conv_b, w_gate_a, b_gate_a, w_gate_x, b_gate_x, lru_param, q_norm_g, w_uq, kv_norm_g, w_ukv, w_proj_rnn, w_proj_mla, w_out, norm2_g, w_up, ffn_conv_w, ffn_conv_b, w_down, final_g, loss_target, m_w_ada, m_b_ada, m_norm1_g, m_w_in, m_conv_w, m_conv_b, m_w_gate_a, m_b_gate_a, m_w_gate_x, m_b_gate_x, m_lru_param, m_q_norm_g, m_w_uq, m_kv_norm_g, m_w_ukv, m_w_proj_rnn, m_w_proj_mla, m_w_out, m_norm2_g, m_w_up, m_ffn_conv_w, m_ffn_conv_b, m_w_down, m_final_g, v_w_ada, v_b_ada, v_norm1_g, v_w_in, v_conv_w, v_conv_b, v_w_gate_a, v_b_gate_a, v_w_gate_x, v_b_gate_x, v_lru_param, v_q_norm_g, v_w_uq, v_kv_norm_g, v_w_ukv, v_w_proj_rnn, v_w_proj_mla, v_w_out, v_norm2_g, v_w_up, v_ffn_conv_w, v_ffn_conv_b, v_w_down, v_final_g):
    given = dict(x=x, c=c, positions=positions, w_ada=w_ada, b_ada=b_ada, norm1_g=norm1_g, w_in=w_in, conv_w=conv_w, conv_b=conv_b, w_gate_a=w_gate_a, b_gate_a=b_gate_a, w_gate_x=w_gate_x, b_gate_x=b_gate_x, lru_param=lru_param, q_norm_g=q_norm_g, w_uq=w_uq, kv_norm_g=kv_norm_g, w_ukv=w_ukv, w_proj_rnn=w_proj_rnn, w_proj_mla=w_proj_mla, w_out=w_out, norm2_g=norm2_g, w_up=w_up, ffn_conv_w=ffn_conv_w, ffn_conv_b=ffn_conv_b, w_down=w_down, final_g=final_g, loss_target=loss_target, m_w_ada=m_w_ada, m_b_ada=m_b_ada, m_norm1_g=m_norm1_g, m_w_in=m_w_in, m_conv_w=m_conv_w, m_conv_b=m_conv_b, m_w_gate_a=m_w_gate_a, m_b_gate_a=m_b_gate_a, m_w_gate_x=m_w_gate_x, m_b_gate_x=m_b_gate_x, m_lru_param=m_lru_param, m_q_norm_g=m_q_norm_g, m_w_uq=m_w_uq, m_kv_norm_g=m_kv_norm_g, m_w_ukv=m_w_ukv, m_w_proj_rnn=m_w_proj_rnn, m_w_proj_mla=m_w_proj_mla, m_w_out=m_w_out, m_norm2_g=m_norm2_g, m_w_up=m_w_up, m_ffn_conv_w=m_ffn_conv_w, m_ffn_conv_b=m_ffn_conv_b, m_w_down=m_w_down, m_final_g=m_final_g, v_w_ada=v_w_ada, v_b_ada=v_b_ada, v_norm1_g=v_norm1_g, v_w_in=v_w_in, v_conv_w=v_conv_w, v_conv_b=v_conv_b, v_w_gate_a=v_w_gate_a, v_b_gate_a=v_b_gate_a, v_w_gate_x=v_w_gate_x, v_b_gate_x=v_b_gate_x, v_lru_param=v_lru_param, v_q_norm_g=v_q_norm_g, v_w_uq=v_w_uq, v_kv_norm_g=v_kv_norm_g, v_w_ukv=v_w_ukv, v_w_proj_rnn=v_w_proj_rnn, v_w_proj_mla=v_w_proj_mla, v_w_out=v_w_out, v_norm2_g=v_norm2_g, v_w_up=v_w_up, v_ffn_conv_w=v_ffn_conv_w, v_ffn_conv_b=v_ffn_conv_b, v_w_down=v_w_down, v_final_g=v_final_g)
    weights = {n: given[n] for n in TWIN_WEIGHTS}
    shared = {n: given[n] for n in SHARED_INPUTS}
    per_example = {n: given[n] for n in ['x', 'c', 'positions']}
    grad_fn = _jax.value_and_grad(_loss, argnums=(0, 1))

    def one_microbatch(ex, loss_target):
        ex = dict(ex)
        diff = ex.pop(TWIN_DIFF_INPUT)
        return grad_fn(weights, diff, {**shared, **ex}, loss_target)

    if N_MICROBATCH == 1:
        loss, (grad_w, grad_x) = one_microbatch(per_example, given["loss_target"])
    else:
        def body(carry, xs):
            loss_sum, grad_sum = carry
            l_k, (gw_k, gx_k) = one_microbatch(xs[0], xs[1])
            with _jax.named_scope("update"):
                return (loss_sum + l_k, _jax.tree.map(_jnp.add, grad_sum, gw_k)), gx_k

        init = (_jnp.zeros((), _jnp.float32), _jax.tree.map(_jnp.zeros_like, weights))
        (loss, grad_w), grad_x = _jax.lax.scan(body, init, (per_example, given["loss_target"]))
    with _jax.named_scope("update"):
        delta_w, new_m, new_v = {}, {}, {}
        for n in TWIN_WEIGHTS:
            delta_w[n], new_m[n], new_v[n] = _adamw(weights[n], grad_w[n], given["m_" + n], given["v_" + n])
    return (loss, grad_x, *[grad_w[n] for n in TWIN_WEIGHTS], *[delta_w[n] for n in TWIN_WEIGHTS],
            *[new_m[n] for n in TWIN_WEIGHTS], *[new_v[n] for n in TWIN_WEIGHTS])
```

```python
import functools
import math

import jax
import jax.numpy as jnp
from jax import lax
from jax.experimental import pallas as pl
from jax.experimental.pallas import tpu as pltpu

F32 = jnp.float32
BF16 = jnp.bfloat16

N_DEV = 8
LANES = 128
SUBLANES = 8
VMEM_LIMIT = 56 * 1024 * 1024

D_RNN = 1280
Q_LORA = 384
KV_LORA = 256
QK_NOPE = 64
QK_ROPE = 32
V_HEAD = 64
N_HEADS = 16
D_FF = 2816
ROPE_THETA = 10000.0
LRU_C = 8.0
EPS = 1e-6
MLA_W = 768
ATT_SCALE = 1.0 / math.sqrt(QK_NOPE + QK_ROPE)

ADAM_LR, ADAM_B1, ADAM_B2, ADAM_EPS, ADAM_WD, ADAM_STEP = 0.001, 0.9, 0.999, 1e-08, 0.01, 10


def _cparams(sem=None):
    return pltpu.CompilerParams(dimension_semantics=sem, vmem_limit_bytes=VMEM_LIMIT)


def _pick(n, prefs):
    for p in prefs:
        if n % p == 0:
            return p
    return n


def _sigmoid(v):
    return 1.0 / (1.0 + jnp.exp(-v))


def _lane(shape):
    return lax.broadcasted_iota(jnp.int32, shape, len(shape) - 1)


def _row(shape):
    return lax.broadcasted_iota(jnp.int32, shape, len(shape) - 2)


def _mm(a, b, *, ta=False, tb=False, out_dtype=F32, name):
    (k_a, m) = a.shape if ta else a.shape[::-1]
    (n, k_b) = b.shape if tb else b.shape[::-1]
    assert k_a == k_b, (a.shape, b.shape, ta, tb)
    k = k_a
    tm = _pick(m, (512, 384, 256, 128))
    tn = _pick(n, (512, 384, 256, 128))
    tk = _pick(k, (1024, 512, 384, 256, 128))
    nk = k // tk
    dims = (((0 if ta else 1,), (1 if tb else 0,)), ((), ()))

    def body(a_ref, b_ref, o_ref, *acc):
        part = lax.dot_general(a_ref[...].astype(BF16), b_ref[...].astype(BF16), dims, preferred_element_type=F32)
        if nk == 1:
            o_ref[...] = part.astype(out_dtype)
            return
        acc_ref, = acc
        kk = pl.program_id(2)

        @pl.when(kk == 0)
        def _():
            acc_ref[...] = part

        @pl.when(kk > 0)
        def _():
            acc_ref[...] += part

        @pl.when(kk == nk - 1)
        def _():
            o_ref[...] = acc_ref[...].astype(out_dtype)

    a_spec = pl.BlockSpec((tk, tm), lambda i, j, kk: (kk, i)) if ta else pl.BlockSpec((tm, tk), lambda i, j, kk: (i, kk))
    b_spec = pl.BlockSpec((tn, tk), lambda i, j, kk: (j, kk)) if tb else pl.BlockSpec((tk, tn), lambda i, j, kk: (kk, j))
    return pl.pallas_call(
        body, name=name,
        grid=(m // tm, n // tn, nk),
        in_specs=[a_spec, b_spec],
        out_specs=pl.BlockSpec((tm, tn), lambda i, j, kk: (i, j)),
        out_shape=jax.ShapeDtypeStruct((m, n), out_dtype),
        scratch_shapes=[] if nk == 1 else [pltpu.VMEM((tm, tn), F32)],
        compiler_params=_cparams(("arbitrary", "arbitrary", "arbitrary")),
    )(a, b)


def _rowwise(fn, row_ins, par_ins, out_defs, red_defs, *, name, tr=256):
    s = row_ins[0].shape[0]
    tr = min(tr, s)
    nr, npar, no = len(row_ins), len(par_ins), len(out_defs)

    def body(*refs):
        rin, pin = refs[:nr], refs[nr:nr + npar]
        outs, reds = refs[nr + npar:nr + npar + no], refs[nr + npar + no:]
        i = pl.program_id(0)

        @pl.when(i == 0)
        def _():
            for r in reds:
                r[...] = jnp.zeros_like(r)

        fn(i, rin, pin, outs, reds)

    in_specs = [pl.BlockSpec((tr, a.shape[1]), lambda i: (i, 0)) for a in row_ins]
    in_specs += [pl.BlockSpec(a.shape, lambda i, nd=a.ndim: (0,) * nd) for a in par_ins]
    out_specs = [pl.BlockSpec((tr, c), lambda i: (i, 0)) for c, _ in out_defs]
    out_specs += [pl.BlockSpec(shp, lambda i: (0, 0)) for shp in red_defs]
    out_shape = [jax.ShapeDtypeStruct((s, c), dt) for c, dt in out_defs]
    out_shape += [jax.ShapeDtypeStruct(shp, F32) for shp in red_defs]
    return pl.pallas_call(
        body, name=name, grid=(s // tr,), in_specs=in_specs, out_specs=out_specs, out_shape=out_shape,
        compiler_params=_cparams(("arbitrary",)),
    )(*row_ins, *par_ins)


def _rms(v):
    return lax.rsqrt(jnp.mean(v * v, axis=-1, keepdims=True) + EPS)


def _colsum(v):
    return jnp.sum(v, axis=0, keepdims=True)


def _rms_bwd(dn, n, rstd):
    return rstd * (dn - n * jnp.mean(dn * n, axis=-1, keepdims=True))


def _norm_mod_fwd(x, gmod, name):
    def fn(i, rin, pin, outs, reds):
        xv = rin[0][...]
        p = pin[0][...]
        n = xv * _rms(xv)
        outs[0][...] = ((n * p[0:1]) * (1.0 + p[1:2]) + p[2:3]).astype(BF16)

    return _rowwise(fn, [x], [gmod], [(x.shape[1], BF16)], [], name=name)[0]


def _rope(v, rot_c, rot_s):
    half = QK_ROPE // 2
    swapped = jnp.where(_lane(v.shape) < QK_NOPE + half, pltpu.roll(v, LANES - half, 1), pltpu.roll(v, half, 1))
    return v * rot_c + swapped * rot_s


def _rope_t(dv, rot_c, rot_s):
    half = QK_ROPE // 2
    ds = dv * rot_s
    lane = _lane(dv.shape)
    swapped = jnp.where(lane < QK_NOPE + half, pltpu.roll(ds, LANES - half, 1), pltpu.roll(ds, half, 1))
    in_rope = (lane >= QK_NOPE) & (lane < QK_NOPE + QK_ROPE)
    return dv * rot_c + jnp.where(in_rope, swapped, 0.0)


def _mla_prep_fwd(proj_mla, rot_c, rot_s, ng):
    o1, o2 = Q_LORA, Q_LORA + KV_LORA

    def fn(i, rin, pin, outs, reds):
        g = pin[0][...]
        ql = rin[0][:, 0:o1]
        kl = rin[0][:, o1:o2]
        outs[0][...] = (ql * _rms(ql) * g[0:1, 0:o1]).astype(BF16)
        outs[1][...] = (kl * _rms(kl) * g[0:1, o1:o2]).astype(BF16)
        kr = pltpu.roll(rin[0][:, o2:o2 + LANES], QK_NOPE, 1)
        outs[2][...] = _rope(kr, rin[1][...], rin[2][...]).astype(BF16)

    return _rowwise(fn, [proj_mla, rot_c, rot_s], [ng], [(Q_LORA, BF16), (KV_LORA, BF16), (LANES, BF16)], [],
                    name="mla_prep_fwd")


def _mla_prep_bwd(proj_mla, dqn, dkvn, dkr, rot_c, rot_s, ng):
    o1, o2 = Q_LORA, Q_LORA + KV_LORA

    def fn(i, rin, pin, outs, reds):
        g = pin[0][...]
        ql = rin[0][:, 0:o1]
        kl = rin[0][:, o1:o2]
        rq, rk = _rms(ql), _rms(kl)
        nq, nk = ql * rq, kl * rk
        dq, dk = rin[1][...], rin[2][...]
        outs[0][:, 0:o1] = _rms_bwd(dq * g[0:1, 0:o1], nq, rq).astype(BF16)
        outs[0][:, o1:o2] = _rms_bwd(dk * g[0:1, o1:o2], nk, rk).astype(BF16)
        dkr_pre = pltpu.roll(_rope_t(rin[3][...], rin[4][...], rin[5][...]), LANES - QK_NOPE, 1)
        outs[0][:, o2:] = jnp.where(_lane(dkr_pre.shape) < QK_ROPE, dkr_pre, 0.0).astype(BF16)
        reds[0][0:1, 0:o1] += _colsum(dq * nq)
        reds[0][0:1, o1:o2] += _colsum(dk * nk)

    return _rowwise(fn, [proj_mla, dqn, dkvn, dkr, rot_c, rot_s], [ng], [(MLA_W, BF16)], [(SUBLANES, MLA_W)],
                    name="mla_prep_bwd")


def _rope_heads(q, rot_c, rot_s, transpose, name):
    def fn(i, rin, pin, outs, reds):
        c, sn = rin[1][...], rin[2][...]
        for h in range(N_HEADS):
            sl = slice(h * LANES, (h + 1) * LANES)
            v = rin[0][:, sl]
            outs[0][:, sl] = (_rope_t(v, c, sn) if transpose else _rope(v, c, sn)).astype(BF16)

    return _rowwise(fn, [q, rot_c, rot_s], [], [(q.shape[1], BF16)], [], name=name)[0]


def _merge_fwd(pr, pm, proj_g):
    d = pr.shape[1]

    def fn(i, rin, pin, outs, reds):
        outs[0][...] = (_sigmoid(rin[2][:, 0:d]) * rin[0][...] + _sigmoid(rin[2][:, d:]) * rin[1][...]).astype(BF16)

    return _rowwise(fn, [pr, pm, proj_g], [], [(d, BF16)], [], name="merge_fwd")[0]


def _merge_bwd(dmerged, pr, pm, proj_g):
    d = pr.shape[1]

    def fn(i, rin, pin, outs, reds):
        dm = rin[0][...]
        sr, sm = _sigmoid(rin[3][:, 0:d]), _sigmoid(rin[3][:, d:])
        outs[0][...] = (dm * sr).astype(BF16)
        outs[1][...] = (dm * sm).astype(BF16)
        outs[2][:, 0:d] = (dm * rin[1][...] * sr * (1.0 - sr)).astype(BF16)
        outs[2][:, d:] = (dm * rin[2][...] * sm * (1.0 - sm)).astype(BF16)

    return _rowwise(fn, [dmerged, pr, pm, proj_g], [], [(d, BF16), (d, BF16), (2 * d, BF16)], [], name="merge_bwd")


def _resid_norm_fwd(x, o, gmod):
    d = x.shape[1]

    def fn(i, rin, pin, outs, reds):
        p = pin[0][...]
        x1 = rin[0][...] + p[3:4] * rin[1][...]
        outs[0][...] = x1
        outs[1][...] = ((x1 * _rms(x1) * p[0:1]) * (1.0 + p[1:2]) + p[2:3]).astype(BF16)

    return _rowwise(fn, [x, o], [gmod], [(d, F32), (d, BF16)], [], name="resid_norm_fwd")


def _final_fwd_bwd(x1, dn, target, par):
    d = x1.shape[1]

    def fn(i, rin, pin, outs, reds):
        p = pin[0][...]
        dnv = rin[1][...]
        x2 = rin[0][...] + p[0:1] * dnv
        rstd = _rms(x2)
        n3 = x2 * rstd
        err = n3 * p[1:2] - rin[2][...]
        dy = err * (1.0 / d)
        dx2 = _rms_bwd(dy * p[1:2], n3, rstd)
        outs[0][...] = dx2
        outs[1][...] = (dx2 * p[0:1]).astype(BF16)
        reds[0][0:1, :] += _colsum(dy * n3)
        reds[0][1:2, :] += _colsum(dx2 * dnv)
        reds[0][2:3, :] += jnp.zeros((1, d), F32) + jnp.sum(err * err)

    return _rowwise(fn, [x1, dn, target], [par], [(d, F32), (d, BF16)], [(SUBLANES, d)], name="final_fwd_bwd")


def _norm2_bwd(x1, dh2, dx2, o, gmod):
    d = x1.shape[1]

    def fn(i, rin, pin, outs, reds):
        p = pin[0][...]
        x1v, dh = rin[0][...], rin[1][...]
        rstd = _rms(x1v)
        n2 = x1v * rstd
        dx1 = rin[2][...] + _rms_bwd(dh * (p[0:1] * (1.0 + p[1:2])), n2, rstd)
        outs[0][...] = dx1
        outs[1][...] = (dx1 * p[3:4]).astype(BF16)
        reds[0][0:1, :] += _colsum(dh * n2 * (1.0 + p[1:2]))
        reds[0][1:2, :] += _colsum(dh * n2 * p[0:1])
        reds[0][2:3, :] += _colsum(dh)
        reds[0][3:4, :] += _colsum(dx1 * rin[3][...])

    return _rowwise(fn, [x1, dh2, dx2, o], [gmod], [(d, F32), (d, BF16)], [(SUBLANES, d)], name="norm2_bwd")


def _norm1_bwd(x, dh_a, dh_b, dh_c, dx1, gmod):
    d = x.shape[1]

    def fn(i, rin, pin, outs, reds):
        p = pin[0][...]
        xv = rin[0][...]
        dh = rin[1][...] + rin[2][...] + rin[3][...]
        rstd = _rms(xv)
        n1 = xv * rstd
        outs[0][...] = rin[4][...] + _rms_bwd(dh * (p[0:1] * (1.0 + p[1:2])), n1, rstd)
        reds[0][0:1, :] += _colsum(dh * n1 * (1.0 + p[1:2]))
        reds[0][1:2, :] += _colsum(dh * n1 * p[0:1])
        reds[0][2:3, :] += _colsum(dh)

    return _rowwise(fn, [x, dh_a, dh_b, dh_c, dx1], [gmod], [(d, F32)], [(SUBLANES, d)], name="norm1_bwd")


RNN_CHUNK = 512


def _shift_down(ref, base, n, j):
    v = ref[pl.ds(base, n + SUBLANES), :]
    return v[SUBLANES:] if j == 0 else pltpu.roll(v, j, 0)[SUBLANES:]


def _shift_up(ref, base, n, j, top_pad):
    v = ref[pl.ds(base + top_pad, n + SUBLANES), :]
    return v[:n] if j == 0 else pltpu.roll(v, n + SUBLANES - j, 0)[:n]


def _one_minus_exp(z):
    series = -z * (1.0 + z * (0.5 + z * (1.0 / 6.0 + z * (1.0 / 24.0 + z * (1.0 / 120.0 + z * (1.0 / 720.0))))))
    return jnp.where(z > -0.1, series, 1.0 - jnp.exp(z))


def _softplus(v):
    return jnp.maximum(v, 0.0) + jnp.log(1.0 + jnp.exp(-jnp.abs(v)))


def _rnn_gates(xc, w, wa, wx, sp):
    xb = xc.astype(BF16)
    ra = _sigmoid(jnp.dot(xb, wa, preferred_element_type=F32) + w[5:6])
    ix = _sigmoid(jnp.dot(xb, wx, preferred_element_type=F32) + w[6:7])
    la = (-LRU_C) * ra * sp
    a = jnp.exp(la)
    mult = jnp.sqrt(_one_minus_exp(2.0 * la))
    return ra, ix, a, mult


def _rnn_fwd(x_rnn, keep, rp, wa_bd, wx_bd):
    s, r = x_rnn.shape
    ts = min(RNN_CHUNK, s)

    def body(x_ref, keep_ref, rp_ref, wa_ref, wx_ref, xc_ref, ra_ref, ix_ref, hs_ref, xpad, a_s, b_s):
        xpad[0:SUBLANES, :] = jnp.zeros((SUBLANES, LANES), F32)
        xpad[SUBLANES:, :] = x_ref[...]
        w = rp_ref[...]
        sp = _softplus(-w[7:8])
        wa, wx = wa_ref[0], wx_ref[0]

        def chunk(c, carry):
            base = pl.multiple_of(c * ts, ts)
            xc = w[4:5] + w[3:4] * _shift_down(xpad, base, ts, 0)
            for j in range(1, 4):
                xc = xc + w[3 - j:4 - j] * _shift_down(xpad, base, ts, j)
            ra, ix, a, mult = _rnn_gates(xc, w, wa, wx, sp)
            kp = keep_ref[pl.ds(base, ts), :]
            xc_ref[pl.ds(base, ts), :] = xc
            ra_ref[pl.ds(base, ts), :] = ra
            ix_ref[pl.ds(base, ts), :] = ix
            a_s[pl.ds(base, ts), :] = a * kp
            b_s[pl.ds(base, ts), :] = jnp.where(kp > 0.0, mult, 1.0) * (ix * xc)
            return carry

        lax.fori_loop(0, s // ts, chunk, 0)

        row = _row((SUBLANES, LANES))

        def tile(i, h):
            r0 = pl.multiple_of(i * SUBLANES, SUBLANES)
            a = a_s[pl.ds(r0, SUBLANES), :]
            b = b_s[pl.ds(r0, SUBLANES), :]
            for sh in (1, 2, 4):
                a_sh = jnp.where(row >= sh, pltpu.roll(a, sh, 0), 1.0)
                b_sh = jnp.where(row >= sh, pltpu.roll(b, sh, 0), 0.0)
                b = a * b_sh + b
                a = a * a_sh
            hv = b + a * h
            hs_ref[pl.ds(r0, SUBLANES), :] = hv
            return hv[SUBLANES - 1:SUBLANES, :]

        lax.fori_loop(0, s // SUBLANES, tile, jnp.zeros((1, LANES), F32), unroll=4)

    col = pl.BlockSpec((s, LANES), lambda g: (0, g))
    return pl.pallas_call(
        body, name="rnn_fwd", grid=(r // LANES,),
        in_specs=[col, pl.BlockSpec((s, 1), lambda g: (0, 0)), pl.BlockSpec((SUBLANES, LANES), lambda g: (0, g)),
                  pl.BlockSpec((1, LANES, LANES), lambda g: (g, 0, 0)), pl.BlockSpec((1, LANES, LANES), lambda g: (g, 0, 0))],
        out_specs=[col] * 4,
        out_shape=[jax.ShapeDtypeStruct((s, r), F32)] * 4,
        scratch_shapes=[pltpu.VMEM((s + SUBLANES, LANES), F32), pltpu.VMEM((s, LANES), F32), pltpu.VMEM((s, LANES), F32)],
        compiler_params=_cparams(("arbitrary",)),
    )(x_rnn, keep, rp, wa_bd, wx_bd)


def _rnn_bwd(x_rnn, xc, ra, ix, hs, dy, keep, rp, wa_bd, wx_bd):
    s, r = x_rnn.shape
    ts = min(RNN_CHUNK, s)
    nt = s // SUBLANES

    def body(x_ref, xc_ref, ra_ref, ix_ref, hs_ref, dy_ref, keep_ref, rp_ref, wa_ref, wx_ref,
             dx_ref, dwa_ref, dwx_ref, red_ref, xpad, hpad, a_s, dh_s, dxc_s):
        zero8 = jnp.zeros((SUBLANES, LANES), F32)
        xpad[0:SUBLANES, :] = zero8
        xpad[SUBLANES:, :] = x_ref[...]
        hpad[0:SUBLANES, :] = zero8
        hpad[SUBLANES:, :] = hs_ref[...]
        a_s[s:, :] = zero8
        dxc_s[s:, :] = zero8
        w = rp_ref[...]
        sp = _softplus(-w[7:8])
        wa, wx = wa_ref[0], wx_ref[0]

        def decay(c, carry):
            base = pl.multiple_of(c * ts, ts)
            a = jnp.exp((-LRU_C) * ra_ref[pl.ds(base, ts), :] * sp)
            a_s[pl.ds(base, ts), :] = a * keep_ref[pl.ds(base, ts), :]
            return carry

        lax.fori_loop(0, s // ts, decay, 0)

        row = _row((SUBLANES, LANES))

        def tile(n, nxt):
            i = nt - 1 - n
            r0 = pl.multiple_of(i * SUBLANES, SUBLANES)
            a_here = a_s[pl.ds(r0, SUBLANES), :]
            a_next = a_s[pl.ds(r0 + SUBLANES, SUBLANES), :]
            a = jnp.where(row < SUBLANES - 1, pltpu.roll(a_here, SUBLANES - 1, 0), a_next[0:1, :])
            b = dy_ref[pl.ds(r0, SUBLANES), :]
            for sh in (1, 2, 4):
                a_sh = jnp.where(row < SUBLANES - sh, pltpu.roll(a, SUBLANES - sh, 0), 1.0)
                b_sh = jnp.where(row < SUBLANES - sh, pltpu.roll(b, SUBLANES - sh, 0), 0.0)
                b = a * b_sh + b
                a = a * a_sh
            dh = b + a * nxt
            dh_s[pl.ds(r0, SUBLANES), :] = dh
            return dh[0:1, :]

        lax.fori_loop(0, nt, tile, jnp.zeros((1, LANES), F32), unroll=4)

        def gates(c, carry):
            dwa, dwx, d_ba, d_bx, d_sp, d_cb = carry
            base = pl.multiple_of(c * ts, ts)
            xcv = xc_ref[pl.ds(base, ts), :]
            rav = ra_ref[pl.ds(base, ts), :]
            ixv = ix_ref[pl.ds(base, ts), :]
            kp = keep_ref[pl.ds(base, ts), :]
            dh = dh_s[pl.ds(base, ts), :]
            h_prev = _shift_down(hpad, base, ts, 1)
            la = (-LRU_C) * rav * sp
            a = jnp.exp(la)
            mult = jnp.sqrt(_one_minus_exp(2.0 * la))
            mult_eff = jnp.where(kp > 0.0, mult, 1.0)
            d_a = dh * h_prev * kp
            d_mult = dh * (ixv * xcv) * kp
            d_ix = dh * mult_eff * xcv
            d_xc = dh * mult_eff * ixv
            d_la = d_a * a - d_mult * (a * a) / mult
            d_pa = d_la * ((-LRU_C) * sp) * rav * (1.0 - rav)
            d_px = d_ix * ixv * (1.0 - ixv)
            xb = xcv.astype(BF16)
            pab, pxb = d_pa.astype(BF16), d_px.astype(BF16)
            tn = (((0,), (0,)), ((), ()))
            nt_ = (((1,), (1,)), ((), ()))
            dwa = dwa + lax.dot_general(xb, pab, tn, preferred_element_type=F32)
            dwx = dwx + lax.dot_general(xb, pxb, tn, preferred_element_type=F32)
            d_xc = d_xc + lax.dot_general(pab, wa, nt_, preferred_element_type=F32)
            d_xc = d_xc + lax.dot_general(pxb, wx, nt_, preferred_element_type=F32)
            dxc_s[pl.ds(base, ts), :] = d_xc
            return (dwa, dwx, d_ba + _colsum(d_pa), d_bx + _colsum(d_px),
                    d_sp + _colsum(d_la * ((-LRU_C) * rav)), d_cb + _colsum(d_xc))

        z1 = jnp.zeros((1, LANES), F32)
        zw = jnp.zeros((LANES, LANES), F32)
        dwa, dwx, d_ba, d_bx, d_sp, d_cb = lax.fori_loop(0, s // ts, gates, (zw, zw, z1, z1, z1, z1))
        dwa_ref[0] = dwa
        dwx_ref[0] = dwx

        def conv(c, carry):
            base = pl.multiple_of(c * ts, ts)
            d_here = dxc_s[pl.ds(base, ts), :]
            dx = w[3:4] * d_here
            for j in range(1, 4):
                dx = dx + w[3 - j:4 - j] * _shift_up(dxc_s, base, ts, j, 0)
            dx_ref[pl.ds(base, ts), :] = dx.astype(BF16)
            return tuple(carry[k] + _colsum(d_here * _shift_down(xpad, base, ts, 3 - k)) for k in range(4))

        d_w = lax.fori_loop(0, s // ts, conv, (z1, z1, z1, z1))
        d_lru = d_sp * (-_sigmoid(-w[7:8]))
        red_ref[...] = jnp.concatenate(list(d_w) + [d_cb, d_ba, d_bx, d_lru], axis=0)

    col = pl.BlockSpec((s, LANES), lambda g: (0, g))
    par = pl.BlockSpec((SUBLANES, LANES), lambda g: (0, g))
    wsp = pl.BlockSpec((1, LANES, LANES), lambda g: (g, 0, 0))
    return pl.pallas_call(
        body, name="rnn_bwd", grid=(r // LANES,),
        in_specs=[col] * 6 + [pl.BlockSpec((s, 1), lambda g: (0, 0)), par, wsp, wsp],
        out_specs=[col, wsp, wsp, par],
        out_shape=[jax.ShapeDtypeStruct((s, r), BF16), jax.ShapeDtypeStruct((r // LANES, LANES, LANES), F32),
                   jax.ShapeDtypeStruct((r // LANES, LANES, LANES), F32), jax.ShapeDtypeStruct((SUBLANES, r), F32)],
        scratch_shapes=[pltpu.VMEM((s + SUBLANES, LANES), F32), pltpu.VMEM((s + SUBLANES, LANES), F32),
                        pltpu.VMEM((s + SUBLANES, LANES), F32), pltpu.VMEM((s, LANES), F32),
                        pltpu.VMEM((s + SUBLANES, LANES), F32)],
        compiler_params=_cparams(("arbitrary",)),
    )(x_rnn, xc, ra, ix, hs, dy, keep, rp, wa_bd, wx_bd)


ATT_BLOCK = 512
NT_DIMS = (((1,), (1,)), ((), ()))
TN_DIMS = (((0,), (0,)), ((), ()))


def _att_scores(q, kvb, krb, i, j, t):
    k_eff = jnp.where(_lane(kvb.shape) < QK_NOPE, kvb, krb)
    sc = lax.dot_general(q, k_eff, NT_DIMS, preferred_element_type=F32) * ATT_SCALE
    qpos = i * t + lax.broadcasted_iota(jnp.int32, sc.shape, 0)
    kpos = j * t + lax.broadcasted_iota(jnp.int32, sc.shape, 1)
    return jnp.where(kpos <= qpos, sc, -jnp.inf), k_eff


def _att_fwd(q, kv, kr):
    s = q.shape[0]
    t = min(ATT_BLOCK, s)
    nb = s // t

    def body(q_ref, kv_ref, kr_ref, y_ref, lse_ref, m_s, l_s, acc_s):
        i, j = pl.program_id(1), pl.program_id(2)

        @pl.when(j == 0)
        def _():
            m_s[...] = jnp.full(m_s.shape, -jnp.inf, F32)
            l_s[...] = jnp.zeros(l_s.shape, F32)
            acc_s[...] = jnp.zeros(acc_s.shape, F32)

        @pl.when(j <= i)
        def _():
            krb = kr_ref[...]
            for hh in range(2):
                sl = slice(hh * LANES, (hh + 1) * LANES)
                kvb = kv_ref[:, sl]
                sc, _ = _att_scores(q_ref[:, sl], kvb, krb, i, j, t)
                m_prev = m_s[hh]
                m_new = jnp.maximum(m_prev, jnp.max(sc, axis=-1, keepdims=True))
                alpha = jnp.exp(m_prev - m_new)
                p = jnp.exp(sc - m_new[:, 0:1])
                l_s[hh] = alpha * l_s[hh] + jnp.sum(p, axis=-1, keepdims=True)
                acc_s[hh] = alpha * acc_s[hh] + jnp.dot(p.astype(BF16), kvb, preferred_element_type=F32)
                m_s[hh] = m_new

        @pl.when(j == i)
        def _():
            lane = _lane((t, LANES))
            o0 = acc_s[0] / l_s[0]
            o1 = acc_s[1] / l_s[1]
            y_ref[...] = jnp.where(lane < V_HEAD, pltpu.roll(o0, V_HEAD, 1), o1).astype(BF16)
            lse_ref[...] = jnp.where(lane < V_HEAD, m_s[0] + jnp.log(l_s[0]), m_s[1] + jnp.log(l_s[1]))

    return pl.pallas_call(
        body, name="att_fwd", grid=(N_HEADS // 2, nb, nb),
        in_specs=[pl.BlockSpec((t, 2 * LANES), lambda p, i, j: (i, p)),
                  pl.BlockSpec((t, 2 * LANES), lambda p, i, j: (jnp.minimum(j, i), p)),
                  pl.BlockSpec((t, LANES), lambda p, i, j: (jnp.minimum(j, i), 0))],
        out_specs=[pl.BlockSpec((t, LANES), lambda p, i, j: (i, p))] * 2,
        out_shape=[jax.ShapeDtypeStruct((s, N_HEADS * V_HEAD), BF16), jax.ShapeDtypeStruct((s, N_HEADS * V_HEAD), F32)],
        scratch_shapes=[pltpu.VMEM((2, t, LANES), F32)] * 3,
        compiler_params=_cparams(("arbitrary", "arbitrary", "arbitrary")),
    )(q, kv, kr)


def _att_bwd(q, kv, kr, y, lse, dy):
    s = q.shape[0]
    t = min(ATT_BLOCK, s)
    nb = s // t

    def body(q_ref, kv_ref, kr_ref, y_ref, lse_ref, dy_ref, dq_ref, dkv_ref, dkr_ref, dkv_s):
        p_, j, i = pl.program_id(0), pl.program_id(1), pl.program_id(2)

        @pl.when((p_ == 0) & (j == 0) & (i == 0))
        def _():
            dkr_ref[...] = jnp.zeros(dkr_ref.shape, F32)

        @pl.when((j == 0) & (i == 0))
        def _():
            dq_ref[...] = jnp.zeros(dq_ref.shape, F32)

        @pl.when(i == 0)
        def _():
            dkv_s[...] = jnp.zeros(dkv_s.shape, F32)

        @pl.when(i >= j)
        def _():
            lane = _lane((t, LANES))
            krb = kr_ref[...]
            dyv = dy_ref[...]
            yv = y_ref[...].astype(F32)
            lsev = lse_ref[...]
            rows = pl.ds(pl.multiple_of(i * t, t), t)
            dkr_acc = jnp.zeros((t, LANES), F32)
            for hh in range(2):
                sl = slice(hh * LANES, (hh + 1) * LANES)
                qb, kvb = q_ref[:, sl], kv_ref[:, sl]
                sc, k_eff = _att_scores(qb, kvb, krb, i, j, t)
                if hh == 0:
                    do_pad = jnp.where(lane >= V_HEAD, pltpu.roll(dyv, V_HEAD, 1), 0.0)
                    o_pad = jnp.where(lane >= V_HEAD, pltpu.roll(yv, V_HEAD, 1), 0.0)
                    lse_h = lsev[:, 0:1]
                else:
                    do_pad = jnp.where(lane >= V_HEAD, dyv, 0.0)
                    o_pad = jnp.where(lane >= V_HEAD, yv, 0.0)
                    lse_h = lsev[:, LANES - 1:LANES]
                delta = jnp.sum(do_pad * o_pad, axis=-1, keepdims=True)
                prob = jnp.exp(sc - lse_h)
                dob = do_pad.astype(BF16)
                dv = lax.dot_general(prob.astype(BF16), dob, TN_DIMS, preferred_element_type=F32)
                dp = lax.dot_general(dob, kvb, NT_DIMS, preferred_element_type=F32)
                ds = (prob * (dp - delta) * ATT_SCALE).astype(BF16)
                dq_ref[rows, sl] += jnp.dot(ds, k_eff, preferred_element_type=F32)
                dk_eff = lax.dot_general(ds, qb, TN_DIMS, preferred_element_type=F32)
                dkv_s[hh] += dv + jnp.where(lane < QK_NOPE, dk_eff, 0.0)
                dkr_acc = dkr_acc + jnp.where(lane >= QK_NOPE, dk_eff, 0.0)
            dkr_ref[pl.ds(pl.multiple_of(j * t, t), t), :] += dkr_acc

        @pl.when(i == nb - 1)
        def _():
            dkv_ref[:, 0:LANES] = dkv_s[0].astype(BF16)
            dkv_ref[:, LANES:] = dkv_s[1].astype(BF16)

    qi = lambda p, j, i: (jnp.maximum(i, j), p)
    return pl.pallas_call(
        body, name="att_bwd", grid=(N_HEADS // 2, nb, nb),
        in_specs=[pl.BlockSpec((t, 2 * LANES), qi),
                  pl.BlockSpec((t, 2 * LANES), lambda p, j, i: (j, p)),
                  pl.BlockSpec((t, LANES), lambda p, j, i: (j, 0)),
                  pl.BlockSpec((t, LANES), qi), pl.BlockSpec((t, LANES), qi), pl.BlockSpec((t, LANES), qi)],
        out_specs=[pl.BlockSpec((s, 2 * LANES), lambda p, j, i: (0, p)),
                   pl.BlockSpec((t, 2 * LANES), lambda p, j, i: (j, p)),
                   pl.BlockSpec((s, LANES), lambda p, j, i: (0, 0))],
        out_shape=[jax.ShapeDtypeStruct((s, N_HEADS * LANES), F32), jax.ShapeDtypeStruct((s, N_HEADS * LANES), BF16),
                   jax.ShapeDtypeStruct((s, LANES), F32)],
        scratch_shapes=[pltpu.VMEM((2, t, LANES), F32)],
        compiler_params=_cparams(("arbitrary", "arbitrary", "arbitrary")),
    )(q, kv, kr, y, lse, dy)


FFN_COLS = 256


def _ffn_conv(pad_ref, w, base, n):
    u = w[3:4] + w[2:3] * _shift_down(pad_ref, base, n, 0)
    for j in range(1, 3):
        u = u + w[2 - j:3 - j] * _shift_down(pad_ref, base, n, j)
    return u


def _ffn_act_fwd(up, fp):
    s, f2 = up.shape
    f = f2 // 2
    tc = FFN_COLS
    ts = min(RNN_CHUNK, s)
    nfb = f // tc

    def body(ug_ref, uv_ref, wg_ref, wv_ref, act_ref, gpad, vpad):
        zero8 = jnp.zeros((SUBLANES, tc), F32)
        gpad[0:SUBLANES, :] = zero8
        vpad[0:SUBLANES, :] = zero8
        gpad[SUBLANES:, :] = ug_ref[...]
        vpad[SUBLANES:, :] = uv_ref[...]
        wg, wv = wg_ref[...], wv_ref[...]

        def chunk(c, carry):
            base = pl.multiple_of(c * ts, ts)
            g = _ffn_conv(gpad, wg, base, ts)
            v = _ffn_conv(vpad, wv, base, ts)
            act_ref[pl.ds(base, ts), :] = (g * _sigmoid(g) * v).astype(BF16)
            return carry

        lax.fori_loop(0, s // ts, chunk, 0)

    return pl.pallas_call(
        body, name="ffn_act_fwd", grid=(nfb,),
        in_specs=[pl.BlockSpec((s, tc), lambda b: (0, b)), pl.BlockSpec((s, tc), lambda b: (0, b + nfb)),
                  pl.BlockSpec((SUBLANES, tc), lambda b: (0, b)), pl.BlockSpec((SUBLANES, tc), lambda b: (0, b + nfb))],
        out_specs=pl.BlockSpec((s, tc), lambda b: (0, b)),
        out_shape=jax.ShapeDtypeStruct((s, f), BF16),
        scratch_shapes=[pltpu.VMEM((s + SUBLANES, tc), F32)] * 2,
        compiler_params=_cparams(("arbitrary",)),
    )(up, up, fp, fp)


def _ffn_act_bwd(up, dact, fp):
    s, f2 = up.shape
    f = f2 // 2
    tc = FFN_COLS
    ts = min(RNN_CHUNK, s)
    nfb = f // tc

    def body(ug_ref, uv_ref, da_ref, wg_ref, wv_ref, dup_ref, red_ref, gpad, vpad, dgs, dvs):
        half = pl.program_id(1)
        wg, wv = wg_ref[...], wv_ref[...]

        @pl.when(half == 0)
        def _():
            zero8 = jnp.zeros((SUBLANES, tc), F32)
            gpad[0:SUBLANES, :] = zero8
            vpad[0:SUBLANES, :] = zero8
            gpad[SUBLANES:, :] = ug_ref[...]
            vpad[SUBLANES:, :] = uv_ref[...]
            dgs[s:, :] = zero8
            dvs[s:, :] = zero8

            def act(c, carry):
                base = pl.multiple_of(c * ts, ts)
                g = _ffn_conv(gpad, wg, base, ts)
                v = _ffn_conv(vpad, wv, base, ts)
                da = da_ref[pl.ds(base, ts), :]
                sg = _sigmoid(g)
                dgs[pl.ds(base, ts), :] = da * v * (sg * (1.0 + g * (1.0 - sg)))
                dvs[pl.ds(base, ts), :] = da * (g * sg)
                return carry

            lax.fori_loop(0, s // ts, act, 0)

        def conv_t(src, pad, w, out_ref, red_ref):
            def chunk(c, carry):
                base = pl.multiple_of(c * ts, ts)
                d_here = src[pl.ds(base, ts), :]
                dx = w[2:3] * d_here
                for j in range(1, 3):
                    dx = dx + w[2 - j:3 - j] * _shift_up(src, base, ts, j, 0)
                out_ref[pl.ds(base, ts), :] = dx.astype(BF16)
                taps = tuple(carry[k] + _colsum(d_here * _shift_down(pad, base, ts, 2 - k)) for k in range(3))
                return taps + (carry[3] + _colsum(d_here),)

            z1 = jnp.zeros((1, tc), F32)
            red = lax.fori_loop(0, s // ts, chunk, (z1, z1, z1, z1))
            red_ref[...] = jnp.concatenate(list(red) + [jnp.zeros((4, tc), F32)], axis=0)

        @pl.when(half == 0)
        def _():
            conv_t(dgs, gpad, wg, dup_ref, red_ref)

        @pl.when(half == 1)
        def _():
            conv_t(dvs, vpad, wv, dup_ref, red_ref)

    gcol = pl.BlockSpec((s, tc), lambda b, h: (0, b))
    vcol = pl.BlockSpec((s, tc), lambda b, h: (0, b + nfb))
    gpar = pl.BlockSpec((SUBLANES, tc), lambda b, h: (0, b))
    vpar = pl.BlockSpec((SUBLANES, tc), lambda b, h: (0, b + nfb))
    return pl.pallas_call(
        body, name="ffn_act_bwd", grid=(nfb, 2),
        in_specs=[gcol, vcol, gcol, gpar, vpar],
        out_specs=[pl.BlockSpec((s, tc), lambda b, h: (0, b + h * nfb)),
                   pl.BlockSpec((SUBLANES, tc), lambda b, h: (0, b + h * nfb))],
        out_shape=[jax.ShapeDtypeStruct((s, f2), BF16), jax.ShapeDtypeStruct((SUBLANES, f2), F32)],
        scratch_shapes=[pltpu.VMEM((s + SUBLANES, tc), F32)] * 4,
        compiler_params=_cparams(("arbitrary", "arbitrary")),
    )(up, up, dact, fp, fp)


def _rows8(rows, width):
    rows = [r.reshape(1, width).astype(F32) for r in rows]
    return jnp.concatenate(rows + [jnp.zeros((SUBLANES - len(rows), width), F32)], axis=0)


def _block_diag(w):
    n, b, _ = w.shape
    w = w.reshape(n // 2, 2, b, b)
    z = jnp.zeros((n // 2, b, b), w.dtype)
    top = jnp.concatenate([w[:, 0], z], axis=2)
    bot = jnp.concatenate([z, w[:, 1]], axis=2)
    return jnp.concatenate([top, bot], axis=1)


def _block_diag_t(bd):
    n, b2, _ = bd.shape
    b = b2 // 2
    return jnp.stack([bd[:, :b, :b], bd[:, b:, b:]], axis=1).reshape(2 * n, b, b)


def _local_step(x, mod, positions, target, wts, sm):
    s, d = x.shape
    o_rnn, o_mla = D_RNN, D_RNN + Q_LORA + KV_LORA + QK_ROPE
    w_in = wts["w_in"]
    w_in_rnn = w_in[:, :o_rnn]
    w_in_mla = jnp.concatenate([w_in[:, o_rnn:o_mla], jnp.zeros((d, MLA_W - (o_mla - o_rnn)), w_in.dtype)], axis=1)
    w_in_g = w_in[:, o_mla:]
    hd = QK_NOPE + QK_ROPE
    w_uq_p = jnp.pad(wts["w_uq"].reshape(Q_LORA, N_HEADS, hd), ((0, 0), (0, 0), (0, LANES - hd))).reshape(Q_LORA, N_HEADS * LANES)
    w_ukv = wts["w_ukv"]
    wa_bd = _block_diag(sm["w_gate_a"]).astype(BF16)
    wx_bd = _block_diag(sm["w_gate_x"]).astype(BF16)

    pos = positions.reshape(s)
    half = QK_ROPE // 2
    inv_freq = ROPE_THETA ** (-jnp.arange(half, dtype=F32) / half)
    ang = pos.astype(F32)[:, None] * inv_freq
    cos, sin = jnp.cos(ang), jnp.sin(ang)
    rot_c = jnp.concatenate([jnp.ones((s, QK_NOPE), F32), cos, cos, jnp.ones((s, LANES - hd), F32)], axis=1)
    rot_s = jnp.concatenate([jnp.zeros((s, QK_NOPE), F32), -sin, sin, jnp.zeros((s, LANES - hd), F32)], axis=1)
    keep = (pos != 0).astype(F32).reshape(s, 1)

    gmod1 = _rows8([sm["norm1_g"], mod[1], mod[0]], d)
    gmod2 = _rows8([sm["norm2_g"], mod[4], mod[3], mod[2]], d)
    rp = jnp.concatenate([sm["conv_w"].reshape(4, D_RNN), _rows8([sm["conv_b"], sm["b_gate_a"], sm["b_gate_x"], sm["lru_param"]], D_RNN)[:4]], axis=0)
    fp = _rows8([sm["ffn_conv_w"][0], sm["ffn_conv_w"][1], sm["ffn_conv_w"][2], sm["ffn_conv_b"]], 2 * D_FF)
    ng = _rows8([jnp.concatenate([sm["q_norm_g"].reshape(-1), sm["kv_norm_g"].reshape(-1), jnp.zeros((MLA_W - Q_LORA - KV_LORA,), F32)])], MLA_W)
    fpar = _rows8([mod[5], sm["final_g"]], d)

    h = _norm_mod_fwd(x, gmod1, "norm1_fwd")
    proj_rnn = _mm(h, w_in_rnn, name="mm_in_rnn")
    proj_mla = _mm(h, w_in_mla, name="mm_in_mla")
    proj_g = _mm(h, w_in_g, name="mm_in_g")
    xc, ra, ix, hs = _rnn_fwd(proj_rnn, keep, rp, wa_bd, wx_bd)
    qn, kvn, kr = _mla_prep_fwd(proj_mla, rot_c, rot_s, ng)
    q_rot = _rope_heads(_mm(qn, w_uq_p, name="mm_uq"), rot_c, rot_s, False, "rope_fwd")
    kv = _mm(kvn, w_ukv, out_dtype=BF16, name="mm_ukv")
    y_mla, lse = _att_fwd(q_rot, kv, kr)
    pr = _mm(hs, wts["w_proj_rnn"], name="mm_proj_rnn")
    pm = _mm(y_mla, wts["w_proj_mla"], name="mm_proj_mla")
    merged = _merge_fwd(pr, pm, proj_g)
    o = _mm(merged, wts["w_out"], name="mm_out")
    x1, h2 = _resid_norm_fwd(x, o, gmod2)
    up = _mm(h2, wts["w_up"], name="mm_up")
    act = _ffn_act_fwd(up, fp)
    dn = _mm(act, wts["w_down"], name="mm_down")

    dx2, ddn, red_f = _final_fwd_bwd(x1, dn, target, fpar)
    gw = {}
    dact = _mm(ddn, wts["w_down"], tb=True, name="mm_d_act")
    gw["w_down"] = _mm(act, ddn, ta=True, out_dtype=BF16, name="mm_dw_down")
    dup, red_ffn = _ffn_act_bwd(up, dact, fp)
    dh2 = _mm(dup, wts["w_up"], tb=True, name="mm_d_h2")
    gw["w_up"] = _mm(h2, dup, ta=True, out_dtype=BF16, name="mm_dw_up")
    dx1, do, red_2 = _norm2_bwd(x1, dh2, dx2, o, gmod2)
    dmerged = _mm(do, wts["w_out"], tb=True, name="mm_d_merged")
    gw["w_out"] = _mm(merged, do, ta=True, out_dtype=BF16, name="mm_dw_out")
    dpr, dpm, dg = _merge_bwd(dmerged, pr, pm, proj_g)
    dy_rnn = _mm(dpr, wts["w_proj_rnn"], tb=True, name="mm_d_yrnn")
    gw["w_proj_rnn"] = _mm(hs, dpr, ta=True, out_dtype=BF16, name="mm_dw_proj_rnn")
    dy_mla = _mm(dpm, wts["w_proj_mla"], tb=True, name="mm_d_ymla")
    gw["w_proj_mla"] = _mm(y_mla, dpm, ta=True, out_dtype=BF16, name="mm_dw_proj_mla")
    dq_rot, dkv, dkr = _att_bwd(q_rot, kv, kr, y_mla, lse, dy_mla)
    dq = _rope_heads(dq_rot, rot_c, rot_s, True, "rope_bwd")
    dqn = _mm(dq, w_uq_p, tb=True, name="mm_d_qn")
    dw_uq_p = _mm(qn, dq, ta=True, out_dtype=BF16, name="mm_dw_uq")
    gw["w_uq"] = dw_uq_p.reshape(Q_LORA, N_HEADS, LANES)[:, :, :hd].reshape(Q_LORA, N_HEADS * hd)
    dkvn = _mm(dkv, w_ukv, tb=True, name="mm_d_kvn")
    gw["w_ukv"] = _mm(kvn, dkv, ta=True, out_dtype=BF16, name="mm_dw_ukv")
    dproj_mla, red_m = _mla_prep_bwd(proj_mla, dqn, dkvn, dkr, rot_c, rot_s, ng)
    dx_rnn, dwa_bd, dwx_bd, red_r = _rnn_bwd(proj_rnn, xc, ra, ix, hs, dy_rnn, keep, rp, wa_bd, wx_bd)
    dh_a = _mm(dx_rnn, w_in_rnn, tb=True, name="mm_d_h_rnn")
    dh_b = _mm(dproj_mla, w_in_mla, tb=True, name="mm_d_h_mla")
    dh_c = _mm(dg, w_in_g, tb=True, name="mm_d_h_g")
    gw["w_in"] = jnp.concatenate([
        _mm(h, dx_rnn, ta=True, out_dtype=BF16, name="mm_dw_in_rnn"),
        _mm(h, dproj_mla, ta=True, out_dtype=BF16, name="mm_dw_in_mla")[:, :o_mla - o_rnn],
        _mm(h, dg, ta=True, out_dtype=BF16, name="mm_dw_in_g")], axis=1)
    grad_x, red_1 = _norm1_bwd(x, dh_a, dh_b, dh_c, dx1, gmod1)

    gs = {
        "norm1_g": red_1[0], "conv_w": red_r[0:4], "conv_b": red_r[4], "w_gate_a": _block_diag_t(dwa_bd),
        "b_gate_a": red_r[5], "w_gate_x": _block_diag_t(dwx_bd), "b_gate_x": red_r[6], "lru_param": red_r[7],
        "q_norm_g": red_m[0, :Q_LORA], "kv_norm_g": red_m[0, Q_LORA:Q_LORA + KV_LORA], "norm2_g": red_2[0],
        "ffn_conv_w": red_ffn[0:3], "ffn_conv_b": red_ffn[3], "final_g": red_f[0],
    }
    dmod = jnp.stack([red_1[2], red_1[1], red_2[3], red_2[2], red_2[1], red_f[1]], axis=0)
    return red_f[2, 0], grad_x, gw, gs, dmod


MESH_IDS = pl.DeviceIdType.MESH
HBM_SPEC = pl.BlockSpec(memory_space=pltpu.HBM)


def _my_slot():
    return 4 * lax.axis_index("x") + 2 * lax.axis_index("y") + lax.axis_index("c")


def _all_gather(arrs, name):
    n = len(arrs)

    def body(*refs):
        ins, outs = refs[:n], refs[n:2 * n]
        send_sems, recv_sems, local_sems = refs[2 * n:]
        x, y, c = lax.axis_index("x"), lax.axis_index("y"), lax.axis_index("c")
        me, sibling = (x, y, c), (x, y, 1 - c)
        chips = [(1 - x, y), (x, 1 - y), (1 - x, 1 - y)]

        def slot(dev):
            return 4 * dev[0] + 2 * dev[1] + dev[2]

        def copy(a, k, block, to, src=None):
            dst = outs[a].at[slot(block)]
            return pltpu.make_async_remote_copy(
                src_ref=dst if src is None else src, dst_ref=dst, send_sem=send_sems.at[a, k], recv_sem=recv_sems.at[a, k],
                device_id=to, device_id_type=MESH_IDS)

        mine = [pltpu.make_async_copy(ins[a], outs[a].at[slot(me)], local_sems.at[a]) for a in range(n)]
        for cp in mine:
            cp.start()
        first = []
        for a in range(n):
            first.append(copy(a, 0, me, sibling, src=ins[a]))
            first += [copy(a, 1 + j, me, (*chip, c), src=ins[a]) for j, chip in enumerate(chips)]
        for cp in first:
            cp.start()
        passed = []
        for j, chip in enumerate(chips):
            for a in range(n):
                copy(a, 1 + j, (*chip, c), me).wait_recv()
                fwd = copy(a, 4 + j, (*chip, c), sibling)
                fwd.start()
                passed.append(fwd)
        for a in range(n):
            copy(a, 0, sibling, me).wait_recv()
            for j, chip in enumerate(chips):
                copy(a, 4 + j, (*chip, 1 - c), me).wait_recv()
        for cp in first + passed:
            cp.wait_send()
        for cp in mine:
            cp.wait()

    return pl.pallas_call(
        body, name=name,
        in_specs=[HBM_SPEC] * n, out_specs=[HBM_SPEC] * n,
        out_shape=[jax.ShapeDtypeStruct((N_DEV,) + a.shape, a.dtype) for a in arrs],
        scratch_shapes=[pltpu.SemaphoreType.DMA((n, 7)), pltpu.SemaphoreType.DMA((n, 7)), pltpu.SemaphoreType.DMA((n,))],
    )(*arrs)


def _all_to_all(arrs, name):
    n = len(arrs)
    flips = [(dx, dy, dc) for dx in (0, 1) for dy in (0, 1) for dc in (0, 1)][1:]

    def body(*refs):
        ins, outs = refs[:n], refs[n:2 * n]
        send_sems, recv_sems, local_sems = refs[2 * n:]
        x, y, c = lax.axis_index("x"), lax.axis_index("y"), lax.axis_index("c")
        me = 4 * x + 2 * y + c
        mine = [pltpu.make_async_copy(ins[a].at[me], outs[a].at[me], local_sems.at[a]) for a in range(n)]
        for cp in mine:
            cp.start()
        sends = []
        for k, (dx, dy, dc) in enumerate(flips):
            peer = (x ^ dx, y ^ dy, c ^ dc)
            peer_slot = 4 * peer[0] + 2 * peer[1] + peer[2]
            for a in range(n):
                sends.append(pltpu.make_async_remote_copy(
                    src_ref=ins[a].at[peer_slot], dst_ref=outs[a].at[me], send_sem=send_sems.at[a, k],
                    recv_sem=recv_sems.at[a, k], device_id=peer, device_id_type=MESH_IDS))
        for cp in sends:
            cp.start()
        for k, (dx, dy, dc) in enumerate(flips):
            peer_slot = 4 * (x ^ dx) + 2 * (y ^ dy) + (c ^ dc)
            for a in range(n):
                pltpu.make_async_remote_copy(
                    src_ref=ins[a].at[peer_slot], dst_ref=outs[a].at[peer_slot], send_sem=send_sems.at[a, k],
                    recv_sem=recv_sems.at[a, k], device_id=(x ^ dx, y ^ dy, c ^ dc), device_id_type=MESH_IDS).wait_recv()
        for cp in sends:
            cp.wait_send()
        for cp in mine:
            cp.wait()

    return pl.pallas_call(
        body, name=name,
        in_specs=[HBM_SPEC] * n, out_specs=[HBM_SPEC] * n,
        out_shape=[jax.ShapeDtypeStruct(a.shape, a.dtype) for a in arrs],
        scratch_shapes=[pltpu.SemaphoreType.DMA((n, 7)), pltpu.SemaphoreType.DMA((n, 7)), pltpu.SemaphoreType.DMA((n,))],
    )(*arrs)


def _sum_sources(parts, name):
    k, r, c = parts.shape
    tr = r if k * r * c <= 2 * 1024 * 1024 else _pick(r, (512, 256, 128, 64, 32, 16, 8))

    def body(p_ref, o_ref):
        acc = p_ref[0].astype(F32)
        for s in range(1, k):
            acc = acc + p_ref[s].astype(F32)
        o_ref[...] = acc

    return pl.pallas_call(
        body, name=name, grid=(r // tr,),
        in_specs=[pl.BlockSpec((k, tr, c), lambda i: (0, i, 0))],
        out_specs=pl.BlockSpec((tr, c), lambda i: (i, 0)),
        out_shape=jax.ShapeDtypeStruct((r, c), F32),
        compiler_params=_cparams(("arbitrary",)),
    )(parts)


def _adamw(parts, w, m, v, name):
    k, r, c = parts.shape
    tr = r if r * c <= 256 * 1024 else _pick(r, (256, 128, 64, 32, 16, 8))

    def body(p_ref, w_ref, m_ref, v_ref, g_ref, d_ref, nm_ref, nv_ref):
        g = p_ref[0].astype(F32)
        for s in range(1, k):
            g = g + p_ref[s].astype(F32)
        m_new = ADAM_B1 * m_ref[...] + (1.0 - ADAM_B1) * g
        v_new = ADAM_B2 * v_ref[...] + (1.0 - ADAM_B2) * jnp.square(g)
        m_hat = m_new / (1.0 - ADAM_B1 ** ADAM_STEP)
        v_hat = v_new / (1.0 - ADAM_B2 ** ADAM_STEP)
        g_ref[...] = g
        d_ref[...] = -ADAM_LR * (m_hat / (jnp.sqrt(v_hat) + ADAM_EPS) + ADAM_WD * w_ref[...])
        nm_ref[...] = m_new
        nv_ref[...] = v_new

    blk = pl.BlockSpec((tr, c), lambda i: (i, 0))
    return pl.pallas_call(
        body, name=name, grid=(r // tr,),
        in_specs=[pl.BlockSpec((k, tr, c), lambda i: (0, i, 0)), blk, blk, blk],
        out_specs=[blk] * 4,
        out_shape=[jax.ShapeDtypeStruct((r, c), F32)] * 4,
        compiler_params=_cparams(("arbitrary",)),
    )(parts, w, m, v)


def _silu(v):
    return v * _sigmoid(v)


def _ada_fwd(c_all, w, b):
    def body(c_ref, w_ref, b_ref, o_ref):
        ca = _silu(c_ref[...]).astype(BF16)
        o_ref[...] = jnp.dot(ca, w_ref[...].astype(BF16), preferred_element_type=F32) + b_ref[...]

    return pl.pallas_call(
        body, name="ada_fwd", out_shape=jax.ShapeDtypeStruct((c_all.shape[0], w.shape[1]), F32),
        compiler_params=_cparams(),
    )(c_all, w, b)


def _ada_bwd(c_all, dmod):
    def body(c_ref, d_ref, o_ref):
        ca = _silu(c_ref[...]).astype(BF16).astype(F32)
        dm = d_ref[...].astype(BF16).astype(F32)
        acc = jnp.zeros(o_ref.shape, F32)
        for bi in range(c_all.shape[0]):
            acc = acc + jnp.transpose(ca[bi:bi + 1, :]) * dm[bi:bi + 1, :]
        o_ref[...] = acc

    return pl.pallas_call(
        body, name="ada_bwd", out_shape=jax.ShapeDtypeStruct((c_all.shape[1], dmod.shape[1]), F32),
        compiler_params=_cparams(),
    )(c_all, dmod)


COL_SHARDED = ("w_in", "w_uq", "w_ukv", "w_up")
ROW_SHARDED = ("w_proj_rnn", "w_proj_mla", "w_out", "w_down")
REPLICATED = ("b_ada", "norm1_g", "conv_b", "w_gate_a", "b_gate_a", "w_gate_x", "b_gate_x", "lru_param", "q_norm_g",
              "kv_norm_g", "norm2_g", "ffn_conv_b", "final_g")
WEIGHTS = ("w_ada", "b_ada", "norm1_g", "w_in", "conv_w", "conv_b", "w_gate_a", "b_gate_a", "w_gate_x", "b_gate_x",
           "lru_param", "q_norm_g", "w_uq", "kv_norm_g", "w_ukv", "w_proj_rnn", "w_proj_mla", "w_out", "norm2_g", "w_up",
           "ffn_conv_w", "ffn_conv_b", "w_down", "final_g")
PACK_LANES = 128


def _pack(vecs):
    flat = jnp.concatenate([v.reshape(-1).astype(F32) for v in vecs])
    pad = (-flat.shape[0]) % (PACK_LANES * SUBLANES)
    return jnp.concatenate([flat, jnp.zeros((pad,), F32)]).reshape(-1, PACK_LANES)


def _unpack(packed, shapes):
    flat = packed.reshape(-1)
    out, off = [], 0
    for shp in shapes:
        size = math.prod(shp)
        out.append(flat[off:off + size].reshape(shp))
        off += size
    return out


def kernel(x, c, positions, w_ada, b_ada, norm1_g, w_in, conv_w, conv_b, w_gate_a, b_gate_a, w_gate_x, b_gate_x, lru_param, q_norm_g, w_uq, kv_norm_g, w_ukv, w_proj_rnn, w_proj_mla, w_out, norm2_g, w_up, ffn_conv_w, ffn_conv_b, w_down, final_g, loss_target, m_w_ada, m_b_ada, m_norm1_g, m_w_in, m_conv_w, m_conv_b, m_w_gate_a, m_b_gate_a, m_w_gate_x, m_b_gate_x, m_lru_param, m_q_norm_g, m_w_uq, m_kv_norm_g, m_w_ukv, m_w_proj_rnn, m_w_proj_mla, m_w_out, m_norm2_g, m_w_up, m_ffn_conv_w, m_ffn_conv_b, m_w_down, m_final_g, v_w_ada, v_b_ada, v_norm1_g, v_w_in, v_conv_w, v_conv_b, v_w_gate_a, v_b_gate_a, v_w_gate_x, v_b_gate_x, v_lru_param, v_q_norm_g, v_w_uq, v_kv_norm_g, v_w_ukv, v_w_proj_rnn, v_w_proj_mla, v_w_out, v_norm2_g, v_w_up, v_ffn_conv_w, v_ffn_conv_b, v_w_down, v_final_g):
    args = dict(locals())
    w = {n: args[n] for n in WEIGHTS}
    m = {n: args["m_" + n] for n in WEIGHTS}
    v = {n: args["v_" + n] for n in WEIGHTS}
    s, d = x.shape[1], x.shape[2]
    me = _my_slot()
    def two_d(a):
        assert a.ndim == 3 and a.shape[0] == 1, a.shape
        return a[0]

    big = COL_SHARDED + ROW_SHARDED
    gathered = _all_gather([two_d(w[n]).astype(BF16) for n in big] + [c, two_d(conv_w), two_d(ffn_conv_w)], "gather_weights")
    wts = {}
    for n, g in zip(big, gathered):
        k, r, cc = g.shape
        wts[n] = jnp.transpose(g, (1, 0, 2)).reshape(r, k * cc) if n in COL_SHARDED else g.reshape(k * r, cc)
    c_all = gathered[len(big)].reshape(N_DEV, d)
    conv_w_all = jnp.transpose(gathered[len(big) + 1], (1, 0, 2)).reshape(conv_w.shape[1], -1)
    ffn_conv_w_all = jnp.transpose(gathered[len(big) + 2], (1, 0, 2)).reshape(ffn_conv_w.shape[1], -1)

    ada_cols = w_ada.shape[2]
    b_cols = lax.dynamic_slice(b_ada, (0, me * ada_cols), (1, ada_cols))
    mod_cols = _ada_fwd(c_all, w_ada[0], b_cols)
    mod_all, = _all_gather([mod_cols], "gather_mod")
    mod = lax.dynamic_index_in_dim(mod_all, me, axis=1, keepdims=False).reshape(6, d)

    sm = {n: w[n][0] for n in REPLICATED if n not in ("b_ada", "final_g")}
    sm["final_g"] = final_g
    sm["conv_w"] = conv_w_all
    sm["ffn_conv_w"] = ffn_conv_w_all
    sq, grad_x, gw, gs, dmod = _local_step(x[0], mod, positions[0], loss_target[0], wts, sm)

    small_names = [n for n in REPLICATED if n != "b_ada"] + ["conv_w", "ffn_conv_w"]
    small_shapes = [gs[n].shape for n in small_names] + [(6 * d,), (1,)]
    partial = _pack([gs[n] for n in small_names] + [dmod, sq.reshape(1)])
    partial_all, = _all_gather([partial], "gather_small")
    summed = _unpack(_sum_sources(partial_all, "sum_small"), small_shapes)
    g_small = dict(zip(small_names, summed[:len(small_names)]))
    g_small["b_ada"] = summed[len(small_names)]
    loss = 0.5 * summed[-1][0] / d
    n_before = sum(math.prod(t) for t in small_shapes[:len(small_names)])
    dmod_all = partial_all.reshape(N_DEV, -1)[:, n_before:n_before + 6 * d]
    dmod_cols = lax.dynamic_slice(dmod_all, (0, me * ada_cols), (N_DEV, ada_cols))

    grads, deltas, new_m, new_v = {}, {}, {}, {}

    def update(n, parts):
        shp = w[n].shape
        g, dl, nm, nv = _adamw(parts, two_d(w[n]), two_d(m[n]), two_d(v[n]), "adamw_" + n)
        grads[n], deltas[n], new_m[n], new_v[n] = g.reshape(shp), dl.reshape(shp), nm.reshape(shp), nv.reshape(shp)

    update("w_ada", _ada_bwd(c_all, dmod_cols)[None])

    chunks = []
    for n in big:
        g = gw[n]
        if n in COL_SHARDED:
            r, cc = g.shape
            chunks.append(jnp.transpose(g.reshape(r, N_DEV, cc // N_DEV), (1, 0, 2)))
        else:
            chunks.append(g.reshape(N_DEV, g.shape[0] // N_DEV, g.shape[1]))
    for n, parts in zip(big, _all_to_all(chunks, "scatter_grads")):
        update(n, parts)

    for n in ("conv_w", "ffn_conv_w"):
        cols = w[n].shape[2]
        update(n, lax.dynamic_slice(g_small[n], (0, me * cols), (g_small[n].shape[0], cols))[None])

    rep_shapes = [w[n].shape for n in REPLICATED]
    g_rep, d_rep, m_rep, v_rep = _adamw(
        _pack([g_small[n] for n in REPLICATED])[None], _pack([w[n] for n in REPLICATED]), _pack([m[n] for n in REPLICATED]),
        _pack([v[n] for n in REPLICATED]), "adamw_replicated")
    for dst, packed in ((grads, g_rep), (deltas, d_rep), (new_m, m_rep), (new_v, v_rep)):
        dst.update(zip(REPLICATED, _unpack(packed, rep_shapes)))

    return (loss, grad_x[None], *[grads[n] for n in WEIGHTS], *[deltas[n] for n in WEIGHTS],
            *[new_m[n] for n in WEIGHTS], *[new_v[n] for n in WEIGHTS])
```

```python
import functools
import math

import jax
import jax.numpy as jnp
from jax import lax
from jax.experimental import pallas as pl
from jax.experimental.pallas import tpu as pltpu

F32 = jnp.float32
BF16 = jnp.bfloat16

N_DEV = 8
LANES = 128
SUBLANES = 8
VMEM_LIMIT = 56 * 1024 * 1024

D_RNN = 1280
Q_LORA = 384
KV_LORA = 256
QK_NOPE = 64
QK_ROPE = 32
V_HEAD = 64
N_HEADS = 16
D_FF = 2816
ROPE_THETA = 10000.0
LRU_C = 8.0
EPS = 1e-6
MLA_W = 768
ATT_SCALE = 1.0 / math.sqrt(QK_NOPE + QK_ROPE)

ADAM_LR, ADAM_B1, ADAM_B2, ADAM_EPS, ADAM_WD, ADAM_STEP = 0.001, 0.9, 0.999, 1e-08, 0.01, 10


def _cparams(sem=None):
    return pltpu.CompilerParams(dimension_semantics=sem, vmem_limit_bytes=VMEM_LIMIT)


def _pick(n, prefs):
    for p in prefs:
        if n % p == 0:
            return p
    return n


def _sigmoid(v):
    return 1.0 / (1.0 + jnp.exp(-v))


def _lane(shape):
    return lax.broadcasted_iota(jnp.int32, shape, len(shape) - 1)


def _row(shape):
    return lax.broadcasted_iota(jnp.int32, shape, len(shape) - 2)


MM_BLOCK_BYTES = 36 * 1024 * 1024


def _divisors(n):
    return [t for t in range(n, 0, -LANES) if n % t == 0] if n % LANES == 0 else [n]


def _mm_tiles(m, n, k, a_bytes, b_bytes, o_bytes):
    tm = _pick(m, (512, 384, 256, 128))
    for tk in _divisors(k):
        for tn in _divisors(n):
            need = 2 * (tm * tk * a_bytes + tk * tn * b_bytes + tm * tn * o_bytes) + (tm * tn * 4 if tk < k else 0)
            if tn <= 2048 and need <= MM_BLOCK_BYTES:
                return tm, tn, tk
    raise ValueError((m, n, k))


def _mm(a, b, *, ta=False, tb=False, out_dtype=F32, name):
    (k_a, m) = a.shape if ta else a.shape[::-1]
    (n, k_b) = b.shape if tb else b.shape[::-1]
    assert k_a == k_b, (a.shape, b.shape, ta, tb)
    k = k_a
    tm, tn, tk = _mm_tiles(m, n, k, a.dtype.itemsize, b.dtype.itemsize, jnp.dtype(out_dtype).itemsize)
    nk = k // tk
    dims = (((0 if ta else 1,), (1 if tb else 0,)), ((), ()))

    def body(a_ref, b_ref, o_ref, *acc):
        part = lax.dot_general(a_ref[...].astype(BF16), b_ref[...].astype(BF16), dims, preferred_element_type=F32)
        if nk == 1:
            o_ref[...] = part.astype(out_dtype)
            return
        acc_ref, = acc
        kk = pl.program_id(2)

        @pl.when(kk == 0)
        def _():
            acc_ref[...] = part

        @pl.when(kk > 0)
        def _():
            acc_ref[...] += part

        @pl.when(kk == nk - 1)
        def _():
            o_ref[...] = acc_ref[...].astype(out_dtype)

    a_spec = pl.BlockSpec((tk, tm), lambda i, j, kk: (kk, i)) if ta else pl.BlockSpec((tm, tk), lambda i, j, kk: (i, kk))
    b_spec = pl.BlockSpec((tn, tk), lambda i, j, kk: (j, kk)) if tb else pl.BlockSpec((tk, tn), lambda i, j, kk: (kk, j))
    return pl.pallas_call(
        body, name=name,
        grid=(m // tm, n // tn, nk),
        in_specs=[a_spec, b_spec],
        out_specs=pl.BlockSpec((tm, tn), lambda i, j, kk: (i, j)),
        out_shape=jax.ShapeDtypeStruct((m, n), out_dtype),
        scratch_shapes=[] if nk == 1 else [pltpu.VMEM((tm, tn), F32)],
        compiler_params=_cparams(("arbitrary", "arbitrary", "arbitrary")),
    )(a, b)


def _rowwise(fn, row_ins, par_ins, out_defs, red_defs, *, name, tr=256):
    s = row_ins[0].shape[0]
    tr = min(tr, s)
    nr, npar, no = len(row_ins), len(par_ins), len(out_defs)

    def body(*refs):
        rin, pin = refs[:nr], refs[nr:nr + npar]
        outs, reds = refs[nr + npar:nr + npar + no], refs[nr + npar + no:]
        i = pl.program_id(0)

        @pl.when(i == 0)
        def _():
            for r in reds:
                r[...] = jnp.zeros_like(r)

        fn(i, rin, pin, outs, reds)

    in_specs = [pl.BlockSpec((tr, a.shape[1]), lambda i: (i, 0)) for a in row_ins]
    in_specs += [pl.BlockSpec(a.shape, lambda i, nd=a.ndim: (0,) * nd) for a in par_ins]
    out_specs = [pl.BlockSpec((tr, c), lambda i: (i, 0)) for c, _ in out_defs]
    out_specs += [pl.BlockSpec(shp, lambda i: (0, 0)) for shp in red_defs]
    out_shape = [jax.ShapeDtypeStruct((s, c), dt) for c, dt in out_defs]
    out_shape += [jax.ShapeDtypeStruct(shp, F32) for shp in red_defs]
    return pl.pallas_call(
        body, name=name, grid=(s // tr,), in_specs=in_specs, out_specs=out_specs, out_shape=out_shape,
        compiler_params=_cparams(("arbitrary",)),
    )(*row_ins, *par_ins)


def _rms(v):
    return lax.rsqrt(jnp.mean(v * v, axis=-1, keepdims=True) + EPS)


def _colsum(v):
    return jnp.sum(v, axis=0, keepdims=True)


def _rms_bwd(dn, n, rstd):
    return rstd * (dn - n * jnp.mean(dn * n, axis=-1, keepdims=True))


def _norm_mod_fwd(x, gmod, name):
    def fn(i, rin, pin, outs, reds):
        xv = rin[0][...]
        p = pin[0][...]
        n = xv * _rms(xv)
        outs[0][...] = ((n * p[0:1]) * (1.0 + p[1:2]) + p[2:3]).astype(BF16)

    return _rowwise(fn, [x], [gmod], [(x.shape[1], BF16)], [], name=name)[0]


def _rope(v, rot_c, rot_s):
    half = QK_ROPE // 2
    swapped = jnp.where(_lane(v.shape) < QK_NOPE + half, pltpu.roll(v, LANES - half, 1), pltpu.roll(v, half, 1))
    return v * rot_c + swapped * rot_s


def _rope_t(dv, rot_c, rot_s):
    half = QK_ROPE // 2
    ds = dv * rot_s
    lane = _lane(dv.shape)
    swapped = jnp.where(lane < QK_NOPE + half, pltpu.roll(ds, LANES - half, 1), pltpu.roll(ds, half, 1))
    in_rope = (lane >= QK_NOPE) & (lane < QK_NOPE + QK_ROPE)
    return dv * rot_c + jnp.where(in_rope, swapped, 0.0)


def _mla_prep_fwd(proj_mla, rot_c, rot_s, ng):
    o1, o2 = Q_LORA, Q_LORA + KV_LORA

    def fn(i, rin, pin, outs, reds):
        g = pin[0][...]
        ql = rin[0][:, 0:o1]
        kl = rin[0][:, o1:o2]
        outs[0][...] = (ql * _rms(ql) * g[0:1, 0:o1]).astype(BF16)
        outs[1][...] = (kl * _rms(kl) * g[0:1, o1:o2]).astype(BF16)
        kr = pltpu.roll(rin[0][:, o2:o2 + LANES], QK_NOPE, 1)
        outs[2][...] = _rope(kr, rin[1][...], rin[2][...]).astype(BF16)

    return _rowwise(fn, [proj_mla, rot_c, rot_s], [ng], [(Q_LORA, BF16), (KV_LORA, BF16), (LANES, BF16)], [],
                    name="mla_prep_fwd")


def _mla_prep_bwd(proj_mla, dqn, dkvn, dkr, rot_c, rot_s, ng):
    o1, o2 = Q_LORA, Q_LORA + KV_LORA

    def fn(i, rin, pin, outs, reds):
        g = pin[0][...]
        ql = rin[0][:, 0:o1]
        kl = rin[0][:, o1:o2]
        rq, rk = _rms(ql), _rms(kl)
        nq, nk = ql * rq, kl * rk
        dq, dk = rin[1][...], rin[2][...]
        outs[0][:, 0:o1] = _rms_bwd(dq * g[0:1, 0:o1], nq, rq).astype(BF16)
        outs[0][:, o1:o2] = _rms_bwd(dk * g[0:1, o1:o2], nk, rk).astype(BF16)
        dkr_pre = pltpu.roll(_rope_t(rin[3][...], rin[4][...], rin[5][...]), LANES - QK_NOPE, 1)
        outs[0][:, o2:] = jnp.where(_lane(dkr_pre.shape) < QK_ROPE, dkr_pre, 0.0).astype(BF16)
        reds[0][0:1, 0:o1] += _colsum(dq * nq)
        reds[0][0:1, o1:o2] += _colsum(dk * nk)

    return _rowwise(fn, [proj_mla, dqn, dkvn, dkr, rot_c, rot_s], [ng], [(MLA_W, BF16)], [(SUBLANES, MLA_W)],
                    name="mla_prep_bwd")


def _rope_heads(q, rot_c, rot_s, transpose, name):
    def fn(i, rin, pin, outs, reds):
        c, sn = rin[1][...] * Q_PRESCALE, rin[2][...] * Q_PRESCALE
        for h in range(N_HEADS):
            sl = slice(h * LANES, (h + 1) * LANES)
            v = rin[0][:, sl]
            outs[0][:, sl] = (_rope_t(v, c, sn) if transpose else _rope(v, c, sn)).astype(BF16)

    return _rowwise(fn, [q, rot_c, rot_s], [], [(q.shape[1], BF16)], [], name=name)[0]


def _merge_fwd(pr, pm, proj_g):
    d = pr.shape[1]

    def fn(i, rin, pin, outs, reds):
        outs[0][...] = (_sigmoid(rin[2][:, 0:d]) * rin[0][...] + _sigmoid(rin[2][:, d:]) * rin[1][...]).astype(BF16)

    return _rowwise(fn, [pr, pm, proj_g], [], [(d, BF16)], [], name="merge_fwd")[0]


def _merge_bwd(dmerged, pr, pm, proj_g):
    d = pr.shape[1]

    def fn(i, rin, pin, outs, reds):
        dm = rin[0][...]
        sr, sm = _sigmoid(rin[3][:, 0:d]), _sigmoid(rin[3][:, d:])
        outs[0][...] = (dm * sr).astype(BF16)
        outs[1][...] = (dm * sm).astype(BF16)
        outs[2][:, 0:d] = (dm * rin[1][...] * sr * (1.0 - sr)).astype(BF16)
        outs[2][:, d:] = (dm * rin[2][...] * sm * (1.0 - sm)).astype(BF16)

    return _rowwise(fn, [dmerged, pr, pm, proj_g], [], [(d, BF16), (d, BF16), (2 * d, BF16)], [], name="merge_bwd")


def _resid_norm_fwd(x, o, gmod):
    d = x.shape[1]

    def fn(i, rin, pin, outs, reds):
        p = pin[0][...]
        x1 = rin[0][...] + p[3:4] * rin[1][...]
        outs[0][...] = x1
        outs[1][...] = ((x1 * _rms(x1) * p[0:1]) * (1.0 + p[1:2]) + p[2:3]).astype(BF16)

    return _rowwise(fn, [x, o], [gmod], [(d, F32), (d, BF16)], [], name="resid_norm_fwd")


def _final_fwd_bwd(x1, dn, target, par):
    d = x1.shape[1]

    def fn(i, rin, pin, outs, reds):
        p = pin[0][...]
        dnv = rin[1][...]
        x2 = rin[0][...] + p[0:1] * dnv
        rstd = _rms(x2)
        n3 = x2 * rstd
        err = n3 * p[1:2] - rin[2][...]
        dy = err * (1.0 / d)
        dx2 = _rms_bwd(dy * p[1:2], n3, rstd)
        outs[0][...] = dx2
        outs[1][...] = (dx2 * p[0:1]).astype(BF16)
        reds[0][0:1, :] += _colsum(dy * n3)
        reds[0][1:2, :] += _colsum(dx2 * dnv)
        reds[0][2:3, :] += jnp.zeros((1, d), F32) + jnp.sum(err * err)

    return _rowwise(fn, [x1, dn, target], [par], [(d, F32), (d, BF16)], [(SUBLANES, d)], name="final_fwd_bwd")


def _norm2_bwd(x1, dh2, dx2, o, gmod):
    d = x1.shape[1]

    def fn(i, rin, pin, outs, reds):
        p = pin[0][...]
        x1v, dh = rin[0][...], rin[1][...]
        rstd = _rms(x1v)
        n2 = x1v * rstd
        dx1 = rin[2][...] + _rms_bwd(dh * (p[0:1] * (1.0 + p[1:2])), n2, rstd)
        outs[0][...] = dx1
        outs[1][...] = (dx1 * p[3:4]).astype(BF16)
        reds[0][0:1, :] += _colsum(dh * n2 * (1.0 + p[1:2]))
        reds[0][1:2, :] += _colsum(dh * n2 * p[0:1])
        reds[0][2:3, :] += _colsum(dh)
        reds[0][3:4, :] += _colsum(dx1 * rin[3][...])

    return _rowwise(fn, [x1, dh2, dx2, o], [gmod], [(d, F32), (d, BF16)], [(SUBLANES, d)], name="norm2_bwd")


def _norm1_bwd(x, dh_a, dh_b, dh_c, dx1, gmod):
    d = x.shape[1]

    def fn(i, rin, pin, outs, reds):
        p = pin[0][...]
        xv = rin[0][...]
        dh = rin[1][...] + rin[2][...] + rin[3][...]
        rstd = _rms(xv)
        n1 = xv * rstd
        outs[0][...] = rin[4][...] + _rms_bwd(dh * (p[0:1] * (1.0 + p[1:2])), n1, rstd)
        reds[0][0:1, :] += _colsum(dh * n1 * (1.0 + p[1:2]))
        reds[0][1:2, :] += _colsum(dh * n1 * p[0:1])
        reds[0][2:3, :] += _colsum(dh)

    return _rowwise(fn, [x, dh_a, dh_b, dh_c, dx1], [gmod], [(d, F32)], [(SUBLANES, d)], name="norm1_bwd")


RNN_CHUNK = 512


def _shift_down(ref, base, n, j):
    v = ref[pl.ds(base, n + SUBLANES), :]
    return v[SUBLANES:] if j == 0 else pltpu.roll(v, j, 0)[SUBLANES:]


def _shift_up(ref, base, n, j, top_pad):
    v = ref[pl.ds(base + top_pad, n + SUBLANES), :]
    return v[:n] if j == 0 else pltpu.roll(v, n + SUBLANES - j, 0)[:n]


def _one_minus_exp(z):
    series = -z * (1.0 + z * (0.5 + z * (1.0 / 6.0 + z * (1.0 / 24.0 + z * (1.0 / 120.0 + z * (1.0 / 720.0))))))
    return jnp.where(z > -0.1, series, 1.0 - jnp.exp(z))


def _softplus(v):
    return jnp.maximum(v, 0.0) + jnp.log(1.0 + jnp.exp(-jnp.abs(v)))


def _rnn_gates(xc, w, wa, wx, sp):
    xb = xc.astype(BF16)
    ra = _sigmoid(jnp.dot(xb, wa, preferred_element_type=F32) + w[5:6])
    ix = _sigmoid(jnp.dot(xb, wx, preferred_element_type=F32) + w[6:7])
    la = (-LRU_C) * ra * sp
    a = jnp.exp(la)
    mult = jnp.sqrt(_one_minus_exp(2.0 * la))
    return ra, ix, a, mult


def _rnn_fwd(x_rnn, keep, rp, wa_bd, wx_bd):
    s, r = x_rnn.shape
    ts = min(RNN_CHUNK, s)

    def body(x_ref, keep_ref, rp_ref, wa_ref, wx_ref, xc_ref, ra_ref, ix_ref, hs_ref, xpad, a_s, b_s):
        xpad[0:SUBLANES, :] = jnp.zeros((SUBLANES, LANES), F32)
        xpad[SUBLANES:, :] = x_ref[...]
        w = rp_ref[...]
        sp = _softplus(-w[7:8])
        wa, wx = wa_ref[0], wx_ref[0]

        def chunk(c, carry):
            base = pl.multiple_of(c * ts, ts)
            xc = w[4:5] + w[3:4] * _shift_down(xpad, base, ts, 0)
            for j in range(1, 4):
                xc = xc + w[3 - j:4 - j] * _shift_down(xpad, base, ts, j)
            ra, ix, a, mult = _rnn_gates(xc, w, wa, wx, sp)
            kp = keep_ref[pl.ds(base, ts), :]
            xc_ref[pl.ds(base, ts), :] = xc
            ra_ref[pl.ds(base, ts), :] = ra
            ix_ref[pl.ds(base, ts), :] = ix
            a_s[pl.ds(base, ts), :] = a * kp
            b_s[pl.ds(base, ts), :] = jnp.where(kp > 0.0, mult, 1.0) * (ix * xc)
            return carry

        lax.fori_loop(0, s // ts, chunk, 0)

        row = _row((SUBLANES, LANES))

        def tile(i, h):
            r0 = pl.multiple_of(i * SUBLANES, SUBLANES)
            a = a_s[pl.ds(r0, SUBLANES), :]
            b = b_s[pl.ds(r0, SUBLANES), :]
            for sh in (1, 2, 4):
                a_sh = jnp.where(row >= sh, pltpu.roll(a, sh, 0), 1.0)
                b_sh = jnp.where(row >= sh, pltpu.roll(b, sh, 0), 0.0)
                b = a * b_sh + b
                a = a * a_sh
            hv = b + a * h
            hs_ref[pl.ds(r0, SUBLANES), :] = hv
            return hv[SUBLANES - 1:SUBLANES, :]

        lax.fori_loop(0, s // SUBLANES, tile, jnp.zeros((1, LANES), F32), unroll=4)

    col = pl.BlockSpec((s, LANES), lambda g: (0, g))
    return pl.pallas_call(
        body, name="rnn_fwd", grid=(r // LANES,),
        in_specs=[col, pl.BlockSpec((s, 1), lambda g: (0, 0)), pl.BlockSpec((SUBLANES, LANES), lambda g: (0, g)),
                  pl.BlockSpec((1, LANES, LANES), lambda g: (g, 0, 0)), pl.BlockSpec((1, LANES, LANES), lambda g: (g, 0, 0))],
        out_specs=[col] * 4,
        out_shape=[jax.ShapeDtypeStruct((s, r), F32)] * 4,
        scratch_shapes=[pltpu.VMEM((s + SUBLANES, LANES), F32), pltpu.VMEM((s, LANES), F32), pltpu.VMEM((s, LANES), F32)],
        compiler_params=_cparams(("arbitrary",)),
    )(x_rnn, keep, rp, wa_bd, wx_bd)


def _rnn_bwd(x_rnn, xc, ra, ix, hs, dy, keep, rp, wa_bd, wx_bd):
    s, r = x_rnn.shape
    ts = min(RNN_CHUNK, s)
    nt = s // SUBLANES

    def body(x_ref, xc_ref, ra_ref, ix_ref, hs_ref, dy_ref, keep_ref, rp_ref, wa_ref, wx_ref,
             dx_ref, dwa_ref, dwx_ref, red_ref, xpad, hpad, a_s, dh_s, dxc_s):
        zero8 = jnp.zeros((SUBLANES, LANES), F32)
        xpad[0:SUBLANES, :] = zero8
        xpad[SUBLANES:, :] = x_ref[...]
        hpad[0:SUBLANES, :] = zero8
        hpad[SUBLANES:, :] = hs_ref[...]
        a_s[s:, :] = zero8
        dxc_s[s:, :] = zero8
        w = rp_ref[...]
        sp = _softplus(-w[7:8])
        wa, wx = wa_ref[0], wx_ref[0]

        def decay(c, carry):
            base = pl.multiple_of(c * ts, ts)
            a = jnp.exp((-LRU_C) * ra_ref[pl.ds(base, ts), :] * sp)
            a_s[pl.ds(base, ts), :] = a * keep_ref[pl.ds(base, ts), :]
            return carry

        lax.fori_loop(0, s // ts, decay, 0)

        row = _row((SUBLANES, LANES))

        def tile(n, nxt):
            i = nt - 1 - n
            r0 = pl.multiple_of(i * SUBLANES, SUBLANES)
            a_here = a_s[pl.ds(r0, SUBLANES), :]
            a_next = a_s[pl.ds(r0 + SUBLANES, SUBLANES), :]
            a = jnp.where(row < SUBLANES - 1, pltpu.roll(a_here, SUBLANES - 1, 0), a_next[0:1, :])
            b = dy_ref[pl.ds(r0, SUBLANES), :]
            for sh in (1, 2, 4):
                a_sh = jnp.where(row < SUBLANES - sh, pltpu.roll(a, SUBLANES - sh, 0), 1.0)
                b_sh = jnp.where(row < SUBLANES - sh, pltpu.roll(b, SUBLANES - sh, 0), 0.0)
                b = a * b_sh + b
                a = a * a_sh
            dh = b + a * nxt
            dh_s[pl.ds(r0, SUBLANES), :] = dh
            return dh[0:1, :]

        lax.fori_loop(0, nt, tile, jnp.zeros((1, LANES), F32), unroll=4)

        def gates(c, carry):
            dwa, dwx, d_ba, d_bx, d_sp, d_cb = carry
            base = pl.multiple_of(c * ts, ts)
            xcv = xc_ref[pl.ds(base, ts), :]
            rav = ra_ref[pl.ds(base, ts), :]
            ixv = ix_ref[pl.ds(base, ts), :]
            kp = keep_ref[pl.ds(base, ts), :]
            dh = dh_s[pl.ds(base, ts), :]
            h_prev = _shift_down(hpad, base, ts, 1)
            la = (-LRU_C) * rav * sp
            a = jnp.exp(la)
            mult = jnp.sqrt(_one_minus_exp(2.0 * la))
            mult_eff = jnp.where(kp > 0.0, mult, 1.0)
            d_a = dh * h_prev * kp
            d_mult = dh * (ixv * xcv) * kp
            d_ix = dh * mult_eff * xcv
            d_xc = dh * mult_eff * ixv
            d_la = d_a * a - d_mult * (a * a) / mult
            d_pa = d_la * ((-LRU_C) * sp) * rav * (1.0 - rav)
            d_px = d_ix * ixv * (1.0 - ixv)
            xb = xcv.astype(BF16)
            pab, pxb = d_pa.astype(BF16), d_px.astype(BF16)
            tn = (((0,), (0,)), ((), ()))
            nt_ = (((1,), (1,)), ((), ()))
            dwa = dwa + lax.dot_general(xb, pab, tn, preferred_element_type=F32)
            dwx = dwx + lax.dot_general(xb, pxb, tn, preferred_element_type=F32)
            d_xc = d_xc + lax.dot_general(pab, wa, nt_, preferred_element_type=F32)
            d_xc = d_xc + lax.dot_general(pxb, wx, nt_, preferred_element_type=F32)
            dxc_s[pl.ds(base, ts), :] = d_xc
            return (dwa, dwx, d_ba + _colsum(d_pa), d_bx + _colsum(d_px),
                    d_sp + _colsum(d_la * ((-LRU_C) * rav)), d_cb + _colsum(d_xc))

        z1 = jnp.zeros((1, LANES), F32)
        zw = jnp.zeros((LANES, LANES), F32)
        dwa, dwx, d_ba, d_bx, d_sp, d_cb = lax.fori_loop(0, s // ts, gates, (zw, zw, z1, z1, z1, z1))
        dwa_ref[0] = dwa
        dwx_ref[0] = dwx

        def conv(c, carry):
            base = pl.multiple_of(c * ts, ts)
            d_here = dxc_s[pl.ds(base, ts), :]
            dx = w[3:4] * d_here
            for j in range(1, 4):
                dx = dx + w[3 - j:4 - j] * _shift_up(dxc_s, base, ts, j, 0)
            dx_ref[pl.ds(base, ts), :] = dx.astype(BF16)
            return tuple(carry[k] + _colsum(d_here * _shift_down(xpad, base, ts, 3 - k)) for k in range(4))

        d_w = lax.fori_loop(0, s // ts, conv, (z1, z1, z1, z1))
        d_lru = d_sp * (-_sigmoid(-w[7:8]))
        red_ref[...] = jnp.concatenate(list(d_w) + [d_cb, d_ba, d_bx, d_lru], axis=0)

    col = pl.BlockSpec((s, LANES), lambda g: (0, g))
    par = pl.BlockSpec((SUBLANES, LANES), lambda g: (0, g))
    wsp = pl.BlockSpec((1, LANES, LANES), lambda g: (g, 0, 0))
    return pl.pallas_call(
        body, name="rnn_bwd", grid=(r // LANES,),
        in_specs=[col] * 6 + [pl.BlockSpec((s, 1), lambda g: (0, 0)), par, wsp, wsp],
        out_specs=[col, wsp, wsp, par],
        out_shape=[jax.ShapeDtypeStruct((s, r), BF16), jax.ShapeDtypeStruct((r // LANES, LANES, LANES), F32),
                   jax.ShapeDtypeStruct((r // LANES, LANES, LANES), F32), jax.ShapeDtypeStruct((SUBLANES, r), F32)],
        scratch_shapes=[pltpu.VMEM((s + SUBLANES, LANES), F32), pltpu.VMEM((s + SUBLANES, LANES), F32),
                        pltpu.VMEM((s + SUBLANES, LANES), F32), pltpu.VMEM((s, LANES), F32),
                        pltpu.VMEM((s + SUBLANES, LANES), F32)],
        compiler_params=_cparams(("arbitrary",)),
    )(x_rnn, xc, ra, ix, hs, dy, keep, rp, wa_bd, wx_bd)


ATT_BLOCK = 512
NT_DIMS = (((1,), (1,)), ((), ()))
TN_DIMS = (((0,), (0,)), ((), ()))


LOG2E = 1.4426950408889634
LN2 = 0.6931471805599453
Q_PRESCALE = ATT_SCALE * LOG2E


def _att_scores(q, kvb, krb, diagonal):
    k_eff = jnp.where(_lane(kvb.shape) < QK_NOPE, kvb, krb)
    sc = lax.dot_general(q, k_eff, NT_DIMS, preferred_element_type=F32)
    if diagonal:
        sc = jnp.where(lax.broadcasted_iota(jnp.int32, sc.shape, 1) <= lax.broadcasted_iota(jnp.int32, sc.shape, 0), sc, -jnp.inf)
    return sc, k_eff


def _att_fwd(q, kv, kr):
    s = q.shape[0]
    t = min(ATT_BLOCK, s)
    nb = s // t

    def body(q_ref, kv_ref, kr_ref, y_ref, lse_ref, m_s, acc_s):
        i, j = pl.program_id(1), pl.program_id(2)

        @pl.when(j == 0)
        def _():
            m_s[...] = jnp.full(m_s.shape, -jnp.inf, F32)
            acc_s[...] = jnp.zeros(acc_s.shape, F32)

        def step(diagonal):
            krb = kr_ref[...]
            lane = _lane((t, LANES))
            for hh in range(2):
                sl = slice(hh * LANES, (hh + 1) * LANES)
                kvb = kv_ref[:, sl]
                sc, _ = _att_scores(q_ref[:, sl], kvb, krb, diagonal)
                m_prev = m_s[hh]
                m_new = jnp.maximum(m_prev, jnp.max(sc, axis=-1, keepdims=True))
                alpha = jnp.exp2(m_prev - m_new)
                p = jnp.exp2(sc - m_new[:, 0:1]).astype(BF16)
                ones_v = jnp.where(lane < QK_NOPE, jnp.ones_like(kvb), kvb)
                acc_s[hh] = alpha * acc_s[hh] + jnp.dot(p, ones_v, preferred_element_type=F32)
                m_s[hh] = m_new

        @pl.when(j < i)
        def _():
            step(False)

        @pl.when(j == i)
        def _():
            step(True)
            lane = _lane((t, LANES))
            a0, a1 = acc_s[0], acc_s[1]
            l0, l1 = a0[:, 0:1], a1[:, 0:1]
            y_ref[...] = jnp.where(lane < V_HEAD, pltpu.roll(a0 / l0, V_HEAD, 1), a1 / l1).astype(BF16)
            lse_ref[...] = jnp.where(lane < V_HEAD, m_s[0] + jnp.log(l0) * LOG2E, m_s[1] + jnp.log(l1) * LOG2E)

    return pl.pallas_call(
        body, name="att_fwd", grid=(N_HEADS // 2, nb, nb),
        in_specs=[pl.BlockSpec((t, 2 * LANES), lambda p, i, j: (i, p)),
                  pl.BlockSpec((t, 2 * LANES), lambda p, i, j: (jnp.minimum(j, i), p)),
                  pl.BlockSpec((t, LANES), lambda p, i, j: (jnp.minimum(j, i), 0))],
        out_specs=[pl.BlockSpec((t, LANES), lambda p, i, j: (i, p))] * 2,
        out_shape=[jax.ShapeDtypeStruct((s, N_HEADS * V_HEAD), BF16), jax.ShapeDtypeStruct((s, N_HEADS * V_HEAD), F32)],
        scratch_shapes=[pltpu.VMEM((2, t, LANES), F32)] * 2,
        compiler_params=_cparams(("arbitrary", "arbitrary", "arbitrary")),
    )(q, kv, kr)


def _att_bwd(q, kv, kr, y, lse, dy):
    s = q.shape[0]
    t = min(ATT_BLOCK, s)
    nb = s // t

    def body(q_ref, kv_ref, kr_ref, y_ref, lse_ref, dy_ref, dq_ref, dkv_ref, dkr_ref, dkv_s):
        p_, j, i = pl.program_id(0), pl.program_id(1), pl.program_id(2)

        @pl.when((p_ == 0) & (j == 0) & (i == 0))
        def _():
            dkr_ref[...] = jnp.zeros(dkr_ref.shape, F32)

        @pl.when((j == 0) & (i == 0))
        def _():
            dq_ref[...] = jnp.zeros(dq_ref.shape, F32)

        @pl.when(i == 0)
        def _():
            dkv_s[...] = jnp.zeros(dkv_s.shape, F32)

        def step(diagonal):
            lane = _lane((t, LANES))
            krb = kr_ref[...]
            dyv = dy_ref[...]
            yv = y_ref[...].astype(F32)
            lsev = lse_ref[...]
            rows = pl.ds(pl.multiple_of(i * t, t), t)
            dkr_acc = jnp.zeros((t, LANES), F32)
            for hh in range(2):
                sl = slice(hh * LANES, (hh + 1) * LANES)
                qb, kvb = q_ref[:, sl], kv_ref[:, sl]
                sc, k_eff = _att_scores(qb, kvb, krb, diagonal)
                if hh == 0:
                    do_pad = jnp.where(lane >= V_HEAD, pltpu.roll(dyv, V_HEAD, 1), 0.0)
                    o_pad = jnp.where(lane >= V_HEAD, pltpu.roll(yv, V_HEAD, 1), 0.0)
                    lse_h = lsev[:, 0:1]
                else:
                    do_pad = jnp.where(lane >= V_HEAD, dyv, 0.0)
                    o_pad = jnp.where(lane >= V_HEAD, yv, 0.0)
                    lse_h = lsev[:, LANES - 1:LANES]
                do_ln2 = do_pad * LN2
                delta = jnp.sum(do_ln2 * o_pad, axis=-1, keepdims=True)
                prob = jnp.exp2(sc - lse_h)
                dv = lax.dot_general(prob.astype(BF16), do_pad.astype(BF16), TN_DIMS, preferred_element_type=F32)
                dp = lax.dot_general(do_ln2.astype(BF16), kvb, NT_DIMS, preferred_element_type=F32)
                ds = (prob * (dp - delta)).astype(BF16)
                dq_ref[rows, sl] += jnp.dot(ds, k_eff, preferred_element_type=F32)
                dk_eff = lax.dot_general(ds, qb, TN_DIMS, preferred_element_type=F32)
                dkv_s[hh] += dv + jnp.where(lane < QK_NOPE, dk_eff, 0.0)
                dkr_acc = dkr_acc + jnp.where(lane >= QK_NOPE, dk_eff, 0.0)
            dkr_ref[pl.ds(pl.multiple_of(j * t, t), t), :] += dkr_acc

        @pl.when(i > j)
        def _():
            step(False)

        @pl.when(i == j)
        def _():
            step(True)

        @pl.when(i == nb - 1)
        def _():
            dkv_ref[:, 0:LANES] = dkv_s[0].astype(BF16)
            dkv_ref[:, LANES:] = dkv_s[1].astype(BF16)

    qi = lambda p, j, i: (jnp.maximum(i, j), p)
    return pl.pallas_call(
        body, name="att_bwd", grid=(N_HEADS // 2, nb, nb),
        in_specs=[pl.BlockSpec((t, 2 * LANES), qi),
                  pl.BlockSpec((t, 2 * LANES), lambda p, j, i: (j, p)),
                  pl.BlockSpec((t, LANES), lambda p, j, i: (j, 0)),
                  pl.BlockSpec((t, LANES), qi), pl.BlockSpec((t, LANES), qi), pl.BlockSpec((t, LANES), qi)],
        out_specs=[pl.BlockSpec((s, 2 * LANES), lambda p, j, i: (0, p)),
                   pl.BlockSpec((t, 2 * LANES), lambda p, j, i: (j, p)),
                   pl.BlockSpec((s, LANES), lambda p, j, i: (0, 0))],
        out_shape=[jax.ShapeDtypeStruct((s, N_HEADS * LANES), F32), jax.ShapeDtypeStruct((s, N_HEADS * LANES), BF16),
                   jax.ShapeDtypeStruct((s, LANES), F32)],
        scratch_shapes=[pltpu.VMEM((2, t, LANES), F32)],
        compiler_params=_cparams(("arbitrary", "arbitrary", "arbitrary")),
    )(q, kv, kr, y, lse, dy)


FFN_COLS = 256


def _ffn_conv(pad_ref, w, base, n):
    u = w[3:4] + w[2:3] * _shift_down(pad_ref, base, n, 0)
    for j in range(1, 3):
        u = u + w[2 - j:3 - j] * _shift_down(pad_ref, base, n, j)
    return u


def _ffn_act_fwd(up, fp):
    s, f2 = up.shape
    f = f2 // 2
    tc = FFN_COLS
    ts = min(RNN_CHUNK, s)
    nfb = f // tc

    def body(ug_ref, uv_ref, wg_ref, wv_ref, act_ref, gpad, vpad):
        zero8 = jnp.zeros((SUBLANES, tc), F32)
        gpad[0:SUBLANES, :] = zero8
        vpad[0:SUBLANES, :] = zero8
        gpad[SUBLANES:, :] = ug_ref[...]
        vpad[SUBLANES:, :] = uv_ref[...]
        wg, wv = wg_ref[...], wv_ref[...]

        def chunk(c, carry):
            base = pl.multiple_of(c * ts, ts)
            g = _ffn_conv(gpad, wg, base, ts)
            v = _ffn_conv(vpad, wv, base, ts)
            act_ref[pl.ds(base, ts), :] = (g * _sigmoid(g) * v).astype(BF16)
            return carry

        lax.fori_loop(0, s // ts, chunk, 0)

    return pl.pallas_call(
        body, name="ffn_act_fwd", grid=(nfb,),
        in_specs=[pl.BlockSpec((s, tc), lambda b: (0, b)), pl.BlockSpec((s, tc), lambda b: (0, b + nfb)),
                  pl.BlockSpec((SUBLANES, tc), lambda b: (0, b)), pl.BlockSpec((SUBLANES, tc), lambda b: (0, b + nfb))],
        out_specs=pl.BlockSpec((s, tc), lambda b: (0, b)),
        out_shape=jax.ShapeDtypeStruct((s, f), BF16),
        scratch_shapes=[pltpu.VMEM((s + SUBLANES, tc), F32)] * 2,
        compiler_params=_cparams(("arbitrary",)),
    )(up, up, fp, fp)


def _ffn_act_bwd(up, dact, fp):
    s, f2 = up.shape
    f = f2 // 2
    tc = FFN_COLS
    ts = min(RNN_CHUNK, s)
    nfb = f // tc

    def body(ug_ref, uv_ref, da_ref, wg_ref, wv_ref, dup_ref, red_ref, gpad, vpad, dgs, dvs):
        half = pl.program_id(1)
        wg, wv = wg_ref[...], wv_ref[...]

        @pl.when(half == 0)
        def _():
            zero8 = jnp.zeros((SUBLANES, tc), F32)
            gpad[0:SUBLANES, :] = zero8
            vpad[0:SUBLANES, :] = zero8
            gpad[SUBLANES:, :] = ug_ref[...]
            vpad[SUBLANES:, :] = uv_ref[...]
            dgs[s:, :] = zero8
            dvs[s:, :] = zero8

            def act(c, carry):
                base = pl.multiple_of(c * ts, ts)
                g = _ffn_conv(gpad, wg, base, ts)
                v = _ffn_conv(vpad, wv, base, ts)
                da = da_ref[pl.ds(base, ts), :]
                sg = _sigmoid(g)
                dgs[pl.ds(base, ts), :] = da * v * (sg * (1.0 + g * (1.0 - sg)))
                dvs[pl.ds(base, ts), :] = da * (g * sg)
                return carry

            lax.fori_loop(0, s // ts, act, 0)

        def conv_t(src, pad, w, out_ref, red_ref):
            def chunk(c, carry):
                base = pl.multiple_of(c * ts, ts)
                d_here = src[pl.ds(base, ts), :]
                dx = w[2:3] * d_here
                for j in range(1, 3):
                    dx = dx + w[2 - j:3 - j] * _shift_up(src, base, ts, j, 0)
                out_ref[pl.ds(base, ts), :] = dx.astype(BF16)
                taps = tuple(carry[k] + _colsum(d_here * _shift_down(pad, base, ts, 2 - k)) for k in range(3))
                return taps + (carry[3] + _colsum(d_here),)

            z1 = jnp.zeros((1, tc), F32)
            red = lax.fori_loop(0, s // ts, chunk, (z1, z1, z1, z1))
            red_ref[...] = jnp.concatenate(list(red) + [jnp.zeros((4, tc), F32)], axis=0)

        @pl.when(half == 0)
        def _():
            conv_t(dgs, gpad, wg, dup_ref, red_ref)

        @pl.when(half == 1)
        def _():
            conv_t(dvs, vpad, wv, dup_ref, red_ref)

    gcol = pl.BlockSpec((s, tc), lambda b, h: (0, b))
    vcol = pl.BlockSpec((s, tc), lambda b, h: (0, b + nfb))
    gpar = pl.BlockSpec((SUBLANES, tc), lambda b, h: (0, b))
    vpar = pl.BlockSpec((SUBLANES, tc), lambda b, h: (0, b + nfb))
    return pl.pallas_call(
        body, name="ffn_act_bwd", grid=(nfb, 2),
        in_specs=[gcol, vcol, gcol, gpar, vpar],
        out_specs=[pl.BlockSpec((s, tc), lambda b, h: (0, b + h * nfb)),
                   pl.BlockSpec((SUBLANES, tc), lambda b, h: (0, b + h * nfb))],
        out_shape=[jax.ShapeDtypeStruct((s, f2), BF16), jax.ShapeDtypeStruct((SUBLANES, f2), F32)],
        scratch_shapes=[pltpu.VMEM((s + SUBLANES, tc), F32)] * 4,
        compiler_params=_cparams(("arbitrary", "arbitrary")),
    )(up, up, dact, fp, fp)


def _rows8(rows, width):
    rows = [r.reshape(1, width).astype(F32) for r in rows]
    return jnp.concatenate(rows + [jnp.zeros((SUBLANES - len(rows), width), F32)], axis=0)


def _block_diag(w):
    n, b, _ = w.shape
    w = w.reshape(n // 2, 2, b, b)
    z = jnp.zeros((n // 2, b, b), w.dtype)
    top = jnp.concatenate([w[:, 0], z], axis=2)
    bot = jnp.concatenate([z, w[:, 1]], axis=2)
    return jnp.concatenate([top, bot], axis=1)


def _block_diag_t(bd):
    n, b2, _ = bd.shape
    b = b2 // 2
    return jnp.stack([bd[:, :b, :b], bd[:, b:, b:]], axis=1).reshape(2 * n, b, b)


def _local_step(x, mod, positions, target, wts, sm):
    s, d = x.shape
    o_rnn, o_mla = D_RNN, D_RNN + Q_LORA + KV_LORA + QK_ROPE
    w_in = wts["w_in"]
    w_in_rnn = w_in[:, :o_rnn]
    w_in_mla = jnp.concatenate([w_in[:, o_rnn:o_mla], jnp.zeros((d, MLA_W - (o_mla - o_rnn)), w_in.dtype)], axis=1)
    w_in_g = w_in[:, o_mla:]
    hd = QK_NOPE + QK_ROPE
    w_uq_p = jnp.pad(wts["w_uq"].reshape(Q_LORA, N_HEADS, hd), ((0, 0), (0, 0), (0, LANES - hd))).reshape(Q_LORA, N_HEADS * LANES)
    w_ukv = wts["w_ukv"]
    wa_bd = _block_diag(sm["w_gate_a"]).astype(BF16)
    wx_bd = _block_diag(sm["w_gate_x"]).astype(BF16)

    pos = positions.reshape(s)
    half = QK_ROPE // 2
    inv_freq = ROPE_THETA ** (-jnp.arange(half, dtype=F32) / half)
    ang = pos.astype(F32)[:, None] * inv_freq
    cos, sin = jnp.cos(ang), jnp.sin(ang)
    rot_c = jnp.concatenate([jnp.ones((s, QK_NOPE), F32), cos, cos, jnp.ones((s, LANES - hd), F32)], axis=1)
    rot_s = jnp.concatenate([jnp.zeros((s, QK_NOPE), F32), -sin, sin, jnp.zeros((s, LANES - hd), F32)], axis=1)
    keep = (pos != 0).astype(F32).reshape(s, 1)

    gmod1 = _rows8([sm["norm1_g"], mod[1], mod[0]], d)
    gmod2 = _rows8([sm["norm2_g"], mod[4], mod[3], mod[2]], d)
    rp = jnp.concatenate([sm["conv_w"].reshape(4, D_RNN), _rows8([sm["conv_b"], sm["b_gate_a"], sm["b_gate_x"], sm["lru_param"]], D_RNN)[:4]], axis=0)
    fp = _rows8([sm["ffn_conv_w"][0], sm["ffn_conv_w"][1], sm["ffn_conv_w"][2], sm["ffn_conv_b"]], 2 * D_FF)
    ng = _rows8([jnp.concatenate([sm["q_norm_g"].reshape(-1), sm["kv_norm_g"].reshape(-1), jnp.zeros((MLA_W - Q_LORA - KV_LORA,), F32)])], MLA_W)
    fpar = _rows8([mod[5], sm["final_g"]], d)

    h = _norm_mod_fwd(x, gmod1, "norm1_fwd")
    proj_rnn = _mm(h, w_in_rnn, name="mm_in_rnn")
    proj_mla = _mm(h, w_in_mla, name="mm_in_mla")
    proj_g = _mm(h, w_in_g, name="mm_in_g")
    xc, ra, ix, hs = _rnn_fwd(proj_rnn, keep, rp, wa_bd, wx_bd)
    qn, kvn, kr = _mla_prep_fwd(proj_mla, rot_c, rot_s, ng)
    q_rot = _rope_heads(_mm(qn, w_uq_p, name="mm_uq"), rot_c, rot_s, False, "rope_fwd")
    kv = _mm(kvn, w_ukv, out_dtype=BF16, name="mm_ukv")
    y_mla, lse = _att_fwd(q_rot, kv, kr)
    pr = _mm(hs, wts["w_proj_rnn"], name="mm_proj_rnn")
    pm = _mm(y_mla, wts["w_proj_mla"], name="mm_proj_mla")
    merged = _merge_fwd(pr, pm, proj_g)
    o = _mm(merged, wts["w_out"], name="mm_out")
    x1, h2 = _resid_norm_fwd(x, o, gmod2)
    up = _mm(h2, wts["w_up"], name="mm_up")
    act = _ffn_act_fwd(up, fp)
    dn = _mm(act, wts["w_down"], name="mm_down")

    dx2, ddn, red_f = _final_fwd_bwd(x1, dn, target, fpar)
    gw = {}
    dact = _mm(ddn, wts["w_down"], tb=True, name="mm_d_act")
    gw["w_down"] = _mm(act, ddn, ta=True, out_dtype=BF16, name="mm_dw_down")
    dup, red_ffn = _ffn_act_bwd(up, dact, fp)
    dh2 = _mm(dup, wts["w_up"], tb=True, name="mm_d_h2")
    gw["w_up"] = _mm(h2, dup, ta=True, out_dtype=BF16, name="mm_dw_up")
    dx1, do, red_2 = _norm2_bwd(x1, dh2, dx2, o, gmod2)
    dmerged = _mm(do, wts["w_out"], tb=True, name="mm_d_merged")
    gw["w_out"] = _mm(merged, do, ta=True, out_dtype=BF16, name="mm_dw_out")
    dpr, dpm, dg = _merge_bwd(dmerged, pr, pm, proj_g)
    dy_rnn = _mm(dpr, wts["w_proj_rnn"], tb=True, name="mm_d_yrnn")
    gw["w_proj_rnn"] = _mm(hs, dpr, ta=True, out_dtype=BF16, name="mm_dw_proj_rnn")
    dy_mla = _mm(dpm, wts["w_proj_mla"], tb=True, name="mm_d_ymla")
    gw["w_proj_mla"] = _mm(y_mla, dpm, ta=True, out_dtype=BF16, name="mm_dw_proj_mla")
    dq_rot, dkv, dkr = _att_bwd(q_rot, kv, kr, y_mla, lse, dy_mla)
    dq = _rope_heads(dq_rot, rot_c, rot_s, True, "rope_bwd")
    dqn = _mm(dq, w_uq_p, tb=True, name="mm_d_qn")
    dw_uq_p = _mm(qn, dq, ta=True, out_dtype=BF16, name="mm_dw_uq")
    gw["w_uq"] = dw_uq_p.reshape(Q_LORA, N_HEADS, LANES)[:, :, :hd].reshape(Q_LORA, N_HEADS * hd)
    dkvn = _mm(dkv, w_ukv, tb=True, name="mm_d_kvn")
    gw["w_ukv"] = _mm(kvn, dkv, ta=True, out_dtype=BF16, name="mm_dw_ukv")
    dproj_mla, red_m = _mla_prep_bwd(proj_mla, dqn, dkvn, dkr, rot_c, rot_s, ng)
    dx_rnn, dwa_bd, dwx_bd, red_r = _rnn_bwd(proj_rnn, xc, ra, ix, hs, dy_rnn, keep, rp, wa_bd, wx_bd)
    dh_a = _mm(dx_rnn, w_in_rnn, tb=True, name="mm_d_h_rnn")
    dh_b = _mm(dproj_mla, w_in_mla, tb=True, name="mm_d_h_mla")
    dh_c = _mm(dg, w_in_g, tb=True, name="mm_d_h_g")
    gw["w_in"] = jnp.concatenate([
        _mm(h, dx_rnn, ta=True, out_dtype=BF16, name="mm_dw_in_rnn"),
        _mm(h, dproj_mla, ta=True, out_dtype=BF16, name="mm_dw_in_mla")[:, :o_mla - o_rnn],
        _mm(h, dg, ta=True, out_dtype=BF16, name="mm_dw_in_g")], axis=1)
    grad_x, red_1 = _norm1_bwd(x, dh_a, dh_b, dh_c, dx1, gmod1)

    gs = {
        "norm1_g": red_1[0], "conv_w": red_r[0:4], "conv_b": red_r[4], "w_gate_a": _block_diag_t(dwa_bd),
        "b_gate_a": red_r[5], "w_gate_x": _block_diag_t(dwx_bd), "b_gate_x": red_r[6], "lru_param": red_r[7],
        "q_norm_g": red_m[0, :Q_LORA], "kv_norm_g": red_m[0, Q_LORA:Q_LORA + KV_LORA], "norm2_g": red_2[0],
        "ffn_conv_w": red_ffn[0:3], "ffn_conv_b": red_ffn[3], "final_g": red_f[0],
    }
    dmod = jnp.stack([red_1[2], red_1[1], red_2[3], red_2[2], red_2[1], red_f[1]], axis=0)
    return red_f[2, 0], grad_x, gw, gs, dmod


MESH_IDS = pl.DeviceIdType.MESH
HBM_SPEC = pl.BlockSpec(memory_space=pltpu.HBM)


def _my_slot():
    return 4 * lax.axis_index("x") + 2 * lax.axis_index("y") + lax.axis_index("c")


def _all_gather(arrs, name):
    n = len(arrs)

    def body(*refs):
        ins, outs = refs[:n], refs[n:2 * n]
        send_sems, recv_sems, local_sems = refs[2 * n:]
        x, y, c = lax.axis_index("x"), lax.axis_index("y"), lax.axis_index("c")
        me, sibling = (x, y, c), (x, y, 1 - c)
        chips = [(1 - x, y), (x, 1 - y), (1 - x, 1 - y)]

        def slot(dev):
            return 4 * dev[0] + 2 * dev[1] + dev[2]

        def copy(a, k, block, to, src=None):
            dst = outs[a].at[slot(block)]
            return pltpu.make_async_remote_copy(
                src_ref=dst if src is None else src, dst_ref=dst, send_sem=send_sems.at[a, k], recv_sem=recv_sems.at[a, k],
                device_id=to, device_id_type=MESH_IDS)

        mine = [pltpu.make_async_copy(ins[a], outs[a].at[slot(me)], local_sems.at[a]) for a in range(n)]
        for cp in mine:
            cp.start()
        first = []
        for a in range(n):
            first.append(copy(a, 0, me, sibling, src=ins[a]))
            first += [copy(a, 1 + j, me, (*chip, c), src=ins[a]) for j, chip in enumerate(chips)]
        for cp in first:
            cp.start()
        passed = []
        for j, chip in enumerate(chips):
            for a in range(n):
                copy(a, 1 + j, (*chip, c), me).wait_recv()
                fwd = copy(a, 4 + j, (*chip, c), sibling)
                fwd.start()
                passed.append(fwd)
        for a in range(n):
            copy(a, 0, sibling, me).wait_recv()
            for j, chip in enumerate(chips):
                copy(a, 4 + j, (*chip, 1 - c), me).wait_recv()
        for cp in first + passed:
            cp.wait_send()
        for cp in mine:
            cp.wait()

    return pl.pallas_call(
        body, name=name,
        in_specs=[HBM_SPEC] * n, out_specs=[HBM_SPEC] * n,
        out_shape=[jax.ShapeDtypeStruct((N_DEV,) + a.shape, a.dtype) for a in arrs],
        scratch_shapes=[pltpu.SemaphoreType.DMA((n, 7)), pltpu.SemaphoreType.DMA((n, 7)), pltpu.SemaphoreType.DMA((n,))],
    )(*arrs)


def _all_to_all(arrs, name):
    n = len(arrs)
    flips = [(dx, dy, dc) for dx in (0, 1) for dy in (0, 1) for dc in (0, 1)][1:]

    def body(*refs):
        ins, outs = refs[:n], refs[n:2 * n]
        send_sems, recv_sems, local_sems = refs[2 * n:]
        x, y, c = lax.axis_index("x"), lax.axis_index("y"), lax.axis_index("c")
        me = 4 * x + 2 * y + c
        mine = [pltpu.make_async_copy(ins[a].at[me], outs[a].at[me], local_sems.at[a]) for a in range(n)]
        for cp in mine:
            cp.start()
        sends = []
        for k, (dx, dy, dc) in enumerate(flips):
            peer = (x ^ dx, y ^ dy, c ^ dc)
            peer_slot = 4 * peer[0] + 2 * peer[1] + peer[2]
            for a in range(n):
                sends.append(pltpu.make_async_remote_copy(
                    src_ref=ins[a].at[peer_slot], dst_ref=outs[a].at[me], send_sem=send_sems.at[a, k],
                    recv_sem=recv_sems.at[a, k], device_id=peer, device_id_type=MESH_IDS))
        for cp in sends:
            cp.start()
        for k, (dx, dy, dc) in enumerate(flips):
            peer_slot = 4 * (x ^ dx) + 2 * (y ^ dy) + (c ^ dc)
            for a in range(n):
                pltpu.make_async_remote_copy(
                    src_ref=ins[a].at[peer_slot], dst_ref=outs[a].at[peer_slot], send_sem=send_sems.at[a, k],
                    recv_sem=recv_sems.at[a, k], device_id=(x ^ dx, y ^ dy, c ^ dc), device_id_type=MESH_IDS).wait_recv()
        for cp in sends:
            cp.wait_send()
        for cp in mine:
            cp.wait()

    return pl.pallas_call(
        body, name=name,
        in_specs=[HBM_SPEC] * n, out_specs=[HBM_SPEC] * n,
        out_shape=[jax.ShapeDtypeStruct(a.shape, a.dtype) for a in arrs],
        scratch_shapes=[pltpu.SemaphoreType.DMA((n, 7)), pltpu.SemaphoreType.DMA((n, 7)), pltpu.SemaphoreType.DMA((n,))],
    )(*arrs)


def _sum_sources(parts, name):
    k, r, c = parts.shape
    tr = r if k * r * c <= 2 * 1024 * 1024 else _pick(r, (512, 256, 128, 64, 32, 16, 8))

    def body(p_ref, o_ref):
        acc = p_ref[0].astype(F32)
        for s in range(1, k):
            acc = acc + p_ref[s].astype(F32)
        o_ref[...] = acc

    return pl.pallas_call(
        body, name=name, grid=(r // tr,),
        in_specs=[pl.BlockSpec((k, tr, c), lambda i: (0, i, 0))],
        out_specs=pl.BlockSpec((tr, c), lambda i: (i, 0)),
        out_shape=jax.ShapeDtypeStruct((r, c), F32),
        compiler_params=_cparams(("arbitrary",)),
    )(parts)


def _adamw(parts, w, m, v, name):
    k, r, c = parts.shape
    tr = r if r * c <= 256 * 1024 else _pick(r, (256, 128, 64, 32, 16, 8))

    def body(p_ref, w_ref, m_ref, v_ref, g_ref, d_ref, nm_ref, nv_ref):
        g = p_ref[0].astype(F32)
        for s in range(1, k):
            g = g + p_ref[s].astype(F32)
        m_new = ADAM_B1 * m_ref[...] + (1.0 - ADAM_B1) * g
        v_new = ADAM_B2 * v_ref[...] + (1.0 - ADAM_B2) * jnp.square(g)
        m_hat = m_new / (1.0 - ADAM_B1 ** ADAM_STEP)
        v_hat = v_new / (1.0 - ADAM_B2 ** ADAM_STEP)
        g_ref[...] = g
        d_ref[...] = -ADAM_LR * (m_hat / (jnp.sqrt(v_hat) + ADAM_EPS) + ADAM_WD * w_ref[...])
        nm_ref[...] = m_new
        nv_ref[...] = v_new

    blk = pl.BlockSpec((tr, c), lambda i: (i, 0))
    return pl.pallas_call(
        body, name=name, grid=(r // tr,),
        in_specs=[pl.BlockSpec((k, tr, c), lambda i: (0, i, 0)), blk, blk, blk],
        out_specs=[blk] * 4,
        out_shape=[jax.ShapeDtypeStruct((r, c), F32)] * 4,
        compiler_params=_cparams(("arbitrary",)),
    )(parts, w, m, v)


def _silu(v):
    return v * _sigmoid(v)


def _ada_fwd(c_all, w, b):
    def body(c_ref, w_ref, b_ref, o_ref):
        ca = _silu(c_ref[...]).astype(BF16)
        o_ref[...] = jnp.dot(ca, w_ref[...].astype(BF16), preferred_element_type=F32) + b_ref[...]

    return pl.pallas_call(
        body, name="ada_fwd", out_shape=jax.ShapeDtypeStruct((c_all.shape[0], w.shape[1]), F32),
        compiler_params=_cparams(),
    )(c_all, w, b)


def _ada_bwd(c_all, dmod):
    def body(c_ref, d_ref, o_ref):
        ca = _silu(c_ref[...]).astype(BF16).astype(F32)
        dm = d_ref[...].astype(BF16).astype(F32)
        acc = jnp.zeros(o_ref.shape, F32)
        for bi in range(c_all.shape[0]):
            acc = acc + jnp.transpose(ca[bi:bi + 1, :]) * dm[bi:bi + 1, :]
        o_ref[...] = acc

    return pl.pallas_call(
        body, name="ada_bwd", out_shape=jax.ShapeDtypeStruct((c_all.shape[1], dmod.shape[1]), F32),
        compiler_params=_cparams(),
    )(c_all, dmod)


COL_SHARDED = ("w_in", "w_uq", "w_ukv", "w_up")
ROW_SHARDED = ("w_proj_rnn", "w_proj_mla", "w_out", "w_down")
REPLICATED = ("b_ada", "norm1_g", "conv_b", "w_gate_a", "b_gate_a", "w_gate_x", "b_gate_x", "lru_param", "q_norm_g",
              "kv_norm_g", "norm2_g", "ffn_conv_b", "final_g")
WEIGHTS = ("w_ada", "b_ada", "norm1_g", "w_in", "conv_w", "conv_b", "w_gate_a", "b_gate_a", "w_gate_x", "b_gate_x",
           "lru_param", "q_norm_g", "w_uq", "kv_norm_g", "w_ukv", "w_proj_rnn", "w_proj_mla", "w_out", "norm2_g", "w_up",
           "ffn_conv_w", "ffn_conv_b", "w_down", "final_g")
PACK_LANES = 128


def _pack(vecs):
    flat = jnp.concatenate([v.reshape(-1).astype(F32) for v in vecs])
    pad = (-flat.shape[0]) % (PACK_LANES * SUBLANES)
    return jnp.concatenate([flat, jnp.zeros((pad,), F32)]).reshape(-1, PACK_LANES)


def _unpack(packed, shapes):
    flat = packed.reshape(-1)
    out, off = [], 0
    for shp in shapes:
        size = math.prod(shp)
        out.append(flat[off:off + size].reshape(shp))
        off += size
    return out


def kernel(x, c, positions, w_ada, b_ada, norm1_g, w_in, conv_w, conv_b, w_gate_a, b_gate_a, w_gate_x, b_gate_x, lru_param, q_norm_g, w_uq, kv_norm_g, w_ukv, w_proj_rnn, w_proj_mla, w_out, norm2_g, w_up, ffn_conv_w, ffn_conv_b, w_down, final_g, loss_target, m_w_ada, m_b_ada, m_norm1_g, m_w_in, m_conv_w, m_conv_b, m_w_gate_a, m_b_gate_a, m_w_gate_x, m_b_gate_x, m_lru_param, m_q_norm_g, m_w_uq, m_kv_norm_g, m_w_ukv, m_w_proj_rnn, m_w_proj_mla, m_w_out, m_norm2_g, m_w_up, m_ffn_conv_w, m_ffn_conv_b, m_w_down, m_final_g, v_w_ada, v_b_ada, v_norm1_g, v_w_in, v_conv_w, v_conv_b, v_w_gate_a, v_b_gate_a, v_w_gate_x, v_b_gate_x, v_lru_param, v_q_norm_g, v_w_uq, v_kv_norm_g, v_w_ukv, v_w_proj_rnn, v_w_proj_mla, v_w_out, v_norm2_g, v_w_up, v_ffn_conv_w, v_ffn_conv_b, v_w_down, v_final_g):
    args = dict(locals())
    w = {n: args[n] for n in WEIGHTS}
    m = {n: args["m_" + n] for n in WEIGHTS}
    v = {n: args["v_" + n] for n in WEIGHTS}
    s, d = x.shape[1], x.shape[2]
    me = _my_slot()
    def two_d(a):
        assert a.ndim == 3 and a.shape[0] == 1, a.shape
        return a[0]

    big = COL_SHARDED + ROW_SHARDED
    gathered = _all_gather([two_d(w[n]).astype(BF16) for n in big] + [c, two_d(conv_w), two_d(ffn_conv_w)], "gather_weights")
    wts = {}
    for n, g in zip(big, gathered):
        k, r, cc = g.shape
        wts[n] = jnp.transpose(g, (1, 0, 2)).reshape(r, k * cc) if n in COL_SHARDED else g.reshape(k * r, cc)
    c_all = gathered[len(big)].reshape(N_DEV, d)
    conv_w_all = jnp.transpose(gathered[len(big) + 1], (1, 0, 2)).reshape(conv_w.shape[1], -1)
    ffn_conv_w_all = jnp.transpose(gathered[len(big) + 2], (1, 0, 2)).reshape(ffn_conv_w.shape[1], -1)

    ada_cols = w_ada.shape[2]
    b_cols = lax.dynamic_slice(b_ada, (0, me * ada_cols), (1, ada_cols))
    mod_cols = _ada_fwd(c_all, w_ada[0], b_cols)
    mod_all, = _all_gather([mod_cols], "gather_mod")
    mod = lax.dynamic_index_in_dim(mod_all, me, axis=1, keepdims=False).reshape(6, d)

    sm = {n: w[n][0] for n in REPLICATED if n not in ("b_ada", "final_g")}
    sm["final_g"] = final_g
    sm["conv_w"] = conv_w_all
    sm["ffn_conv_w"] = ffn_conv_w_all
    sq, grad_x, gw, gs, dmod = _local_step(x[0], mod, positions[0], loss_target[0], wts, sm)

    small_names = [n for n in REPLICATED if n != "b_ada"] + ["conv_w", "ffn_conv_w"]
    small_shapes = [gs[n].shape for n in small_names] + [(6 * d,), (1,)]
    partial = _pack([gs[n] for n in small_names] + [dmod, sq.reshape(1)])
    partial_all, = _all_gather([partial], "gather_small")
    summed = _unpack(_sum_sources(partial_all, "sum_small"), small_shapes)
    g_small = dict(zip(small_names, summed[:len(small_names)]))
    g_small["b_ada"] = summed[len(small_names)]
    loss = 0.5 * summed[-1][0] / d
    n_before = sum(math.prod(t) for t in small_shapes[:len(small_names)])
    dmod_all = partial_all.reshape(N_DEV, -1)[:, n_before:n_before + 6 * d]
    dmod_cols = lax.dynamic_slice(dmod_all, (0, me * ada_cols), (N_DEV, ada_cols))

    grads, deltas, new_m, new_v = {}, {}, {}, {}

    def update(n, parts):
        shp = w[n].shape
        g, dl, nm, nv = _adamw(parts, two_d(w[n]), two_d(m[n]), two_d(v[n]), "adamw_" + n)
        grads[n], deltas[n], new_m[n], new_v[n] = g.reshape(shp), dl.reshape(shp), nm.reshape(shp), nv.reshape(shp)

    update("w_ada", _ada_bwd(c_all, dmod_cols)[None])

    chunks = []
    for n in big:
        g = gw[n]
        if n in COL_SHARDED:
            r, cc = g.shape
            chunks.append(jnp.transpose(g.reshape(r, N_DEV, cc // N_DEV), (1, 0, 2)))
        else:
            chunks.append(g.reshape(N_DEV, g.shape[0] // N_DEV, g.shape[1]))
    for n, parts in zip(big, _all_to_all(chunks, "scatter_grads")):
        update(n, parts)

    for n in ("conv_w", "ffn_conv_w"):
        cols = w[n].shape[2]
        update(n, lax.dynamic_slice(g_small[n], (0, me * cols), (g_small[n].shape[0], cols))[None])

    rep_shapes = [w[n].shape for n in REPLICATED]
    g_rep, d_rep, m_rep, v_rep = _adamw(
        _pack([g_small[n] for n in REPLICATED])[None], _pack([w[n] for n in REPLICATED]), _pack([m[n] for n in REPLICATED]),
        _pack([v[n] for n in REPLICATED]), "adamw_replicated")
    for dst, packed in ((grads, g_rep), (deltas, d_rep), (new_m, m_rep), (new_v, v_rep)):
        dst.update(zip(REPLICATED, _unpack(packed, rep_shapes)))

    return (loss, grad_x[None], *[grads[n] for n in WEIGHTS], *[deltas[n] for n in WEIGHTS],
            *[new_m[n] for n in WEIGHTS], *[new_v[n] for n in WEIGHTS])
```

```python
import functools
import math

import jax
import jax.numpy as jnp
from jax import lax
from jax.experimental import pallas as pl
from jax.experimental.pallas import tpu as pltpu

F32 = jnp.float32
BF16 = jnp.bfloat16

N_DEV = 8
LANES = 128
SUBLANES = 8
VMEM_LIMIT = 56 * 1024 * 1024

D_RNN = 1280
Q_LORA = 384
KV_LORA = 256
QK_NOPE = 64
QK_ROPE = 32
V_HEAD = 64
N_HEADS = 16
D_FF = 2816
ROPE_THETA = 10000.0
LRU_C = 8.0
EPS = 1e-6
MLA_W = 768
ATT_SCALE = 1.0 / math.sqrt(QK_NOPE + QK_ROPE)

ADAM_LR, ADAM_B1, ADAM_B2, ADAM_EPS, ADAM_WD, ADAM_STEP = 0.001, 0.9, 0.999, 1e-08, 0.01, 10


def _cparams(sem=None):
    return pltpu.CompilerParams(dimension_semantics=sem, vmem_limit_bytes=VMEM_LIMIT)


def _pick(n, prefs):
    for p in prefs:
        if n % p == 0:
            return p
    return n


def _sigmoid(v):
    return 1.0 / (1.0 + jnp.exp(-v))


def _lane(shape):
    return lax.broadcasted_iota(jnp.int32, shape, len(shape) - 1)


def _row(shape):
    return lax.broadcasted_iota(jnp.int32, shape, len(shape) - 2)


MM_BLOCK_BYTES = 36 * 1024 * 1024


def _divisors(n):
    return [t for t in range(n, 0, -LANES) if n % t == 0] if n % LANES == 0 else [n]


def _mm_tiles(m, n, k, a_bytes, b_bytes, o_bytes):
    tm = _pick(m, (512, 384, 256, 128))
    for tk in _divisors(k):
        for tn in _divisors(n):
            need = 2 * (tm * tk * a_bytes + tk * tn * b_bytes + tm * tn * o_bytes) + (tm * tn * 4 if tk < k else 0)
            if tn <= 2048 and need <= MM_BLOCK_BYTES:
                return tm, tn, tk
    raise ValueError((m, n, k))


def _mm(a, b, *, ta=False, tb=False, out_dtype=F32, name):
    (k_a, m) = a.shape if ta else a.shape[::-1]
    (n, k_b) = b.shape if tb else b.shape[::-1]
    assert k_a == k_b, (a.shape, b.shape, ta, tb)
    k = k_a
    tm, tn, tk = _mm_tiles(m, n, k, a.dtype.itemsize, b.dtype.itemsize, jnp.dtype(out_dtype).itemsize)
    nk = k // tk
    dims = (((0 if ta else 1,), (1 if tb else 0,)), ((), ()))

    def body(a_ref, b_ref, o_ref, *acc):
        part = lax.dot_general(a_ref[...].astype(BF16), b_ref[...].astype(BF16), dims, preferred_element_type=F32)
        if nk == 1:
            o_ref[...] = part.astype(out_dtype)
            return
        acc_ref, = acc
        kk = pl.program_id(2)

        @pl.when(kk == 0)
        def _():
            acc_ref[...] = part

        @pl.when(kk > 0)
        def _():
            acc_ref[...] += part

        @pl.when(kk == nk - 1)
        def _():
            o_ref[...] = acc_ref[...].astype(out_dtype)

    a_spec = pl.BlockSpec((tk, tm), lambda i, j, kk: (kk, i)) if ta else pl.BlockSpec((tm, tk), lambda i, j, kk: (i, kk))
    b_spec = pl.BlockSpec((tn, tk), lambda i, j, kk: (j, kk)) if tb else pl.BlockSpec((tk, tn), lambda i, j, kk: (kk, j))
    return pl.pallas_call(
        body, name=name,
        grid=(m // tm, n // tn, nk),
        in_specs=[a_spec, b_spec],
        out_specs=pl.BlockSpec((tm, tn), lambda i, j, kk: (i, j)),
        out_shape=jax.ShapeDtypeStruct((m, n), out_dtype),
        scratch_shapes=[] if nk == 1 else [pltpu.VMEM((tm, tn), F32)],
        compiler_params=_cparams(("arbitrary", "arbitrary", "arbitrary")),
    )(a, b)


def _rowwise(fn, row_ins, par_ins, out_defs, red_defs, *, name, tr=256):
    s = row_ins[0].shape[0]
    tr = min(tr, s)
    nr, npar, no = len(row_ins), len(par_ins), len(out_defs)

    def body(*refs):
        rin, pin = refs[:nr], refs[nr:nr + npar]
        outs, reds = refs[nr + npar:nr + npar + no], refs[nr + npar + no:]
        i = pl.program_id(0)

        @pl.when(i == 0)
        def _():
            for r in reds:
                r[...] = jnp.zeros_like(r)

        fn(i, rin, pin, outs, reds)

    in_specs = [pl.BlockSpec((tr, a.shape[1]), lambda i: (i, 0)) for a in row_ins]
    in_specs += [pl.BlockSpec(a.shape, lambda i, nd=a.ndim: (0,) * nd) for a in par_ins]
    out_specs = [pl.BlockSpec((tr, c), lambda i: (i, 0)) for c, _ in out_defs]
    out_specs += [pl.BlockSpec(shp, lambda i: (0, 0)) for shp in red_defs]
    out_shape = [jax.ShapeDtypeStruct((s, c), dt) for c, dt in out_defs]
    out_shape += [jax.ShapeDtypeStruct(shp, F32) for shp in red_defs]
    return pl.pallas_call(
        body, name=name, grid=(s // tr,), in_specs=in_specs, out_specs=out_specs, out_shape=out_shape,
        compiler_params=_cparams(("arbitrary",)),
    )(*row_ins, *par_ins)


def _rms(v):
    return lax.rsqrt(jnp.mean(v * v, axis=-1, keepdims=True) + EPS)


def _colsum(v):
    return jnp.sum(v, axis=0, keepdims=True)


def _rms_bwd(dn, n, rstd):
    return rstd * (dn - n * jnp.mean(dn * n, axis=-1, keepdims=True))


def _norm_mod_fwd(x, gmod, name):
    def fn(i, rin, pin, outs, reds):
        xv = rin[0][...]
        p = pin[0][...]
        n = xv * _rms(xv)
        outs[0][...] = ((n * p[0:1]) * (1.0 + p[1:2]) + p[2:3]).astype(BF16)

    return _rowwise(fn, [x], [gmod], [(x.shape[1], BF16)], [], name=name)[0]


def _rope(v, rot_c, rot_s):
    half = QK_ROPE // 2
    swapped = jnp.where(_lane(v.shape) < QK_NOPE + half, pltpu.roll(v, LANES - half, 1), pltpu.roll(v, half, 1))
    return v * rot_c + swapped * rot_s


def _rope_t(dv, rot_c, rot_s):
    half = QK_ROPE // 2
    ds = dv * rot_s
    lane = _lane(dv.shape)
    swapped = jnp.where(lane < QK_NOPE + half, pltpu.roll(ds, LANES - half, 1), pltpu.roll(ds, half, 1))
    in_rope = (lane >= QK_NOPE) & (lane < QK_NOPE + QK_ROPE)
    return dv * rot_c + jnp.where(in_rope, swapped, 0.0)


def _mla_prep_fwd(proj_mla, rot_c, rot_s, ng):
    o1, o2 = Q_LORA, Q_LORA + KV_LORA

    def fn(i, rin, pin, outs, reds):
        g = pin[0][...]
        ql = rin[0][:, 0:o1]
        kl = rin[0][:, o1:o2]
        outs[0][...] = (ql * _rms(ql) * g[0:1, 0:o1]).astype(BF16)
        outs[1][...] = (kl * _rms(kl) * g[0:1, o1:o2]).astype(BF16)
        kr = pltpu.roll(rin[0][:, o2:o2 + LANES], QK_NOPE, 1)
        outs[2][...] = _rope(kr, rin[1][...], rin[2][...]).astype(BF16)

    return _rowwise(fn, [proj_mla, rot_c, rot_s], [ng], [(Q_LORA, BF16), (KV_LORA, BF16), (LANES, BF16)], [],
                    name="mla_prep_fwd")


def _mla_prep_bwd(proj_mla, dqn, dkvn, dkr, rot_c, rot_s, ng):
    o1, o2 = Q_LORA, Q_LORA + KV_LORA

    def fn(i, rin, pin, outs, reds):
        g = pin[0][...]
        ql = rin[0][:, 0:o1]
        kl = rin[0][:, o1:o2]
        rq, rk = _rms(ql), _rms(kl)
        nq, nk = ql * rq, kl * rk
        dq, dk = rin[1][...], rin[2][...]
        outs[0][:, 0:o1] = _rms_bwd(dq * g[0:1, 0:o1], nq, rq).astype(BF16)
        outs[0][:, o1:o2] = _rms_bwd(dk * g[0:1, o1:o2], nk, rk).astype(BF16)
        dkr_pre = pltpu.roll(_rope_t(rin[3][...], rin[4][...], rin[5][...]), LANES - QK_NOPE, 1)
        outs[0][:, o2:] = jnp.where(_lane(dkr_pre.shape) < QK_ROPE, dkr_pre, 0.0).astype(BF16)
        reds[0][0:1, 0:o1] += _colsum(dq * nq)
        reds[0][0:1, o1:o2] += _colsum(dk * nk)

    return _rowwise(fn, [proj_mla, dqn, dkvn, dkr, rot_c, rot_s], [ng], [(MLA_W, BF16)], [(SUBLANES, MLA_W)],
                    name="mla_prep_bwd")


def _rope_heads(q, rot_c, rot_s, transpose, name):
    def fn(i, rin, pin, outs, reds):
        c, sn = rin[1][...] * Q_PRESCALE, rin[2][...] * Q_PRESCALE
        for h in range(N_HEADS):
            sl = slice(h * LANES, (h + 1) * LANES)
            v = rin[0][:, sl]
            outs[0][:, sl] = (_rope_t(v, c, sn) if transpose else _rope(v, c, sn)).astype(BF16)

    return _rowwise(fn, [q, rot_c, rot_s], [], [(q.shape[1], BF16)], [], name=name)[0]


def _merge_fwd(pr, pm, proj_g):
    d = pr.shape[1]

    def fn(i, rin, pin, outs, reds):
        outs[0][...] = (_sigmoid(rin[2][:, 0:d]) * rin[0][...] + _sigmoid(rin[2][:, d:]) * rin[1][...]).astype(BF16)

    return _rowwise(fn, [pr, pm, proj_g], [], [(d, BF16)], [], name="merge_fwd")[0]


def _merge_bwd(dmerged, pr, pm, proj_g):
    d = pr.shape[1]

    def fn(i, rin, pin, outs, reds):
        dm = rin[0][...]
        sr, sm = _sigmoid(rin[3][:, 0:d]), _sigmoid(rin[3][:, d:])
        outs[0][...] = (dm * sr).astype(BF16)
        outs[1][...] = (dm * sm).astype(BF16)
        outs[2][:, 0:d] = (dm * rin[1][...] * sr * (1.0 - sr)).astype(BF16)
        outs[2][:, d:] = (dm * rin[2][...] * sm * (1.0 - sm)).astype(BF16)

    return _rowwise(fn, [dmerged, pr, pm, proj_g], [], [(d, BF16), (d, BF16), (2 * d, BF16)], [], name="merge_bwd")


def _resid_norm_fwd(x, o, gmod):
    d = x.shape[1]

    def fn(i, rin, pin, outs, reds):
        p = pin[0][...]
        x1 = rin[0][...] + p[3:4] * rin[1][...]
        outs[0][...] = x1
        outs[1][...] = ((x1 * _rms(x1) * p[0:1]) * (1.0 + p[1:2]) + p[2:3]).astype(BF16)

    return _rowwise(fn, [x, o], [gmod], [(d, F32), (d, BF16)], [], name="resid_norm_fwd")


def _final_fwd_bwd(x1, dn, target, par):
    d = x1.shape[1]

    def fn(i, rin, pin, outs, reds):
        p = pin[0][...]
        dnv = rin[1][...]
        x2 = rin[0][...] + p[0:1] * dnv
        rstd = _rms(x2)
        n3 = x2 * rstd
        err = n3 * p[1:2] - rin[2][...]
        dy = err * (1.0 / d)
        dx2 = _rms_bwd(dy * p[1:2], n3, rstd)
        outs[0][...] = dx2
        outs[1][...] = (dx2 * p[0:1]).astype(BF16)
        reds[0][0:1, :] += _colsum(dy * n3)
        reds[0][1:2, :] += _colsum(dx2 * dnv)
        reds[0][2:3, :] += jnp.zeros((1, d), F32) + jnp.sum(err * err)

    return _rowwise(fn, [x1, dn, target], [par], [(d, F32), (d, BF16)], [(SUBLANES, d)], name="final_fwd_bwd")


def _norm2_bwd(x1, dh2, dx2, o, gmod):
    d = x1.shape[1]

    def fn(i, rin, pin, outs, reds):
        p = pin[0][...]
        x1v, dh = rin[0][...], rin[1][...]
        rstd = _rms(x1v)
        n2 = x1v * rstd
        dx1 = rin[2][...] + _rms_bwd(dh * (p[0:1] * (1.0 + p[1:2])), n2, rstd)
        outs[0][...] = dx1
        outs[1][...] = (dx1 * p[3:4]).astype(BF16)
        reds[0][0:1, :] += _colsum(dh * n2 * (1.0 + p[1:2]))
        reds[0][1:2, :] += _colsum(dh * n2 * p[0:1])
        reds[0][2:3, :] += _colsum(dh)
        reds[0][3:4, :] += _colsum(dx1 * rin[3][...])

    return _rowwise(fn, [x1, dh2, dx2, o], [gmod], [(d, F32), (d, BF16)], [(SUBLANES, d)], name="norm2_bwd")


def _norm1_bwd(x, dh_a, dh_b, dh_c, dx1, gmod):
    d = x.shape[1]

    def fn(i, rin, pin, outs, reds):
        p = pin[0][...]
        xv = rin[0][...]
        dh = rin[1][...] + rin[2][...] + rin[3][...]
        rstd = _rms(xv)
        n1 = xv * rstd
        outs[0][...] = rin[4][...] + _rms_bwd(dh * (p[0:1] * (1.0 + p[1:2])), n1, rstd)
        reds[0][0:1, :] += _colsum(dh * n1 * (1.0 + p[1:2]))
        reds[0][1:2, :] += _colsum(dh * n1 * p[0:1])
        reds[0][2:3, :] += _colsum(dh)

    return _rowwise(fn, [x, dh_a, dh_b, dh_c, dx1], [gmod], [(d, F32)], [(SUBLANES, d)], name="norm1_bwd")


RNN_CHUNK = 512


def _shift_down(ref, base, n, j):
    v = ref[pl.ds(base, n + SUBLANES), :]
    return v[SUBLANES:] if j == 0 else pltpu.roll(v, j, 0)[SUBLANES:]


def _shift_up(ref, base, n, j, top_pad):
    v = ref[pl.ds(base + top_pad, n + SUBLANES), :]
    return v[:n] if j == 0 else pltpu.roll(v, n + SUBLANES - j, 0)[:n]


def _one_minus_exp(z):
    series = -z * (1.0 + z * (0.5 + z * (1.0 / 6.0 + z * (1.0 / 24.0 + z * (1.0 / 120.0 + z * (1.0 / 720.0))))))
    return jnp.where(z > -0.1, series, 1.0 - jnp.exp(z))


def _softplus(v):
    return jnp.maximum(v, 0.0) + jnp.log(1.0 + jnp.exp(-jnp.abs(v)))


def _rnn_gates(xc, w, wa, wx, sp):
    xb = xc.astype(BF16)
    ra = _sigmoid(jnp.dot(xb, wa, preferred_element_type=F32) + w[5:6])
    ix = _sigmoid(jnp.dot(xb, wx, preferred_element_type=F32) + w[6:7])
    la = (-LRU_C) * ra * sp
    a = jnp.exp(la)
    mult = jnp.sqrt(_one_minus_exp(2.0 * la))
    return ra, ix, a, mult


def _rnn_fwd(x_rnn, keep, rp, wa_bd, wx_bd):
    s, r = x_rnn.shape
    ts = min(RNN_CHUNK, s)

    def body(x_ref, keep_ref, rp_ref, wa_ref, wx_ref, xc_ref, ra_ref, ix_ref, hs_ref, xpad, a_s, b_s):
        xpad[0:SUBLANES, :] = jnp.zeros((SUBLANES, LANES), F32)
        xpad[SUBLANES:, :] = x_ref[...]
        w = rp_ref[...]
        sp = _softplus(-w[7:8])
        wa, wx = wa_ref[0], wx_ref[0]

        def chunk(c, carry):
            base = pl.multiple_of(c * ts, ts)
            xc = w[4:5] + w[3:4] * _shift_down(xpad, base, ts, 0)
            for j in range(1, 4):
                xc = xc + w[3 - j:4 - j] * _shift_down(xpad, base, ts, j)
            ra, ix, a, mult = _rnn_gates(xc, w, wa, wx, sp)
            kp = keep_ref[pl.ds(base, ts), :]
            xc_ref[pl.ds(base, ts), :] = xc
            ra_ref[pl.ds(base, ts), :] = ra
            ix_ref[pl.ds(base, ts), :] = ix
            a_s[pl.ds(base, ts), :] = a * kp
            b_s[pl.ds(base, ts), :] = jnp.where(kp > 0.0, mult, 1.0) * (ix * xc)
            return carry

        lax.fori_loop(0, s // ts, chunk, 0)

        row = _row((SUBLANES, LANES))

        def tile(i, h):
            r0 = pl.multiple_of(i * SUBLANES, SUBLANES)
            a = a_s[pl.ds(r0, SUBLANES), :]
            b = b_s[pl.ds(r0, SUBLANES), :]
            for sh in (1, 2, 4):
                a_sh = jnp.where(row >= sh, pltpu.roll(a, sh, 0), 1.0)
                b_sh = jnp.where(row >= sh, pltpu.roll(b, sh, 0), 0.0)
                b = a * b_sh + b
                a = a * a_sh
            hv = b + a * h
            hs_ref[pl.ds(r0, SUBLANES), :] = hv
            return hv[SUBLANES - 1:SUBLANES, :]

        lax.fori_loop(0, s // SUBLANES, tile, jnp.zeros((1, LANES), F32), unroll=4)

    col = pl.BlockSpec((s, LANES), lambda g: (0, g))
    return pl.pallas_call(
        body, name="rnn_fwd", grid=(r // LANES,),
        in_specs=[col, pl.BlockSpec((s, 1), lambda g: (0, 0)), pl.BlockSpec((SUBLANES, LANES), lambda g: (0, g)),
                  pl.BlockSpec((1, LANES, LANES), lambda g: (g, 0, 0)), pl.BlockSpec((1, LANES, LANES), lambda g: (g, 0, 0))],
        out_specs=[col] * 4,
        out_shape=[jax.ShapeDtypeStruct((s, r), F32)] * 4,
        scratch_shapes=[pltpu.VMEM((s + SUBLANES, LANES), F32), pltpu.VMEM((s, LANES), F32), pltpu.VMEM((s, LANES), F32)],
        compiler_params=_cparams(("arbitrary",)),
    )(x_rnn, keep, rp, wa_bd, wx_bd)


def _rnn_bwd(x_rnn, xc, ra, ix, hs, dy, keep, rp, wa_bd, wx_bd):
    s, r = x_rnn.shape
    ts = min(RNN_CHUNK, s)
    nt = s // SUBLANES

    def body(x_ref, xc_ref, ra_ref, ix_ref, hs_ref, dy_ref, keep_ref, rp_ref, wa_ref, wx_ref,
             dx_ref, dwa_ref, dwx_ref, red_ref, xpad, hpad, a_s, dh_s, dxc_s):
        zero8 = jnp.zeros((SUBLANES, LANES), F32)
        xpad[0:SUBLANES, :] = zero8
        xpad[SUBLANES:, :] = x_ref[...]
        hpad[0:SUBLANES, :] = zero8
        hpad[SUBLANES:, :] = hs_ref[...]
        a_s[s:, :] = zero8
        dxc_s[s:, :] = zero8
        w = rp_ref[...]
        sp = _softplus(-w[7:8])
        wa, wx = wa_ref[0], wx_ref[0]

        def decay(c, carry):
            base = pl.multiple_of(c * ts, ts)
            a = jnp.exp((-LRU_C) * ra_ref[pl.ds(base, ts), :] * sp)
            a_s[pl.ds(base, ts), :] = a * keep_ref[pl.ds(base, ts), :]
            return carry

        lax.fori_loop(0, s // ts, decay, 0)

        row = _row((SUBLANES, LANES))

        def tile(n, nxt):
            i = nt - 1 - n
            r0 = pl.multiple_of(i * SUBLANES, SUBLANES)
            a_here = a_s[pl.ds(r0, SUBLANES), :]
            a_next = a_s[pl.ds(r0 + SUBLANES, SUBLANES), :]
            a = jnp.where(row < SUBLANES - 1, pltpu.roll(a_here, SUBLANES - 1, 0), a_next[0:1, :])
            b = dy_ref[pl.ds(r0, SUBLANES), :]
            for sh in (1, 2, 4):
                a_sh = jnp.where(row < SUBLANES - sh, pltpu.roll(a, SUBLANES - sh, 0), 1.0)
                b_sh = jnp.where(row < SUBLANES - sh, pltpu.roll(b, SUBLANES - sh, 0), 0.0)
                b = a * b_sh + b
                a = a * a_sh
            dh = b + a * nxt
            dh_s[pl.ds(r0, SUBLANES), :] = dh
            return dh[0:1, :]

        lax.fori_loop(0, nt, tile, jnp.zeros((1, LANES), F32), unroll=4)

        def gates(c, carry):
            dwa, dwx, d_ba, d_bx, d_sp, d_cb = carry
            base = pl.multiple_of(c * ts, ts)
            xcv = xc_ref[pl.ds(base, ts), :]
            rav = ra_ref[pl.ds(base, ts), :]
            ixv = ix_ref[pl.ds(base, ts), :]
            kp = keep_ref[pl.ds(base, ts), :]
            dh = dh_s[pl.ds(base, ts), :]
            h_prev = _shift_down(hpad, base, ts, 1)
            la = (-LRU_C) * rav * sp
            a = jnp.exp(la)
            mult = jnp.sqrt(_one_minus_exp(2.0 * la))
            mult_eff = jnp.where(kp > 0.0, mult, 1.0)
            d_a = dh * h_prev * kp
            d_mult = dh * (ixv * xcv) * kp
            d_ix = dh * mult_eff * xcv
            d_xc = dh * mult_eff * ixv
            d_la = d_a * a - d_mult * (a * a) / mult
            d_pa = d_la * ((-LRU_C) * sp) * rav * (1.0 - rav)
            d_px = d_ix * ixv * (1.0 - ixv)
            xb = xcv.astype(BF16)
            pab, pxb = d_pa.astype(BF16), d_px.astype(BF16)
            tn = (((0,), (0,)), ((), ()))
            nt_ = (((1,), (1,)), ((), ()))
            dwa = dwa + lax.dot_general(xb, pab, tn, preferred_element_type=F32)
            dwx = dwx + lax.dot_general(xb, pxb, tn, preferred_element_type=F32)
            d_xc = d_xc + lax.dot_general(pab, wa, nt_, preferred_element_type=F32)
            d_xc = d_xc + lax.dot_general(pxb, wx, nt_, preferred_element_type=F32)
            dxc_s[pl.ds(base, ts), :] = d_xc
            return (dwa, dwx, d_ba + _colsum(d_pa), d_bx + _colsum(d_px),
                    d_sp + _colsum(d_la * ((-LRU_C) * rav)), d_cb + _colsum(d_xc))

        z1 = jnp.zeros((1, LANES), F32)
        zw = jnp.zeros((LANES, LANES), F32)
        dwa, dwx, d_ba, d_bx, d_sp, d_cb = lax.fori_loop(0, s // ts, gates, (zw, zw, z1, z1, z1, z1))
        dwa_ref[0] = dwa
        dwx_ref[0] = dwx

        def conv(c, carry):
            base = pl.multiple_of(c * ts, ts)
            d_here = dxc_s[pl.ds(base, ts), :]
            dx = w[3:4] * d_here
            for j in range(1, 4):
                dx = dx + w[3 - j:4 - j] * _shift_up(dxc_s, base, ts, j, 0)
            dx_ref[pl.ds(base, ts), :] = dx.astype(BF16)
            return tuple(carry[k] + _colsum(d_here * _shift_down(xpad, base, ts, 3 - k)) for k in range(4))

        d_w = lax.fori_loop(0, s // ts, conv, (z1, z1, z1, z1))
        d_lru = d_sp * (-_sigmoid(-w[7:8]))
        red_ref[...] = jnp.concatenate(list(d_w) + [d_cb, d_ba, d_bx, d_lru], axis=0)

    col = pl.BlockSpec((s, LANES), lambda g: (0, g))
    par = pl.BlockSpec((SUBLANES, LANES), lambda g: (0, g))
    wsp = pl.BlockSpec((1, LANES, LANES), lambda g: (g, 0, 0))
    return pl.pallas_call(
        body, name="rnn_bwd", grid=(r // LANES,),
        in_specs=[col] * 6 + [pl.BlockSpec((s, 1), lambda g: (0, 0)), par, wsp, wsp],
        out_specs=[col, wsp, wsp, par],
        out_shape=[jax.ShapeDtypeStruct((s, r), BF16), jax.ShapeDtypeStruct((r // LANES, LANES, LANES), F32),
                   jax.ShapeDtypeStruct((r // LANES, LANES, LANES), F32), jax.ShapeDtypeStruct((SUBLANES, r), F32)],
        scratch_shapes=[pltpu.VMEM((s + SUBLANES, LANES), F32), pltpu.VMEM((s + SUBLANES, LANES), F32),
                        pltpu.VMEM((s + SUBLANES, LANES), F32), pltpu.VMEM((s, LANES), F32),
                        pltpu.VMEM((s + SUBLANES, LANES), F32)],
        compiler_params=_cparams(("arbitrary",)),
    )(x_rnn, xc, ra, ix, hs, dy, keep, rp, wa_bd, wx_bd)


ATT_BLOCK = 512
NT_DIMS = (((1,), (1,)), ((), ()))
TN_DIMS = (((0,), (0,)), ((), ()))


LOG2E = 1.4426950408889634
LN2 = 0.6931471805599453
Q_PRESCALE = ATT_SCALE * LOG2E


def _att_scores(q, kvb, krb, diagonal):
    k_eff = jnp.where(_lane(kvb.shape) < QK_NOPE, kvb, krb)
    sc = lax.dot_general(q, k_eff, NT_DIMS, preferred_element_type=F32)
    if diagonal:
        sc = jnp.where(lax.broadcasted_iota(jnp.int32, sc.shape, 1) <= lax.broadcasted_iota(jnp.int32, sc.shape, 0), sc, -jnp.inf)
    return sc, k_eff


def _att_fwd(q, kv, kr):
    s = q.shape[0]
    t = min(ATT_BLOCK, s)
    nb = s // t

    def body(q_ref, kv_ref, kr_ref, y_ref, lse_ref, m_s, acc_s):
        i, j = pl.program_id(1), pl.program_id(2)

        @pl.when(j == 0)
        def _():
            m_s[...] = jnp.full(m_s.shape, -jnp.inf, F32)
            acc_s[...] = jnp.zeros(acc_s.shape, F32)

        def step(diagonal):
            krb = kr_ref[...]
            lane = _lane((t, LANES))
            for hh in range(2):
                sl = slice(hh * LANES, (hh + 1) * LANES)
                kvb = kv_ref[:, sl]
                sc, _ = _att_scores(q_ref[:, sl], kvb, krb, diagonal)
                m_prev = m_s[hh]
                m_new = jnp.maximum(m_prev, jnp.max(sc, axis=-1, keepdims=True))
                alpha = jnp.exp2(m_prev - m_new)
                p = jnp.exp2(sc - m_new[:, 0:1]).astype(BF16)
                ones_v = jnp.where(lane < QK_NOPE, jnp.ones_like(kvb), kvb)
                acc_s[hh] = alpha * acc_s[hh] + jnp.dot(p, ones_v, preferred_element_type=F32)
                m_s[hh] = m_new

        @pl.when(j < i)
        def _():
            step(False)

        @pl.when(j == i)
        def _():
            step(True)
            lane = _lane((t, LANES))
            a0, a1 = acc_s[0], acc_s[1]
            l0, l1 = a0[:, 0:1], a1[:, 0:1]
            y_ref[...] = jnp.where(lane < V_HEAD, pltpu.roll(a0 / l0, V_HEAD, 1), a1 / l1).astype(BF16)
            lse_ref[...] = jnp.where(lane < V_HEAD, m_s[0] + jnp.log(l0) * LOG2E, m_s[1] + jnp.log(l1) * LOG2E)

    return pl.pallas_call(
        body, name="att_fwd", grid=(N_HEADS // 2, nb, nb),
        in_specs=[pl.BlockSpec((t, 2 * LANES), lambda p, i, j: (i, p)),
                  pl.BlockSpec((t, 2 * LANES), lambda p, i, j: (jnp.minimum(j, i), p)),
                  pl.BlockSpec((t, LANES), lambda p, i, j: (jnp.minimum(j, i), 0))],
        out_specs=[pl.BlockSpec((t, LANES), lambda p, i, j: (i, p))] * 2,
        out_shape=[jax.ShapeDtypeStruct((s, N_HEADS * V_HEAD), BF16), jax.ShapeDtypeStruct((s, N_HEADS * V_HEAD), F32)],
        scratch_shapes=[pltpu.VMEM((2, t, LANES), F32)] * 2,
        compiler_params=_cparams(("arbitrary", "arbitrary", "arbitrary")),
    )(q, kv, kr)


def _att_bwd(q, kv, kr, y, lse, dy):
    s = q.shape[0]
    t = min(ATT_BLOCK, s)
    nb = s // t

    def body(q_ref, kv_ref, kr_ref, y_ref, lse_ref, dy_ref, dq_ref, dkv_ref, dkr_ref, dkv_s):
        p_, j, i = pl.program_id(0), pl.program_id(1), pl.program_id(2)

        @pl.when((p_ == 0) & (j == 0) & (i == 0))
        def _():
            dkr_ref[...] = jnp.zeros(dkr_ref.shape, F32)

        @pl.when((j == 0) & (i == 0))
        def _():
            dq_ref[...] = jnp.zeros(dq_ref.shape, F32)

        @pl.when(i == 0)
        def _():
            dkv_s[...] = jnp.zeros(dkv_s.shape, F32)

        def step(diagonal):
            lane = _lane((t, LANES))
            krb = kr_ref[...]
            dyv = dy_ref[...]
            yv = y_ref[...].astype(F32)
            lsev = lse_ref[...]
            rows = pl.ds(pl.multiple_of(i * t, t), t)
            dkr_acc = jnp.zeros((t, LANES), F32)
            for hh in range(2):
                sl = slice(hh * LANES, (hh + 1) * LANES)
                qb, kvb = q_ref[:, sl], kv_ref[:, sl]
                sc, k_eff = _att_scores(qb, kvb, krb, diagonal)
                if hh == 0:
                    do_pad = jnp.where(lane >= V_HEAD, pltpu.roll(dyv, V_HEAD, 1), 0.0)
                    o_pad = jnp.where(lane >= V_HEAD, pltpu.roll(yv, V_HEAD, 1), 0.0)
                    lse_h = lsev[:, 0:1]
                else:
                    do_pad = jnp.where(lane >= V_HEAD, dyv, 0.0)
                    o_pad = jnp.where(lane >= V_HEAD, yv, 0.0)
                    lse_h = lsev[:, LANES - 1:LANES]
                do_ln2 = do_pad * LN2
                delta = jnp.sum(do_ln2 * o_pad, axis=-1, keepdims=True)
                prob = jnp.exp2(sc - lse_h)
                dv = lax.dot_general(prob.astype(BF16), do_pad.astype(BF16), TN_DIMS, preferred_element_type=F32)
                dp = lax.dot_general(do_ln2.astype(BF16), kvb, NT_DIMS, preferred_element_type=F32)
                ds = (prob * (dp - delta)).astype(BF16)
                dq_ref[rows, sl] += jnp.dot(ds, k_eff, preferred_element_type=F32)
                dk_eff = lax.dot_general(ds, qb, TN_DIMS, preferred_element_type=F32)
                dkv_s[hh] += dv + jnp.where(lane < QK_NOPE, dk_eff, 0.0)
                dkr_acc = dkr_acc + jnp.where(lane >= QK_NOPE, dk_eff, 0.0)
            dkr_ref[pl.ds(pl.multiple_of(j * t, t), t), :] += dkr_acc

        @pl.when(i > j)
        def _():
            step(False)

        @pl.when(i == j)
        def _():
            step(True)

        @pl.when(i == nb - 1)
        def _():
            dkv_ref[:, 0:LANES] = dkv_s[0].astype(BF16)
            dkv_ref[:, LANES:] = dkv_s[1].astype(BF16)

    qi = lambda p, j, i: (jnp.maximum(i, j), p)
    return pl.pallas_call(
        body, name="att_bwd", grid=(N_HEADS // 2, nb, nb),
        in_specs=[pl.BlockSpec((t, 2 * LANES), qi),
                  pl.BlockSpec((t, 2 * LANES), lambda p, j, i: (j, p)),
                  pl.BlockSpec((t, LANES), lambda p, j, i: (j, 0)),
                  pl.BlockSpec((t, LANES), qi), pl.BlockSpec((t, LANES), qi), pl.BlockSpec((t, LANES), qi)],
        out_specs=[pl.BlockSpec((s, 2 * LANES), lambda p, j, i: (0, p)),
                   pl.BlockSpec((t, 2 * LANES), lambda p, j, i: (j, p)),
                   pl.BlockSpec((s, LANES), lambda p, j, i: (0, 0))],
        out_shape=[jax.ShapeDtypeStruct((s, N_HEADS * LANES), F32), jax.ShapeDtypeStruct((s, N_HEADS * LANES), BF16),
                   jax.ShapeDtypeStruct((s, LANES), F32)],
        scratch_shapes=[pltpu.VMEM((2, t, LANES), F32)],
        compiler_params=_cparams(("arbitrary", "arbitrary", "arbitrary")),
    )(q, kv, kr, y, lse, dy)


FFN_COLS = 256


def _ffn_conv(pad_ref, w, base, n):
    u = w[3:4] + w[2:3] * _shift_down(pad_ref, base, n, 0)
    for j in range(1, 3):
        u = u + w[2 - j:3 - j] * _shift_down(pad_ref, base, n, j)
    return u


def _ffn_act_fwd(up, fp):
    s, f2 = up.shape
    f = f2 // 2
    tc = FFN_COLS
    ts = min(RNN_CHUNK, s)
    nfb = f // tc

    def body(ug_ref, uv_ref, wg_ref, wv_ref, act_ref, gpad, vpad):
        zero8 = jnp.zeros((SUBLANES, tc), F32)
        gpad[0:SUBLANES, :] = zero8
        vpad[0:SUBLANES, :] = zero8
        gpad[SUBLANES:, :] = ug_ref[...]
        vpad[SUBLANES:, :] = uv_ref[...]
        wg, wv = wg_ref[...], wv_ref[...]

        def chunk(c, carry):
            base = pl.multiple_of(c * ts, ts)
            g = _ffn_conv(gpad, wg, base, ts)
            v = _ffn_conv(vpad, wv, base, ts)
            act_ref[pl.ds(base, ts), :] = (g * _sigmoid(g) * v).astype(BF16)
            return carry

        lax.fori_loop(0, s // ts, chunk, 0)

    return pl.pallas_call(
        body, name="ffn_act_fwd", grid=(nfb,),
        in_specs=[pl.BlockSpec((s, tc), lambda b: (0, b)), pl.BlockSpec((s, tc), lambda b: (0, b + nfb)),
                  pl.BlockSpec((SUBLANES, tc), lambda b: (0, b)), pl.BlockSpec((SUBLANES, tc), lambda b: (0, b + nfb))],
        out_specs=pl.BlockSpec((s, tc), lambda b: (0, b)),
        out_shape=jax.ShapeDtypeStruct((s, f), BF16),
        scratch_shapes=[pltpu.VMEM((s + SUBLANES, tc), F32)] * 2,
        compiler_params=_cparams(("arbitrary",)),
    )(up, up, fp, fp)


def _ffn_act_bwd(up, dact, fp):
    s, f2 = up.shape
    f = f2 // 2
    tc = FFN_COLS
    ts = min(RNN_CHUNK, s)
    nfb = f // tc

    def body(ug_ref, uv_ref, da_ref, wg_ref, wv_ref, dup_ref, red_ref, gpad, vpad, dgs, dvs):
        half = pl.program_id(1)
        wg, wv = wg_ref[...], wv_ref[...]

        @pl.when(half == 0)
        def _():
            zero8 = jnp.zeros((SUBLANES, tc), F32)
            gpad[0:SUBLANES, :] = zero8
            vpad[0:SUBLANES, :] = zero8
            gpad[SUBLANES:, :] = ug_ref[...]
            vpad[SUBLANES:, :] = uv_ref[...]
            dgs[s:, :] = zero8
            dvs[s:, :] = zero8

            def act(c, carry):
                base = pl.multiple_of(c * ts, ts)
                g = _ffn_conv(gpad, wg, base, ts)
                v = _ffn_conv(vpad, wv, base, ts)
                da = da_ref[pl.ds(base, ts), :]
                sg = _sigmoid(g)
                dgs[pl.ds(base, ts), :] = da * v * (sg * (1.0 + g * (1.0 - sg)))
                dvs[pl.ds(base, ts), :] = da * (g * sg)
                return carry

            lax.fori_loop(0, s // ts, act, 0)

        def conv_t(src, pad, w, out_ref, red_ref):
            def chunk(c, carry):
                base = pl.multiple_of(c * ts, ts)
                d_here = src[pl.ds(base, ts), :]
                dx = w[2:3] * d_here
                for j in range(1, 3):
                    dx = dx + w[2 - j:3 - j] * _shift_up(src, base, ts, j, 0)
                out_ref[pl.ds(base, ts), :] = dx.astype(BF16)
                taps = tuple(carry[k] + _colsum(d_here * _shift_down(pad, base, ts, 2 - k)) for k in range(3))
                return taps + (carry[3] + _colsum(d_here),)

            z1 = jnp.zeros((1, tc), F32)
            red = lax.fori_loop(0, s // ts, chunk, (z1, z1, z1, z1))
            red_ref[...] = jnp.concatenate(list(red) + [jnp.zeros((4, tc), F32)], axis=0)

        @pl.when(half == 0)
        def _():
            conv_t(dgs, gpad, wg, dup_ref, red_ref)

        @pl.when(half == 1)
        def _():
            conv_t(dvs, vpad, wv, dup_ref, red_ref)

    gcol = pl.BlockSpec((s, tc), lambda b, h: (0, b))
    vcol = pl.BlockSpec((s, tc), lambda b, h: (0, b + nfb))
    gpar = pl.BlockSpec((SUBLANES, tc), lambda b, h: (0, b))
    vpar = pl.BlockSpec((SUBLANES, tc), lambda b, h: (0, b + nfb))
    return pl.pallas_call(
        body, name="ffn_act_bwd", grid=(nfb, 2),
        in_specs=[gcol, vcol, gcol, gpar, vpar],
        out_specs=[pl.BlockSpec((s, tc), lambda b, h: (0, b + h * nfb)),
                   pl.BlockSpec((SUBLANES, tc), lambda b, h: (0, b + h * nfb))],
        out_shape=[jax.ShapeDtypeStruct((s, f2), BF16), jax.ShapeDtypeStruct((SUBLANES, f2), F32)],
        scratch_shapes=[pltpu.VMEM((s + SUBLANES, tc), F32)] * 4,
        compiler_params=_cparams(("arbitrary", "arbitrary")),
    )(up, up, dact, fp, fp)


def _rows8(rows, width):
    rows = [r.reshape(1, width).astype(F32) for r in rows]
    return jnp.concatenate(rows + [jnp.zeros((SUBLANES - len(rows), width), F32)], axis=0)


def _block_diag(w):
    n, b, _ = w.shape
    w = w.reshape(n // 2, 2, b, b)
    z = jnp.zeros((n // 2, b, b), w.dtype)
    top = jnp.concatenate([w[:, 0], z], axis=2)
    bot = jnp.concatenate([z, w[:, 1]], axis=2)
    return jnp.concatenate([top, bot], axis=1)


def _block_diag_t(bd):
    n, b2, _ = bd.shape
    b = b2 // 2
    return jnp.stack([bd[:, :b, :b], bd[:, b:, b:]], axis=1).reshape(2 * n, b, b)


def _local_step(x, mod, positions, target, wts, sm, emit):
    s, d = x.shape
    o_rnn, o_mla = D_RNN, D_RNN + Q_LORA + KV_LORA + QK_ROPE
    w_in = wts["w_in"]
    w_in_rnn = w_in[:, :o_rnn]
    w_in_mla = jnp.concatenate([w_in[:, o_rnn:o_mla], jnp.zeros((d, MLA_W - (o_mla - o_rnn)), w_in.dtype)], axis=1)
    w_in_g = w_in[:, o_mla:]
    hd = QK_NOPE + QK_ROPE
    w_uq_p = jnp.pad(wts["w_uq"].reshape(Q_LORA, N_HEADS, hd), ((0, 0), (0, 0), (0, LANES - hd))).reshape(Q_LORA, N_HEADS * LANES)
    w_ukv = wts["w_ukv"]
    wa_bd = _block_diag(sm["w_gate_a"]).astype(BF16)
    wx_bd = _block_diag(sm["w_gate_x"]).astype(BF16)

    pos = positions.reshape(s)
    half = QK_ROPE // 2
    inv_freq = ROPE_THETA ** (-jnp.arange(half, dtype=F32) / half)
    ang = pos.astype(F32)[:, None] * inv_freq
    cos, sin = jnp.cos(ang), jnp.sin(ang)
    rot_c = jnp.concatenate([jnp.ones((s, QK_NOPE), F32), cos, cos, jnp.ones((s, LANES - hd), F32)], axis=1)
    rot_s = jnp.concatenate([jnp.zeros((s, QK_NOPE), F32), -sin, sin, jnp.zeros((s, LANES - hd), F32)], axis=1)
    keep = (pos != 0).astype(F32).reshape(s, 1)

    gmod1 = _rows8([sm["norm1_g"], mod[1], mod[0]], d)
    gmod2 = _rows8([sm["norm2_g"], mod[4], mod[3], mod[2]], d)
    rp = jnp.concatenate([sm["conv_w"].reshape(4, D_RNN), _rows8([sm["conv_b"], sm["b_gate_a"], sm["b_gate_x"], sm["lru_param"]], D_RNN)[:4]], axis=0)
    fp = _rows8([sm["ffn_conv_w"][0], sm["ffn_conv_w"][1], sm["ffn_conv_w"][2], sm["ffn_conv_b"]], 2 * D_FF)
    ng = _rows8([jnp.concatenate([sm["q_norm_g"].reshape(-1), sm["kv_norm_g"].reshape(-1), jnp.zeros((MLA_W - Q_LORA - KV_LORA,), F32)])], MLA_W)
    fpar = _rows8([mod[5], sm["final_g"]], d)

    h = _norm_mod_fwd(x, gmod1, "norm1_fwd")
    proj_rnn = _mm(h, w_in_rnn, name="mm_in_rnn")
    proj_mla = _mm(h, w_in_mla, name="mm_in_mla")
    proj_g = _mm(h, w_in_g, name="mm_in_g")
    xc, ra, ix, hs = _rnn_fwd(proj_rnn, keep, rp, wa_bd, wx_bd)
    qn, kvn, kr = _mla_prep_fwd(proj_mla, rot_c, rot_s, ng)
    q_rot = _rope_heads(_mm(qn, w_uq_p, name="mm_uq"), rot_c, rot_s, False, "rope_fwd")
    kv = _mm(kvn, w_ukv, out_dtype=BF16, name="mm_ukv")
    y_mla, lse = _att_fwd(q_rot, kv, kr)
    pr = _mm(hs, wts["w_proj_rnn"], name="mm_proj_rnn")
    pm = _mm(y_mla, wts["w_proj_mla"], name="mm_proj_mla")
    merged = _merge_fwd(pr, pm, proj_g)
    o = _mm(merged, wts["w_out"], name="mm_out")
    x1, h2 = _resid_norm_fwd(x, o, gmod2)
    up = _mm(h2, wts["w_up"], name="mm_up")
    act = _ffn_act_fwd(up, fp)
    dn = _mm(act, wts["w_down"], name="mm_down")

    dx2, ddn, red_f = _final_fwd_bwd(x1, dn, target, fpar)
    dact = _mm(ddn, wts["w_down"], tb=True, name="mm_d_act")
    tok = emit("w_down", _mm(act, ddn, ta=True, out_dtype=BF16, name="mm_dw_down"))
    dup, red_ffn = _ffn_act_bwd(up, dact, fp + tok)
    dh2 = _mm(dup, wts["w_up"], tb=True, name="mm_d_h2")
    tok = tok + emit("w_up", _mm(h2, dup, ta=True, out_dtype=BF16, name="mm_dw_up"))
    dx1, do, red_2 = _norm2_bwd(x1, dh2, dx2, o, gmod2 + tok)
    dmerged = _mm(do, wts["w_out"], tb=True, name="mm_d_merged")
    tok = tok + emit("w_out", _mm(merged, do, ta=True, out_dtype=BF16, name="mm_dw_out"))
    dpr, dpm, dg = _merge_bwd(dmerged, pr, pm, proj_g)
    dy_rnn = _mm(dpr, wts["w_proj_rnn"], tb=True, name="mm_d_yrnn")
    tok = tok + emit("w_proj_rnn", _mm(hs, dpr, ta=True, out_dtype=BF16, name="mm_dw_proj_rnn"))
    dy_mla = _mm(dpm, wts["w_proj_mla"], tb=True, name="mm_d_ymla")
    tok = tok + emit("w_proj_mla", _mm(y_mla, dpm, ta=True, out_dtype=BF16, name="mm_dw_proj_mla"))
    dq_rot, dkv, dkr = _att_bwd(q_rot, kv, kr, y_mla, lse, dy_mla)
    dq = _rope_heads(dq_rot, rot_c, rot_s, True, "rope_bwd")
    dqn = _mm(dq, w_uq_p, tb=True, name="mm_d_qn")
    dw_uq_p = _mm(qn, dq, ta=True, out_dtype=BF16, name="mm_dw_uq")
    tok = tok + emit("w_uq", dw_uq_p.reshape(Q_LORA, N_HEADS, LANES)[:, :, :hd].reshape(Q_LORA, N_HEADS * hd))
    dkvn = _mm(dkv, w_ukv, tb=True, name="mm_d_kvn")
    tok = tok + emit("w_ukv", _mm(kvn, dkv, ta=True, out_dtype=BF16, name="mm_dw_ukv"))
    dproj_mla, red_m = _mla_prep_bwd(proj_mla, dqn, dkvn, dkr, rot_c, rot_s, ng + tok)
    dx_rnn, dwa_bd, dwx_bd, red_r = _rnn_bwd(proj_rnn, xc, ra, ix, hs, dy_rnn, keep, rp + tok, wa_bd, wx_bd)
    dw_in = jnp.concatenate([
        _mm(h, dx_rnn, ta=True, out_dtype=BF16, name="mm_dw_in_rnn"),
        _mm(h, dproj_mla, ta=True, out_dtype=BF16, name="mm_dw_in_mla")[:, :o_mla - o_rnn],
        _mm(h, dg, ta=True, out_dtype=BF16, name="mm_dw_in_g")], axis=1)
    tok = tok + emit("w_in", dw_in)
    dh_a = _mm(dx_rnn, w_in_rnn, tb=True, name="mm_d_h_rnn")
    dh_b = _mm(dproj_mla, w_in_mla, tb=True, name="mm_d_h_mla")
    dh_c = _mm(dg, w_in_g, tb=True, name="mm_d_h_g")
    grad_x, red_1 = _norm1_bwd(x, dh_a, dh_b, dh_c, dx1, gmod1 + tok)

    gs = {
        "norm1_g": red_1[0], "conv_w": red_r[0:4], "conv_b": red_r[4], "w_gate_a": _block_diag_t(dwa_bd),
        "b_gate_a": red_r[5], "w_gate_x": _block_diag_t(dwx_bd), "b_gate_x": red_r[6], "lru_param": red_r[7],
        "q_norm_g": red_m[0, :Q_LORA], "kv_norm_g": red_m[0, Q_LORA:Q_LORA + KV_LORA], "norm2_g": red_2[0],
        "ffn_conv_w": red_ffn[0:3], "ffn_conv_b": red_ffn[3], "final_g": red_f[0],
    }
    dmod = jnp.stack([red_1[2], red_1[1], red_2[3], red_2[2], red_2[1], red_f[1]], axis=0)
    return red_f[2, 0], grad_x, gs, dmod


MESH_IDS = pl.DeviceIdType.MESH
HBM_SPEC = pl.BlockSpec(memory_space=pltpu.HBM)


def _my_slot():
    return 4 * lax.axis_index("x") + 2 * lax.axis_index("y") + lax.axis_index("c")


def _all_gather(arrs, name):
    n = len(arrs)

    def body(*refs):
        ins, outs = refs[:n], refs[n:2 * n]
        send_sems, recv_sems, local_sems = refs[2 * n:]
        x, y, c = lax.axis_index("x"), lax.axis_index("y"), lax.axis_index("c")
        me, sibling = (x, y, c), (x, y, 1 - c)
        chips = [(1 - x, y), (x, 1 - y), (1 - x, 1 - y)]

        def slot(dev):
            return 4 * dev[0] + 2 * dev[1] + dev[2]

        def copy(a, k, block, to, src=None):
            dst = outs[a].at[slot(block)]
            return pltpu.make_async_remote_copy(
                src_ref=dst if src is None else src, dst_ref=dst, send_sem=send_sems.at[a, k], recv_sem=recv_sems.at[a, k],
                device_id=to, device_id_type=MESH_IDS)

        mine = [pltpu.make_async_copy(ins[a], outs[a].at[slot(me)], local_sems.at[a]) for a in range(n)]
        for cp in mine:
            cp.start()
        first = []
        for a in range(n):
            first.append(copy(a, 0, me, sibling, src=ins[a]))
            first += [copy(a, 1 + j, me, (*chip, c), src=ins[a]) for j, chip in enumerate(chips)]
        for cp in first:
            cp.start()
        passed = []
        for j, chip in enumerate(chips):
            for a in range(n):
                copy(a, 1 + j, (*chip, c), me).wait_recv()
                fwd = copy(a, 4 + j, (*chip, c), sibling)
                fwd.start()
                passed.append(fwd)
        for a in range(n):
            copy(a, 0, sibling, me).wait_recv()
            for j, chip in enumerate(chips):
                copy(a, 4 + j, (*chip, 1 - c), me).wait_recv()
        for cp in first + passed:
            cp.wait_send()
        for cp in mine:
            cp.wait()

    return pl.pallas_call(
        body, name=name,
        in_specs=[HBM_SPEC] * n, out_specs=[HBM_SPEC] * n,
        out_shape=[jax.ShapeDtypeStruct((N_DEV,) + a.shape, a.dtype) for a in arrs],
        scratch_shapes=[pltpu.SemaphoreType.DMA((n, 7)), pltpu.SemaphoreType.DMA((n, 7)), pltpu.SemaphoreType.DMA((n,))],
    )(*arrs)


def _all_to_all(arrs, name):
    n = len(arrs)
    flips = [(dx, dy, dc) for dx in (0, 1) for dy in (0, 1) for dc in (0, 1)][1:]

    def body(*refs):
        ins, outs = refs[:n], refs[n:2 * n]
        send_sems, recv_sems, local_sems = refs[2 * n:]
        x, y, c = lax.axis_index("x"), lax.axis_index("y"), lax.axis_index("c")
        me = 4 * x + 2 * y + c
        mine = [pltpu.make_async_copy(ins[a].at[me], outs[a].at[me], local_sems.at[a]) for a in range(n)]
        for cp in mine:
            cp.start()
        sends = []
        for k, (dx, dy, dc) in enumerate(flips):
            peer = (x ^ dx, y ^ dy, c ^ dc)
            peer_slot = 4 * peer[0] + 2 * peer[1] + peer[2]
            for a in range(n):
                sends.append(pltpu.make_async_remote_copy(
                    src_ref=ins[a].at[peer_slot], dst_ref=outs[a].at[me], send_sem=send_sems.at[a, k],
                    recv_sem=recv_sems.at[a, k], device_id=peer, device_id_type=MESH_IDS))
        for cp in sends:
            cp.start()
        for k, (dx, dy, dc) in enumerate(flips):
            peer_slot = 4 * (x ^ dx) + 2 * (y ^ dy) + (c ^ dc)
            for a in range(n):
                pltpu.make_async_remote_copy(
                    src_ref=ins[a].at[peer_slot], dst_ref=outs[a].at[peer_slot], send_sem=send_sems.at[a, k],
                    recv_sem=recv_sems.at[a, k], device_id=(x ^ dx, y ^ dy, c ^ dc), device_id_type=MESH_IDS).wait_recv()
        for cp in sends:
            cp.wait_send()
        for cp in mine:
            cp.wait()

    return pl.pallas_call(
        body, name=name,
        in_specs=[HBM_SPEC] * n, out_specs=[HBM_SPEC] * n,
        out_shape=[jax.ShapeDtypeStruct(a.shape, a.dtype) for a in arrs],
        scratch_shapes=[pltpu.SemaphoreType.DMA((n, 7)), pltpu.SemaphoreType.DMA((n, 7)), pltpu.SemaphoreType.DMA((n,))],
    )(*arrs)


SEM_SPEC = pl.BlockSpec(memory_space=pltpu.SEMAPHORE)
DATAFLOW = pltpu.SideEffectType.DATAFLOW_SIDE_EFFECTING
FLIPS = [(dx, dy, dc) for dx in (0, 1) for dy in (0, 1) for dc in (0, 1)][1:]


def _peer(k):
    dx, dy, dc = FLIPS[k]
    peer = (lax.axis_index("x") ^ dx, lax.axis_index("y") ^ dy, lax.axis_index("c") ^ dc)
    return peer, 4 * peer[0] + 2 * peer[1] + peer[2]


def _scatter_start(chunks, name):
    def body(src_ref, land_ref, send_sems, recv_sems, src_thru, land_thru, token):
        me = _my_slot()
        for k in range(len(FLIPS)):
            peer, peer_slot = _peer(k)
            pltpu.make_async_remote_copy(
                src_ref=src_ref.at[peer_slot], dst_ref=land_ref.at[me], send_sem=send_sems.at[k], recv_sem=recv_sems.at[k],
                device_id=peer, device_id_type=MESH_IDS).start()
        token[...] = jnp.zeros(token.shape, F32)

    n = len(FLIPS)
    hbm = pltpu.HBM(chunks.shape, chunks.dtype)
    return pl.pallas_call(
        body, name=name,
        out_shape=(pltpu.SemaphoreType.DMA((n,)), pltpu.SemaphoreType.DMA((n,)), hbm, hbm,
                   jax.ShapeDtypeStruct((SUBLANES, LANES), F32)),
        in_specs=(HBM_SPEC, HBM_SPEC),
        out_specs=(SEM_SPEC, SEM_SPEC, HBM_SPEC, HBM_SPEC, pl.BlockSpec(memory_space=pltpu.VMEM)),
        input_output_aliases={0: 2, 1: 3},
        compiler_params=pltpu.CompilerParams(has_side_effects=DATAFLOW),
    )(pltpu.with_memory_space_constraint(chunks, pltpu.HBM),
      pltpu.with_memory_space_constraint(lax.empty(chunks.shape, chunks.dtype), pltpu.HBM))


def _scatter_wait(send_sems, recv_sems, src_thru, land_thru, after, name):
    def body(src_ref, land_ref, send_sems, recv_sems, after_ref, src_dead, got_ref):
        for k in range(len(FLIPS)):
            peer, peer_slot = _peer(k)
            cp = pltpu.make_async_remote_copy(
                src_ref=src_ref.at[peer_slot], dst_ref=land_ref.at[peer_slot], send_sem=send_sems.at[k],
                recv_sem=recv_sems.at[k], device_id=peer, device_id_type=MESH_IDS)
            cp.wait_send()
            cp.wait_recv()

    hbm = pltpu.HBM(src_thru.shape, src_thru.dtype)
    return pl.pallas_call(
        body, name=name, out_shape=(hbm, hbm),
        in_specs=(HBM_SPEC, HBM_SPEC, SEM_SPEC, SEM_SPEC, pl.BlockSpec(memory_space=pl.ANY)),
        out_specs=(HBM_SPEC, HBM_SPEC), input_output_aliases={0: 0, 1: 1},
        compiler_params=pltpu.CompilerParams(has_side_effects=DATAFLOW),
    )(src_thru, land_thru, send_sems, recv_sems, after)


def _sum_sources(parts, name):
    k, r, c = parts.shape
    tr = r if k * r * c <= 2 * 1024 * 1024 else _pick(r, (512, 256, 128, 64, 32, 16, 8))

    def body(p_ref, o_ref):
        acc = p_ref[0].astype(F32)
        for s in range(1, k):
            acc = acc + p_ref[s].astype(F32)
        o_ref[...] = acc

    return pl.pallas_call(
        body, name=name, grid=(r // tr,),
        in_specs=[pl.BlockSpec((k, tr, c), lambda i: (0, i, 0))],
        out_specs=pl.BlockSpec((tr, c), lambda i: (i, 0)),
        out_shape=jax.ShapeDtypeStruct((r, c), F32),
        compiler_params=_cparams(("arbitrary",)),
    )(parts)


def _adamw(parts, w, m, v, name, own=None):
    k, r, c = parts.shape
    tr = r if r * c <= 256 * 1024 else _pick(r, (256, 128, 64, 32, 16, 8))

    def body(*refs):
        p_ref, w_ref, m_ref, v_ref = refs[:4]
        g_ref, d_ref, nm_ref, nv_ref = refs[-4:]

        def part(s):
            if own is None:
                return p_ref[s].astype(F32)
            return jnp.where(_my_slot() == s, refs[4][...], p_ref[s]).astype(F32)

        g = part(0)
        for s in range(1, k):
            g = g + part(s)
        m_new = ADAM_B1 * m_ref[...] + (1.0 - ADAM_B1) * g
        v_new = ADAM_B2 * v_ref[...] + (1.0 - ADAM_B2) * jnp.square(g)
        m_hat = m_new / (1.0 - ADAM_B1 ** ADAM_STEP)
        v_hat = v_new / (1.0 - ADAM_B2 ** ADAM_STEP)
        g_ref[...] = g
        d_ref[...] = -ADAM_LR * (m_hat / (jnp.sqrt(v_hat) + ADAM_EPS) + ADAM_WD * w_ref[...])
        nm_ref[...] = m_new
        nv_ref[...] = v_new

    blk = pl.BlockSpec((tr, c), lambda i: (i, 0))
    return pl.pallas_call(
        body, name=name, grid=(r // tr,),
        in_specs=[pl.BlockSpec((k, tr, c), lambda i: (0, i, 0)), blk, blk, blk] + ([] if own is None else [blk]),
        out_specs=[blk] * 4,
        out_shape=[jax.ShapeDtypeStruct((r, c), F32)] * 4,
        compiler_params=_cparams(("arbitrary",)),
    )(parts, w, m, v, *([] if own is None else [own]))


def _silu(v):
    return v * _sigmoid(v)


def _ada_fwd(c_all, w, b):
    def body(c_ref, w_ref, b_ref, o_ref):
        ca = _silu(c_ref[...]).astype(BF16)
        o_ref[...] = jnp.dot(ca, w_ref[...].astype(BF16), preferred_element_type=F32) + b_ref[...]

    return pl.pallas_call(
        body, name="ada_fwd", out_shape=jax.ShapeDtypeStruct((c_all.shape[0], w.shape[1]), F32),
        compiler_params=_cparams(),
    )(c_all, w, b)


def _ada_bwd(c_all, dmod):
    def body(c_ref, d_ref, o_ref):
        ca = _silu(c_ref[...]).astype(BF16).astype(F32)
        dm = d_ref[...].astype(BF16).astype(F32)
        acc = jnp.zeros(o_ref.shape, F32)
        for bi in range(c_all.shape[0]):
            acc = acc + jnp.transpose(ca[bi:bi + 1, :]) * dm[bi:bi + 1, :]
        o_ref[...] = acc

    return pl.pallas_call(
        body, name="ada_bwd", out_shape=jax.ShapeDtypeStruct((c_all.shape[1], dmod.shape[1]), F32),
        compiler_params=_cparams(),
    )(c_all, dmod)


COL_SHARDED = ("w_in", "w_uq", "w_ukv", "w_up")
ROW_SHARDED = ("w_proj_rnn", "w_proj_mla", "w_out", "w_down")
REPLICATED = ("b_ada", "norm1_g", "conv_b", "w_gate_a", "b_gate_a", "w_gate_x", "b_gate_x", "lru_param", "q_norm_g",
              "kv_norm_g", "norm2_g", "ffn_conv_b", "final_g")
WEIGHTS = ("w_ada", "b_ada", "norm1_g", "w_in", "conv_w", "conv_b", "w_gate_a", "b_gate_a", "w_gate_x", "b_gate_x",
           "lru_param", "q_norm_g", "w_uq", "kv_norm_g", "w_ukv", "w_proj_rnn", "w_proj_mla", "w_out", "norm2_g", "w_up",
           "ffn_conv_w", "ffn_conv_b", "w_down", "final_g")
PACK_LANES = 128


def _pack(vecs):
    flat = jnp.concatenate([v.reshape(-1).astype(F32) for v in vecs])
    pad = (-flat.shape[0]) % (PACK_LANES * SUBLANES)
    return jnp.concatenate([flat, jnp.zeros((pad,), F32)]).reshape(-1, PACK_LANES)


def _unpack(packed, shapes):
    flat = packed.reshape(-1)
    out, off = [], 0
    for shp in shapes:
        size = math.prod(shp)
        out.append(flat[off:off + size].reshape(shp))
        off += size
    return out


def kernel(x, c, positions, w_ada, b_ada, norm1_g, w_in, conv_w, conv_b, w_gate_a, b_gate_a, w_gate_x, b_gate_x, lru_param, q_norm_g, w_uq, kv_norm_g, w_ukv, w_proj_rnn, w_proj_mla, w_out, norm2_g, w_up, ffn_conv_w, ffn_conv_b, w_down, final_g, loss_target, m_w_ada, m_b_ada, m_norm1_g, m_w_in, m_conv_w, m_conv_b, m_w_gate_a, m_b_gate_a, m_w_gate_x, m_b_gate_x, m_lru_param, m_q_norm_g, m_w_uq, m_kv_norm_g, m_w_ukv, m_w_proj_rnn, m_w_proj_mla, m_w_out, m_norm2_g, m_w_up, m_ffn_conv_w, m_ffn_conv_b, m_w_down, m_final_g, v_w_ada, v_b_ada, v_norm1_g, v_w_in, v_conv_w, v_conv_b, v_w_gate_a, v_b_gate_a, v_w_gate_x, v_b_gate_x, v_lru_param, v_q_norm_g, v_w_uq, v_kv_norm_g, v_w_ukv, v_w_proj_rnn, v_w_proj_mla, v_w_out, v_norm2_g, v_w_up, v_ffn_conv_w, v_ffn_conv_b, v_w_down, v_final_g):
    args = dict(locals())
    w = {n: args[n] for n in WEIGHTS}
    m = {n: args["m_" + n] for n in WEIGHTS}
    v = {n: args["v_" + n] for n in WEIGHTS}
    s, d = x.shape[1], x.shape[2]
    me = _my_slot()
    def two_d(a):
        assert a.ndim == 3 and a.shape[0] == 1, a.shape
        return a[0]

    big = COL_SHARDED + ROW_SHARDED
    gathered = _all_gather([two_d(w[n]).astype(BF16) for n in big] + [c, two_d(conv_w), two_d(ffn_conv_w)], "gather_weights")
    wts = {}
    for n, g in zip(big, gathered):
        k, r, cc = g.shape
        wts[n] = jnp.transpose(g, (1, 0, 2)).reshape(r, k * cc) if n in COL_SHARDED else g.reshape(k * r, cc)
    c_all = gathered[len(big)].reshape(N_DEV, d)
    conv_w_all = jnp.transpose(gathered[len(big) + 1], (1, 0, 2)).reshape(conv_w.shape[1], -1)
    ffn_conv_w_all = jnp.transpose(gathered[len(big) + 2], (1, 0, 2)).reshape(ffn_conv_w.shape[1], -1)

    ada_cols = w_ada.shape[2]
    b_cols = lax.dynamic_slice(b_ada, (0, me * ada_cols), (1, ada_cols))
    mod_cols = _ada_fwd(c_all, w_ada[0], b_cols)
    mod_all, = _all_gather([mod_cols], "gather_mod")
    mod = lax.dynamic_index_in_dim(mod_all, me, axis=1, keepdims=False).reshape(6, d)

    sm = {n: w[n][0] for n in REPLICATED if n not in ("b_ada", "final_g")}
    sm["final_g"] = final_g
    sm["conv_w"] = conv_w_all
    sm["ffn_conv_w"] = ffn_conv_w_all
    in_flight = {}

    def emit(n, g):
        if n in COL_SHARDED:
            r, cc = g.shape
            chunks = jnp.transpose(g.reshape(r, N_DEV, cc // N_DEV), (1, 0, 2))
        else:
            chunks = g.reshape(N_DEV, g.shape[0] // N_DEV, g.shape[1])
        *in_flight[n], token = _scatter_start(chunks, "scatter_start_" + n)
        return token[0, 0]

    sq, grad_x, gs, dmod = _local_step(x[0], mod, positions[0], loss_target[0], wts, sm, emit)

    small_names = [n for n in REPLICATED if n != "b_ada"] + ["conv_w", "ffn_conv_w"]
    small_shapes = [gs[n].shape for n in small_names] + [(6 * d,), (1,)]
    partial = _pack([gs[n] for n in small_names] + [dmod, sq.reshape(1)])
    partial_all, = _all_gather([partial], "gather_small")
    summed = _unpack(_sum_sources(partial_all, "sum_small"), small_shapes)
    g_small = dict(zip(small_names, summed[:len(small_names)]))
    g_small["b_ada"] = summed[len(small_names)]
    loss = 0.5 * summed[-1][0] / d
    n_before = sum(math.prod(t) for t in small_shapes[:len(small_names)])
    dmod_all = partial_all.reshape(N_DEV, -1)[:, n_before:n_before + 6 * d]
    dmod_cols = lax.dynamic_slice(dmod_all, (0, me * ada_cols), (N_DEV, ada_cols))

    grads, deltas, new_m, new_v = {}, {}, {}, {}

    def update(n, parts, own=None):
        shp = w[n].shape
        g, dl, nm, nv = _adamw(parts, two_d(w[n]), two_d(m[n]), two_d(v[n]), "adamw_" + n, own)
        grads[n], deltas[n], new_m[n], new_v[n] = g.reshape(shp), dl.reshape(shp), nm.reshape(shp), nv.reshape(shp)

    update("w_ada", _ada_bwd(c_all, dmod_cols)[None])

    for n in big:
        chunks, landed = _scatter_wait(*in_flight[n], grad_x, "scatter_wait_" + n)
        update(n, landed, lax.dynamic_index_in_dim(chunks, me, axis=0, keepdims=False))

    for n in ("conv_w", "ffn_conv_w"):
        cols = w[n].shape[2]
        update(n, lax.dynamic_slice(g_small[n], (0, me * cols), (g_small[n].shape[0], cols))[None])

    rep_shapes = [w[n].shape for n in REPLICATED]
    g_rep, d_rep, m_rep, v_rep = _adamw(
        _pack([g_small[n] for n in REPLICATED])[None], _pack([w[n] for n in REPLICATED]), _pack([m[n] for n in REPLICATED]),
        _pack([v[n] for n in REPLICATED]), "adamw_replicated")
    for dst, packed in ((grads, g_rep), (deltas, d_rep), (new_m, m_rep), (new_v, v_rep)):
        dst.update(zip(REPLICATED, _unpack(packed, rep_shapes)))

    return (loss, grad_x[None], *[grads[n] for n in WEIGHTS], *[deltas[n] for n in WEIGHTS],
            *[new_m[n] for n in WEIGHTS], *[new_v[n] for n in WEIGHTS])
```

```python
import functools
import math

import jax
import jax.numpy as jnp
from jax import lax
from jax.experimental import pallas as pl
from jax.experimental.pallas import tpu as pltpu

F32 = jnp.float32
BF16 = jnp.bfloat16

N_DEV = 8
LANES = 128
SUBLANES = 8
VMEM_LIMIT = 56 * 1024 * 1024

D_RNN = 1280
Q_LORA = 384
KV_LORA = 256
QK_NOPE = 64
QK_ROPE = 32
V_HEAD = 64
N_HEADS = 16
D_FF = 2816
ROPE_THETA = 10000.0
LRU_C = 8.0
EPS = 1e-6
MLA_W = 768
ATT_SCALE = 1.0 / math.sqrt(QK_NOPE + QK_ROPE)

ADAM_LR, ADAM_B1, ADAM_B2, ADAM_EPS, ADAM_WD, ADAM_STEP = 0.001, 0.9, 0.999, 1e-08, 0.01, 10


def _cparams(sem=None):
    return pltpu.CompilerParams(dimension_semantics=sem, vmem_limit_bytes=VMEM_LIMIT)


def _pick(n, prefs):
    for p in prefs:
        if n % p == 0:
            return p
    return n


def _sigmoid(v):
    return 1.0 / (1.0 + jnp.exp(-v))


def _lane(shape):
    return lax.broadcasted_iota(jnp.int32, shape, len(shape) - 1)


def _row(shape):
    return lax.broadcasted_iota(jnp.int32, shape, len(shape) - 2)


MM_BLOCK_BYTES = 36 * 1024 * 1024


def _divisors(n):
    return [t for t in range(n, 0, -LANES) if n % t == 0] if n % LANES == 0 else [n]


def _mm_tiles(m, n, k, a_bytes, b_bytes, o_bytes):
    tm = _pick(m, (512, 384, 256, 128))
    for tk in _divisors(k):
        for tn in _divisors(n):
            need = 2 * (tm * tk * a_bytes + tk * tn * b_bytes + tm * tn * o_bytes) + (tm * tn * 4 if tk < k else 0)
            if tn <= 2048 and need <= MM_BLOCK_BYTES:
                return tm, tn, tk
    raise ValueError((m, n, k))


def _mm(a, b, *, ta=False, tb=False, out_dtype=F32, name):
    (k_a, m) = a.shape if ta else a.shape[::-1]
    (n, k_b) = b.shape if tb else b.shape[::-1]
    assert k_a == k_b, (a.shape, b.shape, ta, tb)
    k = k_a
    tm, tn, tk = _mm_tiles(m, n, k, a.dtype.itemsize, b.dtype.itemsize, jnp.dtype(out_dtype).itemsize)
    nk = k // tk
    dims = (((0 if ta else 1,), (1 if tb else 0,)), ((), ()))

    def body(a_ref, b_ref, o_ref, *acc):
        part = lax.dot_general(a_ref[...].astype(BF16), b_ref[...].astype(BF16), dims, preferred_element_type=F32)
        if nk == 1:
            o_ref[...] = part.astype(out_dtype)
            return
        acc_ref, = acc
        kk = pl.program_id(2)

        @pl.when(kk == 0)
        def _():
            acc_ref[...] = part

        @pl.when(kk > 0)
        def _():
            acc_ref[...] += part

        @pl.when(kk == nk - 1)
        def _():
            o_ref[...] = acc_ref[...].astype(out_dtype)

    a_spec = pl.BlockSpec((tk, tm), lambda i, j, kk: (kk, i)) if ta else pl.BlockSpec((tm, tk), lambda i, j, kk: (i, kk))
    b_spec = pl.BlockSpec((tn, tk), lambda i, j, kk: (j, kk)) if tb else pl.BlockSpec((tk, tn), lambda i, j, kk: (kk, j))
    return pl.pallas_call(
        body, name=name,
        grid=(m // tm, n // tn, nk),
        in_specs=[a_spec, b_spec],
        out_specs=pl.BlockSpec((tm, tn), lambda i, j, kk: (i, j)),
        out_shape=jax.ShapeDtypeStruct((m, n), out_dtype),
        scratch_shapes=[] if nk == 1 else [pltpu.VMEM((tm, tn), F32)],
        compiler_params=_cparams(("arbitrary", "arbitrary", "arbitrary")),
    )(a, b)


def _rowwise(fn, row_ins, par_ins, out_defs, red_defs, *, name, tr=256):
    s = row_ins[0].shape[0]
    tr = min(tr, s)
    nr, npar, no = len(row_ins), len(par_ins), len(out_defs)

    def body(*refs):
        rin, pin = refs[:nr], refs[nr:nr + npar]
        outs, reds = refs[nr + npar:nr + npar + no], refs[nr + npar + no:]
        i = pl.program_id(0)

        @pl.when(i == 0)
        def _():
            for r in reds:
                r[...] = jnp.zeros_like(r)

        fn(i, rin, pin, outs, reds)

    in_specs = [pl.BlockSpec((tr, a.shape[1]), lambda i: (i, 0)) for a in row_ins]
    in_specs += [pl.BlockSpec(a.shape, lambda i, nd=a.ndim: (0,) * nd) for a in par_ins]
    out_specs = [pl.BlockSpec((tr, c), lambda i: (i, 0)) for c, _ in out_defs]
    out_specs += [pl.BlockSpec(shp, lambda i: (0, 0)) for shp in red_defs]
    out_shape = [jax.ShapeDtypeStruct((s, c), dt) for c, dt in out_defs]
    out_shape += [jax.ShapeDtypeStruct(shp, F32) for shp in red_defs]
    return pl.pallas_call(
        body, name=name, grid=(s // tr,), in_specs=in_specs, out_specs=out_specs, out_shape=out_shape,
        compiler_params=_cparams(("arbitrary",)),
    )(*row_ins, *par_ins)


def _rms(v):
    return lax.rsqrt(jnp.mean(v * v, axis=-1, keepdims=True) + EPS)


def _colsum(v):
    return jnp.sum(v, axis=0, keepdims=True)


def _rms_bwd(dn, n, rstd):
    return rstd * (dn - n * jnp.mean(dn * n, axis=-1, keepdims=True))


def _norm_mod_fwd(x, gmod, name):
    def fn(i, rin, pin, outs, reds):
        xv = rin[0][...]
        p = pin[0][...]
        n = xv * _rms(xv)
        outs[0][...] = ((n * p[0:1]) * (1.0 + p[1:2]) + p[2:3]).astype(BF16)

    return _rowwise(fn, [x], [gmod], [(x.shape[1], BF16)], [], name=name)[0]


def _rope(v, rot_c, rot_s):
    half = QK_ROPE // 2
    swapped = jnp.where(_lane(v.shape) < QK_NOPE + half, pltpu.roll(v, LANES - half, 1), pltpu.roll(v, half, 1))
    return v * rot_c + swapped * rot_s


def _rope_t(dv, rot_c, rot_s):
    half = QK_ROPE // 2
    ds = dv * rot_s
    lane = _lane(dv.shape)
    swapped = jnp.where(lane < QK_NOPE + half, pltpu.roll(ds, LANES - half, 1), pltpu.roll(ds, half, 1))
    in_rope = (lane >= QK_NOPE) & (lane < QK_NOPE + QK_ROPE)
    return dv * rot_c + jnp.where(in_rope, swapped, 0.0)


def _mla_prep_fwd(proj_mla, rot_c, rot_s, ng):
    o1, o2 = Q_LORA, Q_LORA + KV_LORA

    def fn(i, rin, pin, outs, reds):
        g = pin[0][...]
        ql = rin[0][:, 0:o1]
        kl = rin[0][:, o1:o2]
        outs[0][...] = (ql * _rms(ql) * g[0:1, 0:o1]).astype(BF16)
        outs[1][...] = (kl * _rms(kl) * g[0:1, o1:o2]).astype(BF16)
        kr = pltpu.roll(rin[0][:, o2:o2 + LANES], QK_NOPE, 1)
        outs[2][...] = _rope(kr, rin[1][...], rin[2][...]).astype(BF16)

    return _rowwise(fn, [proj_mla, rot_c, rot_s], [ng], [(Q_LORA, BF16), (KV_LORA, BF16), (LANES, BF16)], [],
                    name="mla_prep_fwd")


def _mla_prep_bwd(proj_mla, dqn, dkvn, dkr, rot_c, rot_s, ng):
    o1, o2 = Q_LORA, Q_LORA + KV_LORA

    def fn(i, rin, pin, outs, reds):
        g = pin[0][...]
        ql = rin[0][:, 0:o1]
        kl = rin[0][:, o1:o2]
        rq, rk = _rms(ql), _rms(kl)
        nq, nk = ql * rq, kl * rk
        dq, dk = rin[1][...], rin[2][...]
        outs[0][:, 0:o1] = _rms_bwd(dq * g[0:1, 0:o1], nq, rq).astype(BF16)
        outs[0][:, o1:o2] = _rms_bwd(dk * g[0:1, o1:o2], nk, rk).astype(BF16)
        dkr_pre = pltpu.roll(_rope_t(rin[3][...], rin[4][...], rin[5][...]), LANES - QK_NOPE, 1)
        outs[0][:, o2:] = jnp.where(_lane(dkr_pre.shape) < QK_ROPE, dkr_pre, 0.0).astype(BF16)
        reds[0][0:1, 0:o1] += _colsum(dq * nq)
        reds[0][0:1, o1:o2] += _colsum(dk * nk)

    return _rowwise(fn, [proj_mla, dqn, dkvn, dkr, rot_c, rot_s], [ng], [(MLA_W, BF16)], [(SUBLANES, MLA_W)],
                    name="mla_prep_bwd")


def _rope_heads(q, rot_c, rot_s, transpose, name):
    def fn(i, rin, pin, outs, reds):
        c, sn = rin[1][...] * Q_PRESCALE, rin[2][...] * Q_PRESCALE
        for h in range(N_HEADS):
            sl = slice(h * LANES, (h + 1) * LANES)
            v = rin[0][:, sl]
            outs[0][:, sl] = (_rope_t(v, c, sn) if transpose else _rope(v, c, sn)).astype(BF16)

    return _rowwise(fn, [q, rot_c, rot_s], [], [(q.shape[1], BF16)], [], name=name)[0]


def _merge_fwd(pr, pm, proj_g):
    d = pr.shape[1]

    def fn(i, rin, pin, outs, reds):
        outs[0][...] = (_sigmoid(rin[2][:, 0:d]) * rin[0][...] + _sigmoid(rin[2][:, d:]) * rin[1][...]).astype(BF16)

    return _rowwise(fn, [pr, pm, proj_g], [], [(d, BF16)], [], name="merge_fwd")[0]


def _merge_bwd(dmerged, pr, pm, proj_g):
    d = pr.shape[1]

    def fn(i, rin, pin, outs, reds):
        dm = rin[0][...]
        sr, sm = _sigmoid(rin[3][:, 0:d]), _sigmoid(rin[3][:, d:])
        outs[0][...] = (dm * sr).astype(BF16)
        outs[1][...] = (dm * sm).astype(BF16)
        outs[2][:, 0:d] = (dm * rin[1][...] * sr * (1.0 - sr)).astype(BF16)
        outs[2][:, d:] = (dm * rin[2][...] * sm * (1.0 - sm)).astype(BF16)

    return _rowwise(fn, [dmerged, pr, pm, proj_g], [], [(d, BF16), (d, BF16), (2 * d, BF16)], [], name="merge_bwd")


def _resid_norm_fwd(x, o, gmod):
    d = x.shape[1]

    def fn(i, rin, pin, outs, reds):
        p = pin[0][...]
        x1 = rin[0][...] + p[3:4] * rin[1][...]
        outs[0][...] = x1
        outs[1][...] = ((x1 * _rms(x1) * p[0:1]) * (1.0 + p[1:2]) + p[2:3]).astype(BF16)

    return _rowwise(fn, [x, o], [gmod], [(d, F32), (d, BF16)], [], name="resid_norm_fwd")


def _final_fwd_bwd(x1, dn, target, par):
    d = x1.shape[1]

    def fn(i, rin, pin, outs, reds):
        p = pin[0][...]
        dnv = rin[1][...]
        x2 = rin[0][...] + p[0:1] * dnv
        rstd = _rms(x2)
        n3 = x2 * rstd
        err = n3 * p[1:2] - rin[2][...]
        dy = err * (1.0 / d)
        dx2 = _rms_bwd(dy * p[1:2], n3, rstd)
        outs[0][...] = dx2
        outs[1][...] = (dx2 * p[0:1]).astype(BF16)
        reds[0][0:1, :] += _colsum(dy * n3)
        reds[0][1:2, :] += _colsum(dx2 * dnv)
        reds[0][2:3, :] += jnp.zeros((1, d), F32) + jnp.sum(err * err)

    return _rowwise(fn, [x1, dn, target], [par], [(d, F32), (d, BF16)], [(SUBLANES, d)], name="final_fwd_bwd")


def _norm2_bwd(x1, dh2, dx2, o, gmod):
    d = x1.shape[1]

    def fn(i, rin, pin, outs, reds):
        p = pin[0][...]
        x1v, dh = rin[0][...], rin[1][...]
        rstd = _rms(x1v)
        n2 = x1v * rstd
        dx1 = rin[2][...] + _rms_bwd(dh * (p[0:1] * (1.0 + p[1:2])), n2, rstd)
        outs[0][...] = dx1
        outs[1][...] = (dx1 * p[3:4]).astype(BF16)
        reds[0][0:1, :] += _colsum(dh * n2 * (1.0 + p[1:2]))
        reds[0][1:2, :] += _colsum(dh * n2 * p[0:1])
        reds[0][2:3, :] += _colsum(dh)
        reds[0][3:4, :] += _colsum(dx1 * rin[3][...])

    return _rowwise(fn, [x1, dh2, dx2, o], [gmod], [(d, F32), (d, BF16)], [(SUBLANES, d)], name="norm2_bwd")


def _norm1_bwd(x, dh_a, dh_b, dh_c, dx1, gmod):
    d = x.shape[1]

    def fn(i, rin, pin, outs, reds):
        p = pin[0][...]
        xv = rin[0][...]
        dh = rin[1][...] + rin[2][...] + rin[3][...]
        rstd = _rms(xv)
        n1 = xv * rstd
        outs[0][...] = rin[4][...] + _rms_bwd(dh * (p[0:1] * (1.0 + p[1:2])), n1, rstd)
        reds[0][0:1, :] += _colsum(dh * n1 * (1.0 + p[1:2]))
        reds[0][1:2, :] += _colsum(dh * n1 * p[0:1])
        reds[0][2:3, :] += _colsum(dh)

    return _rowwise(fn, [x, dh_a, dh_b, dh_c, dx1], [gmod], [(d, F32)], [(SUBLANES, d)], name="norm1_bwd")


RNN_CHUNK = 512


def _shift_down(ref, base, n, j):
    v = ref[pl.ds(base, n + SUBLANES), :]
    return v[SUBLANES:] if j == 0 else pltpu.roll(v, j, 0)[SUBLANES:]


def _shift_up(ref, base, n, j, top_pad):
    v = ref[pl.ds(base + top_pad, n + SUBLANES), :]
    return v[:n] if j == 0 else pltpu.roll(v, n + SUBLANES - j, 0)[:n]


def _one_minus_exp(z):
    series = -z * (1.0 + z * (0.5 + z * (1.0 / 6.0 + z * (1.0 / 24.0 + z * (1.0 / 120.0 + z * (1.0 / 720.0))))))
    return jnp.where(z > -0.1, series, 1.0 - jnp.exp(z))


def _softplus(v):
    return jnp.maximum(v, 0.0) + jnp.log(1.0 + jnp.exp(-jnp.abs(v)))


def _rnn_gates(xc, w, wa, wx, sp):
    xb = xc.astype(BF16)
    ra = _sigmoid(jnp.dot(xb, wa, preferred_element_type=F32) + w[5:6])
    ix = _sigmoid(jnp.dot(xb, wx, preferred_element_type=F32) + w[6:7])
    la = (-LRU_C) * ra * sp
    a = jnp.exp(la)
    mult = jnp.sqrt(_one_minus_exp(2.0 * la))
    return ra, ix, a, mult


def _rnn_fwd(x_rnn, keep, rp, wa_bd, wx_bd):
    s, r = x_rnn.shape
    ts = min(RNN_CHUNK, s)

    def body(x_ref, keep_ref, rp_ref, wa_ref, wx_ref, xc_ref, ra_ref, ix_ref, hs_ref, xpad, a_s, b_s):
        xpad[0:SUBLANES, :] = jnp.zeros((SUBLANES, LANES), F32)
        xpad[SUBLANES:, :] = x_ref[...]
        w = rp_ref[...]
        sp = _softplus(-w[7:8])
        wa, wx = wa_ref[0], wx_ref[0]

        def chunk(c, carry):
            base = pl.multiple_of(c * ts, ts)
            xc = w[4:5] + w[3:4] * _shift_down(xpad, base, ts, 0)
            for j in range(1, 4):
                xc = xc + w[3 - j:4 - j] * _shift_down(xpad, base, ts, j)
            ra, ix, a, mult = _rnn_gates(xc, w, wa, wx, sp)
            kp = keep_ref[pl.ds(base, ts), :]
            xc_ref[pl.ds(base, ts), :] = xc
            ra_ref[pl.ds(base, ts), :] = ra
            ix_ref[pl.ds(base, ts), :] = ix
            a_s[pl.ds(base, ts), :] = a * kp
            b_s[pl.ds(base, ts), :] = jnp.where(kp > 0.0, mult, 1.0) * (ix * xc)
            return carry

        lax.fori_loop(0, s // ts, chunk, 0)

        row = _row((SUBLANES, LANES))

        def tile(i, h):
            r0 = pl.multiple_of(i * SUBLANES, SUBLANES)
            a = a_s[pl.ds(r0, SUBLANES), :]
            b = b_s[pl.ds(r0, SUBLANES), :]
            for sh in (1, 2, 4):
                a_sh = jnp.where(row >= sh, pltpu.roll(a, sh, 0), 1.0)
                b_sh = jnp.where(row >= sh, pltpu.roll(b, sh, 0), 0.0)
                b = a * b_sh + b
                a = a * a_sh
            hv = b + a * h
            hs_ref[pl.ds(r0, SUBLANES), :] = hv
            return hv[SUBLANES - 1:SUBLANES, :]

        lax.fori_loop(0, s // SUBLANES, tile, jnp.zeros((1, LANES), F32), unroll=4)

    col = pl.BlockSpec((s, LANES), lambda g: (0, g))
    return pl.pallas_call(
        body, name="rnn_fwd", grid=(r // LANES,),
        in_specs=[col, pl.BlockSpec((s, 1), lambda g: (0, 0)), pl.BlockSpec((SUBLANES, LANES), lambda g: (0, g)),
                  pl.BlockSpec((1, LANES, LANES), lambda g: (g, 0, 0)), pl.BlockSpec((1, LANES, LANES), lambda g: (g, 0, 0))],
        out_specs=[col] * 4,
        out_shape=[jax.ShapeDtypeStruct((s, r), F32)] * 4,
        scratch_shapes=[pltpu.VMEM((s + SUBLANES, LANES), F32), pltpu.VMEM((s, LANES), F32), pltpu.VMEM((s, LANES), F32)],
        compiler_params=_cparams(("arbitrary",)),
    )(x_rnn, keep, rp, wa_bd, wx_bd)


def _rnn_bwd(x_rnn, xc, ra, ix, hs, dy, keep, rp, wa_bd, wx_bd):
    s, r = x_rnn.shape
    ts = min(RNN_CHUNK, s)
    nt = s // SUBLANES

    def body(x_ref, xc_ref, ra_ref, ix_ref, hs_ref, dy_ref, keep_ref, rp_ref, wa_ref, wx_ref,
             dx_ref, dwa_ref, dwx_ref, red_ref, xpad, hpad, a_s, dh_s, dxc_s):
        zero8 = jnp.zeros((SUBLANES, LANES), F32)
        xpad[0:SUBLANES, :] = zero8
        xpad[SUBLANES:, :] = x_ref[...]
        hpad[0:SUBLANES, :] = zero8
        hpad[SUBLANES:, :] = hs_ref[...]
        a_s[s:, :] = zero8
        dxc_s[s:, :] = zero8
        w = rp_ref[...]
        sp = _softplus(-w[7:8])
        wa, wx = wa_ref[0], wx_ref[0]

        def decay(c, carry):
            base = pl.multiple_of(c * ts, ts)
            a = jnp.exp((-LRU_C) * ra_ref[pl.ds(base, ts), :] * sp)
            a_s[pl.ds(base, ts), :] = a * keep_ref[pl.ds(base, ts), :]
            return carry

        lax.fori_loop(0, s // ts, decay, 0)

        row = _row((SUBLANES, LANES))

        def tile(n, nxt):
            i = nt - 1 - n
            r0 = pl.multiple_of(i * SUBLANES, SUBLANES)
            a_here = a_s[pl.ds(r0, SUBLANES), :]
            a_next = a_s[pl.ds(r0 + SUBLANES, SUBLANES), :]
            a = jnp.where(row < SUBLANES - 1, pltpu.roll(a_here, SUBLANES - 1, 0), a_next[0:1, :])
            b = dy_ref[pl.ds(r0, SUBLANES), :]
            for sh in (1, 2, 4):
                a_sh = jnp.where(row < SUBLANES - sh, pltpu.roll(a, SUBLANES - sh, 0), 1.0)
                b_sh = jnp.where(row < SUBLANES - sh, pltpu.roll(b, SUBLANES - sh, 0), 0.0)
                b = a * b_sh + b
                a = a * a_sh
            dh = b + a * nxt
            dh_s[pl.ds(r0, SUBLANES), :] = dh
            return dh[0:1, :]

        lax.fori_loop(0, nt, tile, jnp.zeros((1, LANES), F32), unroll=4)

        def gates(c, carry):
            dwa, dwx, d_ba, d_bx, d_sp, d_cb = carry
            base = pl.multiple_of(c * ts, ts)
            xcv = xc_ref[pl.ds(base, ts), :]
            rav = ra_ref[pl.ds(base, ts), :]
            ixv = ix_ref[pl.ds(base, ts), :]
            kp = keep_ref[pl.ds(base, ts), :]
            dh = dh_s[pl.ds(base, ts), :]
            h_prev = _shift_down(hpad, base, ts, 1)
            la = (-LRU_C) * rav * sp
            a = jnp.exp(la)
            mult = jnp.sqrt(_one_minus_exp(2.0 * la))
            mult_eff = jnp.where(kp > 0.0, mult, 1.0)
            d_a = dh * h_prev * kp
            d_mult = dh * (ixv * xcv) * kp
            d_ix = dh * mult_eff * xcv
            d_xc = dh * mult_eff * ixv
            d_la = d_a * a - d_mult * (a * a) / mult
            d_pa = d_la * ((-LRU_C) * sp) * rav * (1.0 - rav)
            d_px = d_ix * ixv * (1.0 - ixv)
            xb = xcv.astype(BF16)
            pab, pxb = d_pa.astype(BF16), d_px.astype(BF16)
            tn = (((0,), (0,)), ((), ()))
            nt_ = (((1,), (1,)), ((), ()))
            dwa = dwa + lax.dot_general(xb, pab, tn, preferred_element_type=F32)
            dwx = dwx + lax.dot_general(xb, pxb, tn, preferred_element_type=F32)
            d_xc = d_xc + lax.dot_general(pab, wa, nt_, preferred_element_type=F32)
            d_xc = d_xc + lax.dot_general(pxb, wx, nt_, preferred_element_type=F32)
            dxc_s[pl.ds(base, ts), :] = d_xc
            return (dwa, dwx, d_ba + _colsum(d_pa), d_bx + _colsum(d_px),
                    d_sp + _colsum(d_la * ((-LRU_C) * rav)), d_cb + _colsum(d_xc))

        z1 = jnp.zeros((1, LANES), F32)
        zw = jnp.zeros((LANES, LANES), F32)
        dwa, dwx, d_ba, d_bx, d_sp, d_cb = lax.fori_loop(0, s // ts, gates, (zw, zw, z1, z1, z1, z1))
        dwa_ref[0] = dwa
        dwx_ref[0] = dwx

        def conv(c, carry):
            base = pl.multiple_of(c * ts, ts)
            d_here = dxc_s[pl.ds(base, ts), :]
            dx = w[3:4] * d_here
            for j in range(1, 4):
                dx = dx + w[3 - j:4 - j] * _shift_up(dxc_s, base, ts, j, 0)
            dx_ref[pl.ds(base, ts), :] = dx.astype(BF16)
            return tuple(carry[k] + _colsum(d_here * _shift_down(xpad, base, ts, 3 - k)) for k in range(4))

        d_w = lax.fori_loop(0, s // ts, conv, (z1, z1, z1, z1))
        d_lru = d_sp * (-_sigmoid(-w[7:8]))
        red_ref[...] = jnp.concatenate(list(d_w) + [d_cb, d_ba, d_bx, d_lru], axis=0)

    col = pl.BlockSpec((s, LANES), lambda g: (0, g))
    par = pl.BlockSpec((SUBLANES, LANES), lambda g: (0, g))
    wsp = pl.BlockSpec((1, LANES, LANES), lambda g: (g, 0, 0))
    return pl.pallas_call(
        body, name="rnn_bwd", grid=(r // LANES,),
        in_specs=[col] * 6 + [pl.BlockSpec((s, 1), lambda g: (0, 0)), par, wsp, wsp],
        out_specs=[col, wsp, wsp, par],
        out_shape=[jax.ShapeDtypeStruct((s, r), BF16), jax.ShapeDtypeStruct((r // LANES, LANES, LANES), F32),
                   jax.ShapeDtypeStruct((r // LANES, LANES, LANES), F32), jax.ShapeDtypeStruct((SUBLANES, r), F32)],
        scratch_shapes=[pltpu.VMEM((s + SUBLANES, LANES), F32), pltpu.VMEM((s + SUBLANES, LANES), F32),
                        pltpu.VMEM((s + SUBLANES, LANES), F32), pltpu.VMEM((s, LANES), F32),
                        pltpu.VMEM((s + SUBLANES, LANES), F32)],
        compiler_params=_cparams(("arbitrary",)),
    )(x_rnn, xc, ra, ix, hs, dy, keep, rp, wa_bd, wx_bd)


ATT_BLOCK = 512
NT_DIMS = (((1,), (1,)), ((), ()))
TN_DIMS = (((0,), (0,)), ((), ()))


LOG2E = 1.4426950408889634
LN2 = 0.6931471805599453
Q_PRESCALE = ATT_SCALE * LOG2E


def _att_scores(q, kvb, krb, diagonal):
    k_eff = jnp.where(_lane(kvb.shape) < QK_NOPE, kvb, krb)
    sc = lax.dot_general(q, k_eff, NT_DIMS, preferred_element_type=F32)
    if diagonal:
        sc = jnp.where(lax.broadcasted_iota(jnp.int32, sc.shape, 1) <= lax.broadcasted_iota(jnp.int32, sc.shape, 0), sc, -jnp.inf)
    return sc, k_eff


def _att_fwd(q, kv, kr):
    s = q.shape[0]
    t = min(ATT_BLOCK, s)
    nb = s // t

    def body(q_ref, kv_ref, kr_ref, y_ref, lse_ref, m_s, acc_s):
        i, j = pl.program_id(1), pl.program_id(2)

        @pl.when(j == 0)
        def _():
            m_s[...] = jnp.full(m_s.shape, -jnp.inf, F32)
            acc_s[...] = jnp.zeros(acc_s.shape, F32)

        def step(diagonal):
            krb = kr_ref[...]
            lane = _lane((t, LANES))
            for hh in range(2):
                sl = slice(hh * LANES, (hh + 1) * LANES)
                kvb = kv_ref[:, sl]
                sc, _ = _att_scores(q_ref[:, sl], kvb, krb, diagonal)
                m_prev = m_s[hh]
                m_new = jnp.maximum(m_prev, jnp.max(sc, axis=-1, keepdims=True))
                alpha = jnp.exp2(m_prev - m_new)
                p = jnp.exp2(sc - m_new[:, 0:1]).astype(BF16)
                ones_v = jnp.where(lane < QK_NOPE, jnp.ones_like(kvb), kvb)
                acc_s[hh] = alpha * acc_s[hh] + jnp.dot(p, ones_v, preferred_element_type=F32)
                m_s[hh] = m_new

        @pl.when(j < i)
        def _():
            step(False)

        @pl.when(j == i)
        def _():
            step(True)
            lane = _lane((t, LANES))
            a0, a1 = acc_s[0], acc_s[1]
            l0, l1 = a0[:, 0:1], a1[:, 0:1]
            y_ref[...] = jnp.where(lane < V_HEAD, pltpu.roll(a0 / l0, V_HEAD, 1), a1 / l1).astype(BF16)
            lse_ref[...] = jnp.where(lane < V_HEAD, m_s[0] + jnp.log(l0) * LOG2E, m_s[1] + jnp.log(l1) * LOG2E)

    return pl.pallas_call(
        body, name="att_fwd", grid=(N_HEADS // 2, nb, nb),
        in_specs=[pl.BlockSpec((t, 2 * LANES), lambda p, i, j: (i, p)),
                  pl.BlockSpec((t, 2 * LANES), lambda p, i, j: (jnp.minimum(j, i), p)),
                  pl.BlockSpec((t, LANES), lambda p, i, j: (jnp.minimum(j, i), 0))],
        out_specs=[pl.BlockSpec((t, LANES), lambda p, i, j: (i, p))] * 2,
        out_shape=[jax.ShapeDtypeStruct((s, N_HEADS * V_HEAD), BF16), jax.ShapeDtypeStruct((s, N_HEADS * V_HEAD), F32)],
        scratch_shapes=[pltpu.VMEM((2, t, LANES), F32)] * 2,
        compiler_params=_cparams(("arbitrary", "arbitrary", "arbitrary")),
    )(q, kv, kr)


def _att_bwd(q, kv, kr, y, lse, dy):
    s = q.shape[0]
    t = min(ATT_BLOCK, s)
    nb = s // t

    def body(q_ref, kv_ref, kr_ref, y_ref, lse_ref, dy_ref, dq_ref, dkv_ref, dkr_ref, dkv_s):
        p_, j, i = pl.program_id(0), pl.program_id(1), pl.program_id(2)

        @pl.when((p_ == 0) & (j == 0) & (i == 0))
        def _():
            dkr_ref[...] = jnp.zeros(dkr_ref.shape, F32)

        @pl.when((j == 0) & (i == 0))
        def _():
            dq_ref[...] = jnp.zeros(dq_ref.shape, F32)

        @pl.when(i == 0)
        def _():
            dkv_s[...] = jnp.zeros(dkv_s.shape, F32)

        def step(diagonal):
            lane = _lane((t, LANES))
            krb = kr_ref[...]
            dyv = dy_ref[...]
            yv = y_ref[...].astype(F32)
            lsev = lse_ref[...]
            rows = pl.ds(pl.multiple_of(i * t, t), t)
            dkr_acc = jnp.zeros((t, LANES), F32)
            for hh in range(2):
                sl = slice(hh * LANES, (hh + 1) * LANES)
                qb, kvb = q_ref[:, sl], kv_ref[:, sl]
                sc, k_eff = _att_scores(qb, kvb, krb, diagonal)
                if hh == 0:
                    do_pad = jnp.where(lane >= V_HEAD, pltpu.roll(dyv, V_HEAD, 1), 0.0)
                    o_pad = jnp.where(lane >= V_HEAD, pltpu.roll(yv, V_HEAD, 1), 0.0)
                    lse_h = lsev[:, 0:1]
                else:
                    do_pad = jnp.where(lane >= V_HEAD, dyv, 0.0)
                    o_pad = jnp.where(lane >= V_HEAD, yv, 0.0)
                    lse_h = lsev[:, LANES - 1:LANES]
                do_ln2 = do_pad * LN2
                delta = jnp.sum(do_ln2 * o_pad, axis=-1, keepdims=True)
                prob = jnp.exp2(sc - lse_h)
                dv = lax.dot_general(prob.astype(BF16), do_pad.astype(BF16), TN_DIMS, preferred_element_type=F32)
                dp = lax.dot_general(do_ln2.astype(BF16), kvb, NT_DIMS, preferred_element_type=F32)
                ds = (prob * (dp - delta)).astype(BF16)
                dq_ref[rows, sl] += jnp.dot(ds, k_eff, preferred_element_type=F32)
                dk_eff = lax.dot_general(ds, qb, TN_DIMS, preferred_element_type=F32)
                dkv_s[hh] += dv + jnp.where(lane < QK_NOPE, dk_eff, 0.0)
                dkr_acc = dkr_acc + jnp.where(lane >= QK_NOPE, dk_eff, 0.0)
            dkr_ref[pl.ds(pl.multiple_of(j * t, t), t), :] += dkr_acc

        @pl.when(i > j)
        def _():
            step(False)

        @pl.when(i == j)
        def _():
            step(True)

        @pl.when(i == nb - 1)
        def _():
            dkv_ref[:, 0:LANES] = dkv_s[0].astype(BF16)
            dkv_ref[:, LANES:] = dkv_s[1].astype(BF16)

    qi = lambda p, j, i: (jnp.maximum(i, j), p)
    return pl.pallas_call(
        body, name="att_bwd", grid=(N_HEADS // 2, nb, nb),
        in_specs=[pl.BlockSpec((t, 2 * LANES), qi),
                  pl.BlockSpec((t, 2 * LANES), lambda p, j, i: (j, p)),
                  pl.BlockSpec((t, LANES), lambda p, j, i: (j, 0)),
                  pl.BlockSpec((t, LANES), qi), pl.BlockSpec((t, LANES), qi), pl.BlockSpec((t, LANES), qi)],
        out_specs=[pl.BlockSpec((s, 2 * LANES), lambda p, j, i: (0, p)),
                   pl.BlockSpec((t, 2 * LANES), lambda p, j, i: (j, p)),
                   pl.BlockSpec((s, LANES), lambda p, j, i: (0, 0))],
        out_shape=[jax.ShapeDtypeStruct((s, N_HEADS * LANES), F32), jax.ShapeDtypeStruct((s, N_HEADS * LANES), BF16),
                   jax.ShapeDtypeStruct((s, LANES), F32)],
        scratch_shapes=[pltpu.VMEM((2, t, LANES), F32)],
        compiler_params=_cparams(("arbitrary", "arbitrary", "arbitrary")),
    )(q, kv, kr, y, lse, dy)


FFN_COLS = 256


def _ffn_conv(pad_ref, w, base, n):
    u = w[3:4] + w[2:3] * _shift_down(pad_ref, base, n, 0)
    for j in range(1, 3):
        u = u + w[2 - j:3 - j] * _shift_down(pad_ref, base, n, j)
    return u


def _ffn_act_fwd(up, fp):
    s, f2 = up.shape
    f = f2 // 2
    tc = FFN_COLS
    ts = min(RNN_CHUNK, s)
    nfb = f // tc

    def body(ug_ref, uv_ref, wg_ref, wv_ref, act_ref, gpad, vpad):
        zero8 = jnp.zeros((SUBLANES, tc), F32)
        gpad[0:SUBLANES, :] = zero8
        vpad[0:SUBLANES, :] = zero8
        gpad[SUBLANES:, :] = ug_ref[...]
        vpad[SUBLANES:, :] = uv_ref[...]
        wg, wv = wg_ref[...], wv_ref[...]

        def chunk(c, carry):
            base = pl.multiple_of(c * ts, ts)
            g = _ffn_conv(gpad, wg, base, ts)
            v = _ffn_conv(vpad, wv, base, ts)
            act_ref[pl.ds(base, ts), :] = (g * _sigmoid(g) * v).astype(BF16)
            return carry

        lax.fori_loop(0, s // ts, chunk, 0)

    return pl.pallas_call(
        body, name="ffn_act_fwd", grid=(nfb,),
        in_specs=[pl.BlockSpec((s, tc), lambda b: (0, b)), pl.BlockSpec((s, tc), lambda b: (0, b + nfb)),
                  pl.BlockSpec((SUBLANES, tc), lambda b: (0, b)), pl.BlockSpec((SUBLANES, tc), lambda b: (0, b + nfb))],
        out_specs=pl.BlockSpec((s, tc), lambda b: (0, b)),
        out_shape=jax.ShapeDtypeStruct((s, f), BF16),
        scratch_shapes=[pltpu.VMEM((s + SUBLANES, tc), F32)] * 2,
        compiler_params=_cparams(("arbitrary",)),
    )(up, up, fp, fp)


def _ffn_act_bwd(up, dact, fp):
    s, f2 = up.shape
    f = f2 // 2
    tc = FFN_COLS
    ts = min(RNN_CHUNK, s)
    nfb = f // tc

    def body(ug_ref, uv_ref, da_ref, wg_ref, wv_ref, dup_ref, red_ref, gpad, vpad, dgs, dvs):
        half = pl.program_id(1)
        wg, wv = wg_ref[...], wv_ref[...]

        @pl.when(half == 0)
        def _():
            zero8 = jnp.zeros((SUBLANES, tc), F32)
            gpad[0:SUBLANES, :] = zero8
            vpad[0:SUBLANES, :] = zero8
            gpad[SUBLANES:, :] = ug_ref[...]
            vpad[SUBLANES:, :] = uv_ref[...]
            dgs[s:, :] = zero8
            dvs[s:, :] = zero8

            def act(c, carry):
                base = pl.multiple_of(c * ts, ts)
                g = _ffn_conv(gpad, wg, base, ts)
                v = _ffn_conv(vpad, wv, base, ts)
                da = da_ref[pl.ds(base, ts), :]
                sg = _sigmoid(g)
                dgs[pl.ds(base, ts), :] = da * v * (sg * (1.0 + g * (1.0 - sg)))
                dvs[pl.ds(base, ts), :] = da * (g * sg)
                return carry

            lax.fori_loop(0, s // ts, act, 0)

        def conv_t(src, pad, w, out_ref, red_ref):
            def chunk(c, carry):
                base = pl.multiple_of(c * ts, ts)
                d_here = src[pl.ds(base, ts), :]
                dx = w[2:3] * d_here
                for j in range(1, 3):
                    dx = dx + w[2 - j:3 - j] * _shift_up(src, base, ts, j, 0)
                out_ref[pl.ds(base, ts), :] = dx.astype(BF16)
                taps = tuple(carry[k] + _colsum(d_here * _shift_down(pad, base, ts, 2 - k)) for k in range(3))
                return taps + (carry[3] + _colsum(d_here),)

            z1 = jnp.zeros((1, tc), F32)
            red = lax.fori_loop(0, s // ts, chunk, (z1, z1, z1, z1))
            red_ref[...] = jnp.concatenate(list(red) + [jnp.zeros((4, tc), F32)], axis=0)

        @pl.when(half == 0)
        def _():
            conv_t(dgs, gpad, wg, dup_ref, red_ref)

        @pl.when(half == 1)
        def _():
            conv_t(dvs, vpad, wv, dup_ref, red_ref)

    gcol = pl.BlockSpec((s, tc), lambda b, h: (0, b))
    vcol = pl.BlockSpec((s, tc), lambda b, h: (0, b + nfb))
    gpar = pl.BlockSpec((SUBLANES, tc), lambda b, h: (0, b))
    vpar = pl.BlockSpec((SUBLANES, tc), lambda b, h: (0, b + nfb))
    return pl.pallas_call(
        body, name="ffn_act_bwd", grid=(nfb, 2),
        in_specs=[gcol, vcol, gcol, gpar, vpar],
        out_specs=[pl.BlockSpec((s, tc), lambda b, h: (0, b + h * nfb)),
                   pl.BlockSpec((SUBLANES, tc), lambda b, h: (0, b + h * nfb))],
        out_shape=[jax.ShapeDtypeStruct((s, f2), BF16), jax.ShapeDtypeStruct((SUBLANES, f2), F32)],
        scratch_shapes=[pltpu.VMEM((s + SUBLANES, tc), F32)] * 4,
        compiler_params=_cparams(("arbitrary", "arbitrary")),
    )(up, up, dact, fp, fp)


def _rows8(rows, width):
    rows = [r.reshape(1, width).astype(F32) for r in rows]
    return jnp.concatenate(rows + [jnp.zeros((SUBLANES - len(rows), width), F32)], axis=0)


def _block_diag(w):
    n, b, _ = w.shape
    w = w.reshape(n // 2, 2, b, b)
    z = jnp.zeros((n // 2, b, b), w.dtype)
    top = jnp.concatenate([w[:, 0], z], axis=2)
    bot = jnp.concatenate([z, w[:, 1]], axis=2)
    return jnp.concatenate([top, bot], axis=1)


def _block_diag_t(bd):
    n, b2, _ = bd.shape
    b = b2 // 2
    return jnp.stack([bd[:, :b, :b], bd[:, b:, b:]], axis=1).reshape(2 * n, b, b)


def _local_step(x, mod, positions, target, w_in, fetch, sm, emit):
    s, d = x.shape
    o_rnn, o_mla = D_RNN, D_RNN + Q_LORA + KV_LORA + QK_ROPE
    wts = {}
    w_in_rnn = w_in[:, :o_rnn]
    w_in_mla = jnp.concatenate([w_in[:, o_rnn:o_mla], jnp.zeros((d, MLA_W - (o_mla - o_rnn)), w_in.dtype)], axis=1)
    w_in_g = w_in[:, o_mla:]
    hd = QK_NOPE + QK_ROPE
    wa_bd = _block_diag(sm["w_gate_a"]).astype(BF16)
    wx_bd = _block_diag(sm["w_gate_x"]).astype(BF16)

    pos = positions.reshape(s)
    half = QK_ROPE // 2
    inv_freq = ROPE_THETA ** (-jnp.arange(half, dtype=F32) / half)
    ang = pos.astype(F32)[:, None] * inv_freq
    cos, sin = jnp.cos(ang), jnp.sin(ang)
    rot_c = jnp.concatenate([jnp.ones((s, QK_NOPE), F32), cos, cos, jnp.ones((s, LANES - hd), F32)], axis=1)
    rot_s = jnp.concatenate([jnp.zeros((s, QK_NOPE), F32), -sin, sin, jnp.zeros((s, LANES - hd), F32)], axis=1)
    keep = (pos != 0).astype(F32).reshape(s, 1)

    gmod1 = _rows8([sm["norm1_g"], mod[1], mod[0]], d)
    gmod2 = _rows8([sm["norm2_g"], mod[4], mod[3], mod[2]], d)
    rp = jnp.concatenate([sm["conv_w"].reshape(4, D_RNN), _rows8([sm["conv_b"], sm["b_gate_a"], sm["b_gate_x"], sm["lru_param"]], D_RNN)[:4]], axis=0)
    fp = _rows8([sm["ffn_conv_w"][0], sm["ffn_conv_w"][1], sm["ffn_conv_w"][2], sm["ffn_conv_b"]], 2 * D_FF)
    ng = _rows8([jnp.concatenate([sm["q_norm_g"].reshape(-1), sm["kv_norm_g"].reshape(-1), jnp.zeros((MLA_W - Q_LORA - KV_LORA,), F32)])], MLA_W)
    fpar = _rows8([mod[5], sm["final_g"]], d)

    h = _norm_mod_fwd(x, gmod1, "norm1_fwd")
    proj_rnn = _mm(h, w_in_rnn, name="mm_in_rnn")
    proj_mla = _mm(h, w_in_mla, name="mm_in_mla")
    proj_g = _mm(h, w_in_g, name="mm_in_g")
    xc, ra, ix, hs = _rnn_fwd(proj_rnn, keep, rp, wa_bd, wx_bd)
    qn, kvn, kr = _mla_prep_fwd(proj_mla, rot_c, rot_s, ng)
    wts.update(fetch(("w_uq", "w_ukv"), kr))
    w_uq_p = jnp.pad(wts["w_uq"].reshape(Q_LORA, N_HEADS, hd), ((0, 0), (0, 0), (0, LANES - hd))).reshape(Q_LORA, N_HEADS * LANES)
    w_ukv = wts["w_ukv"]
    q_rot = _rope_heads(_mm(qn, w_uq_p, name="mm_uq"), rot_c, rot_s, False, "rope_fwd")
    kv = _mm(kvn, w_ukv, out_dtype=BF16, name="mm_ukv")
    y_mla, lse = _att_fwd(q_rot, kv, kr)
    wts.update(fetch(("w_proj_rnn", "w_proj_mla", "w_out", "w_up", "w_down"), lse))
    pr = _mm(hs, wts["w_proj_rnn"], name="mm_proj_rnn")
    pm = _mm(y_mla, wts["w_proj_mla"], name="mm_proj_mla")
    merged = _merge_fwd(pr, pm, proj_g)
    o = _mm(merged, wts["w_out"], name="mm_out")
    x1, h2 = _resid_norm_fwd(x, o, gmod2)
    up = _mm(h2, wts["w_up"], name="mm_up")
    act = _ffn_act_fwd(up, fp)
    dn = _mm(act, wts["w_down"], name="mm_down")

    dx2, ddn, red_f = _final_fwd_bwd(x1, dn, target, fpar)
    dact = _mm(ddn, wts["w_down"], tb=True, name="mm_d_act")
    tok = emit("w_down", _mm(act, ddn, ta=True, out_dtype=BF16, name="mm_dw_down"))
    dup, red_ffn = _ffn_act_bwd(up, dact, fp + tok)
    dh2 = _mm(dup, wts["w_up"], tb=True, name="mm_d_h2")
    tok = tok + emit("w_up", _mm(h2, dup, ta=True, out_dtype=BF16, name="mm_dw_up"))
    dx1, do, red_2 = _norm2_bwd(x1, dh2, dx2, o, gmod2 + tok)
    dmerged = _mm(do, wts["w_out"], tb=True, name="mm_d_merged")
    tok = tok + emit("w_out", _mm(merged, do, ta=True, out_dtype=BF16, name="mm_dw_out"))
    dpr, dpm, dg = _merge_bwd(dmerged, pr, pm, proj_g)
    dy_rnn = _mm(dpr, wts["w_proj_rnn"], tb=True, name="mm_d_yrnn")
    tok = tok + emit("w_proj_rnn", _mm(hs, dpr, ta=True, out_dtype=BF16, name="mm_dw_proj_rnn"))
    dy_mla = _mm(dpm, wts["w_proj_mla"], tb=True, name="mm_d_ymla")
    tok = tok + emit("w_proj_mla", _mm(y_mla, dpm, ta=True, out_dtype=BF16, name="mm_dw_proj_mla"))
    dq_rot, dkv, dkr = _att_bwd(q_rot, kv, kr, y_mla, lse, dy_mla)
    dq = _rope_heads(dq_rot, rot_c, rot_s, True, "rope_bwd")
    dqn = _mm(dq, w_uq_p, tb=True, name="mm_d_qn")
    dw_uq_p = _mm(qn, dq, ta=True, out_dtype=BF16, name="mm_dw_uq")
    tok = tok + emit("w_uq", dw_uq_p.reshape(Q_LORA, N_HEADS, LANES)[:, :, :hd].reshape(Q_LORA, N_HEADS * hd))
    dkvn = _mm(dkv, w_ukv, tb=True, name="mm_d_kvn")
    tok = tok + emit("w_ukv", _mm(kvn, dkv, ta=True, out_dtype=BF16, name="mm_dw_ukv"))
    dproj_mla, red_m = _mla_prep_bwd(proj_mla, dqn, dkvn, dkr, rot_c, rot_s, ng + tok)
    dx_rnn, dwa_bd, dwx_bd, red_r = _rnn_bwd(proj_rnn, xc, ra, ix, hs, dy_rnn, keep, rp + tok, wa_bd, wx_bd)
    dw_in = jnp.concatenate([
        _mm(h, dx_rnn, ta=True, out_dtype=BF16, name="mm_dw_in_rnn"),
        _mm(h, dproj_mla, ta=True, out_dtype=BF16, name="mm_dw_in_mla")[:, :o_mla - o_rnn],
        _mm(h, dg, ta=True, out_dtype=BF16, name="mm_dw_in_g")], axis=1)
    tok = tok + emit("w_in", dw_in)
    dh_a = _mm(dx_rnn, w_in_rnn, tb=True, name="mm_d_h_rnn")
    dh_b = _mm(dproj_mla, w_in_mla, tb=True, name="mm_d_h_mla")
    dh_c = _mm(dg, w_in_g, tb=True, name="mm_d_h_g")
    grad_x, red_1 = _norm1_bwd(x, dh_a, dh_b, dh_c, dx1, gmod1 + tok)

    gs = {
        "norm1_g": red_1[0], "conv_w": red_r[0:4], "conv_b": red_r[4], "w_gate_a": _block_diag_t(dwa_bd),
        "b_gate_a": red_r[5], "w_gate_x": _block_diag_t(dwx_bd), "b_gate_x": red_r[6], "lru_param": red_r[7],
        "q_norm_g": red_m[0, :Q_LORA], "kv_norm_g": red_m[0, Q_LORA:Q_LORA + KV_LORA], "norm2_g": red_2[0],
        "ffn_conv_w": red_ffn[0:3], "ffn_conv_b": red_ffn[3], "final_g": red_f[0],
    }
    dmod = jnp.stack([red_1[2], red_1[1], red_2[3], red_2[2], red_2[1], red_f[1]], axis=0)
    return red_f[2, 0], grad_x, gs, dmod


MESH_IDS = pl.DeviceIdType.MESH
HBM_SPEC = pl.BlockSpec(memory_space=pltpu.HBM)


def _my_slot():
    return 4 * lax.axis_index("x") + 2 * lax.axis_index("y") + lax.axis_index("c")


def _all_gather(arrs, name):
    n = len(arrs)

    def body(*refs):
        ins, outs = refs[:n], refs[n:2 * n]
        send_sems, recv_sems, local_sems = refs[2 * n:]
        x, y, c = lax.axis_index("x"), lax.axis_index("y"), lax.axis_index("c")
        me, sibling = (x, y, c), (x, y, 1 - c)
        chips = [(1 - x, y), (x, 1 - y), (1 - x, 1 - y)]

        def slot(dev):
            return 4 * dev[0] + 2 * dev[1] + dev[2]

        def copy(a, k, block, to, src=None):
            dst = outs[a].at[slot(block)]
            return pltpu.make_async_remote_copy(
                src_ref=dst if src is None else src, dst_ref=dst, send_sem=send_sems.at[a, k], recv_sem=recv_sems.at[a, k],
                device_id=to, device_id_type=MESH_IDS)

        mine = [pltpu.make_async_copy(ins[a], outs[a].at[slot(me)], local_sems.at[a]) for a in range(n)]
        for cp in mine:
            cp.start()
        first = []
        for a in range(n):
            first.append(copy(a, 0, me, sibling, src=ins[a]))
            first += [copy(a, 1 + j, me, (*chip, c), src=ins[a]) for j, chip in enumerate(chips)]
        for cp in first:
            cp.start()
        passed = []
        for j, chip in enumerate(chips):
            for a in range(n):
                copy(a, 1 + j, (*chip, c), me).wait_recv()
                fwd = copy(a, 4 + j, (*chip, c), sibling)
                fwd.start()
                passed.append(fwd)
        for a in range(n):
            copy(a, 0, sibling, me).wait_recv()
            for j, chip in enumerate(chips):
                copy(a, 4 + j, (*chip, 1 - c), me).wait_recv()
        for cp in first + passed:
            cp.wait_send()
        for cp in mine:
            cp.wait()

    return pl.pallas_call(
        body, name=name,
        in_specs=[HBM_SPEC] * n, out_specs=[HBM_SPEC] * n,
        out_shape=[jax.ShapeDtypeStruct((N_DEV,) + a.shape, a.dtype) for a in arrs],
        scratch_shapes=[pltpu.SemaphoreType.DMA((n, 7)), pltpu.SemaphoreType.DMA((n, 7)), pltpu.SemaphoreType.DMA((n,))],
    )(*arrs)


SEM_SPEC =pl.BlockSpec(memory_space=pltpu.SEMAPHORE)
DATAFLOW = pltpu.SideEffectType.DATAFLOW_SIDE_EFFECTING
FLIPS = [(dx, dy, dc) for dx in (0, 1) for dy in (0, 1) for dc in (0, 1)][1:]


def _peer(k):
    dx, dy, dc = FLIPS[k]
    peer = (lax.axis_index("x") ^ dx, lax.axis_index("y") ^ dy, lax.axis_index("c") ^ dc)
    return peer, 4 * peer[0] + 2 * peer[1] + peer[2]


def _gather_start(shards, after, name):
    n, nf = len(shards), len(FLIPS)

    def body(*refs):
        srcs, lands = refs[:n], refs[n:2 * n]
        send_sems, recv_sems = refs[2 * n + 1:3 * n + 1], refs[3 * n + 1:4 * n + 1]
        token = refs[-1]
        me = _my_slot()
        for a in range(n):
            for k in range(nf):
                peer, _ = _peer(k)
                pltpu.make_async_remote_copy(
                    src_ref=srcs[a], dst_ref=lands[a].at[me], send_sem=send_sems[a].at[k], recv_sem=recv_sems[a].at[k],
                    device_id=peer, device_id_type=MESH_IDS).start()
        token[...] = jnp.zeros(token.shape, F32)

    land_shapes = [(N_DEV,) + a.shape for a in shards]
    sems = [pltpu.SemaphoreType.DMA((nf,))] * n
    out = pl.pallas_call(
        body, name=name,
        out_shape=(*sems, *sems, *[pltpu.HBM(a.shape, a.dtype) for a in shards],
                   *[pltpu.HBM(shp, a.dtype) for shp, a in zip(land_shapes, shards)],
                   jax.ShapeDtypeStruct((SUBLANES, LANES), F32)),
        in_specs=[HBM_SPEC] * (2 * n) + [pl.BlockSpec(memory_space=pl.ANY)],
        out_specs=(*[SEM_SPEC] * (2 * n), *[HBM_SPEC] * (2 * n), pl.BlockSpec(memory_space=pltpu.VMEM)),
        input_output_aliases={i: 2 * n + i for i in range(2 * n)},
        compiler_params=pltpu.CompilerParams(has_side_effects=DATAFLOW),
    )(*[pltpu.with_memory_space_constraint(a, pltpu.HBM) for a in shards],
      *[pltpu.with_memory_space_constraint(lax.empty(shp, a.dtype), pltpu.HBM) for shp, a in zip(land_shapes, shards)],
      after)
    return [(out[a], out[n + a], out[2 * n + a], out[3 * n + a]) for a in range(n)], out[-1]


def _gather_wait(flights, after, name):
    n, nf = len(flights), len(FLIPS)

    def body(*refs):
        send_sems, recv_sems = refs[:n], refs[n:2 * n]
        srcs, lands = refs[2 * n:3 * n], refs[3 * n:4 * n]
        for a in range(n):
            for k in range(nf):
                peer, peer_slot = _peer(k)
                cp = pltpu.make_async_remote_copy(
                    src_ref=srcs[a], dst_ref=lands[a].at[peer_slot], send_sem=send_sems[a].at[k],
                    recv_sem=recv_sems[a].at[k], device_id=peer, device_id_type=MESH_IDS)
                cp.wait_send()
                cp.wait_recv()

    srcs, lands = [f[2] for f in flights], [f[3] for f in flights]
    out = pl.pallas_call(
        body, name=name,
        out_shape=(*[pltpu.HBM(a.shape, a.dtype) for a in srcs], *[pltpu.HBM(a.shape, a.dtype) for a in lands]),
        in_specs=[SEM_SPEC] * (2 * n) + [HBM_SPEC] * (2 * n) + [pl.BlockSpec(memory_space=pl.ANY)],
        out_specs=tuple([HBM_SPEC] * (2 * n)),
        input_output_aliases={2 * n + i: i for i in range(2 * n)},
        compiler_params=pltpu.CompilerParams(has_side_effects=DATAFLOW),
    )(*[f[0] for f in flights], *[f[1] for f in flights], *srcs, *lands, after)
    return list(out[n:])


def _scatter_start(chunks, name):
    def body(src_ref, land_ref, send_sems, recv_sems, src_thru, land_thru, token):
        me = _my_slot()
        for k in range(len(FLIPS)):
            peer, peer_slot = _peer(k)
            pltpu.make_async_remote_copy(
                src_ref=src_ref.at[peer_slot], dst_ref=land_ref.at[me], send_sem=send_sems.at[k], recv_sem=recv_sems.at[k],
                device_id=peer, device_id_type=MESH_IDS).start()
        token[...] = jnp.zeros(token.shape, F32)

    n = len(FLIPS)
    hbm = pltpu.HBM(chunks.shape, chunks.dtype)
    return pl.pallas_call(
        body, name=name,
        out_shape=(pltpu.SemaphoreType.DMA((n,)), pltpu.SemaphoreType.DMA((n,)), hbm, hbm,
                   jax.ShapeDtypeStruct((SUBLANES, LANES), F32)),
        in_specs=(HBM_SPEC, HBM_SPEC),
        out_specs=(SEM_SPEC, SEM_SPEC, HBM_SPEC, HBM_SPEC, pl.BlockSpec(memory_space=pltpu.VMEM)),
        input_output_aliases={0: 2, 1: 3},
        compiler_params=pltpu.CompilerParams(has_side_effects=DATAFLOW),
    )(pltpu.with_memory_space_constraint(chunks, pltpu.HBM),
      pltpu.with_memory_space_constraint(lax.empty(chunks.shape, chunks.dtype), pltpu.HBM))


def _scatter_wait(send_sems, recv_sems, src_thru, land_thru, after, name):
    def body(src_ref, land_ref, send_sems, recv_sems, after_ref, src_dead, got_ref):
        for k in range(len(FLIPS)):
            peer, peer_slot = _peer(k)
            cp = pltpu.make_async_remote_copy(
                src_ref=src_ref.at[peer_slot], dst_ref=land_ref.at[peer_slot], send_sem=send_sems.at[k],
                recv_sem=recv_sems.at[k], device_id=peer, device_id_type=MESH_IDS)
            cp.wait_send()
            cp.wait_recv()

    hbm = pltpu.HBM(src_thru.shape, src_thru.dtype)
    return pl.pallas_call(
        body, name=name, out_shape=(hbm, hbm),
        in_specs=(HBM_SPEC, HBM_SPEC, SEM_SPEC, SEM_SPEC, pl.BlockSpec(memory_space=pl.ANY)),
        out_specs=(HBM_SPEC, HBM_SPEC), input_output_aliases={0: 0, 1: 1},
        compiler_params=pltpu.CompilerParams(has_side_effects=DATAFLOW),
    )(src_thru, land_thru, send_sems, recv_sems, after)


def _sum_sources(parts, name):
    k, r, c = parts.shape
    tr = r if k * r * c <= 2 * 1024 * 1024 else _pick(r, (512, 256, 128, 64, 32, 16, 8))

    def body(p_ref, o_ref):
        acc = p_ref[0].astype(F32)
        for s in range(1, k):
            acc = acc + p_ref[s].astype(F32)
        o_ref[...] = acc

    return pl.pallas_call(
        body, name=name, grid=(r // tr,),
        in_specs=[pl.BlockSpec((k, tr, c), lambda i: (0, i, 0))],
        out_specs=pl.BlockSpec((tr, c), lambda i: (i, 0)),
        out_shape=jax.ShapeDtypeStruct((r, c), F32),
        compiler_params=_cparams(("arbitrary",)),
    )(parts)


def _adamw(parts, w, m, v, name, own=None):
    k, r, c = parts.shape
    tr = r if r * c <= 256 * 1024 else _pick(r, (256, 128, 64, 32, 16, 8))

    def body(*refs):
        p_ref, w_ref, m_ref, v_ref = refs[:4]
        g_ref, d_ref, nm_ref, nv_ref = refs[-4:]

        def part(s):
            if own is None:
                return p_ref[s].astype(F32)
            return jnp.where(_my_slot() == s, refs[4][...], p_ref[s]).astype(F32)

        g = part(0)
        for s in range(1, k):
            g = g + part(s)
        m_new = ADAM_B1 * m_ref[...] + (1.0 - ADAM_B1) * g
        v_new = ADAM_B2 * v_ref[...] + (1.0 - ADAM_B2) * jnp.square(g)
        m_hat = m_new / (1.0 - ADAM_B1 ** ADAM_STEP)
        v_hat = v_new / (1.0 - ADAM_B2 ** ADAM_STEP)
        g_ref[...] = g
        d_ref[...] = -ADAM_LR * (m_hat / (jnp.sqrt(v_hat) + ADAM_EPS) + ADAM_WD * w_ref[...])
        nm_ref[...] = m_new
        nv_ref[...] = v_new

    blk = pl.BlockSpec((tr, c), lambda i: (i, 0))
    return pl.pallas_call(
        body, name=name, grid=(r // tr,),
        in_specs=[pl.BlockSpec((k, tr, c), lambda i: (0, i, 0)), blk, blk, blk] + ([] if own is None else [blk]),
        out_specs=[blk] * 4,
        out_shape=[jax.ShapeDtypeStruct((r, c), F32)] * 4,
        compiler_params=_cparams(("arbitrary",)),
    )(parts, w, m, v, *([] if own is None else [own]))


def _silu(v):
    return v * _sigmoid(v)


def _ada_fwd(c_all, w, b):
    def body(c_ref, w_ref, b_ref, o_ref):
        ca = _silu(c_ref[...]).astype(BF16)
        o_ref[...] = jnp.dot(ca, w_ref[...].astype(BF16), preferred_element_type=F32) + b_ref[...]

    return pl.pallas_call(
        body, name="ada_fwd", out_shape=jax.ShapeDtypeStruct((c_all.shape[0], w.shape[1]), F32),
        compiler_params=_cparams(),
    )(c_all, w, b)


def _ada_bwd(c_all, dmod):
    def body(c_ref, d_ref, o_ref):
        ca = _silu(c_ref[...]).astype(BF16).astype(F32)
        dm = d_ref[...].astype(BF16).astype(F32)
        acc = jnp.zeros(o_ref.shape, F32)
        for bi in range(c_all.shape[0]):
            acc = acc + jnp.transpose(ca[bi:bi + 1, :]) * dm[bi:bi + 1, :]
        o_ref[...] = acc

    return pl.pallas_call(
        body, name="ada_bwd", out_shape=jax.ShapeDtypeStruct((c_all.shape[1], dmod.shape[1]), F32),
        compiler_params=_cparams(),
    )(c_all, dmod)


COL_SHARDED = ("w_in", "w_uq", "w_ukv", "w_up")
ROW_SHARDED = ("w_proj_rnn", "w_proj_mla", "w_out", "w_down")
REPLICATED = ("b_ada", "norm1_g", "conv_b", "w_gate_a", "b_gate_a", "w_gate_x", "b_gate_x", "lru_param", "q_norm_g",
              "kv_norm_g", "norm2_g", "ffn_conv_b", "final_g")
WEIGHTS = ("w_ada", "b_ada", "norm1_g", "w_in", "conv_w", "conv_b", "w_gate_a", "b_gate_a", "w_gate_x", "b_gate_x",
           "lru_param", "q_norm_g", "w_uq", "kv_norm_g", "w_ukv", "w_proj_rnn", "w_proj_mla", "w_out", "norm2_g", "w_up",
           "ffn_conv_w", "ffn_conv_b", "w_down", "final_g")
PACK_LANES = 128


def _pack(vecs):
    flat = jnp.concatenate([v.reshape(-1).astype(F32) for v in vecs])
    pad = (-flat.shape[0]) % (PACK_LANES * SUBLANES)
    return jnp.concatenate([flat, jnp.zeros((pad,), F32)]).reshape(-1, PACK_LANES)


def _unpack(packed, shapes):
    flat = packed.reshape(-1)
    out, off = [], 0
    for shp in shapes:
        size = math.prod(shp)
        out.append(flat[off:off + size].reshape(shp))
        off += size
    return out


def kernel(x, c, positions, w_ada, b_ada, norm1_g, w_in, conv_w, conv_b, w_gate_a, b_gate_a, w_gate_x, b_gate_x, lru_param, q_norm_g, w_uq, kv_norm_g, w_ukv, w_proj_rnn, w_proj_mla, w_out, norm2_g, w_up, ffn_conv_w, ffn_conv_b, w_down, final_g, loss_target, m_w_ada, m_b_ada, m_norm1_g, m_w_in, m_conv_w, m_conv_b, m_w_gate_a, m_b_gate_a, m_w_gate_x, m_b_gate_x, m_lru_param, m_q_norm_g, m_w_uq, m_kv_norm_g, m_w_ukv, m_w_proj_rnn, m_w_proj_mla, m_w_out, m_norm2_g, m_w_up, m_ffn_conv_w, m_ffn_conv_b, m_w_down, m_final_g, v_w_ada, v_b_ada, v_norm1_g, v_w_in, v_conv_w, v_conv_b, v_w_gate_a, v_b_gate_a, v_w_gate_x, v_b_gate_x, v_lru_param, v_q_norm_g, v_w_uq, v_kv_norm_g, v_w_ukv, v_w_proj_rnn, v_w_proj_mla, v_w_out, v_norm2_g, v_w_up, v_ffn_conv_w, v_ffn_conv_b, v_w_down, v_final_g):
    args = dict(locals())
    w = {n: args[n] for n in WEIGHTS}
    m = {n: args["m_" + n] for n in WEIGHTS}
    v = {n: args["v_" + n] for n in WEIGHTS}
    s, d = x.shape[1], x.shape[2]
    me = _my_slot()
    def two_d(a):
        assert a.ndim == 3 and a.shape[0] == 1, a.shape
        return a[0]

    big = COL_SHARDED + ROW_SHARDED
    shard = {n: two_d(w[n]).astype(BF16) for n in big}

    def whole(n, g):
        k, r, cc = g.shape
        return jnp.transpose(g, (1, 0, 2)).reshape(r, k * cc) if n in COL_SHARDED else g.reshape(k * r, cc)

    first = _all_gather([shard["w_in"], c, two_d(conv_w), two_d(ffn_conv_w)], "gather_first")
    later = ("w_uq", "w_ukv", "w_proj_rnn", "w_proj_mla", "w_out", "w_up", "w_down")
    flights, started = _gather_start([shard[n] for n in later], first[0], "gather_start")
    flight = dict(zip(later, flights))

    def fetch(names, after):
        lands = _gather_wait([flight[n] for n in names], after, "gather_wait_" + names[0])
        return {n: whole(n, lax.dynamic_update_index_in_dim(g, shard[n], me, 0)) for n, g in zip(names, lands)}

    c_all = first[1].reshape(N_DEV, d) + started[0, 0]
    conv_w_all = jnp.transpose(first[2], (1, 0, 2)).reshape(conv_w.shape[1], -1)
    ffn_conv_w_all = jnp.transpose(first[3], (1, 0, 2)).reshape(ffn_conv_w.shape[1], -1)

    ada_cols = w_ada.shape[2]
    b_cols = lax.dynamic_slice(b_ada, (0, me * ada_cols), (1, ada_cols))
    mod_cols = _ada_fwd(c_all, w_ada[0], b_cols)
    mod_all, = _all_gather([mod_cols], "gather_mod")
    mod = lax.dynamic_index_in_dim(mod_all, me, axis=1, keepdims=False).reshape(6, d)

    sm = {n: w[n][0] for n in REPLICATED if n not in ("b_ada", "final_g")}
    sm["final_g"] = final_g
    sm["conv_w"] = conv_w_all
    sm["ffn_conv_w"] = ffn_conv_w_all
    in_flight = {}

    def emit(n, g):
        if n in COL_SHARDED:
            r, cc = g.shape
            chunks = jnp.transpose(g.reshape(r, N_DEV, cc // N_DEV), (1, 0, 2))
        else:
            chunks = g.reshape(N_DEV, g.shape[0] // N_DEV, g.shape[1])
        *in_flight[n], token = _scatter_start(chunks, "scatter_start_" + n)
        return token[0, 0]

    sq, grad_x, gs, dmod = _local_step(x[0], mod, positions[0], loss_target[0], whole("w_in", first[0]), fetch, sm, emit)

    small_names = [n for n in REPLICATED if n != "b_ada"] + ["conv_w", "ffn_conv_w"]
    small_shapes = [gs[n].shape for n in small_names] + [(6 * d,), (1,)]
    partial = _pack([gs[n] for n in small_names] + [dmod, sq.reshape(1)])
    partial_all, = _all_gather([partial], "gather_small")
    summed = _unpack(_sum_sources(partial_all, "sum_small"), small_shapes)
    g_small = dict(zip(small_names, summed[:len(small_names)]))
    g_small["b_ada"] = summed[len(small_names)]
    loss = 0.5 * summed[-1][0] / d
    n_before = sum(math.prod(t) for t in small_shapes[:len(small_names)])
    dmod_all = partial_all.reshape(N_DEV, -1)[:, n_before:n_before + 6 * d]
    dmod_cols = lax.dynamic_slice(dmod_all, (0, me * ada_cols), (N_DEV, ada_cols))

    grads, deltas, new_m, new_v = {}, {}, {}, {}

    def update(n, parts, own=None):
        shp = w[n].shape
        g, dl, nm, nv = _adamw(parts, two_d(w[n]), two_d(m[n]), two_d(v[n]), "adamw_" + n, own)
        grads[n], deltas[n], new_m[n], new_v[n] = g.reshape(shp), dl.reshape(shp), nm.reshape(shp), nv.reshape(shp)

    update("w_ada", _ada_bwd(c_all, dmod_cols)[None])

    for n in big:
        chunks, landed = _scatter_wait(*in_flight[n], grad_x, "scatter_wait_" + n)
        update(n, landed, lax.dynamic_index_in_dim(chunks, me, axis=0, keepdims=False))

    for n in ("conv_w", "ffn_conv_w"):
        cols = w[n].shape[2]
        update(n, lax.dynamic_slice(g_small[n], (0, me * cols), (g_small[n].shape[0], cols))[None])

    rep_shapes = [w[n].shape for n in REPLICATED]
    g_rep, d_rep, m_rep, v_rep = _adamw(
        _pack([g_small[n] for n in REPLICATED])[None], _pack([w[n] for n in REPLICATED]), _pack([m[n] for n in REPLICATED]),
        _pack([v[n] for n in REPLICATED]), "adamw_replicated")
    for dst, packed in ((grads, g_rep), (deltas, d_rep), (new_m, m_rep), (new_v, v_rep)):
        dst.update(zip(REPLICATED, _unpack(packed, rep_shapes)))

    return (loss, grad_x[None], *[grads[n] for n in WEIGHTS], *[deltas[n] for n in WEIGHTS],
            *[new_m[n] for n in WEIGHTS], *[new_v[n] for n in WEIGHTS])
```

```python
import functools
import math

import jax
import jax.numpy as jnp
from jax import lax
from jax.experimental import pallas as pl
from jax.experimental.pallas import tpu as pltpu

F32 = jnp.float32
BF16 = jnp.bfloat16

N_DEV = 8
LANES = 128
SUBLANES = 8
VMEM_LIMIT = 56 * 1024 * 1024

D_RNN = 1280
Q_LORA = 384
KV_LORA = 256
QK_NOPE = 64
QK_ROPE = 32
V_HEAD = 64
N_HEADS = 16
D_FF = 2816
ROPE_THETA = 10000.0
LRU_C = 8.0
EPS = 1e-6
MLA_W = 768
ATT_SCALE = 1.0 / math.sqrt(QK_NOPE + QK_ROPE)

ADAM_LR, ADAM_B1, ADAM_B2, ADAM_EPS, ADAM_WD, ADAM_STEP = 0.001, 0.9, 0.999, 1e-08, 0.01, 10


def _cparams(sem=None):
    return pltpu.CompilerParams(dimension_semantics=sem, vmem_limit_bytes=VMEM_LIMIT)


def _pick(n, prefs):
    for p in prefs:
        if n % p == 0:
            return p
    return n


def _sigmoid(v):
    return 1.0 / (1.0 + jnp.exp(-v))


def _lane(shape):
    return lax.broadcasted_iota(jnp.int32, shape, len(shape) - 1)


def _row(shape):
    return lax.broadcasted_iota(jnp.int32, shape, len(shape) - 2)


MM_BLOCK_BYTES = 36 * 1024 * 1024


def _divisors(n):
    return [t for t in range(n, 0, -LANES) if n % t == 0] if n % LANES == 0 else [n]


def _mm_tiles(m, n, k, a_bytes, b_bytes, o_bytes):
    tm = _pick(m, (512, 384, 256, 128))
    for tk in _divisors(k):
        for tn in _divisors(n):
            need = 2 * (tm * tk * a_bytes + tk * tn * b_bytes + tm * tn * o_bytes) + (tm * tn * 4 if tk < k else 0)
            if tn <= 2048 and need <= MM_BLOCK_BYTES:
                return tm, tn, tk
    raise ValueError((m, n, k))


def _mm(a, b, *, ta=False, tb=False, out_dtype=F32, name):
    (k_a, m) = a.shape if ta else a.shape[::-1]
    (n, k_b) = b.shape if tb else b.shape[::-1]
    assert k_a == k_b, (a.shape, b.shape, ta, tb)
    k = k_a
    tm, tn, tk = _mm_tiles(m, n, k, a.dtype.itemsize, b.dtype.itemsize, jnp.dtype(out_dtype).itemsize)
    nk = k // tk
    dims = (((0 if ta else 1,), (1 if tb else 0,)), ((), ()))

    def body(a_ref, b_ref, o_ref, *acc):
        part = lax.dot_general(a_ref[...].astype(BF16), b_ref[...].astype(BF16), dims, preferred_element_type=F32)
        if nk == 1:
            o_ref[...] = part.astype(out_dtype)
            return
        acc_ref, = acc
        kk = pl.program_id(2)

        @pl.when(kk == 0)
        def _():
            acc_ref[...] = part

        @pl.when(kk > 0)
        def _():
            acc_ref[...] += part

        @pl.when(kk == nk - 1)
        def _():
            o_ref[...] = acc_ref[...].astype(out_dtype)

    a_spec = pl.BlockSpec((tk, tm), lambda i, j, kk: (kk, i)) if ta else pl.BlockSpec((tm, tk), lambda i, j, kk: (i, kk))
    b_spec = pl.BlockSpec((tn, tk), lambda i, j, kk: (j, kk)) if tb else pl.BlockSpec((tk, tn), lambda i, j, kk: (kk, j))
    return pl.pallas_call(
        body, name=name,
        grid=(m // tm, n // tn, nk),
        in_specs=[a_spec, b_spec],
        out_specs=pl.BlockSpec((tm, tn), lambda i, j, kk: (i, j)),
        out_shape=jax.ShapeDtypeStruct((m, n), out_dtype),
        scratch_shapes=[] if nk == 1 else [pltpu.VMEM((tm, tn), F32)],
        compiler_params=_cparams(("arbitrary", "arbitrary", "arbitrary")),
    )(a, b)


def _rowwise(fn, row_ins, par_ins, out_defs, red_defs, *, name, tr=256):
    s = row_ins[0].shape[0]
    tr = min(tr, s)
    nr, npar, no = len(row_ins), len(par_ins), len(out_defs)

    def body(*refs):
        rin, pin = refs[:nr], refs[nr:nr + npar]
        outs, reds = refs[nr + npar:nr + npar + no], refs[nr + npar + no:]
        i = pl.program_id(0)

        @pl.when(i == 0)
        def _():
            for r in reds:
                r[...] = jnp.zeros_like(r)

        fn(i, rin, pin, outs, reds)

    in_specs = [pl.BlockSpec((tr, a.shape[1]), lambda i: (i, 0)) for a in row_ins]
    in_specs += [pl.BlockSpec(a.shape, lambda i, nd=a.ndim: (0,) * nd) for a in par_ins]
    out_specs = [pl.BlockSpec((tr, c), lambda i: (i, 0)) for c, _ in out_defs]
    out_specs += [pl.BlockSpec(shp, lambda i: (0, 0)) for shp in red_defs]
    out_shape = [jax.ShapeDtypeStruct((s, c), dt) for c, dt in out_defs]
    out_shape += [jax.ShapeDtypeStruct(shp, F32) for shp in red_defs]
    return pl.pallas_call(
        body, name=name, grid=(s // tr,), in_specs=in_specs, out_specs=out_specs, out_shape=out_shape,
        compiler_params=_cparams(("arbitrary",)),
    )(*row_ins, *par_ins)


def _rms(v):
    return lax.rsqrt(jnp.mean(v * v, axis=-1, keepdims=True) + EPS)


def _colsum(v):
    return jnp.sum(v, axis=0, keepdims=True)


def _rms_bwd(dn, n, rstd):
    return rstd * (dn - n * jnp.mean(dn * n, axis=-1, keepdims=True))


def _norm_mod_fwd(x, gmod, name):
    def fn(i, rin, pin, outs, reds):
        xv = rin[0][...]
        p = pin[0][...]
        n = xv * _rms(xv)
        outs[0][...] = ((n * p[0:1]) * (1.0 + p[1:2]) + p[2:3]).astype(BF16)

    return _rowwise(fn, [x], [gmod], [(x.shape[1], BF16)], [], name=name)[0]


def _rope(v, rot_c, rot_s):
    half = QK_ROPE // 2
    swapped = jnp.where(_lane(v.shape) < QK_NOPE + half, pltpu.roll(v, LANES - half, 1), pltpu.roll(v, half, 1))
    return v * rot_c + swapped * rot_s


def _rope_t(dv, rot_c, rot_s):
    half = QK_ROPE // 2
    ds = dv * rot_s
    lane = _lane(dv.shape)
    swapped = jnp.where(lane < QK_NOPE + half, pltpu.roll(ds, LANES - half, 1), pltpu.roll(ds, half, 1))
    in_rope = (lane >= QK_NOPE) & (lane < QK_NOPE + QK_ROPE)
    return dv * rot_c + jnp.where(in_rope, swapped, 0.0)


def _mla_prep_fwd(proj_mla, rot_c, rot_s, ng):
    o1, o2 = Q_LORA, Q_LORA + KV_LORA

    def fn(i, rin, pin, outs, reds):
        g = pin[0][...]
        ql = rin[0][:, 0:o1]
        kl = rin[0][:, o1:o2]
        outs[0][...] = (ql * _rms(ql) * g[0:1, 0:o1]).astype(BF16)
        outs[1][...] = (kl * _rms(kl) * g[0:1, o1:o2]).astype(BF16)
        kr = pltpu.roll(rin[0][:, o2:o2 + LANES], QK_NOPE, 1)
        outs[2][...] = _rope(kr, rin[1][...], rin[2][...]).astype(BF16)

    return _rowwise(fn, [proj_mla, rot_c, rot_s], [ng], [(Q_LORA, BF16), (KV_LORA, BF16), (LANES, BF16)], [],
                    name="mla_prep_fwd")


def _mla_prep_bwd(proj_mla, dqn, dkvn, dkr, rot_c, rot_s, ng):
    o1, o2 = Q_LORA, Q_LORA + KV_LORA

    def fn(i, rin, pin, outs, reds):
        g = pin[0][...]
        ql = rin[0][:, 0:o1]
        kl = rin[0][:, o1:o2]
        rq, rk = _rms(ql), _rms(kl)
        nq, nk = ql * rq, kl * rk
        dq, dk = rin[1][...], rin[2][...]
        outs[0][:, 0:o1] = _rms_bwd(dq * g[0:1, 0:o1], nq, rq).astype(BF16)
        outs[0][:, o1:o2] = _rms_bwd(dk * g[0:1, o1:o2], nk, rk).astype(BF16)
        dkr_pre = pltpu.roll(_rope_t(rin[3][...], rin[4][...], rin[5][...]), LANES - QK_NOPE, 1)
        outs[0][:, o2:] = jnp.where(_lane(dkr_pre.shape) < QK_ROPE, dkr_pre, 0.0).astype(BF16)
        reds[0][0:1, 0:o1] += _colsum(dq * nq)
        reds[0][0:1, o1:o2] += _colsum(dk * nk)

    return _rowwise(fn, [proj_mla, dqn, dkvn, dkr, rot_c, rot_s], [ng], [(MLA_W, BF16)], [(SUBLANES, MLA_W)],
                    name="mla_prep_bwd")


def _rope_heads(q, rot_c, rot_s, transpose, name):
    def fn(i, rin, pin, outs, reds):
        c, sn = rin[1][...] * Q_PRESCALE, rin[2][...] * Q_PRESCALE
        for h in range(N_HEADS):
            sl = slice(h * LANES, (h + 1) * LANES)
            v = rin[0][:, sl]
            outs[0][:, sl] = (_rope_t(v, c, sn) if transpose else _rope(v, c, sn)).astype(BF16)

    return _rowwise(fn, [q, rot_c, rot_s], [], [(q.shape[1], BF16)], [], name=name)[0]


def _merge_fwd(pr, pm, proj_g):
    d = pr.shape[1]

    def fn(i, rin, pin, outs, reds):
        outs[0][...] = (_sigmoid(rin[2][:, 0:d]) * rin[0][...] + _sigmoid(rin[2][:, d:]) * rin[1][...]).astype(BF16)

    return _rowwise(fn, [pr, pm, proj_g], [], [(d, BF16)], [], name="merge_fwd")[0]


def _merge_bwd(dmerged, pr, pm, proj_g):
    d = pr.shape[1]

    def fn(i, rin, pin, outs, reds):
        dm = rin[0][...]
        sr, sm = _sigmoid(rin[3][:, 0:d]), _sigmoid(rin[3][:, d:])
        outs[0][...] = (dm * sr).astype(BF16)
        outs[1][...] = (dm * sm).astype(BF16)
        outs[2][:, 0:d] = (dm * rin[1][...] * sr * (1.0 - sr)).astype(BF16)
        outs[2][:, d:] = (dm * rin[2][...] * sm * (1.0 - sm)).astype(BF16)

    return _rowwise(fn, [dmerged, pr, pm, proj_g], [], [(d, BF16), (d, BF16), (2 * d, BF16)], [], name="merge_bwd")


def _resid_norm_fwd(x, o, gmod):
    d = x.shape[1]

    def fn(i, rin, pin, outs, reds):
        p = pin[0][...]
        x1 = rin[0][...] + p[3:4] * rin[1][...]
        outs[0][...] = x1
        outs[1][...] = ((x1 * _rms(x1) * p[0:1]) * (1.0 + p[1:2]) + p[2:3]).astype(BF16)

    return _rowwise(fn, [x, o], [gmod], [(d, F32), (d, BF16)], [], name="resid_norm_fwd")


def _final_fwd_bwd(x1, dn, target, par):
    d = x1.shape[1]

    def fn(i, rin, pin, outs, reds):
        p = pin[0][...]
        dnv = rin[1][...]
        x2 = rin[0][...] + p[0:1] * dnv
        rstd = _rms(x2)
        n3 = x2 * rstd
        err = n3 * p[1:2] - rin[2][...]
        dy = err * (1.0 / d)
        dx2 = _rms_bwd(dy * p[1:2], n3, rstd)
        outs[0][...] = dx2
        outs[1][...] = (dx2 * p[0:1]).astype(BF16)
        reds[0][0:1, :] += _colsum(dy * n3)
        reds[0][1:2, :] += _colsum(dx2 * dnv)
        reds[0][2:3, :] += jnp.zeros((1, d), F32) + jnp.sum(err * err)

    return _rowwise(fn, [x1, dn, target], [par], [(d, F32), (d, BF16)], [(SUBLANES, d)], name="final_fwd_bwd")


def _norm2_bwd(x1, dh2, dx2, o, gmod):
    d = x1.shape[1]

    def fn(i, rin, pin, outs, reds):
        p = pin[0][...]
        x1v, dh = rin[0][...], rin[1][...]
        rstd = _rms(x1v)
        n2 = x1v * rstd
        dx1 = rin[2][...] + _rms_bwd(dh * (p[0:1] * (1.0 + p[1:2])), n2, rstd)
        outs[0][...] = dx1
        outs[1][...] = (dx1 * p[3:4]).astype(BF16)
        reds[0][0:1, :] += _colsum(dh * n2 * (1.0 + p[1:2]))
        reds[0][1:2, :] += _colsum(dh * n2 * p[0:1])
        reds[0][2:3, :] += _colsum(dh)
        reds[0][3:4, :] += _colsum(dx1 * rin[3][...])

    return _rowwise(fn, [x1, dh2, dx2, o], [gmod], [(d, F32), (d, BF16)], [(SUBLANES, d)], name="norm2_bwd")


def _norm1_bwd(x, dh_a, dh_b, dh_c, dx1, gmod):
    d = x.shape[1]

    def fn(i, rin, pin, outs, reds):
        p = pin[0][...]
        xv = rin[0][...]
        dh = rin[1][...] + rin[2][...] + rin[3][...]
        rstd = _rms(xv)
        n1 = xv * rstd
        outs[0][...] = rin[4][...] + _rms_bwd(dh * (p[0:1] * (1.0 + p[1:2])), n1, rstd)
        reds[0][0:1, :] += _colsum(dh * n1 * (1.0 + p[1:2]))
        reds[0][1:2, :] += _colsum(dh * n1 * p[0:1])
        reds[0][2:3, :] += _colsum(dh)

    return _rowwise(fn, [x, dh_a, dh_b, dh_c, dx1], [gmod], [(d, F32)], [(SUBLANES, d)], name="norm1_bwd")


RNN_CHUNK = 512


def _shift_down(ref, base, n, j):
    v = ref[pl.ds(base, n + SUBLANES), :]
    return v[SUBLANES:] if j == 0 else pltpu.roll(v, j, 0)[SUBLANES:]


def _shift_up(ref, base, n, j, top_pad):
    v = ref[pl.ds(base + top_pad, n + SUBLANES), :]
    return v[:n] if j == 0 else pltpu.roll(v, n + SUBLANES - j, 0)[:n]


def _one_minus_exp(z):
    series = -z * (1.0 + z * (0.5 + z * (1.0 / 6.0 + z * (1.0 / 24.0 + z * (1.0 / 120.0 + z * (1.0 / 720.0))))))
    return jnp.where(z > -0.1, series, 1.0 - jnp.exp(z))


def _softplus(v):
    return jnp.maximum(v, 0.0) + jnp.log(1.0 + jnp.exp(-jnp.abs(v)))


def _rnn_gates(xc, w, wa, wx, sp):
    xb = xc.astype(BF16)
    ra = _sigmoid(jnp.dot(xb, wa, preferred_element_type=F32) + w[5:6])
    ix = _sigmoid(jnp.dot(xb, wx, preferred_element_type=F32) + w[6:7])
    la = (-LRU_C) * ra * sp
    a = jnp.exp(la)
    mult = jnp.sqrt(_one_minus_exp(2.0 * la))
    return ra, ix, a, mult


def _rnn_fwd(x_rnn, keep, rp, wa_bd, wx_bd):
    s, r = x_rnn.shape
    ts = min(RNN_CHUNK, s)

    def body(x_ref, keep_ref, rp_ref, wa_ref, wx_ref, xc_ref, ra_ref, ix_ref, hs_ref, xpad, a_s, b_s):
        xpad[0:SUBLANES, :] = jnp.zeros((SUBLANES, LANES), F32)
        xpad[SUBLANES:, :] = x_ref[...]
        w = rp_ref[...]
        sp = _softplus(-w[7:8])
        wa, wx = wa_ref[0], wx_ref[0]

        def chunk(c, carry):
            base = pl.multiple_of(c * ts, ts)
            xc = w[4:5] + w[3:4] * _shift_down(xpad, base, ts, 0)
            for j in range(1, 4):
                xc = xc + w[3 - j:4 - j] * _shift_down(xpad, base, ts, j)
            ra, ix, a, mult = _rnn_gates(xc, w, wa, wx, sp)
            kp = keep_ref[pl.ds(base, ts), :]
            xc_ref[pl.ds(base, ts), :] = xc
            ra_ref[pl.ds(base, ts), :] = ra
            ix_ref[pl.ds(base, ts), :] = ix
            a_s[pl.ds(base, ts), :] = a * kp
            b_s[pl.ds(base, ts), :] = jnp.where(kp > 0.0, mult, 1.0) * (ix * xc)
            return carry

        lax.fori_loop(0, s // ts, chunk, 0)

        row = _row((SUBLANES, LANES))

        def tile(i, h):
            r0 = pl.multiple_of(i * SUBLANES, SUBLANES)
            a = a_s[pl.ds(r0, SUBLANES), :]
            b = b_s[pl.ds(r0, SUBLANES), :]
            for sh in (1, 2, 4):
                a_sh = jnp.where(row >= sh, pltpu.roll(a, sh, 0), 1.0)
                b_sh = jnp.where(row >= sh, pltpu.roll(b, sh, 0), 0.0)
                b = a * b_sh + b
                a = a * a_sh
            hv = b + a * h
            hs_ref[pl.ds(r0, SUBLANES), :] = hv
            return hv[SUBLANES - 1:SUBLANES, :]

        lax.fori_loop(0, s // SUBLANES, tile, jnp.zeros((1, LANES), F32), unroll=4)

    col = pl.BlockSpec((s, LANES), lambda g: (0, g))
    return pl.pallas_call(
        body, name="rnn_fwd", grid=(r // LANES,),
        in_specs=[col, pl.BlockSpec((s, 1), lambda g: (0, 0)), pl.BlockSpec((SUBLANES, LANES), lambda g: (0, g)),
                  pl.BlockSpec((1, LANES, LANES), lambda g: (g, 0, 0)), pl.BlockSpec((1, LANES, LANES), lambda g: (g, 0, 0))],
        out_specs=[col] * 4,
        out_shape=[jax.ShapeDtypeStruct((s, r), F32)] * 4,
        scratch_shapes=[pltpu.VMEM((s + SUBLANES, LANES), F32), pltpu.VMEM((s, LANES), F32), pltpu.VMEM((s, LANES), F32)],
        compiler_params=_cparams(("arbitrary",)),
    )(x_rnn, keep, rp, wa_bd, wx_bd)


def _rnn_bwd(x_rnn, xc, ra, ix, hs, dy, keep, rp, wa_bd, wx_bd):
    s, r = x_rnn.shape
    ts = min(RNN_CHUNK, s)
    nt = s // SUBLANES

    def body(x_ref, xc_ref, ra_ref, ix_ref, hs_ref, dy_ref, keep_ref, rp_ref, wa_ref, wx_ref,
             dx_ref, dwa_ref, dwx_ref, red_ref, xpad, hpad, a_s, dh_s, dxc_s):
        zero8 = jnp.zeros((SUBLANES, LANES), F32)
        xpad[0:SUBLANES, :] = zero8
        xpad[SUBLANES:, :] = x_ref[...]
        hpad[0:SUBLANES, :] = zero8
        hpad[SUBLANES:, :] = hs_ref[...]
        a_s[s:, :] = zero8
        dxc_s[s:, :] = zero8
        w = rp_ref[...]
        sp = _softplus(-w[7:8])
        wa, wx = wa_ref[0], wx_ref[0]

        def decay(c, carry):
            base = pl.multiple_of(c * ts, ts)
            a = jnp.exp((-LRU_C) * ra_ref[pl.ds(base, ts), :] * sp)
            a_s[pl.ds(base, ts), :] = a * keep_ref[pl.ds(base, ts), :]
            return carry

        lax.fori_loop(0, s // ts, decay, 0)

        row = _row((SUBLANES, LANES))

        def tile(n, nxt):
            i = nt - 1 - n
            r0 = pl.multiple_of(i * SUBLANES, SUBLANES)
            a_here = a_s[pl.ds(r0, SUBLANES), :]
            a_next = a_s[pl.ds(r0 + SUBLANES, SUBLANES), :]
            a = jnp.where(row < SUBLANES - 1, pltpu.roll(a_here, SUBLANES - 1, 0), a_next[0:1, :])
            b = dy_ref[pl.ds(r0, SUBLANES), :]
            for sh in (1, 2, 4):
                a_sh = jnp.where(row < SUBLANES - sh, pltpu.roll(a, SUBLANES - sh, 0), 1.0)
                b_sh = jnp.where(row < SUBLANES - sh, pltpu.roll(b, SUBLANES - sh, 0), 0.0)
                b = a * b_sh + b
                a = a * a_sh
            dh = b + a * nxt
            dh_s[pl.ds(r0, SUBLANES), :] = dh
            return dh[0:1, :]

        lax.fori_loop(0, nt, tile, jnp.zeros((1, LANES), F32), unroll=4)

        def gates(c, carry):
            dwa, dwx, d_ba, d_bx, d_sp, d_cb = carry
            base = pl.multiple_of(c * ts, ts)
            xcv = xc_ref[pl.ds(base, ts), :]
            rav = ra_ref[pl.ds(base, ts), :]
            ixv = ix_ref[pl.ds(base, ts), :]
            kp = keep_ref[pl.ds(base, ts), :]
            dh = dh_s[pl.ds(base, ts), :]
            h_prev = _shift_down(hpad, base, ts, 1)
            la = (-LRU_C) * rav * sp
            a = jnp.exp(la)
            mult = jnp.sqrt(_one_minus_exp(2.0 * la))
            mult_eff = jnp.where(kp > 0.0, mult, 1.0)
            d_a = dh * h_prev * kp
            d_mult = dh * (ixv * xcv) * kp
            d_ix = dh * mult_eff * xcv
            d_xc = dh * mult_eff * ixv
            d_la = d_a * a - d_mult * (a * a) / mult
            d_pa = d_la * ((-LRU_C) * sp) * rav * (1.0 - rav)
            d_px = d_ix * ixv * (1.0 - ixv)
            xb = xcv.astype(BF16)
            pab, pxb = d_pa.astype(BF16), d_px.astype(BF16)
            tn = (((0,), (0,)), ((), ()))
            nt_ = (((1,), (1,)), ((), ()))
            dwa = dwa + lax.dot_general(xb, pab, tn, preferred_element_type=F32)
            dwx = dwx + lax.dot_general(xb, pxb, tn, preferred_element_type=F32)
            d_xc = d_xc + lax.dot_general(pab, wa, nt_, preferred_element_type=F32)
            d_xc = d_xc + lax.dot_general(pxb, wx, nt_, preferred_element_type=F32)
            dxc_s[pl.ds(base, ts), :] = d_xc
            return (dwa, dwx, d_ba + _colsum(d_pa), d_bx + _colsum(d_px),
                    d_sp + _colsum(d_la * ((-LRU_C) * rav)), d_cb + _colsum(d_xc))

        z1 = jnp.zeros((1, LANES), F32)
        zw = jnp.zeros((LANES, LANES), F32)
        dwa, dwx, d_ba, d_bx, d_sp, d_cb = lax.fori_loop(0, s // ts, gates, (zw, zw, z1, z1, z1, z1))
        dwa_ref[0] = dwa
        dwx_ref[0] = dwx

        def conv(c, carry):
            base = pl.multiple_of(c * ts, ts)
            d_here = dxc_s[pl.ds(base, ts), :]
            dx = w[3:4] * d_here
            for j in range(1, 4):
                dx = dx + w[3 - j:4 - j] * _shift_up(dxc_s, base, ts, j, 0)
            dx_ref[pl.ds(base, ts), :] = dx.astype(BF16)
            return tuple(carry[k] + _colsum(d_here * _shift_down(xpad, base, ts, 3 - k)) for k in range(4))

        d_w = lax.fori_loop(0, s // ts, conv, (z1, z1, z1, z1))
        d_lru = d_sp * (-_sigmoid(-w[7:8]))
        red_ref[...] = jnp.concatenate(list(d_w) + [d_cb, d_ba, d_bx, d_lru], axis=0)

    col = pl.BlockSpec((s, LANES), lambda g: (0, g))
    par = pl.BlockSpec((SUBLANES, LANES), lambda g: (0, g))
    wsp = pl.BlockSpec((1, LANES, LANES), lambda g: (g, 0, 0))
    return pl.pallas_call(
        body, name="rnn_bwd", grid=(r // LANES,),
        in_specs=[col] * 6 + [pl.BlockSpec((s, 1), lambda g: (0, 0)), par, wsp, wsp],
        out_specs=[col, wsp, wsp, par],
        out_shape=[jax.ShapeDtypeStruct((s, r), BF16), jax.ShapeDtypeStruct((r // LANES, LANES, LANES), F32),
                   jax.ShapeDtypeStruct((r // LANES, LANES, LANES), F32), jax.ShapeDtypeStruct((SUBLANES, r), F32)],
        scratch_shapes=[pltpu.VMEM((s + SUBLANES, LANES), F32), pltpu.VMEM((s + SUBLANES, LANES), F32),
                        pltpu.VMEM((s + SUBLANES, LANES), F32), pltpu.VMEM((s, LANES), F32),
                        pltpu.VMEM((s + SUBLANES, LANES), F32)],
        compiler_params=_cparams(("arbitrary",)),
    )(x_rnn, xc, ra, ix, hs, dy, keep, rp, wa_bd, wx_bd)


ATT_BLOCK = 512
NT_DIMS = (((1,), (1,)), ((), ()))
TN_DIMS = (((0,), (0,)), ((), ()))


LOG2E = 1.4426950408889634
LN2 = 0.6931471805599453
Q_PRESCALE = ATT_SCALE * LOG2E


def _att_scores(q, kvb, krb, diagonal):
    k_eff = jnp.where(_lane(kvb.shape) < QK_NOPE, kvb, krb)
    sc = lax.dot_general(q, k_eff, NT_DIMS, preferred_element_type=F32)
    if diagonal:
        sc = jnp.where(lax.broadcasted_iota(jnp.int32, sc.shape, 1) <= lax.broadcasted_iota(jnp.int32, sc.shape, 0), sc, -jnp.inf)
    return sc, k_eff


def _att_fwd(q, kv, kr):
    s = q.shape[0]
    t = min(ATT_BLOCK, s)
    nb = s // t

    def body(q_ref, kv_ref, kr_ref, y_ref, lse_ref, m_s, acc_s):
        i, j = pl.program_id(1), pl.program_id(2)

        @pl.when(j == 0)
        def _():
            m_s[...] = jnp.full(m_s.shape, -jnp.inf, F32)
            acc_s[...] = jnp.zeros(acc_s.shape, F32)

        def step(diagonal):
            krb = kr_ref[...]
            lane = _lane((t, LANES))
            for hh in range(2):
                sl = slice(hh * LANES, (hh + 1) * LANES)
                kvb = kv_ref[:, sl]
                sc, _ = _att_scores(q_ref[:, sl], kvb, krb, diagonal)
                m_prev = m_s[hh]
                m_new = jnp.maximum(m_prev, jnp.max(sc, axis=-1, keepdims=True))
                alpha = jnp.exp2(m_prev - m_new)
                p = jnp.exp2(sc - m_new[:, 0:1]).astype(BF16)
                ones_v = jnp.where(lane < QK_NOPE, jnp.ones_like(kvb), kvb)
                acc_s[hh] = alpha * acc_s[hh] + jnp.dot(p, ones_v, preferred_element_type=F32)
                m_s[hh] = m_new

        @pl.when(j < i)
        def _():
            step(False)

        @pl.when(j == i)
        def _():
            step(True)
            lane = _lane((t, LANES))
            a0, a1 = acc_s[0], acc_s[1]
            l0, l1 = a0[:, 0:1], a1[:, 0:1]
            y_ref[...] = jnp.where(lane < V_HEAD, pltpu.roll(a0 / l0, V_HEAD, 1), a1 / l1).astype(BF16)
            lse_ref[...] = jnp.where(lane < V_HEAD, m_s[0] + jnp.log(l0) * LOG2E, m_s[1] + jnp.log(l1) * LOG2E)

    return pl.pallas_call(
        body, name="att_fwd", grid=(N_HEADS // 2, nb, nb),
        in_specs=[pl.BlockSpec((t, 2 * LANES), lambda p, i, j: (i, p)),
                  pl.BlockSpec((t, 2 * LANES), lambda p, i, j: (jnp.minimum(j, i), p)),
                  pl.BlockSpec((t, LANES), lambda p, i, j: (jnp.minimum(j, i), 0))],
        out_specs=[pl.BlockSpec((t, LANES), lambda p, i, j: (i, p))] * 2,
        out_shape=[jax.ShapeDtypeStruct((s, N_HEADS * V_HEAD), BF16), jax.ShapeDtypeStruct((s, N_HEADS * V_HEAD), F32)],
        scratch_shapes=[pltpu.VMEM((2, t, LANES), F32)] * 2,
        compiler_params=_cparams(("arbitrary", "arbitrary", "arbitrary")),
    )(q, kv, kr)


def _att_bwd(q, kv, kr, y, lse, dy):
    s = q.shape[0]
    t = min(ATT_BLOCK, s)
    nb = s // t

    def body(q_ref, kv_ref, kr_ref, y_ref, lse_ref, dy_ref, dq_ref, dkv_ref, dkr_ref, dkv_s):
        p_, j, i = pl.program_id(0), pl.program_id(1), pl.program_id(2)

        @pl.when((p_ == 0) & (j == 0) & (i == 0))
        def _():
            dkr_ref[...] = jnp.zeros(dkr_ref.shape, F32)

        @pl.when((j == 0) & (i == 0))
        def _():
            dq_ref[...] = jnp.zeros(dq_ref.shape, F32)

        @pl.when(i == 0)
        def _():
            dkv_s[...] = jnp.zeros(dkv_s.shape, F32)

        def step(diagonal):
            lane = _lane((t, LANES))
            krb = kr_ref[...]
            dyv = dy_ref[...]
            yv = y_ref[...].astype(F32)
            lsev = lse_ref[...]
            rows = pl.ds(pl.multiple_of(i * t, t), t)
            dkr_acc = jnp.zeros((t, LANES), F32)
            for hh in range(2):
                sl = slice(hh * LANES, (hh + 1) * LANES)
                qb, kvb = q_ref[:, sl], kv_ref[:, sl]
                sc, k_eff = _att_scores(qb, kvb, krb, diagonal)
                if hh == 0:
                    do_pad = jnp.where(lane >= V_HEAD, pltpu.roll(dyv, V_HEAD, 1), 0.0)
                    o_pad = jnp.where(lane >= V_HEAD, pltpu.roll(yv, V_HEAD, 1), 0.0)
                    lse_h = lsev[:, 0:1]
                else:
                    do_pad = jnp.where(lane >= V_HEAD, dyv, 0.0)
                    o_pad = jnp.where(lane >= V_HEAD, yv, 0.0)
                    lse_h = lsev[:, LANES - 1:LANES]
                do_ln2 = do_pad * LN2
                delta = jnp.sum(do_ln2 * o_pad, axis=-1, keepdims=True)
                prob = jnp.exp2(sc - lse_h)
                dv = lax.dot_general(prob.astype(BF16), do_pad.astype(BF16), TN_DIMS, preferred_element_type=F32)
                dp = lax.dot_general(do_ln2.astype(BF16), kvb, NT_DIMS, preferred_element_type=F32)
                ds = (prob * (dp - delta)).astype(BF16)
                dq_ref[rows, sl] += jnp.dot(ds, k_eff, preferred_element_type=F32)
                dk_eff = lax.dot_general(ds, qb, TN_DIMS, preferred_element_type=F32)
                dkv_s[hh] += dv + jnp.where(lane < QK_NOPE, dk_eff, 0.0)
                dkr_acc = dkr_acc + jnp.where(lane >= QK_NOPE, dk_eff, 0.0)
            dkr_ref[pl.ds(pl.multiple_of(j * t, t), t), :] += dkr_acc

        @pl.when(i > j)
        def _():
            step(False)

        @pl.when(i == j)
        def _():
            step(True)

        @pl.when(i == nb - 1)
        def _():
            dkv_ref[:, 0:LANES] = dkv_s[0].astype(BF16)
            dkv_ref[:, LANES:] = dkv_s[1].astype(BF16)

    qi = lambda p, j, i: (jnp.maximum(i, j), p)
    return pl.pallas_call(
        body, name="att_bwd", grid=(N_HEADS // 2, nb, nb),
        in_specs=[pl.BlockSpec((t, 2 * LANES), qi),
                  pl.BlockSpec((t, 2 * LANES), lambda p, j, i: (j, p)),
                  pl.BlockSpec((t, LANES), lambda p, j, i: (j, 0)),
                  pl.BlockSpec((t, LANES), qi), pl.BlockSpec((t, LANES), qi), pl.BlockSpec((t, LANES), qi)],
        out_specs=[pl.BlockSpec((s, 2 * LANES), lambda p, j, i: (0, p)),
                   pl.BlockSpec((t, 2 * LANES), lambda p, j, i: (j, p)),
                   pl.BlockSpec((s, LANES), lambda p, j, i: (0, 0))],
        out_shape=[jax.ShapeDtypeStruct((s, N_HEADS * LANES), F32), jax.ShapeDtypeStruct((s, N_HEADS * LANES), BF16),
                   jax.ShapeDtypeStruct((s, LANES), F32)],
        scratch_shapes=[pltpu.VMEM((2, t, LANES), F32)],
        compiler_params=_cparams(("arbitrary", "arbitrary", "arbitrary")),
    )(q, kv, kr, y, lse, dy)


FFN_COLS = 256


def _ffn_conv(pad_ref, w, base, n):
    u = w[3:4] + w[2:3] * _shift_down(pad_ref, base, n, 0)
    for j in range(1, 3):
        u = u + w[2 - j:3 - j] * _shift_down(pad_ref, base, n, j)
    return u


def _ffn_act_fwd(up, fp):
    s, f2 = up.shape
    f = f2 // 2
    tc = FFN_COLS
    ts = min(RNN_CHUNK, s)
    nfb = f // tc

    def body(ug_ref, uv_ref, wg_ref, wv_ref, act_ref, gpad, vpad):
        zero8 = jnp.zeros((SUBLANES, tc), F32)
        gpad[0:SUBLANES, :] = zero8
        vpad[0:SUBLANES, :] = zero8
        gpad[SUBLANES:, :] = ug_ref[...]
        vpad[SUBLANES:, :] = uv_ref[...]
        wg, wv = wg_ref[...], wv_ref[...]

        def chunk(c, carry):
            base = pl.multiple_of(c * ts, ts)
            g = _ffn_conv(gpad, wg, base, ts)
            v = _ffn_conv(vpad, wv, base, ts)
            act_ref[pl.ds(base, ts), :] = (g * _sigmoid(g) * v).astype(BF16)
            return carry

        lax.fori_loop(0, s // ts, chunk, 0)

    return pl.pallas_call(
        body, name="ffn_act_fwd", grid=(nfb,),
        in_specs=[pl.BlockSpec((s, tc), lambda b: (0, b)), pl.BlockSpec((s, tc), lambda b: (0, b + nfb)),
                  pl.BlockSpec((SUBLANES, tc), lambda b: (0, b)), pl.BlockSpec((SUBLANES, tc), lambda b: (0, b + nfb))],
        out_specs=pl.BlockSpec((s, tc), lambda b: (0, b)),
        out_shape=jax.ShapeDtypeStruct((s, f), BF16),
        scratch_shapes=[pltpu.VMEM((s + SUBLANES, tc), F32)] * 2,
        compiler_params=_cparams(("arbitrary",)),
    )(up, up, fp, fp)


def _ffn_act_bwd(up, dact, fp):
    s, f2 = up.shape
    f = f2 // 2
    tc = FFN_COLS
    ts = min(RNN_CHUNK, s)
    nfb = f // tc

    def body(ug_ref, uv_ref, da_ref, wg_ref, wv_ref, dup_ref, red_ref, gpad, vpad, dgs, dvs):
        half = pl.program_id(1)
        wg, wv = wg_ref[...], wv_ref[...]

        @pl.when(half == 0)
        def _():
            zero8 = jnp.zeros((SUBLANES, tc), F32)
            gpad[0:SUBLANES, :] = zero8
            vpad[0:SUBLANES, :] = zero8
            gpad[SUBLANES:, :] = ug_ref[...]
            vpad[SUBLANES:, :] = uv_ref[...]
            dgs[s:, :] = zero8
            dvs[s:, :] = zero8

            def act(c, carry):
                base = pl.multiple_of(c * ts, ts)
                g = _ffn_conv(gpad, wg, base, ts)
                v = _ffn_conv(vpad, wv, base, ts)
                da = da_ref[pl.ds(base, ts), :]
                sg = _sigmoid(g)
                dgs[pl.ds(base, ts), :] = da * v * (sg * (1.0 + g * (1.0 - sg)))
                dvs[pl.ds(base, ts), :] = da * (g * sg)
                return carry

            lax.fori_loop(0, s // ts, act, 0)

        def conv_t(src, pad, w, out_ref, red_ref):
            def chunk(c, carry):
                base = pl.multiple_of(c * ts, ts)
                d_here = src[pl.ds(base, ts), :]
                dx = w[2:3] * d_here
                for j in range(1, 3):
                    dx = dx + w[2 - j:3 - j] * _shift_up(src, base, ts, j, 0)
                out_ref[pl.ds(base, ts), :] = dx.astype(BF16)
                taps = tuple(carry[k] + _colsum(d_here * _shift_down(pad, base, ts, 2 - k)) for k in range(3))
                return taps + (carry[3] + _colsum(d_here),)

            z1 = jnp.zeros((1, tc), F32)
            red = lax.fori_loop(0, s // ts, chunk, (z1, z1, z1, z1))
            red_ref[...] = jnp.concatenate(list(red) + [jnp.zeros((4, tc), F32)], axis=0)

        @pl.when(half == 0)
        def _():
            conv_t(dgs, gpad, wg, dup_ref, red_ref)

        @pl.when(half == 1)
        def _():
            conv_t(dvs, vpad, wv, dup_ref, red_ref)

    gcol = pl.BlockSpec((s, tc), lambda b, h: (0, b))
    vcol = pl.BlockSpec((s, tc), lambda b, h: (0, b + nfb))
    gpar = pl.BlockSpec((SUBLANES, tc), lambda b, h: (0, b))
    vpar = pl.BlockSpec((SUBLANES, tc), lambda b, h: (0, b + nfb))
    return pl.pallas_call(
        body, name="ffn_act_bwd", grid=(nfb, 2),
        in_specs=[gcol, vcol, gcol, gpar, vpar],
        out_specs=[pl.BlockSpec((s, tc), lambda b, h: (0, b + h * nfb)),
                   pl.BlockSpec((SUBLANES, tc), lambda b, h: (0, b + h * nfb))],
        out_shape=[jax.ShapeDtypeStruct((s, f2), BF16), jax.ShapeDtypeStruct((SUBLANES, f2), F32)],
        scratch_shapes=[pltpu.VMEM((s + SUBLANES, tc), F32)] * 4,
        compiler_params=_cparams(("arbitrary", "arbitrary")),
    )(up, up, dact, fp, fp)


def _rows8(rows, width):
    rows = [r.reshape(1, width).astype(F32) for r in rows]
    return jnp.concatenate(rows + [jnp.zeros((SUBLANES - len(rows), width), F32)], axis=0)


def _block_diag(w):
    n, b, _ = w.shape
    w = w.reshape(n // 2, 2, b, b)
    z = jnp.zeros((n // 2, b, b), w.dtype)
    top = jnp.concatenate([w[:, 0], z], axis=2)
    bot = jnp.concatenate([z, w[:, 1]], axis=2)
    return jnp.concatenate([top, bot], axis=1)


def _block_diag_t(bd):
    n, b2, _ = bd.shape
    b = b2 // 2
    return jnp.stack([bd[:, :b, :b], bd[:, b:, b:]], axis=1).reshape(2 * n, b, b)


def _local_step(x, mod, positions, target, w_in, fetch, sm, emit):
    s, d = x.shape
    o_rnn, o_mla = D_RNN, D_RNN + Q_LORA + KV_LORA + QK_ROPE
    wts = {}
    w_in_rnn = w_in[:, :o_rnn]
    w_in_mla = jnp.concatenate([w_in[:, o_rnn:o_mla], jnp.zeros((d, MLA_W - (o_mla - o_rnn)), w_in.dtype)], axis=1)
    w_in_g = w_in[:, o_mla:]
    hd = QK_NOPE + QK_ROPE
    wa_bd = _block_diag(sm["w_gate_a"]).astype(BF16)
    wx_bd = _block_diag(sm["w_gate_x"]).astype(BF16)

    pos = positions.reshape(s)
    half = QK_ROPE // 2
    inv_freq = ROPE_THETA ** (-jnp.arange(half, dtype=F32) / half)
    ang = pos.astype(F32)[:, None] * inv_freq
    cos, sin = jnp.cos(ang), jnp.sin(ang)
    rot_c = jnp.concatenate([jnp.ones((s, QK_NOPE), F32), cos, cos, jnp.ones((s, LANES - hd), F32)], axis=1)
    rot_s = jnp.concatenate([jnp.zeros((s, QK_NOPE), F32), -sin, sin, jnp.zeros((s, LANES - hd), F32)], axis=1)
    keep = (pos != 0).astype(F32).reshape(s, 1)

    gmod1 = _rows8([sm["norm1_g"], mod[1], mod[0]], d)
    gmod2 = _rows8([sm["norm2_g"], mod[4], mod[3], mod[2]], d)
    rp = jnp.concatenate([sm["conv_w"].reshape(4, D_RNN), _rows8([sm["conv_b"], sm["b_gate_a"], sm["b_gate_x"], sm["lru_param"]], D_RNN)[:4]], axis=0)
    fp = _rows8([sm["ffn_conv_w"][0], sm["ffn_conv_w"][1], sm["ffn_conv_w"][2], sm["ffn_conv_b"]], 2 * D_FF)
    ng = _rows8([jnp.concatenate([sm["q_norm_g"].reshape(-1), sm["kv_norm_g"].reshape(-1), jnp.zeros((MLA_W - Q_LORA - KV_LORA,), F32)])], MLA_W)
    fpar = _rows8([mod[5], sm["final_g"]], d)

    h = _norm_mod_fwd(x, gmod1, "norm1_fwd")
    proj_rnn = _mm(h, w_in_rnn, name="mm_in_rnn")
    proj_mla = _mm(h, w_in_mla, name="mm_in_mla")
    proj_g = _mm(h, w_in_g, name="mm_in_g")
    xc, ra, ix, hs = _rnn_fwd(proj_rnn, keep, rp, wa_bd, wx_bd)
    qn, kvn, kr = _mla_prep_fwd(proj_mla, rot_c, rot_s, ng)
    wts.update(fetch(("w_uq", "w_ukv"), kr))
    w_uq_p = jnp.pad(wts["w_uq"].reshape(Q_LORA, N_HEADS, hd), ((0, 0), (0, 0), (0, LANES - hd))).reshape(Q_LORA, N_HEADS * LANES)
    w_ukv = wts["w_ukv"]
    q_rot = _rope_heads(_mm(qn, w_uq_p, name="mm_uq"), rot_c, rot_s, False, "rope_fwd")
    kv = _mm(kvn, w_ukv, out_dtype=BF16, name="mm_ukv")
    y_mla, lse = _att_fwd(q_rot, kv, kr)
    wts.update(fetch(("w_proj_rnn", "w_proj_mla", "w_out", "w_up", "w_down"), lse))
    pr = _mm(hs, wts["w_proj_rnn"], name="mm_proj_rnn")
    pm = _mm(y_mla, wts["w_proj_mla"], name="mm_proj_mla")
    merged = _merge_fwd(pr, pm, proj_g)
    o = _mm(merged, wts["w_out"], name="mm_out")
    x1, h2 = _resid_norm_fwd(x, o, gmod2)
    up = _mm(h2, wts["w_up"], name="mm_up")
    act = _ffn_act_fwd(up, fp)
    dn = _mm(act, wts["w_down"], name="mm_down")

    dx2, ddn, red_f = _final_fwd_bwd(x1, dn, target, fpar)
    dact = _mm(ddn, wts["w_down"], tb=True, name="mm_d_act")
    tok = emit("w_down", _mm(act, ddn, ta=True, out_dtype=BF16, name="mm_dw_down"))
    dup, red_ffn = _ffn_act_bwd(up, dact, fp + tok)
    dh2 = _mm(dup, wts["w_up"], tb=True, name="mm_d_h2")
    tok = tok + emit("w_up", _mm(h2, dup, ta=True, out_dtype=BF16, name="mm_dw_up"))
    dx1, do, red_2 = _norm2_bwd(x1, dh2, dx2, o, gmod2 + tok)
    dmerged = _mm(do, wts["w_out"], tb=True, name="mm_d_merged")
    tok = tok + emit("w_out", _mm(merged, do, ta=True, out_dtype=BF16, name="mm_dw_out"))
    dpr, dpm, dg = _merge_bwd(dmerged, pr, pm, proj_g)
    dy_rnn = _mm(dpr, wts["w_proj_rnn"], tb=True, name="mm_d_yrnn")
    tok = tok + emit("w_proj_rnn", _mm(hs, dpr, ta=True, out_dtype=BF16, name="mm_dw_proj_rnn"))
    dy_mla = _mm(dpm, wts["w_proj_mla"], tb=True, name="mm_d_ymla")
    tok = tok + emit("w_proj_mla", _mm(y_mla, dpm, ta=True, out_dtype=BF16, name="mm_dw_proj_mla"))
    dq_rot, dkv, dkr = _att_bwd(q_rot, kv, kr, y_mla, lse, dy_mla)
    dq = _rope_heads(dq_rot, rot_c, rot_s, True, "rope_bwd")
    dqn = _mm(dq, w_uq_p, tb=True, name="mm_d_qn")
    dw_uq_p = _mm(qn, dq, ta=True, out_dtype=BF16, name="mm_dw_uq")
    tok = tok + emit("w_uq", dw_uq_p.reshape(Q_LORA, N_HEADS, LANES)[:, :, :hd].reshape(Q_LORA, N_HEADS * hd))
    dkvn = _mm(dkv, w_ukv, tb=True, name="mm_d_kvn")
    tok = tok + emit("w_ukv", _mm(kvn, dkv, ta=True, out_dtype=BF16, name="mm_dw_ukv"))
    dproj_mla, red_m = _mla_prep_bwd(proj_mla, dqn, dkvn, dkr, rot_c, rot_s, ng + tok)
    dx_rnn, dwa_bd, dwx_bd, red_r = _rnn_bwd(proj_rnn, xc, ra, ix, hs, dy_rnn, keep, rp + tok, wa_bd, wx_bd)
    dw_in = jnp.concatenate([
        _mm(h, dx_rnn, ta=True, out_dtype=BF16, name="mm_dw_in_rnn"),
        _mm(h, dproj_mla, ta=True, out_dtype=BF16, name="mm_dw_in_mla")[:, :o_mla - o_rnn],
        _mm(h, dg, ta=True, out_dtype=BF16, name="mm_dw_in_g")], axis=1)
    tok = tok + emit("w_in", dw_in)
    dh_a = _mm(dx_rnn, w_in_rnn, tb=True, name="mm_d_h_rnn")
    dh_b = _mm(dproj_mla, w_in_mla, tb=True, name="mm_d_h_mla")
    dh_c = _mm(dg, w_in_g, tb=True, name="mm_d_h_g")
    grad_x, red_1 = _norm1_bwd(x, dh_a, dh_b, dh_c, dx1, gmod1 + tok)

    gs = {
        "norm1_g": red_1[0], "conv_w": red_r[0:4], "conv_b": red_r[4], "w_gate_a": _block_diag_t(dwa_bd),
        "b_gate_a": red_r[5], "w_gate_x": _block_diag_t(dwx_bd), "b_gate_x": red_r[6], "lru_param": red_r[7],
        "q_norm_g": red_m[0, :Q_LORA], "kv_norm_g": red_m[0, Q_LORA:Q_LORA + KV_LORA], "norm2_g": red_2[0],
        "ffn_conv_w": red_ffn[0:3], "ffn_conv_b": red_ffn[3], "final_g": red_f[0],
    }
    dmod = jnp.stack([red_1[2], red_1[1], red_2[3], red_2[2], red_2[1], red_f[1]], axis=0)
    return red_f[2, 0], grad_x, gs, dmod


MESH_IDS = pl.DeviceIdType.MESH
HBM_SPEC = pl.BlockSpec(memory_space=pltpu.HBM)


def _my_slot():
    return 4 * lax.axis_index("x") + 2 * lax.axis_index("y") + lax.axis_index("c")


def _all_gather(arrs, name):
    n = len(arrs)

    def body(*refs):
        ins, outs = refs[:n], refs[n:2 * n]
        send_sems, recv_sems, local_sems = refs[2 * n:]
        x, y, c = lax.axis_index("x"), lax.axis_index("y"), lax.axis_index("c")
        me, sibling = (x, y, c), (x, y, 1 - c)
        chips = [(1 - x, y), (x, 1 - y), (1 - x, 1 - y)]

        def slot(dev):
            return 4 * dev[0] + 2 * dev[1] + dev[2]

        def copy(a, k, block, to, src=None):
            dst = outs[a].at[slot(block)]
            return pltpu.make_async_remote_copy(
                src_ref=dst if src is None else src, dst_ref=dst, send_sem=send_sems.at[a, k], recv_sem=recv_sems.at[a, k],
                device_id=to, device_id_type=MESH_IDS)

        mine = [pltpu.make_async_copy(ins[a], outs[a].at[slot(me)], local_sems.at[a]) for a in range(n)]
        for cp in mine:
            cp.start()
        first = []
        for a in range(n):
            first.append(copy(a, 0, me, sibling, src=ins[a]))
            first += [copy(a, 1 + j, me, (*chip, c), src=ins[a]) for j, chip in enumerate(chips)]
        for cp in first:
            cp.start()
        passed = []
        for j, chip in enumerate(chips):
            for a in range(n):
                copy(a, 1 + j, (*chip, c), me).wait_recv()
                fwd = copy(a, 4 + j, (*chip, c), sibling)
                fwd.start()
                passed.append(fwd)
        for a in range(n):
            copy(a, 0, sibling, me).wait_recv()
            for j, chip in enumerate(chips):
                copy(a, 4 + j, (*chip, 1 - c), me).wait_recv()
        for cp in first + passed:
            cp.wait_send()
        for cp in mine:
            cp.wait()

    return pl.pallas_call(
        body, name=name,
        in_specs=[HBM_SPEC] * n, out_specs=[HBM_SPEC] * n,
        out_shape=[jax.ShapeDtypeStruct((N_DEV,) + a.shape, a.dtype) for a in arrs],
        scratch_shapes=[pltpu.SemaphoreType.DMA((n, 7)), pltpu.SemaphoreType.DMA((n, 7)), pltpu.SemaphoreType.DMA((n,))],
    )(*arrs)


SEM_SPEC =pl.BlockSpec(memory_space=pltpu.SEMAPHORE)
DATAFLOW = pltpu.SideEffectType.DATAFLOW_SIDE_EFFECTING
FLIPS = [(dx, dy, dc) for dx in (0, 1) for dy in (0, 1) for dc in (0, 1)][1:]


def _peer(k):
    dx, dy, dc = FLIPS[k]
    peer = (lax.axis_index("x") ^ dx, lax.axis_index("y") ^ dy, lax.axis_index("c") ^ dc)
    return peer, 4 * peer[0] + 2 * peer[1] + peer[2]


def _gather_start(shards, after, name):
    n, nf = len(shards), len(FLIPS)

    def body(*refs):
        srcs, lands = refs[:n], refs[n:2 * n]
        send_sems, recv_sems = refs[2 * n + 1:3 * n + 1], refs[3 * n + 1:4 * n + 1]
        token = refs[-1]
        me = _my_slot()
        for a in range(n):
            for k in range(nf):
                peer, _ = _peer(k)
                pltpu.make_async_remote_copy(
                    src_ref=srcs[a], dst_ref=lands[a].at[me], send_sem=send_sems[a].at[k], recv_sem=recv_sems[a].at[k],
                    device_id=peer, device_id_type=MESH_IDS).start()
        token[...] = jnp.zeros(token.shape, F32)

    land_shapes = [(N_DEV,) + a.shape for a in shards]
    sems = [pltpu.SemaphoreType.DMA((nf,))] * n
    out = pl.pallas_call(
        body, name=name,
        out_shape=(*sems, *sems, *[pltpu.HBM(a.shape, a.dtype) for a in shards],
                   *[pltpu.HBM(shp, a.dtype) for shp, a in zip(land_shapes, shards)],
                   jax.ShapeDtypeStruct((SUBLANES, LANES), F32)),
        in_specs=[HBM_SPEC] * (2 * n) + [pl.BlockSpec(memory_space=pl.ANY)],
        out_specs=(*[SEM_SPEC] * (2 * n), *[HBM_SPEC] * (2 * n), pl.BlockSpec(memory_space=pltpu.VMEM)),
        input_output_aliases={i: 2 * n + i for i in range(2 * n)},
        compiler_params=pltpu.CompilerParams(has_side_effects=DATAFLOW),
    )(*[pltpu.with_memory_space_constraint(a, pltpu.HBM) for a in shards],
      *[pltpu.with_memory_space_constraint(lax.empty(shp, a.dtype), pltpu.HBM) for shp, a in zip(land_shapes, shards)],
      after)
    return [(out[a], out[n + a], out[2 * n + a], out[3 * n + a]) for a in range(n)], out[-1]


def _gather_wait(flights, after, name):
    n, nf = len(flights), len(FLIPS)

    def body(*refs):
        send_sems, recv_sems = refs[:n], refs[n:2 * n]
        srcs, lands = refs[2 * n:3 * n], refs[3 * n:4 * n]
        for a in range(n):
            for k in range(nf):
                peer, peer_slot = _peer(k)
                cp = pltpu.make_async_remote_copy(
                    src_ref=srcs[a], dst_ref=lands[a].at[peer_slot], send_sem=send_sems[a].at[k],
                    recv_sem=recv_sems[a].at[k], device_id=peer, device_id_type=MESH_IDS)
                cp.wait_send()
                cp.wait_recv()

    srcs, lands = [f[2] for f in flights], [f[3] for f in flights]
    out = pl.pallas_call(
        body, name=name,
        out_shape=(*[pltpu.HBM(a.shape, a.dtype) for a in srcs], *[pltpu.HBM(a.shape, a.dtype) for a in lands]),
        in_specs=[SEM_SPEC] * (2 * n) + [HBM_SPEC] * (2 * n) + [pl.BlockSpec(memory_space=pl.ANY)],
        out_specs=tuple([HBM_SPEC] * (2 * n)),
        input_output_aliases={2 * n + i: i for i in range(2 * n)},
        compiler_params=pltpu.CompilerParams(has_side_effects=DATAFLOW),
    )(*[f[0] for f in flights], *[f[1] for f in flights], *srcs, *lands, after)
    return list(out[n:])


def _scatter_start(chunks, name):
    def body(src_ref, land_ref, send_sems, recv_sems, src_thru, land_thru, token):
        me = _my_slot()
        for k in range(len(FLIPS)):
            peer, peer_slot = _peer(k)
            pltpu.make_async_remote_copy(
                src_ref=src_ref.at[peer_slot], dst_ref=land_ref.at[me], send_sem=send_sems.at[k], recv_sem=recv_sems.at[k],
                device_id=peer, device_id_type=MESH_IDS).start()
        token[...] = jnp.zeros(token.shape, F32)

    n = len(FLIPS)
    hbm = pltpu.HBM(chunks.shape, chunks.dtype)
    return pl.pallas_call(
        body, name=name,
        out_shape=(pltpu.SemaphoreType.DMA((n,)), pltpu.SemaphoreType.DMA((n,)), hbm, hbm,
                   jax.ShapeDtypeStruct((SUBLANES, LANES), F32)),
        in_specs=(HBM_SPEC, HBM_SPEC),
        out_specs=(SEM_SPEC, SEM_SPEC, HBM_SPEC, HBM_SPEC, pl.BlockSpec(memory_space=pltpu.VMEM)),
        input_output_aliases={0: 2, 1: 3},
        compiler_params=pltpu.CompilerParams(has_side_effects=DATAFLOW),
    )(pltpu.with_memory_space_constraint(chunks, pltpu.HBM),
      pltpu.with_memory_space_constraint(lax.empty(chunks.shape, chunks.dtype), pltpu.HBM))


def _scatter_wait(send_sems, recv_sems, src_thru, land_thru, after, name):
    def body(src_ref, land_ref, send_sems, recv_sems, after_ref, src_dead, got_ref):
        for k in range(len(FLIPS)):
            peer, peer_slot = _peer(k)
            cp = pltpu.make_async_remote_copy(
                src_ref=src_ref.at[peer_slot], dst_ref=land_ref.at[peer_slot], send_sem=send_sems.at[k],
                recv_sem=recv_sems.at[k], device_id=peer, device_id_type=MESH_IDS)
            cp.wait_send()
            cp.wait_recv()

    hbm = pltpu.HBM(src_thru.shape, src_thru.dtype)
    return pl.pallas_call(
        body, name=name, out_shape=(hbm, hbm),
        in_specs=(HBM_SPEC, HBM_SPEC, SEM_SPEC, SEM_SPEC, pl.BlockSpec(memory_space=pl.ANY)),
        out_specs=(HBM_SPEC, HBM_SPEC), input_output_aliases={0: 0, 1: 1},
        compiler_params=pltpu.CompilerParams(has_side_effects=DATAFLOW),
    )(src_thru, land_thru, send_sems, recv_sems, after)


def _sum_sources(parts, name):
    k, r, c = parts.shape
    tr = r if k * r * c <= 2 * 1024 * 1024 else _pick(r, (512, 256, 128, 64, 32, 16, 8))

    def body(p_ref, o_ref):
        acc = p_ref[0].astype(F32)
        for s in range(1, k):
            acc = acc + p_ref[s].astype(F32)
        o_ref[...] = acc

    return pl.pallas_call(
        body, name=name, grid=(r // tr,),
        in_specs=[pl.BlockSpec((k, tr, c), lambda i: (0, i, 0))],
        out_specs=pl.BlockSpec((tr, c), lambda i: (i, 0)),
        out_shape=jax.ShapeDtypeStruct((r, c), F32),
        compiler_params=_cparams(("arbitrary",)),
    )(parts)


def _adamw(parts, w, m, v, name, own=None):
    k, r, c = parts.shape
    tr = r if r * c <= 256 * 1024 else _pick(r, (256, 128, 64, 32, 16, 8))

    def body(*refs):
        p_ref, w_ref, m_ref, v_ref = refs[:4]
        g_ref, d_ref, nm_ref, nv_ref = refs[-4:]

        def part(s):
            if own is None:
                return p_ref[s].astype(F32)
            return jnp.where(_my_slot() == s, refs[4][...], p_ref[s]).astype(F32)

        g = part(0)
        for s in range(1, k):
            g = g + part(s)
        m_new = ADAM_B1 * m_ref[...] + (1.0 - ADAM_B1) * g
        v_new = ADAM_B2 * v_ref[...] + (1.0 - ADAM_B2) * jnp.square(g)
        m_hat = m_new / (1.0 - ADAM_B1 ** ADAM_STEP)
        v_hat = v_new / (1.0 - ADAM_B2 ** ADAM_STEP)
        g_ref[...] = g
        d_ref[...] = -ADAM_LR * (m_hat / (jnp.sqrt(v_hat) + ADAM_EPS) + ADAM_WD * w_ref[...])
        nm_ref[...] = m_new
        nv_ref[...] = v_new

    blk = pl.BlockSpec((tr, c), lambda i: (i, 0))
    return pl.pallas_call(
        body, name=name, grid=(r // tr,),
        in_specs=[pl.BlockSpec((k, tr, c), lambda i: (0, i, 0)), blk, blk, blk] + ([] if own is None else [blk]),
        out_specs=[blk] * 4,
        out_shape=[jax.ShapeDtypeStruct((r, c), F32)] * 4,
        compiler_params=_cparams(("arbitrary",)),
    )(parts, w, m, v, *([] if own is None else [own]))


def _silu(v):
    return v * _sigmoid(v)


def _ada_fwd(c_all, w, b):
    def body(c_ref, w_ref, b_ref, o_ref):
        ca = _silu(c_ref[...]).astype(BF16)
        o_ref[...] = jnp.dot(ca, w_ref[...].astype(BF16), preferred_element_type=F32) + b_ref[...]

    return pl.pallas_call(
        body, name="ada_fwd", out_shape=jax.ShapeDtypeStruct((c_all.shape[0], w.shape[1]), F32),
        compiler_params=_cparams(),
    )(c_all, w, b)


def _ada_bwd(c_all, dmod):
    def body(c_ref, d_ref, o_ref):
        ca = _silu(c_ref[...]).astype(BF16).astype(F32)
        dm = d_ref[...].astype(BF16).astype(F32)
        acc = jnp.zeros(o_ref.shape, F32)
        for bi in range(c_all.shape[0]):
            acc = acc + jnp.transpose(ca[bi:bi + 1, :]) * dm[bi:bi + 1, :]
        o_ref[...] = acc

    return pl.pallas_call(
        body, name="ada_bwd", out_shape=jax.ShapeDtypeStruct((c_all.shape[1], dmod.shape[1]), F32),
        compiler_params=_cparams(),
    )(c_all, dmod)


COL_SHARDED = ("w_in", "w_uq", "w_ukv", "w_up")
ROW_SHARDED = ("w_proj_rnn", "w_proj_mla", "w_out", "w_down")
REPLICATED = ("b_ada", "norm1_g", "conv_b", "w_gate_a", "b_gate_a", "w_gate_x", "b_gate_x", "lru_param", "q_norm_g",
              "kv_norm_g", "norm2_g", "ffn_conv_b", "final_g")
WEIGHTS = ("w_ada", "b_ada", "norm1_g", "w_in", "conv_w", "conv_b", "w_gate_a", "b_gate_a", "w_gate_x", "b_gate_x",
           "lru_param", "q_norm_g", "w_uq", "kv_norm_g", "w_ukv", "w_proj_rnn", "w_proj_mla", "w_out", "norm2_g", "w_up",
           "ffn_conv_w", "ffn_conv_b", "w_down", "final_g")
PACK_LANES = 128


def _pack(vecs):
    flat = jnp.concatenate([v.reshape(-1).astype(F32) for v in vecs])
    pad = (-flat.shape[0]) % (PACK_LANES * SUBLANES)
    return jnp.concatenate([flat, jnp.zeros((pad,), F32)]).reshape(-1, PACK_LANES)


def _unpack(packed, shapes):
    flat = packed.reshape(-1)
    out, off = [], 0
    for shp in shapes:
        size = math.prod(shp)
        out.append(flat[off:off + size].reshape(shp))
        off += size
    return out


def kernel(x, c, positions, w_ada, b_ada, norm1_g, w_in, conv_w, conv_b, w_gate_a, b_gate_a, w_gate_x, b_gate_x, lru_param, q_norm_g, w_uq, kv_norm_g, w_ukv, w_proj_rnn, w_proj_mla, w_out, norm2_g, w_up, ffn_conv_w, ffn_conv_b, w_down, final_g, loss_target, m_w_ada, m_b_ada, m_norm1_g, m_w_in, m_conv_w, m_conv_b, m_w_gate_a, m_b_gate_a, m_w_gate_x, m_b_gate_x, m_lru_param, m_q_norm_g, m_w_uq, m_kv_norm_g, m_w_ukv, m_w_proj_rnn, m_w_proj_mla, m_w_out, m_norm2_g, m_w_up, m_ffn_conv_w, m_ffn_conv_b, m_w_down, m_final_g, v_w_ada, v_b_ada, v_norm1_g, v_w_in, v_conv_w, v_conv_b, v_w_gate_a, v_b_gate_a, v_w_gate_x, v_b_gate_x, v_lru_param, v_q_norm_g, v_w_uq, v_kv_norm_g, v_w_ukv, v_w_proj_rnn, v_w_proj_mla, v_w_out, v_norm2_g, v_w_up, v_ffn_conv_w, v_ffn_conv_b, v_w_down, v_final_g):
    args = dict(locals())
    w = {n: args[n] for n in WEIGHTS}
    m = {n: args["m_" + n] for n in WEIGHTS}
    v = {n: args["v_" + n] for n in WEIGHTS}
    s, d = x.shape[1], x.shape[2]
    me = _my_slot()
    def two_d(a):
        assert a.ndim == 3 and a.shape[0] == 1, a.shape
        return a[0]

    big = COL_SHARDED + ROW_SHARDED
    shard = {n: two_d(w[n]).astype(BF16) for n in big}

    def whole(n, g):
        k, r, cc = g.shape
        return jnp.transpose(g, (1, 0, 2)).reshape(r, k * cc) if n in COL_SHARDED else g.reshape(k * r, cc)

    first = _all_gather([shard["w_in"], c, two_d(conv_w), two_d(ffn_conv_w)], "gather_first")
    c_all = first[1].reshape(N_DEV, d)
    conv_w_all = jnp.transpose(first[2], (1, 0, 2)).reshape(conv_w.shape[1], -1)
    ffn_conv_w_all = jnp.transpose(first[3], (1, 0, 2)).reshape(ffn_conv_w.shape[1], -1)

    ada_cols = w_ada.shape[2]
    b_cols = lax.dynamic_slice(b_ada, (0, me * ada_cols), (1, ada_cols))
    mod_cols = _ada_fwd(c_all, w_ada[0], b_cols)
    mod_all, = _all_gather([mod_cols], "gather_mod")

    later = ("w_uq", "w_ukv", "w_proj_rnn", "w_proj_mla", "w_out", "w_up", "w_down")
    flights, started = _gather_start([shard[n] for n in later], mod_all, "gather_start")
    flight = dict(zip(later, flights))

    def fetch(names, after):
        lands = _gather_wait([flight[n] for n in names], after, "gather_wait_" + names[0])
        return {n: whole(n, lax.dynamic_update_index_in_dim(g, shard[n], me, 0)) for n, g in zip(names, lands)}

    mod = lax.dynamic_index_in_dim(mod_all, me, axis=1, keepdims=False).reshape(6, d) + started[0, 0]

    sm = {n: w[n][0] for n in REPLICATED if n not in ("b_ada", "final_g")}
    sm["final_g"] = final_g
    sm["conv_w"] = conv_w_all
    sm["ffn_conv_w"] = ffn_conv_w_all
    in_flight = {}

    def emit(n, g):
        if n in COL_SHARDED:
            r, cc = g.shape
            chunks = jnp.transpose(g.reshape(r, N_DEV, cc // N_DEV), (1, 0, 2))
        else:
            chunks = g.reshape(N_DEV, g.shape[0] // N_DEV, g.shape[1])
        *in_flight[n], token = _scatter_start(chunks, "scatter_start_" + n)
        return token[0, 0]

    sq, grad_x, gs, dmod = _local_step(x[0], mod, positions[0], loss_target[0], whole("w_in", first[0]), fetch, sm, emit)

    small_names = [n for n in REPLICATED if n != "b_ada"] + ["conv_w", "ffn_conv_w"]
    small_shapes = [gs[n].shape for n in small_names] + [(6 * d,), (1,)]
    partial = _pack([gs[n] for n in small_names] + [dmod, sq.reshape(1)])
    partial_all, = _all_gather([partial], "gather_small")
    summed = _unpack(_sum_sources(partial_all, "sum_small"), small_shapes)
    g_small = dict(zip(small_names, summed[:len(small_names)]))
    g_small["b_ada"] = summed[len(small_names)]
    loss = 0.5 * summed[-1][0] / d
    n_before = sum(math.prod(t) for t in small_shapes[:len(small_names)])
    dmod_all = partial_all.reshape(N_DEV, -1)[:, n_before:n_before + 6 * d]
    dmod_cols = lax.dynamic_slice(dmod_all, (0, me * ada_cols), (N_DEV, ada_cols))

    grads, deltas, new_m, new_v = {}, {}, {}, {}

    def update(n, parts, own=None):
        shp = w[n].shape
        g, dl, nm, nv = _adamw(parts, two_d(w[n]), two_d(m[n]), two_d(v[n]), "adamw_" + n, own)
        grads[n], deltas[n], new_m[n], new_v[n] = g.reshape(shp), dl.reshape(shp), nm.reshape(shp), nv.reshape(shp)

    update("w_ada", _ada_bwd(c_all, dmod_cols)[None])

    for n in big:
        chunks, landed = _scatter_wait(*in_flight[n], grad_x, "scatter_wait_" + n)
        update(n, landed, lax.dynamic_index_in_dim(chunks, me, axis=0, keepdims=False))

    for n in ("conv_w", "ffn_conv_w"):
        cols = w[n].shape[2]
        update(n, lax.dynamic_slice(g_small[n], (0, me * cols), (g_small[n].shape[0], cols))[None])

    rep_shapes = [w[n].shape for n in REPLICATED]
    g_rep, d_rep, m_rep, v_rep = _adamw(
        _pack([g_small[n] for n in REPLICATED])[None], _pack([w[n] for n in REPLICATED]), _pack([m[n] for n in REPLICATED]),
        _pack([v[n] for n in REPLICATED]), "adamw_replicated")
    for dst, packed in ((grads, g_rep), (deltas, d_rep), (new_m, m_rep), (new_v, v_rep)):
        dst.update(zip(REPLICATED, _unpack(packed, rep_shapes)))

    return (loss, grad_x[None], *[grads[n] for n in WEIGHTS], *[deltas[n] for n in WEIGHTS],
            *[new_m[n] for n in WEIGHTS], *[new_v[n] for n in WEIGHTS])
```

```python
import functools
import math

import jax
import jax.numpy as jnp
from jax import lax
from jax.experimental import pallas as pl
from jax.experimental.pallas import tpu as pltpu

F32 = jnp.float32
BF16 = jnp.bfloat16

N_DEV = 8
LANES = 128
SUBLANES = 8
VMEM_LIMIT = 56 * 1024 * 1024

D_RNN = 1280
Q_LORA = 384
KV_LORA = 256
QK_NOPE = 64
QK_ROPE = 32
V_HEAD = 64
N_HEADS = 16
D_FF = 2816
ROPE_THETA = 10000.0
LRU_C = 8.0
EPS = 1e-6
MLA_W = 768
ATT_SCALE = 1.0 / math.sqrt(QK_NOPE + QK_ROPE)

ADAM_LR, ADAM_B1, ADAM_B2, ADAM_EPS, ADAM_WD, ADAM_STEP = 0.001, 0.9, 0.999, 1e-08, 0.01, 10


def _cparams(sem=None):
    return pltpu.CompilerParams(dimension_semantics=sem, vmem_limit_bytes=VMEM_LIMIT)


def _pick(n, prefs):
    for p in prefs:
        if n % p == 0:
            return p
    return n


def _sigmoid(v):
    return 0.5 * jnp.tanh(0.5 * v) + 0.5


def _lane(shape):
    return lax.broadcasted_iota(jnp.int32, shape, len(shape) - 1)


def _row(shape):
    return lax.broadcasted_iota(jnp.int32, shape, len(shape) - 2)


MM_BLOCK_BYTES = 36 * 1024 * 1024


def _divisors(n):
    return [t for t in range(n, 0, -LANES) if n % t == 0] if n % LANES == 0 else [n]


def _mm_tiles(m, n, k, a_bytes, b_bytes, o_bytes):
    tm = _pick(m, (512, 384, 256, 128))
    for tk in _divisors(k):
        for tn in _divisors(n):
            need = 2 * (tm * tk * a_bytes + tk * tn * b_bytes + tm * tn * o_bytes) + (tm * tn * 4 if tk < k else 0)
            if tn <= 2048 and need <= MM_BLOCK_BYTES:
                return tm, tn, tk
    raise ValueError((m, n, k))


def _mm(a, b, *, ta=False, tb=False, out_dtype=F32, name):
    (k_a, m) = a.shape if ta else a.shape[::-1]
    (n, k_b) = b.shape if tb else b.shape[::-1]
    assert k_a == k_b, (a.shape, b.shape, ta, tb)
    k = k_a
    tm, tn, tk = _mm_tiles(m, n, k, a.dtype.itemsize, b.dtype.itemsize, jnp.dtype(out_dtype).itemsize)
    nk = k // tk
    dims = (((0 if ta else 1,), (1 if tb else 0,)), ((), ()))

    def body(a_ref, b_ref, o_ref, *acc):
        part = lax.dot_general(a_ref[...].astype(BF16), b_ref[...].astype(BF16), dims, preferred_element_type=F32)
        if nk == 1:
            o_ref[...] = part.astype(out_dtype)
            return
        acc_ref, = acc
        kk = pl.program_id(2)

        @pl.when(kk == 0)
        def _():
            acc_ref[...] = part

        @pl.when(kk > 0)
        def _():
            acc_ref[...] += part

        @pl.when(kk == nk - 1)
        def _():
            o_ref[...] = acc_ref[...].astype(out_dtype)

    a_spec = pl.BlockSpec((tk, tm), lambda i, j, kk: (kk, i)) if ta else pl.BlockSpec((tm, tk), lambda i, j, kk: (i, kk))
    b_spec = pl.BlockSpec((tn, tk), lambda i, j, kk: (j, kk)) if tb else pl.BlockSpec((tk, tn), lambda i, j, kk: (kk, j))
    return pl.pallas_call(
        body, name=name,
        grid=(m // tm, n // tn, nk),
        in_specs=[a_spec, b_spec],
        out_specs=pl.BlockSpec((tm, tn), lambda i, j, kk: (i, j)),
        out_shape=jax.ShapeDtypeStruct((m, n), out_dtype),
        scratch_shapes=[] if nk == 1 else [pltpu.VMEM((tm, tn), F32)],
        compiler_params=_cparams(("arbitrary", "arbitrary", "arbitrary")),
    )(a, b)


def _rowwise(fn, row_ins, par_ins, out_defs, red_defs, *, name, tr=256):
    s = row_ins[0].shape[0]
    tr = min(tr, s)
    nr, npar, no = len(row_ins), len(par_ins), len(out_defs)

    def body(*refs):
        rin, pin = refs[:nr], refs[nr:nr + npar]
        outs, reds = refs[nr + npar:nr + npar + no], refs[nr + npar + no:]
        i = pl.program_id(0)

        @pl.when(i == 0)
        def _():
            for r in reds:
                r[...] = jnp.zeros_like(r)

        fn(i, rin, pin, outs, reds)

    in_specs = [pl.BlockSpec((tr, a.shape[1]), lambda i: (i, 0)) for a in row_ins]
    in_specs += [pl.BlockSpec(a.shape, lambda i, nd=a.ndim: (0,) * nd) for a in par_ins]
    out_specs = [pl.BlockSpec((tr, c), lambda i: (i, 0)) for c, _ in out_defs]
    out_specs += [pl.BlockSpec(shp, lambda i: (0, 0)) for shp in red_defs]
    out_shape = [jax.ShapeDtypeStruct((s, c), dt) for c, dt in out_defs]
    out_shape += [jax.ShapeDtypeStruct(shp, F32) for shp in red_defs]
    return pl.pallas_call(
        body, name=name, grid=(s // tr,), in_specs=in_specs, out_specs=out_specs, out_shape=out_shape,
        compiler_params=_cparams(("arbitrary",)),
    )(*row_ins, *par_ins)


def _rms(v):
    return lax.rsqrt(jnp.mean(v * v, axis=-1, keepdims=True) + EPS)


def _colsum(v):
    return jnp.sum(v, axis=0, keepdims=True)


def _rms_bwd(dn, n, rstd):
    return rstd * (dn - n * jnp.mean(dn * n, axis=-1, keepdims=True))


def _norm_mod_fwd(x, gmod, name):
    def fn(i, rin, pin, outs, reds):
        xv = rin[0][...]
        p = pin[0][...]
        n = xv * _rms(xv)
        outs[0][...] = ((n * p[0:1]) * (1.0 + p[1:2]) + p[2:3]).astype(BF16)

    return _rowwise(fn, [x], [gmod], [(x.shape[1], BF16)], [], name=name)[0]


def _rope(v, rot_c, rot_s):
    half = QK_ROPE // 2
    swapped = jnp.where(_lane(v.shape) < QK_NOPE + half, pltpu.roll(v, LANES - half, 1), pltpu.roll(v, half, 1))
    return v * rot_c + swapped * rot_s


def _rope_t(dv, rot_c, rot_s):
    half = QK_ROPE // 2
    ds = dv * rot_s
    lane = _lane(dv.shape)
    swapped = jnp.where(lane < QK_NOPE + half, pltpu.roll(ds, LANES - half, 1), pltpu.roll(ds, half, 1))
    in_rope = (lane >= QK_NOPE) & (lane < QK_NOPE + QK_ROPE)
    return dv * rot_c + jnp.where(in_rope, swapped, 0.0)


def _mla_prep_fwd(proj_mla, rot_c, rot_s, ng):
    o1, o2 = Q_LORA, Q_LORA + KV_LORA

    def fn(i, rin, pin, outs, reds):
        g = pin[0][...]
        ql = rin[0][:, 0:o1]
        kl = rin[0][:, o1:o2]
        outs[0][...] = (ql * _rms(ql) * g[0:1, 0:o1]).astype(BF16)
        outs[1][...] = (kl * _rms(kl) * g[0:1, o1:o2]).astype(BF16)
        kr = pltpu.roll(rin[0][:, o2:o2 + LANES], QK_NOPE, 1)
        outs[2][...] = _rope(kr, rin[1][...], rin[2][...]).astype(BF16)

    return _rowwise(fn, [proj_mla, rot_c, rot_s], [ng], [(Q_LORA, BF16), (KV_LORA, BF16), (LANES, BF16)], [],
                    name="mla_prep_fwd")


def _mla_prep_bwd(proj_mla, dqn, dkvn, dkr, rot_c, rot_s, ng):
    o1, o2 = Q_LORA, Q_LORA + KV_LORA

    def fn(i, rin, pin, outs, reds):
        g = pin[0][...]
        ql = rin[0][:, 0:o1]
        kl = rin[0][:, o1:o2]
        rq, rk = _rms(ql), _rms(kl)
        nq, nk = ql * rq, kl * rk
        dq, dk = rin[1][...], rin[2][...]
        outs[0][:, 0:o1] = _rms_bwd(dq * g[0:1, 0:o1], nq, rq).astype(BF16)
        outs[0][:, o1:o2] = _rms_bwd(dk * g[0:1, o1:o2], nk, rk).astype(BF16)
        dkr_pre = pltpu.roll(_rope_t(rin[3][...], rin[4][...], rin[5][...]), LANES - QK_NOPE, 1)
        outs[0][:, o2:] = jnp.where(_lane(dkr_pre.shape) < QK_ROPE, dkr_pre, 0.0).astype(BF16)
        reds[0][0:1, 0:o1] += _colsum(dq * nq)
        reds[0][0:1, o1:o2] += _colsum(dk * nk)

    return _rowwise(fn, [proj_mla, dqn, dkvn, dkr, rot_c, rot_s], [ng], [(MLA_W, BF16)], [(SUBLANES, MLA_W)],
                    name="mla_prep_bwd")


def _rope_heads(q, rot_c, rot_s, transpose, name):
    def fn(i, rin, pin, outs, reds):
        c, sn = rin[1][...] * Q_PRESCALE, rin[2][...] * Q_PRESCALE
        for h in range(N_HEADS):
            sl = slice(h * LANES, (h + 1) * LANES)
            v = rin[0][:, sl]
            outs[0][:, sl] = (_rope_t(v, c, sn) if transpose else _rope(v, c, sn)).astype(BF16)

    return _rowwise(fn, [q, rot_c, rot_s], [], [(q.shape[1], BF16)], [], name=name)[0]


def _merge_fwd(pr, pm, proj_g):
    d = pr.shape[1]

    def fn(i, rin, pin, outs, reds):
        outs[0][...] = (_sigmoid(rin[2][:, 0:d]) * rin[0][...] + _sigmoid(rin[2][:, d:]) * rin[1][...]).astype(BF16)

    return _rowwise(fn, [pr, pm, proj_g], [], [(d, BF16)], [], name="merge_fwd")[0]


def _merge_bwd(dmerged, pr, pm, proj_g):
    d = pr.shape[1]

    def fn(i, rin, pin, outs, reds):
        dm = rin[0][...]
        sr, sm = _sigmoid(rin[3][:, 0:d]), _sigmoid(rin[3][:, d:])
        outs[0][...] = (dm * sr).astype(BF16)
        outs[1][...] = (dm * sm).astype(BF16)
        outs[2][:, 0:d] = (dm * rin[1][...] * sr * (1.0 - sr)).astype(BF16)
        outs[2][:, d:] = (dm * rin[2][...] * sm * (1.0 - sm)).astype(BF16)

    return _rowwise(fn, [dmerged, pr, pm, proj_g], [], [(d, BF16), (d, BF16), (2 * d, BF16)], [], name="merge_bwd")


def _resid_norm_fwd(x, o, gmod):
    d = x.shape[1]

    def fn(i, rin, pin, outs, reds):
        p = pin[0][...]
        x1 = rin[0][...] + p[3:4] * rin[1][...]
        outs[0][...] = x1
        outs[1][...] = ((x1 * _rms(x1) * p[0:1]) * (1.0 + p[1:2]) + p[2:3]).astype(BF16)

    return _rowwise(fn, [x, o], [gmod], [(d, F32), (d, BF16)], [], name="resid_norm_fwd")


def _final_fwd_bwd(x1, dn, target, par):
    d = x1.shape[1]

    def fn(i, rin, pin, outs, reds):
        p = pin[0][...]
        dnv = rin[1][...]
        x2 = rin[0][...] + p[0:1] * dnv
        rstd = _rms(x2)
        n3 = x2 * rstd
        err = n3 * p[1:2] - rin[2][...]
        dy = err * (1.0 / d)
        dx2 = _rms_bwd(dy * p[1:2], n3, rstd)
        outs[0][...] = dx2
        outs[1][...] = (dx2 * p[0:1]).astype(BF16)
        reds[0][0:1, :] += _colsum(dy * n3)
        reds[0][1:2, :] += _colsum(dx2 * dnv)
        reds[0][2:3, :] += jnp.zeros((1, d), F32) + jnp.sum(err * err)

    return _rowwise(fn, [x1, dn, target], [par], [(d, F32), (d, BF16)], [(SUBLANES, d)], name="final_fwd_bwd")


def _norm2_bwd(x1, dh2, dx2, o, gmod):
    d = x1.shape[1]

    def fn(i, rin, pin, outs, reds):
        p = pin[0][...]
        x1v, dh = rin[0][...], rin[1][...]
        rstd = _rms(x1v)
        n2 = x1v * rstd
        dx1 = rin[2][...] + _rms_bwd(dh * (p[0:1] * (1.0 + p[1:2])), n2, rstd)
        outs[0][...] = dx1
        outs[1][...] = (dx1 * p[3:4]).astype(BF16)
        reds[0][0:1, :] += _colsum(dh * n2 * (1.0 + p[1:2]))
        reds[0][1:2, :] += _colsum(dh * n2 * p[0:1])
        reds[0][2:3, :] += _colsum(dh)
        reds[0][3:4, :] += _colsum(dx1 * rin[3][...])

    return _rowwise(fn, [x1, dh2, dx2, o], [gmod], [(d, F32), (d, BF16)], [(SUBLANES, d)], name="norm2_bwd")


def _norm1_bwd(x, dh_a, dh_b, dh_c, dx1, gmod):
    d = x.shape[1]

    def fn(i, rin, pin, outs, reds):
        p = pin[0][...]
        xv = rin[0][...]
        dh = rin[1][...] + rin[2][...] + rin[3][...]
        rstd = _rms(xv)
        n1 = xv * rstd
        outs[0][...] = rin[4][...] + _rms_bwd(dh * (p[0:1] * (1.0 + p[1:2])), n1, rstd)
        reds[0][0:1, :] += _colsum(dh * n1 * (1.0 + p[1:2]))
        reds[0][1:2, :] += _colsum(dh * n1 * p[0:1])
        reds[0][2:3, :] += _colsum(dh)

    return _rowwise(fn, [x, dh_a, dh_b, dh_c, dx1], [gmod], [(d, F32)], [(SUBLANES, d)], name="norm1_bwd")


RNN_CHUNK = 512


def _shift_down(ref, base, n, j):
    v = ref[pl.ds(base, n + SUBLANES), :]
    return v[SUBLANES:] if j == 0 else pltpu.roll(v, j, 0)[SUBLANES:]


def _shift_up(ref, base, n, j, top_pad):
    v = ref[pl.ds(base + top_pad, n + SUBLANES), :]
    return v[:n] if j == 0 else pltpu.roll(v, n + SUBLANES - j, 0)[:n]


SCAN_GROUP = 128


def _scan_sizes(s):
    sizes = [s]
    while sizes[-1] > SUBLANES:
        assert sizes[-1] % SUBLANES == 0, s
        sizes.append(sizes[-1] // SUBLANES)
    return sizes


def _scan_scratch(s):
    return [pltpu.VMEM((n + 2 * SUBLANES, LANES), F32) for n in _scan_sizes(s)[1:] for _ in range(2)]


def _linear_scan(a_ref, b_ref, out_ref, a_off, s, reverse, levels):
    sizes = _scan_sizes(s)
    lv = [(a_ref, b_ref, a_off, 0)] + [(levels[2 * i], levels[2 * i + 1], 0, SUBLANES) for i in range(len(sizes) - 1)]
    zero8 = jnp.zeros((SUBLANES, LANES), F32)
    for (ar, br, _, _), n in zip(lv[1:], sizes[1:]):
        br[0:SUBLANES, :] = zero8
        br[pl.ds(n + SUBLANES, SUBLANES), :] = zero8
    order = list(range(SUBLANES - 1, -1, -1)) if reverse else list(range(SUBLANES))

    for lvl in range(len(sizes) - 1):
        ar, br, aoff, off = lv[lvl]
        m = sizes[lvl + 1]
        g = min(m, SCAN_GROUP)
        for t0 in range(0, m, g):
            acc_a = acc_b = None
            for r in order:
                sa = pl.ds(off + SUBLANES * t0 + r + aoff, g, stride=SUBLANES)
                sb = pl.ds(off + SUBLANES * t0 + r, g, stride=SUBLANES)
                a, b = ar[sa, :], br[sb, :]
                if acc_a is None:
                    acc_a, acc_b = a, b
                else:
                    acc_b = a * acc_b + b
                    acc_a = a * acc_a
            lv[lvl + 1][0][pl.ds(SUBLANES + t0, g), :] = acc_a
            lv[lvl + 1][1][pl.ds(SUBLANES + t0, g), :] = acc_b

    ar, br, _, off = lv[-1]
    n = sizes[-1]
    a, b = ar[pl.ds(off, n), :], br[pl.ds(off, n), :]
    h, rows = jnp.zeros((1, LANES), F32), [None] * n
    for j in (range(n - 1, -1, -1) if reverse else range(n)):
        h = a[j:j + 1, :] * h + b[j:j + 1, :]
        rows[j] = h
    br[pl.ds(off, n), :] = jnp.concatenate(rows, axis=0)

    for lvl in range(len(sizes) - 2, -1, -1):
        ar, br, aoff, off = lv[lvl]
        m = sizes[lvl + 1]
        up = lv[lvl + 1][1]
        dst = out_ref if lvl == 0 else br
        g = min(m, SCAN_GROUP)
        for t0 in range(0, m, g):
            h = _shift_up(up, t0, g, 1, SUBLANES) if reverse else _shift_down(up, t0, g, 1)
            for r in order:
                sa = pl.ds(off + SUBLANES * t0 + r + aoff, g, stride=SUBLANES)
                sb = pl.ds(off + SUBLANES * t0 + r, g, stride=SUBLANES)
                h = ar[sa, :] * h + br[sb, :]
                dst[sb, :] = h


def _one_minus_exp(z):
    series = -z * (1.0 + z * (0.5 + z * (1.0 / 6.0 + z * (1.0 / 24.0 + z * (1.0 / 120.0 + z * (1.0 / 720.0))))))
    return jnp.where(z > -0.1, series, 1.0 - jnp.exp(z))


def _softplus(v):
    return jnp.maximum(v, 0.0) + jnp.log(1.0 + jnp.exp(-jnp.abs(v)))


def _rnn_gates(xc, w, wa, wx, sp):
    xb = xc.astype(BF16)
    ra = _sigmoid(jnp.dot(xb, wa, preferred_element_type=F32) + w[5:6])
    ix = _sigmoid(jnp.dot(xb, wx, preferred_element_type=F32) + w[6:7])
    la = (-LRU_C) * ra * sp
    a = jnp.exp(la)
    mult = jnp.sqrt(_one_minus_exp(2.0 * la))
    return ra, ix, a, mult


def _rnn_fwd(x_rnn, keep, rp, wa_bd, wx_bd):
    s, r = x_rnn.shape
    ts = min(RNN_CHUNK, s)

    def body(x_ref, keep_ref, rp_ref, wa_ref, wx_ref, xc_ref, ra_ref, ix_ref, hs_ref, xpad, a_s, b_s, *levels):
        xpad[0:SUBLANES, :] = jnp.zeros((SUBLANES, LANES), F32)
        xpad[SUBLANES:, :] = x_ref[...]
        w = rp_ref[...]
        sp = _softplus(-w[7:8])
        wa, wx = wa_ref[0], wx_ref[0]

        def chunk(c, carry):
            base = pl.multiple_of(c * ts, ts)
            xc = w[4:5] + w[3:4] * _shift_down(xpad, base, ts, 0)
            for j in range(1, 4):
                xc = xc + w[3 - j:4 - j] * _shift_down(xpad, base, ts, j)
            ra, ix, a, mult = _rnn_gates(xc, w, wa, wx, sp)
            kp = keep_ref[pl.ds(base, ts), :]
            xc_ref[pl.ds(base, ts), :] = xc
            ra_ref[pl.ds(base, ts), :] = ra
            ix_ref[pl.ds(base, ts), :] = ix
            a_s[pl.ds(base, ts), :] = a * kp
            b_s[pl.ds(base, ts), :] = jnp.where(kp > 0.0, mult, 1.0) * (ix * xc)
            return carry

        lax.fori_loop(0, s // ts, chunk, 0)

        _linear_scan(a_s, b_s, hs_ref, 0, s, False, levels)

    col = pl.BlockSpec((s, LANES), lambda g: (0, g))
    return pl.pallas_call(
        body, name="rnn_fwd", grid=(r // LANES,),
        in_specs=[col, pl.BlockSpec((s, 1), lambda g: (0, 0)), pl.BlockSpec((SUBLANES, LANES), lambda g: (0, g)),
                  pl.BlockSpec((1, LANES, LANES), lambda g: (g, 0, 0)), pl.BlockSpec((1, LANES, LANES), lambda g: (g, 0, 0))],
        out_specs=[col] * 4,
        out_shape=[jax.ShapeDtypeStruct((s, r), F32)] * 4,
        scratch_shapes=[pltpu.VMEM((s + SUBLANES, LANES), F32), pltpu.VMEM((s, LANES), F32), pltpu.VMEM((s, LANES), F32),
                        *_scan_scratch(s)],
        compiler_params=_cparams(("arbitrary",)),
    )(x_rnn, keep, rp, wa_bd, wx_bd)


def _rnn_bwd(x_rnn, xc, ra, ix, hs, dy, keep, rp, wa_bd, wx_bd):
    s, r = x_rnn.shape
    ts = min(RNN_CHUNK, s)

    def body(x_ref, xc_ref, ra_ref, ix_ref, hs_ref, dy_ref, keep_ref, rp_ref, wa_ref, wx_ref,
             dx_ref, dwa_ref, dwx_ref, red_ref, xpad, hpad, a_s, dh_s, dxc_s, *levels):
        zero8 = jnp.zeros((SUBLANES, LANES), F32)
        xpad[0:SUBLANES, :] = zero8
        xpad[SUBLANES:, :] = x_ref[...]
        hpad[0:SUBLANES, :] = zero8
        hpad[SUBLANES:, :] = hs_ref[...]
        a_s[s:, :] = zero8
        dxc_s[s:, :] = zero8
        w = rp_ref[...]
        sp = _softplus(-w[7:8])
        wa, wx = wa_ref[0], wx_ref[0]

        def decay(c, carry):
            base = pl.multiple_of(c * ts, ts)
            a = jnp.exp((-LRU_C) * ra_ref[pl.ds(base, ts), :] * sp)
            a_s[pl.ds(base, ts), :] = a * keep_ref[pl.ds(base, ts), :]
            return carry

        lax.fori_loop(0, s // ts, decay, 0)

        _linear_scan(a_s, dy_ref, dh_s, 1, s, True, levels)

        def gates(c, carry):
            dwa, dwx, d_ba, d_bx, d_sp, d_cb = carry
            base = pl.multiple_of(c * ts, ts)
            xcv = xc_ref[pl.ds(base, ts), :]
            rav = ra_ref[pl.ds(base, ts), :]
            ixv = ix_ref[pl.ds(base, ts), :]
            kp = keep_ref[pl.ds(base, ts), :]
            dh = dh_s[pl.ds(base, ts), :]
            h_prev = _shift_down(hpad, base, ts, 1)
            la = (-LRU_C) * rav * sp
            a = jnp.exp(la)
            mult = jnp.sqrt(_one_minus_exp(2.0 * la))
            mult_eff = jnp.where(kp > 0.0, mult, 1.0)
            d_a = dh * h_prev * kp
            d_mult = dh * (ixv * xcv) * kp
            d_ix = dh * mult_eff * xcv
            d_xc = dh * mult_eff * ixv
            d_la = d_a * a - d_mult * (a * a) / mult
            d_pa = d_la * ((-LRU_C) * sp) * rav * (1.0 - rav)
            d_px = d_ix * ixv * (1.0 - ixv)
            xb = xcv.astype(BF16)
            pab, pxb = d_pa.astype(BF16), d_px.astype(BF16)
            tn = (((0,), (0,)), ((), ()))
            nt_ = (((1,), (1,)), ((), ()))
            dwa = dwa + lax.dot_general(xb, pab, tn, preferred_element_type=F32)
            dwx = dwx + lax.dot_general(xb, pxb, tn, preferred_element_type=F32)
            d_xc = d_xc + lax.dot_general(pab, wa, nt_, preferred_element_type=F32)
            d_xc = d_xc + lax.dot_general(pxb, wx, nt_, preferred_element_type=F32)
            dxc_s[pl.ds(base, ts), :] = d_xc
            return (dwa, dwx, d_ba + _colsum(d_pa), d_bx + _colsum(d_px),
                    d_sp + _colsum(d_la * ((-LRU_C) * rav)), d_cb + _colsum(d_xc))

        z1 = jnp.zeros((1, LANES), F32)
        zw = jnp.zeros((LANES, LANES), F32)
        dwa, dwx, d_ba, d_bx, d_sp, d_cb = lax.fori_loop(0, s // ts, gates, (zw, zw, z1, z1, z1, z1))
        dwa_ref[0] = dwa
        dwx_ref[0] = dwx

        def conv(c, carry):
            base = pl.multiple_of(c * ts, ts)
            d_here = dxc_s[pl.ds(base, ts), :]
            dx = w[3:4] * d_here
            for j in range(1, 4):
                dx = dx + w[3 - j:4 - j] * _shift_up(dxc_s, base, ts, j, 0)
            dx_ref[pl.ds(base, ts), :] = dx.astype(BF16)
            return tuple(carry[k] + _colsum(d_here * _shift_down(xpad, base, ts, 3 - k)) for k in range(4))

        d_w = lax.fori_loop(0, s // ts, conv, (z1, z1, z1, z1))
        d_lru = d_sp * (-_sigmoid(-w[7:8]))
        red_ref[...] = jnp.concatenate(list(d_w) + [d_cb, d_ba, d_bx, d_lru], axis=0)

    col = pl.BlockSpec((s, LANES), lambda g: (0, g))
    par = pl.BlockSpec((SUBLANES, LANES), lambda g: (0, g))
    wsp = pl.BlockSpec((1, LANES, LANES), lambda g: (g, 0, 0))
    return pl.pallas_call(
        body, name="rnn_bwd", grid=(r // LANES,),
        in_specs=[col] * 6 + [pl.BlockSpec((s, 1), lambda g: (0, 0)), par, wsp, wsp],
        out_specs=[col, wsp, wsp, par],
        out_shape=[jax.ShapeDtypeStruct((s, r), BF16), jax.ShapeDtypeStruct((r // LANES, LANES, LANES), F32),
                   jax.ShapeDtypeStruct((r // LANES, LANES, LANES), F32), jax.ShapeDtypeStruct((SUBLANES, r), F32)],
        scratch_shapes=[pltpu.VMEM((s + SUBLANES, LANES), F32), pltpu.VMEM((s + SUBLANES, LANES), F32),
                        pltpu.VMEM((s + SUBLANES, LANES), F32), pltpu.VMEM((s, LANES), F32),
                        pltpu.VMEM((s + SUBLANES, LANES), F32), *_scan_scratch(s)],
        compiler_params=_cparams(("arbitrary",)),
    )(x_rnn, xc, ra, ix, hs, dy, keep, rp, wa_bd, wx_bd)


ATT_BLOCK = 512
NT_DIMS = (((1,), (1,)), ((), ()))
TN_DIMS = (((0,), (0,)), ((), ()))


LOG2E = 1.4426950408889634
LN2 = 0.6931471805599453
Q_PRESCALE = ATT_SCALE * LOG2E


def _att_scores(q, kvb, krb, diagonal):
    k_eff = jnp.where(_lane(kvb.shape) < QK_NOPE, kvb, krb)
    sc = lax.dot_general(q, k_eff, NT_DIMS, preferred_element_type=F32)
    if diagonal:
        sc = jnp.where(lax.broadcasted_iota(jnp.int32, sc.shape, 1) <= lax.broadcasted_iota(jnp.int32, sc.shape, 0), sc, -jnp.inf)
    return sc, k_eff


def _att_fwd(q, kv, kr):
    s = q.shape[0]
    t = min(ATT_BLOCK, s)
    nb = s // t

    def body(q_ref, kv_ref, kr_ref, y_ref, lse_ref, m_s, acc_s):
        i, j = pl.program_id(1), pl.program_id(2)

        @pl.when(j == 0)
        def _():
            m_s[...] = jnp.full(m_s.shape, -jnp.inf, F32)
            acc_s[...] = jnp.zeros(acc_s.shape, F32)

        def step(diagonal):
            krb = kr_ref[...]
            lane = _lane((t, LANES))
            for hh in range(2):
                sl = slice(hh * LANES, (hh + 1) * LANES)
                kvb = kv_ref[:, sl]
                sc, _ = _att_scores(q_ref[:, sl], kvb, krb, diagonal)
                m_prev = m_s[hh]
                m_new = jnp.maximum(m_prev, jnp.max(sc, axis=-1, keepdims=True))
                alpha = jnp.exp2(m_prev - m_new)
                p = jnp.exp2(sc - m_new[:, 0:1]).astype(BF16)
                ones_v = jnp.where(lane < QK_NOPE, jnp.ones_like(kvb), kvb)
                acc_s[hh] = alpha * acc_s[hh] + jnp.dot(p, ones_v, preferred_element_type=F32)
                m_s[hh] = m_new

        @pl.when(j < i)
        def _():
            step(False)

        @pl.when(j == i)
        def _():
            step(True)
            lane = _lane((t, LANES))
            a0, a1 = acc_s[0], acc_s[1]
            l0, l1 = a0[:, 0:1], a1[:, 0:1]
            y_ref[...] = jnp.where(lane < V_HEAD, pltpu.roll(a0 / l0, V_HEAD, 1), a1 / l1).astype(BF16)
            lse_ref[...] = jnp.where(lane < V_HEAD, m_s[0] + jnp.log(l0) * LOG2E, m_s[1] + jnp.log(l1) * LOG2E)

    return pl.pallas_call(
        body, name="att_fwd", grid=(N_HEADS // 2, nb, nb),
        in_specs=[pl.BlockSpec((t, 2 * LANES), lambda p, i, j: (i, p)),
                  pl.BlockSpec((t, 2 * LANES), lambda p, i, j: (jnp.minimum(j, i), p)),
                  pl.BlockSpec((t, LANES), lambda p, i, j: (jnp.minimum(j, i), 0))],
        out_specs=[pl.BlockSpec((t, LANES), lambda p, i, j: (i, p))] * 2,
        out_shape=[jax.ShapeDtypeStruct((s, N_HEADS * V_HEAD), BF16), jax.ShapeDtypeStruct((s, N_HEADS * V_HEAD), F32)],
        scratch_shapes=[pltpu.VMEM((2, t, LANES), F32)] * 2,
        compiler_params=_cparams(("arbitrary", "arbitrary", "arbitrary")),
    )(q, kv, kr)


def _att_bwd(q, kv, kr, y, lse, dy):
    s = q.shape[0]
    t = min(ATT_BLOCK, s)
    nb = s // t

    def body(q_ref, kv_ref, kr_ref, y_ref, lse_ref, dy_ref, dq_ref, dkv_ref, dkr_ref, dkv_s):
        p_, j, i = pl.program_id(0), pl.program_id(1), pl.program_id(2)

        @pl.when((p_ == 0) & (j == 0) & (i == 0))
        def _():
            dkr_ref[...] = jnp.zeros(dkr_ref.shape, F32)

        @pl.when((j == 0) & (i == 0))
        def _():
            dq_ref[...] = jnp.zeros(dq_ref.shape, F32)

        @pl.when(i == 0)
        def _():
            dkv_s[...] = jnp.zeros(dkv_s.shape, F32)

        def step(diagonal):
            lane = _lane((t, LANES))
            krb = kr_ref[...]
            dyv = dy_ref[...]
            yv = y_ref[...].astype(F32)
            lsev = lse_ref[...]
            rows = pl.ds(pl.multiple_of(i * t, t), t)
            dkr_acc = jnp.zeros((t, LANES), F32)
            for hh in range(2):
                sl = slice(hh * LANES, (hh + 1) * LANES)
                qb, kvb = q_ref[:, sl], kv_ref[:, sl]
                sc, k_eff = _att_scores(qb, kvb, krb, diagonal)
                if hh == 0:
                    do_pad = jnp.where(lane >= V_HEAD, pltpu.roll(dyv, V_HEAD, 1), 0.0)
                    o_pad = jnp.where(lane >= V_HEAD, pltpu.roll(yv, V_HEAD, 1), 0.0)
                    lse_h = lsev[:, 0:1]
                else:
                    do_pad = jnp.where(lane >= V_HEAD, dyv, 0.0)
                    o_pad = jnp.where(lane >= V_HEAD, yv, 0.0)
                    lse_h = lsev[:, LANES - 1:LANES]
                do_ln2 = do_pad * LN2
                delta = jnp.sum(do_ln2 * o_pad, axis=-1, keepdims=True)
                prob = jnp.exp2(sc - lse_h)
                dv = lax.dot_general(prob.astype(BF16), do_pad.astype(BF16), TN_DIMS, preferred_element_type=F32)
                dp = lax.dot_general(do_ln2.astype(BF16), kvb, NT_DIMS, preferred_element_type=F32)
                ds = (prob * (dp - delta)).astype(BF16)
                dq_ref[rows, sl] += jnp.dot(ds, k_eff, preferred_element_type=F32)
                dk_eff = lax.dot_general(ds, qb, TN_DIMS, preferred_element_type=F32)
                dkv_s[hh] += dv + jnp.where(lane < QK_NOPE, dk_eff, 0.0)
                dkr_acc = dkr_acc + jnp.where(lane >= QK_NOPE, dk_eff, 0.0)
            dkr_ref[pl.ds(pl.multiple_of(j * t, t), t), :] += dkr_acc

        @pl.when(i > j)
        def _():
            step(False)

        @pl.when(i == j)
        def _():
            step(True)

        @pl.when(i == nb - 1)
        def _():
            dkv_ref[:, 0:LANES] = dkv_s[0].astype(BF16)
            dkv_ref[:, LANES:] = dkv_s[1].astype(BF16)

    qi = lambda p, j, i: (jnp.maximum(i, j), p)
    return pl.pallas_call(
        body, name="att_bwd", grid=(N_HEADS // 2, nb, nb),
        in_specs=[pl.BlockSpec((t, 2 * LANES), qi),
                  pl.BlockSpec((t, 2 * LANES), lambda p, j, i: (j, p)),
                  pl.BlockSpec((t, LANES), lambda p, j, i: (j, 0)),
                  pl.BlockSpec((t, LANES), qi), pl.BlockSpec((t, LANES), qi), pl.BlockSpec((t, LANES), qi)],
        out_specs=[pl.BlockSpec((s, 2 * LANES), lambda p, j, i: (0, p)),
                   pl.BlockSpec((t, 2 * LANES), lambda p, j, i: (j, p)),
                   pl.BlockSpec((s, LANES), lambda p, j, i: (0, 0))],
        out_shape=[jax.ShapeDtypeStruct((s, N_HEADS * LANES), F32), jax.ShapeDtypeStruct((s, N_HEADS * LANES), BF16),
                   jax.ShapeDtypeStruct((s, LANES), F32)],
        scratch_shapes=[pltpu.VMEM((2, t, LANES), F32)],
        compiler_params=_cparams(("arbitrary", "arbitrary", "arbitrary")),
    )(q, kv, kr, y, lse, dy)


FFN_COLS = 256


def _ffn_conv(pad_ref, w, base, n):
    u = w[3:4] + w[2:3] * _shift_down(pad_ref, base, n, 0)
    for j in range(1, 3):
        u = u + w[2 - j:3 - j] * _shift_down(pad_ref, base, n, j)
    return u


def _ffn_act_fwd(up, fp):
    s, f2 = up.shape
    f = f2 // 2
    tc = FFN_COLS
    ts = min(RNN_CHUNK, s)
    nfb = f // tc

    def body(ug_ref, uv_ref, wg_ref, wv_ref, act_ref, gpad, vpad):
        zero8 = jnp.zeros((SUBLANES, tc), F32)
        gpad[0:SUBLANES, :] = zero8
        vpad[0:SUBLANES, :] = zero8
        gpad[SUBLANES:, :] = ug_ref[...]
        vpad[SUBLANES:, :] = uv_ref[...]
        wg, wv = wg_ref[...], wv_ref[...]

        def chunk(c, carry):
            base = pl.multiple_of(c * ts, ts)
            g = _ffn_conv(gpad, wg, base, ts)
            v = _ffn_conv(vpad, wv, base, ts)
            act_ref[pl.ds(base, ts), :] = (g * _sigmoid(g) * v).astype(BF16)
            return carry

        lax.fori_loop(0, s // ts, chunk, 0)

    return pl.pallas_call(
        body, name="ffn_act_fwd", grid=(nfb,),
        in_specs=[pl.BlockSpec((s, tc), lambda b: (0, b)), pl.BlockSpec((s, tc), lambda b: (0, b + nfb)),
                  pl.BlockSpec((SUBLANES, tc), lambda b: (0, b)), pl.BlockSpec((SUBLANES, tc), lambda b: (0, b + nfb))],
        out_specs=pl.BlockSpec((s, tc), lambda b: (0, b)),
        out_shape=jax.ShapeDtypeStruct((s, f), BF16),
        scratch_shapes=[pltpu.VMEM((s + SUBLANES, tc), F32)] * 2,
        compiler_params=_cparams(("arbitrary",)),
    )(up, up, fp, fp)


def _ffn_act_bwd(up, dact, fp):
    s, f2 = up.shape
    f = f2 // 2
    tc = FFN_COLS
    ts = min(RNN_CHUNK, s)
    nfb = f // tc

    def body(ug_ref, uv_ref, da_ref, wg_ref, wv_ref, dup_ref, red_ref, gpad, vpad, dgs, dvs):
        half = pl.program_id(1)
        wg, wv = wg_ref[...], wv_ref[...]

        @pl.when(half == 0)
        def _():
            zero8 = jnp.zeros((SUBLANES, tc), F32)
            gpad[0:SUBLANES, :] = zero8
            vpad[0:SUBLANES, :] = zero8
            gpad[SUBLANES:, :] = ug_ref[...]
            vpad[SUBLANES:, :] = uv_ref[...]
            dgs[s:, :] = zero8
            dvs[s:, :] = zero8

            def act(c, carry):
                base = pl.multiple_of(c * ts, ts)
                g = _ffn_conv(gpad, wg, base, ts)
                v = _ffn_conv(vpad, wv, base, ts)
                da = da_ref[pl.ds(base, ts), :]
                sg = _sigmoid(g)
                dgs[pl.ds(base, ts), :] = da * v * (sg * (1.0 + g * (1.0 - sg)))
                dvs[pl.ds(base, ts), :] = da * (g * sg)
                return carry

            lax.fori_loop(0, s // ts, act, 0)

        def conv_t(src, pad, w, out_ref, red_ref):
            def chunk(c, carry):
                base = pl.multiple_of(c * ts, ts)
                d_here = src[pl.ds(base, ts), :]
                dx = w[2:3] * d_here
                for j in range(1, 3):
                    dx = dx + w[2 - j:3 - j] * _shift_up(src, base, ts, j, 0)
                out_ref[pl.ds(base, ts), :] = dx.astype(BF16)
                taps = tuple(carry[k] + _colsum(d_here * _shift_down(pad, base, ts, 2 - k)) for k in range(3))
                return taps + (carry[3] + _colsum(d_here),)

            z1 = jnp.zeros((1, tc), F32)
            red = lax.fori_loop(0, s // ts, chunk, (z1, z1, z1, z1))
            red_ref[...] = jnp.concatenate(list(red) + [jnp.zeros((4, tc), F32)], axis=0)

        @pl.when(half == 0)
        def _():
            conv_t(dgs, gpad, wg, dup_ref, red_ref)

        @pl.when(half == 1)
        def _():
            conv_t(dvs, vpad, wv, dup_ref, red_ref)

    gcol = pl.BlockSpec((s, tc), lambda b, h: (0, b))
    vcol = pl.BlockSpec((s, tc), lambda b, h: (0, b + nfb))
    gpar = pl.BlockSpec((SUBLANES, tc), lambda b, h: (0, b))
    vpar = pl.BlockSpec((SUBLANES, tc), lambda b, h: (0, b + nfb))
    return pl.pallas_call(
        body, name="ffn_act_bwd", grid=(nfb, 2),
        in_specs=[gcol, vcol, gcol, gpar, vpar],
        out_specs=[pl.BlockSpec((s, tc), lambda b, h: (0, b + h * nfb)),
                   pl.BlockSpec((SUBLANES, tc), lambda b, h: (0, b + h * nfb))],
        out_shape=[jax.ShapeDtypeStruct((s, f2), BF16), jax.ShapeDtypeStruct((SUBLANES, f2), F32)],
        scratch_shapes=[pltpu.VMEM((s + SUBLANES, tc), F32)] * 4,
        compiler_params=_cparams(("arbitrary", "arbitrary")),
    )(up, up, dact, fp, fp)


def _rows8(rows, width):
    rows = [r.reshape(1, width).astype(F32) for r in rows]
    return jnp.concatenate(rows + [jnp.zeros((SUBLANES - len(rows), width), F32)], axis=0)


def _block_diag(w):
    n, b, _ = w.shape
    w = w.reshape(n // 2, 2, b, b)
    z = jnp.zeros((n // 2, b, b), w.dtype)
    top = jnp.concatenate([w[:, 0], z], axis=2)
    bot = jnp.concatenate([z, w[:, 1]], axis=2)
    return jnp.concatenate([top, bot], axis=1)


def _block_diag_t(bd):
    n, b2, _ = bd.shape
    b = b2 // 2
    return jnp.stack([bd[:, :b, :b], bd[:, b:, b:]], axis=1).reshape(2 * n, b, b)


def _local_step(x, mod, positions, target, w_in, fetch, sm, emit):
    s, d = x.shape
    o_rnn, o_mla = D_RNN, D_RNN + Q_LORA + KV_LORA + QK_ROPE
    wts = {}
    w_in_rnn = w_in[:, :o_rnn]
    w_in_mla = jnp.concatenate([w_in[:, o_rnn:o_mla], jnp.zeros((d, MLA_W - (o_mla - o_rnn)), w_in.dtype)], axis=1)
    w_in_g = w_in[:, o_mla:]
    hd = QK_NOPE + QK_ROPE
    wa_bd = _block_diag(sm["w_gate_a"]).astype(BF16)
    wx_bd = _block_diag(sm["w_gate_x"]).astype(BF16)

    pos = positions.reshape(s)
    half = QK_ROPE // 2
    inv_freq = ROPE_THETA ** (-jnp.arange(half, dtype=F32) / half)
    ang = pos.astype(F32)[:, None] * inv_freq
    cos, sin = jnp.cos(ang), jnp.sin(ang)
    rot_c = jnp.concatenate([jnp.ones((s, QK_NOPE), F32), cos, cos, jnp.ones((s, LANES - hd), F32)], axis=1)
    rot_s = jnp.concatenate([jnp.zeros((s, QK_NOPE), F32), -sin, sin, jnp.zeros((s, LANES - hd), F32)], axis=1)
    keep = (pos != 0).astype(F32).reshape(s, 1)

    gmod1 = _rows8([sm["norm1_g"], mod[1], mod[0]], d)
    gmod2 = _rows8([sm["norm2_g"], mod[4], mod[3], mod[2]], d)
    rp = jnp.concatenate([sm["conv_w"].reshape(4, D_RNN), _rows8([sm["conv_b"], sm["b_gate_a"], sm["b_gate_x"], sm["lru_param"]], D_RNN)[:4]], axis=0)
    fp = _rows8([sm["ffn_conv_w"][0], sm["ffn_conv_w"][1], sm["ffn_conv_w"][2], sm["ffn_conv_b"]], 2 * D_FF)
    ng = _rows8([jnp.concatenate([sm["q_norm_g"].reshape(-1), sm["kv_norm_g"].reshape(-1), jnp.zeros((MLA_W - Q_LORA - KV_LORA,), F32)])], MLA_W)
    fpar = _rows8([mod[5], sm["final_g"]], d)

    h = _norm_mod_fwd(x, gmod1, "norm1_fwd")
    proj_rnn = _mm(h, w_in_rnn, name="mm_in_rnn")
    proj_mla = _mm(h, w_in_mla, name="mm_in_mla")
    proj_g = _mm(h, w_in_g, name="mm_in_g")
    xc, ra, ix, hs = _rnn_fwd(proj_rnn, keep, rp, wa_bd, wx_bd)
    qn, kvn, kr = _mla_prep_fwd(proj_mla, rot_c, rot_s, ng)
    wts.update(fetch(("w_uq", "w_ukv"), kr))
    w_uq_p = jnp.pad(wts["w_uq"].reshape(Q_LORA, N_HEADS, hd), ((0, 0), (0, 0), (0, LANES - hd))).reshape(Q_LORA, N_HEADS * LANES)
    w_ukv = wts["w_ukv"]
    q_rot = _rope_heads(_mm(qn, w_uq_p, name="mm_uq"), rot_c, rot_s, False, "rope_fwd")
    kv = _mm(kvn, w_ukv, out_dtype=BF16, name="mm_ukv")
    y_mla, lse = _att_fwd(q_rot, kv, kr)
    wts.update(fetch(("w_proj_rnn", "w_proj_mla", "w_out", "w_up", "w_down"), lse))
    pr = _mm(hs, wts["w_proj_rnn"], name="mm_proj_rnn")
    pm = _mm(y_mla, wts["w_proj_mla"], name="mm_proj_mla")
    merged = _merge_fwd(pr, pm, proj_g)
    o = _mm(merged, wts["w_out"], name="mm_out")
    x1, h2 = _resid_norm_fwd(x, o, gmod2)
    up = _mm(h2, wts["w_up"], name="mm_up")
    act = _ffn_act_fwd(up, fp)
    dn = _mm(act, wts["w_down"], name="mm_down")

    dx2, ddn, red_f = _final_fwd_bwd(x1, dn, target, fpar)
    dact = _mm(ddn, wts["w_down"], tb=True, name="mm_d_act")
    tok = emit("w_down", _mm(act, ddn, ta=True, out_dtype=BF16, name="mm_dw_down"))
    dup, red_ffn = _ffn_act_bwd(up, dact, fp + tok)
    dh2 = _mm(dup, wts["w_up"], tb=True, name="mm_d_h2")
    tok = tok + emit("w_up", _mm(h2, dup, ta=True, out_dtype=BF16, name="mm_dw_up"))
    dx1, do, red_2 = _norm2_bwd(x1, dh2, dx2, o, gmod2 + tok)
    dmerged = _mm(do, wts["w_out"], tb=True, name="mm_d_merged")
    tok = tok + emit("w_out", _mm(merged, do, ta=True, out_dtype=BF16, name="mm_dw_out"))
    dpr, dpm, dg = _merge_bwd(dmerged, pr, pm, proj_g)
    dy_rnn = _mm(dpr, wts["w_proj_rnn"], tb=True, name="mm_d_yrnn")
    tok = tok + emit("w_proj_rnn", _mm(hs, dpr, ta=True, out_dtype=BF16, name="mm_dw_proj_rnn"))
    dy_mla = _mm(dpm, wts["w_proj_mla"], tb=True, name="mm_d_ymla")
    tok = tok + emit("w_proj_mla", _mm(y_mla, dpm, ta=True, out_dtype=BF16, name="mm_dw_proj_mla"))
    dq_rot, dkv, dkr = _att_bwd(q_rot, kv, kr, y_mla, lse, dy_mla)
    dq = _rope_heads(dq_rot, rot_c, rot_s, True, "rope_bwd")
    dqn = _mm(dq, w_uq_p, tb=True, name="mm_d_qn")
    dw_uq_p = _mm(qn, dq, ta=True, out_dtype=BF16, name="mm_dw_uq")
    tok = tok + emit("w_uq", dw_uq_p.reshape(Q_LORA, N_HEADS, LANES)[:, :, :hd].reshape(Q_LORA, N_HEADS * hd))
    dkvn = _mm(dkv, w_ukv, tb=True, name="mm_d_kvn")
    tok = tok + emit("w_ukv", _mm(kvn, dkv, ta=True, out_dtype=BF16, name="mm_dw_ukv"))
    dproj_mla, red_m = _mla_prep_bwd(proj_mla, dqn, dkvn, dkr, rot_c, rot_s, ng + tok)
    dx_rnn, dwa_bd, dwx_bd, red_r = _rnn_bwd(proj_rnn, xc, ra, ix, hs, dy_rnn, keep, rp + tok, wa_bd, wx_bd)
    dw_in = jnp.concatenate([
        _mm(h, dx_rnn, ta=True, out_dtype=BF16, name="mm_dw_in_rnn"),
        _mm(h, dproj_mla, ta=True, out_dtype=BF16, name="mm_dw_in_mla")[:, :o_mla - o_rnn],
        _mm(h, dg, ta=True, out_dtype=BF16, name="mm_dw_in_g")], axis=1)
    tok = tok + emit("w_in", dw_in)
    dh_a = _mm(dx_rnn, w_in_rnn, tb=True, name="mm_d_h_rnn")
    dh_b = _mm(dproj_mla, w_in_mla, tb=True, name="mm_d_h_mla")
    dh_c = _mm(dg, w_in_g, tb=True, name="mm_d_h_g")
    grad_x, red_1 = _norm1_bwd(x, dh_a, dh_b, dh_c, dx1, gmod1 + tok)

    gs = {
        "norm1_g": red_1[0], "conv_w": red_r[0:4], "conv_b": red_r[4], "w_gate_a": _block_diag_t(dwa_bd),
        "b_gate_a": red_r[5], "w_gate_x": _block_diag_t(dwx_bd), "b_gate_x": red_r[6], "lru_param": red_r[7],
        "q_norm_g": red_m[0, :Q_LORA], "kv_norm_g": red_m[0, Q_LORA:Q_LORA + KV_LORA], "norm2_g": red_2[0],
        "ffn_conv_w": red_ffn[0:3], "ffn_conv_b": red_ffn[3], "final_g": red_f[0],
    }
    dmod = jnp.stack([red_1[2], red_1[1], red_2[3], red_2[2], red_2[1], red_f[1]], axis=0)
    return red_f[2, 0], grad_x, gs, dmod


MESH_IDS = pl.DeviceIdType.MESH
HBM_SPEC = pl.BlockSpec(memory_space=pltpu.HBM)


def _my_slot():
    return 4 * lax.axis_index("x") + 2 * lax.axis_index("y") + lax.axis_index("c")


def _all_gather(arrs, name):
    n = len(arrs)

    def body(*refs):
        ins, outs = refs[:n], refs[n:2 * n]
        send_sems, recv_sems, local_sems = refs[2 * n:]
        x, y, c = lax.axis_index("x"), lax.axis_index("y"), lax.axis_index("c")
        me, sibling = (x, y, c), (x, y, 1 - c)
        chips = [(1 - x, y), (x, 1 - y), (1 - x, 1 - y)]

        def slot(dev):
            return 4 * dev[0] + 2 * dev[1] + dev[2]

        def copy(a, k, block, to, src=None):
            dst = outs[a].at[slot(block)]
            return pltpu.make_async_remote_copy(
                src_ref=dst if src is None else src, dst_ref=dst, send_sem=send_sems.at[a, k], recv_sem=recv_sems.at[a, k],
                device_id=to, device_id_type=MESH_IDS)

        mine = [pltpu.make_async_copy(ins[a], outs[a].at[slot(me)], local_sems.at[a]) for a in range(n)]
        for cp in mine:
            cp.start()
        first = []
        for a in range(n):
            first.append(copy(a, 0, me, sibling, src=ins[a]))
            first += [copy(a, 1 + j, me, (*chip, c), src=ins[a]) for j, chip in enumerate(chips)]
        for cp in first:
            cp.start()
        passed = []
        for j, chip in enumerate(chips):
            for a in range(n):
                copy(a, 1 + j, (*chip, c), me).wait_recv()
                fwd = copy(a, 4 + j, (*chip, c), sibling)
                fwd.start()
                passed.append(fwd)
        for a in range(n):
            copy(a, 0, sibling, me).wait_recv()
            for j, chip in enumerate(chips):
                copy(a, 4 + j, (*chip, 1 - c), me).wait_recv()
        for cp in first + passed:
            cp.wait_send()
        for cp in mine:
            cp.wait()

    return pl.pallas_call(
        body, name=name,
        in_specs=[HBM_SPEC] * n, out_specs=[HBM_SPEC] * n,
        out_shape=[jax.ShapeDtypeStruct((N_DEV,) + a.shape, a.dtype) for a in arrs],
        scratch_shapes=[pltpu.SemaphoreType.DMA((n, 7)), pltpu.SemaphoreType.DMA((n, 7)), pltpu.SemaphoreType.DMA((n,))],
    )(*arrs)


SEM_SPEC =pl.BlockSpec(memory_space=pltpu.SEMAPHORE)
DATAFLOW = pltpu.SideEffectType.DATAFLOW_SIDE_EFFECTING
FLIPS = [(dx, dy, dc) for dx in (0, 1) for dy in (0, 1) for dc in (0, 1)][1:]


def _peer(k):
    dx, dy, dc = FLIPS[k]
    peer = (lax.axis_index("x") ^ dx, lax.axis_index("y") ^ dy, lax.axis_index("c") ^ dc)
    return peer, 4 * peer[0] + 2 * peer[1] + peer[2]


def _gather_start(shards, after, name):
    n, nf = len(shards), len(FLIPS)

    def body(*refs):
        srcs, lands = refs[:n], refs[n:2 * n]
        send_sems, recv_sems = refs[2 * n + 1:3 * n + 1], refs[3 * n + 1:4 * n + 1]
        token = refs[-1]
        me = _my_slot()
        for a in range(n):
            for k in range(nf):
                peer, _ = _peer(k)
                pltpu.make_async_remote_copy(
                    src_ref=srcs[a], dst_ref=lands[a].at[me], send_sem=send_sems[a].at[k], recv_sem=recv_sems[a].at[k],
                    device_id=peer, device_id_type=MESH_IDS).start()
        token[...] = jnp.zeros(token.shape, F32)

    land_shapes = [(N_DEV,) + a.shape for a in shards]
    sems = [pltpu.SemaphoreType.DMA((nf,))] * n
    out = pl.pallas_call(
        body, name=name,
        out_shape=(*sems, *sems, *[pltpu.HBM(a.shape, a.dtype) for a in shards],
                   *[pltpu.HBM(shp, a.dtype) for shp, a in zip(land_shapes, shards)],
                   jax.ShapeDtypeStruct((SUBLANES, LANES), F32)),
        in_specs=[HBM_SPEC] * (2 * n) + [pl.BlockSpec(memory_space=pl.ANY)],
        out_specs=(*[SEM_SPEC] * (2 * n), *[HBM_SPEC] * (2 * n), pl.BlockSpec(memory_space=pltpu.VMEM)),
        input_output_aliases={i: 2 * n + i for i in range(2 * n)},
        compiler_params=pltpu.CompilerParams(has_side_effects=DATAFLOW),
    )(*[pltpu.with_memory_space_constraint(a, pltpu.HBM) for a in shards],
      *[pltpu.with_memory_space_constraint(lax.empty(shp, a.dtype), pltpu.HBM) for shp, a in zip(land_shapes, shards)],
      after)
    return [(out[a], out[n + a], out[2 * n + a], out[3 * n + a]) for a in range(n)], out[-1]


def _gather_wait(flights, after, name):
    n, nf = len(flights), len(FLIPS)

    def body(*refs):
        send_sems, recv_sems = refs[:n], refs[n:2 * n]
        srcs, lands = refs[2 * n:3 * n], refs[3 * n:4 * n]
        for a in range(n):
            for k in range(nf):
                peer, peer_slot = _peer(k)
                cp = pltpu.make_async_remote_copy(
                    src_ref=srcs[a], dst_ref=lands[a].at[peer_slot], send_sem=send_sems[a].at[k],
                    recv_sem=recv_sems[a].at[k], device_id=peer, device_id_type=MESH_IDS)
                cp.wait_send()
                cp.wait_recv()

    srcs, lands = [f[2] for f in flights], [f[3] for f in flights]
    out = pl.pallas_call(
        body, name=name,
        out_shape=(*[pltpu.HBM(a.shape, a.dtype) for a in srcs], *[pltpu.HBM(a.shape, a.dtype) for a in lands]),
        in_specs=[SEM_SPEC] * (2 * n) + [HBM_SPEC] * (2 * n) + [pl.BlockSpec(memory_space=pl.ANY)],
        out_specs=tuple([HBM_SPEC] * (2 * n)),
        input_output_aliases={2 * n + i: i for i in range(2 * n)},
        compiler_params=pltpu.CompilerParams(has_side_effects=DATAFLOW),
    )(*[f[0] for f in flights], *[f[1] for f in flights], *srcs, *lands, after)
    return list(out[n:])


def _scatter_start(chunks, name):
    def body(src_ref, land_ref, send_sems, recv_sems, src_thru, land_thru, token):
        me = _my_slot()
        for k in range(len(FLIPS)):
            peer, peer_slot = _peer(k)
            pltpu.make_async_remote_copy(
                src_ref=src_ref.at[peer_slot], dst_ref=land_ref.at[me], send_sem=send_sems.at[k], recv_sem=recv_sems.at[k],
                device_id=peer, device_id_type=MESH_IDS).start()
        token[...] = jnp.zeros(token.shape, F32)

    n = len(FLIPS)
    hbm = pltpu.HBM(chunks.shape, chunks.dtype)
    return pl.pallas_call(
        body, name=name,
        out_shape=(pltpu.SemaphoreType.DMA((n,)), pltpu.SemaphoreType.DMA((n,)), hbm, hbm,
                   jax.ShapeDtypeStruct((SUBLANES, LANES), F32)),
        in_specs=(HBM_SPEC, HBM_SPEC),
        out_specs=(SEM_SPEC, SEM_SPEC, HBM_SPEC, HBM_SPEC, pl.BlockSpec(memory_space=pltpu.VMEM)),
        input_output_aliases={0: 2, 1: 3},
        compiler_params=pltpu.CompilerParams(has_side_effects=DATAFLOW),
    )(pltpu.with_memory_space_constraint(chunks, pltpu.HBM),
      pltpu.with_memory_space_constraint(lax.empty(chunks.shape, chunks.dtype), pltpu.HBM))


def _scatter_wait(send_sems, recv_sems, src_thru, land_thru, after, name):
    def body(src_ref, land_ref, send_sems, recv_sems, after_ref, src_dead, got_ref):
        for k in range(len(FLIPS)):
            peer, peer_slot = _peer(k)
            cp = pltpu.make_async_remote_copy(
                src_ref=src_ref.at[peer_slot], dst_ref=land_ref.at[peer_slot], send_sem=send_sems.at[k],
                recv_sem=recv_sems.at[k], device_id=peer, device_id_type=MESH_IDS)
            cp.wait_send()
            cp.wait_recv()

    hbm = pltpu.HBM(src_thru.shape, src_thru.dtype)
    return pl.pallas_call(
        body, name=name, out_shape=(hbm, hbm),
        in_specs=(HBM_SPEC, HBM_SPEC, SEM_SPEC, SEM_SPEC, pl.BlockSpec(memory_space=pl.ANY)),
        out_specs=(HBM_SPEC, HBM_SPEC), input_output_aliases={0: 0, 1: 1},
        compiler_params=pltpu.CompilerParams(has_side_effects=DATAFLOW),
    )(src_thru, land_thru, send_sems, recv_sems, after)


def _sum_sources(parts, name):
    k, r, c = parts.shape
    tr = r if k * r * c <= 2 * 1024 * 1024 else _pick(r, (512, 256, 128, 64, 32, 16, 8))

    def body(p_ref, o_ref):
        acc = p_ref[0].astype(F32)
        for s in range(1, k):
            acc = acc + p_ref[s].astype(F32)
        o_ref[...] = acc

    return pl.pallas_call(
        body, name=name, grid=(r // tr,),
        in_specs=[pl.BlockSpec((k, tr, c), lambda i: (0, i, 0))],
        out_specs=pl.BlockSpec((tr, c), lambda i: (i, 0)),
        out_shape=jax.ShapeDtypeStruct((r, c), F32),
        compiler_params=_cparams(("arbitrary",)),
    )(parts)


def _adamw(parts, w, m, v, name, own=None):
    k, r, c = parts.shape
    tr = r if r * c <= 256 * 1024 else _pick(r, (256, 128, 64, 32, 16, 8))

    def body(*refs):
        p_ref, w_ref, m_ref, v_ref = refs[:4]
        g_ref, d_ref, nm_ref, nv_ref = refs[-4:]

        def part(s):
            if own is None:
                return p_ref[s].astype(F32)
            return jnp.where(_my_slot() == s, refs[4][...], p_ref[s]).astype(F32)

        g = part(0)
        for s in range(1, k):
            g = g + part(s)
        m_new = ADAM_B1 * m_ref[...] + (1.0 - ADAM_B1) * g
        v_new = ADAM_B2 * v_ref[...] + (1.0 - ADAM_B2) * jnp.square(g)
        m_hat = m_new / (1.0 - ADAM_B1 ** ADAM_STEP)
        v_hat = v_new / (1.0 - ADAM_B2 ** ADAM_STEP)
        g_ref[...] = g
        d_ref[...] = -ADAM_LR * (m_hat / (jnp.sqrt(v_hat) + ADAM_EPS) + ADAM_WD * w_ref[...])
        nm_ref[...] = m_new
        nv_ref[...] = v_new

    blk = pl.BlockSpec((tr, c), lambda i: (i, 0))
    return pl.pallas_call(
        body, name=name, grid=(r // tr,),
        in_specs=[pl.BlockSpec((k, tr, c), lambda i: (0, i, 0)), blk, blk, blk] + ([] if own is None else [blk]),
        out_specs=[blk] * 4,
        out_shape=[jax.ShapeDtypeStruct((r, c), F32)] * 4,
        compiler_params=_cparams(("arbitrary",)),
    )(parts, w, m, v, *([] if own is None else [own]))


def _silu(v):
    return v * _sigmoid(v)


def _ada_fwd(c_all, w, b):
    def body(c_ref, w_ref, b_ref, o_ref):
        ca = _silu(c_ref[...]).astype(BF16)
        o_ref[...] = jnp.dot(ca, w_ref[...].astype(BF16), preferred_element_type=F32) + b_ref[...]

    return pl.pallas_call(
        body, name="ada_fwd", out_shape=jax.ShapeDtypeStruct((c_all.shape[0], w.shape[1]), F32),
        compiler_params=_cparams(),
    )(c_all, w, b)


def _ada_bwd(c_all, dmod):
    def body(c_ref, d_ref, o_ref):
        ca = _silu(c_ref[...]).astype(BF16).astype(F32)
        dm = d_ref[...].astype(BF16).astype(F32)
        acc = jnp.zeros(o_ref.shape, F32)
        for bi in range(c_all.shape[0]):
            acc = acc + jnp.transpose(ca[bi:bi + 1, :]) * dm[bi:bi + 1, :]
        o_ref[...] = acc

    return pl.pallas_call(
        body, name="ada_bwd", out_shape=jax.ShapeDtypeStruct((c_all.shape[1], dmod.shape[1]), F32),
        compiler_params=_cparams(),
    )(c_all, dmod)


COL_SHARDED = ("w_in", "w_uq", "w_ukv", "w_up")
ROW_SHARDED = ("w_proj_rnn", "w_proj_mla", "w_out", "w_down")
REPLICATED = ("b_ada", "norm1_g", "conv_b", "w_gate_a", "b_gate_a", "w_gate_x", "b_gate_x", "lru_param", "q_norm_g",
              "kv_norm_g", "norm2_g", "ffn_conv_b", "final_g")
WEIGHTS = ("w_ada", "b_ada", "norm1_g", "w_in", "conv_w", "conv_b", "w_gate_a", "b_gate_a", "w_gate_x", "b_gate_x",
           "lru_param", "q_norm_g", "w_uq", "kv_norm_g", "w_ukv", "w_proj_rnn", "w_proj_mla", "w_out", "norm2_g", "w_up",
           "ffn_conv_w", "ffn_conv_b", "w_down", "final_g")
PACK_LANES = 128


def _pack(vecs):
    flat = jnp.concatenate([v.reshape(-1).astype(F32) for v in vecs])
    pad = (-flat.shape[0]) % (PACK_LANES * SUBLANES)
    return jnp.concatenate([flat, jnp.zeros((pad,), F32)]).reshape(-1, PACK_LANES)


def _unpack(packed, shapes):
    flat = packed.reshape(-1)
    out, off = [], 0
    for shp in shapes:
        size = math.prod(shp)
        out.append(flat[off:off + size].reshape(shp))
        off += size
    return out


def kernel(x, c, positions, w_ada, b_ada, norm1_g, w_in, conv_w, conv_b, w_gate_a, b_gate_a, w_gate_x, b_gate_x, lru_param, q_norm_g, w_uq, kv_norm_g, w_ukv, w_proj_rnn, w_proj_mla, w_out, norm2_g, w_up, ffn_conv_w, ffn_conv_b, w_down, final_g, loss_target, m_w_ada, m_b_ada, m_norm1_g, m_w_in, m_conv_w, m_conv_b, m_w_gate_a, m_b_gate_a, m_w_gate_x, m_b_gate_x, m_lru_param, m_q_norm_g, m_w_uq, m_kv_norm_g, m_w_ukv, m_w_proj_rnn, m_w_proj_mla, m_w_out, m_norm2_g, m_w_up, m_ffn_conv_w, m_ffn_conv_b, m_w_down, m_final_g, v_w_ada, v_b_ada, v_norm1_g, v_w_in, v_conv_w, v_conv_b, v_w_gate_a, v_b_gate_a, v_w_gate_x, v_b_gate_x, v_lru_param, v_q_norm_g, v_w_uq, v_kv_norm_g, v_w_ukv, v_w_proj_rnn, v_w_proj_mla, v_w_out, v_norm2_g, v_w_up, v_ffn_conv_w, v_ffn_conv_b, v_w_down, v_final_g):
    args = dict(locals())
    w = {n: args[n] for n in WEIGHTS}
    m = {n: args["m_" + n] for n in WEIGHTS}
    v = {n: args["v_" + n] for n in WEIGHTS}
    s, d = x.shape[1], x.shape[2]
    me = _my_slot()
    def two_d(a):
        assert a.ndim == 3 and a.shape[0] == 1, a.shape
        return a[0]

    big = COL_SHARDED + ROW_SHARDED
    shard = {n: two_d(w[n]).astype(BF16) for n in big}

    def whole(n, g):
        k, r, cc = g.shape
        return jnp.transpose(g, (1, 0, 2)).reshape(r, k * cc) if n in COL_SHARDED else g.reshape(k * r, cc)

    first = _all_gather([shard["w_in"], c, two_d(conv_w), two_d(ffn_conv_w)], "gather_first")
    c_all = first[1].reshape(N_DEV, d)
    conv_w_all = jnp.transpose(first[2], (1, 0, 2)).reshape(conv_w.shape[1], -1)
    ffn_conv_w_all = jnp.transpose(first[3], (1, 0, 2)).reshape(ffn_conv_w.shape[1], -1)

    ada_cols = w_ada.shape[2]
    b_cols = lax.dynamic_slice(b_ada, (0, me * ada_cols), (1, ada_cols))
    mod_cols = _ada_fwd(c_all, w_ada[0], b_cols)
    mod_all, = _all_gather([mod_cols], "gather_mod")

    later = ("w_uq", "w_ukv", "w_proj_rnn", "w_proj_mla", "w_out", "w_up", "w_down")
    flights, started = _gather_start([shard[n] for n in later], mod_all, "gather_start")
    flight = dict(zip(later, flights))

    def fetch(names, after):
        lands = _gather_wait([flight[n] for n in names], after, "gather_wait_" + names[0])
        return {n: whole(n, lax.dynamic_update_index_in_dim(g, shard[n], me, 0)) for n, g in zip(names, lands)}

    mod = lax.dynamic_index_in_dim(mod_all, me, axis=1, keepdims=False).reshape(6, d) + started[0, 0]

    sm = {n: w[n][0] for n in REPLICATED if n not in ("b_ada", "final_g")}
    sm["final_g"] = final_g
    sm["conv_w"] = conv_w_all
    sm["ffn_conv_w"] = ffn_conv_w_all
    in_flight = {}

    def emit(n, g):
        if n in COL_SHARDED:
            r, cc = g.shape
            chunks = jnp.transpose(g.reshape(r, N_DEV, cc // N_DEV), (1, 0, 2))
        else:
            chunks = g.reshape(N_DEV, g.shape[0] // N_DEV, g.shape[1])
        *in_flight[n], token = _scatter_start(chunks, "scatter_start_" + n)
        return token[0, 0]

    sq, grad_x, gs, dmod = _local_step(x[0], mod, positions[0], loss_target[0], whole("w_in", first[0]), fetch, sm, emit)

    small_names = [n for n in REPLICATED if n != "b_ada"] + ["conv_w", "ffn_conv_w"]
    small_shapes = [gs[n].shape for n in small_names] + [(6 * d,), (1,)]
    partial = _pack([gs[n] for n in small_names] + [dmod, sq.reshape(1)])
    partial_all, = _all_gather([partial], "gather_small")
    summed = _unpack(_sum_sources(partial_all, "sum_small"), small_shapes)
    g_small = dict(zip(small_names, summed[:len(small_names)]))
    g_small["b_ada"] = summed[len(small_names)]
    loss = 0.5 * summed[-1][0] / d
    n_before = sum(math.prod(t) for t in small_shapes[:len(small_names)])
    dmod_all = partial_all.reshape(N_DEV, -1)[:, n_before:n_before + 6 * d]
    dmod_cols = lax.dynamic_slice(dmod_all, (0, me * ada_cols), (N_DEV, ada_cols))

    grads, deltas, new_m, new_v = {}, {}, {}, {}

    def update(n, parts, own=None):
        shp = w[n].shape
        g, dl, nm, nv = _adamw(parts, two_d(w[n]), two_d(m[n]), two_d(v[n]), "adamw_" + n, own)
        grads[n], deltas[n], new_m[n], new_v[n] = g.reshape(shp), dl.reshape(shp), nm.reshape(shp), nv.reshape(shp)

    update("w_ada", _ada_bwd(c_all, dmod_cols)[None])

    for n in big:
        chunks, landed = _scatter_wait(*in_flight[n], grad_x, "scatter_wait_" + n)
        update(n, landed, lax.dynamic_index_in_dim(chunks, me, axis=0, keepdims=False))

    for n in ("conv_w", "ffn_conv_w"):
        cols = w[n].shape[2]
        update(n, lax.dynamic_slice(g_small[n], (0, me * cols), (g_small[n].shape[0], cols))[None])

    rep_shapes = [w[n].shape for n in REPLICATED]
    g_rep, d_rep, m_rep, v_rep = _adamw(
        _pack([g_small[n] for n in REPLICATED])[None], _pack([w[n] for n in REPLICATED]), _pack([m[n] for n in REPLICATED]),
        _pack([v[n] for n in REPLICATED]), "adamw_replicated")
    for dst, packed in ((grads, g_rep), (deltas, d_rep), (new_m, m_rep), (new_v, v_rep)):
        dst.update(zip(REPLICATED, _unpack(packed, rep_shapes)))

    return (loss, grad_x[None], *[grads[n] for n in WEIGHTS], *[deltas[n] for n in WEIGHTS],
            *[new_m[n] for n in WEIGHTS], *[new_v[n] for n in WEIGHTS])
```

```python
import functools
import math

import jax
import jax.numpy as jnp
from jax import lax
from jax.experimental import pallas as pl
from jax.experimental.pallas import tpu as pltpu

F32 = jnp.float32
BF16 = jnp.bfloat16

N_DEV = 8
LANES = 128
SUBLANES = 8
VMEM_LIMIT = 56 * 1024 * 1024

D_RNN = 1280
Q_LORA = 384
KV_LORA = 256
QK_NOPE = 64
QK_ROPE = 32
V_HEAD = 64
N_HEADS = 16
D_FF = 2816
ROPE_THETA = 10000.0
LRU_C = 8.0
EPS = 1e-6
MLA_W = 768
ATT_SCALE = 1.0 / math.sqrt(QK_NOPE + QK_ROPE)

ADAM_LR, ADAM_B1, ADAM_B2, ADAM_EPS, ADAM_WD, ADAM_STEP = 0.001, 0.9, 0.999, 1e-08, 0.01, 10


def _cparams(sem=None):
    return pltpu.CompilerParams(dimension_semantics=sem, vmem_limit_bytes=VMEM_LIMIT)


def _pick(n, prefs):
    for p in prefs:
        if n % p == 0:
            return p
    return n


def _sigmoid(v):
    return 0.5 * jnp.tanh(0.5 * v) + 0.5


def _lane(shape):
    return lax.broadcasted_iota(jnp.int32, shape, len(shape) - 1)


def _row(shape):
    return lax.broadcasted_iota(jnp.int32, shape, len(shape) - 2)


MM_BLOCK_BYTES = 36 * 1024 * 1024


def _divisors(n):
    return [t for t in range(n, 0, -LANES) if n % t == 0] if n % LANES == 0 else [n]


def _mm_tiles(m, n, k, a_bytes, b_bytes, o_bytes):
    tm = _pick(m, (512, 384, 256, 128))
    for tk in _divisors(k):
        for tn in _divisors(n):
            need = 2 * (tm * tk * a_bytes + tk * tn * b_bytes + tm * tn * o_bytes) + (tm * tn * 4 if tk < k else 0)
            if tn <= 2048 and need <= MM_BLOCK_BYTES:
                return tm, tn, tk
    raise ValueError((m, n, k))


def _mm(a, b, *, ta=False, tb=False, out_dtype=F32, name):
    (k_a, m) = a.shape if ta else a.shape[::-1]
    (n, k_b) = b.shape if tb else b.shape[::-1]
    assert k_a == k_b, (a.shape, b.shape, ta, tb)
    k = k_a
    tm, tn, tk = _mm_tiles(m, n, k, a.dtype.itemsize, b.dtype.itemsize, jnp.dtype(out_dtype).itemsize)
    nk = k // tk
    dims = (((0 if ta else 1,), (1 if tb else 0,)), ((), ()))

    def body(a_ref, b_ref, o_ref, *acc):
        part = lax.dot_general(a_ref[...].astype(BF16), b_ref[...].astype(BF16), dims, preferred_element_type=F32)
        if nk == 1:
            o_ref[...] = part.astype(out_dtype)
            return
        acc_ref, = acc
        kk = pl.program_id(2)

        @pl.when(kk == 0)
        def _():
            acc_ref[...] = part

        @pl.when(kk > 0)
        def _():
            acc_ref[...] += part

        @pl.when(kk == nk - 1)
        def _():
            o_ref[...] = acc_ref[...].astype(out_dtype)

    a_spec = pl.BlockSpec((tk, tm), lambda i, j, kk: (kk, i)) if ta else pl.BlockSpec((tm, tk), lambda i, j, kk: (i, kk))
    b_spec = pl.BlockSpec((tn, tk), lambda i, j, kk: (j, kk)) if tb else pl.BlockSpec((tk, tn), lambda i, j, kk: (kk, j))
    return pl.pallas_call(
        body, name=name,
        grid=(m // tm, n // tn, nk),
        in_specs=[a_spec, b_spec],
        out_specs=pl.BlockSpec((tm, tn), lambda i, j, kk: (i, j)),
        out_shape=jax.ShapeDtypeStruct((m, n), out_dtype),
        scratch_shapes=[] if nk == 1 else [pltpu.VMEM((tm, tn), F32)],
        compiler_params=_cparams(("arbitrary", "arbitrary", "arbitrary")),
    )(a, b)


def _rowwise(fn, row_ins, par_ins, out_defs, red_defs, *, name, tr=256):
    s = row_ins[0].shape[0]
    tr = min(tr, s)
    nr, npar, no = len(row_ins), len(par_ins), len(out_defs)

    def body(*refs):
        rin, pin = refs[:nr], refs[nr:nr + npar]
        outs, reds = refs[nr + npar:nr + npar + no], refs[nr + npar + no:]
        i = pl.program_id(0)

        @pl.when(i == 0)
        def _():
            for r in reds:
                r[...] = jnp.zeros_like(r)

        fn(i, rin, pin, outs, reds)

    in_specs = [pl.BlockSpec((tr, a.shape[1]), lambda i: (i, 0)) for a in row_ins]
    in_specs += [pl.BlockSpec(a.shape, lambda i, nd=a.ndim: (0,) * nd) for a in par_ins]
    out_specs = [pl.BlockSpec((tr, c), lambda i: (i, 0)) for c, _ in out_defs]
    out_specs += [pl.BlockSpec(shp, lambda i: (0, 0)) for shp in red_defs]
    out_shape = [jax.ShapeDtypeStruct((s, c), dt) for c, dt in out_defs]
    out_shape += [jax.ShapeDtypeStruct(shp, F32) for shp in red_defs]
    return pl.pallas_call(
        body, name=name, grid=(s // tr,), in_specs=in_specs, out_specs=out_specs, out_shape=out_shape,
        compiler_params=_cparams(("arbitrary",)),
    )(*row_ins, *par_ins)


def _rms(v):
    return lax.rsqrt(jnp.mean(v * v, axis=-1, keepdims=True) + EPS)


def _colsum(v):
    return jnp.sum(v, axis=0, keepdims=True)


def _rms_bwd(dn, n, rstd):
    return rstd * (dn - n * jnp.mean(dn * n, axis=-1, keepdims=True))


def _norm_mod_fwd(x, gmod, name):
    def fn(i, rin, pin, outs, reds):
        xv = rin[0][...]
        p = pin[0][...]
        n = xv * _rms(xv)
        outs[0][...] = ((n * p[0:1]) * (1.0 + p[1:2]) + p[2:3]).astype(BF16)

    return _rowwise(fn, [x], [gmod], [(x.shape[1], BF16)], [], name=name)[0]


def _rope(v, rot_c, rot_s):
    half = QK_ROPE // 2
    swapped = jnp.where(_lane(v.shape) < QK_NOPE + half, pltpu.roll(v, LANES - half, 1), pltpu.roll(v, half, 1))
    return v * rot_c + swapped * rot_s


def _rope_t(dv, rot_c, rot_s):
    half = QK_ROPE // 2
    ds = dv * rot_s
    lane = _lane(dv.shape)
    swapped = jnp.where(lane < QK_NOPE + half, pltpu.roll(ds, LANES - half, 1), pltpu.roll(ds, half, 1))
    in_rope = (lane >= QK_NOPE) & (lane < QK_NOPE + QK_ROPE)
    return dv * rot_c + jnp.where(in_rope, swapped, 0.0)


def _mla_prep_fwd(proj_mla, rot_c, rot_s, ng):
    o1, o2 = Q_LORA, Q_LORA + KV_LORA

    def fn(i, rin, pin, outs, reds):
        g = pin[0][...]
        ql = rin[0][:, 0:o1]
        kl = rin[0][:, o1:o2]
        outs[0][...] = (ql * _rms(ql) * g[0:1, 0:o1]).astype(BF16)
        outs[1][...] = (kl * _rms(kl) * g[0:1, o1:o2]).astype(BF16)
        kr = pltpu.roll(rin[0][:, o2:o2 + LANES], QK_NOPE, 1)
        outs[2][...] = _rope(kr, rin[1][...], rin[2][...]).astype(BF16)

    return _rowwise(fn, [proj_mla, rot_c, rot_s], [ng], [(Q_LORA, BF16), (KV_LORA, BF16), (LANES, BF16)], [],
                    name="mla_prep_fwd")


def _mla_prep_bwd(proj_mla, dqn, dkvn, dkr, rot_c, rot_s, ng):
    o1, o2 = Q_LORA, Q_LORA + KV_LORA

    def fn(i, rin, pin, outs, reds):
        g = pin[0][...]
        ql = rin[0][:, 0:o1]
        kl = rin[0][:, o1:o2]
        rq, rk = _rms(ql), _rms(kl)
        nq, nk = ql * rq, kl * rk
        dq, dk = rin[1][...], rin[2][...]
        outs[0][:, 0:o1] = _rms_bwd(dq * g[0:1, 0:o1], nq, rq).astype(BF16)
        outs[0][:, o1:o2] = _rms_bwd(dk * g[0:1, o1:o2], nk, rk).astype(BF16)
        dkr_pre = pltpu.roll(_rope_t(rin[3][...], rin[4][...], rin[5][...]), LANES - QK_NOPE, 1)
        outs[0][:, o2:] = jnp.where(_lane(dkr_pre.shape) < QK_ROPE, dkr_pre, 0.0).astype(BF16)
        reds[0][0:1, 0:o1] += _colsum(dq * nq)
        reds[0][0:1, o1:o2] += _colsum(dk * nk)

    return _rowwise(fn, [proj_mla, dqn, dkvn, dkr, rot_c, rot_s], [ng], [(MLA_W, BF16)], [(SUBLANES, MLA_W)],
                    name="mla_prep_bwd")


def _rope_heads(q, rot_c, rot_s, transpose, name):
    def fn(i, rin, pin, outs, reds):
        c, sn = rin[1][...] * Q_PRESCALE, rin[2][...] * Q_PRESCALE
        for h in range(N_HEADS):
            sl = slice(h * LANES, (h + 1) * LANES)
            v = rin[0][:, sl]
            outs[0][:, sl] = (_rope_t(v, c, sn) if transpose else _rope(v, c, sn)).astype(BF16)

    return _rowwise(fn, [q, rot_c, rot_s], [], [(q.shape[1], BF16)], [], name=name)[0]


def _merge_fwd(pr, pm, proj_g):
    d = pr.shape[1]

    def fn(i, rin, pin, outs, reds):
        outs[0][...] = (_sigmoid(rin[2][:, 0:d]) * rin[0][...] + _sigmoid(rin[2][:, d:]) * rin[1][...]).astype(BF16)

    return _rowwise(fn, [pr, pm, proj_g], [], [(d, BF16)], [], name="merge_fwd")[0]


def _merge_bwd(dmerged, pr, pm, proj_g):
    d = pr.shape[1]

    def fn(i, rin, pin, outs, reds):
        dm = rin[0][...]
        sr, sm = _sigmoid(rin[3][:, 0:d]), _sigmoid(rin[3][:, d:])
        outs[0][...] = (dm * sr).astype(BF16)
        outs[1][...] = (dm * sm).astype(BF16)
        outs[2][:, 0:d] = (dm * rin[1][...] * sr * (1.0 - sr)).astype(BF16)
        outs[2][:, d:] = (dm * rin[2][...] * sm * (1.0 - sm)).astype(BF16)

    return _rowwise(fn, [dmerged, pr, pm, proj_g], [], [(d, BF16), (d, BF16), (2 * d, BF16)], [], name="merge_bwd")


def _resid_norm_fwd(x, o, gmod):
    d = x.shape[1]

    def fn(i, rin, pin, outs, reds):
        p = pin[0][...]
        x1 = rin[0][...] + p[3:4] * rin[1][...]
        outs[0][...] = x1
        outs[1][...] = ((x1 * _rms(x1) * p[0:1]) * (1.0 + p[1:2]) + p[2:3]).astype(BF16)

    return _rowwise(fn, [x, o], [gmod], [(d, F32), (d, BF16)], [], name="resid_norm_fwd")


def _final_fwd_bwd(x1, dn, target, par):
    d = x1.shape[1]

    def fn(i, rin, pin, outs, reds):
        p = pin[0][...]
        dnv = rin[1][...]
        x2 = rin[0][...] + p[0:1] * dnv
        rstd = _rms(x2)
        n3 = x2 * rstd
        err = n3 * p[1:2] - rin[2][...]
        dy = err * (1.0 / d)
        dx2 = _rms_bwd(dy * p[1:2], n3, rstd)
        outs[0][...] = dx2
        outs[1][...] = (dx2 * p[0:1]).astype(BF16)
        reds[0][0:1, :] += _colsum(dy * n3)
        reds[0][1:2, :] += _colsum(dx2 * dnv)
        reds[0][2:3, :] += jnp.zeros((1, d), F32) + jnp.sum(err * err)

    return _rowwise(fn, [x1, dn, target], [par], [(d, F32), (d, BF16)], [(SUBLANES, d)], name="final_fwd_bwd")


def _norm2_bwd(x1, dh2, dx2, o, gmod):
    d = x1.shape[1]

    def fn(i, rin, pin, outs, reds):
        p = pin[0][...]
        x1v, dh = rin[0][...], rin[1][...]
        rstd = _rms(x1v)
        n2 = x1v * rstd
        dx1 = rin[2][...] + _rms_bwd(dh * (p[0:1] * (1.0 + p[1:2])), n2, rstd)
        outs[0][...] = dx1
        outs[1][...] = (dx1 * p[3:4]).astype(BF16)
        reds[0][0:1, :] += _colsum(dh * n2 * (1.0 + p[1:2]))
        reds[0][1:2, :] += _colsum(dh * n2 * p[0:1])
        reds[0][2:3, :] += _colsum(dh)
        reds[0][3:4, :] += _colsum(dx1 * rin[3][...])

    return _rowwise(fn, [x1, dh2, dx2, o], [gmod], [(d, F32), (d, BF16)], [(SUBLANES, d)], name="norm2_bwd")


def _norm1_bwd(x, dh_a, dh_b, dh_c, dx1, gmod):
    d = x.shape[1]

    def fn(i, rin, pin, outs, reds):
        p = pin[0][...]
        xv = rin[0][...]
        dh = rin[1][...] + rin[2][...] + rin[3][...]
        rstd = _rms(xv)
        n1 = xv * rstd
        outs[0][...] = rin[4][...] + _rms_bwd(dh * (p[0:1] * (1.0 + p[1:2])), n1, rstd)
        reds[0][0:1, :] += _colsum(dh * n1 * (1.0 + p[1:2]))
        reds[0][1:2, :] += _colsum(dh * n1 * p[0:1])
        reds[0][2:3, :] += _colsum(dh)

    return _rowwise(fn, [x, dh_a, dh_b, dh_c, dx1], [gmod], [(d, F32)], [(SUBLANES, d)], name="norm1_bwd")


RNN_CHUNK = 512


def _shift_down(ref, base, n, j):
    v = ref[pl.ds(base, n + SUBLANES), :]
    return v[SUBLANES:] if j == 0 else pltpu.roll(v, j, 0)[SUBLANES:]


def _shift_up(ref, base, n, j, top_pad):
    v = ref[pl.ds(base + top_pad, n + SUBLANES), :]
    return v[:n] if j == 0 else pltpu.roll(v, n + SUBLANES - j, 0)[:n]


SCAN_GROUP = 128


def _scan_sizes(s):
    sizes = [s]
    while sizes[-1] > SUBLANES:
        assert sizes[-1] % SUBLANES == 0, s
        sizes.append(sizes[-1] // SUBLANES)
    return sizes


def _scan_scratch(s):
    return [pltpu.VMEM((n + 2 * SUBLANES, LANES), F32) for n in _scan_sizes(s)[1:] for _ in range(2)]


def _linear_scan(a_ref, b_ref, out_ref, a_off, s, reverse, levels):
    sizes = _scan_sizes(s)
    lv = [(a_ref, b_ref, a_off, 0)] + [(levels[2 * i], levels[2 * i + 1], 0, SUBLANES) for i in range(len(sizes) - 1)]
    zero8 = jnp.zeros((SUBLANES, LANES), F32)
    for (ar, br, _, _), n in zip(lv[1:], sizes[1:]):
        br[0:SUBLANES, :] = zero8
        br[pl.ds(n + SUBLANES, SUBLANES), :] = zero8
    order = list(range(SUBLANES - 1, -1, -1)) if reverse else list(range(SUBLANES))

    for lvl in range(len(sizes) - 1):
        ar, br, aoff, off = lv[lvl]
        m = sizes[lvl + 1]
        g = min(m, SCAN_GROUP)
        for t0 in range(0, m, g):
            acc_a = acc_b = None
            for r in order:
                sa = pl.ds(off + SUBLANES * t0 + r + aoff, g, stride=SUBLANES)
                sb = pl.ds(off + SUBLANES * t0 + r, g, stride=SUBLANES)
                a, b = ar[sa, :], br[sb, :]
                if acc_a is None:
                    acc_a, acc_b = a, b
                else:
                    acc_b = a * acc_b + b
                    acc_a = a * acc_a
            lv[lvl + 1][0][pl.ds(SUBLANES + t0, g), :] = acc_a
            lv[lvl + 1][1][pl.ds(SUBLANES + t0, g), :] = acc_b

    ar, br, _, off = lv[-1]
    n = sizes[-1]
    a, b = ar[pl.ds(off, n), :], br[pl.ds(off, n), :]
    h, rows = jnp.zeros((1, LANES), F32), [None] * n
    for j in (range(n - 1, -1, -1) if reverse else range(n)):
        h = a[j:j + 1, :] * h + b[j:j + 1, :]
        rows[j] = h
    br[pl.ds(off, n), :] = jnp.concatenate(rows, axis=0)

    for lvl in range(len(sizes) - 2, -1, -1):
        ar, br, aoff, off = lv[lvl]
        m = sizes[lvl + 1]
        up = lv[lvl + 1][1]
        dst = out_ref if lvl == 0 else br
        g = min(m, SCAN_GROUP)
        for t0 in range(0, m, g):
            h = _shift_up(up, t0, g, 1, SUBLANES) if reverse else _shift_down(up, t0, g, 1)
            for r in order:
                sa = pl.ds(off + SUBLANES * t0 + r + aoff, g, stride=SUBLANES)
                sb = pl.ds(off + SUBLANES * t0 + r, g, stride=SUBLANES)
                h = ar[sa, :] * h + br[sb, :]
                dst[sb, :] = h


def _one_minus_exp(z):
    series = -z * (1.0 + z * (0.5 + z * (1.0 / 6.0 + z * (1.0 / 24.0 + z * (1.0 / 120.0 + z * (1.0 / 720.0))))))
    return jnp.where(z > -0.1, series, 1.0 - jnp.exp(z))


def _softplus(v):
    return jnp.maximum(v, 0.0) + jnp.log(1.0 + jnp.exp(-jnp.abs(v)))


def _rnn_gates(xc, w, wa, wx, sp):
    xb = xc.astype(BF16)
    ra = _sigmoid(jnp.dot(xb, wa, preferred_element_type=F32) + w[5:6])
    ix = _sigmoid(jnp.dot(xb, wx, preferred_element_type=F32) + w[6:7])
    la = (-LRU_C) * ra * sp
    a = jnp.exp(la)
    mult = jnp.sqrt(_one_minus_exp(2.0 * la))
    return ra, ix, a, mult


def _rnn_fwd(x_rnn, keep, rp, wa_bd, wx_bd):
    s, r = x_rnn.shape
    ts = min(RNN_CHUNK, s)

    def body(x_ref, keep_ref, rp_ref, wa_ref, wx_ref, xc_ref, ra_ref, ix_ref, hs_ref, xpad, a_s, b_s, *levels):
        xpad[0:SUBLANES, :] = jnp.zeros((SUBLANES, LANES), F32)
        xpad[SUBLANES:, :] = x_ref[...]
        w = rp_ref[...]
        sp = _softplus(-w[7:8])
        wa, wx = wa_ref[0], wx_ref[0]

        def chunk(c, carry):
            base = pl.multiple_of(c * ts, ts)
            xc = w[4:5] + w[3:4] * _shift_down(xpad, base, ts, 0)
            for j in range(1, 4):
                xc = xc + w[3 - j:4 - j] * _shift_down(xpad, base, ts, j)
            ra, ix, a, mult = _rnn_gates(xc, w, wa, wx, sp)
            kp = keep_ref[pl.ds(base, ts), :]
            xc_ref[pl.ds(base, ts), :] = xc
            ra_ref[pl.ds(base, ts), :] = ra
            ix_ref[pl.ds(base, ts), :] = ix
            a_s[pl.ds(base, ts), :] = a * kp
            b_s[pl.ds(base, ts), :] = jnp.where(kp > 0.0, mult, 1.0) * (ix * xc)
            return carry

        lax.fori_loop(0, s // ts, chunk, 0)

        _linear_scan(a_s, b_s, hs_ref, 0, s, False, levels)

    col = pl.BlockSpec((s, LANES), lambda g: (0, g))
    return pl.pallas_call(
        body, name="rnn_fwd", grid=(r // LANES,),
        in_specs=[col, pl.BlockSpec((s, 1), lambda g: (0, 0)), pl.BlockSpec((SUBLANES, LANES), lambda g: (0, g)),
                  pl.BlockSpec((1, LANES, LANES), lambda g: (g, 0, 0)), pl.BlockSpec((1, LANES, LANES), lambda g: (g, 0, 0))],
        out_specs=[col] * 4,
        out_shape=[jax.ShapeDtypeStruct((s, r), F32)] * 4,
        scratch_shapes=[pltpu.VMEM((s + SUBLANES, LANES), F32), pltpu.VMEM((s, LANES), F32), pltpu.VMEM((s, LANES), F32),
                        *_scan_scratch(s)],
        compiler_params=_cparams(("arbitrary",)),
    )(x_rnn, keep, rp, wa_bd, wx_bd)


def _rnn_bwd(x_rnn, xc, ra, ix, hs, dy, keep, rp, wa_bd, wx_bd):
    s, r = x_rnn.shape
    ts = min(RNN_CHUNK, s)

    def body(x_ref, xc_ref, ra_ref, ix_ref, hs_ref, dy_ref, keep_ref, rp_ref, wa_ref, wx_ref,
             dx_ref, dwa_ref, dwx_ref, red_ref, xpad, hpad, a_s, dh_s, dxc_s, *levels):
        zero8 = jnp.zeros((SUBLANES, LANES), F32)
        xpad[0:SUBLANES, :] = zero8
        xpad[SUBLANES:, :] = x_ref[...]
        hpad[0:SUBLANES, :] = zero8
        hpad[SUBLANES:, :] = hs_ref[...]
        a_s[s:, :] = zero8
        dxc_s[s:, :] = zero8
        w = rp_ref[...]
        sp = _softplus(-w[7:8])
        wa, wx = wa_ref[0], wx_ref[0]

        def decay(c, carry):
            base = pl.multiple_of(c * ts, ts)
            a = jnp.exp((-LRU_C) * ra_ref[pl.ds(base, ts), :] * sp)
            a_s[pl.ds(base, ts), :] = a * keep_ref[pl.ds(base, ts), :]
            return carry

        lax.fori_loop(0, s // ts, decay, 0)

        _linear_scan(a_s, dy_ref, dh_s, 1, s, True, levels)

        def gates(c, carry):
            dwa, dwx, d_ba, d_bx, d_sp, d_cb = carry
            base = pl.multiple_of(c * ts, ts)
            xcv = xc_ref[pl.ds(base, ts), :]
            rav = ra_ref[pl.ds(base, ts), :]
            ixv = ix_ref[pl.ds(base, ts), :]
            kp = keep_ref[pl.ds(base, ts), :]
            dh = dh_s[pl.ds(base, ts), :]
            h_prev = _shift_down(hpad, base, ts, 1)
            la = (-LRU_C) * rav * sp
            a = jnp.exp(la)
            mult = jnp.sqrt(_one_minus_exp(2.0 * la))
            mult_eff = jnp.where(kp > 0.0, mult, 1.0)
            d_a = dh * h_prev * kp
            d_mult = dh * (ixv * xcv) * kp
            d_ix = dh * mult_eff * xcv
            d_xc = dh * mult_eff * ixv
            d_la = d_a * a - d_mult * (a * a) / mult
            d_pa = d_la * ((-LRU_C) * sp) * rav * (1.0 - rav)
            d_px = d_ix * ixv * (1.0 - ixv)
            xb = xcv.astype(BF16)
            pab, pxb = d_pa.astype(BF16), d_px.astype(BF16)
            tn = (((0,), (0,)), ((), ()))
            nt_ = (((1,), (1,)), ((), ()))
            dwa = dwa + lax.dot_general(xb, pab, tn, preferred_element_type=F32)
            dwx = dwx + lax.dot_general(xb, pxb, tn, preferred_element_type=F32)
            d_xc = d_xc + lax.dot_general(pab, wa, nt_, preferred_element_type=F32)
            d_xc = d_xc + lax.dot_general(pxb, wx, nt_, preferred_element_type=F32)
            dxc_s[pl.ds(base, ts), :] = d_xc
            return (dwa, dwx, d_ba + _colsum(d_pa), d_bx + _colsum(d_px),
                    d_sp + _colsum(d_la * ((-LRU_C) * rav)), d_cb + _colsum(d_xc))

        z1 = jnp.zeros((1, LANES), F32)
        zw = jnp.zeros((LANES, LANES), F32)
        dwa, dwx, d_ba, d_bx, d_sp, d_cb = lax.fori_loop(0, s // ts, gates, (zw, zw, z1, z1, z1, z1))
        dwa_ref[0] = dwa
        dwx_ref[0] = dwx

        def conv(c, carry):
            base = pl.multiple_of(c * ts, ts)
            d_here = dxc_s[pl.ds(base, ts), :]
            dx = w[3:4] * d_here
            for j in range(1, 4):
                dx = dx + w[3 - j:4 - j] * _shift_up(dxc_s, base, ts, j, 0)
            dx_ref[pl.ds(base, ts), :] = dx.astype(BF16)
            return tuple(carry[k] + _colsum(d_here * _shift_down(xpad, base, ts, 3 - k)) for k in range(4))

        d_w = lax.fori_loop(0, s // ts, conv, (z1, z1, z1, z1))
        d_lru = d_sp * (-_sigmoid(-w[7:8]))
        red_ref[...] = jnp.concatenate(list(d_w) + [d_cb, d_ba, d_bx, d_lru], axis=0)

    col = pl.BlockSpec((s, LANES), lambda g: (0, g))
    par = pl.BlockSpec((SUBLANES, LANES), lambda g: (0, g))
    wsp = pl.BlockSpec((1, LANES, LANES), lambda g: (g, 0, 0))
    return pl.pallas_call(
        body, name="rnn_bwd", grid=(r // LANES,),
        in_specs=[col] * 6 + [pl.BlockSpec((s, 1), lambda g: (0, 0)), par, wsp, wsp],
        out_specs=[col, wsp, wsp, par],
        out_shape=[jax.ShapeDtypeStruct((s, r), BF16), jax.ShapeDtypeStruct((r // LANES, LANES, LANES), F32),
                   jax.ShapeDtypeStruct((r // LANES, LANES, LANES), F32), jax.ShapeDtypeStruct((SUBLANES, r), F32)],
        scratch_shapes=[pltpu.VMEM((s + SUBLANES, LANES), F32), pltpu.VMEM((s + SUBLANES, LANES), F32),
                        pltpu.VMEM((s + SUBLANES, LANES), F32), pltpu.VMEM((s, LANES), F32),
                        pltpu.VMEM((s + SUBLANES, LANES), F32), *_scan_scratch(s)],
        compiler_params=_cparams(("arbitrary",)),
    )(x_rnn, xc, ra, ix, hs, dy, keep, rp, wa_bd, wx_bd)


ATT_BLOCK = 512


LOG2E = 1.4426950408889634
LN2 = 0.6931471805599453
Q_PRESCALE = ATT_SCALE * LOG2E


def _att_scores(q, kvt, krt, diagonal):
    kt_eff = jnp.where(_row(kvt.shape) < QK_NOPE, kvt, krt)
    sc = jnp.dot(q, kt_eff, preferred_element_type=F32)
    if diagonal:
        sc = jnp.where(lax.broadcasted_iota(jnp.int32, sc.shape, 1) <= lax.broadcasted_iota(jnp.int32, sc.shape, 0), sc, -jnp.inf)
    return sc


def _att_fwd(q, kv, kvt, krt):
    s = q.shape[0]
    t = min(ATT_BLOCK, s)
    nb = s // t

    def body(q_ref, kv_ref, kvt_ref, krt_ref, y_ref, lse_ref, m_s, acc_s):
        i, j = pl.program_id(1), pl.program_id(2)

        @pl.when(j == 0)
        def _():
            m_s[...] = jnp.full(m_s.shape, -jnp.inf, F32)
            acc_s[...] = jnp.zeros(acc_s.shape, F32)

        def step(diagonal):
            krt_b = krt_ref[...]
            lane = _lane((t, LANES))
            for hh in range(2):
                sl = slice(hh * LANES, (hh + 1) * LANES)
                kvb = kv_ref[:, sl]
                sc = _att_scores(q_ref[:, sl], kvt_ref[sl, :], krt_b, diagonal)
                m_prev = m_s[hh]
                m_new = jnp.maximum(m_prev, jnp.max(sc, axis=-1, keepdims=True))
                alpha = jnp.exp2(m_prev - m_new)
                p = jnp.exp2(sc - m_new[:, 0:1]).astype(BF16)
                ones_v = jnp.where(lane < QK_NOPE, jnp.ones_like(kvb), kvb)
                acc_s[hh] = alpha * acc_s[hh] + jnp.dot(p, ones_v, preferred_element_type=F32)
                m_s[hh] = m_new

        @pl.when(j < i)
        def _():
            step(False)

        @pl.when(j == i)
        def _():
            step(True)
            lane = _lane((t, LANES))
            a0, a1 = acc_s[0], acc_s[1]
            l0, l1 = a0[:, 0:1], a1[:, 0:1]
            y_ref[...] = jnp.where(lane < V_HEAD, pltpu.roll(a0 / l0, V_HEAD, 1), a1 / l1).astype(BF16)
            lse_ref[...] = jnp.where(lane < V_HEAD, m_s[0] + jnp.log(l0) * LOG2E, m_s[1] + jnp.log(l1) * LOG2E)

    return pl.pallas_call(
        body, name="att_fwd", grid=(N_HEADS // 2, nb, nb),
        in_specs=[pl.BlockSpec((t, 2 * LANES), lambda p, i, j: (i, p)),
                  pl.BlockSpec((t, 2 * LANES), lambda p, i, j: (jnp.minimum(j, i), p)),
                  pl.BlockSpec((2 * LANES, t), lambda p, i, j: (p, jnp.minimum(j, i))),
                  pl.BlockSpec((LANES, t), lambda p, i, j: (0, jnp.minimum(j, i)))],
        out_specs=[pl.BlockSpec((t, LANES), lambda p, i, j: (i, p))] * 2,
        out_shape=[jax.ShapeDtypeStruct((s, N_HEADS * V_HEAD), BF16), jax.ShapeDtypeStruct((s, N_HEADS * V_HEAD), F32)],
        scratch_shapes=[pltpu.VMEM((2, t, LANES), F32)] * 2,
        compiler_params=_cparams(("arbitrary", "arbitrary", "arbitrary")),
    )(q, kv, kvt, krt)


def _att_bwd(q, qt, kv, kvt, kr, krt, y, lse, dy, dyt):
    s = q.shape[0]
    t = min(ATT_BLOCK, s)
    nb = s // t

    def body(q_ref, qt_ref, kv_ref, kvt_ref, kr_ref, krt_ref, y_ref, lse_ref, dy_ref, dyt_ref,
             dq_ref, dkvt_ref, dkrt_ref, dkv_s):
        p_, j, i = pl.program_id(0), pl.program_id(1), pl.program_id(2)

        @pl.when((p_ == 0) & (j == 0) & (i == 0))
        def _():
            dkrt_ref[...] = jnp.zeros(dkrt_ref.shape, F32)

        @pl.when((j == 0) & (i == 0))
        def _():
            dq_ref[...] = jnp.zeros(dq_ref.shape, F32)

        @pl.when(i == 0)
        def _():
            dkv_s[...] = jnp.zeros(dkv_s.shape, F32)

        def step(diagonal):
            lane = _lane((t, LANES))
            row = _row((LANES, t))
            krb, krt_b = kr_ref[...], krt_ref[...]
            dyv = dy_ref[...]
            yv = y_ref[...].astype(F32)
            lsev = lse_ref[...]
            dyt_b = dyt_ref[...]
            rows = pl.ds(pl.multiple_of(i * t, t), t)
            cols = pl.ds(pl.multiple_of(j * t, t), t)
            zeros_t = jnp.zeros((V_HEAD, t), BF16)
            dkr_acc = jnp.zeros((LANES, t), F32)
            for hh in range(2):
                sl = slice(hh * LANES, (hh + 1) * LANES)
                kvb, kvt_b = kv_ref[:, sl], kvt_ref[sl, :]
                k_eff = jnp.where(lane < QK_NOPE, kvb, krb)
                sc = _att_scores(q_ref[:, sl], kvt_b, krt_b, diagonal)
                if hh == 0:
                    do_pad = jnp.where(lane >= V_HEAD, pltpu.roll(dyv, V_HEAD, 1), 0.0)
                    o_pad = jnp.where(lane >= V_HEAD, pltpu.roll(yv, V_HEAD, 1), 0.0)
                    lse_h = lsev[:, 0:1]
                else:
                    do_pad = jnp.where(lane >= V_HEAD, dyv, 0.0)
                    o_pad = jnp.where(lane >= V_HEAD, yv, 0.0)
                    lse_h = lsev[:, LANES - 1:LANES]
                dot_pad = jnp.concatenate([zeros_t, dyt_b[hh * V_HEAD:(hh + 1) * V_HEAD, :]], axis=0)
                do_ln2 = do_pad * LN2
                delta = jnp.sum(do_ln2 * o_pad, axis=-1, keepdims=True)
                prob = jnp.exp2(sc - lse_h)
                dvt = jnp.dot(dot_pad, prob.astype(BF16), preferred_element_type=F32)
                dp = jnp.dot(do_ln2.astype(BF16), kvt_b, preferred_element_type=F32)
                ds = (prob * (dp - delta)).astype(BF16)
                dq_ref[rows, sl] += jnp.dot(ds, k_eff, preferred_element_type=F32)
                dkt = jnp.dot(qt_ref[sl, :], ds, preferred_element_type=F32)
                dkv_s[hh] += dvt + jnp.where(row < QK_NOPE, dkt, 0.0)
                dkr_acc = dkr_acc + jnp.where(row >= QK_NOPE, dkt, 0.0)
            dkrt_ref[:, cols] += dkr_acc

        @pl.when(i > j)
        def _():
            step(False)

        @pl.when(i == j)
        def _():
            step(True)

        @pl.when(i == nb - 1)
        def _():
            dkvt_ref[0:LANES, :] = dkv_s[0].astype(BF16)
            dkvt_ref[LANES:, :] = dkv_s[1].astype(BF16)

    qi = lambda p, j, i: (jnp.maximum(i, j), p)
    qti = lambda p, j, i: (p, jnp.maximum(i, j))
    return pl.pallas_call(
        body, name="att_bwd", grid=(N_HEADS // 2, nb, nb),
        in_specs=[pl.BlockSpec((t, 2 * LANES), qi), pl.BlockSpec((2 * LANES, t), qti),
                  pl.BlockSpec((t, 2 * LANES), lambda p, j, i: (j, p)), pl.BlockSpec((2 * LANES, t), lambda p, j, i: (p, j)),
                  pl.BlockSpec((t, LANES), lambda p, j, i: (j, 0)), pl.BlockSpec((LANES, t), lambda p, j, i: (0, j)),
                  pl.BlockSpec((t, LANES), qi), pl.BlockSpec((t, LANES), qi), pl.BlockSpec((t, LANES), qi),
                  pl.BlockSpec((LANES, t), qti)],
        out_specs=[pl.BlockSpec((s, 2 * LANES), lambda p, j, i: (0, p)),
                   pl.BlockSpec((2 * LANES, t), lambda p, j, i: (p, j)),
                   pl.BlockSpec((LANES, s), lambda p, j, i: (0, 0))],
        out_shape=[jax.ShapeDtypeStruct((s, N_HEADS * LANES), F32), jax.ShapeDtypeStruct((N_HEADS * LANES, s), BF16),
                   jax.ShapeDtypeStruct((LANES, s), F32)],
        scratch_shapes=[pltpu.VMEM((2, LANES, t), F32)],
        compiler_params=_cparams(("arbitrary", "arbitrary", "arbitrary")),
    )(q, qt, kv, kvt, kr, krt, y, lse, dy, dyt)


FFN_COLS = 256


def _ffn_conv(pad_ref, w, base, n):
    u = w[3:4] + w[2:3] * _shift_down(pad_ref, base, n, 0)
    for j in range(1, 3):
        u = u + w[2 - j:3 - j] * _shift_down(pad_ref, base, n, j)
    return u


def _ffn_act_fwd(up, fp):
    s, f2 = up.shape
    f = f2 // 2
    tc = FFN_COLS
    ts = min(RNN_CHUNK, s)
    nfb = f // tc

    def body(ug_ref, uv_ref, wg_ref, wv_ref, act_ref, gpad, vpad):
        zero8 = jnp.zeros((SUBLANES, tc), F32)
        gpad[0:SUBLANES, :] = zero8
        vpad[0:SUBLANES, :] = zero8
        gpad[SUBLANES:, :] = ug_ref[...]
        vpad[SUBLANES:, :] = uv_ref[...]
        wg, wv = wg_ref[...], wv_ref[...]

        def chunk(c, carry):
            base = pl.multiple_of(c * ts, ts)
            g = _ffn_conv(gpad, wg, base, ts)
            v = _ffn_conv(vpad, wv, base, ts)
            act_ref[pl.ds(base, ts), :] = (g * _sigmoid(g) * v).astype(BF16)
            return carry

        lax.fori_loop(0, s // ts, chunk, 0)

    return pl.pallas_call(
        body, name="ffn_act_fwd", grid=(nfb,),
        in_specs=[pl.BlockSpec((s, tc), lambda b: (0, b)), pl.BlockSpec((s, tc), lambda b: (0, b + nfb)),
                  pl.BlockSpec((SUBLANES, tc), lambda b: (0, b)), pl.BlockSpec((SUBLANES, tc), lambda b: (0, b + nfb))],
        out_specs=pl.BlockSpec((s, tc), lambda b: (0, b)),
        out_shape=jax.ShapeDtypeStruct((s, f), BF16),
        scratch_shapes=[pltpu.VMEM((s + SUBLANES, tc), F32)] * 2,
        compiler_params=_cparams(("arbitrary",)),
    )(up, up, fp, fp)


def _ffn_act_bwd(up, dact, fp):
    s, f2 = up.shape
    f = f2 // 2
    tc = FFN_COLS
    ts = min(RNN_CHUNK, s)
    nfb = f // tc

    def body(ug_ref, uv_ref, da_ref, wg_ref, wv_ref, dup_ref, red_ref, gpad, vpad, dgs, dvs):
        half = pl.program_id(1)
        wg, wv = wg_ref[...], wv_ref[...]

        @pl.when(half == 0)
        def _():
            zero8 = jnp.zeros((SUBLANES, tc), F32)
            gpad[0:SUBLANES, :] = zero8
            vpad[0:SUBLANES, :] = zero8
            gpad[SUBLANES:, :] = ug_ref[...]
            vpad[SUBLANES:, :] = uv_ref[...]
            dgs[s:, :] = zero8
            dvs[s:, :] = zero8

            def act(c, carry):
                base = pl.multiple_of(c * ts, ts)
                g = _ffn_conv(gpad, wg, base, ts)
                v = _ffn_conv(vpad, wv, base, ts)
                da = da_ref[pl.ds(base, ts), :]
                sg = _sigmoid(g)
                dgs[pl.ds(base, ts), :] = da * v * (sg * (1.0 + g * (1.0 - sg)))
                dvs[pl.ds(base, ts), :] = da * (g * sg)
                return carry

            lax.fori_loop(0, s // ts, act, 0)

        def conv_t(src, pad, w, out_ref, red_ref):
            def chunk(c, carry):
                base = pl.multiple_of(c * ts, ts)
                d_here = src[pl.ds(base, ts), :]
                dx = w[2:3] * d_here
                for j in range(1, 3):
                    dx = dx + w[2 - j:3 - j] * _shift_up(src, base, ts, j, 0)
                out_ref[pl.ds(base, ts), :] = dx.astype(BF16)
                taps = tuple(carry[k] + _colsum(d_here * _shift_down(pad, base, ts, 2 - k)) for k in range(3))
                return taps + (carry[3] + _colsum(d_here),)

            z1 = jnp.zeros((1, tc), F32)
            red = lax.fori_loop(0, s // ts, chunk, (z1, z1, z1, z1))
            red_ref[...] = jnp.concatenate(list(red) + [jnp.zeros((4, tc), F32)], axis=0)

        @pl.when(half == 0)
        def _():
            conv_t(dgs, gpad, wg, dup_ref, red_ref)

        @pl.when(half == 1)
        def _():
            conv_t(dvs, vpad, wv, dup_ref, red_ref)

    gcol = pl.BlockSpec((s, tc), lambda b, h: (0, b))
    vcol = pl.BlockSpec((s, tc), lambda b, h: (0, b + nfb))
    gpar = pl.BlockSpec((SUBLANES, tc), lambda b, h: (0, b))
    vpar = pl.BlockSpec((SUBLANES, tc), lambda b, h: (0, b + nfb))
    return pl.pallas_call(
        body, name="ffn_act_bwd", grid=(nfb, 2),
        in_specs=[gcol, vcol, gcol, gpar, vpar],
        out_specs=[pl.BlockSpec((s, tc), lambda b, h: (0, b + h * nfb)),
                   pl.BlockSpec((SUBLANES, tc), lambda b, h: (0, b + h * nfb))],
        out_shape=[jax.ShapeDtypeStruct((s, f2), BF16), jax.ShapeDtypeStruct((SUBLANES, f2), F32)],
        scratch_shapes=[pltpu.VMEM((s + SUBLANES, tc), F32)] * 4,
        compiler_params=_cparams(("arbitrary", "arbitrary")),
    )(up, up, dact, fp, fp)


def _rows8(rows, width):
    rows = [r.reshape(1, width).astype(F32) for r in rows]
    return jnp.concatenate(rows + [jnp.zeros((SUBLANES - len(rows), width), F32)], axis=0)


def _block_diag(w):
    n, b, _ = w.shape
    w = w.reshape(n // 2, 2, b, b)
    z = jnp.zeros((n // 2, b, b), w.dtype)
    top = jnp.concatenate([w[:, 0], z], axis=2)
    bot = jnp.concatenate([z, w[:, 1]], axis=2)
    return jnp.concatenate([top, bot], axis=1)


def _block_diag_t(bd):
    n, b2, _ = bd.shape
    b = b2 // 2
    return jnp.stack([bd[:, :b, :b], bd[:, b:, b:]], axis=1).reshape(2 * n, b, b)


def _local_step(x, mod, positions, target, w_in, fetch, sm, emit):
    s, d = x.shape
    o_rnn, o_mla = D_RNN, D_RNN + Q_LORA + KV_LORA + QK_ROPE
    wts = {}
    w_in_rnn = w_in[:, :o_rnn]
    w_in_mla = jnp.concatenate([w_in[:, o_rnn:o_mla], jnp.zeros((d, MLA_W - (o_mla - o_rnn)), w_in.dtype)], axis=1)
    w_in_g = w_in[:, o_mla:]
    hd = QK_NOPE + QK_ROPE
    wa_bd = _block_diag(sm["w_gate_a"]).astype(BF16)
    wx_bd = _block_diag(sm["w_gate_x"]).astype(BF16)

    pos = positions.reshape(s)
    half = QK_ROPE // 2
    inv_freq = ROPE_THETA ** (-jnp.arange(half, dtype=F32) / half)
    ang = pos.astype(F32)[:, None] * inv_freq
    cos, sin = jnp.cos(ang), jnp.sin(ang)
    rot_c = jnp.concatenate([jnp.ones((s, QK_NOPE), F32), cos, cos, jnp.ones((s, LANES - hd), F32)], axis=1)
    rot_s = jnp.concatenate([jnp.zeros((s, QK_NOPE), F32), -sin, sin, jnp.zeros((s, LANES - hd), F32)], axis=1)
    keep = (pos != 0).astype(F32).reshape(s, 1)

    gmod1 = _rows8([sm["norm1_g"], mod[1], mod[0]], d)
    gmod2 = _rows8([sm["norm2_g"], mod[4], mod[3], mod[2]], d)
    rp = jnp.concatenate([sm["conv_w"].reshape(4, D_RNN), _rows8([sm["conv_b"], sm["b_gate_a"], sm["b_gate_x"], sm["lru_param"]], D_RNN)[:4]], axis=0)
    fp = _rows8([sm["ffn_conv_w"][0], sm["ffn_conv_w"][1], sm["ffn_conv_w"][2], sm["ffn_conv_b"]], 2 * D_FF)
    ng = _rows8([jnp.concatenate([sm["q_norm_g"].reshape(-1), sm["kv_norm_g"].reshape(-1), jnp.zeros((MLA_W - Q_LORA - KV_LORA,), F32)])], MLA_W)
    fpar = _rows8([mod[5], sm["final_g"]], d)

    h = _norm_mod_fwd(x, gmod1, "norm1_fwd")
    proj_rnn = _mm(h, w_in_rnn, name="mm_in_rnn")
    proj_mla = _mm(h, w_in_mla, name="mm_in_mla")
    proj_g = _mm(h, w_in_g, name="mm_in_g")
    xc, ra, ix, hs = _rnn_fwd(proj_rnn, keep, rp, wa_bd, wx_bd)
    qn, kvn, kr = _mla_prep_fwd(proj_mla, rot_c, rot_s, ng)
    wts.update(fetch(("w_uq", "w_ukv"), kr))
    w_uq_p = jnp.pad(wts["w_uq"].reshape(Q_LORA, N_HEADS, hd), ((0, 0), (0, 0), (0, LANES - hd))).reshape(Q_LORA, N_HEADS * LANES)
    w_ukv = wts["w_ukv"]
    q_rot = _rope_heads(_mm(qn, w_uq_p, name="mm_uq"), rot_c, rot_s, False, "rope_fwd")
    kv = _mm(kvn, w_ukv, out_dtype=BF16, name="mm_ukv")
    kvt, krt = jnp.transpose(kv), jnp.transpose(kr)
    y_mla, lse = _att_fwd(q_rot, kv, kvt, krt)
    wts.update(fetch(("w_proj_rnn", "w_proj_mla", "w_out", "w_up", "w_down"), lse))
    pr = _mm(hs, wts["w_proj_rnn"], name="mm_proj_rnn")
    pm = _mm(y_mla, wts["w_proj_mla"], name="mm_proj_mla")
    merged = _merge_fwd(pr, pm, proj_g)
    o = _mm(merged, wts["w_out"], name="mm_out")
    x1, h2 = _resid_norm_fwd(x, o, gmod2)
    up = _mm(h2, wts["w_up"], name="mm_up")
    act = _ffn_act_fwd(up, fp)
    dn = _mm(act, wts["w_down"], name="mm_down")

    dx2, ddn, red_f = _final_fwd_bwd(x1, dn, target, fpar)
    dact = _mm(ddn, wts["w_down"], tb=True, name="mm_d_act")
    tok = emit("w_down", _mm(act, ddn, ta=True, out_dtype=BF16, name="mm_dw_down"))
    dup, red_ffn = _ffn_act_bwd(up, dact, fp + tok)
    dh2 = _mm(dup, wts["w_up"], tb=True, name="mm_d_h2")
    tok = tok + emit("w_up", _mm(h2, dup, ta=True, out_dtype=BF16, name="mm_dw_up"))
    dx1, do, red_2 = _norm2_bwd(x1, dh2, dx2, o, gmod2 + tok)
    dmerged = _mm(do, wts["w_out"], tb=True, name="mm_d_merged")
    tok = tok + emit("w_out", _mm(merged, do, ta=True, out_dtype=BF16, name="mm_dw_out"))
    dpr, dpm, dg = _merge_bwd(dmerged, pr, pm, proj_g)
    dy_rnn = _mm(dpr, wts["w_proj_rnn"], tb=True, name="mm_d_yrnn")
    tok = tok + emit("w_proj_rnn", _mm(hs, dpr, ta=True, out_dtype=BF16, name="mm_dw_proj_rnn"))
    dy_mla = _mm(dpm, wts["w_proj_mla"], tb=True, name="mm_d_ymla")
    tok = tok + emit("w_proj_mla", _mm(y_mla, dpm, ta=True, out_dtype=BF16, name="mm_dw_proj_mla"))
    dq_rot, dkvt, dkrt = _att_bwd(q_rot, jnp.transpose(q_rot), kv, kvt, kr, krt, y_mla, lse, dy_mla,
                                  jnp.transpose(dy_mla.astype(BF16)))
    dq = _rope_heads(dq_rot, rot_c, rot_s, True, "rope_bwd")
    dqn = _mm(dq, w_uq_p, tb=True, name="mm_d_qn")
    dw_uq_p = _mm(qn, dq, ta=True, out_dtype=BF16, name="mm_dw_uq")
    tok = tok + emit("w_uq", dw_uq_p.reshape(Q_LORA, N_HEADS, LANES)[:, :, :hd].reshape(Q_LORA, N_HEADS * hd))
    dkvn = jnp.transpose(_mm(w_ukv, dkvt, name="mm_d_kvn"))
    tok = tok + emit("w_ukv", jnp.transpose(_mm(dkvt, kvn, out_dtype=BF16, name="mm_dw_ukv")))
    dproj_mla, red_m = _mla_prep_bwd(proj_mla, dqn, dkvn, jnp.transpose(dkrt), rot_c, rot_s, ng + tok)
    dx_rnn, dwa_bd, dwx_bd, red_r = _rnn_bwd(proj_rnn, xc, ra, ix, hs, dy_rnn, keep, rp + tok, wa_bd, wx_bd)
    dw_in = jnp.concatenate([
        _mm(h, dx_rnn, ta=True, out_dtype=BF16, name="mm_dw_in_rnn"),
        _mm(h, dproj_mla, ta=True, out_dtype=BF16, name="mm_dw_in_mla")[:, :o_mla - o_rnn],
        _mm(h, dg, ta=True, out_dtype=BF16, name="mm_dw_in_g")], axis=1)
    tok = tok + emit("w_in", dw_in)
    dh_a = _mm(dx_rnn, w_in_rnn, tb=True, name="mm_d_h_rnn")
    dh_b = _mm(dproj_mla, w_in_mla, tb=True, name="mm_d_h_mla")
    dh_c = _mm(dg, w_in_g, tb=True, name="mm_d_h_g")
    grad_x, red_1 = _norm1_bwd(x, dh_a, dh_b, dh_c, dx1, gmod1 + tok)

    gs = {
        "norm1_g": red_1[0], "conv_w": red_r[0:4], "conv_b": red_r[4], "w_gate_a": _block_diag_t(dwa_bd),
        "b_gate_a": red_r[5], "w_gate_x": _block_diag_t(dwx_bd), "b_gate_x": red_r[6], "lru_param": red_r[7],
        "q_norm_g": red_m[0, :Q_LORA], "kv_norm_g": red_m[0, Q_LORA:Q_LORA + KV_LORA], "norm2_g": red_2[0],
        "ffn_conv_w": red_ffn[0:3], "ffn_conv_b": red_ffn[3], "final_g": red_f[0],
    }
    dmod = jnp.stack([red_1[2], red_1[1], red_2[3], red_2[2], red_2[1], red_f[1]], axis=0)
    return red_f[2, 0], grad_x, gs, dmod


MESH_IDS = pl.DeviceIdType.MESH
HBM_SPEC = pl.BlockSpec(memory_space=pltpu.HBM)


def _my_slot():
    return 4 * lax.axis_index("x") + 2 * lax.axis_index("y") + lax.axis_index("c")


def _all_gather(arrs, name):
    n = len(arrs)

    def body(*refs):
        ins, outs = refs[:n], refs[n:2 * n]
        send_sems, recv_sems, local_sems = refs[2 * n:]
        x, y, c = lax.axis_index("x"), lax.axis_index("y"), lax.axis_index("c")
        me, sibling = (x, y, c), (x, y, 1 - c)
        chips = [(1 - x, y), (x, 1 - y), (1 - x, 1 - y)]

        def slot(dev):
            return 4 * dev[0] + 2 * dev[1] + dev[2]

        def copy(a, k, block, to, src=None):
            dst = outs[a].at[slot(block)]
            return pltpu.make_async_remote_copy(
                src_ref=dst if src is None else src, dst_ref=dst, send_sem=send_sems.at[a, k], recv_sem=recv_sems.at[a, k],
                device_id=to, device_id_type=MESH_IDS)

        mine = [pltpu.make_async_copy(ins[a], outs[a].at[slot(me)], local_sems.at[a]) for a in range(n)]
        for cp in mine:
            cp.start()
        first = []
        for a in range(n):
            first.append(copy(a, 0, me, sibling, src=ins[a]))
            first += [copy(a, 1 + j, me, (*chip, c), src=ins[a]) for j, chip in enumerate(chips)]
        for cp in first:
            cp.start()
        passed = []
        for j, chip in enumerate(chips):
            for a in range(n):
                copy(a, 1 + j, (*chip, c), me).wait_recv()
                fwd = copy(a, 4 + j, (*chip, c), sibling)
                fwd.start()
                passed.append(fwd)
        for a in range(n):
            copy(a, 0, sibling, me).wait_recv()
            for j, chip in enumerate(chips):
                copy(a, 4 + j, (*chip, 1 - c), me).wait_recv()
        for cp in first + passed:
            cp.wait_send()
        for cp in mine:
            cp.wait()

    return pl.pallas_call(
        body, name=name,
        in_specs=[HBM_SPEC] * n, out_specs=[HBM_SPEC] * n,
        out_shape=[jax.ShapeDtypeStruct((N_DEV,) + a.shape, a.dtype) for a in arrs],
        scratch_shapes=[pltpu.SemaphoreType.DMA((n, 7)), pltpu.SemaphoreType.DMA((n, 7)), pltpu.SemaphoreType.DMA((n,))],
    )(*arrs)


SEM_SPEC =pl.BlockSpec(memory_space=pltpu.SEMAPHORE)
DATAFLOW = pltpu.SideEffectType.DATAFLOW_SIDE_EFFECTING
FLIPS = [(dx, dy, dc) for dx in (0, 1) for dy in (0, 1) for dc in (0, 1)][1:]


def _peer(k):
    dx, dy, dc = FLIPS[k]
    peer = (lax.axis_index("x") ^ dx, lax.axis_index("y") ^ dy, lax.axis_index("c") ^ dc)
    return peer, 4 * peer[0] + 2 * peer[1] + peer[2]


def _gather_start(shards, after, name):
    n, nf = len(shards), len(FLIPS)

    def body(*refs):
        srcs, lands = refs[:n], refs[n:2 * n]
        send_sems, recv_sems = refs[2 * n + 1:3 * n + 1], refs[3 * n + 1:4 * n + 1]
        token = refs[-1]
        me = _my_slot()
        for a in range(n):
            for k in range(nf):
                peer, _ = _peer(k)
                pltpu.make_async_remote_copy(
                    src_ref=srcs[a], dst_ref=lands[a].at[me], send_sem=send_sems[a].at[k], recv_sem=recv_sems[a].at[k],
                    device_id=peer, device_id_type=MESH_IDS).start()
        token[...] = jnp.zeros(token.shape, F32)

    land_shapes = [(N_DEV,) + a.shape for a in shards]
    sems = [pltpu.SemaphoreType.DMA((nf,))] * n
    out = pl.pallas_call(
        body, name=name,
        out_shape=(*sems, *sems, *[pltpu.HBM(a.shape, a.dtype) for a in shards],
                   *[pltpu.HBM(shp, a.dtype) for shp, a in zip(land_shapes, shards)],
                   jax.ShapeDtypeStruct((SUBLANES, LANES), F32)),
        in_specs=[HBM_SPEC] * (2 * n) + [pl.BlockSpec(memory_space=pl.ANY)],
        out_specs=(*[SEM_SPEC] * (2 * n), *[HBM_SPEC] * (2 * n), pl.BlockSpec(memory_space=pltpu.VMEM)),
        input_output_aliases={i: 2 * n + i for i in range(2 * n)},
        compiler_params=pltpu.CompilerParams(has_side_effects=DATAFLOW),
    )(*[pltpu.with_memory_space_constraint(a, pltpu.HBM) for a in shards],
      *[pltpu.with_memory_space_constraint(lax.empty(shp, a.dtype), pltpu.HBM) for shp, a in zip(land_shapes, shards)],
      after)
    return [(out[a], out[n + a], out[2 * n + a], out[3 * n + a]) for a in range(n)], out[-1]


def _gather_wait(flights, after, name):
    n, nf = len(flights), len(FLIPS)

    def body(*refs):
        send_sems, recv_sems = refs[:n], refs[n:2 * n]
        srcs, lands = refs[2 * n:3 * n], refs[3 * n:4 * n]
        for a in range(n):
            for k in range(nf):
                peer, peer_slot = _peer(k)
                cp = pltpu.make_async_remote_copy(
                    src_ref=srcs[a], dst_ref=lands[a].at[peer_slot], send_sem=send_sems[a].at[k],
                    recv_sem=recv_sems[a].at[k], device_id=peer, device_id_type=MESH_IDS)
                cp.wait_send()
                cp.wait_recv()

    srcs, lands = [f[2] for f in flights], [f[3] for f in flights]
    out = pl.pallas_call(
        body, name=name,
        out_shape=(*[pltpu.HBM(a.shape, a.dtype) for a in srcs], *[pltpu.HBM(a.shape, a.dtype) for a in lands]),
        in_specs=[SEM_SPEC] * (2 * n) + [HBM_SPEC] * (2 * n) + [pl.BlockSpec(memory_space=pl.ANY)],
        out_specs=tuple([HBM_SPEC] * (2 * n)),
        input_output_aliases={2 * n + i: i for i in range(2 * n)},
        compiler_params=pltpu.CompilerParams(has_side_effects=DATAFLOW),
    )(*[f[0] for f in flights], *[f[1] for f in flights], *srcs, *lands, after)
    return list(out[n:])


def _scatter_start(chunks, name):
    def body(src_ref, land_ref, send_sems, recv_sems, src_thru, land_thru, token):
        me = _my_slot()
        for k in range(len(FLIPS)):
            peer, peer_slot = _peer(k)
            pltpu.make_async_remote_copy(
                src_ref=src_ref.at[peer_slot], dst_ref=land_ref.at[me], send_sem=send_sems.at[k], recv_sem=recv_sems.at[k],
                device_id=peer, device_id_type=MESH_IDS).start()
        token[...] = jnp.zeros(token.shape, F32)

    n = len(FLIPS)
    hbm = pltpu.HBM(chunks.shape, chunks.dtype)
    return pl.pallas_call(
        body, name=name,
        out_shape=(pltpu.SemaphoreType.DMA((n,)), pltpu.SemaphoreType.DMA((n,)), hbm, hbm,
                   jax.ShapeDtypeStruct((SUBLANES, LANES), F32)),
        in_specs=(HBM_SPEC, HBM_SPEC),
        out_specs=(SEM_SPEC, SEM_SPEC, HBM_SPEC, HBM_SPEC, pl.BlockSpec(memory_space=pltpu.VMEM)),
        input_output_aliases={0: 2, 1: 3},
        compiler_params=pltpu.CompilerParams(has_side_effects=DATAFLOW),
    )(pltpu.with_memory_space_constraint(chunks, pltpu.HBM),
      pltpu.with_memory_space_constraint(lax.empty(chunks.shape, chunks.dtype), pltpu.HBM))


def _scatter_wait(send_sems, recv_sems, src_thru, land_thru, after, name):
    def body(src_ref, land_ref, send_sems, recv_sems, after_ref, src_dead, got_ref):
        for k in range(len(FLIPS)):
            peer, peer_slot = _peer(k)
            cp = pltpu.make_async_remote_copy(
                src_ref=src_ref.at[peer_slot], dst_ref=land_ref.at[peer_slot], send_sem=send_sems.at[k],
                recv_sem=recv_sems.at[k], device_id=peer, device_id_type=MESH_IDS)
            cp.wait_send()
            cp.wait_recv()

    hbm = pltpu.HBM(src_thru.shape, src_thru.dtype)
    return pl.pallas_call(
        body, name=name, out_shape=(hbm, hbm),
        in_specs=(HBM_SPEC, HBM_SPEC, SEM_SPEC, SEM_SPEC, pl.BlockSpec(memory_space=pl.ANY)),
        out_specs=(HBM_SPEC, HBM_SPEC), input_output_aliases={0: 0, 1: 1},
        compiler_params=pltpu.CompilerParams(has_side_effects=DATAFLOW),
    )(src_thru, land_thru, send_sems, recv_sems, after)


def _sum_sources(parts, name):
    k, r, c = parts.shape
    tr = r if k * r * c <= 2 * 1024 * 1024 else _pick(r, (512, 256, 128, 64, 32, 16, 8))

    def body(p_ref, o_ref):
        acc = p_ref[0].astype(F32)
        for s in range(1, k):
            acc = acc + p_ref[s].astype(F32)
        o_ref[...] = acc

    return pl.pallas_call(
        body, name=name, grid=(r // tr,),
        in_specs=[pl.BlockSpec((k, tr, c), lambda i: (0, i, 0))],
        out_specs=pl.BlockSpec((tr, c), lambda i: (i, 0)),
        out_shape=jax.ShapeDtypeStruct((r, c), F32),
        compiler_params=_cparams(("arbitrary",)),
    )(parts)


def _adamw(parts, w, m, v, name, own=None):
    k, r, c = parts.shape
    tr = r if r * c <= 256 * 1024 else _pick(r, (256, 128, 64, 32, 16, 8))

    def body(*refs):
        p_ref, w_ref, m_ref, v_ref = refs[:4]
        g_ref, d_ref, nm_ref, nv_ref = refs[-4:]

        def part(s):
            if own is None:
                return p_ref[s].astype(F32)
            return jnp.where(_my_slot() == s, refs[4][...], p_ref[s]).astype(F32)

        g = part(0)
        for s in range(1, k):
            g = g + part(s)
        m_new = ADAM_B1 * m_ref[...] + (1.0 - ADAM_B1) * g
        v_new = ADAM_B2 * v_ref[...] + (1.0 - ADAM_B2) * jnp.square(g)
        m_hat = m_new / (1.0 - ADAM_B1 ** ADAM_STEP)
        v_hat = v_new / (1.0 - ADAM_B2 ** ADAM_STEP)
        g_ref[...] = g
        d_ref[...] = -ADAM_LR * (m_hat / (jnp.sqrt(v_hat) + ADAM_EPS) + ADAM_WD * w_ref[...])
        nm_ref[...] = m_new
        nv_ref[...] = v_new

    blk = pl.BlockSpec((tr, c), lambda i: (i, 0))
    return pl.pallas_call(
        body, name=name, grid=(r // tr,),
        in_specs=[pl.BlockSpec((k, tr, c), lambda i: (0, i, 0)), blk, blk, blk] + ([] if own is None else [blk]),
        out_specs=[blk] * 4,
        out_shape=[jax.ShapeDtypeStruct((r, c), F32)] * 4,
        compiler_params=_cparams(("arbitrary",)),
    )(parts, w, m, v, *([] if own is None else [own]))


def _silu(v):
    return v * _sigmoid(v)


def _ada_fwd(c_all, w, b):
    def body(c_ref, w_ref, b_ref, o_ref):
        ca = _silu(c_ref[...]).astype(BF16)
        o_ref[...] = jnp.dot(ca, w_ref[...].astype(BF16), preferred_element_type=F32) + b_ref[...]

    return pl.pallas_call(
        body, name="ada_fwd", out_shape=jax.ShapeDtypeStruct((c_all.shape[0], w.shape[1]), F32),
        compiler_params=_cparams(),
    )(c_all, w, b)


def _ada_bwd(c_all, dmod):
    def body(c_ref, d_ref, o_ref):
        ca = _silu(c_ref[...]).astype(BF16).astype(F32)
        dm = d_ref[...].astype(BF16).astype(F32)
        acc = jnp.zeros(o_ref.shape, F32)
        for bi in range(c_all.shape[0]):
            acc = acc + jnp.transpose(ca[bi:bi + 1, :]) * dm[bi:bi + 1, :]
        o_ref[...] = acc

    return pl.pallas_call(
        body, name="ada_bwd", out_shape=jax.ShapeDtypeStruct((c_all.shape[1], dmod.shape[1]), F32),
        compiler_params=_cparams(),
    )(c_all, dmod)


COL_SHARDED = ("w_in", "w_uq", "w_ukv", "w_up")
ROW_SHARDED = ("w_proj_rnn", "w_proj_mla", "w_out", "w_down")
REPLICATED = ("b_ada", "norm1_g", "conv_b", "w_gate_a", "b_gate_a", "w_gate_x", "b_gate_x", "lru_param", "q_norm_g",
              "kv_norm_g", "norm2_g", "ffn_conv_b", "final_g")
WEIGHTS = ("w_ada", "b_ada", "norm1_g", "w_in", "conv_w", "conv_b", "w_gate_a", "b_gate_a", "w_gate_x", "b_gate_x",
           "lru_param", "q_norm_g", "w_uq", "kv_norm_g", "w_ukv", "w_proj_rnn", "w_proj_mla", "w_out", "norm2_g", "w_up",
           "ffn_conv_w", "ffn_conv_b", "w_down", "final_g")
PACK_LANES = 128


def _pack(vecs):
    flat = jnp.concatenate([v.reshape(-1).astype(F32) for v in vecs])
    pad = (-flat.shape[0]) % (PACK_LANES * SUBLANES)
    return jnp.concatenate([flat, jnp.zeros((pad,), F32)]).reshape(-1, PACK_LANES)


def _unpack(packed, shapes):
    flat = packed.reshape(-1)
    out, off = [], 0
    for shp in shapes:
        size = math.prod(shp)
        out.append(flat[off:off + size].reshape(shp))
        off += size
    return out


def kernel(x, c, positions, w_ada, b_ada, norm1_g, w_in, conv_w, conv_b, w_gate_a, b_gate_a, w_gate_x, b_gate_x, lru_param, q_norm_g, w_uq, kv_norm_g, w_ukv, w_proj_rnn, w_proj_mla, w_out, norm2_g, w_up, ffn_conv_w, ffn_conv_b, w_down, final_g, loss_target, m_w_ada, m_b_ada, m_norm1_g, m_w_in, m_conv_w, m_conv_b, m_w_gate_a, m_b_gate_a, m_w_gate_x, m_b_gate_x, m_lru_param, m_q_norm_g, m_w_uq, m_kv_norm_g, m_w_ukv, m_w_proj_rnn, m_w_proj_mla, m_w_out, m_norm2_g, m_w_up, m_ffn_conv_w, m_ffn_conv_b, m_w_down, m_final_g, v_w_ada, v_b_ada, v_norm1_g, v_w_in, v_conv_w, v_conv_b, v_w_gate_a, v_b_gate_a, v_w_gate_x, v_b_gate_x, v_lru_param, v_q_norm_g, v_w_uq, v_kv_norm_g, v_w_ukv, v_w_proj_rnn, v_w_proj_mla, v_w_out, v_norm2_g, v_w_up, v_ffn_conv_w, v_ffn_conv_b, v_w_down, v_final_g):
    args = dict(locals())
    w = {n: args[n] for n in WEIGHTS}
    m = {n: args["m_" + n] for n in WEIGHTS}
    v = {n: args["v_" + n] for n in WEIGHTS}
    s, d = x.shape[1], x.shape[2]
    me = _my_slot()
    def two_d(a):
        assert a.ndim == 3 and a.shape[0] == 1, a.shape
        return a[0]

    big = COL_SHARDED + ROW_SHARDED
    shard = {n: two_d(w[n]).astype(BF16) for n in big}

    def whole(n, g):
        k, r, cc = g.shape
        return jnp.transpose(g, (1, 0, 2)).reshape(r, k * cc) if n in COL_SHARDED else g.reshape(k * r, cc)

    first = _all_gather([shard["w_in"], c, two_d(conv_w), two_d(ffn_conv_w)], "gather_first")
    c_all = first[1].reshape(N_DEV, d)
    conv_w_all = jnp.transpose(first[2], (1, 0, 2)).reshape(conv_w.shape[1], -1)
    ffn_conv_w_all = jnp.transpose(first[3], (1, 0, 2)).reshape(ffn_conv_w.shape[1], -1)

    ada_cols = w_ada.shape[2]
    b_cols = lax.dynamic_slice(b_ada, (0, me * ada_cols), (1, ada_cols))
    mod_cols = _ada_fwd(c_all, w_ada[0], b_cols)
    mod_all, = _all_gather([mod_cols], "gather_mod")

    later = ("w_uq", "w_ukv", "w_proj_rnn", "w_proj_mla", "w_out", "w_up", "w_down")
    flights, started = _gather_start([shard[n] for n in later], mod_all, "gather_start")
    flight = dict(zip(later, flights))

    def fetch(names, after):
        lands = _gather_wait([flight[n] for n in names], after, "gather_wait_" + names[0])
        return {n: whole(n, lax.dynamic_update_index_in_dim(g, shard[n], me, 0)) for n, g in zip(names, lands)}

    mod = lax.dynamic_index_in_dim(mod_all, me, axis=1, keepdims=False).reshape(6, d) + started[0, 0]

    sm = {n: w[n][0] for n in REPLICATED if n not in ("b_ada", "final_g")}
    sm["final_g"] = final_g
    sm["conv_w"] = conv_w_all
    sm["ffn_conv_w"] = ffn_conv_w_all
    in_flight = {}

    def emit(n, g):
        if n in COL_SHARDED:
            r, cc = g.shape
            chunks = jnp.transpose(g.reshape(r, N_DEV, cc // N_DEV), (1, 0, 2))
        else:
            chunks = g.reshape(N_DEV, g.shape[0] // N_DEV, g.shape[1])
        *in_flight[n], token = _scatter_start(chunks, "scatter_start_" + n)
        return token[0, 0]

    sq, grad_x, gs, dmod = _local_step(x[0], mod, positions[0], loss_target[0], whole("w_in", first[0]), fetch, sm, emit)

    small_names = [n for n in REPLICATED if n != "b_ada"] + ["conv_w", "ffn_conv_w"]
    small_shapes = [gs[n].shape for n in small_names] + [(6 * d,), (1,)]
    partial = _pack([gs[n] for n in small_names] + [dmod, sq.reshape(1)])
    partial_all, = _all_gather([partial], "gather_small")
    summed = _unpack(_sum_sources(partial_all, "sum_small"), small_shapes)
    g_small = dict(zip(small_names, summed[:len(small_names)]))
    g_small["b_ada"] = summed[len(small_names)]
    loss = 0.5 * summed[-1][0] / d
    n_before = sum(math.prod(t) for t in small_shapes[:len(small_names)])
    dmod_all = partial_all.reshape(N_DEV, -1)[:, n_before:n_before + 6 * d]
    dmod_cols = lax.dynamic_slice(dmod_all, (0, me * ada_cols), (N_DEV, ada_cols))

    grads, deltas, new_m, new_v = {}, {}, {}, {}

    def update(n, parts, own=None):
        shp = w[n].shape
        g, dl, nm, nv = _adamw(parts, two_d(w[n]), two_d(m[n]), two_d(v[n]), "adamw_" + n, own)
        grads[n], deltas[n], new_m[n], new_v[n] = g.reshape(shp), dl.reshape(shp), nm.reshape(shp), nv.reshape(shp)

    update("w_ada", _ada_bwd(c_all, dmod_cols)[None])

    for n in big:
        chunks, landed = _scatter_wait(*in_flight[n], grad_x, "scatter_wait_" + n)
        update(n, landed, lax.dynamic_index_in_dim(chunks, me, axis=0, keepdims=False))

    for n in ("conv_w", "ffn_conv_w"):
        cols = w[n].shape[2]
        update(n, lax.dynamic_slice(g_small[n], (0, me * cols), (g_small[n].shape[0], cols))[None])

    rep_shapes = [w[n].shape for n in REPLICATED]
    g_rep, d_rep, m_rep, v_rep = _adamw(
        _pack([g_small[n] for n in REPLICATED])[None], _pack([w[n] for n in REPLICATED]), _pack([m[n] for n in REPLICATED]),
        _pack([v[n] for n in REPLICATED]), "adamw_replicated")
    for dst, packed in ((grads, g_rep), (deltas, d_rep), (new_m, m_rep), (new_v, v_rep)):
        dst.update(zip(REPLICATED, _unpack(packed, rep_shapes)))

    return (loss, grad_x[None], *[grads[n] for n in WEIGHTS], *[deltas[n] for n in WEIGHTS],
            *[new_m[n] for n in WEIGHTS], *[new_v[n] for n in WEIGHTS])
```

```python
import functools
import math

import jax
import jax.numpy as jnp
from jax import lax
from jax.experimental import pallas as pl
from jax.experimental.pallas import tpu as pltpu

F32 = jnp.float32
BF16 = jnp.bfloat16

N_DEV = 8
LANES = 128
SUBLANES = 8
VMEM_LIMIT = 56 * 1024 * 1024

D_RNN = 1280
Q_LORA = 384
KV_LORA = 256
QK_NOPE = 64
QK_ROPE = 32
V_HEAD = 64
N_HEADS = 16
D_FF = 2816
ROPE_THETA = 10000.0
LRU_C = 8.0
EPS = 1e-6
MLA_W = 768
ATT_SCALE = 1.0 / math.sqrt(QK_NOPE + QK_ROPE)

ADAM_LR, ADAM_B1, ADAM_B2, ADAM_EPS, ADAM_WD, ADAM_STEP = 0.001, 0.9, 0.999, 1e-08, 0.01, 10


def _cparams(sem=None):
    return pltpu.CompilerParams(dimension_semantics=sem, vmem_limit_bytes=VMEM_LIMIT)


def _pick(n, prefs):
    for p in prefs:
        if n % p == 0:
            return p
    return n


def _sigmoid(v):
    return 0.5 * jnp.tanh(0.5 * v) + 0.5


def _lane(shape):
    return lax.broadcasted_iota(jnp.int32, shape, len(shape) - 1)


def _row(shape):
    return lax.broadcasted_iota(jnp.int32, shape, len(shape) - 2)


MM_BLOCK_BYTES = 36 * 1024 * 1024


def _divisors(n):
    return [t for t in range(n, 0, -LANES) if n % t == 0] if n % LANES == 0 else [n]


def _mm_tiles(m, n, k, a_bytes, b_bytes, o_bytes):
    tm = _pick(m, (512, 384, 256, 128))
    for tk in _divisors(k):
        for tn in _divisors(n):
            need = 2 * (tm * tk * a_bytes + tk * tn * b_bytes + tm * tn * o_bytes) + (tm * tn * 4 if tk < k else 0)
            if tn <= 2048 and need <= MM_BLOCK_BYTES:
                return tm, tn, tk
    raise ValueError((m, n, k))


def _mm(a, b, *, ta=False, tb=False, out_dtype=F32, name):
    (k_a, m) = a.shape if ta else a.shape[::-1]
    (n, k_b) = b.shape if tb else b.shape[::-1]
    assert k_a == k_b, (a.shape, b.shape, ta, tb)
    k = k_a
    tm, tn, tk = _mm_tiles(m, n, k, a.dtype.itemsize, b.dtype.itemsize, jnp.dtype(out_dtype).itemsize)
    nk = k // tk
    dims = (((0 if ta else 1,), (1 if tb else 0,)), ((), ()))

    def body(a_ref, b_ref, o_ref, *acc):
        part = lax.dot_general(a_ref[...].astype(BF16), b_ref[...].astype(BF16), dims, preferred_element_type=F32)
        if nk == 1:
            o_ref[...] = part.astype(out_dtype)
            return
        acc_ref, = acc
        kk = pl.program_id(2)

        @pl.when(kk == 0)
        def _():
            acc_ref[...] = part

        @pl.when(kk > 0)
        def _():
            acc_ref[...] += part

        @pl.when(kk == nk - 1)
        def _():
            o_ref[...] = acc_ref[...].astype(out_dtype)

    a_spec = pl.BlockSpec((tk, tm), lambda i, j, kk: (kk, i)) if ta else pl.BlockSpec((tm, tk), lambda i, j, kk: (i, kk))
    b_spec = pl.BlockSpec((tn, tk), lambda i, j, kk: (j, kk)) if tb else pl.BlockSpec((tk, tn), lambda i, j, kk: (kk, j))
    return pl.pallas_call(
        body, name=name,
        grid=(m // tm, n // tn, nk),
        in_specs=[a_spec, b_spec],
        out_specs=pl.BlockSpec((tm, tn), lambda i, j, kk: (i, j)),
        out_shape=jax.ShapeDtypeStruct((m, n), out_dtype),
        scratch_shapes=[] if nk == 1 else [pltpu.VMEM((tm, tn), F32)],
        compiler_params=_cparams(("arbitrary", "arbitrary", "arbitrary")),
    )(a, b)


def _rowwise(fn, row_ins, par_ins, out_defs, red_defs, *, name, tr=256):
    s = row_ins[0].shape[0]
    tr = min(tr, s)
    nr, npar, no = len(row_ins), len(par_ins), len(out_defs)

    def body(*refs):
        rin, pin = refs[:nr], refs[nr:nr + npar]
        outs, reds = refs[nr + npar:nr + npar + no], refs[nr + npar + no:]
        i = pl.program_id(0)

        @pl.when(i == 0)
        def _():
            for r in reds:
                r[...] = jnp.zeros_like(r)

        fn(i, rin, pin, outs, reds)

    in_specs = [pl.BlockSpec((tr, a.shape[1]), lambda i: (i, 0)) for a in row_ins]
    in_specs += [pl.BlockSpec(a.shape, lambda i, nd=a.ndim: (0,) * nd) for a in par_ins]
    out_specs = [pl.BlockSpec((tr, c), lambda i: (i, 0)) for c, _ in out_defs]
    out_specs += [pl.BlockSpec(shp, lambda i: (0, 0)) for shp in red_defs]
    out_shape = [jax.ShapeDtypeStruct((s, c), dt) for c, dt in out_defs]
    out_shape += [jax.ShapeDtypeStruct(shp, F32) for shp in red_defs]
    return pl.pallas_call(
        body, name=name, grid=(s // tr,), in_specs=in_specs, out_specs=out_specs, out_shape=out_shape,
        compiler_params=_cparams(("arbitrary",)),
    )(*row_ins, *par_ins)


def _rms(v):
    return lax.rsqrt(jnp.mean(v * v, axis=-1, keepdims=True) + EPS)


def _colsum(v):
    return jnp.sum(v, axis=0, keepdims=True)


def _rms_bwd(dn, n, rstd):
    return rstd * (dn - n * jnp.mean(dn * n, axis=-1, keepdims=True))


def _norm_mod_fwd(x, gmod, name):
    def fn(i, rin, pin, outs, reds):
        xv = rin[0][...]
        p = pin[0][...]
        n = xv * _rms(xv)
        outs[0][...] = ((n * p[0:1]) * (1.0 + p[1:2]) + p[2:3]).astype(BF16)

    return _rowwise(fn, [x], [gmod], [(x.shape[1], BF16)], [], name=name)[0]


def _rope(v, rot_c, rot_s):
    half = QK_ROPE // 2
    swapped = jnp.where(_lane(v.shape) < QK_NOPE + half, pltpu.roll(v, LANES - half, 1), pltpu.roll(v, half, 1))
    return v * rot_c + swapped * rot_s


def _rope_t(dv, rot_c, rot_s):
    half = QK_ROPE // 2
    ds = dv * rot_s
    lane = _lane(dv.shape)
    swapped = jnp.where(lane < QK_NOPE + half, pltpu.roll(ds, LANES - half, 1), pltpu.roll(ds, half, 1))
    in_rope = (lane >= QK_NOPE) & (lane < QK_NOPE + QK_ROPE)
    return dv * rot_c + jnp.where(in_rope, swapped, 0.0)


def _mla_prep_fwd(proj_mla, rot_c, rot_s, ng):
    o1, o2 = Q_LORA, Q_LORA + KV_LORA

    def fn(i, rin, pin, outs, reds):
        g = pin[0][...]
        ql = rin[0][:, 0:o1]
        kl = rin[0][:, o1:o2]
        outs[0][...] = (ql * _rms(ql) * g[0:1, 0:o1]).astype(BF16)
        outs[1][...] = (kl * _rms(kl) * g[0:1, o1:o2]).astype(BF16)
        kr = pltpu.roll(rin[0][:, o2:o2 + LANES], QK_NOPE, 1)
        outs[2][...] = _rope(kr, rin[1][...], rin[2][...]).astype(BF16)

    return _rowwise(fn, [proj_mla, rot_c, rot_s], [ng], [(Q_LORA, BF16), (KV_LORA, BF16), (LANES, BF16)], [],
                    name="mla_prep_fwd")


def _mla_prep_bwd(proj_mla, dqn, dkvn, dkr, rot_c, rot_s, ng):
    o1, o2 = Q_LORA, Q_LORA + KV_LORA

    def fn(i, rin, pin, outs, reds):
        g = pin[0][...]
        ql = rin[0][:, 0:o1]
        kl = rin[0][:, o1:o2]
        rq, rk = _rms(ql), _rms(kl)
        nq, nk = ql * rq, kl * rk
        dq, dk = rin[1][...], rin[2][...]
        outs[0][:, 0:o1] = _rms_bwd(dq * g[0:1, 0:o1], nq, rq).astype(BF16)
        outs[0][:, o1:o2] = _rms_bwd(dk * g[0:1, o1:o2], nk, rk).astype(BF16)
        dkr_pre = pltpu.roll(_rope_t(rin[3][...], rin[4][...], rin[5][...]), LANES - QK_NOPE, 1)
        outs[0][:, o2:] = jnp.where(_lane(dkr_pre.shape) < QK_ROPE, dkr_pre, 0.0).astype(BF16)
        reds[0][0:1, 0:o1] += _colsum(dq * nq)
        reds[0][0:1, o1:o2] += _colsum(dk * nk)

    return _rowwise(fn, [proj_mla, dqn, dkvn, dkr, rot_c, rot_s], [ng], [(MLA_W, BF16)], [(SUBLANES, MLA_W)],
                    name="mla_prep_bwd")


def _rope_heads(q, rot_c, rot_s, transpose, name):
    def fn(i, rin, pin, outs, reds):
        c, sn = rin[1][...] * Q_PRESCALE, rin[2][...] * Q_PRESCALE
        for h in range(N_HEADS):
            sl = slice(h * LANES, (h + 1) * LANES)
            v = rin[0][:, sl]
            outs[0][:, sl] = (_rope_t(v, c, sn) if transpose else _rope(v, c, sn)).astype(BF16)

    return _rowwise(fn, [q, rot_c, rot_s], [], [(q.shape[1], BF16)], [], name=name)[0]


def _merge_fwd(pr, pm, proj_g):
    d = pr.shape[1]

    def fn(i, rin, pin, outs, reds):
        outs[0][...] = (_sigmoid(rin[2][:, 0:d]) * rin[0][...] + _sigmoid(rin[2][:, d:]) * rin[1][...]).astype(BF16)

    return _rowwise(fn, [pr, pm, proj_g], [], [(d, BF16)], [], name="merge_fwd")[0]


def _merge_bwd(dmerged, pr, pm, proj_g):
    d = pr.shape[1]

    def fn(i, rin, pin, outs, reds):
        dm = rin[0][...]
        sr, sm = _sigmoid(rin[3][:, 0:d]), _sigmoid(rin[3][:, d:])
        outs[0][...] = (dm * sr).astype(BF16)
        outs[1][...] = (dm * sm).astype(BF16)
        outs[2][:, 0:d] = (dm * rin[1][...] * sr * (1.0 - sr)).astype(BF16)
        outs[2][:, d:] = (dm * rin[2][...] * sm * (1.0 - sm)).astype(BF16)

    return _rowwise(fn, [dmerged, pr, pm, proj_g], [], [(d, BF16), (d, BF16), (2 * d, BF16)], [], name="merge_bwd")


def _resid_norm_fwd(x, o, gmod):
    d = x.shape[1]

    def fn(i, rin, pin, outs, reds):
        p = pin[0][...]
        x1 = rin[0][...] + p[3:4] * rin[1][...]
        outs[0][...] = x1
        outs[1][...] = ((x1 * _rms(x1) * p[0:1]) * (1.0 + p[1:2]) + p[2:3]).astype(BF16)

    return _rowwise(fn, [x, o], [gmod], [(d, F32), (d, BF16)], [], name="resid_norm_fwd")


def _final_fwd_bwd(x1, dn, target, par):
    d = x1.shape[1]

    def fn(i, rin, pin, outs, reds):
        p = pin[0][...]
        dnv = rin[1][...]
        x2 = rin[0][...] + p[0:1] * dnv
        rstd = _rms(x2)
        n3 = x2 * rstd
        err = n3 * p[1:2] - rin[2][...]
        dy = err * (1.0 / d)
        dx2 = _rms_bwd(dy * p[1:2], n3, rstd)
        outs[0][...] = dx2
        outs[1][...] = (dx2 * p[0:1]).astype(BF16)
        reds[0][0:1, :] += _colsum(dy * n3)
        reds[0][1:2, :] += _colsum(dx2 * dnv)
        reds[0][2:3, :] += jnp.zeros((1, d), F32) + jnp.sum(err * err)

    return _rowwise(fn, [x1, dn, target], [par], [(d, F32), (d, BF16)], [(SUBLANES, d)], name="final_fwd_bwd")


def _norm2_bwd(x1, dh2, dx2, o, gmod):
    d = x1.shape[1]

    def fn(i, rin, pin, outs, reds):
        p = pin[0][...]
        x1v, dh = rin[0][...], rin[1][...]
        rstd = _rms(x1v)
        n2 = x1v * rstd
        dx1 = rin[2][...] + _rms_bwd(dh * (p[0:1] * (1.0 + p[1:2])), n2, rstd)
        outs[0][...] = dx1
        outs[1][...] = (dx1 * p[3:4]).astype(BF16)
        reds[0][0:1, :] += _colsum(dh * n2 * (1.0 + p[1:2]))
        reds[0][1:2, :] += _colsum(dh * n2 * p[0:1])
        reds[0][2:3, :] += _colsum(dh)
        reds[0][3:4, :] += _colsum(dx1 * rin[3][...])

    return _rowwise(fn, [x1, dh2, dx2, o], [gmod], [(d, F32), (d, BF16)], [(SUBLANES, d)], name="norm2_bwd")


def _norm1_bwd(x, dh_a, dh_b, dh_c, dx1, gmod):
    d = x.shape[1]

    def fn(i, rin, pin, outs, reds):
        p = pin[0][...]
        xv = rin[0][...]
        dh = rin[1][...] + rin[2][...] + rin[3][...]
        rstd = _rms(xv)
        n1 = xv * rstd
        outs[0][...] = rin[4][...] + _rms_bwd(dh * (p[0:1] * (1.0 + p[1:2])), n1, rstd)
        reds[0][0:1, :] += _colsum(dh * n1 * (1.0 + p[1:2]))
        reds[0][1:2, :] += _colsum(dh * n1 * p[0:1])
        reds[0][2:3, :] += _colsum(dh)

    return _rowwise(fn, [x, dh_a, dh_b, dh_c, dx1], [gmod], [(d, F32)], [(SUBLANES, d)], name="norm1_bwd")


RNN_CHUNK = 512


def _shift_down(ref, base, n, j):
    v = ref[pl.ds(base, n + SUBLANES), :]
    return v[SUBLANES:] if j == 0 else pltpu.roll(v, j, 0)[SUBLANES:]


def _shift_up(ref, base, n, j, top_pad):
    v = ref[pl.ds(base + top_pad, n + SUBLANES), :]
    return v[:n] if j == 0 else pltpu.roll(v, n + SUBLANES - j, 0)[:n]


SCAN_GROUP = 128


def _scan_sizes(s):
    sizes = [s]
    while sizes[-1] > SUBLANES:
        assert sizes[-1] % SUBLANES == 0, s
        sizes.append(sizes[-1] // SUBLANES)
    return sizes


def _scan_scratch(s):
    return [pltpu.VMEM((n + 2 * SUBLANES, LANES), F32) for n in _scan_sizes(s)[1:] for _ in range(2)]


def _linear_scan(a_ref, b_ref, out_ref, a_off, s, reverse, levels):
    sizes = _scan_sizes(s)
    lv = [(a_ref, b_ref, a_off, 0)] + [(levels[2 * i], levels[2 * i + 1], 0, SUBLANES) for i in range(len(sizes) - 1)]
    zero8 = jnp.zeros((SUBLANES, LANES), F32)
    for (ar, br, _, _), n in zip(lv[1:], sizes[1:]):
        br[0:SUBLANES, :] = zero8
        br[pl.ds(n + SUBLANES, SUBLANES), :] = zero8
    order = list(range(SUBLANES - 1, -1, -1)) if reverse else list(range(SUBLANES))

    for lvl in range(len(sizes) - 1):
        ar, br, aoff, off = lv[lvl]
        m = sizes[lvl + 1]
        g = min(m, SCAN_GROUP)
        for t0 in range(0, m, g):
            acc_a = acc_b = None
            for r in order:
                sa = pl.ds(off + SUBLANES * t0 + r + aoff, g, stride=SUBLANES)
                sb = pl.ds(off + SUBLANES * t0 + r, g, stride=SUBLANES)
                a, b = ar[sa, :], br[sb, :]
                if acc_a is None:
                    acc_a, acc_b = a, b
                else:
                    acc_b = a * acc_b + b
                    acc_a = a * acc_a
            lv[lvl + 1][0][pl.ds(SUBLANES + t0, g), :] = acc_a
            lv[lvl + 1][1][pl.ds(SUBLANES + t0, g), :] = acc_b

    ar, br, _, off = lv[-1]
    n = sizes[-1]
    a, b = ar[pl.ds(off, n), :], br[pl.ds(off, n), :]
    h, rows = jnp.zeros((1, LANES), F32), [None] * n
    for j in (range(n - 1, -1, -1) if reverse else range(n)):
        h = a[j:j + 1, :] * h + b[j:j + 1, :]
        rows[j] = h
    br[pl.ds(off, n), :] = jnp.concatenate(rows, axis=0)

    for lvl in range(len(sizes) - 2, -1, -1):
        ar, br, aoff, off = lv[lvl]
        m = sizes[lvl + 1]
        up = lv[lvl + 1][1]
        dst = out_ref if lvl == 0 else br
        g = min(m, SCAN_GROUP)
        for t0 in range(0, m, g):
            h = _shift_up(up, t0, g, 1, SUBLANES) if reverse else _shift_down(up, t0, g, 1)
            for r in order:
                sa = pl.ds(off + SUBLANES * t0 + r + aoff, g, stride=SUBLANES)
                sb = pl.ds(off + SUBLANES * t0 + r, g, stride=SUBLANES)
                h = ar[sa, :] * h + br[sb, :]
                dst[sb, :] = h


def _one_minus_exp(z):
    series = -z * (1.0 + z * (0.5 + z * (1.0 / 6.0 + z * (1.0 / 24.0 + z * (1.0 / 120.0 + z * (1.0 / 720.0))))))
    return jnp.where(z > -0.1, series, 1.0 - jnp.exp(z))


def _softplus(v):
    return jnp.maximum(v, 0.0) + jnp.log(1.0 + jnp.exp(-jnp.abs(v)))


def _rnn_gates(xc, w, wa, wx, sp):
    xb = xc.astype(BF16)
    ra = _sigmoid(jnp.dot(xb, wa, preferred_element_type=F32) + w[5:6])
    ix = _sigmoid(jnp.dot(xb, wx, preferred_element_type=F32) + w[6:7])
    la = (-LRU_C) * ra * sp
    a = jnp.exp(la)
    mult = jnp.sqrt(_one_minus_exp(2.0 * la))
    return ra, ix, a, mult


def _rnn_fwd(x_rnn, keep, rp, wa_bd, wx_bd):
    s, r = x_rnn.shape
    ts = min(RNN_CHUNK, s)

    def body(x_ref, keep_ref, rp_ref, wa_ref, wx_ref, xc_ref, ra_ref, ix_ref, hs_ref, xpad, a_s, b_s, *levels):
        xpad[0:SUBLANES, :] = jnp.zeros((SUBLANES, LANES), F32)
        xpad[SUBLANES:, :] = x_ref[...]
        w = rp_ref[...]
        sp = _softplus(-w[7:8])
        wa, wx = wa_ref[0], wx_ref[0]

        def chunk(c, carry):
            base = pl.multiple_of(c * ts, ts)
            xc = w[4:5] + w[3:4] * _shift_down(xpad, base, ts, 0)
            for j in range(1, 4):
                xc = xc + w[3 - j:4 - j] * _shift_down(xpad, base, ts, j)
            ra, ix, a, mult = _rnn_gates(xc, w, wa, wx, sp)
            kp = keep_ref[pl.ds(base, ts), :]
            xc_ref[pl.ds(base, ts), :] = xc
            ra_ref[pl.ds(base, ts), :] = ra
            ix_ref[pl.ds(base, ts), :] = ix
            a_s[pl.ds(base, ts), :] = a * kp
            b_s[pl.ds(base, ts), :] = jnp.where(kp > 0.0, mult, 1.0) * (ix * xc)
            return carry

        lax.fori_loop(0, s // ts, chunk, 0)

        _linear_scan(a_s, b_s, hs_ref, 0, s, False, levels)

    col = pl.BlockSpec((s, LANES), lambda g: (0, g))
    return pl.pallas_call(
        body, name="rnn_fwd", grid=(r // LANES,),
        in_specs=[col, pl.BlockSpec((s, 1), lambda g: (0, 0)), pl.BlockSpec((SUBLANES, LANES), lambda g: (0, g)),
                  pl.BlockSpec((1, LANES, LANES), lambda g: (g, 0, 0)), pl.BlockSpec((1, LANES, LANES), lambda g: (g, 0, 0))],
        out_specs=[col] * 4,
        out_shape=[jax.ShapeDtypeStruct((s, r), F32)] * 4,
        scratch_shapes=[pltpu.VMEM((s + SUBLANES, LANES), F32), pltpu.VMEM((s, LANES), F32), pltpu.VMEM((s, LANES), F32),
                        *_scan_scratch(s)],
        compiler_params=_cparams(("arbitrary",)),
    )(x_rnn, keep, rp, wa_bd, wx_bd)


def _rnn_bwd(x_rnn, xc, ra, ix, hs, dy, keep, rp, wa_bd, wx_bd):
    s, r = x_rnn.shape
    ts = min(RNN_CHUNK, s)

    def body(x_ref, xc_ref, ra_ref, ix_ref, hs_ref, dy_ref, keep_ref, rp_ref, wa_ref, wx_ref,
             dx_ref, dwa_ref, dwx_ref, red_ref, xpad, hpad, a_s, dh_s, dxc_s, *levels):
        zero8 = jnp.zeros((SUBLANES, LANES), F32)
        xpad[0:SUBLANES, :] = zero8
        xpad[SUBLANES:, :] = x_ref[...]
        hpad[0:SUBLANES, :] = zero8
        hpad[SUBLANES:, :] = hs_ref[...]
        a_s[s:, :] = zero8
        dxc_s[s:, :] = zero8
        w = rp_ref[...]
        sp = _softplus(-w[7:8])
        wa, wx = wa_ref[0], wx_ref[0]

        def decay(c, carry):
            base = pl.multiple_of(c * ts, ts)
            a = jnp.exp((-LRU_C) * ra_ref[pl.ds(base, ts), :] * sp)
            a_s[pl.ds(base, ts), :] = a * keep_ref[pl.ds(base, ts), :]
            return carry

        lax.fori_loop(0, s // ts, decay, 0)

        _linear_scan(a_s, dy_ref, dh_s, 1, s, True, levels)

        def gates(c, carry):
            dwa, dwx, d_ba, d_bx, d_sp, d_cb = carry
            base = pl.multiple_of(c * ts, ts)
            xcv = xc_ref[pl.ds(base, ts), :]
            rav = ra_ref[pl.ds(base, ts), :]
            ixv = ix_ref[pl.ds(base, ts), :]
            kp = keep_ref[pl.ds(base, ts), :]
            dh = dh_s[pl.ds(base, ts), :]
            h_prev = _shift_down(hpad, base, ts, 1)
            la = (-LRU_C) * rav * sp
            a = jnp.exp(la)
            mult = jnp.sqrt(_one_minus_exp(2.0 * la))
            mult_eff = jnp.where(kp > 0.0, mult, 1.0)
            d_a = dh * h_prev * kp
            d_mult = dh * (ixv * xcv) * kp
            d_ix = dh * mult_eff * xcv
            d_xc = dh * mult_eff * ixv
            d_la = d_a * a - d_mult * (a * a) / mult
            d_pa = d_la * ((-LRU_C) * sp) * rav * (1.0 - rav)
            d_px = d_ix * ixv * (1.0 - ixv)
            xb = xcv.astype(BF16)
            pab, pxb = d_pa.astype(BF16), d_px.astype(BF16)
            tn = (((0,), (0,)), ((), ()))
            nt_ = (((1,), (1,)), ((), ()))
            dwa = dwa + lax.dot_general(xb, pab, tn, preferred_element_type=F32)
            dwx = dwx + lax.dot_general(xb, pxb, tn, preferred_element_type=F32)
            d_xc = d_xc + lax.dot_general(pab, wa, nt_, preferred_element_type=F32)
            d_xc = d_xc + lax.dot_general(pxb, wx, nt_, preferred_element_type=F32)
            dxc_s[pl.ds(base, ts), :] = d_xc
            return (dwa, dwx, d_ba + _colsum(d_pa), d_bx + _colsum(d_px),
                    d_sp + _colsum(d_la * ((-LRU_C) * rav)), d_cb + _colsum(d_xc))

        z1 = jnp.zeros((1, LANES), F32)
        zw = jnp.zeros((LANES, LANES), F32)
        dwa, dwx, d_ba, d_bx, d_sp, d_cb = lax.fori_loop(0, s // ts, gates, (zw, zw, z1, z1, z1, z1))
        dwa_ref[0] = dwa
        dwx_ref[0] = dwx

        def conv(c, carry):
            base = pl.multiple_of(c * ts, ts)
            d_here = dxc_s[pl.ds(base, ts), :]
            dx = w[3:4] * d_here
            for j in range(1, 4):
                dx = dx + w[3 - j:4 - j] * _shift_up(dxc_s, base, ts, j, 0)
            dx_ref[pl.ds(base, ts), :] = dx.astype(BF16)
            return tuple(carry[k] + _colsum(d_here * _shift_down(xpad, base, ts, 3 - k)) for k in range(4))

        d_w = lax.fori_loop(0, s // ts, conv, (z1, z1, z1, z1))
        d_lru = d_sp * (-_sigmoid(-w[7:8]))
        red_ref[...] = jnp.concatenate(list(d_w) + [d_cb, d_ba, d_bx, d_lru], axis=0)

    col = pl.BlockSpec((s, LANES), lambda g: (0, g))
    par = pl.BlockSpec((SUBLANES, LANES), lambda g: (0, g))
    wsp = pl.BlockSpec((1, LANES, LANES), lambda g: (g, 0, 0))
    return pl.pallas_call(
        body, name="rnn_bwd", grid=(r // LANES,),
        in_specs=[col] * 6 + [pl.BlockSpec((s, 1), lambda g: (0, 0)), par, wsp, wsp],
        out_specs=[col, wsp, wsp, par],
        out_shape=[jax.ShapeDtypeStruct((s, r), BF16), jax.ShapeDtypeStruct((r // LANES, LANES, LANES), F32),
                   jax.ShapeDtypeStruct((r // LANES, LANES, LANES), F32), jax.ShapeDtypeStruct((SUBLANES, r), F32)],
        scratch_shapes=[pltpu.VMEM((s + SUBLANES, LANES), F32), pltpu.VMEM((s + SUBLANES, LANES), F32),
                        pltpu.VMEM((s + SUBLANES, LANES), F32), pltpu.VMEM((s, LANES), F32),
                        pltpu.VMEM((s + SUBLANES, LANES), F32), *_scan_scratch(s)],
        compiler_params=_cparams(("arbitrary",)),
    )(x_rnn, xc, ra, ix, hs, dy, keep, rp, wa_bd, wx_bd)


ATT_BLOCK = 512


LOG2E = 1.4426950408889634
LN2 = 0.6931471805599453
Q_PRESCALE = ATT_SCALE * LOG2E


def _att_scores(q, kvt, krt, diagonal):
    kt_eff = jnp.where(_row(kvt.shape) < QK_NOPE, kvt, krt)
    sc = jnp.dot(q, kt_eff, preferred_element_type=F32)
    if diagonal:
        sc = jnp.where(lax.broadcasted_iota(jnp.int32, sc.shape, 1) <= lax.broadcasted_iota(jnp.int32, sc.shape, 0), sc, -jnp.inf)
    return sc


def _att_fwd(q, kv, kvt, krt):
    s = q.shape[0]
    t = min(ATT_BLOCK, s)
    nb = s // t

    def body(q_ref, kv_ref, kvt_ref, krt_ref, y_ref, lse_ref, m_s, acc_s):
        i, j = pl.program_id(1), pl.program_id(2)

        @pl.when(j == 0)
        def _():
            m_s[...] = jnp.full(m_s.shape, -jnp.inf, F32)
            acc_s[...] = jnp.zeros(acc_s.shape, F32)

        def step(diagonal):
            krt_b = krt_ref[...]
            lane = _lane((t, LANES))
            groups = [slice(c * LANES, (c + 1) * LANES) for c in range(t // LANES)]
            heads = [slice(hh * LANES, (hh + 1) * LANES) for hh in range(2)]
            scs = [_att_scores(q_ref[:, sl], kvt_ref[sl, :], krt_b, diagonal) for sl in heads]
            stats = []
            for hh in range(2):
                m_prev = m_s[hh]
                m_blk = scs[hh][:, groups[0]]
                for g in groups[1:]:
                    m_blk = jnp.maximum(m_blk, scs[hh][:, g])
                stats.append((m_prev, jnp.maximum(m_prev, jnp.max(m_blk, axis=-1, keepdims=True))))
            for hh in range(2):
                m_prev, m_new = stats[hh]
                kvb = kv_ref[:, heads[hh]]
                ones_v = jnp.where(lane < QK_NOPE, jnp.ones_like(kvb), kvb)
                p = jnp.concatenate([jnp.exp2(scs[hh][:, g] - m_new).astype(BF16) for g in groups], axis=1)
                acc_s[hh] = jnp.exp2(m_prev - m_new) * acc_s[hh] + jnp.dot(p, ones_v, preferred_element_type=F32)
                m_s[hh] = m_new

        @pl.when(j < i)
        def _():
            step(False)

        @pl.when(j == i)
        def _():
            step(True)
            lane = _lane((t, LANES))
            a0, a1 = acc_s[0], acc_s[1]
            l0, l1 = a0[:, 0:1], a1[:, 0:1]
            y_ref[...] = jnp.where(lane < V_HEAD, pltpu.roll(a0 / l0, V_HEAD, 1), a1 / l1).astype(BF16)
            lse_ref[...] = jnp.where(lane < V_HEAD, m_s[0] + jnp.log(l0) * LOG2E, m_s[1] + jnp.log(l1) * LOG2E)

    return pl.pallas_call(
        body, name="att_fwd", grid=(N_HEADS // 2, nb, nb),
        in_specs=[pl.BlockSpec((t, 2 * LANES), lambda p, i, j: (i, p)),
                  pl.BlockSpec((t, 2 * LANES), lambda p, i, j: (jnp.minimum(j, i), p)),
                  pl.BlockSpec((2 * LANES, t), lambda p, i, j: (p, jnp.minimum(j, i))),
                  pl.BlockSpec((LANES, t), lambda p, i, j: (0, jnp.minimum(j, i)))],
        out_specs=[pl.BlockSpec((t, LANES), lambda p, i, j: (i, p))] * 2,
        out_shape=[jax.ShapeDtypeStruct((s, N_HEADS * V_HEAD), BF16), jax.ShapeDtypeStruct((s, N_HEADS * V_HEAD), F32)],
        scratch_shapes=[pltpu.VMEM((2, t, LANES), F32)] * 2,
        compiler_params=_cparams(("arbitrary", "arbitrary", "arbitrary")),
    )(q, kv, kvt, krt)


def _att_bwd(q, qt, kv, kvt, kr, krt, y, lse, dy, dyt):
    s = q.shape[0]
    t = min(ATT_BLOCK, s)
    nb = s // t

    def body(q_ref, qt_ref, kv_ref, kvt_ref, kr_ref, krt_ref, y_ref, lse_ref, dy_ref, dyt_ref,
             dq_ref, dkvt_ref, dkrt_ref, dkv_s):
        p_, j, i = pl.program_id(0), pl.program_id(1), pl.program_id(2)

        @pl.when((p_ == 0) & (j == 0) & (i == 0))
        def _():
            dkrt_ref[...] = jnp.zeros(dkrt_ref.shape, F32)

        @pl.when((j == 0) & (i == 0))
        def _():
            dq_ref[...] = jnp.zeros(dq_ref.shape, F32)

        @pl.when(i == 0)
        def _():
            dkv_s[...] = jnp.zeros(dkv_s.shape, F32)

        def step(diagonal):
            lane = _lane((t, LANES))
            row = _row((LANES, t))
            krb, krt_b = kr_ref[...], krt_ref[...]
            dyv = dy_ref[...]
            yv = y_ref[...].astype(F32)
            lsev = lse_ref[...]
            dyt_b = dyt_ref[...]
            rows = pl.ds(pl.multiple_of(i * t, t), t)
            cols = pl.ds(pl.multiple_of(j * t, t), t)
            zeros_t = jnp.zeros((V_HEAD, t), BF16)
            ones_w = jnp.ones((LANES, LANES), BF16)
            groups = [slice(c * LANES, (c + 1) * LANES) for c in range(t // LANES)]
            heads = [slice(hh * LANES, (hh + 1) * LANES) for hh in range(2)]
            scs, dps, stats = [], [], []
            for hh, sl in enumerate(heads):
                kvt_b = kvt_ref[sl, :]
                scs.append(_att_scores(q_ref[:, sl], kvt_b, krt_b, diagonal))
                mine = (lane < V_HEAD) if hh == 0 else (lane >= V_HEAD)
                lse_rep = jnp.where(mine, lsev, pltpu.roll(lsev, V_HEAD, 1))
                do_pad = jnp.where(lane >= V_HEAD, pltpu.roll(dyv, V_HEAD, 1) if hh == 0 else dyv, 0.0)
                o_pad = jnp.where(lane >= V_HEAD, pltpu.roll(yv, V_HEAD, 1) if hh == 0 else yv, 0.0)
                do_ln2 = do_pad * LN2
                prod = do_ln2 * o_pad
                head_part = prod.astype(BF16)
                rest_part = (prod - head_part.astype(F32)).astype(BF16)
                delta_rep = (jnp.dot(head_part, ones_w, preferred_element_type=F32)
                             + jnp.dot(rest_part, ones_w, preferred_element_type=F32))
                dps.append(jnp.dot(do_ln2.astype(BF16), kvt_b, preferred_element_type=F32))
                stats.append((lse_rep, delta_rep))
            dkr_acc = jnp.zeros((LANES, t), F32)
            for hh, sl in enumerate(heads):
                lse_rep, delta_rep = stats[hh]
                probs, dss = [], []
                for g in groups:
                    pg = jnp.exp2(scs[hh][:, g] - lse_rep)
                    probs.append(pg.astype(BF16))
                    dss.append((pg * (dps[hh][:, g] - delta_rep)).astype(BF16))
                prob, ds = jnp.concatenate(probs, axis=1), jnp.concatenate(dss, axis=1)
                dot_pad = jnp.concatenate([zeros_t, dyt_b[hh * V_HEAD:(hh + 1) * V_HEAD, :]], axis=0)
                k_eff = jnp.where(lane < QK_NOPE, kv_ref[:, sl], krb)
                dvt = jnp.dot(dot_pad, prob, preferred_element_type=F32)
                dq_ref[rows, sl] += jnp.dot(ds, k_eff, preferred_element_type=F32)
                dkt = jnp.dot(qt_ref[sl, :], ds, preferred_element_type=F32)
                dkv_s[hh] += dvt + jnp.where(row < QK_NOPE, dkt, 0.0)
                dkr_acc = dkr_acc + jnp.where(row >= QK_NOPE, dkt, 0.0)
            dkrt_ref[:, cols] += dkr_acc

        @pl.when(i > j)
        def _():
            step(False)

        @pl.when(i == j)
        def _():
            step(True)

        @pl.when(i == nb - 1)
        def _():
            dkvt_ref[0:LANES, :] = dkv_s[0].astype(BF16)
            dkvt_ref[LANES:, :] = dkv_s[1].astype(BF16)

    qi = lambda p, j, i: (jnp.maximum(i, j), p)
    qti = lambda p, j, i: (p, jnp.maximum(i, j))
    return pl.pallas_call(
        body, name="att_bwd", grid=(N_HEADS // 2, nb, nb),
        in_specs=[pl.BlockSpec((t, 2 * LANES), qi), pl.BlockSpec((2 * LANES, t), qti),
                  pl.BlockSpec((t, 2 * LANES), lambda p, j, i: (j, p)), pl.BlockSpec((2 * LANES, t), lambda p, j, i: (p, j)),
                  pl.BlockSpec((t, LANES), lambda p, j, i: (j, 0)), pl.BlockSpec((LANES, t), lambda p, j, i: (0, j)),
                  pl.BlockSpec((t, LANES), qi), pl.BlockSpec((t, LANES), qi), pl.BlockSpec((t, LANES), qi),
                  pl.BlockSpec((LANES, t), qti)],
        out_specs=[pl.BlockSpec((s, 2 * LANES), lambda p, j, i: (0, p)),
                   pl.BlockSpec((2 * LANES, t), lambda p, j, i: (p, j)),
                   pl.BlockSpec((LANES, s), lambda p, j, i: (0, 0))],
        out_shape=[jax.ShapeDtypeStruct((s, N_HEADS * LANES), F32), jax.ShapeDtypeStruct((N_HEADS * LANES, s), BF16),
                   jax.ShapeDtypeStruct((LANES, s), F32)],
        scratch_shapes=[pltpu.VMEM((2, LANES, t), F32)],
        compiler_params=_cparams(("arbitrary", "arbitrary", "arbitrary")),
    )(q, qt, kv, kvt, kr, krt, y, lse, dy, dyt)


FFN_COLS = 256


def _ffn_conv(pad_ref, w, base, n):
    u = w[3:4] + w[2:3] * _shift_down(pad_ref, base, n, 0)
    for j in range(1, 3):
        u = u + w[2 - j:3 - j] * _shift_down(pad_ref, base, n, j)
    return u


def _ffn_act_fwd(up, fp):
    s, f2 = up.shape
    f = f2 // 2
    tc = FFN_COLS
    ts = min(RNN_CHUNK, s)
    nfb = f // tc

    def body(ug_ref, uv_ref, wg_ref, wv_ref, act_ref, gpad, vpad):
        zero8 = jnp.zeros((SUBLANES, tc), F32)
        gpad[0:SUBLANES, :] = zero8
        vpad[0:SUBLANES, :] = zero8
        gpad[SUBLANES:, :] = ug_ref[...]
        vpad[SUBLANES:, :] = uv_ref[...]
        wg, wv = wg_ref[...], wv_ref[...]

        def chunk(c, carry):
            base = pl.multiple_of(c * ts, ts)
            g = _ffn_conv(gpad, wg, base, ts)
            v = _ffn_conv(vpad, wv, base, ts)
            act_ref[pl.ds(base, ts), :] = (g * _sigmoid(g) * v).astype(BF16)
            return carry

        lax.fori_loop(0, s // ts, chunk, 0)

    return pl.pallas_call(
        body, name="ffn_act_fwd", grid=(nfb,),
        in_specs=[pl.BlockSpec((s, tc), lambda b: (0, b)), pl.BlockSpec((s, tc), lambda b: (0, b + nfb)),
                  pl.BlockSpec((SUBLANES, tc), lambda b: (0, b)), pl.BlockSpec((SUBLANES, tc), lambda b: (0, b + nfb))],
        out_specs=pl.BlockSpec((s, tc), lambda b: (0, b)),
        out_shape=jax.ShapeDtypeStruct((s, f), BF16),
        scratch_shapes=[pltpu.VMEM((s + SUBLANES, tc), F32)] * 2,
        compiler_params=_cparams(("arbitrary",)),
    )(up, up, fp, fp)


def _ffn_act_bwd(up, dact, fp):
    s, f2 = up.shape
    f = f2 // 2
    tc = FFN_COLS
    ts = min(RNN_CHUNK, s)
    nfb = f // tc

    def body(ug_ref, uv_ref, da_ref, wg_ref, wv_ref, dup_ref, red_ref, gpad, vpad, dgs, dvs):
        half = pl.program_id(1)
        wg, wv = wg_ref[...], wv_ref[...]

        @pl.when(half == 0)
        def _():
            zero8 = jnp.zeros((SUBLANES, tc), F32)
            gpad[0:SUBLANES, :] = zero8
            vpad[0:SUBLANES, :] = zero8
            gpad[SUBLANES:, :] = ug_ref[...]
            vpad[SUBLANES:, :] = uv_ref[...]
            dgs[s:, :] = zero8
            dvs[s:, :] = zero8

            def act(c, carry):
                base = pl.multiple_of(c * ts, ts)
                g = _ffn_conv(gpad, wg, base, ts)
                v = _ffn_conv(vpad, wv, base, ts)
                da = da_ref[pl.ds(base, ts), :]
                sg = _sigmoid(g)
                dgs[pl.ds(base, ts), :] = da * v * (sg * (1.0 + g * (1.0 - sg)))
                dvs[pl.ds(base, ts), :] = da * (g * sg)
                return carry

            lax.fori_loop(0, s // ts, act, 0)

        def conv_t(src, pad, w, out_ref, red_ref):
            def chunk(c, carry):
                base = pl.multiple_of(c * ts, ts)
                d_here = src[pl.ds(base, ts), :]
                dx = w[2:3] * d_here
                for j in range(1, 3):
                    dx = dx + w[2 - j:3 - j] * _shift_up(src, base, ts, j, 0)
                out_ref[pl.ds(base, ts), :] = dx.astype(BF16)
                taps = tuple(carry[k] + _colsum(d_here * _shift_down(pad, base, ts, 2 - k)) for k in range(3))
                return taps + (carry[3] + _colsum(d_here),)

            z1 = jnp.zeros((1, tc), F32)
            red = lax.fori_loop(0, s // ts, chunk, (z1, z1, z1, z1))
            red_ref[...] = jnp.concatenate(list(red) + [jnp.zeros((4, tc), F32)], axis=0)

        @pl.when(half == 0)
        def _():
            conv_t(dgs, gpad, wg, dup_ref, red_ref)

        @pl.when(half == 1)
        def _():
            conv_t(dvs, vpad, wv, dup_ref, red_ref)

    gcol = pl.BlockSpec((s, tc), lambda b, h: (0, b))
    vcol = pl.BlockSpec((s, tc), lambda b, h: (0, b + nfb))
    gpar = pl.BlockSpec((SUBLANES, tc), lambda b, h: (0, b))
    vpar = pl.BlockSpec((SUBLANES, tc), lambda b, h: (0, b + nfb))
    return pl.pallas_call(
        body, name="ffn_act_bwd", grid=(nfb, 2),
        in_specs=[gcol, vcol, gcol, gpar, vpar],
        out_specs=[pl.BlockSpec((s, tc), lambda b, h: (0, b + h * nfb)),
                   pl.BlockSpec((SUBLANES, tc), lambda b, h: (0, b + h * nfb))],
        out_shape=[jax.ShapeDtypeStruct((s, f2), BF16), jax.ShapeDtypeStruct((SUBLANES, f2), F32)],
        scratch_shapes=[pltpu.VMEM((s + SUBLANES, tc), F32)] * 4,
        compiler_params=_cparams(("arbitrary", "arbitrary")),
    )(up, up, dact, fp, fp)


def _rows8(rows, width):
    rows = [r.reshape(1, width).astype(F32) for r in rows]
    return jnp.concatenate(rows + [jnp.zeros((SUBLANES - len(rows), width), F32)], axis=0)


def _block_diag(w):
    n, b, _ = w.shape
    w = w.reshape(n // 2, 2, b, b)
    z = jnp.zeros((n // 2, b, b), w.dtype)
    top = jnp.concatenate([w[:, 0], z], axis=2)
    bot = jnp.concatenate([z, w[:, 1]], axis=2)
    return jnp.concatenate([top, bot], axis=1)


def _block_diag_t(bd):
    n, b2, _ = bd.shape
    b = b2 // 2
    return jnp.stack([bd[:, :b, :b], bd[:, b:, b:]], axis=1).reshape(2 * n, b, b)


def _local_step(x, mod, positions, target, w_in, fetch, sm, emit):
    s, d = x.shape
    o_rnn, o_mla = D_RNN, D_RNN + Q_LORA + KV_LORA + QK_ROPE
    wts = {}
    w_in_rnn = w_in[:, :o_rnn]
    w_in_mla = jnp.concatenate([w_in[:, o_rnn:o_mla], jnp.zeros((d, MLA_W - (o_mla - o_rnn)), w_in.dtype)], axis=1)
    w_in_g = w_in[:, o_mla:]
    hd = QK_NOPE + QK_ROPE
    wa_bd = _block_diag(sm["w_gate_a"]).astype(BF16)
    wx_bd = _block_diag(sm["w_gate_x"]).astype(BF16)

    pos = positions.reshape(s)
    half = QK_ROPE // 2
    inv_freq = ROPE_THETA ** (-jnp.arange(half, dtype=F32) / half)
    ang = pos.astype(F32)[:, None] * inv_freq
    cos, sin = jnp.cos(ang), jnp.sin(ang)
    rot_c = jnp.concatenate([jnp.ones((s, QK_NOPE), F32), cos, cos, jnp.ones((s, LANES - hd), F32)], axis=1)
    rot_s = jnp.concatenate([jnp.zeros((s, QK_NOPE), F32), -sin, sin, jnp.zeros((s, LANES - hd), F32)], axis=1)
    keep = (pos != 0).astype(F32).reshape(s, 1)

    gmod1 = _rows8([sm["norm1_g"], mod[1], mod[0]], d)
    gmod2 = _rows8([sm["norm2_g"], mod[4], mod[3], mod[2]], d)
    rp = jnp.concatenate([sm["conv_w"].reshape(4, D_RNN), _rows8([sm["conv_b"], sm["b_gate_a"], sm["b_gate_x"], sm["lru_param"]], D_RNN)[:4]], axis=0)
    fp = _rows8([sm["ffn_conv_w"][0], sm["ffn_conv_w"][1], sm["ffn_conv_w"][2], sm["ffn_conv_b"]], 2 * D_FF)
    ng = _rows8([jnp.concatenate([sm["q_norm_g"].reshape(-1), sm["kv_norm_g"].reshape(-1), jnp.zeros((MLA_W - Q_LORA - KV_LORA,), F32)])], MLA_W)
    fpar = _rows8([mod[5], sm["final_g"]], d)

    h = _norm_mod_fwd(x, gmod1, "norm1_fwd")
    proj_rnn = _mm(h, w_in_rnn, name="mm_in_rnn")
    proj_mla = _mm(h, w_in_mla, name="mm_in_mla")
    proj_g = _mm(h, w_in_g, name="mm_in_g")
    xc, ra, ix, hs = _rnn_fwd(proj_rnn, keep, rp, wa_bd, wx_bd)
    qn, kvn, kr = _mla_prep_fwd(proj_mla, rot_c, rot_s, ng)
    wts.update(fetch(("w_uq", "w_ukv"), kr))
    w_uq_p = jnp.pad(wts["w_uq"].reshape(Q_LORA, N_HEADS, hd), ((0, 0), (0, 0), (0, LANES - hd))).reshape(Q_LORA, N_HEADS * LANES)
    w_ukv = wts["w_ukv"]
    q_rot = _rope_heads(_mm(qn, w_uq_p, name="mm_uq"), rot_c, rot_s, False, "rope_fwd")
    kv = _mm(kvn, w_ukv, out_dtype=BF16, name="mm_ukv")
    kvt, krt = jnp.transpose(kv), jnp.transpose(kr)
    y_mla, lse = _att_fwd(q_rot, kv, kvt, krt)
    wts.update(fetch(("w_proj_rnn", "w_proj_mla", "w_out", "w_up", "w_down"), lse))
    pr = _mm(hs, wts["w_proj_rnn"], name="mm_proj_rnn")
    pm = _mm(y_mla, wts["w_proj_mla"], name="mm_proj_mla")
    merged = _merge_fwd(pr, pm, proj_g)
    o = _mm(merged, wts["w_out"], name="mm_out")
    x1, h2 = _resid_norm_fwd(x, o, gmod2)
    up = _mm(h2, wts["w_up"], name="mm_up")
    act = _ffn_act_fwd(up, fp)
    dn = _mm(act, wts["w_down"], name="mm_down")

    dx2, ddn, red_f = _final_fwd_bwd(x1, dn, target, fpar)
    dact = _mm(ddn, wts["w_down"], tb=True, name="mm_d_act")
    tok = emit("w_down", _mm(act, ddn, ta=True, out_dtype=BF16, name="mm_dw_down"))
    dup, red_ffn = _ffn_act_bwd(up, dact, fp + tok)
    dh2 = _mm(dup, wts["w_up"], tb=True, name="mm_d_h2")
    tok = tok + emit("w_up", _mm(h2, dup, ta=True, out_dtype=BF16, name="mm_dw_up"))
    dx1, do, red_2 = _norm2_bwd(x1, dh2, dx2, o, gmod2 + tok)
    dmerged = _mm(do, wts["w_out"], tb=True, name="mm_d_merged")
    tok = tok + emit("w_out", _mm(merged, do, ta=True, out_dtype=BF16, name="mm_dw_out"))
    dpr, dpm, dg = _merge_bwd(dmerged, pr, pm, proj_g)
    dy_rnn = _mm(dpr, wts["w_proj_rnn"], tb=True, name="mm_d_yrnn")
    tok = tok + emit("w_proj_rnn", _mm(hs, dpr, ta=True, out_dtype=BF16, name="mm_dw_proj_rnn"))
    dy_mla = _mm(dpm, wts["w_proj_mla"], tb=True, name="mm_d_ymla")
    tok = tok + emit("w_proj_mla", _mm(y_mla, dpm, ta=True, out_dtype=BF16, name="mm_dw_proj_mla"))
    dq_rot, dkvt, dkrt = _att_bwd(q_rot, jnp.transpose(q_rot), kv, kvt, kr, krt, y_mla, lse, dy_mla,
                                  jnp.transpose(dy_mla.astype(BF16)))
    dq = _rope_heads(dq_rot, rot_c, rot_s, True, "rope_bwd")
    dqn = _mm(dq, w_uq_p, tb=True, name="mm_d_qn")
    dw_uq_p = _mm(qn, dq, ta=True, out_dtype=BF16, name="mm_dw_uq")
    tok = tok + emit("w_uq", dw_uq_p.reshape(Q_LORA, N_HEADS, LANES)[:, :, :hd].reshape(Q_LORA, N_HEADS * hd))
    dkvn = jnp.transpose(_mm(w_ukv, dkvt, name="mm_d_kvn"))
    tok = tok + emit("w_ukv", jnp.transpose(_mm(dkvt, kvn, out_dtype=BF16, name="mm_dw_ukv")))
    dproj_mla, red_m = _mla_prep_bwd(proj_mla, dqn, dkvn, jnp.transpose(dkrt), rot_c, rot_s, ng + tok)
    dx_rnn, dwa_bd, dwx_bd, red_r = _rnn_bwd(proj_rnn, xc, ra, ix, hs, dy_rnn, keep, rp + tok, wa_bd, wx_bd)
    dw_in = jnp.concatenate([
        _mm(h, dx_rnn, ta=True, out_dtype=BF16, name="mm_dw_in_rnn"),
        _mm(h, dproj_mla, ta=True, out_dtype=BF16, name="mm_dw_in_mla")[:, :o_mla - o_rnn],
        _mm(h, dg, ta=True, out_dtype=BF16, name="mm_dw_in_g")], axis=1)
    tok = tok + emit("w_in", dw_in)
    dh_a = _mm(dx_rnn, w_in_rnn, tb=True, name="mm_d_h_rnn")
    dh_b = _mm(dproj_mla, w_in_mla, tb=True, name="mm_d_h_mla")
    dh_c = _mm(dg, w_in_g, tb=True, name="mm_d_h_g")
    grad_x, red_1 = _norm1_bwd(x, dh_a, dh_b, dh_c, dx1, gmod1 + tok)

    gs = {
        "norm1_g": red_1[0], "conv_w": red_r[0:4], "conv_b": red_r[4], "w_gate_a": _block_diag_t(dwa_bd),
        "b_gate_a": red_r[5], "w_gate_x": _block_diag_t(dwx_bd), "b_gate_x": red_r[6], "lru_param": red_r[7],
        "q_norm_g": red_m[0, :Q_LORA], "kv_norm_g": red_m[0, Q_LORA:Q_LORA + KV_LORA], "norm2_g": red_2[0],
        "ffn_conv_w": red_ffn[0:3], "ffn_conv_b": red_ffn[3], "final_g": red_f[0],
    }
    dmod = jnp.stack([red_1[2], red_1[1], red_2[3], red_2[2], red_2[1], red_f[1]], axis=0)
    return red_f[2, 0], grad_x, gs, dmod


MESH_IDS = pl.DeviceIdType.MESH
HBM_SPEC = pl.BlockSpec(memory_space=pltpu.HBM)


def _my_slot():
    return 4 * lax.axis_index("x") + 2 * lax.axis_index("y") + lax.axis_index("c")


def _all_gather(arrs, name):
    n = len(arrs)

    def body(*refs):
        ins, outs = refs[:n], refs[n:2 * n]
        send_sems, recv_sems, local_sems = refs[2 * n:]
        x, y, c = lax.axis_index("x"), lax.axis_index("y"), lax.axis_index("c")
        me, sibling = (x, y, c), (x, y, 1 - c)
        chips = [(1 - x, y), (x, 1 - y), (1 - x, 1 - y)]

        def slot(dev):
            return 4 * dev[0] + 2 * dev[1] + dev[2]

        def copy(a, k, block, to, src=None):
            dst = outs[a].at[slot(block)]
            return pltpu.make_async_remote_copy(
                src_ref=dst if src is None else src, dst_ref=dst, send_sem=send_sems.at[a, k], recv_sem=recv_sems.at[a, k],
                device_id=to, device_id_type=MESH_IDS)

        mine = [pltpu.make_async_copy(ins[a], outs[a].at[slot(me)], local_sems.at[a]) for a in range(n)]
        for cp in mine:
            cp.start()
        first = []
        for a in range(n):
            first.append(copy(a, 0, me, sibling, src=ins[a]))
            first += [copy(a, 1 + j, me, (*chip, c), src=ins[a]) for j, chip in enumerate(chips)]
        for cp in first:
            cp.start()
        passed = []
        for j, chip in enumerate(chips):
            for a in range(n):
                copy(a, 1 + j, (*chip, c), me).wait_recv()
                fwd = copy(a, 4 + j, (*chip, c), sibling)
                fwd.start()
                passed.append(fwd)
        for a in range(n):
            copy(a, 0, sibling, me).wait_recv()
            for j, chip in enumerate(chips):
                copy(a, 4 + j, (*chip, 1 - c), me).wait_recv()
        for cp in first + passed:
            cp.wait_send()
        for cp in mine:
            cp.wait()

    return pl.pallas_call(
        body, name=name,
        in_specs=[HBM_SPEC] * n, out_specs=[HBM_SPEC] * n,
        out_shape=[jax.ShapeDtypeStruct((N_DEV,) + a.shape, a.dtype) for a in arrs],
        scratch_shapes=[pltpu.SemaphoreType.DMA((n, 7)), pltpu.SemaphoreType.DMA((n, 7)), pltpu.SemaphoreType.DMA((n,))],
    )(*arrs)


SEM_SPEC =pl.BlockSpec(memory_space=pltpu.SEMAPHORE)
DATAFLOW = pltpu.SideEffectType.DATAFLOW_SIDE_EFFECTING
FLIPS = [(dx, dy, dc) for dx in (0, 1) for dy in (0, 1) for dc in (0, 1)][1:]


def _peer(k):
    dx, dy, dc = FLIPS[k]
    peer = (lax.axis_index("x") ^ dx, lax.axis_index("y") ^ dy, lax.axis_index("c") ^ dc)
    return peer, 4 * peer[0] + 2 * peer[1] + peer[2]


def _gather_start(shards, after, name):
    n, nf = len(shards), len(FLIPS)

    def body(*refs):
        srcs, lands = refs[:n], refs[n:2 * n]
        send_sems, recv_sems = refs[2 * n + 1:3 * n + 1], refs[3 * n + 1:4 * n + 1]
        token = refs[-1]
        me = _my_slot()
        for a in range(n):
            for k in range(nf):
                peer, _ = _peer(k)
                pltpu.make_async_remote_copy(
                    src_ref=srcs[a], dst_ref=lands[a].at[me], send_sem=send_sems[a].at[k], recv_sem=recv_sems[a].at[k],
                    device_id=peer, device_id_type=MESH_IDS).start()
        token[...] = jnp.zeros(token.shape, F32)

    land_shapes = [(N_DEV,) + a.shape for a in shards]
    sems = [pltpu.SemaphoreType.DMA((nf,))] * n
    out = pl.pallas_call(
        body, name=name,
        out_shape=(*sems, *sems, *[pltpu.HBM(a.shape, a.dtype) for a in shards],
                   *[pltpu.HBM(shp, a.dtype) for shp, a in zip(land_shapes, shards)],
                   jax.ShapeDtypeStruct((SUBLANES, LANES), F32)),
        in_specs=[HBM_SPEC] * (2 * n) + [pl.BlockSpec(memory_space=pl.ANY)],
        out_specs=(*[SEM_SPEC] * (2 * n), *[HBM_SPEC] * (2 * n), pl.BlockSpec(memory_space=pltpu.VMEM)),
        input_output_aliases={i: 2 * n + i for i in range(2 * n)},
        compiler_params=pltpu.CompilerParams(has_side_effects=DATAFLOW),
    )(*[pltpu.with_memory_space_constraint(a, pltpu.HBM) for a in shards],
      *[pltpu.with_memory_space_constraint(lax.empty(shp, a.dtype), pltpu.HBM) for shp, a in zip(land_shapes, shards)],
      after)
    return [(out[a], out[n + a], out[2 * n + a], out[3 * n + a]) for a in range(n)], out[-1]


def _gather_wait(flights, after, name):
    n, nf = len(flights), len(FLIPS)

    def body(*refs):
        send_sems, recv_sems = refs[:n], refs[n:2 * n]
        srcs, lands = refs[2 * n:3 * n], refs[3 * n:4 * n]
        for a in range(n):
            for k in range(nf):
                peer, peer_slot = _peer(k)
                cp = pltpu.make_async_remote_copy(
                    src_ref=srcs[a], dst_ref=lands[a].at[peer_slot], send_sem=send_sems[a].at[k],
                    recv_sem=recv_sems[a].at[k], device_id=peer, device_id_type=MESH_IDS)
                cp.wait_send()
                cp.wait_recv()

    srcs, lands = [f[2] for f in flights], [f[3] for f in flights]
    out = pl.pallas_call(
        body, name=name,
        out_shape=(*[pltpu.HBM(a.shape, a.dtype) for a in srcs], *[pltpu.HBM(a.shape, a.dtype) for a in lands]),
        in_specs=[SEM_SPEC] * (2 * n) + [HBM_SPEC] * (2 * n) + [pl.BlockSpec(memory_space=pl.ANY)],
        out_specs=tuple([HBM_SPEC] * (2 * n)),
        input_output_aliases={2 * n + i: i for i in range(2 * n)},
        compiler_params=pltpu.CompilerParams(has_side_effects=DATAFLOW),
    )(*[f[0] for f in flights], *[f[1] for f in flights], *srcs, *lands, after)
    return list(out[n:])


def _scatter_start(chunks, name):
    def body(src_ref, land_ref, send_sems, recv_sems, src_thru, land_thru, token):
        me = _my_slot()
        for k in range(len(FLIPS)):
            peer, peer_slot = _peer(k)
            pltpu.make_async_remote_copy(
                src_ref=src_ref.at[peer_slot], dst_ref=land_ref.at[me], send_sem=send_sems.at[k], recv_sem=recv_sems.at[k],
                device_id=peer, device_id_type=MESH_IDS).start()
        token[...] = jnp.zeros(token.shape, F32)

    n = len(FLIPS)
    hbm = pltpu.HBM(chunks.shape, chunks.dtype)
    return pl.pallas_call(
        body, name=name,
        out_shape=(pltpu.SemaphoreType.DMA((n,)), pltpu.SemaphoreType.DMA((n,)), hbm, hbm,
                   jax.ShapeDtypeStruct((SUBLANES, LANES), F32)),
        in_specs=(HBM_SPEC, HBM_SPEC),
        out_specs=(SEM_SPEC, SEM_SPEC, HBM_SPEC, HBM_SPEC, pl.BlockSpec(memory_space=pltpu.VMEM)),
        input_output_aliases={0: 2, 1: 3},
        compiler_params=pltpu.CompilerParams(has_side_effects=DATAFLOW),
    )(pltpu.with_memory_space_constraint(chunks, pltpu.HBM),
      pltpu.with_memory_space_constraint(lax.empty(chunks.shape, chunks.dtype), pltpu.HBM))


def _scatter_wait(send_sems, recv_sems, src_thru, land_thru, after, name):
    def body(src_ref, land_ref, send_sems, recv_sems, after_ref, src_dead, got_ref):
        for k in range(len(FLIPS)):
            peer, peer_slot = _peer(k)
            cp = pltpu.make_async_remote_copy(
                src_ref=src_ref.at[peer_slot], dst_ref=land_ref.at[peer_slot], send_sem=send_sems.at[k],
                recv_sem=recv_sems.at[k], device_id=peer, device_id_type=MESH_IDS)
            cp.wait_send()
            cp.wait_recv()

    hbm = pltpu.HBM(src_thru.shape, src_thru.dtype)
    return pl.pallas_call(
        body, name=name, out_shape=(hbm, hbm),
        in_specs=(HBM_SPEC, HBM_SPEC, SEM_SPEC, SEM_SPEC, pl.BlockSpec(memory_space=pl.ANY)),
        out_specs=(HBM_SPEC, HBM_SPEC), input_output_aliases={0: 0, 1: 1},
        compiler_params=pltpu.CompilerParams(has_side_effects=DATAFLOW),
    )(src_thru, land_thru, send_sems, recv_sems, after)


def _sum_sources(parts, name):
    k, r, c = parts.shape
    tr = r if k * r * c <= 2 * 1024 * 1024 else _pick(r, (512, 256, 128, 64, 32, 16, 8))

    def body(p_ref, o_ref):
        acc = p_ref[0].astype(F32)
        for s in range(1, k):
            acc = acc + p_ref[s].astype(F32)
        o_ref[...] = acc

    return pl.pallas_call(
        body, name=name, grid=(r // tr,),
        in_specs=[pl.BlockSpec((k, tr, c), lambda i: (0, i, 0))],
        out_specs=pl.BlockSpec((tr, c), lambda i: (i, 0)),
        out_shape=jax.ShapeDtypeStruct((r, c), F32),
        compiler_params=_cparams(("arbitrary",)),
    )(parts)


def _adamw(parts, w, m, v, name, own=None):
    k, r, c = parts.shape
    tr = r if r * c <= 256 * 1024 else _pick(r, (256, 128, 64, 32, 16, 8))

    def body(*refs):
        p_ref, w_ref, m_ref, v_ref = refs[:4]
        g_ref, d_ref, nm_ref, nv_ref = refs[-4:]

        def part(s):
            if own is None:
                return p_ref[s].astype(F32)
            return jnp.where(_my_slot() == s, refs[4][...], p_ref[s]).astype(F32)

        g = part(0)
        for s in range(1, k):
            g = g + part(s)
        m_new = ADAM_B1 * m_ref[...] + (1.0 - ADAM_B1) * g
        v_new = ADAM_B2 * v_ref[...] + (1.0 - ADAM_B2) * jnp.square(g)
        m_hat = m_new / (1.0 - ADAM_B1 ** ADAM_STEP)
        v_hat = v_new / (1.0 - ADAM_B2 ** ADAM_STEP)
        g_ref[...] = g
        d_ref[...] = -ADAM_LR * (m_hat / (jnp.sqrt(v_hat) + ADAM_EPS) + ADAM_WD * w_ref[...])
        nm_ref[...] = m_new
        nv_ref[...] = v_new

    blk = pl.BlockSpec((tr, c), lambda i: (i, 0))
    return pl.pallas_call(
        body, name=name, grid=(r // tr,),
        in_specs=[pl.BlockSpec((k, tr, c), lambda i: (0, i, 0)), blk, blk, blk] + ([] if own is None else [blk]),
        out_specs=[blk] * 4,
        out_shape=[jax.ShapeDtypeStruct((r, c), F32)] * 4,
        compiler_params=_cparams(("arbitrary",)),
    )(parts, w, m, v, *([] if own is None else [own]))


def _silu(v):
    return v * _sigmoid(v)


def _ada_fwd(c_all, w, b):
    def body(c_ref, w_ref, b_ref, o_ref):
        ca = _silu(c_ref[...]).astype(BF16)
        o_ref[...] = jnp.dot(ca, w_ref[...].astype(BF16), preferred_element_type=F32) + b_ref[...]

    return pl.pallas_call(
        body, name="ada_fwd", out_shape=jax.ShapeDtypeStruct((c_all.shape[0], w.shape[1]), F32),
        compiler_params=_cparams(),
    )(c_all, w, b)


def _ada_bwd(c_all, dmod):
    def body(c_ref, d_ref, o_ref):
        ca = _silu(c_ref[...]).astype(BF16).astype(F32)
        dm = d_ref[...].astype(BF16).astype(F32)
        acc = jnp.zeros(o_ref.shape, F32)
        for bi in range(c_all.shape[0]):
            acc = acc + jnp.transpose(ca[bi:bi + 1, :]) * dm[bi:bi + 1, :]
        o_ref[...] = acc

    return pl.pallas_call(
        body, name="ada_bwd", out_shape=jax.ShapeDtypeStruct((c_all.shape[1], dmod.shape[1]), F32),
        compiler_params=_cparams(),
    )(c_all, dmod)


COL_SHARDED = ("w_in", "w_uq", "w_ukv", "w_up")
ROW_SHARDED = ("w_proj_rnn", "w_proj_mla", "w_out", "w_down")
REPLICATED = ("b_ada", "norm1_g", "conv_b", "w_gate_a", "b_gate_a", "w_gate_x", "b_gate_x", "lru_param", "q_norm_g",
              "kv_norm_g", "norm2_g", "ffn_conv_b", "final_g")
WEIGHTS = ("w_ada", "b_ada", "norm1_g", "w_in", "conv_w", "conv_b", "w_gate_a", "b_gate_a", "w_gate_x", "b_gate_x",
           "lru_param", "q_norm_g", "w_uq", "kv_norm_g", "w_ukv", "w_proj_rnn", "w_proj_mla", "w_out", "norm2_g", "w_up",
           "ffn_conv_w", "ffn_conv_b", "w_down", "final_g")
PACK_LANES = 128


def _pack(vecs):
    flat = jnp.concatenate([v.reshape(-1).astype(F32) for v in vecs])
    pad = (-flat.shape[0]) % (PACK_LANES * SUBLANES)
    return jnp.concatenate([flat, jnp.zeros((pad,), F32)]).reshape(-1, PACK_LANES)


def _unpack(packed, shapes):
    flat = packed.reshape(-1)
    out, off = [], 0
    for shp in shapes:
        size = math.prod(shp)
        out.append(flat[off:off + size].reshape(shp))
        off += size
    return out


def kernel(x, c, positions, w_ada, b_ada, norm1_g, w_in, conv_w, conv_b, w_gate_a, b_gate_a, w_gate_x, b_gate_x, lru_param, q_norm_g, w_uq, kv_norm_g, w_ukv, w_proj_rnn, w_proj_mla, w_out, norm2_g, w_up, ffn_conv_w, ffn_conv_b, w_down, final_g, loss_target, m_w_ada, m_b_ada, m_norm1_g, m_w_in, m_conv_w, m_conv_b, m_w_gate_a, m_b_gate_a, m_w_gate_x, m_b_gate_x, m_lru_param, m_q_norm_g, m_w_uq, m_kv_norm_g, m_w_ukv, m_w_proj_rnn, m_w_proj_mla, m_w_out, m_norm2_g, m_w_up, m_ffn_conv_w, m_ffn_conv_b, m_w_down, m_final_g, v_w_ada, v_b_ada, v_norm1_g, v_w_in, v_conv_w, v_conv_b, v_w_gate_a, v_b_gate_a, v_w_gate_x, v_b_gate_x, v_lru_param, v_q_norm_g, v_w_uq, v_kv_norm_g, v_w_ukv, v_w_proj_rnn, v_w_proj_mla, v_w_out, v_norm2_g, v_w_up, v_ffn_conv_w, v_ffn_conv_b, v_w_down, v_final_g):
    args = dict(locals())
    w = {n: args[n] for n in WEIGHTS}
    m = {n: args["m_" + n] for n in WEIGHTS}
    v = {n: args["v_" + n] for n in WEIGHTS}
    s, d = x.shape[1], x.shape[2]
    me = _my_slot()
    def two_d(a):
        assert a.ndim == 3 and a.shape[0] == 1, a.shape
        return a[0]

    big = COL_SHARDED + ROW_SHARDED
    shard = {n: two_d(w[n]).astype(BF16) for n in big}

    def whole(n, g):
        k, r, cc = g.shape
        return jnp.transpose(g, (1, 0, 2)).reshape(r, k * cc) if n in COL_SHARDED else g.reshape(k * r, cc)

    first = _all_gather([shard["w_in"], c, two_d(conv_w), two_d(ffn_conv_w)], "gather_first")
    c_all = first[1].reshape(N_DEV, d)
    conv_w_all = jnp.transpose(first[2], (1, 0, 2)).reshape(conv_w.shape[1], -1)
    ffn_conv_w_all = jnp.transpose(first[3], (1, 0, 2)).reshape(ffn_conv_w.shape[1], -1)

    ada_cols = w_ada.shape[2]
    b_cols = lax.dynamic_slice(b_ada, (0, me * ada_cols), (1, ada_cols))
    mod_cols = _ada_fwd(c_all, w_ada[0], b_cols)
    mod_all, = _all_gather([mod_cols], "gather_mod")

    later = ("w_uq", "w_ukv", "w_proj_rnn", "w_proj_mla", "w_out", "w_up", "w_down")
    flights, started = _gather_start([shard[n] for n in later], mod_all, "gather_start")
    flight = dict(zip(later, flights))

    def fetch(names, after):
        lands = _gather_wait([flight[n] for n in names], after, "gather_wait_" + names[0])
        return {n: whole(n, lax.dynamic_update_index_in_dim(g, shard[n], me, 0)) for n, g in zip(names, lands)}

    mod = lax.dynamic_index_in_dim(mod_all, me, axis=1, keepdims=False).reshape(6, d) + started[0, 0]

    sm = {n: w[n][0] for n in REPLICATED if n not in ("b_ada", "final_g")}
    sm["final_g"] = final_g
    sm["conv_w"] = conv_w_all
    sm["ffn_conv_w"] = ffn_conv_w_all
    in_flight = {}

    def emit(n, g):
        if n in COL_SHARDED:
            r, cc = g.shape
            chunks = jnp.transpose(g.reshape(r, N_DEV, cc // N_DEV), (1, 0, 2))
        else:
            chunks = g.reshape(N_DEV, g.shape[0] // N_DEV, g.shape[1])
        *in_flight[n], token = _scatter_start(chunks, "scatter_start_" + n)
        return token[0, 0]

    sq, grad_x, gs, dmod = _local_step(x[0], mod, positions[0], loss_target[0], whole("w_in", first[0]), fetch, sm, emit)

    small_names = [n for n in REPLICATED if n != "b_ada"] + ["conv_w", "ffn_conv_w"]
    small_shapes = [gs[n].shape for n in small_names] + [(6 * d,), (1,)]
    partial = _pack([gs[n] for n in small_names] + [dmod, sq.reshape(1)])
    partial_all, = _all_gather([partial], "gather_small")
    summed = _unpack(_sum_sources(partial_all, "sum_small"), small_shapes)
    g_small = dict(zip(small_names, summed[:len(small_names)]))
    g_small["b_ada"] = summed[len(small_names)]
    loss = 0.5 * summed[-1][0] / d
    n_before = sum(math.prod(t) for t in small_shapes[:len(small_names)])
    dmod_all = partial_all.reshape(N_DEV, -1)[:, n_before:n_before + 6 * d]
    dmod_cols = lax.dynamic_slice(dmod_all, (0, me * ada_cols), (N_DEV, ada_cols))

    grads, deltas, new_m, new_v = {}, {}, {}, {}

    def update(n, parts, own=None):
        shp = w[n].shape
        g, dl, nm, nv = _adamw(parts, two_d(w[n]), two_d(m[n]), two_d(v[n]), "adamw_" + n, own)
        grads[n], deltas[n], new_m[n], new_v[n] = g.reshape(shp), dl.reshape(shp), nm.reshape(shp), nv.reshape(shp)

    update("w_ada", _ada_bwd(c_all, dmod_cols)[None])

    for n in big:
        chunks, landed = _scatter_wait(*in_flight[n], grad_x, "scatter_wait_" + n)
        update(n, landed, lax.dynamic_index_in_dim(chunks, me, axis=0, keepdims=False))

    for n in ("conv_w", "ffn_conv_w"):
        cols = w[n].shape[2]
        update(n, lax.dynamic_slice(g_small[n], (0, me * cols), (g_small[n].shape[0], cols))[None])

    rep_shapes = [w[n].shape for n in REPLICATED]
    g_rep, d_rep, m_rep, v_rep = _adamw(
        _pack([g_small[n] for n in REPLICATED])[None], _pack([w[n] for n in REPLICATED]), _pack([m[n] for n in REPLICATED]),
        _pack([v[n] for n in REPLICATED]), "adamw_replicated")
    for dst, packed in ((grads, g_rep), (deltas, d_rep), (new_m, m_rep), (new_v, v_rep)):
        dst.update(zip(REPLICATED, _unpack(packed, rep_shapes)))

    return (loss, grad_x[None], *[grads[n] for n in WEIGHTS], *[deltas[n] for n in WEIGHTS],
            *[new_m[n] for n in WEIGHTS], *[new_v[n] for n in WEIGHTS])
```

```python
import functools
import math

import jax
import jax.numpy as jnp
from jax import lax
from jax.experimental import pallas as pl
from jax.experimental.pallas import tpu as pltpu

F32 = jnp.float32
BF16 = jnp.bfloat16

N_DEV = 8
LANES = 128
SUBLANES = 8
VMEM_LIMIT = 56 * 1024 * 1024

D_RNN = 1280
Q_LORA = 384
KV_LORA = 256
QK_NOPE = 64
QK_ROPE = 32
V_HEAD = 64
N_HEADS = 16
D_FF = 2816
ROPE_THETA = 10000.0
LRU_C = 8.0
EPS = 1e-6
MLA_W = 768
ATT_SCALE = 1.0 / math.sqrt(QK_NOPE + QK_ROPE)

ADAM_LR, ADAM_B1, ADAM_B2, ADAM_EPS, ADAM_WD, ADAM_STEP = 0.001, 0.9, 0.999, 1e-08, 0.01, 10


def _cparams(sem=None):
    return pltpu.CompilerParams(dimension_semantics=sem, vmem_limit_bytes=VMEM_LIMIT)


def _pick(n, prefs):
    for p in prefs:
        if n % p == 0:
            return p
    return n


def _sigmoid(v):
    return 0.5 * jnp.tanh(0.5 * v) + 0.5


def _lane(shape):
    return lax.broadcasted_iota(jnp.int32, shape, len(shape) - 1)


def _row(shape):
    return lax.broadcasted_iota(jnp.int32, shape, len(shape) - 2)


MM_BLOCK_BYTES = 36 * 1024 * 1024


def _divisors(n):
    return [t for t in range(n, 0, -LANES) if n % t == 0] if n % LANES == 0 else [n]


HBM_BYTES_PER_US = 3.0e6
MXU_FLOPS_PER_US = 8.0e8
GRID_STEP_US = 0.35


def _mm_tiles(m, n, k, a_bytes, b_bytes, o_bytes):
    best = None
    for tm in [t for t in _divisors(m) if t <= 1024]:
        for tn in [t for t in _divisors(n) if t <= 2048]:
            for tk in _divisors(k):
                nk = k // tk
                need = 2 * (tm * tk * a_bytes + tk * tn * b_bytes + tm * tn * o_bytes) + (tm * tn * 4 if nk > 1 else 0)
                if need > MM_BLOCK_BYTES:
                    continue
                gi, gj = m // tm, n // tn
                for rows_outer in (True, False):
                    if nk > 1:
                        a_reads, b_reads = gj, gi
                    elif rows_outer:
                        a_reads, b_reads = 1, (gi if gj > 1 else 1)
                    else:
                        a_reads, b_reads = (gj if gi > 1 else 1), 1
                    traffic = m * k * a_bytes * a_reads + k * n * b_bytes * b_reads + m * n * (o_bytes + (8 * nk if nk > 1 else 0))
                    cost = max(traffic / HBM_BYTES_PER_US, 2.0 * m * n * k / MXU_FLOPS_PER_US) + gi * gj * nk * GRID_STEP_US
                    if best is None or cost < best[0]:
                        best = (cost, tm, tn, tk, rows_outer)
                break
    if best is None:
        raise ValueError((m, n, k))
    return best[1:]


def _mm(a, b, *, ta=False, tb=False, out_dtype=F32, also_t=None, name):
    (k_a, m) = a.shape if ta else a.shape[::-1]
    (n, k_b) = b.shape if tb else b.shape[::-1]
    assert k_a == k_b, (a.shape, b.shape, ta, tb)
    k = k_a
    tm, tn, tk, rows_outer = _mm_tiles(m, n, k, a.dtype.itemsize, b.dtype.itemsize, jnp.dtype(out_dtype).itemsize)
    nk = k // tk
    dims = (((0 if ta else 1,), (1 if tb else 0,)), ((), ()))
    n_out = 1 if also_t is None else 2

    def body(a_ref, b_ref, *rest):
        outs, acc = rest[:n_out], rest[n_out:]
        part = lax.dot_general(a_ref[...].astype(BF16), b_ref[...].astype(BF16), dims, preferred_element_type=F32)

        def write(val):
            outs[0][...] = val.astype(out_dtype)
            if also_t is not None:
                outs[1][...] = jnp.transpose(val).astype(also_t)

        if nk == 1:
            write(part)
            return
        acc_ref, = acc
        kk = pl.program_id(2)

        @pl.when(kk == 0)
        def _():
            acc_ref[...] = part

        @pl.when(kk > 0)
        def _():
            acc_ref[...] += part

        @pl.when(kk == nk - 1)
        def _():
            write(acc_ref[...])

    def ij(f):
        return (lambda i, j, kk: f(i, j, kk)) if rows_outer else (lambda j, i, kk: f(i, j, kk))

    a_spec = pl.BlockSpec((tk, tm), ij(lambda i, j, kk: (kk, i))) if ta else pl.BlockSpec((tm, tk), ij(lambda i, j, kk: (i, kk)))
    b_spec = pl.BlockSpec((tn, tk), ij(lambda i, j, kk: (j, kk))) if tb else pl.BlockSpec((tk, tn), ij(lambda i, j, kk: (kk, j)))
    out_specs = [pl.BlockSpec((tm, tn), ij(lambda i, j, kk: (i, j)))]
    out_shape = [jax.ShapeDtypeStruct((m, n), out_dtype)]
    if also_t is not None:
        out_specs.append(pl.BlockSpec((tn, tm), ij(lambda i, j, kk: (j, i))))
        out_shape.append(jax.ShapeDtypeStruct((n, m), also_t))
    res = pl.pallas_call(
        body, name=name,
        grid=(m // tm, n // tn, nk) if rows_outer else (n // tn, m // tm, nk),
        in_specs=[a_spec, b_spec], out_specs=out_specs, out_shape=out_shape,
        scratch_shapes=[] if nk == 1 else [pltpu.VMEM((tm, tn), F32)],
        compiler_params=_cparams(("arbitrary", "arbitrary", "arbitrary")),
    )(a, b)
    return res[0] if also_t is None else res


def _rowwise(fn, row_ins, par_ins, out_defs, red_defs, *, name, tr=256):
    s = row_ins[0].shape[0]
    tr = min(tr, s)
    nr, npar, no = len(row_ins), len(par_ins), len(out_defs)

    def body(*refs):
        rin, pin = refs[:nr], refs[nr:nr + npar]
        outs, reds = refs[nr + npar:nr + npar + no], refs[nr + npar + no:]
        i = pl.program_id(0)

        @pl.when(i == 0)
        def _():
            for r in reds:
                r[...] = jnp.zeros_like(r)

        fn(i, rin, pin, outs, reds)

    in_specs = [pl.BlockSpec((tr, a.shape[1]), lambda i: (i, 0)) for a in row_ins]
    in_specs += [pl.BlockSpec(a.shape, lambda i, nd=a.ndim: (0,) * nd) for a in par_ins]
    out_specs = [pl.BlockSpec((tr, c), lambda i: (i, 0)) for c, _ in out_defs]
    out_specs += [pl.BlockSpec(shp, lambda i: (0, 0)) for shp in red_defs]
    out_shape = [jax.ShapeDtypeStruct((s, c), dt) for c, dt in out_defs]
    out_shape += [jax.ShapeDtypeStruct(shp, F32) for shp in red_defs]
    return pl.pallas_call(
        body, name=name, grid=(s // tr,), in_specs=in_specs, out_specs=out_specs, out_shape=out_shape,
        compiler_params=_cparams(("arbitrary",)),
    )(*row_ins, *par_ins)


def _rms(v):
    return lax.rsqrt(jnp.mean(v * v, axis=-1, keepdims=True) + EPS)


def _colsum(v):
    return jnp.sum(v, axis=0, keepdims=True)


def _rms_bwd(dn, n, rstd):
    return rstd * (dn - n * jnp.mean(dn * n, axis=-1, keepdims=True))


def _norm_mod_fwd(x, gmod, name):
    def fn(i, rin, pin, outs, reds):
        xv = rin[0][...]
        p = pin[0][...]
        n = xv * _rms(xv)
        outs[0][...] = ((n * p[0:1]) * (1.0 + p[1:2]) + p[2:3]).astype(BF16)

    return _rowwise(fn, [x], [gmod], [(x.shape[1], BF16)], [], name=name)[0]


def _rope(v, rot_c, rot_s):
    half = QK_ROPE // 2
    swapped = jnp.where(_lane(v.shape) < QK_NOPE + half, pltpu.roll(v, LANES - half, 1), pltpu.roll(v, half, 1))
    return v * rot_c + swapped * rot_s


def _rope_t(dv, rot_c, rot_s):
    half = QK_ROPE // 2
    ds = dv * rot_s
    lane = _lane(dv.shape)
    swapped = jnp.where(lane < QK_NOPE + half, pltpu.roll(ds, LANES - half, 1), pltpu.roll(ds, half, 1))
    in_rope = (lane >= QK_NOPE) & (lane < QK_NOPE + QK_ROPE)
    return dv * rot_c + jnp.where(in_rope, swapped, 0.0)


def _mla_prep_fwd(proj_mla, rot_c, rot_s, ng):
    o1, o2 = Q_LORA, Q_LORA + KV_LORA

    def fn(i, rin, pin, outs, reds):
        g = pin[0][...]
        ql = rin[0][:, 0:o1]
        kl = rin[0][:, o1:o2]
        outs[0][...] = (ql * _rms(ql) * g[0:1, 0:o1]).astype(BF16)
        outs[1][...] = (kl * _rms(kl) * g[0:1, o1:o2]).astype(BF16)
        kr = pltpu.roll(rin[0][:, o2:o2 + LANES], QK_NOPE, 1)
        outs[2][...] = _rope(kr, rin[1][...], rin[2][...]).astype(BF16)

    return _rowwise(fn, [proj_mla, rot_c, rot_s], [ng], [(Q_LORA, BF16), (KV_LORA, BF16), (LANES, BF16)], [],
                    name="mla_prep_fwd")


def _mla_prep_bwd(proj_mla, dqn, dkvn, dkr, rot_c, rot_s, ng):
    o1, o2 = Q_LORA, Q_LORA + KV_LORA

    def fn(i, rin, pin, outs, reds):
        g = pin[0][...]
        ql = rin[0][:, 0:o1]
        kl = rin[0][:, o1:o2]
        rq, rk = _rms(ql), _rms(kl)
        nq, nk = ql * rq, kl * rk
        dq, dk = rin[1][...], rin[2][...]
        outs[0][:, 0:o1] = _rms_bwd(dq * g[0:1, 0:o1], nq, rq).astype(BF16)
        outs[0][:, o1:o2] = _rms_bwd(dk * g[0:1, o1:o2], nk, rk).astype(BF16)
        dkr_pre = pltpu.roll(_rope_t(rin[3][...], rin[4][...], rin[5][...]), LANES - QK_NOPE, 1)
        outs[0][:, o2:] = jnp.where(_lane(dkr_pre.shape) < QK_ROPE, dkr_pre, 0.0).astype(BF16)
        reds[0][0:1, 0:o1] += _colsum(dq * nq)
        reds[0][0:1, o1:o2] += _colsum(dk * nk)

    return _rowwise(fn, [proj_mla, dqn, dkvn, dkr, rot_c, rot_s], [ng], [(MLA_W, BF16)], [(SUBLANES, MLA_W)],
                    name="mla_prep_bwd")


def _rope_bwd(dq, rot_c, rot_s):
    def fn(i, rin, pin, outs, reds):
        c, sn = rin[1][...] * Q_PRESCALE, rin[2][...] * Q_PRESCALE
        for h in range(N_HEADS):
            sl = slice(h * LANES, (h + 1) * LANES)
            outs[0][:, sl] = _rope_t(rin[0][:, sl], c, sn).astype(BF16)

    return _rowwise(fn, [dq, rot_c, rot_s], [], [(dq.shape[1], BF16)], [], name="rope_bwd")[0]


def _rope_fwd_t(q, rot_c, rot_s):
    s, c = q.shape
    tr = min(256, s)

    def body(q_ref, c_ref, s_ref, o_ref, ot_ref):
        cc, sn = c_ref[...] * Q_PRESCALE, s_ref[...] * Q_PRESCALE
        for h in range(N_HEADS):
            sl = slice(h * LANES, (h + 1) * LANES)
            rot = _rope(q_ref[:, sl], cc, sn)
            o_ref[:, sl] = rot.astype(BF16)
            ot_ref[sl, :] = jnp.transpose(rot).astype(BF16)

    return pl.pallas_call(
        body, name="rope_fwd", grid=(s // tr,),
        in_specs=[pl.BlockSpec((tr, c), lambda i: (i, 0)), pl.BlockSpec((tr, LANES), lambda i: (i, 0)),
                  pl.BlockSpec((tr, LANES), lambda i: (i, 0))],
        out_specs=[pl.BlockSpec((tr, c), lambda i: (i, 0)), pl.BlockSpec((c, tr), lambda i: (0, i))],
        out_shape=[jax.ShapeDtypeStruct((s, c), BF16), jax.ShapeDtypeStruct((c, s), BF16)],
        compiler_params=_cparams(("arbitrary",)),
    )(q, rot_c, rot_s)


def _merge_fwd(pr, pm, proj_g):
    d = pr.shape[1]

    def fn(i, rin, pin, outs, reds):
        outs[0][...] = (_sigmoid(rin[2][:, 0:d]) * rin[0][...] + _sigmoid(rin[2][:, d:]) * rin[1][...]).astype(BF16)

    return _rowwise(fn, [pr, pm, proj_g], [], [(d, BF16)], [], name="merge_fwd")[0]


def _merge_bwd(dmerged, pr, pm, proj_g):
    d = pr.shape[1]

    def fn(i, rin, pin, outs, reds):
        dm = rin[0][...]
        sr, sm = _sigmoid(rin[3][:, 0:d]), _sigmoid(rin[3][:, d:])
        outs[0][...] = (dm * sr).astype(BF16)
        outs[1][...] = (dm * sm).astype(BF16)
        outs[2][:, 0:d] = (dm * rin[1][...] * sr * (1.0 - sr)).astype(BF16)
        outs[2][:, d:] = (dm * rin[2][...] * sm * (1.0 - sm)).astype(BF16)

    return _rowwise(fn, [dmerged, pr, pm, proj_g], [], [(d, BF16), (d, BF16), (2 * d, BF16)], [], name="merge_bwd")


def _resid_norm_fwd(x, o, gmod):
    d = x.shape[1]

    def fn(i, rin, pin, outs, reds):
        p = pin[0][...]
        x1 = rin[0][...] + p[3:4] * rin[1][...]
        outs[0][...] = x1
        outs[1][...] = ((x1 * _rms(x1) * p[0:1]) * (1.0 + p[1:2]) + p[2:3]).astype(BF16)

    return _rowwise(fn, [x, o], [gmod], [(d, F32), (d, BF16)], [], name="resid_norm_fwd")


def _final_fwd_bwd(x1, dn, target, par):
    d = x1.shape[1]

    def fn(i, rin, pin, outs, reds):
        p = pin[0][...]
        dnv = rin[1][...]
        x2 = rin[0][...] + p[0:1] * dnv
        rstd = _rms(x2)
        n3 = x2 * rstd
        err = n3 * p[1:2] - rin[2][...]
        dy = err * (1.0 / d)
        dx2 = _rms_bwd(dy * p[1:2], n3, rstd)
        outs[0][...] = dx2
        outs[1][...] = (dx2 * p[0:1]).astype(BF16)
        reds[0][0:1, :] += _colsum(dy * n3)
        reds[0][1:2, :] += _colsum(dx2 * dnv)
        reds[0][2:3, :] += jnp.zeros((1, d), F32) + jnp.sum(err * err)

    return _rowwise(fn, [x1, dn, target], [par], [(d, F32), (d, BF16)], [(SUBLANES, d)], name="final_fwd_bwd")


def _norm2_bwd(x1, dh2, dx2, o, gmod):
    d = x1.shape[1]

    def fn(i, rin, pin, outs, reds):
        p = pin[0][...]
        x1v, dh = rin[0][...], rin[1][...]
        rstd = _rms(x1v)
        n2 = x1v * rstd
        dx1 = rin[2][...] + _rms_bwd(dh * (p[0:1] * (1.0 + p[1:2])), n2, rstd)
        outs[0][...] = dx1
        outs[1][...] = (dx1 * p[3:4]).astype(BF16)
        reds[0][0:1, :] += _colsum(dh * n2 * (1.0 + p[1:2]))
        reds[0][1:2, :] += _colsum(dh * n2 * p[0:1])
        reds[0][2:3, :] += _colsum(dh)
        reds[0][3:4, :] += _colsum(dx1 * rin[3][...])

    return _rowwise(fn, [x1, dh2, dx2, o], [gmod], [(d, F32), (d, BF16)], [(SUBLANES, d)], name="norm2_bwd")


def _norm1_bwd(x, dh_a, dh_b, dh_c, dx1, gmod):
    d = x.shape[1]

    def fn(i, rin, pin, outs, reds):
        p = pin[0][...]
        xv = rin[0][...]
        dh = rin[1][...] + rin[2][...] + rin[3][...]
        rstd = _rms(xv)
        n1 = xv * rstd
        outs[0][...] = rin[4][...] + _rms_bwd(dh * (p[0:1] * (1.0 + p[1:2])), n1, rstd)
        reds[0][0:1, :] += _colsum(dh * n1 * (1.0 + p[1:2]))
        reds[0][1:2, :] += _colsum(dh * n1 * p[0:1])
        reds[0][2:3, :] += _colsum(dh)

    return _rowwise(fn, [x, dh_a, dh_b, dh_c, dx1], [gmod], [(d, F32)], [(SUBLANES, d)], name="norm1_bwd")


RNN_CHUNK = 512


def _shift_down(ref, base, n, j):
    v = ref[pl.ds(base, n + SUBLANES), :]
    return v[SUBLANES:] if j == 0 else pltpu.roll(v, j, 0)[SUBLANES:]


def _shift_up(ref, base, n, j, top_pad):
    v = ref[pl.ds(base + top_pad, n + SUBLANES), :]
    return v[:n] if j == 0 else pltpu.roll(v, n + SUBLANES - j, 0)[:n]


SCAN_GROUP = 128


def _scan_sizes(s):
    sizes = [s]
    while sizes[-1] > SUBLANES:
        assert sizes[-1] % SUBLANES == 0, s
        sizes.append(sizes[-1] // SUBLANES)
    return sizes


def _scan_scratch(s):
    return [pltpu.VMEM((n + 2 * SUBLANES, LANES), F32) for n in _scan_sizes(s)[1:] for _ in range(2)]


def _linear_scan(a_ref, b_ref, out_ref, a_off, s, reverse, levels):
    sizes = _scan_sizes(s)
    lv = [(a_ref, b_ref, a_off, 0)] + [(levels[2 * i], levels[2 * i + 1], 0, SUBLANES) for i in range(len(sizes) - 1)]
    zero8 = jnp.zeros((SUBLANES, LANES), F32)
    for (ar, br, _, _), n in zip(lv[1:], sizes[1:]):
        br[0:SUBLANES, :] = zero8
        br[pl.ds(n + SUBLANES, SUBLANES), :] = zero8
    order = list(range(SUBLANES - 1, -1, -1)) if reverse else list(range(SUBLANES))

    for lvl in range(len(sizes) - 1):
        ar, br, aoff, off = lv[lvl]
        m = sizes[lvl + 1]
        g = min(m, SCAN_GROUP)
        for t0 in range(0, m, g):
            acc_a = acc_b = None
            for r in order:
                sa = pl.ds(off + SUBLANES * t0 + r + aoff, g, stride=SUBLANES)
                sb = pl.ds(off + SUBLANES * t0 + r, g, stride=SUBLANES)
                a, b = ar[sa, :], br[sb, :]
                if acc_a is None:
                    acc_a, acc_b = a, b
                else:
                    acc_b = a * acc_b + b
                    acc_a = a * acc_a
            lv[lvl + 1][0][pl.ds(SUBLANES + t0, g), :] = acc_a
            lv[lvl + 1][1][pl.ds(SUBLANES + t0, g), :] = acc_b

    ar, br, _, off = lv[-1]
    n = sizes[-1]
    a, b = ar[pl.ds(off, n), :], br[pl.ds(off, n), :]
    h, rows = jnp.zeros((1, LANES), F32), [None] * n
    for j in (range(n - 1, -1, -1) if reverse else range(n)):
        h = a[j:j + 1, :] * h + b[j:j + 1, :]
        rows[j] = h
    br[pl.ds(off, n), :] = jnp.concatenate(rows, axis=0)

    for lvl in range(len(sizes) - 2, -1, -1):
        ar, br, aoff, off = lv[lvl]
        m = sizes[lvl + 1]
        up = lv[lvl + 1][1]
        dst = out_ref if lvl == 0 else br
        g = min(m, SCAN_GROUP)
        for t0 in range(0, m, g):
            h = _shift_up(up, t0, g, 1, SUBLANES) if reverse else _shift_down(up, t0, g, 1)
            for r in order:
                sa = pl.ds(off + SUBLANES * t0 + r + aoff, g, stride=SUBLANES)
                sb = pl.ds(off + SUBLANES * t0 + r, g, stride=SUBLANES)
                h = ar[sa, :] * h + br[sb, :]
                dst[sb, :] = h


def _one_minus_exp(z):
    series = -z * (1.0 + z * (0.5 + z * (1.0 / 6.0 + z * (1.0 / 24.0 + z * (1.0 / 120.0 + z * (1.0 / 720.0))))))
    return jnp.where(z > -0.1, series, 1.0 - jnp.exp(z))


def _softplus(v):
    return jnp.maximum(v, 0.0) + jnp.log(1.0 + jnp.exp(-jnp.abs(v)))


def _rnn_gates(xc, w, wa, wx, sp):
    xb = xc.astype(BF16)
    ra = _sigmoid(jnp.dot(xb, wa, preferred_element_type=F32) + w[5:6])
    ix = _sigmoid(jnp.dot(xb, wx, preferred_element_type=F32) + w[6:7])
    la = (-LRU_C) * ra * sp
    a = jnp.exp(la)
    mult = jnp.sqrt(_one_minus_exp(2.0 * la))
    return ra, ix, a, mult


def _rnn_fwd(x_rnn, keep, rp, wa_bd, wx_bd):
    s, r = x_rnn.shape
    ts = min(RNN_CHUNK, s)

    def body(x_ref, keep_ref, rp_ref, wa_ref, wx_ref, xc_ref, ra_ref, ix_ref, hs_ref, xpad, a_s, b_s, *levels):
        xpad[0:SUBLANES, :] = jnp.zeros((SUBLANES, LANES), F32)
        xpad[SUBLANES:, :] = x_ref[...]
        w = rp_ref[...]
        sp = _softplus(-w[7:8])
        wa, wx = wa_ref[0], wx_ref[0]

        def chunk(c, carry):
            base = pl.multiple_of(c * ts, ts)
            xc = w[4:5] + w[3:4] * _shift_down(xpad, base, ts, 0)
            for j in range(1, 4):
                xc = xc + w[3 - j:4 - j] * _shift_down(xpad, base, ts, j)
            ra, ix, a, mult = _rnn_gates(xc, w, wa, wx, sp)
            kp = keep_ref[pl.ds(base, ts), :]
            xc_ref[pl.ds(base, ts), :] = xc
            ra_ref[pl.ds(base, ts), :] = ra
            ix_ref[pl.ds(base, ts), :] = ix
            a_s[pl.ds(base, ts), :] = a * kp
            b_s[pl.ds(base, ts), :] = jnp.where(kp > 0.0, mult, 1.0) * (ix * xc)
            return carry

        lax.fori_loop(0, s // ts, chunk, 0)

        _linear_scan(a_s, b_s, hs_ref, 0, s, False, levels)

    col = pl.BlockSpec((s, LANES), lambda g: (0, g))
    return pl.pallas_call(
        body, name="rnn_fwd", grid=(r // LANES,),
        in_specs=[col, pl.BlockSpec((s, 1), lambda g: (0, 0)), pl.BlockSpec((SUBLANES, LANES), lambda g: (0, g)),
                  pl.BlockSpec((1, LANES, LANES), lambda g: (g, 0, 0)), pl.BlockSpec((1, LANES, LANES), lambda g: (g, 0, 0))],
        out_specs=[col] * 4,
        out_shape=[jax.ShapeDtypeStruct((s, r), F32)] * 4,
        scratch_shapes=[pltpu.VMEM((s + SUBLANES, LANES), F32), pltpu.VMEM((s, LANES), F32), pltpu.VMEM((s, LANES), F32),
                        *_scan_scratch(s)],
        compiler_params=_cparams(("arbitrary",)),
    )(x_rnn, keep, rp, wa_bd, wx_bd)


def _rnn_bwd(x_rnn, xc, ra, ix, hs, dy, keep, rp, wa_bd, wx_bd):
    s, r = x_rnn.shape
    ts = min(RNN_CHUNK, s)

    def body(x_ref, xc_ref, ra_ref, ix_ref, hs_ref, dy_ref, keep_ref, rp_ref, wa_ref, wx_ref,
             dx_ref, dwa_ref, dwx_ref, red_ref, xpad, hpad, a_s, dh_s, dxc_s, *levels):
        zero8 = jnp.zeros((SUBLANES, LANES), F32)
        xpad[0:SUBLANES, :] = zero8
        xpad[SUBLANES:, :] = x_ref[...]
        hpad[0:SUBLANES, :] = zero8
        hpad[SUBLANES:, :] = hs_ref[...]
        a_s[s:, :] = zero8
        dxc_s[s:, :] = zero8
        w = rp_ref[...]
        sp = _softplus(-w[7:8])
        wa, wx = wa_ref[0], wx_ref[0]

        def decay(c, carry):
            base = pl.multiple_of(c * ts, ts)
            a = jnp.exp((-LRU_C) * ra_ref[pl.ds(base, ts), :] * sp)
            a_s[pl.ds(base, ts), :] = a * keep_ref[pl.ds(base, ts), :]
            return carry

        lax.fori_loop(0, s // ts, decay, 0)

        _linear_scan(a_s, dy_ref, dh_s, 1, s, True, levels)

        def gates(c, carry):
            dwa, dwx, d_ba, d_bx, d_sp, d_cb = carry
            base = pl.multiple_of(c * ts, ts)
            xcv = xc_ref[pl.ds(base, ts), :]
            rav = ra_ref[pl.ds(base, ts), :]
            ixv = ix_ref[pl.ds(base, ts), :]
            kp = keep_ref[pl.ds(base, ts), :]
            dh = dh_s[pl.ds(base, ts), :]
            h_prev = _shift_down(hpad, base, ts, 1)
            la = (-LRU_C) * rav * sp
            a = jnp.exp(la)
            mult = jnp.sqrt(_one_minus_exp(2.0 * la))
            mult_eff = jnp.where(kp > 0.0, mult, 1.0)
            d_a = dh * h_prev * kp
            d_mult = dh * (ixv * xcv) * kp
            d_ix = dh * mult_eff * xcv
            d_xc = dh * mult_eff * ixv
            d_la = d_a * a - d_mult * (a * a) / mult
            d_pa = d_la * ((-LRU_C) * sp) * rav * (1.0 - rav)
            d_px = d_ix * ixv * (1.0 - ixv)
            xb = xcv.astype(BF16)
            pab, pxb = d_pa.astype(BF16), d_px.astype(BF16)
            tn = (((0,), (0,)), ((), ()))
            nt_ = (((1,), (1,)), ((), ()))
            dwa = dwa + lax.dot_general(xb, pab, tn, preferred_element_type=F32)
            dwx = dwx + lax.dot_general(xb, pxb, tn, preferred_element_type=F32)
            d_xc = d_xc + lax.dot_general(pab, wa, nt_, preferred_element_type=F32)
            d_xc = d_xc + lax.dot_general(pxb, wx, nt_, preferred_element_type=F32)
            dxc_s[pl.ds(base, ts), :] = d_xc
            return (dwa, dwx, d_ba + _colsum(d_pa), d_bx + _colsum(d_px),
                    d_sp + _colsum(d_la * ((-LRU_C) * rav)), d_cb + _colsum(d_xc))

        z1 = jnp.zeros((1, LANES), F32)
        zw = jnp.zeros((LANES, LANES), F32)
        dwa, dwx, d_ba, d_bx, d_sp, d_cb = lax.fori_loop(0, s // ts, gates, (zw, zw, z1, z1, z1, z1))
        dwa_ref[0] = dwa
        dwx_ref[0] = dwx

        def conv(c, carry):
            base = pl.multiple_of(c * ts, ts)
            d_here = dxc_s[pl.ds(base, ts), :]
            dx = w[3:4] * d_here
            for j in range(1, 4):
                dx = dx + w[3 - j:4 - j] * _shift_up(dxc_s, base, ts, j, 0)
            dx_ref[pl.ds(base, ts), :] = dx.astype(BF16)
            return tuple(carry[k] + _colsum(d_here * _shift_down(xpad, base, ts, 3 - k)) for k in range(4))

        d_w = lax.fori_loop(0, s // ts, conv, (z1, z1, z1, z1))
        d_lru = d_sp * (-_sigmoid(-w[7:8]))
        red_ref[...] = jnp.concatenate(list(d_w) + [d_cb, d_ba, d_bx, d_lru], axis=0)

    col = pl.BlockSpec((s, LANES), lambda g: (0, g))
    par = pl.BlockSpec((SUBLANES, LANES), lambda g: (0, g))
    wsp = pl.BlockSpec((1, LANES, LANES), lambda g: (g, 0, 0))
    return pl.pallas_call(
        body, name="rnn_bwd", grid=(r // LANES,),
        in_specs=[col] * 6 + [pl.BlockSpec((s, 1), lambda g: (0, 0)), par, wsp, wsp],
        out_specs=[col, wsp, wsp, par],
        out_shape=[jax.ShapeDtypeStruct((s, r), BF16), jax.ShapeDtypeStruct((r // LANES, LANES, LANES), F32),
                   jax.ShapeDtypeStruct((r // LANES, LANES, LANES), F32), jax.ShapeDtypeStruct((SUBLANES, r), F32)],
        scratch_shapes=[pltpu.VMEM((s + SUBLANES, LANES), F32), pltpu.VMEM((s + SUBLANES, LANES), F32),
                        pltpu.VMEM((s + SUBLANES, LANES), F32), pltpu.VMEM((s, LANES), F32),
                        pltpu.VMEM((s + SUBLANES, LANES), F32), *_scan_scratch(s)],
        compiler_params=_cparams(("arbitrary",)),
    )(x_rnn, xc, ra, ix, hs, dy, keep, rp, wa_bd, wx_bd)


ATT_BLOCK = 512


LOG2E = 1.4426950408889634
LN2 = 0.6931471805599453
Q_PRESCALE = ATT_SCALE * LOG2E


def _att_scores(q, kvt, krt, diagonal):
    kt_eff = jnp.where(_row(kvt.shape) < QK_NOPE, kvt, krt)
    sc = jnp.dot(q, kt_eff, preferred_element_type=F32)
    if diagonal:
        sc = jnp.where(lax.broadcasted_iota(jnp.int32, sc.shape, 1) <= lax.broadcasted_iota(jnp.int32, sc.shape, 0), sc, -jnp.inf)
    return sc


def _att_fwd(q, kv, kvt, krt):
    s = q.shape[0]
    t = min(ATT_BLOCK, s)
    nb = s // t

    def body(q_ref, kv_ref, kvt_ref, krt_ref, y_ref, lse_ref, m_s, acc_s):
        i, j = pl.program_id(1), pl.program_id(2)

        @pl.when(j == 0)
        def _():
            m_s[...] = jnp.full(m_s.shape, -jnp.inf, F32)
            acc_s[...] = jnp.zeros(acc_s.shape, F32)

        def step(diagonal):
            krt_b = krt_ref[...]
            lane = _lane((t, LANES))
            groups = [slice(c * LANES, (c + 1) * LANES) for c in range(t // LANES)]
            heads = [slice(hh * LANES, (hh + 1) * LANES) for hh in range(2)]
            scs = [_att_scores(q_ref[:, sl], kvt_ref[sl, :], krt_b, diagonal) for sl in heads]
            stats = []
            for hh in range(2):
                m_prev = m_s[hh]
                m_blk = scs[hh][:, groups[0]]
                for g in groups[1:]:
                    m_blk = jnp.maximum(m_blk, scs[hh][:, g])
                stats.append((m_prev, jnp.maximum(m_prev, jnp.max(m_blk, axis=-1, keepdims=True))))
            for hh in range(2):
                m_prev, m_new = stats[hh]
                kvb = kv_ref[:, heads[hh]]
                ones_v = jnp.where(lane < QK_NOPE, jnp.ones_like(kvb), kvb)
                p = jnp.concatenate([jnp.exp2(scs[hh][:, g] - m_new).astype(BF16) for g in groups], axis=1)
                acc_s[hh] = jnp.exp2(m_prev - m_new) * acc_s[hh] + jnp.dot(p, ones_v, preferred_element_type=F32)
                m_s[hh] = m_new

        @pl.when(j < i)
        def _():
            step(False)

        @pl.when(j == i)
        def _():
            step(True)
            lane = _lane((t, LANES))
            a0, a1 = acc_s[0], acc_s[1]
            l0, l1 = a0[:, 0:1], a1[:, 0:1]
            y_ref[...] = jnp.where(lane < V_HEAD, pltpu.roll(a0 / l0, V_HEAD, 1), a1 / l1).astype(BF16)
            lse_ref[...] = jnp.where(lane < V_HEAD, m_s[0] + jnp.log(l0) * LOG2E, m_s[1] + jnp.log(l1) * LOG2E)

    return pl.pallas_call(
        body, name="att_fwd", grid=(N_HEADS // 2, nb, nb),
        in_specs=[pl.BlockSpec((t, 2 * LANES), lambda p, i, j: (i, p)),
                  pl.BlockSpec((t, 2 * LANES), lambda p, i, j: (jnp.minimum(j, i), p)),
                  pl.BlockSpec((2 * LANES, t), lambda p, i, j: (p, jnp.minimum(j, i))),
                  pl.BlockSpec((LANES, t), lambda p, i, j: (0, jnp.minimum(j, i)))],
        out_specs=[pl.BlockSpec((t, LANES), lambda p, i, j: (i, p))] * 2,
        out_shape=[jax.ShapeDtypeStruct((s, N_HEADS * V_HEAD), BF16), jax.ShapeDtypeStruct((s, N_HEADS * V_HEAD), F32)],
        scratch_shapes=[pltpu.VMEM((2, t, LANES), F32)] * 2,
        compiler_params=_cparams(("arbitrary", "arbitrary", "arbitrary")),
    )(q, kv, kvt, krt)


def _att_bwd(q, qt, kv, kvt, kr, krt, y, lse, dy, dyt):
    s = q.shape[0]
    t = min(ATT_BLOCK, s)
    nb = s // t

    def body(q_ref, qt_ref, kv_ref, kvt_ref, kr_ref, krt_ref, y_ref, lse_ref, dy_ref, dyt_ref,
             dq_ref, dkvt_ref, dkrt_ref, dkv_s):
        p_, j, i = pl.program_id(0), pl.program_id(1), pl.program_id(2)

        @pl.when((p_ == 0) & (j == 0) & (i == 0))
        def _():
            dkrt_ref[...] = jnp.zeros(dkrt_ref.shape, F32)

        @pl.when((j == 0) & (i == 0))
        def _():
            dq_ref[...] = jnp.zeros(dq_ref.shape, F32)

        @pl.when(i == 0)
        def _():
            dkv_s[...] = jnp.zeros(dkv_s.shape, F32)

        def step(diagonal):
            lane = _lane((t, LANES))
            row = _row((LANES, t))
            krb, krt_b = kr_ref[...], krt_ref[...]
            dyv = dy_ref[...]
            yv = y_ref[...].astype(F32)
            lsev = lse_ref[...]
            dyt_b = dyt_ref[...]
            rows = pl.ds(pl.multiple_of(i * t, t), t)
            cols = pl.ds(pl.multiple_of(j * t, t), t)
            zeros_t = jnp.zeros((V_HEAD, t), BF16)
            ones_w = jnp.ones((LANES, LANES), BF16)
            groups = [slice(c * LANES, (c + 1) * LANES) for c in range(t // LANES)]
            heads = [slice(hh * LANES, (hh + 1) * LANES) for hh in range(2)]
            scs, dps, stats = [], [], []
            for hh, sl in enumerate(heads):
                kvt_b = kvt_ref[sl, :]
                scs.append(_att_scores(q_ref[:, sl], kvt_b, krt_b, diagonal))
                mine = (lane < V_HEAD) if hh == 0 else (lane >= V_HEAD)
                lse_rep = jnp.where(mine, lsev, pltpu.roll(lsev, V_HEAD, 1))
                do_pad = jnp.where(lane >= V_HEAD, pltpu.roll(dyv, V_HEAD, 1) if hh == 0 else dyv, 0.0)
                o_pad = jnp.where(lane >= V_HEAD, pltpu.roll(yv, V_HEAD, 1) if hh == 0 else yv, 0.0)
                do_ln2 = do_pad * LN2
                prod = do_ln2 * o_pad
                head_part = prod.astype(BF16)
                rest_part = (prod - head_part.astype(F32)).astype(BF16)
                delta_rep = (jnp.dot(head_part, ones_w, preferred_element_type=F32)
                             + jnp.dot(rest_part, ones_w, preferred_element_type=F32))
                dps.append(jnp.dot(do_ln2.astype(BF16), kvt_b, preferred_element_type=F32))
                stats.append((lse_rep, delta_rep))
            dkr_acc = jnp.zeros((LANES, t), F32)
            for hh, sl in enumerate(heads):
                lse_rep, delta_rep = stats[hh]
                probs, dss = [], []
                for g in groups:
                    pg = jnp.exp2(scs[hh][:, g] - lse_rep)
                    probs.append(pg.astype(BF16))
                    dss.append((pg * (dps[hh][:, g] - delta_rep)).astype(BF16))
                prob, ds = jnp.concatenate(probs, axis=1), jnp.concatenate(dss, axis=1)
                dot_pad = jnp.concatenate([zeros_t, dyt_b[hh * V_HEAD:(hh + 1) * V_HEAD, :]], axis=0)
                k_eff = jnp.where(lane < QK_NOPE, kv_ref[:, sl], krb)
                dvt = jnp.dot(dot_pad, prob, preferred_element_type=F32)
                dq_ref[rows, sl] += jnp.dot(ds, k_eff, preferred_element_type=F32)
                dkt = jnp.dot(qt_ref[sl, :], ds, preferred_element_type=F32)
                dkv_s[hh] += dvt + jnp.where(row < QK_NOPE, dkt, 0.0)
                dkr_acc = dkr_acc + jnp.where(row >= QK_NOPE, dkt, 0.0)
            dkrt_ref[:, cols] += dkr_acc

        @pl.when(i > j)
        def _():
            step(False)

        @pl.when(i == j)
        def _():
            step(True)

        @pl.when(i == nb - 1)
        def _():
            dkvt_ref[0:LANES, :] = dkv_s[0].astype(BF16)
            dkvt_ref[LANES:, :] = dkv_s[1].astype(BF16)

    qi = lambda p, j, i: (jnp.maximum(i, j), p)
    qti = lambda p, j, i: (p, jnp.maximum(i, j))
    return pl.pallas_call(
        body, name="att_bwd", grid=(N_HEADS // 2, nb, nb),
        in_specs=[pl.BlockSpec((t, 2 * LANES), qi), pl.BlockSpec((2 * LANES, t), qti),
                  pl.BlockSpec((t, 2 * LANES), lambda p, j, i: (j, p)), pl.BlockSpec((2 * LANES, t), lambda p, j, i: (p, j)),
                  pl.BlockSpec((t, LANES), lambda p, j, i: (j, 0)), pl.BlockSpec((LANES, t), lambda p, j, i: (0, j)),
                  pl.BlockSpec((t, LANES), qi), pl.BlockSpec((t, LANES), qi), pl.BlockSpec((t, LANES), qi),
                  pl.BlockSpec((LANES, t), qti)],
        out_specs=[pl.BlockSpec((s, 2 * LANES), lambda p, j, i: (0, p)),
                   pl.BlockSpec((2 * LANES, t), lambda p, j, i: (p, j)),
                   pl.BlockSpec((LANES, s), lambda p, j, i: (0, 0))],
        out_shape=[jax.ShapeDtypeStruct((s, N_HEADS * LANES), F32), jax.ShapeDtypeStruct((N_HEADS * LANES, s), BF16),
                   jax.ShapeDtypeStruct((LANES, s), F32)],
        scratch_shapes=[pltpu.VMEM((2, LANES, t), F32)],
        compiler_params=_cparams(("arbitrary", "arbitrary", "arbitrary")),
    )(q, qt, kv, kvt, kr, krt, y, lse, dy, dyt)


FFN_COLS = 256


def _ffn_conv(pad_ref, w, base, n):
    u = w[3:4] + w[2:3] * _shift_down(pad_ref, base, n, 0)
    for j in range(1, 3):
        u = u + w[2 - j:3 - j] * _shift_down(pad_ref, base, n, j)
    return u


def _ffn_act_fwd(up, fp):
    s, f2 = up.shape
    f = f2 // 2
    tc = FFN_COLS
    ts = min(RNN_CHUNK, s)
    nfb = f // tc

    def body(ug_ref, uv_ref, wg_ref, wv_ref, act_ref, gpad, vpad):
        zero8 = jnp.zeros((SUBLANES, tc), F32)
        gpad[0:SUBLANES, :] = zero8
        vpad[0:SUBLANES, :] = zero8
        gpad[SUBLANES:, :] = ug_ref[...]
        vpad[SUBLANES:, :] = uv_ref[...]
        wg, wv = wg_ref[...], wv_ref[...]

        def chunk(c, carry):
            base = pl.multiple_of(c * ts, ts)
            g = _ffn_conv(gpad, wg, base, ts)
            v = _ffn_conv(vpad, wv, base, ts)
            act_ref[pl.ds(base, ts), :] = (g * _sigmoid(g) * v).astype(BF16)
            return carry

        lax.fori_loop(0, s // ts, chunk, 0)

    return pl.pallas_call(
        body, name="ffn_act_fwd", grid=(nfb,),
        in_specs=[pl.BlockSpec((s, tc), lambda b: (0, b)), pl.BlockSpec((s, tc), lambda b: (0, b + nfb)),
                  pl.BlockSpec((SUBLANES, tc), lambda b: (0, b)), pl.BlockSpec((SUBLANES, tc), lambda b: (0, b + nfb))],
        out_specs=pl.BlockSpec((s, tc), lambda b: (0, b)),
        out_shape=jax.ShapeDtypeStruct((s, f), BF16),
        scratch_shapes=[pltpu.VMEM((s + SUBLANES, tc), F32)] * 2,
        compiler_params=_cparams(("arbitrary",)),
    )(up, up, fp, fp)


def _ffn_act_bwd(up, dact, fp):
    s, f2 = up.shape
    f = f2 // 2
    tc = FFN_COLS
    ts = min(RNN_CHUNK, s)
    nfb = f // tc

    def body(ug_ref, uv_ref, da_ref, wg_ref, wv_ref, dup_ref, red_ref, gpad, vpad, dgs, dvs):
        half = pl.program_id(1)
        wg, wv = wg_ref[...], wv_ref[...]

        @pl.when(half == 0)
        def _():
            zero8 = jnp.zeros((SUBLANES, tc), F32)
            gpad[0:SUBLANES, :] = zero8
            vpad[0:SUBLANES, :] = zero8
            gpad[SUBLANES:, :] = ug_ref[...]
            vpad[SUBLANES:, :] = uv_ref[...]
            dgs[s:, :] = zero8
            dvs[s:, :] = zero8

            def act(c, carry):
                base = pl.multiple_of(c * ts, ts)
                g = _ffn_conv(gpad, wg, base, ts)
                v = _ffn_conv(vpad, wv, base, ts)
                da = da_ref[pl.ds(base, ts), :]
                sg = _sigmoid(g)
                dgs[pl.ds(base, ts), :] = da * v * (sg * (1.0 + g * (1.0 - sg)))
                dvs[pl.ds(base, ts), :] = da * (g * sg)
                return carry

            lax.fori_loop(0, s // ts, act, 0)

        def conv_t(src, pad, w, out_ref, red_ref):
            def chunk(c, carry):
                base = pl.multiple_of(c * ts, ts)
                d_here = src[pl.ds(base, ts), :]
                dx = w[2:3] * d_here
                for j in range(1, 3):
                    dx = dx + w[2 - j:3 - j] * _shift_up(src, base, ts, j, 0)
                out_ref[pl.ds(base, ts), :] = dx.astype(BF16)
                taps = tuple(carry[k] + _colsum(d_here * _shift_down(pad, base, ts, 2 - k)) for k in range(3))
                return taps + (carry[3] + _colsum(d_here),)

            z1 = jnp.zeros((1, tc), F32)
            red = lax.fori_loop(0, s // ts, chunk, (z1, z1, z1, z1))
            red_ref[...] = jnp.concatenate(list(red) + [jnp.zeros((4, tc), F32)], axis=0)

        @pl.when(half == 0)
        def _():
            conv_t(dgs, gpad, wg, dup_ref, red_ref)

        @pl.when(half == 1)
        def _():
            conv_t(dvs, vpad, wv, dup_ref, red_ref)

    gcol = pl.BlockSpec((s, tc), lambda b, h: (0, b))
    vcol = pl.BlockSpec((s, tc), lambda b, h: (0, b + nfb))
    gpar = pl.BlockSpec((SUBLANES, tc), lambda b, h: (0, b))
    vpar = pl.BlockSpec((SUBLANES, tc), lambda b, h: (0, b + nfb))
    return pl.pallas_call(
        body, name="ffn_act_bwd", grid=(nfb, 2),
        in_specs=[gcol, vcol, gcol, gpar, vpar],
        out_specs=[pl.BlockSpec((s, tc), lambda b, h: (0, b + h * nfb)),
                   pl.BlockSpec((SUBLANES, tc), lambda b, h: (0, b + h * nfb))],
        out_shape=[jax.ShapeDtypeStruct((s, f2), BF16), jax.ShapeDtypeStruct((SUBLANES, f2), F32)],
        scratch_shapes=[pltpu.VMEM((s + SUBLANES, tc), F32)] * 4,
        compiler_params=_cparams(("arbitrary", "arbitrary")),
    )(up, up, dact, fp, fp)


def _rows8(rows, width):
    rows = [r.reshape(1, width).astype(F32) for r in rows]
    return jnp.concatenate(rows + [jnp.zeros((SUBLANES - len(rows), width), F32)], axis=0)


def _block_diag(w):
    n, b, _ = w.shape
    w = w.reshape(n // 2, 2, b, b)
    z = jnp.zeros((n // 2, b, b), w.dtype)
    top = jnp.concatenate([w[:, 0], z], axis=2)
    bot = jnp.concatenate([z, w[:, 1]], axis=2)
    return jnp.concatenate([top, bot], axis=1)


def _block_diag_t(bd):
    n, b2, _ = bd.shape
    b = b2 // 2
    return jnp.stack([bd[:, :b, :b], bd[:, b:, b:]], axis=1).reshape(2 * n, b, b)


def _local_step(x, mod, positions, target, w_in, fetch, sm, emit):
    s, d = x.shape
    o_rnn, o_mla = D_RNN, D_RNN + Q_LORA + KV_LORA + QK_ROPE
    wts = {}
    w_in_rnn = w_in[:, :o_rnn]
    w_in_mla = jnp.concatenate([w_in[:, o_rnn:o_mla], jnp.zeros((d, MLA_W - (o_mla - o_rnn)), w_in.dtype)], axis=1)
    w_in_g = w_in[:, o_mla:]
    hd = QK_NOPE + QK_ROPE
    wa_bd = _block_diag(sm["w_gate_a"]).astype(BF16)
    wx_bd = _block_diag(sm["w_gate_x"]).astype(BF16)

    pos = positions.reshape(s)
    half = QK_ROPE // 2
    inv_freq = ROPE_THETA ** (-jnp.arange(half, dtype=F32) / half)
    ang = pos.astype(F32)[:, None] * inv_freq
    cos, sin = jnp.cos(ang), jnp.sin(ang)
    rot_c = jnp.concatenate([jnp.ones((s, QK_NOPE), F32), cos, cos, jnp.ones((s, LANES - hd), F32)], axis=1)
    rot_s = jnp.concatenate([jnp.zeros((s, QK_NOPE), F32), -sin, sin, jnp.zeros((s, LANES - hd), F32)], axis=1)
    keep = (pos != 0).astype(F32).reshape(s, 1)

    gmod1 = _rows8([sm["norm1_g"], mod[1], mod[0]], d)
    gmod2 = _rows8([sm["norm2_g"], mod[4], mod[3], mod[2]], d)
    rp = jnp.concatenate([sm["conv_w"].reshape(4, D_RNN), _rows8([sm["conv_b"], sm["b_gate_a"], sm["b_gate_x"], sm["lru_param"]], D_RNN)[:4]], axis=0)
    fp = _rows8([sm["ffn_conv_w"][0], sm["ffn_conv_w"][1], sm["ffn_conv_w"][2], sm["ffn_conv_b"]], 2 * D_FF)
    ng = _rows8([jnp.concatenate([sm["q_norm_g"].reshape(-1), sm["kv_norm_g"].reshape(-1), jnp.zeros((MLA_W - Q_LORA - KV_LORA,), F32)])], MLA_W)
    fpar = _rows8([mod[5], sm["final_g"]], d)

    h = _norm_mod_fwd(x, gmod1, "norm1_fwd")
    proj_rnn = _mm(h, w_in_rnn, name="mm_in_rnn")
    proj_mla = _mm(h, w_in_mla, name="mm_in_mla")
    proj_g = _mm(h, w_in_g, name="mm_in_g")
    xc, ra, ix, hs = _rnn_fwd(proj_rnn, keep, rp, wa_bd, wx_bd)
    qn, kvn, kr = _mla_prep_fwd(proj_mla, rot_c, rot_s, ng)
    wts.update(fetch(("w_uq", "w_ukv"), kr))
    w_uq_p = jnp.pad(wts["w_uq"].reshape(Q_LORA, N_HEADS, hd), ((0, 0), (0, 0), (0, LANES - hd))).reshape(Q_LORA, N_HEADS * LANES)
    w_ukv = wts["w_ukv"]
    q_rot, q_rot_t = _rope_fwd_t(_mm(qn, w_uq_p, name="mm_uq"), rot_c, rot_s)
    kv, kvt = _mm(kvn, w_ukv, out_dtype=BF16, also_t=BF16, name="mm_ukv")
    krt = jnp.transpose(kr)
    y_mla, lse = _att_fwd(q_rot, kv, kvt, krt)
    wts.update(fetch(("w_proj_rnn", "w_proj_mla", "w_out", "w_up", "w_down"), lse))
    pr = _mm(hs, wts["w_proj_rnn"], name="mm_proj_rnn")
    pm = _mm(y_mla, wts["w_proj_mla"], name="mm_proj_mla")
    merged = _merge_fwd(pr, pm, proj_g)
    o = _mm(merged, wts["w_out"], name="mm_out")
    x1, h2 = _resid_norm_fwd(x, o, gmod2)
    up = _mm(h2, wts["w_up"], name="mm_up")
    act = _ffn_act_fwd(up, fp)
    dn = _mm(act, wts["w_down"], name="mm_down")

    dx2, ddn, red_f = _final_fwd_bwd(x1, dn, target, fpar)
    dact = _mm(ddn, wts["w_down"], tb=True, name="mm_d_act")
    tok = emit("w_down", _mm(act, ddn, ta=True, out_dtype=BF16, name="mm_dw_down"))
    dup, red_ffn = _ffn_act_bwd(up, dact, fp + tok)
    dh2 = _mm(dup, wts["w_up"], tb=True, name="mm_d_h2")
    tok = tok + emit("w_up", _mm(h2, dup, ta=True, out_dtype=BF16, name="mm_dw_up"))
    dx1, do, red_2 = _norm2_bwd(x1, dh2, dx2, o, gmod2 + tok)
    dmerged = _mm(do, wts["w_out"], tb=True, name="mm_d_merged")
    tok = tok + emit("w_out", _mm(merged, do, ta=True, out_dtype=BF16, name="mm_dw_out"))
    dpr, dpm, dg = _merge_bwd(dmerged, pr, pm, proj_g)
    dy_rnn = _mm(dpr, wts["w_proj_rnn"], tb=True, name="mm_d_yrnn")
    tok = tok + emit("w_proj_rnn", _mm(hs, dpr, ta=True, out_dtype=BF16, name="mm_dw_proj_rnn"))
    dy_mla, dy_mla_t = _mm(dpm, wts["w_proj_mla"], tb=True, also_t=BF16, name="mm_d_ymla")
    tok = tok + emit("w_proj_mla", _mm(y_mla, dpm, ta=True, out_dtype=BF16, name="mm_dw_proj_mla"))
    dq_rot, dkvt, dkrt = _att_bwd(q_rot, q_rot_t, kv, kvt, kr, krt, y_mla, lse, dy_mla, dy_mla_t)
    dq = _rope_bwd(dq_rot, rot_c, rot_s)
    dqn = _mm(dq, w_uq_p, tb=True, name="mm_d_qn")
    dw_uq_p = _mm(qn, dq, ta=True, out_dtype=BF16, name="mm_dw_uq")
    tok = tok + emit("w_uq", dw_uq_p.reshape(Q_LORA, N_HEADS, LANES)[:, :, :hd].reshape(Q_LORA, N_HEADS * hd))
    dkvn = jnp.transpose(_mm(w_ukv, dkvt, name="mm_d_kvn"))
    tok = tok + emit("w_ukv", jnp.transpose(_mm(dkvt, kvn, out_dtype=BF16, name="mm_dw_ukv")))
    dproj_mla, red_m = _mla_prep_bwd(proj_mla, dqn, dkvn, jnp.transpose(dkrt), rot_c, rot_s, ng + tok)
    dx_rnn, dwa_bd, dwx_bd, red_r = _rnn_bwd(proj_rnn, xc, ra, ix, hs, dy_rnn, keep, rp + tok, wa_bd, wx_bd)
    dw_in = jnp.concatenate([
        _mm(h, dx_rnn, ta=True, out_dtype=BF16, name="mm_dw_in_rnn"),
        _mm(h, dproj_mla, ta=True, out_dtype=BF16, name="mm_dw_in_mla")[:, :o_mla - o_rnn],
        _mm(h, dg, ta=True, out_dtype=BF16, name="mm_dw_in_g")], axis=1)
    tok = tok + emit("w_in", dw_in)
    dh_a = _mm(dx_rnn, w_in_rnn, tb=True, name="mm_d_h_rnn")
    dh_b = _mm(dproj_mla, w_in_mla, tb=True, name="mm_d_h_mla")
    dh_c = _mm(dg, w_in_g, tb=True, name="mm_d_h_g")
    grad_x, red_1 = _norm1_bwd(x, dh_a, dh_b, dh_c, dx1, gmod1 + tok)

    gs = {
        "norm1_g": red_1[0], "conv_w": red_r[0:4], "conv_b": red_r[4], "w_gate_a": _block_diag_t(dwa_bd),
        "b_gate_a": red_r[5], "w_gate_x": _block_diag_t(dwx_bd), "b_gate_x": red_r[6], "lru_param": red_r[7],
        "q_norm_g": red_m[0, :Q_LORA], "kv_norm_g": red_m[0, Q_LORA:Q_LORA + KV_LORA], "norm2_g": red_2[0],
        "ffn_conv_w": red_ffn[0:3], "ffn_conv_b": red_ffn[3], "final_g": red_f[0],
    }
    dmod = jnp.stack([red_1[2], red_1[1], red_2[3], red_2[2], red_2[1], red_f[1]], axis=0)
    return red_f[2, 0], grad_x, gs, dmod


MESH_IDS = pl.DeviceIdType.MESH
HBM_SPEC = pl.BlockSpec(memory_space=pltpu.HBM)


def _my_slot():
    return 4 * lax.axis_index("x") + 2 * lax.axis_index("y") + lax.axis_index("c")


def _all_gather(arrs, name):
    n = len(arrs)

    def body(*refs):
        ins, outs = refs[:n], refs[n:2 * n]
        send_sems, recv_sems, local_sems = refs[2 * n:]
        x, y, c = lax.axis_index("x"), lax.axis_index("y"), lax.axis_index("c")
        me, sibling = (x, y, c), (x, y, 1 - c)
        chips = [(1 - x, y), (x, 1 - y), (1 - x, 1 - y)]

        def slot(dev):
            return 4 * dev[0] + 2 * dev[1] + dev[2]

        def copy(a, k, block, to, src=None):
            dst = outs[a].at[slot(block)]
            return pltpu.make_async_remote_copy(
                src_ref=dst if src is None else src, dst_ref=dst, send_sem=send_sems.at[a, k], recv_sem=recv_sems.at[a, k],
                device_id=to, device_id_type=MESH_IDS)

        mine = [pltpu.make_async_copy(ins[a], outs[a].at[slot(me)], local_sems.at[a]) for a in range(n)]
        for cp in mine:
            cp.start()
        first = []
        for a in range(n):
            first.append(copy(a, 0, me, sibling, src=ins[a]))
            first += [copy(a, 1 + j, me, (*chip, c), src=ins[a]) for j, chip in enumerate(chips)]
        for cp in first:
            cp.start()
        passed = []
        for j, chip in enumerate(chips):
            for a in range(n):
                copy(a, 1 + j, (*chip, c), me).wait_recv()
                fwd = copy(a, 4 + j, (*chip, c), sibling)
                fwd.start()
                passed.append(fwd)
        for a in range(n):
            copy(a, 0, sibling, me).wait_recv()
            for j, chip in enumerate(chips):
                copy(a, 4 + j, (*chip, 1 - c), me).wait_recv()
        for cp in first + passed:
            cp.wait_send()
        for cp in mine:
            cp.wait()

    return pl.pallas_call(
        body, name=name,
        in_specs=[HBM_SPEC] * n, out_specs=[HBM_SPEC] * n,
        out_shape=[jax.ShapeDtypeStruct((N_DEV,) + a.shape, a.dtype) for a in arrs],
        scratch_shapes=[pltpu.SemaphoreType.DMA((n, 7)), pltpu.SemaphoreType.DMA((n, 7)), pltpu.SemaphoreType.DMA((n,))],
    )(*arrs)


SEM_SPEC =pl.BlockSpec(memory_space=pltpu.SEMAPHORE)
DATAFLOW = pltpu.SideEffectType.DATAFLOW_SIDE_EFFECTING
FLIPS = [(dx, dy, dc) for dx in (0, 1) for dy in (0, 1) for dc in (0, 1)][1:]


def _peer(k):
    dx, dy, dc = FLIPS[k]
    peer = (lax.axis_index("x") ^ dx, lax.axis_index("y") ^ dy, lax.axis_index("c") ^ dc)
    return peer, 4 * peer[0] + 2 * peer[1] + peer[2]


def _gather_start(shards, after, name):
    n, nf = len(shards), len(FLIPS)

    def body(*refs):
        srcs, lands = refs[:n], refs[n:2 * n]
        send_sems, recv_sems = refs[2 * n + 1:3 * n + 1], refs[3 * n + 1:4 * n + 1]
        token = refs[-1]
        me = _my_slot()
        for a in range(n):
            for k in range(nf):
                peer, _ = _peer(k)
                pltpu.make_async_remote_copy(
                    src_ref=srcs[a], dst_ref=lands[a].at[me], send_sem=send_sems[a].at[k], recv_sem=recv_sems[a].at[k],
                    device_id=peer, device_id_type=MESH_IDS).start()
        token[...] = jnp.zeros(token.shape, F32)

    land_shapes = [(N_DEV,) + a.shape for a in shards]
    sems = [pltpu.SemaphoreType.DMA((nf,))] * n
    out = pl.pallas_call(
        body, name=name,
        out_shape=(*sems, *sems, *[pltpu.HBM(a.shape, a.dtype) for a in shards],
                   *[pltpu.HBM(shp, a.dtype) for shp, a in zip(land_shapes, shards)],
                   jax.ShapeDtypeStruct((SUBLANES, LANES), F32)),
        in_specs=[HBM_SPEC] * (2 * n) + [pl.BlockSpec(memory_space=pl.ANY)],
        out_specs=(*[SEM_SPEC] * (2 * n), *[HBM_SPEC] * (2 * n), pl.BlockSpec(memory_space=pltpu.VMEM)),
        input_output_aliases={i: 2 * n + i for i in range(2 * n)},
        compiler_params=pltpu.CompilerParams(has_side_effects=DATAFLOW),
    )(*[pltpu.with_memory_space_constraint(a, pltpu.HBM) for a in shards],
      *[pltpu.with_memory_space_constraint(lax.empty(shp, a.dtype), pltpu.HBM) for shp, a in zip(land_shapes, shards)],
      after)
    return [(out[a], out[n + a], out[2 * n + a], out[3 * n + a]) for a in range(n)], out[-1]


def _gather_wait(flights, after, name):
    n, nf = len(flights), len(FLIPS)

    def body(*refs):
        send_sems, recv_sems = refs[:n], refs[n:2 * n]
        srcs, lands = refs[2 * n:3 * n], refs[3 * n:4 * n]
        for a in range(n):
            for k in range(nf):
                peer, peer_slot = _peer(k)
                cp = pltpu.make_async_remote_copy(
                    src_ref=srcs[a], dst_ref=lands[a].at[peer_slot], send_sem=send_sems[a].at[k],
                    recv_sem=recv_sems[a].at[k], device_id=peer, device_id_type=MESH_IDS)
                cp.wait_send()
                cp.wait_recv()

    srcs, lands = [f[2] for f in flights], [f[3] for f in flights]
    out = pl.pallas_call(
        body, name=name,
        out_shape=(*[pltpu.HBM(a.shape, a.dtype) for a in srcs], *[pltpu.HBM(a.shape, a.dtype) for a in lands]),
        in_specs=[SEM_SPEC] * (2 * n) + [HBM_SPEC] * (2 * n) + [pl.BlockSpec(memory_space=pl.ANY)],
        out_specs=tuple([HBM_SPEC] * (2 * n)),
        input_output_aliases={2 * n + i: i for i in range(2 * n)},
        compiler_params=pltpu.CompilerParams(has_side_effects=DATAFLOW),
    )(*[f[0] for f in flights], *[f[1] for f in flights], *srcs, *lands, after)
    return list(out[n:])


def _scatter_start(chunks, name):
    def body(src_ref, land_ref, send_sems, recv_sems, src_thru, land_thru, token):
        me = _my_slot()
        for k in range(len(FLIPS)):
            peer, peer_slot = _peer(k)
            pltpu.make_async_remote_copy(
                src_ref=src_ref.at[peer_slot], dst_ref=land_ref.at[me], send_sem=send_sems.at[k], recv_sem=recv_sems.at[k],
                device_id=peer, device_id_type=MESH_IDS).start()
        token[...] = jnp.zeros(token.shape, F32)

    n = len(FLIPS)
    hbm = pltpu.HBM(chunks.shape, chunks.dtype)
    return pl.pallas_call(
        body, name=name,
        out_shape=(pltpu.SemaphoreType.DMA((n,)), pltpu.SemaphoreType.DMA((n,)), hbm, hbm,
                   jax.ShapeDtypeStruct((SUBLANES, LANES), F32)),
        in_specs=(HBM_SPEC, HBM_SPEC),
        out_specs=(SEM_SPEC, SEM_SPEC, HBM_SPEC, HBM_SPEC, pl.BlockSpec(memory_space=pltpu.VMEM)),
        input_output_aliases={0: 2, 1: 3},
        compiler_params=pltpu.CompilerParams(has_side_effects=DATAFLOW),
    )(pltpu.with_memory_space_constraint(chunks, pltpu.HBM),
      pltpu.with_memory_space_constraint(lax.empty(chunks.shape, chunks.dtype), pltpu.HBM))


def _scatter_wait(send_sems, recv_sems, src_thru, land_thru, after, name):
    def body(src_ref, land_ref, send_sems, recv_sems, after_ref, src_dead, got_ref):
        for k in range(len(FLIPS)):
            peer, peer_slot = _peer(k)
            cp = pltpu.make_async_remote_copy(
                src_ref=src_ref.at[peer_slot], dst_ref=land_ref.at[peer_slot], send_sem=send_sems.at[k],
                recv_sem=recv_sems.at[k], device_id=peer, device_id_type=MESH_IDS)
            cp.wait_send()
            cp.wait_recv()

    hbm = pltpu.HBM(src_thru.shape, src_thru.dtype)
    return pl.pallas_call(
        body, name=name, out_shape=(hbm, hbm),
        in_specs=(HBM_SPEC, HBM_SPEC, SEM_SPEC, SEM_SPEC, pl.BlockSpec(memory_space=pl.ANY)),
        out_specs=(HBM_SPEC, HBM_SPEC), input_output_aliases={0: 0, 1: 1},
        compiler_params=pltpu.CompilerParams(has_side_effects=DATAFLOW),
    )(src_thru, land_thru, send_sems, recv_sems, after)


def _sum_sources(parts, name):
    k, r, c = parts.shape
    tr = r if k * r * c <= 2 * 1024 * 1024 else _pick(r, (512, 256, 128, 64, 32, 16, 8))

    def body(p_ref, o_ref):
        acc = p_ref[0].astype(F32)
        for s in range(1, k):
            acc = acc + p_ref[s].astype(F32)
        o_ref[...] = acc

    return pl.pallas_call(
        body, name=name, grid=(r // tr,),
        in_specs=[pl.BlockSpec((k, tr, c), lambda i: (0, i, 0))],
        out_specs=pl.BlockSpec((tr, c), lambda i: (i, 0)),
        out_shape=jax.ShapeDtypeStruct((r, c), F32),
        compiler_params=_cparams(("arbitrary",)),
    )(parts)


def _adamw(parts, w, m, v, name, own=None):
    k, r, c = parts.shape
    tr = r if r * c <= 256 * 1024 else _pick(r, (256, 128, 64, 32, 16, 8))

    def body(*refs):
        p_ref, w_ref, m_ref, v_ref = refs[:4]
        g_ref, d_ref, nm_ref, nv_ref = refs[-4:]

        def part(s):
            if own is None:
                return p_ref[s].astype(F32)
            return jnp.where(_my_slot() == s, refs[4][...], p_ref[s]).astype(F32)

        g = part(0)
        for s in range(1, k):
            g = g + part(s)
        m_new = ADAM_B1 * m_ref[...] + (1.0 - ADAM_B1) * g
        v_new = ADAM_B2 * v_ref[...] + (1.0 - ADAM_B2) * jnp.square(g)
        m_hat = m_new / (1.0 - ADAM_B1 ** ADAM_STEP)
        v_hat = v_new / (1.0 - ADAM_B2 ** ADAM_STEP)
        g_ref[...] = g
        d_ref[...] = -ADAM_LR * (m_hat / (jnp.sqrt(v_hat) + ADAM_EPS) + ADAM_WD * w_ref[...])
        nm_ref[...] = m_new
        nv_ref[...] = v_new

    blk = pl.BlockSpec((tr, c), lambda i: (i, 0))
    return pl.pallas_call(
        body, name=name, grid=(r // tr,),
        in_specs=[pl.BlockSpec((k, tr, c), lambda i: (0, i, 0)), blk, blk, blk] + ([] if own is None else [blk]),
        out_specs=[blk] * 4,
        out_shape=[jax.ShapeDtypeStruct((r, c), F32)] * 4,
        compiler_params=_cparams(("arbitrary",)),
    )(parts, w, m, v, *([] if own is None else [own]))


def _silu(v):
    return v * _sigmoid(v)


def _ada_fwd(c_all, w, b):
    def body(c_ref, w_ref, b_ref, o_ref):
        ca = _silu(c_ref[...]).astype(BF16)
        o_ref[...] = jnp.dot(ca, w_ref[...].astype(BF16), preferred_element_type=F32) + b_ref[...]

    return pl.pallas_call(
        body, name="ada_fwd", out_shape=jax.ShapeDtypeStruct((c_all.shape[0], w.shape[1]), F32),
        compiler_params=_cparams(),
    )(c_all, w, b)


def _ada_bwd(c_all, dmod):
    def body(c_ref, d_ref, o_ref):
        ca = _silu(c_ref[...]).astype(BF16).astype(F32)
        dm = d_ref[...].astype(BF16).astype(F32)
        acc = jnp.zeros(o_ref.shape, F32)
        for bi in range(c_all.shape[0]):
            acc = acc + jnp.transpose(ca[bi:bi + 1, :]) * dm[bi:bi + 1, :]
        o_ref[...] = acc

    return pl.pallas_call(
        body, name="ada_bwd", out_shape=jax.ShapeDtypeStruct((c_all.shape[1], dmod.shape[1]), F32),
        compiler_params=_cparams(),
    )(c_all, dmod)


COL_SHARDED = ("w_in", "w_uq", "w_ukv", "w_up")
ROW_SHARDED = ("w_proj_rnn", "w_proj_mla", "w_out", "w_down")
REPLICATED = ("b_ada", "norm1_g", "conv_b", "w_gate_a", "b_gate_a", "w_gate_x", "b_gate_x", "lru_param", "q_norm_g",
              "kv_norm_g", "norm2_g", "ffn_conv_b", "final_g")
WEIGHTS = ("w_ada", "b_ada", "norm1_g", "w_in", "conv_w", "conv_b", "w_gate_a", "b_gate_a", "w_gate_x", "b_gate_x",
           "lru_param", "q_norm_g", "w_uq", "kv_norm_g", "w_ukv", "w_proj_rnn", "w_proj_mla", "w_out", "norm2_g", "w_up",
           "ffn_conv_w", "ffn_conv_b", "w_down", "final_g")
PACK_LANES = 128


def _pack(vecs):
    flat = jnp.concatenate([v.reshape(-1).astype(F32) for v in vecs])
    pad = (-flat.shape[0]) % (PACK_LANES * SUBLANES)
    return jnp.concatenate([flat, jnp.zeros((pad,), F32)]).reshape(-1, PACK_LANES)


def _unpack(packed, shapes):
    flat = packed.reshape(-1)
    out, off = [], 0
    for shp in shapes:
        size = math.prod(shp)
        out.append(flat[off:off + size].reshape(shp))
        off += size
    return out


def kernel(x, c, positions, w_ada, b_ada, norm1_g, w_in, conv_w, conv_b, w_gate_a, b_gate_a, w_gate_x, b_gate_x, lru_param, q_norm_g, w_uq, kv_norm_g, w_ukv, w_proj_rnn, w_proj_mla, w_out, norm2_g, w_up, ffn_conv_w, ffn_conv_b, w_down, final_g, loss_target, m_w_ada, m_b_ada, m_norm1_g, m_w_in, m_conv_w, m_conv_b, m_w_gate_a, m_b_gate_a, m_w_gate_x, m_b_gate_x, m_lru_param, m_q_norm_g, m_w_uq, m_kv_norm_g, m_w_ukv, m_w_proj_rnn, m_w_proj_mla, m_w_out, m_norm2_g, m_w_up, m_ffn_conv_w, m_ffn_conv_b, m_w_down, m_final_g, v_w_ada, v_b_ada, v_norm1_g, v_w_in, v_conv_w, v_conv_b, v_w_gate_a, v_b_gate_a, v_w_gate_x, v_b_gate_x, v_lru_param, v_q_norm_g, v_w_uq, v_kv_norm_g, v_w_ukv, v_w_proj_rnn, v_w_proj_mla, v_w_out, v_norm2_g, v_w_up, v_ffn_conv_w, v_ffn_conv_b, v_w_down, v_final_g):
    args = dict(locals())
    w = {n: args[n] for n in WEIGHTS}
    m = {n: args["m_" + n] for n in WEIGHTS}
    v = {n: args["v_" + n] for n in WEIGHTS}
    s, d = x.shape[1], x.shape[2]
    me = _my_slot()
    def two_d(a):
        assert a.ndim == 3 and a.shape[0] == 1, a.shape
        return a[0]

    big = COL_SHARDED + ROW_SHARDED
    shard = {n: two_d(w[n]).astype(BF16) for n in big}

    def whole(n, g):
        k, r, cc = g.shape
        return jnp.transpose(g, (1, 0, 2)).reshape(r, k * cc) if n in COL_SHARDED else g.reshape(k * r, cc)

    first = _all_gather([shard["w_in"], c, two_d(conv_w), two_d(ffn_conv_w)], "gather_first")
    c_all = first[1].reshape(N_DEV, d)
    conv_w_all = jnp.transpose(first[2], (1, 0, 2)).reshape(conv_w.shape[1], -1)
    ffn_conv_w_all = jnp.transpose(first[3], (1, 0, 2)).reshape(ffn_conv_w.shape[1], -1)

    ada_cols = w_ada.shape[2]
    b_cols = lax.dynamic_slice(b_ada, (0, me * ada_cols), (1, ada_cols))
    mod_cols = _ada_fwd(c_all, w_ada[0], b_cols)
    mod_all, = _all_gather([mod_cols], "gather_mod")

    later = ("w_uq", "w_ukv", "w_proj_rnn", "w_proj_mla", "w_out", "w_up", "w_down")
    flights, started = _gather_start([shard[n] for n in later], mod_all, "gather_start")
    flight = dict(zip(later, flights))

    def fetch(names, after):
        lands = _gather_wait([flight[n] for n in names], after, "gather_wait_" + names[0])
        return {n: whole(n, lax.dynamic_update_index_in_dim(g, shard[n], me, 0)) for n, g in zip(names, lands)}

    mod = lax.dynamic_index_in_dim(mod_all, me, axis=1, keepdims=False).reshape(6, d) + started[0, 0]

    sm = {n: w[n][0] for n in REPLICATED if n not in ("b_ada", "final_g")}
    sm["final_g"] = final_g
    sm["conv_w"] = conv_w_all
    sm["ffn_conv_w"] = ffn_conv_w_all
    in_flight = {}

    def emit(n, g):
        if n in COL_SHARDED:
            r, cc = g.shape
            chunks = jnp.transpose(g.reshape(r, N_DEV, cc // N_DEV), (1, 0, 2))
        else:
            chunks = g.reshape(N_DEV, g.shape[0] // N_DEV, g.shape[1])
        *in_flight[n], token = _scatter_start(chunks, "scatter_start_" + n)
        return token[0, 0]

    sq, grad_x, gs, dmod = _local_step(x[0], mod, positions[0], loss_target[0], whole("w_in", first[0]), fetch, sm, emit)

    small_names = [n for n in REPLICATED if n != "b_ada"] + ["conv_w", "ffn_conv_w"]
    small_shapes = [gs[n].shape for n in small_names] + [(6 * d,), (1,)]
    partial = _pack([gs[n] for n in small_names] + [dmod, sq.reshape(1)])
    (small_flight,), small_started = _gather_start([partial], grad_x, "gather_small_start")

    grads, deltas, new_m, new_v = {}, {}, {}, {}

    def update(n, parts, own=None):
        shp = w[n].shape
        g, dl, nm, nv = _adamw(parts, two_d(w[n]), two_d(m[n]), two_d(v[n]), "adamw_" + n, own)
        grads[n], deltas[n], new_m[n], new_v[n] = g.reshape(shp), dl.reshape(shp), nm.reshape(shp), nv.reshape(shp)

    for n in big:
        chunks, landed = _scatter_wait(*in_flight[n], small_started, "scatter_wait_" + n)
        update(n, landed, lax.dynamic_index_in_dim(chunks, me, axis=0, keepdims=False))

    landed, = _gather_wait([small_flight], new_v[big[-1]], "gather_small_wait")
    partial_all = lax.dynamic_update_index_in_dim(landed, partial, me, 0)
    summed = _unpack(_sum_sources(partial_all, "sum_small"), small_shapes)
    g_small = dict(zip(small_names, summed[:len(small_names)]))
    g_small["b_ada"] = summed[len(small_names)]
    loss = 0.5 * summed[-1][0] / d
    n_before = sum(math.prod(t) for t in small_shapes[:len(small_names)])
    dmod_all = partial_all.reshape(N_DEV, -1)[:, n_before:n_before + 6 * d]
    dmod_cols = lax.dynamic_slice(dmod_all, (0, me * ada_cols), (N_DEV, ada_cols))

    update("w_ada", _ada_bwd(c_all, dmod_cols)[None])

    for n in ("conv_w", "ffn_conv_w"):
        cols = w[n].shape[2]
        update(n, lax.dynamic_slice(g_small[n], (0, me * cols), (g_small[n].shape[0], cols))[None])

    rep_shapes = [w[n].shape for n in REPLICATED]
    g_rep, d_rep, m_rep, v_rep = _adamw(
        _pack([g_small[n] for n in REPLICATED])[None], _pack([w[n] for n in REPLICATED]), _pack([m[n] for n in REPLICATED]),
        _pack([v[n] for n in REPLICATED]), "adamw_replicated")
    for dst, packed in ((grads, g_rep), (deltas, d_rep), (new_m, m_rep), (new_v, v_rep)):
        dst.update(zip(REPLICATED, _unpack(packed, rep_shapes)))

    return (loss, grad_x[None], *[grads[n] for n in WEIGHTS], *[deltas[n] for n in WEIGHTS],
            *[new_m[n] for n in WEIGHTS], *[new_v[n] for n in WEIGHTS])
```

```python
import functools
import math

import jax
import jax.numpy as jnp
from jax import lax
from jax.experimental import pallas as pl
from jax.experimental.pallas import tpu as pltpu

F32 = jnp.float32
BF16 = jnp.bfloat16

N_DEV = 8
LANES = 128
SUBLANES = 8
VMEM_LIMIT = 56 * 1024 * 1024

D_RNN = 1280
Q_LORA = 384
KV_LORA = 256
QK_NOPE = 64
QK_ROPE = 32
V_HEAD = 64
N_HEADS = 16
D_FF = 2816
ROPE_THETA = 10000.0
LRU_C = 8.0
EPS = 1e-6
MLA_W = 768
ATT_SCALE = 1.0 / math.sqrt(QK_NOPE + QK_ROPE)

ADAM_LR, ADAM_B1, ADAM_B2, ADAM_EPS, ADAM_WD, ADAM_STEP = 0.001, 0.9, 0.999, 1e-08, 0.01, 10


def _cparams(sem=None):
    return pltpu.CompilerParams(dimension_semantics=sem, vmem_limit_bytes=VMEM_LIMIT)


def _pick(n, prefs):
    for p in prefs:
        if n % p == 0:
            return p
    return n


def _sigmoid(v):
    return 0.5 * jnp.tanh(0.5 * v) + 0.5


def _lane(shape):
    return lax.broadcasted_iota(jnp.int32, shape, len(shape) - 1)


def _row(shape):
    return lax.broadcasted_iota(jnp.int32, shape, len(shape) - 2)


MM_BLOCK_BYTES = 36 * 1024 * 1024


def _divisors(n):
    return [t for t in range(n, 0, -LANES) if n % t == 0] if n % LANES == 0 else [n]


HBM_BYTES_PER_US = 3.0e6
MXU_FLOPS_PER_US = 8.0e8
GRID_STEP_US = 0.35


def _mm_tiles(m, n, k, a_bytes, b_bytes, o_bytes):
    best = None
    for tm in [t for t in _divisors(m) if t <= 1024]:
        for tn in [t for t in _divisors(n) if t <= 2048]:
            for tk in _divisors(k):
                nk = k // tk
                need = 2 * (tm * tk * a_bytes + tk * tn * b_bytes + tm * tn * o_bytes) + (tm * tn * 4 if nk > 1 else 0)
                if need > MM_BLOCK_BYTES:
                    continue
                gi, gj = m // tm, n // tn
                for rows_outer in (True, False):
                    if nk > 1:
                        a_reads, b_reads = gj, gi
                    elif rows_outer:
                        a_reads, b_reads = 1, (gi if gj > 1 else 1)
                    else:
                        a_reads, b_reads = (gj if gi > 1 else 1), 1
                    traffic = m * k * a_bytes * a_reads + k * n * b_bytes * b_reads + m * n * (o_bytes + (8 * nk if nk > 1 else 0))
                    cost = max(traffic / HBM_BYTES_PER_US, 2.0 * m * n * k / MXU_FLOPS_PER_US) + gi * gj * nk * GRID_STEP_US
                    if best is None or cost < best[0]:
                        best = (cost, tm, tn, tk, rows_outer)
                break
    if best is None:
        raise ValueError((m, n, k))
    return best[1:]


def _mm(a, b, *, ta=False, tb=False, out_dtype=F32, also_t=None, name):
    (k_a, m) = a.shape if ta else a.shape[::-1]
    (n, k_b) = b.shape if tb else b.shape[::-1]
    assert k_a == k_b, (a.shape, b.shape, ta, tb)
    k = k_a
    tm, tn, tk, rows_outer = _mm_tiles(m, n, k, a.dtype.itemsize, b.dtype.itemsize, jnp.dtype(out_dtype).itemsize)
    nk = k // tk
    dims = (((0 if ta else 1,), (1 if tb else 0,)), ((), ()))
    n_out = 1 if also_t is None else 2

    def body(a_ref, b_ref, *rest):
        outs, acc = rest[:n_out], rest[n_out:]
        part = lax.dot_general(a_ref[...].astype(BF16), b_ref[...].astype(BF16), dims, preferred_element_type=F32)

        def write(val):
            outs[0][...] = val.astype(out_dtype)
            if also_t is not None:
                outs[1][...] = jnp.transpose(val).astype(also_t)

        if nk == 1:
            write(part)
            return
        acc_ref, = acc
        kk = pl.program_id(2)

        @pl.when(kk == 0)
        def _():
            acc_ref[...] = part

        @pl.when(kk > 0)
        def _():
            acc_ref[...] += part

        @pl.when(kk == nk - 1)
        def _():
            write(acc_ref[...])

    def ij(f):
        return (lambda i, j, kk: f(i, j, kk)) if rows_outer else (lambda j, i, kk: f(i, j, kk))

    a_spec = pl.BlockSpec((tk, tm), ij(lambda i, j, kk: (kk, i))) if ta else pl.BlockSpec((tm, tk), ij(lambda i, j, kk: (i, kk)))
    b_spec = pl.BlockSpec((tn, tk), ij(lambda i, j, kk: (j, kk))) if tb else pl.BlockSpec((tk, tn), ij(lambda i, j, kk: (kk, j)))
    out_specs = [pl.BlockSpec((tm, tn), ij(lambda i, j, kk: (i, j)))]
    out_shape = [jax.ShapeDtypeStruct((m, n), out_dtype)]
    if also_t is not None:
        out_specs.append(pl.BlockSpec((tn, tm), ij(lambda i, j, kk: (j, i))))
        out_shape.append(jax.ShapeDtypeStruct((n, m), also_t))
    res = pl.pallas_call(
        body, name=name,
        grid=(m // tm, n // tn, nk) if rows_outer else (n // tn, m // tm, nk),
        in_specs=[a_spec, b_spec], out_specs=out_specs, out_shape=out_shape,
        scratch_shapes=[] if nk == 1 else [pltpu.VMEM((tm, tn), F32)],
        compiler_params=_cparams(("arbitrary", "arbitrary", "arbitrary")),
    )(a, b)
    return res[0] if also_t is None else res


def _rowwise(fn, row_ins, par_ins, out_defs, red_defs, *, name, tr=256):
    s = row_ins[0].shape[0]
    tr = min(tr, s)
    nr, npar, no = len(row_ins), len(par_ins), len(out_defs)

    def body(*refs):
        rin, pin = refs[:nr], refs[nr:nr + npar]
        outs, reds = refs[nr + npar:nr + npar + no], refs[nr + npar + no:]
        i = pl.program_id(0)

        @pl.when(i == 0)
        def _():
            for r in reds:
                r[...] = jnp.zeros_like(r)

        fn(i, rin, pin, outs, reds)

    in_specs = [pl.BlockSpec((tr, a.shape[1]), lambda i: (i, 0)) for a in row_ins]
    in_specs += [pl.BlockSpec(a.shape, lambda i, nd=a.ndim: (0,) * nd) for a in par_ins]
    out_specs = [pl.BlockSpec((tr, c), lambda i: (i, 0)) for c, _ in out_defs]
    out_specs += [pl.BlockSpec(shp, lambda i: (0, 0)) for shp in red_defs]
    out_shape = [jax.ShapeDtypeStruct((s, c), dt) for c, dt in out_defs]
    out_shape += [jax.ShapeDtypeStruct(shp, F32) for shp in red_defs]
    return pl.pallas_call(
        body, name=name, grid=(s // tr,), in_specs=in_specs, out_specs=out_specs, out_shape=out_shape,
        compiler_params=_cparams(("arbitrary",)),
    )(*row_ins, *par_ins)


def _rms(v):
    return lax.rsqrt(jnp.mean(v * v, axis=-1, keepdims=True) + EPS)


def _colsum(v):
    return jnp.sum(v, axis=0, keepdims=True)


def _rms_bwd(dn, n, rstd):
    return rstd * (dn - n * jnp.mean(dn * n, axis=-1, keepdims=True))


def _norm_mod_fwd(x, gmod, name):
    def fn(i, rin, pin, outs, reds):
        xv = rin[0][...]
        p = pin[0][...]
        n = xv * _rms(xv)
        outs[0][...] = ((n * p[0:1]) * (1.0 + p[1:2]) + p[2:3]).astype(BF16)

    return _rowwise(fn, [x], [gmod], [(x.shape[1], BF16)], [], name=name)[0]


def _rope(v, rot_c, rot_s):
    half = QK_ROPE // 2
    swapped = jnp.where(_lane(v.shape) < QK_NOPE + half, pltpu.roll(v, LANES - half, 1), pltpu.roll(v, half, 1))
    return v * rot_c + swapped * rot_s


def _rope_t(dv, rot_c, rot_s):
    half = QK_ROPE // 2
    ds = dv * rot_s
    lane = _lane(dv.shape)
    swapped = jnp.where(lane < QK_NOPE + half, pltpu.roll(ds, LANES - half, 1), pltpu.roll(ds, half, 1))
    in_rope = (lane >= QK_NOPE) & (lane < QK_NOPE + QK_ROPE)
    return dv * rot_c + jnp.where(in_rope, swapped, 0.0)


def _mla_prep_fwd(proj_mla, rot_c, rot_s, ng):
    o1, o2 = Q_LORA, Q_LORA + KV_LORA

    def fn(i, rin, pin, outs, reds):
        g = pin[0][...]
        ql = rin[0][:, 0:o1]
        kl = rin[0][:, o1:o2]
        outs[0][...] = (ql * _rms(ql) * g[0:1, 0:o1]).astype(BF16)
        outs[1][...] = (kl * _rms(kl) * g[0:1, o1:o2]).astype(BF16)
        kr = pltpu.roll(rin[0][:, o2:o2 + LANES], QK_NOPE, 1)
        outs[2][...] = _rope(kr, rin[1][...], rin[2][...]).astype(BF16)

    return _rowwise(fn, [proj_mla, rot_c, rot_s], [ng], [(Q_LORA, BF16), (KV_LORA, BF16), (LANES, BF16)], [],
                    name="mla_prep_fwd")


def _mla_prep_bwd(proj_mla, dqn, dkvn, dkr, rot_c, rot_s, ng):
    o1, o2 = Q_LORA, Q_LORA + KV_LORA

    def fn(i, rin, pin, outs, reds):
        g = pin[0][...]
        ql = rin[0][:, 0:o1]
        kl = rin[0][:, o1:o2]
        rq, rk = _rms(ql), _rms(kl)
        nq, nk = ql * rq, kl * rk
        dq, dk = rin[1][...], rin[2][...]
        outs[0][:, 0:o1] = _rms_bwd(dq * g[0:1, 0:o1], nq, rq).astype(BF16)
        outs[0][:, o1:o2] = _rms_bwd(dk * g[0:1, o1:o2], nk, rk).astype(BF16)
        dkr_pre = pltpu.roll(_rope_t(rin[3][...], rin[4][...], rin[5][...]), LANES - QK_NOPE, 1)
        outs[0][:, o2:] = jnp.where(_lane(dkr_pre.shape) < QK_ROPE, dkr_pre, 0.0).astype(BF16)
        reds[0][0:1, 0:o1] += _colsum(dq * nq)
        reds[0][0:1, o1:o2] += _colsum(dk * nk)

    return _rowwise(fn, [proj_mla, dqn, dkvn, dkr, rot_c, rot_s], [ng], [(MLA_W, BF16)], [(SUBLANES, MLA_W)],
                    name="mla_prep_bwd")


def _rope_bwd(dq, rot_c, rot_s):
    def fn(i, rin, pin, outs, reds):
        c, sn = rin[1][...] * Q_PRESCALE, rin[2][...] * Q_PRESCALE
        for h in range(N_HEADS):
            sl = slice(h * LANES, (h + 1) * LANES)
            outs[0][:, sl] = _rope_t(rin[0][:, sl], c, sn).astype(BF16)

    return _rowwise(fn, [dq, rot_c, rot_s], [], [(dq.shape[1], BF16)], [], name="rope_bwd")[0]


def _rope_fwd_t(q, rot_c, rot_s):
    s, c = q.shape
    tr = min(256, s)

    def body(q_ref, c_ref, s_ref, o_ref, ot_ref):
        cc, sn = c_ref[...] * Q_PRESCALE, s_ref[...] * Q_PRESCALE
        for h in range(N_HEADS):
            sl = slice(h * LANES, (h + 1) * LANES)
            rot = _rope(q_ref[:, sl], cc, sn)
            o_ref[:, sl] = rot.astype(BF16)
            ot_ref[sl, :] = jnp.transpose(rot).astype(BF16)

    return pl.pallas_call(
        body, name="rope_fwd", grid=(s // tr,),
        in_specs=[pl.BlockSpec((tr, c), lambda i: (i, 0)), pl.BlockSpec((tr, LANES), lambda i: (i, 0)),
                  pl.BlockSpec((tr, LANES), lambda i: (i, 0))],
        out_specs=[pl.BlockSpec((tr, c), lambda i: (i, 0)), pl.BlockSpec((c, tr), lambda i: (0, i))],
        out_shape=[jax.ShapeDtypeStruct((s, c), BF16), jax.ShapeDtypeStruct((c, s), BF16)],
        compiler_params=_cparams(("arbitrary",)),
    )(q, rot_c, rot_s)


def _merge_fwd(pr, pm, proj_g):
    d = pr.shape[1]

    def fn(i, rin, pin, outs, reds):
        outs[0][...] = (_sigmoid(rin[2][:, 0:d]) * rin[0][...] + _sigmoid(rin[2][:, d:]) * rin[1][...]).astype(BF16)

    return _rowwise(fn, [pr, pm, proj_g], [], [(d, BF16)], [], name="merge_fwd")[0]


def _merge_bwd(dmerged, pr, pm, proj_g):
    d = pr.shape[1]

    def fn(i, rin, pin, outs, reds):
        dm = rin[0][...]
        sr, sm = _sigmoid(rin[3][:, 0:d]), _sigmoid(rin[3][:, d:])
        outs[0][...] = (dm * sr).astype(BF16)
        outs[1][...] = (dm * sm).astype(BF16)
        outs[2][:, 0:d] = (dm * rin[1][...] * sr * (1.0 - sr)).astype(BF16)
        outs[2][:, d:] = (dm * rin[2][...] * sm * (1.0 - sm)).astype(BF16)

    return _rowwise(fn, [dmerged, pr, pm, proj_g], [], [(d, BF16), (d, BF16), (2 * d, BF16)], [], name="merge_bwd")


def _resid_norm_fwd(x, o, gmod):
    d = x.shape[1]

    def fn(i, rin, pin, outs, reds):
        p = pin[0][...]
        x1 = rin[0][...] + p[3:4] * rin[1][...]
        outs[0][...] = x1
        outs[1][...] = ((x1 * _rms(x1) * p[0:1]) * (1.0 + p[1:2]) + p[2:3]).astype(BF16)

    return _rowwise(fn, [x, o], [gmod], [(d, F32), (d, BF16)], [], name="resid_norm_fwd")


def _final_fwd_bwd(x1, dn, target, par):
    d = x1.shape[1]

    def fn(i, rin, pin, outs, reds):
        p = pin[0][...]
        dnv = rin[1][...]
        x2 = rin[0][...] + p[0:1] * dnv
        rstd = _rms(x2)
        n3 = x2 * rstd
        err = n3 * p[1:2] - rin[2][...]
        dy = err * (1.0 / d)
        dx2 = _rms_bwd(dy * p[1:2], n3, rstd)
        outs[0][...] = dx2
        outs[1][...] = (dx2 * p[0:1]).astype(BF16)
        reds[0][0:1, :] += _colsum(dy * n3)
        reds[0][1:2, :] += _colsum(dx2 * dnv)
        reds[0][2:3, :] += jnp.zeros((1, d), F32) + jnp.sum(err * err)

    return _rowwise(fn, [x1, dn, target], [par], [(d, F32), (d, BF16)], [(SUBLANES, d)], name="final_fwd_bwd")


def _norm2_bwd(x1, dh2, dx2, o, gmod):
    d = x1.shape[1]

    def fn(i, rin, pin, outs, reds):
        p = pin[0][...]
        x1v, dh = rin[0][...], rin[1][...]
        rstd = _rms(x1v)
        n2 = x1v * rstd
        dx1 = rin[2][...] + _rms_bwd(dh * (p[0:1] * (1.0 + p[1:2])), n2, rstd)
        outs[0][...] = dx1
        outs[1][...] = (dx1 * p[3:4]).astype(BF16)
        reds[0][0:1, :] += _colsum(dh * n2 * (1.0 + p[1:2]))
        reds[0][1:2, :] += _colsum(dh * n2 * p[0:1])
        reds[0][2:3, :] += _colsum(dh)
        reds[0][3:4, :] += _colsum(dx1 * rin[3][...])

    return _rowwise(fn, [x1, dh2, dx2, o], [gmod], [(d, F32), (d, BF16)], [(SUBLANES, d)], name="norm2_bwd")


def _norm1_bwd(x, dh_a, dh_b, dh_c, dx1, gmod):
    d = x.shape[1]

    def fn(i, rin, pin, outs, reds):
        p = pin[0][...]
        xv = rin[0][...]
        dh = rin[1][...] + rin[2][...] + rin[3][...]
        rstd = _rms(xv)
        n1 = xv * rstd
        outs[0][...] = rin[4][...] + _rms_bwd(dh * (p[0:1] * (1.0 + p[1:2])), n1, rstd)
        reds[0][0:1, :] += _colsum(dh * n1 * (1.0 + p[1:2]))
        reds[0][1:2, :] += _colsum(dh * n1 * p[0:1])
        reds[0][2:3, :] += _colsum(dh)

    return _rowwise(fn, [x, dh_a, dh_b, dh_c, dx1], [gmod], [(d, F32)], [(SUBLANES, d)], name="norm1_bwd")


RNN_CHUNK = 512


def _shift_down(ref, base, n, j):
    v = ref[pl.ds(base, n + SUBLANES), :]
    return v[SUBLANES:] if j == 0 else pltpu.roll(v, j, 0)[SUBLANES:]


def _shift_up(ref, base, n, j, top_pad):
    v = ref[pl.ds(base + top_pad, n + SUBLANES), :]
    return v[:n] if j == 0 else pltpu.roll(v, n + SUBLANES - j, 0)[:n]


SCAN_GROUP = 128


def _scan_sizes(s):
    sizes = [s]
    while sizes[-1] > SUBLANES:
        assert sizes[-1] % SUBLANES == 0, s
        sizes.append(sizes[-1] // SUBLANES)
    return sizes


def _scan_scratch(s):
    return [pltpu.VMEM((n + 2 * SUBLANES, LANES), F32) for n in _scan_sizes(s)[1:] for _ in range(2)]


def _linear_scan(a_ref, b_ref, out_ref, a_off, s, reverse, levels):
    sizes = _scan_sizes(s)
    lv = [(a_ref, b_ref, a_off, 0)] + [(levels[2 * i], levels[2 * i + 1], 0, SUBLANES) for i in range(len(sizes) - 1)]
    zero8 = jnp.zeros((SUBLANES, LANES), F32)
    for (ar, br, _, _), n in zip(lv[1:], sizes[1:]):
        br[0:SUBLANES, :] = zero8
        br[pl.ds(n + SUBLANES, SUBLANES), :] = zero8
    order = list(range(SUBLANES - 1, -1, -1)) if reverse else list(range(SUBLANES))

    for lvl in range(len(sizes) - 1):
        ar, br, aoff, off = lv[lvl]
        m = sizes[lvl + 1]
        g = min(m, SCAN_GROUP)
        for t0 in range(0, m, g):
            acc_a = acc_b = None
            for r in order:
                sa = pl.ds(off + SUBLANES * t0 + r + aoff, g, stride=SUBLANES)
                sb = pl.ds(off + SUBLANES * t0 + r, g, stride=SUBLANES)
                a, b = ar[sa, :], br[sb, :]
                if acc_a is None:
                    acc_a, acc_b = a, b
                else:
                    acc_b = a * acc_b + b
                    acc_a = a * acc_a
            lv[lvl + 1][0][pl.ds(SUBLANES + t0, g), :] = acc_a
            lv[lvl + 1][1][pl.ds(SUBLANES + t0, g), :] = acc_b

    ar, br, _, off = lv[-1]
    n = sizes[-1]
    a, b = ar[pl.ds(off, n), :], br[pl.ds(off, n), :]
    h, rows = jnp.zeros((1, LANES), F32), [None] * n
    for j in (range(n - 1, -1, -1) if reverse else range(n)):
        h = a[j:j + 1, :] * h + b[j:j + 1, :]
        rows[j] = h
    br[pl.ds(off, n), :] = jnp.concatenate(rows, axis=0)

    for lvl in range(len(sizes) - 2, -1, -1):
        ar, br, aoff, off = lv[lvl]
        m = sizes[lvl + 1]
        up = lv[lvl + 1][1]
        dst = out_ref if lvl == 0 else br
        g = min(m, SCAN_GROUP)
        for t0 in range(0, m, g):
            h = _shift_up(up, t0, g, 1, SUBLANES) if reverse else _shift_down(up, t0, g, 1)
            for r in order:
                sa = pl.ds(off + SUBLANES * t0 + r + aoff, g, stride=SUBLANES)
                sb = pl.ds(off + SUBLANES * t0 + r, g, stride=SUBLANES)
                h = ar[sa, :] * h + br[sb, :]
                dst[sb, :] = h


def _one_minus_exp(z):
    series = -z * (1.0 + z * (0.5 + z * (1.0 / 6.0 + z * (1.0 / 24.0 + z * (1.0 / 120.0 + z * (1.0 / 720.0))))))
    return jnp.where(z > -0.1, series, 1.0 - jnp.exp(z))


def _softplus(v):
    return jnp.maximum(v, 0.0) + jnp.log(1.0 + jnp.exp(-jnp.abs(v)))


def _rnn_gates(xc, w, wa, wx, sp):
    xb = xc.astype(BF16)
    ra = _sigmoid(jnp.dot(xb, wa, preferred_element_type=F32) + w[5:6])
    ix = _sigmoid(jnp.dot(xb, wx, preferred_element_type=F32) + w[6:7])
    la = (-LRU_C) * ra * sp
    a = jnp.exp(la)
    mult = jnp.sqrt(_one_minus_exp(2.0 * la))
    return ra, ix, a, mult


def _rnn_fwd(x_rnn, keep, rp, wa_bd, wx_bd):
    s, r = x_rnn.shape
    ts = min(RNN_CHUNK, s)

    def body(x_ref, keep_ref, rp_ref, wa_ref, wx_ref, xc_ref, ra_ref, ix_ref, hs_ref, xpad, a_s, b_s, *levels):
        xpad[0:SUBLANES, :] = jnp.zeros((SUBLANES, LANES), F32)
        xpad[SUBLANES:, :] = x_ref[...]
        w = rp_ref[...]
        sp = _softplus(-w[7:8])
        wa, wx = wa_ref[0], wx_ref[0]

        def chunk(c, carry):
            base = pl.multiple_of(c * ts, ts)
            xc = w[4:5] + w[3:4] * _shift_down(xpad, base, ts, 0)
            for j in range(1, 4):
                xc = xc + w[3 - j:4 - j] * _shift_down(xpad, base, ts, j)
            ra, ix, a, mult = _rnn_gates(xc, w, wa, wx, sp)
            kp = keep_ref[pl.ds(base, ts), :]
            xc_ref[pl.ds(base, ts), :] = xc
            ra_ref[pl.ds(base, ts), :] = ra
            ix_ref[pl.ds(base, ts), :] = ix
            a_s[pl.ds(base, ts), :] = a * kp
            b_s[pl.ds(base, ts), :] = jnp.where(kp > 0.0, mult, 1.0) * (ix * xc)
            return carry

        lax.fori_loop(0, s // ts, chunk, 0)

        _linear_scan(a_s, b_s, hs_ref, 0, s, False, levels)

    col = pl.BlockSpec((s, LANES), lambda g: (0, g))
    return pl.pallas_call(
        body, name="rnn_fwd", grid=(r // LANES,),
        in_specs=[col, pl.BlockSpec((s, 1), lambda g: (0, 0)), pl.BlockSpec((SUBLANES, LANES), lambda g: (0, g)),
                  pl.BlockSpec((1, LANES, LANES), lambda g: (g, 0, 0)), pl.BlockSpec((1, LANES, LANES), lambda g: (g, 0, 0))],
        out_specs=[col] * 4,
        out_shape=[jax.ShapeDtypeStruct((s, r), F32)] * 4,
        scratch_shapes=[pltpu.VMEM((s + SUBLANES, LANES), F32), pltpu.VMEM((s, LANES), F32), pltpu.VMEM((s, LANES), F32),
                        *_scan_scratch(s)],
        compiler_params=_cparams(("arbitrary",)),
    )(x_rnn, keep, rp, wa_bd, wx_bd)


def _rnn_bwd(x_rnn, xc, ra, ix, hs, dy, keep, rp, wa_bd, wx_bd):
    s, r = x_rnn.shape
    ts = min(RNN_CHUNK, s)

    def body(x_ref, xc_ref, ra_ref, ix_ref, hs_ref, dy_ref, keep_ref, rp_ref, wa_ref, wx_ref,
             dx_ref, dwa_ref, dwx_ref, red_ref, xpad, hpad, a_s, dh_s, dxc_s, *levels):
        zero8 = jnp.zeros((SUBLANES, LANES), F32)
        xpad[0:SUBLANES, :] = zero8
        xpad[SUBLANES:, :] = x_ref[...]
        hpad[0:SUBLANES, :] = zero8
        hpad[SUBLANES:, :] = hs_ref[...]
        a_s[s:, :] = zero8
        dxc_s[s:, :] = zero8
        w = rp_ref[...]
        sp = _softplus(-w[7:8])
        wa, wx = wa_ref[0], wx_ref[0]

        def decay(c, carry):
            base = pl.multiple_of(c * ts, ts)
            a = jnp.exp((-LRU_C) * ra_ref[pl.ds(base, ts), :] * sp)
            a_s[pl.ds(base, ts), :] = a * keep_ref[pl.ds(base, ts), :]
            return carry

        lax.fori_loop(0, s // ts, decay, 0)

        _linear_scan(a_s, dy_ref, dh_s, 1, s, True, levels)

        def gates(c, carry):
            dwa, dwx, d_ba, d_bx, d_sp, d_cb = carry
            base = pl.multiple_of(c * ts, ts)
            xcv = xc_ref[pl.ds(base, ts), :]
            rav = ra_ref[pl.ds(base, ts), :]
            ixv = ix_ref[pl.ds(base, ts), :]
            kp = keep_ref[pl.ds(base, ts), :]
            dh = dh_s[pl.ds(base, ts), :]
            h_prev = _shift_down(hpad, base, ts, 1)
            la = (-LRU_C) * rav * sp
            a = jnp.exp(la)
            mult = jnp.sqrt(_one_minus_exp(2.0 * la))
            mult_eff = jnp.where(kp > 0.0, mult, 1.0)
            d_a = dh * h_prev * kp
            d_mult = dh * (ixv * xcv) * kp
            d_ix = dh * mult_eff * xcv
            d_xc = dh * mult_eff * ixv
            d_la = d_a * a - d_mult * (a * a) / mult
            d_pa = d_la * ((-LRU_C) * sp) * rav * (1.0 - rav)
            d_px = d_ix * ixv * (1.0 - ixv)
            xb = xcv.astype(BF16)
            pab, pxb = d_pa.astype(BF16), d_px.astype(BF16)
            tn = (((0,), (0,)), ((), ()))
            nt_ = (((1,), (1,)), ((), ()))
            dwa = dwa + lax.dot_general(xb, pab, tn, preferred_element_type=F32)
            dwx = dwx + lax.dot_general(xb, pxb, tn, preferred_element_type=F32)
            d_xc = d_xc + lax.dot_general(pab, wa, nt_, preferred_element_type=F32)
            d_xc = d_xc + lax.dot_general(pxb, wx, nt_, preferred_element_type=F32)
            dxc_s[pl.ds(base, ts), :] = d_xc
            return (dwa, dwx, d_ba + _colsum(d_pa), d_bx + _colsum(d_px),
                    d_sp + _colsum(d_la * ((-LRU_C) * rav)), d_cb + _colsum(d_xc))

        z1 = jnp.zeros((1, LANES), F32)
        zw = jnp.zeros((LANES, LANES), F32)
        dwa, dwx, d_ba, d_bx, d_sp, d_cb = lax.fori_loop(0, s // ts, gates, (zw, zw, z1, z1, z1, z1))
        dwa_ref[0] = dwa
        dwx_ref[0] = dwx

        def conv(c, carry):
            base = pl.multiple_of(c * ts, ts)
            d_here = dxc_s[pl.ds(base, ts), :]
            dx = w[3:4] * d_here
            for j in range(1, 4):
                dx = dx + w[3 - j:4 - j] * _shift_up(dxc_s, base, ts, j, 0)
            dx_ref[pl.ds(base, ts), :] = dx.astype(BF16)
            return tuple(carry[k] + _colsum(d_here * _shift_down(xpad, base, ts, 3 - k)) for k in range(4))

        d_w = lax.fori_loop(0, s // ts, conv, (z1, z1, z1, z1))
        d_lru = d_sp * (-_sigmoid(-w[7:8]))
        red_ref[...] = jnp.concatenate(list(d_w) + [d_cb, d_ba, d_bx, d_lru], axis=0)

    col = pl.BlockSpec((s, LANES), lambda g: (0, g))
    par = pl.BlockSpec((SUBLANES, LANES), lambda g: (0, g))
    wsp = pl.BlockSpec((1, LANES, LANES), lambda g: (g, 0, 0))
    return pl.pallas_call(
        body, name="rnn_bwd", grid=(r // LANES,),
        in_specs=[col] * 6 + [pl.BlockSpec((s, 1), lambda g: (0, 0)), par, wsp, wsp],
        out_specs=[col, wsp, wsp, par],
        out_shape=[jax.ShapeDtypeStruct((s, r), BF16), jax.ShapeDtypeStruct((r // LANES, LANES, LANES), F32),
                   jax.ShapeDtypeStruct((r // LANES, LANES, LANES), F32), jax.ShapeDtypeStruct((SUBLANES, r), F32)],
        scratch_shapes=[pltpu.VMEM((s + SUBLANES, LANES), F32), pltpu.VMEM((s + SUBLANES, LANES), F32),
                        pltpu.VMEM((s + SUBLANES, LANES), F32), pltpu.VMEM((s, LANES), F32),
                        pltpu.VMEM((s + SUBLANES, LANES), F32), *_scan_scratch(s)],
        compiler_params=_cparams(("arbitrary",)),
    )(x_rnn, xc, ra, ix, hs, dy, keep, rp, wa_bd, wx_bd)


ATT_BLOCK = 512


LOG2E = 1.4426950408889634
LN2 = 0.6931471805599453
Q_PRESCALE = ATT_SCALE * LOG2E


def _att_scores(q, kvt, krt, diagonal):
    kt_eff = jnp.where(_row(kvt.shape) < QK_NOPE, kvt, krt)
    sc = jnp.dot(q, kt_eff, preferred_element_type=F32)
    if diagonal:
        sc = jnp.where(lax.broadcasted_iota(jnp.int32, sc.shape, 1) <= lax.broadcasted_iota(jnp.int32, sc.shape, 0), sc, -jnp.inf)
    return sc


def _att_fwd(q, kv, kvt, krt):
    s = q.shape[0]
    t = min(ATT_BLOCK, s)
    nb = s // t

    def body(q_ref, kv_ref, kvt_ref, krt_ref, y_ref, lse_ref, m_s, acc_s):
        i, j = pl.program_id(1), pl.program_id(2)

        @pl.when(j == 0)
        def _():
            m_s[...] = jnp.full(m_s.shape, -jnp.inf, F32)
            acc_s[...] = jnp.zeros(acc_s.shape, F32)

        def step(diagonal):
            krt_b = krt_ref[...]
            lane = _lane((t, LANES))
            groups = [slice(c * LANES, (c + 1) * LANES) for c in range(t // LANES)]
            heads = [slice(hh * LANES, (hh + 1) * LANES) for hh in range(2)]
            scs = [_att_scores(q_ref[:, sl], kvt_ref[sl, :], krt_b, diagonal) for sl in heads]
            stats = []
            for hh in range(2):
                m_prev = m_s[hh]
                m_blk = scs[hh][:, groups[0]]
                for g in groups[1:]:
                    m_blk = jnp.maximum(m_blk, scs[hh][:, g])
                stats.append((m_prev, jnp.maximum(m_prev, jnp.max(m_blk, axis=-1, keepdims=True))))
            for hh in range(2):
                m_prev, m_new = stats[hh]
                kvb = kv_ref[:, heads[hh]]
                ones_v = jnp.where(lane < QK_NOPE, jnp.ones_like(kvb), kvb)
                p = jnp.concatenate([jnp.exp2(scs[hh][:, g] - m_new).astype(BF16) for g in groups], axis=1)
                acc_s[hh] = jnp.exp2(m_prev - m_new) * acc_s[hh] + jnp.dot(p, ones_v, preferred_element_type=F32)
                m_s[hh] = m_new

        @pl.when(j < i)
        def _():
            step(False)

        @pl.when(j == i)
        def _():
            step(True)
            lane = _lane((t, LANES))
            a0, a1 = acc_s[0], acc_s[1]
            l0, l1 = a0[:, 0:1], a1[:, 0:1]
            y_ref[...] = jnp.where(lane < V_HEAD, pltpu.roll(a0 / l0, V_HEAD, 1), a1 / l1).astype(BF16)
            lse_ref[...] = jnp.where(lane < V_HEAD, m_s[0] + jnp.log(l0) * LOG2E, m_s[1] + jnp.log(l1) * LOG2E)

    return pl.pallas_call(
        body, name="att_fwd", grid=(N_HEADS // 2, nb, nb),
        in_specs=[pl.BlockSpec((t, 2 * LANES), lambda p, i, j: (i, p)),
                  pl.BlockSpec((t, 2 * LANES), lambda p, i, j: (jnp.minimum(j, i), p)),
                  pl.BlockSpec((2 * LANES, t), lambda p, i, j: (p, jnp.minimum(j, i))),
                  pl.BlockSpec((LANES, t), lambda p, i, j: (0, jnp.minimum(j, i)))],
        out_specs=[pl.BlockSpec((t, LANES), lambda p, i, j: (i, p))] * 2,
        out_shape=[jax.ShapeDtypeStruct((s, N_HEADS * V_HEAD), BF16), jax.ShapeDtypeStruct((s, N_HEADS * V_HEAD), F32)],
        scratch_shapes=[pltpu.VMEM((2, t, LANES), F32)] * 2,
        compiler_params=_cparams(("arbitrary", "arbitrary", "arbitrary")),
    )(q, kv, kvt, krt)


def _att_bwd(q, qt, kv, kvt, kr, krt, y, lse, dy, dyt):
    s = q.shape[0]
    t = min(ATT_BLOCK, s)
    nb = s // t

    def body(q_ref, qt_ref, kv_ref, kvt_ref, kr_ref, krt_ref, y_ref, lse_ref, dy_ref, dyt_ref,
             dq_ref, dkvt_ref, dkrt_ref, dkv_s):
        p_, j, i = pl.program_id(0), pl.program_id(1), pl.program_id(2)

        @pl.when((p_ == 0) & (j == 0) & (i == 0))
        def _():
            dkrt_ref[...] = jnp.zeros(dkrt_ref.shape, F32)

        @pl.when((j == 0) & (i == 0))
        def _():
            dq_ref[...] = jnp.zeros(dq_ref.shape, F32)

        @pl.when(i == 0)
        def _():
            dkv_s[...] = jnp.zeros(dkv_s.shape, F32)

        def step(diagonal):
            lane = _lane((t, LANES))
            row = _row((LANES, t))
            krb, krt_b = kr_ref[...], krt_ref[...]
            dyv = dy_ref[...]
            yv = y_ref[...].astype(F32)
            lsev = lse_ref[...]
            dyt_b = dyt_ref[...]
            rows = pl.ds(pl.multiple_of(i * t, t), t)
            cols = pl.ds(pl.multiple_of(j * t, t), t)
            zeros_t = jnp.zeros((V_HEAD, t), BF16)
            ones_w = jnp.ones((LANES, LANES), BF16)
            groups = [slice(c * LANES, (c + 1) * LANES) for c in range(t // LANES)]
            heads = [slice(hh * LANES, (hh + 1) * LANES) for hh in range(2)]
            scs, dps, stats = [], [], []
            for hh, sl in enumerate(heads):
                kvt_b = kvt_ref[sl, :]
                scs.append(_att_scores(q_ref[:, sl], kvt_b, krt_b, diagonal))
                mine = (lane < V_HEAD) if hh == 0 else (lane >= V_HEAD)
                lse_rep = jnp.where(mine, lsev, pltpu.roll(lsev, V_HEAD, 1))
                do_pad = jnp.where(lane >= V_HEAD, pltpu.roll(dyv, V_HEAD, 1) if hh == 0 else dyv, 0.0)
                o_pad = jnp.where(lane >= V_HEAD, pltpu.roll(yv, V_HEAD, 1) if hh == 0 else yv, 0.0)
                do_ln2 = do_pad * LN2
                prod = do_ln2 * o_pad
                head_part = prod.astype(BF16)
                rest_part = (prod - head_part.astype(F32)).astype(BF16)
                delta_rep = (jnp.dot(head_part, ones_w, preferred_element_type=F32)
                             + jnp.dot(rest_part, ones_w, preferred_element_type=F32))
                dps.append(jnp.dot(do_ln2.astype(BF16), kvt_b, preferred_element_type=F32))
                stats.append((lse_rep, delta_rep))
            dkr_acc = jnp.zeros((LANES, t), F32)
            for hh, sl in enumerate(heads):
                lse_rep, delta_rep = stats[hh]
                probs, dss = [], []
                for g in groups:
                    pg = jnp.exp2(scs[hh][:, g] - lse_rep)
                    probs.append(pg.astype(BF16))
                    dss.append((pg * (dps[hh][:, g] - delta_rep)).astype(BF16))
                prob, ds = jnp.concatenate(probs, axis=1), jnp.concatenate(dss, axis=1)
                dot_pad = jnp.concatenate([zeros_t, dyt_b[hh * V_HEAD:(hh + 1) * V_HEAD, :]], axis=0)
                k_eff = jnp.where(lane < QK_NOPE, kv_ref[:, sl], krb)
                dvt = jnp.dot(dot_pad, prob, preferred_element_type=F32)
                dq_ref[rows, sl] += jnp.dot(ds, k_eff, preferred_element_type=F32)
                dkt = jnp.dot(qt_ref[sl, :], ds, preferred_element_type=F32)
                dkv_s[hh] += dvt + jnp.where(row < QK_NOPE, dkt, 0.0)
                dkr_acc = dkr_acc + jnp.where(row >= QK_NOPE, dkt, 0.0)
            dkrt_ref[:, cols] += dkr_acc

        @pl.when(i > j)
        def _():
            step(False)

        @pl.when(i == j)
        def _():
            step(True)

        @pl.when(i == nb - 1)
        def _():
            dkvt_ref[0:LANES, :] = dkv_s[0].astype(BF16)
            dkvt_ref[LANES:, :] = dkv_s[1].astype(BF16)

    qi = lambda p, j, i: (jnp.maximum(i, j), p)
    qti = lambda p, j, i: (p, jnp.maximum(i, j))
    return pl.pallas_call(
        body, name="att_bwd", grid=(N_HEADS // 2, nb, nb),
        in_specs=[pl.BlockSpec((t, 2 * LANES), qi), pl.BlockSpec((2 * LANES, t), qti),
                  pl.BlockSpec((t, 2 * LANES), lambda p, j, i: (j, p)), pl.BlockSpec((2 * LANES, t), lambda p, j, i: (p, j)),
                  pl.BlockSpec((t, LANES), lambda p, j, i: (j, 0)), pl.BlockSpec((LANES, t), lambda p, j, i: (0, j)),
                  pl.BlockSpec((t, LANES), qi), pl.BlockSpec((t, LANES), qi), pl.BlockSpec((t, LANES), qi),
                  pl.BlockSpec((LANES, t), qti)],
        out_specs=[pl.BlockSpec((s, 2 * LANES), lambda p, j, i: (0, p)),
                   pl.BlockSpec((2 * LANES, t), lambda p, j, i: (p, j)),
                   pl.BlockSpec((LANES, s), lambda p, j, i: (0, 0))],
        out_shape=[jax.ShapeDtypeStruct((s, N_HEADS * LANES), F32), jax.ShapeDtypeStruct((N_HEADS * LANES, s), BF16),
                   jax.ShapeDtypeStruct((LANES, s), F32)],
        scratch_shapes=[pltpu.VMEM((2, LANES, t), F32)],
        compiler_params=_cparams(("arbitrary", "arbitrary", "arbitrary")),
    )(q, qt, kv, kvt, kr, krt, y, lse, dy, dyt)


FFN_COLS = 256


def _ffn_conv(pad_ref, w, base, n):
    u = w[3:4] + w[2:3] * _shift_down(pad_ref, base, n, 0)
    for j in range(1, 3):
        u = u + w[2 - j:3 - j] * _shift_down(pad_ref, base, n, j)
    return u


def _ffn_act_fwd(up, fp):
    s, f2 = up.shape
    f = f2 // 2
    tc = FFN_COLS
    ts = min(RNN_CHUNK, s)
    nfb = f // tc

    def body(ug_ref, uv_ref, wg_ref, wv_ref, act_ref, gpad, vpad):
        zero8 = jnp.zeros((SUBLANES, tc), F32)
        gpad[0:SUBLANES, :] = zero8
        vpad[0:SUBLANES, :] = zero8
        gpad[SUBLANES:, :] = ug_ref[...]
        vpad[SUBLANES:, :] = uv_ref[...]
        wg, wv = wg_ref[...], wv_ref[...]

        def chunk(c, carry):
            base = pl.multiple_of(c * ts, ts)
            g = _ffn_conv(gpad, wg, base, ts)
            v = _ffn_conv(vpad, wv, base, ts)
            act_ref[pl.ds(base, ts), :] = (g * _sigmoid(g) * v).astype(BF16)
            return carry

        lax.fori_loop(0, s // ts, chunk, 0)

    return pl.pallas_call(
        body, name="ffn_act_fwd", grid=(nfb,),
        in_specs=[pl.BlockSpec((s, tc), lambda b: (0, b)), pl.BlockSpec((s, tc), lambda b: (0, b + nfb)),
                  pl.BlockSpec((SUBLANES, tc), lambda b: (0, b)), pl.BlockSpec((SUBLANES, tc), lambda b: (0, b + nfb))],
        out_specs=pl.BlockSpec((s, tc), lambda b: (0, b)),
        out_shape=jax.ShapeDtypeStruct((s, f), BF16),
        scratch_shapes=[pltpu.VMEM((s + SUBLANES, tc), F32)] * 2,
        compiler_params=_cparams(("arbitrary",)),
    )(up, up, fp, fp)


def _ffn_act_bwd(up, dact, fp):
    s, f2 = up.shape
    f = f2 // 2
    tc = FFN_COLS
    ts = min(RNN_CHUNK, s)
    nfb = f // tc

    def body(ug_ref, uv_ref, da_ref, wg_ref, wv_ref, dup_ref, red_ref, gpad, vpad, dgs, dvs):
        half = pl.program_id(1)
        wg, wv = wg_ref[...], wv_ref[...]

        @pl.when(half == 0)
        def _():
            zero8 = jnp.zeros((SUBLANES, tc), F32)
            gpad[0:SUBLANES, :] = zero8
            vpad[0:SUBLANES, :] = zero8
            gpad[SUBLANES:, :] = ug_ref[...]
            vpad[SUBLANES:, :] = uv_ref[...]
            dgs[s:, :] = zero8
            dvs[s:, :] = zero8

            def act(c, carry):
                base = pl.multiple_of(c * ts, ts)
                g = _ffn_conv(gpad, wg, base, ts)
                v = _ffn_conv(vpad, wv, base, ts)
                da = da_ref[pl.ds(base, ts), :]
                sg = _sigmoid(g)
                dgs[pl.ds(base, ts), :] = da * v * (sg * (1.0 + g * (1.0 - sg)))
                dvs[pl.ds(base, ts), :] = da * (g * sg)
                return carry

            lax.fori_loop(0, s // ts, act, 0)

        def conv_t(src, pad, w, out_ref, red_ref):
            def chunk(c, carry):
                base = pl.multiple_of(c * ts, ts)
                d_here = src[pl.ds(base, ts), :]
                dx = w[2:3] * d_here
                for j in range(1, 3):
                    dx = dx + w[2 - j:3 - j] * _shift_up(src, base, ts, j, 0)
                out_ref[pl.ds(base, ts), :] = dx.astype(BF16)
                taps = tuple(carry[k] + _colsum(d_here * _shift_down(pad, base, ts, 2 - k)) for k in range(3))
                return taps + (carry[3] + _colsum(d_here),)

            z1 = jnp.zeros((1, tc), F32)
            red = lax.fori_loop(0, s // ts, chunk, (z1, z1, z1, z1))
            red_ref[...] = jnp.concatenate(list(red) + [jnp.zeros((4, tc), F32)], axis=0)

        @pl.when(half == 0)
        def _():
            conv_t(dgs, gpad, wg, dup_ref, red_ref)

        @pl.when(half == 1)
        def _():
            conv_t(dvs, vpad, wv, dup_ref, red_ref)

    gcol = pl.BlockSpec((s, tc), lambda b, h: (0, b))
    vcol = pl.BlockSpec((s, tc), lambda b, h: (0, b + nfb))
    gpar = pl.BlockSpec((SUBLANES, tc), lambda b, h: (0, b))
    vpar = pl.BlockSpec((SUBLANES, tc), lambda b, h: (0, b + nfb))
    return pl.pallas_call(
        body, name="ffn_act_bwd", grid=(nfb, 2),
        in_specs=[gcol, vcol, gcol, gpar, vpar],
        out_specs=[pl.BlockSpec((s, tc), lambda b, h: (0, b + h * nfb)),
                   pl.BlockSpec((SUBLANES, tc), lambda b, h: (0, b + h * nfb))],
        out_shape=[jax.ShapeDtypeStruct((s, f2), BF16), jax.ShapeDtypeStruct((SUBLANES, f2), F32)],
        scratch_shapes=[pltpu.VMEM((s + SUBLANES, tc), F32)] * 4,
        compiler_params=_cparams(("arbitrary", "arbitrary")),
    )(up, up, dact, fp, fp)


def _rows8(rows, width):
    rows = [r.reshape(1, width).astype(F32) for r in rows]
    return jnp.concatenate(rows + [jnp.zeros((SUBLANES - len(rows), width), F32)], axis=0)


def _block_diag(w):
    n, b, _ = w.shape
    w = w.reshape(n // 2, 2, b, b)
    z = jnp.zeros((n // 2, b, b), w.dtype)
    top = jnp.concatenate([w[:, 0], z], axis=2)
    bot = jnp.concatenate([z, w[:, 1]], axis=2)
    return jnp.concatenate([top, bot], axis=1)


def _block_diag_t(bd):
    n, b2, _ = bd.shape
    b = b2 // 2
    return jnp.stack([bd[:, :b, :b], bd[:, b:, b:]], axis=1).reshape(2 * n, b, b)


def _local_step(x, mod, positions, target, w_in, fetch, sm, emit):
    s, d = x.shape
    o_rnn, o_mla = D_RNN, D_RNN + Q_LORA + KV_LORA + QK_ROPE
    wts = {}
    w_in_rnn = w_in[:, :o_rnn]
    w_in_mla = jnp.concatenate([w_in[:, o_rnn:o_mla], jnp.zeros((d, MLA_W - (o_mla - o_rnn)), w_in.dtype)], axis=1)
    w_in_g = w_in[:, o_mla:]
    hd = QK_NOPE + QK_ROPE
    wa_bd = _block_diag(sm["w_gate_a"]).astype(BF16)
    wx_bd = _block_diag(sm["w_gate_x"]).astype(BF16)

    pos = positions.reshape(s)
    half = QK_ROPE // 2
    inv_freq = ROPE_THETA ** (-jnp.arange(half, dtype=F32) / half)
    ang = pos.astype(F32)[:, None] * inv_freq
    cos, sin = jnp.cos(ang), jnp.sin(ang)
    rot_c = jnp.concatenate([jnp.ones((s, QK_NOPE), F32), cos, cos, jnp.ones((s, LANES - hd), F32)], axis=1)
    rot_s = jnp.concatenate([jnp.zeros((s, QK_NOPE), F32), -sin, sin, jnp.zeros((s, LANES - hd), F32)], axis=1)
    keep = (pos != 0).astype(F32).reshape(s, 1)

    gmod1 = _rows8([sm["norm1_g"], mod[1], mod[0]], d)
    gmod2 = _rows8([sm["norm2_g"], mod[4], mod[3], mod[2]], d)
    rp = jnp.concatenate([sm["conv_w"].reshape(4, D_RNN), _rows8([sm["conv_b"], sm["b_gate_a"], sm["b_gate_x"], sm["lru_param"]], D_RNN)[:4]], axis=0)
    fp = _rows8([sm["ffn_conv_w"][0], sm["ffn_conv_w"][1], sm["ffn_conv_w"][2], sm["ffn_conv_b"]], 2 * D_FF)
    ng = _rows8([jnp.concatenate([sm["q_norm_g"].reshape(-1), sm["kv_norm_g"].reshape(-1), jnp.zeros((MLA_W - Q_LORA - KV_LORA,), F32)])], MLA_W)
    fpar = _rows8([mod[5], sm["final_g"]], d)

    h = _norm_mod_fwd(x, gmod1, "norm1_fwd")
    proj_rnn = _mm(h, w_in_rnn, name="mm_in_rnn")
    proj_mla = _mm(h, w_in_mla, name="mm_in_mla")
    proj_g = _mm(h, w_in_g, name="mm_in_g")
    xc, ra, ix, hs = _rnn_fwd(proj_rnn, keep, rp, wa_bd, wx_bd)
    qn, kvn, kr = _mla_prep_fwd(proj_mla, rot_c, rot_s, ng)
    wts.update(fetch(("w_uq", "w_ukv"), kr))
    w_uq_p = jnp.pad(wts["w_uq"].reshape(Q_LORA, N_HEADS, hd), ((0, 0), (0, 0), (0, LANES - hd))).reshape(Q_LORA, N_HEADS * LANES)
    w_ukv = wts["w_ukv"]
    q_rot, q_rot_t = _rope_fwd_t(_mm(qn, w_uq_p, name="mm_uq"), rot_c, rot_s)
    kv, kvt = _mm(kvn, w_ukv, out_dtype=BF16, also_t=BF16, name="mm_ukv")
    krt = jnp.transpose(kr)
    y_mla, lse = _att_fwd(q_rot, kv, kvt, krt)
    wts.update(fetch(("w_proj_rnn", "w_proj_mla", "w_out", "w_up", "w_down"), lse))
    pr = _mm(hs, wts["w_proj_rnn"], name="mm_proj_rnn")
    pm = _mm(y_mla, wts["w_proj_mla"], name="mm_proj_mla")
    merged = _merge_fwd(pr, pm, proj_g)
    o = _mm(merged, wts["w_out"], name="mm_out")
    x1, h2 = _resid_norm_fwd(x, o, gmod2)
    up = _mm(h2, wts["w_up"], name="mm_up")
    act = _ffn_act_fwd(up, fp)
    dn = _mm(act, wts["w_down"], name="mm_down")

    dx2, ddn, red_f = _final_fwd_bwd(x1, dn, target, fpar)
    dact = _mm(ddn, wts["w_down"], tb=True, name="mm_d_act")
    tok = emit("w_down", _mm(act, ddn, ta=True, out_dtype=BF16, name="mm_dw_down"))
    dup, red_ffn = _ffn_act_bwd(up, dact, fp + tok)
    dh2 = _mm(dup, wts["w_up"], tb=True, name="mm_d_h2")
    tok = tok + emit("w_up", _mm(h2, dup, ta=True, out_dtype=BF16, name="mm_dw_up"))
    dx1, do, red_2 = _norm2_bwd(x1, dh2, dx2, o, gmod2 + tok)
    dmerged = _mm(do, wts["w_out"], tb=True, name="mm_d_merged")
    tok = tok + emit("w_out", _mm(merged, do, ta=True, out_dtype=BF16, name="mm_dw_out"))
    dpr, dpm, dg = _merge_bwd(dmerged, pr, pm, proj_g)
    dy_rnn = _mm(dpr, wts["w_proj_rnn"], tb=True, name="mm_d_yrnn")
    tok = tok + emit("w_proj_rnn", _mm(hs, dpr, ta=True, out_dtype=BF16, name="mm_dw_proj_rnn"))
    dy_mla, dy_mla_t = _mm(dpm, wts["w_proj_mla"], tb=True, also_t=BF16, name="mm_d_ymla")
    tok = tok + emit("w_proj_mla", _mm(y_mla, dpm, ta=True, out_dtype=BF16, name="mm_dw_proj_mla"))
    dq_rot, dkvt, dkrt = _att_bwd(q_rot, q_rot_t, kv, kvt, kr, krt, y_mla, lse, dy_mla, dy_mla_t)
    dq = _rope_bwd(dq_rot, rot_c, rot_s)
    dqn = _mm(dq, w_uq_p, tb=True, name="mm_d_qn")
    dw_uq_p = _mm(qn, dq, ta=True, out_dtype=BF16, name="mm_dw_uq")
    tok = tok + emit("w_uq", dw_uq_p.reshape(Q_LORA, N_HEADS, LANES)[:, :, :hd].reshape(Q_LORA, N_HEADS * hd))
    dkvn = jnp.transpose(_mm(w_ukv, dkvt, name="mm_d_kvn"))
    tok = tok + emit("w_ukv", jnp.transpose(_mm(dkvt, kvn, out_dtype=BF16, name="mm_dw_ukv")))
    dproj_mla, red_m = _mla_prep_bwd(proj_mla, dqn, dkvn, jnp.transpose(dkrt), rot_c, rot_s, ng + tok)
    dx_rnn, dwa_bd, dwx_bd, red_r = _rnn_bwd(proj_rnn, xc, ra, ix, hs, dy_rnn, keep, rp + tok, wa_bd, wx_bd)
    dw_in = jnp.concatenate([
        _mm(h, dx_rnn, ta=True, out_dtype=BF16, name="mm_dw_in_rnn"),
        _mm(h, dproj_mla, ta=True, out_dtype=BF16, name="mm_dw_in_mla")[:, :o_mla - o_rnn],
        _mm(h, dg, ta=True, out_dtype=BF16, name="mm_dw_in_g")], axis=1)
    tok = tok + emit("w_in", dw_in)
    dh_a = _mm(dx_rnn, w_in_rnn, tb=True, name="mm_d_h_rnn")
    dh_b = _mm(dproj_mla, w_in_mla, tb=True, name="mm_d_h_mla")
    dh_c = _mm(dg, w_in_g, tb=True, name="mm_d_h_g")
    grad_x, red_1 = _norm1_bwd(x, dh_a, dh_b, dh_c, dx1, gmod1 + tok)

    gs = {
        "norm1_g": red_1[0], "conv_w": red_r[0:4], "conv_b": red_r[4], "w_gate_a": _block_diag_t(dwa_bd),
        "b_gate_a": red_r[5], "w_gate_x": _block_diag_t(dwx_bd), "b_gate_x": red_r[6], "lru_param": red_r[7],
        "q_norm_g": red_m[0, :Q_LORA], "kv_norm_g": red_m[0, Q_LORA:Q_LORA + KV_LORA], "norm2_g": red_2[0],
        "ffn_conv_w": red_ffn[0:3], "ffn_conv_b": red_ffn[3], "final_g": red_f[0],
    }
    dmod = jnp.stack([red_1[2], red_1[1], red_2[3], red_2[2], red_2[1], red_f[1]], axis=0)
    return red_f[2, 0], grad_x, gs, dmod


MESH_IDS = pl.DeviceIdType.MESH
HBM_SPEC = pl.BlockSpec(memory_space=pltpu.HBM)


def _my_slot():
    return 4 * lax.axis_index("x") + 2 * lax.axis_index("y") + lax.axis_index("c")


def _all_gather(arrs, name):
    n = len(arrs)

    def body(*refs):
        ins, outs = refs[:n], refs[n:2 * n]
        send_sems, recv_sems, local_sems = refs[2 * n:]
        x, y, c = lax.axis_index("x"), lax.axis_index("y"), lax.axis_index("c")
        me, sibling = (x, y, c), (x, y, 1 - c)
        chips = [(1 - x, y), (x, 1 - y), (1 - x, 1 - y)]

        def slot(dev):
            return 4 * dev[0] + 2 * dev[1] + dev[2]

        def copy(a, k, block, to, src=None):
            dst = outs[a].at[slot(block)]
            return pltpu.make_async_remote_copy(
                src_ref=dst if src is None else src, dst_ref=dst, send_sem=send_sems.at[a, k], recv_sem=recv_sems.at[a, k],
                device_id=to, device_id_type=MESH_IDS)

        mine = [pltpu.make_async_copy(ins[a], outs[a].at[slot(me)], local_sems.at[a]) for a in range(n)]
        for cp in mine:
            cp.start()
        first = []
        for a in range(n):
            first.append(copy(a, 0, me, sibling, src=ins[a]))
            first += [copy(a, 1 + j, me, (*chip, c), src=ins[a]) for j, chip in enumerate(chips)]
        for cp in first:
            cp.start()
        passed = []
        for j, chip in enumerate(chips):
            for a in range(n):
                copy(a, 1 + j, (*chip, c), me).wait_recv()
                fwd = copy(a, 4 + j, (*chip, c), sibling)
                fwd.start()
                passed.append(fwd)
        for a in range(n):
            copy(a, 0, sibling, me).wait_recv()
            for j, chip in enumerate(chips):
                copy(a, 4 + j, (*chip, 1 - c), me).wait_recv()
        for cp in first + passed:
            cp.wait_send()
        for cp in mine:
            cp.wait()

    return pl.pallas_call(
        body, name=name,
        in_specs=[HBM_SPEC] * n, out_specs=[HBM_SPEC] * n,
        out_shape=[jax.ShapeDtypeStruct((N_DEV,) + a.shape, a.dtype) for a in arrs],
        scratch_shapes=[pltpu.SemaphoreType.DMA((n, 7)), pltpu.SemaphoreType.DMA((n, 7)), pltpu.SemaphoreType.DMA((n,))],
    )(*arrs)


SEM_SPEC =pl.BlockSpec(memory_space=pltpu.SEMAPHORE)
DATAFLOW = pltpu.SideEffectType.DATAFLOW_SIDE_EFFECTING
FLIPS = [(dx, dy, dc) for dx in (0, 1) for dy in (0, 1) for dc in (0, 1)][1:]


def _peer(k):
    dx, dy, dc = FLIPS[k]
    peer = (lax.axis_index("x") ^ dx, lax.axis_index("y") ^ dy, lax.axis_index("c") ^ dc)
    return peer, 4 * peer[0] + 2 * peer[1] + peer[2]


def _gather_start(shards, after, name):
    n, nf = len(shards), len(FLIPS)

    def body(*refs):
        srcs, lands = refs[:n], refs[n:2 * n]
        send_sems, recv_sems = refs[2 * n + 1:3 * n + 1], refs[3 * n + 1:4 * n + 1]
        token = refs[-1]
        me = _my_slot()
        for a in range(n):
            for k in range(nf):
                peer, _ = _peer(k)
                pltpu.make_async_remote_copy(
                    src_ref=srcs[a], dst_ref=lands[a].at[me], send_sem=send_sems[a].at[k], recv_sem=recv_sems[a].at[k],
                    device_id=peer, device_id_type=MESH_IDS).start()
        token[...] = jnp.zeros(token.shape, F32)

    land_shapes = [(N_DEV,) + a.shape for a in shards]
    sems = [pltpu.SemaphoreType.DMA((nf,))] * n
    out = pl.pallas_call(
        body, name=name,
        out_shape=(*sems, *sems, *[pltpu.HBM(a.shape, a.dtype) for a in shards],
                   *[pltpu.HBM(shp, a.dtype) for shp, a in zip(land_shapes, shards)],
                   jax.ShapeDtypeStruct((SUBLANES, LANES), F32)),
        in_specs=[HBM_SPEC] * (2 * n) + [pl.BlockSpec(memory_space=pl.ANY)],
        out_specs=(*[SEM_SPEC] * (2 * n), *[HBM_SPEC] * (2 * n), pl.BlockSpec(memory_space=pltpu.VMEM)),
        input_output_aliases={i: 2 * n + i for i in range(2 * n)},
        compiler_params=pltpu.CompilerParams(has_side_effects=DATAFLOW),
    )(*[pltpu.with_memory_space_constraint(a, pltpu.HBM) for a in shards],
      *[pltpu.with_memory_space_constraint(lax.empty(shp, a.dtype), pltpu.HBM) for shp, a in zip(land_shapes, shards)],
      after)
    return [(out[a], out[n + a], out[2 * n + a], out[3 * n + a]) for a in range(n)], out[-1]


def _gather_wait(flights, after, name):
    n, nf = len(flights), len(FLIPS)

    def body(*refs):
        send_sems, recv_sems = refs[:n], refs[n:2 * n]
        srcs, lands = refs[2 * n:3 * n], refs[3 * n:4 * n]
        for a in range(n):
            for k in range(nf):
                peer, peer_slot = _peer(k)
                cp = pltpu.make_async_remote_copy(
                    src_ref=srcs[a], dst_ref=lands[a].at[peer_slot], send_sem=send_sems[a].at[k],
                    recv_sem=recv_sems[a].at[k], device_id=peer, device_id_type=MESH_IDS)
                cp.wait_send()
                cp.wait_recv()

    srcs, lands = [f[2] for f in flights], [f[3] for f in flights]
    out = pl.pallas_call(
        body, name=name,
        out_shape=(*[pltpu.HBM(a.shape, a.dtype) for a in srcs], *[pltpu.HBM(a.shape, a.dtype) for a in lands]),
        in_specs=[SEM_SPEC] * (2 * n) + [HBM_SPEC] * (2 * n) + [pl.BlockSpec(memory_space=pl.ANY)],
        out_specs=tuple([HBM_SPEC] * (2 * n)),
        input_output_aliases={2 * n + i: i for i in range(2 * n)},
        compiler_params=pltpu.CompilerParams(has_side_effects=DATAFLOW),
    )(*[f[0] for f in flights], *[f[1] for f in flights], *srcs, *lands, after)
    return list(out[n:])


def _scatter_start(chunks, name):
    def body(src_ref, land_ref, send_sems, recv_sems, src_thru, land_thru, token):
        me = _my_slot()
        for k in range(len(FLIPS)):
            peer, peer_slot = _peer(k)
            pltpu.make_async_remote_copy(
                src_ref=src_ref.at[peer_slot], dst_ref=land_ref.at[me], send_sem=send_sems.at[k], recv_sem=recv_sems.at[k],
                device_id=peer, device_id_type=MESH_IDS).start()
        token[...] = jnp.zeros(token.shape, F32)

    n = len(FLIPS)
    hbm = pltpu.HBM(chunks.shape, chunks.dtype)
    return pl.pallas_call(
        body, name=name,
        out_shape=(pltpu.SemaphoreType.DMA((n,)), pltpu.SemaphoreType.DMA((n,)), hbm, hbm,
                   jax.ShapeDtypeStruct((SUBLANES, LANES), F32)),
        in_specs=(HBM_SPEC, HBM_SPEC),
        out_specs=(SEM_SPEC, SEM_SPEC, HBM_SPEC, HBM_SPEC, pl.BlockSpec(memory_space=pltpu.VMEM)),
        input_output_aliases={0: 2, 1: 3},
        compiler_params=pltpu.CompilerParams(has_side_effects=DATAFLOW),
    )(pltpu.with_memory_space_constraint(chunks, pltpu.HBM),
      pltpu.with_memory_space_constraint(lax.empty(chunks.shape, chunks.dtype), pltpu.HBM))


def _scatter_wait(send_sems, recv_sems, src_thru, land_thru, after, name):
    def body(src_ref, land_ref, send_sems, recv_sems, after_ref, src_dead, got_ref):
        for k in range(len(FLIPS)):
            peer, peer_slot = _peer(k)
            cp = pltpu.make_async_remote_copy(
                src_ref=src_ref.at[peer_slot], dst_ref=land_ref.at[peer_slot], send_sem=send_sems.at[k],
                recv_sem=recv_sems.at[k], device_id=peer, device_id_type=MESH_IDS)
            cp.wait_send()
            cp.wait_recv()

    hbm = pltpu.HBM(src_thru.shape, src_thru.dtype)
    return pl.pallas_call(
        body, name=name, out_shape=(hbm, hbm),
        in_specs=(HBM_SPEC, HBM_SPEC, SEM_SPEC, SEM_SPEC, pl.BlockSpec(memory_space=pl.ANY)),
        out_specs=(HBM_SPEC, HBM_SPEC), input_output_aliases={0: 0, 1: 1},
        compiler_params=pltpu.CompilerParams(has_side_effects=DATAFLOW),
    )(src_thru, land_thru, send_sems, recv_sems, after)


def _sum_sources(parts, own, name):
    k, r, c = parts.shape
    tr = r if k * r * c <= 2 * 1024 * 1024 else _pick(r, (512, 256, 128, 64, 32, 16, 8))

    def body(p_ref, own_ref, o_ref):
        me = _my_slot()
        acc = jnp.where(me == 0, own_ref[...], p_ref[0]).astype(F32)
        for s in range(1, k):
            acc = acc + jnp.where(me == s, own_ref[...], p_ref[s]).astype(F32)
        o_ref[...] = acc

    blk = pl.BlockSpec((tr, c), lambda i: (i, 0))
    return pl.pallas_call(
        body, name=name, grid=(r // tr,),
        in_specs=[pl.BlockSpec((k, tr, c), lambda i: (0, i, 0)), blk],
        out_specs=blk,
        out_shape=jax.ShapeDtypeStruct((r, c), F32),
        compiler_params=_cparams(("arbitrary",)),
    )(parts, own)


def _adamw(parts, w, m, v, name, own=None):
    k, r, c = parts.shape
    tr = r if r * c <= 256 * 1024 else _pick(r, (256, 128, 64, 32, 16, 8))

    def body(*refs):
        p_ref, w_ref, m_ref, v_ref = refs[:4]
        g_ref, d_ref, nm_ref, nv_ref = refs[-4:]

        def part(s):
            if own is None:
                return p_ref[s].astype(F32)
            return jnp.where(_my_slot() == s, refs[4][...], p_ref[s]).astype(F32)

        g = part(0)
        for s in range(1, k):
            g = g + part(s)
        m_new = ADAM_B1 * m_ref[...] + (1.0 - ADAM_B1) * g
        v_new = ADAM_B2 * v_ref[...] + (1.0 - ADAM_B2) * jnp.square(g)
        m_hat = m_new / (1.0 - ADAM_B1 ** ADAM_STEP)
        v_hat = v_new / (1.0 - ADAM_B2 ** ADAM_STEP)
        g_ref[...] = g
        d_ref[...] = -ADAM_LR * (m_hat / (jnp.sqrt(v_hat) + ADAM_EPS) + ADAM_WD * w_ref[...])
        nm_ref[...] = m_new
        nv_ref[...] = v_new

    blk = pl.BlockSpec((tr, c), lambda i: (i, 0))
    return pl.pallas_call(
        body, name=name, grid=(r // tr,),
        in_specs=[pl.BlockSpec((k, tr, c), lambda i: (0, i, 0)), blk, blk, blk] + ([] if own is None else [blk]),
        out_specs=[blk] * 4,
        out_shape=[jax.ShapeDtypeStruct((r, c), F32)] * 4,
        compiler_params=_cparams(("arbitrary",)),
    )(parts, w, m, v, *([] if own is None else [own]))


def _silu(v):
    return v * _sigmoid(v)


def _ada_fwd(c_all, w, b):
    def body(c_ref, w_ref, b_ref, o_ref):
        ca = _silu(c_ref[...]).astype(BF16)
        o_ref[...] = jnp.dot(ca, w_ref[...].astype(BF16), preferred_element_type=F32) + b_ref[...]

    return pl.pallas_call(
        body, name="ada_fwd", out_shape=jax.ShapeDtypeStruct((c_all.shape[0], w.shape[1]), F32),
        compiler_params=_cparams(),
    )(c_all, w, b)


def _ada_bwd(c_all, dmod):
    def body(c_ref, d_ref, o_ref):
        ca = _silu(c_ref[...]).astype(BF16).astype(F32)
        dm = d_ref[...].astype(BF16).astype(F32)
        acc = jnp.zeros(o_ref.shape, F32)
        for bi in range(c_all.shape[0]):
            acc = acc + jnp.transpose(ca[bi:bi + 1, :]) * dm[bi:bi + 1, :]
        o_ref[...] = acc

    return pl.pallas_call(
        body, name="ada_bwd", out_shape=jax.ShapeDtypeStruct((c_all.shape[1], dmod.shape[1]), F32),
        compiler_params=_cparams(),
    )(c_all, dmod)


COL_SHARDED = ("w_in", "w_uq", "w_ukv", "w_up")
ROW_SHARDED = ("w_proj_rnn", "w_proj_mla", "w_out", "w_down")
REPLICATED = ("b_ada", "norm1_g", "conv_b", "w_gate_a", "b_gate_a", "w_gate_x", "b_gate_x", "lru_param", "q_norm_g",
              "kv_norm_g", "norm2_g", "ffn_conv_b", "final_g")
WEIGHTS = ("w_ada", "b_ada", "norm1_g", "w_in", "conv_w", "conv_b", "w_gate_a", "b_gate_a", "w_gate_x", "b_gate_x",
           "lru_param", "q_norm_g", "w_uq", "kv_norm_g", "w_ukv", "w_proj_rnn", "w_proj_mla", "w_out", "norm2_g", "w_up",
           "ffn_conv_w", "ffn_conv_b", "w_down", "final_g")
PACK_LANES = 128


def _pack(vecs, row_multiple=SUBLANES):
    flat = jnp.concatenate([v.reshape(-1).astype(F32) for v in vecs])
    pad = (-flat.shape[0]) % (PACK_LANES * row_multiple)
    return jnp.concatenate([flat, jnp.zeros((pad,), F32)]).reshape(-1, PACK_LANES)


def _unpack(packed, shapes):
    flat = packed.reshape(-1)
    out, off = [], 0
    for shp in shapes:
        size = math.prod(shp)
        out.append(flat[off:off + size].reshape(shp))
        off += size
    return out


def kernel(x, c, positions, w_ada, b_ada, norm1_g, w_in, conv_w, conv_b, w_gate_a, b_gate_a, w_gate_x, b_gate_x, lru_param, q_norm_g, w_uq, kv_norm_g, w_ukv, w_proj_rnn, w_proj_mla, w_out, norm2_g, w_up, ffn_conv_w, ffn_conv_b, w_down, final_g, loss_target, m_w_ada, m_b_ada, m_norm1_g, m_w_in, m_conv_w, m_conv_b, m_w_gate_a, m_b_gate_a, m_w_gate_x, m_b_gate_x, m_lru_param, m_q_norm_g, m_w_uq, m_kv_norm_g, m_w_ukv, m_w_proj_rnn, m_w_proj_mla, m_w_out, m_norm2_g, m_w_up, m_ffn_conv_w, m_ffn_conv_b, m_w_down, m_final_g, v_w_ada, v_b_ada, v_norm1_g, v_w_in, v_conv_w, v_conv_b, v_w_gate_a, v_b_gate_a, v_w_gate_x, v_b_gate_x, v_lru_param, v_q_norm_g, v_w_uq, v_kv_norm_g, v_w_ukv, v_w_proj_rnn, v_w_proj_mla, v_w_out, v_norm2_g, v_w_up, v_ffn_conv_w, v_ffn_conv_b, v_w_down, v_final_g):
    args = dict(locals())
    w = {n: args[n] for n in WEIGHTS}
    m = {n: args["m_" + n] for n in WEIGHTS}
    v = {n: args["v_" + n] for n in WEIGHTS}
    s, d = x.shape[1], x.shape[2]
    me = _my_slot()
    def two_d(a):
        assert a.ndim == 3 and a.shape[0] == 1, a.shape
        return a[0]

    big = COL_SHARDED + ROW_SHARDED
    shard = {n: two_d(w[n]).astype(BF16) for n in big}

    def whole(n, g):
        k, r, cc = g.shape
        return jnp.transpose(g, (1, 0, 2)).reshape(r, k * cc) if n in COL_SHARDED else g.reshape(k * r, cc)

    first = _all_gather([shard["w_in"], c, two_d(conv_w), two_d(ffn_conv_w)], "gather_first")
    c_all = first[1].reshape(N_DEV, d)
    conv_w_all = jnp.transpose(first[2], (1, 0, 2)).reshape(conv_w.shape[1], -1)
    ffn_conv_w_all = jnp.transpose(first[3], (1, 0, 2)).reshape(ffn_conv_w.shape[1], -1)

    ada_cols = w_ada.shape[2]
    b_cols = lax.dynamic_slice(b_ada, (0, me * ada_cols), (1, ada_cols))
    mod_cols = _ada_fwd(c_all, w_ada[0], b_cols)
    mod_all, = _all_gather([mod_cols], "gather_mod")

    later = ("w_uq", "w_ukv", "w_proj_rnn", "w_proj_mla", "w_out", "w_up", "w_down")
    flights, started = _gather_start([shard[n] for n in later], mod_all, "gather_start")
    flight = dict(zip(later, flights))

    def fetch(names, after):
        lands = _gather_wait([flight[n] for n in names], after, "gather_wait_" + names[0])
        return {n: whole(n, lax.dynamic_update_index_in_dim(g, shard[n], me, 0)) for n, g in zip(names, lands)}

    mod = lax.dynamic_index_in_dim(mod_all, me, axis=1, keepdims=False).reshape(6, d) + started[0, 0]

    sm = {n: w[n][0] for n in REPLICATED if n not in ("b_ada", "final_g")}
    sm["final_g"] = final_g
    sm["conv_w"] = conv_w_all
    sm["ffn_conv_w"] = ffn_conv_w_all
    in_flight = {}

    def emit(n, g):
        if n in COL_SHARDED:
            r, cc = g.shape
            chunks = jnp.transpose(g.reshape(r, N_DEV, cc // N_DEV), (1, 0, 2))
        else:
            chunks = g.reshape(N_DEV, g.shape[0] // N_DEV, g.shape[1])
        *in_flight[n], token = _scatter_start(chunks, "scatter_start_" + n)
        return token[0, 0]

    sq, grad_x, gs, dmod = _local_step(x[0], mod, positions[0], loss_target[0], whole("w_in", first[0]), fetch, sm, emit)

    small_names = [n for n in REPLICATED if n != "b_ada"] + ["conv_w", "ffn_conv_w"]
    small_shapes = [gs[n].shape for n in small_names] + [(6 * d,), (1,)]
    partial = _pack([gs[n] for n in small_names] + [dmod, sq.reshape(1)], N_DEV * SUBLANES)
    *small_flight, small_started = _scatter_start(partial.reshape(N_DEV, -1, PACK_LANES), "scatter_small_start")

    grads, deltas, new_m, new_v = {}, {}, {}, {}

    def update(n, parts, own=None):
        shp = w[n].shape
        g, dl, nm, nv = _adamw(parts, two_d(w[n]), two_d(m[n]), two_d(v[n]), "adamw_" + n, own)
        grads[n], deltas[n], new_m[n], new_v[n] = g.reshape(shp), dl.reshape(shp), nm.reshape(shp), nv.reshape(shp)

    for n in big:
        chunks, landed = _scatter_wait(*in_flight[n], small_started, "scatter_wait_" + n)
        update(n, landed, lax.dynamic_index_in_dim(chunks, me, axis=0, keepdims=False))

    chunks, landed = _scatter_wait(*small_flight, new_v[big[-1]], "scatter_small_wait")
    mine = _sum_sources(landed, lax.dynamic_index_in_dim(chunks, me, axis=0, keepdims=False), "sum_small")
    summed_all, dmod_all = _all_gather([mine, dmod.reshape(1, 6 * d)], "gather_small")
    summed = _unpack(summed_all, small_shapes)
    g_small = dict(zip(small_names, summed[:len(small_names)]))
    g_small["b_ada"] = summed[len(small_names)]
    loss = 0.5 * summed[-1][0] / d
    dmod_cols = lax.dynamic_slice(dmod_all.reshape(N_DEV, 6 * d), (0, me * ada_cols), (N_DEV, ada_cols))

    update("w_ada", _ada_bwd(c_all, dmod_cols)[None])

    for n in ("conv_w", "ffn_conv_w"):
        cols = w[n].shape[2]
        update(n, lax.dynamic_slice(g_small[n], (0, me * cols), (g_small[n].shape[0], cols))[None])

    rep_shapes = [w[n].shape for n in REPLICATED]
    g_rep, d_rep, m_rep, v_rep = _adamw(
        _pack([g_small[n] for n in REPLICATED])[None], _pack([w[n] for n in REPLICATED]), _pack([m[n] for n in REPLICATED]),
        _pack([v[n] for n in REPLICATED]), "adamw_replicated")
    for dst, packed in ((grads, g_rep), (deltas, d_rep), (new_m, m_rep), (new_v, v_rep)):
        dst.update(zip(REPLICATED, _unpack(packed, rep_shapes)))

    return (loss, grad_x[None], *[grads[n] for n in WEIGHTS], *[deltas[n] for n in WEIGHTS],
            *[new_m[n] for n in WEIGHTS], *[new_v[n] for n in WEIGHTS])
```

```python
import functools
import math

import jax
import jax.numpy as jnp
from jax import lax
from jax.experimental import pallas as pl
from jax.experimental.pallas import tpu as pltpu

F32 = jnp.float32
BF16 = jnp.bfloat16

N_DEV = 8
LANES = 128
SUBLANES = 8
VMEM_LIMIT = 56 * 1024 * 1024

D_RNN = 1280
Q_LORA = 384
KV_LORA = 256
QK_NOPE = 64
QK_ROPE = 32
V_HEAD = 64
N_HEADS = 16
D_FF = 2816
ROPE_THETA = 10000.0
LRU_C = 8.0
EPS = 1e-6
MLA_W = 768
ATT_SCALE = 1.0 / math.sqrt(QK_NOPE + QK_ROPE)

ADAM_LR, ADAM_B1, ADAM_B2, ADAM_EPS, ADAM_WD, ADAM_STEP = 0.001, 0.9, 0.999, 1e-08, 0.01, 10


def _cparams(sem=None):
    return pltpu.CompilerParams(dimension_semantics=sem, vmem_limit_bytes=VMEM_LIMIT)


def _pick(n, prefs):
    for p in prefs:
        if n % p == 0:
            return p
    return n


def _sigmoid(v):
    return 0.5 * jnp.tanh(0.5 * v) + 0.5


def _lane(shape):
    return lax.broadcasted_iota(jnp.int32, shape, len(shape) - 1)


def _row(shape):
    return lax.broadcasted_iota(jnp.int32, shape, len(shape) - 2)


MM_BLOCK_BYTES = 36 * 1024 * 1024


def _divisors(n):
    return [t for t in range(n, 0, -LANES) if n % t == 0] if n % LANES == 0 else [n]


HBM_BYTES_PER_US = 3.0e6
MXU_FLOPS_PER_US = 8.0e8
GRID_STEP_US = 0.35


def _mm_tiles(m, n, k, a_bytes, b_bytes, o_bytes):
    best = None
    for tm in [t for t in _divisors(m) if t <= 1024]:
        for tn in [t for t in _divisors(n) if t <= 2048]:
            for tk in _divisors(k):
                nk = k // tk
                need = 2 * (tm * tk * a_bytes + tk * tn * b_bytes + tm * tn * o_bytes) + (tm * tn * 4 if nk > 1 else 0)
                if need > MM_BLOCK_BYTES:
                    continue
                gi, gj = m // tm, n // tn
                for rows_outer in (True, False):
                    if nk > 1:
                        a_reads, b_reads = gj, gi
                    elif rows_outer:
                        a_reads, b_reads = 1, (gi if gj > 1 else 1)
                    else:
                        a_reads, b_reads = (gj if gi > 1 else 1), 1
                    traffic = m * k * a_bytes * a_reads + k * n * b_bytes * b_reads + m * n * (o_bytes + (8 * nk if nk > 1 else 0))
                    cost = max(traffic / HBM_BYTES_PER_US, 2.0 * m * n * k / MXU_FLOPS_PER_US) + gi * gj * nk * GRID_STEP_US
                    if best is None or cost < best[0]:
                        best = (cost, tm, tn, tk, rows_outer)
                break
    if best is None:
        raise ValueError((m, n, k))
    return best[1:]


def _mm(a, b, *, ta=False, tb=False, out_dtype=F32, also_t=None, name):
    (k_a, m) = a.shape if ta else a.shape[::-1]
    (n, k_b) = b.shape if tb else b.shape[::-1]
    assert k_a == k_b, (a.shape, b.shape, ta, tb)
    k = k_a
    tm, tn, tk, rows_outer = _mm_tiles(m, n, k, a.dtype.itemsize, b.dtype.itemsize, jnp.dtype(out_dtype).itemsize)
    nk = k // tk
    dims = (((0 if ta else 1,), (1 if tb else 0,)), ((), ()))
    n_out = 1 if also_t is None else 2

    def body(a_ref, b_ref, *rest):
        outs, acc = rest[:n_out], rest[n_out:]
        part = lax.dot_general(a_ref[...].astype(BF16), b_ref[...].astype(BF16), dims, preferred_element_type=F32)

        def write(val):
            outs[0][...] = val.astype(out_dtype)
            if also_t is not None:
                outs[1][...] = jnp.transpose(val).astype(also_t)

        if nk == 1:
            write(part)
            return
        acc_ref, = acc
        kk = pl.program_id(2)

        @pl.when(kk == 0)
        def _():
            acc_ref[...] = part

        @pl.when(kk > 0)
        def _():
            acc_ref[...] += part

        @pl.when(kk == nk - 1)
        def _():
            write(acc_ref[...])

    def ij(f):
        return (lambda i, j, kk: f(i, j, kk)) if rows_outer else (lambda j, i, kk: f(i, j, kk))

    a_spec = pl.BlockSpec((tk, tm), ij(lambda i, j, kk: (kk, i))) if ta else pl.BlockSpec((tm, tk), ij(lambda i, j, kk: (i, kk)))
    b_spec = pl.BlockSpec((tn, tk), ij(lambda i, j, kk: (j, kk))) if tb else pl.BlockSpec((tk, tn), ij(lambda i, j, kk: (kk, j)))
    out_specs = [pl.BlockSpec((tm, tn), ij(lambda i, j, kk: (i, j)))]
    out_shape = [jax.ShapeDtypeStruct((m, n), out_dtype)]
    if also_t is not None:
        out_specs.append(pl.BlockSpec((tn, tm), ij(lambda i, j, kk: (j, i))))
        out_shape.append(jax.ShapeDtypeStruct((n, m), also_t))
    res = pl.pallas_call(
        body, name=name,
        grid=(m // tm, n // tn, nk) if rows_outer else (n // tn, m // tm, nk),
        in_specs=[a_spec, b_spec], out_specs=out_specs, out_shape=out_shape,
        scratch_shapes=[] if nk == 1 else [pltpu.VMEM((tm, tn), F32)],
        compiler_params=_cparams(("arbitrary", "arbitrary", "arbitrary")),
    )(a, b)
    return res[0] if also_t is None else res


def _rowwise(fn, row_ins, par_ins, out_defs, red_defs, *, name, tr=256):
    s = row_ins[0].shape[0]
    tr = min(tr, s)
    nr, npar, no = len(row_ins), len(par_ins), len(out_defs)

    def body(*refs):
        rin, pin = refs[:nr], refs[nr:nr + npar]
        outs, reds = refs[nr + npar:nr + npar + no], refs[nr + npar + no:]
        i = pl.program_id(0)

        @pl.when(i == 0)
        def _():
            for r in reds:
                r[...] = jnp.zeros_like(r)

        fn(i, rin, pin, outs, reds)

    in_specs = [pl.BlockSpec((tr, a.shape[1]), lambda i: (i, 0)) for a in row_ins]
    in_specs += [pl.BlockSpec(a.shape, lambda i, nd=a.ndim: (0,) * nd) for a in par_ins]
    out_specs = [pl.BlockSpec((tr, c), lambda i: (i, 0)) for c, _ in out_defs]
    out_specs += [pl.BlockSpec(shp, lambda i: (0, 0)) for shp in red_defs]
    out_shape = [jax.ShapeDtypeStruct((s, c), dt) for c, dt in out_defs]
    out_shape += [jax.ShapeDtypeStruct(shp, F32) for shp in red_defs]
    return pl.pallas_call(
        body, name=name, grid=(s // tr,), in_specs=in_specs, out_specs=out_specs, out_shape=out_shape,
        compiler_params=_cparams(("arbitrary",)),
    )(*row_ins, *par_ins)


def _rms(v):
    return lax.rsqrt(jnp.mean(v * v, axis=-1, keepdims=True) + EPS)


def _colsum(v):
    return jnp.sum(v, axis=0, keepdims=True)


def _rms_bwd(dn, n, rstd):
    return rstd * (dn - n * jnp.mean(dn * n, axis=-1, keepdims=True))


def _norm_mod_fwd(x, gmod, name):
    def fn(i, rin, pin, outs, reds):
        xv = rin[0][...]
        p = pin[0][...]
        n = xv * _rms(xv)
        outs[0][...] = ((n * p[0:1]) * (1.0 + p[1:2]) + p[2:3]).astype(BF16)

    return _rowwise(fn, [x], [gmod], [(x.shape[1], BF16)], [], name=name)[0]


def _rope(v, rot_c, rot_s):
    half = QK_ROPE // 2
    swapped = jnp.where(_lane(v.shape) < QK_NOPE + half, pltpu.roll(v, LANES - half, 1), pltpu.roll(v, half, 1))
    return v * rot_c + swapped * rot_s


def _rope_t(dv, rot_c, rot_s):
    half = QK_ROPE // 2
    ds = dv * rot_s
    lane = _lane(dv.shape)
    swapped = jnp.where(lane < QK_NOPE + half, pltpu.roll(ds, LANES - half, 1), pltpu.roll(ds, half, 1))
    in_rope = (lane >= QK_NOPE) & (lane < QK_NOPE + QK_ROPE)
    return dv * rot_c + jnp.where(in_rope, swapped, 0.0)


def _mla_prep_fwd(proj_mla, rot_c, rot_s, ng):
    o1, o2 = Q_LORA, Q_LORA + KV_LORA

    def fn(i, rin, pin, outs, reds):
        g = pin[0][...]
        ql = rin[0][:, 0:o1]
        kl = rin[0][:, o1:o2]
        outs[0][...] = (ql * _rms(ql) * g[0:1, 0:o1]).astype(BF16)
        outs[1][...] = (kl * _rms(kl) * g[0:1, o1:o2]).astype(BF16)
        kr = pltpu.roll(rin[0][:, o2:o2 + LANES], QK_NOPE, 1)
        outs[2][...] = _rope(kr, rin[1][...], rin[2][...]).astype(BF16)

    return _rowwise(fn, [proj_mla, rot_c, rot_s], [ng], [(Q_LORA, BF16), (KV_LORA, BF16), (LANES, BF16)], [],
                    name="mla_prep_fwd")


def _mla_prep_bwd(proj_mla, dqn, dkvn, dkr, rot_c, rot_s, ng):
    o1, o2 = Q_LORA, Q_LORA + KV_LORA

    def fn(i, rin, pin, outs, reds):
        g = pin[0][...]
        ql = rin[0][:, 0:o1]
        kl = rin[0][:, o1:o2]
        rq, rk = _rms(ql), _rms(kl)
        nq, nk = ql * rq, kl * rk
        dq, dk = rin[1][...], rin[2][...]
        outs[0][:, 0:o1] = _rms_bwd(dq * g[0:1, 0:o1], nq, rq).astype(BF16)
        outs[0][:, o1:o2] = _rms_bwd(dk * g[0:1, o1:o2], nk, rk).astype(BF16)
        dkr_pre = pltpu.roll(_rope_t(rin[3][...], rin[4][...], rin[5][...]), LANES - QK_NOPE, 1)
        outs[0][:, o2:] = jnp.where(_lane(dkr_pre.shape) < QK_ROPE, dkr_pre, 0.0).astype(BF16)
        reds[0][0:1, 0:o1] += _colsum(dq * nq)
        reds[0][0:1, o1:o2] += _colsum(dk * nk)

    return _rowwise(fn, [proj_mla, dqn, dkvn, dkr, rot_c, rot_s], [ng], [(MLA_W, BF16)], [(SUBLANES, MLA_W)],
                    name="mla_prep_bwd")


def _rope_bwd(dq, rot_c, rot_s):
    def fn(i, rin, pin, outs, reds):
        c, sn = rin[1][...] * Q_PRESCALE, rin[2][...] * Q_PRESCALE
        for h in range(N_HEADS):
            sl = slice(h * LANES, (h + 1) * LANES)
            outs[0][:, sl] = _rope_t(rin[0][:, sl], c, sn).astype(BF16)

    return _rowwise(fn, [dq, rot_c, rot_s], [], [(dq.shape[1], BF16)], [], name="rope_bwd")[0]


def _rope_fwd_t(q, rot_c, rot_s):
    s, c = q.shape
    tr = min(256, s)

    def body(q_ref, c_ref, s_ref, o_ref, ot_ref):
        cc, sn = c_ref[...] * Q_PRESCALE, s_ref[...] * Q_PRESCALE
        for h in range(N_HEADS):
            sl = slice(h * LANES, (h + 1) * LANES)
            rot = _rope(q_ref[:, sl], cc, sn)
            o_ref[:, sl] = rot.astype(BF16)
            ot_ref[sl, :] = jnp.transpose(rot).astype(BF16)

    return pl.pallas_call(
        body, name="rope_fwd", grid=(s // tr,),
        in_specs=[pl.BlockSpec((tr, c), lambda i: (i, 0)), pl.BlockSpec((tr, LANES), lambda i: (i, 0)),
                  pl.BlockSpec((tr, LANES), lambda i: (i, 0))],
        out_specs=[pl.BlockSpec((tr, c), lambda i: (i, 0)), pl.BlockSpec((c, tr), lambda i: (0, i))],
        out_shape=[jax.ShapeDtypeStruct((s, c), BF16), jax.ShapeDtypeStruct((c, s), BF16)],
        compiler_params=_cparams(("arbitrary",)),
    )(q, rot_c, rot_s)


def _merge_fwd(pr, pm, proj_g):
    d = pr.shape[1]

    def fn(i, rin, pin, outs, reds):
        outs[0][...] = (_sigmoid(rin[2][:, 0:d]) * rin[0][...] + _sigmoid(rin[2][:, d:]) * rin[1][...]).astype(BF16)

    return _rowwise(fn, [pr, pm, proj_g], [], [(d, BF16)], [], name="merge_fwd")[0]


def _merge_bwd(dmerged, pr, pm, proj_g):
    d = pr.shape[1]

    def fn(i, rin, pin, outs, reds):
        dm = rin[0][...]
        sr, sm = _sigmoid(rin[3][:, 0:d]), _sigmoid(rin[3][:, d:])
        outs[0][...] = (dm * sr).astype(BF16)
        outs[1][...] = (dm * sm).astype(BF16)
        outs[2][:, 0:d] = (dm * rin[1][...] * sr * (1.0 - sr)).astype(BF16)
        outs[2][:, d:] = (dm * rin[2][...] * sm * (1.0 - sm)).astype(BF16)

    return _rowwise(fn, [dmerged, pr, pm, proj_g], [], [(d, BF16), (d, BF16), (2 * d, BF16)], [], name="merge_bwd")


def _resid_norm_fwd(x, o, gmod):
    d = x.shape[1]

    def fn(i, rin, pin, outs, reds):
        p = pin[0][...]
        x1 = rin[0][...] + p[3:4] * rin[1][...]
        outs[0][...] = x1
        outs[1][...] = ((x1 * _rms(x1) * p[0:1]) * (1.0 + p[1:2]) + p[2:3]).astype(BF16)

    return _rowwise(fn, [x, o], [gmod], [(d, F32), (d, BF16)], [], name="resid_norm_fwd")


def _final_fwd_bwd(x1, dn, target, par):
    d = x1.shape[1]

    def fn(i, rin, pin, outs, reds):
        p = pin[0][...]
        dnv = rin[1][...]
        x2 = rin[0][...] + p[0:1] * dnv
        rstd = _rms(x2)
        n3 = x2 * rstd
        err = n3 * p[1:2] - rin[2][...]
        dy = err * (1.0 / d)
        dx2 = _rms_bwd(dy * p[1:2], n3, rstd)
        outs[0][...] = dx2
        outs[1][...] = (dx2 * p[0:1]).astype(BF16)
        reds[0][0:1, :] += _colsum(dy * n3)
        reds[0][1:2, :] += _colsum(dx2 * dnv)
        reds[0][2:3, :] += jnp.zeros((1, d), F32) + jnp.sum(err * err)

    return _rowwise(fn, [x1, dn, target], [par], [(d, F32), (d, BF16)], [(SUBLANES, d)], name="final_fwd_bwd")


def _norm2_bwd(x1, dh2, dx2, o, gmod):
    d = x1.shape[1]

    def fn(i, rin, pin, outs, reds):
        p = pin[0][...]
        x1v, dh = rin[0][...], rin[1][...]
        rstd = _rms(x1v)
        n2 = x1v * rstd
        dx1 = rin[2][...] + _rms_bwd(dh * (p[0:1] * (1.0 + p[1:2])), n2, rstd)
        outs[0][...] = dx1
        outs[1][...] = (dx1 * p[3:4]).astype(BF16)
        reds[0][0:1, :] += _colsum(dh * n2 * (1.0 + p[1:2]))
        reds[0][1:2, :] += _colsum(dh * n2 * p[0:1])
        reds[0][2:3, :] += _colsum(dh)
        reds[0][3:4, :] += _colsum(dx1 * rin[3][...])

    return _rowwise(fn, [x1, dh2, dx2, o], [gmod], [(d, F32), (d, BF16)], [(SUBLANES, d)], name="norm2_bwd")


def _norm1_bwd(x, dh_a, dh_b, dh_c, dx1, gmod):
    d = x.shape[1]

    def fn(i, rin, pin, outs, reds):
        p = pin[0][...]
        xv = rin[0][...]
        dh = rin[1][...] + rin[2][...] + rin[3][...]
        rstd = _rms(xv)
        n1 = xv * rstd
        outs[0][...] = rin[4][...] + _rms_bwd(dh * (p[0:1] * (1.0 + p[1:2])), n1, rstd)
        reds[0][0:1, :] += _colsum(dh * n1 * (1.0 + p[1:2]))
        reds[0][1:2, :] += _colsum(dh * n1 * p[0:1])
        reds[0][2:3, :] += _colsum(dh)

    return _rowwise(fn, [x, dh_a, dh_b, dh_c, dx1], [gmod], [(d, F32)], [(SUBLANES, d)], name="norm1_bwd")


RNN_CHUNK = 512


def _shift_down(ref, base, n, j):
    v = ref[pl.ds(base, n + SUBLANES), :]
    return v[SUBLANES:] if j == 0 else pltpu.roll(v, j, 0)[SUBLANES:]


def _shift_up(ref, base, n, j, top_pad):
    v = ref[pl.ds(base + top_pad, n + SUBLANES), :]
    return v[:n] if j == 0 else pltpu.roll(v, n + SUBLANES - j, 0)[:n]


SCAN_GROUP = 128


def _scan_sizes(s):
    sizes = [s]
    while sizes[-1] > SUBLANES:
        assert sizes[-1] % SUBLANES == 0, s
        sizes.append(sizes[-1] // SUBLANES)
    return sizes


def _scan_scratch(s):
    return [pltpu.VMEM((n + 2 * SUBLANES, LANES), F32) for n in _scan_sizes(s)[1:] for _ in range(2)]


def _linear_scan(a_ref, b_ref, out_ref, a_off, s, reverse, levels):
    sizes = _scan_sizes(s)
    lv = [(a_ref, b_ref, a_off, 0)] + [(levels[2 * i], levels[2 * i + 1], 0, SUBLANES) for i in range(len(sizes) - 1)]
    zero8 = jnp.zeros((SUBLANES, LANES), F32)
    for (ar, br, _, _), n in zip(lv[1:], sizes[1:]):
        br[0:SUBLANES, :] = zero8
        br[pl.ds(n + SUBLANES, SUBLANES), :] = zero8
    order = list(range(SUBLANES - 1, -1, -1)) if reverse else list(range(SUBLANES))

    for lvl in range(len(sizes) - 1):
        ar, br, aoff, off = lv[lvl]
        m = sizes[lvl + 1]
        g = min(m, SCAN_GROUP)
        for t0 in range(0, m, g):
            acc_a = acc_b = None
            for r in order:
                sa = pl.ds(off + SUBLANES * t0 + r + aoff, g, stride=SUBLANES)
                sb = pl.ds(off + SUBLANES * t0 + r, g, stride=SUBLANES)
                a, b = ar[sa, :], br[sb, :]
                if acc_a is None:
                    acc_a, acc_b = a, b
                else:
                    acc_b = a * acc_b + b
                    acc_a = a * acc_a
            lv[lvl + 1][0][pl.ds(SUBLANES + t0, g), :] = acc_a
            lv[lvl + 1][1][pl.ds(SUBLANES + t0, g), :] = acc_b

    ar, br, _, off = lv[-1]
    n = sizes[-1]
    a, b = ar[pl.ds(off, n), :], br[pl.ds(off, n), :]
    h, rows = jnp.zeros((1, LANES), F32), [None] * n
    for j in (range(n - 1, -1, -1) if reverse else range(n)):
        h = a[j:j + 1, :] * h + b[j:j + 1, :]
        rows[j] = h
    br[pl.ds(off, n), :] = jnp.concatenate(rows, axis=0)

    for lvl in range(len(sizes) - 2, -1, -1):
        ar, br, aoff, off = lv[lvl]
        m = sizes[lvl + 1]
        up = lv[lvl + 1][1]
        dst = out_ref if lvl == 0 else br
        g = min(m, SCAN_GROUP)
        for t0 in range(0, m, g):
            h = _shift_up(up, t0, g, 1, SUBLANES) if reverse else _shift_down(up, t0, g, 1)
            for r in order:
                sa = pl.ds(off + SUBLANES * t0 + r + aoff, g, stride=SUBLANES)
                sb = pl.ds(off + SUBLANES * t0 + r, g, stride=SUBLANES)
                h = ar[sa, :] * h + br[sb, :]
                dst[sb, :] = h


def _one_minus_exp(z):
    series = -z * (1.0 + z * (0.5 + z * (1.0 / 6.0 + z * (1.0 / 24.0 + z * (1.0 / 120.0 + z * (1.0 / 720.0))))))
    return jnp.where(z > -0.1, series, 1.0 - jnp.exp(z))


def _softplus(v):
    return jnp.maximum(v, 0.0) + jnp.log(1.0 + jnp.exp(-jnp.abs(v)))


def _rnn_gates(xc, w, wa, wx, sp):
    xb = xc.astype(BF16)
    ra = _sigmoid(jnp.dot(xb, wa, preferred_element_type=F32) + w[5:6])
    ix = _sigmoid(jnp.dot(xb, wx, preferred_element_type=F32) + w[6:7])
    la = (-LRU_C) * ra * sp
    a = jnp.exp(la)
    mult = jnp.sqrt(_one_minus_exp(2.0 * la))
    return ra, ix, a, mult


def _rnn_fwd(x_rnn, keep, rp, wa_bd, wx_bd):
    s, r = x_rnn.shape
    ts = min(RNN_CHUNK, s)

    def body(x_ref, keep_ref, rp_ref, wa_ref, wx_ref, xc_ref, ra_ref, ix_ref, hs_ref, xpad, a_s, b_s, *levels):
        xpad[0:SUBLANES, :] = jnp.zeros((SUBLANES, LANES), F32)
        xpad[SUBLANES:, :] = x_ref[...]
        w = rp_ref[...]
        sp = _softplus(-w[7:8])
        wa, wx = wa_ref[0], wx_ref[0]

        def chunk(c, carry):
            base = pl.multiple_of(c * ts, ts)
            xc = w[4:5] + w[3:4] * _shift_down(xpad, base, ts, 0)
            for j in range(1, 4):
                xc = xc + w[3 - j:4 - j] * _shift_down(xpad, base, ts, j)
            ra, ix, a, mult = _rnn_gates(xc, w, wa, wx, sp)
            kp = keep_ref[pl.ds(base, ts), :]
            xc_ref[pl.ds(base, ts), :] = xc
            ra_ref[pl.ds(base, ts), :] = ra
            ix_ref[pl.ds(base, ts), :] = ix
            a_s[pl.ds(base, ts), :] = a * kp
            b_s[pl.ds(base, ts), :] = jnp.where(kp > 0.0, mult, 1.0) * (ix * xc)
            return carry

        lax.fori_loop(0, s // ts, chunk, 0)

        _linear_scan(a_s, b_s, hs_ref, 0, s, False, levels)

    col = pl.BlockSpec((s, LANES), lambda g: (0, g))
    return pl.pallas_call(
        body, name="rnn_fwd", grid=(r // LANES,),
        in_specs=[col, pl.BlockSpec((s, 1), lambda g: (0, 0)), pl.BlockSpec((SUBLANES, LANES), lambda g: (0, g)),
                  pl.BlockSpec((1, LANES, LANES), lambda g: (g, 0, 0)), pl.BlockSpec((1, LANES, LANES), lambda g: (g, 0, 0))],
        out_specs=[col] * 4,
        out_shape=[jax.ShapeDtypeStruct((s, r), F32)] * 4,
        scratch_shapes=[pltpu.VMEM((s + SUBLANES, LANES), F32), pltpu.VMEM((s, LANES), F32), pltpu.VMEM((s, LANES), F32),
                        *_scan_scratch(s)],
        compiler_params=_cparams(("arbitrary",)),
    )(x_rnn, keep, rp, wa_bd, wx_bd)


def _rnn_bwd(x_rnn, xc, ra, ix, hs, dy, keep, rp, wa_bd, wx_bd):
    s, r = x_rnn.shape
    ts = min(RNN_CHUNK, s)

    def body(x_ref, xc_ref, ra_ref, ix_ref, hs_ref, dy_ref, keep_ref, rp_ref, wa_ref, wx_ref,
             dx_ref, dwa_ref, dwx_ref, red_ref, xpad, hpad, a_s, dh_s, dxc_s, *levels):
        zero8 = jnp.zeros((SUBLANES, LANES), F32)
        xpad[0:SUBLANES, :] = zero8
        xpad[SUBLANES:, :] = x_ref[...]
        hpad[0:SUBLANES, :] = zero8
        hpad[SUBLANES:, :] = hs_ref[...]
        a_s[s:, :] = zero8
        dxc_s[s:, :] = zero8
        w = rp_ref[...]
        sp = _softplus(-w[7:8])
        wa, wx = wa_ref[0], wx_ref[0]

        def decay(c, carry):
            base = pl.multiple_of(c * ts, ts)
            a = jnp.exp((-LRU_C) * ra_ref[pl.ds(base, ts), :] * sp)
            a_s[pl.ds(base, ts), :] = a * keep_ref[pl.ds(base, ts), :]
            return carry

        lax.fori_loop(0, s // ts, decay, 0)

        _linear_scan(a_s, dy_ref, dh_s, 1, s, True, levels)

        def gates(c, carry):
            dwa, dwx, d_ba, d_bx, d_sp, d_cb = carry
            base = pl.multiple_of(c * ts, ts)
            xcv = xc_ref[pl.ds(base, ts), :]
            rav = ra_ref[pl.ds(base, ts), :]
            ixv = ix_ref[pl.ds(base, ts), :]
            kp = keep_ref[pl.ds(base, ts), :]
            dh = dh_s[pl.ds(base, ts), :]
            h_prev = _shift_down(hpad, base, ts, 1)
            la = (-LRU_C) * rav * sp
            a = jnp.exp(la)
            mult = jnp.sqrt(_one_minus_exp(2.0 * la))
            mult_eff = jnp.where(kp > 0.0, mult, 1.0)
            d_a = dh * h_prev * kp
            d_mult = dh * (ixv * xcv) * kp
            d_ix = dh * mult_eff * xcv
            d_xc = dh * mult_eff * ixv
            d_la = d_a * a - d_mult * (a * a) / mult
            d_pa = d_la * ((-LRU_C) * sp) * rav * (1.0 - rav)
            d_px = d_ix * ixv * (1.0 - ixv)
            xb = xcv.astype(BF16)
            pab, pxb = d_pa.astype(BF16), d_px.astype(BF16)
            tn = (((0,), (0,)), ((), ()))
            nt_ = (((1,), (1,)), ((), ()))
            dwa = dwa + lax.dot_general(xb, pab, tn, preferred_element_type=F32)
            dwx = dwx + lax.dot_general(xb, pxb, tn, preferred_element_type=F32)
            d_xc = d_xc + lax.dot_general(pab, wa, nt_, preferred_element_type=F32)
            d_xc = d_xc + lax.dot_general(pxb, wx, nt_, preferred_element_type=F32)
            dxc_s[pl.ds(base, ts), :] = d_xc
            return (dwa, dwx, d_ba + _colsum(d_pa), d_bx + _colsum(d_px),
                    d_sp + _colsum(d_la * ((-LRU_C) * rav)), d_cb + _colsum(d_xc))

        z1 = jnp.zeros((1, LANES), F32)
        zw = jnp.zeros((LANES, LANES), F32)
        dwa, dwx, d_ba, d_bx, d_sp, d_cb = lax.fori_loop(0, s // ts, gates, (zw, zw, z1, z1, z1, z1))
        dwa_ref[0] = dwa
        dwx_ref[0] = dwx

        def conv(c, carry):
            base = pl.multiple_of(c * ts, ts)
            d_here = dxc_s[pl.ds(base, ts), :]
            dx = w[3:4] * d_here
            for j in range(1, 4):
                dx = dx + w[3 - j:4 - j] * _shift_up(dxc_s, base, ts, j, 0)
            dx_ref[pl.ds(base, ts), :] = dx.astype(BF16)
            return tuple(carry[k] + _colsum(d_here * _shift_down(xpad, base, ts, 3 - k)) for k in range(4))

        d_w = lax.fori_loop(0, s // ts, conv, (z1, z1, z1, z1))
        d_lru = d_sp * (-_sigmoid(-w[7:8]))
        red_ref[...] = jnp.concatenate(list(d_w) + [d_cb, d_ba, d_bx, d_lru], axis=0)

    col = pl.BlockSpec((s, LANES), lambda g: (0, g))
    par = pl.BlockSpec((SUBLANES, LANES), lambda g: (0, g))
    wsp = pl.BlockSpec((1, LANES, LANES), lambda g: (g, 0, 0))
    return pl.pallas_call(
        body, name="rnn_bwd", grid=(r // LANES,),
        in_specs=[col] * 6 + [pl.BlockSpec((s, 1), lambda g: (0, 0)), par, wsp, wsp],
        out_specs=[col, wsp, wsp, par],
        out_shape=[jax.ShapeDtypeStruct((s, r), BF16), jax.ShapeDtypeStruct((r // LANES, LANES, LANES), F32),
                   jax.ShapeDtypeStruct((r // LANES, LANES, LANES), F32), jax.ShapeDtypeStruct((SUBLANES, r), F32)],
        scratch_shapes=[pltpu.VMEM((s + SUBLANES, LANES), F32), pltpu.VMEM((s + SUBLANES, LANES), F32),
                        pltpu.VMEM((s + SUBLANES, LANES), F32), pltpu.VMEM((s, LANES), F32),
                        pltpu.VMEM((s + SUBLANES, LANES), F32), *_scan_scratch(s)],
        compiler_params=_cparams(("arbitrary",)),
    )(x_rnn, xc, ra, ix, hs, dy, keep, rp, wa_bd, wx_bd)


ATT_BLOCK = 512


LOG2E = 1.4426950408889634
LN2 = 0.6931471805599453
Q_PRESCALE = ATT_SCALE * LOG2E


def _att_scores(q, kvt, krt, diagonal):
    kt_eff = jnp.where(_row(kvt.shape) < QK_NOPE, kvt, krt)
    sc = jnp.dot(q, kt_eff, preferred_element_type=F32)
    if diagonal:
        sc = jnp.where(lax.broadcasted_iota(jnp.int32, sc.shape, 1) <= lax.broadcasted_iota(jnp.int32, sc.shape, 0), sc, -jnp.inf)
    return sc


def _att_fwd(q, kv, kvt, krt):
    s = q.shape[0]
    t = min(ATT_BLOCK, s)
    nb = s // t

    def body(q_ref, kv_ref, kvt_ref, krt_ref, y_ref, lse_ref, m_s, acc_s):
        i, j = pl.program_id(1), pl.program_id(2)

        @pl.when(j == 0)
        def _():
            m_s[...] = jnp.full(m_s.shape, -jnp.inf, F32)
            acc_s[...] = jnp.zeros(acc_s.shape, F32)

        def step(diagonal):
            krt_b = krt_ref[...]
            lane = _lane((t, LANES))
            groups = [slice(c * LANES, (c + 1) * LANES) for c in range(t // LANES)]
            heads = [slice(hh * LANES, (hh + 1) * LANES) for hh in range(2)]
            scs = [_att_scores(q_ref[:, sl], kvt_ref[sl, :], krt_b, diagonal) for sl in heads]
            stats = []
            for hh in range(2):
                m_prev = m_s[hh]
                m_blk = scs[hh][:, groups[0]]
                for g in groups[1:]:
                    m_blk = jnp.maximum(m_blk, scs[hh][:, g])
                stats.append((m_prev, jnp.maximum(m_prev, jnp.max(m_blk, axis=-1, keepdims=True))))
            for hh in range(2):
                m_prev, m_new = stats[hh]
                kvb = kv_ref[:, heads[hh]]
                ones_v = jnp.where(lane < QK_NOPE, jnp.ones_like(kvb), kvb)
                p = jnp.concatenate([jnp.exp2(scs[hh][:, g] - m_new).astype(BF16) for g in groups], axis=1)
                acc_s[hh] = jnp.exp2(m_prev - m_new) * acc_s[hh] + jnp.dot(p, ones_v, preferred_element_type=F32)
                m_s[hh] = m_new

        @pl.when(j < i)
        def _():
            step(False)

        @pl.when(j == i)
        def _():
            step(True)
            lane = _lane((t, LANES))
            a0, a1 = acc_s[0], acc_s[1]
            l0, l1 = a0[:, 0:1], a1[:, 0:1]
            y_ref[...] = jnp.where(lane < V_HEAD, pltpu.roll(a0 / l0, V_HEAD, 1), a1 / l1).astype(BF16)
            lse_ref[...] = jnp.where(lane < V_HEAD, m_s[0] + jnp.log(l0) * LOG2E, m_s[1] + jnp.log(l1) * LOG2E)

    return pl.pallas_call(
        body, name="att_fwd", grid=(N_HEADS // 2, nb, nb),
        in_specs=[pl.BlockSpec((t, 2 * LANES), lambda p, i, j: (i, p)),
                  pl.BlockSpec((t, 2 * LANES), lambda p, i, j: (jnp.minimum(j, i), p)),
                  pl.BlockSpec((2 * LANES, t), lambda p, i, j: (p, jnp.minimum(j, i))),
                  pl.BlockSpec((LANES, t), lambda p, i, j: (0, jnp.minimum(j, i)))],
        out_specs=[pl.BlockSpec((t, LANES), lambda p, i, j: (i, p))] * 2,
        out_shape=[jax.ShapeDtypeStruct((s, N_HEADS * V_HEAD), BF16), jax.ShapeDtypeStruct((s, N_HEADS * V_HEAD), F32)],
        scratch_shapes=[pltpu.VMEM((2, t, LANES), F32)] * 2,
        compiler_params=_cparams(("arbitrary", "arbitrary", "arbitrary")),
    )(q, kv, kvt, krt)


def _att_bwd(q, qt, kv, kvt, kr, krt, y, lse, dy, dyt):
    s = q.shape[0]
    t = min(ATT_BLOCK, s)
    nb = s // t

    def body(q_ref, qt_ref, kv_ref, kvt_ref, kr_ref, krt_ref, y_ref, lse_ref, dy_ref, dyt_ref,
             dq_ref, dkvt_ref, dkrt_ref, dkv_s):
        p_, j, i = pl.program_id(0), pl.program_id(1), pl.program_id(2)

        @pl.when((p_ == 0) & (j == 0) & (i == 0))
        def _():
            dkrt_ref[...] = jnp.zeros(dkrt_ref.shape, F32)

        @pl.when((j == 0) & (i == 0))
        def _():
            dq_ref[...] = jnp.zeros(dq_ref.shape, F32)

        @pl.when(i == 0)
        def _():
            dkv_s[...] = jnp.zeros(dkv_s.shape, F32)

        def step(diagonal):
            lane = _lane((t, LANES))
            row = _row((LANES, t))
            krb, krt_b = kr_ref[...], krt_ref[...]
            dyv = dy_ref[...]
            yv = y_ref[...].astype(F32)
            lsev = lse_ref[...]
            dyt_b = dyt_ref[...]
            rows = pl.ds(pl.multiple_of(i * t, t), t)
            cols = pl.ds(pl.multiple_of(j * t, t), t)
            zeros_t = jnp.zeros((V_HEAD, t), BF16)
            ones_w = jnp.ones((LANES, LANES), BF16)
            groups = [slice(c * LANES, (c + 1) * LANES) for c in range(t // LANES)]
            heads = [slice(hh * LANES, (hh + 1) * LANES) for hh in range(2)]
            scs, dps, stats = [], [], []
            for hh, sl in enumerate(heads):
                kvt_b = kvt_ref[sl, :]
                scs.append(_att_scores(q_ref[:, sl], kvt_b, krt_b, diagonal))
                mine = (lane < V_HEAD) if hh == 0 else (lane >= V_HEAD)
                lse_rep = jnp.where(mine, lsev, pltpu.roll(lsev, V_HEAD, 1))
                do_pad = jnp.where(lane >= V_HEAD, pltpu.roll(dyv, V_HEAD, 1) if hh == 0 else dyv, 0.0)
                o_pad = jnp.where(lane >= V_HEAD, pltpu.roll(yv, V_HEAD, 1) if hh == 0 else yv, 0.0)
                do_ln2 = do_pad * LN2
                prod = do_ln2 * o_pad
                head_part = prod.astype(BF16)
                rest_part = (prod - head_part.astype(F32)).astype(BF16)
                delta_rep = (jnp.dot(head_part, ones_w, preferred_element_type=F32)
                             + jnp.dot(rest_part, ones_w, preferred_element_type=F32))
                dps.append(jnp.dot(do_ln2.astype(BF16), kvt_b, preferred_element_type=F32))
                stats.append((lse_rep, delta_rep))
            dkr_acc = jnp.zeros((LANES, t), F32)
            for hh, sl in enumerate(heads):
                lse_rep, delta_rep = stats[hh]
                probs, dss = [], []
                for g in groups:
                    pg = jnp.exp2(scs[hh][:, g] - lse_rep)
                    probs.append(pg.astype(BF16))
                    dss.append((pg * (dps[hh][:, g] - delta_rep)).astype(BF16))
                prob, ds = jnp.concatenate(probs, axis=1), jnp.concatenate(dss, axis=1)
                dot_pad = jnp.concatenate([zeros_t, dyt_b[hh * V_HEAD:(hh + 1) * V_HEAD, :]], axis=0)
                k_eff = jnp.where(lane < QK_NOPE, kv_ref[:, sl], krb)
                dvt = jnp.dot(dot_pad, prob, preferred_element_type=F32)
                dq_ref[rows, sl] += jnp.dot(ds, k_eff, preferred_element_type=F32)
                dkt = jnp.dot(qt_ref[sl, :], ds, preferred_element_type=F32)
                dkv_s[hh] += dvt + jnp.where(row < QK_NOPE, dkt, 0.0)
                dkr_acc = dkr_acc + jnp.where(row >= QK_NOPE, dkt, 0.0)
            dkrt_ref[:, cols] += dkr_acc

        @pl.when(i > j)
        def _():
            step(False)

        @pl.when(i == j)
        def _():
            step(True)

        @pl.when(i == nb - 1)
        def _():
            dkvt_ref[0:LANES, :] = dkv_s[0].astype(BF16)
            dkvt_ref[LANES:, :] = dkv_s[1].astype(BF16)

    qi = lambda p, j, i: (jnp.maximum(i, j), p)
    qti = lambda p, j, i: (p, jnp.maximum(i, j))
    return pl.pallas_call(
        body, name="att_bwd", grid=(N_HEADS // 2, nb, nb),
        in_specs=[pl.BlockSpec((t, 2 * LANES), qi), pl.BlockSpec((2 * LANES, t), qti),
                  pl.BlockSpec((t, 2 * LANES), lambda p, j, i: (j, p)), pl.BlockSpec((2 * LANES, t), lambda p, j, i: (p, j)),
                  pl.BlockSpec((t, LANES), lambda p, j, i: (j, 0)), pl.BlockSpec((LANES, t), lambda p, j, i: (0, j)),
                  pl.BlockSpec((t, LANES), qi), pl.BlockSpec((t, LANES), qi), pl.BlockSpec((t, LANES), qi),
                  pl.BlockSpec((LANES, t), qti)],
        out_specs=[pl.BlockSpec((s, 2 * LANES), lambda p, j, i: (0, p)),
                   pl.BlockSpec((2 * LANES, t), lambda p, j, i: (p, j)),
                   pl.BlockSpec((LANES, s), lambda p, j, i: (0, 0))],
        out_shape=[jax.ShapeDtypeStruct((s, N_HEADS * LANES), F32), jax.ShapeDtypeStruct((N_HEADS * LANES, s), BF16),
                   jax.ShapeDtypeStruct((LANES, s), F32)],
        scratch_shapes=[pltpu.VMEM((2, LANES, t), F32)],
        compiler_params=_cparams(("arbitrary", "arbitrary", "arbitrary")),
    )(q, qt, kv, kvt, kr, krt, y, lse, dy, dyt)


FFN_COLS = 256


def _ffn_conv(pad_ref, w, base, n):
    u = w[3:4] + w[2:3] * _shift_down(pad_ref, base, n, 0)
    for j in range(1, 3):
        u = u + w[2 - j:3 - j] * _shift_down(pad_ref, base, n, j)
    return u


def _ffn_act_fwd(up, fp):
    s, f2 = up.shape
    f = f2 // 2
    tc = FFN_COLS
    ts = min(RNN_CHUNK, s)
    nfb = f // tc

    def body(ug_ref, uv_ref, wg_ref, wv_ref, act_ref, gpad, vpad):
        zero8 = jnp.zeros((SUBLANES, tc), F32)
        gpad[0:SUBLANES, :] = zero8
        vpad[0:SUBLANES, :] = zero8
        gpad[SUBLANES:, :] = ug_ref[...]
        vpad[SUBLANES:, :] = uv_ref[...]
        wg, wv = wg_ref[...], wv_ref[...]

        def chunk(c, carry):
            base = pl.multiple_of(c * ts, ts)
            g = _ffn_conv(gpad, wg, base, ts)
            v = _ffn_conv(vpad, wv, base, ts)
            act_ref[pl.ds(base, ts), :] = (g * _sigmoid(g) * v).astype(BF16)
            return carry

        lax.fori_loop(0, s // ts, chunk, 0)

    return pl.pallas_call(
        body, name="ffn_act_fwd", grid=(nfb,),
        in_specs=[pl.BlockSpec((s, tc), lambda b: (0, b)), pl.BlockSpec((s, tc), lambda b: (0, b + nfb)),
                  pl.BlockSpec((SUBLANES, tc), lambda b: (0, b)), pl.BlockSpec((SUBLANES, tc), lambda b: (0, b + nfb))],
        out_specs=pl.BlockSpec((s, tc), lambda b: (0, b)),
        out_shape=jax.ShapeDtypeStruct((s, f), BF16),
        scratch_shapes=[pltpu.VMEM((s + SUBLANES, tc), F32)] * 2,
        compiler_params=_cparams(("arbitrary",)),
    )(up, up, fp, fp)


def _ffn_act_bwd(up, dact, fp):
    s, f2 = up.shape
    f = f2 // 2
    tc = FFN_COLS
    ts = min(RNN_CHUNK, s)
    nfb = f // tc

    def body(ug_ref, uv_ref, da_ref, wg_ref, wv_ref, dup_ref, red_ref, gpad, vpad, dgs, dvs):
        half = pl.program_id(1)
        wg, wv = wg_ref[...], wv_ref[...]

        @pl.when(half == 0)
        def _():
            zero8 = jnp.zeros((SUBLANES, tc), F32)
            gpad[0:SUBLANES, :] = zero8
            vpad[0:SUBLANES, :] = zero8
            gpad[SUBLANES:, :] = ug_ref[...]
            vpad[SUBLANES:, :] = uv_ref[...]
            dgs[s:, :] = zero8
            dvs[s:, :] = zero8

            def act(c, carry):
                base = pl.multiple_of(c * ts, ts)
                g = _ffn_conv(gpad, wg, base, ts)
                v = _ffn_conv(vpad, wv, base, ts)
                da = da_ref[pl.ds(base, ts), :]
                sg = _sigmoid(g)
                dgs[pl.ds(base, ts), :] = da * v * (sg * (1.0 + g * (1.0 - sg)))
                dvs[pl.ds(base, ts), :] = da * (g * sg)
                return carry

            lax.fori_loop(0, s // ts, act, 0)

        def conv_t(src, pad, w, out_ref, red_ref):
            def chunk(c, carry):
                base = pl.multiple_of(c * ts, ts)
                d_here = src[pl.ds(base, ts), :]
                dx = w[2:3] * d_here
                for j in range(1, 3):
                    dx = dx + w[2 - j:3 - j] * _shift_up(src, base, ts, j, 0)
                out_ref[pl.ds(base, ts), :] = dx.astype(BF16)
                taps = tuple(carry[k] + _colsum(d_here * _shift_down(pad, base, ts, 2 - k)) for k in range(3))
                return taps + (carry[3] + _colsum(d_here),)

            z1 = jnp.zeros((1, tc), F32)
            red = lax.fori_loop(0, s // ts, chunk, (z1, z1, z1, z1))
            red_ref[...] = jnp.concatenate(list(red) + [jnp.zeros((4, tc), F32)], axis=0)

        @pl.when(half == 0)
        def _():
            conv_t(dgs, gpad, wg, dup_ref, red_ref)

        @pl.when(half == 1)
        def _():
            conv_t(dvs, vpad, wv, dup_ref, red_ref)

    gcol = pl.BlockSpec((s, tc), lambda b, h: (0, b))
    vcol = pl.BlockSpec((s, tc), lambda b, h: (0, b + nfb))
    gpar = pl.BlockSpec((SUBLANES, tc), lambda b, h: (0, b))
    vpar = pl.BlockSpec((SUBLANES, tc), lambda b, h: (0, b + nfb))
    return pl.pallas_call(
        body, name="ffn_act_bwd", grid=(nfb, 2),
        in_specs=[gcol, vcol, gcol, gpar, vpar],
        out_specs=[pl.BlockSpec((s, tc), lambda b, h: (0, b + h * nfb)),
                   pl.BlockSpec((SUBLANES, tc), lambda b, h: (0, b + h * nfb))],
        out_shape=[jax.ShapeDtypeStruct((s, f2), BF16), jax.ShapeDtypeStruct((SUBLANES, f2), F32)],
        scratch_shapes=[pltpu.VMEM((s + SUBLANES, tc), F32)] * 4,
        compiler_params=_cparams(("arbitrary", "arbitrary")),
    )(up, up, dact, fp, fp)


def _rows8(rows, width):
    rows = [r.reshape(1, width).astype(F32) for r in rows]
    return jnp.concatenate(rows + [jnp.zeros((SUBLANES - len(rows), width), F32)], axis=0)


def _block_diag(w):
    n, b, _ = w.shape
    w = w.reshape(n // 2, 2, b, b)
    z = jnp.zeros((n // 2, b, b), w.dtype)
    top = jnp.concatenate([w[:, 0], z], axis=2)
    bot = jnp.concatenate([z, w[:, 1]], axis=2)
    return jnp.concatenate([top, bot], axis=1)


def _block_diag_t(bd):
    n, b2, _ = bd.shape
    b = b2 // 2
    return jnp.stack([bd[:, :b, :b], bd[:, b:, b:]], axis=1).reshape(2 * n, b, b)


def _local_step(x, mod, positions, target, w_in, fetch, sm, emit):
    s, d = x.shape
    o_rnn, o_mla = D_RNN, D_RNN + Q_LORA + KV_LORA + QK_ROPE
    wts = {}
    w_in_rnn = w_in[:, :o_rnn]
    w_in_mla = jnp.concatenate([w_in[:, o_rnn:o_mla], jnp.zeros((d, MLA_W - (o_mla - o_rnn)), w_in.dtype)], axis=1)
    w_in_g = w_in[:, o_mla:]
    hd = QK_NOPE + QK_ROPE
    wa_bd = _block_diag(sm["w_gate_a"]).astype(BF16)
    wx_bd = _block_diag(sm["w_gate_x"]).astype(BF16)

    pos = positions.reshape(s)
    half = QK_ROPE // 2
    inv_freq = ROPE_THETA ** (-jnp.arange(half, dtype=F32) / half)
    ang = pos.astype(F32)[:, None] * inv_freq
    cos, sin = jnp.cos(ang), jnp.sin(ang)
    rot_c = jnp.concatenate([jnp.ones((s, QK_NOPE), F32), cos, cos, jnp.ones((s, LANES - hd), F32)], axis=1)
    rot_s = jnp.concatenate([jnp.zeros((s, QK_NOPE), F32), -sin, sin, jnp.zeros((s, LANES - hd), F32)], axis=1)
    keep = (pos != 0).astype(F32).reshape(s, 1)

    gmod1 = _rows8([sm["norm1_g"], mod[1], mod[0]], d)
    gmod2 = _rows8([sm["norm2_g"], mod[4], mod[3], mod[2]], d)
    rp = jnp.concatenate([sm["conv_w"].reshape(4, D_RNN), _rows8([sm["conv_b"], sm["b_gate_a"], sm["b_gate_x"], sm["lru_param"]], D_RNN)[:4]], axis=0)
    fp = _rows8([sm["ffn_conv_w"][0], sm["ffn_conv_w"][1], sm["ffn_conv_w"][2], sm["ffn_conv_b"]], 2 * D_FF)
    ng = _rows8([jnp.concatenate([sm["q_norm_g"].reshape(-1), sm["kv_norm_g"].reshape(-1), jnp.zeros((MLA_W - Q_LORA - KV_LORA,), F32)])], MLA_W)
    fpar = _rows8([mod[5], sm["final_g"]], d)

    h = _norm_mod_fwd(x, gmod1, "norm1_fwd")
    proj_rnn = _mm(h, w_in_rnn, name="mm_in_rnn")
    proj_mla = _mm(h, w_in_mla, name="mm_in_mla")
    proj_g = _mm(h, w_in_g, name="mm_in_g")
    xc, ra, ix, hs = _rnn_fwd(proj_rnn, keep, rp, wa_bd, wx_bd)
    qn, kvn, kr = _mla_prep_fwd(proj_mla, rot_c, rot_s, ng)
    wts.update(fetch(("w_uq", "w_ukv"), kr))
    w_uq_p = jnp.pad(wts["w_uq"].reshape(Q_LORA, N_HEADS, hd), ((0, 0), (0, 0), (0, LANES - hd))).reshape(Q_LORA, N_HEADS * LANES)
    w_ukv = wts["w_ukv"]
    q_rot, q_rot_t = _rope_fwd_t(_mm(qn, w_uq_p, name="mm_uq"), rot_c, rot_s)
    kv, kvt = _mm(kvn, w_ukv, out_dtype=BF16, also_t=BF16, name="mm_ukv")
    krt = jnp.transpose(kr)
    y_mla, lse = _att_fwd(q_rot, kv, kvt, krt)
    wts.update(fetch(("w_proj_rnn", "w_proj_mla", "w_out", "w_up", "w_down"), lse))
    pr = _mm(hs, wts["w_proj_rnn"], name="mm_proj_rnn")
    pm = _mm(y_mla, wts["w_proj_mla"], name="mm_proj_mla")
    merged = _merge_fwd(pr, pm, proj_g)
    o = _mm(merged, wts["w_out"], name="mm_out")
    x1, h2 = _resid_norm_fwd(x, o, gmod2)
    up = _mm(h2, wts["w_up"], name="mm_up")
    act = _ffn_act_fwd(up, fp)
    dn = _mm(act, wts["w_down"], name="mm_down")

    dx2, ddn, red_f = _final_fwd_bwd(x1, dn, target, fpar)
    dact = _mm(ddn, wts["w_down"], tb=True, name="mm_d_act")
    tok = emit("w_down", _mm(act, ddn, ta=True, out_dtype=BF16, name="mm_dw_down"))
    dup, red_ffn = _ffn_act_bwd(up, dact, fp + tok)
    dh2 = _mm(dup, wts["w_up"], tb=True, name="mm_d_h2")
    tok = tok + emit("w_up", _mm(dup, h2, ta=True, out_dtype=BF16, name="mm_dw_up"))
    dx1, do, red_2 = _norm2_bwd(x1, dh2, dx2, o, gmod2 + tok)
    dmerged = _mm(do, wts["w_out"], tb=True, name="mm_d_merged")
    tok = tok + emit("w_out", _mm(merged, do, ta=True, out_dtype=BF16, name="mm_dw_out"))
    dpr, dpm, dg = _merge_bwd(dmerged, pr, pm, proj_g)
    dy_rnn = _mm(dpr, wts["w_proj_rnn"], tb=True, name="mm_d_yrnn")
    tok = tok + emit("w_proj_rnn", _mm(hs, dpr, ta=True, out_dtype=BF16, name="mm_dw_proj_rnn"))
    dy_mla, dy_mla_t = _mm(dpm, wts["w_proj_mla"], tb=True, also_t=BF16, name="mm_d_ymla")
    tok = tok + emit("w_proj_mla", _mm(y_mla, dpm, ta=True, out_dtype=BF16, name="mm_dw_proj_mla"))
    dq_rot, dkvt, dkrt = _att_bwd(q_rot, q_rot_t, kv, kvt, kr, krt, y_mla, lse, dy_mla, dy_mla_t)
    dq = _rope_bwd(dq_rot, rot_c, rot_s)
    dqn = _mm(dq, w_uq_p, tb=True, name="mm_d_qn")
    dw_uq_pt = _mm(dq, qn, ta=True, out_dtype=BF16, name="mm_dw_uq")
    tok = tok + emit("w_uq", dw_uq_pt.reshape(N_HEADS, LANES, Q_LORA)[:, :hd].reshape(N_HEADS * hd, Q_LORA))
    dkvn = jnp.transpose(_mm(w_ukv, dkvt, name="mm_d_kvn"))
    tok = tok + emit("w_ukv", _mm(dkvt, kvn, out_dtype=BF16, name="mm_dw_ukv"))
    dproj_mla, red_m = _mla_prep_bwd(proj_mla, dqn, dkvn, jnp.transpose(dkrt), rot_c, rot_s, ng + tok)
    dx_rnn, dwa_bd, dwx_bd, red_r = _rnn_bwd(proj_rnn, xc, ra, ix, hs, dy_rnn, keep, rp + tok, wa_bd, wx_bd)
    dw_in_t = jnp.concatenate([
        _mm(dx_rnn, h, ta=True, out_dtype=BF16, name="mm_dw_in_rnn"),
        _mm(dproj_mla, h, ta=True, out_dtype=BF16, name="mm_dw_in_mla")[:o_mla - o_rnn],
        _mm(dg, h, ta=True, out_dtype=BF16, name="mm_dw_in_g")], axis=0)
    tok = tok + emit("w_in", dw_in_t)
    dh_a = _mm(dx_rnn, w_in_rnn, tb=True, name="mm_d_h_rnn")
    dh_b = _mm(dproj_mla, w_in_mla, tb=True, name="mm_d_h_mla")
    dh_c = _mm(dg, w_in_g, tb=True, name="mm_d_h_g")
    grad_x, red_1 = _norm1_bwd(x, dh_a, dh_b, dh_c, dx1, gmod1 + tok)

    gs = {
        "norm1_g": red_1[0], "conv_w": red_r[0:4], "conv_b": red_r[4], "w_gate_a": _block_diag_t(dwa_bd),
        "b_gate_a": red_r[5], "w_gate_x": _block_diag_t(dwx_bd), "b_gate_x": red_r[6], "lru_param": red_r[7],
        "q_norm_g": red_m[0, :Q_LORA], "kv_norm_g": red_m[0, Q_LORA:Q_LORA + KV_LORA], "norm2_g": red_2[0],
        "ffn_conv_w": red_ffn[0:3], "ffn_conv_b": red_ffn[3], "final_g": red_f[0],
    }
    dmod = jnp.stack([red_1[2], red_1[1], red_2[3], red_2[2], red_2[1], red_f[1]], axis=0)
    return red_f[2, 0], grad_x, gs, dmod


MESH_IDS = pl.DeviceIdType.MESH
HBM_SPEC = pl.BlockSpec(memory_space=pltpu.HBM)


def _my_slot():
    return 4 * lax.axis_index("x") + 2 * lax.axis_index("y") + lax.axis_index("c")


def _all_gather(arrs, name):
    n = len(arrs)

    def body(*refs):
        ins, outs = refs[:n], refs[n:2 * n]
        send_sems, recv_sems, local_sems = refs[2 * n:]
        x, y, c = lax.axis_index("x"), lax.axis_index("y"), lax.axis_index("c")
        me, sibling = (x, y, c), (x, y, 1 - c)
        chips = [(1 - x, y), (x, 1 - y), (1 - x, 1 - y)]

        def slot(dev):
            return 4 * dev[0] + 2 * dev[1] + dev[2]

        def copy(a, k, block, to, src=None):
            dst = outs[a].at[slot(block)]
            return pltpu.make_async_remote_copy(
                src_ref=dst if src is None else src, dst_ref=dst, send_sem=send_sems.at[a, k], recv_sem=recv_sems.at[a, k],
                device_id=to, device_id_type=MESH_IDS)

        mine = [pltpu.make_async_copy(ins[a], outs[a].at[slot(me)], local_sems.at[a]) for a in range(n)]
        for cp in mine:
            cp.start()
        first = []
        for a in range(n):
            first.append(copy(a, 0, me, sibling, src=ins[a]))
            first += [copy(a, 1 + j, me, (*chip, c), src=ins[a]) for j, chip in enumerate(chips)]
        for cp in first:
            cp.start()
        passed = []
        for j, chip in enumerate(chips):
            for a in range(n):
                copy(a, 1 + j, (*chip, c), me).wait_recv()
                fwd = copy(a, 4 + j, (*chip, c), sibling)
                fwd.start()
                passed.append(fwd)
        for a in range(n):
            copy(a, 0, sibling, me).wait_recv()
            for j, chip in enumerate(chips):
                copy(a, 4 + j, (*chip, 1 - c), me).wait_recv()
        for cp in first + passed:
            cp.wait_send()
        for cp in mine:
            cp.wait()

    return pl.pallas_call(
        body, name=name,
        in_specs=[HBM_SPEC] * n, out_specs=[HBM_SPEC] * n,
        out_shape=[jax.ShapeDtypeStruct((N_DEV,) + a.shape, a.dtype) for a in arrs],
        scratch_shapes=[pltpu.SemaphoreType.DMA((n, 7)), pltpu.SemaphoreType.DMA((n, 7)), pltpu.SemaphoreType.DMA((n,))],
    )(*arrs)


SEM_SPEC =pl.BlockSpec(memory_space=pltpu.SEMAPHORE)
DATAFLOW = pltpu.SideEffectType.DATAFLOW_SIDE_EFFECTING
FLIPS = [(dx, dy, dc) for dx in (0, 1) for dy in (0, 1) for dc in (0, 1)][1:]


def _peer(k):
    dx, dy, dc = FLIPS[k]
    peer = (lax.axis_index("x") ^ dx, lax.axis_index("y") ^ dy, lax.axis_index("c") ^ dc)
    return peer, 4 * peer[0] + 2 * peer[1] + peer[2]


def _gather_start(shards, after, name):
    n, nf = len(shards), len(FLIPS)

    def body(*refs):
        srcs, lands = refs[:n], refs[n:2 * n]
        send_sems, recv_sems = refs[2 * n + 1:3 * n + 1], refs[3 * n + 1:4 * n + 1]
        token = refs[-1]
        me = _my_slot()
        for a in range(n):
            for k in range(nf):
                peer, _ = _peer(k)
                pltpu.make_async_remote_copy(
                    src_ref=srcs[a], dst_ref=lands[a].at[me], send_sem=send_sems[a].at[k], recv_sem=recv_sems[a].at[k],
                    device_id=peer, device_id_type=MESH_IDS).start()
        token[...] = jnp.zeros(token.shape, F32)

    land_shapes = [(N_DEV,) + a.shape for a in shards]
    sems = [pltpu.SemaphoreType.DMA((nf,))] * n
    out = pl.pallas_call(
        body, name=name,
        out_shape=(*sems, *sems, *[pltpu.HBM(a.shape, a.dtype) for a in shards],
                   *[pltpu.HBM(shp, a.dtype) for shp, a in zip(land_shapes, shards)],
                   jax.ShapeDtypeStruct((SUBLANES, LANES), F32)),
        in_specs=[HBM_SPEC] * (2 * n) + [pl.BlockSpec(memory_space=pl.ANY)],
        out_specs=(*[SEM_SPEC] * (2 * n), *[HBM_SPEC] * (2 * n), pl.BlockSpec(memory_space=pltpu.VMEM)),
        input_output_aliases={i: 2 * n + i for i in range(2 * n)},
        compiler_params=pltpu.CompilerParams(has_side_effects=DATAFLOW),
    )(*[pltpu.with_memory_space_constraint(a, pltpu.HBM) for a in shards],
      *[pltpu.with_memory_space_constraint(lax.empty(shp, a.dtype), pltpu.HBM) for shp, a in zip(land_shapes, shards)],
      after)
    return [(out[a], out[n + a], out[2 * n + a], out[3 * n + a]) for a in range(n)], out[-1]


def _gather_wait(flights, after, name):
    n, nf = len(flights), len(FLIPS)

    def body(*refs):
        send_sems, recv_sems = refs[:n], refs[n:2 * n]
        srcs, lands = refs[2 * n:3 * n], refs[3 * n:4 * n]
        for a in range(n):
            for k in range(nf):
                peer, peer_slot = _peer(k)
                cp = pltpu.make_async_remote_copy(
                    src_ref=srcs[a], dst_ref=lands[a].at[peer_slot], send_sem=send_sems[a].at[k],
                    recv_sem=recv_sems[a].at[k], device_id=peer, device_id_type=MESH_IDS)
                cp.wait_send()
                cp.wait_recv()

    srcs, lands = [f[2] for f in flights], [f[3] for f in flights]
    out = pl.pallas_call(
        body, name=name,
        out_shape=(*[pltpu.HBM(a.shape, a.dtype) for a in srcs], *[pltpu.HBM(a.shape, a.dtype) for a in lands]),
        in_specs=[SEM_SPEC] * (2 * n) + [HBM_SPEC] * (2 * n) + [pl.BlockSpec(memory_space=pl.ANY)],
        out_specs=tuple([HBM_SPEC] * (2 * n)),
        input_output_aliases={2 * n + i: i for i in range(2 * n)},
        compiler_params=pltpu.CompilerParams(has_side_effects=DATAFLOW),
    )(*[f[0] for f in flights], *[f[1] for f in flights], *srcs, *lands, after)
    return list(out[n:])


ROW_ALIGN = 16


def _span_start(slot, rows):
    return (rows * slot) // ROW_ALIGN * ROW_ALIGN


def _chunk_of(src_ref, slot, rows, span):
    if rows is None:
        return src_ref.at[slot]
    return src_ref.at[pl.ds(pl.multiple_of(_span_start(slot, rows), ROW_ALIGN), span)]


def _scatter_start(src, name, rows=None, span=None):
    def body(src_ref, land_ref, send_sems, recv_sems, src_thru, land_thru, token):
        me = _my_slot()
        for k in range(len(FLIPS)):
            peer, peer_slot = _peer(k)
            pltpu.make_async_remote_copy(
                src_ref=_chunk_of(src_ref, peer_slot, rows, span), dst_ref=land_ref.at[me], send_sem=send_sems.at[k],
                recv_sem=recv_sems.at[k], device_id=peer, device_id_type=MESH_IDS).start()
        token[...] = jnp.zeros(token.shape, F32)

    n = len(FLIPS)
    land_shape = src.shape if rows is None else (N_DEV, span, src.shape[1])
    return pl.pallas_call(
        body, name=name,
        out_shape=(pltpu.SemaphoreType.DMA((n,)), pltpu.SemaphoreType.DMA((n,)), pltpu.HBM(src.shape, src.dtype),
                   pltpu.HBM(land_shape, src.dtype), jax.ShapeDtypeStruct((SUBLANES, LANES), F32)),
        in_specs=(HBM_SPEC, HBM_SPEC),
        out_specs=(SEM_SPEC, SEM_SPEC, HBM_SPEC, HBM_SPEC, pl.BlockSpec(memory_space=pltpu.VMEM)),
        input_output_aliases={0: 2, 1: 3},
        compiler_params=pltpu.CompilerParams(has_side_effects=DATAFLOW),
    )(pltpu.with_memory_space_constraint(src, pltpu.HBM),
      pltpu.with_memory_space_constraint(lax.empty(land_shape, src.dtype), pltpu.HBM))


def _scatter_wait(send_sems, recv_sems, src_thru, land_thru, after, name, rows=None, span=None):
    def body(src_ref, land_ref, send_sems, recv_sems, after_ref, src_dead, got_ref):
        for k in range(len(FLIPS)):
            peer, peer_slot = _peer(k)
            cp = pltpu.make_async_remote_copy(
                src_ref=_chunk_of(src_ref, peer_slot, rows, span), dst_ref=land_ref.at[peer_slot], send_sem=send_sems.at[k],
                recv_sem=recv_sems.at[k], device_id=peer, device_id_type=MESH_IDS)
            cp.wait_send()
            cp.wait_recv()

    return pl.pallas_call(
        body, name=name,
        out_shape=(pltpu.HBM(src_thru.shape, src_thru.dtype), pltpu.HBM(land_thru.shape, land_thru.dtype)),
        in_specs=(HBM_SPEC, HBM_SPEC, SEM_SPEC, SEM_SPEC, pl.BlockSpec(memory_space=pl.ANY)),
        out_specs=(HBM_SPEC, HBM_SPEC), input_output_aliases={0: 0, 1: 1},
        compiler_params=pltpu.CompilerParams(has_side_effects=DATAFLOW),
    )(src_thru, land_thru, send_sems, recv_sems, after)


def _sum_sources(parts, own, name):
    k, r, c = parts.shape
    tr = r if k * r * c <= 2 * 1024 * 1024 else _pick(r, (512, 256, 128, 64, 32, 16, 8))

    def body(p_ref, own_ref, o_ref):
        me = _my_slot()
        acc = jnp.where(me == 0, own_ref[...], p_ref[0]).astype(F32)
        for s in range(1, k):
            acc = acc + jnp.where(me == s, own_ref[...], p_ref[s]).astype(F32)
        o_ref[...] = acc

    blk = pl.BlockSpec((tr, c), lambda i: (i, 0))
    return pl.pallas_call(
        body, name=name, grid=(r // tr,),
        in_specs=[pl.BlockSpec((k, tr, c), lambda i: (0, i, 0)), blk],
        out_specs=blk,
        out_shape=jax.ShapeDtypeStruct((r, c), F32),
        compiler_params=_cparams(("arbitrary",)),
    )(parts, own)


def _adamw(parts, w, m, v, name, own=None):
    k, r, c = parts.shape
    tr = r if r * c <= 256 * 1024 else _pick(r, (256, 128, 64, 32, 16, 8))

    def body(*refs):
        p_ref, w_ref, m_ref, v_ref = refs[:4]
        g_ref, d_ref, nm_ref, nv_ref = refs[-4:]

        def part(s):
            if own is None:
                return p_ref[s].astype(F32)
            return jnp.where(_my_slot() == s, refs[4][...], p_ref[s]).astype(F32)

        g = part(0)
        for s in range(1, k):
            g = g + part(s)
        m_new = ADAM_B1 * m_ref[...] + (1.0 - ADAM_B1) * g
        v_new = ADAM_B2 * v_ref[...] + (1.0 - ADAM_B2) * jnp.square(g)
        m_hat = m_new / (1.0 - ADAM_B1 ** ADAM_STEP)
        v_hat = v_new / (1.0 - ADAM_B2 ** ADAM_STEP)
        g_ref[...] = g
        d_ref[...] = -ADAM_LR * (m_hat / (jnp.sqrt(v_hat) + ADAM_EPS) + ADAM_WD * w_ref[...])
        nm_ref[...] = m_new
        nv_ref[...] = v_new

    blk = pl.BlockSpec((tr, c), lambda i: (i, 0))
    return pl.pallas_call(
        body, name=name, grid=(r // tr,),
        in_specs=[pl.BlockSpec((k, tr, c), lambda i: (0, i, 0)), blk, blk, blk] + ([] if own is None else [blk]),
        out_specs=[blk] * 4,
        out_shape=[jax.ShapeDtypeStruct((r, c), F32)] * 4,
        compiler_params=_cparams(("arbitrary",)),
    )(parts, w, m, v, *([] if own is None else [own]))


def _silu(v):
    return v * _sigmoid(v)


def _ada_fwd(c_all, w, b):
    def body(c_ref, w_ref, b_ref, o_ref):
        ca = _silu(c_ref[...]).astype(BF16)
        o_ref[...] = jnp.dot(ca, w_ref[...].astype(BF16), preferred_element_type=F32) + b_ref[...]

    return pl.pallas_call(
        body, name="ada_fwd", out_shape=jax.ShapeDtypeStruct((c_all.shape[0], w.shape[1]), F32),
        compiler_params=_cparams(),
    )(c_all, w, b)


def _ada_bwd(c_all, dmod):
    def body(c_ref, d_ref, o_ref):
        ca = _silu(c_ref[...]).astype(BF16).astype(F32)
        dm = d_ref[...].astype(BF16).astype(F32)
        acc = jnp.zeros(o_ref.shape, F32)
        for bi in range(c_all.shape[0]):
            acc = acc + jnp.transpose(ca[bi:bi + 1, :]) * dm[bi:bi + 1, :]
        o_ref[...] = acc

    return pl.pallas_call(
        body, name="ada_bwd", out_shape=jax.ShapeDtypeStruct((c_all.shape[1], dmod.shape[1]), F32),
        compiler_params=_cparams(),
    )(c_all, dmod)


COL_SHARDED = ("w_in", "w_uq", "w_ukv", "w_up")
ROW_SHARDED = ("w_proj_rnn", "w_proj_mla", "w_out", "w_down")
REPLICATED = ("b_ada", "norm1_g", "conv_b", "w_gate_a", "b_gate_a", "w_gate_x", "b_gate_x", "lru_param", "q_norm_g",
              "kv_norm_g", "norm2_g", "ffn_conv_b", "final_g")
WEIGHTS = ("w_ada", "b_ada", "norm1_g", "w_in", "conv_w", "conv_b", "w_gate_a", "b_gate_a", "w_gate_x", "b_gate_x",
           "lru_param", "q_norm_g", "w_uq", "kv_norm_g", "w_ukv", "w_proj_rnn", "w_proj_mla", "w_out", "norm2_g", "w_up",
           "ffn_conv_w", "ffn_conv_b", "w_down", "final_g")
TRANSPOSED_GRADS = COL_SHARDED
PACK_LANES = 128


def _pack(vecs, row_multiple=SUBLANES):
    flat = jnp.concatenate([v.reshape(-1).astype(F32) for v in vecs])
    pad = (-flat.shape[0]) % (PACK_LANES * row_multiple)
    return jnp.concatenate([flat, jnp.zeros((pad,), F32)]).reshape(-1, PACK_LANES)


def _unpack(packed, shapes):
    flat = packed.reshape(-1)
    out, off = [], 0
    for shp in shapes:
        size = math.prod(shp)
        out.append(flat[off:off + size].reshape(shp))
        off += size
    return out


def kernel(x, c, positions, w_ada, b_ada, norm1_g, w_in, conv_w, conv_b, w_gate_a, b_gate_a, w_gate_x, b_gate_x, lru_param, q_norm_g, w_uq, kv_norm_g, w_ukv, w_proj_rnn, w_proj_mla, w_out, norm2_g, w_up, ffn_conv_w, ffn_conv_b, w_down, final_g, loss_target, m_w_ada, m_b_ada, m_norm1_g, m_w_in, m_conv_w, m_conv_b, m_w_gate_a, m_b_gate_a, m_w_gate_x, m_b_gate_x, m_lru_param, m_q_norm_g, m_w_uq, m_kv_norm_g, m_w_ukv, m_w_proj_rnn, m_w_proj_mla, m_w_out, m_norm2_g, m_w_up, m_ffn_conv_w, m_ffn_conv_b, m_w_down, m_final_g, v_w_ada, v_b_ada, v_norm1_g, v_w_in, v_conv_w, v_conv_b, v_w_gate_a, v_b_gate_a, v_w_gate_x, v_b_gate_x, v_lru_param, v_q_norm_g, v_w_uq, v_kv_norm_g, v_w_ukv, v_w_proj_rnn, v_w_proj_mla, v_w_out, v_norm2_g, v_w_up, v_ffn_conv_w, v_ffn_conv_b, v_w_down, v_final_g):
    args = dict(locals())
    w = {n: args[n] for n in WEIGHTS}
    m = {n: args["m_" + n] for n in WEIGHTS}
    v = {n: args["v_" + n] for n in WEIGHTS}
    s, d = x.shape[1], x.shape[2]
    me = _my_slot()
    def two_d(a):
        assert a.ndim == 3 and a.shape[0] == 1, a.shape
        return a[0]

    big = COL_SHARDED + ROW_SHARDED
    shard = {n: two_d(w[n]).astype(BF16) for n in big}

    def whole(n, g):
        k, r, cc = g.shape
        return jnp.transpose(g, (1, 0, 2)).reshape(r, k * cc) if n in COL_SHARDED else g.reshape(k * r, cc)

    first = _all_gather([shard["w_in"], c, two_d(conv_w), two_d(ffn_conv_w)], "gather_first")
    c_all = first[1].reshape(N_DEV, d)
    conv_w_all = jnp.transpose(first[2], (1, 0, 2)).reshape(conv_w.shape[1], -1)
    ffn_conv_w_all = jnp.transpose(first[3], (1, 0, 2)).reshape(ffn_conv_w.shape[1], -1)

    ada_cols = w_ada.shape[2]
    b_cols = lax.dynamic_slice(b_ada, (0, me * ada_cols), (1, ada_cols))
    mod_cols = _ada_fwd(c_all, w_ada[0], b_cols)
    mod_all, = _all_gather([mod_cols], "gather_mod")

    later = ("w_uq", "w_ukv", "w_proj_rnn", "w_proj_mla", "w_out", "w_up", "w_down")
    flights, started = _gather_start([shard[n] for n in later], mod_all, "gather_start")
    flight = dict(zip(later, flights))

    def fetch(names, after):
        lands = _gather_wait([flight[n] for n in names], after, "gather_wait_" + names[0])
        return {n: whole(n, lax.dynamic_update_index_in_dim(g, shard[n], me, 0)) for n, g in zip(names, lands)}

    mod = lax.dynamic_index_in_dim(mod_all, me, axis=1, keepdims=False).reshape(6, d) + started[0, 0]

    sm = {n: w[n][0] for n in REPLICATED if n not in ("b_ada", "final_g")}
    sm["final_g"] = final_g
    sm["conv_w"] = conv_w_all
    sm["ffn_conv_w"] = ffn_conv_w_all
    in_flight, windows = {}, {}

    def emit(n, g):
        rows = g.shape[0] // N_DEV
        if rows % ROW_ALIGN == 0:
            windows[n] = (None, None)
            g = g.reshape(N_DEV, rows, g.shape[1])
        else:
            span = max(rows * k - _span_start(k, rows) for k in range(N_DEV)) + rows
            windows[n] = (rows, -(-span // ROW_ALIGN) * ROW_ALIGN)
            assert _span_start(N_DEV - 1, rows) + windows[n][1] <= g.shape[0], (n, g.shape)
        *in_flight[n], token = _scatter_start(g, "scatter_start_" + n, *windows[n])
        return token[0, 0]

    sq, grad_x, gs, dmod = _local_step(x[0], mod, positions[0], loss_target[0], whole("w_in", first[0]), fetch, sm, emit)

    small_names = [n for n in REPLICATED if n != "b_ada"] + ["conv_w", "ffn_conv_w"]
    small_shapes = [gs[n].shape for n in small_names] + [(6 * d,), (1,)]
    partial = _pack([gs[n] for n in small_names] + [dmod, sq.reshape(1)], N_DEV * SUBLANES)
    *small_flight, small_started = _scatter_start(partial.reshape(N_DEV, -1, PACK_LANES), "scatter_small_start")

    grads, deltas, new_m, new_v = {}, {}, {}, {}

    def update(n, parts, own=None):
        shp = w[n].shape
        lay = jnp.transpose if n in TRANSPOSED_GRADS else (lambda a: a)
        res = _adamw(parts, lay(two_d(w[n])), lay(two_d(m[n])), lay(two_d(v[n])), "adamw_" + n, own)
        grads[n], deltas[n], new_m[n], new_v[n] = [lay(a).reshape(shp) for a in res]

    for n in big:
        rows, span = windows[n]
        src, landed = _scatter_wait(*in_flight[n], small_started, "scatter_wait_" + n, rows, span)
        if rows is None:
            update(n, landed, lax.dynamic_index_in_dim(src, me, axis=0, keepdims=False))
        else:
            start = _span_start(me, rows)
            own = lax.dynamic_slice(src, (start, 0), (span, src.shape[1]))
            total = _sum_sources(landed, own, "sum_" + n)
            update(n, lax.dynamic_slice(total, (rows * me - start, 0), (rows, src.shape[1]))[None])

    chunks, landed = _scatter_wait(*small_flight, new_v[big[-1]], "scatter_small_wait")
    mine = _sum_sources(landed, lax.dynamic_index_in_dim(chunks, me, axis=0, keepdims=False), "sum_small")
    summed_all, dmod_all = _all_gather([mine, dmod.reshape(1, 6 * d)], "gather_small")
    summed = _unpack(summed_all, small_shapes)
    g_small = dict(zip(small_names, summed[:len(small_names)]))
    g_small["b_ada"] = summed[len(small_names)]
    loss = 0.5 * summed[-1][0] / d
    dmod_cols = lax.dynamic_slice(dmod_all.reshape(N_DEV, 6 * d), (0, me * ada_cols), (N_DEV, ada_cols))

    update("w_ada", _ada_bwd(c_all, dmod_cols)[None])

    for n in ("conv_w", "ffn_conv_w"):
        cols = w[n].shape[2]
        update(n, lax.dynamic_slice(g_small[n], (0, me * cols), (g_small[n].shape[0], cols))[None])

    rep_shapes = [w[n].shape for n in REPLICATED]
    g_rep, d_rep, m_rep, v_rep = _adamw(
        _pack([g_small[n] for n in REPLICATED])[None], _pack([w[n] for n in REPLICATED]), _pack([m[n] for n in REPLICATED]),
        _pack([v[n] for n in REPLICATED]), "adamw_replicated")
    for dst, packed in ((grads, g_rep), (deltas, d_rep), (new_m, m_rep), (new_v, v_rep)):
        dst.update(zip(REPLICATED, _unpack(packed, rep_shapes)))

    return (loss, grad_x[None], *[grads[n] for n in WEIGHTS], *[deltas[n] for n in WEIGHTS],
            *[new_m[n] for n in WEIGHTS], *[new_v[n] for n in WEIGHTS])
```

```python
import functools
import math

import jax
import jax.numpy as jnp
from jax import lax
from jax.experimental import pallas as pl
from jax.experimental.pallas import tpu as pltpu

F32 = jnp.float32
BF16 = jnp.bfloat16

N_DEV = 8
LANES = 128
SUBLANES = 8
VMEM_LIMIT = 56 * 1024 * 1024

D_RNN = 1280
Q_LORA = 384
KV_LORA = 256
QK_NOPE = 64
QK_ROPE = 32
V_HEAD = 64
N_HEADS = 16
D_FF = 2816
ROPE_THETA = 10000.0
LRU_C = 8.0
EPS = 1e-6
MLA_W = 768
ATT_SCALE = 1.0 / math.sqrt(QK_NOPE + QK_ROPE)

ADAM_LR, ADAM_B1, ADAM_B2, ADAM_EPS, ADAM_WD, ADAM_STEP = 0.001, 0.9, 0.999, 1e-08, 0.01, 10


def _cparams(sem=None):
    return pltpu.CompilerParams(dimension_semantics=sem, vmem_limit_bytes=VMEM_LIMIT)


def _pick(n, prefs):
    for p in prefs:
        if n % p == 0:
            return p
    return n


def _sigmoid(v):
    return 0.5 * jnp.tanh(0.5 * v) + 0.5


def _lane(shape):
    return lax.broadcasted_iota(jnp.int32, shape, len(shape) - 1)


def _row(shape):
    return lax.broadcasted_iota(jnp.int32, shape, len(shape) - 2)


MM_BLOCK_BYTES = 36 * 1024 * 1024


def _divisors(n):
    return [t for t in range(n, 0, -LANES) if n % t == 0] if n % LANES == 0 else [n]


HBM_BYTES_PER_US = 3.0e6
MXU_FLOPS_PER_US = 8.0e8
GRID_STEP_US = 0.35


def _mm_tiles(m, n, k, a_bytes, b_bytes, o_bytes):
    best = None
    for tm in [t for t in _divisors(m) if t <= 1024]:
        for tn in [t for t in _divisors(n) if t <= 2048]:
            for tk in _divisors(k):
                nk = k // tk
                need = 2 * (tm * tk * a_bytes + tk * tn * b_bytes + tm * tn * o_bytes) + (tm * tn * 4 if nk > 1 else 0)
                if need > MM_BLOCK_BYTES:
                    continue
                gi, gj = m // tm, n // tn
                for rows_outer in (True, False):
                    if nk > 1:
                        a_reads, b_reads = gj, gi
                    elif rows_outer:
                        a_reads, b_reads = 1, (gi if gj > 1 else 1)
                    else:
                        a_reads, b_reads = (gj if gi > 1 else 1), 1
                    traffic = m * k * a_bytes * a_reads + k * n * b_bytes * b_reads + m * n * (o_bytes + (8 * nk if nk > 1 else 0))
                    cost = max(traffic / HBM_BYTES_PER_US, 2.0 * m * n * k / MXU_FLOPS_PER_US) + gi * gj * nk * GRID_STEP_US
                    if best is None or cost < best[0]:
                        best = (cost, tm, tn, tk, rows_outer)
                break
    if best is None:
        raise ValueError((m, n, k))
    return best[1:]


def _mm(a, b, *, ta=False, tb=False, out_dtype=F32, also_t=None, name):
    (k_a, m) = a.shape if ta else a.shape[::-1]
    (n, k_b) = b.shape if tb else b.shape[::-1]
    assert k_a == k_b, (a.shape, b.shape, ta, tb)
    k = k_a
    tm, tn, tk, rows_outer = _mm_tiles(m, n, k, a.dtype.itemsize, b.dtype.itemsize, jnp.dtype(out_dtype).itemsize)
    nk = k // tk
    dims = (((0 if ta else 1,), (1 if tb else 0,)), ((), ()))
    n_out = 1 if also_t is None else 2

    def body(a_ref, b_ref, *rest):
        outs, acc = rest[:n_out], rest[n_out:]
        part = lax.dot_general(a_ref[...].astype(BF16), b_ref[...].astype(BF16), dims, preferred_element_type=F32)

        def write(val):
            outs[0][...] = val.astype(out_dtype)
            if also_t is not None:
                outs[1][...] = jnp.transpose(val).astype(also_t)

        if nk == 1:
            write(part)
            return
        acc_ref, = acc
        kk = pl.program_id(2)

        @pl.when(kk == 0)
        def _():
            acc_ref[...] = part

        @pl.when(kk > 0)
        def _():
            acc_ref[...] += part

        @pl.when(kk == nk - 1)
        def _():
            write(acc_ref[...])

    def ij(f):
        return (lambda i, j, kk: f(i, j, kk)) if rows_outer else (lambda j, i, kk: f(i, j, kk))

    a_spec = pl.BlockSpec((tk, tm), ij(lambda i, j, kk: (kk, i))) if ta else pl.BlockSpec((tm, tk), ij(lambda i, j, kk: (i, kk)))
    b_spec = pl.BlockSpec((tn, tk), ij(lambda i, j, kk: (j, kk))) if tb else pl.BlockSpec((tk, tn), ij(lambda i, j, kk: (kk, j)))
    out_specs = [pl.BlockSpec((tm, tn), ij(lambda i, j, kk: (i, j)))]
    out_shape = [jax.ShapeDtypeStruct((m, n), out_dtype)]
    if also_t is not None:
        out_specs.append(pl.BlockSpec((tn, tm), ij(lambda i, j, kk: (j, i))))
        out_shape.append(jax.ShapeDtypeStruct((n, m), also_t))
    res = pl.pallas_call(
        body, name=name,
        grid=(m // tm, n // tn, nk) if rows_outer else (n // tn, m // tm, nk),
        in_specs=[a_spec, b_spec], out_specs=out_specs, out_shape=out_shape,
        scratch_shapes=[] if nk == 1 else [pltpu.VMEM((tm, tn), F32)],
        compiler_params=_cparams(("arbitrary", "arbitrary", "arbitrary")),
    )(a, b)
    return res[0] if also_t is None else res


def _rowwise(fn, row_ins, par_ins, out_defs, red_defs, *, name, tr=256):
    s = row_ins[0].shape[0]
    tr = min(tr, s)
    nr, npar, no = len(row_ins), len(par_ins), len(out_defs)

    def body(*refs):
        rin, pin = refs[:nr], refs[nr:nr + npar]
        outs, reds = refs[nr + npar:nr + npar + no], refs[nr + npar + no:]
        i = pl.program_id(0)

        @pl.when(i == 0)
        def _():
            for r in reds:
                r[...] = jnp.zeros_like(r)

        fn(i, rin, pin, outs, reds)

    in_specs = [pl.BlockSpec((tr, a.shape[1]), lambda i: (i, 0)) for a in row_ins]
    in_specs += [pl.BlockSpec(a.shape, lambda i, nd=a.ndim: (0,) * nd) for a in par_ins]
    out_specs = [pl.BlockSpec((tr, c), lambda i: (i, 0)) for c, _ in out_defs]
    out_specs += [pl.BlockSpec(shp, lambda i: (0, 0)) for shp in red_defs]
    out_shape = [jax.ShapeDtypeStruct((s, c), dt) for c, dt in out_defs]
    out_shape += [jax.ShapeDtypeStruct(shp, F32) for shp in red_defs]
    return pl.pallas_call(
        body, name=name, grid=(s // tr,), in_specs=in_specs, out_specs=out_specs, out_shape=out_shape,
        compiler_params=_cparams(("arbitrary",)),
    )(*row_ins, *par_ins)


def _rms(v):
    return lax.rsqrt(jnp.mean(v * v, axis=-1, keepdims=True) + EPS)


def _colsum(v):
    return jnp.sum(v, axis=0, keepdims=True)


def _rms_bwd(dn, n, rstd):
    return rstd * (dn - n * jnp.mean(dn * n, axis=-1, keepdims=True))


def _norm_mod_fwd(x, gmod, name):
    def fn(i, rin, pin, outs, reds):
        xv = rin[0][...]
        p = pin[0][...]
        n = xv * _rms(xv)
        outs[0][...] = ((n * p[0:1]) * (1.0 + p[1:2]) + p[2:3]).astype(BF16)

    return _rowwise(fn, [x], [gmod], [(x.shape[1], BF16)], [], name=name)[0]


def _rope(v, rot_c, rot_s):
    half = QK_ROPE // 2
    swapped = jnp.where(_lane(v.shape) < QK_NOPE + half, pltpu.roll(v, LANES - half, 1), pltpu.roll(v, half, 1))
    return v * rot_c + swapped * rot_s


def _rope_t(dv, rot_c, rot_s):
    half = QK_ROPE // 2
    ds = dv * rot_s
    lane = _lane(dv.shape)
    swapped = jnp.where(lane < QK_NOPE + half, pltpu.roll(ds, LANES - half, 1), pltpu.roll(ds, half, 1))
    in_rope = (lane >= QK_NOPE) & (lane < QK_NOPE + QK_ROPE)
    return dv * rot_c + jnp.where(in_rope, swapped, 0.0)


def _mla_prep_fwd(proj_mla, rot_c, rot_s, ng):
    o1, o2 = Q_LORA, Q_LORA + KV_LORA

    def fn(i, rin, pin, outs, reds):
        g = pin[0][...]
        ql = rin[0][:, 0:o1]
        kl = rin[0][:, o1:o2]
        outs[0][...] = (ql * _rms(ql) * g[0:1, 0:o1]).astype(BF16)
        outs[1][...] = (kl * _rms(kl) * g[0:1, o1:o2]).astype(BF16)
        kr = pltpu.roll(rin[0][:, o2:o2 + LANES], QK_NOPE, 1)
        outs[2][...] = _rope(kr, rin[1][...], rin[2][...]).astype(BF16)

    return _rowwise(fn, [proj_mla, rot_c, rot_s], [ng], [(Q_LORA, BF16), (KV_LORA, BF16), (LANES, BF16)], [],
                    name="mla_prep_fwd")


def _mla_prep_bwd(proj_mla, dqn, dkvn, dkr, rot_c, rot_s, ng):
    o1, o2 = Q_LORA, Q_LORA + KV_LORA

    def fn(i, rin, pin, outs, reds):
        g = pin[0][...]
        ql = rin[0][:, 0:o1]
        kl = rin[0][:, o1:o2]
        rq, rk = _rms(ql), _rms(kl)
        nq, nk = ql * rq, kl * rk
        dq, dk = rin[1][...], rin[2][...]
        outs[0][:, 0:o1] = _rms_bwd(dq * g[0:1, 0:o1], nq, rq).astype(BF16)
        outs[0][:, o1:o2] = _rms_bwd(dk * g[0:1, o1:o2], nk, rk).astype(BF16)
        dkr_pre = pltpu.roll(_rope_t(rin[3][...], rin[4][...], rin[5][...]), LANES - QK_NOPE, 1)
        outs[0][:, o2:] = jnp.where(_lane(dkr_pre.shape) < QK_ROPE, dkr_pre, 0.0).astype(BF16)
        reds[0][0:1, 0:o1] += _colsum(dq * nq)
        reds[0][0:1, o1:o2] += _colsum(dk * nk)

    return _rowwise(fn, [proj_mla, dqn, dkvn, dkr, rot_c, rot_s], [ng], [(MLA_W, BF16)], [(SUBLANES, MLA_W)],
                    name="mla_prep_bwd")


def _rope_bwd(dq, rot_c, rot_s):
    def fn(i, rin, pin, outs, reds):
        c, sn = rin[1][...] * Q_PRESCALE, rin[2][...] * Q_PRESCALE
        for h in range(N_HEADS):
            sl = slice(h * LANES, (h + 1) * LANES)
            outs[0][:, sl] = _rope_t(rin[0][:, sl], c, sn).astype(BF16)

    return _rowwise(fn, [dq, rot_c, rot_s], [], [(dq.shape[1], BF16)], [], name="rope_bwd")[0]


def _rope_fwd_t(q, rot_c, rot_s):
    s, c = q.shape
    tr = min(256, s)

    def body(q_ref, c_ref, s_ref, o_ref, ot_ref):
        cc, sn = c_ref[...] * Q_PRESCALE, s_ref[...] * Q_PRESCALE
        for h in range(N_HEADS):
            sl = slice(h * LANES, (h + 1) * LANES)
            rot = _rope(q_ref[:, sl], cc, sn)
            o_ref[:, sl] = rot.astype(BF16)
            ot_ref[sl, :] = jnp.transpose(rot).astype(BF16)

    return pl.pallas_call(
        body, name="rope_fwd", grid=(s // tr,),
        in_specs=[pl.BlockSpec((tr, c), lambda i: (i, 0)), pl.BlockSpec((tr, LANES), lambda i: (i, 0)),
                  pl.BlockSpec((tr, LANES), lambda i: (i, 0))],
        out_specs=[pl.BlockSpec((tr, c), lambda i: (i, 0)), pl.BlockSpec((c, tr), lambda i: (0, i))],
        out_shape=[jax.ShapeDtypeStruct((s, c), BF16), jax.ShapeDtypeStruct((c, s), BF16)],
        compiler_params=_cparams(("arbitrary",)),
    )(q, rot_c, rot_s)


def _merge_fwd(pr, pm, proj_g):
    d = pr.shape[1]

    def fn(i, rin, pin, outs, reds):
        g_rnn, g_mla = rin[2][:, 0:d].astype(F32), rin[2][:, d:].astype(F32)
        outs[0][...] = (_sigmoid(g_rnn) * rin[0][...].astype(F32) + _sigmoid(g_mla) * rin[1][...].astype(F32)).astype(BF16)

    return _rowwise(fn, [pr, pm, proj_g], [], [(d, BF16)], [], name="merge_fwd")[0]


def _merge_bwd(dmerged, pr, pm, proj_g):
    d = pr.shape[1]

    def fn(i, rin, pin, outs, reds):
        dm = rin[0][...]
        sr, sm = _sigmoid(rin[3][:, 0:d].astype(F32)), _sigmoid(rin[3][:, d:].astype(F32))
        outs[0][...] = (dm * sr).astype(BF16)
        outs[1][...] = (dm * sm).astype(BF16)
        outs[2][:, 0:d] = (dm * rin[1][...].astype(F32) * sr * (1.0 - sr)).astype(BF16)
        outs[2][:, d:] = (dm * rin[2][...].astype(F32) * sm * (1.0 - sm)).astype(BF16)

    return _rowwise(fn, [dmerged, pr, pm, proj_g], [], [(d, BF16), (d, BF16), (2 * d, BF16)], [], name="merge_bwd")


def _resid_norm_fwd(x, o, gmod):
    d = x.shape[1]

    def fn(i, rin, pin, outs, reds):
        p = pin[0][...]
        x1 = rin[0][...] + p[3:4] * rin[1][...]
        outs[0][...] = x1
        outs[1][...] = ((x1 * _rms(x1) * p[0:1]) * (1.0 + p[1:2]) + p[2:3]).astype(BF16)

    return _rowwise(fn, [x, o], [gmod], [(d, F32), (d, BF16)], [], name="resid_norm_fwd")


def _final_fwd_bwd(x1, dn, target, par):
    d = x1.shape[1]

    def fn(i, rin, pin, outs, reds):
        p = pin[0][...]
        dnv = rin[1][...]
        x2 = rin[0][...] + p[0:1] * dnv
        rstd = _rms(x2)
        n3 = x2 * rstd
        err = n3 * p[1:2] - rin[2][...]
        dy = err * (1.0 / d)
        dx2 = _rms_bwd(dy * p[1:2], n3, rstd)
        outs[0][...] = dx2
        outs[1][...] = (dx2 * p[0:1]).astype(BF16)
        reds[0][0:1, :] += _colsum(dy * n3)
        reds[0][1:2, :] += _colsum(dx2 * dnv)
        reds[0][2:3, :] += jnp.zeros((1, d), F32) + jnp.sum(err * err)

    return _rowwise(fn, [x1, dn, target], [par], [(d, F32), (d, BF16)], [(SUBLANES, d)], name="final_fwd_bwd")


def _norm2_bwd(x1, dh2, dx2, o, gmod):
    d = x1.shape[1]

    def fn(i, rin, pin, outs, reds):
        p = pin[0][...]
        x1v, dh = rin[0][...], rin[1][...]
        rstd = _rms(x1v)
        n2 = x1v * rstd
        dx1 = rin[2][...] + _rms_bwd(dh * (p[0:1] * (1.0 + p[1:2])), n2, rstd)
        outs[0][...] = dx1
        outs[1][...] = (dx1 * p[3:4]).astype(BF16)
        reds[0][0:1, :] += _colsum(dh * n2 * (1.0 + p[1:2]))
        reds[0][1:2, :] += _colsum(dh * n2 * p[0:1])
        reds[0][2:3, :] += _colsum(dh)
        reds[0][3:4, :] += _colsum(dx1 * rin[3][...])

    return _rowwise(fn, [x1, dh2, dx2, o], [gmod], [(d, F32), (d, BF16)], [(SUBLANES, d)], name="norm2_bwd")


def _norm1_bwd(x, dh_a, dh_b, dh_c, dx1, gmod):
    d = x.shape[1]

    def fn(i, rin, pin, outs, reds):
        p = pin[0][...]
        xv = rin[0][...]
        dh = rin[1][...] + rin[2][...] + rin[3][...]
        rstd = _rms(xv)
        n1 = xv * rstd
        outs[0][...] = rin[4][...] + _rms_bwd(dh * (p[0:1] * (1.0 + p[1:2])), n1, rstd)
        reds[0][0:1, :] += _colsum(dh * n1 * (1.0 + p[1:2]))
        reds[0][1:2, :] += _colsum(dh * n1 * p[0:1])
        reds[0][2:3, :] += _colsum(dh)

    return _rowwise(fn, [x, dh_a, dh_b, dh_c, dx1], [gmod], [(d, F32)], [(SUBLANES, d)], name="norm1_bwd")


RNN_CHUNK = 512


def _shift_down(ref, base, n, j):
    v = ref[pl.ds(base, n + SUBLANES), :]
    return v[SUBLANES:] if j == 0 else pltpu.roll(v, j, 0)[SUBLANES:]


def _shift_up(ref, base, n, j, top_pad):
    v = ref[pl.ds(base + top_pad, n + SUBLANES), :]
    return v[:n] if j == 0 else pltpu.roll(v, n + SUBLANES - j, 0)[:n]


SCAN_GROUP = 128


def _scan_sizes(s):
    sizes = [s]
    while sizes[-1] > SUBLANES:
        assert sizes[-1] % SUBLANES == 0, s
        sizes.append(sizes[-1] // SUBLANES)
    return sizes


def _scan_scratch(s):
    return [pltpu.VMEM((n + 2 * SUBLANES, LANES), F32) for n in _scan_sizes(s)[1:] for _ in range(2)]


def _linear_scan(a_ref, b_ref, out_ref, a_off, s, reverse, levels):
    sizes = _scan_sizes(s)
    lv = [(a_ref, b_ref, a_off, 0)] + [(levels[2 * i], levels[2 * i + 1], 0, SUBLANES) for i in range(len(sizes) - 1)]
    zero8 = jnp.zeros((SUBLANES, LANES), F32)
    for (ar, br, _, _), n in zip(lv[1:], sizes[1:]):
        br[0:SUBLANES, :] = zero8
        br[pl.ds(n + SUBLANES, SUBLANES), :] = zero8
    order = list(range(SUBLANES - 1, -1, -1)) if reverse else list(range(SUBLANES))

    for lvl in range(len(sizes) - 1):
        ar, br, aoff, off = lv[lvl]
        m = sizes[lvl + 1]
        g = min(m, SCAN_GROUP)
        for t0 in range(0, m, g):
            acc_a = acc_b = None
            for r in order:
                sa = pl.ds(off + SUBLANES * t0 + r + aoff, g, stride=SUBLANES)
                sb = pl.ds(off + SUBLANES * t0 + r, g, stride=SUBLANES)
                a, b = ar[sa, :], br[sb, :]
                if acc_a is None:
                    acc_a, acc_b = a, b
                else:
                    acc_b = a * acc_b + b
                    acc_a = a * acc_a
            lv[lvl + 1][0][pl.ds(SUBLANES + t0, g), :] = acc_a
            lv[lvl + 1][1][pl.ds(SUBLANES + t0, g), :] = acc_b

    ar, br, _, off = lv[-1]
    n = sizes[-1]
    a, b = ar[pl.ds(off, n), :], br[pl.ds(off, n), :]
    h, rows = jnp.zeros((1, LANES), F32), [None] * n
    for j in (range(n - 1, -1, -1) if reverse else range(n)):
        h = a[j:j + 1, :] * h + b[j:j + 1, :]
        rows[j] = h
    br[pl.ds(off, n), :] = jnp.concatenate(rows, axis=0)

    for lvl in range(len(sizes) - 2, -1, -1):
        ar, br, aoff, off = lv[lvl]
        m = sizes[lvl + 1]
        up = lv[lvl + 1][1]
        dst = out_ref if lvl == 0 else br
        g = min(m, SCAN_GROUP)
        for t0 in range(0, m, g):
            h = _shift_up(up, t0, g, 1, SUBLANES) if reverse else _shift_down(up, t0, g, 1)
            for r in order:
                sa = pl.ds(off + SUBLANES * t0 + r + aoff, g, stride=SUBLANES)
                sb = pl.ds(off + SUBLANES * t0 + r, g, stride=SUBLANES)
                h = ar[sa, :] * h + br[sb, :]
                dst[sb, :] = h


def _one_minus_exp(z):
    series = -z * (1.0 + z * (0.5 + z * (1.0 / 6.0 + z * (1.0 / 24.0 + z * (1.0 / 120.0 + z * (1.0 / 720.0))))))
    return jnp.where(z > -0.1, series, 1.0 - jnp.exp(z))


def _softplus(v):
    return jnp.maximum(v, 0.0) + jnp.log(1.0 + jnp.exp(-jnp.abs(v)))


def _rnn_gates(xc, w, wa, wx, sp):
    xb = xc.astype(BF16)
    ra = _sigmoid(jnp.dot(xb, wa, preferred_element_type=F32) + w[5:6])
    ix = _sigmoid(jnp.dot(xb, wx, preferred_element_type=F32) + w[6:7])
    la = (-LRU_C) * ra * sp
    a = jnp.exp(la)
    mult = jnp.sqrt(_one_minus_exp(2.0 * la))
    return ra, ix, a, mult


def _rnn_fwd(x_rnn, keep, rp, wa_bd, wx_bd):
    s, r = x_rnn.shape
    ts = min(RNN_CHUNK, s)

    def body(x_ref, keep_ref, rp_ref, wa_ref, wx_ref, xc_ref, ra_ref, ix_ref, hs_ref, xpad, a_s, b_s, *levels):
        xpad[0:SUBLANES, :] = jnp.zeros((SUBLANES, LANES), F32)
        xpad[SUBLANES:, :] = x_ref[...]
        w = rp_ref[...]
        sp = _softplus(-w[7:8])
        wa, wx = wa_ref[0], wx_ref[0]

        def chunk(c, carry):
            base = pl.multiple_of(c * ts, ts)
            xc = w[4:5] + w[3:4] * _shift_down(xpad, base, ts, 0)
            for j in range(1, 4):
                xc = xc + w[3 - j:4 - j] * _shift_down(xpad, base, ts, j)
            ra, ix, a, mult = _rnn_gates(xc, w, wa, wx, sp)
            kp = keep_ref[pl.ds(base, ts), :]
            xc_ref[pl.ds(base, ts), :] = xc
            ra_ref[pl.ds(base, ts), :] = ra
            ix_ref[pl.ds(base, ts), :] = ix
            a_s[pl.ds(base, ts), :] = a * kp
            b_s[pl.ds(base, ts), :] = jnp.where(kp > 0.0, mult, 1.0) * (ix * xc)
            return carry

        lax.fori_loop(0, s // ts, chunk, 0)

        _linear_scan(a_s, b_s, hs_ref, 0, s, False, levels)

    col = pl.BlockSpec((s, LANES), lambda g: (0, g))
    return pl.pallas_call(
        body, name="rnn_fwd", grid=(r // LANES,),
        in_specs=[col, pl.BlockSpec((s, 1), lambda g: (0, 0)), pl.BlockSpec((SUBLANES, LANES), lambda g: (0, g)),
                  pl.BlockSpec((1, LANES, LANES), lambda g: (g, 0, 0)), pl.BlockSpec((1, LANES, LANES), lambda g: (g, 0, 0))],
        out_specs=[col] * 4,
        out_shape=[jax.ShapeDtypeStruct((s, r), F32)] * 4,
        scratch_shapes=[pltpu.VMEM((s + SUBLANES, LANES), F32), pltpu.VMEM((s, LANES), F32), pltpu.VMEM((s, LANES), F32),
                        *_scan_scratch(s)],
        compiler_params=_cparams(("arbitrary",)),
    )(x_rnn, keep, rp, wa_bd, wx_bd)


def _rnn_bwd(x_rnn, xc, ra, ix, hs, dy, keep, rp, wa_bd, wx_bd):
    s, r = x_rnn.shape
    ts = min(RNN_CHUNK, s)

    def body(x_ref, xc_ref, ra_ref, ix_ref, hs_ref, dy_ref, keep_ref, rp_ref, wa_ref, wx_ref,
             dx_ref, dwa_ref, dwx_ref, red_ref, xpad, hpad, a_s, dh_s, dxc_s, *levels):
        zero8 = jnp.zeros((SUBLANES, LANES), F32)
        xpad[0:SUBLANES, :] = zero8
        xpad[SUBLANES:, :] = x_ref[...]
        hpad[0:SUBLANES, :] = zero8
        hpad[SUBLANES:, :] = hs_ref[...]
        a_s[s:, :] = zero8
        dxc_s[s:, :] = zero8
        w = rp_ref[...]
        sp = _softplus(-w[7:8])
        wa, wx = wa_ref[0], wx_ref[0]

        def decay(c, carry):
            base = pl.multiple_of(c * ts, ts)
            a = jnp.exp((-LRU_C) * ra_ref[pl.ds(base, ts), :] * sp)
            a_s[pl.ds(base, ts), :] = a * keep_ref[pl.ds(base, ts), :]
            return carry

        lax.fori_loop(0, s // ts, decay, 0)

        _linear_scan(a_s, dy_ref, dh_s, 1, s, True, levels)

        def gates(c, carry):
            dwa, dwx, d_ba, d_bx, d_sp, d_cb = carry
            base = pl.multiple_of(c * ts, ts)
            xcv = xc_ref[pl.ds(base, ts), :]
            rav = ra_ref[pl.ds(base, ts), :]
            ixv = ix_ref[pl.ds(base, ts), :]
            kp = keep_ref[pl.ds(base, ts), :]
            dh = dh_s[pl.ds(base, ts), :]
            h_prev = _shift_down(hpad, base, ts, 1)
            la = (-LRU_C) * rav * sp
            a = jnp.exp(la)
            mult = jnp.sqrt(_one_minus_exp(2.0 * la))
            mult_eff = jnp.where(kp > 0.0, mult, 1.0)
            d_a = dh * h_prev * kp
            d_mult = dh * (ixv * xcv) * kp
            d_ix = dh * mult_eff * xcv
            d_xc = dh * mult_eff * ixv
            d_la = d_a * a - d_mult * (a * a) / mult
            d_pa = d_la * ((-LRU_C) * sp) * rav * (1.0 - rav)
            d_px = d_ix * ixv * (1.0 - ixv)
            xb = xcv.astype(BF16)
            pab, pxb = d_pa.astype(BF16), d_px.astype(BF16)
            tn = (((0,), (0,)), ((), ()))
            nt_ = (((1,), (1,)), ((), ()))
            dwa = dwa + lax.dot_general(xb, pab, tn, preferred_element_type=F32)
            dwx = dwx + lax.dot_general(xb, pxb, tn, preferred_element_type=F32)
            d_xc = d_xc + lax.dot_general(pab, wa, nt_, preferred_element_type=F32)
            d_xc = d_xc + lax.dot_general(pxb, wx, nt_, preferred_element_type=F32)
            dxc_s[pl.ds(base, ts), :] = d_xc
            return (dwa, dwx, d_ba + _colsum(d_pa), d_bx + _colsum(d_px),
                    d_sp + _colsum(d_la * ((-LRU_C) * rav)), d_cb + _colsum(d_xc))

        z1 = jnp.zeros((1, LANES), F32)
        zw = jnp.zeros((LANES, LANES), F32)
        dwa, dwx, d_ba, d_bx, d_sp, d_cb = lax.fori_loop(0, s // ts, gates, (zw, zw, z1, z1, z1, z1))
        dwa_ref[0] = dwa
        dwx_ref[0] = dwx

        def conv(c, carry):
            base = pl.multiple_of(c * ts, ts)
            d_here = dxc_s[pl.ds(base, ts), :]
            dx = w[3:4] * d_here
            for j in range(1, 4):
                dx = dx + w[3 - j:4 - j] * _shift_up(dxc_s, base, ts, j, 0)
            dx_ref[pl.ds(base, ts), :] = dx.astype(BF16)
            return tuple(carry[k] + _colsum(d_here * _shift_down(xpad, base, ts, 3 - k)) for k in range(4))

        d_w = lax.fori_loop(0, s // ts, conv, (z1, z1, z1, z1))
        d_lru = d_sp * (-_sigmoid(-w[7:8]))
        red_ref[...] = jnp.concatenate(list(d_w) + [d_cb, d_ba, d_bx, d_lru], axis=0)

    col = pl.BlockSpec((s, LANES), lambda g: (0, g))
    par = pl.BlockSpec((SUBLANES, LANES), lambda g: (0, g))
    wsp = pl.BlockSpec((1, LANES, LANES), lambda g: (g, 0, 0))
    return pl.pallas_call(
        body, name="rnn_bwd", grid=(r // LANES,),
        in_specs=[col] * 6 + [pl.BlockSpec((s, 1), lambda g: (0, 0)), par, wsp, wsp],
        out_specs=[col, wsp, wsp, par],
        out_shape=[jax.ShapeDtypeStruct((s, r), BF16), jax.ShapeDtypeStruct((r // LANES, LANES, LANES), F32),
                   jax.ShapeDtypeStruct((r // LANES, LANES, LANES), F32), jax.ShapeDtypeStruct((SUBLANES, r), F32)],
        scratch_shapes=[pltpu.VMEM((s + SUBLANES, LANES), F32), pltpu.VMEM((s + SUBLANES, LANES), F32),
                        pltpu.VMEM((s + SUBLANES, LANES), F32), pltpu.VMEM((s, LANES), F32),
                        pltpu.VMEM((s + SUBLANES, LANES), F32), *_scan_scratch(s)],
        compiler_params=_cparams(("arbitrary",)),
    )(x_rnn, xc, ra, ix, hs, dy, keep, rp, wa_bd, wx_bd)


ATT_BLOCK = 512


LOG2E = 1.4426950408889634
LN2 = 0.6931471805599453
Q_PRESCALE = ATT_SCALE * LOG2E


def _att_scores(q, kvt, krt, diagonal):
    kt_eff = jnp.where(_row(kvt.shape) < QK_NOPE, kvt, krt)
    sc = jnp.dot(q, kt_eff, preferred_element_type=F32)
    if diagonal:
        sc = jnp.where(lax.broadcasted_iota(jnp.int32, sc.shape, 1) <= lax.broadcasted_iota(jnp.int32, sc.shape, 0), sc, -jnp.inf)
    return sc


def _att_fwd(q, kv, kvt, krt):
    s = q.shape[0]
    t = min(ATT_BLOCK, s)
    nb = s // t

    def body(q_ref, kv_ref, kvt_ref, krt_ref, y_ref, lse_ref, m_s, acc_s):
        i, j = pl.program_id(1), pl.program_id(2)

        @pl.when(j == 0)
        def _():
            m_s[...] = jnp.full(m_s.shape, -jnp.inf, F32)
            acc_s[...] = jnp.zeros(acc_s.shape, F32)

        def step(diagonal):
            krt_b = krt_ref[...]
            lane = _lane((t, LANES))
            groups = [slice(c * LANES, (c + 1) * LANES) for c in range(t // LANES)]
            heads = [slice(hh * LANES, (hh + 1) * LANES) for hh in range(2)]
            scs = [_att_scores(q_ref[:, sl], kvt_ref[sl, :], krt_b, diagonal) for sl in heads]
            stats = []
            for hh in range(2):
                m_prev = m_s[hh]
                m_blk = scs[hh][:, groups[0]]
                for g in groups[1:]:
                    m_blk = jnp.maximum(m_blk, scs[hh][:, g])
                stats.append((m_prev, jnp.maximum(m_prev, jnp.max(m_blk, axis=-1, keepdims=True))))
            for hh in range(2):
                m_prev, m_new = stats[hh]
                kvb = kv_ref[:, heads[hh]]
                ones_v = jnp.where(lane < QK_NOPE, jnp.ones_like(kvb), kvb)
                p = jnp.concatenate([jnp.exp2(scs[hh][:, g] - m_new).astype(BF16) for g in groups], axis=1)
                acc_s[hh] = jnp.exp2(m_prev - m_new) * acc_s[hh] + jnp.dot(p, ones_v, preferred_element_type=F32)
                m_s[hh] = m_new

        @pl.when(j < i)
        def _():
            step(False)

        @pl.when(j == i)
        def _():
            step(True)
            lane = _lane((t, LANES))
            a0, a1 = acc_s[0], acc_s[1]
            l0, l1 = a0[:, 0:1], a1[:, 0:1]
            y_ref[...] = jnp.where(lane < V_HEAD, pltpu.roll(a0 / l0, V_HEAD, 1), a1 / l1).astype(BF16)
            lse_ref[...] = jnp.where(lane < V_HEAD, m_s[0] + jnp.log(l0) * LOG2E, m_s[1] + jnp.log(l1) * LOG2E)

    return pl.pallas_call(
        body, name="att_fwd", grid=(N_HEADS // 2, nb, nb),
        in_specs=[pl.BlockSpec((t, 2 * LANES), lambda p, i, j: (i, p)),
                  pl.BlockSpec((t, 2 * LANES), lambda p, i, j: (jnp.minimum(j, i), p)),
                  pl.BlockSpec((2 * LANES, t), lambda p, i, j: (p, jnp.minimum(j, i))),
                  pl.BlockSpec((LANES, t), lambda p, i, j: (0, jnp.minimum(j, i)))],
        out_specs=[pl.BlockSpec((t, LANES), lambda p, i, j: (i, p))] * 2,
        out_shape=[jax.ShapeDtypeStruct((s, N_HEADS * V_HEAD), BF16), jax.ShapeDtypeStruct((s, N_HEADS * V_HEAD), F32)],
        scratch_shapes=[pltpu.VMEM((2, t, LANES), F32)] * 2,
        compiler_params=_cparams(("arbitrary", "arbitrary", "arbitrary")),
    )(q, kv, kvt, krt)


def _att_bwd(q, qt, kv, kvt, kr, krt, y, lse, dy, dyt):
    s = q.shape[0]
    t = min(ATT_BLOCK, s)
    nb = s // t

    def body(q_ref, qt_ref, kv_ref, kvt_ref, kr_ref, krt_ref, y_ref, lse_ref, dy_ref, dyt_ref,
             dq_ref, dkvt_ref, dkrt_ref, dkv_s):
        p_, j, i = pl.program_id(0), pl.program_id(1), pl.program_id(2)

        @pl.when((p_ == 0) & (j == 0) & (i == 0))
        def _():
            dkrt_ref[...] = jnp.zeros(dkrt_ref.shape, F32)

        @pl.when((j == 0) & (i == 0))
        def _():
            dq_ref[...] = jnp.zeros(dq_ref.shape, F32)

        @pl.when(i == 0)
        def _():
            dkv_s[...] = jnp.zeros(dkv_s.shape, F32)

        def step(diagonal):
            lane = _lane((t, LANES))
            row = _row((LANES, t))
            krb, krt_b = kr_ref[...], krt_ref[...]
            dyv = dy_ref[...]
            yv = y_ref[...].astype(F32)
            lsev = lse_ref[...]
            dyt_b = dyt_ref[...]
            rows = pl.ds(pl.multiple_of(i * t, t), t)
            cols = pl.ds(pl.multiple_of(j * t, t), t)
            zeros_t = jnp.zeros((V_HEAD, t), BF16)
            ones_w = jnp.ones((LANES, LANES), BF16)
            groups = [slice(c * LANES, (c + 1) * LANES) for c in range(t // LANES)]
            heads = [slice(hh * LANES, (hh + 1) * LANES) for hh in range(2)]
            scs, dps, stats = [], [], []
            for hh, sl in enumerate(heads):
                kvt_b = kvt_ref[sl, :]
                scs.append(_att_scores(q_ref[:, sl], kvt_b, krt_b, diagonal))
                mine = (lane < V_HEAD) if hh == 0 else (lane >= V_HEAD)
                lse_rep = jnp.where(mine, lsev, pltpu.roll(lsev, V_HEAD, 1))
                do_pad = jnp.where(lane >= V_HEAD, pltpu.roll(dyv, V_HEAD, 1) if hh == 0 else dyv, 0.0)
                o_pad = jnp.where(lane >= V_HEAD, pltpu.roll(yv, V_HEAD, 1) if hh == 0 else yv, 0.0)
                do_ln2 = do_pad * LN2
                prod = do_ln2 * o_pad
                head_part = prod.astype(BF16)
                rest_part = (prod - head_part.astype(F32)).astype(BF16)
                delta_rep = (jnp.dot(head_part, ones_w, preferred_element_type=F32)
                             + jnp.dot(rest_part, ones_w, preferred_element_type=F32))
                dps.append(jnp.dot(do_ln2.astype(BF16), kvt_b, preferred_element_type=F32))
                stats.append((lse_rep, delta_rep))
            dkr_acc = jnp.zeros((LANES, t), F32)
            for hh, sl in enumerate(heads):
                lse_rep, delta_rep = stats[hh]
                probs, dss = [], []
                for g in groups:
                    pg = jnp.exp2(scs[hh][:, g] - lse_rep)
                    probs.append(pg.astype(BF16))
                    dss.append((pg * (dps[hh][:, g] - delta_rep)).astype(BF16))
                prob, ds = jnp.concatenate(probs, axis=1), jnp.concatenate(dss, axis=1)
                dot_pad = jnp.concatenate([zeros_t, dyt_b[hh * V_HEAD:(hh + 1) * V_HEAD, :]], axis=0)
                k_eff = jnp.where(lane < QK_NOPE, kv_ref[:, sl], krb)
                dvt = jnp.dot(dot_pad, prob, preferred_element_type=F32)
                dq_ref[rows, sl] += jnp.dot(ds, k_eff, preferred_element_type=F32)
                dkt = jnp.dot(qt_ref[sl, :], ds, preferred_element_type=F32)
                dkv_s[hh] += dvt + jnp.where(row < QK_NOPE, dkt, 0.0)
                dkr_acc = dkr_acc + jnp.where(row >= QK_NOPE, dkt, 0.0)
            dkrt_ref[:, cols] += dkr_acc

        @pl.when(i > j)
        def _():
            step(False)

        @pl.when(i == j)
        def _():
            step(True)

        @pl.when(i == nb - 1)
        def _():
            dkvt_ref[0:LANES, :] = dkv_s[0].astype(BF16)
            dkvt_ref[LANES:, :] = dkv_s[1].astype(BF16)

    qi = lambda p, j, i: (jnp.maximum(i, j), p)
    qti = lambda p, j, i: (p, jnp.maximum(i, j))
    return pl.pallas_call(
        body, name="att_bwd", grid=(N_HEADS // 2, nb, nb),
        in_specs=[pl.BlockSpec((t, 2 * LANES), qi), pl.BlockSpec((2 * LANES, t), qti),
                  pl.BlockSpec((t, 2 * LANES), lambda p, j, i: (j, p)), pl.BlockSpec((2 * LANES, t), lambda p, j, i: (p, j)),
                  pl.BlockSpec((t, LANES), lambda p, j, i: (j, 0)), pl.BlockSpec((LANES, t), lambda p, j, i: (0, j)),
                  pl.BlockSpec((t, LANES), qi), pl.BlockSpec((t, LANES), qi), pl.BlockSpec((t, LANES), qi),
                  pl.BlockSpec((LANES, t), qti)],
        out_specs=[pl.BlockSpec((s, 2 * LANES), lambda p, j, i: (0, p)),
                   pl.BlockSpec((2 * LANES, t), lambda p, j, i: (p, j)),
                   pl.BlockSpec((LANES, s), lambda p, j, i: (0, 0))],
        out_shape=[jax.ShapeDtypeStruct((s, N_HEADS * LANES), F32), jax.ShapeDtypeStruct((N_HEADS * LANES, s), BF16),
                   jax.ShapeDtypeStruct((LANES, s), F32)],
        scratch_shapes=[pltpu.VMEM((2, LANES, t), F32)],
        compiler_params=_cparams(("arbitrary", "arbitrary", "arbitrary")),
    )(q, qt, kv, kvt, kr, krt, y, lse, dy, dyt)


FFN_COLS = 256


def _ffn_conv(pad_ref, w, base, n):
    u = w[3:4] + w[2:3] * _shift_down(pad_ref, base, n, 0)
    for j in range(1, 3):
        u = u + w[2 - j:3 - j] * _shift_down(pad_ref, base, n, j)
    return u


def _ffn_act_fwd(up, fp):
    s, f2 = up.shape
    f = f2 // 2
    tc = FFN_COLS
    ts = min(RNN_CHUNK, s)
    nfb = f // tc

    def body(ug_ref, uv_ref, wg_ref, wv_ref, act_ref, gpad, vpad):
        zero8 = jnp.zeros((SUBLANES, tc), F32)
        gpad[0:SUBLANES, :] = zero8
        vpad[0:SUBLANES, :] = zero8
        gpad[SUBLANES:, :] = ug_ref[...]
        vpad[SUBLANES:, :] = uv_ref[...]
        wg, wv = wg_ref[...], wv_ref[...]

        def chunk(c, carry):
            base = pl.multiple_of(c * ts, ts)
            g = _ffn_conv(gpad, wg, base, ts)
            v = _ffn_conv(vpad, wv, base, ts)
            act_ref[pl.ds(base, ts), :] = (g * _sigmoid(g) * v).astype(BF16)
            return carry

        lax.fori_loop(0, s // ts, chunk, 0)

    return pl.pallas_call(
        body, name="ffn_act_fwd", grid=(nfb,),
        in_specs=[pl.BlockSpec((s, tc), lambda b: (0, b)), pl.BlockSpec((s, tc), lambda b: (0, b + nfb)),
                  pl.BlockSpec((SUBLANES, tc), lambda b: (0, b)), pl.BlockSpec((SUBLANES, tc), lambda b: (0, b + nfb))],
        out_specs=pl.BlockSpec((s, tc), lambda b: (0, b)),
        out_shape=jax.ShapeDtypeStruct((s, f), BF16),
        scratch_shapes=[pltpu.VMEM((s + SUBLANES, tc), F32)] * 2,
        compiler_params=_cparams(("arbitrary",)),
    )(up, up, fp, fp)


def _ffn_act_bwd(up, dact, fp):
    s, f2 = up.shape
    f = f2 // 2
    tc = FFN_COLS
    ts = min(RNN_CHUNK, s)
    nfb = f // tc

    def body(ug_ref, uv_ref, da_ref, wg_ref, wv_ref, dup_ref, red_ref, gpad, vpad, dgs, dvs):
        half = pl.program_id(1)
        wg, wv = wg_ref[...], wv_ref[...]

        @pl.when(half == 0)
        def _():
            zero8 = jnp.zeros((SUBLANES, tc), F32)
            gpad[0:SUBLANES, :] = zero8
            vpad[0:SUBLANES, :] = zero8
            gpad[SUBLANES:, :] = ug_ref[...]
            vpad[SUBLANES:, :] = uv_ref[...]
            dgs[s:, :] = zero8
            dvs[s:, :] = zero8

            def act(c, carry):
                base = pl.multiple_of(c * ts, ts)
                g = _ffn_conv(gpad, wg, base, ts)
                v = _ffn_conv(vpad, wv, base, ts)
                da = da_ref[pl.ds(base, ts), :]
                sg = _sigmoid(g)
                dgs[pl.ds(base, ts), :] = da * v * (sg * (1.0 + g * (1.0 - sg)))
                dvs[pl.ds(base, ts), :] = da * (g * sg)
                return carry

            lax.fori_loop(0, s // ts, act, 0)

        def conv_t(src, pad, w, out_ref, red_ref):
            def chunk(c, carry):
                base = pl.multiple_of(c * ts, ts)
                d_here = src[pl.ds(base, ts), :]
                dx = w[2:3] * d_here
                for j in range(1, 3):
                    dx = dx + w[2 - j:3 - j] * _shift_up(src, base, ts, j, 0)
                out_ref[pl.ds(base, ts), :] = dx.astype(BF16)
                taps = tuple(carry[k] + _colsum(d_here * _shift_down(pad, base, ts, 2 - k)) for k in range(3))
                return taps + (carry[3] + _colsum(d_here),)

            z1 = jnp.zeros((1, tc), F32)
            red = lax.fori_loop(0, s // ts, chunk, (z1, z1, z1, z1))
            red_ref[...] = jnp.concatenate(list(red) + [jnp.zeros((4, tc), F32)], axis=0)

        @pl.when(half == 0)
        def _():
            conv_t(dgs, gpad, wg, dup_ref, red_ref)

        @pl.when(half == 1)
        def _():
            conv_t(dvs, vpad, wv, dup_ref, red_ref)

    gcol = pl.BlockSpec((s, tc), lambda b, h: (0, b))
    vcol = pl.BlockSpec((s, tc), lambda b, h: (0, b + nfb))
    gpar = pl.BlockSpec((SUBLANES, tc), lambda b, h: (0, b))
    vpar = pl.BlockSpec((SUBLANES, tc), lambda b, h: (0, b + nfb))
    return pl.pallas_call(
        body, name="ffn_act_bwd", grid=(nfb, 2),
        in_specs=[gcol, vcol, gcol, gpar, vpar],
        out_specs=[pl.BlockSpec((s, tc), lambda b, h: (0, b + h * nfb)),
                   pl.BlockSpec((SUBLANES, tc), lambda b, h: (0, b + h * nfb))],
        out_shape=[jax.ShapeDtypeStruct((s, f2), BF16), jax.ShapeDtypeStruct((SUBLANES, f2), F32)],
        scratch_shapes=[pltpu.VMEM((s + SUBLANES, tc), F32)] * 4,
        compiler_params=_cparams(("arbitrary", "arbitrary")),
    )(up, up, dact, fp, fp)


def _rows8(rows, width):
    rows = [r.reshape(1, width).astype(F32) for r in rows]
    return jnp.concatenate(rows + [jnp.zeros((SUBLANES - len(rows), width), F32)], axis=0)


def _block_diag(w):
    n, b, _ = w.shape
    w = w.reshape(n // 2, 2, b, b)
    z = jnp.zeros((n // 2, b, b), w.dtype)
    top = jnp.concatenate([w[:, 0], z], axis=2)
    bot = jnp.concatenate([z, w[:, 1]], axis=2)
    return jnp.concatenate([top, bot], axis=1)


def _block_diag_t(bd):
    n, b2, _ = bd.shape
    b = b2 // 2
    return jnp.stack([bd[:, :b, :b], bd[:, b:, b:]], axis=1).reshape(2 * n, b, b)


def _local_step(x, mod, positions, target, w_in, fetch, sm, emit):
    s, d = x.shape
    o_rnn, o_mla = D_RNN, D_RNN + Q_LORA + KV_LORA + QK_ROPE
    wts = {}
    w_in_rnn = w_in[:, :o_rnn]
    w_in_mla = jnp.concatenate([w_in[:, o_rnn:o_mla], jnp.zeros((d, MLA_W - (o_mla - o_rnn)), w_in.dtype)], axis=1)
    w_in_g = w_in[:, o_mla:]
    hd = QK_NOPE + QK_ROPE
    wa_bd = _block_diag(sm["w_gate_a"]).astype(BF16)
    wx_bd = _block_diag(sm["w_gate_x"]).astype(BF16)

    pos = positions.reshape(s)
    half = QK_ROPE // 2
    inv_freq = ROPE_THETA ** (-jnp.arange(half, dtype=F32) / half)
    ang = pos.astype(F32)[:, None] * inv_freq
    cos, sin = jnp.cos(ang), jnp.sin(ang)
    rot_c = jnp.concatenate([jnp.ones((s, QK_NOPE), F32), cos, cos, jnp.ones((s, LANES - hd), F32)], axis=1)
    rot_s = jnp.concatenate([jnp.zeros((s, QK_NOPE), F32), -sin, sin, jnp.zeros((s, LANES - hd), F32)], axis=1)
    keep = (pos != 0).astype(F32).reshape(s, 1)

    gmod1 = _rows8([sm["norm1_g"], mod[1], mod[0]], d)
    gmod2 = _rows8([sm["norm2_g"], mod[4], mod[3], mod[2]], d)
    rp = jnp.concatenate([sm["conv_w"].reshape(4, D_RNN), _rows8([sm["conv_b"], sm["b_gate_a"], sm["b_gate_x"], sm["lru_param"]], D_RNN)[:4]], axis=0)
    fp = _rows8([sm["ffn_conv_w"][0], sm["ffn_conv_w"][1], sm["ffn_conv_w"][2], sm["ffn_conv_b"]], 2 * D_FF)
    ng = _rows8([jnp.concatenate([sm["q_norm_g"].reshape(-1), sm["kv_norm_g"].reshape(-1), jnp.zeros((MLA_W - Q_LORA - KV_LORA,), F32)])], MLA_W)
    fpar = _rows8([mod[5], sm["final_g"]], d)

    h = _norm_mod_fwd(x, gmod1, "norm1_fwd")
    proj_rnn = _mm(h, w_in_rnn, name="mm_in_rnn")
    proj_mla = _mm(h, w_in_mla, name="mm_in_mla")
    proj_g = _mm(h, w_in_g, out_dtype=BF16, name="mm_in_g")
    xc, ra, ix, hs = _rnn_fwd(proj_rnn, keep, rp, wa_bd, wx_bd)
    qn, kvn, kr = _mla_prep_fwd(proj_mla, rot_c, rot_s, ng)
    wts.update(fetch(("w_uq", "w_ukv"), kr))
    w_uq_pt = jnp.pad(wts["w_uq"].reshape(N_HEADS, hd, Q_LORA), ((0, 0), (0, LANES - hd), (0, 0))).reshape(N_HEADS * LANES, Q_LORA)
    w_ukv_t = wts["w_ukv"]
    q_rot, q_rot_t = _rope_fwd_t(_mm(qn, w_uq_pt, tb=True, name="mm_uq"), rot_c, rot_s)
    kv, kvt = _mm(kvn, w_ukv_t, tb=True, out_dtype=BF16, also_t=BF16, name="mm_ukv")
    krt = jnp.transpose(kr)
    y_mla, lse = _att_fwd(q_rot, kv, kvt, krt)
    wts.update(fetch(("w_proj_rnn", "w_proj_mla", "w_out", "w_up", "w_down"), lse))
    pr = _mm(hs, wts["w_proj_rnn"], out_dtype=BF16, name="mm_proj_rnn")
    pm = _mm(y_mla, wts["w_proj_mla"], out_dtype=BF16, name="mm_proj_mla")
    merged = _merge_fwd(pr, pm, proj_g)
    o = _mm(merged, wts["w_out"], name="mm_out")
    x1, h2 = _resid_norm_fwd(x, o, gmod2)
    up = _mm(h2, wts["w_up"], tb=True, name="mm_up")
    act = _ffn_act_fwd(up, fp)
    dn = _mm(act, wts["w_down"], name="mm_down")

    dx2, ddn, red_f = _final_fwd_bwd(x1, dn, target, fpar)
    dact = _mm(ddn, wts["w_down"], tb=True, name="mm_d_act")
    tok = emit("w_down", _mm(act, ddn, ta=True, out_dtype=BF16, name="mm_dw_down"))
    dup, red_ffn = _ffn_act_bwd(up, dact, fp + tok)
    dh2 = _mm(dup, wts["w_up"], name="mm_d_h2")
    tok = tok + emit("w_up", _mm(dup, h2, ta=True, out_dtype=BF16, name="mm_dw_up"))
    dx1, do, red_2 = _norm2_bwd(x1, dh2, dx2, o, gmod2 + tok)
    dmerged = _mm(do, wts["w_out"], tb=True, name="mm_d_merged")
    tok = tok + emit("w_out", _mm(merged, do, ta=True, out_dtype=BF16, name="mm_dw_out"))
    dpr, dpm, dg = _merge_bwd(dmerged, pr, pm, proj_g)
    dy_rnn = _mm(dpr, wts["w_proj_rnn"], tb=True, name="mm_d_yrnn")
    tok = tok + emit("w_proj_rnn", _mm(hs, dpr, ta=True, out_dtype=BF16, name="mm_dw_proj_rnn"))
    dy_mla, dy_mla_t = _mm(dpm, wts["w_proj_mla"], tb=True, also_t=BF16, name="mm_d_ymla")
    tok = tok + emit("w_proj_mla", _mm(y_mla, dpm, ta=True, out_dtype=BF16, name="mm_dw_proj_mla"))
    dq_rot, dkvt, dkrt = _att_bwd(q_rot, q_rot_t, kv, kvt, kr, krt, y_mla, lse, dy_mla, dy_mla_t)
    dq = _rope_bwd(dq_rot, rot_c, rot_s)
    dqn = _mm(dq, w_uq_pt, name="mm_d_qn")
    dw_uq_pt = _mm(dq, qn, ta=True, out_dtype=BF16, name="mm_dw_uq")
    tok = tok + emit("w_uq", dw_uq_pt.reshape(N_HEADS, LANES, Q_LORA)[:, :hd].reshape(N_HEADS * hd, Q_LORA))
    dkvn = jnp.transpose(_mm(w_ukv_t, dkvt, ta=True, name="mm_d_kvn"))
    tok = tok + emit("w_ukv", _mm(dkvt, kvn, out_dtype=BF16, name="mm_dw_ukv"))
    dproj_mla, red_m = _mla_prep_bwd(proj_mla, dqn, dkvn, jnp.transpose(dkrt), rot_c, rot_s, ng + tok)
    dx_rnn, dwa_bd, dwx_bd, red_r = _rnn_bwd(proj_rnn, xc, ra, ix, hs, dy_rnn, keep, rp + tok, wa_bd, wx_bd)
    dw_in_t = jnp.concatenate([
        _mm(dx_rnn, h, ta=True, out_dtype=BF16, name="mm_dw_in_rnn"),
        _mm(dproj_mla, h, ta=True, out_dtype=BF16, name="mm_dw_in_mla")[:o_mla - o_rnn],
        _mm(dg, h, ta=True, out_dtype=BF16, name="mm_dw_in_g")], axis=0)
    tok = tok + emit("w_in", dw_in_t)
    dh_a = _mm(dx_rnn, w_in_rnn, tb=True, name="mm_d_h_rnn")
    dh_b = _mm(dproj_mla, w_in_mla, tb=True, name="mm_d_h_mla")
    dh_c = _mm(dg, w_in_g, tb=True, name="mm_d_h_g")
    grad_x, red_1 = _norm1_bwd(x, dh_a, dh_b, dh_c, dx1, gmod1 + tok)

    gs = {
        "norm1_g": red_1[0], "conv_w": red_r[0:4], "conv_b": red_r[4], "w_gate_a": _block_diag_t(dwa_bd),
        "b_gate_a": red_r[5], "w_gate_x": _block_diag_t(dwx_bd), "b_gate_x": red_r[6], "lru_param": red_r[7],
        "q_norm_g": red_m[0, :Q_LORA], "kv_norm_g": red_m[0, Q_LORA:Q_LORA + KV_LORA], "norm2_g": red_2[0],
        "ffn_conv_w": red_ffn[0:3], "ffn_conv_b": red_ffn[3], "final_g": red_f[0],
    }
    dmod = jnp.stack([red_1[2], red_1[1], red_2[3], red_2[2], red_2[1], red_f[1]], axis=0)
    return red_f[2, 0], grad_x, gs, dmod


MESH_IDS = pl.DeviceIdType.MESH
HBM_SPEC = pl.BlockSpec(memory_space=pltpu.HBM)


def _my_slot():
    return 4 * lax.axis_index("x") + 2 * lax.axis_index("y") + lax.axis_index("c")


def _all_gather(arrs, name):
    n = len(arrs)

    def body(*refs):
        ins, outs = refs[:n], refs[n:2 * n]
        send_sems, recv_sems, local_sems = refs[2 * n:]
        x, y, c = lax.axis_index("x"), lax.axis_index("y"), lax.axis_index("c")
        me, sibling = (x, y, c), (x, y, 1 - c)
        chips = [(1 - x, y), (x, 1 - y), (1 - x, 1 - y)]

        def slot(dev):
            return 4 * dev[0] + 2 * dev[1] + dev[2]

        def copy(a, k, block, to, src=None):
            dst = outs[a].at[slot(block)]
            return pltpu.make_async_remote_copy(
                src_ref=dst if src is None else src, dst_ref=dst, send_sem=send_sems.at[a, k], recv_sem=recv_sems.at[a, k],
                device_id=to, device_id_type=MESH_IDS)

        mine = [pltpu.make_async_copy(ins[a], outs[a].at[slot(me)], local_sems.at[a]) for a in range(n)]
        for cp in mine:
            cp.start()
        first = []
        for a in range(n):
            first.append(copy(a, 0, me, sibling, src=ins[a]))
            first += [copy(a, 1 + j, me, (*chip, c), src=ins[a]) for j, chip in enumerate(chips)]
        for cp in first:
            cp.start()
        passed = []
        for j, chip in enumerate(chips):
            for a in range(n):
                copy(a, 1 + j, (*chip, c), me).wait_recv()
                fwd = copy(a, 4 + j, (*chip, c), sibling)
                fwd.start()
                passed.append(fwd)
        for a in range(n):
            copy(a, 0, sibling, me).wait_recv()
            for j, chip in enumerate(chips):
                copy(a, 4 + j, (*chip, 1 - c), me).wait_recv()
        for cp in first + passed:
            cp.wait_send()
        for cp in mine:
            cp.wait()

    return pl.pallas_call(
        body, name=name,
        in_specs=[HBM_SPEC] * n, out_specs=[HBM_SPEC] * n,
        out_shape=[jax.ShapeDtypeStruct((N_DEV,) + a.shape, a.dtype) for a in arrs],
        scratch_shapes=[pltpu.SemaphoreType.DMA((n, 7)), pltpu.SemaphoreType.DMA((n, 7)), pltpu.SemaphoreType.DMA((n,))],
    )(*arrs)


SEM_SPEC =pl.BlockSpec(memory_space=pltpu.SEMAPHORE)
DATAFLOW = pltpu.SideEffectType.DATAFLOW_SIDE_EFFECTING
FLIPS = [(dx, dy, dc) for dx in (0, 1) for dy in (0, 1) for dc in (0, 1)][1:]


def _peer(k):
    dx, dy, dc = FLIPS[k]
    peer = (lax.axis_index("x") ^ dx, lax.axis_index("y") ^ dy, lax.axis_index("c") ^ dc)
    return peer, 4 * peer[0] + 2 * peer[1] + peer[2]


def _gather_start(shards, after, name):
    n, nf = len(shards), len(FLIPS)

    def body(*refs):
        srcs, lands = refs[:n], refs[n:2 * n]
        send_sems, recv_sems = refs[2 * n + 1:3 * n + 1], refs[3 * n + 1:4 * n + 1]
        token = refs[-1]
        me = _my_slot()
        for a in range(n):
            for k in range(nf):
                peer, _ = _peer(k)
                pltpu.make_async_remote_copy(
                    src_ref=srcs[a], dst_ref=lands[a].at[me], send_sem=send_sems[a].at[k], recv_sem=recv_sems[a].at[k],
                    device_id=peer, device_id_type=MESH_IDS).start()
        token[...] = jnp.zeros(token.shape, F32)

    land_shapes = [(N_DEV,) + a.shape for a in shards]
    sems = [pltpu.SemaphoreType.DMA((nf,))] * n
    out = pl.pallas_call(
        body, name=name,
        out_shape=(*sems, *sems, *[pltpu.HBM(a.shape, a.dtype) for a in shards],
                   *[pltpu.HBM(shp, a.dtype) for shp, a in zip(land_shapes, shards)],
                   jax.ShapeDtypeStruct((SUBLANES, LANES), F32)),
        in_specs=[HBM_SPEC] * (2 * n) + [pl.BlockSpec(memory_space=pl.ANY)],
        out_specs=(*[SEM_SPEC] * (2 * n), *[HBM_SPEC] * (2 * n), pl.BlockSpec(memory_space=pltpu.VMEM)),
        input_output_aliases={i: 2 * n + i for i in range(2 * n)},
        compiler_params=pltpu.CompilerParams(has_side_effects=DATAFLOW),
    )(*[pltpu.with_memory_space_constraint(a, pltpu.HBM) for a in shards],
      *[pltpu.with_memory_space_constraint(lax.empty(shp, a.dtype), pltpu.HBM) for shp, a in zip(land_shapes, shards)],
      after)
    return [(out[a], out[n + a], out[2 * n + a], out[3 * n + a]) for a in range(n)], out[-1]


def _gather_wait(flights, after, name):
    n, nf = len(flights), len(FLIPS)

    def body(*refs):
        send_sems, recv_sems = refs[:n], refs[n:2 * n]
        srcs, lands = refs[2 * n:3 * n], refs[3 * n:4 * n]
        for a in range(n):
            for k in range(nf):
                peer, peer_slot = _peer(k)
                cp = pltpu.make_async_remote_copy(
                    src_ref=srcs[a], dst_ref=lands[a].at[peer_slot], send_sem=send_sems[a].at[k],
                    recv_sem=recv_sems[a].at[k], device_id=peer, device_id_type=MESH_IDS)
                cp.wait_send()
                cp.wait_recv()

    srcs, lands = [f[2] for f in flights], [f[3] for f in flights]
    out = pl.pallas_call(
        body, name=name,
        out_shape=(*[pltpu.HBM(a.shape, a.dtype) for a in srcs], *[pltpu.HBM(a.shape, a.dtype) for a in lands]),
        in_specs=[SEM_SPEC] * (2 * n) + [HBM_SPEC] * (2 * n) + [pl.BlockSpec(memory_space=pl.ANY)],
        out_specs=tuple([HBM_SPEC] * (2 * n)),
        input_output_aliases={2 * n + i: i for i in range(2 * n)},
        compiler_params=pltpu.CompilerParams(has_side_effects=DATAFLOW),
    )(*[f[0] for f in flights], *[f[1] for f in flights], *srcs, *lands, after)
    return list(out[n:])


ROW_ALIGN = 16


def _span_start(slot, rows):
    return (rows * slot) // ROW_ALIGN * ROW_ALIGN


def _chunk_of(src_ref, slot, rows, span):
    if rows is None:
        return src_ref.at[slot]
    return src_ref.at[pl.ds(pl.multiple_of(_span_start(slot, rows), ROW_ALIGN), span)]


def _scatter_start(src, name, rows=None, span=None):
    def body(src_ref, land_ref, send_sems, recv_sems, src_thru, land_thru, token):
        me = _my_slot()
        for k in range(len(FLIPS)):
            peer, peer_slot = _peer(k)
            pltpu.make_async_remote_copy(
                src_ref=_chunk_of(src_ref, peer_slot, rows, span), dst_ref=land_ref.at[me], send_sem=send_sems.at[k],
                recv_sem=recv_sems.at[k], device_id=peer, device_id_type=MESH_IDS).start()
        token[...] = jnp.zeros(token.shape, F32)

    n = len(FLIPS)
    land_shape = src.shape if rows is None else (N_DEV, span, src.shape[1])
    return pl.pallas_call(
        body, name=name,
        out_shape=(pltpu.SemaphoreType.DMA((n,)), pltpu.SemaphoreType.DMA((n,)), pltpu.HBM(src.shape, src.dtype),
                   pltpu.HBM(land_shape, src.dtype), jax.ShapeDtypeStruct((SUBLANES, LANES), F32)),
        in_specs=(HBM_SPEC, HBM_SPEC),
        out_specs=(SEM_SPEC, SEM_SPEC, HBM_SPEC, HBM_SPEC, pl.BlockSpec(memory_space=pltpu.VMEM)),
        input_output_aliases={0: 2, 1: 3},
        compiler_params=pltpu.CompilerParams(has_side_effects=DATAFLOW),
    )(pltpu.with_memory_space_constraint(src, pltpu.HBM),
      pltpu.with_memory_space_constraint(lax.empty(land_shape, src.dtype), pltpu.HBM))


def _scatter_wait(send_sems, recv_sems, src_thru, land_thru, after, name, rows=None, span=None):
    def body(src_ref, land_ref, send_sems, recv_sems, after_ref, src_dead, got_ref):
        for k in range(len(FLIPS)):
            peer, peer_slot = _peer(k)
            cp = pltpu.make_async_remote_copy(
                src_ref=_chunk_of(src_ref, peer_slot, rows, span), dst_ref=land_ref.at[peer_slot], send_sem=send_sems.at[k],
                recv_sem=recv_sems.at[k], device_id=peer, device_id_type=MESH_IDS)
            cp.wait_send()
            cp.wait_recv()

    return pl.pallas_call(
        body, name=name,
        out_shape=(pltpu.HBM(src_thru.shape, src_thru.dtype), pltpu.HBM(land_thru.shape, land_thru.dtype)),
        in_specs=(HBM_SPEC, HBM_SPEC, SEM_SPEC, SEM_SPEC, pl.BlockSpec(memory_space=pl.ANY)),
        out_specs=(HBM_SPEC, HBM_SPEC), input_output_aliases={0: 0, 1: 1},
        compiler_params=pltpu.CompilerParams(has_side_effects=DATAFLOW),
    )(src_thru, land_thru, send_sems, recv_sems, after)


def _sum_sources(parts, own, name):
    k, r, c = parts.shape
    tr = r if k * r * c <= 2 * 1024 * 1024 else _pick(r, (512, 256, 128, 64, 32, 16, 8))

    def body(p_ref, own_ref, o_ref):
        me = _my_slot()
        acc = jnp.where(me == 0, own_ref[...], p_ref[0]).astype(F32)
        for s in range(1, k):
            acc = acc + jnp.where(me == s, own_ref[...], p_ref[s]).astype(F32)
        o_ref[...] = acc

    blk = pl.BlockSpec((tr, c), lambda i: (i, 0))
    return pl.pallas_call(
        body, name=name, grid=(r // tr,),
        in_specs=[pl.BlockSpec((k, tr, c), lambda i: (0, i, 0)), blk],
        out_specs=blk,
        out_shape=jax.ShapeDtypeStruct((r, c), F32),
        compiler_params=_cparams(("arbitrary",)),
    )(parts, own)


def _adamw(parts, w, m, v, name, own=None):
    k, r, c = parts.shape
    tr = r if r * c <= 256 * 1024 else _pick(r, (256, 128, 64, 32, 16, 8))

    def body(*refs):
        p_ref, w_ref, m_ref, v_ref = refs[:4]
        g_ref, d_ref, nm_ref, nv_ref = refs[-4:]

        def part(s):
            if own is None:
                return p_ref[s].astype(F32)
            return jnp.where(_my_slot() == s, refs[4][...], p_ref[s]).astype(F32)

        g = part(0)
        for s in range(1, k):
            g = g + part(s)
        m_new = ADAM_B1 * m_ref[...] + (1.0 - ADAM_B1) * g
        v_new = ADAM_B2 * v_ref[...] + (1.0 - ADAM_B2) * jnp.square(g)
        m_hat = m_new / (1.0 - ADAM_B1 ** ADAM_STEP)
        v_hat = v_new / (1.0 - ADAM_B2 ** ADAM_STEP)
        g_ref[...] = g
        d_ref[...] = -ADAM_LR * (m_hat / (jnp.sqrt(v_hat) + ADAM_EPS) + ADAM_WD * w_ref[...])
        nm_ref[...] = m_new
        nv_ref[...] = v_new

    blk = pl.BlockSpec((tr, c), lambda i: (i, 0))
    return pl.pallas_call(
        body, name=name, grid=(r // tr,),
        in_specs=[pl.BlockSpec((k, tr, c), lambda i: (0, i, 0)), blk, blk, blk] + ([] if own is None else [blk]),
        out_specs=[blk] * 4,
        out_shape=[jax.ShapeDtypeStruct((r, c), F32)] * 4,
        compiler_params=_cparams(("arbitrary",)),
    )(parts, w, m, v, *([] if own is None else [own]))


def _silu(v):
    return v * _sigmoid(v)


def _ada_fwd(c_all, w, b):
    def body(c_ref, w_ref, b_ref, o_ref):
        ca = _silu(c_ref[...]).astype(BF16)
        o_ref[...] = jnp.dot(ca, w_ref[...].astype(BF16), preferred_element_type=F32) + b_ref[...]

    return pl.pallas_call(
        body, name="ada_fwd", out_shape=jax.ShapeDtypeStruct((c_all.shape[0], w.shape[1]), F32),
        compiler_params=_cparams(),
    )(c_all, w, b)


def _ada_bwd(c_all, dmod):
    def body(c_ref, d_ref, o_ref):
        ca = _silu(c_ref[...]).astype(BF16).astype(F32)
        dm = d_ref[...].astype(BF16).astype(F32)
        acc = jnp.zeros(o_ref.shape, F32)
        for bi in range(c_all.shape[0]):
            acc = acc + jnp.transpose(ca[bi:bi + 1, :]) * dm[bi:bi + 1, :]
        o_ref[...] = acc

    return pl.pallas_call(
        body, name="ada_bwd", out_shape=jax.ShapeDtypeStruct((c_all.shape[1], dmod.shape[1]), F32),
        compiler_params=_cparams(),
    )(c_all, dmod)


COL_SHARDED = ("w_in", "w_uq", "w_ukv", "w_up")
ROW_SHARDED = ("w_proj_rnn", "w_proj_mla", "w_out", "w_down")
REPLICATED = ("b_ada", "norm1_g", "conv_b", "w_gate_a", "b_gate_a", "w_gate_x", "b_gate_x", "lru_param", "q_norm_g",
              "kv_norm_g", "norm2_g", "ffn_conv_b", "final_g")
WEIGHTS = ("w_ada", "b_ada", "norm1_g", "w_in", "conv_w", "conv_b", "w_gate_a", "b_gate_a", "w_gate_x", "b_gate_x",
           "lru_param", "q_norm_g", "w_uq", "kv_norm_g", "w_ukv", "w_proj_rnn", "w_proj_mla", "w_out", "norm2_g", "w_up",
           "ffn_conv_w", "ffn_conv_b", "w_down", "final_g")
TRANSPOSED_GRADS = COL_SHARDED
GATHERED_TRANSPOSED = ("w_uq", "w_ukv", "w_up")
PACK_LANES = 128


def _pack(vecs, row_multiple=SUBLANES):
    flat = jnp.concatenate([v.reshape(-1).astype(F32) for v in vecs])
    pad = (-flat.shape[0]) % (PACK_LANES * row_multiple)
    return jnp.concatenate([flat, jnp.zeros((pad,), F32)]).reshape(-1, PACK_LANES)


def _unpack(packed, shapes):
    flat = packed.reshape(-1)
    out, off = [], 0
    for shp in shapes:
        size = math.prod(shp)
        out.append(flat[off:off + size].reshape(shp))
        off += size
    return out


def kernel(x, c, positions, w_ada, b_ada, norm1_g, w_in, conv_w, conv_b, w_gate_a, b_gate_a, w_gate_x, b_gate_x, lru_param, q_norm_g, w_uq, kv_norm_g, w_ukv, w_proj_rnn, w_proj_mla, w_out, norm2_g, w_up, ffn_conv_w, ffn_conv_b, w_down, final_g, loss_target, m_w_ada, m_b_ada, m_norm1_g, m_w_in, m_conv_w, m_conv_b, m_w_gate_a, m_b_gate_a, m_w_gate_x, m_b_gate_x, m_lru_param, m_q_norm_g, m_w_uq, m_kv_norm_g, m_w_ukv, m_w_proj_rnn, m_w_proj_mla, m_w_out, m_norm2_g, m_w_up, m_ffn_conv_w, m_ffn_conv_b, m_w_down, m_final_g, v_w_ada, v_b_ada, v_norm1_g, v_w_in, v_conv_w, v_conv_b, v_w_gate_a, v_b_gate_a, v_w_gate_x, v_b_gate_x, v_lru_param, v_q_norm_g, v_w_uq, v_kv_norm_g, v_w_ukv, v_w_proj_rnn, v_w_proj_mla, v_w_out, v_norm2_g, v_w_up, v_ffn_conv_w, v_ffn_conv_b, v_w_down, v_final_g):
    args = dict(locals())
    w = {n: args[n] for n in WEIGHTS}
    m = {n: args["m_" + n] for n in WEIGHTS}
    v = {n: args["v_" + n] for n in WEIGHTS}
    s, d = x.shape[1], x.shape[2]
    me = _my_slot()
    def two_d(a):
        assert a.ndim == 3 and a.shape[0] == 1, a.shape
        return a[0]

    big = COL_SHARDED + ROW_SHARDED
    shard = {n: (jnp.transpose(two_d(w[n])) if n in GATHERED_TRANSPOSED else two_d(w[n])).astype(BF16) for n in big}

    def whole(n, g):
        k, r, cc = g.shape
        by_columns = n in COL_SHARDED and n not in GATHERED_TRANSPOSED
        return jnp.transpose(g, (1, 0, 2)).reshape(r, k * cc) if by_columns else g.reshape(k * r, cc)

    first = _all_gather([shard["w_in"], c, two_d(conv_w), two_d(ffn_conv_w)], "gather_first")
    c_all = first[1].reshape(N_DEV, d)
    conv_w_all = jnp.transpose(first[2], (1, 0, 2)).reshape(conv_w.shape[1], -1)
    ffn_conv_w_all = jnp.transpose(first[3], (1, 0, 2)).reshape(ffn_conv_w.shape[1], -1)

    ada_cols = w_ada.shape[2]
    b_cols = lax.dynamic_slice(b_ada, (0, me * ada_cols), (1, ada_cols))
    mod_cols = _ada_fwd(c_all, w_ada[0], b_cols)
    mod_all, = _all_gather([mod_cols], "gather_mod")

    later = ("w_uq", "w_ukv", "w_proj_rnn", "w_proj_mla", "w_out", "w_up", "w_down")
    flights, started = _gather_start([shard[n] for n in later], mod_all, "gather_start")
    flight = dict(zip(later, flights))

    def fetch(names, after):
        lands = _gather_wait([flight[n] for n in names], after, "gather_wait_" + names[0])
        return {n: whole(n, lax.dynamic_update_index_in_dim(g, shard[n], me, 0)) for n, g in zip(names, lands)}

    mod = lax.dynamic_index_in_dim(mod_all, me, axis=1, keepdims=False).reshape(6, d) + started[0, 0]

    sm = {n: w[n][0] for n in REPLICATED if n not in ("b_ada", "final_g")}
    sm["final_g"] = final_g
    sm["conv_w"] = conv_w_all
    sm["ffn_conv_w"] = ffn_conv_w_all
    in_flight, windows = {}, {}

    def emit(n, g):
        rows = g.shape[0] // N_DEV
        if rows % ROW_ALIGN == 0:
            windows[n] = (None, None)
            g = g.reshape(N_DEV, rows, g.shape[1])
        else:
            span = max(rows * k - _span_start(k, rows) for k in range(N_DEV)) + rows
            windows[n] = (rows, -(-span // ROW_ALIGN) * ROW_ALIGN)
            assert _span_start(N_DEV - 1, rows) + windows[n][1] <= g.shape[0], (n, g.shape)
        *in_flight[n], token = _scatter_start(g, "scatter_start_" + n, *windows[n])
        return token[0, 0]

    sq, grad_x, gs, dmod = _local_step(x[0], mod, positions[0], loss_target[0], whole("w_in", first[0]), fetch, sm, emit)

    small_names = [n for n in REPLICATED if n != "b_ada"] + ["conv_w", "ffn_conv_w"]
    small_shapes = [gs[n].shape for n in small_names] + [(6 * d,), (1,)]
    partial = _pack([gs[n] for n in small_names] + [dmod, sq.reshape(1)], N_DEV * SUBLANES)
    *small_flight, small_started = _scatter_start(partial.reshape(N_DEV, -1, PACK_LANES), "scatter_small_start")

    grads, deltas, new_m, new_v = {}, {}, {}, {}

    def update(n, parts, own=None):
        shp = w[n].shape
        lay = jnp.transpose if n in TRANSPOSED_GRADS else (lambda a: a)
        res = _adamw(parts, lay(two_d(w[n])), lay(two_d(m[n])), lay(two_d(v[n])), "adamw_" + n, own)
        grads[n], deltas[n], new_m[n], new_v[n] = [lay(a).reshape(shp) for a in res]

    for n in big:
        rows, span = windows[n]
        src, landed = _scatter_wait(*in_flight[n], small_started, "scatter_wait_" + n, rows, span)
        if rows is None:
            update(n, landed, lax.dynamic_index_in_dim(src, me, axis=0, keepdims=False))
        else:
            start = _span_start(me, rows)
            own = lax.dynamic_slice(src, (start, 0), (span, src.shape[1]))
            total = _sum_sources(landed, own, "sum_" + n)
            update(n, lax.dynamic_slice(total, (rows * me - start, 0), (rows, src.shape[1]))[None])

    chunks, landed = _scatter_wait(*small_flight, new_v[big[-1]], "scatter_small_wait")
    mine = _sum_sources(landed, lax.dynamic_index_in_dim(chunks, me, axis=0, keepdims=False), "sum_small")
    summed_all, dmod_all = _all_gather([mine, dmod.reshape(1, 6 * d)], "gather_small")
    summed = _unpack(summed_all, small_shapes)
    g_small = dict(zip(small_names, summed[:len(small_names)]))
    g_small["b_ada"] = summed[len(small_names)]
    loss = 0.5 * summed[-1][0] / d
    dmod_cols = lax.dynamic_slice(dmod_all.reshape(N_DEV, 6 * d), (0, me * ada_cols), (N_DEV, ada_cols))

    update("w_ada", _ada_bwd(c_all, dmod_cols)[None])

    for n in ("conv_w", "ffn_conv_w"):
        cols = w[n].shape[2]
        update(n, lax.dynamic_slice(g_small[n], (0, me * cols), (g_small[n].shape[0], cols))[None])

    rep_shapes = [w[n].shape for n in REPLICATED]
    g_rep, d_rep, m_rep, v_rep = _adamw(
        _pack([g_small[n] for n in REPLICATED])[None], _pack([w[n] for n in REPLICATED]), _pack([m[n] for n in REPLICATED]),
        _pack([v[n] for n in REPLICATED]), "adamw_replicated")
    for dst, packed in ((grads, g_rep), (deltas, d_rep), (new_m, m_rep), (new_v, v_rep)):
        dst.update(zip(REPLICATED, _unpack(packed, rep_shapes)))

    return (loss, grad_x[None], *[grads[n] for n in WEIGHTS], *[deltas[n] for n in WEIGHTS],
            *[new_m[n] for n in WEIGHTS], *[new_v[n] for n in WEIGHTS])
```

```python
import functools
import math

import jax
import jax.numpy as jnp
from jax import lax
from jax.experimental import pallas as pl
from jax.experimental.pallas import tpu as pltpu

F32 = jnp.float32
BF16 = jnp.bfloat16

N_DEV = 8
LANES = 128
SUBLANES = 8
VMEM_LIMIT = 56 * 1024 * 1024

D_RNN = 1280
Q_LORA = 384
KV_LORA = 256
QK_NOPE = 64
QK_ROPE = 32
V_HEAD = 64
N_HEADS = 16
D_FF = 2816
ROPE_THETA = 10000.0
LRU_C = 8.0
EPS = 1e-6
MLA_W = 768
ATT_SCALE = 1.0 / math.sqrt(QK_NOPE + QK_ROPE)

ADAM_LR, ADAM_B1, ADAM_B2, ADAM_EPS, ADAM_WD, ADAM_STEP = 0.001, 0.9, 0.999, 1e-08, 0.01, 10


def _cparams(sem=None):
    return pltpu.CompilerParams(dimension_semantics=sem, vmem_limit_bytes=VMEM_LIMIT)


def _pick(n, prefs):
    for p in prefs:
        if n % p == 0:
            return p
    return n


def _sigmoid(v):
    return 0.5 * jnp.tanh(0.5 * v) + 0.5


def _lane(shape):
    return lax.broadcasted_iota(jnp.int32, shape, len(shape) - 1)


def _row(shape):
    return lax.broadcasted_iota(jnp.int32, shape, len(shape) - 2)


MM_BLOCK_BYTES = 36 * 1024 * 1024


def _divisors(n):
    return [t for t in range(n, 0, -LANES) if n % t == 0] if n % LANES == 0 else [n]


HBM_BYTES_PER_US = 3.0e6
MXU_FLOPS_PER_US = 8.0e8
GRID_STEP_US = 0.35


def _mm_tiles(m, n, k, a_bytes, b_bytes, o_bytes):
    best = None
    for tm in [t for t in _divisors(m) if t <= 1024]:
        for tn in [t for t in _divisors(n) if t <= 2048]:
            for tk in _divisors(k):
                nk = k // tk
                need = 2 * (tm * tk * a_bytes + tk * tn * b_bytes + tm * tn * o_bytes) + (tm * tn * 4 if nk > 1 else 0)
                if need > MM_BLOCK_BYTES:
                    continue
                gi, gj = m // tm, n // tn
                for rows_outer in (True, False):
                    if nk > 1:
                        a_reads, b_reads = gj, gi
                    elif rows_outer:
                        a_reads, b_reads = 1, (gi if gj > 1 else 1)
                    else:
                        a_reads, b_reads = (gj if gi > 1 else 1), 1
                    traffic = m * k * a_bytes * a_reads + k * n * b_bytes * b_reads + m * n * (o_bytes + (8 * nk if nk > 1 else 0))
                    cost = max(traffic / HBM_BYTES_PER_US, 2.0 * m * n * k / MXU_FLOPS_PER_US) + gi * gj * nk * GRID_STEP_US
                    if best is None or cost < best[0]:
                        best = (cost, tm, tn, tk, rows_outer)
                break
    if best is None:
        raise ValueError((m, n, k))
    return best[1:]


def _mm(a, b, *, ta=False, tb=False, out_dtype=F32, also_t=None, name):
    (k_a, m) = a.shape if ta else a.shape[::-1]
    (n, k_b) = b.shape if tb else b.shape[::-1]
    assert k_a == k_b, (a.shape, b.shape, ta, tb)
    k = k_a
    tm, tn, tk, rows_outer = _mm_tiles(m, n, k, a.dtype.itemsize, b.dtype.itemsize, jnp.dtype(out_dtype).itemsize)
    nk = k // tk
    dims = (((0 if ta else 1,), (1 if tb else 0,)), ((), ()))
    n_out = 1 if also_t is None else 2

    def body(a_ref, b_ref, *rest):
        outs, acc = rest[:n_out], rest[n_out:]
        part = lax.dot_general(a_ref[...].astype(BF16), b_ref[...].astype(BF16), dims, preferred_element_type=F32)

        def write(val):
            outs[0][...] = val.astype(out_dtype)
            if also_t is not None:
                outs[1][...] = jnp.transpose(val).astype(also_t)

        if nk == 1:
            write(part)
            return
        acc_ref, = acc
        kk = pl.program_id(2)

        @pl.when(kk == 0)
        def _():
            acc_ref[...] = part

        @pl.when(kk > 0)
        def _():
            acc_ref[...] += part

        @pl.when(kk == nk - 1)
        def _():
            write(acc_ref[...])

    def ij(f):
        return (lambda i, j, kk: f(i, j, kk)) if rows_outer else (lambda j, i, kk: f(i, j, kk))

    a_spec = pl.BlockSpec((tk, tm), ij(lambda i, j, kk: (kk, i))) if ta else pl.BlockSpec((tm, tk), ij(lambda i, j, kk: (i, kk)))
    b_spec = pl.BlockSpec((tn, tk), ij(lambda i, j, kk: (j, kk))) if tb else pl.BlockSpec((tk, tn), ij(lambda i, j, kk: (kk, j)))
    out_specs = [pl.BlockSpec((tm, tn), ij(lambda i, j, kk: (i, j)))]
    out_shape = [jax.ShapeDtypeStruct((m, n), out_dtype)]
    if also_t is not None:
        out_specs.append(pl.BlockSpec((tn, tm), ij(lambda i, j, kk: (j, i))))
        out_shape.append(jax.ShapeDtypeStruct((n, m), also_t))
    res = pl.pallas_call(
        body, name=name,
        grid=(m // tm, n // tn, nk) if rows_outer else (n // tn, m // tm, nk),
        in_specs=[a_spec, b_spec], out_specs=out_specs, out_shape=out_shape,
        scratch_shapes=[] if nk == 1 else [pltpu.VMEM((tm, tn), F32)],
        compiler_params=_cparams(("arbitrary", "arbitrary", "arbitrary")),
    )(a, b)
    return res[0] if also_t is None else res


def _rowwise(fn, row_ins, par_ins, out_defs, red_defs, *, name, tr=256):
    s = row_ins[0].shape[0]
    tr = min(tr, s)
    nr, npar, no = len(row_ins), len(par_ins), len(out_defs)

    def body(*refs):
        rin, pin = refs[:nr], refs[nr:nr + npar]
        outs, reds = refs[nr + npar:nr + npar + no], refs[nr + npar + no:]
        i = pl.program_id(0)

        @pl.when(i == 0)
        def _():
            for r in reds:
                r[...] = jnp.zeros_like(r)

        fn(i, rin, pin, outs, reds)

    in_specs = [pl.BlockSpec((tr, a.shape[1]), lambda i: (i, 0)) for a in row_ins]
    in_specs += [pl.BlockSpec(a.shape, lambda i, nd=a.ndim: (0,) * nd) for a in par_ins]
    out_specs = [pl.BlockSpec((tr, c), lambda i: (i, 0)) for c, _ in out_defs]
    out_specs += [pl.BlockSpec(shp, lambda i: (0, 0)) for shp in red_defs]
    out_shape = [jax.ShapeDtypeStruct((s, c), dt) for c, dt in out_defs]
    out_shape += [jax.ShapeDtypeStruct(shp, F32) for shp in red_defs]
    return pl.pallas_call(
        body, name=name, grid=(s // tr,), in_specs=in_specs, out_specs=out_specs, out_shape=out_shape,
        compiler_params=_cparams(("arbitrary",)),
    )(*row_ins, *par_ins)


def _rms(v):
    return lax.rsqrt(jnp.mean(v * v, axis=-1, keepdims=True) + EPS)


def _colsum(v):
    return jnp.sum(v, axis=0, keepdims=True)


def _rms_bwd(dn, n, rstd):
    return rstd * (dn - n * jnp.mean(dn * n, axis=-1, keepdims=True))


def _norm_mod_fwd(x, gmod, name):
    def fn(i, rin, pin, outs, reds):
        xv = rin[0][...]
        p = pin[0][...]
        n = xv * _rms(xv)
        outs[0][...] = ((n * p[0:1]) * (1.0 + p[1:2]) + p[2:3]).astype(BF16)

    return _rowwise(fn, [x], [gmod], [(x.shape[1], BF16)], [], name=name)[0]


def _rope(v, rot_c, rot_s):
    half = QK_ROPE // 2
    swapped = jnp.where(_lane(v.shape) < QK_NOPE + half, pltpu.roll(v, LANES - half, 1), pltpu.roll(v, half, 1))
    return v * rot_c + swapped * rot_s


def _rope_t(dv, rot_c, rot_s):
    half = QK_ROPE // 2
    ds = dv * rot_s
    lane = _lane(dv.shape)
    swapped = jnp.where(lane < QK_NOPE + half, pltpu.roll(ds, LANES - half, 1), pltpu.roll(ds, half, 1))
    in_rope = (lane >= QK_NOPE) & (lane < QK_NOPE + QK_ROPE)
    return dv * rot_c + jnp.where(in_rope, swapped, 0.0)


def _mla_prep_fwd(proj_mla, rot_c, rot_s, ng):
    o1, o2 = Q_LORA, Q_LORA + KV_LORA

    def fn(i, rin, pin, outs, reds):
        g = pin[0][...]
        ql = rin[0][:, 0:o1]
        kl = rin[0][:, o1:o2]
        outs[0][...] = (ql * _rms(ql) * g[0:1, 0:o1]).astype(BF16)
        outs[1][...] = (kl * _rms(kl) * g[0:1, o1:o2]).astype(BF16)
        kr = pltpu.roll(rin[0][:, o2:o2 + LANES], QK_NOPE, 1)
        outs[2][...] = _rope(kr, rin[1][...], rin[2][...]).astype(BF16)

    return _rowwise(fn, [proj_mla, rot_c, rot_s], [ng], [(Q_LORA, BF16), (KV_LORA, BF16), (LANES, BF16)], [],
                    name="mla_prep_fwd")


def _mla_prep_bwd(proj_mla, dqn, dkvn, dkr, rot_c, rot_s, ng):
    o1, o2 = Q_LORA, Q_LORA + KV_LORA

    def fn(i, rin, pin, outs, reds):
        g = pin[0][...]
        ql = rin[0][:, 0:o1]
        kl = rin[0][:, o1:o2]
        rq, rk = _rms(ql), _rms(kl)
        nq, nk = ql * rq, kl * rk
        dq, dk = rin[1][...], rin[2][...]
        outs[0][:, 0:o1] = _rms_bwd(dq * g[0:1, 0:o1], nq, rq).astype(BF16)
        outs[0][:, o1:o2] = _rms_bwd(dk * g[0:1, o1:o2], nk, rk).astype(BF16)
        dkr_pre = pltpu.roll(_rope_t(rin[3][...], rin[4][...], rin[5][...]), LANES - QK_NOPE, 1)
        outs[0][:, o2:] = jnp.where(_lane(dkr_pre.shape) < QK_ROPE, dkr_pre, 0.0).astype(BF16)
        reds[0][0:1, 0:o1] += _colsum(dq * nq)
        reds[0][0:1, o1:o2] += _colsum(dk * nk)

    return _rowwise(fn, [proj_mla, dqn, dkvn, dkr, rot_c, rot_s], [ng], [(MLA_W, BF16)], [(SUBLANES, MLA_W)],
                    name="mla_prep_bwd")


def _rope_bwd(dq, rot_c, rot_s):
    def fn(i, rin, pin, outs, reds):
        c, sn = rin[1][...] * Q_PRESCALE, rin[2][...] * Q_PRESCALE
        for h in range(N_HEADS):
            sl = slice(h * LANES, (h + 1) * LANES)
            outs[0][:, sl] = _rope_t(rin[0][:, sl], c, sn).astype(BF16)

    return _rowwise(fn, [dq, rot_c, rot_s], [], [(dq.shape[1], BF16)], [], name="rope_bwd")[0]


def _rope_fwd_t(q, rot_c, rot_s):
    s, c = q.shape
    tr = min(256, s)

    def body(q_ref, c_ref, s_ref, o_ref, ot_ref):
        cc, sn = c_ref[...] * Q_PRESCALE, s_ref[...] * Q_PRESCALE
        for h in range(N_HEADS):
            sl = slice(h * LANES, (h + 1) * LANES)
            rot = _rope(q_ref[:, sl], cc, sn)
            o_ref[:, sl] = rot.astype(BF16)
            ot_ref[sl, :] = jnp.transpose(rot).astype(BF16)

    return pl.pallas_call(
        body, name="rope_fwd", grid=(s // tr,),
        in_specs=[pl.BlockSpec((tr, c), lambda i: (i, 0)), pl.BlockSpec((tr, LANES), lambda i: (i, 0)),
                  pl.BlockSpec((tr, LANES), lambda i: (i, 0))],
        out_specs=[pl.BlockSpec((tr, c), lambda i: (i, 0)), pl.BlockSpec((c, tr), lambda i: (0, i))],
        out_shape=[jax.ShapeDtypeStruct((s, c), BF16), jax.ShapeDtypeStruct((c, s), BF16)],
        compiler_params=_cparams(("arbitrary",)),
    )(q, rot_c, rot_s)


def _merge_fwd(pr, pm, proj_g):
    d = pr.shape[1]

    def fn(i, rin, pin, outs, reds):
        g_rnn, g_mla = rin[2][:, 0:d].astype(F32), rin[2][:, d:].astype(F32)
        outs[0][...] = (_sigmoid(g_rnn) * rin[0][...].astype(F32) + _sigmoid(g_mla) * rin[1][...].astype(F32)).astype(BF16)

    return _rowwise(fn, [pr, pm, proj_g], [], [(d, BF16)], [], name="merge_fwd")[0]


def _merge_bwd(dmerged, pr, pm, proj_g):
    d = pr.shape[1]

    def fn(i, rin, pin, outs, reds):
        dm = rin[0][...]
        sr, sm = _sigmoid(rin[3][:, 0:d].astype(F32)), _sigmoid(rin[3][:, d:].astype(F32))
        outs[0][...] = (dm * sr).astype(BF16)
        outs[1][...] = (dm * sm).astype(BF16)
        outs[2][:, 0:d] = (dm * rin[1][...].astype(F32) * sr * (1.0 - sr)).astype(BF16)
        outs[2][:, d:] = (dm * rin[2][...].astype(F32) * sm * (1.0 - sm)).astype(BF16)

    return _rowwise(fn, [dmerged, pr, pm, proj_g], [], [(d, BF16), (d, BF16), (2 * d, BF16)], [], name="merge_bwd")


def _resid_norm_fwd(x, o, gmod):
    d = x.shape[1]

    def fn(i, rin, pin, outs, reds):
        p = pin[0][...]
        x1 = rin[0][...] + p[3:4] * rin[1][...]
        outs[0][...] = x1
        outs[1][...] = ((x1 * _rms(x1) * p[0:1]) * (1.0 + p[1:2]) + p[2:3]).astype(BF16)

    return _rowwise(fn, [x, o], [gmod], [(d, F32), (d, BF16)], [], name="resid_norm_fwd")


def _final_fwd_bwd(x1, dn, target, par):
    d = x1.shape[1]

    def fn(i, rin, pin, outs, reds):
        p = pin[0][...]
        dnv = rin[1][...]
        x2 = rin[0][...] + p[0:1] * dnv
        rstd = _rms(x2)
        n3 = x2 * rstd
        err = n3 * p[1:2] - rin[2][...]
        dy = err * (1.0 / d)
        dx2 = _rms_bwd(dy * p[1:2], n3, rstd)
        outs[0][...] = dx2
        outs[1][...] = (dx2 * p[0:1]).astype(BF16)
        reds[0][0:1, :] += _colsum(dy * n3)
        reds[0][1:2, :] += _colsum(dx2 * dnv)
        reds[0][2:3, :] += jnp.zeros((1, d), F32) + jnp.sum(err * err)

    return _rowwise(fn, [x1, dn, target], [par], [(d, F32), (d, BF16)], [(SUBLANES, d)], name="final_fwd_bwd")


def _norm2_bwd(x1, dh2, dx2, o, gmod):
    d = x1.shape[1]

    def fn(i, rin, pin, outs, reds):
        p = pin[0][...]
        x1v, dh = rin[0][...], rin[1][...]
        rstd = _rms(x1v)
        n2 = x1v * rstd
        dx1 = rin[2][...] + _rms_bwd(dh * (p[0:1] * (1.0 + p[1:2])), n2, rstd)
        outs[0][...] = dx1
        outs[1][...] = (dx1 * p[3:4]).astype(BF16)
        reds[0][0:1, :] += _colsum(dh * n2 * (1.0 + p[1:2]))
        reds[0][1:2, :] += _colsum(dh * n2 * p[0:1])
        reds[0][2:3, :] += _colsum(dh)
        reds[0][3:4, :] += _colsum(dx1 * rin[3][...])

    return _rowwise(fn, [x1, dh2, dx2, o], [gmod], [(d, F32), (d, BF16)], [(SUBLANES, d)], name="norm2_bwd")


def _norm1_bwd(x, dh_a, dh_b, dh_c, dx1, gmod):
    d = x.shape[1]

    def fn(i, rin, pin, outs, reds):
        p = pin[0][...]
        xv = rin[0][...]
        dh = rin[1][...] + rin[2][...] + rin[3][...]
        rstd = _rms(xv)
        n1 = xv * rstd
        outs[0][...] = rin[4][...] + _rms_bwd(dh * (p[0:1] * (1.0 + p[1:2])), n1, rstd)
        reds[0][0:1, :] += _colsum(dh * n1 * (1.0 + p[1:2]))
        reds[0][1:2, :] += _colsum(dh * n1 * p[0:1])
        reds[0][2:3, :] += _colsum(dh)

    return _rowwise(fn, [x, dh_a, dh_b, dh_c, dx1], [gmod], [(d, F32)], [(SUBLANES, d)], name="norm1_bwd")


RNN_CHUNK = 512


def _shift_down(ref, base, n, j):
    v = ref[pl.ds(base, n + SUBLANES), :]
    return v[SUBLANES:] if j == 0 else pltpu.roll(v, j, 0)[SUBLANES:]


def _shift_up(ref, base, n, j, top_pad):
    v = ref[pl.ds(base + top_pad, n + SUBLANES), :]
    return v[:n] if j == 0 else pltpu.roll(v, n + SUBLANES - j, 0)[:n]


SCAN_GROUP = 128


def _scan_sizes(s):
    sizes = [s]
    while sizes[-1] > SUBLANES:
        assert sizes[-1] % SUBLANES == 0, s
        sizes.append(sizes[-1] // SUBLANES)
    return sizes


def _scan_scratch(s):
    return [pltpu.VMEM((n + 2 * SUBLANES, LANES), F32) for n in _scan_sizes(s)[1:] for _ in range(2)]


def _linear_scan(a_ref, b_ref, out_ref, a_off, s, reverse, levels):
    sizes = _scan_sizes(s)
    lv = [(a_ref, b_ref, a_off, 0)] + [(levels[2 * i], levels[2 * i + 1], 0, SUBLANES) for i in range(len(sizes) - 1)]
    zero8 = jnp.zeros((SUBLANES, LANES), F32)
    for (ar, br, _, _), n in zip(lv[1:], sizes[1:]):
        br[0:SUBLANES, :] = zero8
        br[pl.ds(n + SUBLANES, SUBLANES), :] = zero8
    order = list(range(SUBLANES - 1, -1, -1)) if reverse else list(range(SUBLANES))

    for lvl in range(len(sizes) - 1):
        ar, br, aoff, off = lv[lvl]
        m = sizes[lvl + 1]
        g = min(m, SCAN_GROUP)
        for t0 in range(0, m, g):
            acc_a = acc_b = None
            for r in order:
                sa = pl.ds(off + SUBLANES * t0 + r + aoff, g, stride=SUBLANES)
                sb = pl.ds(off + SUBLANES * t0 + r, g, stride=SUBLANES)
                a, b = ar[sa, :], br[sb, :]
                if acc_a is None:
                    acc_a, acc_b = a, b
                else:
                    acc_b = a * acc_b + b
                    acc_a = a * acc_a
            lv[lvl + 1][0][pl.ds(SUBLANES + t0, g), :] = acc_a
            lv[lvl + 1][1][pl.ds(SUBLANES + t0, g), :] = acc_b

    ar, br, _, off = lv[-1]
    n = sizes[-1]
    a, b = ar[pl.ds(off, n), :], br[pl.ds(off, n), :]
    h, rows = jnp.zeros((1, LANES), F32), [None] * n
    for j in (range(n - 1, -1, -1) if reverse else range(n)):
        h = a[j:j + 1, :] * h + b[j:j + 1, :]
        rows[j] = h
    br[pl.ds(off, n), :] = jnp.concatenate(rows, axis=0)

    for lvl in range(len(sizes) - 2, -1, -1):
        ar, br, aoff, off = lv[lvl]
        m = sizes[lvl + 1]
        up = lv[lvl + 1][1]
        dst = out_ref if lvl == 0 else br
        g = min(m, SCAN_GROUP)
        for t0 in range(0, m, g):
            h = _shift_up(up, t0, g, 1, SUBLANES) if reverse else _shift_down(up, t0, g, 1)
            for r in order:
                sa = pl.ds(off + SUBLANES * t0 + r + aoff, g, stride=SUBLANES)
                sb = pl.ds(off + SUBLANES * t0 + r, g, stride=SUBLANES)
                h = ar[sa, :] * h + br[sb, :]
                dst[sb, :] = h


def _one_minus_exp(z):
    series = -z * (1.0 + z * (0.5 + z * (1.0 / 6.0 + z * (1.0 / 24.0 + z * (1.0 / 120.0 + z * (1.0 / 720.0))))))
    return jnp.where(z > -0.1, series, 1.0 - jnp.exp(z))


def _softplus(v):
    return jnp.maximum(v, 0.0) + jnp.log(1.0 + jnp.exp(-jnp.abs(v)))


def _rnn_gates(xc, w, wa, wx, sp):
    xb = xc.astype(BF16)
    ra = _sigmoid(jnp.dot(xb, wa, preferred_element_type=F32) + w[5:6])
    ix = _sigmoid(jnp.dot(xb, wx, preferred_element_type=F32) + w[6:7])
    la = (-LRU_C) * ra * sp
    a = jnp.exp(la)
    mult = jnp.sqrt(_one_minus_exp(2.0 * la))
    return ra, ix, a, mult


def _rnn_fwd(x_rnn, keep, rp, wa_bd, wx_bd):
    s, r = x_rnn.shape
    ts = min(RNN_CHUNK, s)

    def body(x_ref, keep_ref, rp_ref, wa_ref, wx_ref, xc_ref, ra_ref, ix_ref, hs_ref, xpad, a_s, b_s, *levels):
        xpad[0:SUBLANES, :] = jnp.zeros((SUBLANES, LANES), F32)
        xpad[SUBLANES:, :] = x_ref[...]
        w = rp_ref[...]
        sp = _softplus(-w[7:8])
        wa, wx = wa_ref[0], wx_ref[0]

        def chunk(c, carry):
            base = pl.multiple_of(c * ts, ts)
            xc = w[4:5] + w[3:4] * _shift_down(xpad, base, ts, 0)
            for j in range(1, 4):
                xc = xc + w[3 - j:4 - j] * _shift_down(xpad, base, ts, j)
            ra, ix, a, mult = _rnn_gates(xc, w, wa, wx, sp)
            kp = keep_ref[pl.ds(base, ts), :]
            xc_ref[pl.ds(base, ts), :] = xc
            ra_ref[pl.ds(base, ts), :] = ra
            ix_ref[pl.ds(base, ts), :] = ix
            a_s[pl.ds(base, ts), :] = a * kp
            b_s[pl.ds(base, ts), :] = jnp.where(kp > 0.0, mult, 1.0) * (ix * xc)
            return carry

        lax.fori_loop(0, s // ts, chunk, 0)

        _linear_scan(a_s, b_s, hs_ref, 0, s, False, levels)

    col = pl.BlockSpec((s, LANES), lambda g: (0, g))
    return pl.pallas_call(
        body, name="rnn_fwd", grid=(r // LANES,),
        in_specs=[col, pl.BlockSpec((s, 1), lambda g: (0, 0)), pl.BlockSpec((SUBLANES, LANES), lambda g: (0, g)),
                  pl.BlockSpec((1, LANES, LANES), lambda g: (g, 0, 0)), pl.BlockSpec((1, LANES, LANES), lambda g: (g, 0, 0))],
        out_specs=[col] * 4,
        out_shape=[jax.ShapeDtypeStruct((s, r), F32)] * 4,
        scratch_shapes=[pltpu.VMEM((s + SUBLANES, LANES), F32), pltpu.VMEM((s, LANES), F32), pltpu.VMEM((s, LANES), F32),
                        *_scan_scratch(s)],
        compiler_params=_cparams(("arbitrary",)),
    )(x_rnn, keep, rp, wa_bd, wx_bd)


def _rnn_bwd(x_rnn, xc, ra, ix, hs, dy, keep, rp, wa_bd, wx_bd):
    s, r = x_rnn.shape
    ts = min(RNN_CHUNK, s)

    def body(x_ref, xc_ref, ra_ref, ix_ref, hs_ref, dy_ref, keep_ref, rp_ref, wa_ref, wx_ref,
             dx_ref, dwa_ref, dwx_ref, red_ref, xpad, hpad, a_s, dh_s, dxc_s, *levels):
        zero8 = jnp.zeros((SUBLANES, LANES), F32)
        xpad[0:SUBLANES, :] = zero8
        xpad[SUBLANES:, :] = x_ref[...]
        hpad[0:SUBLANES, :] = zero8
        hpad[SUBLANES:, :] = hs_ref[...]
        a_s[s:, :] = zero8
        dxc_s[s:, :] = zero8
        w = rp_ref[...]
        sp = _softplus(-w[7:8])
        wa, wx = wa_ref[0], wx_ref[0]

        def decay(c, carry):
            base = pl.multiple_of(c * ts, ts)
            a = jnp.exp((-LRU_C) * ra_ref[pl.ds(base, ts), :] * sp)
            a_s[pl.ds(base, ts), :] = a * keep_ref[pl.ds(base, ts), :]
            return carry

        lax.fori_loop(0, s // ts, decay, 0)

        _linear_scan(a_s, dy_ref, dh_s, 1, s, True, levels)

        def gates(c, carry):
            dwa, dwx, d_ba, d_bx, d_sp, d_cb = carry
            base = pl.multiple_of(c * ts, ts)
            xcv = xc_ref[pl.ds(base, ts), :]
            rav = ra_ref[pl.ds(base, ts), :]
            ixv = ix_ref[pl.ds(base, ts), :]
            kp = keep_ref[pl.ds(base, ts), :]
            dh = dh_s[pl.ds(base, ts), :]
            h_prev = _shift_down(hpad, base, ts, 1)
            la = (-LRU_C) * rav * sp
            a = jnp.exp(la)
            mult = jnp.sqrt(_one_minus_exp(2.0 * la))
            mult_eff = jnp.where(kp > 0.0, mult, 1.0)
            d_a = dh * h_prev * kp
            d_mult = dh * (ixv * xcv) * kp
            d_ix = dh * mult_eff * xcv
            d_xc = dh * mult_eff * ixv
            d_la = d_a * a - d_mult * (a * a) / mult
            d_pa = d_la * ((-LRU_C) * sp) * rav * (1.0 - rav)
            d_px = d_ix * ixv * (1.0 - ixv)
            xb = xcv.astype(BF16)
            pab, pxb = d_pa.astype(BF16), d_px.astype(BF16)
            tn = (((0,), (0,)), ((), ()))
            nt_ = (((1,), (1,)), ((), ()))
            dwa = dwa + lax.dot_general(xb, pab, tn, preferred_element_type=F32)
            dwx = dwx + lax.dot_general(xb, pxb, tn, preferred_element_type=F32)
            d_xc = d_xc + lax.dot_general(pab, wa, nt_, preferred_element_type=F32)
            d_xc = d_xc + lax.dot_general(pxb, wx, nt_, preferred_element_type=F32)
            dxc_s[pl.ds(base, ts), :] = d_xc
            return (dwa, dwx, d_ba + _colsum(d_pa), d_bx + _colsum(d_px),
                    d_sp + _colsum(d_la * ((-LRU_C) * rav)), d_cb + _colsum(d_xc))

        z1 = jnp.zeros((1, LANES), F32)
        zw = jnp.zeros((LANES, LANES), F32)
        dwa, dwx, d_ba, d_bx, d_sp, d_cb = lax.fori_loop(0, s // ts, gates, (zw, zw, z1, z1, z1, z1))
        dwa_ref[0] = dwa
        dwx_ref[0] = dwx

        def conv(c, carry):
            base = pl.multiple_of(c * ts, ts)
            d_here = dxc_s[pl.ds(base, ts), :]
            dx = w[3:4] * d_here
            for j in range(1, 4):
                dx = dx + w[3 - j:4 - j] * _shift_up(dxc_s, base, ts, j, 0)
            dx_ref[pl.ds(base, ts), :] = dx.astype(BF16)
            return tuple(carry[k] + _colsum(d_here * _shift_down(xpad, base, ts, 3 - k)) for k in range(4))

        d_w = lax.fori_loop(0, s // ts, conv, (z1, z1, z1, z1))
        d_lru = d_sp * (-_sigmoid(-w[7:8]))
        red_ref[...] = jnp.concatenate(list(d_w) + [d_cb, d_ba, d_bx, d_lru], axis=0)

    col = pl.BlockSpec((s, LANES), lambda g: (0, g))
    par = pl.BlockSpec((SUBLANES, LANES), lambda g: (0, g))
    wsp = pl.BlockSpec((1, LANES, LANES), lambda g: (g, 0, 0))
    return pl.pallas_call(
        body, name="rnn_bwd", grid=(r // LANES,),
        in_specs=[col] * 6 + [pl.BlockSpec((s, 1), lambda g: (0, 0)), par, wsp, wsp],
        out_specs=[col, wsp, wsp, par],
        out_shape=[jax.ShapeDtypeStruct((s, r), BF16), jax.ShapeDtypeStruct((r // LANES, LANES, LANES), F32),
                   jax.ShapeDtypeStruct((r // LANES, LANES, LANES), F32), jax.ShapeDtypeStruct((SUBLANES, r), F32)],
        scratch_shapes=[pltpu.VMEM((s + SUBLANES, LANES), F32), pltpu.VMEM((s + SUBLANES, LANES), F32),
                        pltpu.VMEM((s + SUBLANES, LANES), F32), pltpu.VMEM((s, LANES), F32),
                        pltpu.VMEM((s + SUBLANES, LANES), F32), *_scan_scratch(s)],
        compiler_params=_cparams(("arbitrary",)),
    )(x_rnn, xc, ra, ix, hs, dy, keep, rp, wa_bd, wx_bd)


ATT_BLOCK = 512


LOG2E = 1.4426950408889634
LN2 = 0.6931471805599453
Q_PRESCALE = ATT_SCALE * LOG2E


def _att_scores(q, kvt, krt, diagonal):
    kt_eff = jnp.where(_row(kvt.shape) < QK_NOPE, kvt, krt)
    sc = jnp.dot(q, kt_eff, preferred_element_type=F32)
    if diagonal:
        sc = jnp.where(lax.broadcasted_iota(jnp.int32, sc.shape, 1) <= lax.broadcasted_iota(jnp.int32, sc.shape, 0), sc, -jnp.inf)
    return sc


def _att_fwd(q, kv, kvt, krt):
    s = q.shape[0]
    t = min(ATT_BLOCK, s)
    nb = s // t

    def body(q_ref, kv_ref, kvt_ref, krt_ref, y_ref, lse_ref, m_s, acc_s):
        i, j = pl.program_id(1), pl.program_id(2)

        @pl.when(j == 0)
        def _():
            m_s[...] = jnp.full(m_s.shape, -jnp.inf, F32)
            acc_s[...] = jnp.zeros(acc_s.shape, F32)

        def step(diagonal):
            krt_b = krt_ref[...]
            lane = _lane((t, LANES))
            groups = [slice(c * LANES, (c + 1) * LANES) for c in range(t // LANES)]
            heads = [slice(hh * LANES, (hh + 1) * LANES) for hh in range(2)]
            scs = [_att_scores(q_ref[:, sl], kvt_ref[sl, :], krt_b, diagonal) for sl in heads]
            stats = []
            for hh in range(2):
                m_prev = m_s[hh]
                m_blk = scs[hh][:, groups[0]]
                for g in groups[1:]:
                    m_blk = jnp.maximum(m_blk, scs[hh][:, g])
                stats.append((m_prev, jnp.maximum(m_prev, jnp.max(m_blk, axis=-1, keepdims=True))))
            for hh in range(2):
                m_prev, m_new = stats[hh]
                kvb = kv_ref[:, heads[hh]]
                ones_v = jnp.where(lane < QK_NOPE, jnp.ones_like(kvb), kvb)
                p = jnp.concatenate([jnp.exp2(scs[hh][:, g] - m_new).astype(BF16) for g in groups], axis=1)
                acc_s[hh] = jnp.exp2(m_prev - m_new) * acc_s[hh] + jnp.dot(p, ones_v, preferred_element_type=F32)
                m_s[hh] = m_new

        @pl.when(j < i)
        def _():
            step(False)

        @pl.when(j == i)
        def _():
            step(True)
            lane = _lane((t, LANES))
            a0, a1 = acc_s[0], acc_s[1]
            l0, l1 = a0[:, 0:1], a1[:, 0:1]
            y_ref[...] = jnp.where(lane < V_HEAD, pltpu.roll(a0 / l0, V_HEAD, 1), a1 / l1).astype(BF16)
            lse_ref[...] = jnp.where(lane < V_HEAD, m_s[0] + jnp.log(l0) * LOG2E, m_s[1] + jnp.log(l1) * LOG2E)

    return pl.pallas_call(
        body, name="att_fwd", grid=(N_HEADS // 2, nb, nb),
        in_specs=[pl.BlockSpec((t, 2 * LANES), lambda p, i, j: (i, p)),
                  pl.BlockSpec((t, 2 * LANES), lambda p, i, j: (jnp.minimum(j, i), p)),
                  pl.BlockSpec((2 * LANES, t), lambda p, i, j: (p, jnp.minimum(j, i))),
                  pl.BlockSpec((LANES, t), lambda p, i, j: (0, jnp.minimum(j, i)))],
        out_specs=[pl.BlockSpec((t, LANES), lambda p, i, j: (i, p))] * 2,
        out_shape=[jax.ShapeDtypeStruct((s, N_HEADS * V_HEAD), BF16), jax.ShapeDtypeStruct((s, N_HEADS * V_HEAD), F32)],
        scratch_shapes=[pltpu.VMEM((2, t, LANES), F32)] * 2,
        compiler_params=_cparams(("arbitrary", "arbitrary", "arbitrary")),
    )(q, kv, kvt, krt)


def _att_bwd(q, qt, kv, kvt, kr, krt, y, lse, dy, dyt):
    s = q.shape[0]
    t = min(ATT_BLOCK, s)
    nb = s // t

    def body(q_ref, qt_ref, kv_ref, kvt_ref, kr_ref, krt_ref, y_ref, lse_ref, dy_ref, dyt_ref,
             dq_ref, dkvt_ref, dkrt_ref, dkv_s):
        p_, j, i = pl.program_id(0), pl.program_id(1), pl.program_id(2)

        @pl.when((p_ == 0) & (j == 0) & (i == 0))
        def _():
            dkrt_ref[...] = jnp.zeros(dkrt_ref.shape, F32)

        @pl.when((j == 0) & (i == 0))
        def _():
            dq_ref[...] = jnp.zeros(dq_ref.shape, F32)

        @pl.when(i == 0)
        def _():
            dkv_s[...] = jnp.zeros(dkv_s.shape, F32)

        def step(diagonal):
            lane = _lane((t, LANES))
            row = _row((LANES, t))
            krb, krt_b = kr_ref[...], krt_ref[...]
            dyv = dy_ref[...]
            yv = y_ref[...].astype(F32)
            lsev = lse_ref[...]
            dyt_b = dyt_ref[...]
            rows = pl.ds(pl.multiple_of(i * t, t), t)
            cols = pl.ds(pl.multiple_of(j * t, t), t)
            zeros_t = jnp.zeros((V_HEAD, t), BF16)
            ones_w = jnp.ones((LANES, LANES), BF16)
            groups = [slice(c * LANES, (c + 1) * LANES) for c in range(t // LANES)]
            heads = [slice(hh * LANES, (hh + 1) * LANES) for hh in range(2)]
            scs, dps, stats = [], [], []
            for hh, sl in enumerate(heads):
                kvt_b = kvt_ref[sl, :]
                scs.append(_att_scores(q_ref[:, sl], kvt_b, krt_b, diagonal))
                mine = (lane < V_HEAD) if hh == 0 else (lane >= V_HEAD)
                lse_rep = jnp.where(mine, lsev, pltpu.roll(lsev, V_HEAD, 1))
                do_pad = jnp.where(lane >= V_HEAD, pltpu.roll(dyv, V_HEAD, 1) if hh == 0 else dyv, 0.0)
                o_pad = jnp.where(lane >= V_HEAD, pltpu.roll(yv, V_HEAD, 1) if hh == 0 else yv, 0.0)
                do_ln2 = do_pad * LN2
                prod = do_ln2 * o_pad
                head_part = prod.astype(BF16)
                rest_part = (prod - head_part.astype(F32)).astype(BF16)
                delta_rep = (jnp.dot(head_part, ones_w, preferred_element_type=F32)
                             + jnp.dot(rest_part, ones_w, preferred_element_type=F32))
                dps.append(jnp.dot(do_ln2.astype(BF16), kvt_b, preferred_element_type=F32))
                stats.append((lse_rep, delta_rep))
            dkr_acc = jnp.zeros((LANES, t), F32)
            for hh, sl in enumerate(heads):
                lse_rep, delta_rep = stats[hh]
                probs, dss = [], []
                for g in groups:
                    pg = jnp.exp2(scs[hh][:, g] - lse_rep)
                    probs.append(pg.astype(BF16))
                    dss.append((pg * (dps[hh][:, g] - delta_rep)).astype(BF16))
                prob, ds = jnp.concatenate(probs, axis=1), jnp.concatenate(dss, axis=1)
                dot_pad = jnp.concatenate([zeros_t, dyt_b[hh * V_HEAD:(hh + 1) * V_HEAD, :]], axis=0)
                k_eff = jnp.where(lane < QK_NOPE, kv_ref[:, sl], krb)
                dvt = jnp.dot(dot_pad, prob, preferred_element_type=F32)
                dq_ref[rows, sl] += jnp.dot(ds, k_eff, preferred_element_type=F32)
                dkt = jnp.dot(qt_ref[sl, :], ds, preferred_element_type=F32)
                dkv_s[hh] += dvt + jnp.where(row < QK_NOPE, dkt, 0.0)
                dkr_acc = dkr_acc + jnp.where(row >= QK_NOPE, dkt, 0.0)
            dkrt_ref[:, cols] += dkr_acc

        @pl.when(i > j)
        def _():
            step(False)

        @pl.when(i == j)
        def _():
            step(True)

        @pl.when(i == nb - 1)
        def _():
            dkvt_ref[0:LANES, :] = dkv_s[0].astype(BF16)
            dkvt_ref[LANES:, :] = dkv_s[1].astype(BF16)

    qi = lambda p, j, i: (jnp.maximum(i, j), p)
    qti = lambda p, j, i: (p, jnp.maximum(i, j))
    return pl.pallas_call(
        body, name="att_bwd", grid=(N_HEADS // 2, nb, nb),
        in_specs=[pl.BlockSpec((t, 2 * LANES), qi), pl.BlockSpec((2 * LANES, t), qti),
                  pl.BlockSpec((t, 2 * LANES), lambda p, j, i: (j, p)), pl.BlockSpec((2 * LANES, t), lambda p, j, i: (p, j)),
                  pl.BlockSpec((t, LANES), lambda p, j, i: (j, 0)), pl.BlockSpec((LANES, t), lambda p, j, i: (0, j)),
                  pl.BlockSpec((t, LANES), qi), pl.BlockSpec((t, LANES), qi), pl.BlockSpec((t, LANES), qi),
                  pl.BlockSpec((LANES, t), qti)],
        out_specs=[pl.BlockSpec((s, 2 * LANES), lambda p, j, i: (0, p)),
                   pl.BlockSpec((2 * LANES, t), lambda p, j, i: (p, j)),
                   pl.BlockSpec((LANES, s), lambda p, j, i: (0, 0))],
        out_shape=[jax.ShapeDtypeStruct((s, N_HEADS * LANES), F32), jax.ShapeDtypeStruct((N_HEADS * LANES, s), BF16),
                   jax.ShapeDtypeStruct((LANES, s), F32)],
        scratch_shapes=[pltpu.VMEM((2, LANES, t), F32)],
        compiler_params=_cparams(("arbitrary", "arbitrary", "arbitrary")),
    )(q, qt, kv, kvt, kr, krt, y, lse, dy, dyt)


FFN_COLS = 256


def _ffn_conv(pad_ref, w, base, n):
    u = w[3:4] + w[2:3] * _shift_down(pad_ref, base, n, 0)
    for j in range(1, 3):
        u = u + w[2 - j:3 - j] * _shift_down(pad_ref, base, n, j)
    return u


def _ffn_act_fwd(up, fp):
    s, f2 = up.shape
    f = f2 // 2
    tc = FFN_COLS
    ts = min(RNN_CHUNK, s)
    nfb = f // tc

    def body(ug_ref, uv_ref, wg_ref, wv_ref, act_ref, gpad, vpad):
        zero8 = jnp.zeros((SUBLANES, tc), F32)
        gpad[0:SUBLANES, :] = zero8
        vpad[0:SUBLANES, :] = zero8
        gpad[SUBLANES:, :] = ug_ref[...]
        vpad[SUBLANES:, :] = uv_ref[...]
        wg, wv = wg_ref[...], wv_ref[...]

        def chunk(c, carry):
            base = pl.multiple_of(c * ts, ts)
            g = _ffn_conv(gpad, wg, base, ts)
            v = _ffn_conv(vpad, wv, base, ts)
            act_ref[pl.ds(base, ts), :] = (g * _sigmoid(g) * v).astype(BF16)
            return carry

        lax.fori_loop(0, s // ts, chunk, 0)

    return pl.pallas_call(
        body, name="ffn_act_fwd", grid=(nfb,),
        in_specs=[pl.BlockSpec((s, tc), lambda b: (0, b)), pl.BlockSpec((s, tc), lambda b: (0, b + nfb)),
                  pl.BlockSpec((SUBLANES, tc), lambda b: (0, b)), pl.BlockSpec((SUBLANES, tc), lambda b: (0, b + nfb))],
        out_specs=pl.BlockSpec((s, tc), lambda b: (0, b)),
        out_shape=jax.ShapeDtypeStruct((s, f), BF16),
        scratch_shapes=[pltpu.VMEM((s + SUBLANES, tc), F32)] * 2,
        compiler_params=_cparams(("arbitrary",)),
    )(up, up, fp, fp)


def _ffn_act_bwd(up, dact, fp):
    s, f2 = up.shape
    f = f2 // 2
    tc = FFN_COLS
    ts = min(RNN_CHUNK, s)
    nfb = f // tc

    def body(ug_ref, uv_ref, da_ref, wg_ref, wv_ref, dup_ref, red_ref, gpad, vpad, dgs, dvs):
        half = pl.program_id(1)
        wg, wv = wg_ref[...], wv_ref[...]

        @pl.when(half == 0)
        def _():
            zero8 = jnp.zeros((SUBLANES, tc), F32)
            gpad[0:SUBLANES, :] = zero8
            vpad[0:SUBLANES, :] = zero8
            gpad[SUBLANES:, :] = ug_ref[...]
            vpad[SUBLANES:, :] = uv_ref[...]
            dgs[s:, :] = zero8
            dvs[s:, :] = zero8

            def act(c, carry):
                base = pl.multiple_of(c * ts, ts)
                g = _ffn_conv(gpad, wg, base, ts)
                v = _ffn_conv(vpad, wv, base, ts)
                da = da_ref[pl.ds(base, ts), :]
                sg = _sigmoid(g)
                dgs[pl.ds(base, ts), :] = da * v * (sg * (1.0 + g * (1.0 - sg)))
                dvs[pl.ds(base, ts), :] = da * (g * sg)
                return carry

            lax.fori_loop(0, s // ts, act, 0)

        def conv_t(src, pad, w, out_ref, red_ref):
            def chunk(c, carry):
                base = pl.multiple_of(c * ts, ts)
                d_here = src[pl.ds(base, ts), :]
                dx = w[2:3] * d_here
                for j in range(1, 3):
                    dx = dx + w[2 - j:3 - j] * _shift_up(src, base, ts, j, 0)
                out_ref[pl.ds(base, ts), :] = dx.astype(BF16)
                taps = tuple(carry[k] + _colsum(d_here * _shift_down(pad, base, ts, 2 - k)) for k in range(3))
                return taps + (carry[3] + _colsum(d_here),)

            z1 = jnp.zeros((1, tc), F32)
            red = lax.fori_loop(0, s // ts, chunk, (z1, z1, z1, z1))
            red_ref[...] = jnp.concatenate(list(red) + [jnp.zeros((4, tc), F32)], axis=0)

        @pl.when(half == 0)
        def _():
            conv_t(dgs, gpad, wg, dup_ref, red_ref)

        @pl.when(half == 1)
        def _():
            conv_t(dvs, vpad, wv, dup_ref, red_ref)

    gcol = pl.BlockSpec((s, tc), lambda b, h: (0, b))
    vcol = pl.BlockSpec((s, tc), lambda b, h: (0, b + nfb))
    gpar = pl.BlockSpec((SUBLANES, tc), lambda b, h: (0, b))
    vpar = pl.BlockSpec((SUBLANES, tc), lambda b, h: (0, b + nfb))
    return pl.pallas_call(
        body, name="ffn_act_bwd", grid=(nfb, 2),
        in_specs=[gcol, vcol, gcol, gpar, vpar],
        out_specs=[pl.BlockSpec((s, tc), lambda b, h: (0, b + h * nfb)),
                   pl.BlockSpec((SUBLANES, tc), lambda b, h: (0, b + h * nfb))],
        out_shape=[jax.ShapeDtypeStruct((s, f2), BF16), jax.ShapeDtypeStruct((SUBLANES, f2), F32)],
        scratch_shapes=[pltpu.VMEM((s + SUBLANES, tc), F32)] * 4,
        compiler_params=_cparams(("arbitrary", "arbitrary")),
    )(up, up, dact, fp, fp)


def _rows8(rows, width):
    rows = [r.reshape(1, width).astype(F32) for r in rows]
    return jnp.concatenate(rows + [jnp.zeros((SUBLANES - len(rows), width), F32)], axis=0)


def _block_diag(w):
    n, b, _ = w.shape
    w = w.reshape(n // 2, 2, b, b)
    z = jnp.zeros((n // 2, b, b), w.dtype)
    top = jnp.concatenate([w[:, 0], z], axis=2)
    bot = jnp.concatenate([z, w[:, 1]], axis=2)
    return jnp.concatenate([top, bot], axis=1)


def _block_diag_t(bd):
    n, b2, _ = bd.shape
    b = b2 // 2
    return jnp.stack([bd[:, :b, :b], bd[:, b:, b:]], axis=1).reshape(2 * n, b, b)


def _local_step(x, mod, positions, target, w_in, fetch, sm, emit):
    s, d = x.shape
    o_rnn, o_mla = D_RNN, D_RNN + Q_LORA + KV_LORA + QK_ROPE
    wts = {}
    w_in_rnn = w_in[:, :o_rnn]
    w_in_mla = jnp.concatenate([w_in[:, o_rnn:o_mla], jnp.zeros((d, MLA_W - (o_mla - o_rnn)), w_in.dtype)], axis=1)
    w_in_g = w_in[:, o_mla:]
    hd = QK_NOPE + QK_ROPE
    wa_bd = _block_diag(sm["w_gate_a"]).astype(BF16)
    wx_bd = _block_diag(sm["w_gate_x"]).astype(BF16)

    pos = positions.reshape(s)
    half = QK_ROPE // 2
    inv_freq = ROPE_THETA ** (-jnp.arange(half, dtype=F32) / half)
    ang = pos.astype(F32)[:, None] * inv_freq
    cos, sin = jnp.cos(ang), jnp.sin(ang)
    rot_c = jnp.concatenate([jnp.ones((s, QK_NOPE), F32), cos, cos, jnp.ones((s, LANES - hd), F32)], axis=1)
    rot_s = jnp.concatenate([jnp.zeros((s, QK_NOPE), F32), -sin, sin, jnp.zeros((s, LANES - hd), F32)], axis=1)
    keep = (pos != 0).astype(F32).reshape(s, 1)

    gmod1 = _rows8([sm["norm1_g"], mod[1], mod[0]], d)
    gmod2 = _rows8([sm["norm2_g"], mod[4], mod[3], mod[2]], d)
    rp = jnp.concatenate([sm["conv_w"].reshape(4, D_RNN), _rows8([sm["conv_b"], sm["b_gate_a"], sm["b_gate_x"], sm["lru_param"]], D_RNN)[:4]], axis=0)
    fp = _rows8([sm["ffn_conv_w"][0], sm["ffn_conv_w"][1], sm["ffn_conv_w"][2], sm["ffn_conv_b"]], 2 * D_FF)
    ng = _rows8([jnp.concatenate([sm["q_norm_g"].reshape(-1), sm["kv_norm_g"].reshape(-1), jnp.zeros((MLA_W - Q_LORA - KV_LORA,), F32)])], MLA_W)
    fpar = _rows8([mod[5], sm["final_g"]], d)

    h = _norm_mod_fwd(x, gmod1, "norm1_fwd")
    proj_rnn = _mm(h, w_in_rnn, name="mm_in_rnn")
    proj_mla = _mm(h, w_in_mla, name="mm_in_mla")
    proj_g = _mm(h, w_in_g, out_dtype=BF16, name="mm_in_g")
    xc, ra, ix, hs = _rnn_fwd(proj_rnn, keep, rp, wa_bd, wx_bd)
    qn, kvn, kr = _mla_prep_fwd(proj_mla, rot_c, rot_s, ng)
    wts.update(fetch(("w_uq", "w_ukv"), kr))
    w_uq_p = jnp.pad(wts["w_uq"].reshape(Q_LORA, N_HEADS, hd), ((0, 0), (0, 0), (0, LANES - hd))).reshape(Q_LORA, N_HEADS * LANES)
    w_ukv = wts["w_ukv"]
    q_rot, q_rot_t = _rope_fwd_t(_mm(qn, w_uq_p, name="mm_uq"), rot_c, rot_s)
    kv, kvt = _mm(kvn, w_ukv, out_dtype=BF16, also_t=BF16, name="mm_ukv")
    krt = jnp.transpose(kr)
    y_mla, lse = _att_fwd(q_rot, kv, kvt, krt)
    wts.update(fetch(("w_proj_rnn", "w_proj_mla", "w_out", "w_up", "w_down"), lse))
    pr = _mm(hs, wts["w_proj_rnn"], out_dtype=BF16, name="mm_proj_rnn")
    pm = _mm(y_mla, wts["w_proj_mla"], out_dtype=BF16, name="mm_proj_mla")
    merged = _merge_fwd(pr, pm, proj_g)
    o = _mm(merged, wts["w_out"], name="mm_out")
    x1, h2 = _resid_norm_fwd(x, o, gmod2)
    up = _mm(h2, wts["w_up"], name="mm_up")
    act = _ffn_act_fwd(up, fp)
    dn = _mm(act, wts["w_down"], name="mm_down")

    dx2, ddn, red_f = _final_fwd_bwd(x1, dn, target, fpar)
    dact = _mm(ddn, wts["w_down"], tb=True, name="mm_d_act")
    tok = emit("w_down", _mm(act, ddn, ta=True, out_dtype=BF16, name="mm_dw_down"))
    dup, red_ffn = _ffn_act_bwd(up, dact, fp + tok)
    dh2 = _mm(dup, wts["w_up"], tb=True, name="mm_d_h2")
    tok = tok + emit("w_up", _mm(dup, h2, ta=True, out_dtype=BF16, name="mm_dw_up"))
    dx1, do, red_2 = _norm2_bwd(x1, dh2, dx2, o, gmod2 + tok)
    dmerged = _mm(do, wts["w_out"], tb=True, name="mm_d_merged")
    tok = tok + emit("w_out", _mm(merged, do, ta=True, out_dtype=BF16, name="mm_dw_out"))
    dpr, dpm, dg = _merge_bwd(dmerged, pr, pm, proj_g)
    dy_rnn = _mm(dpr, wts["w_proj_rnn"], tb=True, name="mm_d_yrnn")
    tok = tok + emit("w_proj_rnn", _mm(hs, dpr, ta=True, out_dtype=BF16, name="mm_dw_proj_rnn"))
    dy_mla, dy_mla_t = _mm(dpm, wts["w_proj_mla"], tb=True, also_t=BF16, name="mm_d_ymla")
    tok = tok + emit("w_proj_mla", _mm(y_mla, dpm, ta=True, out_dtype=BF16, name="mm_dw_proj_mla"))
    dq_rot, dkvt, dkrt = _att_bwd(q_rot, q_rot_t, kv, kvt, kr, krt, y_mla, lse, dy_mla, dy_mla_t)
    dq = _rope_bwd(dq_rot, rot_c, rot_s)
    dqn = _mm(dq, w_uq_p, tb=True, name="mm_d_qn")
    dw_uq_pt = _mm(dq, qn, ta=True, out_dtype=BF16, name="mm_dw_uq")
    tok = tok + emit("w_uq", dw_uq_pt.reshape(N_HEADS, LANES, Q_LORA)[:, :hd].reshape(N_HEADS * hd, Q_LORA))
    dkvn = jnp.transpose(_mm(w_ukv, dkvt, name="mm_d_kvn"))
    tok = tok + emit("w_ukv", _mm(dkvt, kvn, out_dtype=BF16, name="mm_dw_ukv"))
    dproj_mla, red_m = _mla_prep_bwd(proj_mla, dqn, dkvn, jnp.transpose(dkrt), rot_c, rot_s, ng + tok)
    dx_rnn, dwa_bd, dwx_bd, red_r = _rnn_bwd(proj_rnn, xc, ra, ix, hs, dy_rnn, keep, rp + tok, wa_bd, wx_bd)
    dw_in_t = jnp.concatenate([
        _mm(dx_rnn, h, ta=True, out_dtype=BF16, name="mm_dw_in_rnn"),
        _mm(dproj_mla, h, ta=True, out_dtype=BF16, name="mm_dw_in_mla")[:o_mla - o_rnn],
        _mm(dg, h, ta=True, out_dtype=BF16, name="mm_dw_in_g")], axis=0)
    tok = tok + emit("w_in", dw_in_t)
    dh_a = _mm(dx_rnn, w_in_rnn, tb=True, name="mm_d_h_rnn")
    dh_b = _mm(dproj_mla, w_in_mla, tb=True, name="mm_d_h_mla")
    dh_c = _mm(dg, w_in_g, tb=True, name="mm_d_h_g")
    grad_x, red_1 = _norm1_bwd(x, dh_a, dh_b, dh_c, dx1, gmod1 + tok)

    gs = {
        "norm1_g": red_1[0], "conv_w": red_r[0:4], "conv_b": red_r[4], "w_gate_a": _block_diag_t(dwa_bd),
        "b_gate_a": red_r[5], "w_gate_x": _block_diag_t(dwx_bd), "b_gate_x": red_r[6], "lru_param": red_r[7],
        "q_norm_g": red_m[0, :Q_LORA], "kv_norm_g": red_m[0, Q_LORA:Q_LORA + KV_LORA], "norm2_g": red_2[0],
        "ffn_conv_w": red_ffn[0:3], "ffn_conv_b": red_ffn[3], "final_g": red_f[0],
    }
    dmod = jnp.stack([red_1[2], red_1[1], red_2[3], red_2[2], red_2[1], red_f[1]], axis=0)
    return red_f[2, 0], grad_x, gs, dmod


MESH_IDS = pl.DeviceIdType.MESH
HBM_SPEC = pl.BlockSpec(memory_space=pltpu.HBM)


def _my_slot():
    return 4 * lax.axis_index("x") + 2 * lax.axis_index("y") + lax.axis_index("c")


def _all_gather(arrs, name):
    n = len(arrs)

    def body(*refs):
        ins, outs = refs[:n], refs[n:2 * n]
        send_sems, recv_sems, local_sems = refs[2 * n:]
        x, y, c = lax.axis_index("x"), lax.axis_index("y"), lax.axis_index("c")
        me, sibling = (x, y, c), (x, y, 1 - c)
        chips = [(1 - x, y), (x, 1 - y), (1 - x, 1 - y)]

        def slot(dev):
            return 4 * dev[0] + 2 * dev[1] + dev[2]

        def copy(a, k, block, to, src=None):
            dst = outs[a].at[slot(block)]
            return pltpu.make_async_remote_copy(
                src_ref=dst if src is None else src, dst_ref=dst, send_sem=send_sems.at[a, k], recv_sem=recv_sems.at[a, k],
                device_id=to, device_id_type=MESH_IDS)

        mine = [pltpu.make_async_copy(ins[a], outs[a].at[slot(me)], local_sems.at[a]) for a in range(n)]
        for cp in mine:
            cp.start()
        first = []
        for a in range(n):
            first.append(copy(a, 0, me, sibling, src=ins[a]))
            first += [copy(a, 1 + j, me, (*chip, c), src=ins[a]) for j, chip in enumerate(chips)]
        for cp in first:
            cp.start()
        passed = []
        for j, chip in enumerate(chips):
            for a in range(n):
                copy(a, 1 + j, (*chip, c), me).wait_recv()
                fwd = copy(a, 4 + j, (*chip, c), sibling)
                fwd.start()
                passed.append(fwd)
        for a in range(n):
            copy(a, 0, sibling, me).wait_recv()
            for j, chip in enumerate(chips):
                copy(a, 4 + j, (*chip, 1 - c), me).wait_recv()
        for cp in first + passed:
            cp.wait_send()
        for cp in mine:
            cp.wait()

    return pl.pallas_call(
        body, name=name,
        in_specs=[HBM_SPEC] * n, out_specs=[HBM_SPEC] * n,
        out_shape=[jax.ShapeDtypeStruct((N_DEV,) + a.shape, a.dtype) for a in arrs],
        scratch_shapes=[pltpu.SemaphoreType.DMA((n, 7)), pltpu.SemaphoreType.DMA((n, 7)), pltpu.SemaphoreType.DMA((n,))],
    )(*arrs)


SEM_SPEC =pl.BlockSpec(memory_space=pltpu.SEMAPHORE)
DATAFLOW = pltpu.SideEffectType.DATAFLOW_SIDE_EFFECTING
FLIPS = [(dx, dy, dc) for dx in (0, 1) for dy in (0, 1) for dc in (0, 1)][1:]


def _peer(k):
    dx, dy, dc = FLIPS[k]
    peer = (lax.axis_index("x") ^ dx, lax.axis_index("y") ^ dy, lax.axis_index("c") ^ dc)
    return peer, 4 * peer[0] + 2 * peer[1] + peer[2]


def _gather_start(shards, after, name):
    n, nf = len(shards), len(FLIPS)

    def body(*refs):
        srcs, lands = refs[:n], refs[n:2 * n]
        send_sems, recv_sems = refs[2 * n + 1:3 * n + 1], refs[3 * n + 1:4 * n + 1]
        token = refs[-1]
        me = _my_slot()
        for a in range(n):
            for k in range(nf):
                peer, _ = _peer(k)
                pltpu.make_async_remote_copy(
                    src_ref=srcs[a], dst_ref=lands[a].at[me], send_sem=send_sems[a].at[k], recv_sem=recv_sems[a].at[k],
                    device_id=peer, device_id_type=MESH_IDS).start()
        token[...] = jnp.zeros(token.shape, F32)

    land_shapes = [(N_DEV,) + a.shape for a in shards]
    sems = [pltpu.SemaphoreType.DMA((nf,))] * n
    out = pl.pallas_call(
        body, name=name,
        out_shape=(*sems, *sems, *[pltpu.HBM(a.shape, a.dtype) for a in shards],
                   *[pltpu.HBM(shp, a.dtype) for shp, a in zip(land_shapes, shards)],
                   jax.ShapeDtypeStruct((SUBLANES, LANES), F32)),
        in_specs=[HBM_SPEC] * (2 * n) + [pl.BlockSpec(memory_space=pl.ANY)],
        out_specs=(*[SEM_SPEC] * (2 * n), *[HBM_SPEC] * (2 * n), pl.BlockSpec(memory_space=pltpu.VMEM)),
        input_output_aliases={i: 2 * n + i for i in range(2 * n)},
        compiler_params=pltpu.CompilerParams(has_side_effects=DATAFLOW),
    )(*[pltpu.with_memory_space_constraint(a, pltpu.HBM) for a in shards],
      *[pltpu.with_memory_space_constraint(lax.empty(shp, a.dtype), pltpu.HBM) for shp, a in zip(land_shapes, shards)],
      after)
    return [(out[a], out[n + a], out[2 * n + a], out[3 * n + a]) for a in range(n)], out[-1]


def _gather_wait(flights, after, name):
    n, nf = len(flights), len(FLIPS)

    def body(*refs):
        send_sems, recv_sems = refs[:n], refs[n:2 * n]
        srcs, lands = refs[2 * n:3 * n], refs[3 * n:4 * n]
        for a in range(n):
            for k in range(nf):
                peer, peer_slot = _peer(k)
                cp = pltpu.make_async_remote_copy(
                    src_ref=srcs[a], dst_ref=lands[a].at[peer_slot], send_sem=send_sems[a].at[k],
                    recv_sem=recv_sems[a].at[k], device_id=peer, device_id_type=MESH_IDS)
                cp.wait_send()
                cp.wait_recv()

    srcs, lands = [f[2] for f in flights], [f[3] for f in flights]
    out = pl.pallas_call(
        body, name=name,
        out_shape=(*[pltpu.HBM(a.shape, a.dtype) for a in srcs], *[pltpu.HBM(a.shape, a.dtype) for a in lands]),
        in_specs=[SEM_SPEC] * (2 * n) + [HBM_SPEC] * (2 * n) + [pl.BlockSpec(memory_space=pl.ANY)],
        out_specs=tuple([HBM_SPEC] * (2 * n)),
        input_output_aliases={2 * n + i: i for i in range(2 * n)},
        compiler_params=pltpu.CompilerParams(has_side_effects=DATAFLOW),
    )(*[f[0] for f in flights], *[f[1] for f in flights], *srcs, *lands, after)
    return list(out[n:])


ROW_ALIGN = 16


def _span_start(slot, rows):
    return (rows * slot) // ROW_ALIGN * ROW_ALIGN


def _chunk_of(src_ref, slot, rows, span):
    if rows is None:
        return src_ref.at[slot]
    return src_ref.at[pl.ds(pl.multiple_of(_span_start(slot, rows), ROW_ALIGN), span)]


def _scatter_start(src, name, rows=None, span=None):
    def body(src_ref, land_ref, send_sems, recv_sems, src_thru, land_thru, token):
        me = _my_slot()
        for k in range(len(FLIPS)):
            peer, peer_slot = _peer(k)
            pltpu.make_async_remote_copy(
                src_ref=_chunk_of(src_ref, peer_slot, rows, span), dst_ref=land_ref.at[me], send_sem=send_sems.at[k],
                recv_sem=recv_sems.at[k], device_id=peer, device_id_type=MESH_IDS).start()
        token[...] = jnp.zeros(token.shape, F32)

    n = len(FLIPS)
    land_shape = src.shape if rows is None else (N_DEV, span, src.shape[1])
    return pl.pallas_call(
        body, name=name,
        out_shape=(pltpu.SemaphoreType.DMA((n,)), pltpu.SemaphoreType.DMA((n,)), pltpu.HBM(src.shape, src.dtype),
                   pltpu.HBM(land_shape, src.dtype), jax.ShapeDtypeStruct((SUBLANES, LANES), F32)),
        in_specs=(HBM_SPEC, HBM_SPEC),
        out_specs=(SEM_SPEC, SEM_SPEC, HBM_SPEC, HBM_SPEC, pl.BlockSpec(memory_space=pltpu.VMEM)),
        input_output_aliases={0: 2, 1: 3},
        compiler_params=pltpu.CompilerParams(has_side_effects=DATAFLOW),
    )(pltpu.with_memory_space_constraint(src, pltpu.HBM),
      pltpu.with_memory_space_constraint(lax.empty(land_shape, src.dtype), pltpu.HBM))


def _scatter_wait(send_sems, recv_sems, src_thru, land_thru, after, name, rows=None, span=None):
    def body(src_ref, land_ref, send_sems, recv_sems, after_ref, src_dead, got_ref):
        for k in range(len(FLIPS)):
            peer, peer_slot = _peer(k)
            cp = pltpu.make_async_remote_copy(
                src_ref=_chunk_of(src_ref, peer_slot, rows, span), dst_ref=land_ref.at[peer_slot], send_sem=send_sems.at[k],
                recv_sem=recv_sems.at[k], device_id=peer, device_id_type=MESH_IDS)
            cp.wait_send()
            cp.wait_recv()

    return pl.pallas_call(
        body, name=name,
        out_shape=(pltpu.HBM(src_thru.shape, src_thru.dtype), pltpu.HBM(land_thru.shape, land_thru.dtype)),
        in_specs=(HBM_SPEC, HBM_SPEC, SEM_SPEC, SEM_SPEC, pl.BlockSpec(memory_space=pl.ANY)),
        out_specs=(HBM_SPEC, HBM_SPEC), input_output_aliases={0: 0, 1: 1},
        compiler_params=pltpu.CompilerParams(has_side_effects=DATAFLOW),
    )(src_thru, land_thru, send_sems, recv_sems, after)


def _sum_sources(parts, own, name):
    k, r, c = parts.shape
    tr = r if k * r * c <= 2 * 1024 * 1024 else _pick(r, (512, 256, 128, 64, 32, 16, 8))

    def body(p_ref, own_ref, o_ref):
        me = _my_slot()
        acc = jnp.where(me == 0, own_ref[...], p_ref[0]).astype(F32)
        for s in range(1, k):
            acc = acc + jnp.where(me == s, own_ref[...], p_ref[s]).astype(F32)
        o_ref[...] = acc

    blk = pl.BlockSpec((tr, c), lambda i: (i, 0))
    return pl.pallas_call(
        body, name=name, grid=(r // tr,),
        in_specs=[pl.BlockSpec((k, tr, c), lambda i: (0, i, 0)), blk],
        out_specs=blk,
        out_shape=jax.ShapeDtypeStruct((r, c), F32),
        compiler_params=_cparams(("arbitrary",)),
    )(parts, own)


def _adamw_math(g, w, m, v):
    m_new = ADAM_B1 * m + (1.0 - ADAM_B1) * g
    v_new = ADAM_B2 * v + (1.0 - ADAM_B2) * jnp.square(g)
    m_hat = m_new / (1.0 - ADAM_B1 ** ADAM_STEP)
    v_hat = v_new / (1.0 - ADAM_B2 ** ADAM_STEP)
    return -ADAM_LR * (m_hat / (jnp.sqrt(v_hat) + ADAM_EPS) + ADAM_WD * w), m_new, v_new


def _adamw_many(gs, ws, ms, vs, name):
    n = len(gs)

    def body(*refs):
        ins, outs = refs[:4 * n], refs[4 * n:]
        for i in range(n):
            g = ins[i][...]
            outs[4 * i][...] = g
            outs[4 * i + 1][...], outs[4 * i + 2][...], outs[4 * i + 3][...] = _adamw_math(
                g, ins[n + i][...], ins[2 * n + i][...], ins[3 * n + i][...])

    return pl.pallas_call(
        body, name=name,
        out_shape=[jax.ShapeDtypeStruct(a.shape, F32) for a in ws for _ in range(4)],
        compiler_params=_cparams(),
    )(*gs, *ws, *ms, *vs)


def _adamw(parts, w, m, v, name, own=None):
    k, r, c = parts.shape
    tr = r if r * c <= 256 * 1024 else _pick(r, (256, 128, 64, 32, 16, 8))

    def body(*refs):
        p_ref, w_ref, m_ref, v_ref = refs[:4]
        g_ref, d_ref, nm_ref, nv_ref = refs[-4:]

        def part(s):
            if own is None:
                return p_ref[s].astype(F32)
            return jnp.where(_my_slot() == s, refs[4][...], p_ref[s]).astype(F32)

        g = part(0)
        for s in range(1, k):
            g = g + part(s)
        g_ref[...] = g
        d_ref[...], nm_ref[...], nv_ref[...] = _adamw_math(g, w_ref[...], m_ref[...], v_ref[...])

    blk = pl.BlockSpec((tr, c), lambda i: (i, 0))
    return pl.pallas_call(
        body, name=name, grid=(r // tr,),
        in_specs=[pl.BlockSpec((k, tr, c), lambda i: (0, i, 0)), blk, blk, blk] + ([] if own is None else [blk]),
        out_specs=[blk] * 4,
        out_shape=[jax.ShapeDtypeStruct((r, c), F32)] * 4,
        compiler_params=_cparams(("arbitrary",)),
    )(parts, w, m, v, *([] if own is None else [own]))


def _silu(v):
    return v * _sigmoid(v)


def _ada_fwd(c_all, w, b):
    def body(c_ref, w_ref, b_ref, o_ref):
        ca = _silu(c_ref[...]).astype(BF16)
        o_ref[...] = jnp.dot(ca, w_ref[...].astype(BF16), preferred_element_type=F32) + b_ref[...]

    return pl.pallas_call(
        body, name="ada_fwd", out_shape=jax.ShapeDtypeStruct((c_all.shape[0], w.shape[1]), F32),
        compiler_params=_cparams(),
    )(c_all, w, b)


def _ada_bwd(c_all, dmod):
    def body(c_ref, d_ref, o_ref):
        ca = _silu(c_ref[...]).astype(BF16).astype(F32)
        dm = d_ref[...].astype(BF16).astype(F32)
        acc = jnp.zeros(o_ref.shape, F32)
        for bi in range(c_all.shape[0]):
            acc = acc + jnp.transpose(ca[bi:bi + 1, :]) * dm[bi:bi + 1, :]
        o_ref[...] = acc

    return pl.pallas_call(
        body, name="ada_bwd", out_shape=jax.ShapeDtypeStruct((c_all.shape[1], dmod.shape[1]), F32),
        compiler_params=_cparams(),
    )(c_all, dmod)


COL_SHARDED = ("w_in", "w_uq", "w_ukv", "w_up")
ROW_SHARDED = ("w_proj_rnn", "w_proj_mla", "w_out", "w_down")
REPLICATED = ("b_ada", "norm1_g", "conv_b", "w_gate_a", "b_gate_a", "w_gate_x", "b_gate_x", "lru_param", "q_norm_g",
              "kv_norm_g", "norm2_g", "ffn_conv_b", "final_g")
WEIGHTS = ("w_ada", "b_ada", "norm1_g", "w_in", "conv_w", "conv_b", "w_gate_a", "b_gate_a", "w_gate_x", "b_gate_x",
           "lru_param", "q_norm_g", "w_uq", "kv_norm_g", "w_ukv", "w_proj_rnn", "w_proj_mla", "w_out", "norm2_g", "w_up",
           "ffn_conv_w", "ffn_conv_b", "w_down", "final_g")
TRANSPOSED_GRADS = COL_SHARDED
PACK_LANES = 128


def _pack(vecs, row_multiple=SUBLANES):
    flat = jnp.concatenate([v.reshape(-1).astype(F32) for v in vecs])
    pad = (-flat.shape[0]) % (PACK_LANES * row_multiple)
    return jnp.concatenate([flat, jnp.zeros((pad,), F32)]).reshape(-1, PACK_LANES)


def _unpack(packed, shapes):
    flat = packed.reshape(-1)
    out, off = [], 0
    for shp in shapes:
        size = math.prod(shp)
        out.append(flat[off:off + size].reshape(shp))
        off += size
    return out


def kernel(x, c, positions, w_ada, b_ada, norm1_g, w_in, conv_w, conv_b, w_gate_a, b_gate_a, w_gate_x, b_gate_x, lru_param, q_norm_g, w_uq, kv_norm_g, w_ukv, w_proj_rnn, w_proj_mla, w_out, norm2_g, w_up, ffn_conv_w, ffn_conv_b, w_down, final_g, loss_target, m_w_ada, m_b_ada, m_norm1_g, m_w_in, m_conv_w, m_conv_b, m_w_gate_a, m_b_gate_a, m_w_gate_x, m_b_gate_x, m_lru_param, m_q_norm_g, m_w_uq, m_kv_norm_g, m_w_ukv, m_w_proj_rnn, m_w_proj_mla, m_w_out, m_norm2_g, m_w_up, m_ffn_conv_w, m_ffn_conv_b, m_w_down, m_final_g, v_w_ada, v_b_ada, v_norm1_g, v_w_in, v_conv_w, v_conv_b, v_w_gate_a, v_b_gate_a, v_w_gate_x, v_b_gate_x, v_lru_param, v_q_norm_g, v_w_uq, v_kv_norm_g, v_w_ukv, v_w_proj_rnn, v_w_proj_mla, v_w_out, v_norm2_g, v_w_up, v_ffn_conv_w, v_ffn_conv_b, v_w_down, v_final_g):
    args = dict(locals())
    w = {n: args[n] for n in WEIGHTS}
    m = {n: args["m_" + n] for n in WEIGHTS}
    v = {n: args["v_" + n] for n in WEIGHTS}
    s, d = x.shape[1], x.shape[2]
    me = _my_slot()
    def two_d(a):
        assert a.ndim == 3 and a.shape[0] == 1, a.shape
        return a[0]

    big = COL_SHARDED + ROW_SHARDED
    shard = {n: two_d(w[n]).astype(BF16) for n in big}

    def whole(n, g):
        k, r, cc = g.shape
        return jnp.transpose(g, (1, 0, 2)).reshape(r, k * cc) if n in COL_SHARDED else g.reshape(k * r, cc)

    first = _all_gather([shard["w_in"], c, two_d(conv_w), two_d(ffn_conv_w)], "gather_first")
    c_all = first[1].reshape(N_DEV, d)
    conv_w_all = jnp.transpose(first[2], (1, 0, 2)).reshape(conv_w.shape[1], -1)
    ffn_conv_w_all = jnp.transpose(first[3], (1, 0, 2)).reshape(ffn_conv_w.shape[1], -1)

    ada_cols = w_ada.shape[2]
    b_cols = lax.dynamic_slice(b_ada, (0, me * ada_cols), (1, ada_cols))
    mod_cols = _ada_fwd(c_all, w_ada[0], b_cols)
    mod_all, = _all_gather([mod_cols], "gather_mod")

    later = ("w_uq", "w_ukv", "w_proj_rnn", "w_proj_mla", "w_out", "w_up", "w_down")
    flights, started = _gather_start([shard[n] for n in later], mod_all, "gather_start")
    flight = dict(zip(later, flights))

    def fetch(names, after):
        lands = _gather_wait([flight[n] for n in names], after, "gather_wait_" + names[0])
        return {n: whole(n, lax.dynamic_update_index_in_dim(g, shard[n], me, 0)) for n, g in zip(names, lands)}

    mod = lax.dynamic_index_in_dim(mod_all, me, axis=1, keepdims=False).reshape(6, d) + started[0, 0]

    sm = {n: w[n][0] for n in REPLICATED if n not in ("b_ada", "final_g")}
    sm["final_g"] = final_g
    sm["conv_w"] = conv_w_all
    sm["ffn_conv_w"] = ffn_conv_w_all
    in_flight, windows = {}, {}

    def emit(n, g):
        rows = g.shape[0] // N_DEV
        if rows % ROW_ALIGN == 0:
            windows[n] = (None, None)
            g = g.reshape(N_DEV, rows, g.shape[1])
        else:
            span = max(rows * k - _span_start(k, rows) for k in range(N_DEV)) + rows
            windows[n] = (rows, -(-span // ROW_ALIGN) * ROW_ALIGN)
            assert _span_start(N_DEV - 1, rows) + windows[n][1] <= g.shape[0], (n, g.shape)
        *in_flight[n], token = _scatter_start(g, "scatter_start_" + n, *windows[n])
        return token[0, 0]

    sq, grad_x, gs, dmod = _local_step(x[0], mod, positions[0], loss_target[0], whole("w_in", first[0]), fetch, sm, emit)

    small_names = [n for n in REPLICATED if n != "b_ada"] + ["conv_w", "ffn_conv_w"]
    small_shapes = [gs[n].shape for n in small_names] + [(6 * d,), (1,)]
    partial = _pack([gs[n] for n in small_names] + [dmod, sq.reshape(1)], N_DEV * SUBLANES)
    *small_flight, small_started = _scatter_start(partial.reshape(N_DEV, -1, PACK_LANES), "scatter_small_start")

    grads, deltas, new_m, new_v = {}, {}, {}, {}

    def update(n, parts, own=None):
        shp = w[n].shape
        lay = jnp.transpose if n in TRANSPOSED_GRADS else (lambda a: a)
        res = _adamw(parts, lay(two_d(w[n])), lay(two_d(m[n])), lay(two_d(v[n])), "adamw_" + n, own)
        grads[n], deltas[n], new_m[n], new_v[n] = [lay(a).reshape(shp) for a in res]

    for n in big:
        rows, span = windows[n]
        src, landed = _scatter_wait(*in_flight[n], small_started, "scatter_wait_" + n, rows, span)
        if rows is None:
            update(n, landed, lax.dynamic_index_in_dim(src, me, axis=0, keepdims=False))
        else:
            start = _span_start(me, rows)
            own = lax.dynamic_slice(src, (start, 0), (span, src.shape[1]))
            total = _sum_sources(landed, own, "sum_" + n)
            update(n, lax.dynamic_slice(total, (rows * me - start, 0), (rows, src.shape[1]))[None])

    chunks, landed = _scatter_wait(*small_flight, new_v[big[-1]], "scatter_small_wait")
    mine = _sum_sources(landed, lax.dynamic_index_in_dim(chunks, me, axis=0, keepdims=False), "sum_small")
    summed_all, dmod_all = _all_gather([mine, dmod.reshape(1, 6 * d)], "gather_small")
    summed = _unpack(summed_all, small_shapes)
    g_small = dict(zip(small_names, summed[:len(small_names)]))
    g_small["b_ada"] = summed[len(small_names)]
    loss = 0.5 * summed[-1][0] / d
    dmod_cols = lax.dynamic_slice(dmod_all.reshape(N_DEV, 6 * d), (0, me * ada_cols), (N_DEV, ada_cols))

    update("w_ada", _ada_bwd(c_all, dmod_cols)[None])

    for n in ("conv_w", "ffn_conv_w"):
        cols = w[n].shape[2]
        g_small[n] = lax.dynamic_slice(g_small[n], (0, me * cols), (g_small[n].shape[0], cols))
    small = REPLICATED + ("conv_w", "ffn_conv_w")
    as_rows = lambda a: a.reshape(1, -1) if a.ndim == 1 else a
    res = _adamw_many([as_rows(g_small[n].reshape(w[n].shape)) for n in small], [as_rows(w[n]) for n in small],
                      [as_rows(m[n]) for n in small], [as_rows(v[n]) for n in small], "adamw_small")
    for i, n in enumerate(small):
        grads[n], deltas[n], new_m[n], new_v[n] = [a.reshape(w[n].shape) for a in res[4 * i:4 * i + 4]]

    return (loss, grad_x[None], *[grads[n] for n in WEIGHTS], *[deltas[n] for n in WEIGHTS],
            *[new_m[n] for n in WEIGHTS], *[new_v[n] for n in WEIGHTS])
```

```python
import functools
import math

import jax
import jax.numpy as jnp
from jax import lax
from jax.experimental import pallas as pl
from jax.experimental.pallas import tpu as pltpu

F32 = jnp.float32
BF16 = jnp.bfloat16

N_DEV = 8
LANES = 128
SUBLANES = 8
VMEM_LIMIT = 56 * 1024 * 1024

D_RNN = 1280
Q_LORA = 384
KV_LORA = 256
QK_NOPE = 64
QK_ROPE = 32
V_HEAD = 64
N_HEADS = 16
D_FF = 2816
ROPE_THETA = 10000.0
LRU_C = 8.0
EPS = 1e-6
MLA_W = 768
ATT_SCALE = 1.0 / math.sqrt(QK_NOPE + QK_ROPE)

ADAM_LR, ADAM_B1, ADAM_B2, ADAM_EPS, ADAM_WD, ADAM_STEP = 0.001, 0.9, 0.999, 1e-08, 0.01, 10


def _cparams(sem=None):
    return pltpu.CompilerParams(dimension_semantics=sem, vmem_limit_bytes=VMEM_LIMIT)


def _pick(n, prefs):
    for p in prefs:
        if n % p == 0:
            return p
    return n


def _sigmoid(v):
    return 0.5 * jnp.tanh(0.5 * v) + 0.5


def _lane(shape):
    return lax.broadcasted_iota(jnp.int32, shape, len(shape) - 1)


def _row(shape):
    return lax.broadcasted_iota(jnp.int32, shape, len(shape) - 2)


MM_BLOCK_BYTES = 36 * 1024 * 1024


def _divisors(n):
    return [t for t in range(n, 0, -LANES) if n % t == 0] if n % LANES == 0 else [n]


HBM_BYTES_PER_US = 3.0e6
MXU_FLOPS_PER_US = 8.0e8
GRID_STEP_US = 0.35


def _mm_tiles(m, n, k, a_bytes, b_bytes, o_bytes):
    best = None
    for tm in [t for t in _divisors(m) if t <= 1024]:
        for tn in [t for t in _divisors(n) if t <= 2048]:
            for tk in _divisors(k):
                nk = k // tk
                need = 2 * (tm * tk * a_bytes + tk * tn * b_bytes + tm * tn * o_bytes) + (tm * tn * 4 if nk > 1 else 0)
                if need > MM_BLOCK_BYTES:
                    continue
                gi, gj = m // tm, n // tn
                for rows_outer in (True, False):
                    if nk > 1:
                        a_reads, b_reads = gj, gi
                    elif rows_outer:
                        a_reads, b_reads = 1, (gi if gj > 1 else 1)
                    else:
                        a_reads, b_reads = (gj if gi > 1 else 1), 1
                    traffic = m * k * a_bytes * a_reads + k * n * b_bytes * b_reads + m * n * (o_bytes + (8 * nk if nk > 1 else 0))
                    cost = max(traffic / HBM_BYTES_PER_US, 2.0 * m * n * k / MXU_FLOPS_PER_US) + gi * gj * nk * GRID_STEP_US
                    if best is None or cost < best[0]:
                        best = (cost, tm, tn, tk, rows_outer)
                break
    if best is None:
        raise ValueError((m, n, k))
    return best[1:]


def _mm(a, b, *, ta=False, tb=False, out_dtype=F32, also_t=None, name):
    (k_a, m) = a.shape if ta else a.shape[::-1]
    (n, k_b) = b.shape if tb else b.shape[::-1]
    assert k_a == k_b, (a.shape, b.shape, ta, tb)
    k = k_a
    tm, tn, tk, rows_outer = _mm_tiles(m, n, k, a.dtype.itemsize, b.dtype.itemsize, jnp.dtype(out_dtype).itemsize)
    nk = k // tk
    dims = (((0 if ta else 1,), (1 if tb else 0,)), ((), ()))
    n_out = 1 if also_t is None else 2

    def body(a_ref, b_ref, *rest):
        outs, acc = rest[:n_out], rest[n_out:]
        part = lax.dot_general(a_ref[...].astype(BF16), b_ref[...].astype(BF16), dims, preferred_element_type=F32)

        def write(val):
            outs[0][...] = val.astype(out_dtype)
            if also_t is not None:
                outs[1][...] = jnp.transpose(val).astype(also_t)

        if nk == 1:
            write(part)
            return
        acc_ref, = acc
        kk = pl.program_id(2)

        @pl.when(kk == 0)
        def _():
            acc_ref[...] = part

        @pl.when(kk > 0)
        def _():
            acc_ref[...] += part

        @pl.when(kk == nk - 1)
        def _():
            write(acc_ref[...])

    def ij(f):
        return (lambda i, j, kk: f(i, j, kk)) if rows_outer else (lambda j, i, kk: f(i, j, kk))

    a_spec = pl.BlockSpec((tk, tm), ij(lambda i, j, kk: (kk, i))) if ta else pl.BlockSpec((tm, tk), ij(lambda i, j, kk: (i, kk)))
    b_spec = pl.BlockSpec((tn, tk), ij(lambda i, j, kk: (j, kk))) if tb else pl.BlockSpec((tk, tn), ij(lambda i, j, kk: (kk, j)))
    out_specs = [pl.BlockSpec((tm, tn), ij(lambda i, j, kk: (i, j)))]
    out_shape = [jax.ShapeDtypeStruct((m, n), out_dtype)]
    if also_t is not None:
        out_specs.append(pl.BlockSpec((tn, tm), ij(lambda i, j, kk: (j, i))))
        out_shape.append(jax.ShapeDtypeStruct((n, m), also_t))
    res = pl.pallas_call(
        body, name=name,
        grid=(m // tm, n // tn, nk) if rows_outer else (n // tn, m // tm, nk),
        in_specs=[a_spec, b_spec], out_specs=out_specs, out_shape=out_shape,
        scratch_shapes=[] if nk == 1 else [pltpu.VMEM((tm, tn), F32)],
        compiler_params=_cparams(("arbitrary", "arbitrary", "arbitrary")),
    )(a, b)
    return res[0] if also_t is None else res


def _rowwise(fn, row_ins, par_ins, out_defs, red_defs, *, name, tr=256):
    s = row_ins[0].shape[0]
    tr = min(tr, s)
    nr, npar, no = len(row_ins), len(par_ins), len(out_defs)

    def body(*refs):
        rin, pin = refs[:nr], refs[nr:nr + npar]
        outs, reds = refs[nr + npar:nr + npar + no], refs[nr + npar + no:]
        i = pl.program_id(0)

        @pl.when(i == 0)
        def _():
            for r in reds:
                r[...] = jnp.zeros_like(r)

        fn(i, rin, pin, outs, reds)

    in_specs = [pl.BlockSpec((tr, a.shape[1]), lambda i: (i, 0)) for a in row_ins]
    in_specs += [pl.BlockSpec(a.shape, lambda i, nd=a.ndim: (0,) * nd) for a in par_ins]
    out_specs = [pl.BlockSpec((tr, c), lambda i: (i, 0)) for c, _ in out_defs]
    out_specs += [pl.BlockSpec(shp, lambda i: (0, 0)) for shp in red_defs]
    out_shape = [jax.ShapeDtypeStruct((s, c), dt) for c, dt in out_defs]
    out_shape += [jax.ShapeDtypeStruct(shp, F32) for shp in red_defs]
    return pl.pallas_call(
        body, name=name, grid=(s // tr,), in_specs=in_specs, out_specs=out_specs, out_shape=out_shape,
        compiler_params=_cparams(("arbitrary",)),
    )(*row_ins, *par_ins)


def _rms(v):
    return lax.rsqrt(jnp.mean(v * v, axis=-1, keepdims=True) + EPS)


def _colsum(v):
    return jnp.sum(v, axis=0, keepdims=True)


def _rms_bwd(dn, n, rstd):
    return rstd * (dn - n * jnp.mean(dn * n, axis=-1, keepdims=True))


def _norm_mod_fwd(x, gmod, name):
    def fn(i, rin, pin, outs, reds):
        xv = rin[0][...]
        p = pin[0][...]
        n = xv * _rms(xv)
        outs[0][...] = ((n * p[0:1]) * (1.0 + p[1:2]) + p[2:3]).astype(BF16)

    return _rowwise(fn, [x], [gmod], [(x.shape[1], BF16)], [], name=name)[0]


def _rope(v, rot_c, rot_s):
    half = QK_ROPE // 2
    swapped = jnp.where(_lane(v.shape) < QK_NOPE + half, pltpu.roll(v, LANES - half, 1), pltpu.roll(v, half, 1))
    return v * rot_c + swapped * rot_s


def _rope_t(dv, rot_c, rot_s):
    half = QK_ROPE // 2
    ds = dv * rot_s
    lane = _lane(dv.shape)
    swapped = jnp.where(lane < QK_NOPE + half, pltpu.roll(ds, LANES - half, 1), pltpu.roll(ds, half, 1))
    in_rope = (lane >= QK_NOPE) & (lane < QK_NOPE + QK_ROPE)
    return dv * rot_c + jnp.where(in_rope, swapped, 0.0)


def _mla_prep_fwd(proj_mla, rot_c, rot_s, ng):
    o1, o2 = Q_LORA, Q_LORA + KV_LORA

    def fn(i, rin, pin, outs, reds):
        g = pin[0][...]
        ql = rin[0][:, 0:o1]
        kl = rin[0][:, o1:o2]
        outs[0][...] = (ql * _rms(ql) * g[0:1, 0:o1]).astype(BF16)
        outs[1][...] = (kl * _rms(kl) * g[0:1, o1:o2]).astype(BF16)
        kr = pltpu.roll(rin[0][:, o2:o2 + LANES], QK_NOPE, 1)
        outs[2][...] = _rope(kr, rin[1][...], rin[2][...]).astype(BF16)

    return _rowwise(fn, [proj_mla, rot_c, rot_s], [ng], [(Q_LORA, BF16), (KV_LORA, BF16), (LANES, BF16)], [],
                    name="mla_prep_fwd")


def _mla_prep_bwd(proj_mla, dqn, dkvn, dkr, rot_c, rot_s, ng):
    o1, o2 = Q_LORA, Q_LORA + KV_LORA

    def fn(i, rin, pin, outs, reds):
        g = pin[0][...]
        ql = rin[0][:, 0:o1]
        kl = rin[0][:, o1:o2]
        rq, rk = _rms(ql), _rms(kl)
        nq, nk = ql * rq, kl * rk
        dq, dk = rin[1][...], rin[2][...]
        outs[0][:, 0:o1] = _rms_bwd(dq * g[0:1, 0:o1], nq, rq).astype(BF16)
        outs[0][:, o1:o2] = _rms_bwd(dk * g[0:1, o1:o2], nk, rk).astype(BF16)
        dkr_pre = pltpu.roll(_rope_t(rin[3][...], rin[4][...], rin[5][...]), LANES - QK_NOPE, 1)
        outs[0][:, o2:] = jnp.where(_lane(dkr_pre.shape) < QK_ROPE, dkr_pre, 0.0).astype(BF16)
        reds[0][0:1, 0:o1] += _colsum(dq * nq)
        reds[0][0:1, o1:o2] += _colsum(dk * nk)

    return _rowwise(fn, [proj_mla, dqn, dkvn, dkr, rot_c, rot_s], [ng], [(MLA_W, BF16)], [(SUBLANES, MLA_W)],
                    name="mla_prep_bwd")


def _rope_bwd(dq, rot_c, rot_s):
    def fn(i, rin, pin, outs, reds):
        c, sn = rin[1][...] * Q_PRESCALE, rin[2][...] * Q_PRESCALE
        for h in range(N_HEADS):
            sl = slice(h * LANES, (h + 1) * LANES)
            outs[0][:, sl] = _rope_t(rin[0][:, sl], c, sn).astype(BF16)

    return _rowwise(fn, [dq, rot_c, rot_s], [], [(dq.shape[1], BF16)], [], name="rope_bwd")[0]


def _rope_fwd_t(q, rot_c, rot_s):
    s, c = q.shape
    tr = min(256, s)

    def body(q_ref, c_ref, s_ref, o_ref, ot_ref):
        cc, sn = c_ref[...] * Q_PRESCALE, s_ref[...] * Q_PRESCALE
        for h in range(N_HEADS):
            sl = slice(h * LANES, (h + 1) * LANES)
            rot = _rope(q_ref[:, sl], cc, sn)
            o_ref[:, sl] = rot.astype(BF16)
            ot_ref[sl, :] = jnp.transpose(rot).astype(BF16)

    return pl.pallas_call(
        body, name="rope_fwd", grid=(s // tr,),
        in_specs=[pl.BlockSpec((tr, c), lambda i: (i, 0)), pl.BlockSpec((tr, LANES), lambda i: (i, 0)),
                  pl.BlockSpec((tr, LANES), lambda i: (i, 0))],
        out_specs=[pl.BlockSpec((tr, c), lambda i: (i, 0)), pl.BlockSpec((c, tr), lambda i: (0, i))],
        out_shape=[jax.ShapeDtypeStruct((s, c), BF16), jax.ShapeDtypeStruct((c, s), BF16)],
        compiler_params=_cparams(("arbitrary",)),
    )(q, rot_c, rot_s)


def _merge_fwd(pr, pm, proj_g):
    d = pr.shape[1]

    def fn(i, rin, pin, outs, reds):
        g_rnn, g_mla = rin[2][:, 0:d].astype(F32), rin[2][:, d:].astype(F32)
        outs[0][...] = (_sigmoid(g_rnn) * rin[0][...].astype(F32) + _sigmoid(g_mla) * rin[1][...].astype(F32)).astype(BF16)

    return _rowwise(fn, [pr, pm, proj_g], [], [(d, BF16)], [], name="merge_fwd")[0]


def _merge_bwd(dmerged, pr, pm, proj_g):
    d = pr.shape[1]

    def fn(i, rin, pin, outs, reds):
        dm = rin[0][...]
        sr, sm = _sigmoid(rin[3][:, 0:d].astype(F32)), _sigmoid(rin[3][:, d:].astype(F32))
        outs[0][...] = (dm * sr).astype(BF16)
        outs[1][...] = (dm * sm).astype(BF16)
        outs[2][:, 0:d] = (dm * rin[1][...].astype(F32) * sr * (1.0 - sr)).astype(BF16)
        outs[2][:, d:] = (dm * rin[2][...].astype(F32) * sm * (1.0 - sm)).astype(BF16)

    return _rowwise(fn, [dmerged, pr, pm, proj_g], [], [(d, BF16), (d, BF16), (2 * d, BF16)], [], name="merge_bwd")


def _resid_norm_fwd(x, o, gmod):
    d = x.shape[1]

    def fn(i, rin, pin, outs, reds):
        p = pin[0][...]
        x1 = rin[0][...] + p[3:4] * rin[1][...]
        outs[0][...] = x1
        outs[1][...] = ((x1 * _rms(x1) * p[0:1]) * (1.0 + p[1:2]) + p[2:3]).astype(BF16)

    return _rowwise(fn, [x, o], [gmod], [(d, F32), (d, BF16)], [], name="resid_norm_fwd")


def _final_fwd_bwd(x1, dn, target, par):
    d = x1.shape[1]

    def fn(i, rin, pin, outs, reds):
        p = pin[0][...]
        dnv = rin[1][...]
        x2 = rin[0][...] + p[0:1] * dnv
        rstd = _rms(x2)
        n3 = x2 * rstd
        err = n3 * p[1:2] - rin[2][...]
        dy = err * (1.0 / d)
        dx2 = _rms_bwd(dy * p[1:2], n3, rstd)
        outs[0][...] = dx2
        outs[1][...] = (dx2 * p[0:1]).astype(BF16)
        reds[0][0:1, :] += _colsum(dy * n3)
        reds[0][1:2, :] += _colsum(dx2 * dnv)
        reds[0][2:3, :] += jnp.zeros((1, d), F32) + jnp.sum(err * err)

    return _rowwise(fn, [x1, dn, target], [par], [(d, F32), (d, BF16)], [(SUBLANES, d)], name="final_fwd_bwd")


def _norm2_bwd(x1, dh2, dx2, o, gmod):
    d = x1.shape[1]

    def fn(i, rin, pin, outs, reds):
        p = pin[0][...]
        x1v, dh = rin[0][...], rin[1][...]
        rstd = _rms(x1v)
        n2 = x1v * rstd
        dx1 = rin[2][...] + _rms_bwd(dh * (p[0:1] * (1.0 + p[1:2])), n2, rstd)
        outs[0][...] = dx1
        outs[1][...] = (dx1 * p[3:4]).astype(BF16)
        reds[0][0:1, :] += _colsum(dh * n2 * (1.0 + p[1:2]))
        reds[0][1:2, :] += _colsum(dh * n2 * p[0:1])
        reds[0][2:3, :] += _colsum(dh)
        reds[0][3:4, :] += _colsum(dx1 * rin[3][...])

    return _rowwise(fn, [x1, dh2, dx2, o], [gmod], [(d, F32), (d, BF16)], [(SUBLANES, d)], name="norm2_bwd")


def _norm1_bwd(x, dh_a, dh_b, dh_c, dx1, gmod):
    d = x.shape[1]

    def fn(i, rin, pin, outs, reds):
        p = pin[0][...]
        xv = rin[0][...]
        dh = rin[1][...] + rin[2][...] + rin[3][...]
        rstd = _rms(xv)
        n1 = xv * rstd
        outs[0][...] = rin[4][...] + _rms_bwd(dh * (p[0:1] * (1.0 + p[1:2])), n1, rstd)
        reds[0][0:1, :] += _colsum(dh * n1 * (1.0 + p[1:2]))
        reds[0][1:2, :] += _colsum(dh * n1 * p[0:1])
        reds[0][2:3, :] += _colsum(dh)

    return _rowwise(fn, [x, dh_a, dh_b, dh_c, dx1], [gmod], [(d, F32)], [(SUBLANES, d)], name="norm1_bwd")


RNN_CHUNK = 512


def _shift_down(ref, base, n, j):
    v = ref[pl.ds(base, n + SUBLANES), :]
    return v[SUBLANES:] if j == 0 else pltpu.roll(v, j, 0)[SUBLANES:]


def _shift_up(ref, base, n, j, top_pad):
    v = ref[pl.ds(base + top_pad, n + SUBLANES), :]
    return v[:n] if j == 0 else pltpu.roll(v, n + SUBLANES - j, 0)[:n]


SCAN_GROUP = 128


def _scan_sizes(s):
    sizes = [s]
    while sizes[-1] > SUBLANES:
        assert sizes[-1] % SUBLANES == 0, s
        sizes.append(sizes[-1] // SUBLANES)
    return sizes


def _scan_scratch(s):
    return [pltpu.VMEM((n + 2 * SUBLANES, LANES), F32) for n in _scan_sizes(s)[1:] for _ in range(2)]


def _linear_scan(a_ref, b_ref, out_ref, a_off, s, reverse, levels):
    sizes = _scan_sizes(s)
    lv = [(a_ref, b_ref, a_off, 0)] + [(levels[2 * i], levels[2 * i + 1], 0, SUBLANES) for i in range(len(sizes) - 1)]
    zero8 = jnp.zeros((SUBLANES, LANES), F32)
    for (ar, br, _, _), n in zip(lv[1:], sizes[1:]):
        br[0:SUBLANES, :] = zero8
        br[pl.ds(n + SUBLANES, SUBLANES), :] = zero8
    order = list(range(SUBLANES - 1, -1, -1)) if reverse else list(range(SUBLANES))

    for lvl in range(len(sizes) - 1):
        ar, br, aoff, off = lv[lvl]
        m = sizes[lvl + 1]
        g = min(m, SCAN_GROUP)
        for t0 in range(0, m, g):
            acc_a = acc_b = None
            for r in order:
                sa = pl.ds(off + SUBLANES * t0 + r + aoff, g, stride=SUBLANES)
                sb = pl.ds(off + SUBLANES * t0 + r, g, stride=SUBLANES)
                a, b = ar[sa, :], br[sb, :]
                if acc_a is None:
                    acc_a, acc_b = a, b
                else:
                    acc_b = a * acc_b + b
                    acc_a = a * acc_a
            lv[lvl + 1][0][pl.ds(SUBLANES + t0, g), :] = acc_a
            lv[lvl + 1][1][pl.ds(SUBLANES + t0, g), :] = acc_b

    ar, br, _, off = lv[-1]
    n = sizes[-1]
    a, b = ar[pl.ds(off, n), :], br[pl.ds(off, n), :]
    h, rows = jnp.zeros((1, LANES), F32), [None] * n
    for j in (range(n - 1, -1, -1) if reverse else range(n)):
        h = a[j:j + 1, :] * h + b[j:j + 1, :]
        rows[j] = h
    br[pl.ds(off, n), :] = jnp.concatenate(rows, axis=0)

    for lvl in range(len(sizes) - 2, -1, -1):
        ar, br, aoff, off = lv[lvl]
        m = sizes[lvl + 1]
        up = lv[lvl + 1][1]
        dst = out_ref if lvl == 0 else br
        g = min(m, SCAN_GROUP)
        for t0 in range(0, m, g):
            h = _shift_up(up, t0, g, 1, SUBLANES) if reverse else _shift_down(up, t0, g, 1)
            for r in order:
                sa = pl.ds(off + SUBLANES * t0 + r + aoff, g, stride=SUBLANES)
                sb = pl.ds(off + SUBLANES * t0 + r, g, stride=SUBLANES)
                h = ar[sa, :] * h + br[sb, :]
                dst[sb, :] = h


def _one_minus_exp(z):
    series = -z * (1.0 + z * (0.5 + z * (1.0 / 6.0 + z * (1.0 / 24.0 + z * (1.0 / 120.0 + z * (1.0 / 720.0))))))
    return jnp.where(z > -0.1, series, 1.0 - jnp.exp(z))


def _softplus(v):
    return jnp.maximum(v, 0.0) + jnp.log(1.0 + jnp.exp(-jnp.abs(v)))


def _rnn_gates(xc, w, wa, wx, sp):
    xb = xc.astype(BF16)
    ra = _sigmoid(jnp.dot(xb, wa, preferred_element_type=F32) + w[5:6])
    ix = _sigmoid(jnp.dot(xb, wx, preferred_element_type=F32) + w[6:7])
    la = (-LRU_C) * ra * sp
    a = jnp.exp(la)
    mult = jnp.sqrt(_one_minus_exp(2.0 * la))
    return ra, ix, a, mult


def _rnn_fwd(x_rnn, keep, rp, wa_bd, wx_bd):
    s, r = x_rnn.shape
    ts = min(RNN_CHUNK, s)

    def body(x_ref, keep_ref, rp_ref, wa_ref, wx_ref, xc_ref, ra_ref, ix_ref, hs_ref, xpad, a_s, b_s, *levels):
        xpad[0:SUBLANES, :] = jnp.zeros((SUBLANES, LANES), F32)
        xpad[SUBLANES:, :] = x_ref[...]
        w = rp_ref[...]
        sp = _softplus(-w[7:8])
        wa, wx = wa_ref[0], wx_ref[0]

        def chunk(c, carry):
            base = pl.multiple_of(c * ts, ts)
            xc = w[4:5] + w[3:4] * _shift_down(xpad, base, ts, 0)
            for j in range(1, 4):
                xc = xc + w[3 - j:4 - j] * _shift_down(xpad, base, ts, j)
            ra, ix, a, mult = _rnn_gates(xc, w, wa, wx, sp)
            kp = keep_ref[pl.ds(base, ts), :]
            xc_ref[pl.ds(base, ts), :] = xc
            ra_ref[pl.ds(base, ts), :] = ra
            ix_ref[pl.ds(base, ts), :] = ix
            a_s[pl.ds(base, ts), :] = a * kp
            b_s[pl.ds(base, ts), :] = jnp.where(kp > 0.0, mult, 1.0) * (ix * xc)
            return carry

        lax.fori_loop(0, s // ts, chunk, 0)

        _linear_scan(a_s, b_s, hs_ref, 0, s, False, levels)

    col = pl.BlockSpec((s, LANES), lambda g: (0, g))
    return pl.pallas_call(
        body, name="rnn_fwd", grid=(r // LANES,),
        in_specs=[col, pl.BlockSpec((s, 1), lambda g: (0, 0)), pl.BlockSpec((SUBLANES, LANES), lambda g: (0, g)),
                  pl.BlockSpec((1, LANES, LANES), lambda g: (g, 0, 0)), pl.BlockSpec((1, LANES, LANES), lambda g: (g, 0, 0))],
        out_specs=[col] * 4,
        out_shape=[jax.ShapeDtypeStruct((s, r), F32)] * 4,
        scratch_shapes=[pltpu.VMEM((s + SUBLANES, LANES), F32), pltpu.VMEM((s, LANES), F32), pltpu.VMEM((s, LANES), F32),
                        *_scan_scratch(s)],
        compiler_params=_cparams(("arbitrary",)),
    )(x_rnn, keep, rp, wa_bd, wx_bd)


def _rnn_bwd(x_rnn, xc, ra, ix, hs, dy, keep, rp, wa_bd, wx_bd):
    s, r = x_rnn.shape
    ts = min(RNN_CHUNK, s)

    def body(x_ref, xc_ref, ra_ref, ix_ref, hs_ref, dy_ref, keep_ref, rp_ref, wa_ref, wx_ref,
             dx_ref, dwa_ref, dwx_ref, red_ref, xpad, hpad, a_s, dh_s, dxc_s, *levels):
        zero8 = jnp.zeros((SUBLANES, LANES), F32)
        xpad[0:SUBLANES, :] = zero8
        xpad[SUBLANES:, :] = x_ref[...]
        hpad[0:SUBLANES, :] = zero8
        hpad[SUBLANES:, :] = hs_ref[...]
        a_s[s:, :] = zero8
        dxc_s[s:, :] = zero8
        w = rp_ref[...]
        sp = _softplus(-w[7:8])
        wa, wx = wa_ref[0], wx_ref[0]

        def decay(c, carry):
            base = pl.multiple_of(c * ts, ts)
            a = jnp.exp((-LRU_C) * ra_ref[pl.ds(base, ts), :] * sp)
            a_s[pl.ds(base, ts), :] = a * keep_ref[pl.ds(base, ts), :]
            return carry

        lax.fori_loop(0, s // ts, decay, 0)

        _linear_scan(a_s, dy_ref, dh_s, 1, s, True, levels)

        def gates(c, carry):
            dwa, dwx, d_ba, d_bx, d_sp, d_cb = carry
            base = pl.multiple_of(c * ts, ts)
            xcv = xc_ref[pl.ds(base, ts), :]
            rav = ra_ref[pl.ds(base, ts), :]
            ixv = ix_ref[pl.ds(base, ts), :]
            kp = keep_ref[pl.ds(base, ts), :]
            dh = dh_s[pl.ds(base, ts), :]
            h_prev = _shift_down(hpad, base, ts, 1)
            la = (-LRU_C) * rav * sp
            a = jnp.exp(la)
            mult = jnp.sqrt(_one_minus_exp(2.0 * la))
            mult_eff = jnp.where(kp > 0.0, mult, 1.0)
            d_a = dh * h_prev * kp
            d_mult = dh * (ixv * xcv) * kp
            d_ix = dh * mult_eff * xcv
            d_xc = dh * mult_eff * ixv
            d_la = d_a * a - d_mult * (a * a) / mult
            d_pa = d_la * ((-LRU_C) * sp) * rav * (1.0 - rav)
            d_px = d_ix * ixv * (1.0 - ixv)
            xb = xcv.astype(BF16)
            pab, pxb = d_pa.astype(BF16), d_px.astype(BF16)
            tn = (((0,), (0,)), ((), ()))
            nt_ = (((1,), (1,)), ((), ()))
            dwa = dwa + lax.dot_general(xb, pab, tn, preferred_element_type=F32)
            dwx = dwx + lax.dot_general(xb, pxb, tn, preferred_element_type=F32)
            d_xc = d_xc + lax.dot_general(pab, wa, nt_, preferred_element_type=F32)
            d_xc = d_xc + lax.dot_general(pxb, wx, nt_, preferred_element_type=F32)
            dxc_s[pl.ds(base, ts), :] = d_xc
            return (dwa, dwx, d_ba + _colsum(d_pa), d_bx + _colsum(d_px),
                    d_sp + _colsum(d_la * ((-LRU_C) * rav)), d_cb + _colsum(d_xc))

        z1 = jnp.zeros((1, LANES), F32)
        zw = jnp.zeros((LANES, LANES), F32)
        dwa, dwx, d_ba, d_bx, d_sp, d_cb = lax.fori_loop(0, s // ts, gates, (zw, zw, z1, z1, z1, z1))
        dwa_ref[0] = dwa
        dwx_ref[0] = dwx

        def conv(c, carry):
            base = pl.multiple_of(c * ts, ts)
            d_here = dxc_s[pl.ds(base, ts), :]
            dx = w[3:4] * d_here
            for j in range(1, 4):
                dx = dx + w[3 - j:4 - j] * _shift_up(dxc_s, base, ts, j, 0)
            dx_ref[pl.ds(base, ts), :] = dx.astype(BF16)
            return tuple(carry[k] + _colsum(d_here * _shift_down(xpad, base, ts, 3 - k)) for k in range(4))

        d_w = lax.fori_loop(0, s // ts, conv, (z1, z1, z1, z1))
        d_lru = d_sp * (-_sigmoid(-w[7:8]))
        red_ref[...] = jnp.concatenate(list(d_w) + [d_cb, d_ba, d_bx, d_lru], axis=0)

    col = pl.BlockSpec((s, LANES), lambda g: (0, g))
    par = pl.BlockSpec((SUBLANES, LANES), lambda g: (0, g))
    wsp = pl.BlockSpec((1, LANES, LANES), lambda g: (g, 0, 0))
    return pl.pallas_call(
        body, name="rnn_bwd", grid=(r // LANES,),
        in_specs=[col] * 6 + [pl.BlockSpec((s, 1), lambda g: (0, 0)), par, wsp, wsp],
        out_specs=[col, wsp, wsp, par],
        out_shape=[jax.ShapeDtypeStruct((s, r), BF16), jax.ShapeDtypeStruct((r // LANES, LANES, LANES), F32),
                   jax.ShapeDtypeStruct((r // LANES, LANES, LANES), F32), jax.ShapeDtypeStruct((SUBLANES, r), F32)],
        scratch_shapes=[pltpu.VMEM((s + SUBLANES, LANES), F32), pltpu.VMEM((s + SUBLANES, LANES), F32),
                        pltpu.VMEM((s + SUBLANES, LANES), F32), pltpu.VMEM((s, LANES), F32),
                        pltpu.VMEM((s + SUBLANES, LANES), F32), *_scan_scratch(s)],
        compiler_params=_cparams(("arbitrary",)),
    )(x_rnn, xc, ra, ix, hs, dy, keep, rp, wa_bd, wx_bd)


ATT_BLOCK = 512


LOG2E = 1.4426950408889634
LN2 = 0.6931471805599453
Q_PRESCALE = ATT_SCALE * LOG2E


def _att_scores(q, kvt, krt, diagonal):
    kt_eff = jnp.where(_row(kvt.shape) < QK_NOPE, kvt, krt)
    sc = jnp.dot(q, kt_eff, preferred_element_type=F32)
    if diagonal:
        sc = jnp.where(lax.broadcasted_iota(jnp.int32, sc.shape, 1) <= lax.broadcasted_iota(jnp.int32, sc.shape, 0), sc, -jnp.inf)
    return sc


def _att_fwd(q, kv, kvt, krt):
    s = q.shape[0]
    t = min(ATT_BLOCK, s)
    nb = s // t

    pairs = [(i, j) for i in range(nb) for j in range(i + 1)]
    i_tab = jnp.array([p[0] for p in pairs], jnp.int32)
    j_tab = jnp.array([p[1] for p in pairs], jnp.int32)

    def body(i_ref, j_ref, q_ref, kv_ref, kvt_ref, krt_ref, y_ref, lse_ref, m_s, acc_s):
        i, j = i_ref[pl.program_id(1)], j_ref[pl.program_id(1)]

        @pl.when(j == 0)
        def _():
            m_s[...] = jnp.full(m_s.shape, -jnp.inf, F32)
            acc_s[...] = jnp.zeros(acc_s.shape, F32)

        def step(diagonal):
            krt_b = krt_ref[...]
            lane = _lane((t, LANES))
            groups = [slice(c * LANES, (c + 1) * LANES) for c in range(t // LANES)]
            heads = [slice(hh * LANES, (hh + 1) * LANES) for hh in range(2)]
            scs = [_att_scores(q_ref[:, sl], kvt_ref[sl, :], krt_b, diagonal) for sl in heads]
            stats = []
            for hh in range(2):
                m_prev = m_s[hh]
                m_blk = scs[hh][:, groups[0]]
                for g in groups[1:]:
                    m_blk = jnp.maximum(m_blk, scs[hh][:, g])
                stats.append((m_prev, jnp.maximum(m_prev, jnp.max(m_blk, axis=-1, keepdims=True))))
            for hh in range(2):
                m_prev, m_new = stats[hh]
                kvb = kv_ref[:, heads[hh]]
                ones_v = jnp.where(lane < QK_NOPE, jnp.ones_like(kvb), kvb)
                p = jnp.concatenate([jnp.exp2(scs[hh][:, g] - m_new).astype(BF16) for g in groups], axis=1)
                acc_s[hh] = jnp.exp2(m_prev - m_new) * acc_s[hh] + jnp.dot(p, ones_v, preferred_element_type=F32)
                m_s[hh] = m_new

        @pl.when(j < i)
        def _():
            step(False)

        @pl.when(j == i)
        def _():
            step(True)
            lane = _lane((t, LANES))
            a0, a1 = acc_s[0], acc_s[1]
            l0, l1 = a0[:, 0:1], a1[:, 0:1]
            y_ref[...] = jnp.where(lane < V_HEAD, pltpu.roll(a0 / l0, V_HEAD, 1), a1 / l1).astype(BF16)
            lse_ref[...] = jnp.where(lane < V_HEAD, m_s[0] + jnp.log(l0) * LOG2E, m_s[1] + jnp.log(l1) * LOG2E)

    return pl.pallas_call(
        body, name="att_fwd",
        grid_spec=pltpu.PrefetchScalarGridSpec(
            num_scalar_prefetch=2, grid=(N_HEADS // 2, len(pairs)),
            in_specs=[pl.BlockSpec((t, 2 * LANES), lambda p, n, it, jt: (it[n], p)),
                      pl.BlockSpec((t, 2 * LANES), lambda p, n, it, jt: (jt[n], p)),
                      pl.BlockSpec((2 * LANES, t), lambda p, n, it, jt: (p, jt[n])),
                      pl.BlockSpec((LANES, t), lambda p, n, it, jt: (0, jt[n]))],
            out_specs=[pl.BlockSpec((t, LANES), lambda p, n, it, jt: (it[n], p))] * 2,
            scratch_shapes=[pltpu.VMEM((2, t, LANES), F32)] * 2),
        out_shape=[jax.ShapeDtypeStruct((s, N_HEADS * V_HEAD), BF16), jax.ShapeDtypeStruct((s, N_HEADS * V_HEAD), F32)],
        compiler_params=_cparams(("arbitrary", "arbitrary")),
    )(i_tab, j_tab, q, kv, kvt, krt)


def _att_bwd(q, qt, kv, kvt, kr, krt, y, lse, dy, dyt):
    s = q.shape[0]
    t = min(ATT_BLOCK, s)
    nb = s // t

    pairs = [(i, j) for j in range(nb) for i in range(j, nb)]
    i_tab = jnp.array([p[0] for p in pairs], jnp.int32)
    j_tab = jnp.array([p[1] for p in pairs], jnp.int32)

    def body(i_ref, j_ref, q_ref, qt_ref, kv_ref, kvt_ref, kr_ref, krt_ref, y_ref, lse_ref, dy_ref, dyt_ref,
             dq_ref, dkvt_ref, dkrt_ref, dkv_s):
        p_, n = pl.program_id(0), pl.program_id(1)
        i, j = i_ref[n], j_ref[n]

        @pl.when((p_ == 0) & (n == 0))
        def _():
            dkrt_ref[...] = jnp.zeros(dkrt_ref.shape, F32)

        @pl.when(n == 0)
        def _():
            dq_ref[...] = jnp.zeros(dq_ref.shape, F32)

        @pl.when(i == j)
        def _():
            dkv_s[...] = jnp.zeros(dkv_s.shape, F32)

        def step(diagonal):
            lane = _lane((t, LANES))
            row = _row((LANES, t))
            krb, krt_b = kr_ref[...], krt_ref[...]
            dyv = dy_ref[...]
            yv = y_ref[...].astype(F32)
            lsev = lse_ref[...]
            dyt_b = dyt_ref[...]
            rows = pl.ds(pl.multiple_of(i * t, t), t)
            cols = pl.ds(pl.multiple_of(j * t, t), t)
            zeros_t = jnp.zeros((V_HEAD, t), BF16)
            ones_w = jnp.ones((LANES, LANES), BF16)
            groups = [slice(c * LANES, (c + 1) * LANES) for c in range(t // LANES)]
            heads = [slice(hh * LANES, (hh + 1) * LANES) for hh in range(2)]
            scs, dps, stats = [], [], []
            for hh, sl in enumerate(heads):
                kvt_b = kvt_ref[sl, :]
                scs.append(_att_scores(q_ref[:, sl], kvt_b, krt_b, diagonal))
                mine = (lane < V_HEAD) if hh == 0 else (lane >= V_HEAD)
                lse_rep = jnp.where(mine, lsev, pltpu.roll(lsev, V_HEAD, 1))
                do_pad = jnp.where(lane >= V_HEAD, pltpu.roll(dyv, V_HEAD, 1) if hh == 0 else dyv, 0.0)
                o_pad = jnp.where(lane >= V_HEAD, pltpu.roll(yv, V_HEAD, 1) if hh == 0 else yv, 0.0)
                do_ln2 = do_pad * LN2
                prod = do_ln2 * o_pad
                head_part = prod.astype(BF16)
                rest_part = (prod - head_part.astype(F32)).astype(BF16)
                delta_rep = (jnp.dot(head_part, ones_w, preferred_element_type=F32)
                             + jnp.dot(rest_part, ones_w, preferred_element_type=F32))
                dps.append(jnp.dot(do_ln2.astype(BF16), kvt_b, preferred_element_type=F32))
                stats.append((lse_rep, delta_rep))
            dkr_acc = jnp.zeros((LANES, t), F32)
            for hh, sl in enumerate(heads):
                lse_rep, delta_rep = stats[hh]
                probs, dss = [], []
                for g in groups:
                    pg = jnp.exp2(scs[hh][:, g] - lse_rep)
                    probs.append(pg.astype(BF16))
                    dss.append((pg * (dps[hh][:, g] - delta_rep)).astype(BF16))
                prob, ds = jnp.concatenate(probs, axis=1), jnp.concatenate(dss, axis=1)
                dot_pad = jnp.concatenate([zeros_t, dyt_b[hh * V_HEAD:(hh + 1) * V_HEAD, :]], axis=0)
                k_eff = jnp.where(lane < QK_NOPE, kv_ref[:, sl], krb)
                dvt = jnp.dot(dot_pad, prob, preferred_element_type=F32)
                dq_ref[rows, sl] += jnp.dot(ds, k_eff, preferred_element_type=F32)
                dkt = jnp.dot(qt_ref[sl, :], ds, preferred_element_type=F32)
                dkv_s[hh] += dvt + jnp.where(row < QK_NOPE, dkt, 0.0)
                dkr_acc = dkr_acc + jnp.where(row >= QK_NOPE, dkt, 0.0)
            dkrt_ref[:, cols] += dkr_acc

        @pl.when(i > j)
        def _():
            step(False)

        @pl.when(i == j)
        def _():
            step(True)

        @pl.when(i == nb - 1)
        def _():
            dkvt_ref[0:LANES, :] = dkv_s[0].astype(BF16)
            dkvt_ref[LANES:, :] = dkv_s[1].astype(BF16)

    qi = lambda p, n, it, jt: (it[n], p)
    qti = lambda p, n, it, jt: (p, it[n])
    return pl.pallas_call(
        body, name="att_bwd",
        grid_spec=pltpu.PrefetchScalarGridSpec(
            num_scalar_prefetch=2, grid=(N_HEADS // 2, len(pairs)),
            in_specs=[pl.BlockSpec((t, 2 * LANES), qi), pl.BlockSpec((2 * LANES, t), qti),
                      pl.BlockSpec((t, 2 * LANES), lambda p, n, it, jt: (jt[n], p)),
                      pl.BlockSpec((2 * LANES, t), lambda p, n, it, jt: (p, jt[n])),
                      pl.BlockSpec((t, LANES), lambda p, n, it, jt: (jt[n], 0)),
                      pl.BlockSpec((LANES, t), lambda p, n, it, jt: (0, jt[n])),
                      pl.BlockSpec((t, LANES), qi), pl.BlockSpec((t, LANES), qi), pl.BlockSpec((t, LANES), qi),
                      pl.BlockSpec((LANES, t), qti)],
            out_specs=[pl.BlockSpec((s, 2 * LANES), lambda p, n, it, jt: (0, p)),
                       pl.BlockSpec((2 * LANES, t), lambda p, n, it, jt: (p, jt[n])),
                       pl.BlockSpec((LANES, s), lambda p, n, it, jt: (0, 0))],
            scratch_shapes=[pltpu.VMEM((2, LANES, t), F32)]),
        out_shape=[jax.ShapeDtypeStruct((s, N_HEADS * LANES), F32), jax.ShapeDtypeStruct((N_HEADS * LANES, s), BF16),
                   jax.ShapeDtypeStruct((LANES, s), F32)],
        compiler_params=_cparams(("arbitrary", "arbitrary")),
    )(i_tab, j_tab, q, qt, kv, kvt, kr, krt, y, lse, dy, dyt)


FFN_COLS = 256


def _ffn_conv(pad_ref, w, base, n):
    u = w[3:4] + w[2:3] * _shift_down(pad_ref, base, n, 0)
    for j in range(1, 3):
        u = u + w[2 - j:3 - j] * _shift_down(pad_ref, base, n, j)
    return u


def _ffn_act_fwd(up, fp):
    s, f2 = up.shape
    f = f2 // 2
    tc = FFN_COLS
    ts = min(RNN_CHUNK, s)
    nfb = f // tc

    def body(ug_ref, uv_ref, wg_ref, wv_ref, act_ref, gpad, vpad):
        zero8 = jnp.zeros((SUBLANES, tc), F32)
        gpad[0:SUBLANES, :] = zero8
        vpad[0:SUBLANES, :] = zero8
        gpad[SUBLANES:, :] = ug_ref[...]
        vpad[SUBLANES:, :] = uv_ref[...]
        wg, wv = wg_ref[...], wv_ref[...]

        def chunk(c, carry):
            base = pl.multiple_of(c * ts, ts)
            g = _ffn_conv(gpad, wg, base, ts)
            v = _ffn_conv(vpad, wv, base, ts)
            act_ref[pl.ds(base, ts), :] = (g * _sigmoid(g) * v).astype(BF16)
            return carry

        lax.fori_loop(0, s // ts, chunk, 0)

    return pl.pallas_call(
        body, name="ffn_act_fwd", grid=(nfb,),
        in_specs=[pl.BlockSpec((s, tc), lambda b: (0, b)), pl.BlockSpec((s, tc), lambda b: (0, b + nfb)),
                  pl.BlockSpec((SUBLANES, tc), lambda b: (0, b)), pl.BlockSpec((SUBLANES, tc), lambda b: (0, b + nfb))],
        out_specs=pl.BlockSpec((s, tc), lambda b: (0, b)),
        out_shape=jax.ShapeDtypeStruct((s, f), BF16),
        scratch_shapes=[pltpu.VMEM((s + SUBLANES, tc), F32)] * 2,
        compiler_params=_cparams(("arbitrary",)),
    )(up, up, fp, fp)


def _ffn_act_bwd(up, dact, fp):
    s, f2 = up.shape
    f = f2 // 2
    tc = FFN_COLS
    ts = min(RNN_CHUNK, s)
    nfb = f // tc

    def body(ug_ref, uv_ref, da_ref, wg_ref, wv_ref, dup_ref, red_ref, gpad, vpad, dgs, dvs):
        half = pl.program_id(1)
        wg, wv = wg_ref[...], wv_ref[...]

        @pl.when(half == 0)
        def _():
            zero8 = jnp.zeros((SUBLANES, tc), F32)
            gpad[0:SUBLANES, :] = zero8
            vpad[0:SUBLANES, :] = zero8
            gpad[SUBLANES:, :] = ug_ref[...]
            vpad[SUBLANES:, :] = uv_ref[...]
            dgs[s:, :] = zero8
            dvs[s:, :] = zero8

            def act(c, carry):
                base = pl.multiple_of(c * ts, ts)
                g = _ffn_conv(gpad, wg, base, ts)
                v = _ffn_conv(vpad, wv, base, ts)
                da = da_ref[pl.ds(base, ts), :]
                sg = _sigmoid(g)
                dgs[pl.ds(base, ts), :] = da * v * (sg * (1.0 + g * (1.0 - sg)))
                dvs[pl.ds(base, ts), :] = da * (g * sg)
                return carry

            lax.fori_loop(0, s // ts, act, 0)

        def conv_t(src, pad, w, out_ref, red_ref):
            def chunk(c, carry):
                base = pl.multiple_of(c * ts, ts)
                d_here = src[pl.ds(base, ts), :]
                dx = w[2:3] * d_here
                for j in range(1, 3):
                    dx = dx + w[2 - j:3 - j] * _shift_up(src, base, ts, j, 0)
                out_ref[pl.ds(base, ts), :] = dx.astype(BF16)
                taps = tuple(carry[k] + _colsum(d_here * _shift_down(pad, base, ts, 2 - k)) for k in range(3))
                return taps + (carry[3] + _colsum(d_here),)

            z1 = jnp.zeros((1, tc), F32)
            red = lax.fori_loop(0, s // ts, chunk, (z1, z1, z1, z1))
            red_ref[...] = jnp.concatenate(list(red) + [jnp.zeros((4, tc), F32)], axis=0)

        @pl.when(half == 0)
        def _():
            conv_t(dgs, gpad, wg, dup_ref, red_ref)

        @pl.when(half == 1)
        def _():
            conv_t(dvs, vpad, wv, dup_ref, red_ref)

    gcol = pl.BlockSpec((s, tc), lambda b, h: (0, b))
    vcol = pl.BlockSpec((s, tc), lambda b, h: (0, b + nfb))
    gpar = pl.BlockSpec((SUBLANES, tc), lambda b, h: (0, b))
    vpar = pl.BlockSpec((SUBLANES, tc), lambda b, h: (0, b + nfb))
    return pl.pallas_call(
        body, name="ffn_act_bwd", grid=(nfb, 2),
        in_specs=[gcol, vcol, gcol, gpar, vpar],
        out_specs=[pl.BlockSpec((s, tc), lambda b, h: (0, b + h * nfb)),
                   pl.BlockSpec((SUBLANES, tc), lambda b, h: (0, b + h * nfb))],
        out_shape=[jax.ShapeDtypeStruct((s, f2), BF16), jax.ShapeDtypeStruct((SUBLANES, f2), F32)],
        scratch_shapes=[pltpu.VMEM((s + SUBLANES, tc), F32)] * 4,
        compiler_params=_cparams(("arbitrary", "arbitrary")),
    )(up, up, dact, fp, fp)


def _rows8(rows, width):
    rows = [r.reshape(1, width).astype(F32) for r in rows]
    return jnp.concatenate(rows + [jnp.zeros((SUBLANES - len(rows), width), F32)], axis=0)


def _block_diag(w):
    n, b, _ = w.shape
    w = w.reshape(n // 2, 2, b, b)
    z = jnp.zeros((n // 2, b, b), w.dtype)
    top = jnp.concatenate([w[:, 0], z], axis=2)
    bot = jnp.concatenate([z, w[:, 1]], axis=2)
    return jnp.concatenate([top, bot], axis=1)


def _block_diag_t(bd):
    n, b2, _ = bd.shape
    b = b2 // 2
    return jnp.stack([bd[:, :b, :b], bd[:, b:, b:]], axis=1).reshape(2 * n, b, b)


def _local_step(x, mod, positions, target, w_in, fetch, sm, emit):
    s, d = x.shape
    o_rnn, o_mla = D_RNN, D_RNN + Q_LORA + KV_LORA + QK_ROPE
    wts = {}
    w_in_rnn = w_in[:, :o_rnn]
    w_in_mla = jnp.concatenate([w_in[:, o_rnn:o_mla], jnp.zeros((d, MLA_W - (o_mla - o_rnn)), w_in.dtype)], axis=1)
    w_in_g = w_in[:, o_mla:]
    hd = QK_NOPE + QK_ROPE
    wa_bd = _block_diag(sm["w_gate_a"]).astype(BF16)
    wx_bd = _block_diag(sm["w_gate_x"]).astype(BF16)

    pos = positions.reshape(s)
    half = QK_ROPE // 2
    inv_freq = ROPE_THETA ** (-jnp.arange(half, dtype=F32) / half)
    ang = pos.astype(F32)[:, None] * inv_freq
    cos, sin = jnp.cos(ang), jnp.sin(ang)
    rot_c = jnp.concatenate([jnp.ones((s, QK_NOPE), F32), cos, cos, jnp.ones((s, LANES - hd), F32)], axis=1)
    rot_s = jnp.concatenate([jnp.zeros((s, QK_NOPE), F32), -sin, sin, jnp.zeros((s, LANES - hd), F32)], axis=1)
    keep = (pos != 0).astype(F32).reshape(s, 1)

    gmod1 = _rows8([sm["norm1_g"], mod[1], mod[0]], d)
    gmod2 = _rows8([sm["norm2_g"], mod[4], mod[3], mod[2]], d)
    rp = jnp.concatenate([sm["conv_w"].reshape(4, D_RNN), _rows8([sm["conv_b"], sm["b_gate_a"], sm["b_gate_x"], sm["lru_param"]], D_RNN)[:4]], axis=0)
    fp = _rows8([sm["ffn_conv_w"][0], sm["ffn_conv_w"][1], sm["ffn_conv_w"][2], sm["ffn_conv_b"]], 2 * D_FF)
    ng = _rows8([jnp.concatenate([sm["q_norm_g"].reshape(-1), sm["kv_norm_g"].reshape(-1), jnp.zeros((MLA_W - Q_LORA - KV_LORA,), F32)])], MLA_W)
    fpar = _rows8([mod[5], sm["final_g"]], d)

    h = _norm_mod_fwd(x, gmod1, "norm1_fwd")
    proj_rnn = _mm(h, w_in_rnn, name="mm_in_rnn")
    proj_mla = _mm(h, w_in_mla, name="mm_in_mla")
    proj_g = _mm(h, w_in_g, out_dtype=BF16, name="mm_in_g")
    xc, ra, ix, hs = _rnn_fwd(proj_rnn, keep, rp, wa_bd, wx_bd)
    qn, kvn, kr = _mla_prep_fwd(proj_mla, rot_c, rot_s, ng)
    wts.update(fetch(("w_uq", "w_ukv"), kr))
    w_uq_p = jnp.pad(wts["w_uq"].reshape(Q_LORA, N_HEADS, hd), ((0, 0), (0, 0), (0, LANES - hd))).reshape(Q_LORA, N_HEADS * LANES)
    w_ukv = wts["w_ukv"]
    q_rot, q_rot_t = _rope_fwd_t(_mm(qn, w_uq_p, name="mm_uq"), rot_c, rot_s)
    kv, kvt = _mm(kvn, w_ukv, out_dtype=BF16, also_t=BF16, name="mm_ukv")
    krt = jnp.transpose(kr)
    y_mla, lse = _att_fwd(q_rot, kv, kvt, krt)
    wts.update(fetch(("w_proj_rnn", "w_proj_mla", "w_out", "w_up", "w_down"), lse))
    pr = _mm(hs, wts["w_proj_rnn"], out_dtype=BF16, name="mm_proj_rnn")
    pm = _mm(y_mla, wts["w_proj_mla"], out_dtype=BF16, name="mm_proj_mla")
    merged = _merge_fwd(pr, pm, proj_g)
    o = _mm(merged, wts["w_out"], name="mm_out")
    x1, h2 = _resid_norm_fwd(x, o, gmod2)
    up = _mm(h2, wts["w_up"], name="mm_up")
    act = _ffn_act_fwd(up, fp)
    dn = _mm(act, wts["w_down"], name="mm_down")

    dx2, ddn, red_f = _final_fwd_bwd(x1, dn, target, fpar)
    dact = _mm(ddn, wts["w_down"], tb=True, name="mm_d_act")
    tok = emit("w_down", _mm(act, ddn, ta=True, out_dtype=BF16, name="mm_dw_down"))
    dup, red_ffn = _ffn_act_bwd(up, dact, fp + tok)
    dh2 = _mm(dup, wts["w_up"], tb=True, name="mm_d_h2")
    tok = tok + emit("w_up", _mm(dup, h2, ta=True, out_dtype=BF16, name="mm_dw_up"))
    dx1, do, red_2 = _norm2_bwd(x1, dh2, dx2, o, gmod2 + tok)
    dmerged = _mm(do, wts["w_out"], tb=True, name="mm_d_merged")
    tok = tok + emit("w_out", _mm(merged, do, ta=True, out_dtype=BF16, name="mm_dw_out"))
    dpr, dpm, dg = _merge_bwd(dmerged, pr, pm, proj_g)
    dy_rnn = _mm(dpr, wts["w_proj_rnn"], tb=True, name="mm_d_yrnn")
    tok = tok + emit("w_proj_rnn", _mm(hs, dpr, ta=True, out_dtype=BF16, name="mm_dw_proj_rnn"))
    dy_mla, dy_mla_t = _mm(dpm, wts["w_proj_mla"], tb=True, also_t=BF16, name="mm_d_ymla")
    tok = tok + emit("w_proj_mla", _mm(y_mla, dpm, ta=True, out_dtype=BF16, name="mm_dw_proj_mla"))
    dq_rot, dkvt, dkrt = _att_bwd(q_rot, q_rot_t, kv, kvt, kr, krt, y_mla, lse, dy_mla, dy_mla_t)
    dq = _rope_bwd(dq_rot, rot_c, rot_s)
    dqn = _mm(dq, w_uq_p, tb=True, name="mm_d_qn")
    dw_uq_pt = _mm(dq, qn, ta=True, out_dtype=BF16, name="mm_dw_uq")
    tok = tok + emit("w_uq", dw_uq_pt.reshape(N_HEADS, LANES, Q_LORA)[:, :hd].reshape(N_HEADS * hd, Q_LORA))
    dkvn = jnp.transpose(_mm(w_ukv, dkvt, name="mm_d_kvn"))
    tok = tok + emit("w_ukv", _mm(dkvt, kvn, out_dtype=BF16, name="mm_dw_ukv"))
    dproj_mla, red_m = _mla_prep_bwd(proj_mla, dqn, dkvn, jnp.transpose(dkrt), rot_c, rot_s, ng + tok)
    dx_rnn, dwa_bd, dwx_bd, red_r = _rnn_bwd(proj_rnn, xc, ra, ix, hs, dy_rnn, keep, rp + tok, wa_bd, wx_bd)
    dw_in_t = jnp.concatenate([
        _mm(dx_rnn, h, ta=True, out_dtype=BF16, name="mm_dw_in_rnn"),
        _mm(dproj_mla, h, ta=True, out_dtype=BF16, name="mm_dw_in_mla")[:o_mla - o_rnn],
        _mm(dg, h, ta=True, out_dtype=BF16, name="mm_dw_in_g")], axis=0)
    tok = tok + emit("w_in", dw_in_t)
    dh_a = _mm(dx_rnn, w_in_rnn, tb=True, name="mm_d_h_rnn")
    dh_b = _mm(dproj_mla, w_in_mla, tb=True, name="mm_d_h_mla")
    dh_c = _mm(dg, w_in_g, tb=True, name="mm_d_h_g")
    grad_x, red_1 = _norm1_bwd(x, dh_a, dh_b, dh_c, dx1, gmod1 + tok)

    gs = {
        "norm1_g": red_1[0], "conv_w": red_r[0:4], "conv_b": red_r[4], "w_gate_a": _block_diag_t(dwa_bd),
        "b_gate_a": red_r[5], "w_gate_x": _block_diag_t(dwx_bd), "b_gate_x": red_r[6], "lru_param": red_r[7],
        "q_norm_g": red_m[0, :Q_LORA], "kv_norm_g": red_m[0, Q_LORA:Q_LORA + KV_LORA], "norm2_g": red_2[0],
        "ffn_conv_w": red_ffn[0:3], "ffn_conv_b": red_ffn[3], "final_g": red_f[0],
    }
    dmod = jnp.stack([red_1[2], red_1[1], red_2[3], red_2[2], red_2[1], red_f[1]], axis=0)
    return red_f[2, 0], grad_x, gs, dmod


MESH_IDS = pl.DeviceIdType.MESH
HBM_SPEC = pl.BlockSpec(memory_space=pltpu.HBM)


def _my_slot():
    return 4 * lax.axis_index("x") + 2 * lax.axis_index("y") + lax.axis_index("c")


def _all_gather(arrs, name):
    n = len(arrs)

    def body(*refs):
        ins, outs = refs[:n], refs[n:2 * n]
        send_sems, recv_sems, local_sems = refs[2 * n:]
        x, y, c = lax.axis_index("x"), lax.axis_index("y"), lax.axis_index("c")
        me, sibling = (x, y, c), (x, y, 1 - c)
        chips = [(1 - x, y), (x, 1 - y), (1 - x, 1 - y)]

        def slot(dev):
            return 4 * dev[0] + 2 * dev[1] + dev[2]

        def copy(a, k, block, to, src=None):
            dst = outs[a].at[slot(block)]
            return pltpu.make_async_remote_copy(
                src_ref=dst if src is None else src, dst_ref=dst, send_sem=send_sems.at[a, k], recv_sem=recv_sems.at[a, k],
                device_id=to, device_id_type=MESH_IDS)

        mine = [pltpu.make_async_copy(ins[a], outs[a].at[slot(me)], local_sems.at[a]) for a in range(n)]
        for cp in mine:
            cp.start()
        first = []
        for a in range(n):
            first.append(copy(a, 0, me, sibling, src=ins[a]))
            first += [copy(a, 1 + j, me, (*chip, c), src=ins[a]) for j, chip in enumerate(chips)]
        for cp in first:
            cp.start()
        passed = []
        for j, chip in enumerate(chips):
            for a in range(n):
                copy(a, 1 + j, (*chip, c), me).wait_recv()
                fwd = copy(a, 4 + j, (*chip, c), sibling)
                fwd.start()
                passed.append(fwd)
        for a in range(n):
            copy(a, 0, sibling, me).wait_recv()
            for j, chip in enumerate(chips):
                copy(a, 4 + j, (*chip, 1 - c), me).wait_recv()
        for cp in first + passed:
            cp.wait_send()
        for cp in mine:
            cp.wait()

    return pl.pallas_call(
        body, name=name,
        in_specs=[HBM_SPEC] * n, out_specs=[HBM_SPEC] * n,
        out_shape=[jax.ShapeDtypeStruct((N_DEV,) + a.shape, a.dtype) for a in arrs],
        scratch_shapes=[pltpu.SemaphoreType.DMA((n, 7)), pltpu.SemaphoreType.DMA((n, 7)), pltpu.SemaphoreType.DMA((n,))],
    )(*arrs)


SEM_SPEC =pl.BlockSpec(memory_space=pltpu.SEMAPHORE)
DATAFLOW = pltpu.SideEffectType.DATAFLOW_SIDE_EFFECTING
FLIPS = [(dx, dy, dc) for dx in (0, 1) for dy in (0, 1) for dc in (0, 1)][1:]


def _peer(k):
    dx, dy, dc = FLIPS[k]
    peer = (lax.axis_index("x") ^ dx, lax.axis_index("y") ^ dy, lax.axis_index("c") ^ dc)
    return peer, 4 * peer[0] + 2 * peer[1] + peer[2]


def _gather_start(shards, after, name):
    n, nf = len(shards), len(FLIPS)

    def body(*refs):
        srcs, lands = refs[:n], refs[n:2 * n]
        send_sems, recv_sems = refs[2 * n + 1:3 * n + 1], refs[3 * n + 1:4 * n + 1]
        token = refs[-1]
        me = _my_slot()
        for a in range(n):
            for k in range(nf):
                peer, _ = _peer(k)
                pltpu.make_async_remote_copy(
                    src_ref=srcs[a], dst_ref=lands[a].at[me], send_sem=send_sems[a].at[k], recv_sem=recv_sems[a].at[k],
                    device_id=peer, device_id_type=MESH_IDS).start()
        token[...] = jnp.zeros(token.shape, F32)

    land_shapes = [(N_DEV,) + a.shape for a in shards]
    sems = [pltpu.SemaphoreType.DMA((nf,))] * n
    out = pl.pallas_call(
        body, name=name,
        out_shape=(*sems, *sems, *[pltpu.HBM(a.shape, a.dtype) for a in shards],
                   *[pltpu.HBM(shp, a.dtype) for shp, a in zip(land_shapes, shards)],
                   jax.ShapeDtypeStruct((SUBLANES, LANES), F32)),
        in_specs=[HBM_SPEC] * (2 * n) + [pl.BlockSpec(memory_space=pl.ANY)],
        out_specs=(*[SEM_SPEC] * (2 * n), *[HBM_SPEC] * (2 * n), pl.BlockSpec(memory_space=pltpu.VMEM)),
        input_output_aliases={i: 2 * n + i for i in range(2 * n)},
        compiler_params=pltpu.CompilerParams(has_side_effects=DATAFLOW),
    )(*[pltpu.with_memory_space_constraint(a, pltpu.HBM) for a in shards],
      *[pltpu.with_memory_space_constraint(lax.empty(shp, a.dtype), pltpu.HBM) for shp, a in zip(land_shapes, shards)],
      after)
    return [(out[a], out[n + a], out[2 * n + a], out[3 * n + a]) for a in range(n)], out[-1]


def _gather_wait(flights, after, name):
    n, nf = len(flights), len(FLIPS)

    def body(*refs):
        send_sems, recv_sems = refs[:n], refs[n:2 * n]
        srcs, lands = refs[2 * n:3 * n], refs[3 * n:4 * n]
        for a in range(n):
            for k in range(nf):
                peer, peer_slot = _peer(k)
                cp = pltpu.make_async_remote_copy(
                    src_ref=srcs[a], dst_ref=lands[a].at[peer_slot], send_sem=send_sems[a].at[k],
                    recv_sem=recv_sems[a].at[k], device_id=peer, device_id_type=MESH_IDS)
                cp.wait_send()
                cp.wait_recv()

    srcs, lands = [f[2] for f in flights], [f[3] for f in flights]
    out = pl.pallas_call(
        body, name=name,
        out_shape=(*[pltpu.HBM(a.shape, a.dtype) for a in srcs], *[pltpu.HBM(a.shape, a.dtype) for a in lands]),
        in_specs=[SEM_SPEC] * (2 * n) + [HBM_SPEC] * (2 * n) + [pl.BlockSpec(memory_space=pl.ANY)],
        out_specs=tuple([HBM_SPEC] * (2 * n)),
        input_output_aliases={2 * n + i: i for i in range(2 * n)},
        compiler_params=pltpu.CompilerParams(has_side_effects=DATAFLOW),
    )(*[f[0] for f in flights], *[f[1] for f in flights], *srcs, *lands, after)
    return list(out[n:])


ROW_ALIGN = 16


def _span_start(slot, rows):
    return (rows * slot) // ROW_ALIGN * ROW_ALIGN


def _chunk_of(src_ref, slot, rows, span):
    if rows is None:
        return src_ref.at[slot]
    return src_ref.at[pl.ds(pl.multiple_of(_span_start(slot, rows), ROW_ALIGN), span)]


def _scatter_start(src, name, rows=None, span=None):
    def body(src_ref, land_ref, send_sems, recv_sems, src_thru, land_thru, token):
        me = _my_slot()
        for k in range(len(FLIPS)):
            peer, peer_slot = _peer(k)
            pltpu.make_async_remote_copy(
                src_ref=_chunk_of(src_ref, peer_slot, rows, span), dst_ref=land_ref.at[me], send_sem=send_sems.at[k],
                recv_sem=recv_sems.at[k], device_id=peer, device_id_type=MESH_IDS).start()
        token[...] = jnp.zeros(token.shape, F32)

    n = len(FLIPS)
    land_shape = src.shape if rows is None else (N_DEV, span, src.shape[1])
    return pl.pallas_call(
        body, name=name,
        out_shape=(pltpu.SemaphoreType.DMA((n,)), pltpu.SemaphoreType.DMA((n,)), pltpu.HBM(src.shape, src.dtype),
                   pltpu.HBM(land_shape, src.dtype), jax.ShapeDtypeStruct((SUBLANES, LANES), F32)),
        in_specs=(HBM_SPEC, HBM_SPEC),
        out_specs=(SEM_SPEC, SEM_SPEC, HBM_SPEC, HBM_SPEC, pl.BlockSpec(memory_space=pltpu.VMEM)),
        input_output_aliases={0: 2, 1: 3},
        compiler_params=pltpu.CompilerParams(has_side_effects=DATAFLOW),
    )(pltpu.with_memory_space_constraint(src, pltpu.HBM),
      pltpu.with_memory_space_constraint(lax.empty(land_shape, src.dtype), pltpu.HBM))


def _scatter_wait(send_sems, recv_sems, src_thru, land_thru, after, name, rows=None, span=None):
    def body(src_ref, land_ref, send_sems, recv_sems, after_ref, src_dead, got_ref):
        for k in range(len(FLIPS)):
            peer, peer_slot = _peer(k)
            cp = pltpu.make_async_remote_copy(
                src_ref=_chunk_of(src_ref, peer_slot, rows, span), dst_ref=land_ref.at[peer_slot], send_sem=send_sems.at[k],
                recv_sem=recv_sems.at[k], device_id=peer, device_id_type=MESH_IDS)
            cp.wait_send()
            cp.wait_recv()

    return pl.pallas_call(
        body, name=name,
        out_shape=(pltpu.HBM(src_thru.shape, src_thru.dtype), pltpu.HBM(land_thru.shape, land_thru.dtype)),
        in_specs=(HBM_SPEC, HBM_SPEC, SEM_SPEC, SEM_SPEC, pl.BlockSpec(memory_space=pl.ANY)),
        out_specs=(HBM_SPEC, HBM_SPEC), input_output_aliases={0: 0, 1: 1},
        compiler_params=pltpu.CompilerParams(has_side_effects=DATAFLOW),
    )(src_thru, land_thru, send_sems, recv_sems, after)


def _sum_sources(parts, own, name):
    k, r, c = parts.shape
    tr = r if k * r * c <= 2 * 1024 * 1024 else _pick(r, (512, 256, 128, 64, 32, 16, 8))

    def body(p_ref, own_ref, o_ref):
        me = _my_slot()
        acc = jnp.where(me == 0, own_ref[...], p_ref[0]).astype(F32)
        for s in range(1, k):
            acc = acc + jnp.where(me == s, own_ref[...], p_ref[s]).astype(F32)
        o_ref[...] = acc

    blk = pl.BlockSpec((tr, c), lambda i: (i, 0))
    return pl.pallas_call(
        body, name=name, grid=(r // tr,),
        in_specs=[pl.BlockSpec((k, tr, c), lambda i: (0, i, 0)), blk],
        out_specs=blk,
        out_shape=jax.ShapeDtypeStruct((r, c), F32),
        compiler_params=_cparams(("arbitrary",)),
    )(parts, own)


def _adamw_math(g, w, m, v):
    m_new = ADAM_B1 * m + (1.0 - ADAM_B1) * g
    v_new = ADAM_B2 * v + (1.0 - ADAM_B2) * jnp.square(g)
    m_hat = m_new / (1.0 - ADAM_B1 ** ADAM_STEP)
    v_hat = v_new / (1.0 - ADAM_B2 ** ADAM_STEP)
    return -ADAM_LR * (m_hat / (jnp.sqrt(v_hat) + ADAM_EPS) + ADAM_WD * w), m_new, v_new


def _adamw_many(gs, ws, ms, vs, name):
    n = len(gs)

    def body(*refs):
        ins, outs = refs[:4 * n], refs[4 * n:]
        for i in range(n):
            g = ins[i][...]
            outs[4 * i][...] = g
            outs[4 * i + 1][...], outs[4 * i + 2][...], outs[4 * i + 3][...] = _adamw_math(
                g, ins[n + i][...], ins[2 * n + i][...], ins[3 * n + i][...])

    return pl.pallas_call(
        body, name=name,
        out_shape=[jax.ShapeDtypeStruct(a.shape, F32) for a in ws for _ in range(4)],
        compiler_params=_cparams(),
    )(*gs, *ws, *ms, *vs)


def _adamw(parts, w, m, v, name, own=None):
    k, r, c = parts.shape
    tr = r if r * c <= 256 * 1024 else _pick(r, (256, 128, 64, 32, 16, 8))

    def body(*refs):
        p_ref, w_ref, m_ref, v_ref = refs[:4]
        g_ref, d_ref, nm_ref, nv_ref = refs[-4:]

        def part(s):
            if own is None:
                return p_ref[s].astype(F32)
            return jnp.where(_my_slot() == s, refs[4][...], p_ref[s]).astype(F32)

        g = part(0)
        for s in range(1, k):
            g = g + part(s)
        g_ref[...] = g
        d_ref[...], nm_ref[...], nv_ref[...] = _adamw_math(g, w_ref[...], m_ref[...], v_ref[...])

    blk = pl.BlockSpec((tr, c), lambda i: (i, 0))
    return pl.pallas_call(
        body, name=name, grid=(r // tr,),
        in_specs=[pl.BlockSpec((k, tr, c), lambda i: (0, i, 0)), blk, blk, blk] + ([] if own is None else [blk]),
        out_specs=[blk] * 4,
        out_shape=[jax.ShapeDtypeStruct((r, c), F32)] * 4,
        compiler_params=_cparams(("arbitrary",)),
    )(parts, w, m, v, *([] if own is None else [own]))


def _silu(v):
    return v * _sigmoid(v)


def _ada_fwd(c_all, w, b):
    def body(c_ref, w_ref, b_ref, o_ref):
        ca = _silu(c_ref[...]).astype(BF16)
        o_ref[...] = jnp.dot(ca, w_ref[...].astype(BF16), preferred_element_type=F32) + b_ref[...]

    return pl.pallas_call(
        body, name="ada_fwd", out_shape=jax.ShapeDtypeStruct((c_all.shape[0], w.shape[1]), F32),
        compiler_params=_cparams(),
    )(c_all, w, b)


def _ada_bwd(c_all, dmod):
    def body(c_ref, d_ref, o_ref):
        ca = _silu(c_ref[...]).astype(BF16).astype(F32)
        dm = d_ref[...].astype(BF16).astype(F32)
        acc = jnp.zeros(o_ref.shape, F32)
        for bi in range(c_all.shape[0]):
            acc = acc + jnp.transpose(ca[bi:bi + 1, :]) * dm[bi:bi + 1, :]
        o_ref[...] = acc

    return pl.pallas_call(
        body, name="ada_bwd", out_shape=jax.ShapeDtypeStruct((c_all.shape[1], dmod.shape[1]), F32),
        compiler_params=_cparams(),
    )(c_all, dmod)


COL_SHARDED = ("w_in", "w_uq", "w_ukv", "w_up")
ROW_SHARDED = ("w_proj_rnn", "w_proj_mla", "w_out", "w_down")
REPLICATED = ("b_ada", "norm1_g", "conv_b", "w_gate_a", "b_gate_a", "w_gate_x", "b_gate_x", "lru_param", "q_norm_g",
              "kv_norm_g", "norm2_g", "ffn_conv_b", "final_g")
WEIGHTS = ("w_ada", "b_ada", "norm1_g", "w_in", "conv_w", "conv_b", "w_gate_a", "b_gate_a", "w_gate_x", "b_gate_x",
           "lru_param", "q_norm_g", "w_uq", "kv_norm_g", "w_ukv", "w_proj_rnn", "w_proj_mla", "w_out", "norm2_g", "w_up",
           "ffn_conv_w", "ffn_conv_b", "w_down", "final_g")
TRANSPOSED_GRADS = COL_SHARDED
PACK_LANES = 128


def _pack(vecs, row_multiple=SUBLANES):
    flat = jnp.concatenate([v.reshape(-1).astype(F32) for v in vecs])
    pad = (-flat.shape[0]) % (PACK_LANES * row_multiple)
    return jnp.concatenate([flat, jnp.zeros((pad,), F32)]).reshape(-1, PACK_LANES)


def _unpack(packed, shapes):
    flat = packed.reshape(-1)
    out, off = [], 0
    for shp in shapes:
        size = math.prod(shp)
        out.append(flat[off:off + size].reshape(shp))
        off += size
    return out


def kernel(x, c, positions, w_ada, b_ada, norm1_g, w_in, conv_w, conv_b, w_gate_a, b_gate_a, w_gate_x, b_gate_x, lru_param, q_norm_g, w_uq, kv_norm_g, w_ukv, w_proj_rnn, w_proj_mla, w_out, norm2_g, w_up, ffn_conv_w, ffn_conv_b, w_down, final_g, loss_target, m_w_ada, m_b_ada, m_norm1_g, m_w_in, m_conv_w, m_conv_b, m_w_gate_a, m_b_gate_a, m_w_gate_x, m_b_gate_x, m_lru_param, m_q_norm_g, m_w_uq, m_kv_norm_g, m_w_ukv, m_w_proj_rnn, m_w_proj_mla, m_w_out, m_norm2_g, m_w_up, m_ffn_conv_w, m_ffn_conv_b, m_w_down, m_final_g, v_w_ada, v_b_ada, v_norm1_g, v_w_in, v_conv_w, v_conv_b, v_w_gate_a, v_b_gate_a, v_w_gate_x, v_b_gate_x, v_lru_param, v_q_norm_g, v_w_uq, v_kv_norm_g, v_w_ukv, v_w_proj_rnn, v_w_proj_mla, v_w_out, v_norm2_g, v_w_up, v_ffn_conv_w, v_ffn_conv_b, v_w_down, v_final_g):
    args = dict(locals())
    w = {n: args[n] for n in WEIGHTS}
    m = {n: args["m_" + n] for n in WEIGHTS}
    v = {n: args["v_" + n] for n in WEIGHTS}
    s, d = x.shape[1], x.shape[2]
    me = _my_slot()
    def two_d(a):
        assert a.ndim == 3 and a.shape[0] == 1, a.shape
        return a[0]

    big = COL_SHARDED + ROW_SHARDED
    shard = {n: two_d(w[n]).astype(BF16) for n in big}

    def whole(n, g):
        k, r, cc = g.shape
        return jnp.transpose(g, (1, 0, 2)).reshape(r, k * cc) if n in COL_SHARDED else g.reshape(k * r, cc)

    first = _all_gather([shard["w_in"], c, two_d(conv_w), two_d(ffn_conv_w)], "gather_first")
    c_all = first[1].reshape(N_DEV, d)
    conv_w_all = jnp.transpose(first[2], (1, 0, 2)).reshape(conv_w.shape[1], -1)
    ffn_conv_w_all = jnp.transpose(first[3], (1, 0, 2)).reshape(ffn_conv_w.shape[1], -1)

    ada_cols = w_ada.shape[2]
    b_cols = lax.dynamic_slice(b_ada, (0, me * ada_cols), (1, ada_cols))
    mod_cols = _ada_fwd(c_all, w_ada[0], b_cols)
    mod_all, = _all_gather([mod_cols], "gather_mod")

    later = ("w_uq", "w_ukv", "w_proj_rnn", "w_proj_mla", "w_out", "w_up", "w_down")
    flights, started = _gather_start([shard[n] for n in later], mod_all, "gather_start")
    flight = dict(zip(later, flights))

    def fetch(names, after):
        lands = _gather_wait([flight[n] for n in names], after, "gather_wait_" + names[0])
        return {n: whole(n, lax.dynamic_update_index_in_dim(g, shard[n], me, 0)) for n, g in zip(names, lands)}

    mod = lax.dynamic_index_in_dim(mod_all, me, axis=1, keepdims=False).reshape(6, d) + started[0, 0]

    sm = {n: w[n][0] for n in REPLICATED if n not in ("b_ada", "final_g")}
    sm["final_g"] = final_g
    sm["conv_w"] = conv_w_all
    sm["ffn_conv_w"] = ffn_conv_w_all
    in_flight, windows = {}, {}

    def emit(n, g):
        rows = g.shape[0] // N_DEV
        if rows % ROW_ALIGN == 0:
            windows[n] = (None, None)
            g = g.reshape(N_DEV, rows, g.shape[1])
        else:
            span = max(rows * k - _span_start(k, rows) for k in range(N_DEV)) + rows
            windows[n] = (rows, -(-span // ROW_ALIGN) * ROW_ALIGN)
            assert _span_start(N_DEV - 1, rows) + windows[n][1] <= g.shape[0], (n, g.shape)
        *in_flight[n], token = _scatter_start(g, "scatter_start_" + n, *windows[n])
        return token[0, 0]

    sq, grad_x, gs, dmod = _local_step(x[0], mod, positions[0], loss_target[0], whole("w_in", first[0]), fetch, sm, emit)

    small_names = [n for n in REPLICATED if n != "b_ada"] + ["conv_w", "ffn_conv_w"]
    small_shapes = [gs[n].shape for n in small_names] + [(6 * d,), (1,)]
    partial = _pack([gs[n] for n in small_names] + [dmod, sq.reshape(1)], N_DEV * SUBLANES)
    *small_flight, small_started = _scatter_start(partial.reshape(N_DEV, -1, PACK_LANES), "scatter_small_start")

    grads, deltas, new_m, new_v = {}, {}, {}, {}

    def update(n, parts, own=None):
        shp = w[n].shape
        lay = jnp.transpose if n in TRANSPOSED_GRADS else (lambda a: a)
        res = _adamw(parts, lay(two_d(w[n])), lay(two_d(m[n])), lay(two_d(v[n])), "adamw_" + n, own)
        grads[n], deltas[n], new_m[n], new_v[n] = [lay(a).reshape(shp) for a in res]

    for n in big:
        rows, span = windows[n]
        src, landed = _scatter_wait(*in_flight[n], small_started, "scatter_wait_" + n, rows, span)
        if rows is None:
            update(n, landed, lax.dynamic_index_in_dim(src, me, axis=0, keepdims=False))
        else:
            start = _span_start(me, rows)
            own = lax.dynamic_slice(src, (start, 0), (span, src.shape[1]))
            total = _sum_sources(landed, own, "sum_" + n)
            update(n, lax.dynamic_slice(total, (rows * me - start, 0), (rows, src.shape[1]))[None])

    chunks, landed = _scatter_wait(*small_flight, new_v[big[-1]], "scatter_small_wait")
    mine = _sum_sources(landed, lax.dynamic_index_in_dim(chunks, me, axis=0, keepdims=False), "sum_small")
    summed_all, dmod_all = _all_gather([mine, dmod.reshape(1, 6 * d)], "gather_small")
    summed = _unpack(summed_all, small_shapes)
    g_small = dict(zip(small_names, summed[:len(small_names)]))
    g_small["b_ada"] = summed[len(small_names)]
    loss = 0.5 * summed[-1][0] / d
    dmod_cols = lax.dynamic_slice(dmod_all.reshape(N_DEV, 6 * d), (0, me * ada_cols), (N_DEV, ada_cols))

    update("w_ada", _ada_bwd(c_all, dmod_cols)[None])

    for n in ("conv_w", "ffn_conv_w"):
        cols = w[n].shape[2]
        g_small[n] = lax.dynamic_slice(g_small[n], (0, me * cols), (g_small[n].shape[0], cols))
    small = REPLICATED + ("conv_w", "ffn_conv_w")
    as_rows = lambda a: a.reshape(1, -1) if a.ndim == 1 else a
    res = _adamw_many([as_rows(g_small[n].reshape(w[n].shape)) for n in small], [as_rows(w[n]) for n in small],
                      [as_rows(m[n]) for n in small], [as_rows(v[n]) for n in small], "adamw_small")
    for i, n in enumerate(small):
        grads[n], deltas[n], new_m[n], new_v[n] = [a.reshape(w[n].shape) for a in res[4 * i:4 * i + 4]]

    return (loss, grad_x[None], *[grads[n] for n in WEIGHTS], *[deltas[n] for n in WEIGHTS],
            *[new_m[n] for n in WEIGHTS], *[new_v[n] for n in WEIGHTS])
```

```python
import functools
import math

import jax
import jax.numpy as jnp
from jax import lax
from jax.experimental import pallas as pl
from jax.experimental.pallas import tpu as pltpu

F32 = jnp.float32
BF16 = jnp.bfloat16

N_DEV = 8
LANES = 128
SUBLANES = 8
VMEM_LIMIT = 56 * 1024 * 1024

D_RNN = 1280
Q_LORA = 384
KV_LORA = 256
QK_NOPE = 64
QK_ROPE = 32
V_HEAD = 64
N_HEADS = 16
D_FF = 2816
ROPE_THETA = 10000.0
LRU_C = 8.0
EPS = 1e-6
MLA_W = 768
ATT_SCALE = 1.0 / math.sqrt(QK_NOPE + QK_ROPE)

ADAM_LR, ADAM_B1, ADAM_B2, ADAM_EPS, ADAM_WD, ADAM_STEP = 0.001, 0.9, 0.999, 1e-08, 0.01, 10


def _cparams(sem=None):
    return pltpu.CompilerParams(dimension_semantics=sem, vmem_limit_bytes=VMEM_LIMIT)


def _pick(n, prefs):
    for p in prefs:
        if n % p == 0:
            return p
    return n


def _sigmoid(v):
    return 0.5 * jnp.tanh(0.5 * v) + 0.5


def _lane(shape):
    return lax.broadcasted_iota(jnp.int32, shape, len(shape) - 1)


def _row(shape):
    return lax.broadcasted_iota(jnp.int32, shape, len(shape) - 2)


MM_BLOCK_BYTES = 36 * 1024 * 1024


def _divisors(n):
    return [t for t in range(n, 0, -LANES) if n % t == 0] if n % LANES == 0 else [n]


HBM_BYTES_PER_US = 3.0e6
MXU_FLOPS_PER_US = 8.0e8
GRID_STEP_US = 0.35


def _mm_tiles(m, n, k, a_bytes, b_bytes, o_bytes):
    best = None
    for tm in [t for t in _divisors(m) if t <= 1024]:
        for tn in [t for t in _divisors(n) if t <= 2048]:
            for tk in _divisors(k):
                nk = k // tk
                need = 2 * (tm * tk * a_bytes + tk * tn * b_bytes + tm * tn * o_bytes) + (tm * tn * 4 if nk > 1 else 0)
                if need > MM_BLOCK_BYTES:
                    continue
                gi, gj = m // tm, n // tn
                for rows_outer in (True, False):
                    if nk > 1:
                        a_reads, b_reads = gj, gi
                    elif rows_outer:
                        a_reads, b_reads = 1, (gi if gj > 1 else 1)
                    else:
                        a_reads, b_reads = (gj if gi > 1 else 1), 1
                    traffic = m * k * a_bytes * a_reads + k * n * b_bytes * b_reads + m * n * (o_bytes + (8 * nk if nk > 1 else 0))
                    cost = max(traffic / HBM_BYTES_PER_US, 2.0 * m * n * k / MXU_FLOPS_PER_US) + gi * gj * nk * GRID_STEP_US
                    if best is None or cost < best[0]:
                        best = (cost, tm, tn, tk, rows_outer)
                break
    if best is None:
        raise ValueError((m, n, k))
    return best[1:]


def _mm(a, b, *, ta=False, tb=False, out_dtype=F32, also_t=None, name):
    (k_a, m) = a.shape if ta else a.shape[::-1]
    (n, k_b) = b.shape if tb else b.shape[::-1]
    assert k_a == k_b, (a.shape, b.shape, ta, tb)
    k = k_a
    tm, tn, tk, rows_outer = _mm_tiles(m, n, k, a.dtype.itemsize, b.dtype.itemsize, jnp.dtype(out_dtype).itemsize)
    nk = k // tk
    dims = (((0 if ta else 1,), (1 if tb else 0,)), ((), ()))
    n_out = 1 if also_t is None else 2

    def body(a_ref, b_ref, *rest):
        outs, acc = rest[:n_out], rest[n_out:]
        part = lax.dot_general(a_ref[...].astype(BF16), b_ref[...].astype(BF16), dims, preferred_element_type=F32)

        def write(val):
            outs[0][...] = val.astype(out_dtype)
            if also_t is not None:
                outs[1][...] = jnp.transpose(val).astype(also_t)

        if nk == 1:
            write(part)
            return
        acc_ref, = acc
        kk = pl.program_id(2)

        @pl.when(kk == 0)
        def _():
            acc_ref[...] = part

        @pl.when(kk > 0)
        def _():
            acc_ref[...] += part

        @pl.when(kk == nk - 1)
        def _():
            write(acc_ref[...])

    def ij(f):
        return (lambda i, j, kk: f(i, j, kk)) if rows_outer else (lambda j, i, kk: f(i, j, kk))

    a_spec = pl.BlockSpec((tk, tm), ij(lambda i, j, kk: (kk, i))) if ta else pl.BlockSpec((tm, tk), ij(lambda i, j, kk: (i, kk)))
    b_spec = pl.BlockSpec((tn, tk), ij(lambda i, j, kk: (j, kk))) if tb else pl.BlockSpec((tk, tn), ij(lambda i, j, kk: (kk, j)))
    out_specs = [pl.BlockSpec((tm, tn), ij(lambda i, j, kk: (i, j)))]
    out_shape = [jax.ShapeDtypeStruct((m, n), out_dtype)]
    if also_t is not None:
        out_specs.append(pl.BlockSpec((tn, tm), ij(lambda i, j, kk: (j, i))))
        out_shape.append(jax.ShapeDtypeStruct((n, m), also_t))
    res = pl.pallas_call(
        body, name=name,
        grid=(m // tm, n // tn, nk) if rows_outer else (n // tn, m // tm, nk),
        in_specs=[a_spec, b_spec], out_specs=out_specs, out_shape=out_shape,
        scratch_shapes=[] if nk == 1 else [pltpu.VMEM((tm, tn), F32)],
        compiler_params=_cparams(("arbitrary", "arbitrary", "arbitrary")),
    )(a, b)
    return res[0] if also_t is None else res


def _rowwise(fn, row_ins, par_ins, out_defs, red_defs, *, name, tr=256):
    s = row_ins[0].shape[0]
    tr = min(tr, s)
    nr, npar, no = len(row_ins), len(par_ins), len(out_defs)

    def body(*refs):
        rin, pin = refs[:nr], refs[nr:nr + npar]
        outs, reds = refs[nr + npar:nr + npar + no], refs[nr + npar + no:]
        i = pl.program_id(0)

        @pl.when(i == 0)
        def _():
            for r in reds:
                r[...] = jnp.zeros_like(r)

        fn(i, rin, pin, outs, reds)

    in_specs = [pl.BlockSpec((tr, a.shape[1]), lambda i: (i, 0)) for a in row_ins]
    in_specs += [pl.BlockSpec(a.shape, lambda i, nd=a.ndim: (0,) * nd) for a in par_ins]
    out_specs = [pl.BlockSpec((tr, c), lambda i: (i, 0)) for c, _ in out_defs]
    out_specs += [pl.BlockSpec(shp, lambda i: (0, 0)) for shp in red_defs]
    out_shape = [jax.ShapeDtypeStruct((s, c), dt) for c, dt in out_defs]
    out_shape += [jax.ShapeDtypeStruct(shp, F32) for shp in red_defs]
    return pl.pallas_call(
        body, name=name, grid=(s // tr,), in_specs=in_specs, out_specs=out_specs, out_shape=out_shape,
        compiler_params=_cparams(("arbitrary",)),
    )(*row_ins, *par_ins)


def _rms(v):
    return lax.rsqrt(jnp.mean(v * v, axis=-1, keepdims=True) + EPS)


def _colsum(v):
    return jnp.sum(v, axis=0, keepdims=True)


def _rms_bwd(dn, n, rstd):
    return rstd * (dn - n * jnp.mean(dn * n, axis=-1, keepdims=True))


def _norm_mod_fwd(x, gmod, name):
    def fn(i, rin, pin, outs, reds):
        xv = rin[0][...]
        p = pin[0][...]
        n = xv * _rms(xv)
        outs[0][...] = ((n * p[0:1]) * (1.0 + p[1:2]) + p[2:3]).astype(BF16)

    return _rowwise(fn, [x], [gmod], [(x.shape[1], BF16)], [], name=name)[0]


def _rope(v, rot_c, rot_s):
    half = QK_ROPE // 2
    swapped = jnp.where(_lane(v.shape) < QK_NOPE + half, pltpu.roll(v, LANES - half, 1), pltpu.roll(v, half, 1))
    return v * rot_c + swapped * rot_s


def _rope_t(dv, rot_c, rot_s):
    half = QK_ROPE // 2
    ds = dv * rot_s
    lane = _lane(dv.shape)
    swapped = jnp.where(lane < QK_NOPE + half, pltpu.roll(ds, LANES - half, 1), pltpu.roll(ds, half, 1))
    in_rope = (lane >= QK_NOPE) & (lane < QK_NOPE + QK_ROPE)
    return dv * rot_c + jnp.where(in_rope, swapped, 0.0)


def _mla_prep_fwd(proj_mla, rot_c, rot_s, ng):
    o1, o2 = Q_LORA, Q_LORA + KV_LORA

    def fn(i, rin, pin, outs, reds):
        g = pin[0][...]
        ql = rin[0][:, 0:o1]
        kl = rin[0][:, o1:o2]
        outs[0][...] = (ql * _rms(ql) * g[0:1, 0:o1]).astype(BF16)
        outs[1][...] = (kl * _rms(kl) * g[0:1, o1:o2]).astype(BF16)
        kr = pltpu.roll(rin[0][:, o2:o2 + LANES], QK_NOPE, 1)
        outs[2][...] = _rope(kr, rin[1][...], rin[2][...]).astype(BF16)

    return _rowwise(fn, [proj_mla, rot_c, rot_s], [ng], [(Q_LORA, BF16), (KV_LORA, BF16), (LANES, BF16)], [],
                    name="mla_prep_fwd")


def _mla_prep_bwd(proj_mla, dqn, dkvn, dkr, rot_c, rot_s, ng):
    o1, o2 = Q_LORA, Q_LORA + KV_LORA

    def fn(i, rin, pin, outs, reds):
        g = pin[0][...]
        ql = rin[0][:, 0:o1]
        kl = rin[0][:, o1:o2]
        rq, rk = _rms(ql), _rms(kl)
        nq, nk = ql * rq, kl * rk
        dq, dk = rin[1][...], rin[2][...]
        outs[0][:, 0:o1] = _rms_bwd(dq * g[0:1, 0:o1], nq, rq).astype(BF16)
        outs[0][:, o1:o2] = _rms_bwd(dk * g[0:1, o1:o2], nk, rk).astype(BF16)
        dkr_pre = pltpu.roll(_rope_t(rin[3][...], rin[4][...], rin[5][...]), LANES - QK_NOPE, 1)
        outs[0][:, o2:] = jnp.where(_lane(dkr_pre.shape) < QK_ROPE, dkr_pre, 0.0).astype(BF16)
        reds[0][0:1, 0:o1] += _colsum(dq * nq)
        reds[0][0:1, o1:o2] += _colsum(dk * nk)

    return _rowwise(fn, [proj_mla, dqn, dkvn, dkr, rot_c, rot_s], [ng], [(MLA_W, BF16)], [(SUBLANES, MLA_W)],
                    name="mla_prep_bwd")


def _rope_bwd(dq, rot_c, rot_s):
    def fn(i, rin, pin, outs, reds):
        c, sn = rin[1][...] * Q_PRESCALE, rin[2][...] * Q_PRESCALE
        for h in range(N_HEADS):
            sl = slice(h * LANES, (h + 1) * LANES)
            outs[0][:, sl] = _rope_t(rin[0][:, sl], c, sn).astype(BF16)

    return _rowwise(fn, [dq, rot_c, rot_s], [], [(dq.shape[1], BF16)], [], name="rope_bwd")[0]


def _rope_fwd_t(q, rot_c, rot_s):
    s, c = q.shape
    tr = min(256, s)

    def body(q_ref, c_ref, s_ref, o_ref, ot_ref):
        cc, sn = c_ref[...] * Q_PRESCALE, s_ref[...] * Q_PRESCALE
        for h in range(N_HEADS):
            sl = slice(h * LANES, (h + 1) * LANES)
            rot = _rope(q_ref[:, sl], cc, sn)
            o_ref[:, sl] = rot.astype(BF16)
            ot_ref[sl, :] = jnp.transpose(rot).astype(BF16)

    return pl.pallas_call(
        body, name="rope_fwd", grid=(s // tr,),
        in_specs=[pl.BlockSpec((tr, c), lambda i: (i, 0)), pl.BlockSpec((tr, LANES), lambda i: (i, 0)),
                  pl.BlockSpec((tr, LANES), lambda i: (i, 0))],
        out_specs=[pl.BlockSpec((tr, c), lambda i: (i, 0)), pl.BlockSpec((c, tr), lambda i: (0, i))],
        out_shape=[jax.ShapeDtypeStruct((s, c), BF16), jax.ShapeDtypeStruct((c, s), BF16)],
        compiler_params=_cparams(("arbitrary",)),
    )(q, rot_c, rot_s)


def _merge_fwd(pr, pm, proj_g):
    d = pr.shape[1]

    def fn(i, rin, pin, outs, reds):
        g_rnn, g_mla = rin[2][:, 0:d].astype(F32), rin[2][:, d:].astype(F32)
        outs[0][...] = (_sigmoid(g_rnn) * rin[0][...].astype(F32) + _sigmoid(g_mla) * rin[1][...].astype(F32)).astype(BF16)

    return _rowwise(fn, [pr, pm, proj_g], [], [(d, BF16)], [], name="merge_fwd")[0]


def _merge_bwd(dmerged, pr, pm, proj_g):
    d = pr.shape[1]

    def fn(i, rin, pin, outs, reds):
        dm = rin[0][...]
        sr, sm = _sigmoid(rin[3][:, 0:d].astype(F32)), _sigmoid(rin[3][:, d:].astype(F32))
        outs[0][...] = (dm * sr).astype(BF16)
        outs[1][...] = (dm * sm).astype(BF16)
        outs[2][:, 0:d] = (dm * rin[1][...].astype(F32) * sr * (1.0 - sr)).astype(BF16)
        outs[2][:, d:] = (dm * rin[2][...].astype(F32) * sm * (1.0 - sm)).astype(BF16)

    return _rowwise(fn, [dmerged, pr, pm, proj_g], [], [(d, BF16), (d, BF16), (2 * d, BF16)], [], name="merge_bwd")


def _resid_norm_fwd(x, o, gmod):
    d = x.shape[1]

    def fn(i, rin, pin, outs, reds):
        p = pin[0][...]
        x1 = rin[0][...] + p[3:4] * rin[1][...]
        outs[0][...] = x1
        outs[1][...] = ((x1 * _rms(x1) * p[0:1]) * (1.0 + p[1:2]) + p[2:3]).astype(BF16)

    return _rowwise(fn, [x, o], [gmod], [(d, F32), (d, BF16)], [], name="resid_norm_fwd")


def _final_fwd_bwd(x1, dn, target, par):
    d = x1.shape[1]

    def fn(i, rin, pin, outs, reds):
        p = pin[0][...]
        dnv = rin[1][...]
        x2 = rin[0][...] + p[0:1] * dnv
        rstd = _rms(x2)
        n3 = x2 * rstd
        err = n3 * p[1:2] - rin[2][...]
        dy = err * (1.0 / d)
        dx2 = _rms_bwd(dy * p[1:2], n3, rstd)
        outs[0][...] = dx2
        outs[1][...] = (dx2 * p[0:1]).astype(BF16)
        reds[0][0:1, :] += _colsum(dy * n3)
        reds[0][1:2, :] += _colsum(dx2 * dnv)
        reds[0][2:3, :] += jnp.zeros((1, d), F32) + jnp.sum(err * err)

    return _rowwise(fn, [x1, dn, target], [par], [(d, F32), (d, BF16)], [(SUBLANES, d)], name="final_fwd_bwd")


def _norm2_bwd(x1, dh2, dx2, o, gmod):
    d = x1.shape[1]

    def fn(i, rin, pin, outs, reds):
        p = pin[0][...]
        x1v, dh = rin[0][...], rin[1][...]
        rstd = _rms(x1v)
        n2 = x1v * rstd
        dx1 = rin[2][...] + _rms_bwd(dh * (p[0:1] * (1.0 + p[1:2])), n2, rstd)
        outs[0][...] = dx1
        outs[1][...] = (dx1 * p[3:4]).astype(BF16)
        reds[0][0:1, :] += _colsum(dh * n2 * (1.0 + p[1:2]))
        reds[0][1:2, :] += _colsum(dh * n2 * p[0:1])
        reds[0][2:3, :] += _colsum(dh)
        reds[0][3:4, :] += _colsum(dx1 * rin[3][...])

    return _rowwise(fn, [x1, dh2, dx2, o], [gmod], [(d, F32), (d, BF16)], [(SUBLANES, d)], name="norm2_bwd")


def _norm1_bwd(x, dh_a, dh_b, dh_c, dx1, gmod):
    d = x.shape[1]

    def fn(i, rin, pin, outs, reds):
        p = pin[0][...]
        xv = rin[0][...]
        dh = rin[1][...] + rin[2][...] + rin[3][...]
        rstd = _rms(xv)
        n1 = xv * rstd
        outs[0][...] = rin[4][...] + _rms_bwd(dh * (p[0:1] * (1.0 + p[1:2])), n1, rstd)
        reds[0][0:1, :] += _colsum(dh * n1 * (1.0 + p[1:2]))
        reds[0][1:2, :] += _colsum(dh * n1 * p[0:1])
        reds[0][2:3, :] += _colsum(dh)

    return _rowwise(fn, [x, dh_a, dh_b, dh_c, dx1], [gmod], [(d, F32)], [(SUBLANES, d)], name="norm1_bwd")


RNN_CHUNK = 512


def _shift_down(ref, base, n, j):
    v = ref[pl.ds(base, n + SUBLANES), :]
    return v[SUBLANES:] if j == 0 else pltpu.roll(v, j, 0)[SUBLANES:]


def _shift_up(ref, base, n, j, top_pad):
    v = ref[pl.ds(base + top_pad, n + SUBLANES), :]
    return v[:n] if j == 0 else pltpu.roll(v, n + SUBLANES - j, 0)[:n]


SCAN_GROUP = 128


def _scan_sizes(s):
    sizes = [s]
    while sizes[-1] > SUBLANES:
        assert sizes[-1] % SUBLANES == 0, s
        sizes.append(sizes[-1] // SUBLANES)
    return sizes


def _scan_scratch(s):
    return [pltpu.VMEM((n + 2 * SUBLANES, LANES), F32) for n in _scan_sizes(s)[1:] for _ in range(2)]


def _linear_scan(a_ref, b_ref, out_ref, a_off, s, reverse, levels):
    sizes = _scan_sizes(s)
    lv = [(a_ref, b_ref, a_off, 0)] + [(levels[2 * i], levels[2 * i + 1], 0, SUBLANES) for i in range(len(sizes) - 1)]
    zero8 = jnp.zeros((SUBLANES, LANES), F32)
    for (ar, br, _, _), n in zip(lv[1:], sizes[1:]):
        br[0:SUBLANES, :] = zero8
        br[pl.ds(n + SUBLANES, SUBLANES), :] = zero8
    order = list(range(SUBLANES - 1, -1, -1)) if reverse else list(range(SUBLANES))

    for lvl in range(len(sizes) - 1):
        ar, br, aoff, off = lv[lvl]
        m = sizes[lvl + 1]
        g = min(m, SCAN_GROUP)
        for t0 in range(0, m, g):
            acc_a = acc_b = None
            for r in order:
                sa = pl.ds(off + SUBLANES * t0 + r + aoff, g, stride=SUBLANES)
                sb = pl.ds(off + SUBLANES * t0 + r, g, stride=SUBLANES)
                a, b = ar[sa, :], br[sb, :]
                if acc_a is None:
                    acc_a, acc_b = a, b
                else:
                    acc_b = a * acc_b + b
                    acc_a = a * acc_a
            lv[lvl + 1][0][pl.ds(SUBLANES + t0, g), :] = acc_a
            lv[lvl + 1][1][pl.ds(SUBLANES + t0, g), :] = acc_b

    ar, br, _, off = lv[-1]
    n = sizes[-1]
    a, b = ar[pl.ds(off, n), :], br[pl.ds(off, n), :]
    h, rows = jnp.zeros((1, LANES), F32), [None] * n
    for j in (range(n - 1, -1, -1) if reverse else range(n)):
        h = a[j:j + 1, :] * h + b[j:j + 1, :]
        rows[j] = h
    br[pl.ds(off, n), :] = jnp.concatenate(rows, axis=0)

    for lvl in range(len(sizes) - 2, -1, -1):
        ar, br, aoff, off = lv[lvl]
        m = sizes[lvl + 1]
        up = lv[lvl + 1][1]
        dst = out_ref if lvl == 0 else br
        g = min(m, SCAN_GROUP)
        for t0 in range(0, m, g):
            h = _shift_up(up, t0, g, 1, SUBLANES) if reverse else _shift_down(up, t0, g, 1)
            for r in order:
                sa = pl.ds(off + SUBLANES * t0 + r + aoff, g, stride=SUBLANES)
                sb = pl.ds(off + SUBLANES * t0 + r, g, stride=SUBLANES)
                h = ar[sa, :] * h + br[sb, :]
                dst[sb, :] = h


def _one_minus_exp(z):
    series = -z * (1.0 + z * (0.5 + z * (1.0 / 6.0 + z * (1.0 / 24.0 + z * (1.0 / 120.0 + z * (1.0 / 720.0))))))
    return jnp.where(z > -0.1, series, 1.0 - jnp.exp(z))


def _softplus(v):
    return jnp.maximum(v, 0.0) + jnp.log(1.0 + jnp.exp(-jnp.abs(v)))


def _rnn_gates(xc, w, wa, wx, sp):
    xb = xc.astype(BF16)
    ra = _sigmoid(jnp.dot(xb, wa, preferred_element_type=F32) + w[5:6])
    ix = _sigmoid(jnp.dot(xb, wx, preferred_element_type=F32) + w[6:7])
    la = (-LRU_C) * ra * sp
    a = jnp.exp(la)
    mult = jnp.sqrt(_one_minus_exp(2.0 * la))
    return ra, ix, a, mult


def _rnn_fwd(x_rnn, keep, rp, wa_bd, wx_bd):
    s, r = x_rnn.shape
    ts = min(RNN_CHUNK, s)

    def body(x_ref, keep_ref, rp_ref, wa_ref, wx_ref, xc_ref, ra_ref, ix_ref, hs_ref, xpad, a_s, b_s, *levels):
        xpad[0:SUBLANES, :] = jnp.zeros((SUBLANES, LANES), F32)
        xpad[SUBLANES:, :] = x_ref[...]
        w = rp_ref[...]
        sp = _softplus(-w[7:8])
        wa, wx = wa_ref[0], wx_ref[0]

        def chunk(c, carry):
            base = pl.multiple_of(c * ts, ts)
            xc = w[4:5] + w[3:4] * _shift_down(xpad, base, ts, 0)
            for j in range(1, 4):
                xc = xc + w[3 - j:4 - j] * _shift_down(xpad, base, ts, j)
            ra, ix, a, mult = _rnn_gates(xc, w, wa, wx, sp)
            kp = keep_ref[pl.ds(base, ts), :]
            xc_ref[pl.ds(base, ts), :] = xc
            ra_ref[pl.ds(base, ts), :] = ra
            ix_ref[pl.ds(base, ts), :] = ix
            a_s[pl.ds(base, ts), :] = a * kp
            b_s[pl.ds(base, ts), :] = jnp.where(kp > 0.0, mult, 1.0) * (ix * xc)
            return carry

        lax.fori_loop(0, s // ts, chunk, 0)

        _linear_scan(a_s, b_s, hs_ref, 0, s, False, levels)

    col = pl.BlockSpec((s, LANES), lambda g: (0, g))
    return pl.pallas_call(
        body, name="rnn_fwd", grid=(r // LANES,),
        in_specs=[col, pl.BlockSpec((s, 1), lambda g: (0, 0)), pl.BlockSpec((SUBLANES, LANES), lambda g: (0, g)),
                  pl.BlockSpec((1, LANES, LANES), lambda g: (g, 0, 0)), pl.BlockSpec((1, LANES, LANES), lambda g: (g, 0, 0))],
        out_specs=[col] * 4,
        out_shape=[jax.ShapeDtypeStruct((s, r), F32)] * 4,
        scratch_shapes=[pltpu.VMEM((s + SUBLANES, LANES), F32), pltpu.VMEM((s, LANES), F32), pltpu.VMEM((s, LANES), F32),
                        *_scan_scratch(s)],
        compiler_params=_cparams(("arbitrary",)),
    )(x_rnn, keep, rp, wa_bd, wx_bd)


def _rnn_bwd(x_rnn, xc, ra, ix, hs, dy, keep, rp, wa_bd, wx_bd):
    s, r = x_rnn.shape
    ts = min(RNN_CHUNK, s)

    def body(x_ref, xc_ref, ra_ref, ix_ref, hs_ref, dy_ref, keep_ref, rp_ref, wa_ref, wx_ref,
             dx_ref, dwa_ref, dwx_ref, red_ref, xpad, hpad, a_s, dh_s, dxc_s, *levels):
        zero8 = jnp.zeros((SUBLANES, LANES), F32)
        xpad[0:SUBLANES, :] = zero8
        xpad[SUBLANES:, :] = x_ref[...]
        hpad[0:SUBLANES, :] = zero8
        hpad[SUBLANES:, :] = hs_ref[...]
        a_s[s:, :] = zero8
        dxc_s[s:, :] = zero8
        w = rp_ref[...]
        sp = _softplus(-w[7:8])
        wa, wx = wa_ref[0], wx_ref[0]

        def decay(c, carry):
            base = pl.multiple_of(c * ts, ts)
            a = jnp.exp((-LRU_C) * ra_ref[pl.ds(base, ts), :] * sp)
            a_s[pl.ds(base, ts), :] = a * keep_ref[pl.ds(base, ts), :]
            return carry

        lax.fori_loop(0, s // ts, decay, 0)

        _linear_scan(a_s, dy_ref, dh_s, 1, s, True, levels)

        def gates(c, carry):
            dwa, dwx, d_ba, d_bx, d_sp, d_cb = carry
            base = pl.multiple_of(c * ts, ts)
            xcv = xc_ref[pl.ds(base, ts), :]
            rav = ra_ref[pl.ds(base, ts), :]
            ixv = ix_ref[pl.ds(base, ts), :]
            kp = keep_ref[pl.ds(base, ts), :]
            dh = dh_s[pl.ds(base, ts), :]
            h_prev = _shift_down(hpad, base, ts, 1)
            la = (-LRU_C) * rav * sp
            a = jnp.exp(la)
            mult = jnp.sqrt(_one_minus_exp(2.0 * la))
            mult_eff = jnp.where(kp > 0.0, mult, 1.0)
            d_a = dh * h_prev * kp
            d_mult = dh * (ixv * xcv) * kp
            d_ix = dh * mult_eff * xcv
            d_xc = dh * mult_eff * ixv
            d_la = d_a * a - d_mult * (a * a) / mult
            d_pa = d_la * ((-LRU_C) * sp) * rav * (1.0 - rav)
            d_px = d_ix * ixv * (1.0 - ixv)
            xb = xcv.astype(BF16)
            pab, pxb = d_pa.astype(BF16), d_px.astype(BF16)
            tn = (((0,), (0,)), ((), ()))
            nt_ = (((1,), (1,)), ((), ()))
            dwa = dwa + lax.dot_general(xb, pab, tn, preferred_element_type=F32)
            dwx = dwx + lax.dot_general(xb, pxb, tn, preferred_element_type=F32)
            d_xc = d_xc + lax.dot_general(pab, wa, nt_, preferred_element_type=F32)
            d_xc = d_xc + lax.dot_general(pxb, wx, nt_, preferred_element_type=F32)
            dxc_s[pl.ds(base, ts), :] = d_xc
            return (dwa, dwx, d_ba + _colsum(d_pa), d_bx + _colsum(d_px),
                    d_sp + _colsum(d_la * ((-LRU_C) * rav)), d_cb + _colsum(d_xc))

        z1 = jnp.zeros((1, LANES), F32)
        zw = jnp.zeros((LANES, LANES), F32)
        dwa, dwx, d_ba, d_bx, d_sp, d_cb = lax.fori_loop(0, s // ts, gates, (zw, zw, z1, z1, z1, z1))
        dwa_ref[0] = dwa
        dwx_ref[0] = dwx

        def conv(c, carry):
            base = pl.multiple_of(c * ts, ts)
            d_here = dxc_s[pl.ds(base, ts), :]
            dx = w[3:4] * d_here
            for j in range(1, 4):
                dx = dx + w[3 - j:4 - j] * _shift_up(dxc_s, base, ts, j, 0)
            dx_ref[pl.ds(base, ts), :] = dx.astype(BF16)
            return tuple(carry[k] + _colsum(d_here * _shift_down(xpad, base, ts, 3 - k)) for k in range(4))

        d_w = lax.fori_loop(0, s // ts, conv, (z1, z1, z1, z1))
        d_lru = d_sp * (-_sigmoid(-w[7:8]))
        red_ref[...] = jnp.concatenate(list(d_w) + [d_cb, d_ba, d_bx, d_lru], axis=0)

    col = pl.BlockSpec((s, LANES), lambda g: (0, g))
    par = pl.BlockSpec((SUBLANES, LANES), lambda g: (0, g))
    wsp = pl.BlockSpec((1, LANES, LANES), lambda g: (g, 0, 0))
    return pl.pallas_call(
        body, name="rnn_bwd", grid=(r // LANES,),
        in_specs=[col] * 6 + [pl.BlockSpec((s, 1), lambda g: (0, 0)), par, wsp, wsp],
        out_specs=[col, wsp, wsp, par],
        out_shape=[jax.ShapeDtypeStruct((s, r), BF16), jax.ShapeDtypeStruct((r // LANES, LANES, LANES), F32),
                   jax.ShapeDtypeStruct((r // LANES, LANES, LANES), F32), jax.ShapeDtypeStruct((SUBLANES, r), F32)],
        scratch_shapes=[pltpu.VMEM((s + SUBLANES, LANES), F32), pltpu.VMEM((s + SUBLANES, LANES), F32),
                        pltpu.VMEM((s + SUBLANES, LANES), F32), pltpu.VMEM((s, LANES), F32),
                        pltpu.VMEM((s + SUBLANES, LANES), F32), *_scan_scratch(s)],
        compiler_params=_cparams(("arbitrary",)),
    )(x_rnn, xc, ra, ix, hs, dy, keep, rp, wa_bd, wx_bd)


ATT_BLOCK = 512
ATT_FWD_HEADS = 4
ATT_BWD_HEADS = 4


LOG2E = 1.4426950408889634
LN2 = 0.6931471805599453
Q_PRESCALE = ATT_SCALE * LOG2E


def _att_scores(q, kvt, krt, diagonal):
    kt_eff = jnp.where(_row(kvt.shape) < QK_NOPE, kvt, krt)
    sc = jnp.dot(q, kt_eff, preferred_element_type=F32)
    if diagonal:
        sc = jnp.where(lax.broadcasted_iota(jnp.int32, sc.shape, 1) <= lax.broadcasted_iota(jnp.int32, sc.shape, 0), sc, -jnp.inf)
    return sc


def _att_fwd(q, kv, kvt, krt):
    s = q.shape[0]
    t = min(ATT_BLOCK, s)
    nb = s // t
    hp = ATT_FWD_HEADS

    pairs = [(i, j) for i in range(nb) for j in range(i + 1)]
    i_tab = jnp.array([p[0] for p in pairs], jnp.int32)
    j_tab = jnp.array([p[1] for p in pairs], jnp.int32)

    def body(i_ref, j_ref, q_ref, kv_ref, kvt_ref, krt_ref, y_ref, lse_ref, m_s, acc_s):
        i, j = i_ref[pl.program_id(1)], j_ref[pl.program_id(1)]

        @pl.when(j == 0)
        def _():
            m_s[...] = jnp.full(m_s.shape, -jnp.inf, F32)
            acc_s[...] = jnp.zeros(acc_s.shape, F32)

        def step(diagonal):
            krt_b = krt_ref[...]
            lane = _lane((t, LANES))
            groups = [slice(c * LANES, (c + 1) * LANES) for c in range(t // LANES)]
            heads = [slice(hh * LANES, (hh + 1) * LANES) for hh in range(hp)]
            scs = [_att_scores(q_ref[:, sl], kvt_ref[sl, :], krt_b, diagonal) for sl in heads]
            stats = []
            for hh in range(hp):
                m_prev = m_s[hh]
                m_blk = scs[hh][:, groups[0]]
                for g in groups[1:]:
                    m_blk = jnp.maximum(m_blk, scs[hh][:, g])
                stats.append((m_prev, jnp.maximum(m_prev, jnp.max(m_blk, axis=-1, keepdims=True))))
            for hh in range(hp):
                m_prev, m_new = stats[hh]
                kvb = kv_ref[:, heads[hh]]
                ones_v = jnp.where(lane < QK_NOPE, jnp.ones_like(kvb), kvb)
                p = jnp.concatenate([jnp.exp2(scs[hh][:, g] - m_new).astype(BF16) for g in groups], axis=1)
                acc_s[hh] = jnp.exp2(m_prev - m_new) * acc_s[hh] + jnp.dot(p, ones_v, preferred_element_type=F32)
                m_s[hh] = m_new

        @pl.when(j < i)
        def _():
            step(False)

        @pl.when(j == i)
        def _():
            step(True)
            lane = _lane((t, LANES))
            for g in range(hp // 2):
                sl = slice(g * LANES, (g + 1) * LANES)
                a0, a1 = acc_s[2 * g], acc_s[2 * g + 1]
                l0, l1 = a0[:, 0:1], a1[:, 0:1]
                y_ref[:, sl] = jnp.where(lane < V_HEAD, pltpu.roll(a0 / l0, V_HEAD, 1), a1 / l1).astype(BF16)
                lse_ref[:, sl] = jnp.where(lane < V_HEAD, m_s[2 * g] + jnp.log(l0) * LOG2E, m_s[2 * g + 1] + jnp.log(l1) * LOG2E)

    return pl.pallas_call(
        body, name="att_fwd",
        grid_spec=pltpu.PrefetchScalarGridSpec(
            num_scalar_prefetch=2, grid=(N_HEADS // hp, len(pairs)),
            in_specs=[pl.BlockSpec((t, hp * LANES), lambda p, n, it, jt: (it[n], p)),
                      pl.BlockSpec((t, hp * LANES), lambda p, n, it, jt: (jt[n], p)),
                      pl.BlockSpec((hp * LANES, t), lambda p, n, it, jt: (p, jt[n])),
                      pl.BlockSpec((LANES, t), lambda p, n, it, jt: (0, jt[n]))],
            out_specs=[pl.BlockSpec((t, hp // 2 * LANES), lambda p, n, it, jt: (it[n], p))] * 2,
            scratch_shapes=[pltpu.VMEM((hp, t, LANES), F32)] * 2),
        out_shape=[jax.ShapeDtypeStruct((s, N_HEADS * V_HEAD), BF16), jax.ShapeDtypeStruct((s, N_HEADS * V_HEAD), F32)],
        compiler_params=_cparams(("arbitrary", "arbitrary")),
    )(i_tab, j_tab, q, kv, kvt, krt)


def _att_bwd(q, qt, kv, kvt, kr, krt, y, lse, dy, dyt):
    s = q.shape[0]
    t = min(ATT_BLOCK, s)
    nb = s // t

    hp = ATT_BWD_HEADS
    pairs = [(i, j) for j in range(nb) for i in range(j, nb)]
    i_tab = jnp.array([p[0] for p in pairs], jnp.int32)
    j_tab = jnp.array([p[1] for p in pairs], jnp.int32)

    def body(i_ref, j_ref, q_ref, qt_ref, kv_ref, kvt_ref, kr_ref, krt_ref, y_ref, lse_ref, dy_ref, dyt_ref,
             dq_ref, dkvt_ref, dkrt_ref, dkv_s):
        p_, n = pl.program_id(0), pl.program_id(1)
        i, j = i_ref[n], j_ref[n]

        @pl.when((p_ == 0) & (n == 0))
        def _():
            dkrt_ref[...] = jnp.zeros(dkrt_ref.shape, F32)

        @pl.when(n == 0)
        def _():
            dq_ref[...] = jnp.zeros(dq_ref.shape, F32)

        @pl.when(i == j)
        def _():
            dkv_s[...] = jnp.zeros(dkv_s.shape, F32)

        def step(diagonal):
            lane = _lane((t, LANES))
            row = _row((LANES, t))
            krb, krt_b = kr_ref[...], krt_ref[...]
            dyv = dy_ref[...]
            yv = y_ref[...].astype(F32)
            lsev = lse_ref[...]
            dyt_b = dyt_ref[...]
            rows = pl.ds(pl.multiple_of(i * t, t), t)
            cols = pl.ds(pl.multiple_of(j * t, t), t)
            zeros_t = jnp.zeros((V_HEAD, t), BF16)
            ones_w = jnp.ones((LANES, LANES), BF16)
            groups = [slice(c * LANES, (c + 1) * LANES) for c in range(t // LANES)]
            heads = [slice(hh * LANES, (hh + 1) * LANES) for hh in range(hp)]
            scs, dps, stats = [], [], []
            for hh, sl in enumerate(heads):
                kvt_b = kvt_ref[sl, :]
                scs.append(_att_scores(q_ref[:, sl], kvt_b, krt_b, diagonal))
                pair, first = heads[hh // 2], hh % 2 == 0
                lse_g, dy_g, y_g = lsev[:, pair], dyv[:, pair], yv[:, pair]
                mine = (lane < V_HEAD) if first else (lane >= V_HEAD)
                lse_rep = jnp.where(mine, lse_g, pltpu.roll(lse_g, V_HEAD, 1))
                do_pad = jnp.where(lane >= V_HEAD, pltpu.roll(dy_g, V_HEAD, 1) if first else dy_g, 0.0)
                o_pad = jnp.where(lane >= V_HEAD, pltpu.roll(y_g, V_HEAD, 1) if first else y_g, 0.0)
                do_ln2 = do_pad * LN2
                prod = do_ln2 * o_pad
                head_part = prod.astype(BF16)
                rest_part = (prod - head_part.astype(F32)).astype(BF16)
                delta_rep = (jnp.dot(head_part, ones_w, preferred_element_type=F32)
                             + jnp.dot(rest_part, ones_w, preferred_element_type=F32))
                dps.append(jnp.dot(do_ln2.astype(BF16), kvt_b, preferred_element_type=F32))
                stats.append((lse_rep, delta_rep))
            dkr_acc = jnp.zeros((LANES, t), F32)
            for hh, sl in enumerate(heads):
                lse_rep, delta_rep = stats[hh]
                probs, dss = [], []
                for g in groups:
                    pg = jnp.exp2(scs[hh][:, g] - lse_rep)
                    probs.append(pg.astype(BF16))
                    dss.append((pg * (dps[hh][:, g] - delta_rep)).astype(BF16))
                prob, ds = jnp.concatenate(probs, axis=1), jnp.concatenate(dss, axis=1)
                dot_pad = jnp.concatenate([zeros_t, dyt_b[hh * V_HEAD:(hh + 1) * V_HEAD, :]], axis=0)
                k_eff = jnp.where(lane < QK_NOPE, kv_ref[:, sl], krb)
                dvt = jnp.dot(dot_pad, prob, preferred_element_type=F32)
                dq_ref[rows, sl] += jnp.dot(ds, k_eff, preferred_element_type=F32)
                dkt = jnp.dot(qt_ref[sl, :], ds, preferred_element_type=F32)
                dkv_s[hh] += dvt + jnp.where(row < QK_NOPE, dkt, 0.0)
                dkr_acc = dkr_acc + jnp.where(row >= QK_NOPE, dkt, 0.0)
            dkrt_ref[:, cols] += dkr_acc

        @pl.when(i > j)
        def _():
            step(False)

        @pl.when(i == j)
        def _():
            step(True)

        @pl.when(i == nb - 1)
        def _():
            for hh, sl in enumerate([slice(hh * LANES, (hh + 1) * LANES) for hh in range(hp)]):
                dkvt_ref[sl, :] = dkv_s[hh].astype(BF16)

    qi = lambda p, n, it, jt: (it[n], p)
    qti = lambda p, n, it, jt: (p, it[n])
    wide, half = hp * LANES, hp // 2 * LANES
    return pl.pallas_call(
        body, name="att_bwd",
        grid_spec=pltpu.PrefetchScalarGridSpec(
            num_scalar_prefetch=2, grid=(N_HEADS // hp, len(pairs)),
            in_specs=[pl.BlockSpec((t, wide), qi), pl.BlockSpec((wide, t), qti),
                      pl.BlockSpec((t, wide), lambda p, n, it, jt: (jt[n], p)),
                      pl.BlockSpec((wide, t), lambda p, n, it, jt: (p, jt[n])),
                      pl.BlockSpec((t, LANES), lambda p, n, it, jt: (jt[n], 0)),
                      pl.BlockSpec((LANES, t), lambda p, n, it, jt: (0, jt[n])),
                      pl.BlockSpec((t, half), qi), pl.BlockSpec((t, half), qi), pl.BlockSpec((t, half), qi),
                      pl.BlockSpec((half, t), qti)],
            out_specs=[pl.BlockSpec((s, wide), lambda p, n, it, jt: (0, p)),
                       pl.BlockSpec((wide, t), lambda p, n, it, jt: (p, jt[n])),
                       pl.BlockSpec((LANES, s), lambda p, n, it, jt: (0, 0))],
            scratch_shapes=[pltpu.VMEM((hp, LANES, t), F32)]),
        out_shape=[jax.ShapeDtypeStruct((s, N_HEADS * LANES), F32), jax.ShapeDtypeStruct((N_HEADS * LANES, s), BF16),
                   jax.ShapeDtypeStruct((LANES, s), F32)],
        compiler_params=_cparams(("arbitrary", "arbitrary")),
    )(i_tab, j_tab, q, qt, kv, kvt, kr, krt, y, lse, dy, dyt)


FFN_COLS = 256


def _ffn_conv(pad_ref, w, base, n):
    u = w[3:4] + w[2:3] * _shift_down(pad_ref, base, n, 0)
    for j in range(1, 3):
        u = u + w[2 - j:3 - j] * _shift_down(pad_ref, base, n, j)
    return u


def _ffn_act_fwd(up, fp):
    s, f2 = up.shape
    f = f2 // 2
    tc = FFN_COLS
    ts = min(RNN_CHUNK, s)
    nfb = f // tc

    def body(ug_ref, uv_ref, wg_ref, wv_ref, act_ref, gpad, vpad):
        zero8 = jnp.zeros((SUBLANES, tc), F32)
        gpad[0:SUBLANES, :] = zero8
        vpad[0:SUBLANES, :] = zero8
        gpad[SUBLANES:, :] = ug_ref[...]
        vpad[SUBLANES:, :] = uv_ref[...]
        wg, wv = wg_ref[...], wv_ref[...]

        def chunk(c, carry):
            base = pl.multiple_of(c * ts, ts)
            g = _ffn_conv(gpad, wg, base, ts)
            v = _ffn_conv(vpad, wv, base, ts)
            act_ref[pl.ds(base, ts), :] = (g * _sigmoid(g) * v).astype(BF16)
            return carry

        lax.fori_loop(0, s // ts, chunk, 0)

    return pl.pallas_call(
        body, name="ffn_act_fwd", grid=(nfb,),
        in_specs=[pl.BlockSpec((s, tc), lambda b: (0, b)), pl.BlockSpec((s, tc), lambda b: (0, b + nfb)),
                  pl.BlockSpec((SUBLANES, tc), lambda b: (0, b)), pl.BlockSpec((SUBLANES, tc), lambda b: (0, b + nfb))],
        out_specs=pl.BlockSpec((s, tc), lambda b: (0, b)),
        out_shape=jax.ShapeDtypeStruct((s, f), BF16),
        scratch_shapes=[pltpu.VMEM((s + SUBLANES, tc), F32)] * 2,
        compiler_params=_cparams(("arbitrary",)),
    )(up, up, fp, fp)


def _ffn_act_bwd(up, dact, fp):
    s, f2 = up.shape
    f = f2 // 2
    tc = FFN_COLS
    ts = min(RNN_CHUNK, s)
    nfb = f // tc

    def body(ug_ref, uv_ref, da_ref, wg_ref, wv_ref, dup_ref, red_ref, gpad, vpad, dgs, dvs):
        half = pl.program_id(1)
        wg, wv = wg_ref[...], wv_ref[...]

        @pl.when(half == 0)
        def _():
            zero8 = jnp.zeros((SUBLANES, tc), F32)
            gpad[0:SUBLANES, :] = zero8
            vpad[0:SUBLANES, :] = zero8
            gpad[SUBLANES:, :] = ug_ref[...]
            vpad[SUBLANES:, :] = uv_ref[...]
            dgs[s:, :] = zero8
            dvs[s:, :] = zero8

            def act(c, carry):
                base = pl.multiple_of(c * ts, ts)
                g = _ffn_conv(gpad, wg, base, ts)
                v = _ffn_conv(vpad, wv, base, ts)
                da = da_ref[pl.ds(base, ts), :]
                sg = _sigmoid(g)
                dgs[pl.ds(base, ts), :] = da * v * (sg * (1.0 + g * (1.0 - sg)))
                dvs[pl.ds(base, ts), :] = da * (g * sg)
                return carry

            lax.fori_loop(0, s // ts, act, 0)

        def conv_t(src, pad, w, out_ref, red_ref):
            def chunk(c, carry):
                base = pl.multiple_of(c * ts, ts)
                d_here = src[pl.ds(base, ts), :]
                dx = w[2:3] * d_here
                for j in range(1, 3):
                    dx = dx + w[2 - j:3 - j] * _shift_up(src, base, ts, j, 0)
                out_ref[pl.ds(base, ts), :] = dx.astype(BF16)
                taps = tuple(carry[k] + _colsum(d_here * _shift_down(pad, base, ts, 2 - k)) for k in range(3))
                return taps + (carry[3] + _colsum(d_here),)

            z1 = jnp.zeros((1, tc), F32)
            red = lax.fori_loop(0, s // ts, chunk, (z1, z1, z1, z1))
            red_ref[...] = jnp.concatenate(list(red) + [jnp.zeros((4, tc), F32)], axis=0)

        @pl.when(half == 0)
        def _():
            conv_t(dgs, gpad, wg, dup_ref, red_ref)

        @pl.when(half == 1)
        def _():
            conv_t(dvs, vpad, wv, dup_ref, red_ref)

    gcol = pl.BlockSpec((s, tc), lambda b, h: (0, b))
    vcol = pl.BlockSpec((s, tc), lambda b, h: (0, b + nfb))
    gpar = pl.BlockSpec((SUBLANES, tc), lambda b, h: (0, b))
    vpar = pl.BlockSpec((SUBLANES, tc), lambda b, h: (0, b + nfb))
    return pl.pallas_call(
        body, name="ffn_act_bwd", grid=(nfb, 2),
        in_specs=[gcol, vcol, gcol, gpar, vpar],
        out_specs=[pl.BlockSpec((s, tc), lambda b, h: (0, b + h * nfb)),
                   pl.BlockSpec((SUBLANES, tc), lambda b, h: (0, b + h * nfb))],
        out_shape=[jax.ShapeDtypeStruct((s, f2), BF16), jax.ShapeDtypeStruct((SUBLANES, f2), F32)],
        scratch_shapes=[pltpu.VMEM((s + SUBLANES, tc), F32)] * 4,
        compiler_params=_cparams(("arbitrary", "arbitrary")),
    )(up, up, dact, fp, fp)


def _rows8(rows, width):
    rows = [r.reshape(1, width).astype(F32) for r in rows]
    return jnp.concatenate(rows + [jnp.zeros((SUBLANES - len(rows), width), F32)], axis=0)


def _block_diag(w):
    n, b, _ = w.shape
    w = w.reshape(n // 2, 2, b, b)
    z = jnp.zeros((n // 2, b, b), w.dtype)
    top = jnp.concatenate([w[:, 0], z], axis=2)
    bot = jnp.concatenate([z, w[:, 1]], axis=2)
    return jnp.concatenate([top, bot], axis=1)


def _block_diag_t(bd):
    n, b2, _ = bd.shape
    b = b2 // 2
    return jnp.stack([bd[:, :b, :b], bd[:, b:, b:]], axis=1).reshape(2 * n, b, b)


def _local_step(x, mod, positions, target, w_in, fetch, sm, emit):
    s, d = x.shape
    o_rnn, o_mla = D_RNN, D_RNN + Q_LORA + KV_LORA + QK_ROPE
    wts = {}
    w_in_rnn = w_in[:, :o_rnn]
    w_in_mla = jnp.concatenate([w_in[:, o_rnn:o_mla], jnp.zeros((d, MLA_W - (o_mla - o_rnn)), w_in.dtype)], axis=1)
    w_in_g = w_in[:, o_mla:]
    hd = QK_NOPE + QK_ROPE
    wa_bd = _block_diag(sm["w_gate_a"]).astype(BF16)
    wx_bd = _block_diag(sm["w_gate_x"]).astype(BF16)

    pos = positions.reshape(s)
    half = QK_ROPE // 2
    inv_freq = ROPE_THETA ** (-jnp.arange(half, dtype=F32) / half)
    ang = pos.astype(F32)[:, None] * inv_freq
    cos, sin = jnp.cos(ang), jnp.sin(ang)
    rot_c = jnp.concatenate([jnp.ones((s, QK_NOPE), F32), cos, cos, jnp.ones((s, LANES - hd), F32)], axis=1)
    rot_s = jnp.concatenate([jnp.zeros((s, QK_NOPE), F32), -sin, sin, jnp.zeros((s, LANES - hd), F32)], axis=1)
    keep = (pos != 0).astype(F32).reshape(s, 1)

    gmod1 = _rows8([sm["norm1_g"], mod[1], mod[0]], d)
    gmod2 = _rows8([sm["norm2_g"], mod[4], mod[3], mod[2]], d)
    rp = jnp.concatenate([sm["conv_w"].reshape(4, D_RNN), _rows8([sm["conv_b"], sm["b_gate_a"], sm["b_gate_x"], sm["lru_param"]], D_RNN)[:4]], axis=0)
    fp = _rows8([sm["ffn_conv_w"][0], sm["ffn_conv_w"][1], sm["ffn_conv_w"][2], sm["ffn_conv_b"]], 2 * D_FF)
    ng = _rows8([jnp.concatenate([sm["q_norm_g"].reshape(-1), sm["kv_norm_g"].reshape(-1), jnp.zeros((MLA_W - Q_LORA - KV_LORA,), F32)])], MLA_W)
    fpar = _rows8([mod[5], sm["final_g"]], d)

    h = _norm_mod_fwd(x, gmod1, "norm1_fwd")
    proj_rnn = _mm(h, w_in_rnn, name="mm_in_rnn")
    proj_mla = _mm(h, w_in_mla, name="mm_in_mla")
    proj_g = _mm(h, w_in_g, out_dtype=BF16, name="mm_in_g")
    xc, ra, ix, hs = _rnn_fwd(proj_rnn, keep, rp, wa_bd, wx_bd)
    qn, kvn, kr = _mla_prep_fwd(proj_mla, rot_c, rot_s, ng)
    wts.update(fetch(("w_uq", "w_ukv"), kr))
    w_uq_p = jnp.pad(wts["w_uq"].reshape(Q_LORA, N_HEADS, hd), ((0, 0), (0, 0), (0, LANES - hd))).reshape(Q_LORA, N_HEADS * LANES)
    w_ukv = wts["w_ukv"]
    q_rot, q_rot_t = _rope_fwd_t(_mm(qn, w_uq_p, name="mm_uq"), rot_c, rot_s)
    kv, kvt = _mm(kvn, w_ukv, out_dtype=BF16, also_t=BF16, name="mm_ukv")
    krt = jnp.transpose(kr)
    y_mla, lse = _att_fwd(q_rot, kv, kvt, krt)
    wts.update(fetch(("w_proj_rnn", "w_proj_mla", "w_out", "w_up", "w_down"), lse))
    pr = _mm(hs, wts["w_proj_rnn"], out_dtype=BF16, name="mm_proj_rnn")
    pm = _mm(y_mla, wts["w_proj_mla"], out_dtype=BF16, name="mm_proj_mla")
    merged = _merge_fwd(pr, pm, proj_g)
    o = _mm(merged, wts["w_out"], name="mm_out")
    x1, h2 = _resid_norm_fwd(x, o, gmod2)
    up = _mm(h2, wts["w_up"], name="mm_up")
    act = _ffn_act_fwd(up, fp)
    dn = _mm(act, wts["w_down"], name="mm_down")

    dx2, ddn, red_f = _final_fwd_bwd(x1, dn, target, fpar)
    dact = _mm(ddn, wts["w_down"], tb=True, name="mm_d_act")
    tok = emit("w_down", _mm(act, ddn, ta=True, out_dtype=BF16, name="mm_dw_down"))
    dup, red_ffn = _ffn_act_bwd(up, dact, fp + tok)
    dh2 = _mm(dup, wts["w_up"], tb=True, name="mm_d_h2")
    tok = tok + emit("w_up", _mm(dup, h2, ta=True, out_dtype=BF16, name="mm_dw_up"))
    dx1, do, red_2 = _norm2_bwd(x1, dh2, dx2, o, gmod2 + tok)
    dmerged = _mm(do, wts["w_out"], tb=True, name="mm_d_merged")
    tok = tok + emit("w_out", _mm(merged, do, ta=True, out_dtype=BF16, name="mm_dw_out"))
    dpr, dpm, dg = _merge_bwd(dmerged, pr, pm, proj_g)
    dy_rnn = _mm(dpr, wts["w_proj_rnn"], tb=True, name="mm_d_yrnn")
    tok = tok + emit("w_proj_rnn", _mm(hs, dpr, ta=True, out_dtype=BF16, name="mm_dw_proj_rnn"))
    dy_mla, dy_mla_t = _mm(dpm, wts["w_proj_mla"], tb=True, also_t=BF16, name="mm_d_ymla")
    tok = tok + emit("w_proj_mla", _mm(y_mla, dpm, ta=True, out_dtype=BF16, name="mm_dw_proj_mla"))
    dq_rot, dkvt, dkrt = _att_bwd(q_rot, q_rot_t, kv, kvt, kr, krt, y_mla, lse, dy_mla, dy_mla_t)
    dq = _rope_bwd(dq_rot, rot_c, rot_s)
    dqn = _mm(dq, w_uq_p, tb=True, name="mm_d_qn")
    dw_uq_pt = _mm(dq, qn, ta=True, out_dtype=BF16, name="mm_dw_uq")
    tok = tok + emit("w_uq", dw_uq_pt.reshape(N_HEADS, LANES, Q_LORA)[:, :hd].reshape(N_HEADS * hd, Q_LORA))
    dkvn = jnp.transpose(_mm(w_ukv, dkvt, name="mm_d_kvn"))
    tok = tok + emit("w_ukv", _mm(dkvt, kvn, out_dtype=BF16, name="mm_dw_ukv"))
    dproj_mla, red_m = _mla_prep_bwd(proj_mla, dqn, dkvn, jnp.transpose(dkrt), rot_c, rot_s, ng + tok)
    dx_rnn, dwa_bd, dwx_bd, red_r = _rnn_bwd(proj_rnn, xc, ra, ix, hs, dy_rnn, keep, rp + tok, wa_bd, wx_bd)
    dw_in_t = jnp.concatenate([
        _mm(dx_rnn, h, ta=True, out_dtype=BF16, name="mm_dw_in_rnn"),
        _mm(dproj_mla, h, ta=True, out_dtype=BF16, name="mm_dw_in_mla")[:o_mla - o_rnn],
        _mm(dg, h, ta=True, out_dtype=BF16, name="mm_dw_in_g")], axis=0)
    tok = tok + emit("w_in", dw_in_t)
    dh_a = _mm(dx_rnn, w_in_rnn, tb=True, name="mm_d_h_rnn")
    dh_b = _mm(dproj_mla, w_in_mla, tb=True, name="mm_d_h_mla")
    dh_c = _mm(dg, w_in_g, tb=True, name="mm_d_h_g")
    grad_x, red_1 = _norm1_bwd(x, dh_a, dh_b, dh_c, dx1, gmod1 + tok)

    gs = {
        "norm1_g": red_1[0], "conv_w": red_r[0:4], "conv_b": red_r[4], "w_gate_a": _block_diag_t(dwa_bd),
        "b_gate_a": red_r[5], "w_gate_x": _block_diag_t(dwx_bd), "b_gate_x": red_r[6], "lru_param": red_r[7],
        "q_norm_g": red_m[0, :Q_LORA], "kv_norm_g": red_m[0, Q_LORA:Q_LORA + KV_LORA], "norm2_g": red_2[0],
        "ffn_conv_w": red_ffn[0:3], "ffn_conv_b": red_ffn[3], "final_g": red_f[0],
    }
    dmod = jnp.stack([red_1[2], red_1[1], red_2[3], red_2[2], red_2[1], red_f[1]], axis=0)
    return red_f[2, 0], grad_x, gs, dmod


MESH_IDS = pl.DeviceIdType.MESH
HBM_SPEC = pl.BlockSpec(memory_space=pltpu.HBM)


def _my_slot():
    return 4 * lax.axis_index("x") + 2 * lax.axis_index("y") + lax.axis_index("c")


def _all_gather(arrs, name):
    n = len(arrs)

    def body(*refs):
        ins, outs = refs[:n], refs[n:2 * n]
        send_sems, recv_sems, local_sems = refs[2 * n:]
        x, y, c = lax.axis_index("x"), lax.axis_index("y"), lax.axis_index("c")
        me, sibling = (x, y, c), (x, y, 1 - c)
        chips = [(1 - x, y), (x, 1 - y), (1 - x, 1 - y)]

        def slot(dev):
            return 4 * dev[0] + 2 * dev[1] + dev[2]

        def copy(a, k, block, to, src=None):
            dst = outs[a].at[slot(block)]
            return pltpu.make_async_remote_copy(
                src_ref=dst if src is None else src, dst_ref=dst, send_sem=send_sems.at[a, k], recv_sem=recv_sems.at[a, k],
                device_id=to, device_id_type=MESH_IDS)

        mine = [pltpu.make_async_copy(ins[a], outs[a].at[slot(me)], local_sems.at[a]) for a in range(n)]
        for cp in mine:
            cp.start()
        first = []
        for a in range(n):
            first.append(copy(a, 0, me, sibling, src=ins[a]))
            first += [copy(a, 1 + j, me, (*chip, c), src=ins[a]) for j, chip in enumerate(chips)]
        for cp in first:
            cp.start()
        passed = []
        for j, chip in enumerate(chips):
            for a in range(n):
                copy(a, 1 + j, (*chip, c), me).wait_recv()
                fwd = copy(a, 4 + j, (*chip, c), sibling)
                fwd.start()
                passed.append(fwd)
        for a in range(n):
            copy(a, 0, sibling, me).wait_recv()
            for j, chip in enumerate(chips):
                copy(a, 4 + j, (*chip, 1 - c), me).wait_recv()
        for cp in first + passed:
            cp.wait_send()
        for cp in mine:
            cp.wait()

    return pl.pallas_call(
        body, name=name,
        in_specs=[HBM_SPEC] * n, out_specs=[HBM_SPEC] * n,
        out_shape=[jax.ShapeDtypeStruct((N_DEV,) + a.shape, a.dtype) for a in arrs],
        scratch_shapes=[pltpu.SemaphoreType.DMA((n, 7)), pltpu.SemaphoreType.DMA((n, 7)), pltpu.SemaphoreType.DMA((n,))],
    )(*arrs)


SEM_SPEC =pl.BlockSpec(memory_space=pltpu.SEMAPHORE)
DATAFLOW = pltpu.SideEffectType.DATAFLOW_SIDE_EFFECTING
FLIPS = [(dx, dy, dc) for dx in (0, 1) for dy in (0, 1) for dc in (0, 1)][1:]


def _peer(k):
    dx, dy, dc = FLIPS[k]
    peer = (lax.axis_index("x") ^ dx, lax.axis_index("y") ^ dy, lax.axis_index("c") ^ dc)
    return peer, 4 * peer[0] + 2 * peer[1] + peer[2]


def _gather_start(shards, after, name):
    n, nf = len(shards), len(FLIPS)

    def body(*refs):
        srcs, lands = refs[:n], refs[n:2 * n]
        send_sems, recv_sems = refs[2 * n + 1:3 * n + 1], refs[3 * n + 1:4 * n + 1]
        token = refs[-1]
        me = _my_slot()
        for a in range(n):
            for k in range(nf):
                peer, _ = _peer(k)
                pltpu.make_async_remote_copy(
                    src_ref=srcs[a], dst_ref=lands[a].at[me], send_sem=send_sems[a].at[k], recv_sem=recv_sems[a].at[k],
                    device_id=peer, device_id_type=MESH_IDS).start()
        token[...] = jnp.zeros(token.shape, F32)

    land_shapes = [(N_DEV,) + a.shape for a in shards]
    sems = [pltpu.SemaphoreType.DMA((nf,))] * n
    out = pl.pallas_call(
        body, name=name,
        out_shape=(*sems, *sems, *[pltpu.HBM(a.shape, a.dtype) for a in shards],
                   *[pltpu.HBM(shp, a.dtype) for shp, a in zip(land_shapes, shards)],
                   jax.ShapeDtypeStruct((SUBLANES, LANES), F32)),
        in_specs=[HBM_SPEC] * (2 * n) + [pl.BlockSpec(memory_space=pl.ANY)],
        out_specs=(*[SEM_SPEC] * (2 * n), *[HBM_SPEC] * (2 * n), pl.BlockSpec(memory_space=pltpu.VMEM)),
        input_output_aliases={i: 2 * n + i for i in range(2 * n)},
        compiler_params=pltpu.CompilerParams(has_side_effects=DATAFLOW),
    )(*[pltpu.with_memory_space_constraint(a, pltpu.HBM) for a in shards],
      *[pltpu.with_memory_space_constraint(lax.empty(shp, a.dtype), pltpu.HBM) for shp, a in zip(land_shapes, shards)],
      after)
    return [(out[a], out[n + a], out[2 * n + a], out[3 * n + a]) for a in range(n)], out[-1]


def _gather_wait(flights, after, name):
    n, nf = len(flights), len(FLIPS)

    def body(*refs):
        send_sems, recv_sems = refs[:n], refs[n:2 * n]
        srcs, lands = refs[2 * n:3 * n], refs[3 * n:4 * n]
        for a in range(n):
            for k in range(nf):
                peer, peer_slot = _peer(k)
                cp = pltpu.make_async_remote_copy(
                    src_ref=srcs[a], dst_ref=lands[a].at[peer_slot], send_sem=send_sems[a].at[k],
                    recv_sem=recv_sems[a].at[k], device_id=peer, device_id_type=MESH_IDS)
                cp.wait_send()
                cp.wait_recv()

    srcs, lands = [f[2] for f in flights], [f[3] for f in flights]
    out = pl.pallas_call(
        body, name=name,
        out_shape=(*[pltpu.HBM(a.shape, a.dtype) for a in srcs], *[pltpu.HBM(a.shape, a.dtype) for a in lands]),
        in_specs=[SEM_SPEC] * (2 * n) + [HBM_SPEC] * (2 * n) + [pl.BlockSpec(memory_space=pl.ANY)],
        out_specs=tuple([HBM_SPEC] * (2 * n)),
        input_output_aliases={2 * n + i: i for i in range(2 * n)},
        compiler_params=pltpu.CompilerParams(has_side_effects=DATAFLOW),
    )(*[f[0] for f in flights], *[f[1] for f in flights], *srcs, *lands, after)
    return list(out[n:])


ROW_ALIGN = 16


def _span_start(slot, rows):
    return (rows * slot) // ROW_ALIGN * ROW_ALIGN


def _chunk_of(src_ref, slot, rows, span):
    if rows is None:
        return src_ref.at[slot]
    return src_ref.at[pl.ds(pl.multiple_of(_span_start(slot, rows), ROW_ALIGN), span)]


def _scatter_start(src, name, rows=None, span=None):
    def body(src_ref, land_ref, send_sems, recv_sems, src_thru, land_thru, token):
        me = _my_slot()
        for k in range(len(FLIPS)):
            peer, peer_slot = _peer(k)
            pltpu.make_async_remote_copy(
                src_ref=_chunk_of(src_ref, peer_slot, rows, span), dst_ref=land_ref.at[me], send_sem=send_sems.at[k],
                recv_sem=recv_sems.at[k], device_id=peer, device_id_type=MESH_IDS).start()
        token[...] = jnp.zeros(token.shape, F32)

    n = len(FLIPS)
    land_shape = src.shape if rows is None else (N_DEV, span, src.shape[1])
    return pl.pallas_call(
        body, name=name,
        out_shape=(pltpu.SemaphoreType.DMA((n,)), pltpu.SemaphoreType.DMA((n,)), pltpu.HBM(src.shape, src.dtype),
                   pltpu.HBM(land_shape, src.dtype), jax.ShapeDtypeStruct((SUBLANES, LANES), F32)),
        in_specs=(HBM_SPEC, HBM_SPEC),
        out_specs=(SEM_SPEC, SEM_SPEC, HBM_SPEC, HBM_SPEC, pl.BlockSpec(memory_space=pltpu.VMEM)),
        input_output_aliases={0: 2, 1: 3},
        compiler_params=pltpu.CompilerParams(has_side_effects=DATAFLOW),
    )(pltpu.with_memory_space_constraint(src, pltpu.HBM),
      pltpu.with_memory_space_constraint(lax.empty(land_shape, src.dtype), pltpu.HBM))


def _scatter_wait(send_sems, recv_sems, src_thru, land_thru, after, name, rows=None, span=None):
    def body(src_ref, land_ref, send_sems, recv_sems, after_ref, src_dead, got_ref):
        for k in range(len(FLIPS)):
            peer, peer_slot = _peer(k)
            cp = pltpu.make_async_remote_copy(
                src_ref=_chunk_of(src_ref, peer_slot, rows, span), dst_ref=land_ref.at[peer_slot], send_sem=send_sems.at[k],
                recv_sem=recv_sems.at[k], device_id=peer, device_id_type=MESH_IDS)
            cp.wait_send()
            cp.wait_recv()

    return pl.pallas_call(
        body, name=name,
        out_shape=(pltpu.HBM(src_thru.shape, src_thru.dtype), pltpu.HBM(land_thru.shape, land_thru.dtype)),
        in_specs=(HBM_SPEC, HBM_SPEC, SEM_SPEC, SEM_SPEC, pl.BlockSpec(memory_space=pl.ANY)),
        out_specs=(HBM_SPEC, HBM_SPEC), input_output_aliases={0: 0, 1: 1},
        compiler_params=pltpu.CompilerParams(has_side_effects=DATAFLOW),
    )(src_thru, land_thru, send_sems, recv_sems, after)


def _sum_sources(parts, own, name):
    k, r, c = parts.shape
    tr = r if k * r * c <= 2 * 1024 * 1024 else _pick(r, (512, 256, 128, 64, 32, 16, 8))

    def body(p_ref, own_ref, o_ref):
        me = _my_slot()
        acc = jnp.where(me == 0, own_ref[...], p_ref[0]).astype(F32)
        for s in range(1, k):
            acc = acc + jnp.where(me == s, own_ref[...], p_ref[s]).astype(F32)
        o_ref[...] = acc

    blk = pl.BlockSpec((tr, c), lambda i: (i, 0))
    return pl.pallas_call(
        body, name=name, grid=(r // tr,),
        in_specs=[pl.BlockSpec((k, tr, c), lambda i: (0, i, 0)), blk],
        out_specs=blk,
        out_shape=jax.ShapeDtypeStruct((r, c), F32),
        compiler_params=_cparams(("arbitrary",)),
    )(parts, own)


def _adamw_math(g, w, m, v):
    m_new = ADAM_B1 * m + (1.0 - ADAM_B1) * g
    v_new = ADAM_B2 * v + (1.0 - ADAM_B2) * jnp.square(g)
    m_hat = m_new / (1.0 - ADAM_B1 ** ADAM_STEP)
    v_hat = v_new / (1.0 - ADAM_B2 ** ADAM_STEP)
    return -ADAM_LR * (m_hat / (jnp.sqrt(v_hat) + ADAM_EPS) + ADAM_WD * w), m_new, v_new


def _adamw_many(gs, ws, ms, vs, name):
    n = len(gs)

    def body(*refs):
        ins, outs = refs[:4 * n], refs[4 * n:]
        for i in range(n):
            g = ins[i][...]
            outs[4 * i][...] = g
            outs[4 * i + 1][...], outs[4 * i + 2][...], outs[4 * i + 3][...] = _adamw_math(
                g, ins[n + i][...], ins[2 * n + i][...], ins[3 * n + i][...])

    return pl.pallas_call(
        body, name=name,
        out_shape=[jax.ShapeDtypeStruct(a.shape, F32) for a in ws for _ in range(4)],
        compiler_params=_cparams(),
    )(*gs, *ws, *ms, *vs)


def _adamw(parts, w, m, v, name, own=None):
    k, r, c = parts.shape
    tr = r if r * c <= 256 * 1024 else _pick(r, (256, 128, 64, 32, 16, 8))

    def body(*refs):
        p_ref, w_ref, m_ref, v_ref = refs[:4]
        g_ref, d_ref, nm_ref, nv_ref = refs[-4:]

        def part(s):
            if own is None:
                return p_ref[s].astype(F32)
            return jnp.where(_my_slot() == s, refs[4][...], p_ref[s]).astype(F32)

        g = part(0)
        for s in range(1, k):
            g = g + part(s)
        g_ref[...] = g
        d_ref[...], nm_ref[...], nv_ref[...] = _adamw_math(g, w_ref[...], m_ref[...], v_ref[...])

    blk = pl.BlockSpec((tr, c), lambda i: (i, 0))
    return pl.pallas_call(
        body, name=name, grid=(r // tr,),
        in_specs=[pl.BlockSpec((k, tr, c), lambda i: (0, i, 0)), blk, blk, blk] + ([] if own is None else [blk]),
        out_specs=[blk] * 4,
        out_shape=[jax.ShapeDtypeStruct((r, c), F32)] * 4,
        compiler_params=_cparams(("arbitrary",)),
    )(parts, w, m, v, *([] if own is None else [own]))


def _silu(v):
    return v * _sigmoid(v)


def _ada_fwd(c_all, w, b):
    def body(c_ref, w_ref, b_ref, o_ref):
        ca = _silu(c_ref[...]).astype(BF16)
        o_ref[...] = jnp.dot(ca, w_ref[...].astype(BF16), preferred_element_type=F32) + b_ref[...]

    return pl.pallas_call(
        body, name="ada_fwd", out_shape=jax.ShapeDtypeStruct((c_all.shape[0], w.shape[1]), F32),
        compiler_params=_cparams(),
    )(c_all, w, b)


def _ada_bwd(c_all, dmod):
    def body(c_ref, d_ref, o_ref):
        ca = _silu(c_ref[...]).astype(BF16).astype(F32)
        dm = d_ref[...].astype(BF16).astype(F32)
        acc = jnp.zeros(o_ref.shape, F32)
        for bi in range(c_all.shape[0]):
            acc = acc + jnp.transpose(ca[bi:bi + 1, :]) * dm[bi:bi + 1, :]
        o_ref[...] = acc

    return pl.pallas_call(
        body, name="ada_bwd", out_shape=jax.ShapeDtypeStruct((c_all.shape[1], dmod.shape[1]), F32),
        compiler_params=_cparams(),
    )(c_all, dmod)


COL_SHARDED = ("w_in", "w_uq", "w_ukv", "w_up")
ROW_SHARDED = ("w_proj_rnn", "w_proj_mla", "w_out", "w_down")
REPLICATED = ("b_ada", "norm1_g", "conv_b", "w_gate_a", "b_gate_a", "w_gate_x", "b_gate_x", "lru_param", "q_norm_g",
              "kv_norm_g", "norm2_g", "ffn_conv_b", "final_g")
WEIGHTS = ("w_ada", "b_ada", "norm1_g", "w_in", "conv_w", "conv_b", "w_gate_a", "b_gate_a", "w_gate_x", "b_gate_x",
           "lru_param", "q_norm_g", "w_uq", "kv_norm_g", "w_ukv", "w_proj_rnn", "w_proj_mla", "w_out", "norm2_g", "w_up",
           "ffn_conv_w", "ffn_conv_b", "w_down", "final_g")
TRANSPOSED_GRADS = COL_SHARDED
PACK_LANES = 128


def _pack(vecs, row_multiple=SUBLANES):
    flat = jnp.concatenate([v.reshape(-1).astype(F32) for v in vecs])
    pad = (-flat.shape[0]) % (PACK_LANES * row_multiple)
    return jnp.concatenate([flat, jnp.zeros((pad,), F32)]).reshape(-1, PACK_LANES)


def _unpack(packed, shapes):
    flat = packed.reshape(-1)
    out, off = [], 0
    for shp in shapes:
        size = math.prod(shp)
        out.append(flat[off:off + size].reshape(shp))
        off += size
    return out


def kernel(x, c, positions, w_ada, b_ada, norm1_g, w_in, conv_w, conv_b, w_gate_a, b_gate_a, w_gate_x, b_gate_x, lru_param, q_norm_g, w_uq, kv_norm_g, w_ukv, w_proj_rnn, w_proj_mla, w_out, norm2_g, w_up, ffn_conv_w, ffn_conv_b, w_down, final_g, loss_target, m_w_ada, m_b_ada, m_norm1_g, m_w_in, m_conv_w, m_conv_b, m_w_gate_a, m_b_gate_a, m_w_gate_x, m_b_gate_x, m_lru_param, m_q_norm_g, m_w_uq, m_kv_norm_g, m_w_ukv, m_w_proj_rnn, m_w_proj_mla, m_w_out, m_norm2_g, m_w_up, m_ffn_conv_w, m_ffn_conv_b, m_w_down, m_final_g, v_w_ada, v_b_ada, v_norm1_g, v_w_in, v_conv_w, v_conv_b, v_w_gate_a, v_b_gate_a, v_w_gate_x, v_b_gate_x, v_lru_param, v_q_norm_g, v_w_uq, v_kv_norm_g, v_w_ukv, v_w_proj_rnn, v_w_proj_mla, v_w_out, v_norm2_g, v_w_up, v_ffn_conv_w, v_ffn_conv_b, v_w_down, v_final_g):
    args = dict(locals())
    w = {n: args[n] for n in WEIGHTS}
    m = {n: args["m_" + n] for n in WEIGHTS}
    v = {n: args["v_" + n] for n in WEIGHTS}
    s, d = x.shape[1], x.shape[2]
    me = _my_slot()
    def two_d(a):
        assert a.ndim == 3 and a.shape[0] == 1, a.shape
        return a[0]

    big = COL_SHARDED + ROW_SHARDED
    shard = {n: two_d(w[n]).astype(BF16) for n in big}

    def whole(n, g):
        k, r, cc = g.shape
        return jnp.transpose(g, (1, 0, 2)).reshape(r, k * cc) if n in COL_SHARDED else g.reshape(k * r, cc)

    first = _all_gather([shard["w_in"], c, two_d(conv_w), two_d(ffn_conv_w)], "gather_first")
    c_all = first[1].reshape(N_DEV, d)
    conv_w_all = jnp.transpose(first[2], (1, 0, 2)).reshape(conv_w.shape[1], -1)
    ffn_conv_w_all = jnp.transpose(first[3], (1, 0, 2)).reshape(ffn_conv_w.shape[1], -1)

    ada_cols = w_ada.shape[2]
    b_cols = lax.dynamic_slice(b_ada, (0, me * ada_cols), (1, ada_cols))
    mod_cols = _ada_fwd(c_all, w_ada[0], b_cols)
    mod_all, = _all_gather([mod_cols], "gather_mod")

    later = ("w_uq", "w_ukv", "w_proj_rnn", "w_proj_mla", "w_out", "w_up", "w_down")
    flights, started = _gather_start([shard[n] for n in later], mod_all, "gather_start")
    flight = dict(zip(later, flights))

    def fetch(names, after):
        lands = _gather_wait([flight[n] for n in names], after, "gather_wait_" + names[0])
        return {n: whole(n, lax.dynamic_update_index_in_dim(g, shard[n], me, 0)) for n, g in zip(names, lands)}

    mod = lax.dynamic_index_in_dim(mod_all, me, axis=1, keepdims=False).reshape(6, d) + started[0, 0]

    sm = {n: w[n][0] for n in REPLICATED if n not in ("b_ada", "final_g")}
    sm["final_g"] = final_g
    sm["conv_w"] = conv_w_all
    sm["ffn_conv_w"] = ffn_conv_w_all
    in_flight, windows = {}, {}

    def emit(n, g):
        rows = g.shape[0] // N_DEV
        if rows % ROW_ALIGN == 0:
            windows[n] = (None, None)
            g = g.reshape(N_DEV, rows, g.shape[1])
        else:
            span = max(rows * k - _span_start(k, rows) for k in range(N_DEV)) + rows
            windows[n] = (rows, -(-span // ROW_ALIGN) * ROW_ALIGN)
            assert _span_start(N_DEV - 1, rows) + windows[n][1] <= g.shape[0], (n, g.shape)
        *in_flight[n], token = _scatter_start(g, "scatter_start_" + n, *windows[n])
        return token[0, 0]

    sq, grad_x, gs, dmod = _local_step(x[0], mod, positions[0], loss_target[0], whole("w_in", first[0]), fetch, sm, emit)

    small_names = [n for n in REPLICATED if n != "b_ada"] + ["conv_w", "ffn_conv_w"]
    small_shapes = [gs[n].shape for n in small_names] + [(6 * d,), (1,)]
    partial = _pack([gs[n] for n in small_names] + [dmod, sq.reshape(1)], N_DEV * SUBLANES)
    *small_flight, small_started = _scatter_start(partial.reshape(N_DEV, -1, PACK_LANES), "scatter_small_start")

    grads, deltas, new_m, new_v = {}, {}, {}, {}

    def update(n, parts, own=None):
        shp = w[n].shape
        lay = jnp.transpose if n in TRANSPOSED_GRADS else (lambda a: a)
        res = _adamw(parts, lay(two_d(w[n])), lay(two_d(m[n])), lay(two_d(v[n])), "adamw_" + n, own)
        grads[n], deltas[n], new_m[n], new_v[n] = [lay(a).reshape(shp) for a in res]

    for n in big:
        rows, span = windows[n]
        src, landed = _scatter_wait(*in_flight[n], small_started, "scatter_wait_" + n, rows, span)
        if rows is None:
            update(n, landed, lax.dynamic_index_in_dim(src, me, axis=0, keepdims=False))
        else:
            start = _span_start(me, rows)
            own = lax.dynamic_slice(src, (start, 0), (span, src.shape[1]))
            total = _sum_sources(landed, own, "sum_" + n)
            update(n, lax.dynamic_slice(total, (rows * me - start, 0), (rows, src.shape[1]))[None])

    chunks, landed = _scatter_wait(*small_flight, new_v[big[-1]], "scatter_small_wait")
    mine = _sum_sources(landed, lax.dynamic_index_in_dim(chunks, me, axis=0, keepdims=False), "sum_small")
    summed_all, dmod_all = _all_gather([mine, dmod.reshape(1, 6 * d)], "gather_small")
    summed = _unpack(summed_all, small_shapes)
    g_small = dict(zip(small_names, summed[:len(small_names)]))
    g_small["b_ada"] = summed[len(small_names)]
    loss = 0.5 * summed[-1][0] / d
    dmod_cols = lax.dynamic_slice(dmod_all.reshape(N_DEV, 6 * d), (0, me * ada_cols), (N_DEV, ada_cols))

    update("w_ada", _ada_bwd(c_all, dmod_cols)[None])

    for n in ("conv_w", "ffn_conv_w"):
        cols = w[n].shape[2]
        g_small[n] = lax.dynamic_slice(g_small[n], (0, me * cols), (g_small[n].shape[0], cols))
    small = REPLICATED + ("conv_w", "ffn_conv_w")
    as_rows = lambda a: a.reshape(1, -1) if a.ndim == 1 else a
    res = _adamw_many([as_rows(g_small[n].reshape(w[n].shape)) for n in small], [as_rows(w[n]) for n in small],
                      [as_rows(m[n]) for n in small], [as_rows(v[n]) for n in small], "adamw_small")
    for i, n in enumerate(small):
        grads[n], deltas[n], new_m[n], new_v[n] = [a.reshape(w[n].shape) for a in res[4 * i:4 * i + 4]]

    return (loss, grad_x[None], *[grads[n] for n in WEIGHTS], *[deltas[n] for n in WEIGHTS],
            *[new_m[n] for n in WEIGHTS], *[new_v[n] for n in WEIGHTS])
```

```python
import functools
import math

import jax
import jax.numpy as jnp
from jax import lax
from jax.experimental import pallas as pl
from jax.experimental.pallas import tpu as pltpu

F32 = jnp.float32
BF16 = jnp.bfloat16

N_DEV = 8
LANES = 128
SUBLANES = 8
VMEM_LIMIT = 56 * 1024 * 1024

D_RNN = 1280
Q_LORA = 384
KV_LORA = 256
QK_NOPE = 64
QK_ROPE = 32
V_HEAD = 64
N_HEADS = 16
D_FF = 2816
ROPE_THETA = 10000.0
LRU_C = 8.0
EPS = 1e-6
MLA_W = 768
ATT_SCALE = 1.0 / math.sqrt(QK_NOPE + QK_ROPE)

ADAM_LR, ADAM_B1, ADAM_B2, ADAM_EPS, ADAM_WD, ADAM_STEP = 0.001, 0.9, 0.999, 1e-08, 0.01, 10


def _cparams(sem=None):
    return pltpu.CompilerParams(dimension_semantics=sem, vmem_limit_bytes=VMEM_LIMIT)


def _pick(n, prefs):
    for p in prefs:
        if n % p == 0:
            return p
    return n


def _sigmoid(v):
    return 0.5 * jnp.tanh(0.5 * v) + 0.5


def _lane(shape):
    return lax.broadcasted_iota(jnp.int32, shape, len(shape) - 1)


def _row(shape):
    return lax.broadcasted_iota(jnp.int32, shape, len(shape) - 2)


MM_BLOCK_BYTES = 36 * 1024 * 1024


def _divisors(n):
    return [t for t in range(n, 0, -LANES) if n % t == 0] if n % LANES == 0 else [n]


HBM_BYTES_PER_US = 3.0e6
MXU_FLOPS_PER_US = 8.0e8
GRID_STEP_US = 0.35


def _mm_tiles(m, n, k, a_bytes, b_bytes, o_bytes):
    best = None
    for tm in [t for t in _divisors(m) if t <= 1024]:
        for tn in [t for t in _divisors(n) if t <= 2048]:
            for tk in _divisors(k):
                nk = k // tk
                need = 2 * (tm * tk * a_bytes + tk * tn * b_bytes + tm * tn * o_bytes) + (tm * tn * 4 if nk > 1 else 0)
                if need > MM_BLOCK_BYTES:
                    continue
                gi, gj = m // tm, n // tn
                for rows_outer in (True, False):
                    if nk > 1:
                        a_reads, b_reads = gj, gi
                    elif rows_outer:
                        a_reads, b_reads = 1, (gi if gj > 1 else 1)
                    else:
                        a_reads, b_reads = (gj if gi > 1 else 1), 1
                    traffic = m * k * a_bytes * a_reads + k * n * b_bytes * b_reads + m * n * (o_bytes + (8 * nk if nk > 1 else 0))
                    cost = max(traffic / HBM_BYTES_PER_US, 2.0 * m * n * k / MXU_FLOPS_PER_US) + gi * gj * nk * GRID_STEP_US
                    if best is None or cost < best[0]:
                        best = (cost, tm, tn, tk, rows_outer)
                break
    if best is None:
        raise ValueError((m, n, k))
    return best[1:]


def _mm(a, b, *, ta=False, tb=False, out_dtype=F32, also_t=None, name):
    (k_a, m) = a.shape if ta else a.shape[::-1]
    (n, k_b) = b.shape if tb else b.shape[::-1]
    assert k_a == k_b, (a.shape, b.shape, ta, tb)
    k = k_a
    tm, tn, tk, rows_outer = _mm_tiles(m, n, k, a.dtype.itemsize, b.dtype.itemsize, jnp.dtype(out_dtype).itemsize)
    nk = k // tk
    dims = (((0 if ta else 1,), (1 if tb else 0,)), ((), ()))
    n_out = 1 if also_t is None else 2

    def body(a_ref, b_ref, *rest):
        outs, acc = rest[:n_out], rest[n_out:]
        part = lax.dot_general(a_ref[...].astype(BF16), b_ref[...].astype(BF16), dims, preferred_element_type=F32)

        def write(val):
            outs[0][...] = val.astype(out_dtype)
            if also_t is not None:
                outs[1][...] = jnp.transpose(val).astype(also_t)

        if nk == 1:
            write(part)
            return
        acc_ref, = acc
        kk = pl.program_id(2)

        @pl.when(kk == 0)
        def _():
            acc_ref[...] = part

        @pl.when(kk > 0)
        def _():
            acc_ref[...] += part

        @pl.when(kk == nk - 1)
        def _():
            write(acc_ref[...])

    def ij(f):
        return (lambda i, j, kk: f(i, j, kk)) if rows_outer else (lambda j, i, kk: f(i, j, kk))

    a_spec = pl.BlockSpec((tk, tm), ij(lambda i, j, kk: (kk, i))) if ta else pl.BlockSpec((tm, tk), ij(lambda i, j, kk: (i, kk)))
    b_spec = pl.BlockSpec((tn, tk), ij(lambda i, j, kk: (j, kk))) if tb else pl.BlockSpec((tk, tn), ij(lambda i, j, kk: (kk, j)))
    out_specs = [pl.BlockSpec((tm, tn), ij(lambda i, j, kk: (i, j)))]
    out_shape = [jax.ShapeDtypeStruct((m, n), out_dtype)]
    if also_t is not None:
        out_specs.append(pl.BlockSpec((tn, tm), ij(lambda i, j, kk: (j, i))))
        out_shape.append(jax.ShapeDtypeStruct((n, m), also_t))
    res = pl.pallas_call(
        body, name=name,
        grid=(m // tm, n // tn, nk) if rows_outer else (n // tn, m // tm, nk),
        in_specs=[a_spec, b_spec], out_specs=out_specs, out_shape=out_shape,
        scratch_shapes=[] if nk == 1 else [pltpu.VMEM((tm, tn), F32)],
        compiler_params=_cparams(("arbitrary", "arbitrary", "arbitrary")),
    )(a, b)
    return res[0] if also_t is None else res


ROW_BLOCK_BYTES = 28 * 1024 * 1024


def _rowwise(fn, row_ins, par_ins, out_defs, red_defs, *, name):
    s = row_ins[0].shape[0]
    row_bytes = sum(a.shape[1] * a.dtype.itemsize for a in row_ins) + sum(c * jnp.dtype(dt).itemsize for c, dt in out_defs)
    tr = next((t for t in (512, 256, 128) if s % t == 0 and 2 * t * row_bytes <= ROW_BLOCK_BYTES), min(s, 128))
    nr, npar, no = len(row_ins), len(par_ins), len(out_defs)

    def body(*refs):
        rin, pin = refs[:nr], refs[nr:nr + npar]
        outs, reds = refs[nr + npar:nr + npar + no], refs[nr + npar + no:]
        i = pl.program_id(0)

        @pl.when(i == 0)
        def _():
            for r in reds:
                r[...] = jnp.zeros_like(r)

        fn(i, rin, pin, outs, reds)

    in_specs = [pl.BlockSpec((tr, a.shape[1]), lambda i: (i, 0)) for a in row_ins]
    in_specs += [pl.BlockSpec(a.shape, lambda i, nd=a.ndim: (0,) * nd) for a in par_ins]
    out_specs = [pl.BlockSpec((tr, c), lambda i: (i, 0)) for c, _ in out_defs]
    out_specs += [pl.BlockSpec(shp, lambda i: (0, 0)) for shp in red_defs]
    out_shape = [jax.ShapeDtypeStruct((s, c), dt) for c, dt in out_defs]
    out_shape += [jax.ShapeDtypeStruct(shp, F32) for shp in red_defs]
    return pl.pallas_call(
        body, name=name, grid=(s // tr,), in_specs=in_specs, out_specs=out_specs, out_shape=out_shape,
        compiler_params=_cparams(("arbitrary",)),
    )(*row_ins, *par_ins)


def _rms(v):
    return lax.rsqrt(jnp.mean(v * v, axis=-1, keepdims=True) + EPS)


def _colsum(v):
    return jnp.sum(v, axis=0, keepdims=True)


def _rms_bwd(dn, n, rstd):
    return rstd * (dn - n * jnp.mean(dn * n, axis=-1, keepdims=True))


def _norm_mod_fwd(x, gmod, name):
    def fn(i, rin, pin, outs, reds):
        xv = rin[0][...]
        p = pin[0][...]
        n = xv * _rms(xv)
        outs[0][...] = ((n * p[0:1]) * (1.0 + p[1:2]) + p[2:3]).astype(BF16)

    return _rowwise(fn, [x], [gmod], [(x.shape[1], BF16)], [], name=name)[0]


def _rope(v, rot_c, rot_s):
    half = QK_ROPE // 2
    swapped = jnp.where(_lane(v.shape) < QK_NOPE + half, pltpu.roll(v, LANES - half, 1), pltpu.roll(v, half, 1))
    return v * rot_c + swapped * rot_s


def _rope_t(dv, rot_c, rot_s):
    half = QK_ROPE // 2
    ds = dv * rot_s
    lane = _lane(dv.shape)
    swapped = jnp.where(lane < QK_NOPE + half, pltpu.roll(ds, LANES - half, 1), pltpu.roll(ds, half, 1))
    in_rope = (lane >= QK_NOPE) & (lane < QK_NOPE + QK_ROPE)
    return dv * rot_c + jnp.where(in_rope, swapped, 0.0)


def _mla_prep_fwd(proj_mla, rot_c, rot_s, ng):
    o1, o2 = Q_LORA, Q_LORA + KV_LORA

    def fn(i, rin, pin, outs, reds):
        g = pin[0][...]
        ql = rin[0][:, 0:o1]
        kl = rin[0][:, o1:o2]
        outs[0][...] = (ql * _rms(ql) * g[0:1, 0:o1]).astype(BF16)
        outs[1][...] = (kl * _rms(kl) * g[0:1, o1:o2]).astype(BF16)
        kr = pltpu.roll(rin[0][:, o2:o2 + LANES], QK_NOPE, 1)
        outs[2][...] = _rope(kr, rin[1][...], rin[2][...]).astype(BF16)

    return _rowwise(fn, [proj_mla, rot_c, rot_s], [ng], [(Q_LORA, BF16), (KV_LORA, BF16), (LANES, BF16)], [],
                    name="mla_prep_fwd")


def _mla_prep_bwd(proj_mla, dqn, dkvn, dkr, rot_c, rot_s, ng):
    o1, o2 = Q_LORA, Q_LORA + KV_LORA

    def fn(i, rin, pin, outs, reds):
        g = pin[0][...]
        ql = rin[0][:, 0:o1]
        kl = rin[0][:, o1:o2]
        rq, rk = _rms(ql), _rms(kl)
        nq, nk = ql * rq, kl * rk
        dq, dk = rin[1][...], rin[2][...]
        outs[0][:, 0:o1] = _rms_bwd(dq * g[0:1, 0:o1], nq, rq).astype(BF16)
        outs[0][:, o1:o2] = _rms_bwd(dk * g[0:1, o1:o2], nk, rk).astype(BF16)
        dkr_pre = pltpu.roll(_rope_t(rin[3][...], rin[4][...], rin[5][...]), LANES - QK_NOPE, 1)
        outs[0][:, o2:] = jnp.where(_lane(dkr_pre.shape) < QK_ROPE, dkr_pre, 0.0).astype(BF16)
        reds[0][0:1, 0:o1] += _colsum(dq * nq)
        reds[0][0:1, o1:o2] += _colsum(dk * nk)

    return _rowwise(fn, [proj_mla, dqn, dkvn, dkr, rot_c, rot_s], [ng], [(MLA_W, BF16)], [(SUBLANES, MLA_W)],
                    name="mla_prep_bwd")


def _rope_bwd(dq, rot_c, rot_s):
    def fn(i, rin, pin, outs, reds):
        c, sn = rin[1][...] * Q_PRESCALE, rin[2][...] * Q_PRESCALE
        for h in range(N_HEADS):
            sl = slice(h * LANES, (h + 1) * LANES)
            outs[0][:, sl] = _rope_t(rin[0][:, sl], c, sn).astype(BF16)

    return _rowwise(fn, [dq, rot_c, rot_s], [], [(dq.shape[1], BF16)], [], name="rope_bwd")[0]


def _rope_fwd_t(q, rot_c, rot_s):
    s, c = q.shape
    tr = min(256, s)

    def body(q_ref, c_ref, s_ref, o_ref, ot_ref):
        cc, sn = c_ref[...] * Q_PRESCALE, s_ref[...] * Q_PRESCALE
        for h in range(N_HEADS):
            sl = slice(h * LANES, (h + 1) * LANES)
            rot = _rope(q_ref[:, sl], cc, sn)
            o_ref[:, sl] = rot.astype(BF16)
            ot_ref[sl, :] = jnp.transpose(rot).astype(BF16)

    return pl.pallas_call(
        body, name="rope_fwd", grid=(s // tr,),
        in_specs=[pl.BlockSpec((tr, c), lambda i: (i, 0)), pl.BlockSpec((tr, LANES), lambda i: (i, 0)),
                  pl.BlockSpec((tr, LANES), lambda i: (i, 0))],
        out_specs=[pl.BlockSpec((tr, c), lambda i: (i, 0)), pl.BlockSpec((c, tr), lambda i: (0, i))],
        out_shape=[jax.ShapeDtypeStruct((s, c), BF16), jax.ShapeDtypeStruct((c, s), BF16)],
        compiler_params=_cparams(("arbitrary",)),
    )(q, rot_c, rot_s)


def _merge_fwd(pr, pm, proj_g):
    d = pr.shape[1]

    def fn(i, rin, pin, outs, reds):
        g_rnn, g_mla = rin[2][:, 0:d].astype(F32), rin[2][:, d:].astype(F32)
        outs[0][...] = (_sigmoid(g_rnn) * rin[0][...].astype(F32) + _sigmoid(g_mla) * rin[1][...].astype(F32)).astype(BF16)

    return _rowwise(fn, [pr, pm, proj_g], [], [(d, BF16)], [], name="merge_fwd")[0]


def _merge_bwd(dmerged, pr, pm, proj_g):
    d = pr.shape[1]

    def fn(i, rin, pin, outs, reds):
        dm = rin[0][...]
        sr, sm = _sigmoid(rin[3][:, 0:d].astype(F32)), _sigmoid(rin[3][:, d:].astype(F32))
        outs[0][...] = (dm * sr).astype(BF16)
        outs[1][...] = (dm * sm).astype(BF16)
        outs[2][:, 0:d] = (dm * rin[1][...].astype(F32) * sr * (1.0 - sr)).astype(BF16)
        outs[2][:, d:] = (dm * rin[2][...].astype(F32) * sm * (1.0 - sm)).astype(BF16)

    return _rowwise(fn, [dmerged, pr, pm, proj_g], [], [(d, BF16), (d, BF16), (2 * d, BF16)], [], name="merge_bwd")


def _resid_norm_fwd(x, o, gmod):
    d = x.shape[1]

    def fn(i, rin, pin, outs, reds):
        p = pin[0][...]
        x1 = rin[0][...] + p[3:4] * rin[1][...]
        outs[0][...] = x1
        outs[1][...] = ((x1 * _rms(x1) * p[0:1]) * (1.0 + p[1:2]) + p[2:3]).astype(BF16)

    return _rowwise(fn, [x, o], [gmod], [(d, F32), (d, BF16)], [], name="resid_norm_fwd")


def _final_fwd_bwd(x1, dn, target, par):
    d = x1.shape[1]

    def fn(i, rin, pin, outs, reds):
        p = pin[0][...]
        dnv = rin[1][...]
        x2 = rin[0][...] + p[0:1] * dnv
        rstd = _rms(x2)
        n3 = x2 * rstd
        err = n3 * p[1:2] - rin[2][...]
        dy = err * (1.0 / d)
        dx2 = _rms_bwd(dy * p[1:2], n3, rstd)
        outs[0][...] = dx2
        outs[1][...] = (dx2 * p[0:1]).astype(BF16)
        reds[0][0:1, :] += _colsum(dy * n3)
        reds[0][1:2, :] += _colsum(dx2 * dnv)
        reds[0][2:3, :] += jnp.zeros((1, d), F32) + jnp.sum(err * err)

    return _rowwise(fn, [x1, dn, target], [par], [(d, F32), (d, BF16)], [(SUBLANES, d)], name="final_fwd_bwd")


def _norm2_bwd(x1, dh2, dx2, o, gmod):
    d = x1.shape[1]

    def fn(i, rin, pin, outs, reds):
        p = pin[0][...]
        x1v, dh = rin[0][...], rin[1][...]
        rstd = _rms(x1v)
        n2 = x1v * rstd
        dx1 = rin[2][...] + _rms_bwd(dh * (p[0:1] * (1.0 + p[1:2])), n2, rstd)
        outs[0][...] = dx1
        outs[1][...] = (dx1 * p[3:4]).astype(BF16)
        reds[0][0:1, :] += _colsum(dh * n2 * (1.0 + p[1:2]))
        reds[0][1:2, :] += _colsum(dh * n2 * p[0:1])
        reds[0][2:3, :] += _colsum(dh)
        reds[0][3:4, :] += _colsum(dx1 * rin[3][...])

    return _rowwise(fn, [x1, dh2, dx2, o], [gmod], [(d, F32), (d, BF16)], [(SUBLANES, d)], name="norm2_bwd")


def _norm1_bwd(x, dh_a, dh_b, dh_c, dx1, gmod):
    d = x.shape[1]

    def fn(i, rin, pin, outs, reds):
        p = pin[0][...]
        xv = rin[0][...]
        dh = rin[1][...] + rin[2][...] + rin[3][...]
        rstd = _rms(xv)
        n1 = xv * rstd
        outs[0][...] = rin[4][...] + _rms_bwd(dh * (p[0:1] * (1.0 + p[1:2])), n1, rstd)
        reds[0][0:1, :] += _colsum(dh * n1 * (1.0 + p[1:2]))
        reds[0][1:2, :] += _colsum(dh * n1 * p[0:1])
        reds[0][2:3, :] += _colsum(dh)

    return _rowwise(fn, [x, dh_a, dh_b, dh_c, dx1], [gmod], [(d, F32)], [(SUBLANES, d)], name="norm1_bwd")


RNN_CHUNK = 512


def _shift_down(ref, base, n, j):
    v = ref[pl.ds(base, n + SUBLANES), :]
    return v[SUBLANES:] if j == 0 else pltpu.roll(v, j, 0)[SUBLANES:]


def _shift_up(ref, base, n, j, top_pad):
    v = ref[pl.ds(base + top_pad, n + SUBLANES), :]
    return v[:n] if j == 0 else pltpu.roll(v, n + SUBLANES - j, 0)[:n]


SCAN_GROUP = 128


def _scan_sizes(s):
    sizes = [s]
    while sizes[-1] > SUBLANES:
        assert sizes[-1] % SUBLANES == 0, s
        sizes.append(sizes[-1] // SUBLANES)
    return sizes


def _scan_scratch(s):
    return [pltpu.VMEM((n + 2 * SUBLANES, LANES), F32) for n in _scan_sizes(s)[1:] for _ in range(2)]


def _linear_scan(a_ref, b_ref, out_ref, a_off, s, reverse, levels):
    sizes = _scan_sizes(s)
    lv = [(a_ref, b_ref, a_off, 0)] + [(levels[2 * i], levels[2 * i + 1], 0, SUBLANES) for i in range(len(sizes) - 1)]
    zero8 = jnp.zeros((SUBLANES, LANES), F32)
    for (ar, br, _, _), n in zip(lv[1:], sizes[1:]):
        br[0:SUBLANES, :] = zero8
        br[pl.ds(n + SUBLANES, SUBLANES), :] = zero8
    order = list(range(SUBLANES - 1, -1, -1)) if reverse else list(range(SUBLANES))

    for lvl in range(len(sizes) - 1):
        ar, br, aoff, off = lv[lvl]
        m = sizes[lvl + 1]
        g = min(m, SCAN_GROUP)
        for t0 in range(0, m, g):
            acc_a = acc_b = None
            for r in order:
                sa = pl.ds(off + SUBLANES * t0 + r + aoff, g, stride=SUBLANES)
                sb = pl.ds(off + SUBLANES * t0 + r, g, stride=SUBLANES)
                a, b = ar[sa, :], br[sb, :]
                if acc_a is None:
                    acc_a, acc_b = a, b
                else:
                    acc_b = a * acc_b + b
                    acc_a = a * acc_a
            lv[lvl + 1][0][pl.ds(SUBLANES + t0, g), :] = acc_a
            lv[lvl + 1][1][pl.ds(SUBLANES + t0, g), :] = acc_b

    ar, br, _, off = lv[-1]
    n = sizes[-1]
    a, b = ar[pl.ds(off, n), :], br[pl.ds(off, n), :]
    h, rows = jnp.zeros((1, LANES), F32), [None] * n
    for j in (range(n - 1, -1, -1) if reverse else range(n)):
        h = a[j:j + 1, :] * h + b[j:j + 1, :]
        rows[j] = h
    br[pl.ds(off, n), :] = jnp.concatenate(rows, axis=0)

    for lvl in range(len(sizes) - 2, -1, -1):
        ar, br, aoff, off = lv[lvl]
        m = sizes[lvl + 1]
        up = lv[lvl + 1][1]
        dst = out_ref if lvl == 0 else br
        g = min(m, SCAN_GROUP)
        for t0 in range(0, m, g):
            h = _shift_up(up, t0, g, 1, SUBLANES) if reverse else _shift_down(up, t0, g, 1)
            for r in order:
                sa = pl.ds(off + SUBLANES * t0 + r + aoff, g, stride=SUBLANES)
                sb = pl.ds(off + SUBLANES * t0 + r, g, stride=SUBLANES)
                h = ar[sa, :] * h + br[sb, :]
                dst[sb, :] = h


def _one_minus_exp(z):
    series = -z * (1.0 + z * (0.5 + z * (1.0 / 6.0 + z * (1.0 / 24.0 + z * (1.0 / 120.0 + z * (1.0 / 720.0))))))
    return jnp.where(z > -0.1, series, 1.0 - jnp.exp(z))


def _softplus(v):
    return jnp.maximum(v, 0.0) + jnp.log(1.0 + jnp.exp(-jnp.abs(v)))


def _rnn_gates(xc, w, wa, wx, sp):
    xb = xc.astype(BF16)
    ra = _sigmoid(jnp.dot(xb, wa, preferred_element_type=F32) + w[5:6])
    ix = _sigmoid(jnp.dot(xb, wx, preferred_element_type=F32) + w[6:7])
    la = (-LRU_C) * ra * sp
    a = jnp.exp(la)
    mult = jnp.sqrt(_one_minus_exp(2.0 * la))
    return ra, ix, a, mult


def _rnn_fwd(x_rnn, keep, rp, wa_bd, wx_bd):
    s, r = x_rnn.shape
    ts = min(RNN_CHUNK, s)

    def body(x_ref, keep_ref, rp_ref, wa_ref, wx_ref, xc_ref, ra_ref, ix_ref, hs_ref, xpad, a_s, b_s, *levels):
        xpad[0:SUBLANES, :] = jnp.zeros((SUBLANES, LANES), F32)
        xpad[SUBLANES:, :] = x_ref[...]
        w = rp_ref[...]
        sp = _softplus(-w[7:8])
        wa, wx = wa_ref[0], wx_ref[0]

        def chunk(c, carry):
            base = pl.multiple_of(c * ts, ts)
            xc = w[4:5] + w[3:4] * _shift_down(xpad, base, ts, 0)
            for j in range(1, 4):
                xc = xc + w[3 - j:4 - j] * _shift_down(xpad, base, ts, j)
            ra, ix, a, mult = _rnn_gates(xc, w, wa, wx, sp)
            kp = keep_ref[pl.ds(base, ts), :]
            xc_ref[pl.ds(base, ts), :] = xc
            ra_ref[pl.ds(base, ts), :] = ra
            ix_ref[pl.ds(base, ts), :] = ix
            a_s[pl.ds(base, ts), :] = a * kp
            b_s[pl.ds(base, ts), :] = jnp.where(kp > 0.0, mult, 1.0) * (ix * xc)
            return carry

        lax.fori_loop(0, s // ts, chunk, 0)

        _linear_scan(a_s, b_s, hs_ref, 0, s, False, levels)

    col = pl.BlockSpec((s, LANES), lambda g: (0, g))
    return pl.pallas_call(
        body, name="rnn_fwd", grid=(r // LANES,),
        in_specs=[col, pl.BlockSpec((s, 1), lambda g: (0, 0)), pl.BlockSpec((SUBLANES, LANES), lambda g: (0, g)),
                  pl.BlockSpec((1, LANES, LANES), lambda g: (g, 0, 0)), pl.BlockSpec((1, LANES, LANES), lambda g: (g, 0, 0))],
        out_specs=[col] * 4,
        out_shape=[jax.ShapeDtypeStruct((s, r), F32)] * 4,
        scratch_shapes=[pltpu.VMEM((s + SUBLANES, LANES), F32), pltpu.VMEM((s, LANES), F32), pltpu.VMEM((s, LANES), F32),
                        *_scan_scratch(s)],
        compiler_params=_cparams(("arbitrary",)),
    )(x_rnn, keep, rp, wa_bd, wx_bd)


def _rnn_bwd(x_rnn, xc, ra, ix, hs, dy, keep, rp, wa_bd, wx_bd):
    s, r = x_rnn.shape
    ts = min(RNN_CHUNK, s)

    def body(x_ref, xc_ref, ra_ref, ix_ref, hs_ref, dy_ref, keep_ref, rp_ref, wa_ref, wx_ref,
             dx_ref, dwa_ref, dwx_ref, red_ref, xpad, hpad, a_s, dh_s, dxc_s, *levels):
        zero8 = jnp.zeros((SUBLANES, LANES), F32)
        xpad[0:SUBLANES, :] = zero8
        xpad[SUBLANES:, :] = x_ref[...]
        hpad[0:SUBLANES, :] = zero8
        hpad[SUBLANES:, :] = hs_ref[...]
        a_s[s:, :] = zero8
        dxc_s[s:, :] = zero8
        w = rp_ref[...]
        sp = _softplus(-w[7:8])
        wa, wx = wa_ref[0], wx_ref[0]

        def decay(c, carry):
            base = pl.multiple_of(c * ts, ts)
            a = jnp.exp((-LRU_C) * ra_ref[pl.ds(base, ts), :] * sp)
            a_s[pl.ds(base, ts), :] = a * keep_ref[pl.ds(base, ts), :]
            return carry

        lax.fori_loop(0, s // ts, decay, 0)

        _linear_scan(a_s, dy_ref, dh_s, 1, s, True, levels)

        def gates(c, carry):
            dwa, dwx, d_ba, d_bx, d_sp, d_cb = carry
            base = pl.multiple_of(c * ts, ts)
            xcv = xc_ref[pl.ds(base, ts), :]
            rav = ra_ref[pl.ds(base, ts), :]
            ixv = ix_ref[pl.ds(base, ts), :]
            kp = keep_ref[pl.ds(base, ts), :]
            dh = dh_s[pl.ds(base, ts), :]
            h_prev = _shift_down(hpad, base, ts, 1)
            la = (-LRU_C) * rav * sp
            a = jnp.exp(la)
            mult = jnp.sqrt(_one_minus_exp(2.0 * la))
            mult_eff = jnp.where(kp > 0.0, mult, 1.0)
            d_a = dh * h_prev * kp
            d_mult = dh * (ixv * xcv) * kp
            d_ix = dh * mult_eff * xcv
            d_xc = dh * mult_eff * ixv
            d_la = d_a * a - d_mult * (a * a) / mult
            d_pa = d_la * ((-LRU_C) * sp) * rav * (1.0 - rav)
            d_px = d_ix * ixv * (1.0 - ixv)
            xb = xcv.astype(BF16)
            pab, pxb = d_pa.astype(BF16), d_px.astype(BF16)
            tn = (((0,), (0,)), ((), ()))
            nt_ = (((1,), (1,)), ((), ()))
            dwa = dwa + lax.dot_general(xb, pab, tn, preferred_element_type=F32)
            dwx = dwx + lax.dot_general(xb, pxb, tn, preferred_element_type=F32)
            d_xc = d_xc + lax.dot_general(pab, wa, nt_, preferred_element_type=F32)
            d_xc = d_xc + lax.dot_general(pxb, wx, nt_, preferred_element_type=F32)
            dxc_s[pl.ds(base, ts), :] = d_xc
            return (dwa, dwx, d_ba + _colsum(d_pa), d_bx + _colsum(d_px),
                    d_sp + _colsum(d_la * ((-LRU_C) * rav)), d_cb + _colsum(d_xc))

        z1 = jnp.zeros((1, LANES), F32)
        zw = jnp.zeros((LANES, LANES), F32)
        dwa, dwx, d_ba, d_bx, d_sp, d_cb = lax.fori_loop(0, s // ts, gates, (zw, zw, z1, z1, z1, z1))
        dwa_ref[0] = dwa
        dwx_ref[0] = dwx

        def conv(c, carry):
            base = pl.multiple_of(c * ts, ts)
            d_here = dxc_s[pl.ds(base, ts), :]
            dx = w[3:4] * d_here
            for j in range(1, 4):
                dx = dx + w[3 - j:4 - j] * _shift_up(dxc_s, base, ts, j, 0)
            dx_ref[pl.ds(base, ts), :] = dx.astype(BF16)
            return tuple(carry[k] + _colsum(d_here * _shift_down(xpad, base, ts, 3 - k)) for k in range(4))

        d_w = lax.fori_loop(0, s // ts, conv, (z1, z1, z1, z1))
        d_lru = d_sp * (-_sigmoid(-w[7:8]))
        red_ref[...] = jnp.concatenate(list(d_w) + [d_cb, d_ba, d_bx, d_lru], axis=0)

    col = pl.BlockSpec((s, LANES), lambda g: (0, g))
    par = pl.BlockSpec((SUBLANES, LANES), lambda g: (0, g))
    wsp = pl.BlockSpec((1, LANES, LANES), lambda g: (g, 0, 0))
    return pl.pallas_call(
        body, name="rnn_bwd", grid=(r // LANES,),
        in_specs=[col] * 6 + [pl.BlockSpec((s, 1), lambda g: (0, 0)), par, wsp, wsp],
        out_specs=[col, wsp, wsp, par],
        out_shape=[jax.ShapeDtypeStruct((s, r), BF16), jax.ShapeDtypeStruct((r // LANES, LANES, LANES), F32),
                   jax.ShapeDtypeStruct((r // LANES, LANES, LANES), F32), jax.ShapeDtypeStruct((SUBLANES, r), F32)],
        scratch_shapes=[pltpu.VMEM((s + SUBLANES, LANES), F32), pltpu.VMEM((s + SUBLANES, LANES), F32),
                        pltpu.VMEM((s + SUBLANES, LANES), F32), pltpu.VMEM((s, LANES), F32),
                        pltpu.VMEM((s + SUBLANES, LANES), F32), *_scan_scratch(s)],
        compiler_params=_cparams(("arbitrary",)),
    )(x_rnn, xc, ra, ix, hs, dy, keep, rp, wa_bd, wx_bd)


ATT_BLOCK = 512
ATT_FWD_HEADS = 8
ATT_BWD_HEADS = 4


LOG2E = 1.4426950408889634
LN2 = 0.6931471805599453
Q_PRESCALE = ATT_SCALE * LOG2E


def _att_scores(q, kvt, krt, diagonal):
    kt_eff = jnp.where(_row(kvt.shape) < QK_NOPE, kvt, krt)
    sc = jnp.dot(q, kt_eff, preferred_element_type=F32)
    if diagonal:
        sc = jnp.where(lax.broadcasted_iota(jnp.int32, sc.shape, 1) <= lax.broadcasted_iota(jnp.int32, sc.shape, 0), sc, -jnp.inf)
    return sc


def _att_fwd(q, kv, kvt, krt):
    s = q.shape[0]
    t = min(ATT_BLOCK, s)
    nb = s // t
    hp = ATT_FWD_HEADS

    pairs = [(i, j) for i in range(nb) for j in range(i + 1)]
    i_tab = jnp.array([p[0] for p in pairs], jnp.int32)
    j_tab = jnp.array([p[1] for p in pairs], jnp.int32)

    def body(i_ref, j_ref, q_ref, kv_ref, kvt_ref, krt_ref, y_ref, lse_ref, m_s, acc_s):
        i, j = i_ref[pl.program_id(1)], j_ref[pl.program_id(1)]

        @pl.when(j == 0)
        def _():
            m_s[...] = jnp.full(m_s.shape, -jnp.inf, F32)
            acc_s[...] = jnp.zeros(acc_s.shape, F32)

        def step(diagonal):
            krt_b = krt_ref[...]
            lane = _lane((t, LANES))
            groups = [slice(c * LANES, (c + 1) * LANES) for c in range(t // LANES)]
            heads = [slice(hh * LANES, (hh + 1) * LANES) for hh in range(hp)]
            scs = [_att_scores(q_ref[:, sl], kvt_ref[sl, :], krt_b, diagonal) for sl in heads]
            stats = []
            for hh in range(hp):
                m_prev = m_s[hh]
                m_blk = scs[hh][:, groups[0]]
                for g in groups[1:]:
                    m_blk = jnp.maximum(m_blk, scs[hh][:, g])
                stats.append((m_prev, jnp.maximum(m_prev, jnp.max(m_blk, axis=-1, keepdims=True))))
            for hh in range(hp):
                m_prev, m_new = stats[hh]
                kvb = kv_ref[:, heads[hh]]
                ones_v = jnp.where(lane < QK_NOPE, jnp.ones_like(kvb), kvb)
                p = jnp.concatenate([jnp.exp2(scs[hh][:, g] - m_new).astype(BF16) for g in groups], axis=1)
                acc_s[hh] = jnp.exp2(m_prev - m_new) * acc_s[hh] + jnp.dot(p, ones_v, preferred_element_type=F32)
                m_s[hh] = m_new

        @pl.when(j < i)
        def _():
            step(False)

        @pl.when(j == i)
        def _():
            step(True)
            lane = _lane((t, LANES))
            for g in range(hp // 2):
                sl = slice(g * LANES, (g + 1) * LANES)
                a0, a1 = acc_s[2 * g], acc_s[2 * g + 1]
                l0, l1 = a0[:, 0:1], a1[:, 0:1]
                y_ref[:, sl] = jnp.where(lane < V_HEAD, pltpu.roll(a0 / l0, V_HEAD, 1), a1 / l1).astype(BF16)
                lse_ref[:, sl] = jnp.where(lane < V_HEAD, m_s[2 * g] + jnp.log(l0) * LOG2E, m_s[2 * g + 1] + jnp.log(l1) * LOG2E)

    return pl.pallas_call(
        body, name="att_fwd",
        grid_spec=pltpu.PrefetchScalarGridSpec(
            num_scalar_prefetch=2, grid=(N_HEADS // hp, len(pairs)),
            in_specs=[pl.BlockSpec((t, hp * LANES), lambda p, n, it, jt: (it[n], p)),
                      pl.BlockSpec((t, hp * LANES), lambda p, n, it, jt: (jt[n], p)),
                      pl.BlockSpec((hp * LANES, t), lambda p, n, it, jt: (p, jt[n])),
                      pl.BlockSpec((LANES, t), lambda p, n, it, jt: (0, jt[n]))],
            out_specs=[pl.BlockSpec((t, hp // 2 * LANES), lambda p, n, it, jt: (it[n], p))] * 2,
            scratch_shapes=[pltpu.VMEM((hp, t, LANES), F32)] * 2),
        out_shape=[jax.ShapeDtypeStruct((s, N_HEADS * V_HEAD), BF16), jax.ShapeDtypeStruct((s, N_HEADS * V_HEAD), F32)],
        compiler_params=_cparams(("arbitrary", "arbitrary")),
    )(i_tab, j_tab, q, kv, kvt, krt)


def _att_bwd(q, qt, kv, kvt, kr, krt, y, lse, dy, dyt):
    s = q.shape[0]
    t = min(ATT_BLOCK, s)
    nb = s // t

    hp = ATT_BWD_HEADS
    pairs = [(i, j) for j in range(nb) for i in range(j, nb)]
    i_tab = jnp.array([p[0] for p in pairs], jnp.int32)
    j_tab = jnp.array([p[1] for p in pairs], jnp.int32)

    def body(i_ref, j_ref, q_ref, qt_ref, kv_ref, kvt_ref, kr_ref, krt_ref, y_ref, lse_ref, dy_ref, dyt_ref,
             dq_ref, dkvt_ref, dkrt_ref, dkv_s):
        p_, n = pl.program_id(0), pl.program_id(1)
        i, j = i_ref[n], j_ref[n]

        @pl.when((p_ == 0) & (n == 0))
        def _():
            dkrt_ref[...] = jnp.zeros(dkrt_ref.shape, F32)

        @pl.when(n == 0)
        def _():
            dq_ref[...] = jnp.zeros(dq_ref.shape, F32)

        @pl.when(i == j)
        def _():
            dkv_s[...] = jnp.zeros(dkv_s.shape, F32)

        def step(diagonal):
            lane = _lane((t, LANES))
            row = _row((LANES, t))
            krb, krt_b = kr_ref[...], krt_ref[...]
            dyv = dy_ref[...]
            yv = y_ref[...].astype(F32)
            lsev = lse_ref[...]
            dyt_b = dyt_ref[...]
            rows = pl.ds(pl.multiple_of(i * t, t), t)
            cols = pl.ds(pl.multiple_of(j * t, t), t)
            zeros_t = jnp.zeros((V_HEAD, t), BF16)
            ones_w = jnp.ones((LANES, LANES), BF16)
            groups = [slice(c * LANES, (c + 1) * LANES) for c in range(t // LANES)]
            heads = [slice(hh * LANES, (hh + 1) * LANES) for hh in range(hp)]
            scs, dps, stats = [], [], []
            for hh, sl in enumerate(heads):
                kvt_b = kvt_ref[sl, :]
                scs.append(_att_scores(q_ref[:, sl], kvt_b, krt_b, diagonal))
                pair, first = heads[hh // 2], hh % 2 == 0
                lse_g, dy_g, y_g = lsev[:, pair], dyv[:, pair], yv[:, pair]
                mine = (lane < V_HEAD) if first else (lane >= V_HEAD)
                lse_rep = jnp.where(mine, lse_g, pltpu.roll(lse_g, V_HEAD, 1))
                do_pad = jnp.where(lane >= V_HEAD, pltpu.roll(dy_g, V_HEAD, 1) if first else dy_g, 0.0)
                o_pad = jnp.where(lane >= V_HEAD, pltpu.roll(y_g, V_HEAD, 1) if first else y_g, 0.0)
                do_ln2 = do_pad * LN2
                prod = do_ln2 * o_pad
                head_part = prod.astype(BF16)
                rest_part = (prod - head_part.astype(F32)).astype(BF16)
                delta_rep = (jnp.dot(head_part, ones_w, preferred_element_type=F32)
                             + jnp.dot(rest_part, ones_w, preferred_element_type=F32))
                dps.append(jnp.dot(do_ln2.astype(BF16), kvt_b, preferred_element_type=F32))
                stats.append((lse_rep, delta_rep))
            dkr_acc = jnp.zeros((LANES, t), F32)
            for hh, sl in enumerate(heads):
                lse_rep, delta_rep = stats[hh]
                probs, dss = [], []
                for g in groups:
                    pg = jnp.exp2(scs[hh][:, g] - lse_rep)
                    probs.append(pg.astype(BF16))
                    dss.append((pg * (dps[hh][:, g] - delta_rep)).astype(BF16))
                prob, ds = jnp.concatenate(probs, axis=1), jnp.concatenate(dss, axis=1)
                dot_pad = jnp.concatenate([zeros_t, dyt_b[hh * V_HEAD:(hh + 1) * V_HEAD, :]], axis=0)
                k_eff = jnp.where(lane < QK_NOPE, kv_ref[:, sl], krb)
                dvt = jnp.dot(dot_pad, prob, preferred_element_type=F32)
                dq_ref[rows, sl] += jnp.dot(ds, k_eff, preferred_element_type=F32)
                dkt = jnp.dot(qt_ref[sl, :], ds, preferred_element_type=F32)
                dkv_s[hh] += dvt + jnp.where(row < QK_NOPE, dkt, 0.0)
                dkr_acc = dkr_acc + jnp.where(row >= QK_NOPE, dkt, 0.0)
            dkrt_ref[:, cols] += dkr_acc

        @pl.when(i > j)
        def _():
            step(False)

        @pl.when(i == j)
        def _():
            step(True)

        @pl.when(i == nb - 1)
        def _():
            for hh, sl in enumerate([slice(hh * LANES, (hh + 1) * LANES) for hh in range(hp)]):
                dkvt_ref[sl, :] = dkv_s[hh].astype(BF16)

    qi = lambda p, n, it, jt: (it[n], p)
    qti = lambda p, n, it, jt: (p, it[n])
    wide, half = hp * LANES, hp // 2 * LANES
    return pl.pallas_call(
        body, name="att_bwd",
        grid_spec=pltpu.PrefetchScalarGridSpec(
            num_scalar_prefetch=2, grid=(N_HEADS // hp, len(pairs)),
            in_specs=[pl.BlockSpec((t, wide), qi), pl.BlockSpec((wide, t), qti),
                      pl.BlockSpec((t, wide), lambda p, n, it, jt: (jt[n], p)),
                      pl.BlockSpec((wide, t), lambda p, n, it, jt: (p, jt[n])),
                      pl.BlockSpec((t, LANES), lambda p, n, it, jt: (jt[n], 0)),
                      pl.BlockSpec((LANES, t), lambda p, n, it, jt: (0, jt[n])),
                      pl.BlockSpec((t, half), qi), pl.BlockSpec((t, half), qi), pl.BlockSpec((t, half), qi),
                      pl.BlockSpec((half, t), qti)],
            out_specs=[pl.BlockSpec((s, wide), lambda p, n, it, jt: (0, p)),
                       pl.BlockSpec((wide, t), lambda p, n, it, jt: (p, jt[n])),
                       pl.BlockSpec((LANES, s), lambda p, n, it, jt: (0, 0))],
            scratch_shapes=[pltpu.VMEM((hp, LANES, t), F32)]),
        out_shape=[jax.ShapeDtypeStruct((s, N_HEADS * LANES), F32), jax.ShapeDtypeStruct((N_HEADS * LANES, s), BF16),
                   jax.ShapeDtypeStruct((LANES, s), F32)],
        compiler_params=_cparams(("arbitrary", "arbitrary")),
    )(i_tab, j_tab, q, qt, kv, kvt, kr, krt, y, lse, dy, dyt)


FFN_COLS = 256


def _ffn_conv(pad_ref, w, base, n):
    u = w[3:4] + w[2:3] * _shift_down(pad_ref, base, n, 0)
    for j in range(1, 3):
        u = u + w[2 - j:3 - j] * _shift_down(pad_ref, base, n, j)
    return u


def _ffn_act_fwd(up, fp):
    s, f2 = up.shape
    f = f2 // 2
    tc = FFN_COLS
    ts = min(RNN_CHUNK, s)
    nfb = f // tc

    def body(ug_ref, uv_ref, wg_ref, wv_ref, act_ref, gpad, vpad):
        zero8 = jnp.zeros((SUBLANES, tc), F32)
        gpad[0:SUBLANES, :] = zero8
        vpad[0:SUBLANES, :] = zero8
        gpad[SUBLANES:, :] = ug_ref[...]
        vpad[SUBLANES:, :] = uv_ref[...]
        wg, wv = wg_ref[...], wv_ref[...]

        def chunk(c, carry):
            base = pl.multiple_of(c * ts, ts)
            g = _ffn_conv(gpad, wg, base, ts)
            v = _ffn_conv(vpad, wv, base, ts)
            act_ref[pl.ds(base, ts), :] = (g * _sigmoid(g) * v).astype(BF16)
            return carry

        lax.fori_loop(0, s // ts, chunk, 0)

    return pl.pallas_call(
        body, name="ffn_act_fwd", grid=(nfb,),
        in_specs=[pl.BlockSpec((s, tc), lambda b: (0, b)), pl.BlockSpec((s, tc), lambda b: (0, b + nfb)),
                  pl.BlockSpec((SUBLANES, tc), lambda b: (0, b)), pl.BlockSpec((SUBLANES, tc), lambda b: (0, b + nfb))],
        out_specs=pl.BlockSpec((s, tc), lambda b: (0, b)),
        out_shape=jax.ShapeDtypeStruct((s, f), BF16),
        scratch_shapes=[pltpu.VMEM((s + SUBLANES, tc), F32)] * 2,
        compiler_params=_cparams(("arbitrary",)),
    )(up, up, fp, fp)


def _ffn_act_bwd(up, dact, fp):
    s, f2 = up.shape
    f = f2 // 2
    tc = FFN_COLS
    ts = min(RNN_CHUNK, s)
    nfb = f // tc

    def body(ug_ref, uv_ref, da_ref, wg_ref, wv_ref, dup_ref, red_ref, gpad, vpad, dgs, dvs):
        half = pl.program_id(1)
        wg, wv = wg_ref[...], wv_ref[...]

        @pl.when(half == 0)
        def _():
            zero8 = jnp.zeros((SUBLANES, tc), F32)
            gpad[0:SUBLANES, :] = zero8
            vpad[0:SUBLANES, :] = zero8
            gpad[SUBLANES:, :] = ug_ref[...]
            vpad[SUBLANES:, :] = uv_ref[...]
            dgs[s:, :] = zero8
            dvs[s:, :] = zero8

            def act(c, carry):
                base = pl.multiple_of(c * ts, ts)
                g = _ffn_conv(gpad, wg, base, ts)
                v = _ffn_conv(vpad, wv, base, ts)
                da = da_ref[pl.ds(base, ts), :]
                sg = _sigmoid(g)
                dgs[pl.ds(base, ts), :] = da * v * (sg * (1.0 + g * (1.0 - sg)))
                dvs[pl.ds(base, ts), :] = da * (g * sg)
                return carry

            lax.fori_loop(0, s // ts, act, 0)

        def conv_t(src, pad, w, out_ref, red_ref):
            def chunk(c, carry):
                base = pl.multiple_of(c * ts, ts)
                d_here = src[pl.ds(base, ts), :]
                dx = w[2:3] * d_here
                for j in range(1, 3):
                    dx = dx + w[2 - j:3 - j] * _shift_up(src, base, ts, j, 0)
                out_ref[pl.ds(base, ts), :] = dx.astype(BF16)
                taps = tuple(carry[k] + _colsum(d_here * _shift_down(pad, base, ts, 2 - k)) for k in range(3))
                return taps + (carry[3] + _colsum(d_here),)

            z1 = jnp.zeros((1, tc), F32)
            red = lax.fori_loop(0, s // ts, chunk, (z1, z1, z1, z1))
            red_ref[...] = jnp.concatenate(list(red) + [jnp.zeros((4, tc), F32)], axis=0)

        @pl.when(half == 0)
        def _():
            conv_t(dgs, gpad, wg, dup_ref, red_ref)

        @pl.when(half == 1)
        def _():
            conv_t(dvs, vpad, wv, dup_ref, red_ref)

    gcol = pl.BlockSpec((s, tc), lambda b, h: (0, b))
    vcol = pl.BlockSpec((s, tc), lambda b, h: (0, b + nfb))
    gpar = pl.BlockSpec((SUBLANES, tc), lambda b, h: (0, b))
    vpar = pl.BlockSpec((SUBLANES, tc), lambda b, h: (0, b + nfb))
    return pl.pallas_call(
        body, name="ffn_act_bwd", grid=(nfb, 2),
        in_specs=[gcol, vcol, gcol, gpar, vpar],
        out_specs=[pl.BlockSpec((s, tc), lambda b, h: (0, b + h * nfb)),
                   pl.BlockSpec((SUBLANES, tc), lambda b, h: (0, b + h * nfb))],
        out_shape=[jax.ShapeDtypeStruct((s, f2), BF16), jax.ShapeDtypeStruct((SUBLANES, f2), F32)],
        scratch_shapes=[pltpu.VMEM((s + SUBLANES, tc), F32)] * 4,
        compiler_params=_cparams(("arbitrary", "arbitrary")),
    )(up, up, dact, fp, fp)


def _rows8(rows, width):
    rows = [r.reshape(1, width).astype(F32) for r in rows]
    return jnp.concatenate(rows + [jnp.zeros((SUBLANES - len(rows), width), F32)], axis=0)


def _block_diag(w):
    n, b, _ = w.shape
    w = w.reshape(n // 2, 2, b, b)
    z = jnp.zeros((n // 2, b, b), w.dtype)
    top = jnp.concatenate([w[:, 0], z], axis=2)
    bot = jnp.concatenate([z, w[:, 1]], axis=2)
    return jnp.concatenate([top, bot], axis=1)


def _block_diag_t(bd):
    n, b2, _ = bd.shape
    b = b2 // 2
    return jnp.stack([bd[:, :b, :b], bd[:, b:, b:]], axis=1).reshape(2 * n, b, b)


def _local_step(x, mod, positions, target, w_in, fetch, sm, emit):
    s, d = x.shape
    o_rnn, o_mla = D_RNN, D_RNN + Q_LORA + KV_LORA + QK_ROPE
    wts = {}
    w_in_rnn = w_in[:, :o_rnn]
    w_in_mla = jnp.concatenate([w_in[:, o_rnn:o_mla], jnp.zeros((d, MLA_W - (o_mla - o_rnn)), w_in.dtype)], axis=1)
    w_in_g = w_in[:, o_mla:]
    hd = QK_NOPE + QK_ROPE
    wa_bd = _block_diag(sm["w_gate_a"]).astype(BF16)
    wx_bd = _block_diag(sm["w_gate_x"]).astype(BF16)

    pos = positions.reshape(s)
    half = QK_ROPE // 2
    inv_freq = ROPE_THETA ** (-jnp.arange(half, dtype=F32) / half)
    ang = pos.astype(F32)[:, None] * inv_freq
    cos, sin = jnp.cos(ang), jnp.sin(ang)
    rot_c = jnp.concatenate([jnp.ones((s, QK_NOPE), F32), cos, cos, jnp.ones((s, LANES - hd), F32)], axis=1)
    rot_s = jnp.concatenate([jnp.zeros((s, QK_NOPE), F32), -sin, sin, jnp.zeros((s, LANES - hd), F32)], axis=1)
    keep = (pos != 0).astype(F32).reshape(s, 1)

    gmod1 = _rows8([sm["norm1_g"], mod[1], mod[0]], d)
    gmod2 = _rows8([sm["norm2_g"], mod[4], mod[3], mod[2]], d)
    rp = jnp.concatenate([sm["conv_w"].reshape(4, D_RNN), _rows8([sm["conv_b"], sm["b_gate_a"], sm["b_gate_x"], sm["lru_param"]], D_RNN)[:4]], axis=0)
    fp = _rows8([sm["ffn_conv_w"][0], sm["ffn_conv_w"][1], sm["ffn_conv_w"][2], sm["ffn_conv_b"]], 2 * D_FF)
    ng = _rows8([jnp.concatenate([sm["q_norm_g"].reshape(-1), sm["kv_norm_g"].reshape(-1), jnp.zeros((MLA_W - Q_LORA - KV_LORA,), F32)])], MLA_W)
    fpar = _rows8([mod[5], sm["final_g"]], d)

    h = _norm_mod_fwd(x, gmod1, "norm1_fwd")
    proj_rnn = _mm(h, w_in_rnn, name="mm_in_rnn")
    proj_mla = _mm(h, w_in_mla, name="mm_in_mla")
    proj_g = _mm(h, w_in_g, out_dtype=BF16, name="mm_in_g")
    xc, ra, ix, hs = _rnn_fwd(proj_rnn, keep, rp, wa_bd, wx_bd)
    qn, kvn, kr = _mla_prep_fwd(proj_mla, rot_c, rot_s, ng)
    wts.update(fetch(("w_uq", "w_ukv"), kr))
    w_uq_p = jnp.pad(wts["w_uq"].reshape(Q_LORA, N_HEADS, hd), ((0, 0), (0, 0), (0, LANES - hd))).reshape(Q_LORA, N_HEADS * LANES)
    w_ukv = wts["w_ukv"]
    q_rot, q_rot_t = _rope_fwd_t(_mm(qn, w_uq_p, name="mm_uq"), rot_c, rot_s)
    kv, kvt = _mm(kvn, w_ukv, out_dtype=BF16, also_t=BF16, name="mm_ukv")
    krt = jnp.transpose(kr)
    y_mla, lse = _att_fwd(q_rot, kv, kvt, krt)
    wts.update(fetch(("w_proj_rnn", "w_proj_mla", "w_out", "w_up", "w_down"), lse))
    pr = _mm(hs, wts["w_proj_rnn"], out_dtype=BF16, name="mm_proj_rnn")
    pm = _mm(y_mla, wts["w_proj_mla"], out_dtype=BF16, name="mm_proj_mla")
    merged = _merge_fwd(pr, pm, proj_g)
    o = _mm(merged, wts["w_out"], name="mm_out")
    x1, h2 = _resid_norm_fwd(x, o, gmod2)
    up = _mm(h2, wts["w_up"], name="mm_up")
    act = _ffn_act_fwd(up, fp)
    dn = _mm(act, wts["w_down"], name="mm_down")

    dx2, ddn, red_f = _final_fwd_bwd(x1, dn, target, fpar)
    dact = _mm(ddn, wts["w_down"], tb=True, name="mm_d_act")
    tok = emit("w_down", _mm(act, ddn, ta=True, out_dtype=BF16, name="mm_dw_down"))
    dup, red_ffn = _ffn_act_bwd(up, dact, fp + tok)
    dh2 = _mm(dup, wts["w_up"], tb=True, name="mm_d_h2")
    tok = tok + emit("w_up", _mm(dup, h2, ta=True, out_dtype=BF16, name="mm_dw_up"))
    dx1, do, red_2 = _norm2_bwd(x1, dh2, dx2, o, gmod2 + tok)
    dmerged = _mm(do, wts["w_out"], tb=True, name="mm_d_merged")
    tok = tok + emit("w_out", _mm(merged, do, ta=True, out_dtype=BF16, name="mm_dw_out"))
    dpr, dpm, dg = _merge_bwd(dmerged, pr, pm, proj_g)
    dy_rnn = _mm(dpr, wts["w_proj_rnn"], tb=True, name="mm_d_yrnn")
    tok = tok + emit("w_proj_rnn", _mm(hs, dpr, ta=True, out_dtype=BF16, name="mm_dw_proj_rnn"))
    dy_mla, dy_mla_t = _mm(dpm, wts["w_proj_mla"], tb=True, also_t=BF16, name="mm_d_ymla")
    tok = tok + emit("w_proj_mla", _mm(y_mla, dpm, ta=True, out_dtype=BF16, name="mm_dw_proj_mla"))
    dq_rot, dkvt, dkrt = _att_bwd(q_rot, q_rot_t, kv, kvt, kr, krt, y_mla, lse, dy_mla, dy_mla_t)
    dq = _rope_bwd(dq_rot, rot_c, rot_s)
    dqn = _mm(dq, w_uq_p, tb=True, name="mm_d_qn")
    dw_uq_pt = _mm(dq, qn, ta=True, out_dtype=BF16, name="mm_dw_uq")
    tok = tok + emit("w_uq", dw_uq_pt.reshape(N_HEADS, LANES, Q_LORA)[:, :hd].reshape(N_HEADS * hd, Q_LORA))
    dkvn = jnp.transpose(_mm(w_ukv, dkvt, name="mm_d_kvn"))
    tok = tok + emit("w_ukv", _mm(dkvt, kvn, out_dtype=BF16, name="mm_dw_ukv"))
    dproj_mla, red_m = _mla_prep_bwd(proj_mla, dqn, dkvn, jnp.transpose(dkrt), rot_c, rot_s, ng + tok)
    dx_rnn, dwa_bd, dwx_bd, red_r = _rnn_bwd(proj_rnn, xc, ra, ix, hs, dy_rnn, keep, rp + tok, wa_bd, wx_bd)
    dw_in_t = jnp.concatenate([
        _mm(dx_rnn, h, ta=True, out_dtype=BF16, name="mm_dw_in_rnn"),
        _mm(dproj_mla, h, ta=True, out_dtype=BF16, name="mm_dw_in_mla")[:o_mla - o_rnn],
        _mm(dg, h, ta=True, out_dtype=BF16, name="mm_dw_in_g")], axis=0)
    tok = tok + emit("w_in", dw_in_t)
    dh_a = _mm(dx_rnn, w_in_rnn, tb=True, name="mm_d_h_rnn")
    dh_b = _mm(dproj_mla, w_in_mla, tb=True, name="mm_d_h_mla")
    dh_c = _mm(dg, w_in_g, tb=True, name="mm_d_h_g")
    grad_x, red_1 = _norm1_bwd(x, dh_a, dh_b, dh_c, dx1, gmod1 + tok)

    gs = {
        "norm1_g": red_1[0], "conv_w": red_r[0:4], "conv_b": red_r[4], "w_gate_a": _block_diag_t(dwa_bd),
        "b_gate_a": red_r[5], "w_gate_x": _block_diag_t(dwx_bd), "b_gate_x": red_r[6], "lru_param": red_r[7],
        "q_norm_g": red_m[0, :Q_LORA], "kv_norm_g": red_m[0, Q_LORA:Q_LORA + KV_LORA], "norm2_g": red_2[0],
        "ffn_conv_w": red_ffn[0:3], "ffn_conv_b": red_ffn[3], "final_g": red_f[0],
    }
    dmod = jnp.stack([red_1[2], red_1[1], red_2[3], red_2[2], red_2[1], red_f[1]], axis=0)
    return red_f[2, 0], grad_x, gs, dmod


MESH_IDS = pl.DeviceIdType.MESH
HBM_SPEC = pl.BlockSpec(memory_space=pltpu.HBM)


def _my_slot():
    return 4 * lax.axis_index("x") + 2 * lax.axis_index("y") + lax.axis_index("c")


def _all_gather(arrs, name):
    n = len(arrs)

    def body(*refs):
        ins, outs = refs[:n], refs[n:2 * n]
        send_sems, recv_sems, local_sems = refs[2 * n:]
        x, y, c = lax.axis_index("x"), lax.axis_index("y"), lax.axis_index("c")
        me, sibling = (x, y, c), (x, y, 1 - c)
        chips = [(1 - x, y), (x, 1 - y), (1 - x, 1 - y)]

        def slot(dev):
            return 4 * dev[0] + 2 * dev[1] + dev[2]

        def copy(a, k, block, to, src=None):
            dst = outs[a].at[slot(block)]
            return pltpu.make_async_remote_copy(
                src_ref=dst if src is None else src, dst_ref=dst, send_sem=send_sems.at[a, k], recv_sem=recv_sems.at[a, k],
                device_id=to, device_id_type=MESH_IDS)

        mine = [pltpu.make_async_copy(ins[a], outs[a].at[slot(me)], local_sems.at[a]) for a in range(n)]
        for cp in mine:
            cp.start()
        first = []
        for a in range(n):
            first.append(copy(a, 0, me, sibling, src=ins[a]))
            first += [copy(a, 1 + j, me, (*chip, c), src=ins[a]) for j, chip in enumerate(chips)]
        for cp in first:
            cp.start()
        passed = []
        for j, chip in enumerate(chips):
            for a in range(n):
                copy(a, 1 + j, (*chip, c), me).wait_recv()
                fwd = copy(a, 4 + j, (*chip, c), sibling)
                fwd.start()
                passed.append(fwd)
        for a in range(n):
            copy(a, 0, sibling, me).wait_recv()
            for j, chip in enumerate(chips):
                copy(a, 4 + j, (*chip, 1 - c), me).wait_recv()
        for cp in first + passed:
            cp.wait_send()
        for cp in mine:
            cp.wait()

    return pl.pallas_call(
        body, name=name,
        in_specs=[HBM_SPEC] * n, out_specs=[HBM_SPEC] * n,
        out_shape=[jax.ShapeDtypeStruct((N_DEV,) + a.shape, a.dtype) for a in arrs],
        scratch_shapes=[pltpu.SemaphoreType.DMA((n, 7)), pltpu.SemaphoreType.DMA((n, 7)), pltpu.SemaphoreType.DMA((n,))],
    )(*arrs)


SEM_SPEC =pl.BlockSpec(memory_space=pltpu.SEMAPHORE)
DATAFLOW = pltpu.SideEffectType.DATAFLOW_SIDE_EFFECTING
FLIPS = [(dx, dy, dc) for dx in (0, 1) for dy in (0, 1) for dc in (0, 1)][1:]


def _peer(k):
    dx, dy, dc = FLIPS[k]
    peer = (lax.axis_index("x") ^ dx, lax.axis_index("y") ^ dy, lax.axis_index("c") ^ dc)
    return peer, 4 * peer[0] + 2 * peer[1] + peer[2]


def _gather_start(shards, after, name):
    n, nf = len(shards), len(FLIPS)

    def body(*refs):
        srcs, lands = refs[:n], refs[n:2 * n]
        send_sems, recv_sems = refs[2 * n + 1:3 * n + 1], refs[3 * n + 1:4 * n + 1]
        token = refs[-1]
        me = _my_slot()
        for a in range(n):
            for k in range(nf):
                peer, _ = _peer(k)
                pltpu.make_async_remote_copy(
                    src_ref=srcs[a], dst_ref=lands[a].at[me], send_sem=send_sems[a].at[k], recv_sem=recv_sems[a].at[k],
                    device_id=peer, device_id_type=MESH_IDS).start()
        token[...] = jnp.zeros(token.shape, F32)

    land_shapes = [(N_DEV,) + a.shape for a in shards]
    sems = [pltpu.SemaphoreType.DMA((nf,))] * n
    out = pl.pallas_call(
        body, name=name,
        out_shape=(*sems, *sems, *[pltpu.HBM(a.shape, a.dtype) for a in shards],
                   *[pltpu.HBM(shp, a.dtype) for shp, a in zip(land_shapes, shards)],
                   jax.ShapeDtypeStruct((SUBLANES, LANES), F32)),
        in_specs=[HBM_SPEC] * (2 * n) + [pl.BlockSpec(memory_space=pl.ANY)],
        out_specs=(*[SEM_SPEC] * (2 * n), *[HBM_SPEC] * (2 * n), pl.BlockSpec(memory_space=pltpu.VMEM)),
        input_output_aliases={i: 2 * n + i for i in range(2 * n)},
        compiler_params=pltpu.CompilerParams(has_side_effects=DATAFLOW),
    )(*[pltpu.with_memory_space_constraint(a, pltpu.HBM) for a in shards],
      *[pltpu.with_memory_space_constraint(lax.empty(shp, a.dtype), pltpu.HBM) for shp, a in zip(land_shapes, shards)],
      after)
    return [(out[a], out[n + a], out[2 * n + a], out[3 * n + a]) for a in range(n)], out[-1]


def _gather_wait(flights, after, name):
    n, nf = len(flights), len(FLIPS)

    def body(*refs):
        send_sems, recv_sems = refs[:n], refs[n:2 * n]
        srcs, lands = refs[2 * n:3 * n], refs[3 * n:4 * n]
        for a in range(n):
            for k in range(nf):
                peer, peer_slot = _peer(k)
                cp = pltpu.make_async_remote_copy(
                    src_ref=srcs[a], dst_ref=lands[a].at[peer_slot], send_sem=send_sems[a].at[k],
                    recv_sem=recv_sems[a].at[k], device_id=peer, device_id_type=MESH_IDS)
                cp.wait_send()
                cp.wait_recv()

    srcs, lands = [f[2] for f in flights], [f[3] for f in flights]
    out = pl.pallas_call(
        body, name=name,
        out_shape=(*[pltpu.HBM(a.shape, a.dtype) for a in srcs], *[pltpu.HBM(a.shape, a.dtype) for a in lands]),
        in_specs=[SEM_SPEC] * (2 * n) + [HBM_SPEC] * (2 * n) + [pl.BlockSpec(memory_space=pl.ANY)],
        out_specs=tuple([HBM_SPEC] * (2 * n)),
        input_output_aliases={2 * n + i: i for i in range(2 * n)},
        compiler_params=pltpu.CompilerParams(has_side_effects=DATAFLOW),
    )(*[f[0] for f in flights], *[f[1] for f in flights], *srcs, *lands, after)
    return list(out[n:])


ROW_ALIGN = 16


def _span_start(slot, rows):
    return (rows * slot) // ROW_ALIGN * ROW_ALIGN


def _chunk_of(src_ref, slot, rows, span):
    if rows is None:
        return src_ref.at[slot]
    return src_ref.at[pl.ds(pl.multiple_of(_span_start(slot, rows), ROW_ALIGN), span)]


def _scatter_start(src, name, rows=None, span=None):
    def body(src_ref, land_ref, send_sems, recv_sems, src_thru, land_thru, token):
        me = _my_slot()
        for k in range(len(FLIPS)):
            peer, peer_slot = _peer(k)
            pltpu.make_async_remote_copy(
                src_ref=_chunk_of(src_ref, peer_slot, rows, span), dst_ref=land_ref.at[me], send_sem=send_sems.at[k],
                recv_sem=recv_sems.at[k], device_id=peer, device_id_type=MESH_IDS).start()
        token[...] = jnp.zeros(token.shape, F32)

    n = len(FLIPS)
    land_shape = src.shape if rows is None else (N_DEV, span, src.shape[1])
    return pl.pallas_call(
        body, name=name,
        out_shape=(pltpu.SemaphoreType.DMA((n,)), pltpu.SemaphoreType.DMA((n,)), pltpu.HBM(src.shape, src.dtype),
                   pltpu.HBM(land_shape, src.dtype), jax.ShapeDtypeStruct((SUBLANES, LANES), F32)),
        in_specs=(HBM_SPEC, HBM_SPEC),
        out_specs=(SEM_SPEC, SEM_SPEC, HBM_SPEC, HBM_SPEC, pl.BlockSpec(memory_space=pltpu.VMEM)),
        input_output_aliases={0: 2, 1: 3},
        compiler_params=pltpu.CompilerParams(has_side_effects=DATAFLOW),
    )(pltpu.with_memory_space_constraint(src, pltpu.HBM),
      pltpu.with_memory_space_constraint(lax.empty(land_shape, src.dtype), pltpu.HBM))


def _scatter_wait(send_sems, recv_sems, src_thru, land_thru, after, name, rows=None, span=None):
    def body(src_ref, land_ref, send_sems, recv_sems, after_ref, src_dead, got_ref):
        for k in range(len(FLIPS)):
            peer, peer_slot = _peer(k)
            cp = pltpu.make_async_remote_copy(
                src_ref=_chunk_of(src_ref, peer_slot, rows, span), dst_ref=land_ref.at[peer_slot], send_sem=send_sems.at[k],
                recv_sem=recv_sems.at[k], device_id=peer, device_id_type=MESH_IDS)
            cp.wait_send()
            cp.wait_recv()

    return pl.pallas_call(
        body, name=name,
        out_shape=(pltpu.HBM(src_thru.shape, src_thru.dtype), pltpu.HBM(land_thru.shape, land_thru.dtype)),
        in_specs=(HBM_SPEC, HBM_SPEC, SEM_SPEC, SEM_SPEC, pl.BlockSpec(memory_space=pl.ANY)),
        out_specs=(HBM_SPEC, HBM_SPEC), input_output_aliases={0: 0, 1: 1},
        compiler_params=pltpu.CompilerParams(has_side_effects=DATAFLOW),
    )(src_thru, land_thru, send_sems, recv_sems, after)


def _sum_sources(parts, own, name):
    k, r, c = parts.shape
    tr = r if k * r * c <= 2 * 1024 * 1024 else _pick(r, (512, 256, 128, 64, 32, 16, 8))

    def body(p_ref, own_ref, o_ref):
        me = _my_slot()
        acc = jnp.where(me == 0, own_ref[...], p_ref[0]).astype(F32)
        for s in range(1, k):
            acc = acc + jnp.where(me == s, own_ref[...], p_ref[s]).astype(F32)
        o_ref[...] = acc

    blk = pl.BlockSpec((tr, c), lambda i: (i, 0))
    return pl.pallas_call(
        body, name=name, grid=(r // tr,),
        in_specs=[pl.BlockSpec((k, tr, c), lambda i: (0, i, 0)), blk],
        out_specs=blk,
        out_shape=jax.ShapeDtypeStruct((r, c), F32),
        compiler_params=_cparams(("arbitrary",)),
    )(parts, own)


def _adamw_math(g, w, m, v):
    m_new = ADAM_B1 * m + (1.0 - ADAM_B1) * g
    v_new = ADAM_B2 * v + (1.0 - ADAM_B2) * jnp.square(g)
    m_hat = m_new / (1.0 - ADAM_B1 ** ADAM_STEP)
    v_hat = v_new / (1.0 - ADAM_B2 ** ADAM_STEP)
    return -ADAM_LR * (m_hat / (jnp.sqrt(v_hat) + ADAM_EPS) + ADAM_WD * w), m_new, v_new


def _adamw_many(gs, ws, ms, vs, name):
    n = len(gs)

    def body(*refs):
        ins, outs = refs[:4 * n], refs[4 * n:]
        for i in range(n):
            g = ins[i][...]
            outs[4 * i][...] = g
            outs[4 * i + 1][...], outs[4 * i + 2][...], outs[4 * i + 3][...] = _adamw_math(
                g, ins[n + i][...], ins[2 * n + i][...], ins[3 * n + i][...])

    return pl.pallas_call(
        body, name=name,
        out_shape=[jax.ShapeDtypeStruct(a.shape, F32) for a in ws for _ in range(4)],
        compiler_params=_cparams(),
    )(*gs, *ws, *ms, *vs)


def _adamw(parts, w, m, v, name, own=None):
    k, r, c = parts.shape
    tr = r if r * c <= 256 * 1024 else _pick(r, (256, 128, 64, 32, 16, 8))

    def body(*refs):
        p_ref, w_ref, m_ref, v_ref = refs[:4]
        g_ref, d_ref, nm_ref, nv_ref = refs[-4:]

        def part(s):
            if own is None:
                return p_ref[s].astype(F32)
            return jnp.where(_my_slot() == s, refs[4][...], p_ref[s]).astype(F32)

        g = part(0)
        for s in range(1, k):
            g = g + part(s)
        g_ref[...] = g
        d_ref[...], nm_ref[...], nv_ref[...] = _adamw_math(g, w_ref[...], m_ref[...], v_ref[...])

    blk = pl.BlockSpec((tr, c), lambda i: (i, 0))
    return pl.pallas_call(
        body, name=name, grid=(r // tr,),
        in_specs=[pl.BlockSpec((k, tr, c), lambda i: (0, i, 0)), blk, blk, blk] + ([] if own is None else [blk]),
        out_specs=[blk] * 4,
        out_shape=[jax.ShapeDtypeStruct((r, c), F32)] * 4,
        compiler_params=_cparams(("arbitrary",)),
    )(parts, w, m, v, *([] if own is None else [own]))


def _silu(v):
    return v * _sigmoid(v)


def _ada_fwd(c_all, w, b):
    def body(c_ref, w_ref, b_ref, o_ref):
        ca = _silu(c_ref[...]).astype(BF16)
        o_ref[...] = jnp.dot(ca, w_ref[...].astype(BF16), preferred_element_type=F32) + b_ref[...]

    return pl.pallas_call(
        body, name="ada_fwd", out_shape=jax.ShapeDtypeStruct((c_all.shape[0], w.shape[1]), F32),
        compiler_params=_cparams(),
    )(c_all, w, b)


def _ada_bwd(c_all, dmod):
    def body(c_ref, d_ref, o_ref):
        ca = _silu(c_ref[...]).astype(BF16).astype(F32)
        dm = d_ref[...].astype(BF16).astype(F32)
        acc = jnp.zeros(o_ref.shape, F32)
        for bi in range(c_all.shape[0]):
            acc = acc + jnp.transpose(ca[bi:bi + 1, :]) * dm[bi:bi + 1, :]
        o_ref[...] = acc

    return pl.pallas_call(
        body, name="ada_bwd", out_shape=jax.ShapeDtypeStruct((c_all.shape[1], dmod.shape[1]), F32),
        compiler_params=_cparams(),
    )(c_all, dmod)


COL_SHARDED = ("w_in", "w_uq", "w_ukv", "w_up")
ROW_SHARDED = ("w_proj_rnn", "w_proj_mla", "w_out", "w_down")
REPLICATED = ("b_ada", "norm1_g", "conv_b", "w_gate_a", "b_gate_a", "w_gate_x", "b_gate_x", "lru_param", "q_norm_g",
              "kv_norm_g", "norm2_g", "ffn_conv_b", "final_g")
WEIGHTS = ("w_ada", "b_ada", "norm1_g", "w_in", "conv_w", "conv_b", "w_gate_a", "b_gate_a", "w_gate_x", "b_gate_x",
           "lru_param", "q_norm_g", "w_uq", "kv_norm_g", "w_ukv", "w_proj_rnn", "w_proj_mla", "w_out", "norm2_g", "w_up",
           "ffn_conv_w", "ffn_conv_b", "w_down", "final_g")
TRANSPOSED_GRADS = COL_SHARDED
PACK_LANES = 128


def _pack(vecs, row_multiple=SUBLANES):
    flat = jnp.concatenate([v.reshape(-1).astype(F32) for v in vecs])
    pad = (-flat.shape[0]) % (PACK_LANES * row_multiple)
    return jnp.concatenate([flat, jnp.zeros((pad,), F32)]).reshape(-1, PACK_LANES)


def _unpack(packed, shapes):
    flat = packed.reshape(-1)
    out, off = [], 0
    for shp in shapes:
        size = math.prod(shp)
        out.append(flat[off:off + size].reshape(shp))
        off += size
    return out


def kernel(x, c, positions, w_ada, b_ada, norm1_g, w_in, conv_w, conv_b, w_gate_a, b_gate_a, w_gate_x, b_gate_x, lru_param, q_norm_g, w_uq, kv_norm_g, w_ukv, w_proj_rnn, w_proj_mla, w_out, norm2_g, w_up, ffn_conv_w, ffn_conv_b, w_down, final_g, loss_target, m_w_ada, m_b_ada, m_norm1_g, m_w_in, m_conv_w, m_conv_b, m_w_gate_a, m_b_gate_a, m_w_gate_x, m_b_gate_x, m_lru_param, m_q_norm_g, m_w_uq, m_kv_norm_g, m_w_ukv, m_w_proj_rnn, m_w_proj_mla, m_w_out, m_norm2_g, m_w_up, m_ffn_conv_w, m_ffn_conv_b, m_w_down, m_final_g, v_w_ada, v_b_ada, v_norm1_g, v_w_in, v_conv_w, v_conv_b, v_w_gate_a, v_b_gate_a, v_w_gate_x, v_b_gate_x, v_lru_param, v_q_norm_g, v_w_uq, v_kv_norm_g, v_w_ukv, v_w_proj_rnn, v_w_proj_mla, v_w_out, v_norm2_g, v_w_up, v_ffn_conv_w, v_ffn_conv_b, v_w_down, v_final_g):
    args = dict(locals())
    w = {n: args[n] for n in WEIGHTS}
    m = {n: args["m_" + n] for n in WEIGHTS}
    v = {n: args["v_" + n] for n in WEIGHTS}
    s, d = x.shape[1], x.shape[2]
    me = _my_slot()
    def two_d(a):
        assert a.ndim == 3 and a.shape[0] == 1, a.shape
        return a[0]

    big = COL_SHARDED + ROW_SHARDED
    shard = {n: two_d(w[n]).astype(BF16) for n in big}

    def whole(n, g):
        k, r, cc = g.shape
        return jnp.transpose(g, (1, 0, 2)).reshape(r, k * cc) if n in COL_SHARDED else g.reshape(k * r, cc)

    first = _all_gather([shard["w_in"], c, two_d(conv_w), two_d(ffn_conv_w)], "gather_first")
    c_all = first[1].reshape(N_DEV, d)
    conv_w_all = jnp.transpose(first[2], (1, 0, 2)).reshape(conv_w.shape[1], -1)
    ffn_conv_w_all = jnp.transpose(first[3], (1, 0, 2)).reshape(ffn_conv_w.shape[1], -1)

    ada_cols = w_ada.shape[2]
    b_cols = lax.dynamic_slice(b_ada, (0, me * ada_cols), (1, ada_cols))
    mod_cols = _ada_fwd(c_all, w_ada[0], b_cols)
    mod_all, = _all_gather([mod_cols], "gather_mod")

    later = ("w_uq", "w_ukv", "w_proj_rnn", "w_proj_mla", "w_out", "w_up", "w_down")
    flights, started = _gather_start([shard[n] for n in later], mod_all, "gather_start")
    flight = dict(zip(later, flights))

    def fetch(names, after):
        lands = _gather_wait([flight[n] for n in names], after, "gather_wait_" + names[0])
        return {n: whole(n, lax.dynamic_update_index_in_dim(g, shard[n], me, 0)) for n, g in zip(names, lands)}

    mod = lax.dynamic_index_in_dim(mod_all, me, axis=1, keepdims=False).reshape(6, d) + started[0, 0]

    sm = {n: w[n][0] for n in REPLICATED if n not in ("b_ada", "final_g")}
    sm["final_g"] = final_g
    sm["conv_w"] = conv_w_all
    sm["ffn_conv_w"] = ffn_conv_w_all
    in_flight, windows = {}, {}

    def emit(n, g):
        rows = g.shape[0] // N_DEV
        if rows % ROW_ALIGN == 0:
            windows[n] = (None, None)
            g = g.reshape(N_DEV, rows, g.shape[1])
        else:
            span = max(rows * k - _span_start(k, rows) for k in range(N_DEV)) + rows
            windows[n] = (rows, -(-span // ROW_ALIGN) * ROW_ALIGN)
            assert _span_start(N_DEV - 1, rows) + windows[n][1] <= g.shape[0], (n, g.shape)
        *in_flight[n], token = _scatter_start(g, "scatter_start_" + n, *windows[n])
        return token[0, 0]

    sq, grad_x, gs, dmod = _local_step(x[0], mod, positions[0], loss_target[0], whole("w_in", first[0]), fetch, sm, emit)

    small_names = [n for n in REPLICATED if n != "b_ada"] + ["conv_w", "ffn_conv_w"]
    small_shapes = [gs[n].shape for n in small_names] + [(6 * d,), (1,)]
    partial = _pack([gs[n] for n in small_names] + [dmod, sq.reshape(1)], N_DEV * SUBLANES)
    *small_flight, small_started = _scatter_start(partial.reshape(N_DEV, -1, PACK_LANES), "scatter_small_start")

    grads, deltas, new_m, new_v = {}, {}, {}, {}

    def update(n, parts, own=None):
        shp = w[n].shape
        lay = jnp.transpose if n in TRANSPOSED_GRADS else (lambda a: a)
        res = _adamw(parts, lay(two_d(w[n])), lay(two_d(m[n])), lay(two_d(v[n])), "adamw_" + n, own)
        grads[n], deltas[n], new_m[n], new_v[n] = [lay(a).reshape(shp) for a in res]

    for n in big:
        rows, span = windows[n]
        src, landed = _scatter_wait(*in_flight[n], small_started, "scatter_wait_" + n, rows, span)
        if rows is None:
            update(n, landed, lax.dynamic_index_in_dim(src, me, axis=0, keepdims=False))
        else:
            start = _span_start(me, rows)
            own = lax.dynamic_slice(src, (start, 0), (span, src.shape[1]))
            total = _sum_sources(landed, own, "sum_" + n)
            update(n, lax.dynamic_slice(total, (rows * me - start, 0), (rows, src.shape[1]))[None])

    chunks, landed = _scatter_wait(*small_flight, new_v[big[-1]], "scatter_small_wait")
    mine = _sum_sources(landed, lax.dynamic_index_in_dim(chunks, me, axis=0, keepdims=False), "sum_small")
    summed_all, dmod_all = _all_gather([mine, dmod.reshape(1, 6 * d)], "gather_small")
    summed = _unpack(summed_all, small_shapes)
    g_small = dict(zip(small_names, summed[:len(small_names)]))
    g_small["b_ada"] = summed[len(small_names)]
    loss = 0.5 * summed[-1][0] / d
    dmod_cols = lax.dynamic_slice(dmod_all.reshape(N_DEV, 6 * d), (0, me * ada_cols), (N_DEV, ada_cols))

    update("w_ada", _ada_bwd(c_all, dmod_cols)[None])

    for n in ("conv_w", "ffn_conv_w"):
        cols = w[n].shape[2]
        g_small[n] = lax.dynamic_slice(g_small[n], (0, me * cols), (g_small[n].shape[0], cols))
    small = REPLICATED + ("conv_w", "ffn_conv_w")
    as_rows = lambda a: a.reshape(1, -1) if a.ndim == 1 else a
    res = _adamw_many([as_rows(g_small[n].reshape(w[n].shape)) for n in small], [as_rows(w[n]) for n in small],
                      [as_rows(m[n]) for n in small], [as_rows(v[n]) for n in small], "adamw_small")
    for i, n in enumerate(small):
        grads[n], deltas[n], new_m[n], new_v[n] = [a.reshape(w[n].shape) for a in res[4 * i:4 * i + 4]]

    return (loss, grad_x[None], *[grads[n] for n in WEIGHTS], *[deltas[n] for n in WEIGHTS],
            *[new_m[n] for n in WEIGHTS], *[new_v[n] for n in WEIGHTS])
```

```python
import functools
import math

import jax
import jax.numpy as jnp
from jax import lax
from jax.experimental import pallas as pl
from jax.experimental.pallas import tpu as pltpu

F32 = jnp.float32
BF16 = jnp.bfloat16

N_DEV = 8
LANES = 128
SUBLANES = 8
VMEM_LIMIT = 56 * 1024 * 1024

D_RNN = 1280
Q_LORA = 384
KV_LORA = 256
QK_NOPE = 64
QK_ROPE = 32
V_HEAD = 64
N_HEADS = 16
D_FF = 2816
ROPE_THETA = 10000.0
LRU_C = 8.0
EPS = 1e-6
MLA_W = 768
ATT_SCALE = 1.0 / math.sqrt(QK_NOPE + QK_ROPE)

ADAM_LR, ADAM_B1, ADAM_B2, ADAM_EPS, ADAM_WD, ADAM_STEP = 0.001, 0.9, 0.999, 1e-08, 0.01, 10


def _cparams(sem=None):
    return pltpu.CompilerParams(dimension_semantics=sem, vmem_limit_bytes=VMEM_LIMIT)


def _pick(n, prefs):
    for p in prefs:
        if n % p == 0:
            return p
    return n


def _sigmoid(v):
    return 0.5 * jnp.tanh(0.5 * v) + 0.5


def _lane(shape):
    return lax.broadcasted_iota(jnp.int32, shape, len(shape) - 1)


def _row(shape):
    return lax.broadcasted_iota(jnp.int32, shape, len(shape) - 2)


MM_BLOCK_BYTES = 36 * 1024 * 1024


def _divisors(n):
    return [t for t in range(n, 0, -LANES) if n % t == 0] if n % LANES == 0 else [n]


HBM_BYTES_PER_US = 3.0e6
MXU_FLOPS_PER_US = 8.0e8
GRID_STEP_US = 0.35


def _mm_tiles(m, n, k, a_bytes, b_bytes, o_bytes):
    best = None
    for tm in [t for t in _divisors(m) if t <= 1024]:
        for tn in [t for t in _divisors(n) if t <= 2048]:
            for tk in _divisors(k):
                nk = k // tk
                need = 2 * (tm * tk * a_bytes + tk * tn * b_bytes + tm * tn * o_bytes) + (tm * tn * 4 if nk > 1 else 0)
                if need > MM_BLOCK_BYTES:
                    continue
                gi, gj = m // tm, n // tn
                for rows_outer in (True, False):
                    if nk > 1:
                        a_reads, b_reads = gj, gi
                    elif rows_outer:
                        a_reads, b_reads = 1, (gi if gj > 1 else 1)
                    else:
                        a_reads, b_reads = (gj if gi > 1 else 1), 1
                    traffic = m * k * a_bytes * a_reads + k * n * b_bytes * b_reads + m * n * (o_bytes + (8 * nk if nk > 1 else 0))
                    cost = max(traffic / HBM_BYTES_PER_US, 2.0 * m * n * k / MXU_FLOPS_PER_US) + gi * gj * nk * GRID_STEP_US
                    if best is None or cost < best[0]:
                        best = (cost, tm, tn, tk, rows_outer)
                break
    if best is None:
        raise ValueError((m, n, k))
    return best[1:]


def _mm(a, b, *, ta=False, tb=False, out_dtype=F32, also_t=None, name):
    (k_a, m) = a.shape if ta else a.shape[::-1]
    (n, k_b) = b.shape if tb else b.shape[::-1]
    assert k_a == k_b, (a.shape, b.shape, ta, tb)
    k = k_a
    tm, tn, tk, rows_outer = _mm_tiles(m, n, k, a.dtype.itemsize, b.dtype.itemsize, jnp.dtype(out_dtype).itemsize)
    nk = k // tk
    dims = (((0 if ta else 1,), (1 if tb else 0,)), ((), ()))
    n_out = 1 if also_t is None else 2

    def body(a_ref, b_ref, *rest):
        outs, acc = rest[:n_out], rest[n_out:]
        part = lax.dot_general(a_ref[...].astype(BF16), b_ref[...].astype(BF16), dims, preferred_element_type=F32)

        def write(val):
            outs[0][...] = val.astype(out_dtype)
            if also_t is not None:
                outs[1][...] = jnp.transpose(val).astype(also_t)

        if nk == 1:
            write(part)
            return
        acc_ref, = acc
        kk = pl.program_id(2)

        @pl.when(kk == 0)
        def _():
            acc_ref[...] = part

        @pl.when(kk > 0)
        def _():
            acc_ref[...] += part

        @pl.when(kk == nk - 1)
        def _():
            write(acc_ref[...])

    def ij(f):
        return (lambda i, j, kk: f(i, j, kk)) if rows_outer else (lambda j, i, kk: f(i, j, kk))

    a_spec = pl.BlockSpec((tk, tm), ij(lambda i, j, kk: (kk, i))) if ta else pl.BlockSpec((tm, tk), ij(lambda i, j, kk: (i, kk)))
    b_spec = pl.BlockSpec((tn, tk), ij(lambda i, j, kk: (j, kk))) if tb else pl.BlockSpec((tk, tn), ij(lambda i, j, kk: (kk, j)))
    out_specs = [pl.BlockSpec((tm, tn), ij(lambda i, j, kk: (i, j)))]
    out_shape = [jax.ShapeDtypeStruct((m, n), out_dtype)]
    if also_t is not None:
        out_specs.append(pl.BlockSpec((tn, tm), ij(lambda i, j, kk: (j, i))))
        out_shape.append(jax.ShapeDtypeStruct((n, m), also_t))
    res = pl.pallas_call(
        body, name=name,
        grid=(m // tm, n // tn, nk) if rows_outer else (n // tn, m // tm, nk),
        in_specs=[a_spec, b_spec], out_specs=out_specs, out_shape=out_shape,
        scratch_shapes=[] if nk == 1 else [pltpu.VMEM((tm, tn), F32)],
        compiler_params=_cparams(("arbitrary", "arbitrary", "arbitrary")),
    )(a, b)
    return res[0] if also_t is None else res


ROW_BLOCK_BYTES = 28 * 1024 * 1024


def _rowwise(fn, row_ins, par_ins, out_defs, red_defs, *, name):
    s = row_ins[0].shape[0]
    row_bytes = sum(a.shape[1] * a.dtype.itemsize for a in row_ins) + sum(c * jnp.dtype(dt).itemsize for c, dt in out_defs)
    tr = next((t for t in (512, 256, 128) if s % t == 0 and 2 * t * row_bytes <= ROW_BLOCK_BYTES), min(s, 128))
    nr, npar, no = len(row_ins), len(par_ins), len(out_defs)

    def body(*refs):
        rin, pin = refs[:nr], refs[nr:nr + npar]
        outs, reds = refs[nr + npar:nr + npar + no], refs[nr + npar + no:]
        i = pl.program_id(0)

        @pl.when(i == 0)
        def _():
            for r in reds:
                r[...] = jnp.zeros_like(r)

        fn(i, rin, pin, outs, reds)

    in_specs = [pl.BlockSpec((tr, a.shape[1]), lambda i: (i, 0)) for a in row_ins]
    in_specs += [pl.BlockSpec(a.shape, lambda i, nd=a.ndim: (0,) * nd) for a in par_ins]
    out_specs = [pl.BlockSpec((tr, c), lambda i: (i, 0)) for c, _ in out_defs]
    out_specs += [pl.BlockSpec(shp, lambda i: (0, 0)) for shp in red_defs]
    out_shape = [jax.ShapeDtypeStruct((s, c), dt) for c, dt in out_defs]
    out_shape += [jax.ShapeDtypeStruct(shp, F32) for shp in red_defs]
    return pl.pallas_call(
        body, name=name, grid=(s // tr,), in_specs=in_specs, out_specs=out_specs, out_shape=out_shape,
        compiler_params=_cparams(("arbitrary",)),
    )(*row_ins, *par_ins)


def _rms(v):
    return lax.rsqrt(jnp.mean(v * v, axis=-1, keepdims=True) + EPS)


def _colsum(v):
    return jnp.sum(v, axis=0, keepdims=True)


def _rms_bwd(dn, n, rstd):
    return rstd * (dn - n * jnp.mean(dn * n, axis=-1, keepdims=True))


def _norm_mod_fwd(x, gmod, name):
    def fn(i, rin, pin, outs, reds):
        xv = rin[0][...]
        p = pin[0][...]
        n = xv * _rms(xv)
        outs[0][...] = ((n * p[0:1]) * (1.0 + p[1:2]) + p[2:3]).astype(BF16)

    return _rowwise(fn, [x], [gmod], [(x.shape[1], BF16)], [], name=name)[0]


def _rope(v, rot_c, rot_s):
    half = QK_ROPE // 2
    swapped = jnp.where(_lane(v.shape) < QK_NOPE + half, pltpu.roll(v, LANES - half, 1), pltpu.roll(v, half, 1))
    return v * rot_c + swapped * rot_s


def _rope_t(dv, rot_c, rot_s):
    half = QK_ROPE // 2
    ds = dv * rot_s
    lane = _lane(dv.shape)
    swapped = jnp.where(lane < QK_NOPE + half, pltpu.roll(ds, LANES - half, 1), pltpu.roll(ds, half, 1))
    in_rope = (lane >= QK_NOPE) & (lane < QK_NOPE + QK_ROPE)
    return dv * rot_c + jnp.where(in_rope, swapped, 0.0)


def _mla_prep_fwd(proj_mla, rot_c, rot_s, ng):
    o1, o2 = Q_LORA, Q_LORA + KV_LORA

    def fn(i, rin, pin, outs, reds):
        g = pin[0][...]
        ql = rin[0][:, 0:o1]
        kl = rin[0][:, o1:o2]
        outs[0][...] = (ql * _rms(ql) * g[0:1, 0:o1]).astype(BF16)
        outs[1][...] = (kl * _rms(kl) * g[0:1, o1:o2]).astype(BF16)
        kr = pltpu.roll(rin[0][:, o2:o2 + LANES], QK_NOPE, 1)
        outs[2][...] = _rope(kr, rin[1][...], rin[2][...]).astype(BF16)

    return _rowwise(fn, [proj_mla, rot_c, rot_s], [ng], [(Q_LORA, BF16), (KV_LORA, BF16), (LANES, BF16)], [],
                    name="mla_prep_fwd")


def _mla_prep_bwd(proj_mla, dqn, dkvn, dkr, rot_c, rot_s, ng):
    o1, o2 = Q_LORA, Q_LORA + KV_LORA

    def fn(i, rin, pin, outs, reds):
        g = pin[0][...]
        ql = rin[0][:, 0:o1]
        kl = rin[0][:, o1:o2]
        rq, rk = _rms(ql), _rms(kl)
        nq, nk = ql * rq, kl * rk
        dq, dk = rin[1][...], rin[2][...]
        outs[0][:, 0:o1] = _rms_bwd(dq * g[0:1, 0:o1], nq, rq).astype(BF16)
        outs[0][:, o1:o2] = _rms_bwd(dk * g[0:1, o1:o2], nk, rk).astype(BF16)
        dkr_pre = pltpu.roll(_rope_t(rin[3][...], rin[4][...], rin[5][...]), LANES - QK_NOPE, 1)
        outs[0][:, o2:] = jnp.where(_lane(dkr_pre.shape) < QK_ROPE, dkr_pre, 0.0).astype(BF16)
        reds[0][0:1, 0:o1] += _colsum(dq * nq)
        reds[0][0:1, o1:o2] += _colsum(dk * nk)

    return _rowwise(fn, [proj_mla, dqn, dkvn, dkr, rot_c, rot_s], [ng], [(MLA_W, BF16)], [(SUBLANES, MLA_W)],
                    name="mla_prep_bwd")


def _rope_bwd(dq, rot_c, rot_s):
    def fn(i, rin, pin, outs, reds):
        c, sn = rin[1][...] * Q_PRESCALE, rin[2][...] * Q_PRESCALE
        for h in range(N_HEADS):
            sl = slice(h * LANES, (h + 1) * LANES)
            outs[0][:, sl] = _rope_t(rin[0][:, sl], c, sn).astype(BF16)

    return _rowwise(fn, [dq, rot_c, rot_s], [], [(dq.shape[1], BF16)], [], name="rope_bwd")[0]


def _rope_fwd_t(q, rot_c, rot_s):
    s, c = q.shape
    tr = min(256, s)

    def body(q_ref, c_ref, s_ref, o_ref, ot_ref):
        cc, sn = c_ref[...] * Q_PRESCALE, s_ref[...] * Q_PRESCALE
        for h in range(N_HEADS):
            sl = slice(h * LANES, (h + 1) * LANES)
            rot = _rope(q_ref[:, sl], cc, sn)
            o_ref[:, sl] = rot.astype(BF16)
            ot_ref[sl, :] = jnp.transpose(rot).astype(BF16)

    return pl.pallas_call(
        body, name="rope_fwd", grid=(s // tr,),
        in_specs=[pl.BlockSpec((tr, c), lambda i: (i, 0)), pl.BlockSpec((tr, LANES), lambda i: (i, 0)),
                  pl.BlockSpec((tr, LANES), lambda i: (i, 0))],
        out_specs=[pl.BlockSpec((tr, c), lambda i: (i, 0)), pl.BlockSpec((c, tr), lambda i: (0, i))],
        out_shape=[jax.ShapeDtypeStruct((s, c), BF16), jax.ShapeDtypeStruct((c, s), BF16)],
        compiler_params=_cparams(("arbitrary",)),
    )(q, rot_c, rot_s)


def _merge_fwd(pr, pm, proj_g):
    d = pr.shape[1]

    def fn(i, rin, pin, outs, reds):
        g_rnn, g_mla = rin[2][:, 0:d].astype(F32), rin[2][:, d:].astype(F32)
        outs[0][...] = (_sigmoid(g_rnn) * rin[0][...].astype(F32) + _sigmoid(g_mla) * rin[1][...].astype(F32)).astype(BF16)

    return _rowwise(fn, [pr, pm, proj_g], [], [(d, BF16)], [], name="merge_fwd")[0]


def _merge_bwd(dmerged, pr, pm, proj_g):
    d = pr.shape[1]

    def fn(i, rin, pin, outs, reds):
        dm = rin[0][...]
        sr, sm = _sigmoid(rin[3][:, 0:d].astype(F32)), _sigmoid(rin[3][:, d:].astype(F32))
        outs[0][...] = (dm * sr).astype(BF16)
        outs[1][...] = (dm * sm).astype(BF16)
        outs[2][:, 0:d] = (dm * rin[1][...].astype(F32) * sr * (1.0 - sr)).astype(BF16)
        outs[2][:, d:] = (dm * rin[2][...].astype(F32) * sm * (1.0 - sm)).astype(BF16)

    return _rowwise(fn, [dmerged, pr, pm, proj_g], [], [(d, BF16), (d, BF16), (2 * d, BF16)], [], name="merge_bwd")


def _resid_norm_fwd(x, o, gmod):
    d = x.shape[1]

    def fn(i, rin, pin, outs, reds):
        p = pin[0][...]
        x1 = rin[0][...] + p[3:4] * rin[1][...]
        outs[0][...] = x1
        outs[1][...] = ((x1 * _rms(x1) * p[0:1]) * (1.0 + p[1:2]) + p[2:3]).astype(BF16)

    return _rowwise(fn, [x, o], [gmod], [(d, F32), (d, BF16)], [], name="resid_norm_fwd")


def _final_fwd_bwd(x1, dn, target, par):
    d = x1.shape[1]

    def fn(i, rin, pin, outs, reds):
        p = pin[0][...]
        dnv = rin[1][...]
        x2 = rin[0][...] + p[0:1] * dnv
        rstd = _rms(x2)
        n3 = x2 * rstd
        err = n3 * p[1:2] - rin[2][...]
        dy = err * (1.0 / d)
        dx2 = _rms_bwd(dy * p[1:2], n3, rstd)
        outs[0][...] = dx2
        outs[1][...] = (dx2 * p[0:1]).astype(BF16)
        reds[0][0:1, :] += _colsum(dy * n3)
        reds[0][1:2, :] += _colsum(dx2 * dnv)
        reds[0][2:3, :] += jnp.zeros((1, d), F32) + jnp.sum(err * err)

    return _rowwise(fn, [x1, dn, target], [par], [(d, F32), (d, BF16)], [(SUBLANES, d)], name="final_fwd_bwd")


def _norm2_bwd(x1, dh2, dx2, o, gmod):
    d = x1.shape[1]

    def fn(i, rin, pin, outs, reds):
        p = pin[0][...]
        x1v, dh = rin[0][...], rin[1][...]
        rstd = _rms(x1v)
        n2 = x1v * rstd
        dx1 = rin[2][...] + _rms_bwd(dh * (p[0:1] * (1.0 + p[1:2])), n2, rstd)
        outs[0][...] = dx1
        outs[1][...] = (dx1 * p[3:4]).astype(BF16)
        reds[0][0:1, :] += _colsum(dh * n2 * (1.0 + p[1:2]))
        reds[0][1:2, :] += _colsum(dh * n2 * p[0:1])
        reds[0][2:3, :] += _colsum(dh)
        reds[0][3:4, :] += _colsum(dx1 * rin[3][...])

    return _rowwise(fn, [x1, dh2, dx2, o], [gmod], [(d, F32), (d, BF16)], [(SUBLANES, d)], name="norm2_bwd")


def _norm1_bwd(x, dh_a, dh_b, dh_c, dx1, gmod):
    d = x.shape[1]

    def fn(i, rin, pin, outs, reds):
        p = pin[0][...]
        xv = rin[0][...]
        dh = rin[1][...] + rin[2][...] + rin[3][...]
        rstd = _rms(xv)
        n1 = xv * rstd
        outs[0][...] = rin[4][...] + _rms_bwd(dh * (p[0:1] * (1.0 + p[1:2])), n1, rstd)
        reds[0][0:1, :] += _colsum(dh * n1 * (1.0 + p[1:2]))
        reds[0][1:2, :] += _colsum(dh * n1 * p[0:1])
        reds[0][2:3, :] += _colsum(dh)

    return _rowwise(fn, [x, dh_a, dh_b, dh_c, dx1], [gmod], [(d, F32)], [(SUBLANES, d)], name="norm1_bwd")


RNN_CHUNK = 512


def _shift_down(ref, base, n, j):
    v = ref[pl.ds(base, n + SUBLANES), :]
    return v[SUBLANES:] if j == 0 else pltpu.roll(v, j, 0)[SUBLANES:]


def _shift_up(ref, base, n, j, top_pad):
    v = ref[pl.ds(base + top_pad, n + SUBLANES), :]
    return v[:n] if j == 0 else pltpu.roll(v, n + SUBLANES - j, 0)[:n]


SCAN_GROUP = 128


def _scan_sizes(s):
    sizes = [s]
    while sizes[-1] > SUBLANES:
        assert sizes[-1] % SUBLANES == 0, s
        sizes.append(sizes[-1] // SUBLANES)
    return sizes


def _scan_scratch(s):
    return [pltpu.VMEM((n + 2 * SUBLANES, LANES), F32) for n in _scan_sizes(s)[1:] for _ in range(2)]


def _linear_scan(a_ref, b_ref, out_ref, a_off, s, reverse, levels):
    sizes = _scan_sizes(s)
    lv = [(a_ref, b_ref, a_off, 0)] + [(levels[2 * i], levels[2 * i + 1], 0, SUBLANES) for i in range(len(sizes) - 1)]
    zero8 = jnp.zeros((SUBLANES, LANES), F32)
    for (ar, br, _, _), n in zip(lv[1:], sizes[1:]):
        br[0:SUBLANES, :] = zero8
        br[pl.ds(n + SUBLANES, SUBLANES), :] = zero8
    order = list(range(SUBLANES - 1, -1, -1)) if reverse else list(range(SUBLANES))

    for lvl in range(len(sizes) - 1):
        ar, br, aoff, off = lv[lvl]
        m = sizes[lvl + 1]
        g = min(m, SCAN_GROUP)
        for t0 in range(0, m, g):
            acc_a = acc_b = None
            for r in order:
                sa = pl.ds(off + SUBLANES * t0 + r + aoff, g, stride=SUBLANES)
                sb = pl.ds(off + SUBLANES * t0 + r, g, stride=SUBLANES)
                a, b = ar[sa, :], br[sb, :]
                if acc_a is None:
                    acc_a, acc_b = a, b
                else:
                    acc_b = a * acc_b + b
                    acc_a = a * acc_a
            lv[lvl + 1][0][pl.ds(SUBLANES + t0, g), :] = acc_a
            lv[lvl + 1][1][pl.ds(SUBLANES + t0, g), :] = acc_b

    ar, br, _, off = lv[-1]
    n = sizes[-1]
    a, b = ar[pl.ds(off, n), :], br[pl.ds(off, n), :]
    h, rows = jnp.zeros((1, LANES), F32), [None] * n
    for j in (range(n - 1, -1, -1) if reverse else range(n)):
        h = a[j:j + 1, :] * h + b[j:j + 1, :]
        rows[j] = h
    br[pl.ds(off, n), :] = jnp.concatenate(rows, axis=0)

    for lvl in range(len(sizes) - 2, -1, -1):
        ar, br, aoff, off = lv[lvl]
        m = sizes[lvl + 1]
        up = lv[lvl + 1][1]
        dst = out_ref if lvl == 0 else br
        g = min(m, SCAN_GROUP)
        for t0 in range(0, m, g):
            h = _shift_up(up, t0, g, 1, SUBLANES) if reverse else _shift_down(up, t0, g, 1)
            for r in order:
                sa = pl.ds(off + SUBLANES * t0 + r + aoff, g, stride=SUBLANES)
                sb = pl.ds(off + SUBLANES * t0 + r, g, stride=SUBLANES)
                h = ar[sa, :] * h + br[sb, :]
                dst[sb, :] = h


def _one_minus_exp(z):
    series = -z * (1.0 + z * (0.5 + z * (1.0 / 6.0 + z * (1.0 / 24.0 + z * (1.0 / 120.0 + z * (1.0 / 720.0))))))
    return jnp.where(z > -0.1, series, 1.0 - jnp.exp(z))


def _softplus(v):
    return jnp.maximum(v, 0.0) + jnp.log(1.0 + jnp.exp(-jnp.abs(v)))


def _rnn_gates(xc, w, wa, wx, sp):
    xb = xc.astype(BF16)
    ra = _sigmoid(jnp.dot(xb, wa, preferred_element_type=F32) + w[5:6])
    ix = _sigmoid(jnp.dot(xb, wx, preferred_element_type=F32) + w[6:7])
    la = (-LRU_C) * ra * sp
    a = jnp.exp(la)
    mult = jnp.sqrt(_one_minus_exp(2.0 * la))
    return ra, ix, a, mult


def _rnn_fwd(x_rnn, keep, rp, wa_bd, wx_bd):
    s, r = x_rnn.shape
    ts = min(RNN_CHUNK, s)

    def body(x_ref, keep_ref, rp_ref, wa_ref, wx_ref, xc_ref, ra_ref, ix_ref, hs_ref, xpad, a_s, b_s, *levels):
        xpad[0:SUBLANES, :] = jnp.zeros((SUBLANES, LANES), F32)
        xpad[SUBLANES:, :] = x_ref[...]
        w = rp_ref[...]
        sp = _softplus(-w[7:8])
        wa, wx = wa_ref[0], wx_ref[0]

        def chunk(c, carry):
            base = pl.multiple_of(c * ts, ts)
            xc = w[4:5] + w[3:4] * _shift_down(xpad, base, ts, 0)
            for j in range(1, 4):
                xc = xc + w[3 - j:4 - j] * _shift_down(xpad, base, ts, j)
            ra, ix, a, mult = _rnn_gates(xc, w, wa, wx, sp)
            kp = keep_ref[pl.ds(base, ts), :]
            xc_ref[pl.ds(base, ts), :] = xc
            ra_ref[pl.ds(base, ts), :] = ra
            ix_ref[pl.ds(base, ts), :] = ix
            a_s[pl.ds(base, ts), :] = a * kp
            b_s[pl.ds(base, ts), :] = jnp.where(kp > 0.0, mult, 1.0) * (ix * xc)
            return carry

        lax.fori_loop(0, s // ts, chunk, 0)

        _linear_scan(a_s, b_s, hs_ref, 0, s, False, levels)

    col = pl.BlockSpec((s, LANES), lambda g: (0, g))
    return pl.pallas_call(
        body, name="rnn_fwd", grid=(r // LANES,),
        in_specs=[col, pl.BlockSpec((s, 1), lambda g: (0, 0)), pl.BlockSpec((SUBLANES, LANES), lambda g: (0, g)),
                  pl.BlockSpec((1, LANES, LANES), lambda g: (g, 0, 0)), pl.BlockSpec((1, LANES, LANES), lambda g: (g, 0, 0))],
        out_specs=[col] * 4,
        out_shape=[jax.ShapeDtypeStruct((s, r), F32)] * 4,
        scratch_shapes=[pltpu.VMEM((s + SUBLANES, LANES), F32), pltpu.VMEM((s, LANES), F32), pltpu.VMEM((s, LANES), F32),
                        *_scan_scratch(s)],
        compiler_params=_cparams(("arbitrary",)),
    )(x_rnn, keep, rp, wa_bd, wx_bd)


def _rnn_bwd(x_rnn, xc, ra, ix, hs, dy, keep, rp, wa_bd, wx_bd):
    s, r = x_rnn.shape
    ts = min(RNN_CHUNK, s)

    def body(x_ref, xc_ref, ra_ref, ix_ref, hs_ref, dy_ref, keep_ref, rp_ref, wa_ref, wx_ref,
             dx_ref, dwa_ref, dwx_ref, red_ref, xpad, hpad, a_s, dh_s, dxc_s, *levels):
        zero8 = jnp.zeros((SUBLANES, LANES), F32)
        xpad[0:SUBLANES, :] = zero8
        xpad[SUBLANES:, :] = x_ref[...]
        hpad[0:SUBLANES, :] = zero8
        hpad[SUBLANES:, :] = hs_ref[...]
        a_s[s:, :] = zero8
        dxc_s[s:, :] = zero8
        w = rp_ref[...]
        sp = _softplus(-w[7:8])
        wa, wx = wa_ref[0], wx_ref[0]

        def decay(c, carry):
            base = pl.multiple_of(c * ts, ts)
            a = jnp.exp((-LRU_C) * ra_ref[pl.ds(base, ts), :] * sp)
            a_s[pl.ds(base, ts), :] = a * keep_ref[pl.ds(base, ts), :]
            return carry

        lax.fori_loop(0, s // ts, decay, 0)

        _linear_scan(a_s, dy_ref, dh_s, 1, s, True, levels)

        def gates(c, carry):
            dwa, dwx, d_ba, d_bx, d_sp, d_cb = carry
            base = pl.multiple_of(c * ts, ts)
            xcv = xc_ref[pl.ds(base, ts), :]
            rav = ra_ref[pl.ds(base, ts), :]
            ixv = ix_ref[pl.ds(base, ts), :]
            kp = keep_ref[pl.ds(base, ts), :]
            dh = dh_s[pl.ds(base, ts), :]
            h_prev = _shift_down(hpad, base, ts, 1)
            la = (-LRU_C) * rav * sp
            a = jnp.exp(la)
            mult = jnp.sqrt(_one_minus_exp(2.0 * la))
            mult_eff = jnp.where(kp > 0.0, mult, 1.0)
            d_a = dh * h_prev * kp
            d_mult = dh * (ixv * xcv) * kp
            d_ix = dh * mult_eff * xcv
            d_xc = dh * mult_eff * ixv
            d_la = d_a * a - d_mult * (a * a) / mult
            d_pa = d_la * ((-LRU_C) * sp) * rav * (1.0 - rav)
            d_px = d_ix * ixv * (1.0 - ixv)
            xb = xcv.astype(BF16)
            pab, pxb = d_pa.astype(BF16), d_px.astype(BF16)
            tn = (((0,), (0,)), ((), ()))
            nt_ = (((1,), (1,)), ((), ()))
            dwa = dwa + lax.dot_general(xb, pab, tn, preferred_element_type=F32)
            dwx = dwx + lax.dot_general(xb, pxb, tn, preferred_element_type=F32)
            d_xc = d_xc + lax.dot_general(pab, wa, nt_, preferred_element_type=F32)
            d_xc = d_xc + lax.dot_general(pxb, wx, nt_, preferred_element_type=F32)
            dxc_s[pl.ds(base, ts), :] = d_xc
            return (dwa, dwx, d_ba + _colsum(d_pa), d_bx + _colsum(d_px),
                    d_sp + _colsum(d_la * ((-LRU_C) * rav)), d_cb + _colsum(d_xc))

        z1 = jnp.zeros((1, LANES), F32)
        zw = jnp.zeros((LANES, LANES), F32)
        dwa, dwx, d_ba, d_bx, d_sp, d_cb = lax.fori_loop(0, s // ts, gates, (zw, zw, z1, z1, z1, z1))
        dwa_ref[0] = dwa
        dwx_ref[0] = dwx

        def conv(c, carry):
            base = pl.multiple_of(c * ts, ts)
            d_here = dxc_s[pl.ds(base, ts), :]
            dx = w[3:4] * d_here
            for j in range(1, 4):
                dx = dx + w[3 - j:4 - j] * _shift_up(dxc_s, base, ts, j, 0)
            dx_ref[pl.ds(base, ts), :] = dx.astype(BF16)
            return tuple(carry[k] + _colsum(d_here * _shift_down(xpad, base, ts, 3 - k)) for k in range(4))

        d_w = lax.fori_loop(0, s // ts, conv, (z1, z1, z1, z1))
        d_lru = d_sp * (-_sigmoid(-w[7:8]))
        red_ref[...] = jnp.concatenate(list(d_w) + [d_cb, d_ba, d_bx, d_lru], axis=0)

    col = pl.BlockSpec((s, LANES), lambda g: (0, g))
    par = pl.BlockSpec((SUBLANES, LANES), lambda g: (0, g))
    wsp = pl.BlockSpec((1, LANES, LANES), lambda g: (g, 0, 0))
    return pl.pallas_call(
        body, name="rnn_bwd", grid=(r // LANES,),
        in_specs=[col] * 6 + [pl.BlockSpec((s, 1), lambda g: (0, 0)), par, wsp, wsp],
        out_specs=[col, wsp, wsp, par],
        out_shape=[jax.ShapeDtypeStruct((s, r), BF16), jax.ShapeDtypeStruct((r // LANES, LANES, LANES), F32),
                   jax.ShapeDtypeStruct((r // LANES, LANES, LANES), F32), jax.ShapeDtypeStruct((SUBLANES, r), F32)],
        scratch_shapes=[pltpu.VMEM((s + SUBLANES, LANES), F32), pltpu.VMEM((s + SUBLANES, LANES), F32),
                        pltpu.VMEM((s + SUBLANES, LANES), F32), pltpu.VMEM((s, LANES), F32),
                        pltpu.VMEM((s + SUBLANES, LANES), F32), *_scan_scratch(s)],
        compiler_params=_cparams(("arbitrary",)),
    )(x_rnn, xc, ra, ix, hs, dy, keep, rp, wa_bd, wx_bd)


ATT_BLOCK = 512
ATT_FWD_HEADS = 8
ATT_BWD_HEADS = 4


LOG2E = 1.4426950408889634
LN2 = 0.6931471805599453
Q_PRESCALE = ATT_SCALE * LOG2E


def _att_scores(q, kvt, krt, diagonal):
    kt_eff = jnp.where(_row(kvt.shape) < QK_NOPE, kvt, krt)
    sc = jnp.dot(q, kt_eff, preferred_element_type=F32)
    if diagonal:
        sc = jnp.where(lax.broadcasted_iota(jnp.int32, sc.shape, 1) <= lax.broadcasted_iota(jnp.int32, sc.shape, 0), sc, -jnp.inf)
    return sc


def _att_fwd(q, kv, kvt, krt):
    s = q.shape[0]
    t = min(ATT_BLOCK, s)
    nb = s // t
    hp = ATT_FWD_HEADS

    pairs = [(i, j) for i in range(nb) for j in range(i + 1)]
    i_tab = jnp.array([p[0] for p in pairs], jnp.int32)
    j_tab = jnp.array([p[1] for p in pairs], jnp.int32)

    def body(i_ref, j_ref, q_ref, kv_ref, kvt_ref, krt_ref, y_ref, lse_ref, m_s, acc_s):
        i, j = i_ref[pl.program_id(1)], j_ref[pl.program_id(1)]

        @pl.when(j == 0)
        def _():
            m_s[...] = jnp.full(m_s.shape, -jnp.inf, F32)
            acc_s[...] = jnp.zeros(acc_s.shape, F32)

        def step(diagonal):
            krt_b = krt_ref[...]
            lane = _lane((t, LANES))
            groups = [slice(c * LANES, (c + 1) * LANES) for c in range(t // LANES)]
            heads = [slice(hh * LANES, (hh + 1) * LANES) for hh in range(hp)]
            scs = [_att_scores(q_ref[:, sl], kvt_ref[sl, :], krt_b, diagonal) for sl in heads]
            stats = []
            for hh in range(hp):
                m_prev = m_s[hh]
                m_blk = scs[hh][:, groups[0]]
                for g in groups[1:]:
                    m_blk = jnp.maximum(m_blk, scs[hh][:, g])
                stats.append((m_prev, jnp.maximum(m_prev, jnp.max(m_blk, axis=-1, keepdims=True))))
            for hh in range(hp):
                m_prev, m_new = stats[hh]
                kvb = kv_ref[:, heads[hh]]
                ones_v = jnp.where(lane < QK_NOPE, jnp.ones_like(kvb), kvb)
                p = jnp.concatenate([jnp.exp2(scs[hh][:, g] - m_new).astype(BF16) for g in groups], axis=1)
                acc_s[hh] = jnp.exp2(m_prev - m_new) * acc_s[hh] + jnp.dot(p, ones_v, preferred_element_type=F32)
                m_s[hh] = m_new

        @pl.when(j < i)
        def _():
            step(False)

        @pl.when(j == i)
        def _():
            step(True)
            lane = _lane((t, LANES))
            for g in range(hp // 2):
                sl = slice(g * LANES, (g + 1) * LANES)
                a0, a1 = acc_s[2 * g], acc_s[2 * g + 1]
                l0, l1 = a0[:, 0:1], a1[:, 0:1]
                y_ref[:, sl] = jnp.where(lane < V_HEAD, pltpu.roll(a0 / l0, V_HEAD, 1), a1 / l1).astype(BF16)
                lse_ref[:, sl] = jnp.where(lane < V_HEAD, m_s[2 * g] + jnp.log(l0) * LOG2E, m_s[2 * g + 1] + jnp.log(l1) * LOG2E)

    return pl.pallas_call(
        body, name="att_fwd",
        grid_spec=pltpu.PrefetchScalarGridSpec(
            num_scalar_prefetch=2, grid=(N_HEADS // hp, len(pairs)),
            in_specs=[pl.BlockSpec((t, hp * LANES), lambda p, n, it, jt: (it[n], p)),
                      pl.BlockSpec((t, hp * LANES), lambda p, n, it, jt: (jt[n], p)),
                      pl.BlockSpec((hp * LANES, t), lambda p, n, it, jt: (p, jt[n])),
                      pl.BlockSpec((LANES, t), lambda p, n, it, jt: (0, jt[n]))],
            out_specs=[pl.BlockSpec((t, hp // 2 * LANES), lambda p, n, it, jt: (it[n], p))] * 2,
            scratch_shapes=[pltpu.VMEM((hp, t, LANES), F32)] * 2),
        out_shape=[jax.ShapeDtypeStruct((s, N_HEADS * V_HEAD), BF16), jax.ShapeDtypeStruct((s, N_HEADS * V_HEAD), F32)],
        compiler_params=_cparams(("arbitrary", "arbitrary")),
    )(i_tab, j_tab, q, kv, kvt, krt)


def _att_bwd(q, qt, kv, kvt, kr, krt, y, lse, dy, dyt):
    s = q.shape[0]
    t = min(ATT_BLOCK, s)
    nb = s // t

    hp = ATT_BWD_HEADS
    pairs = [(i, j) for j in range(nb) for i in range(j, nb)]
    i_tab = jnp.array([p[0] for p in pairs], jnp.int32)
    j_tab = jnp.array([p[1] for p in pairs], jnp.int32)

    def body(i_ref, j_ref, q_ref, qt_ref, kv_ref, kvt_ref, kr_ref, krt_ref, y_ref, lse_ref, dy_ref, dyt_ref,
             dq_ref, dkvt_ref, dkrt_ref, dkv_s):
        p_, n = pl.program_id(0), pl.program_id(1)
        i, j = i_ref[n], j_ref[n]

        @pl.when((p_ == 0) & (n == 0))
        def _():
            dkrt_ref[...] = jnp.zeros(dkrt_ref.shape, F32)

        @pl.when(n == 0)
        def _():
            dq_ref[...] = jnp.zeros(dq_ref.shape, F32)

        @pl.when(i == j)
        def _():
            dkv_s[...] = jnp.zeros(dkv_s.shape, F32)

        def step(diagonal):
            lane = _lane((t, LANES))
            row = _row((LANES, t))
            krb, krt_b = kr_ref[...], krt_ref[...]
            dyv = dy_ref[...]
            yv = y_ref[...].astype(F32)
            lsev = lse_ref[...]
            dyt_b = dyt_ref[...]
            rows = pl.ds(pl.multiple_of(i * t, t), t)
            cols = pl.ds(pl.multiple_of(j * t, t), t)
            zeros_t = jnp.zeros((V_HEAD, t), BF16)
            ones_w = jnp.ones((LANES, LANES), BF16)
            groups = [slice(c * LANES, (c + 1) * LANES) for c in range(t // LANES)]
            heads = [slice(hh * LANES, (hh + 1) * LANES) for hh in range(hp)]
            scs, dps, stats = [], [], []
            for hh, sl in enumerate(heads):
                kvt_b = kvt_ref[sl, :]
                scs.append(_att_scores(q_ref[:, sl], kvt_b, krt_b, diagonal))
                pair, first = heads[hh // 2], hh % 2 == 0
                lse_g, dy_g, y_g = lsev[:, pair], dyv[:, pair], yv[:, pair]
                mine = (lane < V_HEAD) if first else (lane >= V_HEAD)
                lse_rep = jnp.where(mine, lse_g, pltpu.roll(lse_g, V_HEAD, 1))
                do_pad = jnp.where(lane >= V_HEAD, pltpu.roll(dy_g, V_HEAD, 1) if first else dy_g, 0.0)
                o_pad = jnp.where(lane >= V_HEAD, pltpu.roll(y_g, V_HEAD, 1) if first else y_g, 0.0)
                do_ln2 = do_pad * LN2
                prod = do_ln2 * o_pad
                head_part = prod.astype(BF16)
                rest_part = (prod - head_part.astype(F32)).astype(BF16)
                delta_rep = (jnp.dot(head_part, ones_w, preferred_element_type=F32)
                             + jnp.dot(rest_part, ones_w, preferred_element_type=F32))
                dps.append(jnp.dot(do_ln2.astype(BF16), kvt_b, preferred_element_type=F32))
                stats.append((lse_rep, delta_rep))
            dkr_acc = jnp.zeros((LANES, t), F32)
            for hh, sl in enumerate(heads):
                lse_rep, delta_rep = stats[hh]
                probs, dss = [], []
                for g in groups:
                    pg = jnp.exp2(scs[hh][:, g] - lse_rep)
                    probs.append(pg.astype(BF16))
                    dss.append((pg * (dps[hh][:, g] - delta_rep)).astype(BF16))
                prob, ds = jnp.concatenate(probs, axis=1), jnp.concatenate(dss, axis=1)
                dot_pad = jnp.concatenate([zeros_t, dyt_b[hh * V_HEAD:(hh + 1) * V_HEAD, :]], axis=0)
                k_eff = jnp.where(lane < QK_NOPE, kv_ref[:, sl], krb)
                dvt = jnp.dot(dot_pad, prob, preferred_element_type=F32)
                dq_ref[rows, sl] += jnp.dot(ds, k_eff, preferred_element_type=F32)
                dkt = jnp.dot(qt_ref[sl, :], ds, preferred_element_type=F32)
                dkv_s[hh] += dvt + jnp.where(row < QK_NOPE, dkt, 0.0)
                dkr_acc = dkr_acc + jnp.where(row >= QK_NOPE, dkt, 0.0)
            dkrt_ref[:, cols] += dkr_acc

        @pl.when(i > j)
        def _():
            step(False)

        @pl.when(i == j)
        def _():
            step(True)

        @pl.when(i == nb - 1)
        def _():
            for hh, sl in enumerate([slice(hh * LANES, (hh + 1) * LANES) for hh in range(hp)]):
                dkvt_ref[sl, :] = dkv_s[hh].astype(BF16)

    qi = lambda p, n, it, jt: (it[n], p)
    qti = lambda p, n, it, jt: (p, it[n])
    wide, half = hp * LANES, hp // 2 * LANES
    return pl.pallas_call(
        body, name="att_bwd",
        grid_spec=pltpu.PrefetchScalarGridSpec(
            num_scalar_prefetch=2, grid=(N_HEADS // hp, len(pairs)),
            in_specs=[pl.BlockSpec((t, wide), qi), pl.BlockSpec((wide, t), qti),
                      pl.BlockSpec((t, wide), lambda p, n, it, jt: (jt[n], p)),
                      pl.BlockSpec((wide, t), lambda p, n, it, jt: (p, jt[n])),
                      pl.BlockSpec((t, LANES), lambda p, n, it, jt: (jt[n], 0)),
                      pl.BlockSpec((LANES, t), lambda p, n, it, jt: (0, jt[n])),
                      pl.BlockSpec((t, half), qi), pl.BlockSpec((t, half), qi), pl.BlockSpec((t, half), qi),
                      pl.BlockSpec((half, t), qti)],
            out_specs=[pl.BlockSpec((s, wide), lambda p, n, it, jt: (0, p)),
                       pl.BlockSpec((wide, t), lambda p, n, it, jt: (p, jt[n])),
                       pl.BlockSpec((LANES, s), lambda p, n, it, jt: (0, 0))],
            scratch_shapes=[pltpu.VMEM((hp, LANES, t), F32)]),
        out_shape=[jax.ShapeDtypeStruct((s, N_HEADS * LANES), F32), jax.ShapeDtypeStruct((N_HEADS * LANES, s), BF16),
                   jax.ShapeDtypeStruct((LANES, s), F32)],
        compiler_params=_cparams(("arbitrary", "arbitrary")),
    )(i_tab, j_tab, q, qt, kv, kvt, kr, krt, y, lse, dy, dyt)


FFN_COLS = 256


def _ffn_conv(pad_ref, w, base, n):
    u = w[3:4] + w[2:3] * _shift_down(pad_ref, base, n, 0)
    for j in range(1, 3):
        u = u + w[2 - j:3 - j] * _shift_down(pad_ref, base, n, j)
    return u


def _ffn_act_fwd(up, fp):
    s, f2 = up.shape
    f = f2 // 2
    tc = FFN_COLS
    ts = min(RNN_CHUNK, s)
    nfb = f // tc

    def body(ug_ref, uv_ref, wg_ref, wv_ref, act_ref, gpad, vpad):
        zero8 = jnp.zeros((SUBLANES, tc), F32)
        gpad[0:SUBLANES, :] = zero8
        vpad[0:SUBLANES, :] = zero8
        gpad[SUBLANES:, :] = ug_ref[...]
        vpad[SUBLANES:, :] = uv_ref[...]
        wg, wv = wg_ref[...], wv_ref[...]

        def chunk(c, carry):
            base = pl.multiple_of(c * ts, ts)
            g = _ffn_conv(gpad, wg, base, ts)
            v = _ffn_conv(vpad, wv, base, ts)
            act_ref[pl.ds(base, ts), :] = (g * _sigmoid(g) * v).astype(BF16)
            return carry

        lax.fori_loop(0, s // ts, chunk, 0)

    return pl.pallas_call(
        body, name="ffn_act_fwd", grid=(nfb,),
        in_specs=[pl.BlockSpec((s, tc), lambda b: (0, b)), pl.BlockSpec((s, tc), lambda b: (0, b + nfb)),
                  pl.BlockSpec((SUBLANES, tc), lambda b: (0, b)), pl.BlockSpec((SUBLANES, tc), lambda b: (0, b + nfb))],
        out_specs=pl.BlockSpec((s, tc), lambda b: (0, b)),
        out_shape=jax.ShapeDtypeStruct((s, f), BF16),
        scratch_shapes=[pltpu.VMEM((s + SUBLANES, tc), F32)] * 2,
        compiler_params=_cparams(("arbitrary",)),
    )(up, up, fp, fp)


def _ffn_act_bwd(up, dact, fp):
    s, f2 = up.shape
    f = f2 // 2
    tc = FFN_COLS
    ts = min(RNN_CHUNK, s)
    nfb = f // tc

    def body(ug_ref, uv_ref, da_ref, wg_ref, wv_ref, dup_ref, red_ref, gpad, vpad, dgs, dvs):
        half = pl.program_id(1)
        wg, wv = wg_ref[...], wv_ref[...]

        @pl.when(half == 0)
        def _():
            zero8 = jnp.zeros((SUBLANES, tc), F32)
            gpad[0:SUBLANES, :] = zero8
            vpad[0:SUBLANES, :] = zero8
            gpad[SUBLANES:, :] = ug_ref[...]
            vpad[SUBLANES:, :] = uv_ref[...]
            dgs[s:, :] = zero8
            dvs[s:, :] = zero8

            def act(c, carry):
                base = pl.multiple_of(c * ts, ts)
                g = _ffn_conv(gpad, wg, base, ts)
                v = _ffn_conv(vpad, wv, base, ts)
                da = da_ref[pl.ds(base, ts), :]
                sg = _sigmoid(g)
                dgs[pl.ds(base, ts), :] = da * v * (sg * (1.0 + g * (1.0 - sg)))
                dvs[pl.ds(base, ts), :] = da * (g * sg)
                return carry

            lax.fori_loop(0, s // ts, act, 0)

        def conv_t(src, pad, w, out_ref, red_ref):
            def chunk(c, carry):
                base = pl.multiple_of(c * ts, ts)
                d_here = src[pl.ds(base, ts), :]
                dx = w[2:3] * d_here
                for j in range(1, 3):
                    dx = dx + w[2 - j:3 - j] * _shift_up(src, base, ts, j, 0)
                out_ref[pl.ds(base, ts), :] = dx.astype(BF16)
                taps = tuple(carry[k] + _colsum(d_here * _shift_down(pad, base, ts, 2 - k)) for k in range(3))
                return taps + (carry[3] + _colsum(d_here),)

            z1 = jnp.zeros((1, tc), F32)
            red = lax.fori_loop(0, s // ts, chunk, (z1, z1, z1, z1))
            red_ref[...] = jnp.concatenate(list(red) + [jnp.zeros((4, tc), F32)], axis=0)

        @pl.when(half == 0)
        def _():
            conv_t(dgs, gpad, wg, dup_ref, red_ref)

        @pl.when(half == 1)
        def _():
            conv_t(dvs, vpad, wv, dup_ref, red_ref)

    gcol = pl.BlockSpec((s, tc), lambda b, h: (0, b))
    vcol = pl.BlockSpec((s, tc), lambda b, h: (0, b + nfb))
    gpar = pl.BlockSpec((SUBLANES, tc), lambda b, h: (0, b))
    vpar = pl.BlockSpec((SUBLANES, tc), lambda b, h: (0, b + nfb))
    return pl.pallas_call(
        body, name="ffn_act_bwd", grid=(nfb, 2),
        in_specs=[gcol, vcol, gcol, gpar, vpar],
        out_specs=[pl.BlockSpec((s, tc), lambda b, h: (0, b + h * nfb)),
                   pl.BlockSpec((SUBLANES, tc), lambda b, h: (0, b + h * nfb))],
        out_shape=[jax.ShapeDtypeStruct((s, f2), BF16), jax.ShapeDtypeStruct((SUBLANES, f2), F32)],
        scratch_shapes=[pltpu.VMEM((s + SUBLANES, tc), F32)] * 4,
        compiler_params=_cparams(("arbitrary", "arbitrary")),
    )(up, up, dact, fp, fp)


def _rows8(rows, width):
    rows = [r.reshape(1, width).astype(F32) for r in rows]
    return jnp.concatenate(rows + [jnp.zeros((SUBLANES - len(rows), width), F32)], axis=0)


def _block_diag(w):
    n, b, _ = w.shape
    w = w.reshape(n // 2, 2, b, b)
    z = jnp.zeros((n // 2, b, b), w.dtype)
    top = jnp.concatenate([w[:, 0], z], axis=2)
    bot = jnp.concatenate([z, w[:, 1]], axis=2)
    return jnp.concatenate([top, bot], axis=1)


def _block_diag_t(bd):
    n, b2, _ = bd.shape
    b = b2 // 2
    return jnp.stack([bd[:, :b, :b], bd[:, b:, b:]], axis=1).reshape(2 * n, b, b)


def _local_step(x, mod, positions, target, w_in, fetch, sm, emit):
    s, d = x.shape
    o_rnn, o_mla = D_RNN, D_RNN + Q_LORA + KV_LORA + QK_ROPE
    wts = {}
    w_in_rnn = w_in[:, :o_rnn]
    w_in_mla = jnp.concatenate([w_in[:, o_rnn:o_mla], jnp.zeros((d, MLA_W - (o_mla - o_rnn)), w_in.dtype)], axis=1)
    w_in_g = w_in[:, o_mla:]
    hd = QK_NOPE + QK_ROPE
    wa_bd = _block_diag(sm["w_gate_a"]).astype(BF16)
    wx_bd = _block_diag(sm["w_gate_x"]).astype(BF16)

    pos = positions.reshape(s)
    half = QK_ROPE // 2
    inv_freq = ROPE_THETA ** (-jnp.arange(half, dtype=F32) / half)
    ang = pos.astype(F32)[:, None] * inv_freq
    cos, sin = jnp.cos(ang), jnp.sin(ang)
    rot_c = jnp.concatenate([jnp.ones((s, QK_NOPE), F32), cos, cos, jnp.ones((s, LANES - hd), F32)], axis=1)
    rot_s = jnp.concatenate([jnp.zeros((s, QK_NOPE), F32), -sin, sin, jnp.zeros((s, LANES - hd), F32)], axis=1)
    keep = (pos != 0).astype(F32).reshape(s, 1)

    gmod1 = _rows8([sm["norm1_g"], mod[1], mod[0]], d)
    gmod2 = _rows8([sm["norm2_g"], mod[4], mod[3], mod[2]], d)
    rp = jnp.concatenate([sm["conv_w"].reshape(4, D_RNN), _rows8([sm["conv_b"], sm["b_gate_a"], sm["b_gate_x"], sm["lru_param"]], D_RNN)[:4]], axis=0)
    fp = _rows8([sm["ffn_conv_w"][0], sm["ffn_conv_w"][1], sm["ffn_conv_w"][2], sm["ffn_conv_b"]], 2 * D_FF)
    ng = _rows8([jnp.concatenate([sm["q_norm_g"].reshape(-1), sm["kv_norm_g"].reshape(-1), jnp.zeros((MLA_W - Q_LORA - KV_LORA,), F32)])], MLA_W)
    fpar = _rows8([mod[5], sm["final_g"]], d)

    h = _norm_mod_fwd(x, gmod1, "norm1_fwd")
    proj_rnn = _mm(h, w_in_rnn, name="mm_in_rnn")
    proj_mla = _mm(h, w_in_mla, name="mm_in_mla")
    proj_g = _mm(h, w_in_g, out_dtype=BF16, name="mm_in_g")
    xc, ra, ix, hs = _rnn_fwd(proj_rnn, keep, rp, wa_bd, wx_bd)
    qn, kvn, kr = _mla_prep_fwd(proj_mla, rot_c, rot_s, ng)
    wts.update(fetch(("w_uq", "w_ukv"), kr))
    w_uq_p = jnp.pad(wts["w_uq"].reshape(Q_LORA, N_HEADS, hd), ((0, 0), (0, 0), (0, LANES - hd))).reshape(Q_LORA, N_HEADS * LANES)
    w_ukv = wts["w_ukv"]
    q_rot, q_rot_t = _rope_fwd_t(_mm(qn, w_uq_p, name="mm_uq"), rot_c, rot_s)
    kv, kvt = _mm(kvn, w_ukv, out_dtype=BF16, also_t=BF16, name="mm_ukv")
    krt = jnp.transpose(kr)
    y_mla, lse = _att_fwd(q_rot, kv, kvt, krt)
    wts.update(fetch(("w_proj_rnn", "w_proj_mla", "w_out", "w_up", "w_down"), lse))
    pr = _mm(hs, wts["w_proj_rnn"], out_dtype=BF16, name="mm_proj_rnn")
    pm = _mm(y_mla, wts["w_proj_mla"], out_dtype=BF16, name="mm_proj_mla")
    merged = _merge_fwd(pr, pm, proj_g)
    o = _mm(merged, wts["w_out"], name="mm_out")
    x1, h2 = _resid_norm_fwd(x, o, gmod2)
    up = _mm(h2, wts["w_up"], name="mm_up")
    act = _ffn_act_fwd(up, fp)
    dn = _mm(act, wts["w_down"], name="mm_down")

    dx2, ddn, red_f = _final_fwd_bwd(x1, dn, target, fpar)
    dact = _mm(ddn, wts["w_down"], tb=True, name="mm_d_act")
    tok = emit("w_down", _mm(act, ddn, ta=True, out_dtype=BF16, name="mm_dw_down"))
    dup, red_ffn = _ffn_act_bwd(up, dact, fp + tok)
    dh2 = _mm(dup, wts["w_up"], tb=True, name="mm_d_h2")
    tok = tok + emit("w_up", _mm(dup, h2, ta=True, out_dtype=BF16, name="mm_dw_up"))
    dx1, do, red_2 = _norm2_bwd(x1, dh2, dx2, o, gmod2 + tok)
    dmerged = _mm(do, wts["w_out"], tb=True, name="mm_d_merged")
    tok = tok + emit("w_out", _mm(merged, do, ta=True, out_dtype=BF16, name="mm_dw_out"))
    dpr, dpm, dg = _merge_bwd(dmerged, pr, pm, proj_g)
    dy_rnn = _mm(dpr, wts["w_proj_rnn"], tb=True, name="mm_d_yrnn")
    tok = tok + emit("w_proj_rnn", _mm(hs, dpr, ta=True, out_dtype=BF16, name="mm_dw_proj_rnn"))
    dy_mla, dy_mla_t = _mm(dpm, wts["w_proj_mla"], tb=True, also_t=BF16, name="mm_d_ymla")
    tok = tok + emit("w_proj_mla", _mm(y_mla, dpm, ta=True, out_dtype=BF16, name="mm_dw_proj_mla"))
    dq_rot, dkvt, dkrt = _att_bwd(q_rot, q_rot_t, kv, kvt, kr, krt, y_mla, lse, dy_mla, dy_mla_t)
    dq = _rope_bwd(dq_rot, rot_c, rot_s)
    dqn = _mm(dq, w_uq_p, tb=True, name="mm_d_qn")
    dw_uq_pt = _mm(dq, qn, ta=True, out_dtype=BF16, name="mm_dw_uq")
    tok = tok + emit("w_uq", dw_uq_pt.reshape(N_HEADS, LANES, Q_LORA)[:, :hd].reshape(N_HEADS * hd, Q_LORA))
    dkvn = jnp.transpose(_mm(w_ukv, dkvt, name="mm_d_kvn"))
    tok = tok + emit("w_ukv", _mm(dkvt, kvn, out_dtype=BF16, name="mm_dw_ukv"))
    dproj_mla, red_m = _mla_prep_bwd(proj_mla, dqn, dkvn, jnp.transpose(dkrt), rot_c, rot_s, ng + tok)
    dx_rnn, dwa_bd, dwx_bd, red_r = _rnn_bwd(proj_rnn, xc, ra, ix, hs, dy_rnn, keep, rp + tok, wa_bd, wx_bd)
    dw_in_t = jnp.concatenate([
        _mm(dx_rnn, h, ta=True, out_dtype=BF16, name="mm_dw_in_rnn"),
        _mm(dproj_mla, h, ta=True, out_dtype=BF16, name="mm_dw_in_mla")[:o_mla - o_rnn],
        _mm(dg, h, ta=True, out_dtype=BF16, name="mm_dw_in_g")], axis=0)
    tok = tok + emit("w_in", dw_in_t)
    dh_a = _mm(dx_rnn, w_in_rnn, tb=True, name="mm_d_h_rnn")
    dh_b = _mm(dproj_mla, w_in_mla, tb=True, name="mm_d_h_mla")
    dh_c = _mm(dg, w_in_g, tb=True, name="mm_d_h_g")
    grad_x, red_1 = _norm1_bwd(x, dh_a, dh_b, dh_c, dx1, gmod1 + tok)

    gs = {
        "norm1_g": red_1[0], "conv_w": red_r[0:4], "conv_b": red_r[4], "w_gate_a": _block_diag_t(dwa_bd),
        "b_gate_a": red_r[5], "w_gate_x": _block_diag_t(dwx_bd), "b_gate_x": red_r[6], "lru_param": red_r[7],
        "q_norm_g": red_m[0, :Q_LORA], "kv_norm_g": red_m[0, Q_LORA:Q_LORA + KV_LORA], "norm2_g": red_2[0],
        "ffn_conv_w": red_ffn[0:3], "ffn_conv_b": red_ffn[3], "final_g": red_f[0],
    }
    dmod = jnp.stack([red_1[2], red_1[1], red_2[3], red_2[2], red_2[1], red_f[1]], axis=0)
    return red_f[2, 0], grad_x, gs, dmod


MESH_IDS = pl.DeviceIdType.MESH
HBM_SPEC = pl.BlockSpec(memory_space=pltpu.HBM)


def _my_slot():
    return 4 * lax.axis_index("x") + 2 * lax.axis_index("y") + lax.axis_index("c")


def _all_gather(arrs, name):
    n = len(arrs)

    def body(*refs):
        ins, outs = refs[:n], refs[n:2 * n]
        send_sems, recv_sems, local_sems = refs[2 * n:]
        x, y, c = lax.axis_index("x"), lax.axis_index("y"), lax.axis_index("c")
        me, sibling = (x, y, c), (x, y, 1 - c)
        chips = [(1 - x, y), (x, 1 - y), (1 - x, 1 - y)]

        def slot(dev):
            return 4 * dev[0] + 2 * dev[1] + dev[2]

        def copy(a, k, block, to, src=None):
            dst = outs[a].at[slot(block)]
            return pltpu.make_async_remote_copy(
                src_ref=dst if src is None else src, dst_ref=dst, send_sem=send_sems.at[a, k], recv_sem=recv_sems.at[a, k],
                device_id=to, device_id_type=MESH_IDS)

        mine = [pltpu.make_async_copy(ins[a], outs[a].at[slot(me)], local_sems.at[a]) for a in range(n)]
        for cp in mine:
            cp.start()
        first = []
        for a in range(n):
            first.append(copy(a, 0, me, sibling, src=ins[a]))
            first += [copy(a, 1 + j, me, (*chip, c), src=ins[a]) for j, chip in enumerate(chips)]
        for cp in first:
            cp.start()
        passed = []
        for j, chip in enumerate(chips):
            for a in range(n):
                copy(a, 1 + j, (*chip, c), me).wait_recv()
                fwd = copy(a, 4 + j, (*chip, c), sibling)
                fwd.start()
                passed.append(fwd)
        for a in range(n):
            copy(a, 0, sibling, me).wait_recv()
            for j, chip in enumerate(chips):
                copy(a, 4 + j, (*chip, 1 - c), me).wait_recv()
        for cp in first + passed:
            cp.wait_send()
        for cp in mine:
            cp.wait()

    return pl.pallas_call(
        body, name=name,
        in_specs=[HBM_SPEC] * n, out_specs=[HBM_SPEC] * n,
        out_shape=[jax.ShapeDtypeStruct((N_DEV,) + a.shape, a.dtype) for a in arrs],
        scratch_shapes=[pltpu.SemaphoreType.DMA((n, 7)), pltpu.SemaphoreType.DMA((n, 7)), pltpu.SemaphoreType.DMA((n,))],
    )(*arrs)


def _assemble_columns(g, name):
    k, r, c = g.shape
    tr = _pick(r, (256, 128, 64, 32, 16))

    def body(g_ref, o_ref):
        for s in range(k):
            o_ref[:, s * c:(s + 1) * c] = g_ref[s]

    return pl.pallas_call(
        body, name=name, grid=(r // tr,),
        in_specs=[pl.BlockSpec((k, tr, c), lambda i: (0, i, 0))],
        out_specs=pl.BlockSpec((tr, k * c), lambda i: (i, 0)),
        out_shape=jax.ShapeDtypeStruct((r, k * c), g.dtype),
        compiler_params=_cparams(("arbitrary",)),
    )(g)


SEM_SPEC =pl.BlockSpec(memory_space=pltpu.SEMAPHORE)
DATAFLOW = pltpu.SideEffectType.DATAFLOW_SIDE_EFFECTING
FLIPS = [(dx, dy, dc) for dx in (0, 1) for dy in (0, 1) for dc in (0, 1)][1:]


def _peer(k):
    dx, dy, dc = FLIPS[k]
    peer = (lax.axis_index("x") ^ dx, lax.axis_index("y") ^ dy, lax.axis_index("c") ^ dc)
    return peer, 4 * peer[0] + 2 * peer[1] + peer[2]


def _gather_start(shards, after, name):
    n, nf = len(shards), len(FLIPS)

    def body(*refs):
        srcs, lands = refs[:n], refs[n:2 * n]
        send_sems, recv_sems = refs[2 * n + 1:3 * n + 1], refs[3 * n + 1:4 * n + 1]
        token = refs[-1]
        me = _my_slot()
        for a in range(n):
            for k in range(nf):
                peer, _ = _peer(k)
                pltpu.make_async_remote_copy(
                    src_ref=srcs[a], dst_ref=lands[a].at[me], send_sem=send_sems[a].at[k], recv_sem=recv_sems[a].at[k],
                    device_id=peer, device_id_type=MESH_IDS).start()
        token[...] = jnp.zeros(token.shape, F32)

    land_shapes = [(N_DEV,) + a.shape for a in shards]
    sems = [pltpu.SemaphoreType.DMA((nf,))] * n
    out = pl.pallas_call(
        body, name=name,
        out_shape=(*sems, *sems, *[pltpu.HBM(a.shape, a.dtype) for a in shards],
                   *[pltpu.HBM(shp, a.dtype) for shp, a in zip(land_shapes, shards)],
                   jax.ShapeDtypeStruct((SUBLANES, LANES), F32)),
        in_specs=[HBM_SPEC] * (2 * n) + [pl.BlockSpec(memory_space=pl.ANY)],
        out_specs=(*[SEM_SPEC] * (2 * n), *[HBM_SPEC] * (2 * n), pl.BlockSpec(memory_space=pltpu.VMEM)),
        input_output_aliases={i: 2 * n + i for i in range(2 * n)},
        compiler_params=pltpu.CompilerParams(has_side_effects=DATAFLOW),
    )(*[pltpu.with_memory_space_constraint(a, pltpu.HBM) for a in shards],
      *[pltpu.with_memory_space_constraint(lax.empty(shp, a.dtype), pltpu.HBM) for shp, a in zip(land_shapes, shards)],
      after)
    return [(out[a], out[n + a], out[2 * n + a], out[3 * n + a]) for a in range(n)], out[-1]


def _gather_wait(flights, after, name):
    n, nf = len(flights), len(FLIPS)

    def body(*refs):
        send_sems, recv_sems = refs[:n], refs[n:2 * n]
        srcs, lands = refs[2 * n:3 * n], refs[3 * n:4 * n]
        for a in range(n):
            for k in range(nf):
                peer, peer_slot = _peer(k)
                cp = pltpu.make_async_remote_copy(
                    src_ref=srcs[a], dst_ref=lands[a].at[peer_slot], send_sem=send_sems[a].at[k],
                    recv_sem=recv_sems[a].at[k], device_id=peer, device_id_type=MESH_IDS)
                cp.wait_send()
                cp.wait_recv()

    srcs, lands = [f[2] for f in flights], [f[3] for f in flights]
    out = pl.pallas_call(
        body, name=name,
        out_shape=(*[pltpu.HBM(a.shape, a.dtype) for a in srcs], *[pltpu.HBM(a.shape, a.dtype) for a in lands]),
        in_specs=[SEM_SPEC] * (2 * n) + [HBM_SPEC] * (2 * n) + [pl.BlockSpec(memory_space=pl.ANY)],
        out_specs=tuple([HBM_SPEC] * (2 * n)),
        input_output_aliases={2 * n + i: i for i in range(2 * n)},
        compiler_params=pltpu.CompilerParams(has_side_effects=DATAFLOW),
    )(*[f[0] for f in flights], *[f[1] for f in flights], *srcs, *lands, after)
    return list(out[n:])


ROW_ALIGN = 16


def _span_start(slot, rows):
    return (rows * slot) // ROW_ALIGN * ROW_ALIGN


def _chunk_of(src_ref, slot, rows, span):
    if rows is None:
        return src_ref.at[slot]
    return src_ref.at[pl.ds(pl.multiple_of(_span_start(slot, rows), ROW_ALIGN), span)]


def _scatter_start(src, name, rows=None, span=None):
    def body(src_ref, land_ref, send_sems, recv_sems, src_thru, land_thru, token):
        me = _my_slot()
        for k in range(len(FLIPS)):
            peer, peer_slot = _peer(k)
            pltpu.make_async_remote_copy(
                src_ref=_chunk_of(src_ref, peer_slot, rows, span), dst_ref=land_ref.at[me], send_sem=send_sems.at[k],
                recv_sem=recv_sems.at[k], device_id=peer, device_id_type=MESH_IDS).start()
        token[...] = jnp.zeros(token.shape, F32)

    n = len(FLIPS)
    land_shape = src.shape if rows is None else (N_DEV, span, src.shape[1])
    return pl.pallas_call(
        body, name=name,
        out_shape=(pltpu.SemaphoreType.DMA((n,)), pltpu.SemaphoreType.DMA((n,)), pltpu.HBM(src.shape, src.dtype),
                   pltpu.HBM(land_shape, src.dtype), jax.ShapeDtypeStruct((SUBLANES, LANES), F32)),
        in_specs=(HBM_SPEC, HBM_SPEC),
        out_specs=(SEM_SPEC, SEM_SPEC, HBM_SPEC, HBM_SPEC, pl.BlockSpec(memory_space=pltpu.VMEM)),
        input_output_aliases={0: 2, 1: 3},
        compiler_params=pltpu.CompilerParams(has_side_effects=DATAFLOW),
    )(pltpu.with_memory_space_constraint(src, pltpu.HBM),
      pltpu.with_memory_space_constraint(lax.empty(land_shape, src.dtype), pltpu.HBM))


def _scatter_wait(send_sems, recv_sems, src_thru, land_thru, after, name, rows=None, span=None):
    def body(src_ref, land_ref, send_sems, recv_sems, after_ref, src_dead, got_ref):
        for k in range(len(FLIPS)):
            peer, peer_slot = _peer(k)
            cp = pltpu.make_async_remote_copy(
                src_ref=_chunk_of(src_ref, peer_slot, rows, span), dst_ref=land_ref.at[peer_slot], send_sem=send_sems.at[k],
                recv_sem=recv_sems.at[k], device_id=peer, device_id_type=MESH_IDS)
            cp.wait_send()
            cp.wait_recv()

    return pl.pallas_call(
        body, name=name,
        out_shape=(pltpu.HBM(src_thru.shape, src_thru.dtype), pltpu.HBM(land_thru.shape, land_thru.dtype)),
        in_specs=(HBM_SPEC, HBM_SPEC, SEM_SPEC, SEM_SPEC, pl.BlockSpec(memory_space=pl.ANY)),
        out_specs=(HBM_SPEC, HBM_SPEC), input_output_aliases={0: 0, 1: 1},
        compiler_params=pltpu.CompilerParams(has_side_effects=DATAFLOW),
    )(src_thru, land_thru, send_sems, recv_sems, after)


def _sum_sources(parts, own, name):
    k, r, c = parts.shape
    tr = r if k * r * c <= 2 * 1024 * 1024 else _pick(r, (512, 256, 128, 64, 32, 16, 8))

    def body(p_ref, own_ref, o_ref):
        me = _my_slot()
        acc = jnp.where(me == 0, own_ref[...], p_ref[0]).astype(F32)
        for s in range(1, k):
            acc = acc + jnp.where(me == s, own_ref[...], p_ref[s]).astype(F32)
        o_ref[...] = acc

    blk = pl.BlockSpec((tr, c), lambda i: (i, 0))
    return pl.pallas_call(
        body, name=name, grid=(r // tr,),
        in_specs=[pl.BlockSpec((k, tr, c), lambda i: (0, i, 0)), blk],
        out_specs=blk,
        out_shape=jax.ShapeDtypeStruct((r, c), F32),
        compiler_params=_cparams(("arbitrary",)),
    )(parts, own)


def _adamw_math(g, w, m, v):
    m_new = ADAM_B1 * m + (1.0 - ADAM_B1) * g
    v_new = ADAM_B2 * v + (1.0 - ADAM_B2) * jnp.square(g)
    m_hat = m_new / (1.0 - ADAM_B1 ** ADAM_STEP)
    v_hat = v_new / (1.0 - ADAM_B2 ** ADAM_STEP)
    return -ADAM_LR * (m_hat / (jnp.sqrt(v_hat) + ADAM_EPS) + ADAM_WD * w), m_new, v_new


def _adamw_many(gs, ws, ms, vs, name):
    n = len(gs)

    def body(*refs):
        ins, outs = refs[:4 * n], refs[4 * n:]
        for i in range(n):
            g = ins[i][...]
            outs[4 * i][...] = g
            outs[4 * i + 1][...], outs[4 * i + 2][...], outs[4 * i + 3][...] = _adamw_math(
                g, ins[n + i][...], ins[2 * n + i][...], ins[3 * n + i][...])

    return pl.pallas_call(
        body, name=name,
        out_shape=[jax.ShapeDtypeStruct(a.shape, F32) for a in ws for _ in range(4)],
        compiler_params=_cparams(),
    )(*gs, *ws, *ms, *vs)


def _adamw(parts, w, m, v, name, own=None):
    k, r, c = parts.shape
    tr = r if r * c <= 256 * 1024 else _pick(r, (256, 128, 64, 32, 16, 8))

    def body(*refs):
        p_ref, w_ref, m_ref, v_ref = refs[:4]
        g_ref, d_ref, nm_ref, nv_ref = refs[-4:]

        def part(s):
            if own is None:
                return p_ref[s].astype(F32)
            return jnp.where(_my_slot() == s, refs[4][...], p_ref[s]).astype(F32)

        g = part(0)
        for s in range(1, k):
            g = g + part(s)
        g_ref[...] = g
        d_ref[...], nm_ref[...], nv_ref[...] = _adamw_math(g, w_ref[...], m_ref[...], v_ref[...])

    blk = pl.BlockSpec((tr, c), lambda i: (i, 0))
    return pl.pallas_call(
        body, name=name, grid=(r // tr,),
        in_specs=[pl.BlockSpec((k, tr, c), lambda i: (0, i, 0)), blk, blk, blk] + ([] if own is None else [blk]),
        out_specs=[blk] * 4,
        out_shape=[jax.ShapeDtypeStruct((r, c), F32)] * 4,
        compiler_params=_cparams(("arbitrary",)),
    )(parts, w, m, v, *([] if own is None else [own]))


def _silu(v):
    return v * _sigmoid(v)


def _ada_fwd(c_all, w, b):
    def body(c_ref, w_ref, b_ref, o_ref):
        ca = _silu(c_ref[...]).astype(BF16)
        o_ref[...] = jnp.dot(ca, w_ref[...].astype(BF16), preferred_element_type=F32) + b_ref[...]

    return pl.pallas_call(
        body, name="ada_fwd", out_shape=jax.ShapeDtypeStruct((c_all.shape[0], w.shape[1]), F32),
        compiler_params=_cparams(),
    )(c_all, w, b)


def _ada_bwd(c_all, dmod):
    def body(c_ref, d_ref, o_ref):
        ca = _silu(c_ref[...]).astype(BF16).astype(F32)
        dm = d_ref[...].astype(BF16).astype(F32)
        acc = jnp.zeros(o_ref.shape, F32)
        for bi in range(c_all.shape[0]):
            acc = acc + jnp.transpose(ca[bi:bi + 1, :]) * dm[bi:bi + 1, :]
        o_ref[...] = acc

    return pl.pallas_call(
        body, name="ada_bwd", out_shape=jax.ShapeDtypeStruct((c_all.shape[1], dmod.shape[1]), F32),
        compiler_params=_cparams(),
    )(c_all, dmod)


COL_SHARDED = ("w_in", "w_uq", "w_ukv", "w_up")
ROW_SHARDED = ("w_proj_rnn", "w_proj_mla", "w_out", "w_down")
REPLICATED = ("b_ada", "norm1_g", "conv_b", "w_gate_a", "b_gate_a", "w_gate_x", "b_gate_x", "lru_param", "q_norm_g",
              "kv_norm_g", "norm2_g", "ffn_conv_b", "final_g")
WEIGHTS = ("w_ada", "b_ada", "norm1_g", "w_in", "conv_w", "conv_b", "w_gate_a", "b_gate_a", "w_gate_x", "b_gate_x",
           "lru_param", "q_norm_g", "w_uq", "kv_norm_g", "w_ukv", "w_proj_rnn", "w_proj_mla", "w_out", "norm2_g", "w_up",
           "ffn_conv_w", "ffn_conv_b", "w_down", "final_g")
TRANSPOSED_GRADS = COL_SHARDED
PACK_LANES = 128


def _pack(vecs, row_multiple=SUBLANES):
    flat = jnp.concatenate([v.reshape(-1).astype(F32) for v in vecs])
    pad = (-flat.shape[0]) % (PACK_LANES * row_multiple)
    return jnp.concatenate([flat, jnp.zeros((pad,), F32)]).reshape(-1, PACK_LANES)


def _unpack(packed, shapes):
    flat = packed.reshape(-1)
    out, off = [], 0
    for shp in shapes:
        size = math.prod(shp)
        out.append(flat[off:off + size].reshape(shp))
        off += size
    return out


def kernel(x, c, positions, w_ada, b_ada, norm1_g, w_in, conv_w, conv_b, w_gate_a, b_gate_a, w_gate_x, b_gate_x, lru_param, q_norm_g, w_uq, kv_norm_g, w_ukv, w_proj_rnn, w_proj_mla, w_out, norm2_g, w_up, ffn_conv_w, ffn_conv_b, w_down, final_g, loss_target, m_w_ada, m_b_ada, m_norm1_g, m_w_in, m_conv_w, m_conv_b, m_w_gate_a, m_b_gate_a, m_w_gate_x, m_b_gate_x, m_lru_param, m_q_norm_g, m_w_uq, m_kv_norm_g, m_w_ukv, m_w_proj_rnn, m_w_proj_mla, m_w_out, m_norm2_g, m_w_up, m_ffn_conv_w, m_ffn_conv_b, m_w_down, m_final_g, v_w_ada, v_b_ada, v_norm1_g, v_w_in, v_conv_w, v_conv_b, v_w_gate_a, v_b_gate_a, v_w_gate_x, v_b_gate_x, v_lru_param, v_q_norm_g, v_w_uq, v_kv_norm_g, v_w_ukv, v_w_proj_rnn, v_w_proj_mla, v_w_out, v_norm2_g, v_w_up, v_ffn_conv_w, v_ffn_conv_b, v_w_down, v_final_g):
    args = dict(locals())
    w = {n: args[n] for n in WEIGHTS}
    m = {n: args["m_" + n] for n in WEIGHTS}
    v = {n: args["v_" + n] for n in WEIGHTS}
    s, d = x.shape[1], x.shape[2]
    me = _my_slot()
    def two_d(a):
        assert a.ndim == 3 and a.shape[0] == 1, a.shape
        return a[0]

    big = COL_SHARDED + ROW_SHARDED
    shard = {n: two_d(w[n]).astype(BF16) for n in big}

    def whole(n, g):
        k, r, cc = g.shape
        return _assemble_columns(g, "assemble_" + n) if n in COL_SHARDED else g.reshape(k * r, cc)

    first = _all_gather([shard["w_in"], c, two_d(conv_w), two_d(ffn_conv_w)], "gather_first")
    c_all = first[1].reshape(N_DEV, d)
    conv_w_all = jnp.transpose(first[2], (1, 0, 2)).reshape(conv_w.shape[1], -1)
    ffn_conv_w_all = jnp.transpose(first[3], (1, 0, 2)).reshape(ffn_conv_w.shape[1], -1)

    ada_cols = w_ada.shape[2]
    b_cols = lax.dynamic_slice(b_ada, (0, me * ada_cols), (1, ada_cols))
    mod_cols = _ada_fwd(c_all, w_ada[0], b_cols)
    mod_all, = _all_gather([mod_cols], "gather_mod")

    later = ("w_uq", "w_ukv", "w_proj_rnn", "w_proj_mla", "w_out", "w_up", "w_down")
    flights, started = _gather_start([shard[n] for n in later], mod_all, "gather_start")
    flight = dict(zip(later, flights))

    def fetch(names, after):
        lands = _gather_wait([flight[n] for n in names], after, "gather_wait_" + names[0])
        return {n: whole(n, lax.dynamic_update_index_in_dim(g, shard[n], me, 0)) for n, g in zip(names, lands)}

    mod = lax.dynamic_index_in_dim(mod_all, me, axis=1, keepdims=False).reshape(6, d) + started[0, 0]

    sm = {n: w[n][0] for n in REPLICATED if n not in ("b_ada", "final_g")}
    sm["final_g"] = final_g
    sm["conv_w"] = conv_w_all
    sm["ffn_conv_w"] = ffn_conv_w_all
    in_flight, windows = {}, {}

    def emit(n, g):
        rows = g.shape[0] // N_DEV
        if rows % ROW_ALIGN == 0:
            windows[n] = (None, None)
            g = g.reshape(N_DEV, rows, g.shape[1])
        else:
            span = max(rows * k - _span_start(k, rows) for k in range(N_DEV)) + rows
            windows[n] = (rows, -(-span // ROW_ALIGN) * ROW_ALIGN)
            assert _span_start(N_DEV - 1, rows) + windows[n][1] <= g.shape[0], (n, g.shape)
        *in_flight[n], token = _scatter_start(g, "scatter_start_" + n, *windows[n])
        return token[0, 0]

    sq, grad_x, gs, dmod = _local_step(x[0], mod, positions[0], loss_target[0], whole("w_in", first[0]), fetch, sm, emit)

    small_names = [n for n in REPLICATED if n != "b_ada"] + ["conv_w", "ffn_conv_w"]
    small_shapes = [gs[n].shape for n in small_names] + [(6 * d,), (1,)]
    partial = _pack([gs[n] for n in small_names] + [dmod, sq.reshape(1)], N_DEV * SUBLANES)
    *small_flight, small_started = _scatter_start(partial.reshape(N_DEV, -1, PACK_LANES), "scatter_small_start")

    grads, deltas, new_m, new_v = {}, {}, {}, {}

    def update(n, parts, own=None):
        shp = w[n].shape
        lay = jnp.transpose if n in TRANSPOSED_GRADS else (lambda a: a)
        res = _adamw(parts, lay(two_d(w[n])), lay(two_d(m[n])), lay(two_d(v[n])), "adamw_" + n, own)
        grads[n], deltas[n], new_m[n], new_v[n] = [lay(a).reshape(shp) for a in res]

    for n in big:
        rows, span = windows[n]
        src, landed = _scatter_wait(*in_flight[n], small_started, "scatter_wait_" + n, rows, span)
        if rows is None:
            update(n, landed, lax.dynamic_index_in_dim(src, me, axis=0, keepdims=False))
        else:
            start = _span_start(me, rows)
            own = lax.dynamic_slice(src, (start, 0), (span, src.shape[1]))
            total = _sum_sources(landed, own, "sum_" + n)
            update(n, lax.dynamic_slice(total, (rows * me - start, 0), (rows, src.shape[1]))[None])

    chunks, landed = _scatter_wait(*small_flight, new_v[big[-1]], "scatter_small_wait")
    mine = _sum_sources(landed, lax.dynamic_index_in_dim(chunks, me, axis=0, keepdims=False), "sum_small")
    summed_all, dmod_all = _all_gather([mine, dmod.reshape(1, 6 * d)], "gather_small")
    summed = _unpack(summed_all, small_shapes)
    g_small = dict(zip(small_names, summed[:len(small_names)]))
    g_small["b_ada"] = summed[len(small_names)]
    loss = 0.5 * summed[-1][0] / d
    dmod_cols = lax.dynamic_slice(dmod_all.reshape(N_DEV, 6 * d), (0, me * ada_cols), (N_DEV, ada_cols))

    update("w_ada", _ada_bwd(c_all, dmod_cols)[None])

    for n in ("conv_w", "ffn_conv_w"):
        cols = w[n].shape[2]
        g_small[n] = lax.dynamic_slice(g_small[n], (0, me * cols), (g_small[n].shape[0], cols))
    small = REPLICATED + ("conv_w", "ffn_conv_w")
    as_rows = lambda a: a.reshape(1, -1) if a.ndim == 1 else a
    res = _adamw_many([as_rows(g_small[n].reshape(w[n].shape)) for n in small], [as_rows(w[n]) for n in small],
                      [as_rows(m[n]) for n in small], [as_rows(v[n]) for n in small], "adamw_small")
    for i, n in enumerate(small):
        grads[n], deltas[n], new_m[n], new_v[n] = [a.reshape(w[n].shape) for a in res[4 * i:4 * i + 4]]

    return (loss, grad_x[None], *[grads[n] for n in WEIGHTS], *[deltas[n] for n in WEIGHTS],
            *[new_m[n] for n in WEIGHTS], *[new_v[n] for n in WEIGHTS])
```

```python
import functools
import math

import jax
import jax.numpy as jnp
from jax import lax
from jax.experimental import pallas as pl
from jax.experimental.pallas import tpu as pltpu

F32 = jnp.float32
BF16 = jnp.bfloat16

N_DEV = 8
LANES = 128
SUBLANES = 8
VMEM_LIMIT = 56 * 1024 * 1024

D_RNN = 1280
Q_LORA = 384
KV_LORA = 256
QK_NOPE = 64
QK_ROPE = 32
V_HEAD = 64
N_HEADS = 16
D_FF = 2816
ROPE_THETA = 10000.0
LRU_C = 8.0
EPS = 1e-6
MLA_W = 768
ATT_SCALE = 1.0 / math.sqrt(QK_NOPE + QK_ROPE)

ADAM_LR, ADAM_B1, ADAM_B2, ADAM_EPS, ADAM_WD, ADAM_STEP = 0.001, 0.9, 0.999, 1e-08, 0.01, 10


def _cparams(sem=None):
    return pltpu.CompilerParams(dimension_semantics=sem, vmem_limit_bytes=VMEM_LIMIT)


def _pick(n, prefs):
    for p in prefs:
        if n % p == 0:
            return p
    return n


def _sigmoid(v):
    return 0.5 * jnp.tanh(0.5 * v) + 0.5


def _lane(shape):
    return lax.broadcasted_iota(jnp.int32, shape, len(shape) - 1)


def _row(shape):
    return lax.broadcasted_iota(jnp.int32, shape, len(shape) - 2)


MM_BLOCK_BYTES = 36 * 1024 * 1024


def _divisors(n):
    return [t for t in range(n, 0, -LANES) if n % t == 0] if n % LANES == 0 else [n]


HBM_BYTES_PER_US = 3.0e6
MXU_FLOPS_PER_US = 8.0e8
GRID_STEP_US = 0.35


def _mm_tiles(m, n, k, a_bytes, b_bytes, o_bytes):
    best = None
    for tm in [t for t in _divisors(m) if t <= 1024]:
        for tn in [t for t in _divisors(n) if t <= 2048]:
            for tk in _divisors(k):
                nk = k // tk
                need = 2 * (tm * tk * a_bytes + tk * tn * b_bytes + tm * tn * o_bytes) + (tm * tn * 4 if nk > 1 else 0)
                if need > MM_BLOCK_BYTES:
                    continue
                gi, gj = m // tm, n // tn
                for rows_outer in (True, False):
                    if nk > 1:
                        a_reads, b_reads = gj, gi
                    elif rows_outer:
                        a_reads, b_reads = 1, (gi if gj > 1 else 1)
                    else:
                        a_reads, b_reads = (gj if gi > 1 else 1), 1
                    traffic = m * k * a_bytes * a_reads + k * n * b_bytes * b_reads + m * n * (o_bytes + (8 * nk if nk > 1 else 0))
                    cost = max(traffic / HBM_BYTES_PER_US, 2.0 * m * n * k / MXU_FLOPS_PER_US) + gi * gj * nk * GRID_STEP_US
                    if best is None or cost < best[0]:
                        best = (cost, tm, tn, tk, rows_outer)
                break
    if best is None:
        raise ValueError((m, n, k))
    return best[1:]


def _mm(a, b, *, ta=False, tb=False, out_dtype=F32, also_t=None, name):
    (k_a, m) = a.shape if ta else a.shape[::-1]
    (n, k_b) = b.shape if tb else b.shape[::-1]
    assert k_a == k_b, (a.shape, b.shape, ta, tb)
    k = k_a
    tm, tn, tk, rows_outer = _mm_tiles(m, n, k, a.dtype.itemsize, b.dtype.itemsize, jnp.dtype(out_dtype).itemsize)
    nk = k // tk
    dims = (((0 if ta else 1,), (1 if tb else 0,)), ((), ()))
    n_out = 1 if also_t is None else 2

    def body(a_ref, b_ref, *rest):
        outs, acc = rest[:n_out], rest[n_out:]
        part = lax.dot_general(a_ref[...].astype(BF16), b_ref[...].astype(BF16), dims, preferred_element_type=F32)

        def write(val):
            outs[0][...] = val.astype(out_dtype)
            if also_t is not None:
                outs[1][...] = jnp.transpose(val).astype(also_t)

        if nk == 1:
            write(part)
            return
        acc_ref, = acc
        kk = pl.program_id(2)

        @pl.when(kk == 0)
        def _():
            acc_ref[...] = part

        @pl.when(kk > 0)
        def _():
            acc_ref[...] += part

        @pl.when(kk == nk - 1)
        def _():
            write(acc_ref[...])

    def ij(f):
        return (lambda i, j, kk: f(i, j, kk)) if rows_outer else (lambda j, i, kk: f(i, j, kk))

    a_spec = pl.BlockSpec((tk, tm), ij(lambda i, j, kk: (kk, i))) if ta else pl.BlockSpec((tm, tk), ij(lambda i, j, kk: (i, kk)))
    b_spec = pl.BlockSpec((tn, tk), ij(lambda i, j, kk: (j, kk))) if tb else pl.BlockSpec((tk, tn), ij(lambda i, j, kk: (kk, j)))
    out_specs = [pl.BlockSpec((tm, tn), ij(lambda i, j, kk: (i, j)))]
    out_shape = [jax.ShapeDtypeStruct((m, n), out_dtype)]
    if also_t is not None:
        out_specs.append(pl.BlockSpec((tn, tm), ij(lambda i, j, kk: (j, i))))
        out_shape.append(jax.ShapeDtypeStruct((n, m), also_t))
    res = pl.pallas_call(
        body, name=name,
        grid=(m // tm, n // tn, nk) if rows_outer else (n // tn, m // tm, nk),
        in_specs=[a_spec, b_spec], out_specs=out_specs, out_shape=out_shape,
        scratch_shapes=[] if nk == 1 else [pltpu.VMEM((tm, tn), F32)],
        compiler_params=_cparams(("arbitrary", "arbitrary", "arbitrary")),
    )(a, b)
    return res[0] if also_t is None else res


ROW_BLOCK_BYTES = 28 * 1024 * 1024


def _rowwise(fn, row_ins, par_ins, out_defs, red_defs, *, name):
    s = row_ins[0].shape[0]
    row_bytes = sum(a.shape[1] * a.dtype.itemsize for a in row_ins) + sum(c * jnp.dtype(dt).itemsize for c, dt in out_defs)
    tr = next((t for t in (512, 256, 128) if s % t == 0 and 2 * t * row_bytes <= ROW_BLOCK_BYTES), min(s, 128))
    nr, npar, no = len(row_ins), len(par_ins), len(out_defs)

    def body(*refs):
        rin, pin = refs[:nr], refs[nr:nr + npar]
        outs, reds = refs[nr + npar:nr + npar + no], refs[nr + npar + no:]
        i = pl.program_id(0)

        @pl.when(i == 0)
        def _():
            for r in reds:
                r[...] = jnp.zeros_like(r)

        fn(i, rin, pin, outs, reds)

    in_specs = [pl.BlockSpec((tr, a.shape[1]), lambda i: (i, 0)) for a in row_ins]
    in_specs += [pl.BlockSpec(a.shape, lambda i, nd=a.ndim: (0,) * nd) for a in par_ins]
    out_specs = [pl.BlockSpec((tr, c), lambda i: (i, 0)) for c, _ in out_defs]
    out_specs += [pl.BlockSpec(shp, lambda i: (0, 0)) for shp in red_defs]
    out_shape = [jax.ShapeDtypeStruct((s, c), dt) for c, dt in out_defs]
    out_shape += [jax.ShapeDtypeStruct(shp, F32) for shp in red_defs]
    return pl.pallas_call(
        body, name=name, grid=(s // tr,), in_specs=in_specs, out_specs=out_specs, out_shape=out_shape,
        compiler_params=_cparams(("arbitrary",)),
    )(*row_ins, *par_ins)


def _rms(v):
    return lax.rsqrt(jnp.mean(v * v, axis=-1, keepdims=True) + EPS)


def _colsum(v):
    return jnp.sum(v, axis=0, keepdims=True)


def _rms_bwd(dn, n, rstd):
    return rstd * (dn - n * jnp.mean(dn * n, axis=-1, keepdims=True))


def _norm_mod_fwd(x, gmod, name):
    def fn(i, rin, pin, outs, reds):
        xv = rin[0][...]
        p = pin[0][...]
        n = xv * _rms(xv)
        outs[0][...] = ((n * p[0:1]) * (1.0 + p[1:2]) + p[2:3]).astype(BF16)

    return _rowwise(fn, [x], [gmod], [(x.shape[1], BF16)], [], name=name)[0]


def _rope(v, rot_c, rot_s):
    half = QK_ROPE // 2
    swapped = jnp.where(_lane(v.shape) < QK_NOPE + half, pltpu.roll(v, LANES - half, 1), pltpu.roll(v, half, 1))
    return v * rot_c + swapped * rot_s


def _rope_t(dv, rot_c, rot_s):
    half = QK_ROPE // 2
    ds = dv * rot_s
    lane = _lane(dv.shape)
    swapped = jnp.where(lane < QK_NOPE + half, pltpu.roll(ds, LANES - half, 1), pltpu.roll(ds, half, 1))
    in_rope = (lane >= QK_NOPE) & (lane < QK_NOPE + QK_ROPE)
    return dv * rot_c + jnp.where(in_rope, swapped, 0.0)


def _mla_prep_fwd(proj_mla, rot_c, rot_s, ng):
    o1, o2 = Q_LORA, Q_LORA + KV_LORA

    def fn(i, rin, pin, outs, reds):
        g = pin[0][...]
        ql = rin[0][:, 0:o1]
        kl = rin[0][:, o1:o2]
        outs[0][...] = (ql * _rms(ql) * g[0:1, 0:o1]).astype(BF16)
        outs[1][...] = (kl * _rms(kl) * g[0:1, o1:o2]).astype(BF16)
        kr = pltpu.roll(rin[0][:, o2:o2 + LANES], QK_NOPE, 1)
        outs[2][...] = _rope(kr, rin[1][...], rin[2][...]).astype(BF16)

    return _rowwise(fn, [proj_mla, rot_c, rot_s], [ng], [(Q_LORA, BF16), (KV_LORA, BF16), (LANES, BF16)], [],
                    name="mla_prep_fwd")


def _mla_prep_bwd(proj_mla, dqn, dkvn, dkr, rot_c, rot_s, ng):
    o1, o2 = Q_LORA, Q_LORA + KV_LORA

    def fn(i, rin, pin, outs, reds):
        g = pin[0][...]
        ql = rin[0][:, 0:o1]
        kl = rin[0][:, o1:o2]
        rq, rk = _rms(ql), _rms(kl)
        nq, nk = ql * rq, kl * rk
        dq, dk = rin[1][...], rin[2][...]
        outs[0][:, 0:o1] = _rms_bwd(dq * g[0:1, 0:o1], nq, rq).astype(BF16)
        outs[0][:, o1:o2] = _rms_bwd(dk * g[0:1, o1:o2], nk, rk).astype(BF16)
        dkr_pre = pltpu.roll(_rope_t(rin[3][...], rin[4][...], rin[5][...]), LANES - QK_NOPE, 1)
        outs[0][:, o2:] = jnp.where(_lane(dkr_pre.shape) < QK_ROPE, dkr_pre, 0.0).astype(BF16)
        reds[0][0:1, 0:o1] += _colsum(dq * nq)
        reds[0][0:1, o1:o2] += _colsum(dk * nk)

    return _rowwise(fn, [proj_mla, dqn, dkvn, dkr, rot_c, rot_s], [ng], [(MLA_W, BF16)], [(SUBLANES, MLA_W)],
                    name="mla_prep_bwd")


def _rope_bwd(dq, rot_c, rot_s):
    def fn(i, rin, pin, outs, reds):
        c, sn = rin[1][...] * Q_PRESCALE, rin[2][...] * Q_PRESCALE
        for h in range(N_HEADS):
            sl = slice(h * LANES, (h + 1) * LANES)
            outs[0][:, sl] = _rope_t(rin[0][:, sl], c, sn).astype(BF16)

    return _rowwise(fn, [dq, rot_c, rot_s], [], [(dq.shape[1], BF16)], [], name="rope_bwd")[0]


def _rope_fwd_t(q, rot_c, rot_s):
    s, c = q.shape
    tr = min(256, s)

    def body(q_ref, c_ref, s_ref, o_ref, ot_ref):
        cc, sn = c_ref[...] * Q_PRESCALE, s_ref[...] * Q_PRESCALE
        for h in range(N_HEADS):
            sl = slice(h * LANES, (h + 1) * LANES)
            rot = _rope(q_ref[:, sl], cc, sn)
            o_ref[:, sl] = rot.astype(BF16)
            ot_ref[sl, :] = jnp.transpose(rot).astype(BF16)

    return pl.pallas_call(
        body, name="rope_fwd", grid=(s // tr,),
        in_specs=[pl.BlockSpec((tr, c), lambda i: (i, 0)), pl.BlockSpec((tr, LANES), lambda i: (i, 0)),
                  pl.BlockSpec((tr, LANES), lambda i: (i, 0))],
        out_specs=[pl.BlockSpec((tr, c), lambda i: (i, 0)), pl.BlockSpec((c, tr), lambda i: (0, i))],
        out_shape=[jax.ShapeDtypeStruct((s, c), BF16), jax.ShapeDtypeStruct((c, s), BF16)],
        compiler_params=_cparams(("arbitrary",)),
    )(q, rot_c, rot_s)


def _merge_fwd(pr, pm, proj_g):
    d = pr.shape[1]

    def fn(i, rin, pin, outs, reds):
        g_rnn, g_mla = rin[2][:, 0:d].astype(F32), rin[2][:, d:].astype(F32)
        outs[0][...] = (_sigmoid(g_rnn) * rin[0][...].astype(F32) + _sigmoid(g_mla) * rin[1][...].astype(F32)).astype(BF16)

    return _rowwise(fn, [pr, pm, proj_g], [], [(d, BF16)], [], name="merge_fwd")[0]


def _merge_bwd(dmerged, pr, pm, proj_g):
    d = pr.shape[1]

    def fn(i, rin, pin, outs, reds):
        dm = rin[0][...]
        sr, sm = _sigmoid(rin[3][:, 0:d].astype(F32)), _sigmoid(rin[3][:, d:].astype(F32))
        outs[0][...] = (dm * sr).astype(BF16)
        outs[1][...] = (dm * sm).astype(BF16)
        outs[2][:, 0:d] = (dm * rin[1][...].astype(F32) * sr * (1.0 - sr)).astype(BF16)
        outs[2][:, d:] = (dm * rin[2][...].astype(F32) * sm * (1.0 - sm)).astype(BF16)

    return _rowwise(fn, [dmerged, pr, pm, proj_g], [], [(d, BF16), (d, BF16), (2 * d, BF16)], [], name="merge_bwd")


def _resid_norm_fwd(x, o, gmod):
    d = x.shape[1]

    def fn(i, rin, pin, outs, reds):
        p = pin[0][...]
        x1 = rin[0][...] + p[3:4] * rin[1][...]
        outs[0][...] = x1
        outs[1][...] = ((x1 * _rms(x1) * p[0:1]) * (1.0 + p[1:2]) + p[2:3]).astype(BF16)

    return _rowwise(fn, [x, o], [gmod], [(d, F32), (d, BF16)], [], name="resid_norm_fwd")


def _final_fwd_bwd(x1, dn, target, par):
    d = x1.shape[1]

    def fn(i, rin, pin, outs, reds):
        p = pin[0][...]
        dnv = rin[1][...]
        x2 = rin[0][...] + p[0:1] * dnv
        rstd = _rms(x2)
        n3 = x2 * rstd
        err = n3 * p[1:2] - rin[2][...]
        dy = err * (1.0 / d)
        dx2 = _rms_bwd(dy * p[1:2], n3, rstd)
        outs[0][...] = dx2
        outs[1][...] = (dx2 * p[0:1]).astype(BF16)
        reds[0][0:1, :] += _colsum(dy * n3)
        reds[0][1:2, :] += _colsum(dx2 * dnv)
        reds[0][2:3, :] += jnp.zeros((1, d), F32) + jnp.sum(err * err)

    return _rowwise(fn, [x1, dn, target], [par], [(d, F32), (d, BF16)], [(SUBLANES, d)], name="final_fwd_bwd")


def _norm2_bwd(x1, dh2, dx2, o, gmod):
    d = x1.shape[1]

    def fn(i, rin, pin, outs, reds):
        p = pin[0][...]
        x1v, dh = rin[0][...], rin[1][...]
        rstd = _rms(x1v)
        n2 = x1v * rstd
        dx1 = rin[2][...] + _rms_bwd(dh * (p[0:1] * (1.0 + p[1:2])), n2, rstd)
        outs[0][...] = dx1
        outs[1][...] = (dx1 * p[3:4]).astype(BF16)
        reds[0][0:1, :] += _colsum(dh * n2 * (1.0 + p[1:2]))
        reds[0][1:2, :] += _colsum(dh * n2 * p[0:1])
        reds[0][2:3, :] += _colsum(dh)
        reds[0][3:4, :] += _colsum(dx1 * rin[3][...])

    return _rowwise(fn, [x1, dh2, dx2, o], [gmod], [(d, F32), (d, BF16)], [(SUBLANES, d)], name="norm2_bwd")


def _norm1_bwd(x, dh_a, dh_b, dh_c, dx1, gmod):
    d = x.shape[1]

    def fn(i, rin, pin, outs, reds):
        p = pin[0][...]
        xv = rin[0][...]
        dh = rin[1][...] + rin[2][...] + rin[3][...]
        rstd = _rms(xv)
        n1 = xv * rstd
        outs[0][...] = rin[4][...] + _rms_bwd(dh * (p[0:1] * (1.0 + p[1:2])), n1, rstd)
        reds[0][0:1, :] += _colsum(dh * n1 * (1.0 + p[1:2]))
        reds[0][1:2, :] += _colsum(dh * n1 * p[0:1])
        reds[0][2:3, :] += _colsum(dh)

    return _rowwise(fn, [x, dh_a, dh_b, dh_c, dx1], [gmod], [(d, F32)], [(SUBLANES, d)], name="norm1_bwd")


RNN_CHUNK = 512


def _shift_down(ref, base, n, j):
    v = ref[pl.ds(base, n + SUBLANES), :]
    return v[SUBLANES:] if j == 0 else pltpu.roll(v, j, 0)[SUBLANES:]


def _shift_up(ref, base, n, j, top_pad):
    v = ref[pl.ds(base + top_pad, n + SUBLANES), :]
    return v[:n] if j == 0 else pltpu.roll(v, n + SUBLANES - j, 0)[:n]


SCAN_GROUP = 128


def _scan_sizes(s):
    sizes = [s]
    while sizes[-1] > SUBLANES:
        assert sizes[-1] % SUBLANES == 0, s
        sizes.append(sizes[-1] // SUBLANES)
    return sizes


def _scan_scratch(s):
    return [pltpu.VMEM((n + 2 * SUBLANES, LANES), F32) for n in _scan_sizes(s)[1:] for _ in range(2)]


def _linear_scan(a_ref, b_ref, out_ref, a_off, s, reverse, levels):
    sizes = _scan_sizes(s)
    lv = [(a_ref, b_ref, a_off, 0)] + [(levels[2 * i], levels[2 * i + 1], 0, SUBLANES) for i in range(len(sizes) - 1)]
    zero8 = jnp.zeros((SUBLANES, LANES), F32)
    for (ar, br, _, _), n in zip(lv[1:], sizes[1:]):
        br[0:SUBLANES, :] = zero8
        br[pl.ds(n + SUBLANES, SUBLANES), :] = zero8
    order = list(range(SUBLANES - 1, -1, -1)) if reverse else list(range(SUBLANES))

    for lvl in range(len(sizes) - 1):
        ar, br, aoff, off = lv[lvl]
        m = sizes[lvl + 1]
        g = min(m, SCAN_GROUP)
        for t0 in range(0, m, g):
            acc_a = acc_b = None
            for r in order:
                sa = pl.ds(off + SUBLANES * t0 + r + aoff, g, stride=SUBLANES)
                sb = pl.ds(off + SUBLANES * t0 + r, g, stride=SUBLANES)
                a, b = ar[sa, :], br[sb, :]
                if acc_a is None:
                    acc_a, acc_b = a, b
                else:
                    acc_b = a * acc_b + b
                    acc_a = a * acc_a
            lv[lvl + 1][0][pl.ds(SUBLANES + t0, g), :] = acc_a
            lv[lvl + 1][1][pl.ds(SUBLANES + t0, g), :] = acc_b

    ar, br, _, off = lv[-1]
    n = sizes[-1]
    a, b = ar[pl.ds(off, n), :], br[pl.ds(off, n), :]
    h, rows = jnp.zeros((1, LANES), F32), [None] * n
    for j in (range(n - 1, -1, -1) if reverse else range(n)):
        h = a[j:j + 1, :] * h + b[j:j + 1, :]
        rows[j] = h
    br[pl.ds(off, n), :] = jnp.concatenate(rows, axis=0)

    for lvl in range(len(sizes) - 2, -1, -1):
        ar, br, aoff, off = lv[lvl]
        m = sizes[lvl + 1]
        up = lv[lvl + 1][1]
        dst = out_ref if lvl == 0 else br
        g = min(m, SCAN_GROUP)
        for t0 in range(0, m, g):
            h = _shift_up(up, t0, g, 1, SUBLANES) if reverse else _shift_down(up, t0, g, 1)
            for r in order:
                sa = pl.ds(off + SUBLANES * t0 + r + aoff, g, stride=SUBLANES)
                sb = pl.ds(off + SUBLANES * t0 + r, g, stride=SUBLANES)
                h = ar[sa, :] * h + br[sb, :]
                dst[sb, :] = h


def _one_minus_exp(z):
    series = -z * (1.0 + z * (0.5 + z * (1.0 / 6.0 + z * (1.0 / 24.0 + z * (1.0 / 120.0 + z * (1.0 / 720.0))))))
    return jnp.where(z > -0.1, series, 1.0 - jnp.exp(z))


def _softplus(v):
    return jnp.maximum(v, 0.0) + jnp.log(1.0 + jnp.exp(-jnp.abs(v)))


def _rnn_gates(xc, w, wa, wx, sp):
    xb = xc.astype(BF16)
    ra = _sigmoid(jnp.dot(xb, wa, preferred_element_type=F32) + w[5:6])
    ix = _sigmoid(jnp.dot(xb, wx, preferred_element_type=F32) + w[6:7])
    la = (-LRU_C) * ra * sp
    a = jnp.exp(la)
    mult = jnp.sqrt(_one_minus_exp(2.0 * la))
    return ra, ix, a, mult


def _rnn_fwd(x_rnn, keep, rp, wa_bd, wx_bd):
    s, r = x_rnn.shape
    ts = min(RNN_CHUNK, s)

    def body(x_ref, keep_ref, rp_ref, wa_ref, wx_ref, xc_ref, ra_ref, ix_ref, hs_ref, hsb_ref, xpad, a_s, b_s, *levels):
        xpad[0:SUBLANES, :] = jnp.zeros((SUBLANES, LANES), F32)
        xpad[SUBLANES:, :] = x_ref[...]
        w = rp_ref[...]
        sp = _softplus(-w[7:8])
        wa, wx = wa_ref[0], wx_ref[0]

        def chunk(c, carry):
            base = pl.multiple_of(c * ts, ts)
            xc = w[4:5] + w[3:4] * _shift_down(xpad, base, ts, 0)
            for j in range(1, 4):
                xc = xc + w[3 - j:4 - j] * _shift_down(xpad, base, ts, j)
            ra, ix, a, mult = _rnn_gates(xc, w, wa, wx, sp)
            kp = keep_ref[pl.ds(base, ts), :]
            xc_ref[pl.ds(base, ts), :] = xc
            ra_ref[pl.ds(base, ts), :] = ra
            ix_ref[pl.ds(base, ts), :] = ix
            a_s[pl.ds(base, ts), :] = a * kp
            b_s[pl.ds(base, ts), :] = jnp.where(kp > 0.0, mult, 1.0) * (ix * xc)
            return carry

        lax.fori_loop(0, s // ts, chunk, 0)

        _linear_scan(a_s, b_s, hs_ref, 0, s, False, levels)
        hsb_ref[...] = hs_ref[...].astype(BF16)

    col = pl.BlockSpec((s, LANES), lambda g: (0, g))
    return pl.pallas_call(
        body, name="rnn_fwd", grid=(r // LANES,),
        in_specs=[col, pl.BlockSpec((s, 1), lambda g: (0, 0)), pl.BlockSpec((SUBLANES, LANES), lambda g: (0, g)),
                  pl.BlockSpec((1, LANES, LANES), lambda g: (g, 0, 0)), pl.BlockSpec((1, LANES, LANES), lambda g: (g, 0, 0))],
        out_specs=[col] * 5,
        out_shape=[jax.ShapeDtypeStruct((s, r), F32)] * 4 + [jax.ShapeDtypeStruct((s, r), BF16)],
        scratch_shapes=[pltpu.VMEM((s + SUBLANES, LANES), F32), pltpu.VMEM((s, LANES), F32), pltpu.VMEM((s, LANES), F32),
                        *_scan_scratch(s)],
        compiler_params=_cparams(("arbitrary",)),
    )(x_rnn, keep, rp, wa_bd, wx_bd)


def _rnn_bwd(x_rnn, xc, ra, ix, hs, dy, keep, rp, wa_bd, wx_bd):
    s, r = x_rnn.shape
    ts = min(RNN_CHUNK, s)

    def body(x_ref, xc_ref, ra_ref, ix_ref, hs_ref, dy_ref, keep_ref, rp_ref, wa_ref, wx_ref,
             dx_ref, dwa_ref, dwx_ref, red_ref, xpad, hpad, a_s, dh_s, dxc_s, *levels):
        zero8 = jnp.zeros((SUBLANES, LANES), F32)
        xpad[0:SUBLANES, :] = zero8
        xpad[SUBLANES:, :] = x_ref[...]
        hpad[0:SUBLANES, :] = zero8
        hpad[SUBLANES:, :] = hs_ref[...]
        a_s[s:, :] = zero8
        dxc_s[s:, :] = zero8
        w = rp_ref[...]
        sp = _softplus(-w[7:8])
        wa, wx = wa_ref[0], wx_ref[0]

        def decay(c, carry):
            base = pl.multiple_of(c * ts, ts)
            a = jnp.exp((-LRU_C) * ra_ref[pl.ds(base, ts), :] * sp)
            a_s[pl.ds(base, ts), :] = a * keep_ref[pl.ds(base, ts), :]
            return carry

        lax.fori_loop(0, s // ts, decay, 0)

        _linear_scan(a_s, dy_ref, dh_s, 1, s, True, levels)

        def gates(c, carry):
            dwa, dwx, d_ba, d_bx, d_sp, d_cb = carry
            base = pl.multiple_of(c * ts, ts)
            xcv = xc_ref[pl.ds(base, ts), :]
            rav = ra_ref[pl.ds(base, ts), :]
            ixv = ix_ref[pl.ds(base, ts), :]
            kp = keep_ref[pl.ds(base, ts), :]
            dh = dh_s[pl.ds(base, ts), :]
            h_prev = _shift_down(hpad, base, ts, 1)
            la = (-LRU_C) * rav * sp
            a = jnp.exp(la)
            mult = jnp.sqrt(_one_minus_exp(2.0 * la))
            mult_eff = jnp.where(kp > 0.0, mult, 1.0)
            d_a = dh * h_prev * kp
            d_mult = dh * (ixv * xcv) * kp
            d_ix = dh * mult_eff * xcv
            d_xc = dh * mult_eff * ixv
            d_la = d_a * a - d_mult * (a * a) / mult
            d_pa = d_la * ((-LRU_C) * sp) * rav * (1.0 - rav)
            d_px = d_ix * ixv * (1.0 - ixv)
            xb = xcv.astype(BF16)
            pab, pxb = d_pa.astype(BF16), d_px.astype(BF16)
            tn = (((0,), (0,)), ((), ()))
            nt_ = (((1,), (1,)), ((), ()))
            dwa = dwa + lax.dot_general(xb, pab, tn, preferred_element_type=F32)
            dwx = dwx + lax.dot_general(xb, pxb, tn, preferred_element_type=F32)
            d_xc = d_xc + lax.dot_general(pab, wa, nt_, preferred_element_type=F32)
            d_xc = d_xc + lax.dot_general(pxb, wx, nt_, preferred_element_type=F32)
            dxc_s[pl.ds(base, ts), :] = d_xc
            return (dwa, dwx, d_ba + _colsum(d_pa), d_bx + _colsum(d_px),
                    d_sp + _colsum(d_la * ((-LRU_C) * rav)), d_cb + _colsum(d_xc))

        z1 = jnp.zeros((1, LANES), F32)
        zw = jnp.zeros((LANES, LANES), F32)
        dwa, dwx, d_ba, d_bx, d_sp, d_cb = lax.fori_loop(0, s // ts, gates, (zw, zw, z1, z1, z1, z1))
        dwa_ref[0] = dwa
        dwx_ref[0] = dwx

        def conv(c, carry):
            base = pl.multiple_of(c * ts, ts)
            d_here = dxc_s[pl.ds(base, ts), :]
            dx = w[3:4] * d_here
            for j in range(1, 4):
                dx = dx + w[3 - j:4 - j] * _shift_up(dxc_s, base, ts, j, 0)
            dx_ref[pl.ds(base, ts), :] = dx.astype(BF16)
            return tuple(carry[k] + _colsum(d_here * _shift_down(xpad, base, ts, 3 - k)) for k in range(4))

        d_w = lax.fori_loop(0, s // ts, conv, (z1, z1, z1, z1))
        d_lru = d_sp * (-_sigmoid(-w[7:8]))
        red_ref[...] = jnp.concatenate(list(d_w) + [d_cb, d_ba, d_bx, d_lru], axis=0)

    col = pl.BlockSpec((s, LANES), lambda g: (0, g))
    par = pl.BlockSpec((SUBLANES, LANES), lambda g: (0, g))
    wsp = pl.BlockSpec((1, LANES, LANES), lambda g: (g, 0, 0))
    return pl.pallas_call(
        body, name="rnn_bwd", grid=(r // LANES,),
        in_specs=[col] * 6 + [pl.BlockSpec((s, 1), lambda g: (0, 0)), par, wsp, wsp],
        out_specs=[col, wsp, wsp, par],
        out_shape=[jax.ShapeDtypeStruct((s, r), BF16), jax.ShapeDtypeStruct((r // LANES, LANES, LANES), F32),
                   jax.ShapeDtypeStruct((r // LANES, LANES, LANES), F32), jax.ShapeDtypeStruct((SUBLANES, r), F32)],
        scratch_shapes=[pltpu.VMEM((s + SUBLANES, LANES), F32), pltpu.VMEM((s + SUBLANES, LANES), F32),
                        pltpu.VMEM((s + SUBLANES, LANES), F32), pltpu.VMEM((s, LANES), F32),
                        pltpu.VMEM((s + SUBLANES, LANES), F32), *_scan_scratch(s)],
        compiler_params=_cparams(("arbitrary",)),
    )(x_rnn, xc, ra, ix, hs, dy, keep, rp, wa_bd, wx_bd)


ATT_BLOCK = 512
ATT_FWD_HEADS = 8
ATT_BWD_HEADS = 4


LOG2E = 1.4426950408889634
LN2 = 0.6931471805599453
Q_PRESCALE = ATT_SCALE * LOG2E


def _att_scores(q, kvt, krt, diagonal):
    kt_eff = jnp.where(_row(kvt.shape) < QK_NOPE, kvt, krt)
    sc = jnp.dot(q, kt_eff, preferred_element_type=F32)
    if diagonal:
        sc = jnp.where(lax.broadcasted_iota(jnp.int32, sc.shape, 1) <= lax.broadcasted_iota(jnp.int32, sc.shape, 0), sc, -jnp.inf)
    return sc


def _att_fwd(q, kv, kvt, krt):
    s = q.shape[0]
    t = min(ATT_BLOCK, s)
    nb = s // t
    hp = ATT_FWD_HEADS

    pairs = [(i, j) for i in range(nb) for j in range(i + 1)]
    i_tab = jnp.array([p[0] for p in pairs], jnp.int32)
    j_tab = jnp.array([p[1] for p in pairs], jnp.int32)

    def body(i_ref, j_ref, q_ref, kv_ref, kvt_ref, krt_ref, y_ref, lse_ref, m_s, acc_s):
        i, j = i_ref[pl.program_id(1)], j_ref[pl.program_id(1)]

        @pl.when(j == 0)
        def _():
            m_s[...] = jnp.full(m_s.shape, -jnp.inf, F32)
            acc_s[...] = jnp.zeros(acc_s.shape, F32)

        def step(diagonal):
            krt_b = krt_ref[...]
            lane = _lane((t, LANES))
            groups = [slice(c * LANES, (c + 1) * LANES) for c in range(t // LANES)]
            heads = [slice(hh * LANES, (hh + 1) * LANES) for hh in range(hp)]
            scs = [_att_scores(q_ref[:, sl], kvt_ref[sl, :], krt_b, diagonal) for sl in heads]
            stats = []
            for hh in range(hp):
                m_prev = m_s[hh]
                m_blk = scs[hh][:, groups[0]]
                for g in groups[1:]:
                    m_blk = jnp.maximum(m_blk, scs[hh][:, g])
                stats.append((m_prev, jnp.maximum(m_prev, jnp.max(m_blk, axis=-1, keepdims=True))))
            for hh in range(hp):
                m_prev, m_new = stats[hh]
                kvb = kv_ref[:, heads[hh]]
                ones_v = jnp.where(lane < QK_NOPE, jnp.ones_like(kvb), kvb)
                p = jnp.concatenate([jnp.exp2(scs[hh][:, g] - m_new).astype(BF16) for g in groups], axis=1)
                acc_s[hh] = jnp.exp2(m_prev - m_new) * acc_s[hh] + jnp.dot(p, ones_v, preferred_element_type=F32)
                m_s[hh] = m_new

        @pl.when(j < i)
        def _():
            step(False)

        @pl.when(j == i)
        def _():
            step(True)
            lane = _lane((t, LANES))
            for g in range(hp // 2):
                sl = slice(g * LANES, (g + 1) * LANES)
                a0, a1 = acc_s[2 * g], acc_s[2 * g + 1]
                l0, l1 = a0[:, 0:1], a1[:, 0:1]
                y_ref[:, sl] = jnp.where(lane < V_HEAD, pltpu.roll(a0 / l0, V_HEAD, 1), a1 / l1).astype(BF16)
                lse_ref[:, sl] = jnp.where(lane < V_HEAD, m_s[2 * g] + jnp.log(l0) * LOG2E, m_s[2 * g + 1] + jnp.log(l1) * LOG2E)

    return pl.pallas_call(
        body, name="att_fwd",
        grid_spec=pltpu.PrefetchScalarGridSpec(
            num_scalar_prefetch=2, grid=(N_HEADS // hp, len(pairs)),
            in_specs=[pl.BlockSpec((t, hp * LANES), lambda p, n, it, jt: (it[n], p)),
                      pl.BlockSpec((t, hp * LANES), lambda p, n, it, jt: (jt[n], p)),
                      pl.BlockSpec((hp * LANES, t), lambda p, n, it, jt: (p, jt[n])),
                      pl.BlockSpec((LANES, t), lambda p, n, it, jt: (0, jt[n]))],
            out_specs=[pl.BlockSpec((t, hp // 2 * LANES), lambda p, n, it, jt: (it[n], p))] * 2,
            scratch_shapes=[pltpu.VMEM((hp, t, LANES), F32)] * 2),
        out_shape=[jax.ShapeDtypeStruct((s, N_HEADS * V_HEAD), BF16), jax.ShapeDtypeStruct((s, N_HEADS * V_HEAD), F32)],
        compiler_params=_cparams(("arbitrary", "arbitrary")),
    )(i_tab, j_tab, q, kv, kvt, krt)


def _att_bwd(q, qt, kv, kvt, kr, krt, y, lse, dy, dyt):
    s = q.shape[0]
    t = min(ATT_BLOCK, s)
    nb = s // t

    hp = ATT_BWD_HEADS
    pairs = [(i, j) for j in range(nb) for i in range(j, nb)]
    i_tab = jnp.array([p[0] for p in pairs], jnp.int32)
    j_tab = jnp.array([p[1] for p in pairs], jnp.int32)

    def body(i_ref, j_ref, q_ref, qt_ref, kv_ref, kvt_ref, kr_ref, krt_ref, y_ref, lse_ref, dy_ref, dyt_ref,
             dq_ref, dkvt_ref, dkrt_ref, dkv_s):
        p_, n = pl.program_id(0), pl.program_id(1)
        i, j = i_ref[n], j_ref[n]

        @pl.when((p_ == 0) & (n == 0))
        def _():
            dkrt_ref[...] = jnp.zeros(dkrt_ref.shape, F32)

        @pl.when(n == 0)
        def _():
            dq_ref[...] = jnp.zeros(dq_ref.shape, F32)

        @pl.when(i == j)
        def _():
            dkv_s[...] = jnp.zeros(dkv_s.shape, F32)

        def step(diagonal):
            lane = _lane((t, LANES))
            row = _row((LANES, t))
            krb, krt_b = kr_ref[...], krt_ref[...]
            dyv = dy_ref[...]
            yv = y_ref[...].astype(F32)
            lsev = lse_ref[...]
            dyt_b = dyt_ref[...]
            rows = pl.ds(pl.multiple_of(i * t, t), t)
            cols = pl.ds(pl.multiple_of(j * t, t), t)
            zeros_t = jnp.zeros((V_HEAD, t), BF16)
            ones_w = jnp.ones((LANES, LANES), BF16)
            groups = [slice(c * LANES, (c + 1) * LANES) for c in range(t // LANES)]
            heads = [slice(hh * LANES, (hh + 1) * LANES) for hh in range(hp)]
            scs, dps, stats = [], [], []
            for hh, sl in enumerate(heads):
                kvt_b = kvt_ref[sl, :]
                scs.append(_att_scores(q_ref[:, sl], kvt_b, krt_b, diagonal))
                pair, first = heads[hh // 2], hh % 2 == 0
                lse_g, dy_g, y_g = lsev[:, pair], dyv[:, pair], yv[:, pair]
                mine = (lane < V_HEAD) if first else (lane >= V_HEAD)
                lse_rep = jnp.where(mine, lse_g, pltpu.roll(lse_g, V_HEAD, 1))
                do_pad = jnp.where(lane >= V_HEAD, pltpu.roll(dy_g, V_HEAD, 1) if first else dy_g, 0.0)
                o_pad = jnp.where(lane >= V_HEAD, pltpu.roll(y_g, V_HEAD, 1) if first else y_g, 0.0)
                do_ln2 = do_pad * LN2
                prod = do_ln2 * o_pad
                head_part = prod.astype(BF16)
                rest_part = (prod - head_part.astype(F32)).astype(BF16)
                delta_rep = (jnp.dot(head_part, ones_w, preferred_element_type=F32)
                             + jnp.dot(rest_part, ones_w, preferred_element_type=F32))
                dps.append(jnp.dot(do_ln2.astype(BF16), kvt_b, preferred_element_type=F32))
                stats.append((lse_rep, delta_rep))
            dkr_acc = jnp.zeros((LANES, t), F32)
            for hh, sl in enumerate(heads):
                lse_rep, delta_rep = stats[hh]
                probs, dss = [], []
                for g in groups:
                    pg = jnp.exp2(scs[hh][:, g] - lse_rep)
                    probs.append(pg.astype(BF16))
                    dss.append((pg * (dps[hh][:, g] - delta_rep)).astype(BF16))
                prob, ds = jnp.concatenate(probs, axis=1), jnp.concatenate(dss, axis=1)
                dot_pad = jnp.concatenate([zeros_t, dyt_b[hh * V_HEAD:(hh + 1) * V_HEAD, :]], axis=0)
                k_eff = jnp.where(lane < QK_NOPE, kv_ref[:, sl], krb)
                dvt = jnp.dot(dot_pad, prob, preferred_element_type=F32)
                dq_ref[rows, sl] += jnp.dot(ds, k_eff, preferred_element_type=F32)
                dkt = jnp.dot(qt_ref[sl, :], ds, preferred_element_type=F32)
                dkv_s[hh] += dvt + jnp.where(row < QK_NOPE, dkt, 0.0)
                dkr_acc = dkr_acc + jnp.where(row >= QK_NOPE, dkt, 0.0)
            dkrt_ref[:, cols] += dkr_acc

        @pl.when(i > j)
        def _():
            step(False)

        @pl.when(i == j)
        def _():
            step(True)

        @pl.when(i == nb - 1)
        def _():
            for hh, sl in enumerate([slice(hh * LANES, (hh + 1) * LANES) for hh in range(hp)]):
                dkvt_ref[sl, :] = dkv_s[hh].astype(BF16)

    qi = lambda p, n, it, jt: (it[n], p)
    qti = lambda p, n, it, jt: (p, it[n])
    wide, half = hp * LANES, hp // 2 * LANES
    return pl.pallas_call(
        body, name="att_bwd",
        grid_spec=pltpu.PrefetchScalarGridSpec(
            num_scalar_prefetch=2, grid=(N_HEADS // hp, len(pairs)),
            in_specs=[pl.BlockSpec((t, wide), qi), pl.BlockSpec((wide, t), qti),
                      pl.BlockSpec((t, wide), lambda p, n, it, jt: (jt[n], p)),
                      pl.BlockSpec((wide, t), lambda p, n, it, jt: (p, jt[n])),
                      pl.BlockSpec((t, LANES), lambda p, n, it, jt: (jt[n], 0)),
                      pl.BlockSpec((LANES, t), lambda p, n, it, jt: (0, jt[n])),
                      pl.BlockSpec((t, half), qi), pl.BlockSpec((t, half), qi), pl.BlockSpec((t, half), qi),
                      pl.BlockSpec((half, t), qti)],
            out_specs=[pl.BlockSpec((s, wide), lambda p, n, it, jt: (0, p)),
                       pl.BlockSpec((wide, t), lambda p, n, it, jt: (p, jt[n])),
                       pl.BlockSpec((LANES, s), lambda p, n, it, jt: (0, 0))],
            scratch_shapes=[pltpu.VMEM((hp, LANES, t), F32)]),
        out_shape=[jax.ShapeDtypeStruct((s, N_HEADS * LANES), F32), jax.ShapeDtypeStruct((N_HEADS * LANES, s), BF16),
                   jax.ShapeDtypeStruct((LANES, s), F32)],
        compiler_params=_cparams(("arbitrary", "arbitrary")),
    )(i_tab, j_tab, q, qt, kv, kvt, kr, krt, y, lse, dy, dyt)


FFN_COLS = 256


def _ffn_conv(pad_ref, w, base, n):
    u = w[3:4] + w[2:3] * _shift_down(pad_ref, base, n, 0)
    for j in range(1, 3):
        u = u + w[2 - j:3 - j] * _shift_down(pad_ref, base, n, j)
    return u


def _ffn_act_fwd(up, fp):
    s, f2 = up.shape
    f = f2 // 2
    tc = FFN_COLS
    ts = min(RNN_CHUNK, s)
    nfb = f // tc

    def body(ug_ref, uv_ref, wg_ref, wv_ref, act_ref, gpad, vpad):
        zero8 = jnp.zeros((SUBLANES, tc), F32)
        gpad[0:SUBLANES, :] = zero8
        vpad[0:SUBLANES, :] = zero8
        gpad[SUBLANES:, :] = ug_ref[...]
        vpad[SUBLANES:, :] = uv_ref[...]
        wg, wv = wg_ref[...], wv_ref[...]

        def chunk(c, carry):
            base = pl.multiple_of(c * ts, ts)
            g = _ffn_conv(gpad, wg, base, ts)
            v = _ffn_conv(vpad, wv, base, ts)
            act_ref[pl.ds(base, ts), :] = (g * _sigmoid(g) * v).astype(BF16)
            return carry

        lax.fori_loop(0, s // ts, chunk, 0)

    return pl.pallas_call(
        body, name="ffn_act_fwd", grid=(nfb,),
        in_specs=[pl.BlockSpec((s, tc), lambda b: (0, b)), pl.BlockSpec((s, tc), lambda b: (0, b + nfb)),
                  pl.BlockSpec((SUBLANES, tc), lambda b: (0, b)), pl.BlockSpec((SUBLANES, tc), lambda b: (0, b + nfb))],
        out_specs=pl.BlockSpec((s, tc), lambda b: (0, b)),
        out_shape=jax.ShapeDtypeStruct((s, f), BF16),
        scratch_shapes=[pltpu.VMEM((s + SUBLANES, tc), F32)] * 2,
        compiler_params=_cparams(("arbitrary",)),
    )(up, up, fp, fp)


def _ffn_act_bwd(up, dact, fp):
    s, f2 = up.shape
    f = f2 // 2
    tc = FFN_COLS
    ts = min(RNN_CHUNK, s)
    nfb = f // tc

    def body(ug_ref, uv_ref, da_ref, wg_ref, wv_ref, dup_ref, red_ref, gpad, vpad, dgs, dvs):
        half = pl.program_id(1)
        wg, wv = wg_ref[...], wv_ref[...]

        @pl.when(half == 0)
        def _():
            zero8 = jnp.zeros((SUBLANES, tc), F32)
            gpad[0:SUBLANES, :] = zero8
            vpad[0:SUBLANES, :] = zero8
            gpad[SUBLANES:, :] = ug_ref[...]
            vpad[SUBLANES:, :] = uv_ref[...]
            dgs[s:, :] = zero8
            dvs[s:, :] = zero8

            def act(c, carry):
                base = pl.multiple_of(c * ts, ts)
                g = _ffn_conv(gpad, wg, base, ts)
                v = _ffn_conv(vpad, wv, base, ts)
                da = da_ref[pl.ds(base, ts), :]
                sg = _sigmoid(g)
                dgs[pl.ds(base, ts), :] = da * v * (sg * (1.0 + g * (1.0 - sg)))
                dvs[pl.ds(base, ts), :] = da * (g * sg)
                return carry

            lax.fori_loop(0, s // ts, act, 0)

        def conv_t(src, pad, w, out_ref, red_ref):
            def chunk(c, carry):
                base = pl.multiple_of(c * ts, ts)
                d_here = src[pl.ds(base, ts), :]
                dx = w[2:3] * d_here
                for j in range(1, 3):
                    dx = dx + w[2 - j:3 - j] * _shift_up(src, base, ts, j, 0)
                out_ref[pl.ds(base, ts), :] = dx.astype(BF16)
                taps = tuple(carry[k] + _colsum(d_here * _shift_down(pad, base, ts, 2 - k)) for k in range(3))
                return taps + (carry[3] + _colsum(d_here),)

            z1 = jnp.zeros((1, tc), F32)
            red = lax.fori_loop(0, s // ts, chunk, (z1, z1, z1, z1))
            red_ref[...] = jnp.concatenate(list(red) + [jnp.zeros((4, tc), F32)], axis=0)

        @pl.when(half == 0)
        def _():
            conv_t(dgs, gpad, wg, dup_ref, red_ref)

        @pl.when(half == 1)
        def _():
            conv_t(dvs, vpad, wv, dup_ref, red_ref)

    gcol = pl.BlockSpec((s, tc), lambda b, h: (0, b))
    vcol = pl.BlockSpec((s, tc), lambda b, h: (0, b + nfb))
    gpar = pl.BlockSpec((SUBLANES, tc), lambda b, h: (0, b))
    vpar = pl.BlockSpec((SUBLANES, tc), lambda b, h: (0, b + nfb))
    return pl.pallas_call(
        body, name="ffn_act_bwd", grid=(nfb, 2),
        in_specs=[gcol, vcol, gcol, gpar, vpar],
        out_specs=[pl.BlockSpec((s, tc), lambda b, h: (0, b + h * nfb)),
                   pl.BlockSpec((SUBLANES, tc), lambda b, h: (0, b + h * nfb))],
        out_shape=[jax.ShapeDtypeStruct((s, f2), BF16), jax.ShapeDtypeStruct((SUBLANES, f2), F32)],
        scratch_shapes=[pltpu.VMEM((s + SUBLANES, tc), F32)] * 4,
        compiler_params=_cparams(("arbitrary", "arbitrary")),
    )(up, up, dact, fp, fp)


def _rows8(rows, width):
    rows = [r.reshape(1, width).astype(F32) for r in rows]
    return jnp.concatenate(rows + [jnp.zeros((SUBLANES - len(rows), width), F32)], axis=0)


def _block_diag(w):
    n, b, _ = w.shape
    w = w.reshape(n // 2, 2, b, b)
    z = jnp.zeros((n // 2, b, b), w.dtype)
    top = jnp.concatenate([w[:, 0], z], axis=2)
    bot = jnp.concatenate([z, w[:, 1]], axis=2)
    return jnp.concatenate([top, bot], axis=1)


def _block_diag_t(bd):
    n, b2, _ = bd.shape
    b = b2 // 2
    return jnp.stack([bd[:, :b, :b], bd[:, b:, b:]], axis=1).reshape(2 * n, b, b)


def _local_step(x, mod, positions, target, w_in, fetch, sm, emit):
    s, d = x.shape
    o_rnn, o_mla = D_RNN, D_RNN + Q_LORA + KV_LORA + QK_ROPE
    wts = {}
    w_in_rnn = w_in[:, :o_rnn]
    w_in_mla = jnp.concatenate([w_in[:, o_rnn:o_mla], jnp.zeros((d, MLA_W - (o_mla - o_rnn)), w_in.dtype)], axis=1)
    w_in_g = w_in[:, o_mla:]
    hd = QK_NOPE + QK_ROPE
    wa_bd = _block_diag(sm["w_gate_a"]).astype(BF16)
    wx_bd = _block_diag(sm["w_gate_x"]).astype(BF16)

    pos = positions.reshape(s)
    half = QK_ROPE // 2
    inv_freq = ROPE_THETA ** (-jnp.arange(half, dtype=F32) / half)
    ang = pos.astype(F32)[:, None] * inv_freq
    cos, sin = jnp.cos(ang), jnp.sin(ang)
    rot_c = jnp.concatenate([jnp.ones((s, QK_NOPE), F32), cos, cos, jnp.ones((s, LANES - hd), F32)], axis=1)
    rot_s = jnp.concatenate([jnp.zeros((s, QK_NOPE), F32), -sin, sin, jnp.zeros((s, LANES - hd), F32)], axis=1)
    keep = (pos != 0).astype(F32).reshape(s, 1)

    gmod1 = _rows8([sm["norm1_g"], mod[1], mod[0]], d)
    gmod2 = _rows8([sm["norm2_g"], mod[4], mod[3], mod[2]], d)
    rp = jnp.concatenate([sm["conv_w"].reshape(4, D_RNN), _rows8([sm["conv_b"], sm["b_gate_a"], sm["b_gate_x"], sm["lru_param"]], D_RNN)[:4]], axis=0)
    fp = _rows8([sm["ffn_conv_w"][0], sm["ffn_conv_w"][1], sm["ffn_conv_w"][2], sm["ffn_conv_b"]], 2 * D_FF)
    ng = _rows8([jnp.concatenate([sm["q_norm_g"].reshape(-1), sm["kv_norm_g"].reshape(-1), jnp.zeros((MLA_W - Q_LORA - KV_LORA,), F32)])], MLA_W)
    fpar = _rows8([mod[5], sm["final_g"]], d)

    h = _norm_mod_fwd(x, gmod1, "norm1_fwd")
    proj_rnn = _mm(h, w_in_rnn, name="mm_in_rnn")
    proj_mla = _mm(h, w_in_mla, name="mm_in_mla")
    proj_g = _mm(h, w_in_g, out_dtype=BF16, name="mm_in_g")
    xc, ra, ix, hs, hs_b = _rnn_fwd(proj_rnn, keep, rp, wa_bd, wx_bd)
    qn, kvn, kr = _mla_prep_fwd(proj_mla, rot_c, rot_s, ng)
    wts.update(fetch(("w_uq", "w_ukv"), kr))
    w_uq_p = jnp.pad(wts["w_uq"].reshape(Q_LORA, N_HEADS, hd), ((0, 0), (0, 0), (0, LANES - hd))).reshape(Q_LORA, N_HEADS * LANES)
    w_ukv = wts["w_ukv"]
    q_rot, q_rot_t = _rope_fwd_t(_mm(qn, w_uq_p, name="mm_uq"), rot_c, rot_s)
    kv, kvt = _mm(kvn, w_ukv, out_dtype=BF16, also_t=BF16, name="mm_ukv")
    krt = jnp.transpose(kr)
    y_mla, lse = _att_fwd(q_rot, kv, kvt, krt)
    wts.update(fetch(("w_proj_rnn", "w_proj_mla", "w_out", "w_up", "w_down"), lse))
    pr = _mm(hs_b, wts["w_proj_rnn"], out_dtype=BF16, name="mm_proj_rnn")
    pm = _mm(y_mla, wts["w_proj_mla"], out_dtype=BF16, name="mm_proj_mla")
    merged = _merge_fwd(pr, pm, proj_g)
    o = _mm(merged, wts["w_out"], name="mm_out")
    x1, h2 = _resid_norm_fwd(x, o, gmod2)
    up = _mm(h2, wts["w_up"], name="mm_up")
    act = _ffn_act_fwd(up, fp)
    dn = _mm(act, wts["w_down"], name="mm_down")

    dx2, ddn, red_f = _final_fwd_bwd(x1, dn, target, fpar)
    dact = _mm(ddn, wts["w_down"], tb=True, name="mm_d_act")
    tok = emit("w_down", _mm(act, ddn, ta=True, out_dtype=BF16, name="mm_dw_down"))
    dup, red_ffn = _ffn_act_bwd(up, dact, fp + tok)
    dh2 = _mm(dup, wts["w_up"], tb=True, name="mm_d_h2")
    tok = tok + emit("w_up", _mm(dup, h2, ta=True, out_dtype=BF16, name="mm_dw_up"))
    dx1, do, red_2 = _norm2_bwd(x1, dh2, dx2, o, gmod2 + tok)
    dmerged = _mm(do, wts["w_out"], tb=True, name="mm_d_merged")
    tok = tok + emit("w_out", _mm(merged, do, ta=True, out_dtype=BF16, name="mm_dw_out"))
    dpr, dpm, dg = _merge_bwd(dmerged, pr, pm, proj_g)
    dy_rnn = _mm(dpr, wts["w_proj_rnn"], tb=True, name="mm_d_yrnn")
    tok = tok + emit("w_proj_rnn", _mm(hs_b, dpr, ta=True, out_dtype=BF16, name="mm_dw_proj_rnn"))
    dy_mla, dy_mla_t = _mm(dpm, wts["w_proj_mla"], tb=True, also_t=BF16, name="mm_d_ymla")
    tok = tok + emit("w_proj_mla", _mm(y_mla, dpm, ta=True, out_dtype=BF16, name="mm_dw_proj_mla"))
    dq_rot, dkvt, dkrt = _att_bwd(q_rot, q_rot_t, kv, kvt, kr, krt, y_mla, lse, dy_mla, dy_mla_t)
    dq = _rope_bwd(dq_rot, rot_c, rot_s)
    dqn = _mm(dq, w_uq_p, tb=True, name="mm_d_qn")
    dw_uq_pt = _mm(dq, qn, ta=True, out_dtype=BF16, name="mm_dw_uq")
    tok = tok + emit("w_uq", dw_uq_pt.reshape(N_HEADS, LANES, Q_LORA)[:, :hd].reshape(N_HEADS * hd, Q_LORA))
    dkvn = _mm(w_ukv, dkvt, also_t=F32, name="mm_d_kvn")[1]
    tok = tok + emit("w_ukv", _mm(dkvt, kvn, out_dtype=BF16, name="mm_dw_ukv"))
    dproj_mla, red_m = _mla_prep_bwd(proj_mla, dqn, dkvn, jnp.transpose(dkrt), rot_c, rot_s, ng + tok)
    dx_rnn, dwa_bd, dwx_bd, red_r = _rnn_bwd(proj_rnn, xc, ra, ix, hs, dy_rnn, keep, rp + tok, wa_bd, wx_bd)
    dw_in_t = jnp.concatenate([
        _mm(dx_rnn, h, ta=True, out_dtype=BF16, name="mm_dw_in_rnn"),
        _mm(dproj_mla, h, ta=True, out_dtype=BF16, name="mm_dw_in_mla")[:o_mla - o_rnn],
        _mm(dg, h, ta=True, out_dtype=BF16, name="mm_dw_in_g")], axis=0)
    tok = tok + emit("w_in", dw_in_t)
    dh_a = _mm(dx_rnn, w_in_rnn, tb=True, name="mm_d_h_rnn")
    dh_b = _mm(dproj_mla, w_in_mla, tb=True, name="mm_d_h_mla")
    dh_c = _mm(dg, w_in_g, tb=True, name="mm_d_h_g")
    grad_x, red_1 = _norm1_bwd(x, dh_a, dh_b, dh_c, dx1, gmod1 + tok)

    gs = {
        "norm1_g": red_1[0], "conv_w": red_r[0:4], "conv_b": red_r[4], "w_gate_a": _block_diag_t(dwa_bd),
        "b_gate_a": red_r[5], "w_gate_x": _block_diag_t(dwx_bd), "b_gate_x": red_r[6], "lru_param": red_r[7],
        "q_norm_g": red_m[0, :Q_LORA], "kv_norm_g": red_m[0, Q_LORA:Q_LORA + KV_LORA], "norm2_g": red_2[0],
        "ffn_conv_w": red_ffn[0:3], "ffn_conv_b": red_ffn[3], "final_g": red_f[0],
    }
    dmod = jnp.stack([red_1[2], red_1[1], red_2[3], red_2[2], red_2[1], red_f[1]], axis=0)
    return red_f[2, 0], grad_x, gs, dmod


MESH_IDS = pl.DeviceIdType.MESH
HBM_SPEC = pl.BlockSpec(memory_space=pltpu.HBM)


def _my_slot():
    return 4 * lax.axis_index("x") + 2 * lax.axis_index("y") + lax.axis_index("c")


def _all_gather(arrs, name):
    n = len(arrs)

    def body(*refs):
        ins, outs = refs[:n], refs[n:2 * n]
        send_sems, recv_sems, local_sems = refs[2 * n:]
        x, y, c = lax.axis_index("x"), lax.axis_index("y"), lax.axis_index("c")
        me, sibling = (x, y, c), (x, y, 1 - c)
        chips = [(1 - x, y), (x, 1 - y), (1 - x, 1 - y)]

        def slot(dev):
            return 4 * dev[0] + 2 * dev[1] + dev[2]

        def copy(a, k, block, to, src=None):
            dst = outs[a].at[slot(block)]
            return pltpu.make_async_remote_copy(
                src_ref=dst if src is None else src, dst_ref=dst, send_sem=send_sems.at[a, k], recv_sem=recv_sems.at[a, k],
                device_id=to, device_id_type=MESH_IDS)

        mine = [pltpu.make_async_copy(ins[a], outs[a].at[slot(me)], local_sems.at[a]) for a in range(n)]
        for cp in mine:
            cp.start()
        first = []
        for a in range(n):
            first.append(copy(a, 0, me, sibling, src=ins[a]))
            first += [copy(a, 1 + j, me, (*chip, c), src=ins[a]) for j, chip in enumerate(chips)]
        for cp in first:
            cp.start()
        passed = []
        for j, chip in enumerate(chips):
            for a in range(n):
                copy(a, 1 + j, (*chip, c), me).wait_recv()
                fwd = copy(a, 4 + j, (*chip, c), sibling)
                fwd.start()
                passed.append(fwd)
        for a in range(n):
            copy(a, 0, sibling, me).wait_recv()
            for j, chip in enumerate(chips):
                copy(a, 4 + j, (*chip, 1 - c), me).wait_recv()
        for cp in first + passed:
            cp.wait_send()
        for cp in mine:
            cp.wait()

    return pl.pallas_call(
        body, name=name,
        in_specs=[HBM_SPEC] * n, out_specs=[HBM_SPEC] * n,
        out_shape=[jax.ShapeDtypeStruct((N_DEV,) + a.shape, a.dtype) for a in arrs],
        scratch_shapes=[pltpu.SemaphoreType.DMA((n, 7)), pltpu.SemaphoreType.DMA((n, 7)), pltpu.SemaphoreType.DMA((n,))],
    )(*arrs)


def _assemble_columns(g, name):
    k, r, c = g.shape
    tr = _pick(r, (256, 128, 64, 32, 16))

    def body(g_ref, o_ref):
        for s in range(k):
            o_ref[:, s * c:(s + 1) * c] = g_ref[s]

    return pl.pallas_call(
        body, name=name, grid=(r // tr,),
        in_specs=[pl.BlockSpec((k, tr, c), lambda i: (0, i, 0))],
        out_specs=pl.BlockSpec((tr, k * c), lambda i: (i, 0)),
        out_shape=jax.ShapeDtypeStruct((r, k * c), g.dtype),
        compiler_params=_cparams(("arbitrary",)),
    )(g)


SEM_SPEC =pl.BlockSpec(memory_space=pltpu.SEMAPHORE)
DATAFLOW = pltpu.SideEffectType.DATAFLOW_SIDE_EFFECTING
FLIPS = [(dx, dy, dc) for dx in (0, 1) for dy in (0, 1) for dc in (0, 1)][1:]


def _peer(k):
    dx, dy, dc = FLIPS[k]
    peer = (lax.axis_index("x") ^ dx, lax.axis_index("y") ^ dy, lax.axis_index("c") ^ dc)
    return peer, 4 * peer[0] + 2 * peer[1] + peer[2]


def _gather_start(shards, after, name):
    n, nf = len(shards), len(FLIPS)

    def body(*refs):
        srcs, lands = refs[:n], refs[n:2 * n]
        send_sems, recv_sems = refs[2 * n + 1:3 * n + 1], refs[3 * n + 1:4 * n + 1]
        token = refs[-1]
        me = _my_slot()
        for a in range(n):
            for k in range(nf):
                peer, _ = _peer(k)
                pltpu.make_async_remote_copy(
                    src_ref=srcs[a], dst_ref=lands[a].at[me], send_sem=send_sems[a].at[k], recv_sem=recv_sems[a].at[k],
                    device_id=peer, device_id_type=MESH_IDS).start()
        token[...] = jnp.zeros(token.shape, F32)

    land_shapes = [(N_DEV,) + a.shape for a in shards]
    sems = [pltpu.SemaphoreType.DMA((nf,))] * n
    out = pl.pallas_call(
        body, name=name,
        out_shape=(*sems, *sems, *[pltpu.HBM(a.shape, a.dtype) for a in shards],
                   *[pltpu.HBM(shp, a.dtype) for shp, a in zip(land_shapes, shards)],
                   jax.ShapeDtypeStruct((SUBLANES, LANES), F32)),
        in_specs=[HBM_SPEC] * (2 * n) + [pl.BlockSpec(memory_space=pl.ANY)],
        out_specs=(*[SEM_SPEC] * (2 * n), *[HBM_SPEC] * (2 * n), pl.BlockSpec(memory_space=pltpu.VMEM)),
        input_output_aliases={i: 2 * n + i for i in range(2 * n)},
        compiler_params=pltpu.CompilerParams(has_side_effects=DATAFLOW),
    )(*[pltpu.with_memory_space_constraint(a, pltpu.HBM) for a in shards],
      *[pltpu.with_memory_space_constraint(lax.empty(shp, a.dtype), pltpu.HBM) for shp, a in zip(land_shapes, shards)],
      after)
    return [(out[a], out[n + a], out[2 * n + a], out[3 * n + a]) for a in range(n)], out[-1]


def _gather_wait(flights, after, name):
    n, nf = len(flights), len(FLIPS)

    def body(*refs):
        send_sems, recv_sems = refs[:n], refs[n:2 * n]
        srcs, lands = refs[2 * n:3 * n], refs[3 * n:4 * n]
        for a in range(n):
            for k in range(nf):
                peer, peer_slot = _peer(k)
                cp = pltpu.make_async_remote_copy(
                    src_ref=srcs[a], dst_ref=lands[a].at[peer_slot], send_sem=send_sems[a].at[k],
                    recv_sem=recv_sems[a].at[k], device_id=peer, device_id_type=MESH_IDS)
                cp.wait_send()
                cp.wait_recv()

    srcs, lands = [f[2] for f in flights], [f[3] for f in flights]
    out = pl.pallas_call(
        body, name=name,
        out_shape=(*[pltpu.HBM(a.shape, a.dtype) for a in srcs], *[pltpu.HBM(a.shape, a.dtype) for a in lands]),
        in_specs=[SEM_SPEC] * (2 * n) + [HBM_SPEC] * (2 * n) + [pl.BlockSpec(memory_space=pl.ANY)],
        out_specs=tuple([HBM_SPEC] * (2 * n)),
        input_output_aliases={2 * n + i: i for i in range(2 * n)},
        compiler_params=pltpu.CompilerParams(has_side_effects=DATAFLOW),
    )(*[f[0] for f in flights], *[f[1] for f in flights], *srcs, *lands, after)
    return list(out[n:])


ROW_ALIGN = 16


def _span_start(slot, rows):
    return (rows * slot) // ROW_ALIGN * ROW_ALIGN


def _chunk_of(src_ref, slot, rows, span):
    if rows is None:
        return src_ref.at[slot]
    return src_ref.at[pl.ds(pl.multiple_of(_span_start(slot, rows), ROW_ALIGN), span)]


def _scatter_start(src, name, rows=None, span=None):
    def body(src_ref, land_ref, send_sems, recv_sems, src_thru, land_thru, token):
        me = _my_slot()
        for k in range(len(FLIPS)):
            peer, peer_slot = _peer(k)
            pltpu.make_async_remote_copy(
                src_ref=_chunk_of(src_ref, peer_slot, rows, span), dst_ref=land_ref.at[me], send_sem=send_sems.at[k],
                recv_sem=recv_sems.at[k], device_id=peer, device_id_type=MESH_IDS).start()
        token[...] = jnp.zeros(token.shape, F32)

    n = len(FLIPS)
    land_shape = src.shape if rows is None else (N_DEV, span, src.shape[1])
    return pl.pallas_call(
        body, name=name,
        out_shape=(pltpu.SemaphoreType.DMA((n,)), pltpu.SemaphoreType.DMA((n,)), pltpu.HBM(src.shape, src.dtype),
                   pltpu.HBM(land_shape, src.dtype), jax.ShapeDtypeStruct((SUBLANES, LANES), F32)),
        in_specs=(HBM_SPEC, HBM_SPEC),
        out_specs=(SEM_SPEC, SEM_SPEC, HBM_SPEC, HBM_SPEC, pl.BlockSpec(memory_space=pltpu.VMEM)),
        input_output_aliases={0: 2, 1: 3},
        compiler_params=pltpu.CompilerParams(has_side_effects=DATAFLOW),
    )(pltpu.with_memory_space_constraint(src, pltpu.HBM),
      pltpu.with_memory_space_constraint(lax.empty(land_shape, src.dtype), pltpu.HBM))


def _scatter_wait(send_sems, recv_sems, src_thru, land_thru, after, name, rows=None, span=None):
    def body(src_ref, land_ref, send_sems, recv_sems, after_ref, src_dead, got_ref):
        for k in range(len(FLIPS)):
            peer, peer_slot = _peer(k)
            cp = pltpu.make_async_remote_copy(
                src_ref=_chunk_of(src_ref, peer_slot, rows, span), dst_ref=land_ref.at[peer_slot], send_sem=send_sems.at[k],
                recv_sem=recv_sems.at[k], device_id=peer, device_id_type=MESH_IDS)
            cp.wait_send()
            cp.wait_recv()

    return pl.pallas_call(
        body, name=name,
        out_shape=(pltpu.HBM(src_thru.shape, src_thru.dtype), pltpu.HBM(land_thru.shape, land_thru.dtype)),
        in_specs=(HBM_SPEC, HBM_SPEC, SEM_SPEC, SEM_SPEC, pl.BlockSpec(memory_space=pl.ANY)),
        out_specs=(HBM_SPEC, HBM_SPEC), input_output_aliases={0: 0, 1: 1},
        compiler_params=pltpu.CompilerParams(has_side_effects=DATAFLOW),
    )(src_thru, land_thru, send_sems, recv_sems, after)


def _sum_sources(parts, own, name):
    k, r, c = parts.shape
    tr = r if k * r * c <= 2 * 1024 * 1024 else _pick(r, (512, 256, 128, 64, 32, 16, 8))

    def body(p_ref, own_ref, o_ref):
        me = _my_slot()
        acc = jnp.where(me == 0, own_ref[...], p_ref[0]).astype(F32)
        for s in range(1, k):
            acc = acc + jnp.where(me == s, own_ref[...], p_ref[s]).astype(F32)
        o_ref[...] = acc

    blk = pl.BlockSpec((tr, c), lambda i: (i, 0))
    return pl.pallas_call(
        body, name=name, grid=(r // tr,),
        in_specs=[pl.BlockSpec((k, tr, c), lambda i: (0, i, 0)), blk],
        out_specs=blk,
        out_shape=jax.ShapeDtypeStruct((r, c), F32),
        compiler_params=_cparams(("arbitrary",)),
    )(parts, own)


def _adamw_math(g, w, m, v):
    m_new = ADAM_B1 * m + (1.0 - ADAM_B1) * g
    v_new = ADAM_B2 * v + (1.0 - ADAM_B2) * jnp.square(g)
    m_hat = m_new / (1.0 - ADAM_B1 ** ADAM_STEP)
    v_hat = v_new / (1.0 - ADAM_B2 ** ADAM_STEP)
    return -ADAM_LR * (m_hat / (jnp.sqrt(v_hat) + ADAM_EPS) + ADAM_WD * w), m_new, v_new


def _adamw_many(gs, ws, ms, vs, name):
    n = len(gs)

    def body(*refs):
        ins, outs = refs[:4 * n], refs[4 * n:]
        for i in range(n):
            g = ins[i][...]
            outs[4 * i][...] = g
            outs[4 * i + 1][...], outs[4 * i + 2][...], outs[4 * i + 3][...] = _adamw_math(
                g, ins[n + i][...], ins[2 * n + i][...], ins[3 * n + i][...])

    return pl.pallas_call(
        body, name=name,
        out_shape=[jax.ShapeDtypeStruct(a.shape, F32) for a in ws for _ in range(4)],
        compiler_params=_cparams(),
    )(*gs, *ws, *ms, *vs)


def _adamw(parts, w, m, v, name, own=None):
    k, r, c = parts.shape
    tr = r if r * c <= 256 * 1024 else _pick(r, (256, 128, 64, 32, 16, 8))

    def body(*refs):
        p_ref, w_ref, m_ref, v_ref = refs[:4]
        g_ref, d_ref, nm_ref, nv_ref = refs[-4:]

        def part(s):
            if own is None:
                return p_ref[s].astype(F32)
            return jnp.where(_my_slot() == s, refs[4][...], p_ref[s]).astype(F32)

        g = part(0)
        for s in range(1, k):
            g = g + part(s)
        g_ref[...] = g
        d_ref[...], nm_ref[...], nv_ref[...] = _adamw_math(g, w_ref[...], m_ref[...], v_ref[...])

    blk = pl.BlockSpec((tr, c), lambda i: (i, 0))
    return pl.pallas_call(
        body, name=name, grid=(r // tr,),
        in_specs=[pl.BlockSpec((k, tr, c), lambda i: (0, i, 0)), blk, blk, blk] + ([] if own is None else [blk]),
        out_specs=[blk] * 4,
        out_shape=[jax.ShapeDtypeStruct((r, c), F32)] * 4,
        compiler_params=_cparams(("arbitrary",)),
    )(parts, w, m, v, *([] if own is None else [own]))


def _silu(v):
    return v * _sigmoid(v)


def _ada_fwd(c_all, w, b):
    def body(c_ref, w_ref, b_ref, o_ref):
        ca = _silu(c_ref[...]).astype(BF16)
        o_ref[...] = jnp.dot(ca, w_ref[...].astype(BF16), preferred_element_type=F32) + b_ref[...]

    return pl.pallas_call(
        body, name="ada_fwd", out_shape=jax.ShapeDtypeStruct((c_all.shape[0], w.shape[1]), F32),
        compiler_params=_cparams(),
    )(c_all, w, b)


def _ada_bwd(c_all, dmod):
    def body(c_ref, d_ref, o_ref):
        ca = _silu(c_ref[...]).astype(BF16).astype(F32)
        dm = d_ref[...].astype(BF16).astype(F32)
        acc = jnp.zeros(o_ref.shape, F32)
        for bi in range(c_all.shape[0]):
            acc = acc + jnp.transpose(ca[bi:bi + 1, :]) * dm[bi:bi + 1, :]
        o_ref[...] = acc

    return pl.pallas_call(
        body, name="ada_bwd", out_shape=jax.ShapeDtypeStruct((c_all.shape[1], dmod.shape[1]), F32),
        compiler_params=_cparams(),
    )(c_all, dmod)


COL_SHARDED = ("w_in", "w_uq", "w_ukv", "w_up")
ROW_SHARDED = ("w_proj_rnn", "w_proj_mla", "w_out", "w_down")
REPLICATED = ("b_ada", "norm1_g", "conv_b", "w_gate_a", "b_gate_a", "w_gate_x", "b_gate_x", "lru_param", "q_norm_g",
              "kv_norm_g", "norm2_g", "ffn_conv_b", "final_g")
WEIGHTS = ("w_ada", "b_ada", "norm1_g", "w_in", "conv_w", "conv_b", "w_gate_a", "b_gate_a", "w_gate_x", "b_gate_x",
           "lru_param", "q_norm_g", "w_uq", "kv_norm_g", "w_ukv", "w_proj_rnn", "w_proj_mla", "w_out", "norm2_g", "w_up",
           "ffn_conv_w", "ffn_conv_b", "w_down", "final_g")
TRANSPOSED_GRADS = COL_SHARDED
PACK_LANES = 128


def _pack(vecs, row_multiple=SUBLANES):
    flat = jnp.concatenate([v.reshape(-1).astype(F32) for v in vecs])
    pad = (-flat.shape[0]) % (PACK_LANES * row_multiple)
    return jnp.concatenate([flat, jnp.zeros((pad,), F32)]).reshape(-1, PACK_LANES)


def _unpack(packed, shapes):
    flat = packed.reshape(-1)
    out, off = [], 0
    for shp in shapes:
        size = math.prod(shp)
        out.append(flat[off:off + size].reshape(shp))
        off += size
    return out


def kernel(x, c, positions, w_ada, b_ada, norm1_g, w_in, conv_w, conv_b, w_gate_a, b_gate_a, w_gate_x, b_gate_x, lru_param, q_norm_g, w_uq, kv_norm_g, w_ukv, w_proj_rnn, w_proj_mla, w_out, norm2_g, w_up, ffn_conv_w, ffn_conv_b, w_down, final_g, loss_target, m_w_ada, m_b_ada, m_norm1_g, m_w_in, m_conv_w, m_conv_b, m_w_gate_a, m_b_gate_a, m_w_gate_x, m_b_gate_x, m_lru_param, m_q_norm_g, m_w_uq, m_kv_norm_g, m_w_ukv, m_w_proj_rnn, m_w_proj_mla, m_w_out, m_norm2_g, m_w_up, m_ffn_conv_w, m_ffn_conv_b, m_w_down, m_final_g, v_w_ada, v_b_ada, v_norm1_g, v_w_in, v_conv_w, v_conv_b, v_w_gate_a, v_b_gate_a, v_w_gate_x, v_b_gate_x, v_lru_param, v_q_norm_g, v_w_uq, v_kv_norm_g, v_w_ukv, v_w_proj_rnn, v_w_proj_mla, v_w_out, v_norm2_g, v_w_up, v_ffn_conv_w, v_ffn_conv_b, v_w_down, v_final_g):
    args = dict(locals())
    w = {n: args[n] for n in WEIGHTS}
    m = {n: args["m_" + n] for n in WEIGHTS}
    v = {n: args["v_" + n] for n in WEIGHTS}
    s, d = x.shape[1], x.shape[2]
    me = _my_slot()
    def two_d(a):
        assert a.ndim == 3 and a.shape[0] == 1, a.shape
        return a[0]

    big = COL_SHARDED + ROW_SHARDED
    shard = {n: two_d(w[n]).astype(BF16) for n in big}

    def whole(n, g):
        k, r, cc = g.shape
        return _assemble_columns(g, "assemble_" + n) if n in COL_SHARDED else g.reshape(k * r, cc)

    first = _all_gather([shard["w_in"], c, two_d(conv_w), two_d(ffn_conv_w)], "gather_first")
    c_all = first[1].reshape(N_DEV, d)
    conv_w_all = jnp.transpose(first[2], (1, 0, 2)).reshape(conv_w.shape[1], -1)
    ffn_conv_w_all = jnp.transpose(first[3], (1, 0, 2)).reshape(ffn_conv_w.shape[1], -1)

    ada_cols = w_ada.shape[2]
    b_cols = lax.dynamic_slice(b_ada, (0, me * ada_cols), (1, ada_cols))
    mod_cols = _ada_fwd(c_all, w_ada[0], b_cols)
    mod_all, = _all_gather([mod_cols], "gather_mod")

    later = ("w_uq", "w_ukv", "w_proj_rnn", "w_proj_mla", "w_out", "w_up", "w_down")
    flights, started = _gather_start([shard[n] for n in later], mod_all, "gather_start")
    flight = dict(zip(later, flights))

    def fetch(names, after):
        lands = _gather_wait([flight[n] for n in names], after, "gather_wait_" + names[0])
        return {n: whole(n, lax.dynamic_update_index_in_dim(g, shard[n], me, 0)) for n, g in zip(names, lands)}

    mod = lax.dynamic_index_in_dim(mod_all, me, axis=1, keepdims=False).reshape(6, d) + started[0, 0]

    sm = {n: w[n][0] for n in REPLICATED if n not in ("b_ada", "final_g")}
    sm["final_g"] = final_g
    sm["conv_w"] = conv_w_all
    sm["ffn_conv_w"] = ffn_conv_w_all
    in_flight, windows = {}, {}

    def emit(n, g):
        rows = g.shape[0] // N_DEV
        if rows % ROW_ALIGN == 0:
            windows[n] = (None, None)
            g = g.reshape(N_DEV, rows, g.shape[1])
        else:
            span = max(rows * k - _span_start(k, rows) for k in range(N_DEV)) + rows
            windows[n] = (rows, -(-span // ROW_ALIGN) * ROW_ALIGN)
            assert _span_start(N_DEV - 1, rows) + windows[n][1] <= g.shape[0], (n, g.shape)
        *in_flight[n], token = _scatter_start(g, "scatter_start_" + n, *windows[n])
        return token[0, 0]

    sq, grad_x, gs, dmod = _local_step(x[0], mod, positions[0], loss_target[0], whole("w_in", first[0]), fetch, sm, emit)

    small_names = [n for n in REPLICATED if n != "b_ada"] + ["conv_w", "ffn_conv_w"]
    small_shapes = [gs[n].shape for n in small_names] + [(6 * d,), (1,)]
    partial = _pack([gs[n] for n in small_names] + [dmod, sq.reshape(1)], N_DEV * SUBLANES)
    *small_flight, small_started = _scatter_start(partial.reshape(N_DEV, -1, PACK_LANES), "scatter_small_start")

    grads, deltas, new_m, new_v = {}, {}, {}, {}

    def update(n, parts, own=None):
        shp = w[n].shape
        lay = jnp.transpose if n in TRANSPOSED_GRADS else (lambda a: a)
        res = _adamw(parts, lay(two_d(w[n])), lay(two_d(m[n])), lay(two_d(v[n])), "adamw_" + n, own)
        grads[n], deltas[n], new_m[n], new_v[n] = [lay(a).reshape(shp) for a in res]

    for n in big:
        rows, span = windows[n]
        src, landed = _scatter_wait(*in_flight[n], small_started, "scatter_wait_" + n, rows, span)
        if rows is None:
            update(n, landed, lax.dynamic_index_in_dim(src, me, axis=0, keepdims=False))
        else:
            start = _span_start(me, rows)
            own = lax.dynamic_slice(src, (start, 0), (span, src.shape[1]))
            total = _sum_sources(landed, own, "sum_" + n)
            update(n, lax.dynamic_slice(total, (rows * me - start, 0), (rows, src.shape[1]))[None])

    chunks, landed = _scatter_wait(*small_flight, new_v[big[-1]], "scatter_small_wait")
    mine = _sum_sources(landed, lax.dynamic_index_in_dim(chunks, me, axis=0, keepdims=False), "sum_small")
    summed_all, dmod_all = _all_gather([mine, dmod.reshape(1, 6 * d)], "gather_small")
    summed = _unpack(summed_all, small_shapes)
    g_small = dict(zip(small_names, summed[:len(small_names)]))
    g_small["b_ada"] = summed[len(small_names)]
    loss = 0.5 * summed[-1][0] / d
    dmod_cols = lax.dynamic_slice(dmod_all.reshape(N_DEV, 6 * d), (0, me * ada_cols), (N_DEV, ada_cols))

    update("w_ada", _ada_bwd(c_all, dmod_cols)[None])

    for n in ("conv_w", "ffn_conv_w"):
        cols = w[n].shape[2]
        g_small[n] = lax.dynamic_slice(g_small[n], (0, me * cols), (g_small[n].shape[0], cols))
    small = REPLICATED + ("conv_w", "ffn_conv_w")
    as_rows = lambda a: a.reshape(1, -1) if a.ndim == 1 else a
    res = _adamw_many([as_rows(g_small[n].reshape(w[n].shape)) for n in small], [as_rows(w[n]) for n in small],
                      [as_rows(m[n]) for n in small], [as_rows(v[n]) for n in small], "adamw_small")
    for i, n in enumerate(small):
        grads[n], deltas[n], new_m[n], new_v[n] = [a.reshape(w[n].shape) for a in res[4 * i:4 * i + 4]]

    return (loss, grad_x[None], *[grads[n] for n in WEIGHTS], *[deltas[n] for n in WEIGHTS],
            *[new_m[n] for n in WEIGHTS], *[new_v[n] for n in WEIGHTS])
```

```python
import functools
import math

import jax
import jax.numpy as jnp
from jax import lax
from jax.experimental import pallas as pl
from jax.experimental.pallas import tpu as pltpu

F32 = jnp.float32
BF16 = jnp.bfloat16

N_DEV = 8
LANES = 128
SUBLANES = 8
VMEM_LIMIT = 56 * 1024 * 1024

D_RNN = 1280
Q_LORA = 384
KV_LORA = 256
QK_NOPE = 64
QK_ROPE = 32
V_HEAD = 64
N_HEADS = 16
D_FF = 2816
ROPE_THETA = 10000.0
LRU_C = 8.0
EPS = 1e-6
MLA_W = 768
ATT_SCALE = 1.0 / math.sqrt(QK_NOPE + QK_ROPE)

ADAM_LR, ADAM_B1, ADAM_B2, ADAM_EPS, ADAM_WD, ADAM_STEP = 0.001, 0.9, 0.999, 1e-08, 0.01, 10


def _cparams(sem=None):
    return pltpu.CompilerParams(dimension_semantics=sem, vmem_limit_bytes=VMEM_LIMIT)


def _pick(n, prefs):
    for p in prefs:
        if n % p == 0:
            return p
    return n


def _sigmoid(v):
    return 0.5 * jnp.tanh(0.5 * v) + 0.5


def _lane(shape):
    return lax.broadcasted_iota(jnp.int32, shape, len(shape) - 1)


def _row(shape):
    return lax.broadcasted_iota(jnp.int32, shape, len(shape) - 2)


MM_BLOCK_BYTES = 36 * 1024 * 1024


def _divisors(n):
    return [t for t in range(n, 0, -LANES) if n % t == 0] if n % LANES == 0 else [n]


HBM_BYTES_PER_US = 3.0e6
MXU_FLOPS_PER_US = 8.0e8
GRID_STEP_US = 0.35


def _mm_tiles(m, n, k, a_bytes, b_bytes, o_bytes):
    best = None
    for tm in [t for t in _divisors(m) if t <= 1024]:
        for tn in [t for t in _divisors(n) if t <= 2048]:
            for tk in _divisors(k):
                nk = k // tk
                need = 2 * (tm * tk * a_bytes + tk * tn * b_bytes + tm * tn * o_bytes) + (tm * tn * 4 if nk > 1 else 0)
                if need > MM_BLOCK_BYTES:
                    continue
                gi, gj = m // tm, n // tn
                for rows_outer in (True, False):
                    if nk > 1:
                        a_reads, b_reads = gj, gi
                    elif rows_outer:
                        a_reads, b_reads = 1, (gi if gj > 1 else 1)
                    else:
                        a_reads, b_reads = (gj if gi > 1 else 1), 1
                    traffic = m * k * a_bytes * a_reads + k * n * b_bytes * b_reads + m * n * (o_bytes + (8 * nk if nk > 1 else 0))
                    cost = max(traffic / HBM_BYTES_PER_US, 2.0 * m * n * k / MXU_FLOPS_PER_US) + gi * gj * nk * GRID_STEP_US
                    if best is None or cost < best[0]:
                        best = (cost, tm, tn, tk, rows_outer)
                break
    if best is None:
        raise ValueError((m, n, k))
    return best[1:]


def _mm(a, b, *, ta=False, tb=False, out_dtype=F32, also_t=None, name):
    (k_a, m) = a.shape if ta else a.shape[::-1]
    (n, k_b) = b.shape if tb else b.shape[::-1]
    assert k_a == k_b, (a.shape, b.shape, ta, tb)
    k = k_a
    tm, tn, tk, rows_outer = _mm_tiles(m, n, k, a.dtype.itemsize, b.dtype.itemsize, jnp.dtype(out_dtype).itemsize)
    nk = k // tk
    dims = (((0 if ta else 1,), (1 if tb else 0,)), ((), ()))
    n_out = 1 if also_t is None else 2

    def body(a_ref, b_ref, *rest):
        outs, acc = rest[:n_out], rest[n_out:]
        part = lax.dot_general(a_ref[...].astype(BF16), b_ref[...].astype(BF16), dims, preferred_element_type=F32)

        def write(val):
            outs[0][...] = val.astype(out_dtype)
            if also_t is not None:
                outs[1][...] = jnp.transpose(val).astype(also_t)

        if nk == 1:
            write(part)
            return
        acc_ref, = acc
        kk = pl.program_id(2)

        @pl.when(kk == 0)
        def _():
            acc_ref[...] = part

        @pl.when(kk > 0)
        def _():
            acc_ref[...] += part

        @pl.when(kk == nk - 1)
        def _():
            write(acc_ref[...])

    def ij(f):
        return (lambda i, j, kk: f(i, j, kk)) if rows_outer else (lambda j, i, kk: f(i, j, kk))

    a_spec = pl.BlockSpec((tk, tm), ij(lambda i, j, kk: (kk, i))) if ta else pl.BlockSpec((tm, tk), ij(lambda i, j, kk: (i, kk)))
    b_spec = pl.BlockSpec((tn, tk), ij(lambda i, j, kk: (j, kk))) if tb else pl.BlockSpec((tk, tn), ij(lambda i, j, kk: (kk, j)))
    out_specs = [pl.BlockSpec((tm, tn), ij(lambda i, j, kk: (i, j)))]
    out_shape = [jax.ShapeDtypeStruct((m, n), out_dtype)]
    if also_t is not None:
        out_specs.append(pl.BlockSpec((tn, tm), ij(lambda i, j, kk: (j, i))))
        out_shape.append(jax.ShapeDtypeStruct((n, m), also_t))
    res = pl.pallas_call(
        body, name=name,
        grid=(m // tm, n // tn, nk) if rows_outer else (n // tn, m // tm, nk),
        in_specs=[a_spec, b_spec], out_specs=out_specs, out_shape=out_shape,
        scratch_shapes=[] if nk == 1 else [pltpu.VMEM((tm, tn), F32)],
        compiler_params=_cparams(("arbitrary", "arbitrary", "arbitrary")),
    )(a, b)
    return res[0] if also_t is None else res


ROW_BLOCK_BYTES = 28 * 1024 * 1024


def _rowwise(fn, row_ins, par_ins, out_defs, red_defs, *, name):
    s = row_ins[0].shape[0]
    row_bytes = sum(a.shape[1] * a.dtype.itemsize for a in row_ins) + sum(c * jnp.dtype(dt).itemsize for c, dt in out_defs)
    tr = next((t for t in (512, 256, 128) if s % t == 0 and 2 * t * row_bytes <= ROW_BLOCK_BYTES), min(s, 128))
    nr, npar, no = len(row_ins), len(par_ins), len(out_defs)

    def body(*refs):
        rin, pin = refs[:nr], refs[nr:nr + npar]
        outs, reds = refs[nr + npar:nr + npar + no], refs[nr + npar + no:]
        i = pl.program_id(0)

        @pl.when(i == 0)
        def _():
            for r in reds:
                r[...] = jnp.zeros_like(r)

        fn(i, rin, pin, outs, reds)

    in_specs = [pl.BlockSpec((tr, a.shape[1]), lambda i: (i, 0)) for a in row_ins]
    in_specs += [pl.BlockSpec(a.shape, lambda i, nd=a.ndim: (0,) * nd) for a in par_ins]
    out_specs = [pl.BlockSpec((tr, c), lambda i: (i, 0)) for c, _ in out_defs]
    out_specs += [pl.BlockSpec(shp, lambda i: (0, 0)) for shp in red_defs]
    out_shape = [jax.ShapeDtypeStruct((s, c), dt) for c, dt in out_defs]
    out_shape += [jax.ShapeDtypeStruct(shp, F32) for shp in red_defs]
    return pl.pallas_call(
        body, name=name, grid=(s // tr,), in_specs=in_specs, out_specs=out_specs, out_shape=out_shape,
        compiler_params=_cparams(("arbitrary",)),
    )(*row_ins, *par_ins)


def _rms(v):
    return lax.rsqrt(jnp.mean(v * v, axis=-1, keepdims=True) + EPS)


def _colsum(v):
    return jnp.sum(v, axis=0, keepdims=True)


def _rms_bwd(dn, n, rstd):
    return rstd * (dn - n * jnp.mean(dn * n, axis=-1, keepdims=True))


def _norm_mod_fwd(x, gmod, name):
    def fn(i, rin, pin, outs, reds):
        xv = rin[0][...]
        p = pin[0][...]
        n = xv * _rms(xv)
        outs[0][...] = ((n * p[0:1]) * (1.0 + p[1:2]) + p[2:3]).astype(BF16)

    return _rowwise(fn, [x], [gmod], [(x.shape[1], BF16)], [], name=name)[0]


def _rope(v, rot_c, rot_s):
    half = QK_ROPE // 2
    swapped = jnp.where(_lane(v.shape) < QK_NOPE + half, pltpu.roll(v, LANES - half, 1), pltpu.roll(v, half, 1))
    return v * rot_c + swapped * rot_s


def _rope_t(dv, rot_c, rot_s):
    half = QK_ROPE // 2
    ds = dv * rot_s
    lane = _lane(dv.shape)
    swapped = jnp.where(lane < QK_NOPE + half, pltpu.roll(ds, LANES - half, 1), pltpu.roll(ds, half, 1))
    in_rope = (lane >= QK_NOPE) & (lane < QK_NOPE + QK_ROPE)
    return dv * rot_c + jnp.where(in_rope, swapped, 0.0)


def _mla_prep_fwd(proj_mla, rot_c, rot_s, ng):
    o1, o2 = Q_LORA, Q_LORA + KV_LORA

    def fn(i, rin, pin, outs, reds):
        g = pin[0][...]
        ql = rin[0][:, 0:o1]
        kl = rin[0][:, o1:o2]
        outs[0][...] = (ql * _rms(ql) * g[0:1, 0:o1]).astype(BF16)
        outs[1][...] = (kl * _rms(kl) * g[0:1, o1:o2]).astype(BF16)
        kr = pltpu.roll(rin[0][:, o2:o2 + LANES], QK_NOPE, 1)
        outs[2][...] = _rope(kr, rin[1][...], rin[2][...]).astype(BF16)

    return _rowwise(fn, [proj_mla, rot_c, rot_s], [ng], [(Q_LORA, BF16), (KV_LORA, BF16), (LANES, BF16)], [],
                    name="mla_prep_fwd")


def _mla_prep_bwd(proj_mla, dqn, dkvn, dkr, rot_c, rot_s, ng):
    o1, o2 = Q_LORA, Q_LORA + KV_LORA

    def fn(i, rin, pin, outs, reds):
        g = pin[0][...]
        ql = rin[0][:, 0:o1]
        kl = rin[0][:, o1:o2]
        rq, rk = _rms(ql), _rms(kl)
        nq, nk = ql * rq, kl * rk
        dq, dk = rin[1][...], rin[2][...]
        outs[0][:, 0:o1] = _rms_bwd(dq * g[0:1, 0:o1], nq, rq).astype(BF16)
        outs[0][:, o1:o2] = _rms_bwd(dk * g[0:1, o1:o2], nk, rk).astype(BF16)
        dkr_pre = pltpu.roll(_rope_t(rin[3][...], rin[4][...], rin[5][...]), LANES - QK_NOPE, 1)
        outs[0][:, o2:] = jnp.where(_lane(dkr_pre.shape) < QK_ROPE, dkr_pre, 0.0).astype(BF16)
        reds[0][0:1, 0:o1] += _colsum(dq * nq)
        reds[0][0:1, o1:o2] += _colsum(dk * nk)

    return _rowwise(fn, [proj_mla, dqn, dkvn, dkr, rot_c, rot_s], [ng], [(MLA_W, BF16)], [(SUBLANES, MLA_W)],
                    name="mla_prep_bwd")


def _rope_bwd(dq, rot_c, rot_s):
    def fn(i, rin, pin, outs, reds):
        c, sn = rin[1][...] * Q_PRESCALE, rin[2][...] * Q_PRESCALE
        for h in range(N_HEADS):
            sl = slice(h * LANES, (h + 1) * LANES)
            outs[0][:, sl] = _rope_t(rin[0][:, sl], c, sn).astype(BF16)

    return _rowwise(fn, [dq, rot_c, rot_s], [], [(dq.shape[1], BF16)], [], name="rope_bwd")[0]


def _rope_fwd_t(q, rot_c, rot_s):
    s, c = q.shape
    tr = min(256, s)

    def body(q_ref, c_ref, s_ref, o_ref, ot_ref):
        cc, sn = c_ref[...] * Q_PRESCALE, s_ref[...] * Q_PRESCALE
        for h in range(N_HEADS):
            sl = slice(h * LANES, (h + 1) * LANES)
            rot = _rope(q_ref[:, sl], cc, sn)
            o_ref[:, sl] = rot.astype(BF16)
            ot_ref[sl, :] = jnp.transpose(rot).astype(BF16)

    return pl.pallas_call(
        body, name="rope_fwd", grid=(s // tr,),
        in_specs=[pl.BlockSpec((tr, c), lambda i: (i, 0)), pl.BlockSpec((tr, LANES), lambda i: (i, 0)),
                  pl.BlockSpec((tr, LANES), lambda i: (i, 0))],
        out_specs=[pl.BlockSpec((tr, c), lambda i: (i, 0)), pl.BlockSpec((c, tr), lambda i: (0, i))],
        out_shape=[jax.ShapeDtypeStruct((s, c), BF16), jax.ShapeDtypeStruct((c, s), BF16)],
        compiler_params=_cparams(("arbitrary",)),
    )(q, rot_c, rot_s)


def _merge_fwd(pr, pm, proj_g):
    d = pr.shape[1]

    def fn(i, rin, pin, outs, reds):
        g_rnn, g_mla = rin[2][:, 0:d].astype(F32), rin[2][:, d:].astype(F32)
        outs[0][...] = (_sigmoid(g_rnn) * rin[0][...].astype(F32) + _sigmoid(g_mla) * rin[1][...].astype(F32)).astype(BF16)

    return _rowwise(fn, [pr, pm, proj_g], [], [(d, BF16)], [], name="merge_fwd")[0]


def _merge_bwd(dmerged, pr, pm, proj_g):
    d = pr.shape[1]

    def fn(i, rin, pin, outs, reds):
        dm = rin[0][...]
        sr, sm = _sigmoid(rin[3][:, 0:d].astype(F32)), _sigmoid(rin[3][:, d:].astype(F32))
        outs[0][...] = (dm * sr).astype(BF16)
        outs[1][...] = (dm * sm).astype(BF16)
        outs[2][:, 0:d] = (dm * rin[1][...].astype(F32) * sr * (1.0 - sr)).astype(BF16)
        outs[2][:, d:] = (dm * rin[2][...].astype(F32) * sm * (1.0 - sm)).astype(BF16)

    return _rowwise(fn, [dmerged, pr, pm, proj_g], [], [(d, BF16), (d, BF16), (2 * d, BF16)], [], name="merge_bwd")


def _resid_norm_fwd(x, o, gmod):
    d = x.shape[1]

    def fn(i, rin, pin, outs, reds):
        p = pin[0][...]
        x1 = rin[0][...] + p[3:4] * rin[1][...]
        outs[0][...] = x1
        outs[1][...] = ((x1 * _rms(x1) * p[0:1]) * (1.0 + p[1:2]) + p[2:3]).astype(BF16)

    return _rowwise(fn, [x, o], [gmod], [(d, F32), (d, BF16)], [], name="resid_norm_fwd")


def _final_fwd_bwd(x1, dn, target, par):
    d = x1.shape[1]

    def fn(i, rin, pin, outs, reds):
        p = pin[0][...]
        dnv = rin[1][...]
        x2 = rin[0][...] + p[0:1] * dnv
        rstd = _rms(x2)
        n3 = x2 * rstd
        err = n3 * p[1:2] - rin[2][...]
        dy = err * (1.0 / d)
        dx2 = _rms_bwd(dy * p[1:2], n3, rstd)
        outs[0][...] = dx2
        outs[1][...] = (dx2 * p[0:1]).astype(BF16)
        reds[0][0:1, :] += _colsum(dy * n3)
        reds[0][1:2, :] += _colsum(dx2 * dnv)
        reds[0][2:3, :] += jnp.zeros((1, d), F32) + jnp.sum(err * err)

    return _rowwise(fn, [x1, dn, target], [par], [(d, F32), (d, BF16)], [(SUBLANES, d)], name="final_fwd_bwd")


def _norm2_bwd(x1, dh2, dx2, o, gmod):
    d = x1.shape[1]

    def fn(i, rin, pin, outs, reds):
        p = pin[0][...]
        x1v, dh = rin[0][...], rin[1][...]
        rstd = _rms(x1v)
        n2 = x1v * rstd
        dx1 = rin[2][...] + _rms_bwd(dh * (p[0:1] * (1.0 + p[1:2])), n2, rstd)
        outs[0][...] = dx1
        outs[1][...] = (dx1 * p[3:4]).astype(BF16)
        reds[0][0:1, :] += _colsum(dh * n2 * (1.0 + p[1:2]))
        reds[0][1:2, :] += _colsum(dh * n2 * p[0:1])
        reds[0][2:3, :] += _colsum(dh)
        reds[0][3:4, :] += _colsum(dx1 * rin[3][...])

    return _rowwise(fn, [x1, dh2, dx2, o], [gmod], [(d, F32), (d, BF16)], [(SUBLANES, d)], name="norm2_bwd")


def _norm1_bwd(x, dh_a, dh_b, dh_c, dx1, gmod):
    d = x.shape[1]

    def fn(i, rin, pin, outs, reds):
        p = pin[0][...]
        xv = rin[0][...]
        dh = rin[1][...] + rin[2][...] + rin[3][...]
        rstd = _rms(xv)
        n1 = xv * rstd
        outs[0][...] = rin[4][...] + _rms_bwd(dh * (p[0:1] * (1.0 + p[1:2])), n1, rstd)
        reds[0][0:1, :] += _colsum(dh * n1 * (1.0 + p[1:2]))
        reds[0][1:2, :] += _colsum(dh * n1 * p[0:1])
        reds[0][2:3, :] += _colsum(dh)

    return _rowwise(fn, [x, dh_a, dh_b, dh_c, dx1], [gmod], [(d, F32)], [(SUBLANES, d)], name="norm1_bwd")


RNN_CHUNK = 512


def _shift_down(ref, base, n, j):
    v = ref[pl.ds(base, n + SUBLANES), :]
    return v[SUBLANES:] if j == 0 else pltpu.roll(v, j, 0)[SUBLANES:]


def _shift_up(ref, base, n, j, top_pad):
    v = ref[pl.ds(base + top_pad, n + SUBLANES), :]
    return v[:n] if j == 0 else pltpu.roll(v, n + SUBLANES - j, 0)[:n]


SCAN_GROUP = 128


def _scan_sizes(s):
    sizes = [s]
    while sizes[-1] > SUBLANES:
        assert sizes[-1] % SUBLANES == 0, s
        sizes.append(sizes[-1] // SUBLANES)
    return sizes


def _scan_scratch(s):
    return [pltpu.VMEM((n + 2 * SUBLANES, LANES), F32) for n in _scan_sizes(s)[1:] for _ in range(2)]


def _linear_scan(a_ref, b_ref, out_ref, a_off, s, reverse, levels):
    sizes = _scan_sizes(s)
    lv = [(a_ref, b_ref, a_off, 0)] + [(levels[2 * i], levels[2 * i + 1], 0, SUBLANES) for i in range(len(sizes) - 1)]
    zero8 = jnp.zeros((SUBLANES, LANES), F32)
    for (ar, br, _, _), n in zip(lv[1:], sizes[1:]):
        br[0:SUBLANES, :] = zero8
        br[pl.ds(n + SUBLANES, SUBLANES), :] = zero8
    order = list(range(SUBLANES - 1, -1, -1)) if reverse else list(range(SUBLANES))

    for lvl in range(len(sizes) - 1):
        ar, br, aoff, off = lv[lvl]
        m = sizes[lvl + 1]
        g = min(m, SCAN_GROUP)
        for t0 in range(0, m, g):
            acc_a = acc_b = None
            for r in order:
                sa = pl.ds(off + SUBLANES * t0 + r + aoff, g, stride=SUBLANES)
                sb = pl.ds(off + SUBLANES * t0 + r, g, stride=SUBLANES)
                a, b = ar[sa, :], br[sb, :]
                if acc_a is None:
                    acc_a, acc_b = a, b
                else:
                    acc_b = a * acc_b + b
                    acc_a = a * acc_a
            lv[lvl + 1][0][pl.ds(SUBLANES + t0, g), :] = acc_a
            lv[lvl + 1][1][pl.ds(SUBLANES + t0, g), :] = acc_b

    ar, br, _, off = lv[-1]
    n = sizes[-1]
    a, b = ar[pl.ds(off, n), :], br[pl.ds(off, n), :]
    h, rows = jnp.zeros((1, LANES), F32), [None] * n
    for j in (range(n - 1, -1, -1) if reverse else range(n)):
        h = a[j:j + 1, :] * h + b[j:j + 1, :]
        rows[j] = h
    br[pl.ds(off, n), :] = jnp.concatenate(rows, axis=0)

    for lvl in range(len(sizes) - 2, -1, -1):
        ar, br, aoff, off = lv[lvl]
        m = sizes[lvl + 1]
        up = lv[lvl + 1][1]
        dst = out_ref if lvl == 0 else br
        g = min(m, SCAN_GROUP)
        for t0 in range(0, m, g):
            h = _shift_up(up, t0, g, 1, SUBLANES) if reverse else _shift_down(up, t0, g, 1)
            for r in order:
                sa = pl.ds(off + SUBLANES * t0 + r + aoff, g, stride=SUBLANES)
                sb = pl.ds(off + SUBLANES * t0 + r, g, stride=SUBLANES)
                h = ar[sa, :] * h + br[sb, :]
                dst[sb, :] = h


def _one_minus_exp(z):
    series = -z * (1.0 + z * (0.5 + z * (1.0 / 6.0 + z * (1.0 / 24.0 + z * (1.0 / 120.0 + z * (1.0 / 720.0))))))
    return jnp.where(z > -0.1, series, 1.0 - jnp.exp(z))


def _softplus(v):
    return jnp.maximum(v, 0.0) + jnp.log(1.0 + jnp.exp(-jnp.abs(v)))


def _rnn_gates(xc, w, wa, wx, sp):
    xb = xc.astype(BF16)
    ra = _sigmoid(jnp.dot(xb, wa, preferred_element_type=F32) + w[5:6])
    ix = _sigmoid(jnp.dot(xb, wx, preferred_element_type=F32) + w[6:7])
    la = (-LRU_C) * ra * sp
    a = jnp.exp(la)
    mult = jnp.sqrt(_one_minus_exp(2.0 * la))
    return ra, ix, a, mult


def _rnn_fwd(x_rnn, keep, rp, wa_bd, wx_bd):
    s, r = x_rnn.shape
    ts = min(RNN_CHUNK, s)

    def body(x_ref, keep_ref, rp_ref, wa_ref, wx_ref, xc_ref, ra_ref, ix_ref, hs_ref, hsb_ref, xpad, a_s, b_s, *levels):
        xpad[0:SUBLANES, :] = jnp.zeros((SUBLANES, LANES), F32)
        xpad[SUBLANES:, :] = x_ref[...]
        w = rp_ref[...]
        sp = _softplus(-w[7:8])
        wa, wx = wa_ref[0], wx_ref[0]

        def chunk(c, carry):
            base = pl.multiple_of(c * ts, ts)
            xc = w[4:5] + w[3:4] * _shift_down(xpad, base, ts, 0)
            for j in range(1, 4):
                xc = xc + w[3 - j:4 - j] * _shift_down(xpad, base, ts, j)
            ra, ix, a, mult = _rnn_gates(xc, w, wa, wx, sp)
            kp = keep_ref[pl.ds(base, ts), :]
            xc_ref[pl.ds(base, ts), :] = xc
            ra_ref[pl.ds(base, ts), :] = ra
            ix_ref[pl.ds(base, ts), :] = ix
            a_s[pl.ds(base, ts), :] = a * kp
            b_s[pl.ds(base, ts), :] = jnp.where(kp > 0.0, mult, 1.0) * (ix * xc)
            return carry

        lax.fori_loop(0, s // ts, chunk, 0)

        _linear_scan(a_s, b_s, hs_ref, 0, s, False, levels)
        hsb_ref[...] = hs_ref[...].astype(BF16)

    col = pl.BlockSpec((s, LANES), lambda g: (0, g))
    return pl.pallas_call(
        body, name="rnn_fwd", grid=(r // LANES,),
        in_specs=[col, pl.BlockSpec((s, 1), lambda g: (0, 0)), pl.BlockSpec((SUBLANES, LANES), lambda g: (0, g)),
                  pl.BlockSpec((1, LANES, LANES), lambda g: (g, 0, 0)), pl.BlockSpec((1, LANES, LANES), lambda g: (g, 0, 0))],
        out_specs=[col] * 5,
        out_shape=[jax.ShapeDtypeStruct((s, r), F32)] * 4 + [jax.ShapeDtypeStruct((s, r), BF16)],
        scratch_shapes=[pltpu.VMEM((s + SUBLANES, LANES), F32), pltpu.VMEM((s, LANES), F32), pltpu.VMEM((s, LANES), F32),
                        *_scan_scratch(s)],
        compiler_params=_cparams(("arbitrary",)),
    )(x_rnn, keep, rp, wa_bd, wx_bd)


def _rnn_bwd(x_rnn, xc, ra, ix, hs, dy, keep, rp, wa_bd, wx_bd):
    s, r = x_rnn.shape
    ts = min(RNN_CHUNK, s)

    def body(x_ref, xc_ref, ra_ref, ix_ref, hs_ref, dy_ref, keep_ref, rp_ref, wa_ref, wx_ref,
             dx_ref, dwa_ref, dwx_ref, red_ref, xpad, hpad, a_s, dh_s, dxc_s, *levels):
        zero8 = jnp.zeros((SUBLANES, LANES), F32)
        xpad[0:SUBLANES, :] = zero8
        xpad[SUBLANES:, :] = x_ref[...]
        hpad[0:SUBLANES, :] = zero8
        hpad[SUBLANES:, :] = hs_ref[...]
        a_s[s:, :] = zero8
        dxc_s[s:, :] = zero8
        w = rp_ref[...]
        sp = _softplus(-w[7:8])
        wa, wx = wa_ref[0], wx_ref[0]

        def decay(c, carry):
            base = pl.multiple_of(c * ts, ts)
            a = jnp.exp((-LRU_C) * ra_ref[pl.ds(base, ts), :] * sp)
            a_s[pl.ds(base, ts), :] = a * keep_ref[pl.ds(base, ts), :]
            return carry

        lax.fori_loop(0, s // ts, decay, 0)

        _linear_scan(a_s, dy_ref, dh_s, 1, s, True, levels)

        def gates(c, carry):
            dwa, dwx, d_ba, d_bx, d_sp, d_cb = carry
            base = pl.multiple_of(c * ts, ts)
            xcv = xc_ref[pl.ds(base, ts), :]
            rav = ra_ref[pl.ds(base, ts), :]
            ixv = ix_ref[pl.ds(base, ts), :]
            kp = keep_ref[pl.ds(base, ts), :]
            dh = dh_s[pl.ds(base, ts), :]
            h_prev = _shift_down(hpad, base, ts, 1)
            la = (-LRU_C) * rav * sp
            a = jnp.exp(la)
            mult = jnp.sqrt(_one_minus_exp(2.0 * la))
            mult_eff = jnp.where(kp > 0.0, mult, 1.0)
            d_a = dh * h_prev * kp
            d_mult = dh * (ixv * xcv) * kp
            d_ix = dh * mult_eff * xcv
            d_xc = dh * mult_eff * ixv
            d_la = d_a * a - d_mult * (a * a) / mult
            d_pa = d_la * ((-LRU_C) * sp) * rav * (1.0 - rav)
            d_px = d_ix * ixv * (1.0 - ixv)
            xb = xcv.astype(BF16)
            pab, pxb = d_pa.astype(BF16), d_px.astype(BF16)
            tn = (((0,), (0,)), ((), ()))
            nt_ = (((1,), (1,)), ((), ()))
            dwa = dwa + lax.dot_general(xb, pab, tn, preferred_element_type=F32)
            dwx = dwx + lax.dot_general(xb, pxb, tn, preferred_element_type=F32)
            d_xc = d_xc + lax.dot_general(pab, wa, nt_, preferred_element_type=F32)
            d_xc = d_xc + lax.dot_general(pxb, wx, nt_, preferred_element_type=F32)
            dxc_s[pl.ds(base, ts), :] = d_xc
            return (dwa, dwx, d_ba + _colsum(d_pa), d_bx + _colsum(d_px),
                    d_sp + _colsum(d_la * ((-LRU_C) * rav)), d_cb + _colsum(d_xc))

        z1 = jnp.zeros((1, LANES), F32)
        zw = jnp.zeros((LANES, LANES), F32)
        dwa, dwx, d_ba, d_bx, d_sp, d_cb = lax.fori_loop(0, s // ts, gates, (zw, zw, z1, z1, z1, z1))
        dwa_ref[0] = dwa
        dwx_ref[0] = dwx

        def conv(c, carry):
            base = pl.multiple_of(c * ts, ts)
            d_here = dxc_s[pl.ds(base, ts), :]
            dx = w[3:4] * d_here
            for j in range(1, 4):
                dx = dx + w[3 - j:4 - j] * _shift_up(dxc_s, base, ts, j, 0)
            dx_ref[pl.ds(base, ts), :] = dx.astype(BF16)
            return tuple(carry[k] + _colsum(d_here * _shift_down(xpad, base, ts, 3 - k)) for k in range(4))

        d_w = lax.fori_loop(0, s // ts, conv, (z1, z1, z1, z1))
        d_lru = d_sp * (-_sigmoid(-w[7:8]))
        red_ref[...] = jnp.concatenate(list(d_w) + [d_cb, d_ba, d_bx, d_lru], axis=0)

    col = pl.BlockSpec((s, LANES), lambda g: (0, g))
    par = pl.BlockSpec((SUBLANES, LANES), lambda g: (0, g))
    wsp = pl.BlockSpec((1, LANES, LANES), lambda g: (g, 0, 0))
    return pl.pallas_call(
        body, name="rnn_bwd", grid=(r // LANES,),
        in_specs=[col] * 6 + [pl.BlockSpec((s, 1), lambda g: (0, 0)), par, wsp, wsp],
        out_specs=[col, wsp, wsp, par],
        out_shape=[jax.ShapeDtypeStruct((s, r), BF16), jax.ShapeDtypeStruct((r // LANES, LANES, LANES), F32),
                   jax.ShapeDtypeStruct((r // LANES, LANES, LANES), F32), jax.ShapeDtypeStruct((SUBLANES, r), F32)],
        scratch_shapes=[pltpu.VMEM((s + SUBLANES, LANES), F32), pltpu.VMEM((s + SUBLANES, LANES), F32),
                        pltpu.VMEM((s + SUBLANES, LANES), F32), pltpu.VMEM((s, LANES), F32),
                        pltpu.VMEM((s + SUBLANES, LANES), F32), *_scan_scratch(s)],
        compiler_params=_cparams(("arbitrary",)),
    )(x_rnn, xc, ra, ix, hs, dy, keep, rp, wa_bd, wx_bd)


ATT_BLOCK = 512
ATT_FWD_HEADS = 8
ATT_BWD_HEADS = 4


LOG2E = 1.4426950408889634
LN2 = 0.6931471805599453
Q_PRESCALE = ATT_SCALE * LOG2E


def _att_scores(q, kvt, krt, diagonal):
    kt_eff = jnp.where(_row(kvt.shape) < QK_NOPE, kvt, krt)
    sc = jnp.dot(q, kt_eff, preferred_element_type=F32)
    if diagonal:
        sc = jnp.where(lax.broadcasted_iota(jnp.int32, sc.shape, 1) <= lax.broadcasted_iota(jnp.int32, sc.shape, 0), sc, -jnp.inf)
    return sc


def _att_fwd(q, kv, kvt, krt):
    s = q.shape[0]
    t = min(ATT_BLOCK, s)
    nb = s // t
    hp = ATT_FWD_HEADS

    pairs = [(i, j) for i in range(nb) for j in range(i + 1)]
    i_tab = jnp.array([p[0] for p in pairs], jnp.int32)
    j_tab = jnp.array([p[1] for p in pairs], jnp.int32)

    def body(i_ref, j_ref, q_ref, kv_ref, kvt_ref, krt_ref, y_ref, lse_ref, m_s, acc_s):
        i, j = i_ref[pl.program_id(1)], j_ref[pl.program_id(1)]

        @pl.when(j == 0)
        def _():
            m_s[...] = jnp.full(m_s.shape, -jnp.inf, F32)
            acc_s[...] = jnp.zeros(acc_s.shape, F32)

        def step(diagonal):
            krt_b = krt_ref[...]
            lane = _lane((t, LANES))
            groups = [slice(c * LANES, (c + 1) * LANES) for c in range(t // LANES)]
            heads = [slice(hh * LANES, (hh + 1) * LANES) for hh in range(hp)]
            scs = [_att_scores(q_ref[:, sl], kvt_ref[sl, :], krt_b, diagonal) for sl in heads]
            stats = []
            for hh in range(hp):
                m_prev = m_s[hh]
                m_blk = scs[hh][:, groups[0]]
                for g in groups[1:]:
                    m_blk = jnp.maximum(m_blk, scs[hh][:, g])
                stats.append((m_prev, jnp.maximum(m_prev, jnp.max(m_blk, axis=-1, keepdims=True))))
            for hh in range(hp):
                m_prev, m_new = stats[hh]
                kvb = kv_ref[:, heads[hh]]
                ones_v = jnp.where(lane < QK_NOPE, jnp.ones_like(kvb), kvb)
                p = jnp.concatenate([jnp.exp2(scs[hh][:, g] - m_new).astype(BF16) for g in groups], axis=1)
                acc_s[hh] = jnp.exp2(m_prev - m_new) * acc_s[hh] + jnp.dot(p, ones_v, preferred_element_type=F32)
                m_s[hh] = m_new

        @pl.when(j < i)
        def _():
            step(False)

        @pl.when(j == i)
        def _():
            step(True)
            lane = _lane((t, LANES))
            for g in range(hp // 2):
                sl = slice(g * LANES, (g + 1) * LANES)
                a0, a1 = acc_s[2 * g], acc_s[2 * g + 1]
                l0, l1 = a0[:, 0:1], a1[:, 0:1]
                y_ref[:, sl] = jnp.where(lane < V_HEAD, pltpu.roll(a0 / l0, V_HEAD, 1), a1 / l1).astype(BF16)
                lse_ref[:, sl] = jnp.where(lane < V_HEAD, m_s[2 * g] + jnp.log(l0) * LOG2E, m_s[2 * g + 1] + jnp.log(l1) * LOG2E)

    return pl.pallas_call(
        body, name="att_fwd",
        grid_spec=pltpu.PrefetchScalarGridSpec(
            num_scalar_prefetch=2, grid=(N_HEADS // hp, len(pairs)),
            in_specs=[pl.BlockSpec((t, hp * LANES), lambda p, n, it, jt: (it[n], p)),
                      pl.BlockSpec((t, hp * LANES), lambda p, n, it, jt: (jt[n], p)),
                      pl.BlockSpec((hp * LANES, t), lambda p, n, it, jt: (p, jt[n])),
                      pl.BlockSpec((LANES, t), lambda p, n, it, jt: (0, jt[n]))],
            out_specs=[pl.BlockSpec((t, hp // 2 * LANES), lambda p, n, it, jt: (it[n], p))] * 2,
            scratch_shapes=[pltpu.VMEM((hp, t, LANES), F32)] * 2),
        out_shape=[jax.ShapeDtypeStruct((s, N_HEADS * V_HEAD), BF16), jax.ShapeDtypeStruct((s, N_HEADS * V_HEAD), F32)],
        compiler_params=_cparams(("arbitrary", "arbitrary")),
    )(i_tab, j_tab, q, kv, kvt, krt)


def _att_bwd(q, qt, kv, kvt, kr, krt, y, lse, dy, dyt):
    s = q.shape[0]
    t = min(ATT_BLOCK, s)
    nb = s // t

    hp = ATT_BWD_HEADS
    pairs = [(i, j) for j in range(nb) for i in range(j, nb)]
    i_tab = jnp.array([p[0] for p in pairs], jnp.int32)
    j_tab = jnp.array([p[1] for p in pairs], jnp.int32)

    def body(i_ref, j_ref, q_ref, qt_ref, kv_ref, kvt_ref, kr_ref, krt_ref, y_ref, lse_ref, dy_ref, dyt_ref,
             dq_ref, dkvt_ref, dkrt_ref, dkv_s):
        p_, n = pl.program_id(0), pl.program_id(1)
        i, j = i_ref[n], j_ref[n]

        @pl.when((p_ == 0) & (n == 0))
        def _():
            dkrt_ref[...] = jnp.zeros(dkrt_ref.shape, F32)

        @pl.when(n == 0)
        def _():
            dq_ref[...] = jnp.zeros(dq_ref.shape, F32)

        @pl.when(i == j)
        def _():
            dkv_s[...] = jnp.zeros(dkv_s.shape, F32)

        def step(diagonal):
            lane = _lane((t, LANES))
            row = _row((LANES, t))
            krb, krt_b = kr_ref[...], krt_ref[...]
            dyv = dy_ref[...]
            yv = y_ref[...].astype(F32)
            lsev = lse_ref[...]
            dyt_b = dyt_ref[...]
            rows = pl.ds(pl.multiple_of(i * t, t), t)
            cols = pl.ds(pl.multiple_of(j * t, t), t)
            zeros_t = jnp.zeros((V_HEAD, t), BF16)
            ones_w = jnp.ones((LANES, LANES), BF16)
            groups = [slice(c * LANES, (c + 1) * LANES) for c in range(t // LANES)]
            heads = [slice(hh * LANES, (hh + 1) * LANES) for hh in range(hp)]
            scs, dps, stats = [], [], []
            for hh, sl in enumerate(heads):
                kvt_b = kvt_ref[sl, :]
                scs.append(_att_scores(q_ref[:, sl], kvt_b, krt_b, diagonal))
                pair, first = heads[hh // 2], hh % 2 == 0
                lse_g, dy_g, y_g = lsev[:, pair], dyv[:, pair], yv[:, pair]
                mine = (lane < V_HEAD) if first else (lane >= V_HEAD)
                lse_rep = jnp.where(mine, lse_g, pltpu.roll(lse_g, V_HEAD, 1))
                do_pad = jnp.where(lane >= V_HEAD, pltpu.roll(dy_g, V_HEAD, 1) if first else dy_g, 0.0)
                o_pad = jnp.where(lane >= V_HEAD, pltpu.roll(y_g, V_HEAD, 1) if first else y_g, 0.0)
                do_ln2 = do_pad * LN2
                prod = do_ln2 * o_pad
                head_part = prod.astype(BF16)
                rest_part = (prod - head_part.astype(F32)).astype(BF16)
                delta_rep = (jnp.dot(head_part, ones_w, preferred_element_type=F32)
                             + jnp.dot(rest_part, ones_w, preferred_element_type=F32))
                dps.append(jnp.dot(do_ln2.astype(BF16), kvt_b, preferred_element_type=F32))
                stats.append((lse_rep, delta_rep))
            dkr_acc = jnp.zeros((LANES, t), F32)
            for hh, sl in enumerate(heads):
                lse_rep, delta_rep = stats[hh]
                probs, dss = [], []
                for g in groups:
                    pg = jnp.exp2(scs[hh][:, g] - lse_rep)
                    probs.append(pg.astype(BF16))
                    dss.append((pg * (dps[hh][:, g] - delta_rep)).astype(BF16))
                prob, ds = jnp.concatenate(probs, axis=1), jnp.concatenate(dss, axis=1)
                dot_pad = jnp.concatenate([zeros_t, dyt_b[hh * V_HEAD:(hh + 1) * V_HEAD, :]], axis=0)
                k_eff = jnp.where(lane < QK_NOPE, kv_ref[:, sl], krb)
                dvt = jnp.dot(dot_pad, prob, preferred_element_type=F32)
                dq_ref[rows, sl] += jnp.dot(ds, k_eff, preferred_element_type=F32)
                dkt = jnp.dot(qt_ref[sl, :], ds, preferred_element_type=F32)
                dkv_s[hh] += dvt + jnp.where(row < QK_NOPE, dkt, 0.0)
                dkr_acc = dkr_acc + jnp.where(row >= QK_NOPE, dkt, 0.0)
            dkrt_ref[:, cols] += dkr_acc

        @pl.when(i > j)
        def _():
            step(False)

        @pl.when(i == j)
        def _():
            step(True)

        @pl.when(i == nb - 1)
        def _():
            for hh, sl in enumerate([slice(hh * LANES, (hh + 1) * LANES) for hh in range(hp)]):
                dkvt_ref[sl, :] = dkv_s[hh].astype(BF16)

    qi = lambda p, n, it, jt: (it[n], p)
    qti = lambda p, n, it, jt: (p, it[n])
    wide, half = hp * LANES, hp // 2 * LANES
    return pl.pallas_call(
        body, name="att_bwd",
        grid_spec=pltpu.PrefetchScalarGridSpec(
            num_scalar_prefetch=2, grid=(N_HEADS // hp, len(pairs)),
            in_specs=[pl.BlockSpec((t, wide), qi), pl.BlockSpec((wide, t), qti),
                      pl.BlockSpec((t, wide), lambda p, n, it, jt: (jt[n], p)),
                      pl.BlockSpec((wide, t), lambda p, n, it, jt: (p, jt[n])),
                      pl.BlockSpec((t, LANES), lambda p, n, it, jt: (jt[n], 0)),
                      pl.BlockSpec((LANES, t), lambda p, n, it, jt: (0, jt[n])),
                      pl.BlockSpec((t, half), qi), pl.BlockSpec((t, half), qi), pl.BlockSpec((t, half), qi),
                      pl.BlockSpec((half, t), qti)],
            out_specs=[pl.BlockSpec((s, wide), lambda p, n, it, jt: (0, p)),
                       pl.BlockSpec((wide, t), lambda p, n, it, jt: (p, jt[n])),
                       pl.BlockSpec((LANES, s), lambda p, n, it, jt: (0, 0))],
            scratch_shapes=[pltpu.VMEM((hp, LANES, t), F32)]),
        out_shape=[jax.ShapeDtypeStruct((s, N_HEADS * LANES), F32), jax.ShapeDtypeStruct((N_HEADS * LANES, s), BF16),
                   jax.ShapeDtypeStruct((LANES, s), F32)],
        compiler_params=_cparams(("arbitrary", "arbitrary")),
    )(i_tab, j_tab, q, qt, kv, kvt, kr, krt, y, lse, dy, dyt)


FFN_COLS = 256


def _ffn_conv(pad_ref, w, base, n):
    u = w[3:4] + w[2:3] * _shift_down(pad_ref, base, n, 0)
    for j in range(1, 3):
        u = u + w[2 - j:3 - j] * _shift_down(pad_ref, base, n, j)
    return u


def _ffn_act_fwd(up, fp):
    s, f2 = up.shape
    f = f2 // 2
    tc = FFN_COLS
    ts = min(RNN_CHUNK, s)
    nfb = f // tc

    def body(ug_ref, uv_ref, wg_ref, wv_ref, act_ref, gpad, vpad):
        zero8 = jnp.zeros((SUBLANES, tc), F32)
        gpad[0:SUBLANES, :] = zero8
        vpad[0:SUBLANES, :] = zero8
        gpad[SUBLANES:, :] = ug_ref[...].astype(F32)
        vpad[SUBLANES:, :] = uv_ref[...].astype(F32)
        wg, wv = wg_ref[...], wv_ref[...]

        def chunk(c, carry):
            base = pl.multiple_of(c * ts, ts)
            g = _ffn_conv(gpad, wg, base, ts)
            v = _ffn_conv(vpad, wv, base, ts)
            act_ref[pl.ds(base, ts), :] = (g * _sigmoid(g) * v).astype(BF16)
            return carry

        lax.fori_loop(0, s // ts, chunk, 0)

    return pl.pallas_call(
        body, name="ffn_act_fwd", grid=(nfb,),
        in_specs=[pl.BlockSpec((s, tc), lambda b: (0, b)), pl.BlockSpec((s, tc), lambda b: (0, b + nfb)),
                  pl.BlockSpec((SUBLANES, tc), lambda b: (0, b)), pl.BlockSpec((SUBLANES, tc), lambda b: (0, b + nfb))],
        out_specs=pl.BlockSpec((s, tc), lambda b: (0, b)),
        out_shape=jax.ShapeDtypeStruct((s, f), BF16),
        scratch_shapes=[pltpu.VMEM((s + SUBLANES, tc), F32)] * 2,
        compiler_params=_cparams(("arbitrary",)),
    )(up, up, fp, fp)


def _ffn_act_bwd(up, dact, fp):
    s, f2 = up.shape
    f = f2 // 2
    tc = FFN_COLS
    ts = min(RNN_CHUNK, s)
    nfb = f // tc

    def body(ug_ref, uv_ref, da_ref, wg_ref, wv_ref, dup_ref, red_ref, gpad, vpad, dgs, dvs):
        half = pl.program_id(1)
        wg, wv = wg_ref[...], wv_ref[...]

        @pl.when(half == 0)
        def _():
            zero8 = jnp.zeros((SUBLANES, tc), F32)
            gpad[0:SUBLANES, :] = zero8
            vpad[0:SUBLANES, :] = zero8
            gpad[SUBLANES:, :] = ug_ref[...].astype(F32)
            vpad[SUBLANES:, :] = uv_ref[...].astype(F32)
            dgs[s:, :] = zero8
            dvs[s:, :] = zero8

            def act(c, carry):
                base = pl.multiple_of(c * ts, ts)
                g = _ffn_conv(gpad, wg, base, ts)
                v = _ffn_conv(vpad, wv, base, ts)
                da = da_ref[pl.ds(base, ts), :].astype(F32)
                sg = _sigmoid(g)
                dgs[pl.ds(base, ts), :] = da * v * (sg * (1.0 + g * (1.0 - sg)))
                dvs[pl.ds(base, ts), :] = da * (g * sg)
                return carry

            lax.fori_loop(0, s // ts, act, 0)

        def conv_t(src, pad, w, out_ref, red_ref):
            def chunk(c, carry):
                base = pl.multiple_of(c * ts, ts)
                d_here = src[pl.ds(base, ts), :]
                dx = w[2:3] * d_here
                for j in range(1, 3):
                    dx = dx + w[2 - j:3 - j] * _shift_up(src, base, ts, j, 0)
                out_ref[pl.ds(base, ts), :] = dx.astype(BF16)
                taps = tuple(carry[k] + _colsum(d_here * _shift_down(pad, base, ts, 2 - k)) for k in range(3))
                return taps + (carry[3] + _colsum(d_here),)

            z1 = jnp.zeros((1, tc), F32)
            red = lax.fori_loop(0, s // ts, chunk, (z1, z1, z1, z1))
            red_ref[...] = jnp.concatenate(list(red) + [jnp.zeros((4, tc), F32)], axis=0)

        @pl.when(half == 0)
        def _():
            conv_t(dgs, gpad, wg, dup_ref, red_ref)

        @pl.when(half == 1)
        def _():
            conv_t(dvs, vpad, wv, dup_ref, red_ref)

    gcol = pl.BlockSpec((s, tc), lambda b, h: (0, b))
    vcol = pl.BlockSpec((s, tc), lambda b, h: (0, b + nfb))
    gpar = pl.BlockSpec((SUBLANES, tc), lambda b, h: (0, b))
    vpar = pl.BlockSpec((SUBLANES, tc), lambda b, h: (0, b + nfb))
    return pl.pallas_call(
        body, name="ffn_act_bwd", grid=(nfb, 2),
        in_specs=[gcol, vcol, gcol, gpar, vpar],
        out_specs=[pl.BlockSpec((s, tc), lambda b, h: (0, b + h * nfb)),
                   pl.BlockSpec((SUBLANES, tc), lambda b, h: (0, b + h * nfb))],
        out_shape=[jax.ShapeDtypeStruct((s, f2), BF16), jax.ShapeDtypeStruct((SUBLANES, f2), F32)],
        scratch_shapes=[pltpu.VMEM((s + SUBLANES, tc), F32)] * 4,
        compiler_params=_cparams(("arbitrary", "arbitrary")),
    )(up, up, dact, fp, fp)


def _rows8(rows, width):
    rows = [r.reshape(1, width).astype(F32) for r in rows]
    return jnp.concatenate(rows + [jnp.zeros((SUBLANES - len(rows), width), F32)], axis=0)


def _block_diag(w):
    n, b, _ = w.shape
    w = w.reshape(n // 2, 2, b, b)
    z = jnp.zeros((n // 2, b, b), w.dtype)
    top = jnp.concatenate([w[:, 0], z], axis=2)
    bot = jnp.concatenate([z, w[:, 1]], axis=2)
    return jnp.concatenate([top, bot], axis=1)


def _block_diag_t(bd):
    n, b2, _ = bd.shape
    b = b2 // 2
    return jnp.stack([bd[:, :b, :b], bd[:, b:, b:]], axis=1).reshape(2 * n, b, b)


def _local_step(x, mod, positions, target, w_in, fetch, sm, emit):
    s, d = x.shape
    o_rnn, o_mla = D_RNN, D_RNN + Q_LORA + KV_LORA + QK_ROPE
    wts = {}
    w_in_rnn = w_in[:, :o_rnn]
    w_in_mla = jnp.concatenate([w_in[:, o_rnn:o_mla], jnp.zeros((d, MLA_W - (o_mla - o_rnn)), w_in.dtype)], axis=1)
    w_in_g = w_in[:, o_mla:]
    hd = QK_NOPE + QK_ROPE
    wa_bd = _block_diag(sm["w_gate_a"]).astype(BF16)
    wx_bd = _block_diag(sm["w_gate_x"]).astype(BF16)

    pos = positions.reshape(s)
    half = QK_ROPE // 2
    inv_freq = ROPE_THETA ** (-jnp.arange(half, dtype=F32) / half)
    ang = pos.astype(F32)[:, None] * inv_freq
    cos, sin = jnp.cos(ang), jnp.sin(ang)
    rot_c = jnp.concatenate([jnp.ones((s, QK_NOPE), F32), cos, cos, jnp.ones((s, LANES - hd), F32)], axis=1)
    rot_s = jnp.concatenate([jnp.zeros((s, QK_NOPE), F32), -sin, sin, jnp.zeros((s, LANES - hd), F32)], axis=1)
    keep = (pos != 0).astype(F32).reshape(s, 1)

    gmod1 = _rows8([sm["norm1_g"], mod[1], mod[0]], d)
    gmod2 = _rows8([sm["norm2_g"], mod[4], mod[3], mod[2]], d)
    rp = jnp.concatenate([sm["conv_w"].reshape(4, D_RNN), _rows8([sm["conv_b"], sm["b_gate_a"], sm["b_gate_x"], sm["lru_param"]], D_RNN)[:4]], axis=0)
    fp = _rows8([sm["ffn_conv_w"][0], sm["ffn_conv_w"][1], sm["ffn_conv_w"][2], sm["ffn_conv_b"]], 2 * D_FF)
    ng = _rows8([jnp.concatenate([sm["q_norm_g"].reshape(-1), sm["kv_norm_g"].reshape(-1), jnp.zeros((MLA_W - Q_LORA - KV_LORA,), F32)])], MLA_W)
    fpar = _rows8([mod[5], sm["final_g"]], d)

    h = _norm_mod_fwd(x, gmod1, "norm1_fwd")
    proj_rnn = _mm(h, w_in_rnn, name="mm_in_rnn")
    proj_mla = _mm(h, w_in_mla, name="mm_in_mla")
    proj_g = _mm(h, w_in_g, out_dtype=BF16, name="mm_in_g")
    xc, ra, ix, hs, hs_b = _rnn_fwd(proj_rnn, keep, rp, wa_bd, wx_bd)
    qn, kvn, kr = _mla_prep_fwd(proj_mla, rot_c, rot_s, ng)
    wts.update(fetch(("w_uq", "w_ukv"), kr))
    w_uq_p = jnp.pad(wts["w_uq"].reshape(Q_LORA, N_HEADS, hd), ((0, 0), (0, 0), (0, LANES - hd))).reshape(Q_LORA, N_HEADS * LANES)
    w_ukv = wts["w_ukv"]
    q_rot, q_rot_t = _rope_fwd_t(_mm(qn, w_uq_p, name="mm_uq"), rot_c, rot_s)
    kv, kvt = _mm(kvn, w_ukv, out_dtype=BF16, also_t=BF16, name="mm_ukv")
    krt = jnp.transpose(kr)
    y_mla, lse = _att_fwd(q_rot, kv, kvt, krt)
    wts.update(fetch(("w_proj_rnn", "w_proj_mla", "w_out", "w_up", "w_down"), lse))
    pr = _mm(hs_b, wts["w_proj_rnn"], out_dtype=BF16, name="mm_proj_rnn")
    pm = _mm(y_mla, wts["w_proj_mla"], out_dtype=BF16, name="mm_proj_mla")
    merged = _merge_fwd(pr, pm, proj_g)
    o = _mm(merged, wts["w_out"], name="mm_out")
    x1, h2 = _resid_norm_fwd(x, o, gmod2)
    up = _mm(h2, wts["w_up"], out_dtype=BF16, name="mm_up")
    act = _ffn_act_fwd(up, fp)
    dn = _mm(act, wts["w_down"], name="mm_down")

    dx2, ddn, red_f = _final_fwd_bwd(x1, dn, target, fpar)
    dact = _mm(ddn, wts["w_down"], tb=True, out_dtype=BF16, name="mm_d_act")
    tok = emit("w_down", _mm(act, ddn, ta=True, out_dtype=BF16, name="mm_dw_down"))
    dup, red_ffn = _ffn_act_bwd(up, dact, fp + tok)
    dh2 = _mm(dup, wts["w_up"], tb=True, name="mm_d_h2")
    tok = tok + emit("w_up", _mm(dup, h2, ta=True, out_dtype=BF16, name="mm_dw_up"))
    dx1, do, red_2 = _norm2_bwd(x1, dh2, dx2, o, gmod2 + tok)
    dmerged = _mm(do, wts["w_out"], tb=True, name="mm_d_merged")
    tok = tok + emit("w_out", _mm(merged, do, ta=True, out_dtype=BF16, name="mm_dw_out"))
    dpr, dpm, dg = _merge_bwd(dmerged, pr, pm, proj_g)
    dy_rnn = _mm(dpr, wts["w_proj_rnn"], tb=True, name="mm_d_yrnn")
    tok = tok + emit("w_proj_rnn", _mm(hs_b, dpr, ta=True, out_dtype=BF16, name="mm_dw_proj_rnn"))
    dy_mla, dy_mla_t = _mm(dpm, wts["w_proj_mla"], tb=True, also_t=BF16, name="mm_d_ymla")
    tok = tok + emit("w_proj_mla", _mm(y_mla, dpm, ta=True, out_dtype=BF16, name="mm_dw_proj_mla"))
    dq_rot, dkvt, dkrt = _att_bwd(q_rot, q_rot_t, kv, kvt, kr, krt, y_mla, lse, dy_mla, dy_mla_t)
    dq = _rope_bwd(dq_rot, rot_c, rot_s)
    dqn = _mm(dq, w_uq_p, tb=True, name="mm_d_qn")
    dw_uq_pt = _mm(dq, qn, ta=True, out_dtype=BF16, name="mm_dw_uq")
    tok = tok + emit("w_uq", dw_uq_pt.reshape(N_HEADS, LANES, Q_LORA)[:, :hd].reshape(N_HEADS * hd, Q_LORA))
    dkvn = _mm(w_ukv, dkvt, also_t=F32, name="mm_d_kvn")[1]
    tok = tok + emit("w_ukv", _mm(dkvt, kvn, out_dtype=BF16, name="mm_dw_ukv"))
    dproj_mla, red_m = _mla_prep_bwd(proj_mla, dqn, dkvn, jnp.transpose(dkrt), rot_c, rot_s, ng + tok)
    dx_rnn, dwa_bd, dwx_bd, red_r = _rnn_bwd(proj_rnn, xc, ra, ix, hs, dy_rnn, keep, rp + tok, wa_bd, wx_bd)
    dw_in_t = jnp.concatenate([
        _mm(dx_rnn, h, ta=True, out_dtype=BF16, name="mm_dw_in_rnn"),
        _mm(dproj_mla, h, ta=True, out_dtype=BF16, name="mm_dw_in_mla")[:o_mla - o_rnn],
        _mm(dg, h, ta=True, out_dtype=BF16, name="mm_dw_in_g")], axis=0)
    tok = tok + emit("w_in", dw_in_t)
    dh_a = _mm(dx_rnn, w_in_rnn, tb=True, name="mm_d_h_rnn")
    dh_b = _mm(dproj_mla, w_in_mla, tb=True, name="mm_d_h_mla")
    dh_c = _mm(dg, w_in_g, tb=True, name="mm_d_h_g")
    grad_x, red_1 = _norm1_bwd(x, dh_a, dh_b, dh_c, dx1, gmod1 + tok)

    gs = {
        "norm1_g": red_1[0], "conv_w": red_r[0:4], "conv_b": red_r[4], "w_gate_a": _block_diag_t(dwa_bd),
        "b_gate_a": red_r[5], "w_gate_x": _block_diag_t(dwx_bd), "b_gate_x": red_r[6], "lru_param": red_r[7],
        "q_norm_g": red_m[0, :Q_LORA], "kv_norm_g": red_m[0, Q_LORA:Q_LORA + KV_LORA], "norm2_g": red_2[0],
        "ffn_conv_w": red_ffn[0:3], "ffn_conv_b": red_ffn[3], "final_g": red_f[0],
    }
    dmod = jnp.stack([red_1[2], red_1[1], red_2[3], red_2[2], red_2[1], red_f[1]], axis=0)
    return red_f[2, 0], grad_x, gs, dmod


MESH_IDS = pl.DeviceIdType.MESH
HBM_SPEC = pl.BlockSpec(memory_space=pltpu.HBM)


def _my_slot():
    return 4 * lax.axis_index("x") + 2 * lax.axis_index("y") + lax.axis_index("c")


def _all_gather(arrs, name):
    n = len(arrs)

    def body(*refs):
        ins, outs = refs[:n], refs[n:2 * n]
        send_sems, recv_sems, local_sems = refs[2 * n:]
        x, y, c = lax.axis_index("x"), lax.axis_index("y"), lax.axis_index("c")
        me, sibling = (x, y, c), (x, y, 1 - c)
        chips = [(1 - x, y), (x, 1 - y), (1 - x, 1 - y)]

        def slot(dev):
            return 4 * dev[0] + 2 * dev[1] + dev[2]

        def copy(a, k, block, to, src=None):
            dst = outs[a].at[slot(block)]
            return pltpu.make_async_remote_copy(
                src_ref=dst if src is None else src, dst_ref=dst, send_sem=send_sems.at[a, k], recv_sem=recv_sems.at[a, k],
                device_id=to, device_id_type=MESH_IDS)

        mine = [pltpu.make_async_copy(ins[a], outs[a].at[slot(me)], local_sems.at[a]) for a in range(n)]
        for cp in mine:
            cp.start()
        first = []
        for a in range(n):
            first.append(copy(a, 0, me, sibling, src=ins[a]))
            first += [copy(a, 1 + j, me, (*chip, c), src=ins[a]) for j, chip in enumerate(chips)]
        for cp in first:
            cp.start()
        passed = []
        for j, chip in enumerate(chips):
            for a in range(n):
                copy(a, 1 + j, (*chip, c), me).wait_recv()
                fwd = copy(a, 4 + j, (*chip, c), sibling)
                fwd.start()
                passed.append(fwd)
        for a in range(n):
            copy(a, 0, sibling, me).wait_recv()
            for j, chip in enumerate(chips):
                copy(a, 4 + j, (*chip, 1 - c), me).wait_recv()
        for cp in first + passed:
            cp.wait_send()
        for cp in mine:
            cp.wait()

    return pl.pallas_call(
        body, name=name,
        in_specs=[HBM_SPEC] * n, out_specs=[HBM_SPEC] * n,
        out_shape=[jax.ShapeDtypeStruct((N_DEV,) + a.shape, a.dtype) for a in arrs],
        scratch_shapes=[pltpu.SemaphoreType.DMA((n, 7)), pltpu.SemaphoreType.DMA((n, 7)), pltpu.SemaphoreType.DMA((n,))],
    )(*arrs)


def _assemble_columns(g, name):
    k, r, c = g.shape
    tr = _pick(r, (256, 128, 64, 32, 16))

    def body(g_ref, o_ref):
        for s in range(k):
            o_ref[:, s * c:(s + 1) * c] = g_ref[s]

    return pl.pallas_call(
        body, name=name, grid=(r // tr,),
        in_specs=[pl.BlockSpec((k, tr, c), lambda i: (0, i, 0))],
        out_specs=pl.BlockSpec((tr, k * c), lambda i: (i, 0)),
        out_shape=jax.ShapeDtypeStruct((r, k * c), g.dtype),
        compiler_params=_cparams(("arbitrary",)),
    )(g)


SEM_SPEC =pl.BlockSpec(memory_space=pltpu.SEMAPHORE)
DATAFLOW = pltpu.SideEffectType.DATAFLOW_SIDE_EFFECTING
FLIPS = [(dx, dy, dc) for dx in (0, 1) for dy in (0, 1) for dc in (0, 1)][1:]


def _peer(k):
    dx, dy, dc = FLIPS[k]
    peer = (lax.axis_index("x") ^ dx, lax.axis_index("y") ^ dy, lax.axis_index("c") ^ dc)
    return peer, 4 * peer[0] + 2 * peer[1] + peer[2]


def _gather_start(shards, after, name):
    n, nf = len(shards), len(FLIPS)

    def body(*refs):
        srcs, lands = refs[:n], refs[n:2 * n]
        send_sems, recv_sems = refs[2 * n + 1:3 * n + 1], refs[3 * n + 1:4 * n + 1]
        token = refs[-1]
        me = _my_slot()
        for a in range(n):
            for k in range(nf):
                peer, _ = _peer(k)
                pltpu.make_async_remote_copy(
                    src_ref=srcs[a], dst_ref=lands[a].at[me], send_sem=send_sems[a].at[k], recv_sem=recv_sems[a].at[k],
                    device_id=peer, device_id_type=MESH_IDS).start()
        token[...] = jnp.zeros(token.shape, F32)

    land_shapes = [(N_DEV,) + a.shape for a in shards]
    sems = [pltpu.SemaphoreType.DMA((nf,))] * n
    out = pl.pallas_call(
        body, name=name,
        out_shape=(*sems, *sems, *[pltpu.HBM(a.shape, a.dtype) for a in shards],
                   *[pltpu.HBM(shp, a.dtype) for shp, a in zip(land_shapes, shards)],
                   jax.ShapeDtypeStruct((SUBLANES, LANES), F32)),
        in_specs=[HBM_SPEC] * (2 * n) + [pl.BlockSpec(memory_space=pl.ANY)],
        out_specs=(*[SEM_SPEC] * (2 * n), *[HBM_SPEC] * (2 * n), pl.BlockSpec(memory_space=pltpu.VMEM)),
        input_output_aliases={i: 2 * n + i for i in range(2 * n)},
        compiler_params=pltpu.CompilerParams(has_side_effects=DATAFLOW),
    )(*[pltpu.with_memory_space_constraint(a, pltpu.HBM) for a in shards],
      *[pltpu.with_memory_space_constraint(lax.empty(shp, a.dtype), pltpu.HBM) for shp, a in zip(land_shapes, shards)],
      after)
    return [(out[a], out[n + a], out[2 * n + a], out[3 * n + a]) for a in range(n)], out[-1]


def _gather_wait(flights, after, name):
    n, nf = len(flights), len(FLIPS)

    def body(*refs):
        send_sems, recv_sems = refs[:n], refs[n:2 * n]
        srcs, lands = refs[2 * n:3 * n], refs[3 * n:4 * n]
        for a in range(n):
            for k in range(nf):
                peer, peer_slot = _peer(k)
                cp = pltpu.make_async_remote_copy(
                    src_ref=srcs[a], dst_ref=lands[a].at[peer_slot], send_sem=send_sems[a].at[k],
                    recv_sem=recv_sems[a].at[k], device_id=peer, device_id_type=MESH_IDS)
                cp.wait_send()
                cp.wait_recv()

    srcs, lands = [f[2] for f in flights], [f[3] for f in flights]
    out = pl.pallas_call(
        body, name=name,
        out_shape=(*[pltpu.HBM(a.shape, a.dtype) for a in srcs], *[pltpu.HBM(a.shape, a.dtype) for a in lands]),
        in_specs=[SEM_SPEC] * (2 * n) + [HBM_SPEC] * (2 * n) + [pl.BlockSpec(memory_space=pl.ANY)],
        out_specs=tuple([HBM_SPEC] * (2 * n)),
        input_output_aliases={2 * n + i: i for i in range(2 * n)},
        compiler_params=pltpu.CompilerParams(has_side_effects=DATAFLOW),
    )(*[f[0] for f in flights], *[f[1] for f in flights], *srcs, *lands, after)
    return list(out[n:])


ROW_ALIGN = 16


def _span_start(slot, rows):
    return (rows * slot) // ROW_ALIGN * ROW_ALIGN


def _chunk_of(src_ref, slot, rows, span):
    if rows is None:
        return src_ref.at[slot]
    return src_ref.at[pl.ds(pl.multiple_of(_span_start(slot, rows), ROW_ALIGN), span)]


def _scatter_start(src, name, rows=None, span=None):
    def body(src_ref, land_ref, send_sems, recv_sems, src_thru, land_thru, token):
        me = _my_slot()
        for k in range(len(FLIPS)):
            peer, peer_slot = _peer(k)
            pltpu.make_async_remote_copy(
                src_ref=_chunk_of(src_ref, peer_slot, rows, span), dst_ref=land_ref.at[me], send_sem=send_sems.at[k],
                recv_sem=recv_sems.at[k], device_id=peer, device_id_type=MESH_IDS).start()
        token[...] = jnp.zeros(token.shape, F32)

    n = len(FLIPS)
    land_shape = src.shape if rows is None else (N_DEV, span, src.shape[1])
    return pl.pallas_call(
        body, name=name,
        out_shape=(pltpu.SemaphoreType.DMA((n,)), pltpu.SemaphoreType.DMA((n,)), pltpu.HBM(src.shape, src.dtype),
                   pltpu.HBM(land_shape, src.dtype), jax.ShapeDtypeStruct((SUBLANES, LANES), F32)),
        in_specs=(HBM_SPEC, HBM_SPEC),
        out_specs=(SEM_SPEC, SEM_SPEC, HBM_SPEC, HBM_SPEC, pl.BlockSpec(memory_space=pltpu.VMEM)),
        input_output_aliases={0: 2, 1: 3},
        compiler_params=pltpu.CompilerParams(has_side_effects=DATAFLOW),
    )(pltpu.with_memory_space_constraint(src, pltpu.HBM),
      pltpu.with_memory_space_constraint(lax.empty(land_shape, src.dtype), pltpu.HBM))


def _scatter_wait(send_sems, recv_sems, src_thru, land_thru, after, name, rows=None, span=None):
    def body(src_ref, land_ref, send_sems, recv_sems, after_ref, src_dead, got_ref):
        for k in range(len(FLIPS)):
            peer, peer_slot = _peer(k)
            cp = pltpu.make_async_remote_copy(
                src_ref=_chunk_of(src_ref, peer_slot, rows, span), dst_ref=land_ref.at[peer_slot], send_sem=send_sems.at[k],
                recv_sem=recv_sems.at[k], device_id=peer, device_id_type=MESH_IDS)
            cp.wait_send()
            cp.wait_recv()

    return pl.pallas_call(
        body, name=name,
        out_shape=(pltpu.HBM(src_thru.shape, src_thru.dtype), pltpu.HBM(land_thru.shape, land_thru.dtype)),
        in_specs=(HBM_SPEC, HBM_SPEC, SEM_SPEC, SEM_SPEC, pl.BlockSpec(memory_space=pl.ANY)),
        out_specs=(HBM_SPEC, HBM_SPEC), input_output_aliases={0: 0, 1: 1},
        compiler_params=pltpu.CompilerParams(has_side_effects=DATAFLOW),
    )(src_thru, land_thru, send_sems, recv_sems, after)


def _sum_sources(parts, own, name):
    k, r, c = parts.shape
    tr = r if k * r * c <= 2 * 1024 * 1024 else _pick(r, (512, 256, 128, 64, 32, 16, 8))

    def body(p_ref, own_ref, o_ref):
        me = _my_slot()
        acc = jnp.where(me == 0, own_ref[...], p_ref[0]).astype(F32)
        for s in range(1, k):
            acc = acc + jnp.where(me == s, own_ref[...], p_ref[s]).astype(F32)
        o_ref[...] = acc

    blk = pl.BlockSpec((tr, c), lambda i: (i, 0))
    return pl.pallas_call(
        body, name=name, grid=(r // tr,),
        in_specs=[pl.BlockSpec((k, tr, c), lambda i: (0, i, 0)), blk],
        out_specs=blk,
        out_shape=jax.ShapeDtypeStruct((r, c), F32),
        compiler_params=_cparams(("arbitrary",)),
    )(parts, own)


def _adamw_math(g, w, m, v):
    m_new = ADAM_B1 * m + (1.0 - ADAM_B1) * g
    v_new = ADAM_B2 * v + (1.0 - ADAM_B2) * jnp.square(g)
    m_hat = m_new / (1.0 - ADAM_B1 ** ADAM_STEP)
    v_hat = v_new / (1.0 - ADAM_B2 ** ADAM_STEP)
    return -ADAM_LR * (m_hat / (jnp.sqrt(v_hat) + ADAM_EPS) + ADAM_WD * w), m_new, v_new


def _adamw_many(gs, ws, ms, vs, name):
    n = len(gs)

    def body(*refs):
        ins, outs = refs[:4 * n], refs[4 * n:]
        for i in range(n):
            g = ins[i][...]
            outs[4 * i][...] = g
            outs[4 * i + 1][...], outs[4 * i + 2][...], outs[4 * i + 3][...] = _adamw_math(
                g, ins[n + i][...], ins[2 * n + i][...], ins[3 * n + i][...])

    return pl.pallas_call(
        body, name=name,
        out_shape=[jax.ShapeDtypeStruct(a.shape, F32) for a in ws for _ in range(4)],
        compiler_params=_cparams(),
    )(*gs, *ws, *ms, *vs)


def _adamw(parts, w, m, v, name, own=None):
    k, r, c = parts.shape
    tr = r if r * c <= 256 * 1024 else _pick(r, (256, 128, 64, 32, 16, 8))

    def body(*refs):
        p_ref, w_ref, m_ref, v_ref = refs[:4]
        g_ref, d_ref, nm_ref, nv_ref = refs[-4:]

        def part(s):
            if own is None:
                return p_ref[s].astype(F32)
            return jnp.where(_my_slot() == s, refs[4][...], p_ref[s]).astype(F32)

        g = part(0)
        for s in range(1, k):
            g = g + part(s)
        g_ref[...] = g
        d_ref[...], nm_ref[...], nv_ref[...] = _adamw_math(g, w_ref[...], m_ref[...], v_ref[...])

    blk = pl.BlockSpec((tr, c), lambda i: (i, 0))
    return pl.pallas_call(
        body, name=name, grid=(r // tr,),
        in_specs=[pl.BlockSpec((k, tr, c), lambda i: (0, i, 0)), blk, blk, blk] + ([] if own is None else [blk]),
        out_specs=[blk] * 4,
        out_shape=[jax.ShapeDtypeStruct((r, c), F32)] * 4,
        compiler_params=_cparams(("arbitrary",)),
    )(parts, w, m, v, *([] if own is None else [own]))


def _silu(v):
    return v * _sigmoid(v)


def _ada_fwd(c_all, w, b):
    def body(c_ref, w_ref, b_ref, o_ref):
        ca = _silu(c_ref[...]).astype(BF16)
        o_ref[...] = jnp.dot(ca, w_ref[...].astype(BF16), preferred_element_type=F32) + b_ref[...]

    return pl.pallas_call(
        body, name="ada_fwd", out_shape=jax.ShapeDtypeStruct((c_all.shape[0], w.shape[1]), F32),
        compiler_params=_cparams(),
    )(c_all, w, b)


def _ada_bwd(c_all, dmod):
    def body(c_ref, d_ref, o_ref):
        ca = _silu(c_ref[...]).astype(BF16).astype(F32)
        dm = d_ref[...].astype(BF16).astype(F32)
        acc = jnp.zeros(o_ref.shape, F32)
        for bi in range(c_all.shape[0]):
            acc = acc + jnp.transpose(ca[bi:bi + 1, :]) * dm[bi:bi + 1, :]
        o_ref[...] = acc

    return pl.pallas_call(
        body, name="ada_bwd", out_shape=jax.ShapeDtypeStruct((c_all.shape[1], dmod.shape[1]), F32),
        compiler_params=_cparams(),
    )(c_all, dmod)


COL_SHARDED = ("w_in", "w_uq", "w_ukv", "w_up")
ROW_SHARDED = ("w_proj_rnn", "w_proj_mla", "w_out", "w_down")
REPLICATED = ("b_ada", "norm1_g", "conv_b", "w_gate_a", "b_gate_a", "w_gate_x", "b_gate_x", "lru_param", "q_norm_g",
              "kv_norm_g", "norm2_g", "ffn_conv_b", "final_g")
WEIGHTS = ("w_ada", "b_ada", "norm1_g", "w_in", "conv_w", "conv_b", "w_gate_a", "b_gate_a", "w_gate_x", "b_gate_x",
           "lru_param", "q_norm_g", "w_uq", "kv_norm_g", "w_ukv", "w_proj_rnn", "w_proj_mla", "w_out", "norm2_g", "w_up",
           "ffn_conv_w", "ffn_conv_b", "w_down", "final_g")
TRANSPOSED_GRADS = COL_SHARDED
PACK_LANES = 128


def _pack(vecs, row_multiple=SUBLANES):
    flat = jnp.concatenate([v.reshape(-1).astype(F32) for v in vecs])
    pad = (-flat.shape[0]) % (PACK_LANES * row_multiple)
    return jnp.concatenate([flat, jnp.zeros((pad,), F32)]).reshape(-1, PACK_LANES)


def _unpack(packed, shapes):
    flat = packed.reshape(-1)
    out, off = [], 0
    for shp in shapes:
        size = math.prod(shp)
        out.append(flat[off:off + size].reshape(shp))
        off += size
    return out


def kernel(x, c, positions, w_ada, b_ada, norm1_g, w_in, conv_w, conv_b, w_gate_a, b_gate_a, w_gate_x, b_gate_x, lru_param, q_norm_g, w_uq, kv_norm_g, w_ukv, w_proj_rnn, w_proj_mla, w_out, norm2_g, w_up, ffn_conv_w, ffn_conv_b, w_down, final_g, loss_target, m_w_ada, m_b_ada, m_norm1_g, m_w_in, m_conv_w, m_conv_b, m_w_gate_a, m_b_gate_a, m_w_gate_x, m_b_gate_x, m_lru_param, m_q_norm_g, m_w_uq, m_kv_norm_g, m_w_ukv, m_w_proj_rnn, m_w_proj_mla, m_w_out, m_norm2_g, m_w_up, m_ffn_conv_w, m_ffn_conv_b, m_w_down, m_final_g, v_w_ada, v_b_ada, v_norm1_g, v_w_in, v_conv_w, v_conv_b, v_w_gate_a, v_b_gate_a, v_w_gate_x, v_b_gate_x, v_lru_param, v_q_norm_g, v_w_uq, v_kv_norm_g, v_w_ukv, v_w_proj_rnn, v_w_proj_mla, v_w_out, v_norm2_g, v_w_up, v_ffn_conv_w, v_ffn_conv_b, v_w_down, v_final_g):
    args = dict(locals())
    w = {n: args[n] for n in WEIGHTS}
    m = {n: args["m_" + n] for n in WEIGHTS}
    v = {n: args["v_" + n] for n in WEIGHTS}
    s, d = x.shape[1], x.shape[2]
    me = _my_slot()
    def two_d(a):
        assert a.ndim == 3 and a.shape[0] == 1, a.shape
        return a[0]

    big = COL_SHARDED + ROW_SHARDED
    shard = {n: two_d(w[n]).astype(BF16) for n in big}

    def whole(n, g):
        k, r, cc = g.shape
        return _assemble_columns(g, "assemble_" + n) if n in COL_SHARDED else g.reshape(k * r, cc)

    first = _all_gather([shard["w_in"], c, two_d(conv_w), two_d(ffn_conv_w)], "gather_first")
    c_all = first[1].reshape(N_DEV, d)
    conv_w_all = jnp.transpose(first[2], (1, 0, 2)).reshape(conv_w.shape[1], -1)
    ffn_conv_w_all = jnp.transpose(first[3], (1, 0, 2)).reshape(ffn_conv_w.shape[1], -1)

    ada_cols = w_ada.shape[2]
    b_cols = lax.dynamic_slice(b_ada, (0, me * ada_cols), (1, ada_cols))
    mod_cols = _ada_fwd(c_all, w_ada[0], b_cols)
    mod_all, = _all_gather([mod_cols], "gather_mod")

    later = ("w_uq", "w_ukv", "w_proj_rnn", "w_proj_mla", "w_out", "w_up", "w_down")
    flights, started = _gather_start([shard[n] for n in later], mod_all, "gather_start")
    flight = dict(zip(later, flights))

    def fetch(names, after):
        lands = _gather_wait([flight[n] for n in names], after, "gather_wait_" + names[0])
        return {n: whole(n, lax.dynamic_update_index_in_dim(g, shard[n], me, 0)) for n, g in zip(names, lands)}

    mod = lax.dynamic_index_in_dim(mod_all, me, axis=1, keepdims=False).reshape(6, d) + started[0, 0]

    sm = {n: w[n][0] for n in REPLICATED if n not in ("b_ada", "final_g")}
    sm["final_g"] = final_g
    sm["conv_w"] = conv_w_all
    sm["ffn_conv_w"] = ffn_conv_w_all
    in_flight, windows = {}, {}

    def emit(n, g):
        rows = g.shape[0] // N_DEV
        if rows % ROW_ALIGN == 0:
            windows[n] = (None, None)
            g = g.reshape(N_DEV, rows, g.shape[1])
        else:
            span = max(rows * k - _span_start(k, rows) for k in range(N_DEV)) + rows
            windows[n] = (rows, -(-span // ROW_ALIGN) * ROW_ALIGN)
            assert _span_start(N_DEV - 1, rows) + windows[n][1] <= g.shape[0], (n, g.shape)
        *in_flight[n], token = _scatter_start(g, "scatter_start_" + n, *windows[n])
        return token[0, 0]

    sq, grad_x, gs, dmod = _local_step(x[0], mod, positions[0], loss_target[0], whole("w_in", first[0]), fetch, sm, emit)

    small_names = [n for n in REPLICATED if n != "b_ada"] + ["conv_w", "ffn_conv_w"]
    small_shapes = [gs[n].shape for n in small_names] + [(6 * d,), (1,)]
    partial = _pack([gs[n] for n in small_names] + [dmod, sq.reshape(1)], N_DEV * SUBLANES)
    *small_flight, small_started = _scatter_start(partial.reshape(N_DEV, -1, PACK_LANES), "scatter_small_start")

    grads, deltas, new_m, new_v = {}, {}, {}, {}

    def update(n, parts, own=None):
        shp = w[n].shape
        lay = jnp.transpose if n in TRANSPOSED_GRADS else (lambda a: a)
        res = _adamw(parts, lay(two_d(w[n])), lay(two_d(m[n])), lay(two_d(v[n])), "adamw_" + n, own)
        grads[n], deltas[n], new_m[n], new_v[n] = [lay(a).reshape(shp) for a in res]

    for n in big:
        rows, span = windows[n]
        src, landed = _scatter_wait(*in_flight[n], small_started, "scatter_wait_" + n, rows, span)
        if rows is None:
            update(n, landed, lax.dynamic_index_in_dim(src, me, axis=0, keepdims=False))
        else:
            start = _span_start(me, rows)
            own = lax.dynamic_slice(src, (start, 0), (span, src.shape[1]))
            total = _sum_sources(landed, own, "sum_" + n)
            update(n, lax.dynamic_slice(total, (rows * me - start, 0), (rows, src.shape[1]))[None])

    chunks, landed = _scatter_wait(*small_flight, new_v[big[-1]], "scatter_small_wait")
    mine = _sum_sources(landed, lax.dynamic_index_in_dim(chunks, me, axis=0, keepdims=False), "sum_small")
    summed_all, dmod_all = _all_gather([mine, dmod.reshape(1, 6 * d)], "gather_small")
    summed = _unpack(summed_all, small_shapes)
    g_small = dict(zip(small_names, summed[:len(small_names)]))
    g_small["b_ada"] = summed[len(small_names)]
    loss = 0.5 * summed[-1][0] / d
    dmod_cols = lax.dynamic_slice(dmod_all.reshape(N_DEV, 6 * d), (0, me * ada_cols), (N_DEV, ada_cols))

    update("w_ada", _ada_bwd(c_all, dmod_cols)[None])

    for n in ("conv_w", "ffn_conv_w"):
        cols = w[n].shape[2]
        g_small[n] = lax.dynamic_slice(g_small[n], (0, me * cols), (g_small[n].shape[0], cols))
    small = REPLICATED + ("conv_w", "ffn_conv_w")
    as_rows = lambda a: a.reshape(1, -1) if a.ndim == 1 else a
    res = _adamw_many([as_rows(g_small[n].reshape(w[n].shape)) for n in small], [as_rows(w[n]) for n in small],
                      [as_rows(m[n]) for n in small], [as_rows(v[n]) for n in small], "adamw_small")
    for i, n in enumerate(small):
        grads[n], deltas[n], new_m[n], new_v[n] = [a.reshape(w[n].shape) for a in res[4 * i:4 * i + 4]]

    return (loss, grad_x[None], *[grads[n] for n in WEIGHTS], *[deltas[n] for n in WEIGHTS],
            *[new_m[n] for n in WEIGHTS], *[new_v[n] for n in WEIGHTS])
```

```python
import functools
import math

import jax
import jax.numpy as jnp
from jax import lax
from jax.experimental import pallas as pl
from jax.experimental.pallas import tpu as pltpu

F32 = jnp.float32
BF16 = jnp.bfloat16

N_DEV = 8
LANES = 128
SUBLANES = 8
VMEM_LIMIT = 56 * 1024 * 1024

D_RNN = 1280
Q_LORA = 384
KV_LORA = 256
QK_NOPE = 64
QK_ROPE = 32
V_HEAD = 64
N_HEADS = 16
D_FF = 2816
ROPE_THETA = 10000.0
LRU_C = 8.0
EPS = 1e-6
MLA_W = 768
ATT_SCALE = 1.0 / math.sqrt(QK_NOPE + QK_ROPE)

ADAM_LR, ADAM_B1, ADAM_B2, ADAM_EPS, ADAM_WD, ADAM_STEP = 0.001, 0.9, 0.999, 1e-08, 0.01, 10


def _cparams(sem=None):
    return pltpu.CompilerParams(dimension_semantics=sem, vmem_limit_bytes=VMEM_LIMIT)


def _pick(n, prefs):
    for p in prefs:
        if n % p == 0:
            return p
    return n


def _sigmoid(v):
    return 0.5 * jnp.tanh(0.5 * v) + 0.5


def _lane(shape):
    return lax.broadcasted_iota(jnp.int32, shape, len(shape) - 1)


def _row(shape):
    return lax.broadcasted_iota(jnp.int32, shape, len(shape) - 2)


MM_BLOCK_BYTES = 36 * 1024 * 1024


def _divisors(n):
    return [t for t in range(n, 0, -LANES) if n % t == 0] if n % LANES == 0 else [n]


HBM_BYTES_PER_US = 3.0e6
MXU_FLOPS_PER_US = 8.0e8
GRID_STEP_US = 0.35


def _mm_tiles(m, n, k, a_bytes, b_bytes, o_bytes):
    best = None
    for tm in [t for t in _divisors(m) if t <= 1024]:
        for tn in [t for t in _divisors(n) if t <= 2048]:
            for tk in _divisors(k):
                nk = k // tk
                need = 2 * (tm * tk * a_bytes + tk * tn * b_bytes + tm * tn * o_bytes) + (tm * tn * 4 if nk > 1 else 0)
                if need > MM_BLOCK_BYTES:
                    continue
                gi, gj = m // tm, n // tn
                for rows_outer in (True, False):
                    if nk > 1:
                        a_reads, b_reads = gj, gi
                    elif rows_outer:
                        a_reads, b_reads = 1, (gi if gj > 1 else 1)
                    else:
                        a_reads, b_reads = (gj if gi > 1 else 1), 1
                    traffic = m * k * a_bytes * a_reads + k * n * b_bytes * b_reads + m * n * (o_bytes + (8 * nk if nk > 1 else 0))
                    cost = max(traffic / HBM_BYTES_PER_US, 2.0 * m * n * k / MXU_FLOPS_PER_US) + gi * gj * nk * GRID_STEP_US
                    if best is None or cost < best[0]:
                        best = (cost, tm, tn, tk, rows_outer)
                break
    if best is None:
        raise ValueError((m, n, k))
    return best[1:]


def _mm(a, b, *, ta=False, tb=False, out_dtype=F32, also_t=None, name):
    (k_a, m) = a.shape if ta else a.shape[::-1]
    (n, k_b) = b.shape if tb else b.shape[::-1]
    assert k_a == k_b, (a.shape, b.shape, ta, tb)
    k = k_a
    tm, tn, tk, rows_outer = _mm_tiles(m, n, k, a.dtype.itemsize, b.dtype.itemsize, jnp.dtype(out_dtype).itemsize)
    nk = k // tk
    dims = (((0 if ta else 1,), (1 if tb else 0,)), ((), ()))
    n_out = 1 if also_t is None else 2

    def body(a_ref, b_ref, *rest):
        outs, acc = rest[:n_out], rest[n_out:]
        part = lax.dot_general(a_ref[...].astype(BF16), b_ref[...].astype(BF16), dims, preferred_element_type=F32)

        def write(val):
            outs[0][...] = val.astype(out_dtype)
            if also_t is not None:
                outs[1][...] = jnp.transpose(val).astype(also_t)

        if nk == 1:
            write(part)
            return
        acc_ref, = acc
        kk = pl.program_id(2)

        @pl.when(kk == 0)
        def _():
            acc_ref[...] = part

        @pl.when(kk > 0)
        def _():
            acc_ref[...] += part

        @pl.when(kk == nk - 1)
        def _():
            write(acc_ref[...])

    def ij(f):
        return (lambda i, j, kk: f(i, j, kk)) if rows_outer else (lambda j, i, kk: f(i, j, kk))

    a_spec = pl.BlockSpec((tk, tm), ij(lambda i, j, kk: (kk, i))) if ta else pl.BlockSpec((tm, tk), ij(lambda i, j, kk: (i, kk)))
    b_spec = pl.BlockSpec((tn, tk), ij(lambda i, j, kk: (j, kk))) if tb else pl.BlockSpec((tk, tn), ij(lambda i, j, kk: (kk, j)))
    out_specs = [pl.BlockSpec((tm, tn), ij(lambda i, j, kk: (i, j)))]
    out_shape = [jax.ShapeDtypeStruct((m, n), out_dtype)]
    if also_t is not None:
        out_specs.append(pl.BlockSpec((tn, tm), ij(lambda i, j, kk: (j, i))))
        out_shape.append(jax.ShapeDtypeStruct((n, m), also_t))
    res = pl.pallas_call(
        body, name=name,
        grid=(m // tm, n // tn, nk) if rows_outer else (n // tn, m // tm, nk),
        in_specs=[a_spec, b_spec], out_specs=out_specs, out_shape=out_shape,
        scratch_shapes=[] if nk == 1 else [pltpu.VMEM((tm, tn), F32)],
        compiler_params=_cparams(("arbitrary", "arbitrary", "arbitrary")),
    )(a, b)
    return res[0] if also_t is None else res


ROW_BLOCK_BYTES = 28 * 1024 * 1024


def _rowwise(fn, row_ins, par_ins, out_defs, red_defs, *, name):
    s = row_ins[0].shape[0]
    row_bytes = sum(a.shape[1] * a.dtype.itemsize for a in row_ins) + sum(c * jnp.dtype(dt).itemsize for c, dt in out_defs)
    tr = next((t for t in (512, 256, 128) if s % t == 0 and 2 * t * row_bytes <= ROW_BLOCK_BYTES), min(s, 128))
    nr, npar, no = len(row_ins), len(par_ins), len(out_defs)

    def body(*refs):
        rin, pin = refs[:nr], refs[nr:nr + npar]
        outs, reds = refs[nr + npar:nr + npar + no], refs[nr + npar + no:]
        i = pl.program_id(0)

        @pl.when(i == 0)
        def _():
            for r in reds:
                r[...] = jnp.zeros_like(r)

        fn(i, rin, pin, outs, reds)

    in_specs = [pl.BlockSpec((tr, a.shape[1]), lambda i: (i, 0)) for a in row_ins]
    in_specs += [pl.BlockSpec(a.shape, lambda i, nd=a.ndim: (0,) * nd) for a in par_ins]
    out_specs = [pl.BlockSpec((tr, c), lambda i: (i, 0)) for c, _ in out_defs]
    out_specs += [pl.BlockSpec(shp, lambda i: (0, 0)) for shp in red_defs]
    out_shape = [jax.ShapeDtypeStruct((s, c), dt) for c, dt in out_defs]
    out_shape += [jax.ShapeDtypeStruct(shp, F32) for shp in red_defs]
    return pl.pallas_call(
        body, name=name, grid=(s // tr,), in_specs=in_specs, out_specs=out_specs, out_shape=out_shape,
        compiler_params=_cparams(("arbitrary",)),
    )(*row_ins, *par_ins)


def _rms(v):
    return lax.rsqrt(jnp.mean(v * v, axis=-1, keepdims=True) + EPS)


def _colsum(v):
    return jnp.sum(v, axis=0, keepdims=True)


def _rms_bwd(dn, n, rstd):
    return rstd * (dn - n * jnp.mean(dn * n, axis=-1, keepdims=True))


def _norm_mod_fwd(x, gmod, name):
    def fn(i, rin, pin, outs, reds):
        xv = rin[0][...]
        p = pin[0][...]
        n = xv * _rms(xv)
        outs[0][...] = ((n * p[0:1]) * (1.0 + p[1:2]) + p[2:3]).astype(BF16)

    return _rowwise(fn, [x], [gmod], [(x.shape[1], BF16)], [], name=name)[0]


def _rope(v, rot_c, rot_s):
    half = QK_ROPE // 2
    swapped = jnp.where(_lane(v.shape) < QK_NOPE + half, pltpu.roll(v, LANES - half, 1), pltpu.roll(v, half, 1))
    return v * rot_c + swapped * rot_s


def _rope_t(dv, rot_c, rot_s):
    half = QK_ROPE // 2
    ds = dv * rot_s
    lane = _lane(dv.shape)
    swapped = jnp.where(lane < QK_NOPE + half, pltpu.roll(ds, LANES - half, 1), pltpu.roll(ds, half, 1))
    in_rope = (lane >= QK_NOPE) & (lane < QK_NOPE + QK_ROPE)
    return dv * rot_c + jnp.where(in_rope, swapped, 0.0)


def _mla_prep_fwd(proj_mla, rot_c, rot_s, ng):
    o1, o2 = Q_LORA, Q_LORA + KV_LORA

    def fn(i, rin, pin, outs, reds):
        g = pin[0][...]
        ql = rin[0][:, 0:o1]
        kl = rin[0][:, o1:o2]
        outs[0][...] = (ql * _rms(ql) * g[0:1, 0:o1]).astype(BF16)
        outs[1][...] = (kl * _rms(kl) * g[0:1, o1:o2]).astype(BF16)
        kr = pltpu.roll(rin[0][:, o2:o2 + LANES], QK_NOPE, 1)
        outs[2][...] = _rope(kr, rin[1][...], rin[2][...]).astype(BF16)

    return _rowwise(fn, [proj_mla, rot_c, rot_s], [ng], [(Q_LORA, BF16), (KV_LORA, BF16), (LANES, BF16)], [],
                    name="mla_prep_fwd")


def _mla_prep_bwd(proj_mla, dqn, dkvn, dkr, rot_c, rot_s, ng):
    o1, o2 = Q_LORA, Q_LORA + KV_LORA

    def fn(i, rin, pin, outs, reds):
        g = pin[0][...]
        ql = rin[0][:, 0:o1]
        kl = rin[0][:, o1:o2]
        rq, rk = _rms(ql), _rms(kl)
        nq, nk = ql * rq, kl * rk
        dq, dk = rin[1][...], rin[2][...]
        outs[0][:, 0:o1] = _rms_bwd(dq * g[0:1, 0:o1], nq, rq).astype(BF16)
        outs[0][:, o1:o2] = _rms_bwd(dk * g[0:1, o1:o2], nk, rk).astype(BF16)
        dkr_pre = pltpu.roll(_rope_t(rin[3][...], rin[4][...], rin[5][...]), LANES - QK_NOPE, 1)
        outs[0][:, o2:] = jnp.where(_lane(dkr_pre.shape) < QK_ROPE, dkr_pre, 0.0).astype(BF16)
        reds[0][0:1, 0:o1] += _colsum(dq * nq)
        reds[0][0:1, o1:o2] += _colsum(dk * nk)

    return _rowwise(fn, [proj_mla, dqn, dkvn, dkr, rot_c, rot_s], [ng], [(MLA_W, BF16)], [(SUBLANES, MLA_W)],
                    name="mla_prep_bwd")


def _rope_bwd(dq, rot_c, rot_s):
    def fn(i, rin, pin, outs, reds):
        c, sn = rin[1][...] * Q_PRESCALE, rin[2][...] * Q_PRESCALE
        for h in range(N_HEADS):
            sl = slice(h * LANES, (h + 1) * LANES)
            outs[0][:, sl] = _rope_t(rin[0][:, sl], c, sn).astype(BF16)

    return _rowwise(fn, [dq, rot_c, rot_s], [], [(dq.shape[1], BF16)], [], name="rope_bwd")[0]


def _rope_fwd_t(q, rot_c, rot_s):
    s, c = q.shape
    tr = min(256, s)

    def body(q_ref, c_ref, s_ref, o_ref, ot_ref):
        cc, sn = c_ref[...] * Q_PRESCALE, s_ref[...] * Q_PRESCALE
        for h in range(N_HEADS):
            sl = slice(h * LANES, (h + 1) * LANES)
            rot = _rope(q_ref[:, sl].astype(F32), cc, sn)
            o_ref[:, sl] = rot.astype(BF16)
            ot_ref[sl, :] = jnp.transpose(rot).astype(BF16)

    return pl.pallas_call(
        body, name="rope_fwd", grid=(s // tr,),
        in_specs=[pl.BlockSpec((tr, c), lambda i: (i, 0)), pl.BlockSpec((tr, LANES), lambda i: (i, 0)),
                  pl.BlockSpec((tr, LANES), lambda i: (i, 0))],
        out_specs=[pl.BlockSpec((tr, c), lambda i: (i, 0)), pl.BlockSpec((c, tr), lambda i: (0, i))],
        out_shape=[jax.ShapeDtypeStruct((s, c), BF16), jax.ShapeDtypeStruct((c, s), BF16)],
        compiler_params=_cparams(("arbitrary",)),
    )(q, rot_c, rot_s)


def _merge_fwd(pr, pm, proj_g):
    d = pr.shape[1]

    def fn(i, rin, pin, outs, reds):
        g_rnn, g_mla = rin[2][:, 0:d].astype(F32), rin[2][:, d:].astype(F32)
        outs[0][...] = (_sigmoid(g_rnn) * rin[0][...].astype(F32) + _sigmoid(g_mla) * rin[1][...].astype(F32)).astype(BF16)

    return _rowwise(fn, [pr, pm, proj_g], [], [(d, BF16)], [], name="merge_fwd")[0]


def _merge_bwd(dmerged, pr, pm, proj_g):
    d = pr.shape[1]

    def fn(i, rin, pin, outs, reds):
        dm = rin[0][...]
        sr, sm = _sigmoid(rin[3][:, 0:d].astype(F32)), _sigmoid(rin[3][:, d:].astype(F32))
        outs[0][...] = (dm * sr).astype(BF16)
        outs[1][...] = (dm * sm).astype(BF16)
        outs[2][:, 0:d] = (dm * rin[1][...].astype(F32) * sr * (1.0 - sr)).astype(BF16)
        outs[2][:, d:] = (dm * rin[2][...].astype(F32) * sm * (1.0 - sm)).astype(BF16)

    return _rowwise(fn, [dmerged, pr, pm, proj_g], [], [(d, BF16), (d, BF16), (2 * d, BF16)], [], name="merge_bwd")


def _resid_norm_fwd(x, o, gmod):
    d = x.shape[1]

    def fn(i, rin, pin, outs, reds):
        p = pin[0][...]
        x1 = rin[0][...] + p[3:4] * rin[1][...]
        outs[0][...] = x1
        outs[1][...] = ((x1 * _rms(x1) * p[0:1]) * (1.0 + p[1:2]) + p[2:3]).astype(BF16)

    return _rowwise(fn, [x, o], [gmod], [(d, F32), (d, BF16)], [], name="resid_norm_fwd")


def _final_fwd_bwd(x1, dn, target, par):
    d = x1.shape[1]

    def fn(i, rin, pin, outs, reds):
        p = pin[0][...]
        dnv = rin[1][...]
        x2 = rin[0][...] + p[0:1] * dnv
        rstd = _rms(x2)
        n3 = x2 * rstd
        err = n3 * p[1:2] - rin[2][...]
        dy = err * (1.0 / d)
        dx2 = _rms_bwd(dy * p[1:2], n3, rstd)
        outs[0][...] = dx2
        outs[1][...] = (dx2 * p[0:1]).astype(BF16)
        reds[0][0:1, :] += _colsum(dy * n3)
        reds[0][1:2, :] += _colsum(dx2 * dnv)
        reds[0][2:3, :] += jnp.zeros((1, d), F32) + jnp.sum(err * err)

    return _rowwise(fn, [x1, dn, target], [par], [(d, F32), (d, BF16)], [(SUBLANES, d)], name="final_fwd_bwd")


def _norm2_bwd(x1, dh2, dx2, o, gmod):
    d = x1.shape[1]

    def fn(i, rin, pin, outs, reds):
        p = pin[0][...]
        x1v, dh = rin[0][...], rin[1][...]
        rstd = _rms(x1v)
        n2 = x1v * rstd
        dx1 = rin[2][...] + _rms_bwd(dh * (p[0:1] * (1.0 + p[1:2])), n2, rstd)
        outs[0][...] = dx1
        outs[1][...] = (dx1 * p[3:4]).astype(BF16)
        reds[0][0:1, :] += _colsum(dh * n2 * (1.0 + p[1:2]))
        reds[0][1:2, :] += _colsum(dh * n2 * p[0:1])
        reds[0][2:3, :] += _colsum(dh)
        reds[0][3:4, :] += _colsum(dx1 * rin[3][...])

    return _rowwise(fn, [x1, dh2, dx2, o], [gmod], [(d, F32), (d, BF16)], [(SUBLANES, d)], name="norm2_bwd")


def _norm1_bwd(x, dh_a, dh_b, dh_c, dx1, gmod):
    d = x.shape[1]

    def fn(i, rin, pin, outs, reds):
        p = pin[0][...]
        xv = rin[0][...]
        dh = rin[1][...] + rin[2][...] + rin[3][...]
        rstd = _rms(xv)
        n1 = xv * rstd
        outs[0][...] = rin[4][...] + _rms_bwd(dh * (p[0:1] * (1.0 + p[1:2])), n1, rstd)
        reds[0][0:1, :] += _colsum(dh * n1 * (1.0 + p[1:2]))
        reds[0][1:2, :] += _colsum(dh * n1 * p[0:1])
        reds[0][2:3, :] += _colsum(dh)

    return _rowwise(fn, [x, dh_a, dh_b, dh_c, dx1], [gmod], [(d, F32)], [(SUBLANES, d)], name="norm1_bwd")


RNN_CHUNK = 512


def _shift_down(ref, base, n, j):
    v = ref[pl.ds(base, n + SUBLANES), :]
    return v[SUBLANES:] if j == 0 else pltpu.roll(v, j, 0)[SUBLANES:]


def _shift_up(ref, base, n, j, top_pad):
    v = ref[pl.ds(base + top_pad, n + SUBLANES), :]
    return v[:n] if j == 0 else pltpu.roll(v, n + SUBLANES - j, 0)[:n]


SCAN_GROUP = 128


def _scan_sizes(s):
    sizes = [s]
    while sizes[-1] > SUBLANES:
        assert sizes[-1] % SUBLANES == 0, s
        sizes.append(sizes[-1] // SUBLANES)
    return sizes


def _scan_scratch(s):
    return [pltpu.VMEM((n + 2 * SUBLANES, LANES), F32) for n in _scan_sizes(s)[1:] for _ in range(2)]


def _linear_scan(a_ref, b_ref, out_ref, a_off, s, reverse, levels):
    sizes = _scan_sizes(s)
    lv = [(a_ref, b_ref, a_off, 0)] + [(levels[2 * i], levels[2 * i + 1], 0, SUBLANES) for i in range(len(sizes) - 1)]
    zero8 = jnp.zeros((SUBLANES, LANES), F32)
    for (ar, br, _, _), n in zip(lv[1:], sizes[1:]):
        br[0:SUBLANES, :] = zero8
        br[pl.ds(n + SUBLANES, SUBLANES), :] = zero8
    order = list(range(SUBLANES - 1, -1, -1)) if reverse else list(range(SUBLANES))

    for lvl in range(len(sizes) - 1):
        ar, br, aoff, off = lv[lvl]
        m = sizes[lvl + 1]
        g = min(m, SCAN_GROUP)
        for t0 in range(0, m, g):
            acc_a = acc_b = None
            for r in order:
                sa = pl.ds(off + SUBLANES * t0 + r + aoff, g, stride=SUBLANES)
                sb = pl.ds(off + SUBLANES * t0 + r, g, stride=SUBLANES)
                a, b = ar[sa, :], br[sb, :]
                if acc_a is None:
                    acc_a, acc_b = a, b
                else:
                    acc_b = a * acc_b + b
                    acc_a = a * acc_a
            lv[lvl + 1][0][pl.ds(SUBLANES + t0, g), :] = acc_a
            lv[lvl + 1][1][pl.ds(SUBLANES + t0, g), :] = acc_b

    ar, br, _, off = lv[-1]
    n = sizes[-1]
    a, b = ar[pl.ds(off, n), :], br[pl.ds(off, n), :]
    h, rows = jnp.zeros((1, LANES), F32), [None] * n
    for j in (range(n - 1, -1, -1) if reverse else range(n)):
        h = a[j:j + 1, :] * h + b[j:j + 1, :]
        rows[j] = h
    br[pl.ds(off, n), :] = jnp.concatenate(rows, axis=0)

    for lvl in range(len(sizes) - 2, -1, -1):
        ar, br, aoff, off = lv[lvl]
        m = sizes[lvl + 1]
        up = lv[lvl + 1][1]
        dst = out_ref if lvl == 0 else br
        g = min(m, SCAN_GROUP)
        for t0 in range(0, m, g):
            h = _shift_up(up, t0, g, 1, SUBLANES) if reverse else _shift_down(up, t0, g, 1)
            for r in order:
                sa = pl.ds(off + SUBLANES * t0 + r + aoff, g, stride=SUBLANES)
                sb = pl.ds(off + SUBLANES * t0 + r, g, stride=SUBLANES)
                h = ar[sa, :] * h + br[sb, :]
                dst[sb, :] = h


def _one_minus_exp(z):
    series = -z * (1.0 + z * (0.5 + z * (1.0 / 6.0 + z * (1.0 / 24.0 + z * (1.0 / 120.0 + z * (1.0 / 720.0))))))
    return jnp.where(z > -0.1, series, 1.0 - jnp.exp(z))


def _softplus(v):
    return jnp.maximum(v, 0.0) + jnp.log(1.0 + jnp.exp(-jnp.abs(v)))


def _rnn_gates(xc, w, wa, wx, sp):
    xb = xc.astype(BF16)
    ra = _sigmoid(jnp.dot(xb, wa, preferred_element_type=F32) + w[5:6])
    ix = _sigmoid(jnp.dot(xb, wx, preferred_element_type=F32) + w[6:7])
    la = (-LRU_C) * ra * sp
    a = jnp.exp(la)
    mult = jnp.sqrt(_one_minus_exp(2.0 * la))
    return ra, ix, a, mult


def _rnn_fwd(x_rnn, keep, rp, wa_bd, wx_bd):
    s, r = x_rnn.shape
    ts = min(RNN_CHUNK, s)

    def body(x_ref, keep_ref, rp_ref, wa_ref, wx_ref, xc_ref, ra_ref, ix_ref, hs_ref, hsb_ref, xpad, a_s, b_s, *levels):
        xpad[0:SUBLANES, :] = jnp.zeros((SUBLANES, LANES), F32)
        xpad[SUBLANES:, :] = x_ref[...].astype(F32)
        w = rp_ref[...]
        sp = _softplus(-w[7:8])
        wa, wx = wa_ref[0], wx_ref[0]

        def chunk(c, carry):
            base = pl.multiple_of(c * ts, ts)
            xc = w[4:5] + w[3:4] * _shift_down(xpad, base, ts, 0)
            for j in range(1, 4):
                xc = xc + w[3 - j:4 - j] * _shift_down(xpad, base, ts, j)
            ra, ix, a, mult = _rnn_gates(xc, w, wa, wx, sp)
            kp = keep_ref[pl.ds(base, ts), :]
            xc_ref[pl.ds(base, ts), :] = xc.astype(BF16)
            ra_ref[pl.ds(base, ts), :] = ra.astype(BF16)
            ix_ref[pl.ds(base, ts), :] = ix.astype(BF16)
            a_s[pl.ds(base, ts), :] = a * kp
            b_s[pl.ds(base, ts), :] = jnp.where(kp > 0.0, mult, 1.0) * (ix * xc)
            return carry

        lax.fori_loop(0, s // ts, chunk, 0)

        _linear_scan(a_s, b_s, hs_ref, 0, s, False, levels)
        hsb_ref[...] = hs_ref[...].astype(BF16)

    col = pl.BlockSpec((s, LANES), lambda g: (0, g))
    return pl.pallas_call(
        body, name="rnn_fwd", grid=(r // LANES,),
        in_specs=[col, pl.BlockSpec((s, 1), lambda g: (0, 0)), pl.BlockSpec((SUBLANES, LANES), lambda g: (0, g)),
                  pl.BlockSpec((1, LANES, LANES), lambda g: (g, 0, 0)), pl.BlockSpec((1, LANES, LANES), lambda g: (g, 0, 0))],
        out_specs=[col] * 5,
        out_shape=[jax.ShapeDtypeStruct((s, r), dt) for dt in (BF16, BF16, BF16, F32, BF16)],
        scratch_shapes=[pltpu.VMEM((s + SUBLANES, LANES), F32), pltpu.VMEM((s, LANES), F32), pltpu.VMEM((s, LANES), F32),
                        *_scan_scratch(s)],
        compiler_params=_cparams(("arbitrary",)),
    )(x_rnn, keep, rp, wa_bd, wx_bd)


def _rnn_bwd(x_rnn, xc, ra, ix, hs, dy, keep, rp, wa_bd, wx_bd):
    s, r = x_rnn.shape
    ts = min(RNN_CHUNK, s)

    def body(x_ref, xc_ref, ra_ref, ix_ref, hs_ref, dy_ref, keep_ref, rp_ref, wa_ref, wx_ref,
             dx_ref, dwa_ref, dwx_ref, red_ref, xpad, hpad, a_s, dh_s, dxc_s, *levels):
        zero8 = jnp.zeros((SUBLANES, LANES), F32)
        xpad[0:SUBLANES, :] = zero8
        xpad[SUBLANES:, :] = x_ref[...].astype(F32)
        hpad[0:SUBLANES, :] = zero8
        hpad[SUBLANES:, :] = hs_ref[...]
        a_s[s:, :] = zero8
        dxc_s[s:, :] = zero8
        w = rp_ref[...]
        sp = _softplus(-w[7:8])
        wa, wx = wa_ref[0], wx_ref[0]

        def decay(c, carry):
            base = pl.multiple_of(c * ts, ts)
            a = jnp.exp((-LRU_C) * ra_ref[pl.ds(base, ts), :].astype(F32) * sp)
            a_s[pl.ds(base, ts), :] = a * keep_ref[pl.ds(base, ts), :]
            return carry

        lax.fori_loop(0, s // ts, decay, 0)

        _linear_scan(a_s, dy_ref, dh_s, 1, s, True, levels)

        def gates(c, carry):
            dwa, dwx, d_ba, d_bx, d_sp, d_cb = carry
            base = pl.multiple_of(c * ts, ts)
            xcv = xc_ref[pl.ds(base, ts), :].astype(F32)
            rav = ra_ref[pl.ds(base, ts), :].astype(F32)
            ixv = ix_ref[pl.ds(base, ts), :].astype(F32)
            kp = keep_ref[pl.ds(base, ts), :]
            dh = dh_s[pl.ds(base, ts), :]
            h_prev = _shift_down(hpad, base, ts, 1)
            la = (-LRU_C) * rav * sp
            a = jnp.exp(la)
            mult = jnp.sqrt(_one_minus_exp(2.0 * la))
            mult_eff = jnp.where(kp > 0.0, mult, 1.0)
            d_a = dh * h_prev * kp
            d_mult = dh * (ixv * xcv) * kp
            d_ix = dh * mult_eff * xcv
            d_xc = dh * mult_eff * ixv
            d_la = d_a * a - d_mult * (a * a) / mult
            d_pa = d_la * ((-LRU_C) * sp) * rav * (1.0 - rav)
            d_px = d_ix * ixv * (1.0 - ixv)
            xb = xcv.astype(BF16)
            pab, pxb = d_pa.astype(BF16), d_px.astype(BF16)
            tn = (((0,), (0,)), ((), ()))
            nt_ = (((1,), (1,)), ((), ()))
            dwa = dwa + lax.dot_general(xb, pab, tn, preferred_element_type=F32)
            dwx = dwx + lax.dot_general(xb, pxb, tn, preferred_element_type=F32)
            d_xc = d_xc + lax.dot_general(pab, wa, nt_, preferred_element_type=F32)
            d_xc = d_xc + lax.dot_general(pxb, wx, nt_, preferred_element_type=F32)
            dxc_s[pl.ds(base, ts), :] = d_xc
            return (dwa, dwx, d_ba + _colsum(d_pa), d_bx + _colsum(d_px),
                    d_sp + _colsum(d_la * ((-LRU_C) * rav)), d_cb + _colsum(d_xc))

        z1 = jnp.zeros((1, LANES), F32)
        zw = jnp.zeros((LANES, LANES), F32)
        dwa, dwx, d_ba, d_bx, d_sp, d_cb = lax.fori_loop(0, s // ts, gates, (zw, zw, z1, z1, z1, z1))
        dwa_ref[0] = dwa
        dwx_ref[0] = dwx

        def conv(c, carry):
            base = pl.multiple_of(c * ts, ts)
            d_here = dxc_s[pl.ds(base, ts), :]
            dx = w[3:4] * d_here
            for j in range(1, 4):
                dx = dx + w[3 - j:4 - j] * _shift_up(dxc_s, base, ts, j, 0)
            dx_ref[pl.ds(base, ts), :] = dx.astype(BF16)
            return tuple(carry[k] + _colsum(d_here * _shift_down(xpad, base, ts, 3 - k)) for k in range(4))

        d_w = lax.fori_loop(0, s // ts, conv, (z1, z1, z1, z1))
        d_lru = d_sp * (-_sigmoid(-w[7:8]))
        red_ref[...] = jnp.concatenate(list(d_w) + [d_cb, d_ba, d_bx, d_lru], axis=0)

    col = pl.BlockSpec((s, LANES), lambda g: (0, g))
    par = pl.BlockSpec((SUBLANES, LANES), lambda g: (0, g))
    wsp = pl.BlockSpec((1, LANES, LANES), lambda g: (g, 0, 0))
    return pl.pallas_call(
        body, name="rnn_bwd", grid=(r // LANES,),
        in_specs=[col] * 6 + [pl.BlockSpec((s, 1), lambda g: (0, 0)), par, wsp, wsp],
        out_specs=[col, wsp, wsp, par],
        out_shape=[jax.ShapeDtypeStruct((s, r), BF16), jax.ShapeDtypeStruct((r // LANES, LANES, LANES), F32),
                   jax.ShapeDtypeStruct((r // LANES, LANES, LANES), F32), jax.ShapeDtypeStruct((SUBLANES, r), F32)],
        scratch_shapes=[pltpu.VMEM((s + SUBLANES, LANES), F32), pltpu.VMEM((s + SUBLANES, LANES), F32),
                        pltpu.VMEM((s + SUBLANES, LANES), F32), pltpu.VMEM((s, LANES), F32),
                        pltpu.VMEM((s + SUBLANES, LANES), F32), *_scan_scratch(s)],
        compiler_params=_cparams(("arbitrary",)),
    )(x_rnn, xc, ra, ix, hs, dy, keep, rp, wa_bd, wx_bd)


ATT_BLOCK = 512
ATT_FWD_HEADS = 8
ATT_BWD_HEADS = 4


LOG2E = 1.4426950408889634
LN2 = 0.6931471805599453
Q_PRESCALE = ATT_SCALE * LOG2E


def _att_scores(q, kvt, krt, diagonal):
    kt_eff = jnp.where(_row(kvt.shape) < QK_NOPE, kvt, krt)
    sc = jnp.dot(q, kt_eff, preferred_element_type=F32)
    if diagonal:
        sc = jnp.where(lax.broadcasted_iota(jnp.int32, sc.shape, 1) <= lax.broadcasted_iota(jnp.int32, sc.shape, 0), sc, -jnp.inf)
    return sc


def _att_fwd(q, kv, kvt, krt):
    s = q.shape[0]
    t = min(ATT_BLOCK, s)
    nb = s // t
    hp = ATT_FWD_HEADS

    pairs = [(i, j) for i in range(nb) for j in range(i + 1)]
    i_tab = jnp.array([p[0] for p in pairs], jnp.int32)
    j_tab = jnp.array([p[1] for p in pairs], jnp.int32)

    def body(i_ref, j_ref, q_ref, kv_ref, kvt_ref, krt_ref, y_ref, lse_ref, m_s, acc_s):
        i, j = i_ref[pl.program_id(1)], j_ref[pl.program_id(1)]

        @pl.when(j == 0)
        def _():
            m_s[...] = jnp.full(m_s.shape, -jnp.inf, F32)
            acc_s[...] = jnp.zeros(acc_s.shape, F32)

        def step(diagonal):
            krt_b = krt_ref[...]
            lane = _lane((t, LANES))
            groups = [slice(c * LANES, (c + 1) * LANES) for c in range(t // LANES)]
            heads = [slice(hh * LANES, (hh + 1) * LANES) for hh in range(hp)]
            scs = [_att_scores(q_ref[:, sl], kvt_ref[sl, :], krt_b, diagonal) for sl in heads]
            stats = []
            for hh in range(hp):
                m_prev = m_s[hh]
                m_blk = scs[hh][:, groups[0]]
                for g in groups[1:]:
                    m_blk = jnp.maximum(m_blk, scs[hh][:, g])
                stats.append((m_prev, jnp.maximum(m_prev, jnp.max(m_blk, axis=-1, keepdims=True))))
            for hh in range(hp):
                m_prev, m_new = stats[hh]
                kvb = kv_ref[:, heads[hh]]
                ones_v = jnp.where(lane < QK_NOPE, jnp.ones_like(kvb), kvb)
                p = jnp.concatenate([jnp.exp2(scs[hh][:, g] - m_new).astype(BF16) for g in groups], axis=1)
                acc_s[hh] = jnp.exp2(m_prev - m_new) * acc_s[hh] + jnp.dot(p, ones_v, preferred_element_type=F32)
                m_s[hh] = m_new

        @pl.when(j < i)
        def _():
            step(False)

        @pl.when(j == i)
        def _():
            step(True)
            lane = _lane((t, LANES))
            for g in range(hp // 2):
                sl = slice(g * LANES, (g + 1) * LANES)
                a0, a1 = acc_s[2 * g], acc_s[2 * g + 1]
                l0, l1 = a0[:, 0:1], a1[:, 0:1]
                y_ref[:, sl] = jnp.where(lane < V_HEAD, pltpu.roll(a0 / l0, V_HEAD, 1), a1 / l1).astype(BF16)
                lse_ref[:, sl] = jnp.where(lane < V_HEAD, m_s[2 * g] + jnp.log(l0) * LOG2E, m_s[2 * g + 1] + jnp.log(l1) * LOG2E)

    return pl.pallas_call(
        body, name="att_fwd",
        grid_spec=pltpu.PrefetchScalarGridSpec(
            num_scalar_prefetch=2, grid=(N_HEADS // hp, len(pairs)),
            in_specs=[pl.BlockSpec((t, hp * LANES), lambda p, n, it, jt: (it[n], p)),
                      pl.BlockSpec((t, hp * LANES), lambda p, n, it, jt: (jt[n], p)),
                      pl.BlockSpec((hp * LANES, t), lambda p, n, it, jt: (p, jt[n])),
                      pl.BlockSpec((LANES, t), lambda p, n, it, jt: (0, jt[n]))],
            out_specs=[pl.BlockSpec((t, hp // 2 * LANES), lambda p, n, it, jt: (it[n], p))] * 2,
            scratch_shapes=[pltpu.VMEM((hp, t, LANES), F32)] * 2),
        out_shape=[jax.ShapeDtypeStruct((s, N_HEADS * V_HEAD), BF16), jax.ShapeDtypeStruct((s, N_HEADS * V_HEAD), F32)],
        compiler_params=_cparams(("arbitrary", "arbitrary")),
    )(i_tab, j_tab, q, kv, kvt, krt)


def _att_bwd(q, qt, kv, kvt, kr, krt, y, lse, dy, dyt):
    s = q.shape[0]
    t = min(ATT_BLOCK, s)
    nb = s // t

    hp = ATT_BWD_HEADS
    pairs = [(i, j) for j in range(nb) for i in range(j, nb)]
    i_tab = jnp.array([p[0] for p in pairs], jnp.int32)
    j_tab = jnp.array([p[1] for p in pairs], jnp.int32)

    def body(i_ref, j_ref, q_ref, qt_ref, kv_ref, kvt_ref, kr_ref, krt_ref, y_ref, lse_ref, dy_ref, dyt_ref,
             dq_ref, dkvt_ref, dkrt_ref, dkv_s):
        p_, n = pl.program_id(0), pl.program_id(1)
        i, j = i_ref[n], j_ref[n]

        @pl.when((p_ == 0) & (n == 0))
        def _():
            dkrt_ref[...] = jnp.zeros(dkrt_ref.shape, F32)

        @pl.when(n == 0)
        def _():
            dq_ref[...] = jnp.zeros(dq_ref.shape, F32)

        @pl.when(i == j)
        def _():
            dkv_s[...] = jnp.zeros(dkv_s.shape, F32)

        def step(diagonal):
            lane = _lane((t, LANES))
            row = _row((LANES, t))
            krb, krt_b = kr_ref[...], krt_ref[...]
            dyv = dy_ref[...]
            yv = y_ref[...].astype(F32)
            lsev = lse_ref[...]
            dyt_b = dyt_ref[...]
            rows = pl.ds(pl.multiple_of(i * t, t), t)
            cols = pl.ds(pl.multiple_of(j * t, t), t)
            zeros_t = jnp.zeros((V_HEAD, t), BF16)
            ones_w = jnp.ones((LANES, LANES), BF16)
            groups = [slice(c * LANES, (c + 1) * LANES) for c in range(t // LANES)]
            heads = [slice(hh * LANES, (hh + 1) * LANES) for hh in range(hp)]
            scs, dps, stats = [], [], []
            for hh, sl in enumerate(heads):
                kvt_b = kvt_ref[sl, :]
                scs.append(_att_scores(q_ref[:, sl], kvt_b, krt_b, diagonal))
                pair, first = heads[hh // 2], hh % 2 == 0
                lse_g, dy_g, y_g = lsev[:, pair], dyv[:, pair], yv[:, pair]
                mine = (lane < V_HEAD) if first else (lane >= V_HEAD)
                lse_rep = jnp.where(mine, lse_g, pltpu.roll(lse_g, V_HEAD, 1))
                do_pad = jnp.where(lane >= V_HEAD, pltpu.roll(dy_g, V_HEAD, 1) if first else dy_g, 0.0)
                o_pad = jnp.where(lane >= V_HEAD, pltpu.roll(y_g, V_HEAD, 1) if first else y_g, 0.0)
                do_ln2 = do_pad * LN2
                prod = do_ln2 * o_pad
                head_part = prod.astype(BF16)
                rest_part = (prod - head_part.astype(F32)).astype(BF16)
                delta_rep = (jnp.dot(head_part, ones_w, preferred_element_type=F32)
                             + jnp.dot(rest_part, ones_w, preferred_element_type=F32))
                dps.append(jnp.dot(do_ln2.astype(BF16), kvt_b, preferred_element_type=F32))
                stats.append((lse_rep, delta_rep))
            dkr_acc = jnp.zeros((LANES, t), F32)
            for hh, sl in enumerate(heads):
                lse_rep, delta_rep = stats[hh]
                probs, dss = [], []
                for g in groups:
                    pg = jnp.exp2(scs[hh][:, g] - lse_rep)
                    probs.append(pg.astype(BF16))
                    dss.append((pg * (dps[hh][:, g] - delta_rep)).astype(BF16))
                prob, ds = jnp.concatenate(probs, axis=1), jnp.concatenate(dss, axis=1)
                dot_pad = jnp.concatenate([zeros_t, dyt_b[hh * V_HEAD:(hh + 1) * V_HEAD, :]], axis=0)
                k_eff = jnp.where(lane < QK_NOPE, kv_ref[:, sl], krb)
                dvt = jnp.dot(dot_pad, prob, preferred_element_type=F32)
                dq_ref[rows, sl] += jnp.dot(ds, k_eff, preferred_element_type=F32)
                dkt = jnp.dot(qt_ref[sl, :], ds, preferred_element_type=F32)
                dkv_s[hh] += dvt + jnp.where(row < QK_NOPE, dkt, 0.0)
                dkr_acc = dkr_acc + jnp.where(row >= QK_NOPE, dkt, 0.0)
            dkrt_ref[:, cols] += dkr_acc

        @pl.when(i > j)
        def _():
            step(False)

        @pl.when(i == j)
        def _():
            step(True)

        @pl.when(i == nb - 1)
        def _():
            for hh, sl in enumerate([slice(hh * LANES, (hh + 1) * LANES) for hh in range(hp)]):
                dkvt_ref[sl, :] = dkv_s[hh].astype(BF16)

    qi = lambda p, n, it, jt: (it[n], p)
    qti = lambda p, n, it, jt: (p, it[n])
    wide, half = hp * LANES, hp // 2 * LANES
    return pl.pallas_call(
        body, name="att_bwd",
        grid_spec=pltpu.PrefetchScalarGridSpec(
            num_scalar_prefetch=2, grid=(N_HEADS // hp, len(pairs)),
            in_specs=[pl.BlockSpec((t, wide), qi), pl.BlockSpec((wide, t), qti),
                      pl.BlockSpec((t, wide), lambda p, n, it, jt: (jt[n], p)),
                      pl.BlockSpec((wide, t), lambda p, n, it, jt: (p, jt[n])),
                      pl.BlockSpec((t, LANES), lambda p, n, it, jt: (jt[n], 0)),
                      pl.BlockSpec((LANES, t), lambda p, n, it, jt: (0, jt[n])),
                      pl.BlockSpec((t, half), qi), pl.BlockSpec((t, half), qi), pl.BlockSpec((t, half), qi),
                      pl.BlockSpec((half, t), qti)],
            out_specs=[pl.BlockSpec((s, wide), lambda p, n, it, jt: (0, p)),
                       pl.BlockSpec((wide, t), lambda p, n, it, jt: (p, jt[n])),
                       pl.BlockSpec((LANES, s), lambda p, n, it, jt: (0, 0))],
            scratch_shapes=[pltpu.VMEM((hp, LANES, t), F32)]),
        out_shape=[jax.ShapeDtypeStruct((s, N_HEADS * LANES), F32), jax.ShapeDtypeStruct((N_HEADS * LANES, s), BF16),
                   jax.ShapeDtypeStruct((LANES, s), F32)],
        compiler_params=_cparams(("arbitrary", "arbitrary")),
    )(i_tab, j_tab, q, qt, kv, kvt, kr, krt, y, lse, dy, dyt)


FFN_COLS = 256


def _ffn_conv(pad_ref, w, base, n):
    u = w[3:4] + w[2:3] * _shift_down(pad_ref, base, n, 0)
    for j in range(1, 3):
        u = u + w[2 - j:3 - j] * _shift_down(pad_ref, base, n, j)
    return u


def _ffn_act_fwd(up, fp):
    s, f2 = up.shape
    f = f2 // 2
    tc = FFN_COLS
    ts = min(RNN_CHUNK, s)
    nfb = f // tc

    def body(ug_ref, uv_ref, wg_ref, wv_ref, act_ref, gpad, vpad):
        zero8 = jnp.zeros((SUBLANES, tc), F32)
        gpad[0:SUBLANES, :] = zero8
        vpad[0:SUBLANES, :] = zero8
        gpad[SUBLANES:, :] = ug_ref[...].astype(F32)
        vpad[SUBLANES:, :] = uv_ref[...].astype(F32)
        wg, wv = wg_ref[...], wv_ref[...]

        def chunk(c, carry):
            base = pl.multiple_of(c * ts, ts)
            g = _ffn_conv(gpad, wg, base, ts)
            v = _ffn_conv(vpad, wv, base, ts)
            act_ref[pl.ds(base, ts), :] = (g * _sigmoid(g) * v).astype(BF16)
            return carry

        lax.fori_loop(0, s // ts, chunk, 0)

    return pl.pallas_call(
        body, name="ffn_act_fwd", grid=(nfb,),
        in_specs=[pl.BlockSpec((s, tc), lambda b: (0, b)), pl.BlockSpec((s, tc), lambda b: (0, b + nfb)),
                  pl.BlockSpec((SUBLANES, tc), lambda b: (0, b)), pl.BlockSpec((SUBLANES, tc), lambda b: (0, b + nfb))],
        out_specs=pl.BlockSpec((s, tc), lambda b: (0, b)),
        out_shape=jax.ShapeDtypeStruct((s, f), BF16),
        scratch_shapes=[pltpu.VMEM((s + SUBLANES, tc), F32)] * 2,
        compiler_params=_cparams(("arbitrary",)),
    )(up, up, fp, fp)


def _ffn_act_bwd(up, dact, fp):
    s, f2 = up.shape
    f = f2 // 2
    tc = FFN_COLS
    ts = min(RNN_CHUNK, s)
    nfb = f // tc

    def body(ug_ref, uv_ref, da_ref, wg_ref, wv_ref, dup_ref, red_ref, gpad, vpad, dgs, dvs):
        half = pl.program_id(1)
        wg, wv = wg_ref[...], wv_ref[...]

        @pl.when(half == 0)
        def _():
            zero8 = jnp.zeros((SUBLANES, tc), F32)
            gpad[0:SUBLANES, :] = zero8
            vpad[0:SUBLANES, :] = zero8
            gpad[SUBLANES:, :] = ug_ref[...].astype(F32)
            vpad[SUBLANES:, :] = uv_ref[...].astype(F32)
            dgs[s:, :] = zero8
            dvs[s:, :] = zero8

            def act(c, carry):
                base = pl.multiple_of(c * ts, ts)
                g = _ffn_conv(gpad, wg, base, ts)
                v = _ffn_conv(vpad, wv, base, ts)
                da = da_ref[pl.ds(base, ts), :].astype(F32)
                sg = _sigmoid(g)
                dgs[pl.ds(base, ts), :] = da * v * (sg * (1.0 + g * (1.0 - sg)))
                dvs[pl.ds(base, ts), :] = da * (g * sg)
                return carry

            lax.fori_loop(0, s // ts, act, 0)

        def conv_t(src, pad, w, out_ref, red_ref):
            def chunk(c, carry):
                base = pl.multiple_of(c * ts, ts)
                d_here = src[pl.ds(base, ts), :]
                dx = w[2:3] * d_here
                for j in range(1, 3):
                    dx = dx + w[2 - j:3 - j] * _shift_up(src, base, ts, j, 0)
                out_ref[pl.ds(base, ts), :] = dx.astype(BF16)
                taps = tuple(carry[k] + _colsum(d_here * _shift_down(pad, base, ts, 2 - k)) for k in range(3))
                return taps + (carry[3] + _colsum(d_here),)

            z1 = jnp.zeros((1, tc), F32)
            red = lax.fori_loop(0, s // ts, chunk, (z1, z1, z1, z1))
            red_ref[...] = jnp.concatenate(list(red) + [jnp.zeros((4, tc), F32)], axis=0)

        @pl.when(half == 0)
        def _():
            conv_t(dgs, gpad, wg, dup_ref, red_ref)

        @pl.when(half == 1)
        def _():
            conv_t(dvs, vpad, wv, dup_ref, red_ref)

    gcol = pl.BlockSpec((s, tc), lambda b, h: (0, b))
    vcol = pl.BlockSpec((s, tc), lambda b, h: (0, b + nfb))
    gpar = pl.BlockSpec((SUBLANES, tc), lambda b, h: (0, b))
    vpar = pl.BlockSpec((SUBLANES, tc), lambda b, h: (0, b + nfb))
    return pl.pallas_call(
        body, name="ffn_act_bwd", grid=(nfb, 2),
        in_specs=[gcol, vcol, gcol, gpar, vpar],
        out_specs=[pl.BlockSpec((s, tc), lambda b, h: (0, b + h * nfb)),
                   pl.BlockSpec((SUBLANES, tc), lambda b, h: (0, b + h * nfb))],
        out_shape=[jax.ShapeDtypeStruct((s, f2), BF16), jax.ShapeDtypeStruct((SUBLANES, f2), F32)],
        scratch_shapes=[pltpu.VMEM((s + SUBLANES, tc), F32)] * 4,
        compiler_params=_cparams(("arbitrary", "arbitrary")),
    )(up, up, dact, fp, fp)


def _rows8(rows, width):
    rows = [r.reshape(1, width).astype(F32) for r in rows]
    return jnp.concatenate(rows + [jnp.zeros((SUBLANES - len(rows), width), F32)], axis=0)


def _block_diag(w):
    n, b, _ = w.shape
    w = w.reshape(n // 2, 2, b, b)
    z = jnp.zeros((n // 2, b, b), w.dtype)
    top = jnp.concatenate([w[:, 0], z], axis=2)
    bot = jnp.concatenate([z, w[:, 1]], axis=2)
    return jnp.concatenate([top, bot], axis=1)


def _block_diag_t(bd):
    n, b2, _ = bd.shape
    b = b2 // 2
    return jnp.stack([bd[:, :b, :b], bd[:, b:, b:]], axis=1).reshape(2 * n, b, b)


def _local_step(x, mod, positions, target, w_in, fetch, sm, emit):
    s, d = x.shape
    o_rnn, o_mla = D_RNN, D_RNN + Q_LORA + KV_LORA + QK_ROPE
    wts = {}
    w_in_rnn = w_in[:, :o_rnn]
    w_in_mla = jnp.concatenate([w_in[:, o_rnn:o_mla], jnp.zeros((d, MLA_W - (o_mla - o_rnn)), w_in.dtype)], axis=1)
    w_in_g = w_in[:, o_mla:]
    hd = QK_NOPE + QK_ROPE
    wa_bd = _block_diag(sm["w_gate_a"]).astype(BF16)
    wx_bd = _block_diag(sm["w_gate_x"]).astype(BF16)

    pos = positions.reshape(s)
    half = QK_ROPE // 2
    inv_freq = ROPE_THETA ** (-jnp.arange(half, dtype=F32) / half)
    ang = pos.astype(F32)[:, None] * inv_freq
    cos, sin = jnp.cos(ang), jnp.sin(ang)
    rot_c = jnp.concatenate([jnp.ones((s, QK_NOPE), F32), cos, cos, jnp.ones((s, LANES - hd), F32)], axis=1)
    rot_s = jnp.concatenate([jnp.zeros((s, QK_NOPE), F32), -sin, sin, jnp.zeros((s, LANES - hd), F32)], axis=1)
    keep = (pos != 0).astype(F32).reshape(s, 1)

    gmod1 = _rows8([sm["norm1_g"], mod[1], mod[0]], d)
    gmod2 = _rows8([sm["norm2_g"], mod[4], mod[3], mod[2]], d)
    rp = jnp.concatenate([sm["conv_w"].reshape(4, D_RNN), _rows8([sm["conv_b"], sm["b_gate_a"], sm["b_gate_x"], sm["lru_param"]], D_RNN)[:4]], axis=0)
    fp = _rows8([sm["ffn_conv_w"][0], sm["ffn_conv_w"][1], sm["ffn_conv_w"][2], sm["ffn_conv_b"]], 2 * D_FF)
    ng = _rows8([jnp.concatenate([sm["q_norm_g"].reshape(-1), sm["kv_norm_g"].reshape(-1), jnp.zeros((MLA_W - Q_LORA - KV_LORA,), F32)])], MLA_W)
    fpar = _rows8([mod[5], sm["final_g"]], d)

    h = _norm_mod_fwd(x, gmod1, "norm1_fwd")
    proj_rnn = _mm(h, w_in_rnn, out_dtype=BF16, name="mm_in_rnn")
    proj_mla = _mm(h, w_in_mla, name="mm_in_mla")
    proj_g = _mm(h, w_in_g, out_dtype=BF16, name="mm_in_g")
    xc, ra, ix, hs, hs_b = _rnn_fwd(proj_rnn, keep, rp, wa_bd, wx_bd)
    qn, kvn, kr = _mla_prep_fwd(proj_mla, rot_c, rot_s, ng)
    wts.update(fetch(("w_uq", "w_ukv"), kr))
    w_uq_p = jnp.pad(wts["w_uq"].reshape(Q_LORA, N_HEADS, hd), ((0, 0), (0, 0), (0, LANES - hd))).reshape(Q_LORA, N_HEADS * LANES)
    w_ukv = wts["w_ukv"]
    q_rot, q_rot_t = _rope_fwd_t(_mm(qn, w_uq_p, out_dtype=BF16, name="mm_uq"), rot_c, rot_s)
    kv, kvt = _mm(kvn, w_ukv, out_dtype=BF16, also_t=BF16, name="mm_ukv")
    krt = jnp.transpose(kr)
    y_mla, lse = _att_fwd(q_rot, kv, kvt, krt)
    wts.update(fetch(("w_proj_rnn", "w_proj_mla", "w_out", "w_up", "w_down"), lse))
    pr = _mm(hs_b, wts["w_proj_rnn"], out_dtype=BF16, name="mm_proj_rnn")
    pm = _mm(y_mla, wts["w_proj_mla"], out_dtype=BF16, name="mm_proj_mla")
    merged = _merge_fwd(pr, pm, proj_g)
    o = _mm(merged, wts["w_out"], name="mm_out")
    x1, h2 = _resid_norm_fwd(x, o, gmod2)
    up = _mm(h2, wts["w_up"], out_dtype=BF16, name="mm_up")
    act = _ffn_act_fwd(up, fp)
    dn = _mm(act, wts["w_down"], name="mm_down")

    dx2, ddn, red_f = _final_fwd_bwd(x1, dn, target, fpar)
    dact = _mm(ddn, wts["w_down"], tb=True, out_dtype=BF16, name="mm_d_act")
    tok = emit("w_down", _mm(act, ddn, ta=True, out_dtype=BF16, name="mm_dw_down"))
    dup, red_ffn = _ffn_act_bwd(up, dact, fp + tok)
    dh2 = _mm(dup, wts["w_up"], tb=True, name="mm_d_h2")
    tok = tok + emit("w_up", _mm(dup, h2, ta=True, out_dtype=BF16, name="mm_dw_up"))
    dx1, do, red_2 = _norm2_bwd(x1, dh2, dx2, o, gmod2 + tok)
    dmerged = _mm(do, wts["w_out"], tb=True, name="mm_d_merged")
    tok = tok + emit("w_out", _mm(merged, do, ta=True, out_dtype=BF16, name="mm_dw_out"))
    dpr, dpm, dg = _merge_bwd(dmerged, pr, pm, proj_g)
    dy_rnn = _mm(dpr, wts["w_proj_rnn"], tb=True, name="mm_d_yrnn")
    tok = tok + emit("w_proj_rnn", _mm(hs_b, dpr, ta=True, out_dtype=BF16, name="mm_dw_proj_rnn"))
    dy_mla, dy_mla_t = _mm(dpm, wts["w_proj_mla"], tb=True, also_t=BF16, name="mm_d_ymla")
    tok = tok + emit("w_proj_mla", _mm(y_mla, dpm, ta=True, out_dtype=BF16, name="mm_dw_proj_mla"))
    dq_rot, dkvt, dkrt = _att_bwd(q_rot, q_rot_t, kv, kvt, kr, krt, y_mla, lse, dy_mla, dy_mla_t)
    dq = _rope_bwd(dq_rot, rot_c, rot_s)
    dqn = _mm(dq, w_uq_p, tb=True, name="mm_d_qn")
    dw_uq_pt = _mm(dq, qn, ta=True, out_dtype=BF16, name="mm_dw_uq")
    tok = tok + emit("w_uq", dw_uq_pt.reshape(N_HEADS, LANES, Q_LORA)[:, :hd].reshape(N_HEADS * hd, Q_LORA))
    dkvn = _mm(w_ukv, dkvt, also_t=F32, name="mm_d_kvn")[1]
    tok = tok + emit("w_ukv", _mm(dkvt, kvn, out_dtype=BF16, name="mm_dw_ukv"))
    dproj_mla, red_m = _mla_prep_bwd(proj_mla, dqn, dkvn, jnp.transpose(dkrt), rot_c, rot_s, ng + tok)
    dx_rnn, dwa_bd, dwx_bd, red_r = _rnn_bwd(proj_rnn, xc, ra, ix, hs, dy_rnn, keep, rp + tok, wa_bd, wx_bd)
    dw_in_t = jnp.concatenate([
        _mm(dx_rnn, h, ta=True, out_dtype=BF16, name="mm_dw_in_rnn"),
        _mm(dproj_mla, h, ta=True, out_dtype=BF16, name="mm_dw_in_mla")[:o_mla - o_rnn],
        _mm(dg, h, ta=True, out_dtype=BF16, name="mm_dw_in_g")], axis=0)
    tok = tok + emit("w_in", dw_in_t)
    dh_a = _mm(dx_rnn, w_in_rnn, tb=True, name="mm_d_h_rnn")
    dh_b = _mm(dproj_mla, w_in_mla, tb=True, name="mm_d_h_mla")
    dh_c = _mm(dg, w_in_g, tb=True, name="mm_d_h_g")
    grad_x, red_1 = _norm1_bwd(x, dh_a, dh_b, dh_c, dx1, gmod1 + tok)

    gs = {
        "norm1_g": red_1[0], "conv_w": red_r[0:4], "conv_b": red_r[4], "w_gate_a": _block_diag_t(dwa_bd),
        "b_gate_a": red_r[5], "w_gate_x": _block_diag_t(dwx_bd), "b_gate_x": red_r[6], "lru_param": red_r[7],
        "q_norm_g": red_m[0, :Q_LORA], "kv_norm_g": red_m[0, Q_LORA:Q_LORA + KV_LORA], "norm2_g": red_2[0],
        "ffn_conv_w": red_ffn[0:3], "ffn_conv_b": red_ffn[3], "final_g": red_f[0],
    }
    dmod = jnp.stack([red_1[2], red_1[1], red_2[3], red_2[2], red_2[1], red_f[1]], axis=0)
    return red_f[2, 0], grad_x, gs, dmod


MESH_IDS = pl.DeviceIdType.MESH
HBM_SPEC = pl.BlockSpec(memory_space=pltpu.HBM)


def _my_slot():
    return 4 * lax.axis_index("x") + 2 * lax.axis_index("y") + lax.axis_index("c")


def _all_gather(arrs, name):
    n = len(arrs)

    def body(*refs):
        ins, outs = refs[:n], refs[n:2 * n]
        send_sems, recv_sems, local_sems = refs[2 * n:]
        x, y, c = lax.axis_index("x"), lax.axis_index("y"), lax.axis_index("c")
        me, sibling = (x, y, c), (x, y, 1 - c)
        chips = [(1 - x, y), (x, 1 - y), (1 - x, 1 - y)]

        def slot(dev):
            return 4 * dev[0] + 2 * dev[1] + dev[2]

        def copy(a, k, block, to, src=None):
            dst = outs[a].at[slot(block)]
            return pltpu.make_async_remote_copy(
                src_ref=dst if src is None else src, dst_ref=dst, send_sem=send_sems.at[a, k], recv_sem=recv_sems.at[a, k],
                device_id=to, device_id_type=MESH_IDS)

        mine = [pltpu.make_async_copy(ins[a], outs[a].at[slot(me)], local_sems.at[a]) for a in range(n)]
        for cp in mine:
            cp.start()
        first = []
        for a in range(n):
            first.append(copy(a, 0, me, sibling, src=ins[a]))
            first += [copy(a, 1 + j, me, (*chip, c), src=ins[a]) for j, chip in enumerate(chips)]
        for cp in first:
            cp.start()
        passed = []
        for j, chip in enumerate(chips):
            for a in range(n):
                copy(a, 1 + j, (*chip, c), me).wait_recv()
                fwd = copy(a, 4 + j, (*chip, c), sibling)
                fwd.start()
                passed.append(fwd)
        for a in range(n):
            copy(a, 0, sibling, me).wait_recv()
            for j, chip in enumerate(chips):
                copy(a, 4 + j, (*chip, 1 - c), me).wait_recv()
        for cp in first + passed:
            cp.wait_send()
        for cp in mine:
            cp.wait()

    return pl.pallas_call(
        body, name=name,
        in_specs=[HBM_SPEC] * n, out_specs=[HBM_SPEC] * n,
        out_shape=[jax.ShapeDtypeStruct((N_DEV,) + a.shape, a.dtype) for a in arrs],
        scratch_shapes=[pltpu.SemaphoreType.DMA((n, 7)), pltpu.SemaphoreType.DMA((n, 7)), pltpu.SemaphoreType.DMA((n,))],
    )(*arrs)


def _assemble_columns(g, name):
    k, r, c = g.shape
    tr = _pick(r, (256, 128, 64, 32, 16))

    def body(g_ref, o_ref):
        for s in range(k):
            o_ref[:, s * c:(s + 1) * c] = g_ref[s]

    return pl.pallas_call(
        body, name=name, grid=(r // tr,),
        in_specs=[pl.BlockSpec((k, tr, c), lambda i: (0, i, 0))],
        out_specs=pl.BlockSpec((tr, k * c), lambda i: (i, 0)),
        out_shape=jax.ShapeDtypeStruct((r, k * c), g.dtype),
        compiler_params=_cparams(("arbitrary",)),
    )(g)


SEM_SPEC =pl.BlockSpec(memory_space=pltpu.SEMAPHORE)
DATAFLOW = pltpu.SideEffectType.DATAFLOW_SIDE_EFFECTING
FLIPS = [(dx, dy, dc) for dx in (0, 1) for dy in (0, 1) for dc in (0, 1)][1:]


def _peer(k):
    dx, dy, dc = FLIPS[k]
    peer = (lax.axis_index("x") ^ dx, lax.axis_index("y") ^ dy, lax.axis_index("c") ^ dc)
    return peer, 4 * peer[0] + 2 * peer[1] + peer[2]


def _gather_start(shards, after, name):
    n, nf = len(shards), len(FLIPS)

    def body(*refs):
        srcs, lands = refs[:n], refs[n:2 * n]
        send_sems, recv_sems = refs[2 * n + 1:3 * n + 1], refs[3 * n + 1:4 * n + 1]
        token = refs[-1]
        me = _my_slot()
        for a in range(n):
            for k in range(nf):
                peer, _ = _peer(k)
                pltpu.make_async_remote_copy(
                    src_ref=srcs[a], dst_ref=lands[a].at[me], send_sem=send_sems[a].at[k], recv_sem=recv_sems[a].at[k],
                    device_id=peer, device_id_type=MESH_IDS).start()
        token[...] = jnp.zeros(token.shape, F32)

    land_shapes = [(N_DEV,) + a.shape for a in shards]
    sems = [pltpu.SemaphoreType.DMA((nf,))] * n
    out = pl.pallas_call(
        body, name=name,
        out_shape=(*sems, *sems, *[pltpu.HBM(a.shape, a.dtype) for a in shards],
                   *[pltpu.HBM(shp, a.dtype) for shp, a in zip(land_shapes, shards)],
                   jax.ShapeDtypeStruct((SUBLANES, LANES), F32)),
        in_specs=[HBM_SPEC] * (2 * n) + [pl.BlockSpec(memory_space=pl.ANY)],
        out_specs=(*[SEM_SPEC] * (2 * n), *[HBM_SPEC] * (2 * n), pl.BlockSpec(memory_space=pltpu.VMEM)),
        input_output_aliases={i: 2 * n + i for i in range(2 * n)},
        compiler_params=pltpu.CompilerParams(has_side_effects=DATAFLOW),
    )(*[pltpu.with_memory_space_constraint(a, pltpu.HBM) for a in shards],
      *[pltpu.with_memory_space_constraint(lax.empty(shp, a.dtype), pltpu.HBM) for shp, a in zip(land_shapes, shards)],
      after)
    return [(out[a], out[n + a], out[2 * n + a], out[3 * n + a]) for a in range(n)], out[-1]


def _gather_wait(flights, after, name):
    n, nf = len(flights), len(FLIPS)

    def body(*refs):
        send_sems, recv_sems = refs[:n], refs[n:2 * n]
        srcs, lands = refs[2 * n:3 * n], refs[3 * n:4 * n]
        for a in range(n):
            for k in range(nf):
                peer, peer_slot = _peer(k)
                cp = pltpu.make_async_remote_copy(
                    src_ref=srcs[a], dst_ref=lands[a].at[peer_slot], send_sem=send_sems[a].at[k],
                    recv_sem=recv_sems[a].at[k], device_id=peer, device_id_type=MESH_IDS)
                cp.wait_send()
                cp.wait_recv()

    srcs, lands = [f[2] for f in flights], [f[3] for f in flights]
    out = pl.pallas_call(
        body, name=name,
        out_shape=(*[pltpu.HBM(a.shape, a.dtype) for a in srcs], *[pltpu.HBM(a.shape, a.dtype) for a in lands]),
        in_specs=[SEM_SPEC] * (2 * n) + [HBM_SPEC] * (2 * n) + [pl.BlockSpec(memory_space=pl.ANY)],
        out_specs=tuple([HBM_SPEC] * (2 * n)),
        input_output_aliases={2 * n + i: i for i in range(2 * n)},
        compiler_params=pltpu.CompilerParams(has_side_effects=DATAFLOW),
    )(*[f[0] for f in flights], *[f[1] for f in flights], *srcs, *lands, after)
    return list(out[n:])


ROW_ALIGN = 16


def _span_start(slot, rows):
    return (rows * slot) // ROW_ALIGN * ROW_ALIGN


def _chunk_of(src_ref, slot, rows, span):
    if rows is None:
        return src_ref.at[slot]
    return src_ref.at[pl.ds(pl.multiple_of(_span_start(slot, rows), ROW_ALIGN), span)]


def _scatter_start(src, name, rows=None, span=None):
    def body(src_ref, land_ref, send_sems, recv_sems, src_thru, land_thru, token):
        me = _my_slot()
        for k in range(len(FLIPS)):
            peer, peer_slot = _peer(k)
            pltpu.make_async_remote_copy(
                src_ref=_chunk_of(src_ref, peer_slot, rows, span), dst_ref=land_ref.at[me], send_sem=send_sems.at[k],
                recv_sem=recv_sems.at[k], device_id=peer, device_id_type=MESH_IDS).start()
        token[...] = jnp.zeros(token.shape, F32)

    n = len(FLIPS)
    land_shape = src.shape if rows is None else (N_DEV, span, src.shape[1])
    return pl.pallas_call(
        body, name=name,
        out_shape=(pltpu.SemaphoreType.DMA((n,)), pltpu.SemaphoreType.DMA((n,)), pltpu.HBM(src.shape, src.dtype),
                   pltpu.HBM(land_shape, src.dtype), jax.ShapeDtypeStruct((SUBLANES, LANES), F32)),
        in_specs=(HBM_SPEC, HBM_SPEC),
        out_specs=(SEM_SPEC, SEM_SPEC, HBM_SPEC, HBM_SPEC, pl.BlockSpec(memory_space=pltpu.VMEM)),
        input_output_aliases={0: 2, 1: 3},
        compiler_params=pltpu.CompilerParams(has_side_effects=DATAFLOW),
    )(pltpu.with_memory_space_constraint(src, pltpu.HBM),
      pltpu.with_memory_space_constraint(lax.empty(land_shape, src.dtype), pltpu.HBM))


def _scatter_wait(send_sems, recv_sems, src_thru, land_thru, after, name, rows=None, span=None):
    def body(src_ref, land_ref, send_sems, recv_sems, after_ref, src_dead, got_ref):
        for k in range(len(FLIPS)):
            peer, peer_slot = _peer(k)
            cp = pltpu.make_async_remote_copy(
                src_ref=_chunk_of(src_ref, peer_slot, rows, span), dst_ref=land_ref.at[peer_slot], send_sem=send_sems.at[k],
                recv_sem=recv_sems.at[k], device_id=peer, device_id_type=MESH_IDS)
            cp.wait_send()
            cp.wait_recv()

    return pl.pallas_call(
        body, name=name,
        out_shape=(pltpu.HBM(src_thru.shape, src_thru.dtype), pltpu.HBM(land_thru.shape, land_thru.dtype)),
        in_specs=(HBM_SPEC, HBM_SPEC, SEM_SPEC, SEM_SPEC, pl.BlockSpec(memory_space=pl.ANY)),
        out_specs=(HBM_SPEC, HBM_SPEC), input_output_aliases={0: 0, 1: 1},
        compiler_params=pltpu.CompilerParams(has_side_effects=DATAFLOW),
    )(src_thru, land_thru, send_sems, recv_sems, after)


def _sum_sources(parts, own, name):
    k, r, c = parts.shape
    tr = r if k * r * c <= 2 * 1024 * 1024 else _pick(r, (512, 256, 128, 64, 32, 16, 8))

    def body(p_ref, own_ref, o_ref):
        me = _my_slot()
        acc = jnp.where(me == 0, own_ref[...], p_ref[0]).astype(F32)
        for s in range(1, k):
            acc = acc + jnp.where(me == s, own_ref[...], p_ref[s]).astype(F32)
        o_ref[...] = acc

    blk = pl.BlockSpec((tr, c), lambda i: (i, 0))
    return pl.pallas_call(
        body, name=name, grid=(r // tr,),
        in_specs=[pl.BlockSpec((k, tr, c), lambda i: (0, i, 0)), blk],
        out_specs=blk,
        out_shape=jax.ShapeDtypeStruct((r, c), F32),
        compiler_params=_cparams(("arbitrary",)),
    )(parts, own)


def _adamw_math(g, w, m, v):
    m_new = ADAM_B1 * m + (1.0 - ADAM_B1) * g
    v_new = ADAM_B2 * v + (1.0 - ADAM_B2) * jnp.square(g)
    m_hat = m_new / (1.0 - ADAM_B1 ** ADAM_STEP)
    v_hat = v_new / (1.0 - ADAM_B2 ** ADAM_STEP)
    return -ADAM_LR * (m_hat / (jnp.sqrt(v_hat) + ADAM_EPS) + ADAM_WD * w), m_new, v_new


def _adamw_many(gs, ws, ms, vs, name):
    n = len(gs)

    def body(*refs):
        ins, outs = refs[:4 * n], refs[4 * n:]
        for i in range(n):
            g = ins[i][...]
            outs[4 * i][...] = g
            outs[4 * i + 1][...], outs[4 * i + 2][...], outs[4 * i + 3][...] = _adamw_math(
                g, ins[n + i][...], ins[2 * n + i][...], ins[3 * n + i][...])

    return pl.pallas_call(
        body, name=name,
        out_shape=[jax.ShapeDtypeStruct(a.shape, F32) for a in ws for _ in range(4)],
        compiler_params=_cparams(),
    )(*gs, *ws, *ms, *vs)


def _adamw(parts, w, m, v, name, own=None):
    k, r, c = parts.shape
    tr = r if r * c <= 256 * 1024 else _pick(r, (256, 128, 64, 32, 16, 8))

    def body(*refs):
        p_ref, w_ref, m_ref, v_ref = refs[:4]
        g_ref, d_ref, nm_ref, nv_ref = refs[-4:]

        def part(s):
            if own is None:
                return p_ref[s].astype(F32)
            return jnp.where(_my_slot() == s, refs[4][...], p_ref[s]).astype(F32)

        g = part(0)
        for s in range(1, k):
            g = g + part(s)
        g_ref[...] = g
        d_ref[...], nm_ref[...], nv_ref[...] = _adamw_math(g, w_ref[...], m_ref[...], v_ref[...])

    blk = pl.BlockSpec((tr, c), lambda i: (i, 0))
    return pl.pallas_call(
        body, name=name, grid=(r // tr,),
        in_specs=[pl.BlockSpec((k, tr, c), lambda i: (0, i, 0)), blk, blk, blk] + ([] if own is None else [blk]),
        out_specs=[blk] * 4,
        out_shape=[jax.ShapeDtypeStruct((r, c), F32)] * 4,
        compiler_params=_cparams(("arbitrary",)),
    )(parts, w, m, v, *([] if own is None else [own]))


def _silu(v):
    return v * _sigmoid(v)


def _ada_fwd(c_all, w, b):
    def body(c_ref, w_ref, b_ref, o_ref):
        ca = _silu(c_ref[...]).astype(BF16)
        o_ref[...] = jnp.dot(ca, w_ref[...].astype(BF16), preferred_element_type=F32) + b_ref[...]

    return pl.pallas_call(
        body, name="ada_fwd", out_shape=jax.ShapeDtypeStruct((c_all.shape[0], w.shape[1]), F32),
        compiler_params=_cparams(),
    )(c_all, w, b)


def _ada_bwd(c_all, dmod):
    def body(c_ref, d_ref, o_ref):
        ca = _silu(c_ref[...]).astype(BF16).astype(F32)
        dm = d_ref[...].astype(BF16).astype(F32)
        acc = jnp.zeros(o_ref.shape, F32)
        for bi in range(c_all.shape[0]):
            acc = acc + jnp.transpose(ca[bi:bi + 1, :]) * dm[bi:bi + 1, :]
        o_ref[...] = acc

    return pl.pallas_call(
        body, name="ada_bwd", out_shape=jax.ShapeDtypeStruct((c_all.shape[1], dmod.shape[1]), F32),
        compiler_params=_cparams(),
    )(c_all, dmod)


COL_SHARDED = ("w_in", "w_uq", "w_ukv", "w_up")
ROW_SHARDED = ("w_proj_rnn", "w_proj_mla", "w_out", "w_down")
REPLICATED = ("b_ada", "norm1_g", "conv_b", "w_gate_a", "b_gate_a", "w_gate_x", "b_gate_x", "lru_param", "q_norm_g",
              "kv_norm_g", "norm2_g", "ffn_conv_b", "final_g")
WEIGHTS = ("w_ada", "b_ada", "norm1_g", "w_in", "conv_w", "conv_b", "w_gate_a", "b_gate_a", "w_gate_x", "b_gate_x",
           "lru_param", "q_norm_g", "w_uq", "kv_norm_g", "w_ukv", "w_proj_rnn", "w_proj_mla", "w_out", "norm2_g", "w_up",
           "ffn_conv_w", "ffn_conv_b", "w_down", "final_g")
TRANSPOSED_GRADS = COL_SHARDED
PACK_LANES = 128


def _pack(vecs, row_multiple=SUBLANES):
    flat = jnp.concatenate([v.reshape(-1).astype(F32) for v in vecs])
    pad = (-flat.shape[0]) % (PACK_LANES * row_multiple)
    return jnp.concatenate([flat, jnp.zeros((pad,), F32)]).reshape(-1, PACK_LANES)


def _unpack(packed, shapes):
    flat = packed.reshape(-1)
    out, off = [], 0
    for shp in shapes:
        size = math.prod(shp)
        out.append(flat[off:off + size].reshape(shp))
        off += size
    return out


def kernel(x, c, positions, w_ada, b_ada, norm1_g, w_in, conv_w, conv_b, w_gate_a, b_gate_a, w_gate_x, b_gate_x, lru_param, q_norm_g, w_uq, kv_norm_g, w_ukv, w_proj_rnn, w_proj_mla, w_out, norm2_g, w_up, ffn_conv_w, ffn_conv_b, w_down, final_g, loss_target, m_w_ada, m_b_ada, m_norm1_g, m_w_in, m_conv_w, m_conv_b, m_w_gate_a, m_b_gate_a, m_w_gate_x, m_b_gate_x, m_lru_param, m_q_norm_g, m_w_uq, m_kv_norm_g, m_w_ukv, m_w_proj_rnn, m_w_proj_mla, m_w_out, m_norm2_g, m_w_up, m_ffn_conv_w, m_ffn_conv_b, m_w_down, m_final_g, v_w_ada, v_b_ada, v_norm1_g, v_w_in, v_conv_w, v_conv_b, v_w_gate_a, v_b_gate_a, v_w_gate_x, v_b_gate_x, v_lru_param, v_q_norm_g, v_w_uq, v_kv_norm_g, v_w_ukv, v_w_proj_rnn, v_w_proj_mla, v_w_out, v_norm2_g, v_w_up, v_ffn_conv_w, v_ffn_conv_b, v_w_down, v_final_g):
    args = dict(locals())
    w = {n: args[n] for n in WEIGHTS}
    m = {n: args["m_" + n] for n in WEIGHTS}
    v = {n: args["v_" + n] for n in WEIGHTS}
    s, d = x.shape[1], x.shape[2]
    me = _my_slot()
    def two_d(a):
        assert a.ndim == 3 and a.shape[0] == 1, a.shape
        return a[0]

    big = COL_SHARDED + ROW_SHARDED
    shard = {n: two_d(w[n]).astype(BF16) for n in big}

    def whole(n, g):
        k, r, cc = g.shape
        return _assemble_columns(g, "assemble_" + n) if n in COL_SHARDED else g.reshape(k * r, cc)

    first = _all_gather([shard["w_in"], c, two_d(conv_w), two_d(ffn_conv_w)], "gather_first")
    c_all = first[1].reshape(N_DEV, d)
    conv_w_all = jnp.transpose(first[2], (1, 0, 2)).reshape(conv_w.shape[1], -1)
    ffn_conv_w_all = jnp.transpose(first[3], (1, 0, 2)).reshape(ffn_conv_w.shape[1], -1)

    ada_cols = w_ada.shape[2]
    b_cols = lax.dynamic_slice(b_ada, (0, me * ada_cols), (1, ada_cols))
    mod_cols = _ada_fwd(c_all, w_ada[0], b_cols)
    mod_all, = _all_gather([mod_cols], "gather_mod")

    later = ("w_uq", "w_ukv", "w_proj_rnn", "w_proj_mla", "w_out", "w_up", "w_down")
    flights, started = _gather_start([shard[n] for n in later], mod_all, "gather_start")
    flight = dict(zip(later, flights))

    def fetch(names, after):
        lands = _gather_wait([flight[n] for n in names], after, "gather_wait_" + names[0])
        return {n: whole(n, lax.dynamic_update_index_in_dim(g, shard[n], me, 0)) for n, g in zip(names, lands)}

    mod = lax.dynamic_index_in_dim(mod_all, me, axis=1, keepdims=False).reshape(6, d) + started[0, 0]

    sm = {n: w[n][0] for n in REPLICATED if n not in ("b_ada", "final_g")}
    sm["final_g"] = final_g
    sm["conv_w"] = conv_w_all
    sm["ffn_conv_w"] = ffn_conv_w_all
    in_flight, windows = {}, {}

    def emit(n, g):
        rows = g.shape[0] // N_DEV
        if rows % ROW_ALIGN == 0:
            windows[n] = (None, None)
            g = g.reshape(N_DEV, rows, g.shape[1])
        else:
            span = max(rows * k - _span_start(k, rows) for k in range(N_DEV)) + rows
            windows[n] = (rows, -(-span // ROW_ALIGN) * ROW_ALIGN)
            assert _span_start(N_DEV - 1, rows) + windows[n][1] <= g.shape[0], (n, g.shape)
        *in_flight[n], token = _scatter_start(g, "scatter_start_" + n, *windows[n])
        return token[0, 0]

    sq, grad_x, gs, dmod = _local_step(x[0], mod, positions[0], loss_target[0], whole("w_in", first[0]), fetch, sm, emit)

    small_names = [n for n in REPLICATED if n != "b_ada"] + ["conv_w", "ffn_conv_w"]
    small_shapes = [gs[n].shape for n in small_names] + [(6 * d,), (1,)]
    partial = _pack([gs[n] for n in small_names] + [dmod, sq.reshape(1)], N_DEV * SUBLANES)
    *small_flight, small_started = _scatter_start(partial.reshape(N_DEV, -1, PACK_LANES), "scatter_small_start")

    grads, deltas, new_m, new_v = {}, {}, {}, {}

    def update(n, parts, own=None):
        shp = w[n].shape
        lay = jnp.transpose if n in TRANSPOSED_GRADS else (lambda a: a)
        res = _adamw(parts, lay(two_d(w[n])), lay(two_d(m[n])), lay(two_d(v[n])), "adamw_" + n, own)
        grads[n], deltas[n], new_m[n], new_v[n] = [lay(a).reshape(shp) for a in res]

    for n in big:
        rows, span = windows[n]
        src, landed = _scatter_wait(*in_flight[n], small_started, "scatter_wait_" + n, rows, span)
        if rows is None:
            update(n, landed, lax.dynamic_index_in_dim(src, me, axis=0, keepdims=False))
        else:
            start = _span_start(me, rows)
            own = lax.dynamic_slice(src, (start, 0), (span, src.shape[1]))
            total = _sum_sources(landed, own, "sum_" + n)
            update(n, lax.dynamic_slice(total, (rows * me - start, 0), (rows, src.shape[1]))[None])

    chunks, landed = _scatter_wait(*small_flight, new_v[big[-1]], "scatter_small_wait")
    mine = _sum_sources(landed, lax.dynamic_index_in_dim(chunks, me, axis=0, keepdims=False), "sum_small")
    summed_all, dmod_all = _all_gather([mine, dmod.reshape(1, 6 * d)], "gather_small")
    summed = _unpack(summed_all, small_shapes)
    g_small = dict(zip(small_names, summed[:len(small_names)]))
    g_small["b_ada"] = summed[len(small_names)]
    loss = 0.5 * summed[-1][0] / d
    dmod_cols = lax.dynamic_slice(dmod_all.reshape(N_DEV, 6 * d), (0, me * ada_cols), (N_DEV, ada_cols))

    update("w_ada", _ada_bwd(c_all, dmod_cols)[None])

    for n in ("conv_w", "ffn_conv_w"):
        cols = w[n].shape[2]
        g_small[n] = lax.dynamic_slice(g_small[n], (0, me * cols), (g_small[n].shape[0], cols))
    small = REPLICATED + ("conv_w", "ffn_conv_w")
    as_rows = lambda a: a.reshape(1, -1) if a.ndim == 1 else a
    res = _adamw_many([as_rows(g_small[n].reshape(w[n].shape)) for n in small], [as_rows(w[n]) for n in small],
                      [as_rows(m[n]) for n in small], [as_rows(v[n]) for n in small], "adamw_small")
    for i, n in enumerate(small):
        grads[n], deltas[n], new_m[n], new_v[n] = [a.reshape(w[n].shape) for a in res[4 * i:4 * i + 4]]

    return (loss, grad_x[None], *[grads[n] for n in WEIGHTS], *[deltas[n] for n in WEIGHTS],
            *[new_m[n] for n in WEIGHTS], *[new_v[n] for n in WEIGHTS])
```

```python
import functools
import math

import jax
import jax.numpy as jnp
from jax import lax
from jax.experimental import pallas as pl
from jax.experimental.pallas import tpu as pltpu

F32 = jnp.float32
BF16 = jnp.bfloat16

N_DEV = 8
LANES = 128
SUBLANES = 8
VMEM_LIMIT = 56 * 1024 * 1024

D_RNN = 1280
Q_LORA = 384
KV_LORA = 256
QK_NOPE = 64
QK_ROPE = 32
V_HEAD = 64
N_HEADS = 16
D_FF = 2816
ROPE_THETA = 10000.0
LRU_C = 8.0
EPS = 1e-6
MLA_W = 768
ATT_SCALE = 1.0 / math.sqrt(QK_NOPE + QK_ROPE)

ADAM_LR, ADAM_B1, ADAM_B2, ADAM_EPS, ADAM_WD, ADAM_STEP = 0.001, 0.9, 0.999, 1e-08, 0.01, 10


def _cparams(sem=None):
    return pltpu.CompilerParams(dimension_semantics=sem, vmem_limit_bytes=VMEM_LIMIT)


def _pick(n, prefs):
    for p in prefs:
        if n % p == 0:
            return p
    return n


def _sigmoid(v):
    return 0.5 * jnp.tanh(0.5 * v) + 0.5


def _lane(shape):
    return lax.broadcasted_iota(jnp.int32, shape, len(shape) - 1)


def _row(shape):
    return lax.broadcasted_iota(jnp.int32, shape, len(shape) - 2)


MM_BLOCK_BYTES = 36 * 1024 * 1024


def _divisors(n):
    return [t for t in range(n, 0, -LANES) if n % t == 0] if n % LANES == 0 else [n]


HBM_BYTES_PER_US = 3.0e6
MXU_FLOPS_PER_US = 8.0e8
GRID_STEP_US = 0.35


def _mm_tiles(m, n, k, a_bytes, b_bytes, o_bytes):
    best = None
    for tm in [t for t in _divisors(m) if t <= 1024]:
        for tn in [t for t in _divisors(n) if t <= 2048]:
            for tk in _divisors(k):
                nk = k // tk
                need = 2 * (tm * tk * a_bytes + tk * tn * b_bytes + tm * tn * o_bytes) + (tm * tn * 4 if nk > 1 else 0)
                if need > MM_BLOCK_BYTES:
                    continue
                gi, gj = m // tm, n // tn
                for rows_outer in (True, False):
                    if nk > 1:
                        a_reads, b_reads = gj, gi
                    elif rows_outer:
                        a_reads, b_reads = 1, (gi if gj > 1 else 1)
                    else:
                        a_reads, b_reads = (gj if gi > 1 else 1), 1
                    traffic = m * k * a_bytes * a_reads + k * n * b_bytes * b_reads + m * n * (o_bytes + (8 * nk if nk > 1 else 0))
                    cost = max(traffic / HBM_BYTES_PER_US, 2.0 * m * n * k / MXU_FLOPS_PER_US) + gi * gj * nk * GRID_STEP_US
                    if best is None or cost < best[0]:
                        best = (cost, tm, tn, tk, rows_outer)
                break
    if best is None:
        raise ValueError((m, n, k))
    return best[1:]


def _mm(a, b, *, ta=False, tb=False, out_dtype=F32, also_t=None, name):
    (k_a, m) = a.shape if ta else a.shape[::-1]
    (n, k_b) = b.shape if tb else b.shape[::-1]
    assert k_a == k_b, (a.shape, b.shape, ta, tb)
    k = k_a
    tm, tn, tk, rows_outer = _mm_tiles(m, n, k, a.dtype.itemsize, b.dtype.itemsize, jnp.dtype(out_dtype).itemsize)
    nk = k // tk
    dims = (((0 if ta else 1,), (1 if tb else 0,)), ((), ()))
    n_out = 1 if also_t is None else 2

    def body(a_ref, b_ref, *rest):
        outs, acc = rest[:n_out], rest[n_out:]
        part = lax.dot_general(a_ref[...].astype(BF16), b_ref[...].astype(BF16), dims, preferred_element_type=F32)

        def write(val):
            outs[0][...] = val.astype(out_dtype)
            if also_t is not None:
                outs[1][...] = jnp.transpose(val).astype(also_t)

        if nk == 1:
            write(part)
            return
        acc_ref, = acc
        kk = pl.program_id(2)

        @pl.when(kk == 0)
        def _():
            acc_ref[...] = part

        @pl.when(kk > 0)
        def _():
            acc_ref[...] += part

        @pl.when(kk == nk - 1)
        def _():
            write(acc_ref[...])

    def ij(f):
        return (lambda i, j, kk: f(i, j, kk)) if rows_outer else (lambda j, i, kk: f(i, j, kk))

    a_spec = pl.BlockSpec((tk, tm), ij(lambda i, j, kk: (kk, i))) if ta else pl.BlockSpec((tm, tk), ij(lambda i, j, kk: (i, kk)))
    b_spec = pl.BlockSpec((tn, tk), ij(lambda i, j, kk: (j, kk))) if tb else pl.BlockSpec((tk, tn), ij(lambda i, j, kk: (kk, j)))
    out_specs = [pl.BlockSpec((tm, tn), ij(lambda i, j, kk: (i, j)))]
    out_shape = [jax.ShapeDtypeStruct((m, n), out_dtype)]
    if also_t is not None:
        out_specs.append(pl.BlockSpec((tn, tm), ij(lambda i, j, kk: (j, i))))
        out_shape.append(jax.ShapeDtypeStruct((n, m), also_t))
    res = pl.pallas_call(
        body, name=name,
        grid=(m // tm, n // tn, nk) if rows_outer else (n // tn, m // tm, nk),
        in_specs=[a_spec, b_spec], out_specs=out_specs, out_shape=out_shape,
        scratch_shapes=[] if nk == 1 else [pltpu.VMEM((tm, tn), F32)],
        compiler_params=_cparams(("arbitrary", "arbitrary", "arbitrary")),
    )(a, b)
    return res[0] if also_t is None else res


ROW_BLOCK_BYTES = 28 * 1024 * 1024


def _rowwise(fn, row_ins, par_ins, out_defs, red_defs, *, name):
    s = row_ins[0].shape[0]
    row_bytes = sum(a.shape[1] * a.dtype.itemsize for a in row_ins) + sum(c * jnp.dtype(dt).itemsize for c, dt in out_defs)
    tr = next((t for t in (512, 256, 128) if s % t == 0 and 2 * t * row_bytes <= ROW_BLOCK_BYTES), min(s, 128))
    nr, npar, no = len(row_ins), len(par_ins), len(out_defs)

    def body(*refs):
        rin, pin = refs[:nr], refs[nr:nr + npar]
        outs, reds = refs[nr + npar:nr + npar + no], refs[nr + npar + no:]
        i = pl.program_id(0)

        @pl.when(i == 0)
        def _():
            for r in reds:
                r[...] = jnp.zeros_like(r)

        fn(i, rin, pin, outs, reds)

    in_specs = [pl.BlockSpec((tr, a.shape[1]), lambda i: (i, 0)) for a in row_ins]
    in_specs += [pl.BlockSpec(a.shape, lambda i, nd=a.ndim: (0,) * nd) for a in par_ins]
    out_specs = [pl.BlockSpec((tr, c), lambda i: (i, 0)) for c, _ in out_defs]
    out_specs += [pl.BlockSpec(shp, lambda i: (0, 0)) for shp in red_defs]
    out_shape = [jax.ShapeDtypeStruct((s, c), dt) for c, dt in out_defs]
    out_shape += [jax.ShapeDtypeStruct(shp, F32) for shp in red_defs]
    return pl.pallas_call(
        body, name=name, grid=(s // tr,), in_specs=in_specs, out_specs=out_specs, out_shape=out_shape,
        compiler_params=_cparams(("arbitrary",)),
    )(*row_ins, *par_ins)


def _rms(v):
    return lax.rsqrt(jnp.mean(v * v, axis=-1, keepdims=True) + EPS)


def _colsum(v):
    return jnp.sum(v, axis=0, keepdims=True)


def _rms_bwd(dn, n, rstd):
    return rstd * (dn - n * jnp.mean(dn * n, axis=-1, keepdims=True))


def _norm_mod_fwd(x, gmod, name):
    def fn(i, rin, pin, outs, reds):
        xv = rin[0][...]
        p = pin[0][...]
        n = xv * _rms(xv)
        outs[0][...] = ((n * p[0:1]) * (1.0 + p[1:2]) + p[2:3]).astype(BF16)

    return _rowwise(fn, [x], [gmod], [(x.shape[1], BF16)], [], name=name)[0]


def _rope(v, rot_c, rot_s):
    half = QK_ROPE // 2
    swapped = jnp.where(_lane(v.shape) < QK_NOPE + half, pltpu.roll(v, LANES - half, 1), pltpu.roll(v, half, 1))
    return v * rot_c + swapped * rot_s


def _rope_t(dv, rot_c, rot_s):
    half = QK_ROPE // 2
    ds = dv * rot_s
    lane = _lane(dv.shape)
    swapped = jnp.where(lane < QK_NOPE + half, pltpu.roll(ds, LANES - half, 1), pltpu.roll(ds, half, 1))
    in_rope = (lane >= QK_NOPE) & (lane < QK_NOPE + QK_ROPE)
    return dv * rot_c + jnp.where(in_rope, swapped, 0.0)


def _mla_prep_fwd(proj_mla, rot_c, rot_s, ng):
    o1, o2 = Q_LORA, Q_LORA + KV_LORA

    def fn(i, rin, pin, outs, reds):
        g = pin[0][...]
        ql = rin[0][:, 0:o1]
        kl = rin[0][:, o1:o2]
        outs[0][...] = (ql * _rms(ql) * g[0:1, 0:o1]).astype(BF16)
        outs[1][...] = (kl * _rms(kl) * g[0:1, o1:o2]).astype(BF16)
        kr = pltpu.roll(rin[0][:, o2:o2 + LANES], QK_NOPE, 1)
        outs[2][...] = _rope(kr, rin[1][...], rin[2][...]).astype(BF16)

    return _rowwise(fn, [proj_mla, rot_c, rot_s], [ng], [(Q_LORA, BF16), (KV_LORA, BF16), (LANES, BF16)], [],
                    name="mla_prep_fwd")


def _mla_prep_bwd(proj_mla, dqn, dkvn, dkr, rot_c, rot_s, ng):
    o1, o2 = Q_LORA, Q_LORA + KV_LORA

    def fn(i, rin, pin, outs, reds):
        g = pin[0][...]
        ql = rin[0][:, 0:o1]
        kl = rin[0][:, o1:o2]
        rq, rk = _rms(ql), _rms(kl)
        nq, nk = ql * rq, kl * rk
        dq, dk = rin[1][...], rin[2][...]
        outs[0][:, 0:o1] = _rms_bwd(dq * g[0:1, 0:o1], nq, rq).astype(BF16)
        outs[0][:, o1:o2] = _rms_bwd(dk * g[0:1, o1:o2], nk, rk).astype(BF16)
        dkr_pre = pltpu.roll(_rope_t(rin[3][...], rin[4][...], rin[5][...]), LANES - QK_NOPE, 1)
        outs[0][:, o2:] = jnp.where(_lane(dkr_pre.shape) < QK_ROPE, dkr_pre, 0.0).astype(BF16)
        reds[0][0:1, 0:o1] += _colsum(dq * nq)
        reds[0][0:1, o1:o2] += _colsum(dk * nk)

    return _rowwise(fn, [proj_mla, dqn, dkvn, dkr, rot_c, rot_s], [ng], [(MLA_W, BF16)], [(SUBLANES, MLA_W)],
                    name="mla_prep_bwd")


def _rope_bwd(dq, rot_c, rot_s):
    def fn(i, rin, pin, outs, reds):
        c, sn = rin[1][...] * Q_PRESCALE, rin[2][...] * Q_PRESCALE
        for h in range(N_HEADS):
            sl = slice(h * LANES, (h + 1) * LANES)
            outs[0][:, sl] = _rope_t(rin[0][:, sl], c, sn).astype(BF16)

    return _rowwise(fn, [dq, rot_c, rot_s], [], [(dq.shape[1], BF16)], [], name="rope_bwd")[0]


def _rope_fwd_t(q, rot_c, rot_s):
    s, c = q.shape
    tr = min(256, s)

    def body(q_ref, c_ref, s_ref, o_ref, ot_ref):
        cc, sn = c_ref[...] * Q_PRESCALE, s_ref[...] * Q_PRESCALE
        for h in range(N_HEADS):
            sl = slice(h * LANES, (h + 1) * LANES)
            rot = _rope(q_ref[:, sl], cc, sn)
            o_ref[:, sl] = rot.astype(BF16)
            ot_ref[sl, :] = jnp.transpose(rot).astype(BF16)

    return pl.pallas_call(
        body, name="rope_fwd", grid=(s // tr,),
        in_specs=[pl.BlockSpec((tr, c), lambda i: (i, 0)), pl.BlockSpec((tr, LANES), lambda i: (i, 0)),
                  pl.BlockSpec((tr, LANES), lambda i: (i, 0))],
        out_specs=[pl.BlockSpec((tr, c), lambda i: (i, 0)), pl.BlockSpec((c, tr), lambda i: (0, i))],
        out_shape=[jax.ShapeDtypeStruct((s, c), BF16), jax.ShapeDtypeStruct((c, s), BF16)],
        compiler_params=_cparams(("arbitrary",)),
    )(q, rot_c, rot_s)


def _merge_fwd(pr, pm, proj_g):
    d = pr.shape[1]

    def fn(i, rin, pin, outs, reds):
        g_rnn, g_mla = rin[2][:, 0:d].astype(F32), rin[2][:, d:].astype(F32)
        outs[0][...] = (_sigmoid(g_rnn) * rin[0][...].astype(F32) + _sigmoid(g_mla) * rin[1][...].astype(F32)).astype(BF16)

    return _rowwise(fn, [pr, pm, proj_g], [], [(d, BF16)], [], name="merge_fwd")[0]


def _merge_bwd(dmerged, pr, pm, proj_g):
    d = pr.shape[1]

    def fn(i, rin, pin, outs, reds):
        dm = rin[0][...]
        sr, sm = _sigmoid(rin[3][:, 0:d].astype(F32)), _sigmoid(rin[3][:, d:].astype(F32))
        outs[0][...] = (dm * sr).astype(BF16)
        outs[1][...] = (dm * sm).astype(BF16)
        outs[2][:, 0:d] = (dm * rin[1][...].astype(F32) * sr * (1.0 - sr)).astype(BF16)
        outs[2][:, d:] = (dm * rin[2][...].astype(F32) * sm * (1.0 - sm)).astype(BF16)

    return _rowwise(fn, [dmerged, pr, pm, proj_g], [], [(d, BF16), (d, BF16), (2 * d, BF16)], [], name="merge_bwd")


def _resid_norm_fwd(x, o, gmod):
    d = x.shape[1]

    def fn(i, rin, pin, outs, reds):
        p = pin[0][...]
        x1 = rin[0][...] + p[3:4] * rin[1][...]
        outs[0][...] = x1
        outs[1][...] = ((x1 * _rms(x1) * p[0:1]) * (1.0 + p[1:2]) + p[2:3]).astype(BF16)

    return _rowwise(fn, [x, o], [gmod], [(d, F32), (d, BF16)], [], name="resid_norm_fwd")


def _final_fwd_bwd(x1, dn, target, par):
    d = x1.shape[1]

    def fn(i, rin, pin, outs, reds):
        p = pin[0][...]
        dnv = rin[1][...]
        x2 = rin[0][...] + p[0:1] * dnv
        rstd = _rms(x2)
        n3 = x2 * rstd
        err = n3 * p[1:2] - rin[2][...]
        dy = err * (1.0 / d)
        dx2 = _rms_bwd(dy * p[1:2], n3, rstd)
        outs[0][...] = dx2
        outs[1][...] = (dx2 * p[0:1]).astype(BF16)
        reds[0][0:1, :] += _colsum(dy * n3)
        reds[0][1:2, :] += _colsum(dx2 * dnv)
        reds[0][2:3, :] += jnp.zeros((1, d), F32) + jnp.sum(err * err)

    return _rowwise(fn, [x1, dn, target], [par], [(d, F32), (d, BF16)], [(SUBLANES, d)], name="final_fwd_bwd")


def _norm2_bwd(x1, dh2, dx2, o, gmod):
    d = x1.shape[1]

    def fn(i, rin, pin, outs, reds):
        p = pin[0][...]
        x1v, dh = rin[0][...], rin[1][...]
        rstd = _rms(x1v)
        n2 = x1v * rstd
        dx1 = rin[2][...] + _rms_bwd(dh * (p[0:1] * (1.0 + p[1:2])), n2, rstd)
        outs[0][...] = dx1
        outs[1][...] = (dx1 * p[3:4]).astype(BF16)
        reds[0][0:1, :] += _colsum(dh * n2 * (1.0 + p[1:2]))
        reds[0][1:2, :] += _colsum(dh * n2 * p[0:1])
        reds[0][2:3, :] += _colsum(dh)
        reds[0][3:4, :] += _colsum(dx1 * rin[3][...])

    return _rowwise(fn, [x1, dh2, dx2, o], [gmod], [(d, F32), (d, BF16)], [(SUBLANES, d)], name="norm2_bwd")


def _norm1_bwd(x, dh_a, dh_b, dh_c, dx1, gmod):
    d = x.shape[1]

    def fn(i, rin, pin, outs, reds):
        p = pin[0][...]
        xv = rin[0][...]
        dh = rin[1][...] + rin[2][...] + rin[3][...]
        rstd = _rms(xv)
        n1 = xv * rstd
        outs[0][...] = rin[4][...] + _rms_bwd(dh * (p[0:1] * (1.0 + p[1:2])), n1, rstd)
        reds[0][0:1, :] += _colsum(dh * n1 * (1.0 + p[1:2]))
        reds[0][1:2, :] += _colsum(dh * n1 * p[0:1])
        reds[0][2:3, :] += _colsum(dh)

    return _rowwise(fn, [x, dh_a, dh_b, dh_c, dx1], [gmod], [(d, F32)], [(SUBLANES, d)], name="norm1_bwd")


RNN_CHUNK = 512


def _shift_down(ref, base, n, j):
    v = ref[pl.ds(base, n + SUBLANES), :]
    return v[SUBLANES:] if j == 0 else pltpu.roll(v, j, 0)[SUBLANES:]


def _shift_up(ref, base, n, j, top_pad):
    v = ref[pl.ds(base + top_pad, n + SUBLANES), :]
    return v[:n] if j == 0 else pltpu.roll(v, n + SUBLANES - j, 0)[:n]


SCAN_GROUP = 128


def _scan_sizes(s):
    sizes = [s]
    while sizes[-1] > SUBLANES:
        assert sizes[-1] % SUBLANES == 0, s
        sizes.append(sizes[-1] // SUBLANES)
    return sizes


def _scan_scratch(s):
    return [pltpu.VMEM((n + 2 * SUBLANES, LANES), F32) for n in _scan_sizes(s)[1:] for _ in range(2)]


def _linear_scan(a_ref, b_ref, out_ref, a_off, s, reverse, levels):
    sizes = _scan_sizes(s)
    lv = [(a_ref, b_ref, a_off, 0)] + [(levels[2 * i], levels[2 * i + 1], 0, SUBLANES) for i in range(len(sizes) - 1)]
    zero8 = jnp.zeros((SUBLANES, LANES), F32)
    for (ar, br, _, _), n in zip(lv[1:], sizes[1:]):
        br[0:SUBLANES, :] = zero8
        br[pl.ds(n + SUBLANES, SUBLANES), :] = zero8
    order = list(range(SUBLANES - 1, -1, -1)) if reverse else list(range(SUBLANES))

    for lvl in range(len(sizes) - 1):
        ar, br, aoff, off = lv[lvl]
        m = sizes[lvl + 1]
        g = min(m, SCAN_GROUP)
        for t0 in range(0, m, g):
            acc_a = acc_b = None
            for r in order:
                sa = pl.ds(off + SUBLANES * t0 + r + aoff, g, stride=SUBLANES)
                sb = pl.ds(off + SUBLANES * t0 + r, g, stride=SUBLANES)
                a, b = ar[sa, :], br[sb, :]
                if acc_a is None:
                    acc_a, acc_b = a, b
                else:
                    acc_b = a * acc_b + b
                    acc_a = a * acc_a
            lv[lvl + 1][0][pl.ds(SUBLANES + t0, g), :] = acc_a
            lv[lvl + 1][1][pl.ds(SUBLANES + t0, g), :] = acc_b

    ar, br, _, off = lv[-1]
    n = sizes[-1]
    a, b = ar[pl.ds(off, n), :], br[pl.ds(off, n), :]
    h, rows = jnp.zeros((1, LANES), F32), [None] * n
    for j in (range(n - 1, -1, -1) if reverse else range(n)):
        h = a[j:j + 1, :] * h + b[j:j + 1, :]
        rows[j] = h
    br[pl.ds(off, n), :] = jnp.concatenate(rows, axis=0)

    for lvl in range(len(sizes) - 2, -1, -1):
        ar, br, aoff, off = lv[lvl]
        m = sizes[lvl + 1]
        up = lv[lvl + 1][1]
        dst = out_ref if lvl == 0 else br
        g = min(m, SCAN_GROUP)
        for t0 in range(0, m, g):
            h = _shift_up(up, t0, g, 1, SUBLANES) if reverse else _shift_down(up, t0, g, 1)
            for r in order:
                sa = pl.ds(off + SUBLANES * t0 + r + aoff, g, stride=SUBLANES)
                sb = pl.ds(off + SUBLANES * t0 + r, g, stride=SUBLANES)
                h = ar[sa, :] * h + br[sb, :]
                dst[sb, :] = h


def _one_minus_exp(z):
    series = -z * (1.0 + z * (0.5 + z * (1.0 / 6.0 + z * (1.0 / 24.0 + z * (1.0 / 120.0 + z * (1.0 / 720.0))))))
    return jnp.where(z > -0.1, series, 1.0 - jnp.exp(z))


def _softplus(v):
    return jnp.maximum(v, 0.0) + jnp.log(1.0 + jnp.exp(-jnp.abs(v)))


def _rnn_gates(xc, w, wa, wx, sp):
    xb = xc.astype(BF16)
    ra = _sigmoid(jnp.dot(xb, wa, preferred_element_type=F32) + w[5:6])
    ix = _sigmoid(jnp.dot(xb, wx, preferred_element_type=F32) + w[6:7])
    la = (-LRU_C) * ra * sp
    a = jnp.exp(la)
    mult = jnp.sqrt(_one_minus_exp(2.0 * la))
    return ra, ix, a, mult


def _rnn_fwd(x_rnn, keep, rp, wa_bd, wx_bd):
    s, r = x_rnn.shape
    ts = min(RNN_CHUNK, s)

    def body(x_ref, keep_ref, rp_ref, wa_ref, wx_ref, xc_ref, ra_ref, ix_ref, hs_ref, hsb_ref, xpad, a_s, b_s, *levels):
        xpad[0:SUBLANES, :] = jnp.zeros((SUBLANES, LANES), F32)
        xpad[SUBLANES:, :] = x_ref[...]
        w = rp_ref[...]
        sp = _softplus(-w[7:8])
        wa, wx = wa_ref[0], wx_ref[0]

        def chunk(c, carry):
            base = pl.multiple_of(c * ts, ts)
            xc = w[4:5] + w[3:4] * _shift_down(xpad, base, ts, 0)
            for j in range(1, 4):
                xc = xc + w[3 - j:4 - j] * _shift_down(xpad, base, ts, j)
            ra, ix, a, mult = _rnn_gates(xc, w, wa, wx, sp)
            kp = keep_ref[pl.ds(base, ts), :]
            xc_ref[pl.ds(base, ts), :] = xc
            ra_ref[pl.ds(base, ts), :] = ra
            ix_ref[pl.ds(base, ts), :] = ix
            a_s[pl.ds(base, ts), :] = a * kp
            b_s[pl.ds(base, ts), :] = jnp.where(kp > 0.0, mult, 1.0) * (ix * xc)
            return carry

        lax.fori_loop(0, s // ts, chunk, 0)

        _linear_scan(a_s, b_s, hs_ref, 0, s, False, levels)
        hsb_ref[...] = hs_ref[...].astype(BF16)

    col = pl.BlockSpec((s, LANES), lambda g: (0, g))
    return pl.pallas_call(
        body, name="rnn_fwd", grid=(r // LANES,),
        in_specs=[col, pl.BlockSpec((s, 1), lambda g: (0, 0)), pl.BlockSpec((SUBLANES, LANES), lambda g: (0, g)),
                  pl.BlockSpec((1, LANES, LANES), lambda g: (g, 0, 0)), pl.BlockSpec((1, LANES, LANES), lambda g: (g, 0, 0))],
        out_specs=[col] * 5,
        out_shape=[jax.ShapeDtypeStruct((s, r), F32)] * 4 + [jax.ShapeDtypeStruct((s, r), BF16)],
        scratch_shapes=[pltpu.VMEM((s + SUBLANES, LANES), F32), pltpu.VMEM((s, LANES), F32), pltpu.VMEM((s, LANES), F32),
                        *_scan_scratch(s)],
        compiler_params=_cparams(("arbitrary",)),
    )(x_rnn, keep, rp, wa_bd, wx_bd)


def _rnn_bwd(x_rnn, xc, ra, ix, hs, dy, keep, rp, wa_bd, wx_bd):
    s, r = x_rnn.shape
    ts = min(RNN_CHUNK, s)

    def body(x_ref, xc_ref, ra_ref, ix_ref, hs_ref, dy_ref, keep_ref, rp_ref, wa_ref, wx_ref,
             dx_ref, dwa_ref, dwx_ref, red_ref, xpad, hpad, a_s, dh_s, dxc_s, *levels):
        zero8 = jnp.zeros((SUBLANES, LANES), F32)
        xpad[0:SUBLANES, :] = zero8
        xpad[SUBLANES:, :] = x_ref[...]
        hpad[0:SUBLANES, :] = zero8
        hpad[SUBLANES:, :] = hs_ref[...]
        a_s[s:, :] = zero8
        dxc_s[s:, :] = zero8
        w = rp_ref[...]
        sp = _softplus(-w[7:8])
        wa, wx = wa_ref[0], wx_ref[0]

        def decay(c, carry):
            base = pl.multiple_of(c * ts, ts)
            a = jnp.exp((-LRU_C) * ra_ref[pl.ds(base, ts), :] * sp)
            a_s[pl.ds(base, ts), :] = a * keep_ref[pl.ds(base, ts), :]
            return carry

        lax.fori_loop(0, s // ts, decay, 0)

        _linear_scan(a_s, dy_ref, dh_s, 1, s, True, levels)

        def gates(c, carry):
            dwa, dwx, d_ba, d_bx, d_sp, d_cb = carry
            base = pl.multiple_of(c * ts, ts)
            xcv = xc_ref[pl.ds(base, ts), :]
            rav = ra_ref[pl.ds(base, ts), :]
            ixv = ix_ref[pl.ds(base, ts), :]
            kp = keep_ref[pl.ds(base, ts), :]
            dh = dh_s[pl.ds(base, ts), :]
            h_prev = _shift_down(hpad, base, ts, 1)
            la = (-LRU_C) * rav * sp
            a = jnp.exp(la)
            mult = jnp.sqrt(_one_minus_exp(2.0 * la))
            mult_eff = jnp.where(kp > 0.0, mult, 1.0)
            d_a = dh * h_prev * kp
            d_mult = dh * (ixv * xcv) * kp
            d_ix = dh * mult_eff * xcv
            d_xc = dh * mult_eff * ixv
            d_la = d_a * a - d_mult * (a * a) / mult
            d_pa = d_la * ((-LRU_C) * sp) * rav * (1.0 - rav)
            d_px = d_ix * ixv * (1.0 - ixv)
            xb = xcv.astype(BF16)
            pab, pxb = d_pa.astype(BF16), d_px.astype(BF16)
            tn = (((0,), (0,)), ((), ()))
            nt_ = (((1,), (1,)), ((), ()))
            dwa = dwa + lax.dot_general(xb, pab, tn, preferred_element_type=F32)
            dwx = dwx + lax.dot_general(xb, pxb, tn, preferred_element_type=F32)
            d_xc = d_xc + lax.dot_general(pab, wa, nt_, preferred_element_type=F32)
            d_xc = d_xc + lax.dot_general(pxb, wx, nt_, preferred_element_type=F32)
            dxc_s[pl.ds(base, ts), :] = d_xc
            return (dwa, dwx, d_ba + _colsum(d_pa), d_bx + _colsum(d_px),
                    d_sp + _colsum(d_la * ((-LRU_C) * rav)), d_cb + _colsum(d_xc))

        z1 = jnp.zeros((1, LANES), F32)
        zw = jnp.zeros((LANES, LANES), F32)
        dwa, dwx, d_ba, d_bx, d_sp, d_cb = lax.fori_loop(0, s // ts, gates, (zw, zw, z1, z1, z1, z1))
        dwa_ref[0] = dwa
        dwx_ref[0] = dwx

        def conv(c, carry):
            base = pl.multiple_of(c * ts, ts)
            d_here = dxc_s[pl.ds(base, ts), :]
            dx = w[3:4] * d_here
            for j in range(1, 4):
                dx = dx + w[3 - j:4 - j] * _shift_up(dxc_s, base, ts, j, 0)
            dx_ref[pl.ds(base, ts), :] = dx.astype(BF16)
            return tuple(carry[k] + _colsum(d_here * _shift_down(xpad, base, ts, 3 - k)) for k in range(4))

        d_w = lax.fori_loop(0, s // ts, conv, (z1, z1, z1, z1))
        d_lru = d_sp * (-_sigmoid(-w[7:8]))
        red_ref[...] = jnp.concatenate(list(d_w) + [d_cb, d_ba, d_bx, d_lru], axis=0)

    col = pl.BlockSpec((s, LANES), lambda g: (0, g))
    par = pl.BlockSpec((SUBLANES, LANES), lambda g: (0, g))
    wsp = pl.BlockSpec((1, LANES, LANES), lambda g: (g, 0, 0))
    return pl.pallas_call(
        body, name="rnn_bwd", grid=(r // LANES,),
        in_specs=[col] * 6 + [pl.BlockSpec((s, 1), lambda g: (0, 0)), par, wsp, wsp],
        out_specs=[col, wsp, wsp, par],
        out_shape=[jax.ShapeDtypeStruct((s, r), BF16), jax.ShapeDtypeStruct((r // LANES, LANES, LANES), F32),
                   jax.ShapeDtypeStruct((r // LANES, LANES, LANES), F32), jax.ShapeDtypeStruct((SUBLANES, r), F32)],
        scratch_shapes=[pltpu.VMEM((s + SUBLANES, LANES), F32), pltpu.VMEM((s + SUBLANES, LANES), F32),
                        pltpu.VMEM((s + SUBLANES, LANES), F32), pltpu.VMEM((s, LANES), F32),
                        pltpu.VMEM((s + SUBLANES, LANES), F32), *_scan_scratch(s)],
        compiler_params=_cparams(("arbitrary",)),
    )(x_rnn, xc, ra, ix, hs, dy, keep, rp, wa_bd, wx_bd)


ATT_BLOCK = 512
ATT_FWD_HEADS = 8
ATT_BWD_HEADS = 4


LOG2E = 1.4426950408889634
LN2 = 0.6931471805599453
Q_PRESCALE = ATT_SCALE * LOG2E


def _att_scores(q, kvt, krt, diagonal):
    kt_eff = jnp.where(_row(kvt.shape) < QK_NOPE, kvt, krt)
    sc = jnp.dot(q, kt_eff, preferred_element_type=F32)
    if diagonal:
        sc = jnp.where(lax.broadcasted_iota(jnp.int32, sc.shape, 1) <= lax.broadcasted_iota(jnp.int32, sc.shape, 0), sc, -jnp.inf)
    return sc


def _att_fwd(q, kv, kvt, krt):
    s = q.shape[0]
    t = min(ATT_BLOCK, s)
    nb = s // t
    hp = ATT_FWD_HEADS

    pairs = [(i, j) for i in range(nb) for j in range(i + 1)]
    i_tab = jnp.array([p[0] for p in pairs], jnp.int32)
    j_tab = jnp.array([p[1] for p in pairs], jnp.int32)

    def body(i_ref, j_ref, q_ref, kv_ref, kvt_ref, krt_ref, y_ref, lse_ref, m_s, acc_s):
        i, j = i_ref[pl.program_id(1)], j_ref[pl.program_id(1)]

        @pl.when(j == 0)
        def _():
            m_s[...] = jnp.full(m_s.shape, -jnp.inf, F32)
            acc_s[...] = jnp.zeros(acc_s.shape, F32)

        def step(diagonal):
            krt_b = krt_ref[...]
            lane = _lane((t, LANES))
            groups = [slice(c * LANES, (c + 1) * LANES) for c in range(t // LANES)]
            heads = [slice(hh * LANES, (hh + 1) * LANES) for hh in range(hp)]
            scs = [_att_scores(q_ref[:, sl], kvt_ref[sl, :], krt_b, diagonal) for sl in heads]
            stats = []
            for hh in range(hp):
                m_prev = m_s[hh]
                m_blk = scs[hh][:, groups[0]]
                for g in groups[1:]:
                    m_blk = jnp.maximum(m_blk, scs[hh][:, g])
                stats.append((m_prev, jnp.maximum(m_prev, jnp.max(m_blk, axis=-1, keepdims=True))))
            for hh in range(hp):
                m_prev, m_new = stats[hh]
                kvb = kv_ref[:, heads[hh]]
                ones_v = jnp.where(lane < QK_NOPE, jnp.ones_like(kvb), kvb)
                p = jnp.concatenate([jnp.exp2(scs[hh][:, g] - m_new).astype(BF16) for g in groups], axis=1)
                acc_s[hh] = jnp.exp2(m_prev - m_new) * acc_s[hh] + jnp.dot(p, ones_v, preferred_element_type=F32)
                m_s[hh] = m_new

        @pl.when(j < i)
        def _():
            step(False)

        @pl.when(j == i)
        def _():
            step(True)
            lane = _lane((t, LANES))
            for g in range(hp // 2):
                sl = slice(g * LANES, (g + 1) * LANES)
                a0, a1 = acc_s[2 * g], acc_s[2 * g + 1]
                l0, l1 = a0[:, 0:1], a1[:, 0:1]
                y_ref[:, sl] = jnp.where(lane < V_HEAD, pltpu.roll(a0 / l0, V_HEAD, 1), a1 / l1).astype(BF16)
                lse_ref[:, sl] = jnp.where(lane < V_HEAD, m_s[2 * g] + jnp.log(l0) * LOG2E, m_s[2 * g + 1] + jnp.log(l1) * LOG2E)

    return pl.pallas_call(
        body, name="att_fwd",
        grid_spec=pltpu.PrefetchScalarGridSpec(
            num_scalar_prefetch=2, grid=(N_HEADS // hp, len(pairs)),
            in_specs=[pl.BlockSpec((t, hp * LANES), lambda p, n, it, jt: (it[n], p)),
                      pl.BlockSpec((t, hp * LANES), lambda p, n, it, jt: (jt[n], p)),
                      pl.BlockSpec((hp * LANES, t), lambda p, n, it, jt: (p, jt[n])),
                      pl.BlockSpec((LANES, t), lambda p, n, it, jt: (0, jt[n]))],
            out_specs=[pl.BlockSpec((t, hp // 2 * LANES), lambda p, n, it, jt: (it[n], p))] * 2,
            scratch_shapes=[pltpu.VMEM((hp, t, LANES), F32)] * 2),
        out_shape=[jax.ShapeDtypeStruct((s, N_HEADS * V_HEAD), BF16), jax.ShapeDtypeStruct((s, N_HEADS * V_HEAD), F32)],
        compiler_params=_cparams(("arbitrary", "arbitrary")),
    )(i_tab, j_tab, q, kv, kvt, krt)


def _att_bwd(q, qt, kv, kvt, kr, krt, y, lse, dy, dyt):
    s = q.shape[0]
    t = min(ATT_BLOCK, s)
    nb = s // t

    hp = ATT_BWD_HEADS
    pairs = [(i, j) for j in range(nb) for i in range(j, nb)]
    i_tab = jnp.array([p[0] for p in pairs], jnp.int32)
    j_tab = jnp.array([p[1] for p in pairs], jnp.int32)

    def body(i_ref, j_ref, q_ref, qt_ref, kv_ref, kvt_ref, kr_ref, krt_ref, y_ref, lse_ref, dy_ref, dyt_ref,
             dq_ref, dkvt_ref, dkrt_ref, dkv_s):
        p_, n = pl.program_id(0), pl.program_id(1)
        i, j = i_ref[n], j_ref[n]

        @pl.when((p_ == 0) & (n == 0))
        def _():
            dkrt_ref[...] = jnp.zeros(dkrt_ref.shape, F32)

        @pl.when(n == 0)
        def _():
            dq_ref[...] = jnp.zeros(dq_ref.shape, F32)

        @pl.when(i == j)
        def _():
            dkv_s[...] = jnp.zeros(dkv_s.shape, F32)

        def step(diagonal):
            lane = _lane((t, LANES))
            row = _row((LANES, t))
            krb, krt_b = kr_ref[...], krt_ref[...]
            dyv = dy_ref[...]
            yv = y_ref[...].astype(F32)
            lsev = lse_ref[...]
            dyt_b = dyt_ref[...]
            rows = pl.ds(pl.multiple_of(i * t, t), t)
            cols = pl.ds(pl.multiple_of(j * t, t), t)
            zeros_t = jnp.zeros((V_HEAD, t), BF16)
            ones_w = jnp.ones((LANES, LANES), BF16)
            groups = [slice(c * LANES, (c + 1) * LANES) for c in range(t // LANES)]
            heads = [slice(hh * LANES, (hh + 1) * LANES) for hh in range(hp)]
            scs, dps, stats = [], [], []
            for hh, sl in enumerate(heads):
                kvt_b = kvt_ref[sl, :]
                scs.append(_att_scores(q_ref[:, sl], kvt_b, krt_b, diagonal))
                pair, first = heads[hh // 2], hh % 2 == 0
                lse_g, dy_g, y_g = lsev[:, pair], dyv[:, pair], yv[:, pair]
                mine = (lane < V_HEAD) if first else (lane >= V_HEAD)
                lse_rep = jnp.where(mine, lse_g, pltpu.roll(lse_g, V_HEAD, 1))
                do_pad = jnp.where(lane >= V_HEAD, pltpu.roll(dy_g, V_HEAD, 1) if first else dy_g, 0.0)
                o_pad = jnp.where(lane >= V_HEAD, pltpu.roll(y_g, V_HEAD, 1) if first else y_g, 0.0)
                do_ln2 = do_pad * LN2
                prod = do_ln2 * o_pad
                head_part = prod.astype(BF16)
                rest_part = (prod - head_part.astype(F32)).astype(BF16)
                delta_rep = (jnp.dot(head_part, ones_w, preferred_element_type=F32)
                             + jnp.dot(rest_part, ones_w, preferred_element_type=F32))
                dps.append(jnp.dot(do_ln2.astype(BF16), kvt_b, preferred_element_type=F32))
                stats.append((lse_rep, delta_rep))
            dkr_acc = jnp.zeros((LANES, t), F32)
            for hh, sl in enumerate(heads):
                lse_rep, delta_rep = stats[hh]
                probs, dss = [], []
                for g in groups:
                    pg = jnp.exp2(scs[hh][:, g] - lse_rep)
                    probs.append(pg.astype(BF16))
                    dss.append((pg * (dps[hh][:, g] - delta_rep)).astype(BF16))
                prob, ds = jnp.concatenate(probs, axis=1), jnp.concatenate(dss, axis=1)
                dot_pad = jnp.concatenate([zeros_t, dyt_b[hh * V_HEAD:(hh + 1) * V_HEAD, :]], axis=0)
                k_eff = jnp.where(lane < QK_NOPE, kv_ref[:, sl], krb)
                dvt = jnp.dot(dot_pad, prob, preferred_element_type=F32)
                dq_ref[rows, sl] += jnp.dot(ds, k_eff, preferred_element_type=F32)
                dkt = jnp.dot(qt_ref[sl, :], ds, preferred_element_type=F32)
                dkv_s[hh] += dvt + jnp.where(row < QK_NOPE, dkt, 0.0)
                dkr_acc = dkr_acc + jnp.where(row >= QK_NOPE, dkt, 0.0)
            dkrt_ref[:, cols] += dkr_acc

        @pl.when(i > j)
        def _():
            step(False)

        @pl.when(i == j)
        def _():
            step(True)

        @pl.when(i == nb - 1)
        def _():
            for hh, sl in enumerate([slice(hh * LANES, (hh + 1) * LANES) for hh in range(hp)]):
                dkvt_ref[sl, :] = dkv_s[hh].astype(BF16)

    qi = lambda p, n, it, jt: (it[n], p)
    qti = lambda p, n, it, jt: (p, it[n])
    wide, half = hp * LANES, hp // 2 * LANES
    return pl.pallas_call(
        body, name="att_bwd",
        grid_spec=pltpu.PrefetchScalarGridSpec(
            num_scalar_prefetch=2, grid=(N_HEADS // hp, len(pairs)),
            in_specs=[pl.BlockSpec((t, wide), qi), pl.BlockSpec((wide, t), qti),
                      pl.BlockSpec((t, wide), lambda p, n, it, jt: (jt[n], p)),
                      pl.BlockSpec((wide, t), lambda p, n, it, jt: (p, jt[n])),
                      pl.BlockSpec((t, LANES), lambda p, n, it, jt: (jt[n], 0)),
                      pl.BlockSpec((LANES, t), lambda p, n, it, jt: (0, jt[n])),
                      pl.BlockSpec((t, half), qi), pl.BlockSpec((t, half), qi), pl.BlockSpec((t, half), qi),
                      pl.BlockSpec((half, t), qti)],
            out_specs=[pl.BlockSpec((s, wide), lambda p, n, it, jt: (0, p)),
                       pl.BlockSpec((wide, t), lambda p, n, it, jt: (p, jt[n])),
                       pl.BlockSpec((LANES, s), lambda p, n, it, jt: (0, 0))],
            scratch_shapes=[pltpu.VMEM((hp, LANES, t), F32)]),
        out_shape=[jax.ShapeDtypeStruct((s, N_HEADS * LANES), F32), jax.ShapeDtypeStruct((N_HEADS * LANES, s), BF16),
                   jax.ShapeDtypeStruct((LANES, s), F32)],
        compiler_params=_cparams(("arbitrary", "arbitrary")),
    )(i_tab, j_tab, q, qt, kv, kvt, kr, krt, y, lse, dy, dyt)


FFN_COLS = 256


def _ffn_conv(pad_ref, w, base, n):
    u = w[3:4] + w[2:3] * _shift_down(pad_ref, base, n, 0)
    for j in range(1, 3):
        u = u + w[2 - j:3 - j] * _shift_down(pad_ref, base, n, j)
    return u


def _ffn_act_fwd(up, fp):
    s, f2 = up.shape
    f = f2 // 2
    tc = FFN_COLS
    ts = min(RNN_CHUNK, s)
    nfb = f // tc

    def body(ug_ref, uv_ref, wg_ref, wv_ref, act_ref, gpad, vpad):
        zero8 = jnp.zeros((SUBLANES, tc), F32)
        gpad[0:SUBLANES, :] = zero8
        vpad[0:SUBLANES, :] = zero8
        gpad[SUBLANES:, :] = ug_ref[...].astype(F32)
        vpad[SUBLANES:, :] = uv_ref[...].astype(F32)
        wg, wv = wg_ref[...], wv_ref[...]

        def chunk(c, carry):
            base = pl.multiple_of(c * ts, ts)
            g = _ffn_conv(gpad, wg, base, ts)
            v = _ffn_conv(vpad, wv, base, ts)
            act_ref[pl.ds(base, ts), :] = (g * _sigmoid(g) * v).astype(BF16)
            return carry

        lax.fori_loop(0, s // ts, chunk, 0)

    return pl.pallas_call(
        body, name="ffn_act_fwd", grid=(nfb,),
        in_specs=[pl.BlockSpec((s, tc), lambda b: (0, b)), pl.BlockSpec((s, tc), lambda b: (0, b + nfb)),
                  pl.BlockSpec((SUBLANES, tc), lambda b: (0, b)), pl.BlockSpec((SUBLANES, tc), lambda b: (0, b + nfb))],
        out_specs=pl.BlockSpec((s, tc), lambda b: (0, b)),
        out_shape=jax.ShapeDtypeStruct((s, f), BF16),
        scratch_shapes=[pltpu.VMEM((s + SUBLANES, tc), F32)] * 2,
        compiler_params=_cparams(("arbitrary",)),
    )(up, up, fp, fp)


def _ffn_act_bwd(up, dact, fp):
    s, f2 = up.shape
    f = f2 // 2
    tc = FFN_COLS
    ts = min(RNN_CHUNK, s)
    nfb = f // tc

    def body(ug_ref, uv_ref, da_ref, wg_ref, wv_ref, dup_ref, red_ref, gpad, vpad, dgs, dvs):
        half = pl.program_id(1)
        wg, wv = wg_ref[...], wv_ref[...]

        @pl.when(half == 0)
        def _():
            zero8 = jnp.zeros((SUBLANES, tc), F32)
            gpad[0:SUBLANES, :] = zero8
            vpad[0:SUBLANES, :] = zero8
            gpad[SUBLANES:, :] = ug_ref[...].astype(F32)
            vpad[SUBLANES:, :] = uv_ref[...].astype(F32)
            dgs[s:, :] = zero8
            dvs[s:, :] = zero8

            def act(c, carry):
                base = pl.multiple_of(c * ts, ts)
                g = _ffn_conv(gpad, wg, base, ts)
                v = _ffn_conv(vpad, wv, base, ts)
                da = da_ref[pl.ds(base, ts), :].astype(F32)
                sg = _sigmoid(g)
                dgs[pl.ds(base, ts), :] = da * v * (sg * (1.0 + g * (1.0 - sg)))
                dvs[pl.ds(base, ts), :] = da * (g * sg)
                return carry

            lax.fori_loop(0, s // ts, act, 0)

        def conv_t(src, pad, w, out_ref, red_ref):
            def chunk(c, carry):
                base = pl.multiple_of(c * ts, ts)
                d_here = src[pl.ds(base, ts), :]
                dx = w[2:3] * d_here
                for j in range(1, 3):
                    dx = dx + w[2 - j:3 - j] * _shift_up(src, base, ts, j, 0)
                out_ref[pl.ds(base, ts), :] = dx.astype(BF16)
                taps = tuple(carry[k] + _colsum(d_here * _shift_down(pad, base, ts, 2 - k)) for k in range(3))
                return taps + (carry[3] + _colsum(d_here),)

            z1 = jnp.zeros((1, tc), F32)
            red = lax.fori_loop(0, s // ts, chunk, (z1, z1, z1, z1))
            red_ref[...] = jnp.concatenate(list(red) + [jnp.zeros((4, tc), F32)], axis=0)

        @pl.when(half == 0)
        def _():
            conv_t(dgs, gpad, wg, dup_ref, red_ref)

        @pl.when(half == 1)
        def _():
            conv_t(dvs, vpad, wv, dup_ref, red_ref)

    gcol = pl.BlockSpec((s, tc), lambda b, h: (0, b))
    vcol = pl.BlockSpec((s, tc), lambda b, h: (0, b + nfb))
    gpar = pl.BlockSpec((SUBLANES, tc), lambda b, h: (0, b))
    vpar = pl.BlockSpec((SUBLANES, tc), lambda b, h: (0, b + nfb))
    return pl.pallas_call(
        body, name="ffn_act_bwd", grid=(nfb, 2),
        in_specs=[gcol, vcol, gcol, gpar, vpar],
        out_specs=[pl.BlockSpec((s, tc), lambda b, h: (0, b + h * nfb)),
                   pl.BlockSpec((SUBLANES, tc), lambda b, h: (0, b + h * nfb))],
        out_shape=[jax.ShapeDtypeStruct((s, f2), BF16), jax.ShapeDtypeStruct((SUBLANES, f2), F32)],
        scratch_shapes=[pltpu.VMEM((s + SUBLANES, tc), F32)] * 4,
        compiler_params=_cparams(("arbitrary", "arbitrary")),
    )(up, up, dact, fp, fp)


def _rows8(rows, width):
    rows = [r.reshape(1, width).astype(F32) for r in rows]
    return jnp.concatenate(rows + [jnp.zeros((SUBLANES - len(rows), width), F32)], axis=0)


def _block_diag(w):
    n, b, _ = w.shape
    w = w.reshape(n // 2, 2, b, b)
    z = jnp.zeros((n // 2, b, b), w.dtype)
    top = jnp.concatenate([w[:, 0], z], axis=2)
    bot = jnp.concatenate([z, w[:, 1]], axis=2)
    return jnp.concatenate([top, bot], axis=1)


def _block_diag_t(bd):
    n, b2, _ = bd.shape
    b = b2 // 2
    return jnp.stack([bd[:, :b, :b], bd[:, b:, b:]], axis=1).reshape(2 * n, b, b)


def _local_step(x, mod, positions, target, w_in, fetch, sm, emit):
    s, d = x.shape
    o_rnn, o_mla = D_RNN, D_RNN + Q_LORA + KV_LORA + QK_ROPE
    wts = {}
    w_in_rnn = w_in[:, :o_rnn]
    w_in_mla = jnp.concatenate([w_in[:, o_rnn:o_mla], jnp.zeros((d, MLA_W - (o_mla - o_rnn)), w_in.dtype)], axis=1)
    w_in_g = w_in[:, o_mla:]
    hd = QK_NOPE + QK_ROPE
    wa_bd = _block_diag(sm["w_gate_a"]).astype(BF16)
    wx_bd = _block_diag(sm["w_gate_x"]).astype(BF16)

    pos = positions.reshape(s)
    half = QK_ROPE // 2
    inv_freq = ROPE_THETA ** (-jnp.arange(half, dtype=F32) / half)
    ang = pos.astype(F32)[:, None] * inv_freq
    cos, sin = jnp.cos(ang), jnp.sin(ang)
    rot_c = jnp.concatenate([jnp.ones((s, QK_NOPE), F32), cos, cos, jnp.ones((s, LANES - hd), F32)], axis=1)
    rot_s = jnp.concatenate([jnp.zeros((s, QK_NOPE), F32), -sin, sin, jnp.zeros((s, LANES - hd), F32)], axis=1)
    keep = (pos != 0).astype(F32).reshape(s, 1)

    gmod1 = _rows8([sm["norm1_g"], mod[1], mod[0]], d)
    gmod2 = _rows8([sm["norm2_g"], mod[4], mod[3], mod[2]], d)
    rp = jnp.concatenate([sm["conv_w"].reshape(4, D_RNN), _rows8([sm["conv_b"], sm["b_gate_a"], sm["b_gate_x"], sm["lru_param"]], D_RNN)[:4]], axis=0)
    fp = _rows8([sm["ffn_conv_w"][0], sm["ffn_conv_w"][1], sm["ffn_conv_w"][2], sm["ffn_conv_b"]], 2 * D_FF)
    ng = _rows8([jnp.concatenate([sm["q_norm_g"].reshape(-1), sm["kv_norm_g"].reshape(-1), jnp.zeros((MLA_W - Q_LORA - KV_LORA,), F32)])], MLA_W)
    fpar = _rows8([mod[5], sm["final_g"]], d)

    h = _norm_mod_fwd(x, gmod1, "norm1_fwd")
    proj_rnn = _mm(h, w_in_rnn, name="mm_in_rnn")
    proj_mla = _mm(h, w_in_mla, name="mm_in_mla")
    proj_g = _mm(h, w_in_g, out_dtype=BF16, name="mm_in_g")
    xc, ra, ix, hs, hs_b = _rnn_fwd(proj_rnn, keep, rp, wa_bd, wx_bd)
    qn, kvn, kr = _mla_prep_fwd(proj_mla, rot_c, rot_s, ng)
    wts.update(fetch(("w_uq", "w_ukv"), kr))
    w_uq_p = jnp.pad(wts["w_uq"].reshape(Q_LORA, N_HEADS, hd), ((0, 0), (0, 0), (0, LANES - hd))).reshape(Q_LORA, N_HEADS * LANES)
    w_ukv = wts["w_ukv"]
    q_rot, q_rot_t = _rope_fwd_t(_mm(qn, w_uq_p, name="mm_uq"), rot_c, rot_s)
    kv, kvt = _mm(kvn, w_ukv, out_dtype=BF16, also_t=BF16, name="mm_ukv")
    krt = jnp.transpose(kr)
    y_mla, lse = _att_fwd(q_rot, kv, kvt, krt)
    wts.update(fetch(("w_proj_rnn", "w_proj_mla", "w_out", "w_up", "w_down"), lse))
    pr = _mm(hs_b, wts["w_proj_rnn"], out_dtype=BF16, name="mm_proj_rnn")
    pm = _mm(y_mla, wts["w_proj_mla"], out_dtype=BF16, name="mm_proj_mla")
    merged = _merge_fwd(pr, pm, proj_g)
    o = _mm(merged, wts["w_out"], name="mm_out")
    x1, h2 = _resid_norm_fwd(x, o, gmod2)
    up = _mm(h2, wts["w_up"], out_dtype=BF16, name="mm_up")
    act = _ffn_act_fwd(up, fp)
    dn = _mm(act, wts["w_down"], name="mm_down")

    dx2, ddn, red_f = _final_fwd_bwd(x1, dn, target, fpar)
    dact = _mm(ddn, wts["w_down"], tb=True, out_dtype=BF16, name="mm_d_act")
    tok = emit("w_down", _mm(act, ddn, ta=True, out_dtype=BF16, name="mm_dw_down"))
    dup, red_ffn = _ffn_act_bwd(up, dact, fp + tok)
    dh2 = _mm(dup, wts["w_up"], tb=True, name="mm_d_h2")
    tok = tok + emit("w_up", _mm(dup, h2, ta=True, out_dtype=BF16, name="mm_dw_up"))
    dx1, do, red_2 = _norm2_bwd(x1, dh2, dx2, o, gmod2 + tok)
    dmerged = _mm(do, wts["w_out"], tb=True, name="mm_d_merged")
    tok = tok + emit("w_out", _mm(merged, do, ta=True, out_dtype=BF16, name="mm_dw_out"))
    dpr, dpm, dg = _merge_bwd(dmerged, pr, pm, proj_g)
    dy_rnn = _mm(dpr, wts["w_proj_rnn"], tb=True, name="mm_d_yrnn")
    tok = tok + emit("w_proj_rnn", _mm(hs_b, dpr, ta=True, out_dtype=BF16, name="mm_dw_proj_rnn"))
    dy_mla, dy_mla_t = _mm(dpm, wts["w_proj_mla"], tb=True, also_t=BF16, name="mm_d_ymla")
    tok = tok + emit("w_proj_mla", _mm(y_mla, dpm, ta=True, out_dtype=BF16, name="mm_dw_proj_mla"))
    dq_rot, dkvt, dkrt = _att_bwd(q_rot, q_rot_t, kv, kvt, kr, krt, y_mla, lse, dy_mla, dy_mla_t)
    dq = _rope_bwd(dq_rot, rot_c, rot_s)
    dqn = _mm(dq, w_uq_p, tb=True, name="mm_d_qn")
    dw_uq_pt = _mm(dq, qn, ta=True, out_dtype=BF16, name="mm_dw_uq")
    tok = tok + emit("w_uq", dw_uq_pt.reshape(N_HEADS, LANES, Q_LORA)[:, :hd].reshape(N_HEADS * hd, Q_LORA))
    dkvn = _mm(w_ukv, dkvt, also_t=F32, name="mm_d_kvn")[1]
    tok = tok + emit("w_ukv", _mm(dkvt, kvn, out_dtype=BF16, name="mm_dw_ukv"))
    dproj_mla, red_m = _mla_prep_bwd(proj_mla, dqn, dkvn, jnp.transpose(dkrt), rot_c, rot_s, ng + tok)
    dx_rnn, dwa_bd, dwx_bd, red_r = _rnn_bwd(proj_rnn, xc, ra, ix, hs, dy_rnn, keep, rp + tok, wa_bd, wx_bd)
    dw_in_t = jnp.concatenate([
        _mm(dx_rnn, h, ta=True, out_dtype=BF16, name="mm_dw_in_rnn"),
        _mm(dproj_mla, h, ta=True, out_dtype=BF16, name="mm_dw_in_mla")[:o_mla - o_rnn],
        _mm(dg, h, ta=True, out_dtype=BF16, name="mm_dw_in_g")], axis=0)
    tok = tok + emit("w_in", dw_in_t)
    dh_a = _mm(dx_rnn, w_in_rnn, tb=True, name="mm_d_h_rnn")
    dh_b = _mm(dproj_mla, w_in_mla, tb=True, name="mm_d_h_mla")
    dh_c = _mm(dg, w_in_g, tb=True, name="mm_d_h_g")
    grad_x, red_1 = _norm1_bwd(x, dh_a, dh_b, dh_c, dx1, gmod1 + tok)

    gs = {
        "norm1_g": red_1[0], "conv_w": red_r[0:4], "conv_b": red_r[4], "w_gate_a": _block_diag_t(dwa_bd),
        "b_gate_a": red_r[5], "w_gate_x": _block_diag_t(dwx_bd), "b_gate_x": red_r[6], "lru_param": red_r[7],
        "q_norm_g": red_m[0, :Q_LORA], "kv_norm_g": red_m[0, Q_LORA:Q_LORA + KV_LORA], "norm2_g": red_2[0],
        "ffn_conv_w": red_ffn[0:3], "ffn_conv_b": red_ffn[3], "final_g": red_f[0],
    }
    dmod = jnp.stack([red_1[2], red_1[1], red_2[3], red_2[2], red_2[1], red_f[1]], axis=0)
    return red_f[2, 0], grad_x, gs, dmod


MESH_IDS = pl.DeviceIdType.MESH
HBM_SPEC = pl.BlockSpec(memory_space=pltpu.HBM)


def _my_slot():
    return 4 * lax.axis_index("x") + 2 * lax.axis_index("y") + lax.axis_index("c")


def _all_gather(arrs, name):
    n = len(arrs)

    def body(*refs):
        ins, outs = refs[:n], refs[n:2 * n]
        send_sems, recv_sems, local_sems = refs[2 * n:]
        x, y, c = lax.axis_index("x"), lax.axis_index("y"), lax.axis_index("c")
        me, sibling = (x, y, c), (x, y, 1 - c)
        chips = [(1 - x, y), (x, 1 - y), (1 - x, 1 - y)]

        def slot(dev):
            return 4 * dev[0] + 2 * dev[1] + dev[2]

        def copy(a, k, block, to, src=None):
            dst = outs[a].at[slot(block)]
            return pltpu.make_async_remote_copy(
                src_ref=dst if src is None else src, dst_ref=dst, send_sem=send_sems.at[a, k], recv_sem=recv_sems.at[a, k],
                device_id=to, device_id_type=MESH_IDS)

        mine = [pltpu.make_async_copy(ins[a], outs[a].at[slot(me)], local_sems.at[a]) for a in range(n)]
        for cp in mine:
            cp.start()
        first = []
        for a in range(n):
            first.append(copy(a, 0, me, sibling, src=ins[a]))
            first += [copy(a, 1 + j, me, (*chip, c), src=ins[a]) for j, chip in enumerate(chips)]
        for cp in first:
            cp.start()
        passed = []
        for j, chip in enumerate(chips):
            for a in range(n):
                copy(a, 1 + j, (*chip, c), me).wait_recv()
                fwd = copy(a, 4 + j, (*chip, c), sibling)
                fwd.start()
                passed.append(fwd)
        for a in range(n):
            copy(a, 0, sibling, me).wait_recv()
            for j, chip in enumerate(chips):
                copy(a, 4 + j, (*chip, 1 - c), me).wait_recv()
        for cp in first + passed:
            cp.wait_send()
        for cp in mine:
            cp.wait()

    return pl.pallas_call(
        body, name=name,
        in_specs=[HBM_SPEC] * n, out_specs=[HBM_SPEC] * n,
        out_shape=[jax.ShapeDtypeStruct((N_DEV,) + a.shape, a.dtype) for a in arrs],
        scratch_shapes=[pltpu.SemaphoreType.DMA((n, 7)), pltpu.SemaphoreType.DMA((n, 7)), pltpu.SemaphoreType.DMA((n,))],
    )(*arrs)


def _assemble_columns(g, name):
    k, r, c = g.shape
    tr = _pick(r, (256, 128, 64, 32, 16))

    def body(g_ref, o_ref):
        for s in range(k):
            o_ref[:, s * c:(s + 1) * c] = g_ref[s]

    return pl.pallas_call(
        body, name=name, grid=(r // tr,),
        in_specs=[pl.BlockSpec((k, tr, c), lambda i: (0, i, 0))],
        out_specs=pl.BlockSpec((tr, k * c), lambda i: (i, 0)),
        out_shape=jax.ShapeDtypeStruct((r, k * c), g.dtype),
        compiler_params=_cparams(("arbitrary",)),
    )(g)


SEM_SPEC =pl.BlockSpec(memory_space=pltpu.SEMAPHORE)
DATAFLOW = pltpu.SideEffectType.DATAFLOW_SIDE_EFFECTING
FLIPS = [(dx, dy, dc) for dx in (0, 1) for dy in (0, 1) for dc in (0, 1)][1:]


def _peer(k):
    dx, dy, dc = FLIPS[k]
    peer = (lax.axis_index("x") ^ dx, lax.axis_index("y") ^ dy, lax.axis_index("c") ^ dc)
    return peer, 4 * peer[0] + 2 * peer[1] + peer[2]


def _gather_start(shards, after, name):
    n, nf = len(shards), len(FLIPS)

    def body(*refs):
        srcs, lands = refs[:n], refs[n:2 * n]
        send_sems, recv_sems = refs[2 * n + 1:3 * n + 1], refs[3 * n + 1:4 * n + 1]
        token = refs[-1]
        me = _my_slot()
        for a in range(n):
            for k in range(nf):
                peer, _ = _peer(k)
                pltpu.make_async_remote_copy(
                    src_ref=srcs[a], dst_ref=lands[a].at[me], send_sem=send_sems[a].at[k], recv_sem=recv_sems[a].at[k],
                    device_id=peer, device_id_type=MESH_IDS).start()
        token[...] = jnp.zeros(token.shape, F32)

    land_shapes = [(N_DEV,) + a.shape for a in shards]
    sems = [pltpu.SemaphoreType.DMA((nf,))] * n
    out = pl.pallas_call(
        body, name=name,
        out_shape=(*sems, *sems, *[pltpu.HBM(a.shape, a.dtype) for a in shards],
                   *[pltpu.HBM(shp, a.dtype) for shp, a in zip(land_shapes, shards)],
                   jax.ShapeDtypeStruct((SUBLANES, LANES), F32)),
        in_specs=[HBM_SPEC] * (2 * n) + [pl.BlockSpec(memory_space=pl.ANY)],
        out_specs=(*[SEM_SPEC] * (2 * n), *[HBM_SPEC] * (2 * n), pl.BlockSpec(memory_space=pltpu.VMEM)),
        input_output_aliases={i: 2 * n + i for i in range(2 * n)},
        compiler_params=pltpu.CompilerParams(has_side_effects=DATAFLOW),
    )(*[pltpu.with_memory_space_constraint(a, pltpu.HBM) for a in shards],
      *[pltpu.with_memory_space_constraint(lax.empty(shp, a.dtype), pltpu.HBM) for shp, a in zip(land_shapes, shards)],
      after)
    return [(out[a], out[n + a], out[2 * n + a], out[3 * n + a]) for a in range(n)], out[-1]


def _gather_wait(flights, after, name):
    n, nf = len(flights), len(FLIPS)

    def body(*refs):
        send_sems, recv_sems = refs[:n], refs[n:2 * n]
        srcs, lands = refs[2 * n:3 * n], refs[3 * n:4 * n]
        for a in range(n):
            for k in range(nf):
                peer, peer_slot = _peer(k)
                cp = pltpu.make_async_remote_copy(
                    src_ref=srcs[a], dst_ref=lands[a].at[peer_slot], send_sem=send_sems[a].at[k],
                    recv_sem=recv_sems[a].at[k], device_id=peer, device_id_type=MESH_IDS)
                cp.wait_send()
                cp.wait_recv()

    srcs, lands = [f[2] for f in flights], [f[3] for f in flights]
    out = pl.pallas_call(
        body, name=name,
        out_shape=(*[pltpu.HBM(a.shape, a.dtype) for a in srcs], *[pltpu.HBM(a.shape, a.dtype) for a in lands]),
        in_specs=[SEM_SPEC] * (2 * n) + [HBM_SPEC] * (2 * n) + [pl.BlockSpec(memory_space=pl.ANY)],
        out_specs=tuple([HBM_SPEC] * (2 * n)),
        input_output_aliases={2 * n + i: i for i in range(2 * n)},
        compiler_params=pltpu.CompilerParams(has_side_effects=DATAFLOW),
    )(*[f[0] for f in flights], *[f[1] for f in flights], *srcs, *lands, after)
    return list(out[n:])


ROW_ALIGN = 16


def _span_start(slot, rows):
    return (rows * slot) // ROW_ALIGN * ROW_ALIGN


def _chunk_of(src_ref, slot, rows, span):
    if rows is None:
        return src_ref.at[slot]
    return src_ref.at[pl.ds(pl.multiple_of(_span_start(slot, rows), ROW_ALIGN), span)]


def _scatter_start(src, name, rows=None, span=None):
    def body(src_ref, land_ref, send_sems, recv_sems, src_thru, land_thru, token):
        me = _my_slot()
        for k in range(len(FLIPS)):
            peer, peer_slot = _peer(k)
            pltpu.make_async_remote_copy(
                src_ref=_chunk_of(src_ref, peer_slot, rows, span), dst_ref=land_ref.at[me], send_sem=send_sems.at[k],
                recv_sem=recv_sems.at[k], device_id=peer, device_id_type=MESH_IDS).start()
        token[...] = jnp.zeros(token.shape, F32)

    n = len(FLIPS)
    land_shape = src.shape if rows is None else (N_DEV, span, src.shape[1])
    return pl.pallas_call(
        body, name=name,
        out_shape=(pltpu.SemaphoreType.DMA((n,)), pltpu.SemaphoreType.DMA((n,)), pltpu.HBM(src.shape, src.dtype),
                   pltpu.HBM(land_shape, src.dtype), jax.ShapeDtypeStruct((SUBLANES, LANES), F32)),
        in_specs=(HBM_SPEC, HBM_SPEC),
        out_specs=(SEM_SPEC, SEM_SPEC, HBM_SPEC, HBM_SPEC, pl.BlockSpec(memory_space=pltpu.VMEM)),
        input_output_aliases={0: 2, 1: 3},
        compiler_params=pltpu.CompilerParams(has_side_effects=DATAFLOW),
    )(pltpu.with_memory_space_constraint(src, pltpu.HBM),
      pltpu.with_memory_space_constraint(lax.empty(land_shape, src.dtype), pltpu.HBM))


def _scatter_wait(send_sems, recv_sems, src_thru, land_thru, after, name, rows=None, span=None):
    def body(src_ref, land_ref, send_sems, recv_sems, after_ref, src_dead, got_ref):
        for k in range(len(FLIPS)):
            peer, peer_slot = _peer(k)
            cp = pltpu.make_async_remote_copy(
                src_ref=_chunk_of(src_ref, peer_slot, rows, span), dst_ref=land_ref.at[peer_slot], send_sem=send_sems.at[k],
                recv_sem=recv_sems.at[k], device_id=peer, device_id_type=MESH_IDS)
            cp.wait_send()
            cp.wait_recv()

    return pl.pallas_call(
        body, name=name,
        out_shape=(pltpu.HBM(src_thru.shape, src_thru.dtype), pltpu.HBM(land_thru.shape, land_thru.dtype)),
        in_specs=(HBM_SPEC, HBM_SPEC, SEM_SPEC, SEM_SPEC, pl.BlockSpec(memory_space=pl.ANY)),
        out_specs=(HBM_SPEC, HBM_SPEC), input_output_aliases={0: 0, 1: 1},
        compiler_params=pltpu.CompilerParams(has_side_effects=DATAFLOW),
    )(src_thru, land_thru, send_sems, recv_sems, after)


def _sum_sources(parts, own, name):
    k, r, c = parts.shape
    tr = r if k * r * c <= 2 * 1024 * 1024 else _pick(r, (512, 256, 128, 64, 32, 16, 8))

    def body(p_ref, own_ref, o_ref):
        me = _my_slot()
        acc = jnp.where(me == 0, own_ref[...], p_ref[0]).astype(F32)
        for s in range(1, k):
            acc = acc + jnp.where(me == s, own_ref[...], p_ref[s]).astype(F32)
        o_ref[...] = acc

    blk = pl.BlockSpec((tr, c), lambda i: (i, 0))
    return pl.pallas_call(
        body, name=name, grid=(r // tr,),
        in_specs=[pl.BlockSpec((k, tr, c), lambda i: (0, i, 0)), blk],
        out_specs=blk,
        out_shape=jax.ShapeDtypeStruct((r, c), F32),
        compiler_params=_cparams(("arbitrary",)),
    )(parts, own)


def _adamw_math(g, w, m, v):
    m_new = ADAM_B1 * m + (1.0 - ADAM_B1) * g
    v_new = ADAM_B2 * v + (1.0 - ADAM_B2) * jnp.square(g)
    m_hat = m_new / (1.0 - ADAM_B1 ** ADAM_STEP)
    v_hat = v_new / (1.0 - ADAM_B2 ** ADAM_STEP)
    return -ADAM_LR * (m_hat / (jnp.sqrt(v_hat) + ADAM_EPS) + ADAM_WD * w), m_new, v_new


def _adamw_many(gs, ws, ms, vs, name):
    n = len(gs)

    def body(*refs):
        ins, outs = refs[:4 * n], refs[4 * n:]
        for i in range(n):
            g = ins[i][...]
            outs[4 * i][...] = g
            outs[4 * i + 1][...], outs[4 * i + 2][...], outs[4 * i + 3][...] = _adamw_math(
                g, ins[n + i][...], ins[2 * n + i][...], ins[3 * n + i][...])

    return pl.pallas_call(
        body, name=name,
        out_shape=[jax.ShapeDtypeStruct(a.shape, F32) for a in ws for _ in range(4)],
        compiler_params=_cparams(),
    )(*gs, *ws, *ms, *vs)


def _adamw(parts, w, m, v, name, own=None):
    k, r, c = parts.shape
    tr = r if r * c <= 256 * 1024 else _pick(r, (256, 128, 64, 32, 16, 8))

    def body(*refs):
        p_ref, w_ref, m_ref, v_ref = refs[:4]
        g_ref, d_ref, nm_ref, nv_ref = refs[-4:]

        def part(s):
            if own is None:
                return p_ref[s].astype(F32)
            return jnp.where(_my_slot() == s, refs[4][...], p_ref[s]).astype(F32)

        g = part(0)
        for s in range(1, k):
            g = g + part(s)
        g_ref[...] = g
        d_ref[...], nm_ref[...], nv_ref[...] = _adamw_math(g, w_ref[...], m_ref[...], v_ref[...])

    blk = pl.BlockSpec((tr, c), lambda i: (i, 0))
    return pl.pallas_call(
        body, name=name, grid=(r // tr,),
        in_specs=[pl.BlockSpec((k, tr, c), lambda i: (0, i, 0)), blk, blk, blk] + ([] if own is None else [blk]),
        out_specs=[blk] * 4,
        out_shape=[jax.ShapeDtypeStruct((r, c), F32)] * 4,
        input_output_aliases={2: 2, 3: 3},
        compiler_params=_cparams(("arbitrary",)),
    )(parts, w, m, v, *([] if own is None else [own]))


def _silu(v):
    return v * _sigmoid(v)


def _ada_fwd(c_all, w, b):
    def body(c_ref, w_ref, b_ref, o_ref):
        ca = _silu(c_ref[...]).astype(BF16)
        o_ref[...] = jnp.dot(ca, w_ref[...].astype(BF16), preferred_element_type=F32) + b_ref[...]

    return pl.pallas_call(
        body, name="ada_fwd", out_shape=jax.ShapeDtypeStruct((c_all.shape[0], w.shape[1]), F32),
        compiler_params=_cparams(),
    )(c_all, w, b)


def _ada_bwd(c_all, dmod):
    def body(c_ref, d_ref, o_ref):
        ca = _silu(c_ref[...]).astype(BF16).astype(F32)
        dm = d_ref[...].astype(BF16).astype(F32)
        acc = jnp.zeros(o_ref.shape, F32)
        for bi in range(c_all.shape[0]):
            acc = acc + jnp.transpose(ca[bi:bi + 1, :]) * dm[bi:bi + 1, :]
        o_ref[...] = acc

    return pl.pallas_call(
        body, name="ada_bwd", out_shape=jax.ShapeDtypeStruct((c_all.shape[1], dmod.shape[1]), F32),
        compiler_params=_cparams(),
    )(c_all, dmod)


COL_SHARDED = ("w_in", "w_uq", "w_ukv", "w_up")
ROW_SHARDED = ("w_proj_rnn", "w_proj_mla", "w_out", "w_down")
REPLICATED = ("b_ada", "norm1_g", "conv_b", "w_gate_a", "b_gate_a", "w_gate_x", "b_gate_x", "lru_param", "q_norm_g",
              "kv_norm_g", "norm2_g", "ffn_conv_b", "final_g")
WEIGHTS = ("w_ada", "b_ada", "norm1_g", "w_in", "conv_w", "conv_b", "w_gate_a", "b_gate_a", "w_gate_x", "b_gate_x",
           "lru_param", "q_norm_g", "w_uq", "kv_norm_g", "w_ukv", "w_proj_rnn", "w_proj_mla", "w_out", "norm2_g", "w_up",
           "ffn_conv_w", "ffn_conv_b", "w_down", "final_g")
TRANSPOSED_GRADS = COL_SHARDED
PACK_LANES = 128


def _pack(vecs, row_multiple=SUBLANES):
    flat = jnp.concatenate([v.reshape(-1).astype(F32) for v in vecs])
    pad = (-flat.shape[0]) % (PACK_LANES * row_multiple)
    return jnp.concatenate([flat, jnp.zeros((pad,), F32)]).reshape(-1, PACK_LANES)


def _unpack(packed, shapes):
    flat = packed.reshape(-1)
    out, off = [], 0
    for shp in shapes:
        size = math.prod(shp)
        out.append(flat[off:off + size].reshape(shp))
        off += size
    return out


def kernel(x, c, positions, w_ada, b_ada, norm1_g, w_in, conv_w, conv_b, w_gate_a, b_gate_a, w_gate_x, b_gate_x, lru_param, q_norm_g, w_uq, kv_norm_g, w_ukv, w_proj_rnn, w_proj_mla, w_out, norm2_g, w_up, ffn_conv_w, ffn_conv_b, w_down, final_g, loss_target, m_w_ada, m_b_ada, m_norm1_g, m_w_in, m_conv_w, m_conv_b, m_w_gate_a, m_b_gate_a, m_w_gate_x, m_b_gate_x, m_lru_param, m_q_norm_g, m_w_uq, m_kv_norm_g, m_w_ukv, m_w_proj_rnn, m_w_proj_mla, m_w_out, m_norm2_g, m_w_up, m_ffn_conv_w, m_ffn_conv_b, m_w_down, m_final_g, v_w_ada, v_b_ada, v_norm1_g, v_w_in, v_conv_w, v_conv_b, v_w_gate_a, v_b_gate_a, v_w_gate_x, v_b_gate_x, v_lru_param, v_q_norm_g, v_w_uq, v_kv_norm_g, v_w_ukv, v_w_proj_rnn, v_w_proj_mla, v_w_out, v_norm2_g, v_w_up, v_ffn_conv_w, v_ffn_conv_b, v_w_down, v_final_g):
    args = dict(locals())
    w = {n: args[n] for n in WEIGHTS}
    m = {n: args["m_" + n] for n in WEIGHTS}
    v = {n: args["v_" + n] for n in WEIGHTS}
    s, d = x.shape[1], x.shape[2]
    me = _my_slot()
    def two_d(a):
        assert a.ndim == 3 and a.shape[0] == 1, a.shape
        return a[0]

    big = COL_SHARDED + ROW_SHARDED
    shard = {n: two_d(w[n]).astype(BF16) for n in big}

    def whole(n, g):
        k, r, cc = g.shape
        return _assemble_columns(g, "assemble_" + n) if n in COL_SHARDED else g.reshape(k * r, cc)

    first = _all_gather([shard["w_in"], c, two_d(conv_w), two_d(ffn_conv_w)], "gather_first")
    c_all = first[1].reshape(N_DEV, d)
    conv_w_all = jnp.transpose(first[2], (1, 0, 2)).reshape(conv_w.shape[1], -1)
    ffn_conv_w_all = jnp.transpose(first[3], (1, 0, 2)).reshape(ffn_conv_w.shape[1], -1)

    ada_cols = w_ada.shape[2]
    b_cols = lax.dynamic_slice(b_ada, (0, me * ada_cols), (1, ada_cols))
    mod_cols = _ada_fwd(c_all, w_ada[0], b_cols)
    mod_all, = _all_gather([mod_cols], "gather_mod")

    later = ("w_uq", "w_ukv", "w_proj_rnn", "w_proj_mla", "w_out", "w_up", "w_down")
    flights, started = _gather_start([shard[n] for n in later], mod_all, "gather_start")
    flight = dict(zip(later, flights))

    def fetch(names, after):
        lands = _gather_wait([flight[n] for n in names], after, "gather_wait_" + names[0])
        return {n: whole(n, lax.dynamic_update_index_in_dim(g, shard[n], me, 0)) for n, g in zip(names, lands)}

    mod = lax.dynamic_index_in_dim(mod_all, me, axis=1, keepdims=False).reshape(6, d) + started[0, 0]

    sm = {n: w[n][0] for n in REPLICATED if n not in ("b_ada", "final_g")}
    sm["final_g"] = final_g
    sm["conv_w"] = conv_w_all
    sm["ffn_conv_w"] = ffn_conv_w_all
    in_flight, windows = {}, {}

    def emit(n, g):
        rows = g.shape[0] // N_DEV
        if rows % ROW_ALIGN == 0:
            windows[n] = (None, None)
            g = g.reshape(N_DEV, rows, g.shape[1])
        else:
            span = max(rows * k - _span_start(k, rows) for k in range(N_DEV)) + rows
            windows[n] = (rows, -(-span // ROW_ALIGN) * ROW_ALIGN)
            assert _span_start(N_DEV - 1, rows) + windows[n][1] <= g.shape[0], (n, g.shape)
        *in_flight[n], token = _scatter_start(g, "scatter_start_" + n, *windows[n])
        return token[0, 0]

    sq, grad_x, gs, dmod = _local_step(x[0], mod, positions[0], loss_target[0], whole("w_in", first[0]), fetch, sm, emit)

    small_names = [n for n in REPLICATED if n != "b_ada"] + ["conv_w", "ffn_conv_w"]
    small_shapes = [gs[n].shape for n in small_names] + [(6 * d,), (1,)]
    partial = _pack([gs[n] for n in small_names] + [dmod, sq.reshape(1)], N_DEV * SUBLANES)
    *small_flight, small_started = _scatter_start(partial.reshape(N_DEV, -1, PACK_LANES), "scatter_small_start")

    grads, deltas, new_m, new_v = {}, {}, {}, {}

    def update(n, parts, own=None):
        shp = w[n].shape
        lay = jnp.transpose if n in TRANSPOSED_GRADS else (lambda a: a)
        res = _adamw(parts, lay(two_d(w[n])), lay(two_d(m[n])), lay(two_d(v[n])), "adamw_" + n, own)
        grads[n], deltas[n], new_m[n], new_v[n] = [lay(a).reshape(shp) for a in res]

    for n in big:
        rows, span = windows[n]
        src, landed = _scatter_wait(*in_flight[n], small_started, "scatter_wait_" + n, rows, span)
        if rows is None:
            update(n, landed, lax.dynamic_index_in_dim(src, me, axis=0, keepdims=False))
        else:
            start = _span_start(me, rows)
            own = lax.dynamic_slice(src, (start, 0), (span, src.shape[1]))
            total = _sum_sources(landed, own, "sum_" + n)
            update(n, lax.dynamic_slice(total, (rows * me - start, 0), (rows, src.shape[1]))[None])

    chunks, landed = _scatter_wait(*small_flight, new_v[big[-1]], "scatter_small_wait")
    mine = _sum_sources(landed, lax.dynamic_index_in_dim(chunks, me, axis=0, keepdims=False), "sum_small")
    summed_all, dmod_all = _all_gather([mine, dmod.reshape(1, 6 * d)], "gather_small")
    summed = _unpack(summed_all, small_shapes)
    g_small = dict(zip(small_names, summed[:len(small_names)]))
    g_small["b_ada"] = summed[len(small_names)]
    loss = 0.5 * summed[-1][0] / d
    dmod_cols = lax.dynamic_slice(dmod_all.reshape(N_DEV, 6 * d), (0, me * ada_cols), (N_DEV, ada_cols))

    update("w_ada", _ada_bwd(c_all, dmod_cols)[None])

    for n in ("conv_w", "ffn_conv_w"):
        cols = w[n].shape[2]
        g_small[n] = lax.dynamic_slice(g_small[n], (0, me * cols), (g_small[n].shape[0], cols))
    small = REPLICATED + ("conv_w", "ffn_conv_w")
    as_rows = lambda a: a.reshape(1, -1) if a.ndim == 1 else a
    res = _adamw_many([as_rows(g_small[n].reshape(w[n].shape)) for n in small], [as_rows(w[n]) for n in small],
                      [as_rows(m[n]) for n in small], [as_rows(v[n]) for n in small], "adamw_small")
    for i, n in enumerate(small):
        grads[n], deltas[n], new_m[n], new_v[n] = [a.reshape(w[n].shape) for a in res[4 * i:4 * i + 4]]

    return (loss, grad_x[None], *[grads[n] for n in WEIGHTS], *[deltas[n] for n in WEIGHTS],
            *[new_m[n] for n in WEIGHTS], *[new_v[n] for n in WEIGHTS])
```
